```python
import math
import jax, jax.numpy as jnp
from jax import lax
import numpy as np

D_MODEL = 2048
BATCH = 8
SEQ = 2048
DEPTH = 1

HG_HEADS = 8
HG_DK = 128
HG_DV = 128
HG_WIDTH = HG_HEADS * HG_DK
HG_CHUNK = 64

AT_HEADS = 16
AT_KV_HEADS = 4
AT_HEAD_DIM = 64
AT_GROUP = AT_HEADS // AT_KV_HEADS
AT_WIDTH = AT_HEADS * AT_HEAD_DIM
KV_WIDTH = AT_KV_HEADS * AT_HEAD_DIM
WINDOW = 128
BLOCK = 128

N_BUCKETS = 32
MAX_EXACT = N_BUCKETS // 2
MAX_DISTANCE = 128

D_FF = 4 * D_MODEL
N_BRANCH = 2
EPS = 1e-6
NEG_INF = -1e30

IN_WIDTH = 4 * HG_WIDTH + AT_WIDTH + 2 * KV_WIDTH + N_BRANCH * D_MODEL
IN_OFFSETS = (
    HG_WIDTH,
    2 * HG_WIDTH,
    3 * HG_WIDTH,
    4 * HG_WIDTH,
    4 * HG_WIDTH + AT_WIDTH,
    4 * HG_WIDTH + AT_WIDTH + KV_WIDTH,
    4 * HG_WIDTH + AT_WIDTH + 2 * KV_WIDTH,
    4 * HG_WIDTH + AT_WIDTH + 2 * KV_WIDTH + D_MODEL,
)

kernel_name = "hgrn2_swa_sink_gated_hybrid_block"


def rms_norm(x, g):
    xf = x.astype(jnp.float32)
    y = xf * lax.rsqrt(jnp.mean(xf * xf, axis=-1, keepdims=True) + EPS)
    return (y * g.astype(jnp.float32)).astype(x.dtype)


def modulate(h, shift, scale):
    return h * (1.0 + scale[:, None, :]) + shift[:, None, :]


def t5_causal_bucket(n):
    nf = jnp.maximum(n, 1).astype(jnp.float32)
    large = MAX_EXACT + (jnp.log(nf / MAX_EXACT) / math.log(MAX_DISTANCE / MAX_EXACT)
                         * (N_BUCKETS - MAX_EXACT)).astype(jnp.int32)
    large = jnp.minimum(large, N_BUCKETS - 1)
    return jnp.where(n < MAX_EXACT, n, large)


def band_geometry(n_blocks):
    i = jnp.arange(BLOCK, dtype=jnp.int32)[:, None]
    j = jnp.arange(2 * BLOCK, dtype=jnp.int32)[None, :]
    dist = i - j + BLOCK
    blk = jnp.arange(n_blocks, dtype=jnp.int32)[:, None, None]
    key_pos = blk * BLOCK - BLOCK + j
    mask = (dist >= 0) & (dist < WINDOW) & (key_pos >= 0)
    bucket = t5_causal_bucket(jnp.maximum(dist, 0))
    return mask, bucket


def hgrn2_chunkwise(q, log_f, k, v):
    B, H, L, DK = q.shape
    DV = v.shape[-1]
    C = HG_CHUNK
    N = L // C
    q = q.reshape(B, H, N, C, DK)
    k = k.reshape(B, H, N, C, DK)
    v = v.reshape(B, H, N, C, DV)
    b = jnp.cumsum(log_f.reshape(B, H, N, C, DK), axis=3)
    ref = b[:, :, :, C // 2 - 1:C // 2]
    b_last = b[:, :, :, C - 1:]
    scores = jnp.einsum('bhncd,bhnsd->bhncs', q * jnp.exp(b - ref), k * jnp.exp(ref - b))
    causal = jnp.tril(jnp.ones((C, C), dtype=bool))
    scores = jnp.where(causal, scores, 0.0)
    o = jnp.einsum('bhncs,bhnsv->bhncv', scores, v)
    upd = jnp.einsum('bhncd,bhncv->bhndv', k * jnp.exp(b_last - b), v)
    decay = jnp.exp(b_last[:, :, :, 0])

    def step(S, xs):
        dec, u = xs
        return dec[..., None] * S + u, S

    _, S_prev = lax.scan(step, jnp.zeros((B, H, DK, DV), q.dtype),
                         (jnp.moveaxis(decay, 2, 0), jnp.moveaxis(upd, 2, 0)))
    S_prev = jnp.moveaxis(S_prev, 0, 2)
    o = o + jnp.einsum('bhncd,bhndv->bhncv', q * jnp.exp(b), S_prev)
    return o.reshape(B, H, L, DV)


def sink_swa(q, k, v, sinks, rel_bias_table):
    B, L = q.shape[0], q.shape[1]
    nb = L // BLOCK
    qb = q.reshape(B, nb, BLOCK, AT_KV_HEADS, AT_GROUP, AT_HEAD_DIM)

    def band(t):
        tb = t.reshape(B, nb, BLOCK, AT_KV_HEADS, AT_HEAD_DIM)
        prev = jnp.pad(tb, ((0, 0), (1, 0), (0, 0), (0, 0), (0, 0)))[:, :-1]
        return jnp.concatenate([prev, tb], axis=2)

    kk, vv = band(k), band(v)
    mask, bucket = band_geometry(nb)
    bias = jnp.transpose(rel_bias_table[bucket], (2, 0, 1)).astype(jnp.float32)
    bias = bias.reshape(AT_KV_HEADS, AT_GROUP, BLOCK, 2 * BLOCK)
    scale = AT_HEAD_DIM ** -0.5
    logits = jnp.einsum('bnqkgd,bnskd->bnkgqs', qb, kk).astype(jnp.float32) * scale + bias
    logits = jnp.where(mask[None, :, None, None], logits, NEG_INF)
    sink = jnp.broadcast_to(sinks.astype(jnp.float32).reshape(AT_KV_HEADS, AT_GROUP, 1, 1),
                            logits.shape[:-1] + (1,))
    p = jax.nn.softmax(jnp.concatenate([logits, sink], axis=-1), axis=-1)[..., :2 * BLOCK]
    o = jnp.einsum('bnkgqs,bnskd->bnqkgd', p.astype(vv.dtype), vv)
    return o.reshape(B, L, AT_WIDTH)


def _fwd_setup_inputs(seed: int = 0) -> dict:
    key = jax.random.key(seed)
    ks = jax.random.split(key, 20)

    def nrm(k, shape, s):
        return jax.random.normal(k, shape, jnp.float32) * s

    return {
        "x": nrm(ks[0], (BATCH, SEQ, D_MODEL), 1.0),
        "c": nrm(ks[1], (BATCH, D_MODEL), 1.0),
        "w_ada": nrm(ks[2], (DEPTH, D_MODEL, 6 * D_MODEL), 0.5 * D_MODEL ** -0.5),
        "b_ada": nrm(ks[3], (DEPTH, 6 * D_MODEL), 0.02),
        "norm1_g": 1.0 + nrm(ks[4], (DEPTH, D_MODEL), 0.02),
        "norm2_g": 1.0 + nrm(ks[5], (DEPTH, D_MODEL), 0.02),
        "w_in": nrm(ks[6], (DEPTH, D_MODEL, IN_WIDTH), D_MODEL ** -0.5),
        "hg_lb_logits": nrm(ks[7], (DEPTH + 1, HG_WIDTH), 1.0),
        "hg_out_norm_g": 1.0 + nrm(ks[8], (DEPTH, HG_DV), 0.02),
        "q_norm_g": 1.0 + nrm(ks[9], (DEPTH, AT_HEAD_DIM), 0.02),
        "k_norm_g": 1.0 + nrm(ks[10], (DEPTH, AT_HEAD_DIM), 0.02),
        "attn_sinks": nrm(ks[11], (DEPTH, AT_HEADS), 1.0),
        "rel_bias_table": nrm(ks[12], (N_BUCKETS, AT_HEADS), 0.5),
        "w_branch_hg": nrm(ks[13], (DEPTH, HG_WIDTH, D_MODEL), HG_WIDTH ** -0.5),
        "w_branch_attn": nrm(ks[14], (DEPTH, AT_WIDTH, D_MODEL), AT_WIDTH ** -0.5),
        "w_out": nrm(ks[15], (DEPTH, D_MODEL, D_MODEL), D_MODEL ** -0.5),
        "w_ff1": nrm(ks[16], (DEPTH, D_MODEL, D_FF), D_MODEL ** -0.5),
        "w_ff2": nrm(ks[17], (DEPTH, D_FF, D_MODEL), D_FF ** -0.5),
    }


def _fwd_reference(x, c, w_ada, b_ada, norm1_g, norm2_g, w_in, hg_lb_logits, hg_out_norm_g,
              q_norm_g, k_norm_g, attn_sinks, rel_bias_table, w_branch_hg, w_branch_attn,
              w_out, w_ff1, w_ff2):
    B, L, _ = x.shape
    lb_all = jnp.cumsum(jax.nn.softmax(hg_lb_logits.astype(jnp.float32), axis=0), axis=0)
    c_act = jax.nn.silu(c)
    for l in range(DEPTH):
        ada = c_act @ w_ada[l] + b_ada[l]
        shift1, scale1, gate1, shift2, scale2, gate2 = jnp.split(ada, 6, axis=-1)

        h = modulate(rms_norm(x, norm1_g[l]), shift1, scale1)
        proj = h @ w_in[l]
        hq, hf, hi, hg, aq, ak, av, gate_hg, gate_at = jnp.split(proj, IN_OFFSETS, axis=-1)

        lb = lb_all[l]
        f = lb + (1.0 - lb) * jax.nn.sigmoid(hf.astype(jnp.float32))
        log_f = jnp.log(f)

        def to_heads(t):
            return t.reshape(B, L, HG_HEADS, HG_DK).transpose(0, 2, 1, 3)

        o_hg = hgrn2_chunkwise(to_heads(jax.nn.silu(hq.astype(jnp.float32))), to_heads(log_f),
                               to_heads(1.0 - f), to_heads(hi.astype(jnp.float32)))
        o_hg = o_hg.transpose(0, 2, 1, 3).astype(x.dtype)
        o_hg = rms_norm(o_hg, hg_out_norm_g[l]) * jax.nn.silu(hg.reshape(B, L, HG_HEADS, HG_DV))
        o_hg = o_hg.reshape(B, L, HG_WIDTH)

        q = rms_norm(aq.reshape(B, L, AT_HEADS, AT_HEAD_DIM), q_norm_g[l])
        k = rms_norm(ak.reshape(B, L, AT_KV_HEADS, AT_HEAD_DIM), k_norm_g[l])
        v = av.reshape(B, L, AT_KV_HEADS, AT_HEAD_DIM)
        o_at = sink_swa(q, k, v, attn_sinks[l], rel_bias_table)

        merged = (jax.nn.sigmoid(gate_hg) * (o_hg @ w_branch_hg[l])
                  + jax.nn.sigmoid(gate_at) * (o_at @ w_branch_attn[l]))
        x = x + gate1[:, None, :] * (merged @ w_out[l])

        h2 = modulate(rms_norm(x, norm2_g[l]), shift2, scale2)
        ff = jnp.square(jax.nn.relu(h2 @ w_ff1[l])) @ w_ff2[l]
        x = x + gate2[:, None, :] * ff
    return x


import jax as _jax
import jax.numpy as _jnp

TWIN_FORMAT = 'train_step'
FWD_PARAMS = ['x', 'c', 'w_ada', 'b_ada', 'norm1_g', 'norm2_g', 'w_in', 'hg_lb_logits', 'hg_out_norm_g', 'q_norm_g', 'k_norm_g', 'attn_sinks', 'rel_bias_table', 'w_branch_hg', 'w_branch_attn', 'w_out', 'w_ff1', 'w_ff2']
TWIN_WEIGHTS = ['w_ada', 'b_ada', 'norm1_g', 'norm2_g', 'w_in', 'hg_lb_logits', 'hg_out_norm_g', 'q_norm_g', 'k_norm_g', 'attn_sinks', 'rel_bias_table', 'w_branch_hg', 'w_branch_attn', 'w_out', 'w_ff1', 'w_ff2']
TWIN_DIFF_INPUT = 'x'
TWIN_INPUTS = ['x', 'c', 'w_ada', 'b_ada', 'norm1_g', 'norm2_g', 'w_in', 'hg_lb_logits', 'hg_out_norm_g', 'q_norm_g', 'k_norm_g', 'attn_sinks', 'rel_bias_table', 'w_branch_hg', 'w_branch_attn', 'w_out', 'w_ff1', 'w_ff2', 'loss_target', 'm_w_ada', 'm_b_ada', 'm_norm1_g', 'm_norm2_g', 'm_w_in', 'm_hg_lb_logits', 'm_hg_out_norm_g', 'm_q_norm_g', 'm_k_norm_g', 'm_attn_sinks', 'm_rel_bias_table', 'm_w_branch_hg', 'm_w_branch_attn', 'm_w_out', 'm_w_ff1', 'm_w_ff2', 'v_w_ada', 'v_b_ada', 'v_norm1_g', 'v_norm2_g', 'v_w_in', 'v_hg_lb_logits', 'v_hg_out_norm_g', 'v_q_norm_g', 'v_k_norm_g', 'v_attn_sinks', 'v_rel_bias_table', 'v_w_branch_hg', 'v_w_branch_attn', 'v_w_out', 'v_w_ff1', 'v_w_ff2']
TWIN_OUTPUTS = ['loss', 'grad_x', 'grad_w_ada', 'grad_b_ada', 'grad_norm1_g', 'grad_norm2_g', 'grad_w_in', 'grad_hg_lb_logits', 'grad_hg_out_norm_g', 'grad_q_norm_g', 'grad_k_norm_g', 'grad_attn_sinks', 'grad_rel_bias_table', 'grad_w_branch_hg', 'grad_w_branch_attn', 'grad_w_out', 'grad_w_ff1', 'grad_w_ff2', 'delta_w_ada', 'delta_b_ada', 'delta_norm1_g', 'delta_norm2_g', 'delta_w_in', 'delta_hg_lb_logits', 'delta_hg_out_norm_g', 'delta_q_norm_g', 'delta_k_norm_g', 'delta_attn_sinks', 'delta_rel_bias_table', 'delta_w_branch_hg', 'delta_w_branch_attn', 'delta_w_out', 'delta_w_ff1', 'delta_w_ff2', 'new_m_w_ada', 'new_m_b_ada', 'new_m_norm1_g', 'new_m_norm2_g', 'new_m_w_in', 'new_m_hg_lb_logits', 'new_m_hg_out_norm_g', 'new_m_q_norm_g', 'new_m_k_norm_g', 'new_m_attn_sinks', 'new_m_rel_bias_table', 'new_m_w_branch_hg', 'new_m_w_branch_attn', 'new_m_w_out', 'new_m_w_ff1', 'new_m_w_ff2', 'new_v_w_ada', 'new_v_b_ada', 'new_v_norm1_g', 'new_v_norm2_g', 'new_v_w_in', 'new_v_hg_lb_logits', 'new_v_hg_out_norm_g', 'new_v_q_norm_g', 'new_v_k_norm_g', 'new_v_attn_sinks', 'new_v_rel_bias_table', 'new_v_w_branch_hg', 'new_v_w_branch_attn', 'new_v_w_out', 'new_v_w_ff1', 'new_v_w_ff2']
TWIN_LEAF_KINDS = {'loss': 'loss', 'grad_x': 'grad_x', 'grad_w_ada': 'grad_w', 'grad_b_ada': 'grad_w', 'grad_norm1_g': 'grad_w', 'grad_norm2_g': 'grad_w', 'grad_w_in': 'grad_w', 'grad_hg_lb_logits': 'grad_w', 'grad_hg_out_norm_g': 'grad_w', 'grad_q_norm_g': 'grad_w', 'grad_k_norm_g': 'grad_w', 'grad_attn_sinks': 'grad_w', 'grad_rel_bias_table': 'grad_w', 'grad_w_branch_hg': 'grad_w', 'grad_w_branch_attn': 'grad_w', 'grad_w_out': 'grad_w', 'grad_w_ff1': 'grad_w', 'grad_w_ff2': 'grad_w', 'delta_w_ada': 'delta_w', 'delta_b_ada': 'delta_w', 'delta_norm1_g': 'delta_w', 'delta_norm2_g': 'delta_w', 'delta_w_in': 'delta_w', 'delta_hg_lb_logits': 'delta_w', 'delta_hg_out_norm_g': 'delta_w', 'delta_q_norm_g': 'delta_w', 'delta_k_norm_g': 'delta_w', 'delta_attn_sinks': 'delta_w', 'delta_rel_bias_table': 'delta_w', 'delta_w_branch_hg': 'delta_w', 'delta_w_branch_attn': 'delta_w', 'delta_w_out': 'delta_w', 'delta_w_ff1': 'delta_w', 'delta_w_ff2': 'delta_w', 'new_m_w_ada': 'new_m', 'new_m_b_ada': 'new_m', 'new_m_norm1_g': 'new_m', 'new_m_norm2_g': 'new_m', 'new_m_w_in': 'new_m', 'new_m_hg_lb_logits': 'new_m', 'new_m_hg_out_norm_g': 'new_m', 'new_m_q_norm_g': 'new_m', 'new_m_k_norm_g': 'new_m', 'new_m_attn_sinks': 'new_m', 'new_m_rel_bias_table': 'new_m', 'new_m_w_branch_hg': 'new_m', 'new_m_w_branch_attn': 'new_m', 'new_m_w_out': 'new_m', 'new_m_w_ff1': 'new_m', 'new_m_w_ff2': 'new_m', 'new_v_w_ada': 'new_v', 'new_v_b_ada': 'new_v', 'new_v_norm1_g': 'new_v', 'new_v_norm2_g': 'new_v', 'new_v_w_in': 'new_v', 'new_v_hg_lb_logits': 'new_v', 'new_v_hg_out_norm_g': 'new_v', 'new_v_q_norm_g': 'new_v', 'new_v_k_norm_g': 'new_v', 'new_v_attn_sinks': 'new_v', 'new_v_rel_bias_table': 'new_v', 'new_v_w_branch_hg': 'new_v', 'new_v_w_branch_attn': 'new_v', 'new_v_w_out': 'new_v', 'new_v_w_ff1': 'new_v', 'new_v_w_ff2': 'new_v'}


def _forward(args):
    return _fwd_reference(*[args[k] for k in FWD_PARAMS])


def _output_shape():
    out = _jax.eval_shape(lambda: _forward(_fwd_setup_inputs(0)))
    return out.shape, out.dtype

N_MICROBATCH = 1
ADAM_LR = 0.001
ADAM_B1 = 0.9
ADAM_B2 = 0.999
ADAM_EPS = 1e-08
ADAM_WD = 0.01
ADAM_STEP = 10
PER_EXAMPLE_BATCH_AXIS = {'x': 0, 'c': 0, 'loss_target': 0}
SHARED_INPUTS = []
_WEIGHT_DTYPES = {'w_ada': _jnp.float32, 'b_ada': _jnp.float32, 'norm1_g': _jnp.float32, 'norm2_g': _jnp.float32, 'w_in': _jnp.float32, 'hg_lb_logits': _jnp.float32, 'hg_out_norm_g': _jnp.float32, 'q_norm_g': _jnp.float32, 'k_norm_g': _jnp.float32, 'attn_sinks': _jnp.float32, 'rel_bias_table': _jnp.float32, 'w_branch_hg': _jnp.float32, 'w_branch_attn': _jnp.float32, 'w_out': _jnp.float32, 'w_ff1': _jnp.float32, 'w_ff2': _jnp.float32}
MOMENT_SCALE = {'w_ada': 8.201043e-01, 'b_ada': 1.770448e+00, 'norm1_g': 1.177831e-01, 'norm2_g': 3.056669e+00, 'w_in': 3.230093e-02, 'hg_lb_logits': 1.198905e-03, 'hg_out_norm_g': 1.508261e+00, 'q_norm_g': 9.742395e-02, 'k_norm_g': 9.718619e-02, 'attn_sinks': 3.237911e-02, 'rel_bias_table': 9.371608e-03, 'w_branch_hg': 2.893096e-02, 'w_branch_attn': 5.108382e-02, 'w_out': 5.719382e-02, 'w_ff1': 9.466567e-02, 'w_ff2': 3.649775e-01}


def _to_microbatches(a, axis):
    t = _jnp.moveaxis(a, axis, 0)
    t = t.reshape((N_MICROBATCH, t.shape[0] // N_MICROBATCH) + t.shape[1:])
    return _jnp.moveaxis(t, 1, axis + 1)


def setup_inputs(seed: int = 0) -> dict:
    inp = _fwd_setup_inputs(seed)
    key = _jax.random.fold_in(_jax.random.key(seed), 7919)
    shape, _ = _output_shape()
    out = dict(inp)
    out["loss_target"] = _jax.random.normal(_jax.random.fold_in(key, 0), shape, _jnp.float32)
    for i, name in enumerate(TWIN_WEIGHTS):
        w = inp[name].astype(_jnp.float32)
        if MOMENT_SCALE is None:
            s = _jnp.sqrt(_jnp.mean(_jnp.square(w)) + 1e-30)
        else:
            s = MOMENT_SCALE[name]
        km, kv = _jax.random.split(_jax.random.fold_in(key, i + 1))
        out[name] = w
        out["m_" + name] = s * _jax.random.normal(km, w.shape, _jnp.float32)
        out["v_" + name] = (s * s) * _jax.random.uniform(kv, w.shape, _jnp.float32, 0.5, 1.5)
    if N_MICROBATCH > 1:
        for name, axis in PER_EXAMPLE_BATCH_AXIS.items():
            out[name] = _to_microbatches(out[name], axis)
    return {'x': out['x'], 'c': out['c'], 'w_ada': out['w_ada'], 'b_ada': out['b_ada'], 'norm1_g': out['norm1_g'], 'norm2_g': out['norm2_g'], 'w_in': out['w_in'], 'hg_lb_logits': out['hg_lb_logits'], 'hg_out_norm_g': out['hg_out_norm_g'], 'q_norm_g': out['q_norm_g'], 'k_norm_g': out['k_norm_g'], 'attn_sinks': out['attn_sinks'], 'rel_bias_table': out['rel_bias_table'], 'w_branch_hg': out['w_branch_hg'], 'w_branch_attn': out['w_branch_attn'], 'w_out': out['w_out'], 'w_ff1': out['w_ff1'], 'w_ff2': out['w_ff2'], 'loss_target': out['loss_target'], 'm_w_ada': out['m_w_ada'], 'm_b_ada': out['m_b_ada'], 'm_norm1_g': out['m_norm1_g'], 'm_norm2_g': out['m_norm2_g'], 'm_w_in': out['m_w_in'], 'm_hg_lb_logits': out['m_hg_lb_logits'], 'm_hg_out_norm_g': out['m_hg_out_norm_g'], 'm_q_norm_g': out['m_q_norm_g'], 'm_k_norm_g': out['m_k_norm_g'], 'm_attn_sinks': out['m_attn_sinks'], 'm_rel_bias_table': out['m_rel_bias_table'], 'm_w_branch_hg': out['m_w_branch_hg'], 'm_w_branch_attn': out['m_w_branch_attn'], 'm_w_out': out['m_w_out'], 'm_w_ff1': out['m_w_ff1'], 'm_w_ff2': out['m_w_ff2'], 'v_w_ada': out['v_w_ada'], 'v_b_ada': out['v_b_ada'], 'v_norm1_g': out['v_norm1_g'], 'v_norm2_g': out['v_norm2_g'], 'v_w_in': out['v_w_in'], 'v_hg_lb_logits': out['v_hg_lb_logits'], 'v_hg_out_norm_g': out['v_hg_out_norm_g'], 'v_q_norm_g': out['v_q_norm_g'], 'v_k_norm_g': out['v_k_norm_g'], 'v_attn_sinks': out['v_attn_sinks'], 'v_rel_bias_table': out['v_rel_bias_table'], 'v_w_branch_hg': out['v_w_branch_hg'], 'v_w_branch_attn': out['v_w_branch_attn'], 'v_w_out': out['v_w_out'], 'v_w_ff1': out['v_w_ff1'], 'v_w_ff2': out['v_w_ff2']}


def _loss(weights, diff, rest, loss_target):
    with _jax.named_scope("forward"):
        args = {**rest, TWIN_DIFF_INPUT: diff, **{k: w.astype(_WEIGHT_DTYPES[k]) for k, w in weights.items()}}
        y = _forward(args)
    with _jax.named_scope("loss_head"):
        err = _jnp.square(y.astype(_jnp.float32) - loss_target)
        return 0.5 * _jnp.sum(_jnp.mean(err, axis=-1)) if err.ndim else 0.5 * err


def _adamw(w, g, m, v):
    m = ADAM_B1 * m + (1.0 - ADAM_B1) * g
    v = ADAM_B2 * v + (1.0 - ADAM_B2) * _jnp.square(g)
    m_hat = m / (1.0 - ADAM_B1 ** ADAM_STEP)
    v_hat = v / (1.0 - ADAM_B2 ** ADAM_STEP)
    delta = -ADAM_LR * (m_hat / (_jnp.sqrt(v_hat) + ADAM_EPS) + ADAM_WD * w)
    return delta, m, v


def reference(x, c, w_ada, b_ada, norm1_g, norm2_g, w_in, hg_lb_logits, hg_out_norm_g, q_norm_g, k_norm_g, attn_sinks, rel_bias_table, w_branch_hg, w_branch_attn, w_out, w_ff1, w_ff2, loss_target, m_w_ada, m_b_ada, m_norm1_g, m_norm2_g, m_w_in, m_hg_lb_logits, m_hg_out_norm_g, m_q_norm_g, m_k_norm_g, m_attn_sinks, m_rel_bias_table, m_w_branch_hg, m_w_branch_attn, m_w_out, m_w_ff1, m_w_ff2, v_w_ada, v_b_ada, v_norm1_g, v_norm2_g, v_w_in, v_hg_lb_logits, v_hg_out_norm_g, v_q_norm_g, v_k_norm_g, v_attn_sinks, v_rel_bias_table, v_w_branch_hg, v_w_branch_attn, v_w_out, v_w_ff1, v_w_ff2):
    given = dict(x=x, c=c, w_ada=w_ada, b_ada=b_ada, norm1_g=norm1_g, norm2_g=norm2_g, w_in=w_in, hg_lb_logits=hg_lb_logits, hg_out_norm_g=hg_out_norm_g, q_norm_g=q_norm_g, k_norm_g=k_norm_g, attn_sinks=attn_sinks, rel_bias_table=rel_bias_table, w_branch_hg=w_branch_hg, w_branch_attn=w_branch_attn, w_out=w_out, w_ff1=w_ff1, w_ff2=w_ff2, loss_target=loss_target, m_w_ada=m_w_ada, m_b_ada=m_b_ada, m_norm1_g=m_norm1_g, m_norm2_g=m_norm2_g, m_w_in=m_w_in, m_hg_lb_logits=m_hg_lb_logits, m_hg_out_norm_g=m_hg_out_norm_g, m_q_norm_g=m_q_norm_g, m_k_norm_g=m_k_norm_g, m_attn_sinks=m_attn_sinks, m_rel_bias_table=m_rel_bias_table, m_w_branch_hg=m_w_branch_hg, m_w_branch_attn=m_w_branch_attn, m_w_out=m_w_out, m_w_ff1=m_w_ff1, m_w_ff2=m_w_ff2, v_w_ada=v_w_ada, v_b_ada=v_b_ada, v_norm1_g=v_norm1_g, v_norm2_g=v_norm2_g, v_w_in=v_w_in, v_hg_lb_logits=v_hg_lb_logits, v_hg_out_norm_g=v_hg_out_norm_g, v_q_norm_g=v_q_norm_g, v_k_norm_g=v_k_norm_g, v_attn_sinks=v_attn_sinks, v_rel_bias_table=v_rel_bias_table, v_w_branch_hg=v_w_branch_hg, v_w_branch_attn=v_w_branch_attn, v_w_out=v_w_out, v_w_ff1=v_w_ff1, v_w_ff2=v_w_ff2)
    weights = {n: given[n] for n in TWIN_WEIGHTS}
    shared = {n: given[n] for n in SHARED_INPUTS}
    per_example = {n: given[n] for n in ['x', 'c']}
    grad_fn = _jax.value_and_grad(_loss, argnums=(0, 1))

    def one_microbatch(ex, loss_target):
        ex = dict(ex)
        diff = ex.pop(TWIN_DIFF_INPUT)
        return grad_fn(weights, diff, {**shared, **ex}, loss_target)

    if N_MICROBATCH == 1:
        loss, (grad_w, grad_x) = one_microbatch(per_example, given["loss_target"])
    else:
        def body(carry, xs):
            loss_sum, grad_sum = carry
            l_k, (gw_k, gx_k) = one_microbatch(xs[0], xs[1])
            with _jax.named_scope("update"):
                return (loss_sum + l_k, _jax.tree.map(_jnp.add, grad_sum, gw_k)), gx_k

        init = (_jnp.zeros((), _jnp.float32), _jax.tree.map(_jnp.zeros_like, weights))
        (loss, grad_w), grad_x = _jax.lax.scan(body, init, (per_example, given["loss_target"]))
    with _jax.named_scope("update"):
        delta_w, new_m, new_v = {}, {}, {}
        for n in TWIN_WEIGHTS:
            delta_w[n], new_m[n], new_v[n] = _adamw(weights[n], grad_w[n], given["m_" + n], given["v_" + n])
    return (loss, grad_x, *[grad_w[n] for n in TWIN_WEIGHTS], *[delta_w[n] for n in TWIN_WEIGHTS],
            *[new_m[n] for n in TWIN_WEIGHTS], *[new_v[n] for n in TWIN_WEIGHTS])
```

```python
import functools
import math

import jax
import jax.numpy as jnp
from jax import lax
from jax.experimental import pallas as pl
from jax.experimental.pallas import tpu as pltpu

F32 = jnp.float32
BF16 = jnp.bfloat16
EPS = 1e-6
NEG_INF = -1e30
HG_DK = 128
HG_CHUNK = 64
AT_BLOCK = 128
N_BUCKETS = 32
MAX_EXACT = 16
MAX_DISTANCE = 128
N_DEV = 8
VMEM_LIMIT = 56 * 1024 * 1024
ADAM_LR, ADAM_B1, ADAM_B2, ADAM_EPS, ADAM_WD, ADAM_STEP = 0.001, 0.9, 0.999, 1e-08, 0.01, 10
HIGHEST = lax.Precision.HIGHEST

NN = (((1,), (0,)), ((), ()))
NT = (((1,), (1,)), ((), ()))
TN = (((0,), (0,)), ((), ()))


def _pcall(body, **kw):
    return pl.pallas_call(body, **kw)


def _params(sem=None):
    return pltpu.CompilerParams(dimension_semantics=sem, vmem_limit_bytes=VMEM_LIMIT)


def _tile(n, pref, unit):
    if n <= pref:
        return n
    t = (pref // unit) * unit
    while t >= unit:
        if n % t == 0:
            return t
        t -= unit
    return n


def _dot(a, b, dn, precision=None):
    return lax.dot_general(a, b, dn, preferred_element_type=F32, precision=precision)


def _bdot(a, b, dn):
    return _dot(a.astype(BF16), b.astype(BF16), dn)


def _exchange(items, name):
    n = len(items)

    def body(*refs):
        ins, outs = refs[:n], refs[n:2 * n]
        send_sems, recv_sems = refs[2 * n], refs[2 * n + 1]
        x, y, c = lax.axis_index("x"), lax.axis_index("y"), lax.axis_index("c")
        copies = []
        for k, (_, rel) in enumerate(items):
            peer = (1 - x if rel & 4 else x, 1 - y if rel & 2 else y, 1 - c if rel & 1 else c)
            cp = pltpu.make_async_remote_copy(
                src_ref=ins[k], dst_ref=outs[k], send_sem=send_sems.at[k], recv_sem=recv_sems.at[k],
                device_id=peer, device_id_type=pl.DeviceIdType.MESH)
            cp.start()
            copies.append(cp)
        for cp in copies:
            cp.wait()

    outs = _pcall(
        body, name=name,
        out_shape=tuple(jax.ShapeDtypeStruct(a.shape, a.dtype) for a, _ in items),
        in_specs=[pl.BlockSpec(memory_space=pl.ANY)] * n,
        out_specs=tuple(pl.BlockSpec(memory_space=pl.ANY) for _ in items),
        scratch_shapes=[pltpu.SemaphoreType.DMA((n,)), pltpu.SemaphoreType.DMA((n,))],
    )(*[a for a, _ in items])
    return list(outs)


def _gather_small(v, me, name):
    got = _exchange([(v, rel) for rel in range(1, N_DEV)], name)
    out = jnp.zeros((N_DEV,) + v.shape, v.dtype)
    out = lax.dynamic_update_slice(out, v[None], (me, 0, 0))
    for rel in range(1, N_DEV):
        out = lax.dynamic_update_slice(out, got[rel - 1][None], (me ^ rel, 0, 0))
    return out


CHIP_RELS = (0, 4, 2, 6)


def _allgather_weights(blocks, me, name):
    nw = len(blocks)
    g1 = _exchange([(b, rel) for b in blocks for rel in CHIP_RELS[1:]], name + "_ici")
    by_rel = []
    for i, b in enumerate(blocks):
        by_rel.append({0: b, 4: g1[3 * i], 2: g1[3 * i + 1], 6: g1[3 * i + 2]})
    g2 = _exchange([(by_rel[i][rel], 1) for i in range(nw) for rel in CHIP_RELS], name + "_d2d")
    for i in range(nw):
        for j, rel in enumerate(CHIP_RELS):
            by_rel[i][rel ^ 1] = g2[4 * i + j]
    return by_rel


def _assemble(by_rel, me, axis):
    blk = by_rel[0]
    shape = list(blk.shape)
    n = shape[axis]
    shape[axis] = n * N_DEV
    full = jnp.zeros(shape, blk.dtype)
    for rel in range(N_DEV):
        idx = [0, 0]
        idx[axis] = (me ^ rel) * n
        full = lax.dynamic_update_slice(full, by_rel[rel], tuple(idx))
    return full


def _block_of(full, d, axis):
    n = full.shape[axis] // N_DEV
    idx = [0, 0]
    idx[axis] = d * n
    sizes = list(full.shape)
    sizes[axis] = n
    return lax.dynamic_slice(full, tuple(idx), tuple(sizes))


def _mm(a, b, mode, out_dtype, name, tm=1024, tn=1024, tk=512):
    if mode == "nn":
        (M, K), (K2, N) = a.shape, b.shape
    elif mode == "nt":
        (M, K), (N, K2) = a.shape, b.shape
    else:
        (K, M), (K2, N) = a.shape, b.shape
    assert K == K2, (a.shape, b.shape, mode)
    tm, tn, tk = _tile(M, tm, 128), _tile(N, tn, 128), _tile(K, tk, 128)
    nk = K // tk
    dn = {"nn": NN, "nt": NT, "tn": TN}[mode]

    def body(a_ref, b_ref, o_ref, acc_ref):
        k = pl.program_id(2)

        @pl.when(k == 0)
        def _():
            acc_ref[...] = jnp.zeros_like(acc_ref)

        acc_ref[...] += _bdot(a_ref[...], b_ref[...], dn)

        @pl.when(k == nk - 1)
        def _():
            o_ref[...] = acc_ref[...].astype(out_dtype)

    a_spec = pl.BlockSpec((tk, tm), lambda i, j, k: (k, i)) if mode == "tn" else pl.BlockSpec((tm, tk), lambda i, j, k: (i, k))
    b_spec = pl.BlockSpec((tn, tk), lambda i, j, k: (j, k)) if mode == "nt" else pl.BlockSpec((tk, tn), lambda i, j, k: (k, j))
    return _pcall(
        body, name=name, grid=(M // tm, N // tn, nk),
        out_shape=jax.ShapeDtypeStruct((M, N), out_dtype),
        in_specs=[a_spec, b_spec], out_specs=pl.BlockSpec((tm, tn), lambda i, j, k: (i, j)),
        scratch_shapes=[pltpu.VMEM((tm, tn), F32)],
        compiler_params=_params(("parallel", "parallel", "arbitrary")),
    )(a, b)


def _rowwise(fn, row_ins, bcast_ins, row_outs, acc_outs, name, rt=256):
    L = row_ins[0][0].shape[0]
    rt = _tile(L, rt, 16)
    nr, nb, no, na = len(row_ins), len(bcast_ins), len(row_outs), len(acc_outs)

    def body(*refs):
        i = pl.program_id(0)
        vals = [r[...] for r in refs[:nr + nb]]
        outs, accs = fn(*vals)
        for r, v in zip(refs[nr + nb:nr + nb + no], outs):
            r[...] = v.astype(r.dtype)
        acc_refs = refs[nr + nb + no:]

        @pl.when(i == 0)
        def _():
            for r in acc_refs:
                r[...] = jnp.zeros_like(r)

        for r, v in zip(acc_refs, accs):
            r[...] += v

    in_specs = [pl.BlockSpec((rt, w), functools.partial(lambda i, cb: (i, cb), cb=cb)) for _, w, cb in row_ins]
    in_specs += [pl.BlockSpec(b.shape, lambda i: (0, 0)) for b in bcast_ins]
    out_specs = [pl.BlockSpec((rt, w), lambda i: (i, 0)) for w, _ in row_outs]
    out_specs += [pl.BlockSpec(s, lambda i: (0, 0)) for s in acc_outs]
    out_shape = [jax.ShapeDtypeStruct((L, w), dt) for w, dt in row_outs] + [jax.ShapeDtypeStruct(s, F32) for s in acc_outs]
    res = _pcall(
        body, name=name, grid=(L // rt,), out_shape=tuple(out_shape), in_specs=in_specs, out_specs=tuple(out_specs),
        compiler_params=_params(("arbitrary",)),
    )(*[a for a, _, _ in row_ins], *bcast_ins)
    return list(res)


def _whole(fn, ins, out_shapes, name):
    def body(*refs):
        outs = fn(*[r[...] for r in refs[:len(ins)]])
        for r, v in zip(refs[len(ins):], outs):
            r[...] = v.astype(r.dtype)

    res = _pcall(body, name=name, out_shape=tuple(jax.ShapeDtypeStruct(s, dt) for s, dt in out_shapes),
                 compiler_params=_params())(*ins)
    return list(res)


def _silu(x):
    return x * jax.nn.sigmoid(x)


def _rms(x, g):
    return (x * lax.rsqrt(jnp.mean(x * x, axis=-1, keepdims=True) + EPS)) * g


def _modnorm(x, g, shift, scale):
    return _rms(x, g) * (1.0 + scale) + shift


def _adamw(w, g, m, v):
    m = ADAM_B1 * m + (1.0 - ADAM_B1) * g
    v = ADAM_B2 * v + (1.0 - ADAM_B2) * jnp.square(g)
    m_hat = m / (1.0 - ADAM_B1 ** ADAM_STEP)
    v_hat = v / (1.0 - ADAM_B2 ** ADAM_STEP)
    delta = -ADAM_LR * (m_hat / (jnp.sqrt(v_hat) + ADAM_EPS) + ADAM_WD * w)
    return delta, m, v


def _lower_bound(lg):
    e = jnp.exp(lg - jnp.max(lg, axis=0, keepdims=True))
    return e[0:1] / jnp.sum(e, axis=0, keepdims=True)


def _hg_chunk(hq, hf, hi, lb, st):
    C = hq.shape[0]
    row = lax.broadcasted_iota(jnp.int32, (C, C), 0)
    col = lax.broadcasted_iota(jnp.int32, (C, C), 1)
    tri = row >= col
    sg = jax.nn.sigmoid(hf)
    f = lb + (1.0 - lb) * sg
    lf = jnp.log(f)
    k = 1.0 - f
    q = _silu(hq)
    b = _dot(tri.astype(F32), lf, NN, precision=HIGHEST)
    m = b[C // 2 - 1:C // 2]
    bl = b[C - 1:C]
    e_qm, e_km, e_kl, e_q = jnp.exp(b - m), jnp.exp(m - b), jnp.exp(bl - b), jnp.exp(b)
    qe, ke, kd, qb = q * e_qm, k * e_km, k * e_kl, q * e_q
    sc = jnp.where(tri, _bdot(qe, ke, NT), 0.0)
    o = _bdot(sc, hi, NN) + _bdot(qb, st, NT)
    dec = jnp.exp(bl)
    st_next = st * dec + _bdot(hi, kd, TN)
    return o, st_next, dict(tri=tri, sg=sg, f=f, k=k, q=q, qe=qe, ke=ke, kd=kd, qb=qb, sc=sc, dec=dec,
                            e_qm=e_qm, e_km=e_km, e_kl=e_kl, e_q=e_q)


def _hg_out(o, hgate, gout):
    return _rms(o, gout) * _silu(hgate)


def _hgrn_fwd(p4, lb_logits, gout, H):
    L = p4.shape[0]
    N = L // HG_CHUNK

    def body(hq_ref, hf_ref, hi_ref, hg_ref, lg_ref, gout_ref, o_ref, s_ref, st_ref):
        @pl.when(pl.program_id(1) == 0)
        def _():
            st_ref[...] = jnp.zeros_like(st_ref)

        st = st_ref[...]
        s_ref[0, 0] = st
        o, st_next, _ = _hg_chunk(hq_ref[...], hf_ref[...], hi_ref[...], _lower_bound(lg_ref[...]), st)
        st_ref[...] = st_next
        o_ref[...] = _hg_out(o, hg_ref[...], gout_ref[...]).astype(o_ref.dtype)

    blk = lambda s: pl.BlockSpec((HG_CHUNK, HG_DK), functools.partial(lambda h, n, s: (n, s * H + h), s=s))
    return _pcall(
        body, name="hgrn_fwd", grid=(H, N),
        out_shape=(jax.ShapeDtypeStruct((L, H * HG_DK), BF16), jax.ShapeDtypeStruct((H, N, HG_DK, HG_DK), F32)),
        in_specs=[blk(0), blk(1), blk(2), blk(3), pl.BlockSpec((2, HG_DK), lambda h, n: (0, h)),
                  pl.BlockSpec((1, HG_DK), lambda h, n: (0, 0))],
        out_specs=(pl.BlockSpec((HG_CHUNK, HG_DK), lambda h, n: (n, h)),
                   pl.BlockSpec((1, 1, HG_DK, HG_DK), lambda h, n: (h, n, 0, 0))),
        scratch_shapes=[pltpu.VMEM((HG_DK, HG_DK), F32)],
        compiler_params=_params(("arbitrary", "arbitrary")),
    )(p4, p4, p4, p4, lb_logits, gout)


def _hgrn_bwd(p4, lb_logits, gout, s_all, d_out, H):
    L = p4.shape[0]
    N = L // HG_CHUNK
    C = HG_CHUNK

    def body(hq_ref, hf_ref, hi_ref, hg_ref, lg_ref, gout_ref, s_ref, do_ref,
             dq_ref, df_ref, di_ref, dg_ref, dlb_ref, dgo_ref, dst_ref):
        @pl.when(pl.program_id(1) == 0)
        def _():
            dst_ref[...] = jnp.zeros_like(dst_ref)
            dlb_ref[...] = jnp.zeros_like(dlb_ref)

        @pl.when(jnp.logical_and(pl.program_id(0) == 0, pl.program_id(1) == 0))
        def _():
            dgo_ref[...] = jnp.zeros_like(dgo_ref)

        hq, hf, hi, hgate = hq_ref[...], hf_ref[...], hi_ref[...], hg_ref[...]
        lb = _lower_bound(lg_ref[...])
        st = s_ref[0, 0]
        o, _, t = _hg_chunk(hq, hf, hi, lb, st)
        _, out_vjp = jax.vjp(_hg_out, o, hgate, gout_ref[...])
        do, d_hgate, d_gout = out_vjp(do_ref[...])
        dst_next = dst_ref[...]
        tri = t["tri"]
        dsc = jnp.where(tri, _bdot(do, hi, NT), 0.0)
        dv = _bdot(t["sc"], do, TN) + _bdot(t["kd"], dst_next, NT)
        dqe = _bdot(dsc, t["ke"], NN)
        dke = _bdot(dsc, t["qe"], TN)
        dqb = _bdot(do, st, NN)
        dkd = _bdot(hi, dst_next, NN)
        ddec = jnp.sum(dst_next * st, axis=0, keepdims=True)
        dst_ref[...] = _bdot(do, t["qb"], TN) + dst_next * t["dec"]
        dq = dqe * t["e_qm"] + dqb * t["e_q"]
        dk = dke * t["e_km"] + dkd * t["e_kl"]
        tq, tk, td, tb = dqe * t["qe"], dke * t["ke"], dkd * t["kd"], dqb * t["qb"]
        db = tq - tk - td + tb
        dm = jnp.sum(tk - tq, axis=0, keepdims=True)
        dbl = jnp.sum(td, axis=0, keepdims=True) + ddec * t["dec"]
        rowi = lax.broadcasted_iota(jnp.int32, (C, HG_DK), 0)
        db = db + jnp.where(rowi == C // 2 - 1, dm, 0.0) + jnp.where(rowi == C - 1, dbl, 0.0)
        dlf = _dot(tri.astype(F32), db, TN, precision=HIGHEST)
        df = dlf / t["f"] - dk
        sg = t["sg"]
        df_ref[...] = (df * (1.0 - lb) * sg * (1.0 - sg)).astype(df_ref.dtype)
        dlb_ref[...] += jnp.sum(df * (1.0 - sg), axis=0, keepdims=True)
        sq = jax.nn.sigmoid(hq)
        dq_ref[...] = (dq * (sq * (1.0 + hq * (1.0 - sq)))).astype(dq_ref.dtype)
        di_ref[...] = dv.astype(di_ref.dtype)
        dg_ref[...] = d_hgate.astype(dg_ref.dtype)
        dgo_ref[...] += d_gout

    blk = lambda s: pl.BlockSpec((C, HG_DK), functools.partial(lambda h, n, s: (N - 1 - n, s * H + h), s=s))
    oblk = pl.BlockSpec((C, HG_DK), lambda h, n: (N - 1 - n, h))
    vec = pl.BlockSpec((1, HG_DK), lambda h, n: (0, h))
    W = H * HG_DK
    return _pcall(
        body, name="hgrn_bwd", grid=(H, N),
        out_shape=tuple([jax.ShapeDtypeStruct((L, W), BF16)] * 4 + [jax.ShapeDtypeStruct((1, W), F32), jax.ShapeDtypeStruct((1, HG_DK), F32)]),
        in_specs=[blk(0), blk(1), blk(2), blk(3), pl.BlockSpec((2, HG_DK), lambda h, n: (0, h)),
                  pl.BlockSpec((1, HG_DK), lambda h, n: (0, 0)),
                  pl.BlockSpec((1, 1, HG_DK, HG_DK), lambda h, n: (h, N - 1 - n, 0, 0)), oblk],
        out_specs=(oblk, oblk, oblk, oblk, vec, pl.BlockSpec((1, HG_DK), lambda h, n: (0, 0))),
        scratch_shapes=[pltpu.VMEM((HG_DK, HG_DK), F32)],
        compiler_params=_params(("arbitrary", "arbitrary")),
    )(p4, p4, p4, p4, lb_logits, gout, s_all, d_out)


def _bucket_ids():
    i = jnp.arange(AT_BLOCK, dtype=jnp.int32)[:, None]
    j = jnp.arange(2 * AT_BLOCK, dtype=jnp.int32)[None, :]
    n = jnp.maximum(i - j + AT_BLOCK, 0)
    nf = jnp.maximum(n, 1).astype(F32)
    large = MAX_EXACT + (jnp.log(nf / MAX_EXACT) / math.log(MAX_DISTANCE / MAX_EXACT) * (N_BUCKETS - MAX_EXACT)).astype(jnp.int32)
    large = jnp.minimum(large, N_BUCKETS - 1)
    return jnp.where(n < MAX_EXACT, n, large).reshape(1, -1)


def _onehot(bucket):
    ids = lax.broadcasted_iota(jnp.int32, (N_BUCKETS, bucket.shape[1]), 0)
    return (ids == bucket).astype(F32)


def _attn_probs(qn, kpn, kcn, bias_g, sink, first, scale):
    i = lax.broadcasted_iota(jnp.int32, (AT_BLOCK, AT_BLOCK), 0)
    j = lax.broadcasted_iota(jnp.int32, (AT_BLOCK, AT_BLOCK), 1)
    lp = _bdot(qn, kpn, NT) * scale + bias_g[:, :AT_BLOCK]
    lc = _bdot(qn, kcn, NT) * scale + bias_g[:, AT_BLOCK:]
    lp = jnp.where(jnp.logical_and(j > i, jnp.logical_not(first)), lp, NEG_INF)
    lc = jnp.where(j <= i, lc, NEG_INF)
    m = jnp.maximum(jnp.maximum(jnp.max(lp, axis=-1, keepdims=True), jnp.max(lc, axis=-1, keepdims=True)), sink)
    pp, pc, ps = jnp.exp(lp - m), jnp.exp(lc - m), jnp.exp(sink - m)
    den = jnp.sum(pp, axis=-1, keepdims=True) + jnp.sum(pc, axis=-1, keepdims=True) + ps
    return pp / den, pc / den, ps / den


def _attn_fwd(q_t, kp, vp, qg, kg, sinks, bias, KVH):
    AH, L, DH = q_t.shape
    G = AH // KVH
    NB = L // AT_BLOCK
    scale = DH ** -0.5

    def body(q_ref, kp_ref, kc_ref, vp_ref, vc_ref, qg_ref, kg_ref, sk_ref, b_ref, o_ref):
        first = pl.program_id(1) == 0
        kpn, kcn = _rms(kp_ref[0], kg_ref[...]), _rms(kc_ref[0], kg_ref[...])
        for g in range(G):
            qn = _rms(q_ref[g], qg_ref[...])
            pp, pc, _ = _attn_probs(qn, kpn, kcn, b_ref[g], sk_ref[0, :, g:g + 1], first, scale)
            o_ref[g] = (_bdot(pp, vp_ref[0], NN) + _bdot(pc, vc_ref[0], NN)).astype(o_ref.dtype)

    kblk = lambda off: pl.BlockSpec((1, AT_BLOCK, DH), functools.partial(lambda h, n, off: (h, n + off, 0), off=off))
    return _pcall(
        body, name="attn_fwd", grid=(KVH, NB),
        out_shape=jax.ShapeDtypeStruct((AH, L, DH), BF16),
        in_specs=[pl.BlockSpec((G, AT_BLOCK, DH), lambda h, n: (h, n, 0)), kblk(0), kblk(1), kblk(0), kblk(1),
                  pl.BlockSpec((1, DH), lambda h, n: (0, 0)), pl.BlockSpec((1, DH), lambda h, n: (0, 0)),
                  pl.BlockSpec((1, 1, G), lambda h, n: (h, 0, 0)),
                  pl.BlockSpec((G, AT_BLOCK, 2 * AT_BLOCK), lambda h, n: (h, 0, 0))],
        out_specs=pl.BlockSpec((G, AT_BLOCK, DH), lambda h, n: (h, n, 0)),
        compiler_params=_params(("arbitrary", "arbitrary")),
    )(q_t, kp, kp, vp, vp, qg, kg, sinks, bias)


def _attn_bwd(q_t, kp, vp, qg, kg, sinks, bias, do_t, KVH):
    AH, L, DH = q_t.shape
    G = AH // KVH
    NB = L // AT_BLOCK
    B = AT_BLOCK
    scale = DH ** -0.5

    def body(q_ref, kp_ref, kc_ref, vp_ref, vc_ref, qg_ref, kg_ref, sk_ref, b_ref, do_ref,
             dq_ref, dk_ref, dv_ref, dqg_ref, dkg_ref, dsk_ref, db_ref):
        n = pl.program_id(1)
        first = n == 0

        @pl.when(first)
        def _():
            for r in (dk_ref, dv_ref, dsk_ref, db_ref):
                r[...] = jnp.zeros_like(r)

        @pl.when(jnp.logical_and(first, pl.program_id(0) == 0))
        def _():
            dqg_ref[...] = jnp.zeros_like(dqg_ref)
            dkg_ref[...] = jnp.zeros_like(dkg_ref)

        kp_raw, kc_raw, kgv, qgv = kp_ref[0], kc_ref[0], kg_ref[...], qg_ref[...]
        kpn, kp_vjp = jax.vjp(_rms, kp_raw, kgv)
        kcn, kc_vjp = jax.vjp(_rms, kc_raw, kgv)
        dkpn = jnp.zeros((B, DH), F32)
        dkcn = jnp.zeros((B, DH), F32)
        dvp = jnp.zeros((B, DH), F32)
        dvc = jnp.zeros((B, DH), F32)
        dqg = jnp.zeros((1, DH), F32)
        dsk = []
        for g in range(G):
            qn, q_vjp = jax.vjp(_rms, q_ref[g], qgv)
            sink = sk_ref[0, :, g:g + 1]
            pp, pc, ps = _attn_probs(qn, kpn, kcn, b_ref[g], sink, first, scale)
            do = do_ref[g]
            dvp += _bdot(pp, do, TN)
            dvc += _bdot(pc, do, TN)
            dpp = _bdot(do, vp_ref[0], NT)
            dpc = _bdot(do, vc_ref[0], NT)
            dsum = jnp.sum(dpp * pp, axis=-1, keepdims=True) + jnp.sum(dpc * pc, axis=-1, keepdims=True)
            dlp = pp * (dpp - dsum)
            dlc = pc * (dpc - dsum)
            dsk.append(jnp.sum(-ps * dsum, axis=0, keepdims=True))
            db_ref[g, :, :B] += dlp
            db_ref[g, :, B:] += dlc
            dlp, dlc = dlp * scale, dlc * scale
            dqn = _bdot(dlp, kpn, NN) + _bdot(dlc, kcn, NN)
            dkpn += _bdot(dlp, qn, TN)
            dkcn += _bdot(dlc, qn, TN)
            dq_raw, dqg_g = q_vjp(dqn)
            dq_ref[g] = dq_raw.astype(dq_ref.dtype)
            dqg += dqg_g
        dkp_raw, dkg_p = kp_vjp(dkpn)
        dkc_raw, dkg_c = kc_vjp(dkcn)
        r0 = pl.multiple_of(n * B, B)
        r1 = pl.multiple_of(n * B + B, B)
        dk_ref[0, pl.ds(r0, B), :] += dkp_raw
        dk_ref[0, pl.ds(r1, B), :] += dkc_raw
        dv_ref[0, pl.ds(r0, B), :] += dvp
        dv_ref[0, pl.ds(r1, B), :] += dvc
        dqg_ref[...] += dqg
        dkg_ref[...] += dkg_p + dkg_c
        lane = lax.broadcasted_iota(jnp.int32, (1, G), 1)
        dsk_row = jnp.zeros((1, G), F32)
        for g in range(G):
            dsk_row = jnp.where(lane == g, dsk[g], dsk_row)
        dsk_ref[0] += dsk_row

    kblk = lambda off: pl.BlockSpec((1, B, DH), functools.partial(lambda h, n, off: (h, n + off, 0), off=off))
    qblk = pl.BlockSpec((G, B, DH), lambda h, n: (h, n, 0))
    accblk = pl.BlockSpec((1, L + B, DH), lambda h, n: (h, 0, 0))
    vecblk = pl.BlockSpec((1, DH), lambda h, n: (0, 0))
    return _pcall(
        body, name="attn_bwd", grid=(KVH, NB),
        out_shape=(jax.ShapeDtypeStruct((AH, L, DH), BF16), jax.ShapeDtypeStruct((KVH, L + B, DH), F32),
                   jax.ShapeDtypeStruct((KVH, L + B, DH), F32), jax.ShapeDtypeStruct((1, DH), F32),
                   jax.ShapeDtypeStruct((1, DH), F32), jax.ShapeDtypeStruct((KVH, 1, G), F32),
                   jax.ShapeDtypeStruct((AH, B, 2 * B), F32)),
        in_specs=[qblk, kblk(0), kblk(1), kblk(0), kblk(1),
                  pl.BlockSpec((1, DH), lambda h, n: (0, 0)), pl.BlockSpec((1, DH), lambda h, n: (0, 0)),
                  pl.BlockSpec((1, 1, G), lambda h, n: (h, 0, 0)),
                  pl.BlockSpec((G, B, 2 * B), lambda h, n: (h, 0, 0)), qblk],
        out_specs=(qblk, accblk, accblk, vecblk, vecblk, pl.BlockSpec((1, 1, G), lambda h, n: (h, 0, 0)),
                   pl.BlockSpec((G, B, 2 * B), lambda h, n: (h, 0, 0))),
        compiler_params=_params(("arbitrary", "arbitrary")),
    )(q_t, kp, kp, vp, vp, qg, kg, sinks, bias, do_t)


def _heads_first(t, nh):
    L = t.shape[0]
    return jnp.transpose(t.reshape(L, nh, t.shape[1] // nh), (1, 0, 2))


def _heads_last(t):
    nh, L, dh = t.shape
    return jnp.transpose(t, (1, 0, 2)).reshape(L, nh * dh)


def kernel(x, c, w_ada, b_ada, norm1_g, norm2_g, w_in, hg_lb_logits, hg_out_norm_g, q_norm_g, k_norm_g, attn_sinks, rel_bias_table, w_branch_hg, w_branch_attn, w_out, w_ff1, w_ff2, loss_target, m_w_ada, m_b_ada, m_norm1_g, m_norm2_g, m_w_in, m_hg_lb_logits, m_hg_out_norm_g, m_q_norm_g, m_k_norm_g, m_attn_sinks, m_rel_bias_table, m_w_branch_hg, m_w_branch_attn, m_w_out, m_w_ff1, m_w_ff2, v_w_ada, v_b_ada, v_norm1_g, v_norm2_g, v_w_in, v_hg_lb_logits, v_hg_out_norm_g, v_q_norm_g, v_k_norm_g, v_attn_sinks, v_rel_bias_table, v_w_branch_hg, v_w_branch_attn, v_w_out, v_w_ff1, v_w_ff2):
    me = 4 * lax.axis_index("x") + 2 * lax.axis_index("y") + lax.axis_index("c")
    x2 = x[0]
    tgt = loss_target[0]
    L, D = x2.shape
    HGW = hg_lb_logits.shape[1]
    H = HGW // HG_DK
    AH = attn_sinks.shape[1]
    DH = q_norm_g.shape[1]
    ATW = AH * DH
    INW = w_in.shape[2] * N_DEV
    KVW = (INW - 4 * HGW - ATW - 2 * D) // 2
    KVH = KVW // DH
    G = AH // KVH
    ADA_N = w_ada.shape[2]

    c_all = _gather_small(c, me, "gather_c")[:, 0, :]
    b_cols = lax.dynamic_slice(b_ada, (0, me * ADA_N), (1, ADA_N))
    (ada_cols,) = _whole(lambda cv, w, b: (_bdot(_silu(cv), w, NN) + b,), [c_all, w_ada[0], b_cols],
                         [((N_DEV, ADA_N), F32)], "ada_fwd")
    ada_all = _gather_small(ada_cols, me, "gather_ada")
    ada_row = lax.dynamic_slice(ada_all, (0, me, 0), (N_DEV, 1, ADA_N)).reshape(1, 6 * D)
    shift1, scale1, gate1, shift2, scale2, gate2 = [ada_row[:, i * D:(i + 1) * D] for i in range(6)]

    wnames = ("in", "bhg", "bat", "out", "ff1", "ff2")
    waxis = dict(zip(wnames, (1, 1, 1, 0, 1, 0)))
    wsrc = dict(zip(wnames, (w_in, w_branch_hg, w_branch_attn, w_out, w_ff1, w_ff2)))
    by_rel = _allgather_weights([wsrc[k][0].astype(BF16) for k in wnames], me, "ag")
    wf = {k: _assemble(by_rel[i], me, waxis[k]) for i, k in enumerate(wnames)}

    (h,) = _rowwise(lambda xv, g, sh, sc: ((_modnorm(xv, g, sh, sc),), ()), [(x2, D, 0)], [norm1_g, shift1, scale1],
                    [(D, BF16)], [], "norm1")
    o4, oa = 4 * HGW, 4 * HGW + ATW + 2 * KVW
    p4 = _mm(h, wf["in"][:, :o4], "nn", F32, "proj_hg")
    pa = _mm(h, wf["in"][:, o4:oa], "nn", F32, "proj_at")
    pg = _mm(h, wf["in"][:, oa:], "nn", F32, "proj_gate")

    o_hg, s_all = _hgrn_fwd(p4, hg_lb_logits, hg_out_norm_g, H)

    bucket = _bucket_ids()
    (bias_flat,) = _whole(lambda tb, bk: (_dot(tb, _onehot(bk), TN, precision=HIGHEST),), [rel_bias_table, bucket],
                          [((AH, AT_BLOCK * 2 * AT_BLOCK), F32)], "bias_fwd")
    bias = bias_flat.reshape(AH, AT_BLOCK, 2 * AT_BLOCK)
    q_t = _heads_first(pa[:, :ATW], AH)
    pad = lambda t: jnp.pad(t, ((0, 0), (AT_BLOCK, 0), (0, 0)))
    kp = pad(_heads_first(pa[:, ATW:ATW + KVW], KVH))
    vp = pad(_heads_first(pa[:, ATW + KVW:], KVH))
    sinks3 = attn_sinks.reshape(KVH, 1, G)
    o_at = _heads_last(_attn_fwd(q_t, kp, vp, q_norm_g, k_norm_g, sinks3, bias, KVH))

    bh = _mm(o_hg, wf["bhg"], "nn", F32, "branch_hg")
    ba = _mm(o_at, wf["bat"], "nn", F32, "branch_at")

    def merge_fn(bhv, bav, ghg, gat):
        return jax.nn.sigmoid(ghg) * bhv + jax.nn.sigmoid(gat) * bav

    (merged,) = _rowwise(lambda *a: ((merge_fn(*a),), ()), [(bh, D, 0), (ba, D, 0), (pg, D, 0), (pg, D, 1)], [],
                         [(D, BF16)], [], "merge")
    mo = _mm(merged, wf["out"], "nn", F32, "out_proj")

    def resid1(xv, mov, g1, g2n, sh, sc):
        x1v = xv + g1 * mov
        return (x1v, _modnorm(x1v, g2n, sh, sc)), ()

    x1, h2 = _rowwise(resid1, [(x2, D, 0), (mo, D, 0)], [gate1, norm2_g, shift2, scale2], [(D, F32), (D, BF16)], [], "resid1")
    u = _mm(h2, wf["ff1"], "nn", F32, "ff1")
    DFF = u.shape[1]
    (act,) = _rowwise(lambda uv: ((jnp.square(jnp.maximum(uv, 0.0)),), ()), [(u, DFF, 0)], [], [(DFF, BF16)], [], "relu2")
    ff = _mm(act, wf["ff2"], "nn", F32, "ff2")

    def loss_fn(x1v, ffv, tv, g2):
        e = x1v + g2 * ffv - tv
        dy = e * (1.0 / D)
        return (dy, dy * g2), (jnp.sum(e * e, axis=0, keepdims=True), jnp.sum(dy * ffv, axis=0, keepdims=True))

    dy, d_ff, sq_sum, d_gate2 = _rowwise(loss_fn, [(x1, D, 0), (ff, D, 0), (tgt, D, 0)], [gate2],
                                         [(D, F32), (D, BF16)], [(1, D), (1, D)], "loss")
    loss = lax.psum(jnp.sum(sq_sum) * (0.5 / D), ("x", "y", "c"))

    gw = {}
    gw["ff2"] = _mm(act, d_ff, "tn", BF16, "dw_ff2")
    d_act = _mm(d_ff, wf["ff2"], "nt", F32, "d_act")
    (d_u,) = _rowwise(lambda dav, uv: ((dav * (2.0 * jnp.maximum(uv, 0.0)),), ()), [(d_act, DFF, 0), (u, DFF, 0)], [],
                      [(DFF, BF16)], [], "relu2_bwd")
    gw["ff1"] = _mm(h2, d_u, "tn", BF16, "dw_ff1")
    d_h2 = _mm(d_u, wf["ff1"], "nt", F32, "d_h2")

    def norm2_bwd(dh2v, x1v, dyv, mov, g2n, sh, sc, g1):
        _, vjp = jax.vjp(_modnorm, x1v, g2n, sh, sc)
        dx, dg, dsh, dsc = vjp(dh2v)
        dx1 = dyv + dx
        return (dx1, dx1 * g1), (dg, dsh, dsc, jnp.sum(dx1 * mov, axis=0, keepdims=True))

    d_x1, d_mo, d_g2n, d_shift2, d_scale2, d_gate1 = _rowwise(
        norm2_bwd, [(d_h2, D, 0), (x1, D, 0), (dy, D, 0), (mo, D, 0)], [norm2_g, shift2, scale2, gate1],
        [(D, F32), (D, BF16)], [(1, D)] * 4, "norm2_bwd")
    gw["out"] = _mm(merged, d_mo, "tn", BF16, "dw_out")
    d_merged = _mm(d_mo, wf["out"], "nt", F32, "d_merged")

    def merge_bwd(dmv, bhv, bav, ghg, gat):
        _, vjp = jax.vjp(merge_fn, bhv, bav, ghg, gat)
        return vjp(dmv), ()

    d_bh, d_ba, d_ghg, d_gat = _rowwise(merge_bwd, [(d_merged, D, 0), (bh, D, 0), (ba, D, 0), (pg, D, 0), (pg, D, 1)], [],
                                        [(D, BF16)] * 4, [], "merge_bwd")
    gw["bhg"] = _mm(o_hg, d_bh, "tn", BF16, "dw_bhg")
    gw["bat"] = _mm(o_at, d_ba, "tn", BF16, "dw_bat")
    d_ohg = _mm(d_bh, wf["bhg"], "nt", F32, "d_ohg")
    d_oat = _mm(d_ba, wf["bat"], "nt", BF16, "d_oat")

    d_hq, d_hf, d_hi, d_hg, d_lb, d_gout_h = _hgrn_bwd(p4, hg_lb_logits, hg_out_norm_g, s_all, d_ohg, H)
    dq_t, dkp, dvp, d_qg, d_kg, d_sk, d_bias = _attn_bwd(q_t, kp, vp, q_norm_g, k_norm_g, sinks3, bias,
                                                         _heads_first(d_oat, AH), KVH)
    d_aq = _heads_last(dq_t)
    d_ak = _heads_last(dkp[:, AT_BLOCK:, :]).astype(BF16)
    d_av = _heads_last(dvp[:, AT_BLOCK:, :]).astype(BF16)
    d_proj = jnp.concatenate([d_hq, d_hf, d_hi, d_hg, d_aq, d_ak, d_av, d_ghg, d_gat], axis=1)
    gw["in"] = _mm(h, d_proj, "tn", BF16, "dw_in")
    d_h = _mm(d_proj, wf["in"], "nt", F32, "d_h")

    def norm1_bwd(dhv, xv, dx1v, g1n, sh, sc):
        _, vjp = jax.vjp(_modnorm, xv, g1n, sh, sc)
        dx, dg, dsh, dsc = vjp(dhv)
        return (dx1v + dx,), (dg, dsh, dsc)

    grad_x, d_g1n, d_shift1, d_scale1 = _rowwise(norm1_bwd, [(d_h, D, 0), (x2, D, 0), (d_x1, D, 0)],
                                                 [norm1_g, shift1, scale1], [(D, F32)], [(1, D)] * 3, "norm1_bwd")

    send1, keep = [], []
    for k in wnames:
        for rel in CHIP_RELS:
            send1.append((_block_of(gw[k], me ^ rel ^ 1, waxis[k]), 1))
            keep.append(_block_of(gw[k], me ^ rel, waxis[k]))
    got1 = _exchange(send1, "rs_d2d")
    part = []
    for i, (kp_, g1_) in enumerate(zip(keep, got1)):
        odt = F32 if i % 4 == 0 else BF16
        (p,) = _rowwise(lambda a, b: ((a.astype(F32) + b.astype(F32),), ()), [(kp_, kp_.shape[1], 0), (g1_, g1_.shape[1], 0)],
                        [], [(kp_.shape[1], odt)], [], "rs_add%d" % i)
        part.append(p)
    got2 = _exchange([(part[4 * i + j], CHIP_RELS[j]) for i in range(len(wnames)) for j in (1, 2, 3)], "rs_ici")

    def update_fn(w, m, v, p0, p1, p2, p3):
        g = ((p0 + p1.astype(F32)) + p2.astype(F32)) + p3.astype(F32)
        delta, mn, vn = _adamw(w, g, m, v)
        return (g, delta, mn, vn), ()

    wmv = dict(zip(wnames, ((w_in, m_w_in, v_w_in), (w_branch_hg, m_w_branch_hg, v_w_branch_hg),
                            (w_branch_attn, m_w_branch_attn, v_w_branch_attn), (w_out, m_w_out, v_w_out),
                            (w_ff1, m_w_ff1, v_w_ff1), (w_ff2, m_w_ff2, v_w_ff2))))
    res = {}
    for i, k in enumerate(wnames):
        w, m, v = (t[0] for t in wmv[k])
        n = w.shape[1]
        res[k] = [t[None] for t in _rowwise(
            update_fn, [(t, n, 0) for t in (w, m, v, part[4 * i], got2[3 * i], got2[3 * i + 1], got2[3 * i + 2])], [],
            [(n, F32)] * 4, [], "update_" + k)]

    d_ada_row = jnp.concatenate([d_shift1, d_scale1, d_gate1, d_shift2, d_scale2, d_gate2], axis=1)
    d_ada_all = _gather_small(d_ada_row, me, "gather_dada")[:, 0, :]
    d_ada_cols = lax.dynamic_slice(d_ada_all, (0, me * ADA_N), (N_DEV, ADA_N))

    def ada_update(cv, dav, w, m, v):
        g = _bdot(_silu(cv), dav, TN)
        delta, mn, vn = _adamw(w, g, m, v)
        return (g, delta, mn, vn), ()

    rt_ada = _tile(D, 256, 16)
    res["ada"] = [t[None] for t in _ada_update_call(ada_update, c_all, d_ada_cols, w_ada[0], m_w_ada[0], v_w_ada[0], rt_ada)]

    d_sinks = d_sk.reshape(1, AH)
    (d_table_t,) = _whole(lambda db, bk: (_dot(db, _onehot(bk), NT, precision=HIGHEST),),
                          [d_bias.reshape(AH, AT_BLOCK * 2 * AT_BLOCK), bucket], [((AH, N_BUCKETS), F32)], "bias_bwd")
    smalls = [d_g1n, d_g2n, d_lb, d_gout_h, d_qg, d_kg, d_sinks, d_table_t.T.reshape(1, N_BUCKETS * AH)]
    widths = [s.shape[1] for s in smalls]
    lanes = [-(-w // 128) * 128 for w in widths]
    smalls = [jnp.pad(s, ((0, 0), (0, p - w))) for s, w, p in zip(smalls, widths, lanes)]
    packed = _gather_small(jnp.concatenate(smalls, axis=1), me, "gather_small")[:, 0, :]
    offs = [sum(lanes[:i]) for i in range(len(lanes))]

    def small_update(pk, dada, lg, *wmv_flat):
        tot = pk[0:1]
        for d in range(1, N_DEV):
            tot = tot + pk[d:d + 1]
        gb = dada[0:1]
        for d in range(1, N_DEV):
            gb = gb + dada[d:d + 1]
        gs = [tot[:, offs[i]:offs[i] + widths[i]] for i in range(len(widths))]
        _, lb_vjp = jax.vjp(lambda t: _softmax0(t), lg)
        (g_lg,) = lb_vjp(gs[2])
        grads = [gb, gs[0], gs[1], g_lg, gs[3], gs[4], gs[5], gs[6], gs[7]]
        outs = []
        for i, g in enumerate(grads):
            w, m, v = wmv_flat[3 * i:3 * i + 3]
            delta, mn, vn = _adamw(w, g, m, v)
            outs += [g, delta, mn, vn]
        return tuple(outs)

    tbl = lambda t: t.reshape(1, N_BUCKETS * AH)
    small_wmv = [(b_ada, m_b_ada, v_b_ada), (norm1_g, m_norm1_g, v_norm1_g), (norm2_g, m_norm2_g, v_norm2_g),
                 (hg_lb_logits, m_hg_lb_logits, v_hg_lb_logits), (hg_out_norm_g, m_hg_out_norm_g, v_hg_out_norm_g),
                 (q_norm_g, m_q_norm_g, v_q_norm_g), (k_norm_g, m_k_norm_g, v_k_norm_g),
                 (attn_sinks, m_attn_sinks, v_attn_sinks),
                 (tbl(rel_bias_table), tbl(m_rel_bias_table), tbl(v_rel_bias_table))]
    flat = [t for trip in small_wmv for t in trip]
    out_shapes = [(trip[0].shape, F32) for trip in small_wmv for _ in range(4)]
    sres = _whole(small_update, [packed, d_ada_all, hg_lb_logits] + flat, out_shapes, "small_update")
    names_small = ("b_ada", "norm1_g", "norm2_g", "lb", "gout", "qg", "kg", "sinks", "table")
    for i, k in enumerate(names_small):
        r = sres[4 * i:4 * i + 4]
        if k == "table":
            r = [t.reshape(N_BUCKETS, AH) for t in r]
        res[k] = r

    order = ("ada", "b_ada", "norm1_g", "norm2_g", "in", "lb", "gout", "qg", "kg", "sinks", "table", "bhg", "bat", "out", "ff1", "ff2")
    outs = [loss, grad_x[None]]
    for j in range(4):
        outs += [res[k][j] for k in order]
    return tuple(outs)


def _softmax0(lg):
    e = jnp.exp(lg - jnp.max(lg, axis=0, keepdims=True))
    return e[0:1] / jnp.sum(e, axis=0, keepdims=True)


def _ada_update_call(fn, c_all, d_cols, w, m, v, rt):
    D, n = w.shape

    def body(c_ref, d_ref, w_ref, m_ref, v_ref, g_out, dl_out, m_out, v_out):
        outs, _ = fn(c_ref[...], d_ref[...], w_ref[...], m_ref[...], v_ref[...])
        for r, val in zip((g_out, dl_out, m_out, v_out), outs):
            r[...] = val

    wblk = pl.BlockSpec((rt, n), lambda i: (i, 0))
    return _pcall(
        body, name="update_ada", grid=(D // rt,), out_shape=tuple([jax.ShapeDtypeStruct((D, n), F32)] * 4),
        in_specs=[pl.BlockSpec((N_DEV, rt), lambda i: (0, i)), pl.BlockSpec((N_DEV, n), lambda i: (0, 0)), wblk, wblk, wblk],
        out_specs=(wblk, wblk, wblk, wblk), compiler_params=_params(("arbitrary",)),
    )(c_all, d_cols, w, m, v)
```

```python
import functools
import math

import jax
import jax.numpy as jnp
from jax import lax
from jax.experimental import pallas as pl
from jax.experimental.pallas import tpu as pltpu

F32 = jnp.float32
BF16 = jnp.bfloat16
EPS = 1e-6
NEG_INF = -1e30
HG_DK = 128
HG_CHUNK = 64
AT_BLOCK = 128
N_BUCKETS = 32
MAX_EXACT = 16
MAX_DISTANCE = 128
N_DEV = 8
LANES = 128
VMEM_LIMIT = 56 * 1024 * 1024
ADAM_LR, ADAM_B1, ADAM_B2, ADAM_EPS, ADAM_WD, ADAM_STEP = 0.001, 0.9, 0.999, 1e-08, 0.01, 10
HIGHEST = lax.Precision.HIGHEST
MESH = pl.DeviceIdType.MESH
ANY = pl.BlockSpec(memory_space=pl.ANY)
CHIP_RELS = (0, 4, 2, 6)

NN = (((1,), (0,)), ((), ()))
NT = (((1,), (1,)), ((), ()))
TN = (((0,), (0,)), ((), ()))


def _tile(n, pref, unit):
    if n <= pref:
        return n
    t = (pref // unit) * unit
    while t >= unit:
        if n % t == 0:
            return t
        t -= unit
    return n


def _dot(a, b, dn, precision=None):
    return lax.dot_general(a, b, dn, preferred_element_type=F32, precision=precision)


def _bdot(a, b, dn):
    return _dot(a.astype(BF16), b.astype(BF16), dn)


def _position():
    x, y, c = lax.axis_index("x"), lax.axis_index("y"), lax.axis_index("c")
    return dict(x=x, y=y, c=c, me=4 * x + 2 * y + c)


def _peer_position(p, rel):
    x = 1 - p["x"] if rel & 4 else p["x"]
    y = 1 - p["y"] if rel & 2 else p["y"]
    c = 1 - p["c"] if rel & 1 else p["c"]
    return dict(x=x, y=y, c=c, me=4 * x + 2 * y + c)


class _Comm:
    def __init__(self):
        self.ins, self.outs, self.alias, self.plans, self.res = [], [], {}, [], None

    def inp(self, arr):
        self.ins.append(arr)
        return ("i", len(self.ins) - 1)

    def out(self, shape, dtype, alias=None):
        self.outs.append(jax.ShapeDtypeStruct(tuple(shape), dtype))
        if alias is not None:
            self.alias[alias[1]] = len(self.outs) - 1
        return ("o", len(self.outs) - 1)

    def copy(self, src, src_view, dst, dst_view, rel):
        self.plans.append((src, src_view, dst, dst_view, rel))

    def result(self, handle):
        return self.res[handle[1]]

    def build(self, in_refs, out_refs, send_sems, recv_sems):
        pos = _position()
        ref = lambda h: in_refs[h[1]] if h[0] == "i" else out_refs[h[1]]
        ops = []
        for k, (src, sv, dst, dv, rel) in enumerate(self.plans):
            s = sv(ref(src), pos)
            if rel == 0:
                cp = pltpu.make_async_copy(s, dv(ref(dst), pos), send_sems.at[k])
                ops.append((cp.start, cp.wait))
                continue
            peer = _peer_position(pos, rel)
            mk = lambda d: pltpu.make_async_remote_copy(
                src_ref=s, dst_ref=d, send_sem=send_sems.at[k], recv_sem=recv_sems.at[k],
                device_id=(peer["x"], peer["y"], peer["c"]), device_id_type=MESH)
            out_cp, in_cp = mk(dv(ref(dst), pos)), mk(dv(ref(dst), peer))

            def wait(out_cp=out_cp, in_cp=in_cp):
                out_cp.wait_send()
                in_cp.wait_recv()

            ops.append((out_cp.start, wait))
        return ops


def _call(body, args, *, name, out_shape, in_specs=None, out_specs=None, grid=None, scratch_shapes=(), comm=None,
          prefetch=None, aliases=None):
    single = not isinstance(out_shape, (tuple, list))
    out_shape = (out_shape,) if single else tuple(out_shape)
    n_in, n_out, n_scr = len(args), len(out_shape), len(scratch_shapes)
    vm = pl.BlockSpec(memory_space=pltpu.VMEM)
    in_specs = [vm] * n_in if in_specs is None else list(in_specs)
    out_specs = [vm] * n_out if out_specs is None else (list(out_specs) if isinstance(out_specs, (tuple, list)) else [out_specs])
    n_pf = 0 if prefetch is None else len(prefetch)
    kw = {} if aliases is None else {"input_output_aliases": dict(aliases)}
    if comm is None:
        fn = body
        all_args, all_scratch = list(args), list(scratch_shapes)
    else:
        n_ci, n_co, n_x = len(comm.ins), len(comm.outs), len(comm.plans)

        def fn(*refs):
            pf, refs = refs[:n_pf], refs[n_pf:]
            o_in, c_in = refs[:n_in], refs[n_in:n_in + n_ci]
            o_out = refs[n_in + n_ci:n_in + n_ci + n_out]
            c_out = refs[n_in + n_ci + n_out:n_in + n_ci + n_out + n_co]
            scr = refs[n_in + n_ci + n_out + n_co:]
            ops = comm.build(c_in, c_out, scr[n_scr], scr[n_scr + 1])
            if grid:
                first = functools.reduce(jnp.logical_and, [pl.program_id(i) == 0 for i in range(len(grid))])
                last = functools.reduce(jnp.logical_and, [pl.program_id(i) == g - 1 for i, g in enumerate(grid)])

                @pl.when(first)
                def _():
                    for start, _w in ops:
                        start()
            else:
                for start, _w in ops:
                    start()
            body(*pf, *o_in, *o_out, *scr[:n_scr])
            if grid:
                @pl.when(last)
                def _():
                    for _s, wait in ops:
                        wait()
            else:
                for _s, wait in ops:
                    wait()

        all_args = list(args) + list(comm.ins)
        in_specs = in_specs + [ANY] * n_ci
        out_shape = out_shape + tuple(comm.outs)
        out_specs = out_specs + [ANY] * n_co
        all_scratch = list(scratch_shapes) + [pltpu.SemaphoreType.DMA((n_x,)), pltpu.SemaphoreType.DMA((n_x,))]
        kw["input_output_aliases"] = {n_pf + n_in + i: n_out + o for i, o in comm.alias.items()}
    sem = None if grid is None else ("arbitrary",) * len(grid)
    params = pltpu.CompilerParams(dimension_semantics=sem, vmem_limit_bytes=VMEM_LIMIT)
    if prefetch is None:
        spec = dict(in_specs=in_specs, out_specs=tuple(out_specs), scratch_shapes=all_scratch)
        if grid is not None:
            spec["grid"] = grid
    else:
        spec = dict(grid_spec=pltpu.PrefetchScalarGridSpec(
            num_scalar_prefetch=n_pf, grid=grid, in_specs=in_specs, out_specs=tuple(out_specs), scratch_shapes=all_scratch))
        all_args = list(prefetch) + all_args
    res = pl.pallas_call(fn, name=name, out_shape=out_shape, compiler_params=params, **spec, **kw)(*all_args)
    res = list(res)
    if comm is not None:
        comm.res = res[n_out:]
        res = res[:n_out]
    return res[0] if single else res


def _whole_view(ref, pos):
    return ref


def _block_view(axis, n, index):
    def view(ref, pos):
        off = pl.multiple_of(index(pos) * n, n)
        return ref.at[:, pl.ds(off, n)] if axis == 1 else ref.at[pl.ds(off, n), :]
    return view


def _slot_view(i, rows=None):
    def view(ref, pos):
        return ref.at[i] if rows is None else ref.at[i, pl.ds(rows[0], rows[1] - rows[0]), :]
    return view


def _exchange(items, name):
    cm = _Comm()
    for a, rel in items:
        cm.copy(cm.inp(a), _whole_view, cm.out(a.shape, a.dtype), _whole_view, rel)
    _call(lambda: None, [], name=name, out_shape=(), comm=cm)
    return cm.res


def _gather_small(v, me, name):
    got = _exchange([(v, rel) for rel in range(1, N_DEV)], name)
    out = jnp.zeros((N_DEV,) + v.shape, v.dtype)
    out = lax.dynamic_update_slice(out, v[None], (me, 0, 0))
    for rel in range(1, N_DEV):
        out = lax.dynamic_update_slice(out, got[rel - 1][None], (me ^ rel, 0, 0))
    return out


def _ag_ici(cm, blk, axis):
    n = blk.shape[axis]
    shape = list(blk.shape)
    shape[axis] = n * N_DEV
    hi, ho = cm.inp(blk), cm.out(shape, blk.dtype)
    own = _block_view(axis, n, lambda p: p["me"])
    for rel in CHIP_RELS:
        cm.copy(hi, _whole_view, ho, own, rel)
    return ho


def _ag_d2d(cm, full, axis):
    n = full.shape[axis] // N_DEV
    hi = cm.inp(full)
    ho = cm.out(full.shape, full.dtype, alias=hi)
    for r in CHIP_RELS:
        v = _block_view(axis, n, functools.partial(lambda p, r: p["me"] ^ r, r=r))
        cm.copy(hi, v, ho, v, 1)
    return ho


def _rs_d2d(cm, gw, axis):
    n = gw.shape[axis] // N_DEV
    shape = list(gw.shape)
    shape[axis] = n
    hi, ho = cm.inp(gw), cm.out([4] + shape, gw.dtype)
    for i, r in enumerate(CHIP_RELS):
        cm.copy(hi, _block_view(axis, n, functools.partial(lambda p, r: p["me"] ^ r ^ 1, r=r)), ho, _slot_view(i), 1)
    return ho


def _rs_ici(cm, part, rows=None, recv=None):
    if recv is None:
        ho = cm.out((3,) + part.shape[1:], part.dtype)
    else:
        ho = cm.out(recv.shape, recv.dtype, alias=cm.inp(recv))
    hi = cm.inp(part)
    for i in (1, 2, 3):
        cm.copy(hi, _slot_view(i, rows), ho, _slot_view(i - 1, rows), CHIP_RELS[i])
    return ho


def _rs_add(gw, recv, axis, base, name, tw=None):
    _, R, n = recv.shape
    if axis == 1:
        tw = n if tw is None else tw
        gw_spec = pl.BlockSpec((R, tw), lambda i, t, b: (0, b[i] + t))
        rv_spec = pl.BlockSpec((None, R, tw), lambda i, t, b: (i, 0, t))
        grid = (4, n // tw)
    else:
        tw = _tile(n, 1024, LANES)
        gw_spec = pl.BlockSpec((R, tw), lambda i, t, b: (b[i], t))
        rv_spec = pl.BlockSpec((None, R, tw), lambda i, t, b: (i, 0, t))
        grid = (4, n // tw)

    def body(b_ref, g_ref, r_ref, o_ref):
        o_ref[...] = (g_ref[...].astype(F32) + r_ref[...].astype(F32)).astype(o_ref.dtype)

    return _call(body, [gw, recv], name=name, out_shape=jax.ShapeDtypeStruct(recv.shape, recv.dtype), grid=grid,
                 in_specs=[gw_spec, rv_spec], out_specs=rv_spec, prefetch=[base])


def _ag_w_in(src, a, D, INW):
    wm = LANES * a

    def main_place(ref, p):
        off = pl.multiple_of(((2 * a + 1) * (p["me"] // 2) + (a + 1) * p["c"]) * LANES, LANES)
        return ref.at[:, pl.ds(off, wm)]

    def main_src(ref, p):
        return ref.at[:, pl.ds(pl.multiple_of(p["c"] * LANES, LANES), wm)]

    def mid_src(ref, p):
        return ref.at[:, pl.ds(pl.multiple_of((1 - p["c"]) * wm, LANES), LANES)]

    def mid_place(ref, p):
        return ref.at[p["me"]]

    def body(src_ref, full_ref, mid_ref, send_sems, recv_sems):
        pos = _position()
        sib = _peer_position(pos, 1)

        def remote(k, s, d, to):
            return pltpu.make_async_remote_copy(src_ref=s, dst_ref=d, send_sem=send_sems.at[k], recv_sem=recv_sems.at[k],
                                                device_id=(to["x"], to["y"], to["c"]), device_id_type=MESH)

        local = [pltpu.make_async_copy(main_src(src_ref, pos), main_place(full_ref, pos), send_sems.at[16]),
                 pltpu.make_async_copy(mid_src(src_ref, pos), mid_place(mid_ref, pos), send_sems.at[17])]
        for cp in local:
            cp.start()
        sends = []
        for i, rel in enumerate(CHIP_RELS):
            to = sib if rel == 0 else _peer_position(pos, rel)
            sends.append(remote(2 * i, main_src(src_ref, pos), main_place(full_ref, pos), to))
            sends.append(remote(2 * i + 1, mid_src(src_ref, pos), mid_place(mid_ref, pos), to))
        for cp in sends:
            cp.start()
        for i, rel in enumerate(CHIP_RELS[1:], start=1):
            frm = _peer_position(pos, rel)
            remote(2 * i, main_src(src_ref, pos), main_place(full_ref, frm), frm).wait_recv()
            fwd = remote(8 + 2 * i, main_place(full_ref, frm), main_place(full_ref, frm), sib)
            fwd.start()
            sends.append(fwd)
            remote(2 * i + 1, mid_src(src_ref, pos), mid_place(mid_ref, frm), frm).wait_recv()
            fwd = remote(9 + 2 * i, mid_place(mid_ref, frm), mid_place(mid_ref, frm), sib)
            fwd.start()
            sends.append(fwd)
        remote(0, main_src(src_ref, pos), main_place(full_ref, sib), sib).wait_recv()
        remote(1, mid_src(src_ref, pos), mid_place(mid_ref, sib), sib).wait_recv()
        for i, rel in enumerate(CHIP_RELS[1:], start=1):
            frm = _peer_position(sib, rel)
            remote(8 + 2 * i, main_src(src_ref, pos), main_place(full_ref, frm), sib).wait_recv()
            remote(9 + 2 * i, mid_src(src_ref, pos), mid_place(mid_ref, frm), sib).wait_recv()
        for cp in sends:
            cp.wait_send()
        for cp in local:
            cp.wait()

    return _call(body, [src], name="ag_w_in", in_specs=[ANY], out_specs=[ANY, ANY],
                 out_shape=(jax.ShapeDtypeStruct((D, INW), BF16), jax.ShapeDtypeStruct((N_DEV, D, LANES), BF16)),
                 scratch_shapes=[pltpu.SemaphoreType.DMA((18,)), pltpu.SemaphoreType.DMA((18,))])


def _patch_mid(full, mid, a):
    D = full.shape[0]

    def body(full_ref, e_ref, o_ref, out_ref):
        out_ref[...] = e_ref[...] + o_ref[...]

    return _call(body, [full, mid, mid], name="patch_mid", grid=(N_DEV // 2,),
                 out_shape=jax.ShapeDtypeStruct(full.shape, full.dtype),
                 in_specs=[ANY, pl.BlockSpec((None, D, LANES), lambda j: (2 * j, 0, 0)),
                           pl.BlockSpec((None, D, LANES), lambda j: (2 * j + 1, 0, 0))],
                 out_specs=pl.BlockSpec((D, LANES), lambda j: (0, (2 * a + 1) * j + a)), aliases={0: 0})


def _mm(a, b, mode, out_dtype, name, tm=1024, tn=1024, tk=512, b_off=0, n=None, comm=None):
    if mode == "nn":
        (M, K), (K2, N) = a.shape, b.shape
    elif mode == "nt":
        (M, K), (N, K2) = a.shape, b.shape
    else:
        (K, M), (K2, N) = a.shape, b.shape
    assert K == K2, (a.shape, b.shape, mode)
    if n is not None:
        N = n
    tm, tk = _tile(M, tm, LANES), _tile(K, tk, LANES)
    tn = _tile(math.gcd(N, b_off) if b_off else N, tn, LANES)
    nk, jb = K // tk, b_off // tn
    dn = {"nn": NN, "nt": NT, "tn": TN}[mode]

    def body(a_ref, b_ref, o_ref, acc_ref):
        k = pl.program_id(2)

        @pl.when(k == 0)
        def _():
            acc_ref[...] = jnp.zeros_like(acc_ref)

        acc_ref[...] += _bdot(a_ref[...], b_ref[...], dn)

        @pl.when(k == nk - 1)
        def _():
            o_ref[...] = acc_ref[...].astype(out_dtype)

    a_spec = pl.BlockSpec((tk, tm), lambda i, j, k: (k, i)) if mode == "tn" else pl.BlockSpec((tm, tk), lambda i, j, k: (i, k))
    b_spec = pl.BlockSpec((tn, tk), lambda i, j, k: (j, k)) if mode == "nt" else pl.BlockSpec((tk, tn), lambda i, j, k: (k, j + jb))
    return _call(body, [a, b], name=name, grid=(M // tm, N // tn, nk), out_shape=jax.ShapeDtypeStruct((M, N), out_dtype),
                 in_specs=[a_spec, b_spec], out_specs=pl.BlockSpec((tm, tn), lambda i, j, k: (i, j)),
                 scratch_shapes=[pltpu.VMEM((tm, tn), F32)], comm=comm)


def _rowwise(fn, row_ins, bcast_ins, row_outs, acc_outs, name, rt=256, comm=None):
    L = row_ins[0][0].shape[-2]
    rt = _tile(L, rt, 16)
    nr, nb, no = len(row_ins), len(bcast_ins), len(row_outs)

    def body(*refs):
        i = pl.program_id(0)
        vals = [r[...] for r in refs[:nr + nb]]
        outs, accs = fn(*vals)
        for r, v in zip(refs[nr + nb:nr + nb + no], outs):
            r[...] = v.astype(r.dtype)
        acc_refs = refs[nr + nb + no:]

        @pl.when(i == 0)
        def _():
            for r in acc_refs:
                r[...] = jnp.zeros_like(r)

        for r, v in zip(acc_refs, accs):
            r[...] += v

    in_specs = []
    for spec in row_ins:
        w, cb = spec[1], spec[2]
        if len(spec) == 4:
            in_specs.append(pl.BlockSpec((None, rt, w), functools.partial(lambda i, cb, ld: (ld, i, cb), cb=cb, ld=spec[3])))
        else:
            in_specs.append(pl.BlockSpec((rt, w), functools.partial(lambda i, cb: (i, cb), cb=cb)))
    in_specs += [pl.BlockSpec(b.shape, lambda i: (0, 0)) for b in bcast_ins]
    out_specs = [pl.BlockSpec((rt, w), lambda i: (i, 0)) for w, _ in row_outs]
    out_specs += [pl.BlockSpec(s, lambda i: (0, 0)) for s in acc_outs]
    out_shape = [jax.ShapeDtypeStruct((L, w), dt) for w, dt in row_outs] + [jax.ShapeDtypeStruct(s, F32) for s in acc_outs]
    return _call(body, [s[0] for s in row_ins] + list(bcast_ins), name=name, grid=(L // rt,), out_shape=tuple(out_shape),
                 in_specs=in_specs, out_specs=out_specs, comm=comm)


def _whole(fn, ins, out_shapes, name):
    def body(*refs):
        outs = fn(*[r[...] for r in refs[:len(ins)]])
        for r, v in zip(refs[len(ins):], outs):
            r[...] = v.astype(r.dtype)

    return _call(body, list(ins), name=name, out_shape=tuple(jax.ShapeDtypeStruct(s, dt) for s, dt in out_shapes))


def _silu(x):
    return x * jax.nn.sigmoid(x)


def _rms(x, g):
    return (x * lax.rsqrt(jnp.mean(x * x, axis=-1, keepdims=True) + EPS)) * g


def _modnorm(x, g, shift, scale):
    return _rms(x, g) * (1.0 + scale) + shift


def _adamw(w, g, m, v):
    m = ADAM_B1 * m + (1.0 - ADAM_B1) * g
    v = ADAM_B2 * v + (1.0 - ADAM_B2) * jnp.square(g)
    m_hat = m / (1.0 - ADAM_B1 ** ADAM_STEP)
    v_hat = v / (1.0 - ADAM_B2 ** ADAM_STEP)
    delta = -ADAM_LR * (m_hat / (jnp.sqrt(v_hat) + ADAM_EPS) + ADAM_WD * w)
    return delta, m, v


def _lower_bound(lg):
    e = jnp.exp(lg - jnp.max(lg, axis=0, keepdims=True))
    return e[0:1] / jnp.sum(e, axis=0, keepdims=True)


def _hg_chunk(hq, hf, hi, lb, st):
    C = hq.shape[0]
    row = lax.broadcasted_iota(jnp.int32, (C, C), 0)
    col = lax.broadcasted_iota(jnp.int32, (C, C), 1)
    tri = row >= col
    sg = jax.nn.sigmoid(hf)
    f = lb + (1.0 - lb) * sg
    lf = jnp.log(f)
    k = 1.0 - f
    q = _silu(hq)
    b = _dot(tri.astype(F32), lf, NN, precision=HIGHEST)
    m = b[C // 2 - 1:C // 2]
    bl = b[C - 1:C]
    e_qm, e_km, e_kl, e_q = jnp.exp(b - m), jnp.exp(m - b), jnp.exp(bl - b), jnp.exp(b)
    qe, ke, kd, qb = q * e_qm, k * e_km, k * e_kl, q * e_q
    sc = jnp.where(tri, _bdot(qe, ke, NT), 0.0)
    o = _bdot(sc, hi, NN) + _bdot(qb, st, NT)
    dec = jnp.exp(bl)
    st_next = st * dec + _bdot(hi, kd, TN)
    return o, st_next, dict(tri=tri, sg=sg, f=f, k=k, q=q, qe=qe, ke=ke, kd=kd, qb=qb, sc=sc, dec=dec,
                            e_qm=e_qm, e_km=e_km, e_kl=e_kl, e_q=e_q)


def _hg_out(o, hgate, gout):
    return _rms(o, gout) * _silu(hgate)


def _hgrn_fwd(p4, lb_logits, gout, H, comm=None):
    L = p4.shape[0]
    N = L // HG_CHUNK

    def body(hq_ref, hf_ref, hi_ref, hg_ref, lg_ref, gout_ref, o_ref, s_ref, st_ref):
        @pl.when(pl.program_id(1) == 0)
        def _():
            st_ref[...] = jnp.zeros_like(st_ref)

        st = st_ref[...]
        s_ref[0, 0] = st
        o, st_next, _ = _hg_chunk(hq_ref[...], hf_ref[...], hi_ref[...], _lower_bound(lg_ref[...]), st)
        st_ref[...] = st_next
        o_ref[...] = _hg_out(o, hg_ref[...], gout_ref[...]).astype(o_ref.dtype)

    blk = lambda s: pl.BlockSpec((HG_CHUNK, HG_DK), functools.partial(lambda h, n, s: (n, s * H + h), s=s))
    return _call(
        body, [p4, p4, p4, p4, lb_logits, gout], name="hgrn_fwd", grid=(H, N),
        out_shape=(jax.ShapeDtypeStruct((L, H * HG_DK), BF16), jax.ShapeDtypeStruct((H, N, HG_DK, HG_DK), F32)),
        in_specs=[blk(0), blk(1), blk(2), blk(3), pl.BlockSpec((2, HG_DK), lambda h, n: (0, h)),
                  pl.BlockSpec((1, HG_DK), lambda h, n: (0, 0))],
        out_specs=(pl.BlockSpec((HG_CHUNK, HG_DK), lambda h, n: (n, h)),
                   pl.BlockSpec((1, 1, HG_DK, HG_DK), lambda h, n: (h, n, 0, 0))),
        scratch_shapes=[pltpu.VMEM((HG_DK, HG_DK), F32)], comm=comm)


def _hgrn_bwd(p4, lb_logits, gout, s_all, d_out, H, comm=None):
    L = p4.shape[0]
    N = L // HG_CHUNK
    C = HG_CHUNK

    def body(hq_ref, hf_ref, hi_ref, hg_ref, lg_ref, gout_ref, s_ref, do_ref,
             dq_ref, df_ref, di_ref, dg_ref, dlb_ref, dgo_ref, dst_ref):
        @pl.when(pl.program_id(1) == 0)
        def _():
            dst_ref[...] = jnp.zeros_like(dst_ref)
            dlb_ref[...] = jnp.zeros_like(dlb_ref)

        @pl.when(jnp.logical_and(pl.program_id(0) == 0, pl.program_id(1) == 0))
        def _():
            dgo_ref[...] = jnp.zeros_like(dgo_ref)

        hq, hf, hi, hgate = hq_ref[...], hf_ref[...], hi_ref[...], hg_ref[...]
        lb = _lower_bound(lg_ref[...])
        st = s_ref[0, 0]
        o, _, t = _hg_chunk(hq, hf, hi, lb, st)
        _, out_vjp = jax.vjp(_hg_out, o, hgate, gout_ref[...])
        do, d_hgate, d_gout = out_vjp(do_ref[...])
        dst_next = dst_ref[...]
        tri = t["tri"]
        dsc = jnp.where(tri, _bdot(do, hi, NT), 0.0)
        dv = _bdot(t["sc"], do, TN) + _bdot(t["kd"], dst_next, NT)
        dqe = _bdot(dsc, t["ke"], NN)
        dke = _bdot(dsc, t["qe"], TN)
        dqb = _bdot(do, st, NN)
        dkd = _bdot(hi, dst_next, NN)
        ddec = jnp.sum(dst_next * st, axis=0, keepdims=True)
        dst_ref[...] = _bdot(do, t["qb"], TN) + dst_next * t["dec"]
        dq = dqe * t["e_qm"] + dqb * t["e_q"]
        dk = dke * t["e_km"] + dkd * t["e_kl"]
        tq, tk, td, tb = dqe * t["qe"], dke * t["ke"], dkd * t["kd"], dqb * t["qb"]
        db = tq - tk - td + tb
        dm = jnp.sum(tk - tq, axis=0, keepdims=True)
        dbl = jnp.sum(td, axis=0, keepdims=True) + ddec * t["dec"]
        rowi = lax.broadcasted_iota(jnp.int32, (C, HG_DK), 0)
        db = db + jnp.where(rowi == C // 2 - 1, dm, 0.0) + jnp.where(rowi == C - 1, dbl, 0.0)
        dlf = _dot(tri.astype(F32), db, TN, precision=HIGHEST)
        df = dlf / t["f"] - dk
        sg = t["sg"]
        df_ref[...] = (df * (1.0 - lb) * sg * (1.0 - sg)).astype(df_ref.dtype)
        dlb_ref[...] += jnp.sum(df * (1.0 - sg), axis=0, keepdims=True)
        sq = jax.nn.sigmoid(hq)
        dq_ref[...] = (dq * (sq * (1.0 + hq * (1.0 - sq)))).astype(dq_ref.dtype)
        di_ref[...] = dv.astype(di_ref.dtype)
        dg_ref[...] = d_hgate.astype(dg_ref.dtype)
        dgo_ref[...] += d_gout

    blk = lambda s: pl.BlockSpec((C, HG_DK), functools.partial(lambda h, n, s: (N - 1 - n, s * H + h), s=s))
    oblk = pl.BlockSpec((C, HG_DK), lambda h, n: (N - 1 - n, h))
    vec = pl.BlockSpec((1, HG_DK), lambda h, n: (0, h))
    W = H * HG_DK
    return _call(
        body, [p4, p4, p4, p4, lb_logits, gout, s_all, d_out], name="hgrn_bwd", grid=(H, N),
        out_shape=tuple([jax.ShapeDtypeStruct((L, W), BF16)] * 4 + [jax.ShapeDtypeStruct((1, W), F32), jax.ShapeDtypeStruct((1, HG_DK), F32)]),
        in_specs=[blk(0), blk(1), blk(2), blk(3), pl.BlockSpec((2, HG_DK), lambda h, n: (0, h)),
                  pl.BlockSpec((1, HG_DK), lambda h, n: (0, 0)),
                  pl.BlockSpec((1, 1, HG_DK, HG_DK), lambda h, n: (h, N - 1 - n, 0, 0)), oblk],
        out_specs=(oblk, oblk, oblk, oblk, vec, pl.BlockSpec((1, HG_DK), lambda h, n: (0, 0))),
        scratch_shapes=[pltpu.VMEM((HG_DK, HG_DK), F32)], comm=comm)


def _bucket_ids():
    i = jnp.arange(AT_BLOCK, dtype=jnp.int32)[:, None]
    j = jnp.arange(2 * AT_BLOCK, dtype=jnp.int32)[None, :]
    n = jnp.maximum(i - j + AT_BLOCK, 0)
    nf = jnp.maximum(n, 1).astype(F32)
    large = MAX_EXACT + (jnp.log(nf / MAX_EXACT) / math.log(MAX_DISTANCE / MAX_EXACT) * (N_BUCKETS - MAX_EXACT)).astype(jnp.int32)
    large = jnp.minimum(large, N_BUCKETS - 1)
    return jnp.where(n < MAX_EXACT, n, large).reshape(1, -1)


def _onehot(bucket):
    ids = lax.broadcasted_iota(jnp.int32, (N_BUCKETS, bucket.shape[1]), 0)
    return (ids == bucket).astype(F32)


def _attn_probs(qn, kpn, kcn, bias_g, sink, first, scale):
    i = lax.broadcasted_iota(jnp.int32, (AT_BLOCK, AT_BLOCK), 0)
    j = lax.broadcasted_iota(jnp.int32, (AT_BLOCK, AT_BLOCK), 1)
    lp = _bdot(qn, kpn, NT) * scale + bias_g[:, :AT_BLOCK]
    lc = _bdot(qn, kcn, NT) * scale + bias_g[:, AT_BLOCK:]
    lp = jnp.where(jnp.logical_and(j > i, jnp.logical_not(first)), lp, NEG_INF)
    lc = jnp.where(j <= i, lc, NEG_INF)
    m = jnp.maximum(jnp.maximum(jnp.max(lp, axis=-1, keepdims=True), jnp.max(lc, axis=-1, keepdims=True)), sink)
    pp, pc, ps = jnp.exp(lp - m), jnp.exp(lc - m), jnp.exp(sink - m)
    den = jnp.sum(pp, axis=-1, keepdims=True) + jnp.sum(pc, axis=-1, keepdims=True) + ps
    return pp / den, pc / den, ps / den


def _attn_fwd(q_t, kp, vp, qg, kg, sinks, bias, KVH, comm=None):
    AH, L, DH = q_t.shape
    G = AH // KVH
    NB = L // AT_BLOCK
    scale = DH ** -0.5

    def body(q_ref, kp_ref, kc_ref, vp_ref, vc_ref, qg_ref, kg_ref, sk_ref, b_ref, o_ref):
        first = pl.program_id(1) == 0
        kpn, kcn = _rms(kp_ref[0], kg_ref[...]), _rms(kc_ref[0], kg_ref[...])
        for g in range(G):
            qn = _rms(q_ref[g], qg_ref[...])
            pp, pc, _ = _attn_probs(qn, kpn, kcn, b_ref[g], sk_ref[0, :, g:g + 1], first, scale)
            o_ref[g] = (_bdot(pp, vp_ref[0], NN) + _bdot(pc, vc_ref[0], NN)).astype(o_ref.dtype)

    kblk = lambda off: pl.BlockSpec((1, AT_BLOCK, DH), functools.partial(lambda h, n, off: (h, n + off, 0), off=off))
    return _call(
        body, [q_t, kp, kp, vp, vp, qg, kg, sinks, bias], name="attn_fwd", grid=(KVH, NB),
        out_shape=jax.ShapeDtypeStruct((AH, L, DH), BF16),
        in_specs=[pl.BlockSpec((G, AT_BLOCK, DH), lambda h, n: (h, n, 0)), kblk(0), kblk(1), kblk(0), kblk(1),
                  pl.BlockSpec((1, DH), lambda h, n: (0, 0)), pl.BlockSpec((1, DH), lambda h, n: (0, 0)),
                  pl.BlockSpec((1, 1, G), lambda h, n: (h, 0, 0)),
                  pl.BlockSpec((G, AT_BLOCK, 2 * AT_BLOCK), lambda h, n: (h, 0, 0))],
        out_specs=pl.BlockSpec((G, AT_BLOCK, DH), lambda h, n: (h, n, 0)), comm=comm)


def _attn_bwd(q_t, kp, vp, qg, kg, sinks, bias, do_t, KVH, comm=None):
    AH, L, DH = q_t.shape
    G = AH // KVH
    NB = L // AT_BLOCK
    B = AT_BLOCK
    scale = DH ** -0.5

    def body(q_ref, kp_ref, kc_ref, vp_ref, vc_ref, qg_ref, kg_ref, sk_ref, b_ref, do_ref,
             dq_ref, dk_ref, dv_ref, dqg_ref, dkg_ref, dsk_ref, db_ref):
        n = pl.program_id(1)
        first = n == 0

        @pl.when(first)
        def _():
            for r in (dk_ref, dv_ref, dsk_ref, db_ref):
                r[...] = jnp.zeros_like(r)

        @pl.when(jnp.logical_and(first, pl.program_id(0) == 0))
        def _():
            dqg_ref[...] = jnp.zeros_like(dqg_ref)
            dkg_ref[...] = jnp.zeros_like(dkg_ref)

        kp_raw, kc_raw, kgv, qgv = kp_ref[0], kc_ref[0], kg_ref[...], qg_ref[...]
        kpn, kp_vjp = jax.vjp(_rms, kp_raw, kgv)
        kcn, kc_vjp = jax.vjp(_rms, kc_raw, kgv)
        dkpn = jnp.zeros((B, DH), F32)
        dkcn = jnp.zeros((B, DH), F32)
        dvp = jnp.zeros((B, DH), F32)
        dvc = jnp.zeros((B, DH), F32)
        dqg = jnp.zeros((1, DH), F32)
        dsk = []
        for g in range(G):
            qn, q_vjp = jax.vjp(_rms, q_ref[g], qgv)
            sink = sk_ref[0, :, g:g + 1]
            pp, pc, ps = _attn_probs(qn, kpn, kcn, b_ref[g], sink, first, scale)
            do = do_ref[g]
            dvp += _bdot(pp, do, TN)
            dvc += _bdot(pc, do, TN)
            dpp = _bdot(do, vp_ref[0], NT)
            dpc = _bdot(do, vc_ref[0], NT)
            dsum = jnp.sum(dpp * pp, axis=-1, keepdims=True) + jnp.sum(dpc * pc, axis=-1, keepdims=True)
            dlp = pp * (dpp - dsum)
            dlc = pc * (dpc - dsum)
            dsk.append(jnp.sum(-ps * dsum, axis=0, keepdims=True))
            db_ref[g, :, :B] += dlp
            db_ref[g, :, B:] += dlc
            dlp, dlc = dlp * scale, dlc * scale
            dqn = _bdot(dlp, kpn, NN) + _bdot(dlc, kcn, NN)
            dkpn += _bdot(dlp, qn, TN)
            dkcn += _bdot(dlc, qn, TN)
            dq_raw, dqg_g = q_vjp(dqn)
            dq_ref[g] = dq_raw.astype(dq_ref.dtype)
            dqg += dqg_g
        dkp_raw, dkg_p = kp_vjp(dkpn)
        dkc_raw, dkg_c = kc_vjp(dkcn)
        r0 = pl.multiple_of(n * B, B)
        r1 = pl.multiple_of(n * B + B, B)
        dk_ref[0, pl.ds(r0, B), :] += dkp_raw
        dk_ref[0, pl.ds(r1, B), :] += dkc_raw
        dv_ref[0, pl.ds(r0, B), :] += dvp
        dv_ref[0, pl.ds(r1, B), :] += dvc
        dqg_ref[...] += dqg
        dkg_ref[...] += dkg_p + dkg_c
        lane = lax.broadcasted_iota(jnp.int32, (1, G), 1)
        dsk_row = jnp.zeros((1, G), F32)
        for g in range(G):
            dsk_row = jnp.where(lane == g, dsk[g], dsk_row)
        dsk_ref[0] += dsk_row

    kblk = lambda off: pl.BlockSpec((1, B, DH), functools.partial(lambda h, n, off: (h, n + off, 0), off=off))
    qblk = pl.BlockSpec((G, B, DH), lambda h, n: (h, n, 0))
    accblk = pl.BlockSpec((1, L + B, DH), lambda h, n: (h, 0, 0))
    vecblk = pl.BlockSpec((1, DH), lambda h, n: (0, 0))
    return _call(
        body, [q_t, kp, kp, vp, vp, qg, kg, sinks, bias, do_t], name="attn_bwd", grid=(KVH, NB),
        out_shape=(jax.ShapeDtypeStruct((AH, L, DH), BF16), jax.ShapeDtypeStruct((KVH, L + B, DH), F32),
                   jax.ShapeDtypeStruct((KVH, L + B, DH), F32), jax.ShapeDtypeStruct((1, DH), F32),
                   jax.ShapeDtypeStruct((1, DH), F32), jax.ShapeDtypeStruct((KVH, 1, G), F32),
                   jax.ShapeDtypeStruct((AH, B, 2 * B), F32)),
        in_specs=[qblk, kblk(0), kblk(1), kblk(0), kblk(1),
                  pl.BlockSpec((1, DH), lambda h, n: (0, 0)), pl.BlockSpec((1, DH), lambda h, n: (0, 0)),
                  pl.BlockSpec((1, 1, G), lambda h, n: (h, 0, 0)),
                  pl.BlockSpec((G, B, 2 * B), lambda h, n: (h, 0, 0)), qblk],
        out_specs=(qblk, accblk, accblk, vecblk, vecblk, pl.BlockSpec((1, 1, G), lambda h, n: (h, 0, 0)),
                   pl.BlockSpec((G, B, 2 * B), lambda h, n: (h, 0, 0))), comm=comm)


def _heads_first(t, nh):
    L = t.shape[0]
    return jnp.transpose(t.reshape(L, nh, t.shape[1] // nh), (1, 0, 2))


def _heads_last(t):
    nh, L, dh = t.shape
    return jnp.transpose(t, (1, 0, 2)).reshape(L, nh * dh)


def _softmax0(lg):
    e = jnp.exp(lg - jnp.max(lg, axis=0, keepdims=True))
    return e[0:1] / jnp.sum(e, axis=0, keepdims=True)


def _ada_update_call(fn, c_all, d_cols, w, m, v, rt):
    D, n = w.shape

    def body(c_ref, d_ref, w_ref, m_ref, v_ref, g_out, dl_out, m_out, v_out):
        outs, _ = fn(c_ref[...], d_ref[...], w_ref[...], m_ref[...], v_ref[...])
        for r, val in zip((g_out, dl_out, m_out, v_out), outs):
            r[...] = val

    wblk = pl.BlockSpec((rt, n), lambda i: (i, 0))
    return _call(
        body, [c_all, d_cols, w, m, v], name="update_ada", grid=(D // rt,), out_shape=tuple([jax.ShapeDtypeStruct((D, n), F32)] * 4),
        in_specs=[pl.BlockSpec((N_DEV, rt), lambda i: (0, i)), pl.BlockSpec((N_DEV, n), lambda i: (0, 0)), wblk, wblk, wblk],
        out_specs=(wblk, wblk, wblk, wblk))


def kernel(x, c, w_ada, b_ada, norm1_g, norm2_g, w_in, hg_lb_logits, hg_out_norm_g, q_norm_g, k_norm_g, attn_sinks, rel_bias_table, w_branch_hg, w_branch_attn, w_out, w_ff1, w_ff2, loss_target, m_w_ada, m_b_ada, m_norm1_g, m_norm2_g, m_w_in, m_hg_lb_logits, m_hg_out_norm_g, m_q_norm_g, m_k_norm_g, m_attn_sinks, m_rel_bias_table, m_w_branch_hg, m_w_branch_attn, m_w_out, m_w_ff1, m_w_ff2, v_w_ada, v_b_ada, v_norm1_g, v_norm2_g, v_w_in, v_hg_lb_logits, v_hg_out_norm_g, v_q_norm_g, v_k_norm_g, v_attn_sinks, v_rel_bias_table, v_w_branch_hg, v_w_branch_attn, v_w_out, v_w_ff1, v_w_ff2):
    cc = lax.axis_index("c")
    me = 4 * lax.axis_index("x") + 2 * lax.axis_index("y") + cc
    x2 = x[0]
    tgt = loss_target[0]
    L, D = x2.shape
    HGW = hg_lb_logits.shape[1]
    H = HGW // HG_DK
    AH = attn_sinks.shape[1]
    DH = q_norm_g.shape[1]
    ATW = AH * DH
    BW = w_in.shape[2]
    INW = BW * N_DEV
    A = BW // LANES
    assert BW == LANES * A + LANES // 2
    KVW = (INW - 4 * HGW - ATW - 2 * D) // 2
    KVH = KVW // DH
    G = AH // KVH
    ADA_N = w_ada.shape[2]
    PAIR = 2 * A + 1

    src_in = lax.dynamic_update_slice(jnp.zeros((D, LANES * (A + 1)), BF16), w_in[0].astype(BF16), (0, (LANES // 2) * cc))
    w_in_gapped, w_in_mid = _ag_w_in(src_in, A, D, INW)
    w_in_full = _patch_mid(w_in_gapped, w_in_mid, A)

    c_all = _gather_small(c, me, "gather_c")[:, 0, :]
    b_cols = lax.dynamic_slice(b_ada, (0, me * ADA_N), (1, ADA_N))
    (ada_cols,) = _whole(lambda cv, w, b: (_bdot(_silu(cv), w, NN) + b,), [c_all, w_ada[0], b_cols],
                         [((N_DEV, ADA_N), F32)], "ada_fwd")
    ada_all = _gather_small(ada_cols, me, "gather_ada")
    ada_row = lax.dynamic_slice(ada_all, (0, me, 0), (N_DEV, 1, ADA_N)).reshape(1, 6 * D)
    shift1, scale1, gate1, shift2, scale2, gate2 = [ada_row[:, i * D:(i + 1) * D] for i in range(6)]

    wnames = ("bhg", "bat", "out", "ff1", "ff2")
    waxis = dict(zip(wnames, (1, 1, 0, 1, 0)))
    wsrc = dict(zip(wnames, (w_branch_hg, w_branch_attn, w_out, w_ff1, w_ff2)))
    wblk = {k: wsrc[k][0].astype(BF16) for k in wnames}
    wf = {}

    (h,) = _rowwise(lambda xv, g, sh, sc: ((_modnorm(xv, g, sh, sc),), ()), [(x2, D, 0)], [norm1_g, shift1, scale1],
                    [(D, BF16)], [], "norm1")
    o4, oa = 4 * HGW, 4 * HGW + ATW + 2 * KVW
    cm = _Comm()
    hs = {k: _ag_ici(cm, wblk[k], waxis[k]) for k in ("bhg", "bat")}
    p4 = _mm(h, w_in_full, "nn", F32, "proj_hg", n=o4, comm=cm)
    half = {k: cm.result(hs[k]) for k in hs}
    pa = _mm(h, w_in_full, "nn", F32, "proj_at", b_off=o4, n=oa - o4)
    cm = _Comm()
    hs = {k: _ag_d2d(cm, half[k], waxis[k]) for k in ("bhg", "bat")}
    hs["out"] = _ag_ici(cm, wblk["out"], waxis["out"])
    pg = _mm(h, w_in_full, "nn", F32, "proj_gate", b_off=oa, n=INW - oa, comm=cm)
    wf["bhg"], wf["bat"], half["out"] = (cm.result(hs[k]) for k in ("bhg", "bat", "out"))

    cm = _Comm()
    hs = {"out": _ag_d2d(cm, half["out"], waxis["out"]), "ff1": _ag_ici(cm, wblk["ff1"], waxis["ff1"])}
    o_hg, s_all = _hgrn_fwd(p4, hg_lb_logits, hg_out_norm_g, H, comm=cm)
    wf["out"], half["ff1"] = cm.result(hs["out"]), cm.result(hs["ff1"])

    bucket = _bucket_ids()
    (bias_flat,) = _whole(lambda tb, bk: (_dot(tb, _onehot(bk), TN, precision=HIGHEST),), [rel_bias_table, bucket],
                          [((AH, AT_BLOCK * 2 * AT_BLOCK), F32)], "bias_fwd")
    bias = bias_flat.reshape(AH, AT_BLOCK, 2 * AT_BLOCK)
    q_t = _heads_first(pa[:, :ATW], AH)
    pad = lambda t: jnp.pad(t, ((0, 0), (AT_BLOCK, 0), (0, 0)))
    kp = pad(_heads_first(pa[:, ATW:ATW + KVW], KVH))
    vp = pad(_heads_first(pa[:, ATW + KVW:], KVH))
    sinks3 = attn_sinks.reshape(KVH, 1, G)
    cm = _Comm()
    hs = {"ff1": _ag_d2d(cm, half["ff1"], waxis["ff1"]), "ff2": _ag_ici(cm, wblk["ff2"], waxis["ff2"])}
    o_at = _heads_last(_attn_fwd(q_t, kp, vp, q_norm_g, k_norm_g, sinks3, bias, KVH, comm=cm))
    wf["ff1"], half["ff2"] = cm.result(hs["ff1"]), cm.result(hs["ff2"])

    bh = _mm(o_hg, wf["bhg"], "nn", F32, "branch_hg")
    ba = _mm(o_at, wf["bat"], "nn", F32, "branch_at")

    def merge_fn(bhv, bav, ghg, gat):
        return jax.nn.sigmoid(ghg) * bhv + jax.nn.sigmoid(gat) * bav

    (merged,) = _rowwise(lambda *a: ((merge_fn(*a),), ()), [(bh, D, 0), (ba, D, 0), (pg, D, 0), (pg, D, 1)], [],
                         [(D, BF16)], [], "merge")
    mo = _mm(merged, wf["out"], "nn", F32, "out_proj")

    def resid1(xv, mov, g1, g2n, sh, sc):
        x1v = xv + g1 * mov
        return (x1v, _modnorm(x1v, g2n, sh, sc)), ()

    x1, h2 = _rowwise(resid1, [(x2, D, 0), (mo, D, 0)], [gate1, norm2_g, shift2, scale2], [(D, F32), (D, BF16)], [], "resid1")
    cm = _Comm()
    hs = {"ff2": _ag_d2d(cm, half["ff2"], waxis["ff2"])}
    u = _mm(h2, wf["ff1"], "nn", F32, "ff1", comm=cm)
    wf["ff2"] = cm.result(hs["ff2"])
    DFF = u.shape[1]
    (act,) = _rowwise(lambda uv: ((jnp.square(jnp.maximum(uv, 0.0)),), ()), [(u, DFF, 0)], [], [(DFF, BF16)], [], "relu2")
    ff = _mm(act, wf["ff2"], "nn", F32, "ff2")

    def loss_fn(x1v, ffv, tv, g2):
        e = x1v + g2 * ffv - tv
        dy = e * (1.0 / D)
        return (dy, dy * g2), (jnp.sum(e * e, axis=0, keepdims=True), jnp.sum(dy * ffv, axis=0, keepdims=True))

    dy, d_ff, sq_sum, d_gate2 = _rowwise(loss_fn, [(x1, D, 0), (ff, D, 0), (tgt, D, 0)], [gate2],
                                         [(D, F32), (D, BF16)], [(1, D), (1, D)], "loss")
    loss = lax.psum(jnp.sum(sq_sum) * (0.5 / D), ("x", "y", "c"))

    owner_base = jnp.stack([me ^ r for r in CHIP_RELS]).astype(jnp.int32)
    gw, recv1, part, recv2 = {}, {}, {}, {}
    gw["ff2"] = _mm(act, d_ff, "tn", BF16, "dw_ff2")
    cm = _Comm()
    hh = _rs_d2d(cm, gw["ff2"], waxis["ff2"])
    d_act = _mm(d_ff, wf["ff2"], "nt", F32, "d_act", comm=cm)
    part["ff2"] = _rs_add(gw["ff2"], cm.result(hh), waxis["ff2"], owner_base, "rs_add_ff2")
    (d_u,) = _rowwise(lambda dav, uv: ((dav * (2.0 * jnp.maximum(uv, 0.0)),), ()), [(d_act, DFF, 0), (u, DFF, 0)], [],
                      [(DFF, BF16)], [], "relu2_bwd")
    rows_ff2 = part["ff2"].shape[1]
    cm = _Comm()
    hh = _rs_ici(cm, part["ff2"], rows=(0, rows_ff2 // 2))
    gw["ff1"] = _mm(h2, d_u, "tn", BF16, "dw_ff1", comm=cm)
    cm2 = _Comm()
    hh2 = _rs_ici(cm2, part["ff2"], rows=(rows_ff2 // 2, rows_ff2), recv=cm.result(hh))
    hh1 = _rs_d2d(cm2, gw["ff1"], waxis["ff1"])
    d_h2 = _mm(d_u, wf["ff1"], "nt", F32, "d_h2", comm=cm2)
    recv2["ff2"] = cm2.result(hh2)
    part["ff1"] = _rs_add(gw["ff1"], cm2.result(hh1), waxis["ff1"], owner_base, "rs_add_ff1")

    def norm2_bwd(dh2v, x1v, dyv, mov, g2n, sh, sc, g1):
        _, vjp = jax.vjp(_modnorm, x1v, g2n, sh, sc)
        dx, dg, dsh, dsc = vjp(dh2v)
        dx1 = dyv + dx
        return (dx1, dx1 * g1), (dg, dsh, dsc, jnp.sum(dx1 * mov, axis=0, keepdims=True))

    d_x1, d_mo, d_g2n, d_shift2, d_scale2, d_gate1 = _rowwise(
        norm2_bwd, [(d_h2, D, 0), (x1, D, 0), (dy, D, 0), (mo, D, 0)], [norm2_g, shift2, scale2, gate1],
        [(D, F32), (D, BF16)], [(1, D)] * 4, "norm2_bwd")
    gw["out"] = _mm(merged, d_mo, "tn", BF16, "dw_out")
    cm = _Comm()
    hh = _rs_d2d(cm, gw["out"], waxis["out"])
    d_merged = _mm(d_mo, wf["out"], "nt", F32, "d_merged", comm=cm)
    part["out"] = _rs_add(gw["out"], cm.result(hh), waxis["out"], owner_base, "rs_add_out")

    def merge_bwd(dmv, bhv, bav, ghg, gat):
        _, vjp = jax.vjp(merge_fn, bhv, bav, ghg, gat)
        return vjp(dmv), ()

    d_bh, d_ba, d_ghg, d_gat = _rowwise(merge_bwd, [(d_merged, D, 0), (bh, D, 0), (ba, D, 0), (pg, D, 0), (pg, D, 1)], [],
                                        [(D, BF16)] * 4, [], "merge_bwd")
    gw["bhg"] = _mm(o_hg, d_bh, "tn", BF16, "dw_bhg")
    gw["bat"] = _mm(o_at, d_ba, "tn", BF16, "dw_bat")
    cm = _Comm()
    hh = {k: _rs_d2d(cm, gw[k], waxis[k]) for k in ("bhg", "bat")}
    d_ohg = _mm(d_bh, wf["bhg"], "nt", F32, "d_ohg", comm=cm)
    for k in ("bhg", "bat"):
        part[k] = _rs_add(gw[k], cm.result(hh[k]), waxis[k], owner_base, "rs_add_" + k)
    d_oat = _mm(d_ba, wf["bat"], "nt", BF16, "d_oat")

    cm = _Comm()
    hh = {k: _rs_ici(cm, part[k]) for k in ("ff1", "out", "bhg", "bat")}
    d_hq, d_hf, d_hi, d_hg, d_lb, d_gout_h = _hgrn_bwd(p4, hg_lb_logits, hg_out_norm_g, s_all, d_ohg, H, comm=cm)
    for k in hh:
        recv2[k] = cm.result(hh[k])
    dq_t, dkp, dvp, d_qg, d_kg, d_sk, d_bias = _attn_bwd(q_t, kp, vp, q_norm_g, k_norm_g, sinks3, bias,
                                                         _heads_first(d_oat, AH), KVH)
    d_aq = _heads_last(dq_t)
    d_ak = _heads_last(dkp[:, AT_BLOCK:, :]).astype(BF16)
    d_av = _heads_last(dvp[:, AT_BLOCK:, :]).astype(BF16)
    d_proj = jnp.concatenate([d_hq, d_hf, d_hi, d_hg, d_aq, d_ak, d_av, d_ghg, d_gat], axis=1)
    gw_in = _mm(h, d_proj, "tn", BF16, "dw_in")

    wm = LANES * A
    cm = _Comm()
    hi_ = cm.inp(gw_in)
    h_main, h_mid = cm.out((4, D, wm), BF16), cm.out((4, D, LANES), BF16)
    for i, r in enumerate(CHIP_RELS):
        def main_view(ref, p, r=r):
            o = p["me"] ^ r ^ 1
            return ref.at[:, pl.ds(pl.multiple_of((PAIR * (o // 2) + (A + 1) * (1 - p["c"])) * LANES, LANES), wm)]

        def mid_view(ref, p, r=r):
            o = p["me"] ^ r
            return ref.at[:, pl.ds(pl.multiple_of((PAIR * (o // 2) + A) * LANES, LANES), LANES)]

        cm.copy(hi_, main_view, h_main, _slot_view(i), 1)
        cm.copy(hi_, mid_view, h_mid, _slot_view(i), 1)
    _call(lambda: None, [], name="rs_d2d_in", out_shape=(), comm=cm)
    chip = jnp.stack([(me ^ r) // 2 for r in CHIP_RELS]).astype(jnp.int32)
    part_main = _rs_add(gw_in, cm.result(h_main), 1, PAIR * chip + (A + 1) * cc, "rs_add_in_main", tw=LANES)
    part_mid = _rs_add(gw_in, cm.result(h_mid), 1, PAIR * chip + A, "rs_add_in_mid", tw=LANES)
    cm = _Comm()
    hh_main, hh_mid = _rs_ici(cm, part_main), _rs_ici(cm, part_mid)
    d_h = _mm(d_proj, w_in_full, "nt", F32, "d_h", comm=cm)
    rx_main, rx_mid = cm.result(hh_main), cm.result(hh_mid)

    def norm1_bwd(dhv, xv, dx1v, g1n, sh, sc):
        _, vjp = jax.vjp(_modnorm, xv, g1n, sh, sc)
        dx, dg, dsh, dsc = vjp(dhv)
        return (dx1v + dx,), (dg, dsh, dsc)

    grad_x, d_g1n, d_shift1, d_scale1 = _rowwise(norm1_bwd, [(d_h, D, 0), (x2, D, 0), (d_x1, D, 0)],
                                                 [norm1_g, shift1, scale1], [(D, F32)], [(1, D)] * 3, "norm1_bwd")

    def sum4(p0, p1, p2, p3):
        return ((p0.astype(F32) + p1.astype(F32)) + p2.astype(F32)) + p3.astype(F32)

    def update_fn(w, m, v, p0, p1, p2, p3):
        g = sum4(p0, p1, p2, p3)
        delta, mn, vn = _adamw(w, g, m, v)
        return (g, delta, mn, vn), ()

    wmv = dict(zip(wnames, ((w_branch_hg, m_w_branch_hg, v_w_branch_hg), (w_branch_attn, m_w_branch_attn, v_w_branch_attn),
                            (w_out, m_w_out, v_w_out), (w_ff1, m_w_ff1, v_w_ff1), (w_ff2, m_w_ff2, v_w_ff2))))
    res = {}
    for k in wnames:
        w, m, v = (t[0] for t in wmv[k])
        n = w.shape[1]
        ins = [(t, n, 0) for t in (w, m, v)] + [(part[k], n, 0, 0)] + [(recv2[k], n, 0, i) for i in range(3)]
        res[k] = [t[None] for t in _rowwise(update_fn, ins, [], [(n, F32)] * 4, [], "update_" + k)]

    g_main, = _rowwise(lambda *p: ((sum4(*p),), ()), [(part_main, wm, 0, 0)] + [(rx_main, wm, 0, i) for i in range(3)], [],
                       [(wm, F32)], [], "sum_in_main")
    g_mid, = _rowwise(lambda *p: ((sum4(*p),), ()), [(part_mid, LANES, 0, 0)] + [(rx_mid, LANES, 0, i) for i in range(3)], [],
                      [(LANES, F32)], [], "sum_in_mid")
    g_in = jnp.where(cc == 0, jnp.concatenate([g_main, g_mid[:, :LANES // 2]], axis=1),
                     jnp.concatenate([g_mid[:, LANES // 2:], g_main], axis=1))

    def update_given(w, m, v, g):
        delta, mn, vn = _adamw(w, g, m, v)
        return (g, delta, mn, vn), ()

    res["in"] = [t[None] for t in _rowwise(update_given, [(t, BW, 0) for t in (w_in[0], m_w_in[0], v_w_in[0], g_in)], [],
                                           [(BW, F32)] * 4, [], "update_in")]

    d_ada_row = jnp.concatenate([d_shift1, d_scale1, d_gate1, d_shift2, d_scale2, d_gate2], axis=1)
    d_ada_all = _gather_small(d_ada_row, me, "gather_dada")[:, 0, :]
    d_ada_cols = lax.dynamic_slice(d_ada_all, (0, me * ADA_N), (N_DEV, ADA_N))

    def ada_update(cv, dav, w, m, v):
        g = _bdot(_silu(cv), dav, TN)
        delta, mn, vn = _adamw(w, g, m, v)
        return (g, delta, mn, vn), ()

    res["ada"] = [t[None] for t in _ada_update_call(ada_update, c_all, d_ada_cols, w_ada[0], m_w_ada[0], v_w_ada[0], _tile(D, 256, 16))]

    d_sinks = d_sk.reshape(1, AH)
    (d_table_t,) = _whole(lambda db, bk: (_dot(db, _onehot(bk), NT, precision=HIGHEST),),
                          [d_bias.reshape(AH, AT_BLOCK * 2 * AT_BLOCK), bucket], [((AH, N_BUCKETS), F32)], "bias_bwd")
    smalls = [d_g1n, d_g2n, d_lb, d_gout_h, d_qg, d_kg, d_sinks, d_table_t.T.reshape(1, N_BUCKETS * AH)]
    widths = [s.shape[1] for s in smalls]
    lanes = [-(-w // LANES) * LANES for w in widths]
    smalls = [jnp.pad(s, ((0, 0), (0, p - w))) for s, w, p in zip(smalls, widths, lanes)]
    packed = _gather_small(jnp.concatenate(smalls, axis=1), me, "gather_small")[:, 0, :]
    offs = [sum(lanes[:i]) for i in range(len(lanes))]

    def small_update(pk, dada, lg, *wmv_flat):
        tot = pk[0:1]
        for d in range(1, N_DEV):
            tot = tot + pk[d:d + 1]
        gb = dada[0:1]
        for d in range(1, N_DEV):
            gb = gb + dada[d:d + 1]
        gs = [tot[:, offs[i]:offs[i] + widths[i]] for i in range(len(widths))]
        _, lb_vjp = jax.vjp(_softmax0, lg)
        (g_lg,) = lb_vjp(gs[2])
        grads = [gb, gs[0], gs[1], g_lg, gs[3], gs[4], gs[5], gs[6], gs[7]]
        outs = []
        for i, g in enumerate(grads):
            w, m, v = wmv_flat[3 * i:3 * i + 3]
            delta, mn, vn = _adamw(w, g, m, v)
            outs += [g, delta, mn, vn]
        return tuple(outs)

    tbl = lambda t: t.reshape(1, N_BUCKETS * AH)
    small_wmv = [(b_ada, m_b_ada, v_b_ada), (norm1_g, m_norm1_g, v_norm1_g), (norm2_g, m_norm2_g, v_norm2_g),
                 (hg_lb_logits, m_hg_lb_logits, v_hg_lb_logits), (hg_out_norm_g, m_hg_out_norm_g, v_hg_out_norm_g),
                 (q_norm_g, m_q_norm_g, v_q_norm_g), (k_norm_g, m_k_norm_g, v_k_norm_g),
                 (attn_sinks, m_attn_sinks, v_attn_sinks),
                 (tbl(rel_bias_table), tbl(m_rel_bias_table), tbl(v_rel_bias_table))]
    flat = [t for trip in small_wmv for t in trip]
    out_shapes = [(trip[0].shape, F32) for trip in small_wmv for _ in range(4)]
    sres = _whole(small_update, [packed, d_ada_all, hg_lb_logits] + flat, out_shapes, "small_update")
    names_small = ("b_ada", "norm1_g", "norm2_g", "lb", "gout", "qg", "kg", "sinks", "table")
    for i, k in enumerate(names_small):
        r = sres[4 * i:4 * i + 4]
        if k == "table":
            r = [t.reshape(N_BUCKETS, AH) for t in r]
        res[k] = r

    order = ("ada", "b_ada", "norm1_g", "norm2_g", "in", "lb", "gout", "qg", "kg", "sinks", "table", "bhg", "bat", "out", "ff1", "ff2")
    outs = [loss, grad_x[None]]
    for j in range(4):
        outs += [res[k][j] for k in order]
    return tuple(outs)
```

```python
import functools
import math

import jax
import jax.numpy as jnp
from jax import lax
from jax.experimental import pallas as pl
from jax.experimental.pallas import tpu as pltpu

F32 = jnp.float32
BF16 = jnp.bfloat16
EPS = 1e-6
NEG_INF = -1e30
HG_DK = 128
HG_CHUNK = 64
AT_BLOCK = 128
N_BUCKETS = 32
MAX_EXACT = 16
MAX_DISTANCE = 128
N_DEV = 8
LANES = 128
VMEM_LIMIT = 56 * 1024 * 1024
ADAM_LR, ADAM_B1, ADAM_B2, ADAM_EPS, ADAM_WD, ADAM_STEP = 0.001, 0.9, 0.999, 1e-08, 0.01, 10
HIGHEST = lax.Precision.HIGHEST
MESH = pl.DeviceIdType.MESH
ANY = pl.BlockSpec(memory_space=pl.ANY)
CHIP_RELS = (0, 4, 2, 6)

NN = (((1,), (0,)), ((), ()))
NT = (((1,), (1,)), ((), ()))
TN = (((0,), (0,)), ((), ()))


def _tile(n, pref, unit):
    if n <= pref:
        return n
    t = (pref // unit) * unit
    while t >= unit:
        if n % t == 0:
            return t
        t -= unit
    return n


def _dot(a, b, dn, precision=None):
    return lax.dot_general(a, b, dn, preferred_element_type=F32, precision=precision)


def _bdot(a, b, dn):
    return _dot(a.astype(BF16), b.astype(BF16), dn)


def _position():
    x, y, c = lax.axis_index("x"), lax.axis_index("y"), lax.axis_index("c")
    return dict(x=x, y=y, c=c, me=4 * x + 2 * y + c)


def _peer_position(p, rel):
    x = 1 - p["x"] if rel & 4 else p["x"]
    y = 1 - p["y"] if rel & 2 else p["y"]
    c = 1 - p["c"] if rel & 1 else p["c"]
    return dict(x=x, y=y, c=c, me=4 * x + 2 * y + c)


class _Comm:
    def __init__(self):
        self.ins, self.outs, self.alias, self.plans, self.res = [], [], {}, [], None

    def inp(self, arr):
        self.ins.append(arr)
        return ("i", len(self.ins) - 1)

    def out(self, shape, dtype, alias=None):
        self.outs.append(jax.ShapeDtypeStruct(tuple(shape), dtype))
        if alias is not None:
            self.alias[alias[1]] = len(self.outs) - 1
        return ("o", len(self.outs) - 1)

    def copy(self, src, src_view, dst, dst_view, rel):
        self.plans.append((src, src_view, dst, dst_view, rel))

    def result(self, handle):
        return self.res[handle[1]]

    def build(self, in_refs, out_refs, send_sems, recv_sems):
        pos = _position()
        ref = lambda h: in_refs[h[1]] if h[0] == "i" else out_refs[h[1]]
        ops = []
        for k, (src, sv, dst, dv, rel) in enumerate(self.plans):
            s = sv(ref(src), pos)
            if rel == 0:
                cp = pltpu.make_async_copy(s, dv(ref(dst), pos), send_sems.at[k])
                ops.append((cp.start, cp.wait))
                continue
            peer = _peer_position(pos, rel)
            mk = lambda d: pltpu.make_async_remote_copy(
                src_ref=s, dst_ref=d, send_sem=send_sems.at[k], recv_sem=recv_sems.at[k],
                device_id=(peer["x"], peer["y"], peer["c"]), device_id_type=MESH)
            out_cp, in_cp = mk(dv(ref(dst), pos)), mk(dv(ref(dst), peer))

            def wait(out_cp=out_cp, in_cp=in_cp):
                out_cp.wait_send()
                in_cp.wait_recv()

            ops.append((out_cp.start, wait))
        return ops


def _call(body, args, *, name, out_shape, in_specs=None, out_specs=None, grid=None, scratch_shapes=(), comm=None,
          prefetch=None, aliases=None):
    single = not isinstance(out_shape, (tuple, list))
    out_shape = (out_shape,) if single else tuple(out_shape)
    n_in, n_out, n_scr = len(args), len(out_shape), len(scratch_shapes)
    vm = pl.BlockSpec(memory_space=pltpu.VMEM)
    in_specs = [vm] * n_in if in_specs is None else list(in_specs)
    out_specs = [vm] * n_out if out_specs is None else (list(out_specs) if isinstance(out_specs, (tuple, list)) else [out_specs])
    n_pf = 0 if prefetch is None else len(prefetch)
    kw = {} if aliases is None else {"input_output_aliases": dict(aliases)}
    if comm is None:
        fn = body
        all_args, all_scratch = list(args), list(scratch_shapes)
    else:
        n_ci, n_co, n_x = len(comm.ins), len(comm.outs), len(comm.plans)

        def fn(*refs):
            pf, refs = refs[:n_pf], refs[n_pf:]
            o_in, c_in = refs[:n_in], refs[n_in:n_in + n_ci]
            o_out = refs[n_in + n_ci:n_in + n_ci + n_out]
            c_out = refs[n_in + n_ci + n_out:n_in + n_ci + n_out + n_co]
            scr = refs[n_in + n_ci + n_out + n_co:]
            ops = comm.build(c_in, c_out, scr[n_scr], scr[n_scr + 1])
            if grid:
                first = functools.reduce(jnp.logical_and, [pl.program_id(i) == 0 for i in range(len(grid))])
                last = functools.reduce(jnp.logical_and, [pl.program_id(i) == g - 1 for i, g in enumerate(grid)])

                @pl.when(first)
                def _():
                    for start, _w in ops:
                        start()
            else:
                for start, _w in ops:
                    start()
            body(*pf, *o_in, *o_out, *scr[:n_scr])
            if grid:
                @pl.when(last)
                def _():
                    for _s, wait in ops:
                        wait()
            else:
                for _s, wait in ops:
                    wait()

        all_args = list(args) + list(comm.ins)
        in_specs = in_specs + [ANY] * n_ci
        out_shape = out_shape + tuple(comm.outs)
        out_specs = out_specs + [ANY] * n_co
        all_scratch = list(scratch_shapes) + [pltpu.SemaphoreType.DMA((n_x,)), pltpu.SemaphoreType.DMA((n_x,))]
        kw["input_output_aliases"] = {n_pf + n_in + i: n_out + o for i, o in comm.alias.items()}
    sem = None if grid is None else ("arbitrary",) * len(grid)
    params = pltpu.CompilerParams(dimension_semantics=sem, vmem_limit_bytes=VMEM_LIMIT)
    if prefetch is None:
        spec = dict(in_specs=in_specs, out_specs=tuple(out_specs), scratch_shapes=all_scratch)
        if grid is not None:
            spec["grid"] = grid
    else:
        spec = dict(grid_spec=pltpu.PrefetchScalarGridSpec(
            num_scalar_prefetch=n_pf, grid=grid, in_specs=in_specs, out_specs=tuple(out_specs), scratch_shapes=all_scratch))
        all_args = list(prefetch) + all_args
    res = pl.pallas_call(fn, name=name, out_shape=out_shape, compiler_params=params, **spec, **kw)(*all_args)
    res = list(res)
    if comm is not None:
        comm.res = res[n_out:]
        res = res[:n_out]
    return res[0] if single else res


def _whole_view(ref, pos):
    return ref


def _block_view(axis, n, index, rows=None):
    def view(ref, pos):
        off = pl.multiple_of(index(pos) * n, n)
        if rows is None:
            return ref.at[:, pl.ds(off, n)] if axis == 1 else ref.at[pl.ds(off, n), :]
        lo, cnt = rows[0], rows[1] - rows[0]
        if axis == 1:
            return ref.at[pl.ds(lo, cnt), pl.ds(off, n)]
        return ref.at[pl.ds(pl.multiple_of(off + lo, 16), cnt), :]
    return view


def _rows_view(rows):
    def view(ref, pos):
        return ref if rows is None else ref.at[pl.ds(rows[0], rows[1] - rows[0]), :]
    return view


def _slot_view(i, rows=None):
    def view(ref, pos):
        return ref.at[i] if rows is None else ref.at[i, pl.ds(rows[0], rows[1] - rows[0]), :]
    return view


def _exchange(items, name):
    cm = _Comm()
    for a, rel in items:
        cm.copy(cm.inp(a), _whole_view, cm.out(a.shape, a.dtype), _whole_view, rel)
    _call(lambda: None, [], name=name, out_shape=(), comm=cm)
    return cm.res


def _gather_small(v, me, name):
    got = _exchange([(v, rel) for rel in range(1, N_DEV)], name)
    out = jnp.zeros((N_DEV,) + v.shape, v.dtype)
    out = lax.dynamic_update_slice(out, v[None], (me, 0, 0))
    for rel in range(1, N_DEV):
        out = lax.dynamic_update_slice(out, got[rel - 1][None], (me ^ rel, 0, 0))
    return out


def _ag_ici(cm, blk, axis, rows=None, into=None):
    n = blk.shape[axis]
    shape = list(blk.shape)
    shape[axis] = n * N_DEV
    hi = cm.inp(blk)
    ho = cm.out(shape, blk.dtype) if into is None else cm.out(shape, blk.dtype, alias=cm.inp(into))
    own = _block_view(axis, n, lambda p: p["me"], rows)
    for rel in CHIP_RELS:
        cm.copy(hi, _rows_view(rows), ho, own, rel)
    return ho


def _ag_d2d(cm, full, axis):
    n = full.shape[axis] // N_DEV
    hi = cm.inp(full)
    ho = cm.out(full.shape, full.dtype, alias=hi)
    for r in CHIP_RELS:
        v = _block_view(axis, n, functools.partial(lambda p, r: p["me"] ^ r, r=r))
        cm.copy(hi, v, ho, v, 1)
    return ho


def _rs_d2d(cm, gw, axis):
    n = gw.shape[axis] // N_DEV
    shape = list(gw.shape)
    shape[axis] = n
    hi, ho = cm.inp(gw), cm.out([4] + shape, gw.dtype)
    for i, r in enumerate(CHIP_RELS):
        cm.copy(hi, _block_view(axis, n, functools.partial(lambda p, r: p["me"] ^ r ^ 1, r=r)), ho, _slot_view(i), 1)
    return ho


def _rs_ici(cm, part, rows=None, recv=None):
    if recv is None:
        ho = cm.out((3,) + part.shape[1:], part.dtype)
    else:
        ho = cm.out(recv.shape, recv.dtype, alias=cm.inp(recv))
    hi = cm.inp(part)
    for i in (1, 2, 3):
        cm.copy(hi, _slot_view(i, rows), ho, _slot_view(i - 1, rows), CHIP_RELS[i])
    return ho


def _rs_add(gw, recv, axis, base, name, tw=None):
    _, R, n = recv.shape
    if axis == 1:
        tw = n if tw is None else tw
        gw_spec = pl.BlockSpec((R, tw), lambda i, t, b: (0, b[i] + t))
        rv_spec = pl.BlockSpec((None, R, tw), lambda i, t, b: (i, 0, t))
        grid = (4, n // tw)
    else:
        tw = _tile(n, 1024, LANES)
        gw_spec = pl.BlockSpec((R, tw), lambda i, t, b: (b[i], t))
        rv_spec = pl.BlockSpec((None, R, tw), lambda i, t, b: (i, 0, t))
        grid = (4, n // tw)

    def body(b_ref, g_ref, r_ref, o_ref):
        o_ref[...] = (g_ref[...].astype(F32) + r_ref[...].astype(F32)).astype(o_ref.dtype)

    return _call(body, [gw, recv], name=name, out_shape=jax.ShapeDtypeStruct(recv.shape, recv.dtype), grid=grid,
                 in_specs=[gw_spec, rv_spec], out_specs=rv_spec, prefetch=[base])


def _ag_w_in(src, a, D, INW):
    wm = LANES * a

    def main_place(ref, p):
        off = pl.multiple_of(((2 * a + 1) * (p["me"] // 2) + (a + 1) * p["c"]) * LANES, LANES)
        return ref.at[:, pl.ds(off, wm)]

    def main_src(ref, p):
        return ref.at[:, pl.ds(pl.multiple_of(p["c"] * LANES, LANES), wm)]

    def mid_src(ref, p):
        return ref.at[:, pl.ds(pl.multiple_of((1 - p["c"]) * wm, LANES), LANES)]

    def mid_place(ref, p):
        return ref.at[p["me"]]

    def body(src_ref, full_ref, mid_ref, send_sems, recv_sems):
        pos = _position()
        sib = _peer_position(pos, 1)

        def remote(k, s, d, to):
            return pltpu.make_async_remote_copy(src_ref=s, dst_ref=d, send_sem=send_sems.at[k], recv_sem=recv_sems.at[k],
                                                device_id=(to["x"], to["y"], to["c"]), device_id_type=MESH)

        local = [pltpu.make_async_copy(main_src(src_ref, pos), main_place(full_ref, pos), send_sems.at[16]),
                 pltpu.make_async_copy(mid_src(src_ref, pos), mid_place(mid_ref, pos), send_sems.at[17])]
        for cp in local:
            cp.start()
        sends = []
        for i, rel in enumerate(CHIP_RELS):
            to = sib if rel == 0 else _peer_position(pos, rel)
            sends.append(remote(2 * i, main_src(src_ref, pos), main_place(full_ref, pos), to))
            sends.append(remote(2 * i + 1, mid_src(src_ref, pos), mid_place(mid_ref, pos), to))
        for cp in sends:
            cp.start()
        for i, rel in enumerate(CHIP_RELS[1:], start=1):
            frm = _peer_position(pos, rel)
            remote(2 * i, main_src(src_ref, pos), main_place(full_ref, frm), frm).wait_recv()
            fwd = remote(8 + 2 * i, main_place(full_ref, frm), main_place(full_ref, frm), sib)
            fwd.start()
            sends.append(fwd)
            remote(2 * i + 1, mid_src(src_ref, pos), mid_place(mid_ref, frm), frm).wait_recv()
            fwd = remote(9 + 2 * i, mid_place(mid_ref, frm), mid_place(mid_ref, frm), sib)
            fwd.start()
            sends.append(fwd)
        remote(0, main_src(src_ref, pos), main_place(full_ref, sib), sib).wait_recv()
        remote(1, mid_src(src_ref, pos), mid_place(mid_ref, sib), sib).wait_recv()
        for i, rel in enumerate(CHIP_RELS[1:], start=1):
            frm = _peer_position(sib, rel)
            remote(8 + 2 * i, main_src(src_ref, pos), main_place(full_ref, frm), sib).wait_recv()
            remote(9 + 2 * i, mid_src(src_ref, pos), mid_place(mid_ref, frm), sib).wait_recv()
        for cp in sends:
            cp.wait_send()
        for cp in local:
            cp.wait()

    return _call(body, [src], name="ag_w_in", in_specs=[ANY], out_specs=[ANY, ANY],
                 out_shape=(jax.ShapeDtypeStruct((D, INW), BF16), jax.ShapeDtypeStruct((N_DEV, D, LANES), BF16)),
                 scratch_shapes=[pltpu.SemaphoreType.DMA((18,)), pltpu.SemaphoreType.DMA((18,))])


def _patch_mid(full, mid, a):
    D = full.shape[0]

    def body(full_ref, e_ref, o_ref, out_ref):
        out_ref[...] = e_ref[...] + o_ref[...]

    return _call(body, [full, mid, mid], name="patch_mid", grid=(N_DEV // 2,),
                 out_shape=jax.ShapeDtypeStruct(full.shape, full.dtype),
                 in_specs=[ANY, pl.BlockSpec((None, D, LANES), lambda j: (2 * j, 0, 0)),
                           pl.BlockSpec((None, D, LANES), lambda j: (2 * j + 1, 0, 0))],
                 out_specs=pl.BlockSpec((D, LANES), lambda j: (0, (2 * a + 1) * j + a)), aliases={0: 0})


def _mm(a, b, mode, out_dtype, name, tm=1024, tn=1024, tk=512, b_off=0, n=None, comm=None):
    if mode == "nn":
        (M, K), (K2, N) = a.shape, b.shape
    elif mode == "nt":
        (M, K), (N, K2) = a.shape, b.shape
    else:
        (K, M), (K2, N) = a.shape, b.shape
    assert K == K2, (a.shape, b.shape, mode)
    if n is not None:
        N = n
    tm, tk = _tile(M, tm, LANES), _tile(K, tk, LANES)
    tn = _tile(math.gcd(N, b_off) if b_off else N, tn, LANES)
    nk, jb = K // tk, b_off // tn
    dn = {"nn": NN, "nt": NT, "tn": TN}[mode]

    def body(a_ref, b_ref, o_ref, acc_ref):
        k = pl.program_id(2)

        @pl.when(k == 0)
        def _():
            acc_ref[...] = jnp.zeros_like(acc_ref)

        acc_ref[...] += _bdot(a_ref[...], b_ref[...], dn)

        @pl.when(k == nk - 1)
        def _():
            o_ref[...] = acc_ref[...].astype(out_dtype)

    a_spec = pl.BlockSpec((tk, tm), lambda i, j, k: (k, i)) if mode == "tn" else pl.BlockSpec((tm, tk), lambda i, j, k: (i, k))
    b_spec = pl.BlockSpec((tn, tk), lambda i, j, k: (j, k)) if mode == "nt" else pl.BlockSpec((tk, tn), lambda i, j, k: (k, j + jb))
    return _call(body, [a, b], name=name, grid=(M // tm, N // tn, nk), out_shape=jax.ShapeDtypeStruct((M, N), out_dtype),
                 in_specs=[a_spec, b_spec], out_specs=pl.BlockSpec((tm, tn), lambda i, j, k: (i, j)),
                 scratch_shapes=[pltpu.VMEM((tm, tn), F32)], comm=comm)


def _rowwise(fn, row_ins, bcast_ins, row_outs, acc_outs, name, rt=256, comm=None):
    L = row_ins[0][0].shape[-2]
    rt = _tile(L, rt, 16)
    nr, nb, no = len(row_ins), len(bcast_ins), len(row_outs)

    def body(*refs):
        i = pl.program_id(0)
        vals = [r[...] for r in refs[:nr + nb]]
        outs, accs = fn(*vals)
        for r, v in zip(refs[nr + nb:nr + nb + no], outs):
            r[...] = v.astype(r.dtype)
        acc_refs = refs[nr + nb + no:]

        @pl.when(i == 0)
        def _():
            for r in acc_refs:
                r[...] = jnp.zeros_like(r)

        for r, v in zip(acc_refs, accs):
            r[...] += v

    in_specs = []
    for spec in row_ins:
        w, cb = spec[1], spec[2]
        if len(spec) == 4:
            in_specs.append(pl.BlockSpec((None, rt, w), functools.partial(lambda i, cb, ld: (ld, i, cb), cb=cb, ld=spec[3])))
        else:
            in_specs.append(pl.BlockSpec((rt, w), functools.partial(lambda i, cb: (i, cb), cb=cb)))
    in_specs += [pl.BlockSpec(b.shape, lambda i: (0, 0)) for b in bcast_ins]
    out_specs = [pl.BlockSpec((rt, w), lambda i: (i, 0)) for w, _ in row_outs]
    out_specs += [pl.BlockSpec(s, lambda i: (0, 0)) for s in acc_outs]
    out_shape = [jax.ShapeDtypeStruct((L, w), dt) for w, dt in row_outs] + [jax.ShapeDtypeStruct(s, F32) for s in acc_outs]
    return _call(body, [s[0] for s in row_ins] + list(bcast_ins), name=name, grid=(L // rt,), out_shape=tuple(out_shape),
                 in_specs=in_specs, out_specs=out_specs, comm=comm)


def _whole(fn, ins, out_shapes, name):
    def body(*refs):
        outs = fn(*[r[...] for r in refs[:len(ins)]])
        for r, v in zip(refs[len(ins):], outs):
            r[...] = v.astype(r.dtype)

    return _call(body, list(ins), name=name, out_shape=tuple(jax.ShapeDtypeStruct(s, dt) for s, dt in out_shapes))


def _silu(x):
    return x * jax.nn.sigmoid(x)


def _rms(x, g):
    return (x * lax.rsqrt(jnp.mean(x * x, axis=-1, keepdims=True) + EPS)) * g


def _modnorm(x, g, shift, scale):
    return _rms(x, g) * (1.0 + scale) + shift


def _adamw(w, g, m, v):
    m = ADAM_B1 * m + (1.0 - ADAM_B1) * g
    v = ADAM_B2 * v + (1.0 - ADAM_B2) * jnp.square(g)
    m_hat = m / (1.0 - ADAM_B1 ** ADAM_STEP)
    v_hat = v / (1.0 - ADAM_B2 ** ADAM_STEP)
    delta = -ADAM_LR * (m_hat / (jnp.sqrt(v_hat) + ADAM_EPS) + ADAM_WD * w)
    return delta, m, v


def _lower_bound(lg):
    e = jnp.exp(lg - jnp.max(lg, axis=0, keepdims=True))
    return e[0:1] / jnp.sum(e, axis=0, keepdims=True)


def _hg_chunk(hq, hf, hi, lb, st):
    C = hq.shape[0]
    row = lax.broadcasted_iota(jnp.int32, (C, C), 0)
    col = lax.broadcasted_iota(jnp.int32, (C, C), 1)
    tri = row >= col
    sg = jax.nn.sigmoid(hf)
    f = lb + (1.0 - lb) * sg
    lf = jnp.log(f)
    k = 1.0 - f
    q = _silu(hq)
    b = _dot(tri.astype(F32), lf, NN, precision=HIGHEST)
    m = b[C // 2 - 1:C // 2]
    bl = b[C - 1:C]
    e_qm, e_km, e_kl, e_q = jnp.exp(b - m), jnp.exp(m - b), jnp.exp(bl - b), jnp.exp(b)
    qe, ke, kd, qb = q * e_qm, k * e_km, k * e_kl, q * e_q
    sc = jnp.where(tri, _bdot(qe, ke, NT), 0.0)
    o = _bdot(sc, hi, NN) + _bdot(qb, st, NT)
    dec = jnp.exp(bl)
    st_next = st * dec + _bdot(hi, kd, TN)
    return o, st_next, dict(tri=tri, sg=sg, f=f, k=k, q=q, qe=qe, ke=ke, kd=kd, qb=qb, sc=sc, dec=dec,
                            e_qm=e_qm, e_km=e_km, e_kl=e_kl, e_q=e_q)


def _hg_out(o, hgate, gout):
    return _rms(o, gout) * _silu(hgate)


HG_GROUP = 8


def _hgrn_fwd(p4, lb_logits, gout, H, comm=None):
    L = p4.shape[0]
    C = HG_CHUNK
    GR = _tile(L // C, HG_GROUP, 1)
    T = GR * C
    N = L // T

    def body(hq_ref, hf_ref, hi_ref, hg_ref, lg_ref, gout_ref, o_ref, s_ref, st_ref):
        @pl.when(pl.program_id(1) == 0)
        def _():
            st_ref[...] = jnp.zeros_like(st_ref)

        lb = _lower_bound(lg_ref[...])
        st = st_ref[...]
        for ci in range(GR):
            rows = pl.ds(ci * C, C)
            s_ref[0, ci] = st
            o, st, _ = _hg_chunk(hq_ref[rows, :], hf_ref[rows, :], hi_ref[rows, :], lb, st)
            o_ref[rows, :] = _hg_out(o, hg_ref[rows, :], gout_ref[...]).astype(o_ref.dtype)
        st_ref[...] = st

    blk = lambda s: pl.BlockSpec((T, HG_DK), functools.partial(lambda h, n, s: (n, s * H + h), s=s))
    return _call(
        body, [p4, p4, p4, p4, lb_logits, gout], name="hgrn_fwd", grid=(H, N),
        out_shape=(jax.ShapeDtypeStruct((L, H * HG_DK), BF16), jax.ShapeDtypeStruct((H, N * GR, HG_DK, HG_DK), F32)),
        in_specs=[blk(0), blk(1), blk(2), blk(3), pl.BlockSpec((2, HG_DK), lambda h, n: (0, h)),
                  pl.BlockSpec((1, HG_DK), lambda h, n: (0, 0))],
        out_specs=(pl.BlockSpec((T, HG_DK), lambda h, n: (n, h)),
                   pl.BlockSpec((1, GR, HG_DK, HG_DK), lambda h, n: (h, n, 0, 0))),
        scratch_shapes=[pltpu.VMEM((HG_DK, HG_DK), F32)], comm=comm)


def _hgrn_bwd(p4, lb_logits, gout, s_all, d_out, H, comm=None):
    L = p4.shape[0]
    C = HG_CHUNK
    GR = _tile(L // C, HG_GROUP, 1)
    T = GR * C
    N = L // T

    def body(hq_ref, hf_ref, hi_ref, hg_ref, lg_ref, gout_ref, s_ref, do_ref,
             dq_ref, df_ref, di_ref, dg_ref, dlb_ref, dgo_ref, dst_ref):
        @pl.when(pl.program_id(1) == 0)
        def _():
            dst_ref[...] = jnp.zeros_like(dst_ref)
            dlb_ref[...] = jnp.zeros_like(dlb_ref)

        @pl.when(jnp.logical_and(pl.program_id(0) == 0, pl.program_id(1) == 0))
        def _():
            dgo_ref[...] = jnp.zeros_like(dgo_ref)

        lb = _lower_bound(lg_ref[...])
        dst = dst_ref[...]
        d_lb = jnp.zeros((1, HG_DK), F32)
        d_go = jnp.zeros((1, HG_DK), F32)
        for ci in reversed(range(GR)):
            rows = pl.ds(ci * C, C)
            dst, d_lb_c, d_go_c = chunk_bwd(rows, lb, s_ref[0, ci], dst, hq_ref, hf_ref, hi_ref, hg_ref, gout_ref, do_ref,
                                            dq_ref, df_ref, di_ref, dg_ref)
            d_lb += d_lb_c
            d_go += d_go_c
        dst_ref[...] = dst
        dlb_ref[...] += d_lb
        dgo_ref[...] += d_go

    def chunk_bwd(rows, lb, st, dst_next, hq_ref, hf_ref, hi_ref, hg_ref, gout_ref, do_ref, dq_ref, df_ref, di_ref, dg_ref):
        hq, hf, hi, hgate = hq_ref[rows, :], hf_ref[rows, :], hi_ref[rows, :], hg_ref[rows, :]
        o, _, t = _hg_chunk(hq, hf, hi, lb, st)
        _, out_vjp = jax.vjp(_hg_out, o, hgate, gout_ref[...])
        do, d_hgate, d_gout = out_vjp(do_ref[rows, :])
        tri = t["tri"]
        dsc = jnp.where(tri, _bdot(do, hi, NT), 0.0)
        dv = _bdot(t["sc"], do, TN) + _bdot(t["kd"], dst_next, NT)
        dqe = _bdot(dsc, t["ke"], NN)
        dke = _bdot(dsc, t["qe"], TN)
        dqb = _bdot(do, st, NN)
        dkd = _bdot(hi, dst_next, NN)
        ddec = jnp.sum(dst_next * st, axis=0, keepdims=True)
        dst_prev = _bdot(do, t["qb"], TN) + dst_next * t["dec"]
        dq = dqe * t["e_qm"] + dqb * t["e_q"]
        dk = dke * t["e_km"] + dkd * t["e_kl"]
        tq, tk, td, tb = dqe * t["qe"], dke * t["ke"], dkd * t["kd"], dqb * t["qb"]
        db = tq - tk - td + tb
        dm = jnp.sum(tk - tq, axis=0, keepdims=True)
        dbl = jnp.sum(td, axis=0, keepdims=True) + ddec * t["dec"]
        rowi = lax.broadcasted_iota(jnp.int32, (C, HG_DK), 0)
        db = db + jnp.where(rowi == C // 2 - 1, dm, 0.0) + jnp.where(rowi == C - 1, dbl, 0.0)
        dlf = _dot(tri.astype(F32), db, TN, precision=HIGHEST)
        df = dlf / t["f"] - dk
        sg = t["sg"]
        df_ref[rows, :] = (df * (1.0 - lb) * sg * (1.0 - sg)).astype(df_ref.dtype)
        sq = jax.nn.sigmoid(hq)
        dq_ref[rows, :] = (dq * (sq * (1.0 + hq * (1.0 - sq)))).astype(dq_ref.dtype)
        di_ref[rows, :] = dv.astype(di_ref.dtype)
        dg_ref[rows, :] = d_hgate.astype(dg_ref.dtype)
        return dst_prev, jnp.sum(df * (1.0 - sg), axis=0, keepdims=True), d_gout

    blk = lambda s: pl.BlockSpec((T, HG_DK), functools.partial(lambda h, n, s: (N - 1 - n, s * H + h), s=s))
    oblk = pl.BlockSpec((T, HG_DK), lambda h, n: (N - 1 - n, h))
    vec = pl.BlockSpec((1, HG_DK), lambda h, n: (0, h))
    W = H * HG_DK
    return _call(
        body, [p4, p4, p4, p4, lb_logits, gout, s_all, d_out], name="hgrn_bwd", grid=(H, N),
        out_shape=tuple([jax.ShapeDtypeStruct((L, W), BF16)] * 4 + [jax.ShapeDtypeStruct((1, W), F32), jax.ShapeDtypeStruct((1, HG_DK), F32)]),
        in_specs=[blk(0), blk(1), blk(2), blk(3), pl.BlockSpec((2, HG_DK), lambda h, n: (0, h)),
                  pl.BlockSpec((1, HG_DK), lambda h, n: (0, 0)),
                  pl.BlockSpec((1, GR, HG_DK, HG_DK), lambda h, n: (h, N - 1 - n, 0, 0)), oblk],
        out_specs=(oblk, oblk, oblk, oblk, vec, pl.BlockSpec((1, HG_DK), lambda h, n: (0, 0))),
        scratch_shapes=[pltpu.VMEM((HG_DK, HG_DK), F32)], comm=comm)


def _bucket_ids():
    i = jnp.arange(AT_BLOCK, dtype=jnp.int32)[:, None]
    j = jnp.arange(2 * AT_BLOCK, dtype=jnp.int32)[None, :]
    n = jnp.maximum(i - j + AT_BLOCK, 0)
    nf = jnp.maximum(n, 1).astype(F32)
    large = MAX_EXACT + (jnp.log(nf / MAX_EXACT) / math.log(MAX_DISTANCE / MAX_EXACT) * (N_BUCKETS - MAX_EXACT)).astype(jnp.int32)
    large = jnp.minimum(large, N_BUCKETS - 1)
    return jnp.where(n < MAX_EXACT, n, large).reshape(1, -1)


def _onehot(bucket):
    ids = lax.broadcasted_iota(jnp.int32, (N_BUCKETS, bucket.shape[1]), 0)
    return (ids == bucket).astype(F32)


def _attn_probs(qn, kpn, kcn, bias_g, sink, first, scale):
    rows = qn.shape[0]
    i = jnp.bitwise_and(lax.broadcasted_iota(jnp.int32, (rows, AT_BLOCK), 0), AT_BLOCK - 1)
    j = lax.broadcasted_iota(jnp.int32, (rows, AT_BLOCK), 1)
    lp = _bdot(qn, kpn, NT) * scale + bias_g[:, :AT_BLOCK]
    lc = _bdot(qn, kcn, NT) * scale + bias_g[:, AT_BLOCK:]
    lp = jnp.where(jnp.logical_and(j > i, jnp.logical_not(first)), lp, NEG_INF)
    lc = jnp.where(j <= i, lc, NEG_INF)
    m = jnp.maximum(jnp.maximum(jnp.max(lp, axis=-1, keepdims=True), jnp.max(lc, axis=-1, keepdims=True)), sink)
    pp, pc, ps = jnp.exp(lp - m), jnp.exp(lc - m), jnp.exp(sink - m)
    den = jnp.sum(pp, axis=-1, keepdims=True) + jnp.sum(pc, axis=-1, keepdims=True) + ps
    return pp / den, pc / den, ps / den


def _sink_rows(sk_ref, G):
    head = lax.broadcasted_iota(jnp.int32, (G * AT_BLOCK, 1), 0) // AT_BLOCK
    sink = jnp.zeros((G * AT_BLOCK, 1), F32)
    for g in range(G):
        sink = jnp.where(head == g, sk_ref[0, g:g + 1, :], sink)
    return sink


def _attn_fwd(q_t, kp, vp, qg, kg, sinks, bias, KVH, comm=None):
    AH, L, DH = q_t.shape
    G = AH // KVH
    NB = L // AT_BLOCK
    scale = DH ** -0.5

    def body(q_ref, kp_ref, kc_ref, vp_ref, vc_ref, qg_ref, kg_ref, sk_ref, b_ref, o_ref):
        first = pl.program_id(1) == 0
        kpn, kcn = _rms(kp_ref[0], kg_ref[...]), _rms(kc_ref[0], kg_ref[...])
        qn = _rms(q_ref[...].reshape(G * AT_BLOCK, DH), qg_ref[...])
        sink = _sink_rows(sk_ref, G)
        pp, pc, _ = _attn_probs(qn, kpn, kcn, b_ref[...].reshape(G * AT_BLOCK, 2 * AT_BLOCK), sink, first, scale)
        o = _bdot(pp, vp_ref[0], NN) + _bdot(pc, vc_ref[0], NN)
        o_ref[...] = o.reshape(G, AT_BLOCK, DH).astype(o_ref.dtype)

    kblk = lambda off: pl.BlockSpec((1, AT_BLOCK, DH), functools.partial(lambda h, n, off: (h, n + off, 0), off=off))
    return _call(
        body, [q_t, kp, kp, vp, vp, qg, kg, sinks, bias], name="attn_fwd", grid=(KVH, NB),
        out_shape=jax.ShapeDtypeStruct((AH, L, DH), BF16),
        in_specs=[pl.BlockSpec((G, AT_BLOCK, DH), lambda h, n: (h, n, 0)), kblk(0), kblk(1), kblk(0), kblk(1),
                  pl.BlockSpec((1, DH), lambda h, n: (0, 0)), pl.BlockSpec((1, DH), lambda h, n: (0, 0)),
                  pl.BlockSpec((1, G, 1), lambda h, n: (h, 0, 0)),
                  pl.BlockSpec((G, AT_BLOCK, 2 * AT_BLOCK), lambda h, n: (h, 0, 0))],
        out_specs=pl.BlockSpec((G, AT_BLOCK, DH), lambda h, n: (h, n, 0)), comm=comm)


def _attn_bwd(q_t, kp, vp, qg, kg, sinks, bias, do_t, KVH, comm=None):
    AH, L, DH = q_t.shape
    G = AH // KVH
    NB = L // AT_BLOCK
    B = AT_BLOCK
    scale = DH ** -0.5

    def body(q_ref, kp_ref, kc_ref, vp_ref, vc_ref, qg_ref, kg_ref, sk_ref, b_ref, do_ref,
             dq_ref, dk_ref, dv_ref, dqg_ref, dkg_ref, dsk_ref, db_ref):
        n = pl.program_id(1)
        first = n == 0

        @pl.when(first)
        def _():
            for r in (dk_ref, dv_ref, dsk_ref, db_ref):
                r[...] = jnp.zeros_like(r)

        @pl.when(jnp.logical_and(first, pl.program_id(0) == 0))
        def _():
            dqg_ref[...] = jnp.zeros_like(dqg_ref)
            dkg_ref[...] = jnp.zeros_like(dkg_ref)

        kp_raw, kc_raw, kgv, qgv = kp_ref[0], kc_ref[0], kg_ref[...], qg_ref[...]
        kpn, kp_vjp = jax.vjp(_rms, kp_raw, kgv)
        kcn, kc_vjp = jax.vjp(_rms, kc_raw, kgv)
        qn, q_vjp = jax.vjp(_rms, q_ref[...].reshape(G * B, DH), qgv)
        pp, pc, ps = _attn_probs(qn, kpn, kcn, b_ref[...].reshape(G * B, 2 * B), _sink_rows(sk_ref, G), first, scale)
        do = do_ref[...].reshape(G * B, DH)
        dvp = _bdot(pp, do, TN)
        dvc = _bdot(pc, do, TN)
        dpp = _bdot(do, vp_ref[0], NT)
        dpc = _bdot(do, vc_ref[0], NT)
        dsum = jnp.sum(dpp * pp, axis=-1, keepdims=True) + jnp.sum(dpc * pc, axis=-1, keepdims=True)
        dlp = pp * (dpp - dsum)
        dlc = pc * (dpc - dsum)
        dsk_ref[0] += jnp.sum((-ps * dsum).reshape(G, B, 1), axis=1)
        db_ref[:, :, :B] += dlp.reshape(G, B, B)
        db_ref[:, :, B:] += dlc.reshape(G, B, B)
        dlp, dlc = dlp * scale, dlc * scale
        dqn = _bdot(dlp, kpn, NN) + _bdot(dlc, kcn, NN)
        dq_raw, dqg = q_vjp(dqn)
        dq_ref[...] = dq_raw.reshape(G, B, DH).astype(dq_ref.dtype)
        dkp_raw, dkg_p = kp_vjp(_bdot(dlp, qn, TN))
        dkc_raw, dkg_c = kc_vjp(_bdot(dlc, qn, TN))
        r0 = pl.multiple_of(n * B, B)
        r1 = pl.multiple_of(n * B + B, B)
        dk_ref[0, pl.ds(r0, B), :] += dkp_raw
        dk_ref[0, pl.ds(r1, B), :] += dkc_raw
        dv_ref[0, pl.ds(r0, B), :] += dvp
        dv_ref[0, pl.ds(r1, B), :] += dvc
        dqg_ref[...] += dqg
        dkg_ref[...] += dkg_p + dkg_c

    kblk = lambda off: pl.BlockSpec((1, B, DH), functools.partial(lambda h, n, off: (h, n + off, 0), off=off))
    qblk = pl.BlockSpec((G, B, DH), lambda h, n: (h, n, 0))
    accblk = pl.BlockSpec((1, L + B, DH), lambda h, n: (h, 0, 0))
    vecblk = pl.BlockSpec((1, DH), lambda h, n: (0, 0))
    return _call(
        body, [q_t, kp, kp, vp, vp, qg, kg, sinks, bias, do_t], name="attn_bwd", grid=(KVH, NB),
        out_shape=(jax.ShapeDtypeStruct((AH, L, DH), BF16), jax.ShapeDtypeStruct((KVH, L + B, DH), F32),
                   jax.ShapeDtypeStruct((KVH, L + B, DH), F32), jax.ShapeDtypeStruct((1, DH), F32),
                   jax.ShapeDtypeStruct((1, DH), F32), jax.ShapeDtypeStruct((KVH, G, 1), F32),
                   jax.ShapeDtypeStruct((AH, B, 2 * B), F32)),
        in_specs=[qblk, kblk(0), kblk(1), kblk(0), kblk(1),
                  pl.BlockSpec((1, DH), lambda h, n: (0, 0)), pl.BlockSpec((1, DH), lambda h, n: (0, 0)),
                  pl.BlockSpec((1, G, 1), lambda h, n: (h, 0, 0)),
                  pl.BlockSpec((G, B, 2 * B), lambda h, n: (h, 0, 0)), qblk],
        out_specs=(qblk, accblk, accblk, vecblk, vecblk, pl.BlockSpec((1, G, 1), lambda h, n: (h, 0, 0)),
                   pl.BlockSpec((G, B, 2 * B), lambda h, n: (h, 0, 0))), comm=comm)


def _heads_first(t, nh):
    L = t.shape[0]
    return jnp.transpose(t.reshape(L, nh, t.shape[1] // nh), (1, 0, 2))


def _heads_last(t):
    nh, L, dh = t.shape
    return jnp.transpose(t, (1, 0, 2)).reshape(L, nh * dh)


def _softmax0(lg):
    e = jnp.exp(lg - jnp.max(lg, axis=0, keepdims=True))
    return e[0:1] / jnp.sum(e, axis=0, keepdims=True)


def _ada_update_call(fn, c_all, d_cols, w, m, v, rt):
    D, n = w.shape

    def body(c_ref, d_ref, w_ref, m_ref, v_ref, g_out, dl_out, m_out, v_out):
        outs, _ = fn(c_ref[...], d_ref[...], w_ref[...], m_ref[...], v_ref[...])
        for r, val in zip((g_out, dl_out, m_out, v_out), outs):
            r[...] = val

    wblk = pl.BlockSpec((rt, n), lambda i: (i, 0))
    return _call(
        body, [c_all, d_cols, w, m, v], name="update_ada", grid=(D // rt,), out_shape=tuple([jax.ShapeDtypeStruct((D, n), F32)] * 4),
        in_specs=[pl.BlockSpec((N_DEV, rt), lambda i: (0, i)), pl.BlockSpec((N_DEV, n), lambda i: (0, 0)), wblk, wblk, wblk],
        out_specs=(wblk, wblk, wblk, wblk))


def kernel(x, c, w_ada, b_ada, norm1_g, norm2_g, w_in, hg_lb_logits, hg_out_norm_g, q_norm_g, k_norm_g, attn_sinks, rel_bias_table, w_branch_hg, w_branch_attn, w_out, w_ff1, w_ff2, loss_target, m_w_ada, m_b_ada, m_norm1_g, m_norm2_g, m_w_in, m_hg_lb_logits, m_hg_out_norm_g, m_q_norm_g, m_k_norm_g, m_attn_sinks, m_rel_bias_table, m_w_branch_hg, m_w_branch_attn, m_w_out, m_w_ff1, m_w_ff2, v_w_ada, v_b_ada, v_norm1_g, v_norm2_g, v_w_in, v_hg_lb_logits, v_hg_out_norm_g, v_q_norm_g, v_k_norm_g, v_attn_sinks, v_rel_bias_table, v_w_branch_hg, v_w_branch_attn, v_w_out, v_w_ff1, v_w_ff2):
    cc = lax.axis_index("c")
    me = 4 * lax.axis_index("x") + 2 * lax.axis_index("y") + cc
    x2 = x[0]
    tgt = loss_target[0]
    L, D = x2.shape
    HGW = hg_lb_logits.shape[1]
    H = HGW // HG_DK
    AH = attn_sinks.shape[1]
    DH = q_norm_g.shape[1]
    ATW = AH * DH
    BW = w_in.shape[2]
    INW = BW * N_DEV
    A = BW // LANES
    assert BW == LANES * A + LANES // 2
    KVW = (INW - 4 * HGW - ATW - 2 * D) // 2
    KVH = KVW // DH
    G = AH // KVH
    ADA_N = w_ada.shape[2]
    PAIR = 2 * A + 1

    w_in_b = w_in[0].astype(BF16)
    src_in = jnp.where(cc == 0, jnp.pad(w_in_b, ((0, 0), (0, LANES // 2))), jnp.pad(w_in_b, ((0, 0), (LANES // 2, 0))))
    w_in_gapped, w_in_mid = _ag_w_in(src_in, A, D, INW)
    w_in_full = _patch_mid(w_in_gapped, w_in_mid, A)

    c_all = _gather_small(c, me, "gather_c")[:, 0, :]
    b_cols = lax.dynamic_slice(b_ada, (0, me * ADA_N), (1, ADA_N))
    (ada_cols,) = _whole(lambda cv, w, b: (_bdot(_silu(cv), w, NN) + b,), [c_all, w_ada[0], b_cols],
                         [((N_DEV, ADA_N), F32)], "ada_fwd")
    ada_all = _gather_small(ada_cols, me, "gather_ada")
    ada_row = lax.dynamic_slice(ada_all, (0, me, 0), (N_DEV, 1, ADA_N)).reshape(1, 6 * D)
    shift1, scale1, gate1, shift2, scale2, gate2 = [ada_row[:, i * D:(i + 1) * D] for i in range(6)]

    wnames = ("bhg", "bat", "out", "ff1", "ff2")
    waxis = dict(zip(wnames, (1, 1, 0, 1, 0)))
    wsrc = dict(zip(wnames, (w_branch_hg, w_branch_attn, w_out, w_ff1, w_ff2)))
    wblk = {k: wsrc[k][0].astype(BF16) for k in wnames}
    wf = {}

    (h,) = _rowwise(lambda xv, g, sh, sc: ((_modnorm(xv, g, sh, sc),), ()), [(x2, D, 0)], [norm1_g, shift1, scale1],
                    [(D, BF16)], [], "norm1")
    o4, oa = 4 * HGW, 4 * HGW + ATW + 2 * KVW
    r1, r2 = wblk["ff1"].shape[0], wblk["ff2"].shape[0]
    cm = _Comm()
    hs = {k: _ag_ici(cm, wblk[k], waxis[k]) for k in ("bhg", "bat")}
    p4 = _mm(h, w_in_full, "nn", F32, "proj_hg", n=o4, comm=cm)
    half = {k: cm.result(hs[k]) for k in hs}
    pa = _mm(h, w_in_full, "nn", F32, "proj_at", b_off=o4, n=oa - o4)
    cm = _Comm()
    hs = {k: _ag_d2d(cm, half[k], waxis[k]) for k in ("bhg", "bat")}
    hs["out"] = _ag_ici(cm, wblk["out"], waxis["out"])
    hs["ff2"] = _ag_ici(cm, wblk["ff2"], waxis["ff2"], rows=(0, r2 // 4))
    pg = _mm(h, w_in_full, "nn", F32, "proj_gate", b_off=oa, n=INW - oa, comm=cm)
    wf["bhg"], wf["bat"], half["out"], half["ff2"] = (cm.result(hs[k]) for k in ("bhg", "bat", "out", "ff2"))

    cm = _Comm()
    hs = {"out": _ag_d2d(cm, half["out"], waxis["out"]), "ff1": _ag_ici(cm, wblk["ff1"], waxis["ff1"], rows=(0, r1 // 2))}
    o_hg, s_all = _hgrn_fwd(p4, hg_lb_logits, hg_out_norm_g, H, comm=cm)
    wf["out"], half["ff1"] = cm.result(hs["out"]), cm.result(hs["ff1"])

    bucket = _bucket_ids()
    (bias_flat,) = _whole(lambda tb, bk: (_dot(tb, _onehot(bk), TN, precision=HIGHEST),), [rel_bias_table, bucket],
                          [((AH, AT_BLOCK * 2 * AT_BLOCK), F32)], "bias_fwd")
    bias = bias_flat.reshape(AH, AT_BLOCK, 2 * AT_BLOCK)
    q_t = _heads_first(pa[:, :ATW], AH)
    pad = lambda t: jnp.pad(t, ((0, 0), (AT_BLOCK, 0), (0, 0)))
    kp = pad(_heads_first(pa[:, ATW:ATW + KVW], KVH))
    vp = pad(_heads_first(pa[:, ATW + KVW:], KVH))
    sinks3 = attn_sinks.reshape(KVH, G, 1)
    cm = _Comm()
    hs = {"ff1": _ag_ici(cm, wblk["ff1"], waxis["ff1"], rows=(r1 // 2, r1), into=half["ff1"])}
    o_at = _heads_last(_attn_fwd(q_t, kp, vp, q_norm_g, k_norm_g, sinks3, bias, KVH, comm=cm))
    half["ff1"] = cm.result(hs["ff1"])

    cm = _Comm()
    hs = {"ff1": _ag_d2d(cm, half["ff1"], waxis["ff1"])}
    bh = _mm(o_hg, wf["bhg"], "nn", F32, "branch_hg", comm=cm)
    wf["ff1"] = cm.result(hs["ff1"])
    ba = _mm(o_at, wf["bat"], "nn", F32, "branch_at")

    def merge_fn(bhv, bav, ghg, gat):
        return jax.nn.sigmoid(ghg) * bhv + jax.nn.sigmoid(gat) * bav

    (merged,) = _rowwise(lambda *a: ((merge_fn(*a),), ()), [(bh, D, 0), (ba, D, 0), (pg, D, 0), (pg, D, 1)], [],
                         [(D, BF16)], [], "merge")
    cm = _Comm()
    hs = {"ff2": _ag_ici(cm, wblk["ff2"], waxis["ff2"], rows=(r2 // 4, r2 // 2), into=half["ff2"])}
    mo = _mm(merged, wf["out"], "nn", F32, "out_proj", comm=cm)
    half["ff2"] = cm.result(hs["ff2"])

    def resid1(xv, mov, g1, g2n, sh, sc):
        x1v = xv + g1 * mov
        return (x1v, _modnorm(x1v, g2n, sh, sc)), ()

    x1, h2 = _rowwise(resid1, [(x2, D, 0), (mo, D, 0)], [gate1, norm2_g, shift2, scale2], [(D, F32), (D, BF16)], [], "resid1")
    cm = _Comm()
    hs = {"ff2": _ag_ici(cm, wblk["ff2"], waxis["ff2"], rows=(r2 // 2, r2), into=half["ff2"])}
    u = _mm(h2, wf["ff1"], "nn", F32, "ff1", comm=cm)
    half["ff2"] = cm.result(hs["ff2"])
    DFF = u.shape[1]
    cm = _Comm()
    hs = {"ff2": _ag_d2d(cm, half["ff2"], waxis["ff2"])}
    (act,) = _rowwise(lambda uv: ((jnp.square(jnp.maximum(uv, 0.0)),), ()), [(u, DFF, 0)], [], [(DFF, BF16)], [], "relu2", comm=cm)
    wf["ff2"] = cm.result(hs["ff2"])
    ff = _mm(act, wf["ff2"], "nn", F32, "ff2")

    def loss_fn(x1v, ffv, tv, g2):
        e = x1v + g2 * ffv - tv
        dy = e * (1.0 / D)
        return (dy, dy * g2), (jnp.sum(e * e, axis=0, keepdims=True), jnp.sum(dy * ffv, axis=0, keepdims=True))

    dy, d_ff, sq_sum, d_gate2 = _rowwise(loss_fn, [(x1, D, 0), (ff, D, 0), (tgt, D, 0)], [gate2],
                                         [(D, F32), (D, BF16)], [(1, D), (1, D)], "loss")
    loss = lax.psum(jnp.sum(sq_sum) * (0.5 / D), ("x", "y", "c"))

    owner_base = jnp.stack([me ^ r for r in CHIP_RELS]).astype(jnp.int32)
    gw, recv1, part, recv2 = {}, {}, {}, {}
    gw["ff2"] = _mm(act, d_ff, "tn", BF16, "dw_ff2")
    cm = _Comm()
    hh = _rs_d2d(cm, gw["ff2"], waxis["ff2"])
    d_act = _mm(d_ff, wf["ff2"], "nt", F32, "d_act", comm=cm)
    part["ff2"] = _rs_add(gw["ff2"], cm.result(hh), waxis["ff2"], owner_base, "rs_add_ff2")
    (d_u,) = _rowwise(lambda dav, uv: ((dav * (2.0 * jnp.maximum(uv, 0.0)),), ()), [(d_act, DFF, 0), (u, DFF, 0)], [],
                      [(DFF, BF16)], [], "relu2_bwd")
    rows_ff2 = part["ff2"].shape[1]
    cm = _Comm()
    hh = _rs_ici(cm, part["ff2"], rows=(0, rows_ff2 // 2))
    gw["ff1"] = _mm(h2, d_u, "tn", BF16, "dw_ff1", comm=cm)
    cm2 = _Comm()
    hh2 = _rs_ici(cm2, part["ff2"], rows=(rows_ff2 // 2, rows_ff2), recv=cm.result(hh))
    hh1 = _rs_d2d(cm2, gw["ff1"], waxis["ff1"])
    d_h2 = _mm(d_u, wf["ff1"], "nt", F32, "d_h2", comm=cm2)
    recv2["ff2"] = cm2.result(hh2)
    part["ff1"] = _rs_add(gw["ff1"], cm2.result(hh1), waxis["ff1"], owner_base, "rs_add_ff1")

    def norm2_bwd(dh2v, x1v, dyv, mov, g2n, sh, sc, g1):
        _, vjp = jax.vjp(_modnorm, x1v, g2n, sh, sc)
        dx, dg, dsh, dsc = vjp(dh2v)
        dx1 = dyv + dx
        return (dx1, dx1 * g1), (dg, dsh, dsc, jnp.sum(dx1 * mov, axis=0, keepdims=True))

    d_x1, d_mo, d_g2n, d_shift2, d_scale2, d_gate1 = _rowwise(
        norm2_bwd, [(d_h2, D, 0), (x1, D, 0), (dy, D, 0), (mo, D, 0)], [norm2_g, shift2, scale2, gate1],
        [(D, F32), (D, BF16)], [(1, D)] * 4, "norm2_bwd")
    gw["out"] = _mm(merged, d_mo, "tn", BF16, "dw_out")
    cm = _Comm()
    hh = _rs_d2d(cm, gw["out"], waxis["out"])
    d_merged = _mm(d_mo, wf["out"], "nt", F32, "d_merged", comm=cm)
    part["out"] = _rs_add(gw["out"], cm.result(hh), waxis["out"], owner_base, "rs_add_out")

    def merge_bwd(dmv, bhv, bav, ghg, gat):
        _, vjp = jax.vjp(merge_fn, bhv, bav, ghg, gat)
        return vjp(dmv), ()

    d_bh, d_ba, d_ghg, d_gat = _rowwise(merge_bwd, [(d_merged, D, 0), (bh, D, 0), (ba, D, 0), (pg, D, 0), (pg, D, 1)], [],
                                        [(D, BF16)] * 4, [], "merge_bwd")
    gw["bhg"] = _mm(o_hg, d_bh, "tn", BF16, "dw_bhg")
    gw["bat"] = _mm(o_at, d_ba, "tn", BF16, "dw_bat")
    cm = _Comm()
    hh = {k: _rs_d2d(cm, gw[k], waxis[k]) for k in ("bhg", "bat")}
    d_ohg = _mm(d_bh, wf["bhg"], "nt", F32, "d_ohg", comm=cm)
    for k in ("bhg", "bat"):
        part[k] = _rs_add(gw[k], cm.result(hh[k]), waxis[k], owner_base, "rs_add_" + k)
    d_oat = _mm(d_ba, wf["bat"], "nt", BF16, "d_oat")

    cm = _Comm()
    hh = {"ff1": _rs_ici(cm, part["ff1"])}
    d_hq, d_hf, d_hi, d_hg, d_lb, d_gout_h = _hgrn_bwd(p4, hg_lb_logits, hg_out_norm_g, s_all, d_ohg, H, comm=cm)
    recv2["ff1"] = cm.result(hh["ff1"])
    cm = _Comm()
    hh = {k: _rs_ici(cm, part[k]) for k in ("out", "bhg", "bat")}
    dq_t, dkp, dvp, d_qg, d_kg, d_sk, d_bias = _attn_bwd(q_t, kp, vp, q_norm_g, k_norm_g, sinks3, bias,
                                                         _heads_first(d_oat, AH), KVH, comm=cm)
    for k in hh:
        recv2[k] = cm.result(hh[k])
    d_aq = _heads_last(dq_t)
    d_ak = _heads_last(dkp[:, AT_BLOCK:, :]).astype(BF16)
    d_av = _heads_last(dvp[:, AT_BLOCK:, :]).astype(BF16)
    d_proj = jnp.concatenate([d_hq, d_hf, d_hi, d_hg, d_aq, d_ak, d_av, d_ghg, d_gat], axis=1)
    gw_in = _mm(h, d_proj, "tn", BF16, "dw_in")

    wm = LANES * A
    cm = _Comm()
    hi_ = cm.inp(gw_in)
    h_main, h_mid = cm.out((4, D, wm), BF16), cm.out((4, D, LANES), BF16)
    for i, r in enumerate(CHIP_RELS):
        def main_view(ref, p, r=r):
            o = p["me"] ^ r ^ 1
            return ref.at[:, pl.ds(pl.multiple_of((PAIR * (o // 2) + (A + 1) * (1 - p["c"])) * LANES, LANES), wm)]

        def mid_view(ref, p, r=r):
            o = p["me"] ^ r
            return ref.at[:, pl.ds(pl.multiple_of((PAIR * (o // 2) + A) * LANES, LANES), LANES)]

        cm.copy(hi_, main_view, h_main, _slot_view(i), 1)
        cm.copy(hi_, mid_view, h_mid, _slot_view(i), 1)
    _call(lambda: None, [], name="rs_d2d_in", out_shape=(), comm=cm)
    chip = jnp.stack([(me ^ r) // 2 for r in CHIP_RELS]).astype(jnp.int32)
    part_main = _rs_add(gw_in, cm.result(h_main), 1, PAIR * chip + (A + 1) * cc, "rs_add_in_main", tw=LANES)
    part_mid = _rs_add(gw_in, cm.result(h_mid), 1, PAIR * chip + A, "rs_add_in_mid", tw=LANES)
    cm = _Comm()
    hh_main, hh_mid = _rs_ici(cm, part_main), _rs_ici(cm, part_mid)
    d_h = _mm(d_proj, w_in_full, "nt", F32, "d_h", comm=cm)
    rx_main, rx_mid = cm.result(hh_main), cm.result(hh_mid)

    def norm1_bwd(dhv, xv, dx1v, g1n, sh, sc):
        _, vjp = jax.vjp(_modnorm, xv, g1n, sh, sc)
        dx, dg, dsh, dsc = vjp(dhv)
        return (dx1v + dx,), (dg, dsh, dsc)

    grad_x, d_g1n, d_shift1, d_scale1 = _rowwise(norm1_bwd, [(d_h, D, 0), (x2, D, 0), (d_x1, D, 0)],
                                                 [norm1_g, shift1, scale1], [(D, F32)], [(1, D)] * 3, "norm1_bwd")

    def sum4(p0, p1, p2, p3):
        return ((p0.astype(F32) + p1.astype(F32)) + p2.astype(F32)) + p3.astype(F32)

    def update_fn(w, m, v, p0, p1, p2, p3):
        g = sum4(p0, p1, p2, p3)
        delta, mn, vn = _adamw(w, g, m, v)
        return (g, delta, mn, vn), ()

    wmv = dict(zip(wnames, ((w_branch_hg, m_w_branch_hg, v_w_branch_hg), (w_branch_attn, m_w_branch_attn, v_w_branch_attn),
                            (w_out, m_w_out, v_w_out), (w_ff1, m_w_ff1, v_w_ff1), (w_ff2, m_w_ff2, v_w_ff2))))
    res = {}
    for k in wnames:
        w, m, v = (t[0] for t in wmv[k])
        n = w.shape[1]
        ins = [(t, n, 0) for t in (w, m, v)] + [(part[k], n, 0, 0)] + [(recv2[k], n, 0, i) for i in range(3)]
        res[k] = [t[None] for t in _rowwise(update_fn, ins, [], [(n, F32)] * 4, [], "update_" + k)]

    g_main, = _rowwise(lambda *p: ((sum4(*p),), ()), [(part_main, wm, 0, 0)] + [(rx_main, wm, 0, i) for i in range(3)], [],
                       [(wm, F32)], [], "sum_in_main")
    g_mid, = _rowwise(lambda *p: ((sum4(*p),), ()), [(part_mid, LANES, 0, 0)] + [(rx_mid, LANES, 0, i) for i in range(3)], [],
                      [(LANES, F32)], [], "sum_in_mid")
    g_in = jnp.where(cc == 0, jnp.concatenate([g_main, g_mid[:, :LANES // 2]], axis=1),
                     jnp.concatenate([g_mid[:, LANES // 2:], g_main], axis=1))

    def update_given(w, m, v, g):
        delta, mn, vn = _adamw(w, g, m, v)
        return (g, delta, mn, vn), ()

    res["in"] = [t[None] for t in _rowwise(update_given, [(t, BW, 0) for t in (w_in[0], m_w_in[0], v_w_in[0], g_in)], [],
                                           [(BW, F32)] * 4, [], "update_in")]

    d_ada_row = jnp.concatenate([d_shift1, d_scale1, d_gate1, d_shift2, d_scale2, d_gate2], axis=1)
    d_ada_all = _gather_small(d_ada_row, me, "gather_dada")[:, 0, :]
    d_ada_cols = lax.dynamic_slice(d_ada_all, (0, me * ADA_N), (N_DEV, ADA_N))

    def ada_update(cv, dav, w, m, v):
        g = _bdot(_silu(cv), dav, TN)
        delta, mn, vn = _adamw(w, g, m, v)
        return (g, delta, mn, vn), ()

    res["ada"] = [t[None] for t in _ada_update_call(ada_update, c_all, d_ada_cols, w_ada[0], m_w_ada[0], v_w_ada[0], _tile(D, 256, 16))]

    d_sinks = d_sk.reshape(1, AH)
    (d_table_t,) = _whole(lambda db, bk: (_dot(db, _onehot(bk), NT, precision=HIGHEST),),
                          [d_bias.reshape(AH, AT_BLOCK * 2 * AT_BLOCK), bucket], [((AH, N_BUCKETS), F32)], "bias_bwd")
    smalls = [d_g1n, d_g2n, d_lb, d_gout_h, d_qg, d_kg, d_sinks, d_table_t.T.reshape(1, N_BUCKETS * AH)]
    widths = [s.shape[1] for s in smalls]
    lanes = [-(-w // LANES) * LANES for w in widths]
    smalls = [jnp.pad(s, ((0, 0), (0, p - w))) for s, w, p in zip(smalls, widths, lanes)]
    packed = _gather_small(jnp.concatenate(smalls, axis=1), me, "gather_small")[:, 0, :]
    offs = [sum(lanes[:i]) for i in range(len(lanes))]

    def small_update(pk, dada, lg, *wmv_flat):
        tot = pk[0:1]
        for d in range(1, N_DEV):
            tot = tot + pk[d:d + 1]
        gb = dada[0:1]
        for d in range(1, N_DEV):
            gb = gb + dada[d:d + 1]
        gs = [tot[:, offs[i]:offs[i] + widths[i]] for i in range(len(widths))]
        _, lb_vjp = jax.vjp(_softmax0, lg)
        (g_lg,) = lb_vjp(gs[2])
        grads = [gb, gs[0], gs[1], g_lg, gs[3], gs[4], gs[5], gs[6], gs[7]]
        outs = []
        for i, g in enumerate(grads):
            w, m, v = wmv_flat[3 * i:3 * i + 3]
            delta, mn, vn = _adamw(w, g, m, v)
            outs += [g, delta, mn, vn]
        return tuple(outs)

    tbl = lambda t: t.reshape(1, N_BUCKETS * AH)
    small_wmv = [(b_ada, m_b_ada, v_b_ada), (norm1_g, m_norm1_g, v_norm1_g), (norm2_g, m_norm2_g, v_norm2_g),
                 (hg_lb_logits, m_hg_lb_logits, v_hg_lb_logits), (hg_out_norm_g, m_hg_out_norm_g, v_hg_out_norm_g),
                 (q_norm_g, m_q_norm_g, v_q_norm_g), (k_norm_g, m_k_norm_g, v_k_norm_g),
                 (attn_sinks, m_attn_sinks, v_attn_sinks),
                 (tbl(rel_bias_table), tbl(m_rel_bias_table), tbl(v_rel_bias_table))]
    flat = [t for trip in small_wmv for t in trip]
    out_shapes = [(trip[0].shape, F32) for trip in small_wmv for _ in range(4)]
    sres = _whole(small_update, [packed, d_ada_all, hg_lb_logits] + flat, out_shapes, "small_update")
    names_small = ("b_ada", "norm1_g", "norm2_g", "lb", "gout", "qg", "kg", "sinks", "table")
    for i, k in enumerate(names_small):
        r = sres[4 * i:4 * i + 4]
        if k == "table":
            r = [t.reshape(N_BUCKETS, AH) for t in r]
        res[k] = r

    order = ("ada", "b_ada", "norm1_g", "norm2_g", "in", "lb", "gout", "qg", "kg", "sinks", "table", "bhg", "bat", "out", "ff1", "ff2")
    outs = [loss, grad_x[None]]
    for j in range(4):
        outs += [res[k][j] for k in order]
    return tuple(outs)
```

```python
import functools
import math

import jax
import jax.numpy as jnp
from jax import lax
from jax.experimental import pallas as pl
from jax.experimental.pallas import tpu as pltpu

F32 = jnp.float32
BF16 = jnp.bfloat16
EPS = 1e-6
NEG_INF = -1e30
HG_DK = 128
HG_CHUNK = 64
AT_BLOCK = 128
N_BUCKETS = 32
MAX_EXACT = 16
MAX_DISTANCE = 128
N_DEV = 8
LANES = 128
VMEM_LIMIT = 56 * 1024 * 1024
ADAM_LR, ADAM_B1, ADAM_B2, ADAM_EPS, ADAM_WD, ADAM_STEP = 0.001, 0.9, 0.999, 1e-08, 0.01, 10
HIGHEST = lax.Precision.HIGHEST
MESH = pl.DeviceIdType.MESH
ANY = pl.BlockSpec(memory_space=pl.ANY)
CHIP_RELS = (0, 4, 2, 6)

NN = (((1,), (0,)), ((), ()))
NT = (((1,), (1,)), ((), ()))
TN = (((0,), (0,)), ((), ()))


def _tile(n, pref, unit):
    if n <= pref:
        return n
    t = (pref // unit) * unit
    while t >= unit:
        if n % t == 0:
            return t
        t -= unit
    return n


def _dot(a, b, dn, precision=None):
    return lax.dot_general(a, b, dn, preferred_element_type=F32, precision=precision)


def _bdot(a, b, dn):
    return _dot(a.astype(BF16), b.astype(BF16), dn)


def _position():
    x, y, c = lax.axis_index("x"), lax.axis_index("y"), lax.axis_index("c")
    return dict(x=x, y=y, c=c, me=4 * x + 2 * y + c)


def _peer_position(p, rel):
    x = 1 - p["x"] if rel & 4 else p["x"]
    y = 1 - p["y"] if rel & 2 else p["y"]
    c = 1 - p["c"] if rel & 1 else p["c"]
    return dict(x=x, y=y, c=c, me=4 * x + 2 * y + c)


class _Comm:
    def __init__(self):
        self.ins, self.outs, self.alias, self.plans, self.res = [], [], {}, [], None

    def inp(self, arr):
        self.ins.append(arr)
        return ("i", len(self.ins) - 1)

    def out(self, shape, dtype, alias=None):
        self.outs.append(jax.ShapeDtypeStruct(tuple(shape), dtype))
        if alias is not None:
            self.alias[alias[1]] = len(self.outs) - 1
        return ("o", len(self.outs) - 1)

    def copy(self, src, src_view, dst, dst_view, rel):
        self.plans.append((src, src_view, dst, dst_view, rel))

    def result(self, handle):
        return self.res[handle[1]]

    def build(self, in_refs, out_refs, send_sems, recv_sems):
        pos = _position()
        ref = lambda h: in_refs[h[1]] if h[0] == "i" else out_refs[h[1]]
        ops = []
        for k, (src, sv, dst, dv, rel) in enumerate(self.plans):
            s = sv(ref(src), pos)
            if rel == 0:
                cp = pltpu.make_async_copy(s, dv(ref(dst), pos), send_sems.at[k])
                ops.append((cp.start, cp.wait))
                continue
            peer = _peer_position(pos, rel)
            mk = lambda d: pltpu.make_async_remote_copy(
                src_ref=s, dst_ref=d, send_sem=send_sems.at[k], recv_sem=recv_sems.at[k],
                device_id=(peer["x"], peer["y"], peer["c"]), device_id_type=MESH)
            out_cp, in_cp = mk(dv(ref(dst), pos)), mk(dv(ref(dst), peer))

            def wait(out_cp=out_cp, in_cp=in_cp):
                out_cp.wait_send()
                in_cp.wait_recv()

            ops.append((out_cp.start, wait))
        return ops


def _call(body, args, *, name, out_shape, in_specs=None, out_specs=None, grid=None, scratch_shapes=(), comm=None,
          prefetch=None, aliases=None):
    single = not isinstance(out_shape, (tuple, list))
    out_shape = (out_shape,) if single else tuple(out_shape)
    n_in, n_out, n_scr = len(args), len(out_shape), len(scratch_shapes)
    vm = pl.BlockSpec(memory_space=pltpu.VMEM)
    in_specs = [vm] * n_in if in_specs is None else list(in_specs)
    out_specs = [vm] * n_out if out_specs is None else (list(out_specs) if isinstance(out_specs, (tuple, list)) else [out_specs])
    n_pf = 0 if prefetch is None else len(prefetch)
    kw = {} if aliases is None else {"input_output_aliases": dict(aliases)}
    if comm is None:
        fn = body
        all_args, all_scratch = list(args), list(scratch_shapes)
    else:
        n_ci, n_co, n_x = len(comm.ins), len(comm.outs), len(comm.plans)

        def fn(*refs):
            pf, refs = refs[:n_pf], refs[n_pf:]
            o_in, c_in = refs[:n_in], refs[n_in:n_in + n_ci]
            o_out = refs[n_in + n_ci:n_in + n_ci + n_out]
            c_out = refs[n_in + n_ci + n_out:n_in + n_ci + n_out + n_co]
            scr = refs[n_in + n_ci + n_out + n_co:]
            ops = comm.build(c_in, c_out, scr[n_scr], scr[n_scr + 1])
            if grid:
                first = functools.reduce(jnp.logical_and, [pl.program_id(i) == 0 for i in range(len(grid))])
                last = functools.reduce(jnp.logical_and, [pl.program_id(i) == g - 1 for i, g in enumerate(grid)])

                @pl.when(first)
                def _():
                    for start, _w in ops:
                        start()
            else:
                for start, _w in ops:
                    start()
            body(*pf, *o_in, *o_out, *scr[:n_scr])
            if grid:
                @pl.when(last)
                def _():
                    for _s, wait in ops:
                        wait()
            else:
                for _s, wait in ops:
                    wait()

        all_args = list(args) + list(comm.ins)
        in_specs = in_specs + [ANY] * n_ci
        out_shape = out_shape + tuple(comm.outs)
        out_specs = out_specs + [ANY] * n_co
        all_scratch = list(scratch_shapes) + [pltpu.SemaphoreType.DMA((n_x,)), pltpu.SemaphoreType.DMA((n_x,))]
        kw["input_output_aliases"] = {n_pf + n_in + i: n_out + o for i, o in comm.alias.items()}
    sem = None if grid is None else ("arbitrary",) * len(grid)
    params = pltpu.CompilerParams(dimension_semantics=sem, vmem_limit_bytes=VMEM_LIMIT)
    if prefetch is None:
        spec = dict(in_specs=in_specs, out_specs=tuple(out_specs), scratch_shapes=all_scratch)
        if grid is not None:
            spec["grid"] = grid
    else:
        spec = dict(grid_spec=pltpu.PrefetchScalarGridSpec(
            num_scalar_prefetch=n_pf, grid=grid, in_specs=in_specs, out_specs=tuple(out_specs), scratch_shapes=all_scratch))
        all_args = list(prefetch) + all_args
    res = pl.pallas_call(fn, name=name, out_shape=out_shape, compiler_params=params, **spec, **kw)(*all_args)
    res = list(res)
    if comm is not None:
        comm.res = res[n_out:]
        res = res[:n_out]
    return res[0] if single else res


def _whole_view(ref, pos):
    return ref


def _block_view(axis, n, index, rows=None):
    def view(ref, pos):
        off = pl.multiple_of(index(pos) * n, n)
        if rows is None:
            return ref.at[:, pl.ds(off, n)] if axis == 1 else ref.at[pl.ds(off, n), :]
        lo, cnt = rows[0], rows[1] - rows[0]
        if axis == 1:
            return ref.at[pl.ds(lo, cnt), pl.ds(off, n)]
        return ref.at[pl.ds(pl.multiple_of(off + lo, 16), cnt), :]
    return view


def _rows_view(rows):
    def view(ref, pos):
        return ref if rows is None else ref.at[pl.ds(rows[0], rows[1] - rows[0]), :]
    return view


def _slot_view(i, rows=None):
    def view(ref, pos):
        return ref.at[i] if rows is None else ref.at[i, pl.ds(rows[0], rows[1] - rows[0]), :]
    return view


def _exchange(items, name):
    cm = _Comm()
    for a, rel in items:
        cm.copy(cm.inp(a), _whole_view, cm.out(a.shape, a.dtype), _whole_view, rel)
    _call(lambda: None, [], name=name, out_shape=(), comm=cm)
    return cm.res


def _gather_small(v, me, name):
    got = _exchange([(v, rel) for rel in range(1, N_DEV)], name)
    out = jnp.zeros((N_DEV,) + v.shape, v.dtype)
    out = lax.dynamic_update_slice(out, v[None], (me, 0, 0))
    for rel in range(1, N_DEV):
        out = lax.dynamic_update_slice(out, got[rel - 1][None], (me ^ rel, 0, 0))
    return out


def _ag_ici(cm, blk, axis, rows=None, into=None):
    n = blk.shape[axis]
    shape = list(blk.shape)
    shape[axis] = n * N_DEV
    hi = cm.inp(blk)
    ho = cm.out(shape, blk.dtype) if into is None else cm.out(shape, blk.dtype, alias=cm.inp(into))
    own = _block_view(axis, n, lambda p: p["me"], rows)
    for rel in CHIP_RELS:
        cm.copy(hi, _rows_view(rows), ho, own, rel)
    return ho


def _ag_d2d(cm, full, axis):
    n = full.shape[axis] // N_DEV
    hi = cm.inp(full)
    ho = cm.out(full.shape, full.dtype, alias=hi)
    for r in CHIP_RELS:
        v = _block_view(axis, n, functools.partial(lambda p, r: p["me"] ^ r, r=r))
        cm.copy(hi, v, ho, v, 1)
    return ho


def _rs_d2d(cm, gw, axis):
    n = gw.shape[axis] // N_DEV
    shape = list(gw.shape)
    shape[axis] = n
    hi, ho = cm.inp(gw), cm.out([4] + shape, gw.dtype)
    for i, r in enumerate(CHIP_RELS):
        cm.copy(hi, _block_view(axis, n, functools.partial(lambda p, r: p["me"] ^ r ^ 1, r=r)), ho, _slot_view(i), 1)
    return ho


def _rs_ici(cm, part, rows=None, recv=None):
    if recv is None:
        ho = cm.out((3,) + part.shape[1:], part.dtype)
    else:
        ho = cm.out(recv.shape, recv.dtype, alias=cm.inp(recv))
    hi = cm.inp(part)
    for i in (1, 2, 3):
        cm.copy(hi, _slot_view(i, rows), ho, _slot_view(i - 1, rows), CHIP_RELS[i])
    return ho


def _rs_add(gw, recv, axis, base, name, tw=None):
    _, R, n = recv.shape
    if axis == 1:
        tw = n if tw is None else tw
        gw_spec = pl.BlockSpec((R, tw), lambda i, t, b: (0, b[i] + t))
        rv_spec = pl.BlockSpec((None, R, tw), lambda i, t, b: (i, 0, t))
        grid = (4, n // tw)
    else:
        tw = _tile(n, 1024, LANES)
        gw_spec = pl.BlockSpec((R, tw), lambda i, t, b: (b[i], t))
        rv_spec = pl.BlockSpec((None, R, tw), lambda i, t, b: (i, 0, t))
        grid = (4, n // tw)

    def body(b_ref, g_ref, r_ref, o_ref):
        o_ref[...] = (g_ref[...].astype(F32) + r_ref[...].astype(F32)).astype(o_ref.dtype)

    return _call(body, [gw, recv], name=name, out_shape=jax.ShapeDtypeStruct(recv.shape, recv.dtype), grid=grid,
                 in_specs=[gw_spec, rv_spec], out_specs=rv_spec, prefetch=[base])


def _ag_w_in(src, a, D, INW):
    wm = LANES * a

    def main_place(ref, p):
        off = pl.multiple_of(((2 * a + 1) * (p["me"] // 2) + (a + 1) * p["c"]) * LANES, LANES)
        return ref.at[:, pl.ds(off, wm)]

    def main_src(ref, p):
        return ref.at[:, pl.ds(pl.multiple_of(p["c"] * LANES, LANES), wm)]

    def mid_src(ref, p):
        return ref.at[:, pl.ds(pl.multiple_of((1 - p["c"]) * wm, LANES), LANES)]

    def mid_place(ref, p):
        return ref.at[p["me"]]

    def body(src_ref, full_ref, mid_ref, send_sems, recv_sems):
        pos = _position()
        sib = _peer_position(pos, 1)

        def remote(k, s, d, to):
            return pltpu.make_async_remote_copy(src_ref=s, dst_ref=d, send_sem=send_sems.at[k], recv_sem=recv_sems.at[k],
                                                device_id=(to["x"], to["y"], to["c"]), device_id_type=MESH)

        local = [pltpu.make_async_copy(main_src(src_ref, pos), main_place(full_ref, pos), send_sems.at[16]),
                 pltpu.make_async_copy(mid_src(src_ref, pos), mid_place(mid_ref, pos), send_sems.at[17])]
        for cp in local:
            cp.start()
        sends = []
        for i, rel in enumerate(CHIP_RELS):
            to = sib if rel == 0 else _peer_position(pos, rel)
            sends.append(remote(2 * i, main_src(src_ref, pos), main_place(full_ref, pos), to))
            sends.append(remote(2 * i + 1, mid_src(src_ref, pos), mid_place(mid_ref, pos), to))
        for cp in sends:
            cp.start()
        for i, rel in enumerate(CHIP_RELS[1:], start=1):
            frm = _peer_position(pos, rel)
            remote(2 * i, main_src(src_ref, pos), main_place(full_ref, frm), frm).wait_recv()
            fwd = remote(8 + 2 * i, main_place(full_ref, frm), main_place(full_ref, frm), sib)
            fwd.start()
            sends.append(fwd)
            remote(2 * i + 1, mid_src(src_ref, pos), mid_place(mid_ref, frm), frm).wait_recv()
            fwd = remote(9 + 2 * i, mid_place(mid_ref, frm), mid_place(mid_ref, frm), sib)
            fwd.start()
            sends.append(fwd)
        remote(0, main_src(src_ref, pos), main_place(full_ref, sib), sib).wait_recv()
        remote(1, mid_src(src_ref, pos), mid_place(mid_ref, sib), sib).wait_recv()
        for i, rel in enumerate(CHIP_RELS[1:], start=1):
            frm = _peer_position(sib, rel)
            remote(8 + 2 * i, main_src(src_ref, pos), main_place(full_ref, frm), sib).wait_recv()
            remote(9 + 2 * i, mid_src(src_ref, pos), mid_place(mid_ref, frm), sib).wait_recv()
        for cp in sends:
            cp.wait_send()
        for cp in local:
            cp.wait()

    return _call(body, [src], name="ag_w_in", in_specs=[ANY], out_specs=[ANY, ANY],
                 out_shape=(jax.ShapeDtypeStruct((D, INW), BF16), jax.ShapeDtypeStruct((N_DEV, D, LANES), BF16)),
                 scratch_shapes=[pltpu.SemaphoreType.DMA((18,)), pltpu.SemaphoreType.DMA((18,))])


def _patch_mid(full, mid, a):
    D = full.shape[0]

    def body(full_ref, e_ref, o_ref, out_ref):
        out_ref[...] = e_ref[...] + o_ref[...]

    return _call(body, [full, mid, mid], name="patch_mid", grid=(N_DEV // 2,),
                 out_shape=jax.ShapeDtypeStruct(full.shape, full.dtype),
                 in_specs=[ANY, pl.BlockSpec((None, D, LANES), lambda j: (2 * j, 0, 0)),
                           pl.BlockSpec((None, D, LANES), lambda j: (2 * j + 1, 0, 0))],
                 out_specs=pl.BlockSpec((D, LANES), lambda j: (0, (2 * a + 1) * j + a)), aliases={0: 0})


MM_RESIDENT = 2048


def _mm(a, b, mode, out_dtype, name, b_off=0, n=None, comm=None, extras=(), epi=None, tn=None):
    if mode == "nn":
        (M, K), (K2, N) = a.shape, b.shape
    elif mode == "nt":
        (M, K), (N, K2) = a.shape, b.shape
    else:
        (K, M), (K2, N) = a.shape, b.shape
    assert K == K2, (a.shape, b.shape, mode)
    if n is not None:
        N = n
    single = not isinstance(out_dtype, (tuple, list))
    out_dtypes = (out_dtype,) if single else tuple(out_dtype)
    if epi is None:
        epi = lambda r: (r,)
    tk = K if K <= MM_RESIDENT else (MM_RESIDENT if K % MM_RESIDENT == 0 else _tile(K, 512, LANES))
    nk = K // tk
    if M > MM_RESIDENT and mode == "tn" and N <= MM_RESIDENT and not b_off:
        tm, tn = _tile(M, 512, LANES), N
    elif nk > 1:
        tm, tn = _tile(M, 1024, LANES), _tile(N, tn or 1024, LANES)
    else:
        tm = _tile(M, MM_RESIDENT, LANES)
        tn = _tile(math.gcd(N, b_off) if b_off else N, tn or 512, LANES)
    jb = b_off // tn
    dn = {"nn": NN, "nt": NT, "tn": TN}[mode]
    ne, no = len(extras), len(out_dtypes)

    def body(a_ref, b_ref, *rest):
        e_refs, o_refs = rest[:ne], rest[ne:ne + no]

        def finish(r):
            for o_ref, v in zip(o_refs, epi(r, *[e[...] for e in e_refs])):
                o_ref[...] = v.astype(o_ref.dtype)

        if nk == 1:
            finish(_bdot(a_ref[...], b_ref[...], dn))
            return
        acc_ref = rest[ne + no]
        k = pl.program_id(2)

        @pl.when(k == 0)
        def _():
            acc_ref[...] = _bdot(a_ref[...], b_ref[...], dn)

        @pl.when(jnp.logical_and(k > 0, k < nk - 1))
        def _():
            acc_ref[...] += _bdot(a_ref[...], b_ref[...], dn)

        @pl.when(k == nk - 1)
        def _():
            finish(acc_ref[...] + _bdot(a_ref[...], b_ref[...], dn))

    a_spec = pl.BlockSpec((tk, tm), lambda i, j, k: (k, i)) if mode == "tn" else pl.BlockSpec((tm, tk), lambda i, j, k: (i, k))
    b_spec = pl.BlockSpec((tn, tk), lambda i, j, k: (j, k)) if mode == "nt" else pl.BlockSpec((tk, tn), lambda i, j, k: (k, j + jb))
    o_spec = pl.BlockSpec((tm, tn), lambda i, j, k: (i, j))
    res = _call(body, [a, b] + list(extras), name=name, grid=(M // tm, N // tn, nk),
                out_shape=tuple(jax.ShapeDtypeStruct((M, N), dt) for dt in out_dtypes),
                in_specs=[a_spec, b_spec] + [o_spec] * ne, out_specs=[o_spec] * no,
                scratch_shapes=[pltpu.VMEM((tm, tn), F32)] if nk > 1 else [], comm=comm)
    return res[0] if single else res


def _rowwise(fn, row_ins, bcast_ins, row_outs, acc_outs, name, rt=256, comm=None):
    L = row_ins[0][0].shape[-2]
    rt = _tile(L, rt, 16)
    nr, nb, no = len(row_ins), len(bcast_ins), len(row_outs)

    def body(*refs):
        i = pl.program_id(0)
        vals = [r[...] for r in refs[:nr + nb]]
        outs, accs = fn(*vals)
        for r, v in zip(refs[nr + nb:nr + nb + no], outs):
            r[...] = v.astype(r.dtype)
        acc_refs = refs[nr + nb + no:]

        @pl.when(i == 0)
        def _():
            for r in acc_refs:
                r[...] = jnp.zeros_like(r)

        for r, v in zip(acc_refs, accs):
            r[...] += v

    in_specs = []
    for spec in row_ins:
        w, cb = spec[1], spec[2]
        if len(spec) == 4:
            in_specs.append(pl.BlockSpec((None, rt, w), functools.partial(lambda i, cb, ld: (ld, i, cb), cb=cb, ld=spec[3])))
        else:
            in_specs.append(pl.BlockSpec((rt, w), functools.partial(lambda i, cb: (i, cb), cb=cb)))
    in_specs += [pl.BlockSpec(b.shape, lambda i: (0, 0)) for b in bcast_ins]
    out_specs = [pl.BlockSpec((rt, w), lambda i: (i, 0)) for w, _ in row_outs]
    out_specs += [pl.BlockSpec(s, lambda i: (0, 0)) for s in acc_outs]
    out_shape = [jax.ShapeDtypeStruct((L, w), dt) for w, dt in row_outs] + [jax.ShapeDtypeStruct(s, F32) for s in acc_outs]
    return _call(body, [s[0] for s in row_ins] + list(bcast_ins), name=name, grid=(L // rt,), out_shape=tuple(out_shape),
                 in_specs=in_specs, out_specs=out_specs, comm=comm)


def _whole(fn, ins, out_shapes, name):
    def body(*refs):
        outs = fn(*[r[...] for r in refs[:len(ins)]])
        for r, v in zip(refs[len(ins):], outs):
            r[...] = v.astype(r.dtype)

    return _call(body, list(ins), name=name, out_shape=tuple(jax.ShapeDtypeStruct(s, dt) for s, dt in out_shapes))


def _silu(x):
    return x * jax.nn.sigmoid(x)


def _rms(x, g):
    return (x * lax.rsqrt(jnp.mean(x * x, axis=-1, keepdims=True) + EPS)) * g


def _modnorm(x, g, shift, scale):
    return _rms(x, g) * (1.0 + scale) + shift


def _adamw(w, g, m, v):
    m = ADAM_B1 * m + (1.0 - ADAM_B1) * g
    v = ADAM_B2 * v + (1.0 - ADAM_B2) * jnp.square(g)
    m_hat = m / (1.0 - ADAM_B1 ** ADAM_STEP)
    v_hat = v / (1.0 - ADAM_B2 ** ADAM_STEP)
    delta = -ADAM_LR * (m_hat / (jnp.sqrt(v_hat) + ADAM_EPS) + ADAM_WD * w)
    return delta, m, v


def _lower_bound(lg):
    e = jnp.exp(lg - jnp.max(lg, axis=0, keepdims=True))
    return e[0:1] / jnp.sum(e, axis=0, keepdims=True)


def _hg_chunk(hq, hf, hi, lb, st):
    C = hq.shape[0]
    row = lax.broadcasted_iota(jnp.int32, (C, C), 0)
    col = lax.broadcasted_iota(jnp.int32, (C, C), 1)
    tri = row >= col
    sg = jax.nn.sigmoid(hf)
    f = lb + (1.0 - lb) * sg
    lf = jnp.log(f)
    k = 1.0 - f
    q = _silu(hq)
    b = _dot(tri.astype(F32), lf, NN, precision=HIGHEST)
    m = b[C // 2 - 1:C // 2]
    bl = b[C - 1:C]
    e_qm, e_km, e_kl, e_q = jnp.exp(b - m), jnp.exp(m - b), jnp.exp(bl - b), jnp.exp(b)
    qe, ke, kd, qb = q * e_qm, k * e_km, k * e_kl, q * e_q
    sc = jnp.where(tri, _bdot(qe, ke, NT), 0.0)
    o = _bdot(sc, hi, NN) + _bdot(qb, st, NT)
    dec = jnp.exp(bl)
    st_next = st * dec + _bdot(hi, kd, TN)
    return o, st_next, dict(tri=tri, sg=sg, f=f, k=k, q=q, qe=qe, ke=ke, kd=kd, qb=qb, sc=sc, dec=dec,
                            e_qm=e_qm, e_km=e_km, e_kl=e_kl, e_q=e_q)


def _hg_out(o, hgate, gout):
    return _rms(o, gout) * _silu(hgate)


HG_GROUP = 8


def _hgrn_fwd(p4, lb_logits, gout, H, comm=None):
    L = p4.shape[0]
    C = HG_CHUNK
    GR = _tile(L // C, HG_GROUP, 1)
    T = GR * C
    N = L // T

    def body(hq_ref, hf_ref, hi_ref, hg_ref, lg_ref, gout_ref, o_ref, s_ref, st_ref):
        @pl.when(pl.program_id(1) == 0)
        def _():
            st_ref[...] = jnp.zeros_like(st_ref)

        lb = _lower_bound(lg_ref[...])
        st = st_ref[...]
        for ci in range(GR):
            rows = pl.ds(ci * C, C)
            s_ref[0, ci] = st
            o, st, _ = _hg_chunk(hq_ref[rows, :], hf_ref[rows, :], hi_ref[rows, :], lb, st)
            o_ref[rows, :] = _hg_out(o, hg_ref[rows, :], gout_ref[...]).astype(o_ref.dtype)
        st_ref[...] = st

    blk = lambda s: pl.BlockSpec((T, HG_DK), functools.partial(lambda h, n, s: (n, s * H + h), s=s))
    return _call(
        body, [p4, p4, p4, p4, lb_logits, gout], name="hgrn_fwd", grid=(H, N),
        out_shape=(jax.ShapeDtypeStruct((L, H * HG_DK), BF16), jax.ShapeDtypeStruct((H, N * GR, HG_DK, HG_DK), F32)),
        in_specs=[blk(0), blk(1), blk(2), blk(3), pl.BlockSpec((2, HG_DK), lambda h, n: (0, h)),
                  pl.BlockSpec((1, HG_DK), lambda h, n: (0, 0))],
        out_specs=(pl.BlockSpec((T, HG_DK), lambda h, n: (n, h)),
                   pl.BlockSpec((1, GR, HG_DK, HG_DK), lambda h, n: (h, n, 0, 0))),
        scratch_shapes=[pltpu.VMEM((HG_DK, HG_DK), F32)], comm=comm)


def _hgrn_bwd(p4, lb_logits, gout, s_all, d_out, H, comm=None):
    L = p4.shape[0]
    C = HG_CHUNK
    GR = _tile(L // C, HG_GROUP, 1)
    T = GR * C
    N = L // T

    def body(hq_ref, hf_ref, hi_ref, hg_ref, lg_ref, gout_ref, s_ref, do_ref,
             dq_ref, df_ref, di_ref, dg_ref, dlb_ref, dgo_ref, dst_ref):
        @pl.when(pl.program_id(1) == 0)
        def _():
            dst_ref[...] = jnp.zeros_like(dst_ref)
            dlb_ref[...] = jnp.zeros_like(dlb_ref)

        @pl.when(jnp.logical_and(pl.program_id(0) == 0, pl.program_id(1) == 0))
        def _():
            dgo_ref[...] = jnp.zeros_like(dgo_ref)

        lb = _lower_bound(lg_ref[...])
        dst = dst_ref[...]
        d_lb = jnp.zeros((1, HG_DK), F32)
        d_go = jnp.zeros((1, HG_DK), F32)
        for ci in reversed(range(GR)):
            rows = pl.ds(ci * C, C)
            dst, d_lb_c, d_go_c = chunk_bwd(rows, lb, s_ref[0, ci], dst, hq_ref, hf_ref, hi_ref, hg_ref, gout_ref, do_ref,
                                            dq_ref, df_ref, di_ref, dg_ref)
            d_lb += d_lb_c
            d_go += d_go_c
        dst_ref[...] = dst
        dlb_ref[...] += d_lb
        dgo_ref[...] += d_go

    def chunk_bwd(rows, lb, st, dst_next, hq_ref, hf_ref, hi_ref, hg_ref, gout_ref, do_ref, dq_ref, df_ref, di_ref, dg_ref):
        hq, hf, hi, hgate = hq_ref[rows, :], hf_ref[rows, :], hi_ref[rows, :], hg_ref[rows, :]
        o, _, t = _hg_chunk(hq, hf, hi, lb, st)
        _, out_vjp = jax.vjp(_hg_out, o, hgate, gout_ref[...])
        do, d_hgate, d_gout = out_vjp(do_ref[rows, :])
        tri = t["tri"]
        dsc = jnp.where(tri, _bdot(do, hi, NT), 0.0)
        dv = _bdot(t["sc"], do, TN) + _bdot(t["kd"], dst_next, NT)
        dqe = _bdot(dsc, t["ke"], NN)
        dke = _bdot(dsc, t["qe"], TN)
        dqb = _bdot(do, st, NN)
        dkd = _bdot(hi, dst_next, NN)
        ddec = jnp.sum(dst_next * st, axis=0, keepdims=True)
        dst_prev = _bdot(do, t["qb"], TN) + dst_next * t["dec"]
        dq = dqe * t["e_qm"] + dqb * t["e_q"]
        dk = dke * t["e_km"] + dkd * t["e_kl"]
        tq, tk, td, tb = dqe * t["qe"], dke * t["ke"], dkd * t["kd"], dqb * t["qb"]
        db = tq - tk - td + tb
        dm = jnp.sum(tk - tq, axis=0, keepdims=True)
        dbl = jnp.sum(td, axis=0, keepdims=True) + ddec * t["dec"]
        rowi = lax.broadcasted_iota(jnp.int32, (C, HG_DK), 0)
        db = db + jnp.where(rowi == C // 2 - 1, dm, 0.0) + jnp.where(rowi == C - 1, dbl, 0.0)
        dlf = _dot(tri.astype(F32), db, TN, precision=HIGHEST)
        df = dlf / t["f"] - dk
        sg = t["sg"]
        df_ref[rows, :] = (df * (1.0 - lb) * sg * (1.0 - sg)).astype(df_ref.dtype)
        sq = jax.nn.sigmoid(hq)
        dq_ref[rows, :] = (dq * (sq * (1.0 + hq * (1.0 - sq)))).astype(dq_ref.dtype)
        di_ref[rows, :] = dv.astype(di_ref.dtype)
        dg_ref[rows, :] = d_hgate.astype(dg_ref.dtype)
        return dst_prev, jnp.sum(df * (1.0 - sg), axis=0, keepdims=True), d_gout

    blk = lambda s: pl.BlockSpec((T, HG_DK), functools.partial(lambda h, n, s: (N - 1 - n, s * H + h), s=s))
    oblk = pl.BlockSpec((T, HG_DK), lambda h, n: (N - 1 - n, h))
    vec = pl.BlockSpec((1, HG_DK), lambda h, n: (0, h))
    W = H * HG_DK
    return _call(
        body, [p4, p4, p4, p4, lb_logits, gout, s_all, d_out], name="hgrn_bwd", grid=(H, N),
        out_shape=tuple([jax.ShapeDtypeStruct((L, W), BF16)] * 4 + [jax.ShapeDtypeStruct((1, W), F32), jax.ShapeDtypeStruct((1, HG_DK), F32)]),
        in_specs=[blk(0), blk(1), blk(2), blk(3), pl.BlockSpec((2, HG_DK), lambda h, n: (0, h)),
                  pl.BlockSpec((1, HG_DK), lambda h, n: (0, 0)),
                  pl.BlockSpec((1, GR, HG_DK, HG_DK), lambda h, n: (h, N - 1 - n, 0, 0)), oblk],
        out_specs=(oblk, oblk, oblk, oblk, vec, pl.BlockSpec((1, HG_DK), lambda h, n: (0, 0))),
        scratch_shapes=[pltpu.VMEM((HG_DK, HG_DK), F32)], comm=comm)


def _bucket_ids():
    i = jnp.arange(AT_BLOCK, dtype=jnp.int32)[:, None]
    j = jnp.arange(2 * AT_BLOCK, dtype=jnp.int32)[None, :]
    n = jnp.maximum(i - j + AT_BLOCK, 0)
    nf = jnp.maximum(n, 1).astype(F32)
    large = MAX_EXACT + (jnp.log(nf / MAX_EXACT) / math.log(MAX_DISTANCE / MAX_EXACT) * (N_BUCKETS - MAX_EXACT)).astype(jnp.int32)
    large = jnp.minimum(large, N_BUCKETS - 1)
    return jnp.where(n < MAX_EXACT, n, large).reshape(1, -1)


def _onehot(bucket):
    ids = lax.broadcasted_iota(jnp.int32, (N_BUCKETS, bucket.shape[1]), 0)
    return (ids == bucket).astype(F32)


def _attn_probs(qn, kpn, kcn, bias_g, sink, first, scale):
    rows = qn.shape[0]
    i = jnp.bitwise_and(lax.broadcasted_iota(jnp.int32, (rows, AT_BLOCK), 0), AT_BLOCK - 1)
    j = lax.broadcasted_iota(jnp.int32, (rows, AT_BLOCK), 1)
    lp = _bdot(qn, kpn, NT) * scale + bias_g[:, :AT_BLOCK]
    lc = _bdot(qn, kcn, NT) * scale + bias_g[:, AT_BLOCK:]
    lp = jnp.where(jnp.logical_and(j > i, jnp.logical_not(first)), lp, NEG_INF)
    lc = jnp.where(j <= i, lc, NEG_INF)
    m = jnp.maximum(jnp.maximum(jnp.max(lp, axis=-1, keepdims=True), jnp.max(lc, axis=-1, keepdims=True)), sink)
    pp, pc, ps = jnp.exp(lp - m), jnp.exp(lc - m), jnp.exp(sink - m)
    den = jnp.sum(pp, axis=-1, keepdims=True) + jnp.sum(pc, axis=-1, keepdims=True) + ps
    return pp / den, pc / den, ps / den


def _sink_rows(sk_ref, G):
    head = lax.broadcasted_iota(jnp.int32, (G * AT_BLOCK, 1), 0) // AT_BLOCK
    sink = jnp.zeros((G * AT_BLOCK, 1), F32)
    for g in range(G):
        sink = jnp.where(head == g, sk_ref[0, g:g + 1, :], sink)
    return sink


def _attn_fwd(q_t, kp, vp, qg, kg, sinks, bias, KVH, comm=None):
    AH, L, DH = q_t.shape
    G = AH // KVH
    NB = L // AT_BLOCK
    scale = DH ** -0.5

    def body(q_ref, kp_ref, kc_ref, vp_ref, vc_ref, qg_ref, kg_ref, sk_ref, b_ref, o_ref):
        first = pl.program_id(1) == 0
        kpn, kcn = _rms(kp_ref[0], kg_ref[...]), _rms(kc_ref[0], kg_ref[...])
        qn = _rms(q_ref[...].reshape(G * AT_BLOCK, DH), qg_ref[...])
        sink = _sink_rows(sk_ref, G)
        pp, pc, _ = _attn_probs(qn, kpn, kcn, b_ref[...].reshape(G * AT_BLOCK, 2 * AT_BLOCK), sink, first, scale)
        o = _bdot(pp, vp_ref[0], NN) + _bdot(pc, vc_ref[0], NN)
        o_ref[...] = o.reshape(G, AT_BLOCK, DH).astype(o_ref.dtype)

    kblk = lambda off: pl.BlockSpec((1, AT_BLOCK, DH), functools.partial(lambda h, n, off: (h, n + off, 0), off=off))
    return _call(
        body, [q_t, kp, kp, vp, vp, qg, kg, sinks, bias], name="attn_fwd", grid=(KVH, NB),
        out_shape=jax.ShapeDtypeStruct((AH, L, DH), BF16),
        in_specs=[pl.BlockSpec((G, AT_BLOCK, DH), lambda h, n: (h, n, 0)), kblk(0), kblk(1), kblk(0), kblk(1),
                  pl.BlockSpec((1, DH), lambda h, n: (0, 0)), pl.BlockSpec((1, DH), lambda h, n: (0, 0)),
                  pl.BlockSpec((1, G, 1), lambda h, n: (h, 0, 0)),
                  pl.BlockSpec((G, AT_BLOCK, 2 * AT_BLOCK), lambda h, n: (h, 0, 0))],
        out_specs=pl.BlockSpec((G, AT_BLOCK, DH), lambda h, n: (h, n, 0)), comm=comm)


def _attn_bwd(q_t, kp, vp, qg, kg, sinks, bias, do_t, KVH, comm=None):
    AH, L, DH = q_t.shape
    G = AH // KVH
    NB = L // AT_BLOCK
    B = AT_BLOCK
    scale = DH ** -0.5

    def body(q_ref, kp_ref, kc_ref, vp_ref, vc_ref, qg_ref, kg_ref, sk_ref, b_ref, do_ref,
             dq_ref, dk_ref, dv_ref, dqg_ref, dkg_ref, dsk_ref, db_ref):
        n = pl.program_id(1)
        first = n == 0

        @pl.when(first)
        def _():
            for r in (dk_ref, dv_ref, dsk_ref, db_ref):
                r[...] = jnp.zeros_like(r)

        @pl.when(jnp.logical_and(first, pl.program_id(0) == 0))
        def _():
            dqg_ref[...] = jnp.zeros_like(dqg_ref)
            dkg_ref[...] = jnp.zeros_like(dkg_ref)

        kp_raw, kc_raw, kgv, qgv = kp_ref[0], kc_ref[0], kg_ref[...], qg_ref[...]
        kpn, kp_vjp = jax.vjp(_rms, kp_raw, kgv)
        kcn, kc_vjp = jax.vjp(_rms, kc_raw, kgv)
        qn, q_vjp = jax.vjp(_rms, q_ref[...].reshape(G * B, DH), qgv)
        pp, pc, ps = _attn_probs(qn, kpn, kcn, b_ref[...].reshape(G * B, 2 * B), _sink_rows(sk_ref, G), first, scale)
        do = do_ref[...].reshape(G * B, DH)
        dvp = _bdot(pp, do, TN)
        dvc = _bdot(pc, do, TN)
        dpp = _bdot(do, vp_ref[0], NT)
        dpc = _bdot(do, vc_ref[0], NT)
        dsum = jnp.sum(dpp * pp, axis=-1, keepdims=True) + jnp.sum(dpc * pc, axis=-1, keepdims=True)
        dlp = pp * (dpp - dsum)
        dlc = pc * (dpc - dsum)
        dsk_ref[0] += jnp.sum((-ps * dsum).reshape(G, B, 1), axis=1)
        db_ref[:, :, :B] += dlp.reshape(G, B, B)
        db_ref[:, :, B:] += dlc.reshape(G, B, B)
        dlp, dlc = dlp * scale, dlc * scale
        dqn = _bdot(dlp, kpn, NN) + _bdot(dlc, kcn, NN)
        dq_raw, dqg = q_vjp(dqn)
        dq_ref[...] = dq_raw.reshape(G, B, DH).astype(dq_ref.dtype)
        dkp_raw, dkg_p = kp_vjp(_bdot(dlp, qn, TN))
        dkc_raw, dkg_c = kc_vjp(_bdot(dlc, qn, TN))
        r0 = pl.multiple_of(n * B, B)
        r1 = pl.multiple_of(n * B + B, B)
        dk_ref[0, pl.ds(r0, B), :] += dkp_raw
        dk_ref[0, pl.ds(r1, B), :] += dkc_raw
        dv_ref[0, pl.ds(r0, B), :] += dvp
        dv_ref[0, pl.ds(r1, B), :] += dvc
        dqg_ref[...] += dqg
        dkg_ref[...] += dkg_p + dkg_c

    kblk = lambda off: pl.BlockSpec((1, B, DH), functools.partial(lambda h, n, off: (h, n + off, 0), off=off))
    qblk = pl.BlockSpec((G, B, DH), lambda h, n: (h, n, 0))
    accblk = pl.BlockSpec((1, L + B, DH), lambda h, n: (h, 0, 0))
    vecblk = pl.BlockSpec((1, DH), lambda h, n: (0, 0))
    return _call(
        body, [q_t, kp, kp, vp, vp, qg, kg, sinks, bias, do_t], name="attn_bwd", grid=(KVH, NB),
        out_shape=(jax.ShapeDtypeStruct((AH, L, DH), BF16), jax.ShapeDtypeStruct((KVH, L + B, DH), F32),
                   jax.ShapeDtypeStruct((KVH, L + B, DH), F32), jax.ShapeDtypeStruct((1, DH), F32),
                   jax.ShapeDtypeStruct((1, DH), F32), jax.ShapeDtypeStruct((KVH, G, 1), F32),
                   jax.ShapeDtypeStruct((AH, B, 2 * B), F32)),
        in_specs=[qblk, kblk(0), kblk(1), kblk(0), kblk(1),
                  pl.BlockSpec((1, DH), lambda h, n: (0, 0)), pl.BlockSpec((1, DH), lambda h, n: (0, 0)),
                  pl.BlockSpec((1, G, 1), lambda h, n: (h, 0, 0)),
                  pl.BlockSpec((G, B, 2 * B), lambda h, n: (h, 0, 0)), qblk],
        out_specs=(qblk, accblk, accblk, vecblk, vecblk, pl.BlockSpec((1, G, 1), lambda h, n: (h, 0, 0)),
                   pl.BlockSpec((G, B, 2 * B), lambda h, n: (h, 0, 0))), comm=comm)


def _heads_first(t, nh):
    L = t.shape[0]
    return jnp.transpose(t.reshape(L, nh, t.shape[1] // nh), (1, 0, 2))


def _heads_last(t):
    nh, L, dh = t.shape
    return jnp.transpose(t, (1, 0, 2)).reshape(L, nh * dh)


def _softmax0(lg):
    e = jnp.exp(lg - jnp.max(lg, axis=0, keepdims=True))
    return e[0:1] / jnp.sum(e, axis=0, keepdims=True)


def _ada_update_call(fn, c_all, d_cols, w, m, v, rt):
    D, n = w.shape

    def body(c_ref, d_ref, w_ref, m_ref, v_ref, g_out, dl_out, m_out, v_out):
        outs, _ = fn(c_ref[...], d_ref[...], w_ref[...], m_ref[...], v_ref[...])
        for r, val in zip((g_out, dl_out, m_out, v_out), outs):
            r[...] = val

    wblk = pl.BlockSpec((rt, n), lambda i: (i, 0))
    return _call(
        body, [c_all, d_cols, w, m, v], name="update_ada", grid=(D // rt,), out_shape=tuple([jax.ShapeDtypeStruct((D, n), F32)] * 4),
        in_specs=[pl.BlockSpec((N_DEV, rt), lambda i: (0, i)), pl.BlockSpec((N_DEV, n), lambda i: (0, 0)), wblk, wblk, wblk],
        out_specs=(wblk, wblk, wblk, wblk))


def kernel(x, c, w_ada, b_ada, norm1_g, norm2_g, w_in, hg_lb_logits, hg_out_norm_g, q_norm_g, k_norm_g, attn_sinks, rel_bias_table, w_branch_hg, w_branch_attn, w_out, w_ff1, w_ff2, loss_target, m_w_ada, m_b_ada, m_norm1_g, m_norm2_g, m_w_in, m_hg_lb_logits, m_hg_out_norm_g, m_q_norm_g, m_k_norm_g, m_attn_sinks, m_rel_bias_table, m_w_branch_hg, m_w_branch_attn, m_w_out, m_w_ff1, m_w_ff2, v_w_ada, v_b_ada, v_norm1_g, v_norm2_g, v_w_in, v_hg_lb_logits, v_hg_out_norm_g, v_q_norm_g, v_k_norm_g, v_attn_sinks, v_rel_bias_table, v_w_branch_hg, v_w_branch_attn, v_w_out, v_w_ff1, v_w_ff2):
    cc = lax.axis_index("c")
    me = 4 * lax.axis_index("x") + 2 * lax.axis_index("y") + cc
    x2 = x[0]
    tgt = loss_target[0]
    L, D = x2.shape
    HGW = hg_lb_logits.shape[1]
    H = HGW // HG_DK
    AH = attn_sinks.shape[1]
    DH = q_norm_g.shape[1]
    ATW = AH * DH
    BW = w_in.shape[2]
    INW = BW * N_DEV
    A = BW // LANES
    assert BW == LANES * A + LANES // 2
    KVW = (INW - 4 * HGW - ATW - 2 * D) // 2
    KVH = KVW // DH
    G = AH // KVH
    ADA_N = w_ada.shape[2]
    PAIR = 2 * A + 1

    w_in_b = w_in[0].astype(BF16)
    src_in = jnp.where(cc == 0, jnp.pad(w_in_b, ((0, 0), (0, LANES // 2))), jnp.pad(w_in_b, ((0, 0), (LANES // 2, 0))))
    w_in_gapped, w_in_mid = _ag_w_in(src_in, A, D, INW)
    w_in_full = _patch_mid(w_in_gapped, w_in_mid, A)

    c_all = _gather_small(c, me, "gather_c")[:, 0, :]
    b_cols = lax.dynamic_slice(b_ada, (0, me * ADA_N), (1, ADA_N))
    (ada_cols,) = _whole(lambda cv, w, b: (_bdot(_silu(cv), w, NN) + b,), [c_all, w_ada[0], b_cols],
                         [((N_DEV, ADA_N), F32)], "ada_fwd")
    ada_all = _gather_small(ada_cols, me, "gather_ada")
    ada_row = lax.dynamic_slice(ada_all, (0, me, 0), (N_DEV, 1, ADA_N)).reshape(1, 6 * D)
    shift1, scale1, gate1, shift2, scale2, gate2 = [ada_row[:, i * D:(i + 1) * D] for i in range(6)]

    wnames = ("bhg", "bat", "out", "ff1", "ff2")
    waxis = dict(zip(wnames, (1, 1, 0, 1, 0)))
    wsrc = dict(zip(wnames, (w_branch_hg, w_branch_attn, w_out, w_ff1, w_ff2)))
    wblk = {k: wsrc[k][0].astype(BF16) for k in wnames}
    wf = {}

    (h,) = _rowwise(lambda xv, g, sh, sc: ((_modnorm(xv, g, sh, sc),), ()), [(x2, D, 0)], [norm1_g, shift1, scale1],
                    [(D, BF16)], [], "norm1")
    o4, oa = 4 * HGW, 4 * HGW + ATW + 2 * KVW
    r1, r2 = wblk["ff1"].shape[0], wblk["ff2"].shape[0]
    cm = _Comm()
    hs = {k: _ag_ici(cm, wblk[k], waxis[k]) for k in ("bhg", "bat")}
    p4 = _mm(h, w_in_full, "nn", F32, "proj_hg", n=o4, comm=cm)
    half = {k: cm.result(hs[k]) for k in hs}
    pa = _mm(h, w_in_full, "nn", F32, "proj_at", b_off=o4, n=oa - o4)
    cm = _Comm()
    hs = {k: _ag_d2d(cm, half[k], waxis[k]) for k in ("bhg", "bat")}
    hs["out"] = _ag_ici(cm, wblk["out"], waxis["out"])
    hs["ff2"] = _ag_ici(cm, wblk["ff2"], waxis["ff2"], rows=(0, r2 // 4))
    pg = _mm(h, w_in_full, "nn", F32, "proj_gate", b_off=oa, n=INW - oa, comm=cm)
    wf["bhg"], wf["bat"], half["out"], half["ff2"] = (cm.result(hs[k]) for k in ("bhg", "bat", "out", "ff2"))

    cm = _Comm()
    hs = {"out": _ag_d2d(cm, half["out"], waxis["out"]), "ff1": _ag_ici(cm, wblk["ff1"], waxis["ff1"], rows=(0, r1 // 2))}
    o_hg, s_all = _hgrn_fwd(p4, hg_lb_logits, hg_out_norm_g, H, comm=cm)
    wf["out"], half["ff1"] = cm.result(hs["out"]), cm.result(hs["ff1"])

    bucket = _bucket_ids()
    (bias_flat,) = _whole(lambda tb, bk: (_dot(tb, _onehot(bk), TN, precision=HIGHEST),), [rel_bias_table, bucket],
                          [((AH, AT_BLOCK * 2 * AT_BLOCK), F32)], "bias_fwd")
    bias = bias_flat.reshape(AH, AT_BLOCK, 2 * AT_BLOCK)
    q_t = _heads_first(pa[:, :ATW], AH)
    pad = lambda t: jnp.pad(t, ((0, 0), (AT_BLOCK, 0), (0, 0)))
    kp = pad(_heads_first(pa[:, ATW:ATW + KVW], KVH))
    vp = pad(_heads_first(pa[:, ATW + KVW:], KVH))
    sinks3 = attn_sinks.reshape(KVH, G, 1)
    cm = _Comm()
    hs = {"ff1": _ag_ici(cm, wblk["ff1"], waxis["ff1"], rows=(r1 // 2, r1), into=half["ff1"])}
    o_at = _heads_last(_attn_fwd(q_t, kp, vp, q_norm_g, k_norm_g, sinks3, bias, KVH, comm=cm))
    half["ff1"] = cm.result(hs["ff1"])

    cm = _Comm()
    hs = {"ff1": _ag_d2d(cm, half["ff1"], waxis["ff1"])}
    bh = _mm(o_hg, wf["bhg"], "nn", F32, "branch_hg", comm=cm)
    wf["ff1"] = cm.result(hs["ff1"])
    ba = _mm(o_at, wf["bat"], "nn", F32, "branch_at")

    def merge_fn(bhv, bav, ghg, gat):
        return jax.nn.sigmoid(ghg) * bhv + jax.nn.sigmoid(gat) * bav

    (merged,) = _rowwise(lambda *a: ((merge_fn(*a),), ()), [(bh, D, 0), (ba, D, 0), (pg, D, 0), (pg, D, 1)], [],
                         [(D, BF16)], [], "merge")
    cm = _Comm()
    hs = {"ff2": _ag_ici(cm, wblk["ff2"], waxis["ff2"], rows=(r2 // 4, r2 // 2), into=half["ff2"])}
    mo = _mm(merged, wf["out"], "nn", F32, "out_proj", comm=cm)
    half["ff2"] = cm.result(hs["ff2"])

    def resid1(xv, mov, g1, g2n, sh, sc):
        x1v = xv + g1 * mov
        return (x1v, _modnorm(x1v, g2n, sh, sc)), ()

    x1, h2 = _rowwise(resid1, [(x2, D, 0), (mo, D, 0)], [gate1, norm2_g, shift2, scale2], [(D, F32), (D, BF16)], [], "resid1")
    cm = _Comm()
    hs = {"ff2": _ag_ici(cm, wblk["ff2"], waxis["ff2"], rows=(r2 // 2, r2), into=half["ff2"])}
    u, act = _mm(h2, wf["ff1"], "nn", (F32, BF16), "ff1", comm=cm, epi=lambda r: (r, jnp.square(jnp.maximum(r, 0.0))))
    half["ff2"] = cm.result(hs["ff2"])
    cm = _Comm()
    hs = {"ff2": _ag_d2d(cm, half["ff2"], waxis["ff2"])}
    _call(lambda: None, [], name="ag_d2d_ff2", out_shape=(), comm=cm)
    wf["ff2"] = cm.result(hs["ff2"])
    ff = _mm(act, wf["ff2"], "nn", F32, "ff2")

    def loss_fn(x1v, ffv, tv, g2):
        e = x1v + g2 * ffv - tv
        dy = e * (1.0 / D)
        return (dy, dy * g2), (jnp.sum(e * e, axis=0, keepdims=True), jnp.sum(dy * ffv, axis=0, keepdims=True))

    dy, d_ff, sq_sum, d_gate2 = _rowwise(loss_fn, [(x1, D, 0), (ff, D, 0), (tgt, D, 0)], [gate2],
                                         [(D, F32), (D, BF16)], [(1, D), (1, D)], "loss")
    loss = lax.psum(jnp.sum(sq_sum) * (0.5 / D), ("x", "y", "c"))

    owner_base = jnp.stack([me ^ r for r in CHIP_RELS]).astype(jnp.int32)
    gw, recv1, part, recv2 = {}, {}, {}, {}
    gw["ff2"] = _mm(act, d_ff, "tn", BF16, "dw_ff2")
    cm = _Comm()
    hh = _rs_d2d(cm, gw["ff2"], waxis["ff2"])
    d_u = _mm(d_ff, wf["ff2"], "nt", BF16, "d_act", comm=cm, extras=[u], epi=lambda r, uv: (r * (2.0 * jnp.maximum(uv, 0.0)),))
    part["ff2"] = _rs_add(gw["ff2"], cm.result(hh), waxis["ff2"], owner_base, "rs_add_ff2")
    rows_ff2 = part["ff2"].shape[1]
    cm = _Comm()
    hh = _rs_ici(cm, part["ff2"], rows=(0, rows_ff2 // 2))
    gw["ff1"] = _mm(h2, d_u, "tn", BF16, "dw_ff1", comm=cm)
    cm2 = _Comm()
    hh2 = _rs_ici(cm2, part["ff2"], rows=(rows_ff2 // 2, rows_ff2), recv=cm.result(hh))
    hh1 = _rs_d2d(cm2, gw["ff1"], waxis["ff1"])
    d_h2 = _mm(d_u, wf["ff1"], "nt", F32, "d_h2", comm=cm2)
    recv2["ff2"] = cm2.result(hh2)
    part["ff1"] = _rs_add(gw["ff1"], cm2.result(hh1), waxis["ff1"], owner_base, "rs_add_ff1")

    def norm2_bwd(dh2v, x1v, dyv, mov, g2n, sh, sc, g1):
        _, vjp = jax.vjp(_modnorm, x1v, g2n, sh, sc)
        dx, dg, dsh, dsc = vjp(dh2v)
        dx1 = dyv + dx
        return (dx1, dx1 * g1), (dg, dsh, dsc, jnp.sum(dx1 * mov, axis=0, keepdims=True))

    d_x1, d_mo, d_g2n, d_shift2, d_scale2, d_gate1 = _rowwise(
        norm2_bwd, [(d_h2, D, 0), (x1, D, 0), (dy, D, 0), (mo, D, 0)], [norm2_g, shift2, scale2, gate1],
        [(D, F32), (D, BF16)], [(1, D)] * 4, "norm2_bwd")
    gw["out"] = _mm(merged, d_mo, "tn", BF16, "dw_out")
    cm = _Comm()
    hh = _rs_d2d(cm, gw["out"], waxis["out"])
    d_merged = _mm(d_mo, wf["out"], "nt", F32, "d_merged", comm=cm)
    part["out"] = _rs_add(gw["out"], cm.result(hh), waxis["out"], owner_base, "rs_add_out")

    def merge_bwd(dmv, bhv, bav, ghg, gat):
        _, vjp = jax.vjp(merge_fn, bhv, bav, ghg, gat)
        return vjp(dmv), ()

    d_bh, d_ba, d_ghg, d_gat = _rowwise(merge_bwd, [(d_merged, D, 0), (bh, D, 0), (ba, D, 0), (pg, D, 0), (pg, D, 1)], [],
                                        [(D, BF16)] * 4, [], "merge_bwd")
    gw["bhg"] = _mm(o_hg, d_bh, "tn", BF16, "dw_bhg")
    gw["bat"] = _mm(o_at, d_ba, "tn", BF16, "dw_bat")
    cm = _Comm()
    hh = {k: _rs_d2d(cm, gw[k], waxis[k]) for k in ("bhg", "bat")}
    d_ohg = _mm(d_bh, wf["bhg"], "nt", F32, "d_ohg", comm=cm)
    for k in ("bhg", "bat"):
        part[k] = _rs_add(gw[k], cm.result(hh[k]), waxis[k], owner_base, "rs_add_" + k)
    d_oat = _mm(d_ba, wf["bat"], "nt", BF16, "d_oat")

    cm = _Comm()
    hh = {"ff1": _rs_ici(cm, part["ff1"])}
    d_hq, d_hf, d_hi, d_hg, d_lb, d_gout_h = _hgrn_bwd(p4, hg_lb_logits, hg_out_norm_g, s_all, d_ohg, H, comm=cm)
    recv2["ff1"] = cm.result(hh["ff1"])
    cm = _Comm()
    hh = {k: _rs_ici(cm, part[k]) for k in ("out", "bhg", "bat")}
    dq_t, dkp, dvp, d_qg, d_kg, d_sk, d_bias = _attn_bwd(q_t, kp, vp, q_norm_g, k_norm_g, sinks3, bias,
                                                         _heads_first(d_oat, AH), KVH, comm=cm)
    for k in hh:
        recv2[k] = cm.result(hh[k])
    d_aq = _heads_last(dq_t)
    d_ak = _heads_last(dkp[:, AT_BLOCK:, :]).astype(BF16)
    d_av = _heads_last(dvp[:, AT_BLOCK:, :]).astype(BF16)
    d_proj = jnp.concatenate([d_hq, d_hf, d_hi, d_hg, d_aq, d_ak, d_av, d_ghg, d_gat], axis=1)
    gw_in = _mm(h, d_proj, "tn", BF16, "dw_in")

    wm = LANES * A
    cm = _Comm()
    hi_ = cm.inp(gw_in)
    h_main, h_mid = cm.out((4, D, wm), BF16), cm.out((4, D, LANES), BF16)
    for i, r in enumerate(CHIP_RELS):
        def main_view(ref, p, r=r):
            o = p["me"] ^ r ^ 1
            return ref.at[:, pl.ds(pl.multiple_of((PAIR * (o // 2) + (A + 1) * (1 - p["c"])) * LANES, LANES), wm)]

        def mid_view(ref, p, r=r):
            o = p["me"] ^ r
            return ref.at[:, pl.ds(pl.multiple_of((PAIR * (o // 2) + A) * LANES, LANES), LANES)]

        cm.copy(hi_, main_view, h_main, _slot_view(i), 1)
        cm.copy(hi_, mid_view, h_mid, _slot_view(i), 1)
    _call(lambda: None, [], name="rs_d2d_in", out_shape=(), comm=cm)
    chip = jnp.stack([(me ^ r) // 2 for r in CHIP_RELS]).astype(jnp.int32)
    part_main = _rs_add(gw_in, cm.result(h_main), 1, PAIR * chip + (A + 1) * cc, "rs_add_in_main", tw=LANES)
    part_mid = _rs_add(gw_in, cm.result(h_mid), 1, PAIR * chip + A, "rs_add_in_mid", tw=LANES)
    cm = _Comm()
    hh_main, hh_mid = _rs_ici(cm, part_main), _rs_ici(cm, part_mid)
    d_h = _mm(d_proj, w_in_full, "nt", F32, "d_h", comm=cm)
    rx_main, rx_mid = cm.result(hh_main), cm.result(hh_mid)

    def norm1_bwd(dhv, xv, dx1v, g1n, sh, sc):
        _, vjp = jax.vjp(_modnorm, xv, g1n, sh, sc)
        dx, dg, dsh, dsc = vjp(dhv)
        return (dx1v + dx,), (dg, dsh, dsc)

    grad_x, d_g1n, d_shift1, d_scale1 = _rowwise(norm1_bwd, [(d_h, D, 0), (x2, D, 0), (d_x1, D, 0)],
                                                 [norm1_g, shift1, scale1], [(D, F32)], [(1, D)] * 3, "norm1_bwd")

    def sum4(p0, p1, p2, p3):
        return ((p0.astype(F32) + p1.astype(F32)) + p2.astype(F32)) + p3.astype(F32)

    def update_fn(w, m, v, p0, p1, p2, p3):
        g = sum4(p0, p1, p2, p3)
        delta, mn, vn = _adamw(w, g, m, v)
        return (g, delta, mn, vn), ()

    wmv = dict(zip(wnames, ((w_branch_hg, m_w_branch_hg, v_w_branch_hg), (w_branch_attn, m_w_branch_attn, v_w_branch_attn),
                            (w_out, m_w_out, v_w_out), (w_ff1, m_w_ff1, v_w_ff1), (w_ff2, m_w_ff2, v_w_ff2))))
    res = {}
    for k in wnames:
        w, m, v = (t[0] for t in wmv[k])
        n = w.shape[1]
        ins = [(t, n, 0) for t in (w, m, v)] + [(part[k], n, 0, 0)] + [(recv2[k], n, 0, i) for i in range(3)]
        res[k] = [t[None] for t in _rowwise(update_fn, ins, [], [(n, F32)] * 4, [], "update_" + k)]

    g_main, = _rowwise(lambda *p: ((sum4(*p),), ()), [(part_main, wm, 0, 0)] + [(rx_main, wm, 0, i) for i in range(3)], [],
                       [(wm, F32)], [], "sum_in_main")
    g_mid, = _rowwise(lambda *p: ((sum4(*p),), ()), [(part_mid, LANES, 0, 0)] + [(rx_mid, LANES, 0, i) for i in range(3)], [],
                      [(LANES, F32)], [], "sum_in_mid")
    g_in = jnp.where(cc == 0, jnp.concatenate([g_main, g_mid[:, :LANES // 2]], axis=1),
                     jnp.concatenate([g_mid[:, LANES // 2:], g_main], axis=1))

    def update_given(w, m, v, g):
        delta, mn, vn = _adamw(w, g, m, v)
        return (g, delta, mn, vn), ()

    res["in"] = [t[None] for t in _rowwise(update_given, [(t, BW, 0) for t in (w_in[0], m_w_in[0], v_w_in[0], g_in)], [],
                                           [(BW, F32)] * 4, [], "update_in")]

    d_ada_row = jnp.concatenate([d_shift1, d_scale1, d_gate1, d_shift2, d_scale2, d_gate2], axis=1)
    d_ada_all = _gather_small(d_ada_row, me, "gather_dada")[:, 0, :]
    d_ada_cols = lax.dynamic_slice(d_ada_all, (0, me * ADA_N), (N_DEV, ADA_N))

    def ada_update(cv, dav, w, m, v):
        g = _bdot(_silu(cv), dav, TN)
        delta, mn, vn = _adamw(w, g, m, v)
        return (g, delta, mn, vn), ()

    res["ada"] = [t[None] for t in _ada_update_call(ada_update, c_all, d_ada_cols, w_ada[0], m_w_ada[0], v_w_ada[0], _tile(D, 256, 16))]

    d_sinks = d_sk.reshape(1, AH)
    (d_table_t,) = _whole(lambda db, bk: (_dot(db, _onehot(bk), NT, precision=HIGHEST),),
                          [d_bias.reshape(AH, AT_BLOCK * 2 * AT_BLOCK), bucket], [((AH, N_BUCKETS), F32)], "bias_bwd")
    smalls = [d_g1n, d_g2n, d_lb, d_gout_h, d_qg, d_kg, d_sinks, d_table_t.T.reshape(1, N_BUCKETS * AH)]
    widths = [s.shape[1] for s in smalls]
    lanes = [-(-w // LANES) * LANES for w in widths]
    smalls = [jnp.pad(s, ((0, 0), (0, p - w))) for s, w, p in zip(smalls, widths, lanes)]
    packed = _gather_small(jnp.concatenate(smalls, axis=1), me, "gather_small")[:, 0, :]
    offs = [sum(lanes[:i]) for i in range(len(lanes))]

    def small_update(pk, dada, lg, *wmv_flat):
        tot = pk[0:1]
        for d in range(1, N_DEV):
            tot = tot + pk[d:d + 1]
        gb = dada[0:1]
        for d in range(1, N_DEV):
            gb = gb + dada[d:d + 1]
        gs = [tot[:, offs[i]:offs[i] + widths[i]] for i in range(len(widths))]
        _, lb_vjp = jax.vjp(_softmax0, lg)
        (g_lg,) = lb_vjp(gs[2])
        grads = [gb, gs[0], gs[1], g_lg, gs[3], gs[4], gs[5], gs[6], gs[7]]
        outs = []
        for i, g in enumerate(grads):
            w, m, v = wmv_flat[3 * i:3 * i + 3]
            delta, mn, vn = _adamw(w, g, m, v)
            outs += [g, delta, mn, vn]
        return tuple(outs)

    tbl = lambda t: t.reshape(1, N_BUCKETS * AH)
    small_wmv = [(b_ada, m_b_ada, v_b_ada), (norm1_g, m_norm1_g, v_norm1_g), (norm2_g, m_norm2_g, v_norm2_g),
                 (hg_lb_logits, m_hg_lb_logits, v_hg_lb_logits), (hg_out_norm_g, m_hg_out_norm_g, v_hg_out_norm_g),
                 (q_norm_g, m_q_norm_g, v_q_norm_g), (k_norm_g, m_k_norm_g, v_k_norm_g),
                 (attn_sinks, m_attn_sinks, v_attn_sinks),
                 (tbl(rel_bias_table), tbl(m_rel_bias_table), tbl(v_rel_bias_table))]
    flat = [t for trip in small_wmv for t in trip]
    out_shapes = [(trip[0].shape, F32) for trip in small_wmv for _ in range(4)]
    sres = _whole(small_update, [packed, d_ada_all, hg_lb_logits] + flat, out_shapes, "small_update")
    names_small = ("b_ada", "norm1_g", "norm2_g", "lb", "gout", "qg", "kg", "sinks", "table")
    for i, k in enumerate(names_small):
        r = sres[4 * i:4 * i + 4]
        if k == "table":
            r = [t.reshape(N_BUCKETS, AH) for t in r]
        res[k] = r

    order = ("ada", "b_ada", "norm1_g", "norm2_g", "in", "lb", "gout", "qg", "kg", "sinks", "table", "bhg", "bat", "out", "ff1", "ff2")
    outs = [loss, grad_x[None]]
    for j in range(4):
        outs += [res[k][j] for k in order]
    return tuple(outs)
```

```python
import functools
import math

import jax
import jax.numpy as jnp
from jax import lax
from jax.experimental import pallas as pl
from jax.experimental.pallas import tpu as pltpu

F32 = jnp.float32
BF16 = jnp.bfloat16
EPS = 1e-6
NEG_INF = -1e30
HG_DK = 128
HG_CHUNK = 64
AT_BLOCK = 128
N_BUCKETS = 32
MAX_EXACT = 16
MAX_DISTANCE = 128
N_DEV = 8
LANES = 128
VMEM_LIMIT = 56 * 1024 * 1024
ADAM_LR, ADAM_B1, ADAM_B2, ADAM_EPS, ADAM_WD, ADAM_STEP = 0.001, 0.9, 0.999, 1e-08, 0.01, 10
HIGHEST = lax.Precision.HIGHEST
MESH = pl.DeviceIdType.MESH
ANY = pl.BlockSpec(memory_space=pl.ANY)
CHIP_RELS = (0, 4, 2, 6)

NN = (((1,), (0,)), ((), ()))
NT = (((1,), (1,)), ((), ()))
TN = (((0,), (0,)), ((), ()))


def _tile(n, pref, unit):
    if n <= pref:
        return n
    t = (pref // unit) * unit
    while t >= unit:
        if n % t == 0:
            return t
        t -= unit
    return n


def _dot(a, b, dn, precision=None):
    return lax.dot_general(a, b, dn, preferred_element_type=F32, precision=precision)


def _bdot(a, b, dn):
    return _dot(a.astype(BF16), b.astype(BF16), dn)


def _position():
    x, y, c = lax.axis_index("x"), lax.axis_index("y"), lax.axis_index("c")
    return dict(x=x, y=y, c=c, me=4 * x + 2 * y + c)


def _peer_position(p, rel):
    x = 1 - p["x"] if rel & 4 else p["x"]
    y = 1 - p["y"] if rel & 2 else p["y"]
    c = 1 - p["c"] if rel & 1 else p["c"]
    return dict(x=x, y=y, c=c, me=4 * x + 2 * y + c)


class _Comm:
    def __init__(self):
        self.ins, self.outs, self.alias, self.plans, self.res = [], [], {}, [], None

    def inp(self, arr):
        self.ins.append(arr)
        return ("i", len(self.ins) - 1)

    def out(self, shape, dtype, alias=None):
        self.outs.append(jax.ShapeDtypeStruct(tuple(shape), dtype))
        if alias is not None:
            self.alias[alias[1]] = len(self.outs) - 1
        return ("o", len(self.outs) - 1)

    def copy(self, src, src_view, dst, dst_view, rel):
        self.plans.append((src, src_view, dst, dst_view, rel))

    def result(self, handle):
        return self.res[handle[1]]

    def build(self, in_refs, out_refs, send_sems, recv_sems):
        pos = _position()
        ref = lambda h: in_refs[h[1]] if h[0] == "i" else out_refs[h[1]]
        ops = []
        for k, (src, sv, dst, dv, rel) in enumerate(self.plans):
            s = sv(ref(src), pos)
            if rel == 0:
                cp = pltpu.make_async_copy(s, dv(ref(dst), pos), send_sems.at[k])
                ops.append((cp.start, cp.wait))
                continue
            peer = _peer_position(pos, rel)
            mk = lambda d: pltpu.make_async_remote_copy(
                src_ref=s, dst_ref=d, send_sem=send_sems.at[k], recv_sem=recv_sems.at[k],
                device_id=(peer["x"], peer["y"], peer["c"]), device_id_type=MESH)
            out_cp, in_cp = mk(dv(ref(dst), pos)), mk(dv(ref(dst), peer))

            def wait(out_cp=out_cp, in_cp=in_cp):
                out_cp.wait_send()
                in_cp.wait_recv()

            ops.append((out_cp.start, wait))
        return ops


def _call(body, args, *, name, out_shape, in_specs=None, out_specs=None, grid=None, scratch_shapes=(), comm=None,
          prefetch=None, aliases=None):
    single = not isinstance(out_shape, (tuple, list))
    out_shape = (out_shape,) if single else tuple(out_shape)
    n_in, n_out, n_scr = len(args), len(out_shape), len(scratch_shapes)
    vm = pl.BlockSpec(memory_space=pltpu.VMEM)
    in_specs = [vm] * n_in if in_specs is None else list(in_specs)
    out_specs = [vm] * n_out if out_specs is None else (list(out_specs) if isinstance(out_specs, (tuple, list)) else [out_specs])
    n_pf = 0 if prefetch is None else len(prefetch)
    kw = {} if aliases is None else {"input_output_aliases": dict(aliases)}
    if comm is None:
        fn = body
        all_args, all_scratch = list(args), list(scratch_shapes)
    else:
        n_ci, n_co, n_x = len(comm.ins), len(comm.outs), len(comm.plans)

        def fn(*refs):
            pf, refs = refs[:n_pf], refs[n_pf:]
            o_in, c_in = refs[:n_in], refs[n_in:n_in + n_ci]
            o_out = refs[n_in + n_ci:n_in + n_ci + n_out]
            c_out = refs[n_in + n_ci + n_out:n_in + n_ci + n_out + n_co]
            scr = refs[n_in + n_ci + n_out + n_co:]
            ops = comm.build(c_in, c_out, scr[n_scr], scr[n_scr + 1])
            if grid:
                first = functools.reduce(jnp.logical_and, [pl.program_id(i) == 0 for i in range(len(grid))])
                last = functools.reduce(jnp.logical_and, [pl.program_id(i) == g - 1 for i, g in enumerate(grid)])

                @pl.when(first)
                def _():
                    for start, _w in ops:
                        start()
            else:
                for start, _w in ops:
                    start()
            body(*pf, *o_in, *o_out, *scr[:n_scr])
            if grid:
                @pl.when(last)
                def _():
                    for _s, wait in ops:
                        wait()
            else:
                for _s, wait in ops:
                    wait()

        all_args = list(args) + list(comm.ins)
        in_specs = in_specs + [ANY] * n_ci
        out_shape = out_shape + tuple(comm.outs)
        out_specs = out_specs + [ANY] * n_co
        all_scratch = list(scratch_shapes) + [pltpu.SemaphoreType.DMA((n_x,)), pltpu.SemaphoreType.DMA((n_x,))]
        kw["input_output_aliases"] = {n_pf + n_in + i: n_out + o for i, o in comm.alias.items()}
    sem = None if grid is None else ("arbitrary",) * len(grid)
    params = pltpu.CompilerParams(dimension_semantics=sem, vmem_limit_bytes=VMEM_LIMIT)
    if prefetch is None:
        spec = dict(in_specs=in_specs, out_specs=tuple(out_specs), scratch_shapes=all_scratch)
        if grid is not None:
            spec["grid"] = grid
    else:
        spec = dict(grid_spec=pltpu.PrefetchScalarGridSpec(
            num_scalar_prefetch=n_pf, grid=grid, in_specs=in_specs, out_specs=tuple(out_specs), scratch_shapes=all_scratch))
        all_args = list(prefetch) + all_args
    res = pl.pallas_call(fn, name=name, out_shape=out_shape, compiler_params=params, **spec, **kw)(*all_args)
    res = list(res)
    if comm is not None:
        comm.res = res[n_out:]
        res = res[:n_out]
    return res[0] if single else res


def _whole_view(ref, pos):
    return ref


def _block_view(axis, n, index, rows=None):
    def view(ref, pos):
        off = pl.multiple_of(index(pos) * n, n)
        if rows is None:
            return ref.at[:, pl.ds(off, n)] if axis == 1 else ref.at[pl.ds(off, n), :]
        lo, cnt = rows[0], rows[1] - rows[0]
        if axis == 1:
            return ref.at[pl.ds(lo, cnt), pl.ds(off, n)]
        return ref.at[pl.ds(pl.multiple_of(off + lo, 16), cnt), :]
    return view


def _rows_view(rows):
    def view(ref, pos):
        return ref if rows is None else ref.at[pl.ds(rows[0], rows[1] - rows[0]), :]
    return view


def _slot_view(i, rows=None):
    def view(ref, pos):
        return ref.at[i] if rows is None else ref.at[i, pl.ds(rows[0], rows[1] - rows[0]), :]
    return view


def _exchange(items, name):
    cm = _Comm()
    for a, rel in items:
        cm.copy(cm.inp(a), _whole_view, cm.out(a.shape, a.dtype), _whole_view, rel)
    _call(lambda: None, [], name=name, out_shape=(), comm=cm)
    return cm.res


def _gather_small(v, me, name):
    cm = _Comm()
    hi, ho = cm.inp(v), cm.out((N_DEV,) + v.shape, v.dtype)
    for rel in range(N_DEV):
        cm.copy(hi, _whole_view, ho, lambda ref, p: ref.at[p["me"]], rel)
    _call(lambda: None, [], name=name, out_shape=(), comm=cm)
    return cm.result(ho)


def _ag_ici(cm, blk, axis, rows=None, into=None):
    n = blk.shape[axis]
    shape = list(blk.shape)
    shape[axis] = n * N_DEV
    hi = cm.inp(blk)
    ho = cm.out(shape, blk.dtype) if into is None else cm.out(shape, blk.dtype, alias=cm.inp(into))
    own = _block_view(axis, n, lambda p: p["me"], rows)
    for rel in CHIP_RELS:
        cm.copy(hi, _rows_view(rows), ho, own, rel)
    return ho


def _ag_d2d(cm, full, axis):
    n = full.shape[axis] // N_DEV
    hi = cm.inp(full)
    ho = cm.out(full.shape, full.dtype, alias=hi)
    for r in CHIP_RELS:
        v = _block_view(axis, n, functools.partial(lambda p, r: p["me"] ^ r, r=r))
        cm.copy(hi, v, ho, v, 1)
    return ho


def _rs_d2d(cm, gw, axis):
    n = gw.shape[axis] // N_DEV
    shape = list(gw.shape)
    shape[axis] = n
    hi, ho = cm.inp(gw), cm.out([4] + shape, gw.dtype)
    for i, r in enumerate(CHIP_RELS):
        cm.copy(hi, _block_view(axis, n, functools.partial(lambda p, r: p["me"] ^ r ^ 1, r=r)), ho, _slot_view(i), 1)
    return ho


def _rs_ici(cm, part, rows=None, recv=None):
    if recv is None:
        ho = cm.out((3,) + part.shape[1:], part.dtype)
    else:
        ho = cm.out(recv.shape, recv.dtype, alias=cm.inp(recv))
    hi = cm.inp(part)
    for i in (1, 2, 3):
        cm.copy(hi, _slot_view(i, rows), ho, _slot_view(i - 1, rows), CHIP_RELS[i])
    return ho


def _rs_add(gw, recv, axis, base, name, tw=None):
    _, R, n = recv.shape
    if axis == 1:
        tw = n if tw is None else tw
        gw_spec = pl.BlockSpec((R, tw), lambda i, t, b: (0, b[i] + t))
        rv_spec = pl.BlockSpec((None, R, tw), lambda i, t, b: (i, 0, t))
        grid = (4, n // tw)
    else:
        tw = _tile(n, 1024, LANES)
        gw_spec = pl.BlockSpec((R, tw), lambda i, t, b: (b[i], t))
        rv_spec = pl.BlockSpec((None, R, tw), lambda i, t, b: (i, 0, t))
        grid = (4, n // tw)

    def body(b_ref, g_ref, r_ref, o_ref):
        o_ref[...] = (g_ref[...].astype(F32) + r_ref[...].astype(F32)).astype(o_ref.dtype)

    return _call(body, [gw, recv], name=name, out_shape=jax.ShapeDtypeStruct(recv.shape, recv.dtype), grid=grid,
                 in_specs=[gw_spec, rv_spec], out_specs=rv_spec, prefetch=[base])


def _ag_w_in(src, a, D, INW):
    wm = LANES * a

    def main_place(ref, p):
        off = pl.multiple_of(((2 * a + 1) * (p["me"] // 2) + (a + 1) * p["c"]) * LANES, LANES)
        return ref.at[:, pl.ds(off, wm)]

    def main_src(ref, p):
        return ref.at[:, pl.ds(pl.multiple_of(p["c"] * LANES, LANES), wm)]

    def mid_src(ref, p):
        return ref.at[:, pl.ds(pl.multiple_of((1 - p["c"]) * wm, LANES), LANES)]

    def mid_place(ref, p):
        return ref.at[p["me"]]

    def body(src_ref, full_ref, mid_ref, send_sems, recv_sems):
        pos = _position()
        sib = _peer_position(pos, 1)

        def remote(k, s, d, to):
            return pltpu.make_async_remote_copy(src_ref=s, dst_ref=d, send_sem=send_sems.at[k], recv_sem=recv_sems.at[k],
                                                device_id=(to["x"], to["y"], to["c"]), device_id_type=MESH)

        local = [pltpu.make_async_copy(main_src(src_ref, pos), main_place(full_ref, pos), send_sems.at[16]),
                 pltpu.make_async_copy(mid_src(src_ref, pos), mid_place(mid_ref, pos), send_sems.at[17])]
        for cp in local:
            cp.start()
        sends = []
        for i, rel in enumerate(CHIP_RELS):
            to = sib if rel == 0 else _peer_position(pos, rel)
            sends.append(remote(2 * i, main_src(src_ref, pos), main_place(full_ref, pos), to))
            sends.append(remote(2 * i + 1, mid_src(src_ref, pos), mid_place(mid_ref, pos), to))
        for cp in sends:
            cp.start()
        for i, rel in enumerate(CHIP_RELS[1:], start=1):
            frm = _peer_position(pos, rel)
            remote(2 * i, main_src(src_ref, pos), main_place(full_ref, frm), frm).wait_recv()
            fwd = remote(8 + 2 * i, main_place(full_ref, frm), main_place(full_ref, frm), sib)
            fwd.start()
            sends.append(fwd)
            remote(2 * i + 1, mid_src(src_ref, pos), mid_place(mid_ref, frm), frm).wait_recv()
            fwd = remote(9 + 2 * i, mid_place(mid_ref, frm), mid_place(mid_ref, frm), sib)
            fwd.start()
            sends.append(fwd)
        remote(0, main_src(src_ref, pos), main_place(full_ref, sib), sib).wait_recv()
        remote(1, mid_src(src_ref, pos), mid_place(mid_ref, sib), sib).wait_recv()
        for i, rel in enumerate(CHIP_RELS[1:], start=1):
            frm = _peer_position(sib, rel)
            remote(8 + 2 * i, main_src(src_ref, pos), main_place(full_ref, frm), sib).wait_recv()
            remote(9 + 2 * i, mid_src(src_ref, pos), mid_place(mid_ref, frm), sib).wait_recv()
        for cp in sends:
            cp.wait_send()
        for cp in local:
            cp.wait()

    return _call(body, [src], name="ag_w_in", in_specs=[ANY], out_specs=[ANY, ANY],
                 out_shape=(jax.ShapeDtypeStruct((D, INW), BF16), jax.ShapeDtypeStruct((N_DEV, D, LANES), BF16)),
                 scratch_shapes=[pltpu.SemaphoreType.DMA((18,)), pltpu.SemaphoreType.DMA((18,))])


def _patch_mid(full, mid, a):
    D = full.shape[0]

    def body(full_ref, e_ref, o_ref, out_ref):
        out_ref[...] = e_ref[...] + o_ref[...]

    return _call(body, [full, mid, mid], name="patch_mid", grid=(N_DEV // 2,),
                 out_shape=jax.ShapeDtypeStruct(full.shape, full.dtype),
                 in_specs=[ANY, pl.BlockSpec((None, D, LANES), lambda j: (2 * j, 0, 0)),
                           pl.BlockSpec((None, D, LANES), lambda j: (2 * j + 1, 0, 0))],
                 out_specs=pl.BlockSpec((D, LANES), lambda j: (0, (2 * a + 1) * j + a)), aliases={0: 0})


MM_RESIDENT = 2048


def _mm(a, b, mode, out_dtype, name, b_off=0, n=None, comm=None, extras=(), epi=None, tn=None):
    if mode == "nn":
        (M, K), (K2, N) = a.shape, b.shape
    elif mode == "nt":
        (M, K), (N, K2) = a.shape, b.shape
    else:
        (K, M), (K2, N) = a.shape, b.shape
    assert K == K2, (a.shape, b.shape, mode)
    if n is not None:
        N = n
    single = not isinstance(out_dtype, (tuple, list))
    out_dtypes = (out_dtype,) if single else tuple(out_dtype)
    if epi is None:
        epi = lambda r: (r,)
    tk = K if K <= MM_RESIDENT else (MM_RESIDENT if K % MM_RESIDENT == 0 else _tile(K, 512, LANES))
    nk = K // tk
    if M > MM_RESIDENT and mode == "tn" and N <= MM_RESIDENT and not b_off:
        tm, tn = _tile(M, 512, LANES), N
    elif nk > 1:
        tm, tn = _tile(M, 1024, LANES), _tile(N, tn or 1024, LANES)
    else:
        tm = _tile(M, MM_RESIDENT, LANES)
        tn = _tile(math.gcd(N, b_off) if b_off else N, tn or 512, LANES)
    jb = b_off // tn
    dn = {"nn": NN, "nt": NT, "tn": TN}[mode]
    ne, no = len(extras), len(out_dtypes)

    def body(a_ref, b_ref, *rest):
        e_refs, o_refs = rest[:ne], rest[ne:ne + no]

        def finish(r):
            for o_ref, v in zip(o_refs, epi(r, *[e[...] for e in e_refs])):
                o_ref[...] = v.astype(o_ref.dtype)

        if nk == 1:
            finish(_bdot(a_ref[...], b_ref[...], dn))
            return
        acc_ref = rest[ne + no]
        k = pl.program_id(2)

        @pl.when(k == 0)
        def _():
            acc_ref[...] = _bdot(a_ref[...], b_ref[...], dn)

        @pl.when(jnp.logical_and(k > 0, k < nk - 1))
        def _():
            acc_ref[...] += _bdot(a_ref[...], b_ref[...], dn)

        @pl.when(k == nk - 1)
        def _():
            finish(acc_ref[...] + _bdot(a_ref[...], b_ref[...], dn))

    a_spec = pl.BlockSpec((tk, tm), lambda i, j, k: (k, i)) if mode == "tn" else pl.BlockSpec((tm, tk), lambda i, j, k: (i, k))
    b_spec = pl.BlockSpec((tn, tk), lambda i, j, k: (j, k)) if mode == "nt" else pl.BlockSpec((tk, tn), lambda i, j, k: (k, j + jb))
    o_spec = pl.BlockSpec((tm, tn), lambda i, j, k: (i, j))
    res = _call(body, [a, b] + list(extras), name=name, grid=(M // tm, N // tn, nk),
                out_shape=tuple(jax.ShapeDtypeStruct((M, N), dt) for dt in out_dtypes),
                in_specs=[a_spec, b_spec] + [o_spec] * ne, out_specs=[o_spec] * no,
                scratch_shapes=[pltpu.VMEM((tm, tn), F32)] if nk > 1 else [], comm=comm)
    return res[0] if single else res


def _rowwise(fn, row_ins, bcast_ins, row_outs, acc_outs, name, rt=256, comm=None):
    L = row_ins[0][0].shape[-2]
    rt = _tile(L, rt, 16)
    nr, nb, no = len(row_ins), len(bcast_ins), len(row_outs)

    def body(*refs):
        i = pl.program_id(0)
        vals = [r[...] for r in refs[:nr + nb]]
        outs, accs = fn(*vals)
        for r, v in zip(refs[nr + nb:nr + nb + no], outs):
            r[...] = v.astype(r.dtype)
        acc_refs = refs[nr + nb + no:]

        @pl.when(i == 0)
        def _():
            for r in acc_refs:
                r[...] = jnp.zeros_like(r)

        for r, v in zip(acc_refs, accs):
            r[...] += v

    in_specs = []
    for spec in row_ins:
        w, cb = spec[1], spec[2]
        if len(spec) == 4:
            in_specs.append(pl.BlockSpec((None, rt, w), functools.partial(lambda i, cb, ld: (ld, i, cb), cb=cb, ld=spec[3])))
        else:
            in_specs.append(pl.BlockSpec((rt, w), functools.partial(lambda i, cb: (i, cb), cb=cb)))
    in_specs += [pl.BlockSpec(b.shape, lambda i: (0, 0)) for b in bcast_ins]
    out_specs = [pl.BlockSpec((rt, w), lambda i: (i, 0)) for w, _ in row_outs]
    out_specs += [pl.BlockSpec(s, lambda i: (0, 0)) for s in acc_outs]
    out_shape = [jax.ShapeDtypeStruct((L, w), dt) for w, dt in row_outs] + [jax.ShapeDtypeStruct(s, F32) for s in acc_outs]
    return _call(body, [s[0] for s in row_ins] + list(bcast_ins), name=name, grid=(L // rt,), out_shape=tuple(out_shape),
                 in_specs=in_specs, out_specs=out_specs, comm=comm)


def _whole(fn, ins, out_shapes, name):
    def body(*refs):
        outs = fn(*[r[...] for r in refs[:len(ins)]])
        for r, v in zip(refs[len(ins):], outs):
            r[...] = v.astype(r.dtype)

    return _call(body, list(ins), name=name, out_shape=tuple(jax.ShapeDtypeStruct(s, dt) for s, dt in out_shapes))


def _silu(x):
    return x * jax.nn.sigmoid(x)


def _rms(x, g):
    return (x * lax.rsqrt(jnp.mean(x * x, axis=-1, keepdims=True) + EPS)) * g


def _modnorm(x, g, shift, scale):
    return _rms(x, g) * (1.0 + scale) + shift


def _adamw(w, g, m, v):
    m = ADAM_B1 * m + (1.0 - ADAM_B1) * g
    v = ADAM_B2 * v + (1.0 - ADAM_B2) * jnp.square(g)
    m_hat = m / (1.0 - ADAM_B1 ** ADAM_STEP)
    v_hat = v / (1.0 - ADAM_B2 ** ADAM_STEP)
    delta = -ADAM_LR * (m_hat / (jnp.sqrt(v_hat) + ADAM_EPS) + ADAM_WD * w)
    return delta, m, v


def _lower_bound(lg):
    e = jnp.exp(lg - jnp.max(lg, axis=0, keepdims=True))
    return e[0:1] / jnp.sum(e, axis=0, keepdims=True)


def _hg_chunk(hq, hf, hi, lb, st):
    C = hq.shape[0]
    row = lax.broadcasted_iota(jnp.int32, (C, C), 0)
    col = lax.broadcasted_iota(jnp.int32, (C, C), 1)
    tri = row >= col
    sg = jax.nn.sigmoid(hf)
    f = lb + (1.0 - lb) * sg
    lf = jnp.log(f)
    k = 1.0 - f
    q = _silu(hq)
    b = _dot(tri.astype(F32), lf, NN, precision=HIGHEST)
    m = b[C // 2 - 1:C // 2]
    bl = b[C - 1:C]
    e_qm, e_km, e_kl, e_q = jnp.exp(b - m), jnp.exp(m - b), jnp.exp(bl - b), jnp.exp(b)
    qe, ke, kd, qb = q * e_qm, k * e_km, k * e_kl, q * e_q
    sc = jnp.where(tri, _bdot(qe, ke, NT), 0.0)
    o = _bdot(sc, hi, NN) + _bdot(qb, st, NT)
    dec = jnp.exp(bl)
    st_next = st * dec + _bdot(hi, kd, TN)
    return o, st_next, dict(tri=tri, sg=sg, f=f, k=k, q=q, qe=qe, ke=ke, kd=kd, qb=qb, sc=sc, dec=dec,
                            e_qm=e_qm, e_km=e_km, e_kl=e_kl, e_q=e_q)


def _hg_out(o, hgate, gout):
    return _rms(o, gout) * _silu(hgate)


HG_GROUP = 16


def _hgrn_fwd(p4, lb_logits, gout, H, comm=None):
    L = p4.shape[0]
    C = HG_CHUNK
    GR = _tile(L // C, HG_GROUP, 1)
    T = GR * C
    N = L // T

    def body(hq_ref, hf_ref, hi_ref, hg_ref, lg_ref, gout_ref, o_ref, s_ref, st_ref):
        @pl.when(pl.program_id(1) == 0)
        def _():
            st_ref[...] = jnp.zeros_like(st_ref)

        lb = _lower_bound(lg_ref[...])
        st = st_ref[...]
        for ci in range(GR):
            rows = pl.ds(ci * C, C)
            s_ref[0, ci] = st
            o, st, _ = _hg_chunk(hq_ref[rows, :], hf_ref[rows, :], hi_ref[rows, :], lb, st)
            o_ref[rows, :] = _hg_out(o, hg_ref[rows, :], gout_ref[...]).astype(o_ref.dtype)
        st_ref[...] = st

    blk = lambda s: pl.BlockSpec((T, HG_DK), functools.partial(lambda h, n, s: (n, s * H + h), s=s))
    return _call(
        body, [p4, p4, p4, p4, lb_logits, gout], name="hgrn_fwd", grid=(H, N),
        out_shape=(jax.ShapeDtypeStruct((L, H * HG_DK), BF16), jax.ShapeDtypeStruct((H, N * GR, HG_DK, HG_DK), F32)),
        in_specs=[blk(0), blk(1), blk(2), blk(3), pl.BlockSpec((2, HG_DK), lambda h, n: (0, h)),
                  pl.BlockSpec((1, HG_DK), lambda h, n: (0, 0))],
        out_specs=(pl.BlockSpec((T, HG_DK), lambda h, n: (n, h)),
                   pl.BlockSpec((1, GR, HG_DK, HG_DK), lambda h, n: (h, n, 0, 0))),
        scratch_shapes=[pltpu.VMEM((HG_DK, HG_DK), F32)], comm=comm)


def _hgrn_bwd(p4, lb_logits, gout, s_all, d_out, H, comm=None):
    L = p4.shape[0]
    C = HG_CHUNK
    GR = _tile(L // C, HG_GROUP, 1)
    T = GR * C
    N = L // T

    def body(hq_ref, hf_ref, hi_ref, hg_ref, lg_ref, gout_ref, s_ref, do_ref,
             dq_ref, df_ref, di_ref, dg_ref, dlb_ref, dgo_ref, dst_ref):
        @pl.when(pl.program_id(1) == 0)
        def _():
            dst_ref[...] = jnp.zeros_like(dst_ref)
            dlb_ref[...] = jnp.zeros_like(dlb_ref)

        @pl.when(jnp.logical_and(pl.program_id(0) == 0, pl.program_id(1) == 0))
        def _():
            dgo_ref[...] = jnp.zeros_like(dgo_ref)

        lb = _lower_bound(lg_ref[...])
        dst = dst_ref[...]
        d_lb = jnp.zeros((1, HG_DK), F32)
        d_go = jnp.zeros((1, HG_DK), F32)
        for ci in reversed(range(GR)):
            rows = pl.ds(ci * C, C)
            dst, d_lb_c, d_go_c = chunk_bwd(rows, lb, s_ref[0, ci], dst, hq_ref, hf_ref, hi_ref, hg_ref, gout_ref, do_ref,
                                            dq_ref, df_ref, di_ref, dg_ref)
            d_lb += d_lb_c
            d_go += d_go_c
        dst_ref[...] = dst
        dlb_ref[...] += d_lb
        dgo_ref[...] += d_go

    def chunk_bwd(rows, lb, st, dst_next, hq_ref, hf_ref, hi_ref, hg_ref, gout_ref, do_ref, dq_ref, df_ref, di_ref, dg_ref):
        hq, hf, hi, hgate = hq_ref[rows, :], hf_ref[rows, :], hi_ref[rows, :], hg_ref[rows, :]
        o, _, t = _hg_chunk(hq, hf, hi, lb, st)
        _, out_vjp = jax.vjp(_hg_out, o, hgate, gout_ref[...])
        do, d_hgate, d_gout = out_vjp(do_ref[rows, :])
        tri = t["tri"]
        dsc = jnp.where(tri, _bdot(do, hi, NT), 0.0)
        dv = _bdot(t["sc"], do, TN) + _bdot(t["kd"], dst_next, NT)
        dqe = _bdot(dsc, t["ke"], NN)
        dke = _bdot(dsc, t["qe"], TN)
        dqb = _bdot(do, st, NN)
        dkd = _bdot(hi, dst_next, NN)
        ddec = jnp.sum(dst_next * st, axis=0, keepdims=True)
        dst_prev = _bdot(do, t["qb"], TN) + dst_next * t["dec"]
        dq = dqe * t["e_qm"] + dqb * t["e_q"]
        dk = dke * t["e_km"] + dkd * t["e_kl"]
        tq, tk, td, tb = dqe * t["qe"], dke * t["ke"], dkd * t["kd"], dqb * t["qb"]
        db = tq - tk - td + tb
        dm = jnp.sum(tk - tq, axis=0, keepdims=True)
        dbl = jnp.sum(td, axis=0, keepdims=True) + ddec * t["dec"]
        rowi = lax.broadcasted_iota(jnp.int32, (C, HG_DK), 0)
        db = db + jnp.where(rowi == C // 2 - 1, dm, 0.0) + jnp.where(rowi == C - 1, dbl, 0.0)
        dlf = _dot(tri.astype(F32), db, TN, precision=HIGHEST)
        df = dlf / t["f"] - dk
        sg = t["sg"]
        df_ref[rows, :] = (df * (1.0 - lb) * sg * (1.0 - sg)).astype(df_ref.dtype)
        sq = jax.nn.sigmoid(hq)
        dq_ref[rows, :] = (dq * (sq * (1.0 + hq * (1.0 - sq)))).astype(dq_ref.dtype)
        di_ref[rows, :] = dv.astype(di_ref.dtype)
        dg_ref[rows, :] = d_hgate.astype(dg_ref.dtype)
        return dst_prev, jnp.sum(df * (1.0 - sg), axis=0, keepdims=True), d_gout

    blk = lambda s: pl.BlockSpec((T, HG_DK), functools.partial(lambda h, n, s: (N - 1 - n, s * H + h), s=s))
    oblk = pl.BlockSpec((T, HG_DK), lambda h, n: (N - 1 - n, h))
    vec = pl.BlockSpec((1, HG_DK), lambda h, n: (0, h))
    W = H * HG_DK
    return _call(
        body, [p4, p4, p4, p4, lb_logits, gout, s_all, d_out], name="hgrn_bwd", grid=(H, N),
        out_shape=tuple([jax.ShapeDtypeStruct((L, W), BF16)] * 4 + [jax.ShapeDtypeStruct((1, W), F32), jax.ShapeDtypeStruct((1, HG_DK), F32)]),
        in_specs=[blk(0), blk(1), blk(2), blk(3), pl.BlockSpec((2, HG_DK), lambda h, n: (0, h)),
                  pl.BlockSpec((1, HG_DK), lambda h, n: (0, 0)),
                  pl.BlockSpec((1, GR, HG_DK, HG_DK), lambda h, n: (h, N - 1 - n, 0, 0)), oblk],
        out_specs=(oblk, oblk, oblk, oblk, vec, pl.BlockSpec((1, HG_DK), lambda h, n: (0, 0))),
        scratch_shapes=[pltpu.VMEM((HG_DK, HG_DK), F32)], comm=comm)


def _bucket_ids():
    i = jnp.arange(AT_BLOCK, dtype=jnp.int32)[:, None]
    j = jnp.arange(2 * AT_BLOCK, dtype=jnp.int32)[None, :]
    n = jnp.maximum(i - j + AT_BLOCK, 0)
    nf = jnp.maximum(n, 1).astype(F32)
    large = MAX_EXACT + (jnp.log(nf / MAX_EXACT) / math.log(MAX_DISTANCE / MAX_EXACT) * (N_BUCKETS - MAX_EXACT)).astype(jnp.int32)
    large = jnp.minimum(large, N_BUCKETS - 1)
    return jnp.where(n < MAX_EXACT, n, large).reshape(1, -1)


def _onehot(bucket):
    ids = lax.broadcasted_iota(jnp.int32, (N_BUCKETS, bucket.shape[1]), 0)
    return (ids == bucket).astype(F32)


def _attn_probs(qn, kpn, kcn, bias_g, sink, first, scale):
    rows = qn.shape[0]
    i = jnp.bitwise_and(lax.broadcasted_iota(jnp.int32, (rows, AT_BLOCK), 0), AT_BLOCK - 1)
    j = lax.broadcasted_iota(jnp.int32, (rows, AT_BLOCK), 1)
    lp = _bdot(qn, kpn, NT) * scale + bias_g[:, :AT_BLOCK]
    lc = _bdot(qn, kcn, NT) * scale + bias_g[:, AT_BLOCK:]
    lp = jnp.where(jnp.logical_and(j > i, jnp.logical_not(first)), lp, NEG_INF)
    lc = jnp.where(j <= i, lc, NEG_INF)
    m = jnp.maximum(jnp.maximum(jnp.max(lp, axis=-1, keepdims=True), jnp.max(lc, axis=-1, keepdims=True)), sink)
    pp, pc, ps = jnp.exp(lp - m), jnp.exp(lc - m), jnp.exp(sink - m)
    den = jnp.sum(pp, axis=-1, keepdims=True) + jnp.sum(pc, axis=-1, keepdims=True) + ps
    return pp / den, pc / den, ps / den


def _sink_rows(sk_ref, G):
    head = lax.broadcasted_iota(jnp.int32, (G * AT_BLOCK, 1), 0) // AT_BLOCK
    sink = jnp.zeros((G * AT_BLOCK, 1), F32)
    for g in range(G):
        sink = jnp.where(head == g, sk_ref[0, g:g + 1, :], sink)
    return sink


def _attn_fwd(q_t, kp, vp, qg, kg, sinks, bias, KVH, comm=None):
    AH, L, DH = q_t.shape
    G = AH // KVH
    NB = L // AT_BLOCK
    scale = DH ** -0.5

    def body(q_ref, kp_ref, kc_ref, vp_ref, vc_ref, qg_ref, kg_ref, sk_ref, b_ref, o_ref):
        first = pl.program_id(1) == 0
        kpn, kcn = _rms(kp_ref[0], kg_ref[...]), _rms(kc_ref[0], kg_ref[...])
        qn = _rms(q_ref[...].reshape(G * AT_BLOCK, DH), qg_ref[...])
        sink = _sink_rows(sk_ref, G)
        pp, pc, _ = _attn_probs(qn, kpn, kcn, b_ref[...].reshape(G * AT_BLOCK, 2 * AT_BLOCK), sink, first, scale)
        o = _bdot(pp, vp_ref[0], NN) + _bdot(pc, vc_ref[0], NN)
        o_ref[...] = o.reshape(G, AT_BLOCK, DH).astype(o_ref.dtype)

    kblk = lambda off: pl.BlockSpec((1, AT_BLOCK, DH), functools.partial(lambda h, n, off: (h, n + off, 0), off=off))
    return _call(
        body, [q_t, kp, kp, vp, vp, qg, kg, sinks, bias], name="attn_fwd", grid=(KVH, NB),
        out_shape=jax.ShapeDtypeStruct((AH, L, DH), BF16),
        in_specs=[pl.BlockSpec((G, AT_BLOCK, DH), lambda h, n: (h, n, 0)), kblk(0), kblk(1), kblk(0), kblk(1),
                  pl.BlockSpec((1, DH), lambda h, n: (0, 0)), pl.BlockSpec((1, DH), lambda h, n: (0, 0)),
                  pl.BlockSpec((1, G, 1), lambda h, n: (h, 0, 0)),
                  pl.BlockSpec((G, AT_BLOCK, 2 * AT_BLOCK), lambda h, n: (h, 0, 0))],
        out_specs=pl.BlockSpec((G, AT_BLOCK, DH), lambda h, n: (h, n, 0)), comm=comm)


def _attn_bwd(q_t, kp, vp, qg, kg, sinks, bias, do_t, KVH, comm=None):
    AH, L, DH = q_t.shape
    G = AH // KVH
    NB = L // AT_BLOCK
    B = AT_BLOCK
    scale = DH ** -0.5

    def body(q_ref, kp_ref, kc_ref, vp_ref, vc_ref, qg_ref, kg_ref, sk_ref, b_ref, do_ref,
             dq_ref, dk_ref, dv_ref, dqg_ref, dkg_ref, dsk_ref, db_ref):
        n = pl.program_id(1)
        first = n == 0

        @pl.when(first)
        def _():
            for r in (dk_ref, dv_ref, dsk_ref, db_ref):
                r[...] = jnp.zeros_like(r)

        @pl.when(jnp.logical_and(first, pl.program_id(0) == 0))
        def _():
            dqg_ref[...] = jnp.zeros_like(dqg_ref)
            dkg_ref[...] = jnp.zeros_like(dkg_ref)

        kp_raw, kc_raw, kgv, qgv = kp_ref[0], kc_ref[0], kg_ref[...], qg_ref[...]
        kpn, kp_vjp = jax.vjp(_rms, kp_raw, kgv)
        kcn, kc_vjp = jax.vjp(_rms, kc_raw, kgv)
        qn, q_vjp = jax.vjp(_rms, q_ref[...].reshape(G * B, DH), qgv)
        pp, pc, ps = _attn_probs(qn, kpn, kcn, b_ref[...].reshape(G * B, 2 * B), _sink_rows(sk_ref, G), first, scale)
        do = do_ref[...].reshape(G * B, DH)
        dvp = _bdot(pp, do, TN)
        dvc = _bdot(pc, do, TN)
        dpp = _bdot(do, vp_ref[0], NT)
        dpc = _bdot(do, vc_ref[0], NT)
        dsum = jnp.sum(dpp * pp, axis=-1, keepdims=True) + jnp.sum(dpc * pc, axis=-1, keepdims=True)
        dlp = pp * (dpp - dsum)
        dlc = pc * (dpc - dsum)
        dsk_ref[0] += jnp.sum((-ps * dsum).reshape(G, B, 1), axis=1)
        db_ref[:, :, :B] += dlp.reshape(G, B, B)
        db_ref[:, :, B:] += dlc.reshape(G, B, B)
        dlp, dlc = dlp * scale, dlc * scale
        dqn = _bdot(dlp, kpn, NN) + _bdot(dlc, kcn, NN)
        dq_raw, dqg = q_vjp(dqn)
        dq_ref[...] = dq_raw.reshape(G, B, DH).astype(dq_ref.dtype)
        dkp_raw, dkg_p = kp_vjp(_bdot(dlp, qn, TN))
        dkc_raw, dkg_c = kc_vjp(_bdot(dlc, qn, TN))
        r0 = pl.multiple_of(n * B, B)
        r1 = pl.multiple_of(n * B + B, B)
        dk_ref[0, pl.ds(r0, B), :] += dkp_raw
        dk_ref[0, pl.ds(r1, B), :] += dkc_raw
        dv_ref[0, pl.ds(r0, B), :] += dvp
        dv_ref[0, pl.ds(r1, B), :] += dvc
        dqg_ref[...] += dqg
        dkg_ref[...] += dkg_p + dkg_c

    kblk = lambda off: pl.BlockSpec((1, B, DH), functools.partial(lambda h, n, off: (h, n + off, 0), off=off))
    qblk = pl.BlockSpec((G, B, DH), lambda h, n: (h, n, 0))
    accblk = pl.BlockSpec((1, L + B, DH), lambda h, n: (h, 0, 0))
    vecblk = pl.BlockSpec((1, DH), lambda h, n: (0, 0))
    return _call(
        body, [q_t, kp, kp, vp, vp, qg, kg, sinks, bias, do_t], name="attn_bwd", grid=(KVH, NB),
        out_shape=(jax.ShapeDtypeStruct((AH, L, DH), BF16), jax.ShapeDtypeStruct((KVH, L + B, DH), F32),
                   jax.ShapeDtypeStruct((KVH, L + B, DH), F32), jax.ShapeDtypeStruct((1, DH), F32),
                   jax.ShapeDtypeStruct((1, DH), F32), jax.ShapeDtypeStruct((KVH, G, 1), F32),
                   jax.ShapeDtypeStruct((AH, B, 2 * B), F32)),
        in_specs=[qblk, kblk(0), kblk(1), kblk(0), kblk(1),
                  pl.BlockSpec((1, DH), lambda h, n: (0, 0)), pl.BlockSpec((1, DH), lambda h, n: (0, 0)),
                  pl.BlockSpec((1, G, 1), lambda h, n: (h, 0, 0)),
                  pl.BlockSpec((G, B, 2 * B), lambda h, n: (h, 0, 0)), qblk],
        out_specs=(qblk, accblk, accblk, vecblk, vecblk, pl.BlockSpec((1, G, 1), lambda h, n: (h, 0, 0)),
                   pl.BlockSpec((G, B, 2 * B), lambda h, n: (h, 0, 0))), comm=comm)


def _heads_first(t, nh):
    L = t.shape[0]
    return jnp.transpose(t.reshape(L, nh, t.shape[1] // nh), (1, 0, 2))


def _heads_last(t):
    nh, L, dh = t.shape
    return jnp.transpose(t, (1, 0, 2)).reshape(L, nh * dh)


def _softmax0(lg):
    e = jnp.exp(lg - jnp.max(lg, axis=0, keepdims=True))
    return e[0:1] / jnp.sum(e, axis=0, keepdims=True)


def _ada_update_call(fn, c_all, d_cols, w, m, v, rt):
    D, n = w.shape

    def body(c_ref, d_ref, w_ref, m_ref, v_ref, g_out, dl_out, m_out, v_out):
        outs, _ = fn(c_ref[...], d_ref[...], w_ref[...], m_ref[...], v_ref[...])
        for r, val in zip((g_out, dl_out, m_out, v_out), outs):
            r[...] = val

    wblk = pl.BlockSpec((rt, n), lambda i: (i, 0))
    return _call(
        body, [c_all, d_cols, w, m, v], name="update_ada", grid=(D // rt,), out_shape=tuple([jax.ShapeDtypeStruct((D, n), F32)] * 4),
        in_specs=[pl.BlockSpec((N_DEV, rt), lambda i: (0, i)), pl.BlockSpec((N_DEV, n), lambda i: (0, 0)), wblk, wblk, wblk],
        out_specs=(wblk, wblk, wblk, wblk))


def kernel(x, c, w_ada, b_ada, norm1_g, norm2_g, w_in, hg_lb_logits, hg_out_norm_g, q_norm_g, k_norm_g, attn_sinks, rel_bias_table, w_branch_hg, w_branch_attn, w_out, w_ff1, w_ff2, loss_target, m_w_ada, m_b_ada, m_norm1_g, m_norm2_g, m_w_in, m_hg_lb_logits, m_hg_out_norm_g, m_q_norm_g, m_k_norm_g, m_attn_sinks, m_rel_bias_table, m_w_branch_hg, m_w_branch_attn, m_w_out, m_w_ff1, m_w_ff2, v_w_ada, v_b_ada, v_norm1_g, v_norm2_g, v_w_in, v_hg_lb_logits, v_hg_out_norm_g, v_q_norm_g, v_k_norm_g, v_attn_sinks, v_rel_bias_table, v_w_branch_hg, v_w_branch_attn, v_w_out, v_w_ff1, v_w_ff2):
    cc = lax.axis_index("c")
    me = 4 * lax.axis_index("x") + 2 * lax.axis_index("y") + cc
    x2 = x[0]
    tgt = loss_target[0]
    L, D = x2.shape
    HGW = hg_lb_logits.shape[1]
    H = HGW // HG_DK
    AH = attn_sinks.shape[1]
    DH = q_norm_g.shape[1]
    ATW = AH * DH
    BW = w_in.shape[2]
    INW = BW * N_DEV
    A = BW // LANES
    assert BW == LANES * A + LANES // 2
    KVW = (INW - 4 * HGW - ATW - 2 * D) // 2
    KVH = KVW // DH
    G = AH // KVH
    ADA_N = w_ada.shape[2]
    PAIR = 2 * A + 1

    w_in_b = w_in[0].astype(BF16)
    src_in = jnp.where(cc == 0, jnp.pad(w_in_b, ((0, 0), (0, LANES // 2))), jnp.pad(w_in_b, ((0, 0), (LANES // 2, 0))))
    w_in_gapped, w_in_mid = _ag_w_in(src_in, A, D, INW)
    w_in_full = _patch_mid(w_in_gapped, w_in_mid, A)

    c_all = _gather_small(c, me, "gather_c")[:, 0, :]
    b_cols = lax.dynamic_slice(b_ada, (0, me * ADA_N), (1, ADA_N))
    (ada_cols,) = _whole(lambda cv, w, b: (_bdot(_silu(cv), w, NN) + b,), [c_all, w_ada[0], b_cols],
                         [((N_DEV, ADA_N), F32)], "ada_fwd")
    ada_all = _gather_small(ada_cols, me, "gather_ada")
    ada_row = lax.dynamic_slice(ada_all, (0, me, 0), (N_DEV, 1, ADA_N)).reshape(1, 6 * D)
    shift1, scale1, gate1, shift2, scale2, gate2 = [ada_row[:, i * D:(i + 1) * D] for i in range(6)]

    wnames = ("bhg", "bat", "out", "ff1", "ff2")
    waxis = dict(zip(wnames, (1, 1, 0, 1, 0)))
    wsrc = dict(zip(wnames, (w_branch_hg, w_branch_attn, w_out, w_ff1, w_ff2)))
    wblk = {k: wsrc[k][0].astype(BF16) for k in wnames}
    wf = {}

    (h,) = _rowwise(lambda xv, g, sh, sc: ((_modnorm(xv, g, sh, sc),), ()), [(x2, D, 0)], [norm1_g, shift1, scale1],
                    [(D, BF16)], [], "norm1")
    o4, oa = 4 * HGW, 4 * HGW + ATW + 2 * KVW
    r1, r2, ro = wblk["ff1"].shape[0], wblk["ff2"].shape[0], wblk["out"].shape[0]
    cm = _Comm()
    hs = {k: _ag_ici(cm, wblk[k], waxis[k]) for k in ("bhg", "bat")}
    hs["out"] = _ag_ici(cm, wblk["out"], waxis["out"], rows=(0, ro // 2))
    p4 = _mm(h, w_in_full, "nn", F32, "proj_hg", n=o4, comm=cm)
    half = {k: cm.result(hs[k]) for k in hs}
    cm = _Comm()
    hs = {"out": _ag_ici(cm, wblk["out"], waxis["out"], rows=(ro // 2, ro), into=half["out"])}
    pa = _mm(h, w_in_full, "nn", F32, "proj_at", b_off=o4, n=oa - o4, comm=cm)
    half["out"] = cm.result(hs["out"])
    cm = _Comm()
    hs = {k: _ag_d2d(cm, half[k], waxis[k]) for k in ("bhg", "bat")}
    hs["ff2"] = _ag_ici(cm, wblk["ff2"], waxis["ff2"], rows=(0, r2 // 4))
    pg = _mm(h, w_in_full, "nn", F32, "proj_gate", b_off=oa, n=INW - oa, comm=cm)
    wf["bhg"], wf["bat"], half["ff2"] = (cm.result(hs[k]) for k in ("bhg", "bat", "ff2"))

    cm = _Comm()
    hs = {"out": _ag_d2d(cm, half["out"], waxis["out"]), "ff1": _ag_ici(cm, wblk["ff1"], waxis["ff1"], rows=(0, r1 // 2))}
    o_hg, s_all = _hgrn_fwd(p4, hg_lb_logits, hg_out_norm_g, H, comm=cm)
    wf["out"], half["ff1"] = cm.result(hs["out"]), cm.result(hs["ff1"])

    bucket = _bucket_ids()
    (bias_flat,) = _whole(lambda tb, bk: (_dot(tb, _onehot(bk), TN, precision=HIGHEST),), [rel_bias_table, bucket],
                          [((AH, AT_BLOCK * 2 * AT_BLOCK), F32)], "bias_fwd")
    bias = bias_flat.reshape(AH, AT_BLOCK, 2 * AT_BLOCK)
    q_t = _heads_first(pa[:, :ATW], AH)
    pad = lambda t: jnp.pad(t, ((0, 0), (AT_BLOCK, 0), (0, 0)))
    kp = pad(_heads_first(pa[:, ATW:ATW + KVW], KVH))
    vp = pad(_heads_first(pa[:, ATW + KVW:], KVH))
    sinks3 = attn_sinks.reshape(KVH, G, 1)
    cm = _Comm()
    hs = {"ff1": _ag_ici(cm, wblk["ff1"], waxis["ff1"], rows=(r1 // 2, r1), into=half["ff1"])}
    o_at = _heads_last(_attn_fwd(q_t, kp, vp, q_norm_g, k_norm_g, sinks3, bias, KVH, comm=cm))
    half["ff1"] = cm.result(hs["ff1"])

    bh = _mm(o_hg, wf["bhg"], "nn", F32, "branch_hg")
    ba = _mm(o_at, wf["bat"], "nn", F32, "branch_at")

    def merge_fn(bhv, bav, ghg, gat):
        return jax.nn.sigmoid(ghg) * bhv + jax.nn.sigmoid(gat) * bav

    cm = _Comm()
    hs = {"ff1": _ag_d2d(cm, half["ff1"], waxis["ff1"])}
    (merged,) = _rowwise(lambda *a: ((merge_fn(*a),), ()), [(bh, D, 0), (ba, D, 0), (pg, D, 0), (pg, D, 1)], [],
                         [(D, BF16)], [], "merge", comm=cm)
    wf["ff1"] = cm.result(hs["ff1"])
    cm = _Comm()
    hs = {"ff2": _ag_ici(cm, wblk["ff2"], waxis["ff2"], rows=(r2 // 4, 3 * r2 // 8), into=half["ff2"])}
    mo = _mm(merged, wf["out"], "nn", F32, "out_proj", comm=cm)
    half["ff2"] = cm.result(hs["ff2"])

    def resid1(xv, mov, g1, g2n, sh, sc):
        x1v = xv + g1 * mov
        return (x1v, _modnorm(x1v, g2n, sh, sc)), ()

    cm = _Comm()
    hs = {"ff2": _ag_ici(cm, wblk["ff2"], waxis["ff2"], rows=(3 * r2 // 8, r2 // 2), into=half["ff2"])}
    x1, h2 = _rowwise(resid1, [(x2, D, 0), (mo, D, 0)], [gate1, norm2_g, shift2, scale2], [(D, F32), (D, BF16)], [], "resid1",
                      comm=cm)
    half["ff2"] = cm.result(hs["ff2"])
    cm = _Comm()
    hs = {"ff2": _ag_ici(cm, wblk["ff2"], waxis["ff2"], rows=(r2 // 2, r2), into=half["ff2"])}
    u, act = _mm(h2, wf["ff1"], "nn", (F32, BF16), "ff1", comm=cm, epi=lambda r: (r, jnp.square(jnp.maximum(r, 0.0))))
    half["ff2"] = cm.result(hs["ff2"])
    cm = _Comm()
    hs = {"ff2": _ag_d2d(cm, half["ff2"], waxis["ff2"])}
    _call(lambda: None, [], name="ag_d2d_ff2", out_shape=(), comm=cm)
    wf["ff2"] = cm.result(hs["ff2"])
    ff = _mm(act, wf["ff2"], "nn", F32, "ff2")

    def loss_fn(x1v, ffv, tv, g2):
        e = x1v + g2 * ffv - tv
        dy = e * (1.0 / D)
        return (dy, dy * g2), (jnp.sum(e * e, axis=0, keepdims=True), jnp.sum(dy * ffv, axis=0, keepdims=True))

    dy, d_ff, sq_sum, d_gate2 = _rowwise(loss_fn, [(x1, D, 0), (ff, D, 0), (tgt, D, 0)], [gate2],
                                         [(D, F32), (D, BF16)], [(1, D), (1, D)], "loss")
    loss = lax.psum(jnp.sum(sq_sum) * (0.5 / D), ("x", "y", "c"))

    owner_base = jnp.stack([me ^ r for r in CHIP_RELS]).astype(jnp.int32)
    gw, recv1, part, recv2 = {}, {}, {}, {}
    gw["ff2"] = _mm(act, d_ff, "tn", BF16, "dw_ff2")
    cm = _Comm()
    hh = _rs_d2d(cm, gw["ff2"], waxis["ff2"])
    d_u = _mm(d_ff, wf["ff2"], "nt", BF16, "d_act", comm=cm, extras=[u], epi=lambda r, uv: (r * (2.0 * jnp.maximum(uv, 0.0)),))
    part["ff2"] = _rs_add(gw["ff2"], cm.result(hh), waxis["ff2"], owner_base, "rs_add_ff2")
    rows_ff2 = part["ff2"].shape[1]
    cm = _Comm()
    hh = _rs_ici(cm, part["ff2"], rows=(0, rows_ff2 // 2))
    gw["ff1"] = _mm(h2, d_u, "tn", BF16, "dw_ff1", comm=cm)
    cm2 = _Comm()
    hh2 = _rs_ici(cm2, part["ff2"], rows=(rows_ff2 // 2, rows_ff2), recv=cm.result(hh))
    hh1 = _rs_d2d(cm2, gw["ff1"], waxis["ff1"])
    d_h2 = _mm(d_u, wf["ff1"], "nt", F32, "d_h2", comm=cm2)
    recv2["ff2"] = cm2.result(hh2)
    part["ff1"] = _rs_add(gw["ff1"], cm2.result(hh1), waxis["ff1"], owner_base, "rs_add_ff1")

    def norm2_bwd(dh2v, x1v, dyv, mov, g2n, sh, sc, g1):
        _, vjp = jax.vjp(_modnorm, x1v, g2n, sh, sc)
        dx, dg, dsh, dsc = vjp(dh2v)
        dx1 = dyv + dx
        return (dx1, dx1 * g1), (dg, dsh, dsc, jnp.sum(dx1 * mov, axis=0, keepdims=True))

    d_x1, d_mo, d_g2n, d_shift2, d_scale2, d_gate1 = _rowwise(
        norm2_bwd, [(d_h2, D, 0), (x1, D, 0), (dy, D, 0), (mo, D, 0)], [norm2_g, shift2, scale2, gate1],
        [(D, F32), (D, BF16)], [(1, D)] * 4, "norm2_bwd")
    gw["out"] = _mm(merged, d_mo, "tn", BF16, "dw_out")
    cm = _Comm()
    hh = _rs_d2d(cm, gw["out"], waxis["out"])
    d_merged = _mm(d_mo, wf["out"], "nt", F32, "d_merged", comm=cm)
    part["out"] = _rs_add(gw["out"], cm.result(hh), waxis["out"], owner_base, "rs_add_out")

    def merge_bwd(dmv, bhv, bav, ghg, gat):
        _, vjp = jax.vjp(merge_fn, bhv, bav, ghg, gat)
        return vjp(dmv), ()

    d_bh, d_ba, d_ghg, d_gat = _rowwise(merge_bwd, [(d_merged, D, 0), (bh, D, 0), (ba, D, 0), (pg, D, 0), (pg, D, 1)], [],
                                        [(D, BF16)] * 4, [], "merge_bwd")
    gw["bhg"] = _mm(o_hg, d_bh, "tn", BF16, "dw_bhg")
    gw["bat"] = _mm(o_at, d_ba, "tn", BF16, "dw_bat")
    cm = _Comm()
    hh = {k: _rs_d2d(cm, gw[k], waxis[k]) for k in ("bhg", "bat")}
    d_ohg = _mm(d_bh, wf["bhg"], "nt", F32, "d_ohg", comm=cm)
    for k in ("bhg", "bat"):
        part[k] = _rs_add(gw[k], cm.result(hh[k]), waxis[k], owner_base, "rs_add_" + k)
    d_oat = _mm(d_ba, wf["bat"], "nt", BF16, "d_oat")

    cm = _Comm()
    hh = {"ff1": _rs_ici(cm, part["ff1"])}
    d_hq, d_hf, d_hi, d_hg, d_lb, d_gout_h = _hgrn_bwd(p4, hg_lb_logits, hg_out_norm_g, s_all, d_ohg, H, comm=cm)
    recv2["ff1"] = cm.result(hh["ff1"])
    cm = _Comm()
    hh = {k: _rs_ici(cm, part[k]) for k in ("out", "bhg", "bat")}
    dq_t, dkp, dvp, d_qg, d_kg, d_sk, d_bias = _attn_bwd(q_t, kp, vp, q_norm_g, k_norm_g, sinks3, bias,
                                                         _heads_first(d_oat, AH), KVH, comm=cm)
    for k in hh:
        recv2[k] = cm.result(hh[k])
    d_aq = _heads_last(dq_t)
    d_ak = _heads_last(dkp[:, AT_BLOCK:, :]).astype(BF16)
    d_av = _heads_last(dvp[:, AT_BLOCK:, :]).astype(BF16)
    d_proj = jnp.concatenate([d_hq, d_hf, d_hi, d_hg, d_aq, d_ak, d_av, d_ghg, d_gat], axis=1)
    gw_in = _mm(h, d_proj, "tn", BF16, "dw_in")

    wm = LANES * A
    cm = _Comm()
    hi_ = cm.inp(gw_in)
    h_main, h_mid = cm.out((4, D, wm), BF16), cm.out((4, D, LANES), BF16)
    for i, r in enumerate(CHIP_RELS):
        def main_view(ref, p, r=r):
            o = p["me"] ^ r ^ 1
            return ref.at[:, pl.ds(pl.multiple_of((PAIR * (o // 2) + (A + 1) * (1 - p["c"])) * LANES, LANES), wm)]

        def mid_view(ref, p, r=r):
            o = p["me"] ^ r
            return ref.at[:, pl.ds(pl.multiple_of((PAIR * (o // 2) + A) * LANES, LANES), LANES)]

        cm.copy(hi_, main_view, h_main, _slot_view(i), 1)
        cm.copy(hi_, mid_view, h_mid, _slot_view(i), 1)
    _call(lambda: None, [], name="rs_d2d_in", out_shape=(), comm=cm)
    chip = jnp.stack([(me ^ r) // 2 for r in CHIP_RELS]).astype(jnp.int32)
    part_main = _rs_add(gw_in, cm.result(h_main), 1, PAIR * chip + (A + 1) * cc, "rs_add_in_main", tw=LANES)
    part_mid = _rs_add(gw_in, cm.result(h_mid), 1, PAIR * chip + A, "rs_add_in_mid", tw=LANES)
    shares = dict(zip(("d_h", "norm1_bwd", "ff1", "ff2", "out", "bhg", "bat"), (26, 8, 10, 10, 5, 3, 2)))
    unit, lo, in_rows = D // sum(shares.values()), 0, {}
    for k_, s_ in shares.items():
        in_rows[k_] = (lo, lo + s_ * unit)
        lo += s_ * unit
    rx_state = {"main": None}

    def ride_in(k_):
        cm_ = _Comm()
        rx_state["h"] = _rs_ici(cm_, part_main, rows=in_rows[k_], recv=rx_state["main"])
        return cm_

    def rode_in(cm_):
        rx_state["main"] = cm_.result(rx_state["h"])

    cm = ride_in("d_h")
    hh_mid = _rs_ici(cm, part_mid)
    d_h = _mm(d_proj, w_in_full, "nt", F32, "d_h", comm=cm)
    rode_in(cm)
    rx_mid = cm.result(hh_mid)

    def norm1_bwd(dhv, xv, dx1v, g1n, sh, sc):
        _, vjp = jax.vjp(_modnorm, xv, g1n, sh, sc)
        dx, dg, dsh, dsc = vjp(dhv)
        return (dx1v + dx,), (dg, dsh, dsc)

    cm = ride_in("norm1_bwd")
    grad_x, d_g1n, d_shift1, d_scale1 = _rowwise(norm1_bwd, [(d_h, D, 0), (x2, D, 0), (d_x1, D, 0)],
                                                 [norm1_g, shift1, scale1], [(D, F32)], [(1, D)] * 3, "norm1_bwd", comm=cm)
    rode_in(cm)

    def sum4(p0, p1, p2, p3):
        return ((p0.astype(F32) + p1.astype(F32)) + p2.astype(F32)) + p3.astype(F32)

    def update_fn(w, m, v, p0, p1, p2, p3):
        g = sum4(p0, p1, p2, p3)
        delta, mn, vn = _adamw(w, g, m, v)
        return (g, delta, mn, vn), ()

    wmv = dict(zip(wnames, ((w_branch_hg, m_w_branch_hg, v_w_branch_hg), (w_branch_attn, m_w_branch_attn, v_w_branch_attn),
                            (w_out, m_w_out, v_w_out), (w_ff1, m_w_ff1, v_w_ff1), (w_ff2, m_w_ff2, v_w_ff2))))
    res = {}
    for k in ("ff1", "ff2", "out", "bhg", "bat"):
        w, m, v = (t[0] for t in wmv[k])
        n = w.shape[1]
        ins = [(t, n, 0) for t in (w, m, v)] + [(part[k], n, 0, 0)] + [(recv2[k], n, 0, i) for i in range(3)]
        cm = ride_in(k)
        res[k] = [t[None] for t in _rowwise(update_fn, ins, [], [(n, F32)] * 4, [], "update_" + k, comm=cm)]
        rode_in(cm)
    rx_main = rx_state["main"]

    g_main, = _rowwise(lambda *p: ((sum4(*p),), ()), [(part_main, wm, 0, 0)] + [(rx_main, wm, 0, i) for i in range(3)], [],
                       [(wm, F32)], [], "sum_in_main")
    g_mid, = _rowwise(lambda *p: ((sum4(*p),), ()), [(part_mid, LANES, 0, 0)] + [(rx_mid, LANES, 0, i) for i in range(3)], [],
                      [(LANES, F32)], [], "sum_in_mid")
    g_in = jnp.where(cc == 0, jnp.concatenate([g_main, g_mid[:, :LANES // 2]], axis=1),
                     jnp.concatenate([g_mid[:, LANES // 2:], g_main], axis=1))

    def update_given(w, m, v, g):
        delta, mn, vn = _adamw(w, g, m, v)
        return (g, delta, mn, vn), ()

    res["in"] = [t[None] for t in _rowwise(update_given, [(t, BW, 0) for t in (w_in[0], m_w_in[0], v_w_in[0], g_in)], [],
                                           [(BW, F32)] * 4, [], "update_in")]

    d_ada_row = jnp.concatenate([d_shift1, d_scale1, d_gate1, d_shift2, d_scale2, d_gate2], axis=1)
    d_ada_all = _gather_small(d_ada_row, me, "gather_dada")[:, 0, :]
    d_ada_cols = lax.dynamic_slice(d_ada_all, (0, me * ADA_N), (N_DEV, ADA_N))

    def ada_update(cv, dav, w, m, v):
        g = _bdot(_silu(cv), dav, TN)
        delta, mn, vn = _adamw(w, g, m, v)
        return (g, delta, mn, vn), ()

    res["ada"] = [t[None] for t in _ada_update_call(ada_update, c_all, d_ada_cols, w_ada[0], m_w_ada[0], v_w_ada[0], _tile(D, 256, 16))]

    d_sinks = d_sk.reshape(1, AH)
    (d_table_t,) = _whole(lambda db, bk: (_dot(db, _onehot(bk), NT, precision=HIGHEST),),
                          [d_bias.reshape(AH, AT_BLOCK * 2 * AT_BLOCK), bucket], [((AH, N_BUCKETS), F32)], "bias_bwd")
    smalls = [d_g1n, d_g2n, d_lb, d_gout_h, d_qg, d_kg, d_sinks, d_table_t.T.reshape(1, N_BUCKETS * AH)]
    widths = [s.shape[1] for s in smalls]
    lanes = [-(-w // LANES) * LANES for w in widths]
    smalls = [jnp.pad(s, ((0, 0), (0, p - w))) for s, w, p in zip(smalls, widths, lanes)]
    packed = _gather_small(jnp.concatenate(smalls, axis=1), me, "gather_small")[:, 0, :]
    offs = [sum(lanes[:i]) for i in range(len(lanes))]

    def small_update(pk, dada, lg, *wmv_flat):
        tot = pk[0:1]
        for d in range(1, N_DEV):
            tot = tot + pk[d:d + 1]
        gb = dada[0:1]
        for d in range(1, N_DEV):
            gb = gb + dada[d:d + 1]
        gs = [tot[:, offs[i]:offs[i] + widths[i]] for i in range(len(widths))]
        _, lb_vjp = jax.vjp(_softmax0, lg)
        (g_lg,) = lb_vjp(gs[2])
        grads = [gb, gs[0], gs[1], g_lg, gs[3], gs[4], gs[5], gs[6], gs[7]]
        outs = []
        for i, g in enumerate(grads):
            w, m, v = wmv_flat[3 * i:3 * i + 3]
            delta, mn, vn = _adamw(w, g, m, v)
            outs += [g, delta, mn, vn]
        return tuple(outs)

    tbl = lambda t: t.reshape(1, N_BUCKETS * AH)
    small_wmv = [(b_ada, m_b_ada, v_b_ada), (norm1_g, m_norm1_g, v_norm1_g), (norm2_g, m_norm2_g, v_norm2_g),
                 (hg_lb_logits, m_hg_lb_logits, v_hg_lb_logits), (hg_out_norm_g, m_hg_out_norm_g, v_hg_out_norm_g),
                 (q_norm_g, m_q_norm_g, v_q_norm_g), (k_norm_g, m_k_norm_g, v_k_norm_g),
                 (attn_sinks, m_attn_sinks, v_attn_sinks),
                 (tbl(rel_bias_table), tbl(m_rel_bias_table), tbl(v_rel_bias_table))]
    flat = [t for trip in small_wmv for t in trip]
    out_shapes = [(trip[0].shape, F32) for trip in small_wmv for _ in range(4)]
    sres = _whole(small_update, [packed, d_ada_all, hg_lb_logits] + flat, out_shapes, "small_update")
    names_small = ("b_ada", "norm1_g", "norm2_g", "lb", "gout", "qg", "kg", "sinks", "table")
    for i, k in enumerate(names_small):
        r = sres[4 * i:4 * i + 4]
        if k == "table":
            r = [t.reshape(N_BUCKETS, AH) for t in r]
        res[k] = r

    order = ("ada", "b_ada", "norm1_g", "norm2_g", "in", "lb", "gout", "qg", "kg", "sinks", "table", "bhg", "bat", "out", "ff1", "ff2")
    outs = [loss, grad_x[None]]
    for j in range(4):
        outs += [res[k][j] for k in order]
    return tuple(outs)
```

```python
import functools
import math

import jax
import jax.numpy as jnp
from jax import lax
from jax.experimental import pallas as pl
from jax.experimental.pallas import tpu as pltpu

F32 = jnp.float32
BF16 = jnp.bfloat16
EPS = 1e-6
NEG_INF = -1e30
HG_DK = 128
HG_CHUNK = 64
AT_BLOCK = 128
N_BUCKETS = 32
MAX_EXACT = 16
MAX_DISTANCE = 128
N_DEV = 8
LANES = 128
VMEM_LIMIT = 56 * 1024 * 1024
ADAM_LR, ADAM_B1, ADAM_B2, ADAM_EPS, ADAM_WD, ADAM_STEP = 0.001, 0.9, 0.999, 1e-08, 0.01, 10
HIGHEST = lax.Precision.HIGHEST
MESH = pl.DeviceIdType.MESH
ANY = pl.BlockSpec(memory_space=pl.ANY)
CHIP_RELS = (0, 4, 2, 6)

NN = (((1,), (0,)), ((), ()))
NT = (((1,), (1,)), ((), ()))
TN = (((0,), (0,)), ((), ()))


def _tile(n, pref, unit):
    if n <= pref:
        return n
    t = (pref // unit) * unit
    while t >= unit:
        if n % t == 0:
            return t
        t -= unit
    return n


def _dot(a, b, dn, precision=None):
    return lax.dot_general(a, b, dn, preferred_element_type=F32, precision=precision)


def _bdot(a, b, dn):
    return _dot(a.astype(BF16), b.astype(BF16), dn)


def _position():
    x, y, c = lax.axis_index("x"), lax.axis_index("y"), lax.axis_index("c")
    return dict(x=x, y=y, c=c, me=4 * x + 2 * y + c)


def _peer_position(p, rel):
    x = 1 - p["x"] if rel & 4 else p["x"]
    y = 1 - p["y"] if rel & 2 else p["y"]
    c = 1 - p["c"] if rel & 1 else p["c"]
    return dict(x=x, y=y, c=c, me=4 * x + 2 * y + c)


class _Comm:
    def __init__(self):
        self.ins, self.outs, self.alias, self.plans, self.res = [], [], {}, [], None

    def inp(self, arr):
        self.ins.append(arr)
        return ("i", len(self.ins) - 1)

    def out(self, shape, dtype, alias=None):
        self.outs.append(jax.ShapeDtypeStruct(tuple(shape), dtype))
        if alias is not None:
            self.alias[alias[1]] = len(self.outs) - 1
        return ("o", len(self.outs) - 1)

    def copy(self, src, src_view, dst, dst_view, rel):
        self.plans.append((src, src_view, dst, dst_view, rel))

    def result(self, handle):
        return self.res[handle[1]]

    def build(self, in_refs, out_refs, send_sems, recv_sems):
        pos = _position()
        ref = lambda h: in_refs[h[1]] if h[0] == "i" else out_refs[h[1]]
        ops = []
        for k, (src, sv, dst, dv, rel) in enumerate(self.plans):
            s = sv(ref(src), pos)
            if rel == 0:
                cp = pltpu.make_async_copy(s, dv(ref(dst), pos), send_sems.at[k])
                ops.append((cp.start, cp.wait))
                continue
            peer = _peer_position(pos, rel)
            mk = lambda d: pltpu.make_async_remote_copy(
                src_ref=s, dst_ref=d, send_sem=send_sems.at[k], recv_sem=recv_sems.at[k],
                device_id=(peer["x"], peer["y"], peer["c"]), device_id_type=MESH)
            out_cp, in_cp = mk(dv(ref(dst), pos)), mk(dv(ref(dst), peer))

            def wait(out_cp=out_cp, in_cp=in_cp):
                out_cp.wait_send()
                in_cp.wait_recv()

            ops.append((out_cp.start, wait))
        return ops


def _call(body, args, *, name, out_shape, in_specs=None, out_specs=None, grid=None, scratch_shapes=(), comm=None,
          prefetch=None, aliases=None):
    single = not isinstance(out_shape, (tuple, list))
    out_shape = (out_shape,) if single else tuple(out_shape)
    n_in, n_out, n_scr = len(args), len(out_shape), len(scratch_shapes)
    vm = pl.BlockSpec(memory_space=pltpu.VMEM)
    in_specs = [vm] * n_in if in_specs is None else list(in_specs)
    out_specs = [vm] * n_out if out_specs is None else (list(out_specs) if isinstance(out_specs, (tuple, list)) else [out_specs])
    n_pf = 0 if prefetch is None else len(prefetch)
    kw = {} if aliases is None else {"input_output_aliases": dict(aliases)}
    if comm is None:
        fn = body
        all_args, all_scratch = list(args), list(scratch_shapes)
    else:
        n_ci, n_co, n_x = len(comm.ins), len(comm.outs), len(comm.plans)

        def fn(*refs):
            pf, refs = refs[:n_pf], refs[n_pf:]
            o_in, c_in = refs[:n_in], refs[n_in:n_in + n_ci]
            o_out = refs[n_in + n_ci:n_in + n_ci + n_out]
            c_out = refs[n_in + n_ci + n_out:n_in + n_ci + n_out + n_co]
            scr = refs[n_in + n_ci + n_out + n_co:]
            ops = comm.build(c_in, c_out, scr[n_scr], scr[n_scr + 1])
            if grid:
                first = functools.reduce(jnp.logical_and, [pl.program_id(i) == 0 for i in range(len(grid))])
                last = functools.reduce(jnp.logical_and, [pl.program_id(i) == g - 1 for i, g in enumerate(grid)])

                @pl.when(first)
                def _():
                    for start, _w in ops:
                        start()
            else:
                for start, _w in ops:
                    start()
            body(*pf, *o_in, *o_out, *scr[:n_scr])
            if grid:
                @pl.when(last)
                def _():
                    for _s, wait in ops:
                        wait()
            else:
                for _s, wait in ops:
                    wait()

        all_args = list(args) + list(comm.ins)
        in_specs = in_specs + [ANY] * n_ci
        out_shape = out_shape + tuple(comm.outs)
        out_specs = out_specs + [ANY] * n_co
        all_scratch = list(scratch_shapes) + [pltpu.SemaphoreType.DMA((n_x,)), pltpu.SemaphoreType.DMA((n_x,))]
        kw["input_output_aliases"] = {n_pf + n_in + i: n_out + o for i, o in comm.alias.items()}
    sem = None if grid is None else ("arbitrary",) * len(grid)
    params = pltpu.CompilerParams(dimension_semantics=sem, vmem_limit_bytes=VMEM_LIMIT)
    if prefetch is None:
        spec = dict(in_specs=in_specs, out_specs=tuple(out_specs), scratch_shapes=all_scratch)
        if grid is not None:
            spec["grid"] = grid
    else:
        spec = dict(grid_spec=pltpu.PrefetchScalarGridSpec(
            num_scalar_prefetch=n_pf, grid=grid, in_specs=in_specs, out_specs=tuple(out_specs), scratch_shapes=all_scratch))
        all_args = list(prefetch) + all_args
    res = pl.pallas_call(fn, name=name, out_shape=out_shape, compiler_params=params, **spec, **kw)(*all_args)
    res = list(res)
    if comm is not None:
        comm.res = res[n_out:]
        res = res[:n_out]
    return res[0] if single else res


def _whole_view(ref, pos):
    return ref


def _block_view(axis, n, index, rows=None):
    def view(ref, pos):
        off = pl.multiple_of(index(pos) * n, n)
        if rows is None:
            return ref.at[:, pl.ds(off, n)] if axis == 1 else ref.at[pl.ds(off, n), :]
        lo, cnt = rows[0], rows[1] - rows[0]
        if axis == 1:
            return ref.at[pl.ds(lo, cnt), pl.ds(off, n)]
        return ref.at[pl.ds(pl.multiple_of(off + lo, 16), cnt), :]
    return view


def _rows_view(rows):
    def view(ref, pos):
        return ref if rows is None else ref.at[pl.ds(rows[0], rows[1] - rows[0]), :]
    return view


def _slot_view(i, rows=None):
    def view(ref, pos):
        return ref.at[i] if rows is None else ref.at[i, pl.ds(rows[0], rows[1] - rows[0]), :]
    return view


def _exchange(items, name):
    cm = _Comm()
    for a, rel in items:
        cm.copy(cm.inp(a), _whole_view, cm.out(a.shape, a.dtype), _whole_view, rel)
    _call(lambda: None, [], name=name, out_shape=(), comm=cm)
    return cm.res


def _gather_small(v, me, name):
    cm = _Comm()
    hi, ho = cm.inp(v), cm.out((N_DEV,) + v.shape, v.dtype)
    for rel in range(N_DEV):
        cm.copy(hi, _whole_view, ho, lambda ref, p: ref.at[p["me"]], rel)
    _call(lambda: None, [], name=name, out_shape=(), comm=cm)
    return cm.result(ho)


def _ag_ici(cm, blk, axis, rows=None, into=None):
    n = blk.shape[axis]
    shape = list(blk.shape)
    shape[axis] = n * N_DEV
    hi = cm.inp(blk)
    ho = cm.out(shape, blk.dtype) if into is None else cm.out(shape, blk.dtype, alias=cm.inp(into))
    own = _block_view(axis, n, lambda p: p["me"], rows)
    for rel in CHIP_RELS:
        cm.copy(hi, _rows_view(rows), ho, own, rel)
    return ho


def _ag_d2d(cm, full, axis):
    n = full.shape[axis] // N_DEV
    hi = cm.inp(full)
    ho = cm.out(full.shape, full.dtype, alias=hi)
    for r in CHIP_RELS:
        v = _block_view(axis, n, functools.partial(lambda p, r: p["me"] ^ r, r=r))
        cm.copy(hi, v, ho, v, 1)
    return ho


def _rs_d2d(cm, gw, axis):
    n = gw.shape[axis] // N_DEV
    shape = list(gw.shape)
    shape[axis] = n
    hi, ho = cm.inp(gw), cm.out([4] + shape, gw.dtype)
    for i, r in enumerate(CHIP_RELS):
        cm.copy(hi, _block_view(axis, n, functools.partial(lambda p, r: p["me"] ^ r ^ 1, r=r)), ho, _slot_view(i), 1)
    return ho


def _rs_ici(cm, part, rows=None, recv=None):
    if recv is None:
        ho = cm.out((3,) + part.shape[1:], part.dtype)
    else:
        ho = cm.out(recv.shape, recv.dtype, alias=cm.inp(recv))
    hi = cm.inp(part)
    for i in (1, 2, 3):
        cm.copy(hi, _slot_view(i, rows), ho, _slot_view(i - 1, rows), CHIP_RELS[i])
    return ho


def _rs_add(gw, recv, axis, base, name, tw=None):
    _, R, n = recv.shape
    if axis == 1:
        tw = n if tw is None else tw
        gw_spec = pl.BlockSpec((R, tw), lambda i, t, b: (0, b[i] + t))
        rv_spec = pl.BlockSpec((None, R, tw), lambda i, t, b: (i, 0, t))
        grid = (4, n // tw)
    else:
        tw = _tile(n, 1024, LANES)
        gw_spec = pl.BlockSpec((R, tw), lambda i, t, b: (b[i], t))
        rv_spec = pl.BlockSpec((None, R, tw), lambda i, t, b: (i, 0, t))
        grid = (4, n // tw)

    def body(b_ref, g_ref, r_ref, o_ref):
        o_ref[...] = (g_ref[...].astype(F32) + r_ref[...].astype(F32)).astype(o_ref.dtype)

    return _call(body, [gw, recv], name=name, out_shape=jax.ShapeDtypeStruct(recv.shape, recv.dtype), grid=grid,
                 in_specs=[gw_spec, rv_spec], out_specs=rv_spec, prefetch=[base])


def _ag_w_in(src, a, D, INW):
    wm = LANES * a

    hd = D // 2
    ALL, TOP, BOT = (0, D), (0, hd), (hd, D)

    def main_place(ref, p, rows=ALL):
        off = pl.multiple_of(((2 * a + 1) * (p["me"] // 2) + (a + 1) * p["c"]) * LANES, LANES)
        return ref.at[pl.ds(rows[0], rows[1] - rows[0]), pl.ds(off, wm)]

    def main_src(ref, p):
        return ref.at[:, pl.ds(pl.multiple_of(p["c"] * LANES, LANES), wm)]

    def mid_src(ref, p):
        return ref.at[:, pl.ds(pl.multiple_of((1 - p["c"]) * wm, LANES), LANES)]

    def mid_place(ref, p, rows=ALL):
        return ref.at[p["me"], pl.ds(rows[0], rows[1] - rows[0]), :]

    def body(src_ref, full_ref, mid_ref, send_sems, recv_sems):
        pos = _position()
        sib, xn, yn = (_peer_position(pos, r) for r in (1, 4, 2))
        dg = _peer_position(pos, 6)
        started = []

        def remote(k, s, d, to):
            return pltpu.make_async_remote_copy(src_ref=s, dst_ref=d, send_sem=send_sems.at[k], recv_sem=recv_sems.at[k],
                                                device_id=(to["x"], to["y"], to["c"]), device_id_type=MESH)

        def send(k, owner, rows, to, from_src=False):
            for j, (src_v, place) in enumerate(((main_src, main_place), (mid_src, mid_place))):
                s = src_v(src_ref, pos) if from_src else place(full_ref if j == 0 else mid_ref, owner, rows)
                cp = remote(k + j, s, place(full_ref if j == 0 else mid_ref, owner, rows), to)
                cp.start()
                started.append(cp)

        def landed(k, owner, rows, frm):
            for j, place in enumerate((main_place, mid_place)):
                ref = full_ref if j == 0 else mid_ref
                remote(k + j, place(ref, owner, rows), place(ref, owner, rows), frm).wait_recv()

        local = [pltpu.make_async_copy(main_src(src_ref, pos), main_place(full_ref, pos), send_sems.at[18]),
                 pltpu.make_async_copy(mid_src(src_ref, pos), mid_place(mid_ref, pos), send_sems.at[19])]
        for cp in local:
            cp.start()
        send(0, pos, ALL, sib, from_src=True)
        send(2, pos, ALL, xn, from_src=True)
        send(4, pos, ALL, yn, from_src=True)
        landed(2, xn, ALL, xn)
        send(10, xn, ALL, sib)
        send(6, xn, TOP, yn)
        landed(4, yn, ALL, yn)
        send(12, yn, ALL, sib)
        send(8, yn, BOT, xn)
        landed(6, dg, TOP, yn)
        send(14, dg, TOP, sib)
        landed(8, dg, BOT, xn)
        send(16, dg, BOT, sib)
        sib_of = lambda p: _peer_position(p, 1)
        landed(0, sib, ALL, sib)
        landed(10, sib_of(xn), ALL, sib)
        landed(12, sib_of(yn), ALL, sib)
        landed(14, sib_of(dg), TOP, sib)
        landed(16, sib_of(dg), BOT, sib)
        for cp in started:
            cp.wait_send()
        for cp in local:
            cp.wait()

    return _call(body, [src], name="ag_w_in", in_specs=[ANY], out_specs=[ANY, ANY],
                 out_shape=(jax.ShapeDtypeStruct((D, INW), BF16), jax.ShapeDtypeStruct((N_DEV, D, LANES), BF16)),
                 scratch_shapes=[pltpu.SemaphoreType.DMA((20,)), pltpu.SemaphoreType.DMA((20,))])


def _patch_mid(full, mid, a):
    D = full.shape[0]

    def body(full_ref, e_ref, o_ref, out_ref):
        out_ref[...] = e_ref[...] + o_ref[...]

    return _call(body, [full, mid, mid], name="patch_mid", grid=(N_DEV // 2,),
                 out_shape=jax.ShapeDtypeStruct(full.shape, full.dtype),
                 in_specs=[ANY, pl.BlockSpec((None, D, LANES), lambda j: (2 * j, 0, 0)),
                           pl.BlockSpec((None, D, LANES), lambda j: (2 * j + 1, 0, 0))],
                 out_specs=pl.BlockSpec((D, LANES), lambda j: (0, (2 * a + 1) * j + a)), aliases={0: 0})


MM_RESIDENT = 2048


def _mm(a, b, mode, out_dtype, name, b_off=0, n=None, comm=None, extras=(), epi=None, tn=None):
    if mode == "nn":
        (M, K), (K2, N) = a.shape, b.shape
    elif mode == "nt":
        (M, K), (N, K2) = a.shape, b.shape
    else:
        (K, M), (K2, N) = a.shape, b.shape
    assert K == K2, (a.shape, b.shape, mode)
    if n is not None:
        N = n
    single = not isinstance(out_dtype, (tuple, list))
    out_dtypes = (out_dtype,) if single else tuple(out_dtype)
    if epi is None:
        epi = lambda r: (r,)
    tk = K if K <= MM_RESIDENT else (MM_RESIDENT if K % MM_RESIDENT == 0 else _tile(K, 512, LANES))
    nk = K // tk
    if M > MM_RESIDENT and mode == "tn" and N <= MM_RESIDENT and not b_off:
        tm, tn = _tile(M, 512, LANES), N
    elif nk > 1:
        tm, tn = _tile(M, 1024, LANES), _tile(N, tn or 1024, LANES)
    else:
        tm = _tile(M, MM_RESIDENT, LANES)
        tn = _tile(math.gcd(N, b_off) if b_off else N, tn or 512, LANES)
    jb = b_off // tn
    dn = {"nn": NN, "nt": NT, "tn": TN}[mode]
    ne, no = len(extras), len(out_dtypes)

    def body(a_ref, b_ref, *rest):
        e_refs, o_refs = rest[:ne], rest[ne:ne + no]

        def finish(r):
            for o_ref, v in zip(o_refs, epi(r, *[e[...] for e in e_refs])):
                o_ref[...] = v.astype(o_ref.dtype)

        if nk == 1:
            finish(_bdot(a_ref[...], b_ref[...], dn))
            return
        acc_ref = rest[ne + no]
        k = pl.program_id(2)

        @pl.when(k == 0)
        def _():
            acc_ref[...] = _bdot(a_ref[...], b_ref[...], dn)

        @pl.when(jnp.logical_and(k > 0, k < nk - 1))
        def _():
            acc_ref[...] += _bdot(a_ref[...], b_ref[...], dn)

        @pl.when(k == nk - 1)
        def _():
            finish(acc_ref[...] + _bdot(a_ref[...], b_ref[...], dn))

    a_spec = pl.BlockSpec((tk, tm), lambda i, j, k: (k, i)) if mode == "tn" else pl.BlockSpec((tm, tk), lambda i, j, k: (i, k))
    b_spec = pl.BlockSpec((tn, tk), lambda i, j, k: (j, k)) if mode == "nt" else pl.BlockSpec((tk, tn), lambda i, j, k: (k, j + jb))
    o_spec = pl.BlockSpec((tm, tn), lambda i, j, k: (i, j))
    res = _call(body, [a, b] + list(extras), name=name, grid=(M // tm, N // tn, nk),
                out_shape=tuple(jax.ShapeDtypeStruct((M, N), dt) for dt in out_dtypes),
                in_specs=[a_spec, b_spec] + [o_spec] * ne, out_specs=[o_spec] * no,
                scratch_shapes=[pltpu.VMEM((tm, tn), F32)] if nk > 1 else [], comm=comm)
    return res[0] if single else res


def _rowwise(fn, row_ins, bcast_ins, row_outs, acc_outs, name, rt=256, comm=None):
    L = row_ins[0][0].shape[-2]
    rt = _tile(L, rt, 16)
    nr, nb, no = len(row_ins), len(bcast_ins), len(row_outs)

    def body(*refs):
        i = pl.program_id(0)
        vals = [r[...] for r in refs[:nr + nb]]
        outs, accs = fn(*vals)
        for r, v in zip(refs[nr + nb:nr + nb + no], outs):
            r[...] = v.astype(r.dtype)
        acc_refs = refs[nr + nb + no:]

        @pl.when(i == 0)
        def _():
            for r in acc_refs:
                r[...] = jnp.zeros_like(r)

        for r, v in zip(acc_refs, accs):
            r[...] += v

    in_specs = []
    for spec in row_ins:
        w, cb = spec[1], spec[2]
        if len(spec) == 4:
            in_specs.append(pl.BlockSpec((None, rt, w), functools.partial(lambda i, cb, ld: (ld, i, cb), cb=cb, ld=spec[3])))
        else:
            in_specs.append(pl.BlockSpec((rt, w), functools.partial(lambda i, cb: (i, cb), cb=cb)))
    in_specs += [pl.BlockSpec(b.shape, lambda i: (0, 0)) for b in bcast_ins]
    out_specs = [pl.BlockSpec((rt, w), lambda i: (i, 0)) for w, _ in row_outs]
    out_specs += [pl.BlockSpec(s, lambda i: (0, 0)) for s in acc_outs]
    out_shape = [jax.ShapeDtypeStruct((L, w), dt) for w, dt in row_outs] + [jax.ShapeDtypeStruct(s, F32) for s in acc_outs]
    return _call(body, [s[0] for s in row_ins] + list(bcast_ins), name=name, grid=(L // rt,), out_shape=tuple(out_shape),
                 in_specs=in_specs, out_specs=out_specs, comm=comm)


def _whole(fn, ins, out_shapes, name):
    def body(*refs):
        outs = fn(*[r[...] for r in refs[:len(ins)]])
        for r, v in zip(refs[len(ins):], outs):
            r[...] = v.astype(r.dtype)

    return _call(body, list(ins), name=name, out_shape=tuple(jax.ShapeDtypeStruct(s, dt) for s, dt in out_shapes))


def _silu(x):
    return x * jax.nn.sigmoid(x)


def _rms(x, g):
    return (x * lax.rsqrt(jnp.mean(x * x, axis=-1, keepdims=True) + EPS)) * g


def _modnorm(x, g, shift, scale):
    return _rms(x, g) * (1.0 + scale) + shift


def _adamw(w, g, m, v):
    m = ADAM_B1 * m + (1.0 - ADAM_B1) * g
    v = ADAM_B2 * v + (1.0 - ADAM_B2) * jnp.square(g)
    m_hat = m / (1.0 - ADAM_B1 ** ADAM_STEP)
    v_hat = v / (1.0 - ADAM_B2 ** ADAM_STEP)
    delta = -ADAM_LR * (m_hat / (jnp.sqrt(v_hat) + ADAM_EPS) + ADAM_WD * w)
    return delta, m, v


def _lower_bound(lg):
    e = jnp.exp(lg - jnp.max(lg, axis=0, keepdims=True))
    return e[0:1] / jnp.sum(e, axis=0, keepdims=True)


def _hg_chunk(hq, hf, hi, lb, st):
    C = hq.shape[0]
    row = lax.broadcasted_iota(jnp.int32, (C, C), 0)
    col = lax.broadcasted_iota(jnp.int32, (C, C), 1)
    tri = row >= col
    sg = jax.nn.sigmoid(hf)
    f = lb + (1.0 - lb) * sg
    lf = jnp.log(f)
    k = 1.0 - f
    q = _silu(hq)
    b = _dot(tri.astype(F32), lf, NN, precision=HIGHEST)
    m = b[C // 2 - 1:C // 2]
    bl = b[C - 1:C]
    e_qm, e_km, e_kl, e_q = jnp.exp(b - m), jnp.exp(m - b), jnp.exp(bl - b), jnp.exp(b)
    qe, ke, kd, qb = q * e_qm, k * e_km, k * e_kl, q * e_q
    sc = jnp.where(tri, _bdot(qe, ke, NT), 0.0)
    o = _bdot(sc, hi, NN) + _bdot(qb, st, NT)
    dec = jnp.exp(bl)
    st_next = st * dec + _bdot(hi, kd, TN)
    return o, st_next, dict(tri=tri, sg=sg, f=f, k=k, q=q, qe=qe, ke=ke, kd=kd, qb=qb, sc=sc, dec=dec,
                            e_qm=e_qm, e_km=e_km, e_kl=e_kl, e_q=e_q)


def _hg_out(o, hgate, gout):
    return _rms(o, gout) * _silu(hgate)


HG_GROUP = 16


def _hgrn_fwd(p4, lb_logits, gout, H, comm=None):
    L = p4.shape[0]
    C = HG_CHUNK
    GR = _tile(L // C, HG_GROUP, 1)
    T = GR * C
    N = L // T

    def body(hq_ref, hf_ref, hi_ref, hg_ref, lg_ref, gout_ref, o_ref, s_ref, st_ref):
        @pl.when(pl.program_id(1) == 0)
        def _():
            st_ref[...] = jnp.zeros_like(st_ref)

        lb = _lower_bound(lg_ref[...])
        st = st_ref[...]
        for ci in range(GR):
            rows = pl.ds(ci * C, C)
            s_ref[0, ci] = st
            o, st, _ = _hg_chunk(hq_ref[rows, :], hf_ref[rows, :], hi_ref[rows, :], lb, st)
            o_ref[rows, :] = _hg_out(o, hg_ref[rows, :], gout_ref[...]).astype(o_ref.dtype)
        st_ref[...] = st

    blk = lambda s: pl.BlockSpec((T, HG_DK), functools.partial(lambda h, n, s: (n, s * H + h), s=s))
    return _call(
        body, [p4, p4, p4, p4, lb_logits, gout], name="hgrn_fwd", grid=(H, N),
        out_shape=(jax.ShapeDtypeStruct((L, H * HG_DK), BF16), jax.ShapeDtypeStruct((H, N * GR, HG_DK, HG_DK), F32)),
        in_specs=[blk(0), blk(1), blk(2), blk(3), pl.BlockSpec((2, HG_DK), lambda h, n: (0, h)),
                  pl.BlockSpec((1, HG_DK), lambda h, n: (0, 0))],
        out_specs=(pl.BlockSpec((T, HG_DK), lambda h, n: (n, h)),
                   pl.BlockSpec((1, GR, HG_DK, HG_DK), lambda h, n: (h, n, 0, 0))),
        scratch_shapes=[pltpu.VMEM((HG_DK, HG_DK), F32)], comm=comm)


def _hgrn_bwd(p4, lb_logits, gout, s_all, d_out, H, comm=None):
    L = p4.shape[0]
    C = HG_CHUNK
    GR = _tile(L // C, HG_GROUP, 1)
    T = GR * C
    N = L // T

    def body(hq_ref, hf_ref, hi_ref, hg_ref, lg_ref, gout_ref, s_ref, do_ref,
             dq_ref, df_ref, di_ref, dg_ref, dlb_ref, dgo_ref, dst_ref):
        @pl.when(pl.program_id(1) == 0)
        def _():
            dst_ref[...] = jnp.zeros_like(dst_ref)
            dlb_ref[...] = jnp.zeros_like(dlb_ref)

        @pl.when(jnp.logical_and(pl.program_id(0) == 0, pl.program_id(1) == 0))
        def _():
            dgo_ref[...] = jnp.zeros_like(dgo_ref)

        lb = _lower_bound(lg_ref[...])
        dst = dst_ref[...]
        d_lb = jnp.zeros((1, HG_DK), F32)
        d_go = jnp.zeros((1, HG_DK), F32)
        for ci in reversed(range(GR)):
            rows = pl.ds(ci * C, C)
            dst, d_lb_c, d_go_c = chunk_bwd(rows, lb, s_ref[0, ci], dst, hq_ref, hf_ref, hi_ref, hg_ref, gout_ref, do_ref,
                                            dq_ref, df_ref, di_ref, dg_ref)
            d_lb += d_lb_c
            d_go += d_go_c
        dst_ref[...] = dst
        dlb_ref[...] += d_lb
        dgo_ref[...] += d_go

    def chunk_bwd(rows, lb, st, dst_next, hq_ref, hf_ref, hi_ref, hg_ref, gout_ref, do_ref, dq_ref, df_ref, di_ref, dg_ref):
        hq, hf, hi, hgate = hq_ref[rows, :], hf_ref[rows, :], hi_ref[rows, :], hg_ref[rows, :]
        o, _, t = _hg_chunk(hq, hf, hi, lb, st)
        _, out_vjp = jax.vjp(_hg_out, o, hgate, gout_ref[...])
        do, d_hgate, d_gout = out_vjp(do_ref[rows, :])
        tri = t["tri"]
        dsc = jnp.where(tri, _bdot(do, hi, NT), 0.0)
        dv = _bdot(t["sc"], do, TN) + _bdot(t["kd"], dst_next, NT)
        dqe = _bdot(dsc, t["ke"], NN)
        dke = _bdot(dsc, t["qe"], TN)
        dqb = _bdot(do, st, NN)
        dkd = _bdot(hi, dst_next, NN)
        ddec = jnp.sum(dst_next * st, axis=0, keepdims=True)
        dst_prev = _bdot(do, t["qb"], TN) + dst_next * t["dec"]
        dq = dqe * t["e_qm"] + dqb * t["e_q"]
        dk = dke * t["e_km"] + dkd * t["e_kl"]
        tq, tk, td, tb = dqe * t["qe"], dke * t["ke"], dkd * t["kd"], dqb * t["qb"]
        db = tq - tk - td + tb
        dm = jnp.sum(tk - tq, axis=0, keepdims=True)
        dbl = jnp.sum(td, axis=0, keepdims=True) + ddec * t["dec"]
        rowi = lax.broadcasted_iota(jnp.int32, (C, HG_DK), 0)
        db = db + jnp.where(rowi == C // 2 - 1, dm, 0.0) + jnp.where(rowi == C - 1, dbl, 0.0)
        dlf = _dot(tri.astype(F32), db, TN, precision=HIGHEST)
        df = dlf / t["f"] - dk
        sg = t["sg"]
        df_ref[rows, :] = (df * (1.0 - lb) * sg * (1.0 - sg)).astype(df_ref.dtype)
        sq = jax.nn.sigmoid(hq)
        dq_ref[rows, :] = (dq * (sq * (1.0 + hq * (1.0 - sq)))).astype(dq_ref.dtype)
        di_ref[rows, :] = dv.astype(di_ref.dtype)
        dg_ref[rows, :] = d_hgate.astype(dg_ref.dtype)
        return dst_prev, jnp.sum(df * (1.0 - sg), axis=0, keepdims=True), d_gout

    blk = lambda s: pl.BlockSpec((T, HG_DK), functools.partial(lambda h, n, s: (N - 1 - n, s * H + h), s=s))
    oblk = pl.BlockSpec((T, HG_DK), lambda h, n: (N - 1 - n, h))
    vec = pl.BlockSpec((1, HG_DK), lambda h, n: (0, h))
    W = H * HG_DK
    return _call(
        body, [p4, p4, p4, p4, lb_logits, gout, s_all, d_out], name="hgrn_bwd", grid=(H, N),
        out_shape=tuple([jax.ShapeDtypeStruct((L, W), BF16)] * 4 + [jax.ShapeDtypeStruct((1, W), F32), jax.ShapeDtypeStruct((1, HG_DK), F32)]),
        in_specs=[blk(0), blk(1), blk(2), blk(3), pl.BlockSpec((2, HG_DK), lambda h, n: (0, h)),
                  pl.BlockSpec((1, HG_DK), lambda h, n: (0, 0)),
                  pl.BlockSpec((1, GR, HG_DK, HG_DK), lambda h, n: (h, N - 1 - n, 0, 0)), oblk],
        out_specs=(oblk, oblk, oblk, oblk, vec, pl.BlockSpec((1, HG_DK), lambda h, n: (0, 0))),
        scratch_shapes=[pltpu.VMEM((HG_DK, HG_DK), F32)], comm=comm)


def _bucket_ids():
    i = jnp.arange(AT_BLOCK, dtype=jnp.int32)[:, None]
    j = jnp.arange(2 * AT_BLOCK, dtype=jnp.int32)[None, :]
    n = jnp.maximum(i - j + AT_BLOCK, 0)
    nf = jnp.maximum(n, 1).astype(F32)
    large = MAX_EXACT + (jnp.log(nf / MAX_EXACT) / math.log(MAX_DISTANCE / MAX_EXACT) * (N_BUCKETS - MAX_EXACT)).astype(jnp.int32)
    large = jnp.minimum(large, N_BUCKETS - 1)
    return jnp.where(n < MAX_EXACT, n, large).reshape(1, -1)


def _onehot(bucket):
    ids = lax.broadcasted_iota(jnp.int32, (N_BUCKETS, bucket.shape[1]), 0)
    return (ids == bucket).astype(F32)


def _attn_probs(qn, kpn, kcn, bias_g, sink, first, scale):
    rows = qn.shape[0]
    i = jnp.bitwise_and(lax.broadcasted_iota(jnp.int32, (rows, AT_BLOCK), 0), AT_BLOCK - 1)
    j = lax.broadcasted_iota(jnp.int32, (rows, AT_BLOCK), 1)
    lp = _bdot(qn, kpn, NT) * scale + bias_g[:, :AT_BLOCK]
    lc = _bdot(qn, kcn, NT) * scale + bias_g[:, AT_BLOCK:]
    lp = jnp.where(jnp.logical_and(j > i, jnp.logical_not(first)), lp, NEG_INF)
    lc = jnp.where(j <= i, lc, NEG_INF)
    m = jnp.maximum(jnp.maximum(jnp.max(lp, axis=-1, keepdims=True), jnp.max(lc, axis=-1, keepdims=True)), sink)
    pp, pc, ps = jnp.exp(lp - m), jnp.exp(lc - m), jnp.exp(sink - m)
    den = jnp.sum(pp, axis=-1, keepdims=True) + jnp.sum(pc, axis=-1, keepdims=True) + ps
    return pp / den, pc / den, ps / den


def _sink_rows(sk_ref, G):
    head = lax.broadcasted_iota(jnp.int32, (G * AT_BLOCK, 1), 0) // AT_BLOCK
    sink = jnp.zeros((G * AT_BLOCK, 1), F32)
    for g in range(G):
        sink = jnp.where(head == g, sk_ref[0, g:g + 1, :], sink)
    return sink


def _attn_fwd(q_t, kp, vp, qg, kg, sinks, bias, KVH, comm=None):
    AH, L, DH = q_t.shape
    G = AH // KVH
    NB = L // AT_BLOCK
    scale = DH ** -0.5

    def body(q_ref, kp_ref, kc_ref, vp_ref, vc_ref, qg_ref, kg_ref, sk_ref, b_ref, o_ref):
        first = pl.program_id(1) == 0
        kpn, kcn = _rms(kp_ref[0], kg_ref[...]), _rms(kc_ref[0], kg_ref[...])
        qn = _rms(q_ref[...].reshape(G * AT_BLOCK, DH), qg_ref[...])
        sink = _sink_rows(sk_ref, G)
        pp, pc, _ = _attn_probs(qn, kpn, kcn, b_ref[...].reshape(G * AT_BLOCK, 2 * AT_BLOCK), sink, first, scale)
        o = _bdot(pp, vp_ref[0], NN) + _bdot(pc, vc_ref[0], NN)
        o_ref[...] = o.reshape(G, AT_BLOCK, DH).astype(o_ref.dtype)

    kblk = lambda off: pl.BlockSpec((1, AT_BLOCK, DH), functools.partial(lambda h, n, off: (h, n + off, 0), off=off))
    return _call(
        body, [q_t, kp, kp, vp, vp, qg, kg, sinks, bias], name="attn_fwd", grid=(KVH, NB),
        out_shape=jax.ShapeDtypeStruct((AH, L, DH), BF16),
        in_specs=[pl.BlockSpec((G, AT_BLOCK, DH), lambda h, n: (h, n, 0)), kblk(0), kblk(1), kblk(0), kblk(1),
                  pl.BlockSpec((1, DH), lambda h, n: (0, 0)), pl.BlockSpec((1, DH), lambda h, n: (0, 0)),
                  pl.BlockSpec((1, G, 1), lambda h, n: (h, 0, 0)),
                  pl.BlockSpec((G, AT_BLOCK, 2 * AT_BLOCK), lambda h, n: (h, 0, 0))],
        out_specs=pl.BlockSpec((G, AT_BLOCK, DH), lambda h, n: (h, n, 0)), comm=comm)


def _attn_bwd(q_t, kp, vp, qg, kg, sinks, bias, do_t, KVH, comm=None):
    AH, L, DH = q_t.shape
    G = AH // KVH
    NB = L // AT_BLOCK
    B = AT_BLOCK
    scale = DH ** -0.5

    def body(q_ref, kp_ref, kc_ref, vp_ref, vc_ref, qg_ref, kg_ref, sk_ref, b_ref, do_ref,
             dq_ref, dk_ref, dv_ref, dqg_ref, dkg_ref, dsk_ref, db_ref):
        n = pl.program_id(1)
        first = n == 0

        @pl.when(first)
        def _():
            for r in (dk_ref, dv_ref, dsk_ref, db_ref):
                r[...] = jnp.zeros_like(r)

        @pl.when(jnp.logical_and(first, pl.program_id(0) == 0))
        def _():
            dqg_ref[...] = jnp.zeros_like(dqg_ref)
            dkg_ref[...] = jnp.zeros_like(dkg_ref)

        kp_raw, kc_raw, kgv, qgv = kp_ref[0], kc_ref[0], kg_ref[...], qg_ref[...]
        kpn, kp_vjp = jax.vjp(_rms, kp_raw, kgv)
        kcn, kc_vjp = jax.vjp(_rms, kc_raw, kgv)
        qn, q_vjp = jax.vjp(_rms, q_ref[...].reshape(G * B, DH), qgv)
        pp, pc, ps = _attn_probs(qn, kpn, kcn, b_ref[...].reshape(G * B, 2 * B), _sink_rows(sk_ref, G), first, scale)
        do = do_ref[...].reshape(G * B, DH)
        dvp = _bdot(pp, do, TN)
        dvc = _bdot(pc, do, TN)
        dpp = _bdot(do, vp_ref[0], NT)
        dpc = _bdot(do, vc_ref[0], NT)
        dsum = jnp.sum(dpp * pp, axis=-1, keepdims=True) + jnp.sum(dpc * pc, axis=-1, keepdims=True)
        dlp = pp * (dpp - dsum)
        dlc = pc * (dpc - dsum)
        dsk_ref[0] += jnp.sum((-ps * dsum).reshape(G, B, 1), axis=1)
        db_ref[:, :, :B] += dlp.reshape(G, B, B)
        db_ref[:, :, B:] += dlc.reshape(G, B, B)
        dlp, dlc = dlp * scale, dlc * scale
        dqn = _bdot(dlp, kpn, NN) + _bdot(dlc, kcn, NN)
        dq_raw, dqg = q_vjp(dqn)
        dq_ref[...] = dq_raw.reshape(G, B, DH).astype(dq_ref.dtype)
        dkp_raw, dkg_p = kp_vjp(_bdot(dlp, qn, TN))
        dkc_raw, dkg_c = kc_vjp(_bdot(dlc, qn, TN))
        r0 = pl.multiple_of(n * B, B)
        r1 = pl.multiple_of(n * B + B, B)
        dk_ref[0, pl.ds(r0, B), :] += dkp_raw
        dk_ref[0, pl.ds(r1, B), :] += dkc_raw
        dv_ref[0, pl.ds(r0, B), :] += dvp
        dv_ref[0, pl.ds(r1, B), :] += dvc
        dqg_ref[...] += dqg
        dkg_ref[...] += dkg_p + dkg_c

    kblk = lambda off: pl.BlockSpec((1, B, DH), functools.partial(lambda h, n, off: (h, n + off, 0), off=off))
    qblk = pl.BlockSpec((G, B, DH), lambda h, n: (h, n, 0))
    accblk = pl.BlockSpec((1, L + B, DH), lambda h, n: (h, 0, 0))
    vecblk = pl.BlockSpec((1, DH), lambda h, n: (0, 0))
    return _call(
        body, [q_t, kp, kp, vp, vp, qg, kg, sinks, bias, do_t], name="attn_bwd", grid=(KVH, NB),
        out_shape=(jax.ShapeDtypeStruct((AH, L, DH), BF16), jax.ShapeDtypeStruct((KVH, L + B, DH), F32),
                   jax.ShapeDtypeStruct((KVH, L + B, DH), F32), jax.ShapeDtypeStruct((1, DH), F32),
                   jax.ShapeDtypeStruct((1, DH), F32), jax.ShapeDtypeStruct((KVH, G, 1), F32),
                   jax.ShapeDtypeStruct((AH, B, 2 * B), F32)),
        in_specs=[qblk, kblk(0), kblk(1), kblk(0), kblk(1),
                  pl.BlockSpec((1, DH), lambda h, n: (0, 0)), pl.BlockSpec((1, DH), lambda h, n: (0, 0)),
                  pl.BlockSpec((1, G, 1), lambda h, n: (h, 0, 0)),
                  pl.BlockSpec((G, B, 2 * B), lambda h, n: (h, 0, 0)), qblk],
        out_specs=(qblk, accblk, accblk, vecblk, vecblk, pl.BlockSpec((1, G, 1), lambda h, n: (h, 0, 0)),
                   pl.BlockSpec((G, B, 2 * B), lambda h, n: (h, 0, 0))), comm=comm)


def _heads_first(t, nh):
    L = t.shape[0]
    return jnp.transpose(t.reshape(L, nh, t.shape[1] // nh), (1, 0, 2))


def _heads_last(t):
    nh, L, dh = t.shape
    return jnp.transpose(t, (1, 0, 2)).reshape(L, nh * dh)


def _softmax0(lg):
    e = jnp.exp(lg - jnp.max(lg, axis=0, keepdims=True))
    return e[0:1] / jnp.sum(e, axis=0, keepdims=True)


def _ada_update_call(fn, c_all, d_cols, w, m, v, rt):
    D, n = w.shape

    def body(c_ref, d_ref, w_ref, m_ref, v_ref, g_out, dl_out, m_out, v_out):
        outs, _ = fn(c_ref[...], d_ref[...], w_ref[...], m_ref[...], v_ref[...])
        for r, val in zip((g_out, dl_out, m_out, v_out), outs):
            r[...] = val

    wblk = pl.BlockSpec((rt, n), lambda i: (i, 0))
    return _call(
        body, [c_all, d_cols, w, m, v], name="update_ada", grid=(D // rt,), out_shape=tuple([jax.ShapeDtypeStruct((D, n), F32)] * 4),
        in_specs=[pl.BlockSpec((N_DEV, rt), lambda i: (0, i)), pl.BlockSpec((N_DEV, n), lambda i: (0, 0)), wblk, wblk, wblk],
        out_specs=(wblk, wblk, wblk, wblk))


def kernel(x, c, w_ada, b_ada, norm1_g, norm2_g, w_in, hg_lb_logits, hg_out_norm_g, q_norm_g, k_norm_g, attn_sinks, rel_bias_table, w_branch_hg, w_branch_attn, w_out, w_ff1, w_ff2, loss_target, m_w_ada, m_b_ada, m_norm1_g, m_norm2_g, m_w_in, m_hg_lb_logits, m_hg_out_norm_g, m_q_norm_g, m_k_norm_g, m_attn_sinks, m_rel_bias_table, m_w_branch_hg, m_w_branch_attn, m_w_out, m_w_ff1, m_w_ff2, v_w_ada, v_b_ada, v_norm1_g, v_norm2_g, v_w_in, v_hg_lb_logits, v_hg_out_norm_g, v_q_norm_g, v_k_norm_g, v_attn_sinks, v_rel_bias_table, v_w_branch_hg, v_w_branch_attn, v_w_out, v_w_ff1, v_w_ff2):
    cc = lax.axis_index("c")
    me = 4 * lax.axis_index("x") + 2 * lax.axis_index("y") + cc
    x2 = x[0]
    tgt = loss_target[0]
    L, D = x2.shape
    HGW = hg_lb_logits.shape[1]
    H = HGW // HG_DK
    AH = attn_sinks.shape[1]
    DH = q_norm_g.shape[1]
    ATW = AH * DH
    BW = w_in.shape[2]
    INW = BW * N_DEV
    A = BW // LANES
    assert BW == LANES * A + LANES // 2
    KVW = (INW - 4 * HGW - ATW - 2 * D) // 2
    KVH = KVW // DH
    G = AH // KVH
    ADA_N = w_ada.shape[2]
    PAIR = 2 * A + 1

    w_in_b = w_in[0].astype(BF16)
    src_in = jnp.where(cc == 0, jnp.pad(w_in_b, ((0, 0), (0, LANES // 2))), jnp.pad(w_in_b, ((0, 0), (LANES // 2, 0))))
    w_in_gapped, w_in_mid = _ag_w_in(src_in, A, D, INW)
    w_in_full = _patch_mid(w_in_gapped, w_in_mid, A)

    c_all = _gather_small(c, me, "gather_c")[:, 0, :]
    b_cols = lax.dynamic_slice(b_ada, (0, me * ADA_N), (1, ADA_N))
    (ada_cols,) = _whole(lambda cv, w, b: (_bdot(_silu(cv), w, NN) + b,), [c_all, w_ada[0], b_cols],
                         [((N_DEV, ADA_N), F32)], "ada_fwd")
    ada_all = _gather_small(ada_cols, me, "gather_ada")
    ada_row = lax.dynamic_slice(ada_all, (0, me, 0), (N_DEV, 1, ADA_N)).reshape(1, 6 * D)
    shift1, scale1, gate1, shift2, scale2, gate2 = [ada_row[:, i * D:(i + 1) * D] for i in range(6)]

    wnames = ("bhg", "bat", "out", "ff1", "ff2")
    waxis = dict(zip(wnames, (1, 1, 0, 1, 0)))
    wsrc = dict(zip(wnames, (w_branch_hg, w_branch_attn, w_out, w_ff1, w_ff2)))
    wblk = {k: wsrc[k][0].astype(BF16) for k in wnames}
    wf = {}

    (h,) = _rowwise(lambda xv, g, sh, sc: ((_modnorm(xv, g, sh, sc),), ()), [(x2, D, 0)], [norm1_g, shift1, scale1],
                    [(D, BF16)], [], "norm1")
    o4, oa = 4 * HGW, 4 * HGW + ATW + 2 * KVW
    r1, r2, ro = wblk["ff1"].shape[0], wblk["ff2"].shape[0], wblk["out"].shape[0]
    cm = _Comm()
    hs = {k: _ag_ici(cm, wblk[k], waxis[k]) for k in ("bhg", "bat")}
    hs["out"] = _ag_ici(cm, wblk["out"], waxis["out"], rows=(0, ro // 2))
    p4 = _mm(h, w_in_full, "nn", F32, "proj_hg", n=o4, comm=cm)
    half = {k: cm.result(hs[k]) for k in hs}
    cm = _Comm()
    hs = {"out": _ag_ici(cm, wblk["out"], waxis["out"], rows=(ro // 2, ro), into=half["out"])}
    pa = _mm(h, w_in_full, "nn", F32, "proj_at", b_off=o4, n=oa - o4, comm=cm)
    half["out"] = cm.result(hs["out"])
    cm = _Comm()
    hs = {k: _ag_d2d(cm, half[k], waxis[k]) for k in ("bhg", "bat")}
    hs["ff2"] = _ag_ici(cm, wblk["ff2"], waxis["ff2"], rows=(0, r2 // 4))
    pg = _mm(h, w_in_full, "nn", F32, "proj_gate", b_off=oa, n=INW - oa, comm=cm)
    wf["bhg"], wf["bat"], half["ff2"] = (cm.result(hs[k]) for k in ("bhg", "bat", "ff2"))

    cm = _Comm()
    hs = {"out": _ag_d2d(cm, half["out"], waxis["out"]), "ff1": _ag_ici(cm, wblk["ff1"], waxis["ff1"], rows=(0, r1 // 2))}
    o_hg, s_all = _hgrn_fwd(p4, hg_lb_logits, hg_out_norm_g, H, comm=cm)
    wf["out"], half["ff1"] = cm.result(hs["out"]), cm.result(hs["ff1"])

    bucket = _bucket_ids()
    (bias_flat,) = _whole(lambda tb, bk: (_dot(tb, _onehot(bk), TN, precision=HIGHEST),), [rel_bias_table, bucket],
                          [((AH, AT_BLOCK * 2 * AT_BLOCK), F32)], "bias_fwd")
    bias = bias_flat.reshape(AH, AT_BLOCK, 2 * AT_BLOCK)
    q_t = _heads_first(pa[:, :ATW], AH)
    pad = lambda t: jnp.pad(t, ((0, 0), (AT_BLOCK, 0), (0, 0)))
    kp = pad(_heads_first(pa[:, ATW:ATW + KVW], KVH))
    vp = pad(_heads_first(pa[:, ATW + KVW:], KVH))
    sinks3 = attn_sinks.reshape(KVH, G, 1)
    cm = _Comm()
    hs = {"ff1": _ag_ici(cm, wblk["ff1"], waxis["ff1"], rows=(r1 // 2, r1), into=half["ff1"])}
    o_at = _heads_last(_attn_fwd(q_t, kp, vp, q_norm_g, k_norm_g, sinks3, bias, KVH, comm=cm))
    half["ff1"] = cm.result(hs["ff1"])

    bh = _mm(o_hg, wf["bhg"], "nn", F32, "branch_hg")
    ba = _mm(o_at, wf["bat"], "nn", F32, "branch_at")

    def merge_fn(bhv, bav, ghg, gat):
        return jax.nn.sigmoid(ghg) * bhv + jax.nn.sigmoid(gat) * bav

    cm = _Comm()
    hs = {"ff1": _ag_d2d(cm, half["ff1"], waxis["ff1"])}
    (merged,) = _rowwise(lambda *a: ((merge_fn(*a),), ()), [(bh, D, 0), (ba, D, 0), (pg, D, 0), (pg, D, 1)], [],
                         [(D, BF16)], [], "merge", comm=cm)
    wf["ff1"] = cm.result(hs["ff1"])
    cm = _Comm()
    hs = {"ff2": _ag_ici(cm, wblk["ff2"], waxis["ff2"], rows=(r2 // 4, 3 * r2 // 8), into=half["ff2"])}
    mo = _mm(merged, wf["out"], "nn", F32, "out_proj", comm=cm)
    half["ff2"] = cm.result(hs["ff2"])

    def resid1(xv, mov, g1, g2n, sh, sc):
        x1v = xv + g1 * mov
        return (x1v, _modnorm(x1v, g2n, sh, sc)), ()

    cm = _Comm()
    hs = {"ff2": _ag_ici(cm, wblk["ff2"], waxis["ff2"], rows=(3 * r2 // 8, r2 // 2), into=half["ff2"])}
    x1, h2 = _rowwise(resid1, [(x2, D, 0), (mo, D, 0)], [gate1, norm2_g, shift2, scale2], [(D, F32), (D, BF16)], [], "resid1",
                      comm=cm)
    half["ff2"] = cm.result(hs["ff2"])
    cm = _Comm()
    hs = {"ff2": _ag_ici(cm, wblk["ff2"], waxis["ff2"], rows=(r2 // 2, r2), into=half["ff2"])}
    u, act = _mm(h2, wf["ff1"], "nn", (F32, BF16), "ff1", comm=cm, epi=lambda r: (r, jnp.square(jnp.maximum(r, 0.0))))
    half["ff2"] = cm.result(hs["ff2"])
    cm = _Comm()
    hs = {"ff2": _ag_d2d(cm, half["ff2"], waxis["ff2"])}
    _call(lambda: None, [], name="ag_d2d_ff2", out_shape=(), comm=cm)
    wf["ff2"] = cm.result(hs["ff2"])
    ff = _mm(act, wf["ff2"], "nn", F32, "ff2")

    def loss_fn(x1v, ffv, tv, g2):
        e = x1v + g2 * ffv - tv
        dy = e * (1.0 / D)
        return (dy, dy * g2), (jnp.sum(e * e, axis=0, keepdims=True), jnp.sum(dy * ffv, axis=0, keepdims=True))

    dy, d_ff, sq_sum, d_gate2 = _rowwise(loss_fn, [(x1, D, 0), (ff, D, 0), (tgt, D, 0)], [gate2],
                                         [(D, F32), (D, BF16)], [(1, D), (1, D)], "loss")
    loss = lax.psum(jnp.sum(sq_sum) * (0.5 / D), ("x", "y", "c"))

    owner_base = jnp.stack([me ^ r for r in CHIP_RELS]).astype(jnp.int32)
    gw, recv1, part, recv2 = {}, {}, {}, {}
    gw["ff2"] = _mm(act, d_ff, "tn", BF16, "dw_ff2")
    cm = _Comm()
    hh = _rs_d2d(cm, gw["ff2"], waxis["ff2"])
    d_u = _mm(d_ff, wf["ff2"], "nt", BF16, "d_act", comm=cm, extras=[u], epi=lambda r, uv: (r * (2.0 * jnp.maximum(uv, 0.0)),))
    part["ff2"] = _rs_add(gw["ff2"], cm.result(hh), waxis["ff2"], owner_base, "rs_add_ff2")
    rows_ff2 = part["ff2"].shape[1]
    cm = _Comm()
    hh = _rs_ici(cm, part["ff2"], rows=(0, rows_ff2 // 2))
    gw["ff1"] = _mm(h2, d_u, "tn", BF16, "dw_ff1", comm=cm)
    cm2 = _Comm()
    hh2 = _rs_ici(cm2, part["ff2"], rows=(rows_ff2 // 2, rows_ff2), recv=cm.result(hh))
    hh1 = _rs_d2d(cm2, gw["ff1"], waxis["ff1"])
    d_h2 = _mm(d_u, wf["ff1"], "nt", F32, "d_h2", comm=cm2)
    recv2["ff2"] = cm2.result(hh2)
    part["ff1"] = _rs_add(gw["ff1"], cm2.result(hh1), waxis["ff1"], owner_base, "rs_add_ff1")

    def norm2_bwd(dh2v, x1v, dyv, mov, g2n, sh, sc, g1):
        _, vjp = jax.vjp(_modnorm, x1v, g2n, sh, sc)
        dx, dg, dsh, dsc = vjp(dh2v)
        dx1 = dyv + dx
        return (dx1, dx1 * g1), (dg, dsh, dsc, jnp.sum(dx1 * mov, axis=0, keepdims=True))

    d_x1, d_mo, d_g2n, d_shift2, d_scale2, d_gate1 = _rowwise(
        norm2_bwd, [(d_h2, D, 0), (x1, D, 0), (dy, D, 0), (mo, D, 0)], [norm2_g, shift2, scale2, gate1],
        [(D, F32), (D, BF16)], [(1, D)] * 4, "norm2_bwd")
    gw["out"] = _mm(merged, d_mo, "tn", BF16, "dw_out")
    cm = _Comm()
    hh = _rs_d2d(cm, gw["out"], waxis["out"])
    d_merged = _mm(d_mo, wf["out"], "nt", F32, "d_merged", comm=cm)
    part["out"] = _rs_add(gw["out"], cm.result(hh), waxis["out"], owner_base, "rs_add_out")

    def merge_bwd(dmv, bhv, bav, ghg, gat):
        _, vjp = jax.vjp(merge_fn, bhv, bav, ghg, gat)
        return vjp(dmv), ()

    d_bh, d_ba, d_ghg, d_gat = _rowwise(merge_bwd, [(d_merged, D, 0), (bh, D, 0), (ba, D, 0), (pg, D, 0), (pg, D, 1)], [],
                                        [(D, BF16)] * 4, [], "merge_bwd")
    gw["bhg"] = _mm(o_hg, d_bh, "tn", BF16, "dw_bhg")
    gw["bat"] = _mm(o_at, d_ba, "tn", BF16, "dw_bat")
    cm = _Comm()
    hh = {k: _rs_d2d(cm, gw[k], waxis[k]) for k in ("bhg", "bat")}
    d_ohg = _mm(d_bh, wf["bhg"], "nt", F32, "d_ohg", comm=cm)
    for k in ("bhg", "bat"):
        part[k] = _rs_add(gw[k], cm.result(hh[k]), waxis[k], owner_base, "rs_add_" + k)
    d_oat = _mm(d_ba, wf["bat"], "nt", BF16, "d_oat")

    cm = _Comm()
    hh = {"ff1": _rs_ici(cm, part["ff1"])}
    d_hq, d_hf, d_hi, d_hg, d_lb, d_gout_h = _hgrn_bwd(p4, hg_lb_logits, hg_out_norm_g, s_all, d_ohg, H, comm=cm)
    recv2["ff1"] = cm.result(hh["ff1"])
    cm = _Comm()
    hh = {k: _rs_ici(cm, part[k]) for k in ("out", "bhg", "bat")}
    dq_t, dkp, dvp, d_qg, d_kg, d_sk, d_bias = _attn_bwd(q_t, kp, vp, q_norm_g, k_norm_g, sinks3, bias,
                                                         _heads_first(d_oat, AH), KVH, comm=cm)
    for k in hh:
        recv2[k] = cm.result(hh[k])
    d_aq = _heads_last(dq_t)
    d_ak = _heads_last(dkp[:, AT_BLOCK:, :]).astype(BF16)
    d_av = _heads_last(dvp[:, AT_BLOCK:, :]).astype(BF16)
    d_proj = jnp.concatenate([d_hq, d_hf, d_hi, d_hg, d_aq, d_ak, d_av, d_ghg, d_gat], axis=1)
    gw_in = _mm(h, d_proj, "tn", BF16, "dw_in")

    wm = LANES * A
    cm = _Comm()
    hi_ = cm.inp(gw_in)
    h_main, h_mid = cm.out((4, D, wm), BF16), cm.out((4, D, LANES), BF16)
    for i, r in enumerate(CHIP_RELS):
        def main_view(ref, p, r=r):
            o = p["me"] ^ r ^ 1
            return ref.at[:, pl.ds(pl.multiple_of((PAIR * (o // 2) + (A + 1) * (1 - p["c"])) * LANES, LANES), wm)]

        def mid_view(ref, p, r=r):
            o = p["me"] ^ r
            return ref.at[:, pl.ds(pl.multiple_of((PAIR * (o // 2) + A) * LANES, LANES), LANES)]

        cm.copy(hi_, main_view, h_main, _slot_view(i), 1)
        cm.copy(hi_, mid_view, h_mid, _slot_view(i), 1)
    _call(lambda: None, [], name="rs_d2d_in", out_shape=(), comm=cm)
    chip = jnp.stack([(me ^ r) // 2 for r in CHIP_RELS]).astype(jnp.int32)
    part_main = _rs_add(gw_in, cm.result(h_main), 1, PAIR * chip + (A + 1) * cc, "rs_add_in_main", tw=LANES)
    part_mid = _rs_add(gw_in, cm.result(h_mid), 1, PAIR * chip + A, "rs_add_in_mid", tw=LANES)
    cm = _Comm()
    hh_main, hh_mid = _rs_ici(cm, part_main), _rs_ici(cm, part_mid)
    d_h = _mm(d_proj, w_in_full, "nt", F32, "d_h", comm=cm)
    rx_main, rx_mid = cm.result(hh_main), cm.result(hh_mid)

    def norm1_bwd(dhv, xv, dx1v, g1n, sh, sc):
        _, vjp = jax.vjp(_modnorm, xv, g1n, sh, sc)
        dx, dg, dsh, dsc = vjp(dhv)
        return (dx1v + dx,), (dg, dsh, dsc)

    grad_x, d_g1n, d_shift1, d_scale1 = _rowwise(norm1_bwd, [(d_h, D, 0), (x2, D, 0), (d_x1, D, 0)],
                                                 [norm1_g, shift1, scale1], [(D, F32)], [(1, D)] * 3, "norm1_bwd")

    def sum4(p0, p1, p2, p3):
        return ((p0.astype(F32) + p1.astype(F32)) + p2.astype(F32)) + p3.astype(F32)

    def update_fn(w, m, v, p0, p1, p2, p3):
        g = sum4(p0, p1, p2, p3)
        delta, mn, vn = _adamw(w, g, m, v)
        return (g, delta, mn, vn), ()

    wmv = dict(zip(wnames, ((w_branch_hg, m_w_branch_hg, v_w_branch_hg), (w_branch_attn, m_w_branch_attn, v_w_branch_attn),
                            (w_out, m_w_out, v_w_out), (w_ff1, m_w_ff1, v_w_ff1), (w_ff2, m_w_ff2, v_w_ff2))))
    res = {}
    for k in wnames:
        w, m, v = (t[0] for t in wmv[k])
        n = w.shape[1]
        ins = [(t, n, 0) for t in (w, m, v)] + [(part[k], n, 0, 0)] + [(recv2[k], n, 0, i) for i in range(3)]
        res[k] = [t[None] for t in _rowwise(update_fn, ins, [], [(n, F32)] * 4, [], "update_" + k)]

    g_main, = _rowwise(lambda *p: ((sum4(*p),), ()), [(part_main, wm, 0, 0)] + [(rx_main, wm, 0, i) for i in range(3)], [],
                       [(wm, F32)], [], "sum_in_main")
    g_mid, = _rowwise(lambda *p: ((sum4(*p),), ()), [(part_mid, LANES, 0, 0)] + [(rx_mid, LANES, 0, i) for i in range(3)], [],
                      [(LANES, F32)], [], "sum_in_mid")
    g_in = jnp.where(cc == 0, jnp.concatenate([g_main, g_mid[:, :LANES // 2]], axis=1),
                     jnp.concatenate([g_mid[:, LANES // 2:], g_main], axis=1))

    def update_given(w, m, v, g):
        delta, mn, vn = _adamw(w, g, m, v)
        return (g, delta, mn, vn), ()

    res["in"] = [t[None] for t in _rowwise(update_given, [(t, BW, 0) for t in (w_in[0], m_w_in[0], v_w_in[0], g_in)], [],
                                           [(BW, F32)] * 4, [], "update_in")]

    d_ada_row = jnp.concatenate([d_shift1, d_scale1, d_gate1, d_shift2, d_scale2, d_gate2], axis=1)
    d_ada_all = _gather_small(d_ada_row, me, "gather_dada")[:, 0, :]
    d_ada_cols = lax.dynamic_slice(d_ada_all, (0, me * ADA_N), (N_DEV, ADA_N))

    def ada_update(cv, dav, w, m, v):
        g = _bdot(_silu(cv), dav, TN)
        delta, mn, vn = _adamw(w, g, m, v)
        return (g, delta, mn, vn), ()

    res["ada"] = [t[None] for t in _ada_update_call(ada_update, c_all, d_ada_cols, w_ada[0], m_w_ada[0], v_w_ada[0], _tile(D, 256, 16))]

    d_sinks = d_sk.reshape(1, AH)
    (d_table_t,) = _whole(lambda db, bk: (_dot(db, _onehot(bk), NT, precision=HIGHEST),),
                          [d_bias.reshape(AH, AT_BLOCK * 2 * AT_BLOCK), bucket], [((AH, N_BUCKETS), F32)], "bias_bwd")
    smalls = [d_g1n, d_g2n, d_lb, d_gout_h, d_qg, d_kg, d_sinks, d_table_t.T.reshape(1, N_BUCKETS * AH)]
    widths = [s.shape[1] for s in smalls]
    lanes = [-(-w // LANES) * LANES for w in widths]
    smalls = [jnp.pad(s, ((0, 0), (0, p - w))) for s, w, p in zip(smalls, widths, lanes)]
    packed = _gather_small(jnp.concatenate(smalls, axis=1), me, "gather_small")[:, 0, :]
    offs = [sum(lanes[:i]) for i in range(len(lanes))]

    def small_update(pk, dada, lg, *wmv_flat):
        tot = pk[0:1]
        for d in range(1, N_DEV):
            tot = tot + pk[d:d + 1]
        gb = dada[0:1]
        for d in range(1, N_DEV):
            gb = gb + dada[d:d + 1]
        gs = [tot[:, offs[i]:offs[i] + widths[i]] for i in range(len(widths))]
        _, lb_vjp = jax.vjp(_softmax0, lg)
        (g_lg,) = lb_vjp(gs[2])
        grads = [gb, gs[0], gs[1], g_lg, gs[3], gs[4], gs[5], gs[6], gs[7]]
        outs = []
        for i, g in enumerate(grads):
            w, m, v = wmv_flat[3 * i:3 * i + 3]
            delta, mn, vn = _adamw(w, g, m, v)
            outs += [g, delta, mn, vn]
        return tuple(outs)

    tbl = lambda t: t.reshape(1, N_BUCKETS * AH)
    small_wmv = [(b_ada, m_b_ada, v_b_ada), (norm1_g, m_norm1_g, v_norm1_g), (norm2_g, m_norm2_g, v_norm2_g),
                 (hg_lb_logits, m_hg_lb_logits, v_hg_lb_logits), (hg_out_norm_g, m_hg_out_norm_g, v_hg_out_norm_g),
                 (q_norm_g, m_q_norm_g, v_q_norm_g), (k_norm_g, m_k_norm_g, v_k_norm_g),
                 (attn_sinks, m_attn_sinks, v_attn_sinks),
                 (tbl(rel_bias_table), tbl(m_rel_bias_table), tbl(v_rel_bias_table))]
    flat = [t for trip in small_wmv for t in trip]
    out_shapes = [(trip[0].shape, F32) for trip in small_wmv for _ in range(4)]
    sres = _whole(small_update, [packed, d_ada_all, hg_lb_logits] + flat, out_shapes, "small_update")
    names_small = ("b_ada", "norm1_g", "norm2_g", "lb", "gout", "qg", "kg", "sinks", "table")
    for i, k in enumerate(names_small):
        r = sres[4 * i:4 * i + 4]
        if k == "table":
            r = [t.reshape(N_BUCKETS, AH) for t in r]
        res[k] = r

    order = ("ada", "b_ada", "norm1_g", "norm2_g", "in", "lb", "gout", "qg", "kg", "sinks", "table", "bhg", "bat", "out", "ff1", "ff2")
    outs = [loss, grad_x[None]]
    for j in range(4):
        outs += [res[k][j] for k in order]
    return tuple(outs)
```

```python
import functools
import math

import jax
import jax.numpy as jnp
from jax import lax
from jax.experimental import pallas as pl
from jax.experimental.pallas import tpu as pltpu

F32 = jnp.float32
BF16 = jnp.bfloat16
EPS = 1e-6
NEG_INF = -1e30
HG_DK = 128
HG_CHUNK = 64
AT_BLOCK = 128
N_BUCKETS = 32
MAX_EXACT = 16
MAX_DISTANCE = 128
N_DEV = 8
LANES = 128
VMEM_LIMIT = 56 * 1024 * 1024
ADAM_LR, ADAM_B1, ADAM_B2, ADAM_EPS, ADAM_WD, ADAM_STEP = 0.001, 0.9, 0.999, 1e-08, 0.01, 10
HIGHEST = lax.Precision.HIGHEST
MESH = pl.DeviceIdType.MESH
ANY = pl.BlockSpec(memory_space=pl.ANY)
CHIP_RELS = (0, 4, 2, 6)

NN = (((1,), (0,)), ((), ()))
NT = (((1,), (1,)), ((), ()))
TN = (((0,), (0,)), ((), ()))


def _tile(n, pref, unit):
    if n <= pref:
        return n
    t = (pref // unit) * unit
    while t >= unit:
        if n % t == 0:
            return t
        t -= unit
    return n


def _dot(a, b, dn, precision=None):
    return lax.dot_general(a, b, dn, preferred_element_type=F32, precision=precision)


def _bdot(a, b, dn):
    return _dot(a.astype(BF16), b.astype(BF16), dn)


def _position():
    x, y, c = lax.axis_index("x"), lax.axis_index("y"), lax.axis_index("c")
    return dict(x=x, y=y, c=c, me=4 * x + 2 * y + c)


def _peer_position(p, rel):
    x = 1 - p["x"] if rel & 4 else p["x"]
    y = 1 - p["y"] if rel & 2 else p["y"]
    c = 1 - p["c"] if rel & 1 else p["c"]
    return dict(x=x, y=y, c=c, me=4 * x + 2 * y + c)


class _Comm:
    def __init__(self):
        self.ins, self.outs, self.alias, self.plans, self.res = [], [], {}, [], None

    def inp(self, arr):
        self.ins.append(arr)
        return ("i", len(self.ins) - 1)

    def out(self, shape, dtype, alias=None):
        self.outs.append(jax.ShapeDtypeStruct(tuple(shape), dtype))
        if alias is not None:
            self.alias[alias[1]] = len(self.outs) - 1
        return ("o", len(self.outs) - 1)

    def copy(self, src, src_view, dst, dst_view, rel):
        self.plans.append((src, src_view, dst, dst_view, rel))

    def result(self, handle):
        return self.res[handle[1]]

    def build(self, in_refs, out_refs, send_sems, recv_sems):
        pos = _position()
        ref = lambda h: in_refs[h[1]] if h[0] == "i" else out_refs[h[1]]
        ops = []
        for k, (src, sv, dst, dv, rel) in enumerate(self.plans):
            s = sv(ref(src), pos)
            if rel == 0:
                cp = pltpu.make_async_copy(s, dv(ref(dst), pos), send_sems.at[k])
                ops.append((cp.start, cp.wait))
                continue
            peer = _peer_position(pos, rel)
            mk = lambda d: pltpu.make_async_remote_copy(
                src_ref=s, dst_ref=d, send_sem=send_sems.at[k], recv_sem=recv_sems.at[k],
                device_id=(peer["x"], peer["y"], peer["c"]), device_id_type=MESH)
            out_cp, in_cp = mk(dv(ref(dst), pos)), mk(dv(ref(dst), peer))

            def wait(out_cp=out_cp, in_cp=in_cp):
                out_cp.wait_send()
                in_cp.wait_recv()

            ops.append((out_cp.start, wait))
        return ops


def _call(body, args, *, name, out_shape, in_specs=None, out_specs=None, grid=None, scratch_shapes=(), comm=None,
          prefetch=None, aliases=None):
    single = not isinstance(out_shape, (tuple, list))
    out_shape = (out_shape,) if single else tuple(out_shape)
    n_in, n_out, n_scr = len(args), len(out_shape), len(scratch_shapes)
    vm = pl.BlockSpec(memory_space=pltpu.VMEM)
    in_specs = [vm] * n_in if in_specs is None else list(in_specs)
    out_specs = [vm] * n_out if out_specs is None else (list(out_specs) if isinstance(out_specs, (tuple, list)) else [out_specs])
    n_pf = 0 if prefetch is None else len(prefetch)
    kw = {} if aliases is None else {"input_output_aliases": dict(aliases)}
    if comm is None:
        fn = body
        all_args, all_scratch = list(args), list(scratch_shapes)
    else:
        n_ci, n_co, n_x = len(comm.ins), len(comm.outs), len(comm.plans)

        def fn(*refs):
            pf, refs = refs[:n_pf], refs[n_pf:]
            o_in, c_in = refs[:n_in], refs[n_in:n_in + n_ci]
            o_out = refs[n_in + n_ci:n_in + n_ci + n_out]
            c_out = refs[n_in + n_ci + n_out:n_in + n_ci + n_out + n_co]
            scr = refs[n_in + n_ci + n_out + n_co:]
            ops = comm.build(c_in, c_out, scr[n_scr], scr[n_scr + 1])
            if grid:
                first = functools.reduce(jnp.logical_and, [pl.program_id(i) == 0 for i in range(len(grid))])
                last = functools.reduce(jnp.logical_and, [pl.program_id(i) == g - 1 for i, g in enumerate(grid)])

                @pl.when(first)
                def _():
                    for start, _w in ops:
                        start()
            else:
                for start, _w in ops:
                    start()
            body(*pf, *o_in, *o_out, *scr[:n_scr])
            if grid:
                @pl.when(last)
                def _():
                    for _s, wait in ops:
                        wait()
            else:
                for _s, wait in ops:
                    wait()

        all_args = list(args) + list(comm.ins)
        in_specs = in_specs + [ANY] * n_ci
        out_shape = out_shape + tuple(comm.outs)
        out_specs = out_specs + [ANY] * n_co
        all_scratch = list(scratch_shapes) + [pltpu.SemaphoreType.DMA((n_x,)), pltpu.SemaphoreType.DMA((n_x,))]
        kw["input_output_aliases"] = {n_pf + n_in + i: n_out + o for i, o in comm.alias.items()}
    sem = None if grid is None else ("arbitrary",) * len(grid)
    params = pltpu.CompilerParams(dimension_semantics=sem, vmem_limit_bytes=VMEM_LIMIT)
    if prefetch is None:
        spec = dict(in_specs=in_specs, out_specs=tuple(out_specs), scratch_shapes=all_scratch)
        if grid is not None:
            spec["grid"] = grid
    else:
        spec = dict(grid_spec=pltpu.PrefetchScalarGridSpec(
            num_scalar_prefetch=n_pf, grid=grid, in_specs=in_specs, out_specs=tuple(out_specs), scratch_shapes=all_scratch))
        all_args = list(prefetch) + all_args
    res = pl.pallas_call(fn, name=name, out_shape=out_shape, compiler_params=params, **spec, **kw)(*all_args)
    res = list(res)
    if comm is not None:
        comm.res = res[n_out:]
        res = res[:n_out]
    return res[0] if single else res


def _whole_view(ref, pos):
    return ref


def _block_view(axis, n, index, rows=None):
    def view(ref, pos):
        off = pl.multiple_of(index(pos) * n, n)
        if rows is None:
            return ref.at[:, pl.ds(off, n)] if axis == 1 else ref.at[pl.ds(off, n), :]
        lo, cnt = rows[0], rows[1] - rows[0]
        if axis == 1:
            return ref.at[pl.ds(lo, cnt), pl.ds(off, n)]
        return ref.at[pl.ds(pl.multiple_of(off + lo, 16), cnt), :]
    return view


def _rows_view(rows):
    def view(ref, pos):
        return ref if rows is None else ref.at[pl.ds(rows[0], rows[1] - rows[0]), :]
    return view


def _slot_view(i, rows=None):
    def view(ref, pos):
        return ref.at[i] if rows is None else ref.at[i, pl.ds(rows[0], rows[1] - rows[0]), :]
    return view


def _exchange(items, name):
    cm = _Comm()
    for a, rel in items:
        cm.copy(cm.inp(a), _whole_view, cm.out(a.shape, a.dtype), _whole_view, rel)
    _call(lambda: None, [], name=name, out_shape=(), comm=cm)
    return cm.res


def _gather_small(v, me, name):
    cm = _Comm()
    hi, ho = cm.inp(v), cm.out((N_DEV,) + v.shape, v.dtype)
    for rel in range(N_DEV):
        cm.copy(hi, _whole_view, ho, lambda ref, p: ref.at[p["me"]], rel)
    _call(lambda: None, [], name=name, out_shape=(), comm=cm)
    return cm.result(ho)


def _ag_ici(cm, blk, axis, rows=None, into=None):
    n = blk.shape[axis]
    shape = list(blk.shape)
    shape[axis] = n * N_DEV
    hi = cm.inp(blk)
    ho = cm.out(shape, blk.dtype) if into is None else cm.out(shape, blk.dtype, alias=cm.inp(into))
    own = _block_view(axis, n, lambda p: p["me"], rows)
    for rel in (0, 4, 2):
        cm.copy(hi, _rows_view(rows), ho, own, rel)
    return ho


def _ag_diag(cm, full, axis):
    n = full.shape[axis] // N_DEV
    r = full.shape[0] if axis == 1 else n
    hi = cm.inp(full)
    ho = cm.out(full.shape, full.dtype, alias=hi)
    top = _block_view(axis, n, lambda p: p["me"] ^ 4, (0, r // 2))
    bot = _block_view(axis, n, lambda p: p["me"] ^ 2, (r // 2, r))
    cm.copy(hi, top, ho, top, 2)
    cm.copy(hi, bot, ho, bot, 4)
    return ho


def _ag_d2d(cm, full, axis):
    n = full.shape[axis] // N_DEV
    hi = cm.inp(full)
    ho = cm.out(full.shape, full.dtype, alias=hi)
    for r in CHIP_RELS:
        v = _block_view(axis, n, functools.partial(lambda p, r: p["me"] ^ r, r=r))
        cm.copy(hi, v, ho, v, 1)
    return ho


def _rs_d2d(cm, gw, axis):
    n = gw.shape[axis] // N_DEV
    shape = list(gw.shape)
    shape[axis] = n
    hi, ho = cm.inp(gw), cm.out([4] + shape, gw.dtype)
    for i, r in enumerate(CHIP_RELS):
        cm.copy(hi, _block_view(axis, n, functools.partial(lambda p, r: p["me"] ^ r ^ 1, r=r)), ho, _slot_view(i), 1)
    return ho


def _rs_ici(cm, part, rows=None, recv=None):
    if recv is None:
        ho = cm.out((3,) + part.shape[1:], part.dtype)
    else:
        ho = cm.out(recv.shape, recv.dtype, alias=cm.inp(recv))
    hi = cm.inp(part)
    for i in (1, 2, 3):
        cm.copy(hi, _slot_view(i, rows), ho, _slot_view(i - 1, rows), CHIP_RELS[i])
    return ho


def _rs_add(gw, recv, axis, base, name, tw=None):
    _, R, n = recv.shape
    if axis == 1:
        tw = n if tw is None else tw
        gw_spec = pl.BlockSpec((R, tw), lambda i, t, b: (0, b[i] + t))
        rv_spec = pl.BlockSpec((None, R, tw), lambda i, t, b: (i, 0, t))
        grid = (4, n // tw)
    else:
        tw = _tile(n, 1024, LANES)
        gw_spec = pl.BlockSpec((R, tw), lambda i, t, b: (b[i], t))
        rv_spec = pl.BlockSpec((None, R, tw), lambda i, t, b: (i, 0, t))
        grid = (4, n // tw)

    def body(b_ref, g_ref, r_ref, o_ref):
        o_ref[...] = (g_ref[...].astype(F32) + r_ref[...].astype(F32)).astype(o_ref.dtype)

    return _call(body, [gw, recv], name=name, out_shape=jax.ShapeDtypeStruct(recv.shape, recv.dtype), grid=grid,
                 in_specs=[gw_spec, rv_spec], out_specs=rv_spec, prefetch=[base])


def _ag_w_in(src, a, D, INW):
    wm = LANES * a

    hd = D // 2
    ALL, TOP, BOT = (0, D), (0, hd), (hd, D)

    def main_place(ref, p, rows=ALL):
        off = pl.multiple_of(((2 * a + 1) * (p["me"] // 2) + (a + 1) * p["c"]) * LANES, LANES)
        return ref.at[pl.ds(rows[0], rows[1] - rows[0]), pl.ds(off, wm)]

    def main_src(ref, p):
        return ref.at[:, pl.ds(pl.multiple_of(p["c"] * LANES, LANES), wm)]

    def mid_src(ref, p):
        return ref.at[:, pl.ds(pl.multiple_of((1 - p["c"]) * wm, LANES), LANES)]

    def mid_place(ref, p, rows=ALL):
        return ref.at[p["me"], pl.ds(rows[0], rows[1] - rows[0]), :]

    def body(src_ref, full_ref, mid_ref, send_sems, recv_sems):
        pos = _position()
        sib, xn, yn = (_peer_position(pos, r) for r in (1, 4, 2))
        dg = _peer_position(pos, 6)
        started = []

        def remote(k, s, d, to):
            return pltpu.make_async_remote_copy(src_ref=s, dst_ref=d, send_sem=send_sems.at[k], recv_sem=recv_sems.at[k],
                                                device_id=(to["x"], to["y"], to["c"]), device_id_type=MESH)

        def send(k, owner, rows, to, from_src=False):
            for j, (src_v, place) in enumerate(((main_src, main_place), (mid_src, mid_place))):
                s = src_v(src_ref, pos) if from_src else place(full_ref if j == 0 else mid_ref, owner, rows)
                cp = remote(k + j, s, place(full_ref if j == 0 else mid_ref, owner, rows), to)
                cp.start()
                started.append(cp)

        def landed(k, owner, rows, frm):
            for j, place in enumerate((main_place, mid_place)):
                ref = full_ref if j == 0 else mid_ref
                remote(k + j, place(ref, owner, rows), place(ref, owner, rows), frm).wait_recv()

        local = [pltpu.make_async_copy(main_src(src_ref, pos), main_place(full_ref, pos), send_sems.at[18]),
                 pltpu.make_async_copy(mid_src(src_ref, pos), mid_place(mid_ref, pos), send_sems.at[19])]
        for cp in local:
            cp.start()
        send(0, pos, ALL, sib, from_src=True)
        send(2, pos, ALL, xn, from_src=True)
        send(4, pos, ALL, yn, from_src=True)
        landed(2, xn, ALL, xn)
        send(10, xn, ALL, sib)
        send(6, xn, TOP, yn)
        landed(4, yn, ALL, yn)
        send(12, yn, ALL, sib)
        send(8, yn, BOT, xn)
        landed(6, dg, TOP, yn)
        send(14, dg, TOP, sib)
        landed(8, dg, BOT, xn)
        send(16, dg, BOT, sib)
        sib_of = lambda p: _peer_position(p, 1)
        landed(0, sib, ALL, sib)
        landed(10, sib_of(xn), ALL, sib)
        landed(12, sib_of(yn), ALL, sib)
        landed(14, sib_of(dg), TOP, sib)
        landed(16, sib_of(dg), BOT, sib)
        for cp in started:
            cp.wait_send()
        for cp in local:
            cp.wait()

    return _call(body, [src], name="ag_w_in", in_specs=[ANY], out_specs=[ANY, ANY],
                 out_shape=(jax.ShapeDtypeStruct((D, INW), BF16), jax.ShapeDtypeStruct((N_DEV, D, LANES), BF16)),
                 scratch_shapes=[pltpu.SemaphoreType.DMA((20,)), pltpu.SemaphoreType.DMA((20,))])


def _patch_mid(full, mid, a):
    D = full.shape[0]

    def body(full_ref, e_ref, o_ref, out_ref):
        out_ref[...] = e_ref[...] + o_ref[...]

    return _call(body, [full, mid, mid], name="patch_mid", grid=(N_DEV // 2,),
                 out_shape=jax.ShapeDtypeStruct(full.shape, full.dtype),
                 in_specs=[ANY, pl.BlockSpec((None, D, LANES), lambda j: (2 * j, 0, 0)),
                           pl.BlockSpec((None, D, LANES), lambda j: (2 * j + 1, 0, 0))],
                 out_specs=pl.BlockSpec((D, LANES), lambda j: (0, (2 * a + 1) * j + a)), aliases={0: 0})


MM_RESIDENT = 2048


def _mm(a, b, mode, out_dtype, name, b_off=0, n=None, comm=None, extras=(), epi=None, tn=None):
    if mode == "nn":
        (M, K), (K2, N) = a.shape, b.shape
    elif mode == "nt":
        (M, K), (N, K2) = a.shape, b.shape
    else:
        (K, M), (K2, N) = a.shape, b.shape
    assert K == K2, (a.shape, b.shape, mode)
    if n is not None:
        N = n
    single = not isinstance(out_dtype, (tuple, list))
    out_dtypes = (out_dtype,) if single else tuple(out_dtype)
    if epi is None:
        epi = lambda r: (r,)
    tk = K if K <= MM_RESIDENT else (MM_RESIDENT if K % MM_RESIDENT == 0 else _tile(K, 512, LANES))
    nk = K // tk
    if M > MM_RESIDENT and mode == "tn" and N <= MM_RESIDENT and not b_off:
        tm, tn = _tile(M, 512, LANES), N
    elif nk > 1:
        tm, tn = _tile(M, 1024, LANES), _tile(N, tn or 1024, LANES)
    else:
        tm = _tile(M, MM_RESIDENT, LANES)
        tn = _tile(math.gcd(N, b_off) if b_off else N, tn or 512, LANES)
    jb = b_off // tn
    dn = {"nn": NN, "nt": NT, "tn": TN}[mode]
    ne, no = len(extras), len(out_dtypes)

    def body(a_ref, b_ref, *rest):
        e_refs, o_refs = rest[:ne], rest[ne:ne + no]

        def finish(r):
            for o_ref, v in zip(o_refs, epi(r, *[e[...] for e in e_refs])):
                o_ref[...] = v.astype(o_ref.dtype)

        if nk == 1:
            finish(_bdot(a_ref[...], b_ref[...], dn))
            return
        acc_ref = rest[ne + no]
        k = pl.program_id(2)

        @pl.when(k == 0)
        def _():
            acc_ref[...] = _bdot(a_ref[...], b_ref[...], dn)

        @pl.when(jnp.logical_and(k > 0, k < nk - 1))
        def _():
            acc_ref[...] += _bdot(a_ref[...], b_ref[...], dn)

        @pl.when(k == nk - 1)
        def _():
            finish(acc_ref[...] + _bdot(a_ref[...], b_ref[...], dn))

    a_spec = pl.BlockSpec((tk, tm), lambda i, j, k: (k, i)) if mode == "tn" else pl.BlockSpec((tm, tk), lambda i, j, k: (i, k))
    b_spec = pl.BlockSpec((tn, tk), lambda i, j, k: (j, k)) if mode == "nt" else pl.BlockSpec((tk, tn), lambda i, j, k: (k, j + jb))
    o_spec = pl.BlockSpec((tm, tn), lambda i, j, k: (i, j))
    res = _call(body, [a, b] + list(extras), name=name, grid=(M // tm, N // tn, nk),
                out_shape=tuple(jax.ShapeDtypeStruct((M, N), dt) for dt in out_dtypes),
                in_specs=[a_spec, b_spec] + [o_spec] * ne, out_specs=[o_spec] * no,
                scratch_shapes=[pltpu.VMEM((tm, tn), F32)] if nk > 1 else [], comm=comm)
    return res[0] if single else res


def _rowwise(fn, row_ins, bcast_ins, row_outs, acc_outs, name, rt=256, comm=None):
    L = row_ins[0][0].shape[-2]
    rt = _tile(L, rt, 16)
    nr, nb, no = len(row_ins), len(bcast_ins), len(row_outs)

    def body(*refs):
        i = pl.program_id(0)
        vals = [r[...] for r in refs[:nr + nb]]
        outs, accs = fn(*vals)
        for r, v in zip(refs[nr + nb:nr + nb + no], outs):
            r[...] = v.astype(r.dtype)
        acc_refs = refs[nr + nb + no:]

        @pl.when(i == 0)
        def _():
            for r in acc_refs:
                r[...] = jnp.zeros_like(r)

        for r, v in zip(acc_refs, accs):
            r[...] += v

    in_specs = []
    for spec in row_ins:
        w, cb = spec[1], spec[2]
        if len(spec) == 4:
            in_specs.append(pl.BlockSpec((None, rt, w), functools.partial(lambda i, cb, ld: (ld, i, cb), cb=cb, ld=spec[3])))
        else:
            in_specs.append(pl.BlockSpec((rt, w), functools.partial(lambda i, cb: (i, cb), cb=cb)))
    in_specs += [pl.BlockSpec(b.shape, lambda i: (0, 0)) for b in bcast_ins]
    out_specs = [pl.BlockSpec((rt, w), lambda i: (i, 0)) for w, _ in row_outs]
    out_specs += [pl.BlockSpec(s, lambda i: (0, 0)) for s in acc_outs]
    out_shape = [jax.ShapeDtypeStruct((L, w), dt) for w, dt in row_outs] + [jax.ShapeDtypeStruct(s, F32) for s in acc_outs]
    return _call(body, [s[0] for s in row_ins] + list(bcast_ins), name=name, grid=(L // rt,), out_shape=tuple(out_shape),
                 in_specs=in_specs, out_specs=out_specs, comm=comm)


def _whole(fn, ins, out_shapes, name):
    def body(*refs):
        outs = fn(*[r[...] for r in refs[:len(ins)]])
        for r, v in zip(refs[len(ins):], outs):
            r[...] = v.astype(r.dtype)

    return _call(body, list(ins), name=name, out_shape=tuple(jax.ShapeDtypeStruct(s, dt) for s, dt in out_shapes))


def _silu(x):
    return x * jax.nn.sigmoid(x)


def _rms(x, g):
    return (x * lax.rsqrt(jnp.mean(x * x, axis=-1, keepdims=True) + EPS)) * g


def _modnorm(x, g, shift, scale):
    return _rms(x, g) * (1.0 + scale) + shift


def _adamw(w, g, m, v):
    m = ADAM_B1 * m + (1.0 - ADAM_B1) * g
    v = ADAM_B2 * v + (1.0 - ADAM_B2) * jnp.square(g)
    m_hat = m / (1.0 - ADAM_B1 ** ADAM_STEP)
    v_hat = v / (1.0 - ADAM_B2 ** ADAM_STEP)
    delta = -ADAM_LR * (m_hat / (jnp.sqrt(v_hat) + ADAM_EPS) + ADAM_WD * w)
    return delta, m, v


def _lower_bound(lg):
    e = jnp.exp(lg - jnp.max(lg, axis=0, keepdims=True))
    return e[0:1] / jnp.sum(e, axis=0, keepdims=True)


def _hg_chunk(hq, hf, hi, lb, st):
    C = hq.shape[0]
    row = lax.broadcasted_iota(jnp.int32, (C, C), 0)
    col = lax.broadcasted_iota(jnp.int32, (C, C), 1)
    tri = row >= col
    sg = jax.nn.sigmoid(hf)
    f = lb + (1.0 - lb) * sg
    lf = jnp.log(f)
    k = 1.0 - f
    q = _silu(hq)
    b = _dot(tri.astype(F32), lf, NN, precision=HIGHEST)
    m = b[C // 2 - 1:C // 2]
    bl = b[C - 1:C]
    e_qm, e_km, e_kl, e_q = jnp.exp(b - m), jnp.exp(m - b), jnp.exp(bl - b), jnp.exp(b)
    qe, ke, kd, qb = q * e_qm, k * e_km, k * e_kl, q * e_q
    sc = jnp.where(tri, _bdot(qe, ke, NT), 0.0)
    o = _bdot(sc, hi, NN) + _bdot(qb, st, NT)
    dec = jnp.exp(bl)
    st_next = st * dec + _bdot(hi, kd, TN)
    return o, st_next, dict(tri=tri, sg=sg, f=f, k=k, q=q, qe=qe, ke=ke, kd=kd, qb=qb, sc=sc, dec=dec,
                            e_qm=e_qm, e_km=e_km, e_kl=e_kl, e_q=e_q)


def _hg_out(o, hgate, gout):
    return _rms(o, gout) * _silu(hgate)


HG_GROUP = 16


def _hgrn_fwd(p4, lb_logits, gout, H, comm=None):
    L = p4.shape[0]
    C = HG_CHUNK
    GR = _tile(L // C, HG_GROUP, 1)
    T = GR * C
    N = L // T

    def body(hq_ref, hf_ref, hi_ref, hg_ref, lg_ref, gout_ref, o_ref, s_ref, st_ref):
        @pl.when(pl.program_id(1) == 0)
        def _():
            st_ref[...] = jnp.zeros_like(st_ref)

        lb = _lower_bound(lg_ref[...])
        st = st_ref[...]
        for ci in range(GR):
            rows = pl.ds(ci * C, C)
            s_ref[0, ci] = st
            o, st, _ = _hg_chunk(hq_ref[rows, :], hf_ref[rows, :], hi_ref[rows, :], lb, st)
            o_ref[rows, :] = _hg_out(o, hg_ref[rows, :], gout_ref[...]).astype(o_ref.dtype)
        st_ref[...] = st

    blk = lambda s: pl.BlockSpec((T, HG_DK), functools.partial(lambda h, n, s: (n, s * H + h), s=s))
    return _call(
        body, [p4, p4, p4, p4, lb_logits, gout], name="hgrn_fwd", grid=(H, N),
        out_shape=(jax.ShapeDtypeStruct((L, H * HG_DK), BF16), jax.ShapeDtypeStruct((H, N * GR, HG_DK, HG_DK), F32)),
        in_specs=[blk(0), blk(1), blk(2), blk(3), pl.BlockSpec((2, HG_DK), lambda h, n: (0, h)),
                  pl.BlockSpec((1, HG_DK), lambda h, n: (0, 0))],
        out_specs=(pl.BlockSpec((T, HG_DK), lambda h, n: (n, h)),
                   pl.BlockSpec((1, GR, HG_DK, HG_DK), lambda h, n: (h, n, 0, 0))),
        scratch_shapes=[pltpu.VMEM((HG_DK, HG_DK), F32)], comm=comm)


def _hgrn_bwd(p4, lb_logits, gout, s_all, d_out, H, comm=None):
    L = p4.shape[0]
    C = HG_CHUNK
    GR = _tile(L // C, HG_GROUP, 1)
    T = GR * C
    N = L // T

    def body(hq_ref, hf_ref, hi_ref, hg_ref, lg_ref, gout_ref, s_ref, do_ref,
             dq_ref, df_ref, di_ref, dg_ref, dlb_ref, dgo_ref, dst_ref):
        @pl.when(pl.program_id(1) == 0)
        def _():
            dst_ref[...] = jnp.zeros_like(dst_ref)
            dlb_ref[...] = jnp.zeros_like(dlb_ref)

        @pl.when(jnp.logical_and(pl.program_id(0) == 0, pl.program_id(1) == 0))
        def _():
            dgo_ref[...] = jnp.zeros_like(dgo_ref)

        lb = _lower_bound(lg_ref[...])
        dst = dst_ref[...]
        d_lb = jnp.zeros((1, HG_DK), F32)
        d_go = jnp.zeros((1, HG_DK), F32)
        for ci in reversed(range(GR)):
            rows = pl.ds(ci * C, C)
            dst, d_lb_c, d_go_c = chunk_bwd(rows, lb, s_ref[0, ci], dst, hq_ref, hf_ref, hi_ref, hg_ref, gout_ref, do_ref,
                                            dq_ref, df_ref, di_ref, dg_ref)
            d_lb += d_lb_c
            d_go += d_go_c
        dst_ref[...] = dst
        dlb_ref[...] += d_lb
        dgo_ref[...] += d_go

    def chunk_bwd(rows, lb, st, dst_next, hq_ref, hf_ref, hi_ref, hg_ref, gout_ref, do_ref, dq_ref, df_ref, di_ref, dg_ref):
        hq, hf, hi, hgate = hq_ref[rows, :], hf_ref[rows, :], hi_ref[rows, :], hg_ref[rows, :]
        o, _, t = _hg_chunk(hq, hf, hi, lb, st)
        _, out_vjp = jax.vjp(_hg_out, o, hgate, gout_ref[...])
        do, d_hgate, d_gout = out_vjp(do_ref[rows, :])
        tri = t["tri"]
        dsc = jnp.where(tri, _bdot(do, hi, NT), 0.0)
        dv = _bdot(t["sc"], do, TN) + _bdot(t["kd"], dst_next, NT)
        dqe = _bdot(dsc, t["ke"], NN)
        dke = _bdot(dsc, t["qe"], TN)
        dqb = _bdot(do, st, NN)
        dkd = _bdot(hi, dst_next, NN)
        ddec = jnp.sum(dst_next * st, axis=0, keepdims=True)
        dst_prev = _bdot(do, t["qb"], TN) + dst_next * t["dec"]
        dq = dqe * t["e_qm"] + dqb * t["e_q"]
        dk = dke * t["e_km"] + dkd * t["e_kl"]
        tq, tk, td, tb = dqe * t["qe"], dke * t["ke"], dkd * t["kd"], dqb * t["qb"]
        db = tq - tk - td + tb
        dm = jnp.sum(tk - tq, axis=0, keepdims=True)
        dbl = jnp.sum(td, axis=0, keepdims=True) + ddec * t["dec"]
        rowi = lax.broadcasted_iota(jnp.int32, (C, HG_DK), 0)
        db = db + jnp.where(rowi == C // 2 - 1, dm, 0.0) + jnp.where(rowi == C - 1, dbl, 0.0)
        dlf = _dot(tri.astype(F32), db, TN, precision=HIGHEST)
        df = dlf / t["f"] - dk
        sg = t["sg"]
        df_ref[rows, :] = (df * (1.0 - lb) * sg * (1.0 - sg)).astype(df_ref.dtype)
        sq = jax.nn.sigmoid(hq)
        dq_ref[rows, :] = (dq * (sq * (1.0 + hq * (1.0 - sq)))).astype(dq_ref.dtype)
        di_ref[rows, :] = dv.astype(di_ref.dtype)
        dg_ref[rows, :] = d_hgate.astype(dg_ref.dtype)
        return dst_prev, jnp.sum(df * (1.0 - sg), axis=0, keepdims=True), d_gout

    blk = lambda s: pl.BlockSpec((T, HG_DK), functools.partial(lambda h, n, s: (N - 1 - n, s * H + h), s=s))
    oblk = pl.BlockSpec((T, HG_DK), lambda h, n: (N - 1 - n, h))
    vec = pl.BlockSpec((1, HG_DK), lambda h, n: (0, h))
    W = H * HG_DK
    return _call(
        body, [p4, p4, p4, p4, lb_logits, gout, s_all, d_out], name="hgrn_bwd", grid=(H, N),
        out_shape=tuple([jax.ShapeDtypeStruct((L, W), BF16)] * 4 + [jax.ShapeDtypeStruct((1, W), F32), jax.ShapeDtypeStruct((1, HG_DK), F32)]),
        in_specs=[blk(0), blk(1), blk(2), blk(3), pl.BlockSpec((2, HG_DK), lambda h, n: (0, h)),
                  pl.BlockSpec((1, HG_DK), lambda h, n: (0, 0)),
                  pl.BlockSpec((1, GR, HG_DK, HG_DK), lambda h, n: (h, N - 1 - n, 0, 0)), oblk],
        out_specs=(oblk, oblk, oblk, oblk, vec, pl.BlockSpec((1, HG_DK), lambda h, n: (0, 0))),
        scratch_shapes=[pltpu.VMEM((HG_DK, HG_DK), F32)], comm=comm)


def _bucket_ids():
    i = jnp.arange(AT_BLOCK, dtype=jnp.int32)[:, None]
    j = jnp.arange(2 * AT_BLOCK, dtype=jnp.int32)[None, :]
    n = jnp.maximum(i - j + AT_BLOCK, 0)
    nf = jnp.maximum(n, 1).astype(F32)
    large = MAX_EXACT + (jnp.log(nf / MAX_EXACT) / math.log(MAX_DISTANCE / MAX_EXACT) * (N_BUCKETS - MAX_EXACT)).astype(jnp.int32)
    large = jnp.minimum(large, N_BUCKETS - 1)
    return jnp.where(n < MAX_EXACT, n, large).reshape(1, -1)


def _onehot(bucket):
    ids = lax.broadcasted_iota(jnp.int32, (N_BUCKETS, bucket.shape[1]), 0)
    return (ids == bucket).astype(F32)


def _attn_probs(qn, kpn, kcn, bias_g, sink, first, scale):
    rows = qn.shape[0]
    i = jnp.bitwise_and(lax.broadcasted_iota(jnp.int32, (rows, AT_BLOCK), 0), AT_BLOCK - 1)
    j = lax.broadcasted_iota(jnp.int32, (rows, AT_BLOCK), 1)
    lp = _bdot(qn, kpn, NT) * scale + bias_g[:, :AT_BLOCK]
    lc = _bdot(qn, kcn, NT) * scale + bias_g[:, AT_BLOCK:]
    lp = jnp.where(jnp.logical_and(j > i, jnp.logical_not(first)), lp, NEG_INF)
    lc = jnp.where(j <= i, lc, NEG_INF)
    m = jnp.maximum(jnp.maximum(jnp.max(lp, axis=-1, keepdims=True), jnp.max(lc, axis=-1, keepdims=True)), sink)
    pp, pc, ps = jnp.exp(lp - m), jnp.exp(lc - m), jnp.exp(sink - m)
    den = jnp.sum(pp, axis=-1, keepdims=True) + jnp.sum(pc, axis=-1, keepdims=True) + ps
    return pp / den, pc / den, ps / den


def _sink_rows(sk_ref, G):
    head = lax.broadcasted_iota(jnp.int32, (G * AT_BLOCK, 1), 0) // AT_BLOCK
    sink = jnp.zeros((G * AT_BLOCK, 1), F32)
    for g in range(G):
        sink = jnp.where(head == g, sk_ref[0, g:g + 1, :], sink)
    return sink


def _attn_fwd(q_t, kp, vp, qg, kg, sinks, bias, KVH, comm=None):
    AH, L, DH = q_t.shape
    G = AH // KVH
    NB = L // AT_BLOCK
    scale = DH ** -0.5

    def body(q_ref, kp_ref, kc_ref, vp_ref, vc_ref, qg_ref, kg_ref, sk_ref, b_ref, o_ref):
        first = pl.program_id(1) == 0
        kpn, kcn = _rms(kp_ref[0], kg_ref[...]), _rms(kc_ref[0], kg_ref[...])
        qn = _rms(q_ref[...].reshape(G * AT_BLOCK, DH), qg_ref[...])
        sink = _sink_rows(sk_ref, G)
        pp, pc, _ = _attn_probs(qn, kpn, kcn, b_ref[...].reshape(G * AT_BLOCK, 2 * AT_BLOCK), sink, first, scale)
        o = _bdot(pp, vp_ref[0], NN) + _bdot(pc, vc_ref[0], NN)
        o_ref[...] = o.reshape(G, AT_BLOCK, DH).astype(o_ref.dtype)

    kblk = lambda off: pl.BlockSpec((1, AT_BLOCK, DH), functools.partial(lambda h, n, off: (h, n + off, 0), off=off))
    return _call(
        body, [q_t, kp, kp, vp, vp, qg, kg, sinks, bias], name="attn_fwd", grid=(KVH, NB),
        out_shape=jax.ShapeDtypeStruct((AH, L, DH), BF16),
        in_specs=[pl.BlockSpec((G, AT_BLOCK, DH), lambda h, n: (h, n, 0)), kblk(0), kblk(1), kblk(0), kblk(1),
                  pl.BlockSpec((1, DH), lambda h, n: (0, 0)), pl.BlockSpec((1, DH), lambda h, n: (0, 0)),
                  pl.BlockSpec((1, G, 1), lambda h, n: (h, 0, 0)),
                  pl.BlockSpec((G, AT_BLOCK, 2 * AT_BLOCK), lambda h, n: (h, 0, 0))],
        out_specs=pl.BlockSpec((G, AT_BLOCK, DH), lambda h, n: (h, n, 0)), comm=comm)


def _attn_bwd(q_t, kp, vp, qg, kg, sinks, bias, do_t, KVH, comm=None):
    AH, L, DH = q_t.shape
    G = AH // KVH
    NB = L // AT_BLOCK
    B = AT_BLOCK
    scale = DH ** -0.5

    def body(q_ref, kp_ref, kc_ref, vp_ref, vc_ref, qg_ref, kg_ref, sk_ref, b_ref, do_ref,
             dq_ref, dk_ref, dv_ref, dqg_ref, dkg_ref, dsk_ref, db_ref):
        n = pl.program_id(1)
        first = n == 0

        @pl.when(first)
        def _():
            for r in (dk_ref, dv_ref, dsk_ref, db_ref):
                r[...] = jnp.zeros_like(r)

        @pl.when(jnp.logical_and(first, pl.program_id(0) == 0))
        def _():
            dqg_ref[...] = jnp.zeros_like(dqg_ref)
            dkg_ref[...] = jnp.zeros_like(dkg_ref)

        kp_raw, kc_raw, kgv, qgv = kp_ref[0], kc_ref[0], kg_ref[...], qg_ref[...]
        kpn, kp_vjp = jax.vjp(_rms, kp_raw, kgv)
        kcn, kc_vjp = jax.vjp(_rms, kc_raw, kgv)
        qn, q_vjp = jax.vjp(_rms, q_ref[...].reshape(G * B, DH), qgv)
        pp, pc, ps = _attn_probs(qn, kpn, kcn, b_ref[...].reshape(G * B, 2 * B), _sink_rows(sk_ref, G), first, scale)
        do = do_ref[...].reshape(G * B, DH)
        dvp = _bdot(pp, do, TN)
        dvc = _bdot(pc, do, TN)
        dpp = _bdot(do, vp_ref[0], NT)
        dpc = _bdot(do, vc_ref[0], NT)
        dsum = jnp.sum(dpp * pp, axis=-1, keepdims=True) + jnp.sum(dpc * pc, axis=-1, keepdims=True)
        dlp = pp * (dpp - dsum)
        dlc = pc * (dpc - dsum)
        dsk_ref[0] += jnp.sum((-ps * dsum).reshape(G, B, 1), axis=1)
        db_ref[:, :, :B] += dlp.reshape(G, B, B)
        db_ref[:, :, B:] += dlc.reshape(G, B, B)
        dlp, dlc = dlp * scale, dlc * scale
        dqn = _bdot(dlp, kpn, NN) + _bdot(dlc, kcn, NN)
        dq_raw, dqg = q_vjp(dqn)
        dq_ref[...] = dq_raw.reshape(G, B, DH).astype(dq_ref.dtype)
        dkp_raw, dkg_p = kp_vjp(_bdot(dlp, qn, TN))
        dkc_raw, dkg_c = kc_vjp(_bdot(dlc, qn, TN))
        r0 = pl.multiple_of(n * B, B)
        r1 = pl.multiple_of(n * B + B, B)
        dk_ref[0, pl.ds(r0, B), :] += dkp_raw
        dk_ref[0, pl.ds(r1, B), :] += dkc_raw
        dv_ref[0, pl.ds(r0, B), :] += dvp
        dv_ref[0, pl.ds(r1, B), :] += dvc
        dqg_ref[...] += dqg
        dkg_ref[...] += dkg_p + dkg_c

    kblk = lambda off: pl.BlockSpec((1, B, DH), functools.partial(lambda h, n, off: (h, n + off, 0), off=off))
    qblk = pl.BlockSpec((G, B, DH), lambda h, n: (h, n, 0))
    accblk = pl.BlockSpec((1, L + B, DH), lambda h, n: (h, 0, 0))
    vecblk = pl.BlockSpec((1, DH), lambda h, n: (0, 0))
    return _call(
        body, [q_t, kp, kp, vp, vp, qg, kg, sinks, bias, do_t], name="attn_bwd", grid=(KVH, NB),
        out_shape=(jax.ShapeDtypeStruct((AH, L, DH), BF16), jax.ShapeDtypeStruct((KVH, L + B, DH), F32),
                   jax.ShapeDtypeStruct((KVH, L + B, DH), F32), jax.ShapeDtypeStruct((1, DH), F32),
                   jax.ShapeDtypeStruct((1, DH), F32), jax.ShapeDtypeStruct((KVH, G, 1), F32),
                   jax.ShapeDtypeStruct((AH, B, 2 * B), F32)),
        in_specs=[qblk, kblk(0), kblk(1), kblk(0), kblk(1),
                  pl.BlockSpec((1, DH), lambda h, n: (0, 0)), pl.BlockSpec((1, DH), lambda h, n: (0, 0)),
                  pl.BlockSpec((1, G, 1), lambda h, n: (h, 0, 0)),
                  pl.BlockSpec((G, B, 2 * B), lambda h, n: (h, 0, 0)), qblk],
        out_specs=(qblk, accblk, accblk, vecblk, vecblk, pl.BlockSpec((1, G, 1), lambda h, n: (h, 0, 0)),
                   pl.BlockSpec((G, B, 2 * B), lambda h, n: (h, 0, 0))), comm=comm)


def _heads_first(t, nh):
    L = t.shape[0]
    return jnp.transpose(t.reshape(L, nh, t.shape[1] // nh), (1, 0, 2))


def _heads_last(t):
    nh, L, dh = t.shape
    return jnp.transpose(t, (1, 0, 2)).reshape(L, nh * dh)


def _softmax0(lg):
    e = jnp.exp(lg - jnp.max(lg, axis=0, keepdims=True))
    return e[0:1] / jnp.sum(e, axis=0, keepdims=True)


def _ada_update_call(fn, c_all, d_cols, w, m, v, rt):
    D, n = w.shape

    def body(c_ref, d_ref, w_ref, m_ref, v_ref, g_out, dl_out, m_out, v_out):
        outs, _ = fn(c_ref[...], d_ref[...], w_ref[...], m_ref[...], v_ref[...])
        for r, val in zip((g_out, dl_out, m_out, v_out), outs):
            r[...] = val

    wblk = pl.BlockSpec((rt, n), lambda i: (i, 0))
    return _call(
        body, [c_all, d_cols, w, m, v], name="update_ada", grid=(D // rt,), out_shape=tuple([jax.ShapeDtypeStruct((D, n), F32)] * 4),
        in_specs=[pl.BlockSpec((N_DEV, rt), lambda i: (0, i)), pl.BlockSpec((N_DEV, n), lambda i: (0, 0)), wblk, wblk, wblk],
        out_specs=(wblk, wblk, wblk, wblk))


def kernel(x, c, w_ada, b_ada, norm1_g, norm2_g, w_in, hg_lb_logits, hg_out_norm_g, q_norm_g, k_norm_g, attn_sinks, rel_bias_table, w_branch_hg, w_branch_attn, w_out, w_ff1, w_ff2, loss_target, m_w_ada, m_b_ada, m_norm1_g, m_norm2_g, m_w_in, m_hg_lb_logits, m_hg_out_norm_g, m_q_norm_g, m_k_norm_g, m_attn_sinks, m_rel_bias_table, m_w_branch_hg, m_w_branch_attn, m_w_out, m_w_ff1, m_w_ff2, v_w_ada, v_b_ada, v_norm1_g, v_norm2_g, v_w_in, v_hg_lb_logits, v_hg_out_norm_g, v_q_norm_g, v_k_norm_g, v_attn_sinks, v_rel_bias_table, v_w_branch_hg, v_w_branch_attn, v_w_out, v_w_ff1, v_w_ff2):
    cc = lax.axis_index("c")
    me = 4 * lax.axis_index("x") + 2 * lax.axis_index("y") + cc
    x2 = x[0]
    tgt = loss_target[0]
    L, D = x2.shape
    HGW = hg_lb_logits.shape[1]
    H = HGW // HG_DK
    AH = attn_sinks.shape[1]
    DH = q_norm_g.shape[1]
    ATW = AH * DH
    BW = w_in.shape[2]
    INW = BW * N_DEV
    A = BW // LANES
    assert BW == LANES * A + LANES // 2
    KVW = (INW - 4 * HGW - ATW - 2 * D) // 2
    KVH = KVW // DH
    G = AH // KVH
    ADA_N = w_ada.shape[2]
    PAIR = 2 * A + 1

    w_in_b = w_in[0].astype(BF16)
    src_in = jnp.where(cc == 0, jnp.pad(w_in_b, ((0, 0), (0, LANES // 2))), jnp.pad(w_in_b, ((0, 0), (LANES // 2, 0))))
    w_in_gapped, w_in_mid = _ag_w_in(src_in, A, D, INW)
    w_in_full = _patch_mid(w_in_gapped, w_in_mid, A)

    c_all = _gather_small(c, me, "gather_c")[:, 0, :]
    b_cols = lax.dynamic_slice(b_ada, (0, me * ADA_N), (1, ADA_N))
    (ada_cols,) = _whole(lambda cv, w, b: (_bdot(_silu(cv), w, NN) + b,), [c_all, w_ada[0], b_cols],
                         [((N_DEV, ADA_N), F32)], "ada_fwd")
    ada_all = _gather_small(ada_cols, me, "gather_ada")
    ada_row = lax.dynamic_slice(ada_all, (0, me, 0), (N_DEV, 1, ADA_N)).reshape(1, 6 * D)
    shift1, scale1, gate1, shift2, scale2, gate2 = [ada_row[:, i * D:(i + 1) * D] for i in range(6)]

    wnames = ("bhg", "bat", "out", "ff1", "ff2")
    waxis = dict(zip(wnames, (1, 1, 0, 1, 0)))
    wsrc = dict(zip(wnames, (w_branch_hg, w_branch_attn, w_out, w_ff1, w_ff2)))
    wblk = {k: wsrc[k][0].astype(BF16) for k in wnames}
    wf = {}

    (h,) = _rowwise(lambda xv, g, sh, sc: ((_modnorm(xv, g, sh, sc),), ()), [(x2, D, 0)], [norm1_g, shift1, scale1],
                    [(D, BF16)], [], "norm1")
    o4, oa = 4 * HGW, 4 * HGW + ATW + 2 * KVW
    r1, r2, ro = wblk["ff1"].shape[0], wblk["ff2"].shape[0], wblk["out"].shape[0]
    small = ("bhg", "bat", "out")
    cm = _Comm()
    hs = {k: _ag_ici(cm, wblk[k], waxis[k]) for k in small}
    p4 = _mm(h, w_in_full, "nn", F32, "proj_hg", n=o4, comm=cm)
    half = {k: cm.result(hs[k]) for k in hs}
    cm = _Comm()
    hs = {k: _ag_diag(cm, half[k], waxis[k]) for k in small}
    pa = _mm(h, w_in_full, "nn", F32, "proj_at", b_off=o4, n=oa - o4, comm=cm)
    half = {k: cm.result(hs[k]) for k in hs}
    cm = _Comm()
    hs = {k: _ag_d2d(cm, half[k], waxis[k]) for k in small}
    hs["ff1"] = _ag_ici(cm, wblk["ff1"], waxis["ff1"], rows=(0, r1 // 2))
    pg = _mm(h, w_in_full, "nn", F32, "proj_gate", b_off=oa, n=INW - oa, comm=cm)
    wf["bhg"], wf["bat"], wf["out"], half["ff1"] = (cm.result(hs[k]) for k in ("bhg", "bat", "out", "ff1"))

    cm = _Comm()
    hs = {"ff1": _ag_ici(cm, wblk["ff1"], waxis["ff1"], rows=(r1 // 2, r1), into=half["ff1"]),
          "ff2": _ag_ici(cm, wblk["ff2"], waxis["ff2"], rows=(0, r2 // 2))}
    o_hg, s_all = _hgrn_fwd(p4, hg_lb_logits, hg_out_norm_g, H, comm=cm)
    half["ff1"], half["ff2"] = cm.result(hs["ff1"]), cm.result(hs["ff2"])

    bucket = _bucket_ids()
    (bias_flat,) = _whole(lambda tb, bk: (_dot(tb, _onehot(bk), TN, precision=HIGHEST),), [rel_bias_table, bucket],
                          [((AH, AT_BLOCK * 2 * AT_BLOCK), F32)], "bias_fwd")
    bias = bias_flat.reshape(AH, AT_BLOCK, 2 * AT_BLOCK)
    q_t = _heads_first(pa[:, :ATW], AH)
    pad = lambda t: jnp.pad(t, ((0, 0), (AT_BLOCK, 0), (0, 0)))
    kp = pad(_heads_first(pa[:, ATW:ATW + KVW], KVH))
    vp = pad(_heads_first(pa[:, ATW + KVW:], KVH))
    sinks3 = attn_sinks.reshape(KVH, G, 1)
    cm = _Comm()
    hs = {"ff1": _ag_diag(cm, half["ff1"], waxis["ff1"]),
          "ff2": _ag_ici(cm, wblk["ff2"], waxis["ff2"], rows=(r2 // 2, r2), into=half["ff2"])}
    o_at = _heads_last(_attn_fwd(q_t, kp, vp, q_norm_g, k_norm_g, sinks3, bias, KVH, comm=cm))
    half["ff1"], half["ff2"] = cm.result(hs["ff1"]), cm.result(hs["ff2"])

    bh = _mm(o_hg, wf["bhg"], "nn", F32, "branch_hg")
    ba = _mm(o_at, wf["bat"], "nn", F32, "branch_at")

    def merge_fn(bhv, bav, ghg, gat):
        return jax.nn.sigmoid(ghg) * bhv + jax.nn.sigmoid(gat) * bav

    cm = _Comm()
    hs = {"ff1": _ag_d2d(cm, half["ff1"], waxis["ff1"])}
    (merged,) = _rowwise(lambda *a: ((merge_fn(*a),), ()), [(bh, D, 0), (ba, D, 0), (pg, D, 0), (pg, D, 1)], [],
                         [(D, BF16)], [], "merge", comm=cm)
    wf["ff1"] = cm.result(hs["ff1"])
    cm = _Comm()
    hs = {"ff2": _ag_diag(cm, half["ff2"], waxis["ff2"])}
    mo = _mm(merged, wf["out"], "nn", F32, "out_proj", comm=cm)
    half["ff2"] = cm.result(hs["ff2"])

    def resid1(xv, mov, g1, g2n, sh, sc):
        x1v = xv + g1 * mov
        return (x1v, _modnorm(x1v, g2n, sh, sc)), ()

    x1, h2 = _rowwise(resid1, [(x2, D, 0), (mo, D, 0)], [gate1, norm2_g, shift2, scale2], [(D, F32), (D, BF16)], [], "resid1")
    cm = _Comm()
    hs = {"ff2": _ag_d2d(cm, half["ff2"], waxis["ff2"])}
    u, act = _mm(h2, wf["ff1"], "nn", (F32, BF16), "ff1", comm=cm, epi=lambda r: (r, jnp.square(jnp.maximum(r, 0.0))))
    wf["ff2"] = cm.result(hs["ff2"])
    ff = _mm(act, wf["ff2"], "nn", F32, "ff2")

    def loss_fn(x1v, ffv, tv, g2):
        e = x1v + g2 * ffv - tv
        dy = e * (1.0 / D)
        return (dy, dy * g2), (jnp.sum(e * e, axis=0, keepdims=True), jnp.sum(dy * ffv, axis=0, keepdims=True))

    dy, d_ff, sq_sum, d_gate2 = _rowwise(loss_fn, [(x1, D, 0), (ff, D, 0), (tgt, D, 0)], [gate2],
                                         [(D, F32), (D, BF16)], [(1, D), (1, D)], "loss")
    loss = lax.psum(jnp.sum(sq_sum) * (0.5 / D), ("x", "y", "c"))

    owner_base = jnp.stack([me ^ r for r in CHIP_RELS]).astype(jnp.int32)
    gw, recv1, part, recv2 = {}, {}, {}, {}
    gw["ff2"] = _mm(act, d_ff, "tn", BF16, "dw_ff2")
    cm = _Comm()
    hh = _rs_d2d(cm, gw["ff2"], waxis["ff2"])
    d_u = _mm(d_ff, wf["ff2"], "nt", BF16, "d_act", comm=cm, extras=[u], epi=lambda r, uv: (r * (2.0 * jnp.maximum(uv, 0.0)),))
    part["ff2"] = _rs_add(gw["ff2"], cm.result(hh), waxis["ff2"], owner_base, "rs_add_ff2")
    rows_ff2 = part["ff2"].shape[1]
    cm = _Comm()
    hh = _rs_ici(cm, part["ff2"], rows=(0, rows_ff2 // 2))
    gw["ff1"] = _mm(h2, d_u, "tn", BF16, "dw_ff1", comm=cm)
    cm2 = _Comm()
    hh2 = _rs_ici(cm2, part["ff2"], rows=(rows_ff2 // 2, rows_ff2), recv=cm.result(hh))
    hh1 = _rs_d2d(cm2, gw["ff1"], waxis["ff1"])
    d_h2 = _mm(d_u, wf["ff1"], "nt", F32, "d_h2", comm=cm2)
    recv2["ff2"] = cm2.result(hh2)
    part["ff1"] = _rs_add(gw["ff1"], cm2.result(hh1), waxis["ff1"], owner_base, "rs_add_ff1")

    def norm2_bwd(dh2v, x1v, dyv, mov, g2n, sh, sc, g1):
        _, vjp = jax.vjp(_modnorm, x1v, g2n, sh, sc)
        dx, dg, dsh, dsc = vjp(dh2v)
        dx1 = dyv + dx
        return (dx1, dx1 * g1), (dg, dsh, dsc, jnp.sum(dx1 * mov, axis=0, keepdims=True))

    d_x1, d_mo, d_g2n, d_shift2, d_scale2, d_gate1 = _rowwise(
        norm2_bwd, [(d_h2, D, 0), (x1, D, 0), (dy, D, 0), (mo, D, 0)], [norm2_g, shift2, scale2, gate1],
        [(D, F32), (D, BF16)], [(1, D)] * 4, "norm2_bwd")
    gw["out"] = _mm(merged, d_mo, "tn", BF16, "dw_out")
    cm = _Comm()
    hh = _rs_d2d(cm, gw["out"], waxis["out"])
    d_merged = _mm(d_mo, wf["out"], "nt", F32, "d_merged", comm=cm)
    part["out"] = _rs_add(gw["out"], cm.result(hh), waxis["out"], owner_base, "rs_add_out")

    def merge_bwd(dmv, bhv, bav, ghg, gat):
        _, vjp = jax.vjp(merge_fn, bhv, bav, ghg, gat)
        return vjp(dmv), ()

    d_bh, d_ba, d_ghg, d_gat = _rowwise(merge_bwd, [(d_merged, D, 0), (bh, D, 0), (ba, D, 0), (pg, D, 0), (pg, D, 1)], [],
                                        [(D, BF16)] * 4, [], "merge_bwd")
    gw["bhg"] = _mm(o_hg, d_bh, "tn", BF16, "dw_bhg")
    gw["bat"] = _mm(o_at, d_ba, "tn", BF16, "dw_bat")
    cm = _Comm()
    hh = {k: _rs_d2d(cm, gw[k], waxis[k]) for k in ("bhg", "bat")}
    d_ohg = _mm(d_bh, wf["bhg"], "nt", F32, "d_ohg", comm=cm)
    for k in ("bhg", "bat"):
        part[k] = _rs_add(gw[k], cm.result(hh[k]), waxis[k], owner_base, "rs_add_" + k)
    d_oat = _mm(d_ba, wf["bat"], "nt", BF16, "d_oat")

    cm = _Comm()
    hh = {"ff1": _rs_ici(cm, part["ff1"])}
    d_hq, d_hf, d_hi, d_hg, d_lb, d_gout_h = _hgrn_bwd(p4, hg_lb_logits, hg_out_norm_g, s_all, d_ohg, H, comm=cm)
    recv2["ff1"] = cm.result(hh["ff1"])
    cm = _Comm()
    hh = {k: _rs_ici(cm, part[k]) for k in ("out", "bhg", "bat")}
    dq_t, dkp, dvp, d_qg, d_kg, d_sk, d_bias = _attn_bwd(q_t, kp, vp, q_norm_g, k_norm_g, sinks3, bias,
                                                         _heads_first(d_oat, AH), KVH, comm=cm)
    for k in hh:
        recv2[k] = cm.result(hh[k])
    d_aq = _heads_last(dq_t)
    d_ak = _heads_last(dkp[:, AT_BLOCK:, :]).astype(BF16)
    d_av = _heads_last(dvp[:, AT_BLOCK:, :]).astype(BF16)
    d_proj = jnp.concatenate([d_hq, d_hf, d_hi, d_hg, d_aq, d_ak, d_av, d_ghg, d_gat], axis=1)
    gw_in = _mm(h, d_proj, "tn", BF16, "dw_in")

    wm = LANES * A
    cm = _Comm()
    hi_ = cm.inp(gw_in)
    h_main, h_mid = cm.out((4, D, wm), BF16), cm.out((4, D, LANES), BF16)
    for i, r in enumerate(CHIP_RELS):
        def main_view(ref, p, r=r):
            o = p["me"] ^ r ^ 1
            return ref.at[:, pl.ds(pl.multiple_of((PAIR * (o // 2) + (A + 1) * (1 - p["c"])) * LANES, LANES), wm)]

        def mid_view(ref, p, r=r):
            o = p["me"] ^ r
            return ref.at[:, pl.ds(pl.multiple_of((PAIR * (o // 2) + A) * LANES, LANES), LANES)]

        cm.copy(hi_, main_view, h_main, _slot_view(i), 1)
        cm.copy(hi_, mid_view, h_mid, _slot_view(i), 1)
    _call(lambda: None, [], name="rs_d2d_in", out_shape=(), comm=cm)
    chip = jnp.stack([(me ^ r) // 2 for r in CHIP_RELS]).astype(jnp.int32)
    part_main = _rs_add(gw_in, cm.result(h_main), 1, PAIR * chip + (A + 1) * cc, "rs_add_in_main", tw=LANES)
    part_mid = _rs_add(gw_in, cm.result(h_mid), 1, PAIR * chip + A, "rs_add_in_mid", tw=LANES)
    cm = _Comm()
    hh_main, hh_mid = _rs_ici(cm, part_main), _rs_ici(cm, part_mid)
    d_h = _mm(d_proj, w_in_full, "nt", F32, "d_h", comm=cm)
    rx_main, rx_mid = cm.result(hh_main), cm.result(hh_mid)

    def norm1_bwd(dhv, xv, dx1v, g1n, sh, sc):
        _, vjp = jax.vjp(_modnorm, xv, g1n, sh, sc)
        dx, dg, dsh, dsc = vjp(dhv)
        return (dx1v + dx,), (dg, dsh, dsc)

    grad_x, d_g1n, d_shift1, d_scale1 = _rowwise(norm1_bwd, [(d_h, D, 0), (x2, D, 0), (d_x1, D, 0)],
                                                 [norm1_g, shift1, scale1], [(D, F32)], [(1, D)] * 3, "norm1_bwd")

    def sum4(p0, p1, p2, p3):
        return ((p0.astype(F32) + p1.astype(F32)) + p2.astype(F32)) + p3.astype(F32)

    def update_fn(w, m, v, p0, p1, p2, p3):
        g = sum4(p0, p1, p2, p3)
        delta, mn, vn = _adamw(w, g, m, v)
        return (g, delta, mn, vn), ()

    wmv = dict(zip(wnames, ((w_branch_hg, m_w_branch_hg, v_w_branch_hg), (w_branch_attn, m_w_branch_attn, v_w_branch_attn),
                            (w_out, m_w_out, v_w_out), (w_ff1, m_w_ff1, v_w_ff1), (w_ff2, m_w_ff2, v_w_ff2))))
    res = {}
    for k in wnames:
        w, m, v = (t[0] for t in wmv[k])
        n = w.shape[1]
        ins = [(t, n, 0) for t in (w, m, v)] + [(part[k], n, 0, 0)] + [(recv2[k], n, 0, i) for i in range(3)]
        res[k] = [t[None] for t in _rowwise(update_fn, ins, [], [(n, F32)] * 4, [], "update_" + k)]

    g_main, = _rowwise(lambda *p: ((sum4(*p),), ()), [(part_main, wm, 0, 0)] + [(rx_main, wm, 0, i) for i in range(3)], [],
                       [(wm, F32)], [], "sum_in_main")
    g_mid, = _rowwise(lambda *p: ((sum4(*p),), ()), [(part_mid, LANES, 0, 0)] + [(rx_mid, LANES, 0, i) for i in range(3)], [],
                      [(LANES, F32)], [], "sum_in_mid")
    g_in = jnp.where(cc == 0, jnp.concatenate([g_main, g_mid[:, :LANES // 2]], axis=1),
                     jnp.concatenate([g_mid[:, LANES // 2:], g_main], axis=1))

    def update_given(w, m, v, g):
        delta, mn, vn = _adamw(w, g, m, v)
        return (g, delta, mn, vn), ()

    res["in"] = [t[None] for t in _rowwise(update_given, [(t, BW, 0) for t in (w_in[0], m_w_in[0], v_w_in[0], g_in)], [],
                                           [(BW, F32)] * 4, [], "update_in")]

    d_ada_row = jnp.concatenate([d_shift1, d_scale1, d_gate1, d_shift2, d_scale2, d_gate2], axis=1)
    d_ada_all = _gather_small(d_ada_row, me, "gather_dada")[:, 0, :]
    d_ada_cols = lax.dynamic_slice(d_ada_all, (0, me * ADA_N), (N_DEV, ADA_N))

    def ada_update(cv, dav, w, m, v):
        g = _bdot(_silu(cv), dav, TN)
        delta, mn, vn = _adamw(w, g, m, v)
        return (g, delta, mn, vn), ()

    res["ada"] = [t[None] for t in _ada_update_call(ada_update, c_all, d_ada_cols, w_ada[0], m_w_ada[0], v_w_ada[0], _tile(D, 256, 16))]

    d_sinks = d_sk.reshape(1, AH)
    (d_table_t,) = _whole(lambda db, bk: (_dot(db, _onehot(bk), NT, precision=HIGHEST),),
                          [d_bias.reshape(AH, AT_BLOCK * 2 * AT_BLOCK), bucket], [((AH, N_BUCKETS), F32)], "bias_bwd")
    smalls = [d_g1n, d_g2n, d_lb, d_gout_h, d_qg, d_kg, d_sinks, d_table_t.T.reshape(1, N_BUCKETS * AH)]
    widths = [s.shape[1] for s in smalls]
    lanes = [-(-w // LANES) * LANES for w in widths]
    smalls = [jnp.pad(s, ((0, 0), (0, p - w))) for s, w, p in zip(smalls, widths, lanes)]
    packed = _gather_small(jnp.concatenate(smalls, axis=1), me, "gather_small")[:, 0, :]
    offs = [sum(lanes[:i]) for i in range(len(lanes))]

    def small_update(pk, dada, lg, *wmv_flat):
        tot = pk[0:1]
        for d in range(1, N_DEV):
            tot = tot + pk[d:d + 1]
        gb = dada[0:1]
        for d in range(1, N_DEV):
            gb = gb + dada[d:d + 1]
        gs = [tot[:, offs[i]:offs[i] + widths[i]] for i in range(len(widths))]
        _, lb_vjp = jax.vjp(_softmax0, lg)
        (g_lg,) = lb_vjp(gs[2])
        grads = [gb, gs[0], gs[1], g_lg, gs[3], gs[4], gs[5], gs[6], gs[7]]
        outs = []
        for i, g in enumerate(grads):
            w, m, v = wmv_flat[3 * i:3 * i + 3]
            delta, mn, vn = _adamw(w, g, m, v)
            outs += [g, delta, mn, vn]
        return tuple(outs)

    tbl = lambda t: t.reshape(1, N_BUCKETS * AH)
    small_wmv = [(b_ada, m_b_ada, v_b_ada), (norm1_g, m_norm1_g, v_norm1_g), (norm2_g, m_norm2_g, v_norm2_g),
                 (hg_lb_logits, m_hg_lb_logits, v_hg_lb_logits), (hg_out_norm_g, m_hg_out_norm_g, v_hg_out_norm_g),
                 (q_norm_g, m_q_norm_g, v_q_norm_g), (k_norm_g, m_k_norm_g, v_k_norm_g),
                 (attn_sinks, m_attn_sinks, v_attn_sinks),
                 (tbl(rel_bias_table), tbl(m_rel_bias_table), tbl(v_rel_bias_table))]
    flat = [t for trip in small_wmv for t in trip]
    out_shapes = [(trip[0].shape, F32) for trip in small_wmv for _ in range(4)]
    sres = _whole(small_update, [packed, d_ada_all, hg_lb_logits] + flat, out_shapes, "small_update")
    names_small = ("b_ada", "norm1_g", "norm2_g", "lb", "gout", "qg", "kg", "sinks", "table")
    for i, k in enumerate(names_small):
        r = sres[4 * i:4 * i + 4]
        if k == "table":
            r = [t.reshape(N_BUCKETS, AH) for t in r]
        res[k] = r

    order = ("ada", "b_ada", "norm1_g", "norm2_g", "in", "lb", "gout", "qg", "kg", "sinks", "table", "bhg", "bat", "out", "ff1", "ff2")
    outs = [loss, grad_x[None]]
    for j in range(4):
        outs += [res[k][j] for k in order]
    return tuple(outs)
```

```python
import functools
import math

import jax
import jax.numpy as jnp
from jax import lax
from jax.experimental import pallas as pl
from jax.experimental.pallas import tpu as pltpu

F32 = jnp.float32
BF16 = jnp.bfloat16
EPS = 1e-6
NEG_INF = -1e30
HG_DK = 128
HG_CHUNK = 64
AT_BLOCK = 128
N_BUCKETS = 32
MAX_EXACT = 16
MAX_DISTANCE = 128
N_DEV = 8
LANES = 128
VMEM_LIMIT = 56 * 1024 * 1024
ADAM_LR, ADAM_B1, ADAM_B2, ADAM_EPS, ADAM_WD, ADAM_STEP = 0.001, 0.9, 0.999, 1e-08, 0.01, 10
HIGHEST = lax.Precision.HIGHEST
MESH = pl.DeviceIdType.MESH
ANY = pl.BlockSpec(memory_space=pl.ANY)
CHIP_RELS = (0, 4, 2, 6)

NN = (((1,), (0,)), ((), ()))
NT = (((1,), (1,)), ((), ()))
TN = (((0,), (0,)), ((), ()))


def _tile(n, pref, unit):
    if n <= pref:
        return n
    t = (pref // unit) * unit
    while t >= unit:
        if n % t == 0:
            return t
        t -= unit
    return n


def _dot(a, b, dn, precision=None):
    return lax.dot_general(a, b, dn, preferred_element_type=F32, precision=precision)


def _bdot(a, b, dn):
    return _dot(a.astype(BF16), b.astype(BF16), dn)


def _position():
    x, y, c = lax.axis_index("x"), lax.axis_index("y"), lax.axis_index("c")
    return dict(x=x, y=y, c=c, me=4 * x + 2 * y + c)


def _peer_position(p, rel):
    x = 1 - p["x"] if rel & 4 else p["x"]
    y = 1 - p["y"] if rel & 2 else p["y"]
    c = 1 - p["c"] if rel & 1 else p["c"]
    return dict(x=x, y=y, c=c, me=4 * x + 2 * y + c)


class _Comm:
    def __init__(self):
        self.ins, self.outs, self.alias, self.plans, self.res = [], [], {}, [], None

    def inp(self, arr):
        self.ins.append(arr)
        return ("i", len(self.ins) - 1)

    def out(self, shape, dtype, alias=None):
        self.outs.append(jax.ShapeDtypeStruct(tuple(shape), dtype))
        if alias is not None:
            self.alias[alias[1]] = len(self.outs) - 1
        return ("o", len(self.outs) - 1)

    def copy(self, src, src_view, dst, dst_view, rel):
        self.plans.append((src, src_view, dst, dst_view, rel))

    def result(self, handle):
        return self.res[handle[1]]

    def build(self, in_refs, out_refs, send_sems, recv_sems):
        pos = _position()
        ref = lambda h: in_refs[h[1]] if h[0] == "i" else out_refs[h[1]]
        ops = []
        for k, (src, sv, dst, dv, rel) in enumerate(self.plans):
            s = sv(ref(src), pos)
            if rel == 0:
                cp = pltpu.make_async_copy(s, dv(ref(dst), pos), send_sems.at[k])
                ops.append((cp.start, cp.wait))
                continue
            peer = _peer_position(pos, rel)
            mk = lambda d: pltpu.make_async_remote_copy(
                src_ref=s, dst_ref=d, send_sem=send_sems.at[k], recv_sem=recv_sems.at[k],
                device_id=(peer["x"], peer["y"], peer["c"]), device_id_type=MESH)
            out_cp, in_cp = mk(dv(ref(dst), pos)), mk(dv(ref(dst), peer))

            def wait(out_cp=out_cp, in_cp=in_cp):
                out_cp.wait_send()
                in_cp.wait_recv()

            ops.append((out_cp.start, wait))
        return ops


def _call(body, args, *, name, out_shape, in_specs=None, out_specs=None, grid=None, scratch_shapes=(), comm=None,
          prefetch=None, aliases=None, after=()):
    single = not isinstance(out_shape, (tuple, list))
    out_shape = (out_shape,) if single else tuple(out_shape)
    n_in, n_out, n_scr = len(args), len(out_shape), len(scratch_shapes)
    vm = pl.BlockSpec(memory_space=pltpu.VMEM)
    in_specs = [vm] * n_in if in_specs is None else list(in_specs)
    out_specs = [vm] * n_out if out_specs is None else (list(out_specs) if isinstance(out_specs, (tuple, list)) else [out_specs])
    n_pf = 0 if prefetch is None else len(prefetch)
    kw = {} if aliases is None else {"input_output_aliases": dict(aliases)}
    if comm is None and after:
        n_dep = len(after)

        def fn(*refs):
            body(*refs[:n_pf + n_in], *refs[n_pf + n_in + n_dep:])

        all_args, all_scratch = list(args) + list(after), list(scratch_shapes)
        in_specs = in_specs + [ANY] * n_dep
    elif comm is None:
        fn = body
        all_args, all_scratch = list(args), list(scratch_shapes)
    else:
        n_ci, n_co, n_x = len(comm.ins), len(comm.outs), len(comm.plans)

        def fn(*refs):
            pf, refs = refs[:n_pf], refs[n_pf:]
            o_in, c_in = refs[:n_in], refs[n_in:n_in + n_ci]
            o_out = refs[n_in + n_ci:n_in + n_ci + n_out]
            c_out = refs[n_in + n_ci + n_out:n_in + n_ci + n_out + n_co]
            scr = refs[n_in + n_ci + n_out + n_co:]
            ops = comm.build(c_in, c_out, scr[n_scr], scr[n_scr + 1])
            if grid:
                first = functools.reduce(jnp.logical_and, [pl.program_id(i) == 0 for i in range(len(grid))])
                last = functools.reduce(jnp.logical_and, [pl.program_id(i) == g - 1 for i, g in enumerate(grid)])

                @pl.when(first)
                def _():
                    for start, _w in ops:
                        start()
            else:
                for start, _w in ops:
                    start()
            body(*pf, *o_in, *o_out, *scr[:n_scr])
            if grid:
                @pl.when(last)
                def _():
                    for _s, wait in ops:
                        wait()
            else:
                for _s, wait in ops:
                    wait()

        all_args = list(args) + list(comm.ins)
        in_specs = in_specs + [ANY] * n_ci
        out_shape = out_shape + tuple(comm.outs)
        out_specs = out_specs + [ANY] * n_co
        all_scratch = list(scratch_shapes) + [pltpu.SemaphoreType.DMA((n_x,)), pltpu.SemaphoreType.DMA((n_x,))]
        kw["input_output_aliases"] = {n_pf + n_in + i: n_out + o for i, o in comm.alias.items()}
    sem = None if grid is None else ("arbitrary",) * len(grid)
    params = pltpu.CompilerParams(dimension_semantics=sem, vmem_limit_bytes=VMEM_LIMIT)
    if prefetch is None:
        spec = dict(in_specs=in_specs, out_specs=tuple(out_specs), scratch_shapes=all_scratch)
        if grid is not None:
            spec["grid"] = grid
    else:
        spec = dict(grid_spec=pltpu.PrefetchScalarGridSpec(
            num_scalar_prefetch=n_pf, grid=grid, in_specs=in_specs, out_specs=tuple(out_specs), scratch_shapes=all_scratch))
        all_args = list(prefetch) + all_args
    res = pl.pallas_call(fn, name=name, out_shape=out_shape, compiler_params=params, **spec, **kw)(*all_args)
    res = list(res)
    if comm is not None:
        comm.res = res[n_out:]
        res = res[:n_out]
    return res[0] if single else res


def _whole_view(ref, pos):
    return ref


def _block_view(axis, n, index, rows=None):
    def view(ref, pos):
        off = pl.multiple_of(index(pos) * n, n)
        if rows is None:
            return ref.at[:, pl.ds(off, n)] if axis == 1 else ref.at[pl.ds(off, n), :]
        lo, cnt = rows[0], rows[1] - rows[0]
        if axis == 1:
            return ref.at[pl.ds(lo, cnt), pl.ds(off, n)]
        return ref.at[pl.ds(pl.multiple_of(off + lo, 16), cnt), :]
    return view


def _rows_view(rows):
    def view(ref, pos):
        return ref if rows is None else ref.at[pl.ds(rows[0], rows[1] - rows[0]), :]
    return view


def _slot_view(i, rows=None):
    def view(ref, pos):
        return ref.at[i] if rows is None else ref.at[i, pl.ds(rows[0], rows[1] - rows[0]), :]
    return view


def _exchange(items, name):
    cm = _Comm()
    for a, rel in items:
        cm.copy(cm.inp(a), _whole_view, cm.out(a.shape, a.dtype), _whole_view, rel)
    _call(lambda: None, [], name=name, out_shape=(), comm=cm)
    return cm.res


def _gather_small(v, me, name):
    cm = _Comm()
    hi, ho = cm.inp(v), cm.out((N_DEV,) + v.shape, v.dtype)
    for rel in range(N_DEV):
        cm.copy(hi, _whole_view, ho, lambda ref, p: ref.at[p["me"]], rel)
    _call(lambda: None, [], name=name, out_shape=(), comm=cm)
    return cm.result(ho)


def _ag_ici(cm, blk, axis, rows=None, into=None):
    n = blk.shape[axis]
    shape = list(blk.shape)
    shape[axis] = n * N_DEV
    hi = cm.inp(blk)
    ho = cm.out(shape, blk.dtype) if into is None else cm.out(shape, blk.dtype, alias=cm.inp(into))
    own = _block_view(axis, n, lambda p: p["me"], rows)
    for rel in CHIP_RELS:
        cm.copy(hi, _rows_view(rows), ho, own, rel)
    return ho


def _ag_d2d(cm, full, axis):
    n = full.shape[axis] // N_DEV
    hi = cm.inp(full)
    ho = cm.out(full.shape, full.dtype, alias=hi)
    for r in CHIP_RELS:
        v = _block_view(axis, n, functools.partial(lambda p, r: p["me"] ^ r, r=r))
        cm.copy(hi, v, ho, v, 1)
    return ho


def _rs_d2d(cm, gw, axis):
    n = gw.shape[axis] // N_DEV
    shape = list(gw.shape)
    shape[axis] = n
    hi, ho = cm.inp(gw), cm.out([4] + shape, gw.dtype)
    for i, r in enumerate(CHIP_RELS):
        cm.copy(hi, _block_view(axis, n, functools.partial(lambda p, r: p["me"] ^ r ^ 1, r=r)), ho, _slot_view(i), 1)
    return ho


def _rs_ici(cm, part, rows=None, recv=None):
    if recv is None:
        ho = cm.out((3,) + part.shape[1:], part.dtype)
    else:
        ho = cm.out(recv.shape, recv.dtype, alias=cm.inp(recv))
    hi = cm.inp(part)
    for i in (1, 2, 3):
        cm.copy(hi, _slot_view(i, rows), ho, _slot_view(i - 1, rows), CHIP_RELS[i])
    return ho


def _rs_add(gw, recv, axis, base, name, tw=None):
    _, R, n = recv.shape
    if axis == 1:
        tw = n if tw is None else tw
        gw_spec = pl.BlockSpec((R, tw), lambda i, t, b: (0, b[i] + t))
        rv_spec = pl.BlockSpec((None, R, tw), lambda i, t, b: (i, 0, t))
        grid = (4, n // tw)
    else:
        tw = _tile(n, 1024, LANES)
        gw_spec = pl.BlockSpec((R, tw), lambda i, t, b: (b[i], t))
        rv_spec = pl.BlockSpec((None, R, tw), lambda i, t, b: (i, 0, t))
        grid = (4, n // tw)

    def body(b_ref, g_ref, r_ref, o_ref):
        o_ref[...] = (g_ref[...].astype(F32) + r_ref[...].astype(F32)).astype(o_ref.dtype)

    return _call(body, [gw, recv], name=name, out_shape=jax.ShapeDtypeStruct(recv.shape, recv.dtype), grid=grid,
                 in_specs=[gw_spec, rv_spec], out_specs=rv_spec, prefetch=[base])


HBM_SPEC = pl.BlockSpec(memory_space=pltpu.HBM)
SEM_SPEC = pl.BlockSpec(memory_space=pltpu.SEMAPHORE)
SPLIT_PARAMS = pltpu.CompilerParams(has_side_effects=pltpu.SideEffectType.DATAFLOW_SIDE_EFFECTING)


def _split_copies(refs, plans, send_sems, recv_sems):
    pos = _position()
    out = []
    for k, (si, sv, li, lv, rel) in enumerate(plans):
        peer = _peer_position(pos, rel)
        mk = lambda d: pltpu.make_async_remote_copy(
            src_ref=sv(refs[si], pos), dst_ref=d, send_sem=send_sems.at[k], recv_sem=recv_sems.at[k],
            device_id=(peer["x"], peer["y"], peer["c"]), device_id_type=MESH)
        out.append((mk(lv(refs[li], pos)), mk(lv(refs[li], peer))))
    return out


def _split_start(arrays, plans, name):
    n = len(arrays)

    def body(*refs):
        send_sems, recv_sems = refs[n], refs[n + 1]
        for out_cp, _ in _split_copies(refs[:n], plans, send_sems, recv_sems):
            out_cp.start()
        refs[-1][...] = jnp.zeros_like(refs[-1])

    sems = pltpu.SemaphoreType.DMA((len(plans),))
    res = pl.pallas_call(
        body, name=name,
        out_shape=(sems, sems) + tuple(pltpu.HBM(a.shape, a.dtype) for a in arrays) + (jax.ShapeDtypeStruct((8, LANES), F32),),
        in_specs=[HBM_SPEC] * n, out_specs=(SEM_SPEC, SEM_SPEC) + (HBM_SPEC,) * n + (pl.BlockSpec(memory_space=pltpu.VMEM),),
        input_output_aliases={i: 2 + i for i in range(n)}, compiler_params=SPLIT_PARAMS,
    )(*[pltpu.with_memory_space_constraint(a, pltpu.HBM) for a in arrays])
    return res[0], res[1], list(res[2:2 + n]), res[-1]


def _split_wait(send_sems, recv_sems, arrays, plans, after, name):
    n, na = len(arrays), len(after)

    def body(*refs):
        for out_cp, in_cp in _split_copies(refs[:n], plans, refs[n], refs[n + 1]):
            out_cp.wait_send()
            in_cp.wait_recv()

    res = pl.pallas_call(
        body, name=name, out_shape=tuple(pltpu.HBM(a.shape, a.dtype) for a in arrays),
        in_specs=[HBM_SPEC] * n + [SEM_SPEC, SEM_SPEC] + [ANY] * na, out_specs=(HBM_SPEC,) * n,
        input_output_aliases={i: i for i in range(n)}, compiler_params=SPLIT_PARAMS,
    )(*arrays, send_sems, recv_sems, *after)
    return list(res)


def _ag_w_in(src, a, D, INW):
    wm = LANES * a

    hd = D // 2
    ALL, TOP, BOT = (0, D), (0, hd), (hd, D)

    def main_place(ref, p, rows=ALL):
        off = pl.multiple_of(((2 * a + 1) * (p["me"] // 2) + (a + 1) * p["c"]) * LANES, LANES)
        return ref.at[pl.ds(rows[0], rows[1] - rows[0]), pl.ds(off, wm)]

    def main_src(ref, p):
        return ref.at[:, pl.ds(pl.multiple_of(p["c"] * LANES, LANES), wm)]

    def mid_src(ref, p):
        return ref.at[:, pl.ds(pl.multiple_of((1 - p["c"]) * wm, LANES), LANES)]

    def mid_place(ref, p, rows=ALL):
        return ref.at[p["me"], pl.ds(rows[0], rows[1] - rows[0]), :]

    def body(src_ref, full_ref, mid_ref, send_sems, recv_sems):
        pos = _position()
        sib, xn, yn = (_peer_position(pos, r) for r in (1, 4, 2))
        dg = _peer_position(pos, 6)
        started = []

        def remote(k, s, d, to):
            return pltpu.make_async_remote_copy(src_ref=s, dst_ref=d, send_sem=send_sems.at[k], recv_sem=recv_sems.at[k],
                                                device_id=(to["x"], to["y"], to["c"]), device_id_type=MESH)

        def send(k, owner, rows, to, from_src=False):
            for j, (src_v, place) in enumerate(((main_src, main_place), (mid_src, mid_place))):
                s = src_v(src_ref, pos) if from_src else place(full_ref if j == 0 else mid_ref, owner, rows)
                cp = remote(k + j, s, place(full_ref if j == 0 else mid_ref, owner, rows), to)
                cp.start()
                started.append(cp)

        def landed(k, owner, rows, frm):
            for j, place in enumerate((main_place, mid_place)):
                ref = full_ref if j == 0 else mid_ref
                remote(k + j, place(ref, owner, rows), place(ref, owner, rows), frm).wait_recv()

        local = [pltpu.make_async_copy(main_src(src_ref, pos), main_place(full_ref, pos), send_sems.at[18]),
                 pltpu.make_async_copy(mid_src(src_ref, pos), mid_place(mid_ref, pos), send_sems.at[19])]
        for cp in local:
            cp.start()
        send(0, pos, ALL, sib, from_src=True)
        send(2, pos, ALL, xn, from_src=True)
        send(4, pos, ALL, yn, from_src=True)
        landed(2, xn, ALL, xn)
        send(10, xn, ALL, sib)
        send(6, xn, TOP, yn)
        landed(4, yn, ALL, yn)
        send(12, yn, ALL, sib)
        send(8, yn, BOT, xn)
        landed(6, dg, TOP, yn)
        send(14, dg, TOP, sib)
        landed(8, dg, BOT, xn)
        send(16, dg, BOT, sib)
        sib_of = lambda p: _peer_position(p, 1)
        landed(0, sib, ALL, sib)
        landed(10, sib_of(xn), ALL, sib)
        landed(12, sib_of(yn), ALL, sib)
        landed(14, sib_of(dg), TOP, sib)
        landed(16, sib_of(dg), BOT, sib)
        for cp in started:
            cp.wait_send()
        for cp in local:
            cp.wait()

    return _call(body, [src], name="ag_w_in", in_specs=[ANY], out_specs=[ANY, ANY],
                 out_shape=(jax.ShapeDtypeStruct((D, INW), BF16), jax.ShapeDtypeStruct((N_DEV, D, LANES), BF16)),
                 scratch_shapes=[pltpu.SemaphoreType.DMA((20,)), pltpu.SemaphoreType.DMA((20,))])


def _patch_mid(full, mid, a):
    D = full.shape[0]

    def body(full_ref, e_ref, o_ref, out_ref):
        out_ref[...] = e_ref[...] + o_ref[...]

    return _call(body, [full, mid, mid], name="patch_mid", grid=(N_DEV // 2,),
                 out_shape=jax.ShapeDtypeStruct(full.shape, full.dtype),
                 in_specs=[ANY, pl.BlockSpec((None, D, LANES), lambda j: (2 * j, 0, 0)),
                           pl.BlockSpec((None, D, LANES), lambda j: (2 * j + 1, 0, 0))],
                 out_specs=pl.BlockSpec((D, LANES), lambda j: (0, (2 * a + 1) * j + a)), aliases={0: 0})


MM_RESIDENT = 2048


def _mm(a, b, mode, out_dtype, name, b_off=0, n=None, comm=None, extras=(), epi=None, tn=None, after=()):
    if mode == "nn":
        (M, K), (K2, N) = a.shape, b.shape
    elif mode == "nt":
        (M, K), (N, K2) = a.shape, b.shape
    else:
        (K, M), (K2, N) = a.shape, b.shape
    assert K == K2, (a.shape, b.shape, mode)
    if n is not None:
        N = n
    single = not isinstance(out_dtype, (tuple, list))
    out_dtypes = (out_dtype,) if single else tuple(out_dtype)
    if epi is None:
        epi = lambda r: (r,)
    tk = K if K <= MM_RESIDENT else (MM_RESIDENT if K % MM_RESIDENT == 0 else _tile(K, 512, LANES))
    nk = K // tk
    if M > MM_RESIDENT and mode == "tn" and N <= MM_RESIDENT and not b_off:
        tm, tn = _tile(M, 512, LANES), N
    elif nk > 1:
        tm, tn = _tile(M, 1024, LANES), _tile(N, tn or 1024, LANES)
    else:
        tm = _tile(M, MM_RESIDENT, LANES)
        tn = _tile(math.gcd(N, b_off) if b_off else N, tn or 512, LANES)
    jb = b_off // tn
    dn = {"nn": NN, "nt": NT, "tn": TN}[mode]
    ne, no = len(extras), len(out_dtypes)

    def body(a_ref, b_ref, *rest):
        e_refs, o_refs = rest[:ne], rest[ne:ne + no]

        def finish(r):
            for o_ref, v in zip(o_refs, epi(r, *[e[...] for e in e_refs])):
                o_ref[...] = v.astype(o_ref.dtype)

        if nk == 1:
            finish(_bdot(a_ref[...], b_ref[...], dn))
            return
        acc_ref = rest[ne + no]
        k = pl.program_id(2)

        @pl.when(k == 0)
        def _():
            acc_ref[...] = _bdot(a_ref[...], b_ref[...], dn)

        @pl.when(jnp.logical_and(k > 0, k < nk - 1))
        def _():
            acc_ref[...] += _bdot(a_ref[...], b_ref[...], dn)

        @pl.when(k == nk - 1)
        def _():
            finish(acc_ref[...] + _bdot(a_ref[...], b_ref[...], dn))

    a_spec = pl.BlockSpec((tk, tm), lambda i, j, k: (k, i)) if mode == "tn" else pl.BlockSpec((tm, tk), lambda i, j, k: (i, k))
    b_spec = pl.BlockSpec((tn, tk), lambda i, j, k: (j, k)) if mode == "nt" else pl.BlockSpec((tk, tn), lambda i, j, k: (k, j + jb))
    o_spec = pl.BlockSpec((tm, tn), lambda i, j, k: (i, j))
    res = _call(body, [a, b] + list(extras), name=name, grid=(M // tm, N // tn, nk),
                out_shape=tuple(jax.ShapeDtypeStruct((M, N), dt) for dt in out_dtypes),
                in_specs=[a_spec, b_spec] + [o_spec] * ne, out_specs=[o_spec] * no,
                scratch_shapes=[pltpu.VMEM((tm, tn), F32)] if nk > 1 else [], comm=comm, after=after)
    return res[0] if single else res


def _rowwise(fn, row_ins, bcast_ins, row_outs, acc_outs, name, rt=256, comm=None):
    L = row_ins[0][0].shape[-2]
    rt = _tile(L, rt, 16)
    nr, nb, no = len(row_ins), len(bcast_ins), len(row_outs)

    def body(*refs):
        i = pl.program_id(0)
        vals = [r[...] for r in refs[:nr + nb]]
        outs, accs = fn(*vals)
        for r, v in zip(refs[nr + nb:nr + nb + no], outs):
            r[...] = v.astype(r.dtype)
        acc_refs = refs[nr + nb + no:]

        @pl.when(i == 0)
        def _():
            for r in acc_refs:
                r[...] = jnp.zeros_like(r)

        for r, v in zip(acc_refs, accs):
            r[...] += v

    in_specs = []
    for spec in row_ins:
        w, cb = spec[1], spec[2]
        if len(spec) == 4:
            in_specs.append(pl.BlockSpec((None, rt, w), functools.partial(lambda i, cb, ld: (ld, i, cb), cb=cb, ld=spec[3])))
        else:
            in_specs.append(pl.BlockSpec((rt, w), functools.partial(lambda i, cb: (i, cb), cb=cb)))
    in_specs += [pl.BlockSpec(b.shape, lambda i: (0, 0)) for b in bcast_ins]
    out_specs = [pl.BlockSpec((rt, w), lambda i: (i, 0)) for w, _ in row_outs]
    out_specs += [pl.BlockSpec(s, lambda i: (0, 0)) for s in acc_outs]
    out_shape = [jax.ShapeDtypeStruct((L, w), dt) for w, dt in row_outs] + [jax.ShapeDtypeStruct(s, F32) for s in acc_outs]
    return _call(body, [s[0] for s in row_ins] + list(bcast_ins), name=name, grid=(L // rt,), out_shape=tuple(out_shape),
                 in_specs=in_specs, out_specs=out_specs, comm=comm)


def _whole(fn, ins, out_shapes, name):
    def body(*refs):
        outs = fn(*[r[...] for r in refs[:len(ins)]])
        for r, v in zip(refs[len(ins):], outs):
            r[...] = v.astype(r.dtype)

    return _call(body, list(ins), name=name, out_shape=tuple(jax.ShapeDtypeStruct(s, dt) for s, dt in out_shapes))


def _silu(x):
    return x * jax.nn.sigmoid(x)


def _rms(x, g):
    return (x * lax.rsqrt(jnp.mean(x * x, axis=-1, keepdims=True) + EPS)) * g


def _modnorm(x, g, shift, scale):
    return _rms(x, g) * (1.0 + scale) + shift


def _adamw(w, g, m, v):
    m = ADAM_B1 * m + (1.0 - ADAM_B1) * g
    v = ADAM_B2 * v + (1.0 - ADAM_B2) * jnp.square(g)
    m_hat = m / (1.0 - ADAM_B1 ** ADAM_STEP)
    v_hat = v / (1.0 - ADAM_B2 ** ADAM_STEP)
    delta = -ADAM_LR * (m_hat / (jnp.sqrt(v_hat) + ADAM_EPS) + ADAM_WD * w)
    return delta, m, v


def _lower_bound(lg):
    e = jnp.exp(lg - jnp.max(lg, axis=0, keepdims=True))
    return e[0:1] / jnp.sum(e, axis=0, keepdims=True)


def _hg_chunk(hq, hf, hi, lb, st):
    C = hq.shape[0]
    row = lax.broadcasted_iota(jnp.int32, (C, C), 0)
    col = lax.broadcasted_iota(jnp.int32, (C, C), 1)
    tri = row >= col
    sg = jax.nn.sigmoid(hf)
    f = lb + (1.0 - lb) * sg
    lf = jnp.log(f)
    k = 1.0 - f
    q = _silu(hq)
    b = _dot(tri.astype(F32), lf, NN, precision=HIGHEST)
    m = b[C // 2 - 1:C // 2]
    bl = b[C - 1:C]
    e_qm, e_km, e_kl, e_q = jnp.exp(b - m), jnp.exp(m - b), jnp.exp(bl - b), jnp.exp(b)
    qe, ke, kd, qb = q * e_qm, k * e_km, k * e_kl, q * e_q
    sc = jnp.where(tri, _bdot(qe, ke, NT), 0.0)
    o = _bdot(sc, hi, NN) + _bdot(qb, st, NT)
    dec = jnp.exp(bl)
    st_next = st * dec + _bdot(hi, kd, TN)
    return o, st_next, dict(tri=tri, sg=sg, f=f, k=k, q=q, qe=qe, ke=ke, kd=kd, qb=qb, sc=sc, dec=dec,
                            e_qm=e_qm, e_km=e_km, e_kl=e_kl, e_q=e_q)


def _hg_out(o, hgate, gout):
    return _rms(o, gout) * _silu(hgate)


HG_GROUP = 16


def _hgrn_fwd(p4, lb_logits, gout, H, comm=None):
    L = p4.shape[0]
    C = HG_CHUNK
    GR = _tile(L // C, HG_GROUP, 1)
    T = GR * C
    N = L // T

    def body(hq_ref, hf_ref, hi_ref, hg_ref, lg_ref, gout_ref, o_ref, s_ref, st_ref):
        @pl.when(pl.program_id(1) == 0)
        def _():
            st_ref[...] = jnp.zeros_like(st_ref)

        lb = _lower_bound(lg_ref[...])
        st = st_ref[...]
        for ci in range(GR):
            rows = pl.ds(ci * C, C)
            s_ref[0, ci] = st
            o, st, _ = _hg_chunk(hq_ref[rows, :], hf_ref[rows, :], hi_ref[rows, :], lb, st)
            o_ref[rows, :] = _hg_out(o, hg_ref[rows, :], gout_ref[...]).astype(o_ref.dtype)
        st_ref[...] = st

    blk = lambda s: pl.BlockSpec((T, HG_DK), functools.partial(lambda h, n, s: (n, s * H + h), s=s))
    return _call(
        body, [p4, p4, p4, p4, lb_logits, gout], name="hgrn_fwd", grid=(H, N),
        out_shape=(jax.ShapeDtypeStruct((L, H * HG_DK), BF16), jax.ShapeDtypeStruct((H, N * GR, HG_DK, HG_DK), F32)),
        in_specs=[blk(0), blk(1), blk(2), blk(3), pl.BlockSpec((2, HG_DK), lambda h, n: (0, h)),
                  pl.BlockSpec((1, HG_DK), lambda h, n: (0, 0))],
        out_specs=(pl.BlockSpec((T, HG_DK), lambda h, n: (n, h)),
                   pl.BlockSpec((1, GR, HG_DK, HG_DK), lambda h, n: (h, n, 0, 0))),
        scratch_shapes=[pltpu.VMEM((HG_DK, HG_DK), F32)], comm=comm)


def _hgrn_bwd(p4, lb_logits, gout, s_all, d_out, H, comm=None):
    L = p4.shape[0]
    C = HG_CHUNK
    GR = _tile(L // C, HG_GROUP, 1)
    T = GR * C
    N = L // T

    def body(hq_ref, hf_ref, hi_ref, hg_ref, lg_ref, gout_ref, s_ref, do_ref,
             dq_ref, df_ref, di_ref, dg_ref, dlb_ref, dgo_ref, dst_ref):
        @pl.when(pl.program_id(1) == 0)
        def _():
            dst_ref[...] = jnp.zeros_like(dst_ref)
            dlb_ref[...] = jnp.zeros_like(dlb_ref)

        @pl.when(jnp.logical_and(pl.program_id(0) == 0, pl.program_id(1) == 0))
        def _():
            dgo_ref[...] = jnp.zeros_like(dgo_ref)

        lb = _lower_bound(lg_ref[...])
        dst = dst_ref[...]
        d_lb = jnp.zeros((1, HG_DK), F32)
        d_go = jnp.zeros((1, HG_DK), F32)
        for ci in reversed(range(GR)):
            rows = pl.ds(ci * C, C)
            dst, d_lb_c, d_go_c = chunk_bwd(rows, lb, s_ref[0, ci], dst, hq_ref, hf_ref, hi_ref, hg_ref, gout_ref, do_ref,
                                            dq_ref, df_ref, di_ref, dg_ref)
            d_lb += d_lb_c
            d_go += d_go_c
        dst_ref[...] = dst
        dlb_ref[...] += d_lb
        dgo_ref[...] += d_go

    def chunk_bwd(rows, lb, st, dst_next, hq_ref, hf_ref, hi_ref, hg_ref, gout_ref, do_ref, dq_ref, df_ref, di_ref, dg_ref):
        hq, hf, hi, hgate = hq_ref[rows, :], hf_ref[rows, :], hi_ref[rows, :], hg_ref[rows, :]
        o, _, t = _hg_chunk(hq, hf, hi, lb, st)
        _, out_vjp = jax.vjp(_hg_out, o, hgate, gout_ref[...])
        do, d_hgate, d_gout = out_vjp(do_ref[rows, :])
        tri = t["tri"]
        dsc = jnp.where(tri, _bdot(do, hi, NT), 0.0)
        dv = _bdot(t["sc"], do, TN) + _bdot(t["kd"], dst_next, NT)
        dqe = _bdot(dsc, t["ke"], NN)
        dke = _bdot(dsc, t["qe"], TN)
        dqb = _bdot(do, st, NN)
        dkd = _bdot(hi, dst_next, NN)
        ddec = jnp.sum(dst_next * st, axis=0, keepdims=True)
        dst_prev = _bdot(do, t["qb"], TN) + dst_next * t["dec"]
        dq = dqe * t["e_qm"] + dqb * t["e_q"]
        dk = dke * t["e_km"] + dkd * t["e_kl"]
        tq, tk, td, tb = dqe * t["qe"], dke * t["ke"], dkd * t["kd"], dqb * t["qb"]
        db = tq - tk - td + tb
        dm = jnp.sum(tk - tq, axis=0, keepdims=True)
        dbl = jnp.sum(td, axis=0, keepdims=True) + ddec * t["dec"]
        rowi = lax.broadcasted_iota(jnp.int32, (C, HG_DK), 0)
        db = db + jnp.where(rowi == C // 2 - 1, dm, 0.0) + jnp.where(rowi == C - 1, dbl, 0.0)
        dlf = _dot(tri.astype(F32), db, TN, precision=HIGHEST)
        df = dlf / t["f"] - dk
        sg = t["sg"]
        df_ref[rows, :] = (df * (1.0 - lb) * sg * (1.0 - sg)).astype(df_ref.dtype)
        sq = jax.nn.sigmoid(hq)
        dq_ref[rows, :] = (dq * (sq * (1.0 + hq * (1.0 - sq)))).astype(dq_ref.dtype)
        di_ref[rows, :] = dv.astype(di_ref.dtype)
        dg_ref[rows, :] = d_hgate.astype(dg_ref.dtype)
        return dst_prev, jnp.sum(df * (1.0 - sg), axis=0, keepdims=True), d_gout

    blk = lambda s: pl.BlockSpec((T, HG_DK), functools.partial(lambda h, n, s: (N - 1 - n, s * H + h), s=s))
    oblk = pl.BlockSpec((T, HG_DK), lambda h, n: (N - 1 - n, h))
    vec = pl.BlockSpec((1, HG_DK), lambda h, n: (0, h))
    W = H * HG_DK
    return _call(
        body, [p4, p4, p4, p4, lb_logits, gout, s_all, d_out], name="hgrn_bwd", grid=(H, N),
        out_shape=tuple([jax.ShapeDtypeStruct((L, W), BF16)] * 4 + [jax.ShapeDtypeStruct((1, W), F32), jax.ShapeDtypeStruct((1, HG_DK), F32)]),
        in_specs=[blk(0), blk(1), blk(2), blk(3), pl.BlockSpec((2, HG_DK), lambda h, n: (0, h)),
                  pl.BlockSpec((1, HG_DK), lambda h, n: (0, 0)),
                  pl.BlockSpec((1, GR, HG_DK, HG_DK), lambda h, n: (h, N - 1 - n, 0, 0)), oblk],
        out_specs=(oblk, oblk, oblk, oblk, vec, pl.BlockSpec((1, HG_DK), lambda h, n: (0, 0))),
        scratch_shapes=[pltpu.VMEM((HG_DK, HG_DK), F32)], comm=comm)


def _bucket_ids():
    i = jnp.arange(AT_BLOCK, dtype=jnp.int32)[:, None]
    j = jnp.arange(2 * AT_BLOCK, dtype=jnp.int32)[None, :]
    n = jnp.maximum(i - j + AT_BLOCK, 0)
    nf = jnp.maximum(n, 1).astype(F32)
    large = MAX_EXACT + (jnp.log(nf / MAX_EXACT) / math.log(MAX_DISTANCE / MAX_EXACT) * (N_BUCKETS - MAX_EXACT)).astype(jnp.int32)
    large = jnp.minimum(large, N_BUCKETS - 1)
    return jnp.where(n < MAX_EXACT, n, large).reshape(1, -1)


def _onehot(bucket):
    ids = lax.broadcasted_iota(jnp.int32, (N_BUCKETS, bucket.shape[1]), 0)
    return (ids == bucket).astype(F32)


def _attn_probs(qn, kpn, kcn, bias_g, sink, first, scale):
    rows = qn.shape[0]
    i = jnp.bitwise_and(lax.broadcasted_iota(jnp.int32, (rows, AT_BLOCK), 0), AT_BLOCK - 1)
    j = lax.broadcasted_iota(jnp.int32, (rows, AT_BLOCK), 1)
    lp = _bdot(qn, kpn, NT) * scale + bias_g[:, :AT_BLOCK]
    lc = _bdot(qn, kcn, NT) * scale + bias_g[:, AT_BLOCK:]
    lp = jnp.where(jnp.logical_and(j > i, jnp.logical_not(first)), lp, NEG_INF)
    lc = jnp.where(j <= i, lc, NEG_INF)
    m = jnp.maximum(jnp.maximum(jnp.max(lp, axis=-1, keepdims=True), jnp.max(lc, axis=-1, keepdims=True)), sink)
    pp, pc, ps = jnp.exp(lp - m), jnp.exp(lc - m), jnp.exp(sink - m)
    den = jnp.sum(pp, axis=-1, keepdims=True) + jnp.sum(pc, axis=-1, keepdims=True) + ps
    return pp / den, pc / den, ps / den


def _sink_rows(sk_ref, G):
    head = lax.broadcasted_iota(jnp.int32, (G * AT_BLOCK, 1), 0) // AT_BLOCK
    sink = jnp.zeros((G * AT_BLOCK, 1), F32)
    for g in range(G):
        sink = jnp.where(head == g, sk_ref[0, g:g + 1, :], sink)
    return sink


def _attn_fwd(q_t, kp, vp, qg, kg, sinks, bias, KVH, comm=None):
    AH, L, DH = q_t.shape
    G = AH // KVH
    NB = L // AT_BLOCK
    scale = DH ** -0.5

    def body(q_ref, kp_ref, kc_ref, vp_ref, vc_ref, qg_ref, kg_ref, sk_ref, b_ref, o_ref):
        first = pl.program_id(1) == 0
        kpn, kcn = _rms(kp_ref[0], kg_ref[...]), _rms(kc_ref[0], kg_ref[...])
        qn = _rms(q_ref[...].reshape(G * AT_BLOCK, DH), qg_ref[...])
        sink = _sink_rows(sk_ref, G)
        pp, pc, _ = _attn_probs(qn, kpn, kcn, b_ref[...].reshape(G * AT_BLOCK, 2 * AT_BLOCK), sink, first, scale)
        o = _bdot(pp, vp_ref[0], NN) + _bdot(pc, vc_ref[0], NN)
        o_ref[...] = o.reshape(G, AT_BLOCK, DH).astype(o_ref.dtype)

    kblk = lambda off: pl.BlockSpec((1, AT_BLOCK, DH), functools.partial(lambda h, n, off: (h, n + off, 0), off=off))
    return _call(
        body, [q_t, kp, kp, vp, vp, qg, kg, sinks, bias], name="attn_fwd", grid=(KVH, NB),
        out_shape=jax.ShapeDtypeStruct((AH, L, DH), BF16),
        in_specs=[pl.BlockSpec((G, AT_BLOCK, DH), lambda h, n: (h, n, 0)), kblk(0), kblk(1), kblk(0), kblk(1),
                  pl.BlockSpec((1, DH), lambda h, n: (0, 0)), pl.BlockSpec((1, DH), lambda h, n: (0, 0)),
                  pl.BlockSpec((1, G, 1), lambda h, n: (h, 0, 0)),
                  pl.BlockSpec((G, AT_BLOCK, 2 * AT_BLOCK), lambda h, n: (h, 0, 0))],
        out_specs=pl.BlockSpec((G, AT_BLOCK, DH), lambda h, n: (h, n, 0)), comm=comm)


def _attn_bwd(q_t, kp, vp, qg, kg, sinks, bias, do_t, KVH, comm=None):
    AH, L, DH = q_t.shape
    G = AH // KVH
    NB = L // AT_BLOCK
    B = AT_BLOCK
    scale = DH ** -0.5

    def body(q_ref, kp_ref, kc_ref, vp_ref, vc_ref, qg_ref, kg_ref, sk_ref, b_ref, do_ref,
             dq_ref, dk_ref, dv_ref, dqg_ref, dkg_ref, dsk_ref, db_ref):
        n = pl.program_id(1)
        first = n == 0

        @pl.when(first)
        def _():
            for r in (dk_ref, dv_ref, dsk_ref, db_ref):
                r[...] = jnp.zeros_like(r)

        @pl.when(jnp.logical_and(first, pl.program_id(0) == 0))
        def _():
            dqg_ref[...] = jnp.zeros_like(dqg_ref)
            dkg_ref[...] = jnp.zeros_like(dkg_ref)

        kp_raw, kc_raw, kgv, qgv = kp_ref[0], kc_ref[0], kg_ref[...], qg_ref[...]
        kpn, kp_vjp = jax.vjp(_rms, kp_raw, kgv)
        kcn, kc_vjp = jax.vjp(_rms, kc_raw, kgv)
        qn, q_vjp = jax.vjp(_rms, q_ref[...].reshape(G * B, DH), qgv)
        pp, pc, ps = _attn_probs(qn, kpn, kcn, b_ref[...].reshape(G * B, 2 * B), _sink_rows(sk_ref, G), first, scale)
        do = do_ref[...].reshape(G * B, DH)
        dvp = _bdot(pp, do, TN)
        dvc = _bdot(pc, do, TN)
        dpp = _bdot(do, vp_ref[0], NT)
        dpc = _bdot(do, vc_ref[0], NT)
        dsum = jnp.sum(dpp * pp, axis=-1, keepdims=True) + jnp.sum(dpc * pc, axis=-1, keepdims=True)
        dlp = pp * (dpp - dsum)
        dlc = pc * (dpc - dsum)
        dsk_ref[0] += jnp.sum((-ps * dsum).reshape(G, B, 1), axis=1)
        db_ref[:, :, :B] += dlp.reshape(G, B, B)
        db_ref[:, :, B:] += dlc.reshape(G, B, B)
        dlp, dlc = dlp * scale, dlc * scale
        dqn = _bdot(dlp, kpn, NN) + _bdot(dlc, kcn, NN)
        dq_raw, dqg = q_vjp(dqn)
        dq_ref[...] = dq_raw.reshape(G, B, DH).astype(dq_ref.dtype)
        dkp_raw, dkg_p = kp_vjp(_bdot(dlp, qn, TN))
        dkc_raw, dkg_c = kc_vjp(_bdot(dlc, qn, TN))
        r0 = pl.multiple_of(n * B, B)
        r1 = pl.multiple_of(n * B + B, B)
        dk_ref[0, pl.ds(r0, B), :] += dkp_raw
        dk_ref[0, pl.ds(r1, B), :] += dkc_raw
        dv_ref[0, pl.ds(r0, B), :] += dvp
        dv_ref[0, pl.ds(r1, B), :] += dvc
        dqg_ref[...] += dqg
        dkg_ref[...] += dkg_p + dkg_c

    kblk = lambda off: pl.BlockSpec((1, B, DH), functools.partial(lambda h, n, off: (h, n + off, 0), off=off))
    qblk = pl.BlockSpec((G, B, DH), lambda h, n: (h, n, 0))
    accblk = pl.BlockSpec((1, L + B, DH), lambda h, n: (h, 0, 0))
    vecblk = pl.BlockSpec((1, DH), lambda h, n: (0, 0))
    return _call(
        body, [q_t, kp, kp, vp, vp, qg, kg, sinks, bias, do_t], name="attn_bwd", grid=(KVH, NB),
        out_shape=(jax.ShapeDtypeStruct((AH, L, DH), BF16), jax.ShapeDtypeStruct((KVH, L + B, DH), F32),
                   jax.ShapeDtypeStruct((KVH, L + B, DH), F32), jax.ShapeDtypeStruct((1, DH), F32),
                   jax.ShapeDtypeStruct((1, DH), F32), jax.ShapeDtypeStruct((KVH, G, 1), F32),
                   jax.ShapeDtypeStruct((AH, B, 2 * B), F32)),
        in_specs=[qblk, kblk(0), kblk(1), kblk(0), kblk(1),
                  pl.BlockSpec((1, DH), lambda h, n: (0, 0)), pl.BlockSpec((1, DH), lambda h, n: (0, 0)),
                  pl.BlockSpec((1, G, 1), lambda h, n: (h, 0, 0)),
                  pl.BlockSpec((G, B, 2 * B), lambda h, n: (h, 0, 0)), qblk],
        out_specs=(qblk, accblk, accblk, vecblk, vecblk, pl.BlockSpec((1, G, 1), lambda h, n: (h, 0, 0)),
                   pl.BlockSpec((G, B, 2 * B), lambda h, n: (h, 0, 0))), comm=comm)


def _heads_first(t, nh):
    L = t.shape[0]
    return jnp.transpose(t.reshape(L, nh, t.shape[1] // nh), (1, 0, 2))


def _heads_last(t):
    nh, L, dh = t.shape
    return jnp.transpose(t, (1, 0, 2)).reshape(L, nh * dh)


def _softmax0(lg):
    e = jnp.exp(lg - jnp.max(lg, axis=0, keepdims=True))
    return e[0:1] / jnp.sum(e, axis=0, keepdims=True)


def _ada_update_call(fn, c_all, d_cols, w, m, v, rt):
    D, n = w.shape

    def body(c_ref, d_ref, w_ref, m_ref, v_ref, g_out, dl_out, m_out, v_out):
        outs, _ = fn(c_ref[...], d_ref[...], w_ref[...], m_ref[...], v_ref[...])
        for r, val in zip((g_out, dl_out, m_out, v_out), outs):
            r[...] = val

    wblk = pl.BlockSpec((rt, n), lambda i: (i, 0))
    return _call(
        body, [c_all, d_cols, w, m, v], name="update_ada", grid=(D // rt,), out_shape=tuple([jax.ShapeDtypeStruct((D, n), F32)] * 4),
        in_specs=[pl.BlockSpec((N_DEV, rt), lambda i: (0, i)), pl.BlockSpec((N_DEV, n), lambda i: (0, 0)), wblk, wblk, wblk],
        out_specs=(wblk, wblk, wblk, wblk))


def kernel(x, c, w_ada, b_ada, norm1_g, norm2_g, w_in, hg_lb_logits, hg_out_norm_g, q_norm_g, k_norm_g, attn_sinks, rel_bias_table, w_branch_hg, w_branch_attn, w_out, w_ff1, w_ff2, loss_target, m_w_ada, m_b_ada, m_norm1_g, m_norm2_g, m_w_in, m_hg_lb_logits, m_hg_out_norm_g, m_q_norm_g, m_k_norm_g, m_attn_sinks, m_rel_bias_table, m_w_branch_hg, m_w_branch_attn, m_w_out, m_w_ff1, m_w_ff2, v_w_ada, v_b_ada, v_norm1_g, v_norm2_g, v_w_in, v_hg_lb_logits, v_hg_out_norm_g, v_q_norm_g, v_k_norm_g, v_attn_sinks, v_rel_bias_table, v_w_branch_hg, v_w_branch_attn, v_w_out, v_w_ff1, v_w_ff2):
    cc = lax.axis_index("c")
    me = 4 * lax.axis_index("x") + 2 * lax.axis_index("y") + cc
    x2 = x[0]
    tgt = loss_target[0]
    L, D = x2.shape
    HGW = hg_lb_logits.shape[1]
    H = HGW // HG_DK
    AH = attn_sinks.shape[1]
    DH = q_norm_g.shape[1]
    ATW = AH * DH
    BW = w_in.shape[2]
    INW = BW * N_DEV
    A = BW // LANES
    assert BW == LANES * A + LANES // 2
    KVW = (INW - 4 * HGW - ATW - 2 * D) // 2
    KVH = KVW // DH
    G = AH // KVH
    ADA_N = w_ada.shape[2]
    PAIR = 2 * A + 1

    w_in_b = w_in[0].astype(BF16)
    src_in = jnp.where(cc == 0, jnp.pad(w_in_b, ((0, 0), (0, LANES // 2))), jnp.pad(w_in_b, ((0, 0), (LANES // 2, 0))))
    w_in_gapped, w_in_mid = _ag_w_in(src_in, A, D, INW)
    w_in_full = _patch_mid(w_in_gapped, w_in_mid, A)

    c_all = _gather_small(c, me, "gather_c")[:, 0, :]
    b_cols = lax.dynamic_slice(b_ada, (0, me * ADA_N), (1, ADA_N))
    (ada_cols,) = _whole(lambda cv, w, b: (_bdot(_silu(cv), w, NN) + b,), [c_all, w_ada[0], b_cols],
                         [((N_DEV, ADA_N), F32)], "ada_fwd")
    ada_all = _gather_small(ada_cols, me, "gather_ada")
    ada_row = lax.dynamic_slice(ada_all, (0, me, 0), (N_DEV, 1, ADA_N)).reshape(1, 6 * D)
    shift1, scale1, gate1, shift2, scale2, gate2 = [ada_row[:, i * D:(i + 1) * D] for i in range(6)]

    wnames = ("bhg", "bat", "out", "ff1", "ff2")
    waxis = dict(zip(wnames, (1, 1, 0, 1, 0)))
    wsrc = dict(zip(wnames, (w_branch_hg, w_branch_attn, w_out, w_ff1, w_ff2)))
    wblk = {k: wsrc[k][0].astype(BF16) for k in wnames}
    wf = {}

    (h,) = _rowwise(lambda xv, g, sh, sc: ((_modnorm(xv, g, sh, sc),), ()), [(x2, D, 0)], [norm1_g, shift1, scale1],
                    [(D, BF16)], [], "norm1")
    o4, oa = 4 * HGW, 4 * HGW + ATW + 2 * KVW
    r1, r2, ro = wblk["ff1"].shape[0], wblk["ff2"].shape[0], wblk["out"].shape[0]
    cm = _Comm()
    hs = {k: _ag_ici(cm, wblk[k], waxis[k]) for k in ("bhg", "bat")}
    hs["out"] = _ag_ici(cm, wblk["out"], waxis["out"], rows=(0, ro // 2))
    p4 = _mm(h, w_in_full, "nn", F32, "proj_hg", n=o4, comm=cm)
    half = {k: cm.result(hs[k]) for k in hs}
    cm = _Comm()
    hs = {"out": _ag_ici(cm, wblk["out"], waxis["out"], rows=(ro // 2, ro), into=half["out"])}
    pa = _mm(h, w_in_full, "nn", F32, "proj_at", b_off=o4, n=oa - o4, comm=cm)
    half["out"] = cm.result(hs["out"])
    cm = _Comm()
    hs = {k: _ag_d2d(cm, half[k], waxis[k]) for k in ("bhg", "bat")}
    hs["ff2"] = _ag_ici(cm, wblk["ff2"], waxis["ff2"], rows=(0, r2 // 4))
    pg = _mm(h, w_in_full, "nn", F32, "proj_gate", b_off=oa, n=INW - oa, comm=cm)
    wf["bhg"], wf["bat"], half["ff2"] = (cm.result(hs[k]) for k in ("bhg", "bat", "ff2"))

    cm = _Comm()
    hs = {"out": _ag_d2d(cm, half["out"], waxis["out"]), "ff1": _ag_ici(cm, wblk["ff1"], waxis["ff1"], rows=(0, r1 // 2))}
    o_hg, s_all = _hgrn_fwd(p4, hg_lb_logits, hg_out_norm_g, H, comm=cm)
    wf["out"], half["ff1"] = cm.result(hs["out"]), cm.result(hs["ff1"])

    bucket = _bucket_ids()
    (bias_flat,) = _whole(lambda tb, bk: (_dot(tb, _onehot(bk), TN, precision=HIGHEST),), [rel_bias_table, bucket],
                          [((AH, AT_BLOCK * 2 * AT_BLOCK), F32)], "bias_fwd")
    bias = bias_flat.reshape(AH, AT_BLOCK, 2 * AT_BLOCK)
    q_t = _heads_first(pa[:, :ATW], AH)
    pad = lambda t: jnp.pad(t, ((0, 0), (AT_BLOCK, 0), (0, 0)))
    kp = pad(_heads_first(pa[:, ATW:ATW + KVW], KVH))
    vp = pad(_heads_first(pa[:, ATW + KVW:], KVH))
    sinks3 = attn_sinks.reshape(KVH, G, 1)
    cm = _Comm()
    hs = {"ff1": _ag_ici(cm, wblk["ff1"], waxis["ff1"], rows=(r1 // 2, r1), into=half["ff1"])}
    o_at = _heads_last(_attn_fwd(q_t, kp, vp, q_norm_g, k_norm_g, sinks3, bias, KVH, comm=cm))
    half["ff1"] = cm.result(hs["ff1"])

    bh = _mm(o_hg, wf["bhg"], "nn", F32, "branch_hg")
    ba = _mm(o_at, wf["bat"], "nn", F32, "branch_at")

    def merge_fn(bhv, bav, ghg, gat):
        return jax.nn.sigmoid(ghg) * bhv + jax.nn.sigmoid(gat) * bav

    cm = _Comm()
    hs = {"ff1": _ag_d2d(cm, half["ff1"], waxis["ff1"])}
    (merged,) = _rowwise(lambda *a: ((merge_fn(*a),), ()), [(bh, D, 0), (ba, D, 0), (pg, D, 0), (pg, D, 1)], [],
                         [(D, BF16)], [], "merge", comm=cm)
    wf["ff1"] = cm.result(hs["ff1"])
    cm = _Comm()
    hs = {"ff2": _ag_ici(cm, wblk["ff2"], waxis["ff2"], rows=(r2 // 4, 3 * r2 // 8), into=half["ff2"])}
    mo = _mm(merged, wf["out"], "nn", F32, "out_proj", comm=cm)
    half["ff2"] = cm.result(hs["ff2"])

    def resid1(xv, mov, g1, g2n, sh, sc):
        x1v = xv + g1 * mov
        return (x1v, _modnorm(x1v, g2n, sh, sc)), ()

    cm = _Comm()
    hs = {"ff2": _ag_ici(cm, wblk["ff2"], waxis["ff2"], rows=(3 * r2 // 8, r2 // 2), into=half["ff2"])}
    x1, h2 = _rowwise(resid1, [(x2, D, 0), (mo, D, 0)], [gate1, norm2_g, shift2, scale2], [(D, F32), (D, BF16)], [], "resid1",
                      comm=cm)
    half["ff2"] = cm.result(hs["ff2"])
    cm = _Comm()
    hs = {"ff2": _ag_ici(cm, wblk["ff2"], waxis["ff2"], rows=(r2 // 2, r2), into=half["ff2"])}
    u, act = _mm(h2, wf["ff1"], "nn", (F32, BF16), "ff1", comm=cm, epi=lambda r: (r, jnp.square(jnp.maximum(r, 0.0))))
    half["ff2"] = cm.result(hs["ff2"])
    cm = _Comm()
    hs = {"ff2": _ag_d2d(cm, half["ff2"], waxis["ff2"])}
    _call(lambda: None, [], name="ag_d2d_ff2", out_shape=(), comm=cm)
    wf["ff2"] = cm.result(hs["ff2"])
    ff = _mm(act, wf["ff2"], "nn", F32, "ff2")

    def loss_fn(x1v, ffv, tv, g2):
        e = x1v + g2 * ffv - tv
        dy = e * (1.0 / D)
        return (dy, dy * g2), (jnp.sum(e * e, axis=0, keepdims=True), jnp.sum(dy * ffv, axis=0, keepdims=True))

    dy, d_ff, sq_sum, d_gate2 = _rowwise(loss_fn, [(x1, D, 0), (ff, D, 0), (tgt, D, 0)], [gate2],
                                         [(D, F32), (D, BF16)], [(1, D), (1, D)], "loss")
    loss = lax.psum(jnp.sum(sq_sum) * (0.5 / D), ("x", "y", "c"))

    owner_base = jnp.stack([me ^ r for r in CHIP_RELS]).astype(jnp.int32)
    gw, recv1, part, recv2 = {}, {}, {}, {}
    gw["ff2"] = _mm(act, d_ff, "tn", BF16, "dw_ff2")
    cm = _Comm()
    hh = _rs_d2d(cm, gw["ff2"], waxis["ff2"])
    d_u = _mm(d_ff, wf["ff2"], "nt", BF16, "d_act", comm=cm, extras=[u], epi=lambda r, uv: (r * (2.0 * jnp.maximum(uv, 0.0)),))
    part["ff2"] = _rs_add(gw["ff2"], cm.result(hh), waxis["ff2"], owner_base, "rs_add_ff2")
    rows_ff2 = part["ff2"].shape[1]
    cm = _Comm()
    hh = _rs_ici(cm, part["ff2"], rows=(0, rows_ff2 // 2))
    gw["ff1"] = _mm(h2, d_u, "tn", BF16, "dw_ff1", comm=cm)
    cm2 = _Comm()
    hh2 = _rs_ici(cm2, part["ff2"], rows=(rows_ff2 // 2, rows_ff2), recv=cm.result(hh))
    hh1 = _rs_d2d(cm2, gw["ff1"], waxis["ff1"])
    d_h2 = _mm(d_u, wf["ff1"], "nt", F32, "d_h2", comm=cm2)
    recv2["ff2"] = cm2.result(hh2)
    part["ff1"] = _rs_add(gw["ff1"], cm2.result(hh1), waxis["ff1"], owner_base, "rs_add_ff1")

    def norm2_bwd(dh2v, x1v, dyv, mov, g2n, sh, sc, g1):
        _, vjp = jax.vjp(_modnorm, x1v, g2n, sh, sc)
        dx, dg, dsh, dsc = vjp(dh2v)
        dx1 = dyv + dx
        return (dx1, dx1 * g1), (dg, dsh, dsc, jnp.sum(dx1 * mov, axis=0, keepdims=True))

    d_x1, d_mo, d_g2n, d_shift2, d_scale2, d_gate1 = _rowwise(
        norm2_bwd, [(d_h2, D, 0), (x1, D, 0), (dy, D, 0), (mo, D, 0)], [norm2_g, shift2, scale2, gate1],
        [(D, F32), (D, BF16)], [(1, D)] * 4, "norm2_bwd")
    gw["out"] = _mm(merged, d_mo, "tn", BF16, "dw_out")
    cm = _Comm()
    hh = _rs_d2d(cm, gw["out"], waxis["out"])
    d_merged = _mm(d_mo, wf["out"], "nt", F32, "d_merged", comm=cm)
    part["out"] = _rs_add(gw["out"], cm.result(hh), waxis["out"], owner_base, "rs_add_out")

    def merge_bwd(dmv, bhv, bav, ghg, gat):
        _, vjp = jax.vjp(merge_fn, bhv, bav, ghg, gat)
        return vjp(dmv), ()

    d_bh, d_ba, d_ghg, d_gat = _rowwise(merge_bwd, [(d_merged, D, 0), (bh, D, 0), (ba, D, 0), (pg, D, 0), (pg, D, 1)], [],
                                        [(D, BF16)] * 4, [], "merge_bwd")
    gw["bhg"] = _mm(o_hg, d_bh, "tn", BF16, "dw_bhg")
    gw["bat"] = _mm(o_at, d_ba, "tn", BF16, "dw_bat")
    cm = _Comm()
    hh = {k: _rs_d2d(cm, gw[k], waxis[k]) for k in ("bhg", "bat")}
    d_ohg = _mm(d_bh, wf["bhg"], "nt", F32, "d_ohg", comm=cm)
    for k in ("bhg", "bat"):
        part[k] = _rs_add(gw[k], cm.result(hh[k]), waxis[k], owner_base, "rs_add_" + k)
    d_oat = _mm(d_ba, wf["bat"], "nt", BF16, "d_oat")

    cm = _Comm()
    hh = {"ff1": _rs_ici(cm, part["ff1"])}
    d_hq, d_hf, d_hi, d_hg, d_lb, d_gout_h = _hgrn_bwd(p4, hg_lb_logits, hg_out_norm_g, s_all, d_ohg, H, comm=cm)
    recv2["ff1"] = cm.result(hh["ff1"])
    cm = _Comm()
    hh = {k: _rs_ici(cm, part[k]) for k in ("out", "bhg", "bat")}
    dq_t, dkp, dvp, d_qg, d_kg, d_sk, d_bias = _attn_bwd(q_t, kp, vp, q_norm_g, k_norm_g, sinks3, bias,
                                                         _heads_first(d_oat, AH), KVH, comm=cm)
    for k in hh:
        recv2[k] = cm.result(hh[k])
    d_aq = _heads_last(dq_t)
    d_ak = _heads_last(dkp[:, AT_BLOCK:, :]).astype(BF16)
    d_av = _heads_last(dvp[:, AT_BLOCK:, :]).astype(BF16)
    d_proj = jnp.concatenate([d_hq, d_hf, d_hi, d_hg, d_aq, d_ak, d_av, d_ghg, d_gat], axis=1)
    gw_in = _mm(h, d_proj, "tn", BF16, "dw_in")

    wm = LANES * A
    cm = _Comm()
    hi_ = cm.inp(gw_in)
    h_main, h_mid = cm.out((4, D, wm), BF16), cm.out((4, D, LANES), BF16)
    for i, r in enumerate(CHIP_RELS):
        def main_view(ref, p, r=r):
            o = p["me"] ^ r ^ 1
            return ref.at[:, pl.ds(pl.multiple_of((PAIR * (o // 2) + (A + 1) * (1 - p["c"])) * LANES, LANES), wm)]

        def mid_view(ref, p, r=r):
            o = p["me"] ^ r
            return ref.at[:, pl.ds(pl.multiple_of((PAIR * (o // 2) + A) * LANES, LANES), LANES)]

        cm.copy(hi_, main_view, h_main, _slot_view(i), 1)
        cm.copy(hi_, mid_view, h_mid, _slot_view(i), 1)
    _call(lambda: None, [], name="rs_d2d_in", out_shape=(), comm=cm)
    chip = jnp.stack([(me ^ r) // 2 for r in CHIP_RELS]).astype(jnp.int32)
    part_main = _rs_add(gw_in, cm.result(h_main), 1, PAIR * chip + (A + 1) * cc, "rs_add_in_main", tw=LANES)
    part_mid = _rs_add(gw_in, cm.result(h_mid), 1, PAIR * chip + A, "rs_add_in_mid", tw=LANES)
    xin_plans = [(s, _slot_view(i), s + 2, _slot_view(i - 1), CHIP_RELS[i]) for s in (0, 1) for i in (1, 2, 3)]
    xin_send, xin_recv, xin_arrays, xin_token = _split_start(
        [part_main, part_mid, lax.empty((3, D, wm), BF16), lax.empty((3, D, LANES), BF16)], xin_plans, "xin_start")
    d_h = _mm(d_proj, w_in_full, "nt", F32, "d_h", after=[xin_token])

    def norm1_bwd(dhv, xv, dx1v, g1n, sh, sc):
        _, vjp = jax.vjp(_modnorm, xv, g1n, sh, sc)
        dx, dg, dsh, dsc = vjp(dhv)
        return (dx1v + dx,), (dg, dsh, dsc)

    grad_x, d_g1n, d_shift1, d_scale1 = _rowwise(norm1_bwd, [(d_h, D, 0), (x2, D, 0), (d_x1, D, 0)],
                                                 [norm1_g, shift1, scale1], [(D, F32)], [(1, D)] * 3, "norm1_bwd")

    def sum4(p0, p1, p2, p3):
        return ((p0.astype(F32) + p1.astype(F32)) + p2.astype(F32)) + p3.astype(F32)

    def update_fn(w, m, v, p0, p1, p2, p3):
        g = sum4(p0, p1, p2, p3)
        delta, mn, vn = _adamw(w, g, m, v)
        return (g, delta, mn, vn), ()

    wmv = dict(zip(wnames, ((w_branch_hg, m_w_branch_hg, v_w_branch_hg), (w_branch_attn, m_w_branch_attn, v_w_branch_attn),
                            (w_out, m_w_out, v_w_out), (w_ff1, m_w_ff1, v_w_ff1), (w_ff2, m_w_ff2, v_w_ff2))))
    res = {}
    for k in wnames:
        w, m, v = (t[0] for t in wmv[k])
        n = w.shape[1]
        ins = [(t, n, 0) for t in (w, m, v)] + [(part[k], n, 0, 0)] + [(recv2[k], n, 0, i) for i in range(3)]
        res[k] = [t[None] for t in _rowwise(update_fn, ins, [], [(n, F32)] * 4, [], "update_" + k)]

    d_ada_row = jnp.concatenate([d_shift1, d_scale1, d_gate1, d_shift2, d_scale2, d_gate2], axis=1)
    d_ada_all = _gather_small(d_ada_row, me, "gather_dada")[:, 0, :]
    d_ada_cols = lax.dynamic_slice(d_ada_all, (0, me * ADA_N), (N_DEV, ADA_N))

    def ada_update(cv, dav, w, m, v):
        g = _bdot(_silu(cv), dav, TN)
        delta, mn, vn = _adamw(w, g, m, v)
        return (g, delta, mn, vn), ()

    res["ada"] = [t[None] for t in _ada_update_call(ada_update, c_all, d_ada_cols, w_ada[0], m_w_ada[0], v_w_ada[0], _tile(D, 256, 16))]

    d_sinks = d_sk.reshape(1, AH)
    (d_table_t,) = _whole(lambda db, bk: (_dot(db, _onehot(bk), NT, precision=HIGHEST),),
                          [d_bias.reshape(AH, AT_BLOCK * 2 * AT_BLOCK), bucket], [((AH, N_BUCKETS), F32)], "bias_bwd")
    smalls = [d_g1n, d_g2n, d_lb, d_gout_h, d_qg, d_kg, d_sinks, d_table_t.T.reshape(1, N_BUCKETS * AH)]
    widths = [s.shape[1] for s in smalls]
    lanes = [-(-w // LANES) * LANES for w in widths]
    smalls = [jnp.pad(s, ((0, 0), (0, p - w))) for s, w, p in zip(smalls, widths, lanes)]
    packed = _gather_small(jnp.concatenate(smalls, axis=1), me, "gather_small")[:, 0, :]
    offs = [sum(lanes[:i]) for i in range(len(lanes))]

    def small_update(pk, dada, lg, *wmv_flat):
        tot = pk[0:1]
        for d in range(1, N_DEV):
            tot = tot + pk[d:d + 1]
        gb = dada[0:1]
        for d in range(1, N_DEV):
            gb = gb + dada[d:d + 1]
        gs = [tot[:, offs[i]:offs[i] + widths[i]] for i in range(len(widths))]
        _, lb_vjp = jax.vjp(_softmax0, lg)
        (g_lg,) = lb_vjp(gs[2])
        grads = [gb, gs[0], gs[1], g_lg, gs[3], gs[4], gs[5], gs[6], gs[7]]
        outs = []
        for i, g in enumerate(grads):
            w, m, v = wmv_flat[3 * i:3 * i + 3]
            delta, mn, vn = _adamw(w, g, m, v)
            outs += [g, delta, mn, vn]
        return tuple(outs)

    tbl = lambda t: t.reshape(1, N_BUCKETS * AH)
    small_wmv = [(b_ada, m_b_ada, v_b_ada), (norm1_g, m_norm1_g, v_norm1_g), (norm2_g, m_norm2_g, v_norm2_g),
                 (hg_lb_logits, m_hg_lb_logits, v_hg_lb_logits), (hg_out_norm_g, m_hg_out_norm_g, v_hg_out_norm_g),
                 (q_norm_g, m_q_norm_g, v_q_norm_g), (k_norm_g, m_k_norm_g, v_k_norm_g),
                 (attn_sinks, m_attn_sinks, v_attn_sinks),
                 (tbl(rel_bias_table), tbl(m_rel_bias_table), tbl(v_rel_bias_table))]
    flat = [t for trip in small_wmv for t in trip]
    out_shapes = [(trip[0].shape, F32) for trip in small_wmv for _ in range(4)]
    sres = _whole(small_update, [packed, d_ada_all, hg_lb_logits] + flat, out_shapes, "small_update")
    names_small = ("b_ada", "norm1_g", "norm2_g", "lb", "gout", "qg", "kg", "sinks", "table")
    for i, k in enumerate(names_small):
        r = sres[4 * i:4 * i + 4]
        if k == "table":
            r = [t.reshape(N_BUCKETS, AH) for t in r]
        res[k] = r

    behind = [grad_x, sres[0], res["ada"][0]] + [res[k][0] for k in wnames]
    part_main, part_mid, rx_main, rx_mid = _split_wait(xin_send, xin_recv, xin_arrays, xin_plans, behind, "xin_wait")
    g_main, = _rowwise(lambda *p: ((sum4(*p),), ()), [(part_main, wm, 0, 0)] + [(rx_main, wm, 0, i) for i in range(3)], [],
                       [(wm, F32)], [], "sum_in_main")
    g_mid, = _rowwise(lambda *p: ((sum4(*p),), ()), [(part_mid, LANES, 0, 0)] + [(rx_mid, LANES, 0, i) for i in range(3)], [],
                      [(LANES, F32)], [], "sum_in_mid")
    g_in = jnp.where(cc == 0, jnp.concatenate([g_main, g_mid[:, :LANES // 2]], axis=1),
                     jnp.concatenate([g_mid[:, LANES // 2:], g_main], axis=1))

    def update_given(w, m, v, g):
        delta, mn, vn = _adamw(w, g, m, v)
        return (g, delta, mn, vn), ()

    res["in"] = [t[None] for t in _rowwise(update_given, [(t, BW, 0) for t in (w_in[0], m_w_in[0], v_w_in[0], g_in)], [],
                                           [(BW, F32)] * 4, [], "update_in")]

    order = ("ada", "b_ada", "norm1_g", "norm2_g", "in", "lb", "gout", "qg", "kg", "sinks", "table", "bhg", "bat", "out", "ff1", "ff2")
    outs = [loss, grad_x[None]]
    for j in range(4):
        outs += [res[k][j] for k in order]
    return tuple(outs)
```

```python
import functools
import math

import jax
import jax.numpy as jnp
from jax import lax
from jax.experimental import pallas as pl
from jax.experimental.pallas import tpu as pltpu

F32 = jnp.float32
BF16 = jnp.bfloat16
EPS = 1e-6
NEG_INF = -1e30
HG_DK = 128
HG_CHUNK = 64
AT_BLOCK = 128
N_BUCKETS = 32
MAX_EXACT = 16
MAX_DISTANCE = 128
N_DEV = 8
LANES = 128
VMEM_LIMIT = 56 * 1024 * 1024
ADAM_LR, ADAM_B1, ADAM_B2, ADAM_EPS, ADAM_WD, ADAM_STEP = 0.001, 0.9, 0.999, 1e-08, 0.01, 10
HIGHEST = lax.Precision.HIGHEST
MESH = pl.DeviceIdType.MESH
ANY = pl.BlockSpec(memory_space=pl.ANY)
CHIP_RELS = (0, 4, 2, 6)

NN = (((1,), (0,)), ((), ()))
NT = (((1,), (1,)), ((), ()))
TN = (((0,), (0,)), ((), ()))


def _tile(n, pref, unit):
    if n <= pref:
        return n
    t = (pref // unit) * unit
    while t >= unit:
        if n % t == 0:
            return t
        t -= unit
    return n


def _dot(a, b, dn, precision=None):
    return lax.dot_general(a, b, dn, preferred_element_type=F32, precision=precision)


def _bdot(a, b, dn):
    return _dot(a.astype(BF16), b.astype(BF16), dn)


def _position():
    x, y, c = lax.axis_index("x"), lax.axis_index("y"), lax.axis_index("c")
    return dict(x=x, y=y, c=c, me=4 * x + 2 * y + c)


def _peer_position(p, rel):
    x = 1 - p["x"] if rel & 4 else p["x"]
    y = 1 - p["y"] if rel & 2 else p["y"]
    c = 1 - p["c"] if rel & 1 else p["c"]
    return dict(x=x, y=y, c=c, me=4 * x + 2 * y + c)


class _Comm:
    def __init__(self):
        self.ins, self.outs, self.alias, self.plans, self.res = [], [], {}, [], None

    def inp(self, arr):
        self.ins.append(arr)
        return ("i", len(self.ins) - 1)

    def out(self, shape, dtype, alias=None):
        self.outs.append(jax.ShapeDtypeStruct(tuple(shape), dtype))
        if alias is not None:
            self.alias[alias[1]] = len(self.outs) - 1
        return ("o", len(self.outs) - 1)

    def copy(self, src, src_view, dst, dst_view, rel):
        self.plans.append((src, src_view, dst, dst_view, rel))

    def result(self, handle):
        return self.res[handle[1]]

    def build(self, in_refs, out_refs, send_sems, recv_sems):
        pos = _position()
        ref = lambda h: in_refs[h[1]] if h[0] == "i" else out_refs[h[1]]
        ops = []
        for k, (src, sv, dst, dv, rel) in enumerate(self.plans):
            s = sv(ref(src), pos)
            if rel == 0:
                cp = pltpu.make_async_copy(s, dv(ref(dst), pos), send_sems.at[k])
                ops.append((cp.start, cp.wait))
                continue
            peer = _peer_position(pos, rel)
            mk = lambda d: pltpu.make_async_remote_copy(
                src_ref=s, dst_ref=d, send_sem=send_sems.at[k], recv_sem=recv_sems.at[k],
                device_id=(peer["x"], peer["y"], peer["c"]), device_id_type=MESH)
            out_cp, in_cp = mk(dv(ref(dst), pos)), mk(dv(ref(dst), peer))

            def wait(out_cp=out_cp, in_cp=in_cp):
                out_cp.wait_send()
                in_cp.wait_recv()

            ops.append((out_cp.start, wait))
        return ops


def _call(body, args, *, name, out_shape, in_specs=None, out_specs=None, grid=None, scratch_shapes=(), comm=None,
          prefetch=None, aliases=None, after=()):
    single = not isinstance(out_shape, (tuple, list))
    out_shape = (out_shape,) if single else tuple(out_shape)
    n_in, n_out, n_scr = len(args), len(out_shape), len(scratch_shapes)
    vm = pl.BlockSpec(memory_space=pltpu.VMEM)
    in_specs = [vm] * n_in if in_specs is None else list(in_specs)
    out_specs = [vm] * n_out if out_specs is None else (list(out_specs) if isinstance(out_specs, (tuple, list)) else [out_specs])
    n_pf = 0 if prefetch is None else len(prefetch)
    kw = {} if aliases is None else {"input_output_aliases": dict(aliases)}
    if comm is None and after:
        n_dep = len(after)

        def fn(*refs):
            body(*refs[:n_pf + n_in], *refs[n_pf + n_in + n_dep:])

        all_args, all_scratch = list(args) + list(after), list(scratch_shapes)
        in_specs = in_specs + [ANY] * n_dep
    elif comm is None:
        fn = body
        all_args, all_scratch = list(args), list(scratch_shapes)
    else:
        n_ci, n_co, n_x = len(comm.ins), len(comm.outs), len(comm.plans)

        def fn(*refs):
            pf, refs = refs[:n_pf], refs[n_pf:]
            o_in, c_in = refs[:n_in], refs[n_in:n_in + n_ci]
            o_out = refs[n_in + n_ci:n_in + n_ci + n_out]
            c_out = refs[n_in + n_ci + n_out:n_in + n_ci + n_out + n_co]
            scr = refs[n_in + n_ci + n_out + n_co:]
            ops = comm.build(c_in, c_out, scr[n_scr], scr[n_scr + 1])
            if grid:
                first = functools.reduce(jnp.logical_and, [pl.program_id(i) == 0 for i in range(len(grid))])
                last = functools.reduce(jnp.logical_and, [pl.program_id(i) == g - 1 for i, g in enumerate(grid)])

                @pl.when(first)
                def _():
                    for start, _w in ops:
                        start()
            else:
                for start, _w in ops:
                    start()
            body(*pf, *o_in, *o_out, *scr[:n_scr])
            if grid:
                @pl.when(last)
                def _():
                    for _s, wait in ops:
                        wait()
            else:
                for _s, wait in ops:
                    wait()

        all_args = list(args) + list(comm.ins)
        in_specs = in_specs + [ANY] * n_ci
        out_shape = out_shape + tuple(comm.outs)
        out_specs = out_specs + [ANY] * n_co
        all_scratch = list(scratch_shapes) + [pltpu.SemaphoreType.DMA((n_x,)), pltpu.SemaphoreType.DMA((n_x,))]
        kw["input_output_aliases"] = {n_pf + n_in + i: n_out + o for i, o in comm.alias.items()}
    sem = None if grid is None else ("arbitrary",) * len(grid)
    params = pltpu.CompilerParams(dimension_semantics=sem, vmem_limit_bytes=VMEM_LIMIT)
    if prefetch is None:
        spec = dict(in_specs=in_specs, out_specs=tuple(out_specs), scratch_shapes=all_scratch)
        if grid is not None:
            spec["grid"] = grid
    else:
        spec = dict(grid_spec=pltpu.PrefetchScalarGridSpec(
            num_scalar_prefetch=n_pf, grid=grid, in_specs=in_specs, out_specs=tuple(out_specs), scratch_shapes=all_scratch))
        all_args = list(prefetch) + all_args
    res = pl.pallas_call(fn, name=name, out_shape=out_shape, compiler_params=params, **spec, **kw)(*all_args)
    res = list(res)
    if comm is not None:
        comm.res = res[n_out:]
        res = res[:n_out]
    return res[0] if single else res


def _whole_view(ref, pos):
    return ref


def _block_view(axis, n, index, rows=None):
    def view(ref, pos):
        off = pl.multiple_of(index(pos) * n, n)
        if rows is None:
            return ref.at[:, pl.ds(off, n)] if axis == 1 else ref.at[pl.ds(off, n), :]
        lo, cnt = rows[0], rows[1] - rows[0]
        if axis == 1:
            return ref.at[pl.ds(lo, cnt), pl.ds(off, n)]
        return ref.at[pl.ds(pl.multiple_of(off + lo, 16), cnt), :]
    return view


def _rows_view(rows):
    def view(ref, pos):
        return ref if rows is None else ref.at[pl.ds(rows[0], rows[1] - rows[0]), :]
    return view


def _slot_view(i, rows=None):
    def view(ref, pos):
        return ref.at[i] if rows is None else ref.at[i, pl.ds(rows[0], rows[1] - rows[0]), :]
    return view


def _exchange(items, name):
    cm = _Comm()
    for a, rel in items:
        cm.copy(cm.inp(a), _whole_view, cm.out(a.shape, a.dtype), _whole_view, rel)
    _call(lambda: None, [], name=name, out_shape=(), comm=cm)
    return cm.res


def _gather_small(v, me, name):
    cm = _Comm()
    hi, ho = cm.inp(v), cm.out((N_DEV,) + v.shape, v.dtype)
    for rel in range(N_DEV):
        cm.copy(hi, _whole_view, ho, lambda ref, p: ref.at[p["me"]], rel)
    _call(lambda: None, [], name=name, out_shape=(), comm=cm)
    return cm.result(ho)


def _ag_ici(cm, blk, axis, rows=None, into=None):
    n = blk.shape[axis]
    shape = list(blk.shape)
    shape[axis] = n * N_DEV
    hi = cm.inp(blk)
    ho = cm.out(shape, blk.dtype) if into is None else cm.out(shape, blk.dtype, alias=cm.inp(into))
    own = _block_view(axis, n, lambda p: p["me"], rows)
    for rel in CHIP_RELS:
        cm.copy(hi, _rows_view(rows), ho, own, rel)
    return ho


def _ag_d2d(cm, full, axis):
    n = full.shape[axis] // N_DEV
    hi = cm.inp(full)
    ho = cm.out(full.shape, full.dtype, alias=hi)
    for r in CHIP_RELS:
        v = _block_view(axis, n, functools.partial(lambda p, r: p["me"] ^ r, r=r))
        cm.copy(hi, v, ho, v, 1)
    return ho


def _rs_d2d(cm, gw, axis):
    n = gw.shape[axis] // N_DEV
    shape = list(gw.shape)
    shape[axis] = n
    hi, ho = cm.inp(gw), cm.out([4] + shape, gw.dtype)
    for i, r in enumerate(CHIP_RELS):
        cm.copy(hi, _block_view(axis, n, functools.partial(lambda p, r: p["me"] ^ r ^ 1, r=r)), ho, _slot_view(i), 1)
    return ho


def _rs_ici(cm, part, rows=None, recv=None):
    if recv is None:
        ho = cm.out((3,) + part.shape[1:], part.dtype)
    else:
        ho = cm.out(recv.shape, recv.dtype, alias=cm.inp(recv))
    hi = cm.inp(part)
    for i in (1, 2, 3):
        cm.copy(hi, _slot_view(i, rows), ho, _slot_view(i - 1, rows), CHIP_RELS[i])
    return ho


def _rs_add(gw, recv, axis, base, name, tw=None):
    _, R, n = recv.shape
    if axis == 1:
        tw = n if tw is None else tw
        gw_spec = pl.BlockSpec((R, tw), lambda i, t, b: (0, b[i] + t))
        rv_spec = pl.BlockSpec((None, R, tw), lambda i, t, b: (i, 0, t))
        grid = (4, n // tw)
    else:
        tw = _tile(n, 1024, LANES)
        gw_spec = pl.BlockSpec((R, tw), lambda i, t, b: (b[i], t))
        rv_spec = pl.BlockSpec((None, R, tw), lambda i, t, b: (i, 0, t))
        grid = (4, n // tw)

    def body(b_ref, g_ref, r_ref, o_ref):
        o_ref[...] = (g_ref[...].astype(F32) + r_ref[...].astype(F32)).astype(o_ref.dtype)

    return _call(body, [gw, recv], name=name, out_shape=jax.ShapeDtypeStruct(recv.shape, recv.dtype), grid=grid,
                 in_specs=[gw_spec, rv_spec], out_specs=rv_spec, prefetch=[base])


HBM_SPEC = pl.BlockSpec(memory_space=pltpu.HBM)
SEM_SPEC = pl.BlockSpec(memory_space=pltpu.SEMAPHORE)
SPLIT_PARAMS = pltpu.CompilerParams(has_side_effects=pltpu.SideEffectType.DATAFLOW_SIDE_EFFECTING)


def _split_copies(refs, plans, send_sems, recv_sems):
    pos = _position()
    out = []
    for k, (si, sv, li, lv, rel) in enumerate(plans):
        peer = _peer_position(pos, rel)
        mk = lambda d: pltpu.make_async_remote_copy(
            src_ref=sv(refs[si], pos), dst_ref=d, send_sem=send_sems.at[k], recv_sem=recv_sems.at[k],
            device_id=(peer["x"], peer["y"], peer["c"]), device_id_type=MESH)
        out.append((mk(lv(refs[li], pos)), mk(lv(refs[li], peer))))
    return out


def _split_start(arrays, plans, name):
    n = len(arrays)

    def body(*refs):
        send_sems, recv_sems = refs[n], refs[n + 1]
        for out_cp, _ in _split_copies(refs[:n], plans, send_sems, recv_sems):
            out_cp.start()
        refs[-1][...] = jnp.zeros_like(refs[-1])

    sems = pltpu.SemaphoreType.DMA((len(plans),))
    res = pl.pallas_call(
        body, name=name,
        out_shape=(sems, sems) + tuple(pltpu.HBM(a.shape, a.dtype) for a in arrays) + (jax.ShapeDtypeStruct((8, LANES), F32),),
        in_specs=[HBM_SPEC] * n, out_specs=(SEM_SPEC, SEM_SPEC) + (HBM_SPEC,) * n + (pl.BlockSpec(memory_space=pltpu.VMEM),),
        input_output_aliases={i: 2 + i for i in range(n)}, compiler_params=SPLIT_PARAMS,
    )(*[pltpu.with_memory_space_constraint(a, pltpu.HBM) for a in arrays])
    return res[0], res[1], list(res[2:2 + n]), res[-1]


def _split_wait(send_sems, recv_sems, arrays, plans, after, name):
    n, na = len(arrays), len(after)

    def body(*refs):
        for out_cp, in_cp in _split_copies(refs[:n], plans, refs[n], refs[n + 1]):
            out_cp.wait_send()
            in_cp.wait_recv()

    res = pl.pallas_call(
        body, name=name, out_shape=tuple(pltpu.HBM(a.shape, a.dtype) for a in arrays),
        in_specs=[HBM_SPEC] * n + [SEM_SPEC, SEM_SPEC] + [ANY] * na, out_specs=(HBM_SPEC,) * n,
        input_output_aliases={i: i for i in range(n)}, compiler_params=SPLIT_PARAMS,
    )(*arrays, send_sems, recv_sems, *after)
    return list(res)


def _ag_split_start(blks, axes, name):
    nw = len(blks)
    fulls = []
    for b, ax in zip(blks, axes):
        shape = list(b.shape)
        shape[ax] *= N_DEV
        fulls.append(lax.empty(tuple(shape), b.dtype))
    plans = [(i, _whole_view, nw + i, _block_view(axes[i], blks[i].shape[axes[i]], lambda p: p["me"]), rel)
             for i in range(nw) for rel in CHIP_RELS[1:]]
    send_sems, recv_sems, arrays, token = _split_start(list(blks) + fulls, plans, name)
    return dict(sems=(send_sems, recv_sems), arrays=arrays, plans=plans, token=token, nw=nw)


def _ag_split_wait(h, after, name):
    arrays = _split_wait(h["sems"][0], h["sems"][1], h["arrays"], h["plans"], after, name)
    return arrays[:h["nw"]], arrays[h["nw"]:]


def _ag_d2d_own(cm, full, blk, axis):
    n = blk.shape[axis]
    hb, hi = cm.inp(blk), cm.inp(full)
    ho = cm.out(full.shape, full.dtype, alias=hi)
    own = _block_view(axis, n, lambda p: p["me"])
    cm.copy(hb, _whole_view, ho, own, 0)
    cm.copy(hb, _whole_view, ho, own, 1)
    for r in CHIP_RELS[1:]:
        v = _block_view(axis, n, functools.partial(lambda p, r: p["me"] ^ r, r=r))
        cm.copy(hi, v, ho, v, 1)
    return ho


def _rs_split_start(parts, name):
    nw = len(parts)
    lands = [lax.empty((3,) + p.shape[1:], p.dtype) for p in parts]
    plans = [(s, _slot_view(i), nw + s, _slot_view(i - 1), CHIP_RELS[i]) for s in range(nw) for i in (1, 2, 3)]
    send_sems, recv_sems, arrays, token = _split_start(list(parts) + lands, plans, name)
    return dict(sems=(send_sems, recv_sems), arrays=arrays, plans=plans, token=token, nw=nw)


def _rs_split_wait(h, after, name):
    arrays = _split_wait(h["sems"][0], h["sems"][1], h["arrays"], h["plans"], after, name)
    return arrays[:h["nw"]], arrays[h["nw"]:]


def _behind(xs, tokens):
    out = lax.optimization_barrier((tuple(xs), tuple(tokens)))
    return list(out[0])


def _ag_w_in(src, a, D, INW):
    wm = LANES * a

    hd = D // 2
    ALL, TOP, BOT = (0, D), (0, hd), (hd, D)

    def main_place(ref, p, rows=ALL):
        off = pl.multiple_of(((2 * a + 1) * (p["me"] // 2) + (a + 1) * p["c"]) * LANES, LANES)
        return ref.at[pl.ds(rows[0], rows[1] - rows[0]), pl.ds(off, wm)]

    def main_src(ref, p):
        return ref.at[:, pl.ds(pl.multiple_of(p["c"] * LANES, LANES), wm)]

    def mid_src(ref, p):
        return ref.at[:, pl.ds(pl.multiple_of((1 - p["c"]) * wm, LANES), LANES)]

    def mid_place(ref, p, rows=ALL):
        return ref.at[p["me"], pl.ds(rows[0], rows[1] - rows[0]), :]

    def body(src_ref, full_ref, mid_ref, send_sems, recv_sems):
        pos = _position()
        sib, xn, yn = (_peer_position(pos, r) for r in (1, 4, 2))
        dg = _peer_position(pos, 6)
        started = []

        def remote(k, s, d, to):
            return pltpu.make_async_remote_copy(src_ref=s, dst_ref=d, send_sem=send_sems.at[k], recv_sem=recv_sems.at[k],
                                                device_id=(to["x"], to["y"], to["c"]), device_id_type=MESH)

        def send(k, owner, rows, to, from_src=False):
            for j, (src_v, place) in enumerate(((main_src, main_place), (mid_src, mid_place))):
                s = src_v(src_ref, pos) if from_src else place(full_ref if j == 0 else mid_ref, owner, rows)
                cp = remote(k + j, s, place(full_ref if j == 0 else mid_ref, owner, rows), to)
                cp.start()
                started.append(cp)

        def landed(k, owner, rows, frm):
            for j, place in enumerate((main_place, mid_place)):
                ref = full_ref if j == 0 else mid_ref
                remote(k + j, place(ref, owner, rows), place(ref, owner, rows), frm).wait_recv()

        local = [pltpu.make_async_copy(main_src(src_ref, pos), main_place(full_ref, pos), send_sems.at[18]),
                 pltpu.make_async_copy(mid_src(src_ref, pos), mid_place(mid_ref, pos), send_sems.at[19])]
        for cp in local:
            cp.start()
        send(0, pos, ALL, sib, from_src=True)
        send(2, pos, ALL, xn, from_src=True)
        send(4, pos, ALL, yn, from_src=True)
        landed(2, xn, ALL, xn)
        send(10, xn, ALL, sib)
        send(6, xn, TOP, yn)
        landed(4, yn, ALL, yn)
        send(12, yn, ALL, sib)
        send(8, yn, BOT, xn)
        landed(6, dg, TOP, yn)
        send(14, dg, TOP, sib)
        landed(8, dg, BOT, xn)
        send(16, dg, BOT, sib)
        sib_of = lambda p: _peer_position(p, 1)
        landed(0, sib, ALL, sib)
        landed(10, sib_of(xn), ALL, sib)
        landed(12, sib_of(yn), ALL, sib)
        landed(14, sib_of(dg), TOP, sib)
        landed(16, sib_of(dg), BOT, sib)
        for cp in started:
            cp.wait_send()
        for cp in local:
            cp.wait()

    return _call(body, [src], name="ag_w_in", in_specs=[ANY], out_specs=[ANY, ANY],
                 out_shape=(jax.ShapeDtypeStruct((D, INW), BF16), jax.ShapeDtypeStruct((N_DEV, D, LANES), BF16)),
                 scratch_shapes=[pltpu.SemaphoreType.DMA((20,)), pltpu.SemaphoreType.DMA((20,))])


def _patch_mid(full, mid, a):
    D = full.shape[0]

    def body(full_ref, e_ref, o_ref, out_ref):
        out_ref[...] = e_ref[...] + o_ref[...]

    return _call(body, [full, mid, mid], name="patch_mid", grid=(N_DEV // 2,),
                 out_shape=jax.ShapeDtypeStruct(full.shape, full.dtype),
                 in_specs=[ANY, pl.BlockSpec((None, D, LANES), lambda j: (2 * j, 0, 0)),
                           pl.BlockSpec((None, D, LANES), lambda j: (2 * j + 1, 0, 0))],
                 out_specs=pl.BlockSpec((D, LANES), lambda j: (0, (2 * a + 1) * j + a)), aliases={0: 0})


MM_RESIDENT = 2048


def _mm(a, b, mode, out_dtype, name, b_off=0, n=None, comm=None, extras=(), epi=None, tn=None, after=()):
    if mode == "nn":
        (M, K), (K2, N) = a.shape, b.shape
    elif mode == "nt":
        (M, K), (N, K2) = a.shape, b.shape
    else:
        (K, M), (K2, N) = a.shape, b.shape
    assert K == K2, (a.shape, b.shape, mode)
    if n is not None:
        N = n
    single = not isinstance(out_dtype, (tuple, list))
    out_dtypes = (out_dtype,) if single else tuple(out_dtype)
    if epi is None:
        epi = lambda r: (r,)
    tk = K if K <= MM_RESIDENT else (MM_RESIDENT if K % MM_RESIDENT == 0 else _tile(K, 512, LANES))
    nk = K // tk
    if M > MM_RESIDENT and mode == "tn" and N <= MM_RESIDENT and not b_off:
        tm, tn = _tile(M, 512, LANES), N
    elif nk > 1:
        tm, tn = _tile(M, 1024, LANES), _tile(N, tn or 1024, LANES)
    else:
        tm = _tile(M, MM_RESIDENT, LANES)
        tn = _tile(math.gcd(N, b_off) if b_off else N, tn or 512, LANES)
    jb = b_off // tn
    dn = {"nn": NN, "nt": NT, "tn": TN}[mode]
    ne, no = len(extras), len(out_dtypes)

    def body(a_ref, b_ref, *rest):
        e_refs, o_refs = rest[:ne], rest[ne:ne + no]

        def finish(r):
            for o_ref, v in zip(o_refs, epi(r, *[e[...] for e in e_refs])):
                o_ref[...] = v.astype(o_ref.dtype)

        if nk == 1:
            finish(_bdot(a_ref[...], b_ref[...], dn))
            return
        acc_ref = rest[ne + no]
        k = pl.program_id(2)

        @pl.when(k == 0)
        def _():
            acc_ref[...] = _bdot(a_ref[...], b_ref[...], dn)

        @pl.when(jnp.logical_and(k > 0, k < nk - 1))
        def _():
            acc_ref[...] += _bdot(a_ref[...], b_ref[...], dn)

        @pl.when(k == nk - 1)
        def _():
            finish(acc_ref[...] + _bdot(a_ref[...], b_ref[...], dn))

    a_spec = pl.BlockSpec((tk, tm), lambda i, j, k: (k, i)) if mode == "tn" else pl.BlockSpec((tm, tk), lambda i, j, k: (i, k))
    b_spec = pl.BlockSpec((tn, tk), lambda i, j, k: (j, k)) if mode == "nt" else pl.BlockSpec((tk, tn), lambda i, j, k: (k, j + jb))
    o_spec = pl.BlockSpec((tm, tn), lambda i, j, k: (i, j))
    res = _call(body, [a, b] + list(extras), name=name, grid=(M // tm, N // tn, nk),
                out_shape=tuple(jax.ShapeDtypeStruct((M, N), dt) for dt in out_dtypes),
                in_specs=[a_spec, b_spec] + [o_spec] * ne, out_specs=[o_spec] * no,
                scratch_shapes=[pltpu.VMEM((tm, tn), F32)] if nk > 1 else [], comm=comm, after=after)
    return res[0] if single else res


def _rowwise(fn, row_ins, bcast_ins, row_outs, acc_outs, name, rt=256, comm=None):
    L = row_ins[0][0].shape[-2]
    rt = _tile(L, rt, 16)
    nr, nb, no = len(row_ins), len(bcast_ins), len(row_outs)

    def body(*refs):
        i = pl.program_id(0)
        vals = [r[...] for r in refs[:nr + nb]]
        outs, accs = fn(*vals)
        for r, v in zip(refs[nr + nb:nr + nb + no], outs):
            r[...] = v.astype(r.dtype)
        acc_refs = refs[nr + nb + no:]

        @pl.when(i == 0)
        def _():
            for r in acc_refs:
                r[...] = jnp.zeros_like(r)

        for r, v in zip(acc_refs, accs):
            r[...] += v

    in_specs = []
    for spec in row_ins:
        w, cb = spec[1], spec[2]
        if len(spec) == 4:
            in_specs.append(pl.BlockSpec((None, rt, w), functools.partial(lambda i, cb, ld: (ld, i, cb), cb=cb, ld=spec[3])))
        else:
            in_specs.append(pl.BlockSpec((rt, w), functools.partial(lambda i, cb: (i, cb), cb=cb)))
    in_specs += [pl.BlockSpec(b.shape, lambda i: (0, 0)) for b in bcast_ins]
    out_specs = [pl.BlockSpec((rt, w), lambda i: (i, 0)) for w, _ in row_outs]
    out_specs += [pl.BlockSpec(s, lambda i: (0, 0)) for s in acc_outs]
    out_shape = [jax.ShapeDtypeStruct((L, w), dt) for w, dt in row_outs] + [jax.ShapeDtypeStruct(s, F32) for s in acc_outs]
    return _call(body, [s[0] for s in row_ins] + list(bcast_ins), name=name, grid=(L // rt,), out_shape=tuple(out_shape),
                 in_specs=in_specs, out_specs=out_specs, comm=comm)


def _whole(fn, ins, out_shapes, name):
    def body(*refs):
        outs = fn(*[r[...] for r in refs[:len(ins)]])
        for r, v in zip(refs[len(ins):], outs):
            r[...] = v.astype(r.dtype)

    return _call(body, list(ins), name=name, out_shape=tuple(jax.ShapeDtypeStruct(s, dt) for s, dt in out_shapes))


def _silu(x):
    return x * jax.nn.sigmoid(x)


def _rms(x, g):
    return (x * lax.rsqrt(jnp.mean(x * x, axis=-1, keepdims=True) + EPS)) * g


def _modnorm(x, g, shift, scale):
    return _rms(x, g) * (1.0 + scale) + shift


def _adamw(w, g, m, v):
    m = ADAM_B1 * m + (1.0 - ADAM_B1) * g
    v = ADAM_B2 * v + (1.0 - ADAM_B2) * jnp.square(g)
    m_hat = m / (1.0 - ADAM_B1 ** ADAM_STEP)
    v_hat = v / (1.0 - ADAM_B2 ** ADAM_STEP)
    delta = -ADAM_LR * (m_hat / (jnp.sqrt(v_hat) + ADAM_EPS) + ADAM_WD * w)
    return delta, m, v


def _lower_bound(lg):
    e = jnp.exp(lg - jnp.max(lg, axis=0, keepdims=True))
    return e[0:1] / jnp.sum(e, axis=0, keepdims=True)


def _hg_chunk(hq, hf, hi, lb, st):
    C = hq.shape[0]
    row = lax.broadcasted_iota(jnp.int32, (C, C), 0)
    col = lax.broadcasted_iota(jnp.int32, (C, C), 1)
    tri = row >= col
    sg = jax.nn.sigmoid(hf)
    f = lb + (1.0 - lb) * sg
    lf = jnp.log(f)
    k = 1.0 - f
    q = _silu(hq)
    b = _dot(tri.astype(F32), lf, NN, precision=HIGHEST)
    m = b[C // 2 - 1:C // 2]
    bl = b[C - 1:C]
    e_qm, e_km, e_kl, e_q = jnp.exp(b - m), jnp.exp(m - b), jnp.exp(bl - b), jnp.exp(b)
    qe, ke, kd, qb = q * e_qm, k * e_km, k * e_kl, q * e_q
    sc = jnp.where(tri, _bdot(qe, ke, NT), 0.0)
    o = _bdot(sc, hi, NN) + _bdot(qb, st, NT)
    dec = jnp.exp(bl)
    st_next = st * dec + _bdot(hi, kd, TN)
    return o, st_next, dict(tri=tri, sg=sg, f=f, k=k, q=q, qe=qe, ke=ke, kd=kd, qb=qb, sc=sc, dec=dec,
                            e_qm=e_qm, e_km=e_km, e_kl=e_kl, e_q=e_q)


def _hg_out(o, hgate, gout):
    return _rms(o, gout) * _silu(hgate)


HG_GROUP = 16


def _hgrn_fwd(p4, lb_logits, gout, H, comm=None):
    L = p4.shape[0]
    C = HG_CHUNK
    GR = _tile(L // C, HG_GROUP, 1)
    T = GR * C
    N = L // T

    def body(hq_ref, hf_ref, hi_ref, hg_ref, lg_ref, gout_ref, o_ref, s_ref, st_ref):
        @pl.when(pl.program_id(1) == 0)
        def _():
            st_ref[...] = jnp.zeros_like(st_ref)

        lb = _lower_bound(lg_ref[...])
        st = st_ref[...]
        for ci in range(GR):
            rows = pl.ds(ci * C, C)
            s_ref[0, ci] = st
            o, st, _ = _hg_chunk(hq_ref[rows, :], hf_ref[rows, :], hi_ref[rows, :], lb, st)
            o_ref[rows, :] = _hg_out(o, hg_ref[rows, :], gout_ref[...]).astype(o_ref.dtype)
        st_ref[...] = st

    blk = lambda s: pl.BlockSpec((T, HG_DK), functools.partial(lambda h, n, s: (n, s * H + h), s=s))
    return _call(
        body, [p4, p4, p4, p4, lb_logits, gout], name="hgrn_fwd", grid=(H, N),
        out_shape=(jax.ShapeDtypeStruct((L, H * HG_DK), BF16), jax.ShapeDtypeStruct((H, N * GR, HG_DK, HG_DK), F32)),
        in_specs=[blk(0), blk(1), blk(2), blk(3), pl.BlockSpec((2, HG_DK), lambda h, n: (0, h)),
                  pl.BlockSpec((1, HG_DK), lambda h, n: (0, 0))],
        out_specs=(pl.BlockSpec((T, HG_DK), lambda h, n: (n, h)),
                   pl.BlockSpec((1, GR, HG_DK, HG_DK), lambda h, n: (h, n, 0, 0))),
        scratch_shapes=[pltpu.VMEM((HG_DK, HG_DK), F32)], comm=comm)


def _hgrn_bwd(p4, lb_logits, gout, s_all, d_out, H, comm=None):
    L = p4.shape[0]
    C = HG_CHUNK
    GR = _tile(L // C, HG_GROUP, 1)
    T = GR * C
    N = L // T

    def body(hq_ref, hf_ref, hi_ref, hg_ref, lg_ref, gout_ref, s_ref, do_ref,
             dq_ref, df_ref, di_ref, dg_ref, dlb_ref, dgo_ref, dst_ref):
        @pl.when(pl.program_id(1) == 0)
        def _():
            dst_ref[...] = jnp.zeros_like(dst_ref)
            dlb_ref[...] = jnp.zeros_like(dlb_ref)

        @pl.when(jnp.logical_and(pl.program_id(0) == 0, pl.program_id(1) == 0))
        def _():
            dgo_ref[...] = jnp.zeros_like(dgo_ref)

        lb = _lower_bound(lg_ref[...])
        dst = dst_ref[...]
        d_lb = jnp.zeros((1, HG_DK), F32)
        d_go = jnp.zeros((1, HG_DK), F32)
        for ci in reversed(range(GR)):
            rows = pl.ds(ci * C, C)
            dst, d_lb_c, d_go_c = chunk_bwd(rows, lb, s_ref[0, ci], dst, hq_ref, hf_ref, hi_ref, hg_ref, gout_ref, do_ref,
                                            dq_ref, df_ref, di_ref, dg_ref)
            d_lb += d_lb_c
            d_go += d_go_c
        dst_ref[...] = dst
        dlb_ref[...] += d_lb
        dgo_ref[...] += d_go

    def chunk_bwd(rows, lb, st, dst_next, hq_ref, hf_ref, hi_ref, hg_ref, gout_ref, do_ref, dq_ref, df_ref, di_ref, dg_ref):
        hq, hf, hi, hgate = hq_ref[rows, :], hf_ref[rows, :], hi_ref[rows, :], hg_ref[rows, :]
        o, _, t = _hg_chunk(hq, hf, hi, lb, st)
        _, out_vjp = jax.vjp(_hg_out, o, hgate, gout_ref[...])
        do, d_hgate, d_gout = out_vjp(do_ref[rows, :])
        tri = t["tri"]
        dsc = jnp.where(tri, _bdot(do, hi, NT), 0.0)
        dv = _bdot(t["sc"], do, TN) + _bdot(t["kd"], dst_next, NT)
        dqe = _bdot(dsc, t["ke"], NN)
        dke = _bdot(dsc, t["qe"], TN)
        dqb = _bdot(do, st, NN)
        dkd = _bdot(hi, dst_next, NN)
        ddec = jnp.sum(dst_next * st, axis=0, keepdims=True)
        dst_prev = _bdot(do, t["qb"], TN) + dst_next * t["dec"]
        dq = dqe * t["e_qm"] + dqb * t["e_q"]
        dk = dke * t["e_km"] + dkd * t["e_kl"]
        tq, tk, td, tb = dqe * t["qe"], dke * t["ke"], dkd * t["kd"], dqb * t["qb"]
        db = tq - tk - td + tb
        dm = jnp.sum(tk - tq, axis=0, keepdims=True)
        dbl = jnp.sum(td, axis=0, keepdims=True) + ddec * t["dec"]
        rowi = lax.broadcasted_iota(jnp.int32, (C, HG_DK), 0)
        db = db + jnp.where(rowi == C // 2 - 1, dm, 0.0) + jnp.where(rowi == C - 1, dbl, 0.0)
        dlf = _dot(tri.astype(F32), db, TN, precision=HIGHEST)
        df = dlf / t["f"] - dk
        sg = t["sg"]
        df_ref[rows, :] = (df * (1.0 - lb) * sg * (1.0 - sg)).astype(df_ref.dtype)
        sq = jax.nn.sigmoid(hq)
        dq_ref[rows, :] = (dq * (sq * (1.0 + hq * (1.0 - sq)))).astype(dq_ref.dtype)
        di_ref[rows, :] = dv.astype(di_ref.dtype)
        dg_ref[rows, :] = d_hgate.astype(dg_ref.dtype)
        return dst_prev, jnp.sum(df * (1.0 - sg), axis=0, keepdims=True), d_gout

    blk = lambda s: pl.BlockSpec((T, HG_DK), functools.partial(lambda h, n, s: (N - 1 - n, s * H + h), s=s))
    oblk = pl.BlockSpec((T, HG_DK), lambda h, n: (N - 1 - n, h))
    vec = pl.BlockSpec((1, HG_DK), lambda h, n: (0, h))
    W = H * HG_DK
    return _call(
        body, [p4, p4, p4, p4, lb_logits, gout, s_all, d_out], name="hgrn_bwd", grid=(H, N),
        out_shape=tuple([jax.ShapeDtypeStruct((L, W), BF16)] * 4 + [jax.ShapeDtypeStruct((1, W), F32), jax.ShapeDtypeStruct((1, HG_DK), F32)]),
        in_specs=[blk(0), blk(1), blk(2), blk(3), pl.BlockSpec((2, HG_DK), lambda h, n: (0, h)),
                  pl.BlockSpec((1, HG_DK), lambda h, n: (0, 0)),
                  pl.BlockSpec((1, GR, HG_DK, HG_DK), lambda h, n: (h, N - 1 - n, 0, 0)), oblk],
        out_specs=(oblk, oblk, oblk, oblk, vec, pl.BlockSpec((1, HG_DK), lambda h, n: (0, 0))),
        scratch_shapes=[pltpu.VMEM((HG_DK, HG_DK), F32)], comm=comm)


def _bucket_ids():
    i = jnp.arange(AT_BLOCK, dtype=jnp.int32)[:, None]
    j = jnp.arange(2 * AT_BLOCK, dtype=jnp.int32)[None, :]
    n = jnp.maximum(i - j + AT_BLOCK, 0)
    nf = jnp.maximum(n, 1).astype(F32)
    large = MAX_EXACT + (jnp.log(nf / MAX_EXACT) / math.log(MAX_DISTANCE / MAX_EXACT) * (N_BUCKETS - MAX_EXACT)).astype(jnp.int32)
    large = jnp.minimum(large, N_BUCKETS - 1)
    return jnp.where(n < MAX_EXACT, n, large).reshape(1, -1)


def _onehot(bucket):
    ids = lax.broadcasted_iota(jnp.int32, (N_BUCKETS, bucket.shape[1]), 0)
    return (ids == bucket).astype(F32)


def _attn_probs(qn, kpn, kcn, bias_g, sink, first, scale):
    rows = qn.shape[0]
    i = jnp.bitwise_and(lax.broadcasted_iota(jnp.int32, (rows, AT_BLOCK), 0), AT_BLOCK - 1)
    j = lax.broadcasted_iota(jnp.int32, (rows, AT_BLOCK), 1)
    lp = _bdot(qn, kpn, NT) * scale + bias_g[:, :AT_BLOCK]
    lc = _bdot(qn, kcn, NT) * scale + bias_g[:, AT_BLOCK:]
    lp = jnp.where(jnp.logical_and(j > i, jnp.logical_not(first)), lp, NEG_INF)
    lc = jnp.where(j <= i, lc, NEG_INF)
    m = jnp.maximum(jnp.maximum(jnp.max(lp, axis=-1, keepdims=True), jnp.max(lc, axis=-1, keepdims=True)), sink)
    pp, pc, ps = jnp.exp(lp - m), jnp.exp(lc - m), jnp.exp(sink - m)
    den = jnp.sum(pp, axis=-1, keepdims=True) + jnp.sum(pc, axis=-1, keepdims=True) + ps
    return pp / den, pc / den, ps / den


def _sink_rows(sk_ref, G):
    head = lax.broadcasted_iota(jnp.int32, (G * AT_BLOCK, 1), 0) // AT_BLOCK
    sink = jnp.zeros((G * AT_BLOCK, 1), F32)
    for g in range(G):
        sink = jnp.where(head == g, sk_ref[0, g:g + 1, :], sink)
    return sink


def _attn_fwd(q_t, kp, vp, qg, kg, sinks, bias, KVH, comm=None):
    AH, L, DH = q_t.shape
    G = AH // KVH
    NB = L // AT_BLOCK
    scale = DH ** -0.5

    def body(q_ref, kp_ref, kc_ref, vp_ref, vc_ref, qg_ref, kg_ref, sk_ref, b_ref, o_ref):
        first = pl.program_id(1) == 0
        kpn, kcn = _rms(kp_ref[0], kg_ref[...]), _rms(kc_ref[0], kg_ref[...])
        qn = _rms(q_ref[...].reshape(G * AT_BLOCK, DH), qg_ref[...])
        sink = _sink_rows(sk_ref, G)
        pp, pc, _ = _attn_probs(qn, kpn, kcn, b_ref[...].reshape(G * AT_BLOCK, 2 * AT_BLOCK), sink, first, scale)
        o = _bdot(pp, vp_ref[0], NN) + _bdot(pc, vc_ref[0], NN)
        o_ref[...] = o.reshape(G, AT_BLOCK, DH).astype(o_ref.dtype)

    kblk = lambda off: pl.BlockSpec((1, AT_BLOCK, DH), functools.partial(lambda h, n, off: (h, n + off, 0), off=off))
    return _call(
        body, [q_t, kp, kp, vp, vp, qg, kg, sinks, bias], name="attn_fwd", grid=(KVH, NB),
        out_shape=jax.ShapeDtypeStruct((AH, L, DH), BF16),
        in_specs=[pl.BlockSpec((G, AT_BLOCK, DH), lambda h, n: (h, n, 0)), kblk(0), kblk(1), kblk(0), kblk(1),
                  pl.BlockSpec((1, DH), lambda h, n: (0, 0)), pl.BlockSpec((1, DH), lambda h, n: (0, 0)),
                  pl.BlockSpec((1, G, 1), lambda h, n: (h, 0, 0)),
                  pl.BlockSpec((G, AT_BLOCK, 2 * AT_BLOCK), lambda h, n: (h, 0, 0))],
        out_specs=pl.BlockSpec((G, AT_BLOCK, DH), lambda h, n: (h, n, 0)), comm=comm)


def _attn_bwd(q_t, kp, vp, qg, kg, sinks, bias, do_t, KVH, comm=None):
    AH, L, DH = q_t.shape
    G = AH // KVH
    NB = L // AT_BLOCK
    B = AT_BLOCK
    scale = DH ** -0.5

    def body(q_ref, kp_ref, kc_ref, vp_ref, vc_ref, qg_ref, kg_ref, sk_ref, b_ref, do_ref,
             dq_ref, dk_ref, dv_ref, dqg_ref, dkg_ref, dsk_ref, db_ref):
        n = pl.program_id(1)
        first = n == 0

        @pl.when(first)
        def _():
            for r in (dk_ref, dv_ref, dsk_ref, db_ref):
                r[...] = jnp.zeros_like(r)

        @pl.when(jnp.logical_and(first, pl.program_id(0) == 0))
        def _():
            dqg_ref[...] = jnp.zeros_like(dqg_ref)
            dkg_ref[...] = jnp.zeros_like(dkg_ref)

        kp_raw, kc_raw, kgv, qgv = kp_ref[0], kc_ref[0], kg_ref[...], qg_ref[...]
        kpn, kp_vjp = jax.vjp(_rms, kp_raw, kgv)
        kcn, kc_vjp = jax.vjp(_rms, kc_raw, kgv)
        qn, q_vjp = jax.vjp(_rms, q_ref[...].reshape(G * B, DH), qgv)
        pp, pc, ps = _attn_probs(qn, kpn, kcn, b_ref[...].reshape(G * B, 2 * B), _sink_rows(sk_ref, G), first, scale)
        do = do_ref[...].reshape(G * B, DH)
        dvp = _bdot(pp, do, TN)
        dvc = _bdot(pc, do, TN)
        dpp = _bdot(do, vp_ref[0], NT)
        dpc = _bdot(do, vc_ref[0], NT)
        dsum = jnp.sum(dpp * pp, axis=-1, keepdims=True) + jnp.sum(dpc * pc, axis=-1, keepdims=True)
        dlp = pp * (dpp - dsum)
        dlc = pc * (dpc - dsum)
        dsk_ref[0] += jnp.sum((-ps * dsum).reshape(G, B, 1), axis=1)
        db_ref[:, :, :B] += dlp.reshape(G, B, B)
        db_ref[:, :, B:] += dlc.reshape(G, B, B)
        dlp, dlc = dlp * scale, dlc * scale
        dqn = _bdot(dlp, kpn, NN) + _bdot(dlc, kcn, NN)
        dq_raw, dqg = q_vjp(dqn)
        dq_ref[...] = dq_raw.reshape(G, B, DH).astype(dq_ref.dtype)
        dkp_raw, dkg_p = kp_vjp(_bdot(dlp, qn, TN))
        dkc_raw, dkg_c = kc_vjp(_bdot(dlc, qn, TN))
        r0 = pl.multiple_of(n * B, B)
        r1 = pl.multiple_of(n * B + B, B)
        dk_ref[0, pl.ds(r0, B), :] += dkp_raw
        dk_ref[0, pl.ds(r1, B), :] += dkc_raw
        dv_ref[0, pl.ds(r0, B), :] += dvp
        dv_ref[0, pl.ds(r1, B), :] += dvc
        dqg_ref[...] += dqg
        dkg_ref[...] += dkg_p + dkg_c

    kblk = lambda off: pl.BlockSpec((1, B, DH), functools.partial(lambda h, n, off: (h, n + off, 0), off=off))
    qblk = pl.BlockSpec((G, B, DH), lambda h, n: (h, n, 0))
    accblk = pl.BlockSpec((1, L + B, DH), lambda h, n: (h, 0, 0))
    vecblk = pl.BlockSpec((1, DH), lambda h, n: (0, 0))
    return _call(
        body, [q_t, kp, kp, vp, vp, qg, kg, sinks, bias, do_t], name="attn_bwd", grid=(KVH, NB),
        out_shape=(jax.ShapeDtypeStruct((AH, L, DH), BF16), jax.ShapeDtypeStruct((KVH, L + B, DH), F32),
                   jax.ShapeDtypeStruct((KVH, L + B, DH), F32), jax.ShapeDtypeStruct((1, DH), F32),
                   jax.ShapeDtypeStruct((1, DH), F32), jax.ShapeDtypeStruct((KVH, G, 1), F32),
                   jax.ShapeDtypeStruct((AH, B, 2 * B), F32)),
        in_specs=[qblk, kblk(0), kblk(1), kblk(0), kblk(1),
                  pl.BlockSpec((1, DH), lambda h, n: (0, 0)), pl.BlockSpec((1, DH), lambda h, n: (0, 0)),
                  pl.BlockSpec((1, G, 1), lambda h, n: (h, 0, 0)),
                  pl.BlockSpec((G, B, 2 * B), lambda h, n: (h, 0, 0)), qblk],
        out_specs=(qblk, accblk, accblk, vecblk, vecblk, pl.BlockSpec((1, G, 1), lambda h, n: (h, 0, 0)),
                   pl.BlockSpec((G, B, 2 * B), lambda h, n: (h, 0, 0))), comm=comm)


def _heads_first(t, nh):
    L = t.shape[0]
    return jnp.transpose(t.reshape(L, nh, t.shape[1] // nh), (1, 0, 2))


def _heads_last(t):
    nh, L, dh = t.shape
    return jnp.transpose(t, (1, 0, 2)).reshape(L, nh * dh)


def _softmax0(lg):
    e = jnp.exp(lg - jnp.max(lg, axis=0, keepdims=True))
    return e[0:1] / jnp.sum(e, axis=0, keepdims=True)


def _ada_update_call(fn, c_all, d_cols, w, m, v, rt):
    D, n = w.shape

    def body(c_ref, d_ref, w_ref, m_ref, v_ref, g_out, dl_out, m_out, v_out):
        outs, _ = fn(c_ref[...], d_ref[...], w_ref[...], m_ref[...], v_ref[...])
        for r, val in zip((g_out, dl_out, m_out, v_out), outs):
            r[...] = val

    wblk = pl.BlockSpec((rt, n), lambda i: (i, 0))
    return _call(
        body, [c_all, d_cols, w, m, v], name="update_ada", grid=(D // rt,), out_shape=tuple([jax.ShapeDtypeStruct((D, n), F32)] * 4),
        in_specs=[pl.BlockSpec((N_DEV, rt), lambda i: (0, i)), pl.BlockSpec((N_DEV, n), lambda i: (0, 0)), wblk, wblk, wblk],
        out_specs=(wblk, wblk, wblk, wblk))


def kernel(x, c, w_ada, b_ada, norm1_g, norm2_g, w_in, hg_lb_logits, hg_out_norm_g, q_norm_g, k_norm_g, attn_sinks, rel_bias_table, w_branch_hg, w_branch_attn, w_out, w_ff1, w_ff2, loss_target, m_w_ada, m_b_ada, m_norm1_g, m_norm2_g, m_w_in, m_hg_lb_logits, m_hg_out_norm_g, m_q_norm_g, m_k_norm_g, m_attn_sinks, m_rel_bias_table, m_w_branch_hg, m_w_branch_attn, m_w_out, m_w_ff1, m_w_ff2, v_w_ada, v_b_ada, v_norm1_g, v_norm2_g, v_w_in, v_hg_lb_logits, v_hg_out_norm_g, v_q_norm_g, v_k_norm_g, v_attn_sinks, v_rel_bias_table, v_w_branch_hg, v_w_branch_attn, v_w_out, v_w_ff1, v_w_ff2):
    cc = lax.axis_index("c")
    me = 4 * lax.axis_index("x") + 2 * lax.axis_index("y") + cc
    x2 = x[0]
    tgt = loss_target[0]
    L, D = x2.shape
    HGW = hg_lb_logits.shape[1]
    H = HGW // HG_DK
    AH = attn_sinks.shape[1]
    DH = q_norm_g.shape[1]
    ATW = AH * DH
    BW = w_in.shape[2]
    INW = BW * N_DEV
    A = BW // LANES
    assert BW == LANES * A + LANES // 2
    KVW = (INW - 4 * HGW - ATW - 2 * D) // 2
    KVH = KVW // DH
    G = AH // KVH
    ADA_N = w_ada.shape[2]
    PAIR = 2 * A + 1

    c_all = _gather_small(c, me, "gather_c")[:, 0, :]
    b_cols = lax.dynamic_slice(b_ada, (0, me * ADA_N), (1, ADA_N))
    (ada_cols,) = _whole(lambda cv, w, b: (_bdot(_silu(cv), w, NN) + b,), [c_all, w_ada[0], b_cols],
                         [((N_DEV, ADA_N), F32)], "ada_fwd")
    ada_all = _gather_small(ada_cols, me, "gather_ada")
    ada_row = lax.dynamic_slice(ada_all, (0, me, 0), (N_DEV, 1, ADA_N)).reshape(1, 6 * D)

    w_in_b = w_in[0].astype(BF16)
    src_in = jnp.where(cc == 0, jnp.pad(w_in_b, ((0, 0), (0, LANES // 2))), jnp.pad(w_in_b, ((0, 0), (LANES // 2, 0))))
    src_in, ada_row = _behind([src_in, ada_row], [ada_row])
    shift1, scale1, gate1, shift2, scale2, gate2 = [ada_row[:, i * D:(i + 1) * D] for i in range(6)]
    w_in_gapped, w_in_mid = _ag_w_in(src_in, A, D, INW)
    w_in_full = _patch_mid(w_in_gapped, w_in_mid, A)

    wnames = ("bhg", "bat", "out", "ff1", "ff2")
    small = ("bhg", "bat", "out")
    waxis = dict(zip(wnames, (1, 1, 0, 1, 0)))
    wsrc = dict(zip(wnames, (w_branch_hg, w_branch_attn, w_out, w_ff1, w_ff2)))
    wblk = {k: wsrc[k][0].astype(BF16) for k in wnames}
    wf = {}
    blks = _behind([wblk[k] for k in wnames], [w_in_full])
    ag_small = _ag_split_start(blks[:3], [waxis[k] for k in small], "ag_small_start")
    (blk_ff1,) = _behind([blks[3]], [ag_small["token"]])
    ag_ff1 = _ag_split_start([blk_ff1], [waxis["ff1"]], "ag_ff1_start")
    (blk_ff2,) = _behind([blks[4]], [ag_ff1["token"]])
    ag_ff2 = _ag_split_start([blk_ff2], [waxis["ff2"]], "ag_ff2_start")

    (x2_,) = _behind([x2], [ag_ff2["token"]])
    (h,) = _rowwise(lambda xv, g, sh, sc: ((_modnorm(xv, g, sh, sc),), ()), [(x2_, D, 0)], [norm1_g, shift1, scale1],
                    [(D, BF16)], [], "norm1")
    o4, oa = 4 * HGW, 4 * HGW + ATW + 2 * KVW
    p4 = _mm(h, w_in_full, "nn", F32, "proj_hg", n=o4)
    pa = _mm(h, w_in_full, "nn", F32, "proj_at", b_off=o4, n=oa - o4)
    pg = _mm(h, w_in_full, "nn", F32, "proj_gate", b_off=oa, n=INW - oa)
    o_hg, s_all = _hgrn_fwd(p4, hg_lb_logits, hg_out_norm_g, H)
    small_blks, small_fulls = _ag_split_wait(ag_small, [o_hg], "ag_small_wait")

    bucket = _bucket_ids()
    (bias_flat,) = _whole(lambda tb, bk: (_dot(tb, _onehot(bk), TN, precision=HIGHEST),), [rel_bias_table, bucket],
                          [((AH, AT_BLOCK * 2 * AT_BLOCK), F32)], "bias_fwd")
    bias = bias_flat.reshape(AH, AT_BLOCK, 2 * AT_BLOCK)
    q_t = _heads_first(pa[:, :ATW], AH)
    pad = lambda t: jnp.pad(t, ((0, 0), (AT_BLOCK, 0), (0, 0)))
    kp = pad(_heads_first(pa[:, ATW:ATW + KVW], KVH))
    vp = pad(_heads_first(pa[:, ATW + KVW:], KVH))
    sinks3 = attn_sinks.reshape(KVH, G, 1)
    cm = _Comm()
    hs = {k: _ag_d2d_own(cm, small_fulls[i], small_blks[i], waxis[k]) for i, k in enumerate(small)}
    o_at = _heads_last(_attn_fwd(q_t, kp, vp, q_norm_g, k_norm_g, sinks3, bias, KVH, comm=cm))
    for k in small:
        wf[k] = cm.result(hs[k])

    bh = _mm(o_hg, wf["bhg"], "nn", F32, "branch_hg")
    ba = _mm(o_at, wf["bat"], "nn", F32, "branch_at")

    def merge_fn(bhv, bav, ghg, gat):
        return jax.nn.sigmoid(ghg) * bhv + jax.nn.sigmoid(gat) * bav

    (blk_ff1,), (full_ff1,) = _ag_split_wait(ag_ff1, [ba], "ag_ff1_wait")
    cm = _Comm()
    hs = {"ff1": _ag_d2d_own(cm, full_ff1, blk_ff1, waxis["ff1"])}
    (merged,) = _rowwise(lambda *a: ((merge_fn(*a),), ()), [(bh, D, 0), (ba, D, 0), (pg, D, 0), (pg, D, 1)], [],
                         [(D, BF16)], [], "merge", comm=cm)
    wf["ff1"] = cm.result(hs["ff1"])
    mo = _mm(merged, wf["out"], "nn", F32, "out_proj")

    def resid1(xv, mov, g1, g2n, sh, sc):
        x1v = xv + g1 * mov
        return (x1v, _modnorm(x1v, g2n, sh, sc)), ()

    x1, h2 = _rowwise(resid1, [(x2, D, 0), (mo, D, 0)], [gate1, norm2_g, shift2, scale2], [(D, F32), (D, BF16)], [], "resid1")
    (blk_ff2,), (full_ff2,) = _ag_split_wait(ag_ff2, [h2], "ag_ff2_wait")
    cm = _Comm()
    hs = {"ff2": _ag_d2d_own(cm, full_ff2, blk_ff2, waxis["ff2"])}
    u, act = _mm(h2, wf["ff1"], "nn", (F32, BF16), "ff1", comm=cm, epi=lambda r: (r, jnp.square(jnp.maximum(r, 0.0))))
    wf["ff2"] = cm.result(hs["ff2"])
    ff = _mm(act, wf["ff2"], "nn", F32, "ff2")

    def loss_fn(x1v, ffv, tv, g2):
        e = x1v + g2 * ffv - tv
        dy = e * (1.0 / D)
        return (dy, dy * g2), (jnp.sum(e * e, axis=0, keepdims=True), jnp.sum(dy * ffv, axis=0, keepdims=True))

    dy, d_ff, sq_sum, d_gate2 = _rowwise(loss_fn, [(x1, D, 0), (ff, D, 0), (tgt, D, 0)], [gate2],
                                         [(D, F32), (D, BF16)], [(1, D), (1, D)], "loss")
    loss = lax.psum(jnp.sum(sq_sum) * (0.5 / D), ("x", "y", "c"))

    owner_base = jnp.stack([me ^ r for r in CHIP_RELS]).astype(jnp.int32)
    gw, recv1, part, recv2 = {}, {}, {}, {}
    gw["ff2"] = _mm(act, d_ff, "tn", BF16, "dw_ff2")
    cm = _Comm()
    hh = _rs_d2d(cm, gw["ff2"], waxis["ff2"])
    d_u = _mm(d_ff, wf["ff2"], "nt", BF16, "d_act", comm=cm, extras=[u], epi=lambda r, uv: (r * (2.0 * jnp.maximum(uv, 0.0)),))
    part["ff2"] = _rs_add(gw["ff2"], cm.result(hh), waxis["ff2"], owner_base, "rs_add_ff2")
    rs_ff2 = _rs_split_start([part["ff2"]], "rs_ff2_start")
    gw["ff1"] = _mm(h2, d_u, "tn", BF16, "dw_ff1", after=[rs_ff2["token"]])
    cm = _Comm()
    hh1 = _rs_d2d(cm, gw["ff1"], waxis["ff1"])
    d_h2 = _mm(d_u, wf["ff1"], "nt", F32, "d_h2", comm=cm)
    part["ff1"] = _rs_add(gw["ff1"], cm.result(hh1), waxis["ff1"], owner_base, "rs_add_ff1")
    (part_ff1,) = _behind([part["ff1"]], [rs_ff2["token"]])
    rs_ff1 = _rs_split_start([part_ff1], "rs_ff1_start")
    (d_h2,) = _behind([d_h2], [rs_ff1["token"]])

    def norm2_bwd(dh2v, x1v, dyv, mov, g2n, sh, sc, g1):
        _, vjp = jax.vjp(_modnorm, x1v, g2n, sh, sc)
        dx, dg, dsh, dsc = vjp(dh2v)
        dx1 = dyv + dx
        return (dx1, dx1 * g1), (dg, dsh, dsc, jnp.sum(dx1 * mov, axis=0, keepdims=True))

    d_x1, d_mo, d_g2n, d_shift2, d_scale2, d_gate1 = _rowwise(
        norm2_bwd, [(d_h2, D, 0), (x1, D, 0), (dy, D, 0), (mo, D, 0)], [norm2_g, shift2, scale2, gate1],
        [(D, F32), (D, BF16)], [(1, D)] * 4, "norm2_bwd")
    gw["out"] = _mm(merged, d_mo, "tn", BF16, "dw_out")
    cm = _Comm()
    hh = _rs_d2d(cm, gw["out"], waxis["out"])
    d_merged = _mm(d_mo, wf["out"], "nt", F32, "d_merged", comm=cm)
    part["out"] = _rs_add(gw["out"], cm.result(hh), waxis["out"], owner_base, "rs_add_out")

    def merge_bwd(dmv, bhv, bav, ghg, gat):
        _, vjp = jax.vjp(merge_fn, bhv, bav, ghg, gat)
        return vjp(dmv), ()

    d_bh, d_ba, d_ghg, d_gat = _rowwise(merge_bwd, [(d_merged, D, 0), (bh, D, 0), (ba, D, 0), (pg, D, 0), (pg, D, 1)], [],
                                        [(D, BF16)] * 4, [], "merge_bwd")
    gw["bhg"] = _mm(o_hg, d_bh, "tn", BF16, "dw_bhg")
    gw["bat"] = _mm(o_at, d_ba, "tn", BF16, "dw_bat")
    cm = _Comm()
    hh = {k: _rs_d2d(cm, gw[k], waxis[k]) for k in ("bhg", "bat")}
    d_ohg = _mm(d_bh, wf["bhg"], "nt", F32, "d_ohg", comm=cm)
    for k in ("bhg", "bat"):
        part[k] = _rs_add(gw[k], cm.result(hh[k]), waxis[k], owner_base, "rs_add_" + k)
    rs_small = _rs_split_start(_behind([part[k] for k in small], [rs_ff1["token"]]), "rs_small_start")
    (d_ba_,) = _behind([d_ba], [rs_small["token"]])
    d_oat = _mm(d_ba_, wf["bat"], "nt", BF16, "d_oat")

    d_hq, d_hf, d_hi, d_hg, d_lb, d_gout_h = _hgrn_bwd(p4, hg_lb_logits, hg_out_norm_g, s_all, d_ohg, H)
    dq_t, dkp, dvp, d_qg, d_kg, d_sk, d_bias = _attn_bwd(q_t, kp, vp, q_norm_g, k_norm_g, sinks3, bias,
                                                         _heads_first(d_oat, AH), KVH)
    d_aq = _heads_last(dq_t)
    d_ak = _heads_last(dkp[:, AT_BLOCK:, :]).astype(BF16)
    d_av = _heads_last(dvp[:, AT_BLOCK:, :]).astype(BF16)
    d_proj = jnp.concatenate([d_hq, d_hf, d_hi, d_hg, d_aq, d_ak, d_av, d_ghg, d_gat], axis=1)
    gw_in = _mm(h, d_proj, "tn", BF16, "dw_in")

    wm = LANES * A
    cm = _Comm()
    hi_ = cm.inp(gw_in)
    h_main, h_mid = cm.out((4, D, wm), BF16), cm.out((4, D, LANES), BF16)
    for i, r in enumerate(CHIP_RELS):
        def main_view(ref, p, r=r):
            o = p["me"] ^ r ^ 1
            return ref.at[:, pl.ds(pl.multiple_of((PAIR * (o // 2) + (A + 1) * (1 - p["c"])) * LANES, LANES), wm)]

        def mid_view(ref, p, r=r):
            o = p["me"] ^ r
            return ref.at[:, pl.ds(pl.multiple_of((PAIR * (o // 2) + A) * LANES, LANES), LANES)]

        cm.copy(hi_, main_view, h_main, _slot_view(i), 1)
        cm.copy(hi_, mid_view, h_mid, _slot_view(i), 1)
    _call(lambda: None, [], name="rs_d2d_in", out_shape=(), comm=cm)
    chip = jnp.stack([(me ^ r) // 2 for r in CHIP_RELS]).astype(jnp.int32)
    part_main = _rs_add(gw_in, cm.result(h_main), 1, PAIR * chip + (A + 1) * cc, "rs_add_in_main", tw=LANES)
    part_mid = _rs_add(gw_in, cm.result(h_mid), 1, PAIR * chip + A, "rs_add_in_mid", tw=LANES)
    rs_in = _rs_split_start([part_main, part_mid], "rs_in_start")
    d_h = _mm(d_proj, w_in_full, "nt", F32, "d_h", after=[rs_in["token"]])

    def norm1_bwd(dhv, xv, dx1v, g1n, sh, sc):
        _, vjp = jax.vjp(_modnorm, xv, g1n, sh, sc)
        dx, dg, dsh, dsc = vjp(dhv)
        return (dx1v + dx,), (dg, dsh, dsc)

    grad_x, d_g1n, d_shift1, d_scale1 = _rowwise(norm1_bwd, [(d_h, D, 0), (x2, D, 0), (d_x1, D, 0)],
                                                 [norm1_g, shift1, scale1], [(D, F32)], [(1, D)] * 3, "norm1_bwd")

    def sum4(p0, p1, p2, p3):
        return ((p0.astype(F32) + p1.astype(F32)) + p2.astype(F32)) + p3.astype(F32)

    def update_fn(w, m, v, p0, p1, p2, p3):
        g = sum4(p0, p1, p2, p3)
        delta, mn, vn = _adamw(w, g, m, v)
        return (g, delta, mn, vn), ()

    wmv = dict(zip(wnames, ((w_branch_hg, m_w_branch_hg, v_w_branch_hg), (w_branch_attn, m_w_branch_attn, v_w_branch_attn),
                            (w_out, m_w_out, v_w_out), (w_ff1, m_w_ff1, v_w_ff1), (w_ff2, m_w_ff2, v_w_ff2))))
    res = {}

    def update(k, p, rx):
        w, m, v = (t[0] for t in wmv[k])
        n = w.shape[1]
        ins = [(t, n, 0) for t in (w, m, v)] + [(p, n, 0, 0)] + [(rx, n, 0, i) for i in range(3)]
        res[k] = [t[None] for t in _rowwise(update_fn, ins, [], [(n, F32)] * 4, [], "update_" + k)]

    (p_,), (rx_,) = _rs_split_wait(rs_ff2, [grad_x], "rs_ff2_wait")
    update("ff2", p_, rx_)
    (p_,), (rx_,) = _rs_split_wait(rs_ff1, [res["ff2"][0]], "rs_ff1_wait")
    update("ff1", p_, rx_)
    ps_, rxs_ = _rs_split_wait(rs_small, [res["ff1"][0]], "rs_small_wait")
    for i, k in enumerate(small):
        update(k, ps_[i], rxs_[i])
    (part_main, part_mid), (rx_main, rx_mid) = _rs_split_wait(rs_in, [res[k][0] for k in small], "rs_in_wait")
    g_main, = _rowwise(lambda *p: ((sum4(*p),), ()), [(part_main, wm, 0, 0)] + [(rx_main, wm, 0, i) for i in range(3)], [],
                       [(wm, F32)], [], "sum_in_main")
    g_mid, = _rowwise(lambda *p: ((sum4(*p),), ()), [(part_mid, LANES, 0, 0)] + [(rx_mid, LANES, 0, i) for i in range(3)], [],
                      [(LANES, F32)], [], "sum_in_mid")
    g_in = jnp.where(cc == 0, jnp.concatenate([g_main, g_mid[:, :LANES // 2]], axis=1),
                     jnp.concatenate([g_mid[:, LANES // 2:], g_main], axis=1))

    def update_given(w, m, v, g):
        delta, mn, vn = _adamw(w, g, m, v)
        return (g, delta, mn, vn), ()

    res["in"] = [t[None] for t in _rowwise(update_given, [(t, BW, 0) for t in (w_in[0], m_w_in[0], v_w_in[0], g_in)], [],
                                           [(BW, F32)] * 4, [], "update_in")]

    d_ada_row = jnp.concatenate([d_shift1, d_scale1, d_gate1, d_shift2, d_scale2, d_gate2], axis=1)
    (d_ada_row,) = _behind([d_ada_row], [g_mid])
    d_ada_all = _gather_small(d_ada_row, me, "gather_dada")[:, 0, :]
    d_ada_cols = lax.dynamic_slice(d_ada_all, (0, me * ADA_N), (N_DEV, ADA_N))

    def ada_update(cv, dav, w, m, v):
        g = _bdot(_silu(cv), dav, TN)
        delta, mn, vn = _adamw(w, g, m, v)
        return (g, delta, mn, vn), ()

    res["ada"] = [t[None] for t in _ada_update_call(ada_update, c_all, d_ada_cols, w_ada[0], m_w_ada[0], v_w_ada[0], _tile(D, 256, 16))]

    d_sinks = d_sk.reshape(1, AH)
    (d_table_t,) = _whole(lambda db, bk: (_dot(db, _onehot(bk), NT, precision=HIGHEST),),
                          [d_bias.reshape(AH, AT_BLOCK * 2 * AT_BLOCK), bucket], [((AH, N_BUCKETS), F32)], "bias_bwd")
    smalls = [d_g1n, d_g2n, d_lb, d_gout_h, d_qg, d_kg, d_sinks, d_table_t.T.reshape(1, N_BUCKETS * AH)]
    widths = [s.shape[1] for s in smalls]
    lanes = [-(-w // LANES) * LANES for w in widths]
    smalls = [jnp.pad(s, ((0, 0), (0, p - w))) for s, w, p in zip(smalls, widths, lanes)]
    (smalls_row,) = _behind([jnp.concatenate(smalls, axis=1)], [g_mid])
    packed = _gather_small(smalls_row, me, "gather_small")[:, 0, :]
    offs = [sum(lanes[:i]) for i in range(len(lanes))]

    def small_update(pk, dada, lg, *wmv_flat):
        tot = pk[0:1]
        for d in range(1, N_DEV):
            tot = tot + pk[d:d + 1]
        gb = dada[0:1]
        for d in range(1, N_DEV):
            gb = gb + dada[d:d + 1]
        gs = [tot[:, offs[i]:offs[i] + widths[i]] for i in range(len(widths))]
        _, lb_vjp = jax.vjp(_softmax0, lg)
        (g_lg,) = lb_vjp(gs[2])
        grads = [gb, gs[0], gs[1], g_lg, gs[3], gs[4], gs[5], gs[6], gs[7]]
        outs = []
        for i, g in enumerate(grads):
            w, m, v = wmv_flat[3 * i:3 * i + 3]
            delta, mn, vn = _adamw(w, g, m, v)
            outs += [g, delta, mn, vn]
        return tuple(outs)

    tbl = lambda t: t.reshape(1, N_BUCKETS * AH)
    small_wmv = [(b_ada, m_b_ada, v_b_ada), (norm1_g, m_norm1_g, v_norm1_g), (norm2_g, m_norm2_g, v_norm2_g),
                 (hg_lb_logits, m_hg_lb_logits, v_hg_lb_logits), (hg_out_norm_g, m_hg_out_norm_g, v_hg_out_norm_g),
                 (q_norm_g, m_q_norm_g, v_q_norm_g), (k_norm_g, m_k_norm_g, v_k_norm_g),
                 (attn_sinks, m_attn_sinks, v_attn_sinks),
                 (tbl(rel_bias_table), tbl(m_rel_bias_table), tbl(v_rel_bias_table))]
    flat = [t for trip in small_wmv for t in trip]
    out_shapes = [(trip[0].shape, F32) for trip in small_wmv for _ in range(4)]
    sres = _whole(small_update, [packed, d_ada_all, hg_lb_logits] + flat, out_shapes, "small_update")
    names_small = ("b_ada", "norm1_g", "norm2_g", "lb", "gout", "qg", "kg", "sinks", "table")
    for i, k in enumerate(names_small):
        r = sres[4 * i:4 * i + 4]
        if k == "table":
            r = [t.reshape(N_BUCKETS, AH) for t in r]
        res[k] = r

    order = ("ada", "b_ada", "norm1_g", "norm2_g", "in", "lb", "gout", "qg", "kg", "sinks", "table", "bhg", "bat", "out", "ff1", "ff2")
    outs = [loss, grad_x[None]]
    for j in range(4):
        outs += [res[k][j] for k in order]
    return tuple(outs)
```

```python
import functools
import math

import jax
import jax.numpy as jnp
from jax import lax
from jax.experimental import pallas as pl
from jax.experimental.pallas import tpu as pltpu

F32 = jnp.float32
BF16 = jnp.bfloat16
EPS = 1e-6
NEG_INF = -1e30
HG_DK = 128
HG_CHUNK = 64
AT_BLOCK = 128
N_BUCKETS = 32
MAX_EXACT = 16
MAX_DISTANCE = 128
N_DEV = 8
LANES = 128
VMEM_LIMIT = 56 * 1024 * 1024
ADAM_LR, ADAM_B1, ADAM_B2, ADAM_EPS, ADAM_WD, ADAM_STEP = 0.001, 0.9, 0.999, 1e-08, 0.01, 10
HIGHEST = lax.Precision.HIGHEST
MESH = pl.DeviceIdType.MESH
ANY = pl.BlockSpec(memory_space=pl.ANY)
CHIP_RELS = (0, 4, 2, 6)

NN = (((1,), (0,)), ((), ()))
NT = (((1,), (1,)), ((), ()))
TN = (((0,), (0,)), ((), ()))


def _tile(n, pref, unit):
    if n <= pref:
        return n
    t = (pref // unit) * unit
    while t >= unit:
        if n % t == 0:
            return t
        t -= unit
    return n


def _dot(a, b, dn, precision=None):
    return lax.dot_general(a, b, dn, preferred_element_type=F32, precision=precision)


def _bdot(a, b, dn):
    return _dot(a.astype(BF16), b.astype(BF16), dn)


def _position():
    x, y, c = lax.axis_index("x"), lax.axis_index("y"), lax.axis_index("c")
    return dict(x=x, y=y, c=c, me=4 * x + 2 * y + c)


def _peer_position(p, rel):
    x = 1 - p["x"] if rel & 4 else p["x"]
    y = 1 - p["y"] if rel & 2 else p["y"]
    c = 1 - p["c"] if rel & 1 else p["c"]
    return dict(x=x, y=y, c=c, me=4 * x + 2 * y + c)


class _Comm:
    def __init__(self):
        self.ins, self.outs, self.alias, self.plans, self.res = [], [], {}, [], None

    def inp(self, arr):
        self.ins.append(arr)
        return ("i", len(self.ins) - 1)

    def out(self, shape, dtype, alias=None):
        self.outs.append(jax.ShapeDtypeStruct(tuple(shape), dtype))
        if alias is not None:
            self.alias[alias[1]] = len(self.outs) - 1
        return ("o", len(self.outs) - 1)

    def copy(self, src, src_view, dst, dst_view, rel):
        self.plans.append((src, src_view, dst, dst_view, rel))

    def result(self, handle):
        return self.res[handle[1]]

    def build(self, in_refs, out_refs, send_sems, recv_sems):
        pos = _position()
        ref = lambda h: in_refs[h[1]] if h[0] == "i" else out_refs[h[1]]
        ops = []
        for k, (src, sv, dst, dv, rel) in enumerate(self.plans):
            s = sv(ref(src), pos)
            if rel == 0:
                cp = pltpu.make_async_copy(s, dv(ref(dst), pos), send_sems.at[k])
                ops.append((cp.start, cp.wait))
                continue
            peer = _peer_position(pos, rel)
            mk = lambda d: pltpu.make_async_remote_copy(
                src_ref=s, dst_ref=d, send_sem=send_sems.at[k], recv_sem=recv_sems.at[k],
                device_id=(peer["x"], peer["y"], peer["c"]), device_id_type=MESH)
            out_cp, in_cp = mk(dv(ref(dst), pos)), mk(dv(ref(dst), peer))

            def wait(out_cp=out_cp, in_cp=in_cp):
                out_cp.wait_send()
                in_cp.wait_recv()

            ops.append((out_cp.start, wait))
        return ops


def _call(body, args, *, name, out_shape, in_specs=None, out_specs=None, grid=None, scratch_shapes=(), comm=None,
          prefetch=None, aliases=None, after=()):
    single = not isinstance(out_shape, (tuple, list))
    out_shape = (out_shape,) if single else tuple(out_shape)
    n_in, n_out, n_scr = len(args), len(out_shape), len(scratch_shapes)
    vm = pl.BlockSpec(memory_space=pltpu.VMEM)
    in_specs = [vm] * n_in if in_specs is None else list(in_specs)
    out_specs = [vm] * n_out if out_specs is None else (list(out_specs) if isinstance(out_specs, (tuple, list)) else [out_specs])
    n_pf = 0 if prefetch is None else len(prefetch)
    kw = {} if aliases is None else {"input_output_aliases": dict(aliases)}
    if comm is None and after:
        n_dep = len(after)

        def fn(*refs):
            body(*refs[:n_pf + n_in], *refs[n_pf + n_in + n_dep:])

        all_args, all_scratch = list(args) + list(after), list(scratch_shapes)
        in_specs = in_specs + [ANY] * n_dep
    elif comm is None:
        fn = body
        all_args, all_scratch = list(args), list(scratch_shapes)
    else:
        n_ci, n_co, n_x = len(comm.ins), len(comm.outs), len(comm.plans)

        def fn(*refs):
            pf, refs = refs[:n_pf], refs[n_pf:]
            o_in, c_in = refs[:n_in], refs[n_in:n_in + n_ci]
            o_out = refs[n_in + n_ci:n_in + n_ci + n_out]
            c_out = refs[n_in + n_ci + n_out:n_in + n_ci + n_out + n_co]
            scr = refs[n_in + n_ci + n_out + n_co:]
            ops = comm.build(c_in, c_out, scr[n_scr], scr[n_scr + 1])
            if grid:
                first = functools.reduce(jnp.logical_and, [pl.program_id(i) == 0 for i in range(len(grid))])
                last = functools.reduce(jnp.logical_and, [pl.program_id(i) == g - 1 for i, g in enumerate(grid)])

                @pl.when(first)
                def _():
                    for start, _w in ops:
                        start()
            else:
                for start, _w in ops:
                    start()
            body(*pf, *o_in, *o_out, *scr[:n_scr])
            if grid:
                @pl.when(last)
                def _():
                    for _s, wait in ops:
                        wait()
            else:
                for _s, wait in ops:
                    wait()

        all_args = list(args) + list(comm.ins)
        in_specs = in_specs + [ANY] * n_ci
        out_shape = out_shape + tuple(comm.outs)
        out_specs = out_specs + [ANY] * n_co
        all_scratch = list(scratch_shapes) + [pltpu.SemaphoreType.DMA((n_x,)), pltpu.SemaphoreType.DMA((n_x,))]
        kw["input_output_aliases"] = {n_pf + n_in + i: n_out + o for i, o in comm.alias.items()}
    sem = None if grid is None else ("arbitrary",) * len(grid)
    params = pltpu.CompilerParams(dimension_semantics=sem, vmem_limit_bytes=VMEM_LIMIT)
    if prefetch is None:
        spec = dict(in_specs=in_specs, out_specs=tuple(out_specs), scratch_shapes=all_scratch)
        if grid is not None:
            spec["grid"] = grid
    else:
        spec = dict(grid_spec=pltpu.PrefetchScalarGridSpec(
            num_scalar_prefetch=n_pf, grid=grid, in_specs=in_specs, out_specs=tuple(out_specs), scratch_shapes=all_scratch))
        all_args = list(prefetch) + all_args
    res = pl.pallas_call(fn, name=name, out_shape=out_shape, compiler_params=params, **spec, **kw)(*all_args)
    res = list(res)
    if comm is not None:
        comm.res = res[n_out:]
        res = res[:n_out]
    return res[0] if single else res


def _whole_view(ref, pos):
    return ref


def _block_view(axis, n, index, rows=None):
    def view(ref, pos):
        off = pl.multiple_of(index(pos) * n, n)
        if rows is None:
            return ref.at[:, pl.ds(off, n)] if axis == 1 else ref.at[pl.ds(off, n), :]
        lo, cnt = rows[0], rows[1] - rows[0]
        if axis == 1:
            return ref.at[pl.ds(lo, cnt), pl.ds(off, n)]
        return ref.at[pl.ds(pl.multiple_of(off + lo, 16), cnt), :]
    return view


def _rows_view(rows):
    def view(ref, pos):
        return ref if rows is None else ref.at[pl.ds(rows[0], rows[1] - rows[0]), :]
    return view


def _slot_view(i, rows=None):
    def view(ref, pos):
        return ref.at[i] if rows is None else ref.at[i, pl.ds(rows[0], rows[1] - rows[0]), :]
    return view


def _exchange(items, name):
    cm = _Comm()
    for a, rel in items:
        cm.copy(cm.inp(a), _whole_view, cm.out(a.shape, a.dtype), _whole_view, rel)
    _call(lambda: None, [], name=name, out_shape=(), comm=cm)
    return cm.res


def _gather_small(v, me, name):
    cm = _Comm()
    hi, ho = cm.inp(v), cm.out((N_DEV,) + v.shape, v.dtype)
    for rel in range(N_DEV):
        cm.copy(hi, _whole_view, ho, lambda ref, p: ref.at[p["me"]], rel)
    _call(lambda: None, [], name=name, out_shape=(), comm=cm)
    return cm.result(ho)


def _ag_ici(cm, blk, axis, rows=None, into=None):
    n = blk.shape[axis]
    shape = list(blk.shape)
    shape[axis] = n * N_DEV
    hi = cm.inp(blk)
    ho = cm.out(shape, blk.dtype) if into is None else cm.out(shape, blk.dtype, alias=cm.inp(into))
    own = _block_view(axis, n, lambda p: p["me"], rows)
    for rel in CHIP_RELS:
        cm.copy(hi, _rows_view(rows), ho, own, rel)
    return ho


def _ag_d2d(cm, full, axis):
    n = full.shape[axis] // N_DEV
    hi = cm.inp(full)
    ho = cm.out(full.shape, full.dtype, alias=hi)
    for r in CHIP_RELS:
        v = _block_view(axis, n, functools.partial(lambda p, r: p["me"] ^ r, r=r))
        cm.copy(hi, v, ho, v, 1)
    return ho


def _rs_d2d(cm, gw, axis):
    n = gw.shape[axis] // N_DEV
    shape = list(gw.shape)
    shape[axis] = n
    hi, ho = cm.inp(gw), cm.out([4] + shape, gw.dtype)
    for i, r in enumerate(CHIP_RELS):
        cm.copy(hi, _block_view(axis, n, functools.partial(lambda p, r: p["me"] ^ r ^ 1, r=r)), ho, _slot_view(i), 1)
    return ho


def _rs_ici(cm, part, rows=None, recv=None):
    if recv is None:
        ho = cm.out((3,) + part.shape[1:], part.dtype)
    else:
        ho = cm.out(recv.shape, recv.dtype, alias=cm.inp(recv))
    hi = cm.inp(part)
    for i in (1, 2, 3):
        cm.copy(hi, _slot_view(i, rows), ho, _slot_view(i - 1, rows), CHIP_RELS[i])
    return ho


def _rs_add(gw, recv, axis, base, name, tw=None):
    _, R, n = recv.shape
    if axis == 1:
        tw = n if tw is None else tw
        gw_spec = pl.BlockSpec((R, tw), lambda i, t, b: (0, b[i] + t))
        rv_spec = pl.BlockSpec((None, R, tw), lambda i, t, b: (i, 0, t))
        grid = (4, n // tw)
    else:
        tw = _tile(n, 1024, LANES)
        gw_spec = pl.BlockSpec((R, tw), lambda i, t, b: (b[i], t))
        rv_spec = pl.BlockSpec((None, R, tw), lambda i, t, b: (i, 0, t))
        grid = (4, n // tw)

    def body(b_ref, g_ref, r_ref, o_ref):
        o_ref[...] = (g_ref[...].astype(F32) + r_ref[...].astype(F32)).astype(o_ref.dtype)

    return _call(body, [gw, recv], name=name, out_shape=jax.ShapeDtypeStruct(recv.shape, recv.dtype), grid=grid,
                 in_specs=[gw_spec, rv_spec], out_specs=rv_spec, prefetch=[base])


HBM_SPEC = pl.BlockSpec(memory_space=pltpu.HBM)
SEM_SPEC = pl.BlockSpec(memory_space=pltpu.SEMAPHORE)
SPLIT_PARAMS = pltpu.CompilerParams(has_side_effects=pltpu.SideEffectType.DATAFLOW_SIDE_EFFECTING)


def _split_copies(refs, plans, send_sems, recv_sems):
    pos = _position()
    out = []
    for k, (si, sv, li, lv, rel) in enumerate(plans):
        peer = _peer_position(pos, rel)
        mk = lambda d: pltpu.make_async_remote_copy(
            src_ref=sv(refs[si], pos), dst_ref=d, send_sem=send_sems.at[k], recv_sem=recv_sems.at[k],
            device_id=(peer["x"], peer["y"], peer["c"]), device_id_type=MESH)
        out.append((mk(lv(refs[li], pos)), mk(lv(refs[li], peer))))
    return out


def _split_start(arrays, plans, name):
    n = len(arrays)

    def body(*refs):
        send_sems, recv_sems = refs[n], refs[n + 1]
        for out_cp, _ in _split_copies(refs[:n], plans, send_sems, recv_sems):
            out_cp.start()
        refs[-1][...] = jnp.zeros_like(refs[-1])

    sems = pltpu.SemaphoreType.DMA((len(plans),))
    res = pl.pallas_call(
        body, name=name,
        out_shape=(sems, sems) + tuple(pltpu.HBM(a.shape, a.dtype) for a in arrays) + (jax.ShapeDtypeStruct((8, LANES), F32),),
        in_specs=[HBM_SPEC] * n, out_specs=(SEM_SPEC, SEM_SPEC) + (HBM_SPEC,) * n + (pl.BlockSpec(memory_space=pltpu.VMEM),),
        input_output_aliases={i: 2 + i for i in range(n)}, compiler_params=SPLIT_PARAMS,
    )(*[pltpu.with_memory_space_constraint(a, pltpu.HBM) for a in arrays])
    return res[0], res[1], list(res[2:2 + n]), res[-1]


def _split_wait(send_sems, recv_sems, arrays, plans, after, name):
    n, na = len(arrays), len(after)

    def body(*refs):
        for out_cp, in_cp in _split_copies(refs[:n], plans, refs[n], refs[n + 1]):
            out_cp.wait_send()
            in_cp.wait_recv()

    res = pl.pallas_call(
        body, name=name, out_shape=tuple(pltpu.HBM(a.shape, a.dtype) for a in arrays),
        in_specs=[HBM_SPEC] * n + [SEM_SPEC, SEM_SPEC] + [ANY] * na, out_specs=(HBM_SPEC,) * n,
        input_output_aliases={i: i for i in range(n)}, compiler_params=SPLIT_PARAMS,
    )(*arrays, send_sems, recv_sems, *after)
    return list(res)


def _rs_split_start(parts, name):
    nw = len(parts)
    lands = [lax.empty((3,) + p.shape[1:], p.dtype) for p in parts]
    plans = [(s, _slot_view(i), nw + s, _slot_view(i - 1), CHIP_RELS[i]) for s in range(nw) for i in (1, 2, 3)]
    send_sems, recv_sems, arrays, token = _split_start(list(parts) + lands, plans, name)
    return dict(sems=(send_sems, recv_sems), arrays=arrays, plans=plans, token=token, nw=nw)


def _rs_split_wait(h, after, name):
    arrays = _split_wait(h["sems"][0], h["sems"][1], h["arrays"], h["plans"], after, name)
    return arrays[:h["nw"]], arrays[h["nw"]:]


def _behind(xs, tokens):
    out = lax.optimization_barrier((tuple(xs), tuple(tokens)))
    return list(out[0])


def _ag_w_in(src, a, D, INW):
    wm = LANES * a

    hd = D // 2
    ALL, TOP, BOT = (0, D), (0, hd), (hd, D)

    def main_place(ref, p, rows=ALL):
        off = pl.multiple_of(((2 * a + 1) * (p["me"] // 2) + (a + 1) * p["c"]) * LANES, LANES)
        return ref.at[pl.ds(rows[0], rows[1] - rows[0]), pl.ds(off, wm)]

    def main_src(ref, p):
        return ref.at[:, pl.ds(pl.multiple_of(p["c"] * LANES, LANES), wm)]

    def mid_src(ref, p):
        return ref.at[:, pl.ds(pl.multiple_of((1 - p["c"]) * wm, LANES), LANES)]

    def mid_place(ref, p, rows=ALL):
        return ref.at[p["me"], pl.ds(rows[0], rows[1] - rows[0]), :]

    def body(src_ref, full_ref, mid_ref, send_sems, recv_sems):
        pos = _position()
        sib, xn, yn = (_peer_position(pos, r) for r in (1, 4, 2))
        dg = _peer_position(pos, 6)
        started = []

        def remote(k, s, d, to):
            return pltpu.make_async_remote_copy(src_ref=s, dst_ref=d, send_sem=send_sems.at[k], recv_sem=recv_sems.at[k],
                                                device_id=(to["x"], to["y"], to["c"]), device_id_type=MESH)

        def send(k, owner, rows, to, from_src=False):
            for j, (src_v, place) in enumerate(((main_src, main_place), (mid_src, mid_place))):
                s = src_v(src_ref, pos) if from_src else place(full_ref if j == 0 else mid_ref, owner, rows)
                cp = remote(k + j, s, place(full_ref if j == 0 else mid_ref, owner, rows), to)
                cp.start()
                started.append(cp)

        def landed(k, owner, rows, frm):
            for j, place in enumerate((main_place, mid_place)):
                ref = full_ref if j == 0 else mid_ref
                remote(k + j, place(ref, owner, rows), place(ref, owner, rows), frm).wait_recv()

        local = [pltpu.make_async_copy(main_src(src_ref, pos), main_place(full_ref, pos), send_sems.at[18]),
                 pltpu.make_async_copy(mid_src(src_ref, pos), mid_place(mid_ref, pos), send_sems.at[19])]
        for cp in local:
            cp.start()
        send(0, pos, ALL, sib, from_src=True)
        send(2, pos, ALL, xn, from_src=True)
        send(4, pos, ALL, yn, from_src=True)
        landed(2, xn, ALL, xn)
        send(10, xn, ALL, sib)
        send(6, xn, TOP, yn)
        landed(4, yn, ALL, yn)
        send(12, yn, ALL, sib)
        send(8, yn, BOT, xn)
        landed(6, dg, TOP, yn)
        send(14, dg, TOP, sib)
        landed(8, dg, BOT, xn)
        send(16, dg, BOT, sib)
        sib_of = lambda p: _peer_position(p, 1)
        landed(0, sib, ALL, sib)
        landed(10, sib_of(xn), ALL, sib)
        landed(12, sib_of(yn), ALL, sib)
        landed(14, sib_of(dg), TOP, sib)
        landed(16, sib_of(dg), BOT, sib)
        for cp in started:
            cp.wait_send()
        for cp in local:
            cp.wait()

    return _call(body, [src], name="ag_w_in", in_specs=[ANY], out_specs=[ANY, ANY],
                 out_shape=(jax.ShapeDtypeStruct((D, INW), BF16), jax.ShapeDtypeStruct((N_DEV, D, LANES), BF16)),
                 scratch_shapes=[pltpu.SemaphoreType.DMA((20,)), pltpu.SemaphoreType.DMA((20,))])


def _patch_mid(full, mid, a):
    D = full.shape[0]

    def body(full_ref, e_ref, o_ref, out_ref):
        out_ref[...] = e_ref[...] + o_ref[...]

    return _call(body, [full, mid, mid], name="patch_mid", grid=(N_DEV // 2,),
                 out_shape=jax.ShapeDtypeStruct(full.shape, full.dtype),
                 in_specs=[ANY, pl.BlockSpec((None, D, LANES), lambda j: (2 * j, 0, 0)),
                           pl.BlockSpec((None, D, LANES), lambda j: (2 * j + 1, 0, 0))],
                 out_specs=pl.BlockSpec((D, LANES), lambda j: (0, (2 * a + 1) * j + a)), aliases={0: 0})


MM_RESIDENT = 2048


def _mm(a, b, mode, out_dtype, name, b_off=0, n=None, comm=None, extras=(), epi=None, tn=None, after=()):
    if mode == "nn":
        (M, K), (K2, N) = a.shape, b.shape
    elif mode == "nt":
        (M, K), (N, K2) = a.shape, b.shape
    else:
        (K, M), (K2, N) = a.shape, b.shape
    assert K == K2, (a.shape, b.shape, mode)
    if n is not None:
        N = n
    single = not isinstance(out_dtype, (tuple, list))
    out_dtypes = (out_dtype,) if single else tuple(out_dtype)
    if epi is None:
        epi = lambda r: (r,)
    tk = K if K <= MM_RESIDENT else (MM_RESIDENT if K % MM_RESIDENT == 0 else _tile(K, 512, LANES))
    nk = K // tk
    if M > MM_RESIDENT and mode == "tn" and N <= MM_RESIDENT and not b_off:
        tm, tn = _tile(M, 512, LANES), N
    elif nk > 1:
        tm, tn = _tile(M, 1024, LANES), _tile(N, tn or 1024, LANES)
    else:
        tm = _tile(M, MM_RESIDENT, LANES)
        tn = _tile(math.gcd(N, b_off) if b_off else N, tn or 512, LANES)
    jb = b_off // tn
    dn = {"nn": NN, "nt": NT, "tn": TN}[mode]
    ne, no = len(extras), len(out_dtypes)

    def body(a_ref, b_ref, *rest):
        e_refs, o_refs = rest[:ne], rest[ne:ne + no]

        def finish(r):
            for o_ref, v in zip(o_refs, epi(r, *[e[...] for e in e_refs])):
                o_ref[...] = v.astype(o_ref.dtype)

        if nk == 1:
            finish(_bdot(a_ref[...], b_ref[...], dn))
            return
        acc_ref = rest[ne + no]
        k = pl.program_id(2)

        @pl.when(k == 0)
        def _():
            acc_ref[...] = _bdot(a_ref[...], b_ref[...], dn)

        @pl.when(jnp.logical_and(k > 0, k < nk - 1))
        def _():
            acc_ref[...] += _bdot(a_ref[...], b_ref[...], dn)

        @pl.when(k == nk - 1)
        def _():
            finish(acc_ref[...] + _bdot(a_ref[...], b_ref[...], dn))

    a_spec = pl.BlockSpec((tk, tm), lambda i, j, k: (k, i)) if mode == "tn" else pl.BlockSpec((tm, tk), lambda i, j, k: (i, k))
    b_spec = pl.BlockSpec((tn, tk), lambda i, j, k: (j, k)) if mode == "nt" else pl.BlockSpec((tk, tn), lambda i, j, k: (k, j + jb))
    o_spec = pl.BlockSpec((tm, tn), lambda i, j, k: (i, j))
    res = _call(body, [a, b] + list(extras), name=name, grid=(M // tm, N // tn, nk),
                out_shape=tuple(jax.ShapeDtypeStruct((M, N), dt) for dt in out_dtypes),
                in_specs=[a_spec, b_spec] + [o_spec] * ne, out_specs=[o_spec] * no,
                scratch_shapes=[pltpu.VMEM((tm, tn), F32)] if nk > 1 else [], comm=comm, after=after)
    return res[0] if single else res


def _rowwise(fn, row_ins, bcast_ins, row_outs, acc_outs, name, rt=256, comm=None):
    L = row_ins[0][0].shape[-2]
    rt = _tile(L, rt, 16)
    nr, nb, no = len(row_ins), len(bcast_ins), len(row_outs)

    def body(*refs):
        i = pl.program_id(0)
        vals = [r[...] for r in refs[:nr + nb]]
        outs, accs = fn(*vals)
        for r, v in zip(refs[nr + nb:nr + nb + no], outs):
            r[...] = v.astype(r.dtype)
        acc_refs = refs[nr + nb + no:]

        @pl.when(i == 0)
        def _():
            for r in acc_refs:
                r[...] = jnp.zeros_like(r)

        for r, v in zip(acc_refs, accs):
            r[...] += v

    in_specs = []
    for spec in row_ins:
        w, cb = spec[1], spec[2]
        if len(spec) == 4:
            in_specs.append(pl.BlockSpec((None, rt, w), functools.partial(lambda i, cb, ld: (ld, i, cb), cb=cb, ld=spec[3])))
        else:
            in_specs.append(pl.BlockSpec((rt, w), functools.partial(lambda i, cb: (i, cb), cb=cb)))
    in_specs += [pl.BlockSpec(b.shape, lambda i: (0, 0)) for b in bcast_ins]
    out_specs = [pl.BlockSpec((rt, w), lambda i: (i, 0)) for w, _ in row_outs]
    out_specs += [pl.BlockSpec(s, lambda i: (0, 0)) for s in acc_outs]
    out_shape = [jax.ShapeDtypeStruct((L, w), dt) for w, dt in row_outs] + [jax.ShapeDtypeStruct(s, F32) for s in acc_outs]
    return _call(body, [s[0] for s in row_ins] + list(bcast_ins), name=name, grid=(L // rt,), out_shape=tuple(out_shape),
                 in_specs=in_specs, out_specs=out_specs, comm=comm)


def _whole(fn, ins, out_shapes, name):
    def body(*refs):
        outs = fn(*[r[...] for r in refs[:len(ins)]])
        for r, v in zip(refs[len(ins):], outs):
            r[...] = v.astype(r.dtype)

    return _call(body, list(ins), name=name, out_shape=tuple(jax.ShapeDtypeStruct(s, dt) for s, dt in out_shapes))


def _silu(x):
    return x * jax.nn.sigmoid(x)


def _rms(x, g):
    return (x * lax.rsqrt(jnp.mean(x * x, axis=-1, keepdims=True) + EPS)) * g


def _modnorm(x, g, shift, scale):
    return _rms(x, g) * (1.0 + scale) + shift


def _adamw(w, g, m, v):
    m = ADAM_B1 * m + (1.0 - ADAM_B1) * g
    v = ADAM_B2 * v + (1.0 - ADAM_B2) * jnp.square(g)
    m_hat = m / (1.0 - ADAM_B1 ** ADAM_STEP)
    v_hat = v / (1.0 - ADAM_B2 ** ADAM_STEP)
    delta = -ADAM_LR * (m_hat / (jnp.sqrt(v_hat) + ADAM_EPS) + ADAM_WD * w)
    return delta, m, v


def _lower_bound(lg):
    e = jnp.exp(lg - jnp.max(lg, axis=0, keepdims=True))
    return e[0:1] / jnp.sum(e, axis=0, keepdims=True)


def _hg_chunk(hq, hf, hi, lb, st):
    C = hq.shape[0]
    row = lax.broadcasted_iota(jnp.int32, (C, C), 0)
    col = lax.broadcasted_iota(jnp.int32, (C, C), 1)
    tri = row >= col
    sg = jax.nn.sigmoid(hf)
    f = lb + (1.0 - lb) * sg
    lf = jnp.log(f)
    k = 1.0 - f
    q = _silu(hq)
    b = _dot(tri.astype(F32), lf, NN, precision=HIGHEST)
    m = b[C // 2 - 1:C // 2]
    bl = b[C - 1:C]
    e_qm, e_km, e_kl, e_q = jnp.exp(b - m), jnp.exp(m - b), jnp.exp(bl - b), jnp.exp(b)
    qe, ke, kd, qb = q * e_qm, k * e_km, k * e_kl, q * e_q
    sc = jnp.where(tri, _bdot(qe, ke, NT), 0.0)
    o = _bdot(sc, hi, NN) + _bdot(qb, st, NT)
    dec = jnp.exp(bl)
    st_next = st * dec + _bdot(hi, kd, TN)
    return o, st_next, dict(tri=tri, sg=sg, f=f, k=k, q=q, qe=qe, ke=ke, kd=kd, qb=qb, sc=sc, dec=dec,
                            e_qm=e_qm, e_km=e_km, e_kl=e_kl, e_q=e_q)


def _hg_out(o, hgate, gout):
    return _rms(o, gout) * _silu(hgate)


HG_GROUP = 16


def _hgrn_fwd(p4, lb_logits, gout, H, comm=None):
    L = p4.shape[0]
    C = HG_CHUNK
    GR = _tile(L // C, HG_GROUP, 1)
    T = GR * C
    N = L // T

    def body(hq_ref, hf_ref, hi_ref, hg_ref, lg_ref, gout_ref, o_ref, s_ref, st_ref):
        @pl.when(pl.program_id(1) == 0)
        def _():
            st_ref[...] = jnp.zeros_like(st_ref)

        lb = _lower_bound(lg_ref[...])
        st = st_ref[...]
        for ci in range(GR):
            rows = pl.ds(ci * C, C)
            s_ref[0, ci] = st
            o, st, _ = _hg_chunk(hq_ref[rows, :], hf_ref[rows, :], hi_ref[rows, :], lb, st)
            o_ref[rows, :] = _hg_out(o, hg_ref[rows, :], gout_ref[...]).astype(o_ref.dtype)
        st_ref[...] = st

    blk = lambda s: pl.BlockSpec((T, HG_DK), functools.partial(lambda h, n, s: (n, s * H + h), s=s))
    return _call(
        body, [p4, p4, p4, p4, lb_logits, gout], name="hgrn_fwd", grid=(H, N),
        out_shape=(jax.ShapeDtypeStruct((L, H * HG_DK), BF16), jax.ShapeDtypeStruct((H, N * GR, HG_DK, HG_DK), F32)),
        in_specs=[blk(0), blk(1), blk(2), blk(3), pl.BlockSpec((2, HG_DK), lambda h, n: (0, h)),
                  pl.BlockSpec((1, HG_DK), lambda h, n: (0, 0))],
        out_specs=(pl.BlockSpec((T, HG_DK), lambda h, n: (n, h)),
                   pl.BlockSpec((1, GR, HG_DK, HG_DK), lambda h, n: (h, n, 0, 0))),
        scratch_shapes=[pltpu.VMEM((HG_DK, HG_DK), F32)], comm=comm)


def _hgrn_bwd(p4, lb_logits, gout, s_all, d_out, H, comm=None):
    L = p4.shape[0]
    C = HG_CHUNK
    GR = _tile(L // C, HG_GROUP, 1)
    T = GR * C
    N = L // T

    def body(hq_ref, hf_ref, hi_ref, hg_ref, lg_ref, gout_ref, s_ref, do_ref,
             dq_ref, df_ref, di_ref, dg_ref, dlb_ref, dgo_ref, dst_ref):
        @pl.when(pl.program_id(1) == 0)
        def _():
            dst_ref[...] = jnp.zeros_like(dst_ref)
            dlb_ref[...] = jnp.zeros_like(dlb_ref)

        @pl.when(jnp.logical_and(pl.program_id(0) == 0, pl.program_id(1) == 0))
        def _():
            dgo_ref[...] = jnp.zeros_like(dgo_ref)

        lb = _lower_bound(lg_ref[...])
        dst = dst_ref[...]
        d_lb = jnp.zeros((1, HG_DK), F32)
        d_go = jnp.zeros((1, HG_DK), F32)
        for ci in reversed(range(GR)):
            rows = pl.ds(ci * C, C)
            dst, d_lb_c, d_go_c = chunk_bwd(rows, lb, s_ref[0, ci], dst, hq_ref, hf_ref, hi_ref, hg_ref, gout_ref, do_ref,
                                            dq_ref, df_ref, di_ref, dg_ref)
            d_lb += d_lb_c
            d_go += d_go_c
        dst_ref[...] = dst
        dlb_ref[...] += d_lb
        dgo_ref[...] += d_go

    def chunk_bwd(rows, lb, st, dst_next, hq_ref, hf_ref, hi_ref, hg_ref, gout_ref, do_ref, dq_ref, df_ref, di_ref, dg_ref):
        hq, hf, hi, hgate = hq_ref[rows, :], hf_ref[rows, :], hi_ref[rows, :], hg_ref[rows, :]
        o, _, t = _hg_chunk(hq, hf, hi, lb, st)
        _, out_vjp = jax.vjp(_hg_out, o, hgate, gout_ref[...])
        do, d_hgate, d_gout = out_vjp(do_ref[rows, :])
        tri = t["tri"]
        dsc = jnp.where(tri, _bdot(do, hi, NT), 0.0)
        dv = _bdot(t["sc"], do, TN) + _bdot(t["kd"], dst_next, NT)
        dqe = _bdot(dsc, t["ke"], NN)
        dke = _bdot(dsc, t["qe"], TN)
        dqb = _bdot(do, st, NN)
        dkd = _bdot(hi, dst_next, NN)
        ddec = jnp.sum(dst_next * st, axis=0, keepdims=True)
        dst_prev = _bdot(do, t["qb"], TN) + dst_next * t["dec"]
        dq = dqe * t["e_qm"] + dqb * t["e_q"]
        dk = dke * t["e_km"] + dkd * t["e_kl"]
        tq, tk, td, tb = dqe * t["qe"], dke * t["ke"], dkd * t["kd"], dqb * t["qb"]
        db = tq - tk - td + tb
        dm = jnp.sum(tk - tq, axis=0, keepdims=True)
        dbl = jnp.sum(td, axis=0, keepdims=True) + ddec * t["dec"]
        rowi = lax.broadcasted_iota(jnp.int32, (C, HG_DK), 0)
        db = db + jnp.where(rowi == C // 2 - 1, dm, 0.0) + jnp.where(rowi == C - 1, dbl, 0.0)
        dlf = _dot(tri.astype(F32), db, TN, precision=HIGHEST)
        df = dlf / t["f"] - dk
        sg = t["sg"]
        df_ref[rows, :] = (df * (1.0 - lb) * sg * (1.0 - sg)).astype(df_ref.dtype)
        sq = jax.nn.sigmoid(hq)
        dq_ref[rows, :] = (dq * (sq * (1.0 + hq * (1.0 - sq)))).astype(dq_ref.dtype)
        di_ref[rows, :] = dv.astype(di_ref.dtype)
        dg_ref[rows, :] = d_hgate.astype(dg_ref.dtype)
        return dst_prev, jnp.sum(df * (1.0 - sg), axis=0, keepdims=True), d_gout

    blk = lambda s: pl.BlockSpec((T, HG_DK), functools.partial(lambda h, n, s: (N - 1 - n, s * H + h), s=s))
    oblk = pl.BlockSpec((T, HG_DK), lambda h, n: (N - 1 - n, h))
    vec = pl.BlockSpec((1, HG_DK), lambda h, n: (0, h))
    W = H * HG_DK
    return _call(
        body, [p4, p4, p4, p4, lb_logits, gout, s_all, d_out], name="hgrn_bwd", grid=(H, N),
        out_shape=tuple([jax.ShapeDtypeStruct((L, W), BF16)] * 4 + [jax.ShapeDtypeStruct((1, W), F32), jax.ShapeDtypeStruct((1, HG_DK), F32)]),
        in_specs=[blk(0), blk(1), blk(2), blk(3), pl.BlockSpec((2, HG_DK), lambda h, n: (0, h)),
                  pl.BlockSpec((1, HG_DK), lambda h, n: (0, 0)),
                  pl.BlockSpec((1, GR, HG_DK, HG_DK), lambda h, n: (h, N - 1 - n, 0, 0)), oblk],
        out_specs=(oblk, oblk, oblk, oblk, vec, pl.BlockSpec((1, HG_DK), lambda h, n: (0, 0))),
        scratch_shapes=[pltpu.VMEM((HG_DK, HG_DK), F32)], comm=comm)


def _bucket_ids():
    i = jnp.arange(AT_BLOCK, dtype=jnp.int32)[:, None]
    j = jnp.arange(2 * AT_BLOCK, dtype=jnp.int32)[None, :]
    n = jnp.maximum(i - j + AT_BLOCK, 0)
    nf = jnp.maximum(n, 1).astype(F32)
    large = MAX_EXACT + (jnp.log(nf / MAX_EXACT) / math.log(MAX_DISTANCE / MAX_EXACT) * (N_BUCKETS - MAX_EXACT)).astype(jnp.int32)
    large = jnp.minimum(large, N_BUCKETS - 1)
    return jnp.where(n < MAX_EXACT, n, large).reshape(1, -1)


def _onehot(bucket):
    ids = lax.broadcasted_iota(jnp.int32, (N_BUCKETS, bucket.shape[1]), 0)
    return (ids == bucket).astype(F32)


def _attn_probs(qn, kpn, kcn, bias_g, sink, first, scale):
    rows = qn.shape[0]
    i = jnp.bitwise_and(lax.broadcasted_iota(jnp.int32, (rows, AT_BLOCK), 0), AT_BLOCK - 1)
    j = lax.broadcasted_iota(jnp.int32, (rows, AT_BLOCK), 1)
    lp = _bdot(qn, kpn, NT) * scale + bias_g[:, :AT_BLOCK]
    lc = _bdot(qn, kcn, NT) * scale + bias_g[:, AT_BLOCK:]
    lp = jnp.where(jnp.logical_and(j > i, jnp.logical_not(first)), lp, NEG_INF)
    lc = jnp.where(j <= i, lc, NEG_INF)
    m = jnp.maximum(jnp.maximum(jnp.max(lp, axis=-1, keepdims=True), jnp.max(lc, axis=-1, keepdims=True)), sink)
    pp, pc, ps = jnp.exp(lp - m), jnp.exp(lc - m), jnp.exp(sink - m)
    den = jnp.sum(pp, axis=-1, keepdims=True) + jnp.sum(pc, axis=-1, keepdims=True) + ps
    return pp / den, pc / den, ps / den


def _sink_rows(sk_ref, G):
    head = lax.broadcasted_iota(jnp.int32, (G * AT_BLOCK, 1), 0) // AT_BLOCK
    sink = jnp.zeros((G * AT_BLOCK, 1), F32)
    for g in range(G):
        sink = jnp.where(head == g, sk_ref[0, g:g + 1, :], sink)
    return sink


def _attn_fwd(q_t, kp, vp, qg, kg, sinks, bias, KVH, comm=None):
    AH, L, DH = q_t.shape
    G = AH // KVH
    NB = L // AT_BLOCK
    scale = DH ** -0.5

    def body(q_ref, kp_ref, kc_ref, vp_ref, vc_ref, qg_ref, kg_ref, sk_ref, b_ref, o_ref):
        first = pl.program_id(1) == 0
        kpn, kcn = _rms(kp_ref[0], kg_ref[...]), _rms(kc_ref[0], kg_ref[...])
        qn = _rms(q_ref[...].reshape(G * AT_BLOCK, DH), qg_ref[...])
        sink = _sink_rows(sk_ref, G)
        pp, pc, _ = _attn_probs(qn, kpn, kcn, b_ref[...].reshape(G * AT_BLOCK, 2 * AT_BLOCK), sink, first, scale)
        o = _bdot(pp, vp_ref[0], NN) + _bdot(pc, vc_ref[0], NN)
        o_ref[...] = o.reshape(G, AT_BLOCK, DH).astype(o_ref.dtype)

    kblk = lambda off: pl.BlockSpec((1, AT_BLOCK, DH), functools.partial(lambda h, n, off: (h, n + off, 0), off=off))
    return _call(
        body, [q_t, kp, kp, vp, vp, qg, kg, sinks, bias], name="attn_fwd", grid=(KVH, NB),
        out_shape=jax.ShapeDtypeStruct((AH, L, DH), BF16),
        in_specs=[pl.BlockSpec((G, AT_BLOCK, DH), lambda h, n: (h, n, 0)), kblk(0), kblk(1), kblk(0), kblk(1),
                  pl.BlockSpec((1, DH), lambda h, n: (0, 0)), pl.BlockSpec((1, DH), lambda h, n: (0, 0)),
                  pl.BlockSpec((1, G, 1), lambda h, n: (h, 0, 0)),
                  pl.BlockSpec((G, AT_BLOCK, 2 * AT_BLOCK), lambda h, n: (h, 0, 0))],
        out_specs=pl.BlockSpec((G, AT_BLOCK, DH), lambda h, n: (h, n, 0)), comm=comm)


def _attn_bwd(q_t, kp, vp, qg, kg, sinks, bias, do_t, KVH, comm=None):
    AH, L, DH = q_t.shape
    G = AH // KVH
    NB = L // AT_BLOCK
    B = AT_BLOCK
    scale = DH ** -0.5

    def body(q_ref, kp_ref, kc_ref, vp_ref, vc_ref, qg_ref, kg_ref, sk_ref, b_ref, do_ref,
             dq_ref, dk_ref, dv_ref, dqg_ref, dkg_ref, dsk_ref, db_ref):
        n = pl.program_id(1)
        first = n == 0

        @pl.when(first)
        def _():
            for r in (dk_ref, dv_ref, dsk_ref, db_ref):
                r[...] = jnp.zeros_like(r)

        @pl.when(jnp.logical_and(first, pl.program_id(0) == 0))
        def _():
            dqg_ref[...] = jnp.zeros_like(dqg_ref)
            dkg_ref[...] = jnp.zeros_like(dkg_ref)

        kp_raw, kc_raw, kgv, qgv = kp_ref[0], kc_ref[0], kg_ref[...], qg_ref[...]
        kpn, kp_vjp = jax.vjp(_rms, kp_raw, kgv)
        kcn, kc_vjp = jax.vjp(_rms, kc_raw, kgv)
        qn, q_vjp = jax.vjp(_rms, q_ref[...].reshape(G * B, DH), qgv)
        pp, pc, ps = _attn_probs(qn, kpn, kcn, b_ref[...].reshape(G * B, 2 * B), _sink_rows(sk_ref, G), first, scale)
        do = do_ref[...].reshape(G * B, DH)
        dvp = _bdot(pp, do, TN)
        dvc = _bdot(pc, do, TN)
        dpp = _bdot(do, vp_ref[0], NT)
        dpc = _bdot(do, vc_ref[0], NT)
        dsum = jnp.sum(dpp * pp, axis=-1, keepdims=True) + jnp.sum(dpc * pc, axis=-1, keepdims=True)
        dlp = pp * (dpp - dsum)
        dlc = pc * (dpc - dsum)
        dsk_ref[0] += jnp.sum((-ps * dsum).reshape(G, B, 1), axis=1)
        db_ref[:, :, :B] += dlp.reshape(G, B, B)
        db_ref[:, :, B:] += dlc.reshape(G, B, B)
        dlp, dlc = dlp * scale, dlc * scale
        dqn = _bdot(dlp, kpn, NN) + _bdot(dlc, kcn, NN)
        dq_raw, dqg = q_vjp(dqn)
        dq_ref[...] = dq_raw.reshape(G, B, DH).astype(dq_ref.dtype)
        dkp_raw, dkg_p = kp_vjp(_bdot(dlp, qn, TN))
        dkc_raw, dkg_c = kc_vjp(_bdot(dlc, qn, TN))
        r0 = pl.multiple_of(n * B, B)
        r1 = pl.multiple_of(n * B + B, B)
        dk_ref[0, pl.ds(r0, B), :] += dkp_raw
        dk_ref[0, pl.ds(r1, B), :] += dkc_raw
        dv_ref[0, pl.ds(r0, B), :] += dvp
        dv_ref[0, pl.ds(r1, B), :] += dvc
        dqg_ref[...] += dqg
        dkg_ref[...] += dkg_p + dkg_c

    kblk = lambda off: pl.BlockSpec((1, B, DH), functools.partial(lambda h, n, off: (h, n + off, 0), off=off))
    qblk = pl.BlockSpec((G, B, DH), lambda h, n: (h, n, 0))
    accblk = pl.BlockSpec((1, L + B, DH), lambda h, n: (h, 0, 0))
    vecblk = pl.BlockSpec((1, DH), lambda h, n: (0, 0))
    return _call(
        body, [q_t, kp, kp, vp, vp, qg, kg, sinks, bias, do_t], name="attn_bwd", grid=(KVH, NB),
        out_shape=(jax.ShapeDtypeStruct((AH, L, DH), BF16), jax.ShapeDtypeStruct((KVH, L + B, DH), F32),
                   jax.ShapeDtypeStruct((KVH, L + B, DH), F32), jax.ShapeDtypeStruct((1, DH), F32),
                   jax.ShapeDtypeStruct((1, DH), F32), jax.ShapeDtypeStruct((KVH, G, 1), F32),
                   jax.ShapeDtypeStruct((AH, B, 2 * B), F32)),
        in_specs=[qblk, kblk(0), kblk(1), kblk(0), kblk(1),
                  pl.BlockSpec((1, DH), lambda h, n: (0, 0)), pl.BlockSpec((1, DH), lambda h, n: (0, 0)),
                  pl.BlockSpec((1, G, 1), lambda h, n: (h, 0, 0)),
                  pl.BlockSpec((G, B, 2 * B), lambda h, n: (h, 0, 0)), qblk],
        out_specs=(qblk, accblk, accblk, vecblk, vecblk, pl.BlockSpec((1, G, 1), lambda h, n: (h, 0, 0)),
                   pl.BlockSpec((G, B, 2 * B), lambda h, n: (h, 0, 0))), comm=comm)


def _heads_first(t, nh):
    L = t.shape[0]
    return jnp.transpose(t.reshape(L, nh, t.shape[1] // nh), (1, 0, 2))


def _heads_last(t):
    nh, L, dh = t.shape
    return jnp.transpose(t, (1, 0, 2)).reshape(L, nh * dh)


def _softmax0(lg):
    e = jnp.exp(lg - jnp.max(lg, axis=0, keepdims=True))
    return e[0:1] / jnp.sum(e, axis=0, keepdims=True)


def _ada_update_call(fn, c_all, d_cols, w, m, v, rt):
    D, n = w.shape

    def body(c_ref, d_ref, w_ref, m_ref, v_ref, g_out, dl_out, m_out, v_out):
        outs, _ = fn(c_ref[...], d_ref[...], w_ref[...], m_ref[...], v_ref[...])
        for r, val in zip((g_out, dl_out, m_out, v_out), outs):
            r[...] = val

    wblk = pl.BlockSpec((rt, n), lambda i: (i, 0))
    return _call(
        body, [c_all, d_cols, w, m, v], name="update_ada", grid=(D // rt,), out_shape=tuple([jax.ShapeDtypeStruct((D, n), F32)] * 4),
        in_specs=[pl.BlockSpec((N_DEV, rt), lambda i: (0, i)), pl.BlockSpec((N_DEV, n), lambda i: (0, 0)), wblk, wblk, wblk],
        out_specs=(wblk, wblk, wblk, wblk))


def kernel(x, c, w_ada, b_ada, norm1_g, norm2_g, w_in, hg_lb_logits, hg_out_norm_g, q_norm_g, k_norm_g, attn_sinks, rel_bias_table, w_branch_hg, w_branch_attn, w_out, w_ff1, w_ff2, loss_target, m_w_ada, m_b_ada, m_norm1_g, m_norm2_g, m_w_in, m_hg_lb_logits, m_hg_out_norm_g, m_q_norm_g, m_k_norm_g, m_attn_sinks, m_rel_bias_table, m_w_branch_hg, m_w_branch_attn, m_w_out, m_w_ff1, m_w_ff2, v_w_ada, v_b_ada, v_norm1_g, v_norm2_g, v_w_in, v_hg_lb_logits, v_hg_out_norm_g, v_q_norm_g, v_k_norm_g, v_attn_sinks, v_rel_bias_table, v_w_branch_hg, v_w_branch_attn, v_w_out, v_w_ff1, v_w_ff2):
    cc = lax.axis_index("c")
    me = 4 * lax.axis_index("x") + 2 * lax.axis_index("y") + cc
    x2 = x[0]
    tgt = loss_target[0]
    L, D = x2.shape
    HGW = hg_lb_logits.shape[1]
    H = HGW // HG_DK
    AH = attn_sinks.shape[1]
    DH = q_norm_g.shape[1]
    ATW = AH * DH
    BW = w_in.shape[2]
    INW = BW * N_DEV
    A = BW // LANES
    assert BW == LANES * A + LANES // 2
    KVW = (INW - 4 * HGW - ATW - 2 * D) // 2
    KVH = KVW // DH
    G = AH // KVH
    ADA_N = w_ada.shape[2]
    PAIR = 2 * A + 1

    c_all = _gather_small(c, me, "gather_c")[:, 0, :]
    b_cols = lax.dynamic_slice(b_ada, (0, me * ADA_N), (1, ADA_N))
    (ada_cols,) = _whole(lambda cv, w, b: (_bdot(_silu(cv), w, NN) + b,), [c_all, w_ada[0], b_cols],
                         [((N_DEV, ADA_N), F32)], "ada_fwd")
    ada_all = _gather_small(ada_cols, me, "gather_ada")
    ada_row = lax.dynamic_slice(ada_all, (0, me, 0), (N_DEV, 1, ADA_N)).reshape(1, 6 * D)

    w_in_b = w_in[0].astype(BF16)
    src_in = jnp.where(cc == 0, jnp.pad(w_in_b, ((0, 0), (0, LANES // 2))), jnp.pad(w_in_b, ((0, 0), (LANES // 2, 0))))
    (src_in,) = _behind([src_in], [ada_row])
    shift1, scale1, gate1, shift2, scale2, gate2 = [ada_row[:, i * D:(i + 1) * D] for i in range(6)]
    w_in_gapped, w_in_mid = _ag_w_in(src_in, A, D, INW)
    w_in_full = _patch_mid(w_in_gapped, w_in_mid, A)

    wnames = ("bhg", "bat", "out", "ff1", "ff2")
    small = ("bhg", "bat", "out")
    waxis = dict(zip(wnames, (1, 1, 0, 1, 0)))
    wsrc = dict(zip(wnames, (w_branch_hg, w_branch_attn, w_out, w_ff1, w_ff2)))
    wblk = {k: wsrc[k][0].astype(BF16) for k in wnames}
    wf = {}

    (h,) = _rowwise(lambda xv, g, sh, sc: ((_modnorm(xv, g, sh, sc),), ()), [(x2, D, 0)], [norm1_g, shift1, scale1],
                    [(D, BF16)], [], "norm1")
    o4, oa = 4 * HGW, 4 * HGW + ATW + 2 * KVW
    r1, r2, ro = wblk["ff1"].shape[0], wblk["ff2"].shape[0], wblk["out"].shape[0]
    cm = _Comm()
    hs = {k: _ag_ici(cm, wblk[k], waxis[k]) for k in ("bhg", "bat")}
    hs["out"] = _ag_ici(cm, wblk["out"], waxis["out"], rows=(0, ro // 2))
    p4 = _mm(h, w_in_full, "nn", F32, "proj_hg", n=o4, comm=cm)
    half = {k: cm.result(hs[k]) for k in hs}
    cm = _Comm()
    hs = {"out": _ag_ici(cm, wblk["out"], waxis["out"], rows=(ro // 2, ro), into=half["out"])}
    pa = _mm(h, w_in_full, "nn", F32, "proj_at", b_off=o4, n=oa - o4, comm=cm)
    half["out"] = cm.result(hs["out"])
    cm = _Comm()
    hs = {k: _ag_d2d(cm, half[k], waxis[k]) for k in ("bhg", "bat")}
    hs["ff2"] = _ag_ici(cm, wblk["ff2"], waxis["ff2"], rows=(0, r2 // 4))
    pg = _mm(h, w_in_full, "nn", F32, "proj_gate", b_off=oa, n=INW - oa, comm=cm)
    wf["bhg"], wf["bat"], half["ff2"] = (cm.result(hs[k]) for k in ("bhg", "bat", "ff2"))

    cm = _Comm()
    hs = {"out": _ag_d2d(cm, half["out"], waxis["out"]), "ff1": _ag_ici(cm, wblk["ff1"], waxis["ff1"], rows=(0, r1 // 2))}
    o_hg, s_all = _hgrn_fwd(p4, hg_lb_logits, hg_out_norm_g, H, comm=cm)
    wf["out"], half["ff1"] = cm.result(hs["out"]), cm.result(hs["ff1"])

    bucket = _bucket_ids()
    (bias_flat,) = _whole(lambda tb, bk: (_dot(tb, _onehot(bk), TN, precision=HIGHEST),), [rel_bias_table, bucket],
                          [((AH, AT_BLOCK * 2 * AT_BLOCK), F32)], "bias_fwd")
    bias = bias_flat.reshape(AH, AT_BLOCK, 2 * AT_BLOCK)
    q_t = _heads_first(pa[:, :ATW], AH)
    pad = lambda t: jnp.pad(t, ((0, 0), (AT_BLOCK, 0), (0, 0)))
    kp = pad(_heads_first(pa[:, ATW:ATW + KVW], KVH))
    vp = pad(_heads_first(pa[:, ATW + KVW:], KVH))
    sinks3 = attn_sinks.reshape(KVH, G, 1)
    cm = _Comm()
    hs = {"ff1": _ag_ici(cm, wblk["ff1"], waxis["ff1"], rows=(r1 // 2, r1), into=half["ff1"])}
    o_at = _heads_last(_attn_fwd(q_t, kp, vp, q_norm_g, k_norm_g, sinks3, bias, KVH, comm=cm))
    half["ff1"] = cm.result(hs["ff1"])

    bh = _mm(o_hg, wf["bhg"], "nn", F32, "branch_hg")
    ba = _mm(o_at, wf["bat"], "nn", F32, "branch_at")

    def merge_fn(bhv, bav, ghg, gat):
        return jax.nn.sigmoid(ghg) * bhv + jax.nn.sigmoid(gat) * bav

    cm = _Comm()
    hs = {"ff1": _ag_d2d(cm, half["ff1"], waxis["ff1"])}
    (merged,) = _rowwise(lambda *a: ((merge_fn(*a),), ()), [(bh, D, 0), (ba, D, 0), (pg, D, 0), (pg, D, 1)], [],
                         [(D, BF16)], [], "merge", comm=cm)
    wf["ff1"] = cm.result(hs["ff1"])
    cm = _Comm()
    hs = {"ff2": _ag_ici(cm, wblk["ff2"], waxis["ff2"], rows=(r2 // 4, 3 * r2 // 8), into=half["ff2"])}
    mo = _mm(merged, wf["out"], "nn", F32, "out_proj", comm=cm)
    half["ff2"] = cm.result(hs["ff2"])

    def resid1(xv, mov, g1, g2n, sh, sc):
        x1v = xv + g1 * mov
        return (x1v, _modnorm(x1v, g2n, sh, sc)), ()

    cm = _Comm()
    hs = {"ff2": _ag_ici(cm, wblk["ff2"], waxis["ff2"], rows=(3 * r2 // 8, r2 // 2), into=half["ff2"])}
    x1, h2 = _rowwise(resid1, [(x2, D, 0), (mo, D, 0)], [gate1, norm2_g, shift2, scale2], [(D, F32), (D, BF16)], [], "resid1",
                      comm=cm)
    half["ff2"] = cm.result(hs["ff2"])
    cm = _Comm()
    hs = {"ff2": _ag_ici(cm, wblk["ff2"], waxis["ff2"], rows=(r2 // 2, r2), into=half["ff2"])}
    u, act = _mm(h2, wf["ff1"], "nn", (F32, BF16), "ff1", comm=cm, epi=lambda r: (r, jnp.square(jnp.maximum(r, 0.0))))
    half["ff2"] = cm.result(hs["ff2"])
    cm = _Comm()
    hs = {"ff2": _ag_d2d(cm, half["ff2"], waxis["ff2"])}
    _call(lambda: None, [], name="ag_d2d_ff2", out_shape=(), comm=cm)
    wf["ff2"] = cm.result(hs["ff2"])
    ff = _mm(act, wf["ff2"], "nn", F32, "ff2")

    def loss_fn(x1v, ffv, tv, g2):
        e = x1v + g2 * ffv - tv
        dy = e * (1.0 / D)
        return (dy, dy * g2), (jnp.sum(e * e, axis=0, keepdims=True), jnp.sum(dy * ffv, axis=0, keepdims=True))

    dy, d_ff, sq_sum, d_gate2 = _rowwise(loss_fn, [(x1, D, 0), (ff, D, 0), (tgt, D, 0)], [gate2],
                                         [(D, F32), (D, BF16)], [(1, D), (1, D)], "loss")
    loss = lax.psum(jnp.sum(sq_sum) * (0.5 / D), ("x", "y", "c"))

    owner_base = jnp.stack([me ^ r for r in CHIP_RELS]).astype(jnp.int32)
    gw, recv1, part, recv2 = {}, {}, {}, {}
    gw["ff2"] = _mm(act, d_ff, "tn", BF16, "dw_ff2")
    cm = _Comm()
    hh = _rs_d2d(cm, gw["ff2"], waxis["ff2"])
    d_u = _mm(d_ff, wf["ff2"], "nt", BF16, "d_act", comm=cm, extras=[u], epi=lambda r, uv: (r * (2.0 * jnp.maximum(uv, 0.0)),))
    part["ff2"] = _rs_add(gw["ff2"], cm.result(hh), waxis["ff2"], owner_base, "rs_add_ff2")
    rows_ff2 = part["ff2"].shape[1]
    cm = _Comm()
    hh = _rs_ici(cm, part["ff2"], rows=(0, rows_ff2 // 2))
    gw["ff1"] = _mm(h2, d_u, "tn", BF16, "dw_ff1", comm=cm)
    cm2 = _Comm()
    hh2 = _rs_ici(cm2, part["ff2"], rows=(rows_ff2 // 2, rows_ff2), recv=cm.result(hh))
    hh1 = _rs_d2d(cm2, gw["ff1"], waxis["ff1"])
    d_h2 = _mm(d_u, wf["ff1"], "nt", F32, "d_h2", comm=cm2)
    recv2["ff2"] = cm2.result(hh2)
    part["ff1"] = _rs_add(gw["ff1"], cm2.result(hh1), waxis["ff1"], owner_base, "rs_add_ff1")

    def norm2_bwd(dh2v, x1v, dyv, mov, g2n, sh, sc, g1):
        _, vjp = jax.vjp(_modnorm, x1v, g2n, sh, sc)
        dx, dg, dsh, dsc = vjp(dh2v)
        dx1 = dyv + dx
        return (dx1, dx1 * g1), (dg, dsh, dsc, jnp.sum(dx1 * mov, axis=0, keepdims=True))

    d_x1, d_mo, d_g2n, d_shift2, d_scale2, d_gate1 = _rowwise(
        norm2_bwd, [(d_h2, D, 0), (x1, D, 0), (dy, D, 0), (mo, D, 0)], [norm2_g, shift2, scale2, gate1],
        [(D, F32), (D, BF16)], [(1, D)] * 4, "norm2_bwd")
    gw["out"] = _mm(merged, d_mo, "tn", BF16, "dw_out")
    cm = _Comm()
    hh = _rs_d2d(cm, gw["out"], waxis["out"])
    d_merged = _mm(d_mo, wf["out"], "nt", F32, "d_merged", comm=cm)
    part["out"] = _rs_add(gw["out"], cm.result(hh), waxis["out"], owner_base, "rs_add_out")

    def merge_bwd(dmv, bhv, bav, ghg, gat):
        _, vjp = jax.vjp(merge_fn, bhv, bav, ghg, gat)
        return vjp(dmv), ()

    d_bh, d_ba, d_ghg, d_gat = _rowwise(merge_bwd, [(d_merged, D, 0), (bh, D, 0), (ba, D, 0), (pg, D, 0), (pg, D, 1)], [],
                                        [(D, BF16)] * 4, [], "merge_bwd")
    gw["bhg"] = _mm(o_hg, d_bh, "tn", BF16, "dw_bhg")
    gw["bat"] = _mm(o_at, d_ba, "tn", BF16, "dw_bat")
    cm = _Comm()
    hh = {k: _rs_d2d(cm, gw[k], waxis[k]) for k in ("bhg", "bat")}
    d_ohg = _mm(d_bh, wf["bhg"], "nt", F32, "d_ohg", comm=cm)
    for k in ("bhg", "bat"):
        part[k] = _rs_add(gw[k], cm.result(hh[k]), waxis[k], owner_base, "rs_add_" + k)
    d_oat = _mm(d_ba, wf["bat"], "nt", BF16, "d_oat")

    cm = _Comm()
    hh = {"ff1": _rs_ici(cm, part["ff1"])}
    d_hq, d_hf, d_hi, d_hg, d_lb, d_gout_h = _hgrn_bwd(p4, hg_lb_logits, hg_out_norm_g, s_all, d_ohg, H, comm=cm)
    recv2["ff1"] = cm.result(hh["ff1"])
    cm = _Comm()
    hh = {k: _rs_ici(cm, part[k]) for k in small}
    dq_t, dkp, dvp, d_qg, d_kg, d_sk, d_bias = _attn_bwd(q_t, kp, vp, q_norm_g, k_norm_g, sinks3, bias,
                                                         _heads_first(d_oat, AH), KVH, comm=cm)
    for k in hh:
        recv2[k] = cm.result(hh[k])
    d_aq = _heads_last(dq_t)
    d_ak = _heads_last(dkp[:, AT_BLOCK:, :]).astype(BF16)
    d_av = _heads_last(dvp[:, AT_BLOCK:, :]).astype(BF16)
    d_proj = jnp.concatenate([d_hq, d_hf, d_hi, d_hg, d_aq, d_ak, d_av, d_ghg, d_gat], axis=1)
    gw_in = _mm(h, d_proj, "tn", BF16, "dw_in")

    wm = LANES * A
    cm = _Comm()
    hi_ = cm.inp(gw_in)
    h_main, h_mid = cm.out((4, D, wm), BF16), cm.out((4, D, LANES), BF16)
    for i, r in enumerate(CHIP_RELS):
        def main_view(ref, p, r=r):
            o = p["me"] ^ r ^ 1
            return ref.at[:, pl.ds(pl.multiple_of((PAIR * (o // 2) + (A + 1) * (1 - p["c"])) * LANES, LANES), wm)]

        def mid_view(ref, p, r=r):
            o = p["me"] ^ r
            return ref.at[:, pl.ds(pl.multiple_of((PAIR * (o // 2) + A) * LANES, LANES), LANES)]

        cm.copy(hi_, main_view, h_main, _slot_view(i), 1)
        cm.copy(hi_, mid_view, h_mid, _slot_view(i), 1)
    _call(lambda: None, [], name="rs_d2d_in", out_shape=(), comm=cm)
    chip = jnp.stack([(me ^ r) // 2 for r in CHIP_RELS]).astype(jnp.int32)
    part_main = _rs_add(gw_in, cm.result(h_main), 1, PAIR * chip + (A + 1) * cc, "rs_add_in_main", tw=LANES)
    part_mid = _rs_add(gw_in, cm.result(h_mid), 1, PAIR * chip + A, "rs_add_in_mid", tw=LANES)
    rs_in = _rs_split_start([part_main, part_mid], "rs_in_start")
    d_h = _mm(d_proj, w_in_full, "nt", F32, "d_h", after=[rs_in["token"]])

    def norm1_bwd(dhv, xv, dx1v, g1n, sh, sc):
        _, vjp = jax.vjp(_modnorm, xv, g1n, sh, sc)
        dx, dg, dsh, dsc = vjp(dhv)
        return (dx1v + dx,), (dg, dsh, dsc)

    grad_x, d_g1n, d_shift1, d_scale1 = _rowwise(norm1_bwd, [(d_h, D, 0), (x2, D, 0), (d_x1, D, 0)],
                                                 [norm1_g, shift1, scale1], [(D, F32)], [(1, D)] * 3, "norm1_bwd")

    def sum4(p0, p1, p2, p3):
        return ((p0.astype(F32) + p1.astype(F32)) + p2.astype(F32)) + p3.astype(F32)

    def update_fn(w, m, v, p0, p1, p2, p3):
        g = sum4(p0, p1, p2, p3)
        delta, mn, vn = _adamw(w, g, m, v)
        return (g, delta, mn, vn), ()

    wmv = dict(zip(wnames, ((w_branch_hg, m_w_branch_hg, v_w_branch_hg), (w_branch_attn, m_w_branch_attn, v_w_branch_attn),
                            (w_out, m_w_out, v_w_out), (w_ff1, m_w_ff1, v_w_ff1), (w_ff2, m_w_ff2, v_w_ff2))))
    res = {}

    def update(k, p, rx):
        w, m, v = (t[0] for t in wmv[k])
        n = w.shape[1]
        ins = [(t, n, 0) for t in (w, m, v)] + [(p, n, 0, 0)] + [(rx, n, 0, i) for i in range(3)]
        res[k] = [t[None] for t in _rowwise(update_fn, ins, [], [(n, F32)] * 4, [], "update_" + k)]

    for k in wnames:
        update(k, part[k], recv2[k])
    (part_main, part_mid), (rx_main, rx_mid) = _rs_split_wait(rs_in, [grad_x] + [res[k][0] for k in wnames], "rs_in_wait")
    g_main, = _rowwise(lambda *p: ((sum4(*p),), ()), [(part_main, wm, 0, 0)] + [(rx_main, wm, 0, i) for i in range(3)], [],
                       [(wm, F32)], [], "sum_in_main")
    g_mid, = _rowwise(lambda *p: ((sum4(*p),), ()), [(part_mid, LANES, 0, 0)] + [(rx_mid, LANES, 0, i) for i in range(3)], [],
                      [(LANES, F32)], [], "sum_in_mid")
    g_in = jnp.where(cc == 0, jnp.concatenate([g_main, g_mid[:, :LANES // 2]], axis=1),
                     jnp.concatenate([g_mid[:, LANES // 2:], g_main], axis=1))

    def update_given(w, m, v, g):
        delta, mn, vn = _adamw(w, g, m, v)
        return (g, delta, mn, vn), ()

    res["in"] = [t[None] for t in _rowwise(update_given, [(t, BW, 0) for t in (w_in[0], m_w_in[0], v_w_in[0], g_in)], [],
                                           [(BW, F32)] * 4, [], "update_in")]

    d_ada_row = jnp.concatenate([d_shift1, d_scale1, d_gate1, d_shift2, d_scale2, d_gate2], axis=1)
    (d_ada_row,) = _behind([d_ada_row], [g_mid])
    d_ada_all = _gather_small(d_ada_row, me, "gather_dada")[:, 0, :]
    d_ada_cols = lax.dynamic_slice(d_ada_all, (0, me * ADA_N), (N_DEV, ADA_N))

    def ada_update(cv, dav, w, m, v):
        g = _bdot(_silu(cv), dav, TN)
        delta, mn, vn = _adamw(w, g, m, v)
        return (g, delta, mn, vn), ()

    res["ada"] = [t[None] for t in _ada_update_call(ada_update, c_all, d_ada_cols, w_ada[0], m_w_ada[0], v_w_ada[0], _tile(D, 256, 16))]

    d_sinks = d_sk.reshape(1, AH)
    (d_table_t,) = _whole(lambda db, bk: (_dot(db, _onehot(bk), NT, precision=HIGHEST),),
                          [d_bias.reshape(AH, AT_BLOCK * 2 * AT_BLOCK), bucket], [((AH, N_BUCKETS), F32)], "bias_bwd")
    smalls = [d_g1n, d_g2n, d_lb, d_gout_h, d_qg, d_kg, d_sinks, d_table_t.T.reshape(1, N_BUCKETS * AH)]
    widths = [s.shape[1] for s in smalls]
    lanes = [-(-w // LANES) * LANES for w in widths]
    smalls = [jnp.pad(s, ((0, 0), (0, p - w))) for s, w, p in zip(smalls, widths, lanes)]
    (smalls_row,) = _behind([jnp.concatenate(smalls, axis=1)], [g_mid])
    packed = _gather_small(smalls_row, me, "gather_small")[:, 0, :]
    offs = [sum(lanes[:i]) for i in range(len(lanes))]

    def small_update(pk, dada, lg, *wmv_flat):
        tot = pk[0:1]
        for d in range(1, N_DEV):
            tot = tot + pk[d:d + 1]
        gb = dada[0:1]
        for d in range(1, N_DEV):
            gb = gb + dada[d:d + 1]
        gs = [tot[:, offs[i]:offs[i] + widths[i]] for i in range(len(widths))]
        _, lb_vjp = jax.vjp(_softmax0, lg)
        (g_lg,) = lb_vjp(gs[2])
        grads = [gb, gs[0], gs[1], g_lg, gs[3], gs[4], gs[5], gs[6], gs[7]]
        outs = []
        for i, g in enumerate(grads):
            w, m, v = wmv_flat[3 * i:3 * i + 3]
            delta, mn, vn = _adamw(w, g, m, v)
            outs += [g, delta, mn, vn]
        return tuple(outs)

    tbl = lambda t: t.reshape(1, N_BUCKETS * AH)
    small_wmv = [(b_ada, m_b_ada, v_b_ada), (norm1_g, m_norm1_g, v_norm1_g), (norm2_g, m_norm2_g, v_norm2_g),
                 (hg_lb_logits, m_hg_lb_logits, v_hg_lb_logits), (hg_out_norm_g, m_hg_out_norm_g, v_hg_out_norm_g),
                 (q_norm_g, m_q_norm_g, v_q_norm_g), (k_norm_g, m_k_norm_g, v_k_norm_g),
                 (attn_sinks, m_attn_sinks, v_attn_sinks),
                 (tbl(rel_bias_table), tbl(m_rel_bias_table), tbl(v_rel_bias_table))]
    flat = [t for trip in small_wmv for t in trip]
    out_shapes = [(trip[0].shape, F32) for trip in small_wmv for _ in range(4)]
    sres = _whole(small_update, [packed, d_ada_all, hg_lb_logits] + flat, out_shapes, "small_update")
    names_small = ("b_ada", "norm1_g", "norm2_g", "lb", "gout", "qg", "kg", "sinks", "table")
    for i, k in enumerate(names_small):
        r = sres[4 * i:4 * i + 4]
        if k == "table":
            r = [t.reshape(N_BUCKETS, AH) for t in r]
        res[k] = r

    order = ("ada", "b_ada", "norm1_g", "norm2_g", "in", "lb", "gout", "qg", "kg", "sinks", "table", "bhg", "bat", "out", "ff1", "ff2")
    outs = [loss, grad_x[None]]
    for j in range(4):
        outs += [res[k][j] for k in order]
    return tuple(outs)
```

```python
import functools
import math

import jax
import jax.numpy as jnp
from jax import lax
from jax.experimental import pallas as pl
from jax.experimental.pallas import tpu as pltpu

F32 = jnp.float32
BF16 = jnp.bfloat16
EPS = 1e-6
NEG_INF = -1e30
HG_DK = 128
HG_CHUNK = 64
AT_BLOCK = 128
N_BUCKETS = 32
MAX_EXACT = 16
MAX_DISTANCE = 128
N_DEV = 8
LANES = 128
VMEM_LIMIT = 56 * 1024 * 1024
ADAM_LR, ADAM_B1, ADAM_B2, ADAM_EPS, ADAM_WD, ADAM_STEP = 0.001, 0.9, 0.999, 1e-08, 0.01, 10
HIGHEST = lax.Precision.HIGHEST
MESH = pl.DeviceIdType.MESH
ANY = pl.BlockSpec(memory_space=pl.ANY)
CHIP_RELS = (0, 4, 2, 6)

NN = (((1,), (0,)), ((), ()))
NT = (((1,), (1,)), ((), ()))
TN = (((0,), (0,)), ((), ()))


def _tile(n, pref, unit):
    if n <= pref:
        return n
    t = (pref // unit) * unit
    while t >= unit:
        if n % t == 0:
            return t
        t -= unit
    return n


def _dot(a, b, dn, precision=None):
    return lax.dot_general(a, b, dn, preferred_element_type=F32, precision=precision)


def _bdot(a, b, dn):
    return _dot(a.astype(BF16), b.astype(BF16), dn)


def _position():
    x, y, c = lax.axis_index("x"), lax.axis_index("y"), lax.axis_index("c")
    return dict(x=x, y=y, c=c, me=4 * x + 2 * y + c)


def _peer_position(p, rel):
    x = 1 - p["x"] if rel & 4 else p["x"]
    y = 1 - p["y"] if rel & 2 else p["y"]
    c = 1 - p["c"] if rel & 1 else p["c"]
    return dict(x=x, y=y, c=c, me=4 * x + 2 * y + c)


class _Comm:
    def __init__(self):
        self.ins, self.outs, self.alias, self.plans, self.res = [], [], {}, [], None

    def inp(self, arr):
        self.ins.append(arr)
        return ("i", len(self.ins) - 1)

    def out(self, shape, dtype, alias=None):
        self.outs.append(jax.ShapeDtypeStruct(tuple(shape), dtype))
        if alias is not None:
            self.alias[alias[1]] = len(self.outs) - 1
        return ("o", len(self.outs) - 1)

    def copy(self, src, src_view, dst, dst_view, rel):
        self.plans.append((src, src_view, dst, dst_view, rel))

    def result(self, handle):
        return self.res[handle[1]]

    def build(self, in_refs, out_refs, send_sems, recv_sems):
        pos = _position()
        ref = lambda h: in_refs[h[1]] if h[0] == "i" else out_refs[h[1]]
        ops = []
        for k, (src, sv, dst, dv, rel) in enumerate(self.plans):
            s = sv(ref(src), pos)
            if rel == 0:
                cp = pltpu.make_async_copy(s, dv(ref(dst), pos), send_sems.at[k])
                ops.append((cp.start, cp.wait))
                continue
            peer = _peer_position(pos, rel)
            mk = lambda d: pltpu.make_async_remote_copy(
                src_ref=s, dst_ref=d, send_sem=send_sems.at[k], recv_sem=recv_sems.at[k],
                device_id=(peer["x"], peer["y"], peer["c"]), device_id_type=MESH)
            out_cp, in_cp = mk(dv(ref(dst), pos)), mk(dv(ref(dst), peer))

            def wait(out_cp=out_cp, in_cp=in_cp):
                out_cp.wait_send()
                in_cp.wait_recv()

            ops.append((out_cp.start, wait))
        return ops


def _call(body, args, *, name, out_shape, in_specs=None, out_specs=None, grid=None, scratch_shapes=(), comm=None,
          prefetch=None, aliases=None, after=()):
    single = not isinstance(out_shape, (tuple, list))
    out_shape = (out_shape,) if single else tuple(out_shape)
    n_in, n_out, n_scr = len(args), len(out_shape), len(scratch_shapes)
    vm = pl.BlockSpec(memory_space=pltpu.VMEM)
    in_specs = [vm] * n_in if in_specs is None else list(in_specs)
    out_specs = [vm] * n_out if out_specs is None else (list(out_specs) if isinstance(out_specs, (tuple, list)) else [out_specs])
    n_pf = 0 if prefetch is None else len(prefetch)
    kw = {} if aliases is None else {"input_output_aliases": dict(aliases)}
    if comm is None and after:
        n_dep = len(after)

        def fn(*refs):
            body(*refs[:n_pf + n_in], *refs[n_pf + n_in + n_dep:])

        all_args, all_scratch = list(args) + list(after), list(scratch_shapes)
        in_specs = in_specs + [ANY] * n_dep
    elif comm is None:
        fn = body
        all_args, all_scratch = list(args), list(scratch_shapes)
    else:
        n_ci, n_co, n_x = len(comm.ins), len(comm.outs), len(comm.plans)

        def fn(*refs):
            pf, refs = refs[:n_pf], refs[n_pf:]
            o_in, c_in = refs[:n_in], refs[n_in:n_in + n_ci]
            o_out = refs[n_in + n_ci:n_in + n_ci + n_out]
            c_out = refs[n_in + n_ci + n_out:n_in + n_ci + n_out + n_co]
            scr = refs[n_in + n_ci + n_out + n_co:]
            ops = comm.build(c_in, c_out, scr[n_scr], scr[n_scr + 1])
            if grid:
                first = functools.reduce(jnp.logical_and, [pl.program_id(i) == 0 for i in range(len(grid))])
                last = functools.reduce(jnp.logical_and, [pl.program_id(i) == g - 1 for i, g in enumerate(grid)])

                @pl.when(first)
                def _():
                    for start, _w in ops:
                        start()
            else:
                for start, _w in ops:
                    start()
            body(*pf, *o_in, *o_out, *scr[:n_scr])
            if grid:
                @pl.when(last)
                def _():
                    for _s, wait in ops:
                        wait()
            else:
                for _s, wait in ops:
                    wait()

        all_args = list(args) + list(comm.ins)
        in_specs = in_specs + [ANY] * n_ci
        out_shape = out_shape + tuple(comm.outs)
        out_specs = out_specs + [ANY] * n_co
        all_scratch = list(scratch_shapes) + [pltpu.SemaphoreType.DMA((n_x,)), pltpu.SemaphoreType.DMA((n_x,))]
        kw["input_output_aliases"] = {n_pf + n_in + i: n_out + o for i, o in comm.alias.items()}
    sem = None if grid is None else ("arbitrary",) * len(grid)
    params = pltpu.CompilerParams(dimension_semantics=sem, vmem_limit_bytes=VMEM_LIMIT)
    if prefetch is None:
        spec = dict(in_specs=in_specs, out_specs=tuple(out_specs), scratch_shapes=all_scratch)
        if grid is not None:
            spec["grid"] = grid
    else:
        spec = dict(grid_spec=pltpu.PrefetchScalarGridSpec(
            num_scalar_prefetch=n_pf, grid=grid, in_specs=in_specs, out_specs=tuple(out_specs), scratch_shapes=all_scratch))
        all_args = list(prefetch) + all_args
    res = pl.pallas_call(fn, name=name, out_shape=out_shape, compiler_params=params, **spec, **kw)(*all_args)
    res = list(res)
    if comm is not None:
        comm.res = res[n_out:]
        res = res[:n_out]
    return res[0] if single else res


def _whole_view(ref, pos):
    return ref


def _block_view(axis, n, index, rows=None):
    def view(ref, pos):
        off = pl.multiple_of(index(pos) * n, n)
        if rows is None:
            return ref.at[:, pl.ds(off, n)] if axis == 1 else ref.at[pl.ds(off, n), :]
        lo, cnt = rows[0], rows[1] - rows[0]
        if axis == 1:
            return ref.at[pl.ds(lo, cnt), pl.ds(off, n)]
        return ref.at[pl.ds(pl.multiple_of(off + lo, 16), cnt), :]
    return view


def _rows_view(rows):
    def view(ref, pos):
        return ref if rows is None else ref.at[pl.ds(rows[0], rows[1] - rows[0]), :]
    return view


def _slot_view(i, rows=None):
    def view(ref, pos):
        return ref.at[i] if rows is None else ref.at[i, pl.ds(rows[0], rows[1] - rows[0]), :]
    return view


def _exchange(items, name):
    cm = _Comm()
    for a, rel in items:
        cm.copy(cm.inp(a), _whole_view, cm.out(a.shape, a.dtype), _whole_view, rel)
    _call(lambda: None, [], name=name, out_shape=(), comm=cm)
    return cm.res


def _gather_small(v, me, name):
    cm = _Comm()
    hi, ho = cm.inp(v), cm.out((N_DEV,) + v.shape, v.dtype)
    for rel in range(N_DEV):
        cm.copy(hi, _whole_view, ho, lambda ref, p: ref.at[p["me"]], rel)
    _call(lambda: None, [], name=name, out_shape=(), comm=cm)
    return cm.result(ho)


def _ag_ici(cm, blk, axis, rows=None, into=None):
    n = blk.shape[axis]
    shape = list(blk.shape)
    shape[axis] = n * N_DEV
    hi = cm.inp(blk)
    ho = cm.out(shape, blk.dtype) if into is None else cm.out(shape, blk.dtype, alias=cm.inp(into))
    own = _block_view(axis, n, lambda p: p["me"], rows)
    for rel in CHIP_RELS:
        cm.copy(hi, _rows_view(rows), ho, own, rel)
    return ho


def _ag_d2d(cm, full, axis):
    n = full.shape[axis] // N_DEV
    hi = cm.inp(full)
    ho = cm.out(full.shape, full.dtype, alias=hi)
    for r in CHIP_RELS:
        v = _block_view(axis, n, functools.partial(lambda p, r: p["me"] ^ r, r=r))
        cm.copy(hi, v, ho, v, 1)
    return ho


def _rs_d2d(cm, gw, axis):
    n = gw.shape[axis] // N_DEV
    shape = list(gw.shape)
    shape[axis] = n
    hi, ho = cm.inp(gw), cm.out([4] + shape, gw.dtype)
    for i, r in enumerate(CHIP_RELS):
        cm.copy(hi, _block_view(axis, n, functools.partial(lambda p, r: p["me"] ^ r ^ 1, r=r)), ho, _slot_view(i), 1)
    return ho


def _rs_ici(cm, part, rows=None, recv=None):
    if recv is None:
        ho = cm.out((3,) + part.shape[1:], part.dtype)
    else:
        ho = cm.out(recv.shape, recv.dtype, alias=cm.inp(recv))
    hi = cm.inp(part)
    for i in (1, 2, 3):
        cm.copy(hi, _slot_view(i, rows), ho, _slot_view(i - 1, rows), CHIP_RELS[i])
    return ho


def _rs_add(gw, recv, axis, base, name, tw=None):
    _, R, n = recv.shape
    if axis == 1:
        tw = n if tw is None else tw
        gw_spec = pl.BlockSpec((R, tw), lambda i, t, b: (0, b[i] + t))
        rv_spec = pl.BlockSpec((None, R, tw), lambda i, t, b: (i, 0, t))
        grid = (4, n // tw)
    else:
        tw = _tile(n, 1024, LANES)
        gw_spec = pl.BlockSpec((R, tw), lambda i, t, b: (b[i], t))
        rv_spec = pl.BlockSpec((None, R, tw), lambda i, t, b: (i, 0, t))
        grid = (4, n // tw)

    def body(b_ref, g_ref, r_ref, o_ref):
        o_ref[...] = (g_ref[...].astype(F32) + r_ref[...].astype(F32)).astype(o_ref.dtype)

    return _call(body, [gw, recv], name=name, out_shape=jax.ShapeDtypeStruct(recv.shape, recv.dtype), grid=grid,
                 in_specs=[gw_spec, rv_spec], out_specs=rv_spec, prefetch=[base])


HBM_SPEC = pl.BlockSpec(memory_space=pltpu.HBM)
SEM_SPEC = pl.BlockSpec(memory_space=pltpu.SEMAPHORE)
SPLIT_PARAMS = pltpu.CompilerParams(has_side_effects=pltpu.SideEffectType.DATAFLOW_SIDE_EFFECTING)


def _split_copies(refs, plans, send_sems, recv_sems):
    pos = _position()
    out = []
    for k, (si, sv, li, lv, rel) in enumerate(plans):
        peer = _peer_position(pos, rel)
        mk = lambda d: pltpu.make_async_remote_copy(
            src_ref=sv(refs[si], pos), dst_ref=d, send_sem=send_sems.at[k], recv_sem=recv_sems.at[k],
            device_id=(peer["x"], peer["y"], peer["c"]), device_id_type=MESH)
        out.append((mk(lv(refs[li], pos)), mk(lv(refs[li], peer))))
    return out


def _split_start(arrays, plans, name):
    n = len(arrays)

    def body(*refs):
        send_sems, recv_sems = refs[n], refs[n + 1]
        for out_cp, _ in _split_copies(refs[:n], plans, send_sems, recv_sems):
            out_cp.start()
        refs[-1][...] = jnp.zeros_like(refs[-1])

    sems = pltpu.SemaphoreType.DMA((len(plans),))
    res = pl.pallas_call(
        body, name=name,
        out_shape=(sems, sems) + tuple(pltpu.HBM(a.shape, a.dtype) for a in arrays) + (jax.ShapeDtypeStruct((8, LANES), F32),),
        in_specs=[HBM_SPEC] * n, out_specs=(SEM_SPEC, SEM_SPEC) + (HBM_SPEC,) * n + (pl.BlockSpec(memory_space=pltpu.VMEM),),
        input_output_aliases={i: 2 + i for i in range(n)}, compiler_params=SPLIT_PARAMS,
    )(*[pltpu.with_memory_space_constraint(a, pltpu.HBM) for a in arrays])
    return res[0], res[1], list(res[2:2 + n]), res[-1]


def _split_wait(send_sems, recv_sems, arrays, plans, after, name):
    n, na = len(arrays), len(after)

    def body(*refs):
        for out_cp, in_cp in _split_copies(refs[:n], plans, refs[n], refs[n + 1]):
            out_cp.wait_send()
            in_cp.wait_recv()

    res = pl.pallas_call(
        body, name=name, out_shape=tuple(pltpu.HBM(a.shape, a.dtype) for a in arrays),
        in_specs=[HBM_SPEC] * n + [SEM_SPEC, SEM_SPEC] + [ANY] * na, out_specs=(HBM_SPEC,) * n,
        input_output_aliases={i: i for i in range(n)}, compiler_params=SPLIT_PARAMS,
    )(*arrays, send_sems, recv_sems, *after)
    return list(res)


def _ag_split_start(blks, axes, name):
    nw = len(blks)
    fulls = []
    for b, ax in zip(blks, axes):
        shape = list(b.shape)
        shape[ax] *= N_DEV
        fulls.append(lax.empty(tuple(shape), b.dtype))
    plans = [(i, _whole_view, nw + i, _block_view(axes[i], blks[i].shape[axes[i]], lambda p: p["me"]), rel)
             for i in range(nw) for rel in CHIP_RELS[1:]]
    send_sems, recv_sems, arrays, token = _split_start(list(blks) + fulls, plans, name)
    return dict(sems=(send_sems, recv_sems), arrays=arrays, plans=plans, token=token, nw=nw)


def _ag_split_wait(h, after, name):
    arrays = _split_wait(h["sems"][0], h["sems"][1], h["arrays"], h["plans"], after, name)
    return arrays[:h["nw"]], arrays[h["nw"]:]


def _ag_d2d_own(cm, full, blk, axis):
    n = blk.shape[axis]
    hb, hi = cm.inp(blk), cm.inp(full)
    ho = cm.out(full.shape, full.dtype, alias=hi)
    own = _block_view(axis, n, lambda p: p["me"])
    cm.copy(hb, _whole_view, ho, own, 0)
    cm.copy(hb, _whole_view, ho, own, 1)
    for r in CHIP_RELS[1:]:
        v = _block_view(axis, n, functools.partial(lambda p, r: p["me"] ^ r, r=r))
        cm.copy(hi, v, ho, v, 1)
    return ho


def _rs_split_start(parts, name):
    nw = len(parts)
    lands = [lax.empty((3,) + p.shape[1:], p.dtype) for p in parts]
    plans = [(s, _slot_view(i), nw + s, _slot_view(i - 1), CHIP_RELS[i]) for s in range(nw) for i in (1, 2, 3)]
    send_sems, recv_sems, arrays, token = _split_start(list(parts) + lands, plans, name)
    return dict(sems=(send_sems, recv_sems), arrays=arrays, plans=plans, token=token, nw=nw)


def _rs_split_wait(h, after, name):
    arrays = _split_wait(h["sems"][0], h["sems"][1], h["arrays"], h["plans"], after, name)
    return arrays[:h["nw"]], arrays[h["nw"]:]


def _behind(xs, tokens):
    out = lax.optimization_barrier((tuple(xs), tuple(tokens)))
    return list(out[0])


def _ag_w_in(src, a, D, INW):
    wm = LANES * a

    hd = D // 2
    ALL, TOP, BOT = (0, D), (0, hd), (hd, D)

    def main_place(ref, p, rows=ALL):
        off = pl.multiple_of(((2 * a + 1) * (p["me"] // 2) + (a + 1) * p["c"]) * LANES, LANES)
        return ref.at[pl.ds(rows[0], rows[1] - rows[0]), pl.ds(off, wm)]

    def main_src(ref, p):
        return ref.at[:, pl.ds(pl.multiple_of(p["c"] * LANES, LANES), wm)]

    def mid_src(ref, p):
        return ref.at[:, pl.ds(pl.multiple_of((1 - p["c"]) * wm, LANES), LANES)]

    def mid_place(ref, p, rows=ALL):
        return ref.at[p["me"], pl.ds(rows[0], rows[1] - rows[0]), :]

    def body(src_ref, full_ref, mid_ref, send_sems, recv_sems):
        pos = _position()
        sib, xn, yn = (_peer_position(pos, r) for r in (1, 4, 2))
        dg = _peer_position(pos, 6)
        started = []

        def remote(k, s, d, to):
            return pltpu.make_async_remote_copy(src_ref=s, dst_ref=d, send_sem=send_sems.at[k], recv_sem=recv_sems.at[k],
                                                device_id=(to["x"], to["y"], to["c"]), device_id_type=MESH)

        def send(k, owner, rows, to, from_src=False):
            for j, (src_v, place) in enumerate(((main_src, main_place), (mid_src, mid_place))):
                s = src_v(src_ref, pos) if from_src else place(full_ref if j == 0 else mid_ref, owner, rows)
                cp = remote(k + j, s, place(full_ref if j == 0 else mid_ref, owner, rows), to)
                cp.start()
                started.append(cp)

        def landed(k, owner, rows, frm):
            for j, place in enumerate((main_place, mid_place)):
                ref = full_ref if j == 0 else mid_ref
                remote(k + j, place(ref, owner, rows), place(ref, owner, rows), frm).wait_recv()

        local = [pltpu.make_async_copy(main_src(src_ref, pos), main_place(full_ref, pos), send_sems.at[18]),
                 pltpu.make_async_copy(mid_src(src_ref, pos), mid_place(mid_ref, pos), send_sems.at[19])]
        for cp in local:
            cp.start()
        send(0, pos, ALL, sib, from_src=True)
        send(2, pos, ALL, xn, from_src=True)
        send(4, pos, ALL, yn, from_src=True)
        landed(2, xn, ALL, xn)
        send(10, xn, ALL, sib)
        send(6, xn, TOP, yn)
        landed(4, yn, ALL, yn)
        send(12, yn, ALL, sib)
        send(8, yn, BOT, xn)
        landed(6, dg, TOP, yn)
        send(14, dg, TOP, sib)
        landed(8, dg, BOT, xn)
        send(16, dg, BOT, sib)
        sib_of = lambda p: _peer_position(p, 1)
        landed(0, sib, ALL, sib)
        landed(10, sib_of(xn), ALL, sib)
        landed(12, sib_of(yn), ALL, sib)
        landed(14, sib_of(dg), TOP, sib)
        landed(16, sib_of(dg), BOT, sib)
        for cp in started:
            cp.wait_send()
        for cp in local:
            cp.wait()

    return _call(body, [src], name="ag_w_in", in_specs=[ANY], out_specs=[ANY, ANY],
                 out_shape=(jax.ShapeDtypeStruct((D, INW), BF16), jax.ShapeDtypeStruct((N_DEV, D, LANES), BF16)),
                 scratch_shapes=[pltpu.SemaphoreType.DMA((20,)), pltpu.SemaphoreType.DMA((20,))])


def _patch_mid(full, mid, a):
    D = full.shape[0]

    def body(full_ref, e_ref, o_ref, out_ref):
        out_ref[...] = e_ref[...] + o_ref[...]

    return _call(body, [full, mid, mid], name="patch_mid", grid=(N_DEV // 2,),
                 out_shape=jax.ShapeDtypeStruct(full.shape, full.dtype),
                 in_specs=[ANY, pl.BlockSpec((None, D, LANES), lambda j: (2 * j, 0, 0)),
                           pl.BlockSpec((None, D, LANES), lambda j: (2 * j + 1, 0, 0))],
                 out_specs=pl.BlockSpec((D, LANES), lambda j: (0, (2 * a + 1) * j + a)), aliases={0: 0})


MM_RESIDENT = 2048


def _mm(a, b, mode, out_dtype, name, b_off=0, n=None, comm=None, extras=(), epi=None, tn=None, after=()):
    if mode == "nn":
        (M, K), (K2, N) = a.shape, b.shape
    elif mode == "nt":
        (M, K), (N, K2) = a.shape, b.shape
    else:
        (K, M), (K2, N) = a.shape, b.shape
    assert K == K2, (a.shape, b.shape, mode)
    if n is not None:
        N = n
    single = not isinstance(out_dtype, (tuple, list))
    out_dtypes = (out_dtype,) if single else tuple(out_dtype)
    if epi is None:
        epi = lambda r: (r,)
    tk = K if K <= MM_RESIDENT else (MM_RESIDENT if K % MM_RESIDENT == 0 else _tile(K, 512, LANES))
    nk = K // tk
    if M > MM_RESIDENT and mode == "tn" and N <= MM_RESIDENT and not b_off:
        tm, tn = _tile(M, 512, LANES), N
    elif nk > 1:
        tm, tn = _tile(M, 1024, LANES), _tile(N, tn or 1024, LANES)
    else:
        tm = _tile(M, MM_RESIDENT, LANES)
        tn = _tile(math.gcd(N, b_off) if b_off else N, tn or 512, LANES)
    jb = b_off // tn
    dn = {"nn": NN, "nt": NT, "tn": TN}[mode]
    ne, no = len(extras), len(out_dtypes)

    def body(a_ref, b_ref, *rest):
        e_refs, o_refs = rest[:ne], rest[ne:ne + no]

        def finish(r):
            for o_ref, v in zip(o_refs, epi(r, *[e[...] for e in e_refs])):
                o_ref[...] = v.astype(o_ref.dtype)

        if nk == 1:
            finish(_bdot(a_ref[...], b_ref[...], dn))
            return
        acc_ref = rest[ne + no]
        k = pl.program_id(2)

        @pl.when(k == 0)
        def _():
            acc_ref[...] = _bdot(a_ref[...], b_ref[...], dn)

        @pl.when(jnp.logical_and(k > 0, k < nk - 1))
        def _():
            acc_ref[...] += _bdot(a_ref[...], b_ref[...], dn)

        @pl.when(k == nk - 1)
        def _():
            finish(acc_ref[...] + _bdot(a_ref[...], b_ref[...], dn))

    a_spec = pl.BlockSpec((tk, tm), lambda i, j, k: (k, i)) if mode == "tn" else pl.BlockSpec((tm, tk), lambda i, j, k: (i, k))
    b_spec = pl.BlockSpec((tn, tk), lambda i, j, k: (j, k)) if mode == "nt" else pl.BlockSpec((tk, tn), lambda i, j, k: (k, j + jb))
    o_spec = pl.BlockSpec((tm, tn), lambda i, j, k: (i, j))
    res = _call(body, [a, b] + list(extras), name=name, grid=(M // tm, N // tn, nk),
                out_shape=tuple(jax.ShapeDtypeStruct((M, N), dt) for dt in out_dtypes),
                in_specs=[a_spec, b_spec] + [o_spec] * ne, out_specs=[o_spec] * no,
                scratch_shapes=[pltpu.VMEM((tm, tn), F32)] if nk > 1 else [], comm=comm, after=after)
    return res[0] if single else res


def _rowwise(fn, row_ins, bcast_ins, row_outs, acc_outs, name, rt=256, comm=None):
    L = row_ins[0][0].shape[-2]
    rt = _tile(L, rt, 16)
    nr, nb, no = len(row_ins), len(bcast_ins), len(row_outs)

    def body(*refs):
        i = pl.program_id(0)
        vals = [r[...] for r in refs[:nr + nb]]
        outs, accs = fn(*vals)
        for r, v in zip(refs[nr + nb:nr + nb + no], outs):
            r[...] = v.astype(r.dtype)
        acc_refs = refs[nr + nb + no:]

        @pl.when(i == 0)
        def _():
            for r in acc_refs:
                r[...] = jnp.zeros_like(r)

        for r, v in zip(acc_refs, accs):
            r[...] += v

    in_specs = []
    for spec in row_ins:
        w, cb = spec[1], spec[2]
        if len(spec) == 4:
            in_specs.append(pl.BlockSpec((None, rt, w), functools.partial(lambda i, cb, ld: (ld, i, cb), cb=cb, ld=spec[3])))
        else:
            in_specs.append(pl.BlockSpec((rt, w), functools.partial(lambda i, cb: (i, cb), cb=cb)))
    in_specs += [pl.BlockSpec(b.shape, lambda i: (0, 0)) for b in bcast_ins]
    out_specs = [pl.BlockSpec((rt, w), lambda i: (i, 0)) for w, _ in row_outs]
    out_specs += [pl.BlockSpec(s, lambda i: (0, 0)) for s in acc_outs]
    out_shape = [jax.ShapeDtypeStruct((L, w), dt) for w, dt in row_outs] + [jax.ShapeDtypeStruct(s, F32) for s in acc_outs]
    return _call(body, [s[0] for s in row_ins] + list(bcast_ins), name=name, grid=(L // rt,), out_shape=tuple(out_shape),
                 in_specs=in_specs, out_specs=out_specs, comm=comm)


def _whole(fn, ins, out_shapes, name):
    def body(*refs):
        outs = fn(*[r[...] for r in refs[:len(ins)]])
        for r, v in zip(refs[len(ins):], outs):
            r[...] = v.astype(r.dtype)

    return _call(body, list(ins), name=name, out_shape=tuple(jax.ShapeDtypeStruct(s, dt) for s, dt in out_shapes))


def _silu(x):
    return x * jax.nn.sigmoid(x)


def _rms(x, g):
    return (x * lax.rsqrt(jnp.mean(x * x, axis=-1, keepdims=True) + EPS)) * g


def _modnorm(x, g, shift, scale):
    return _rms(x, g) * (1.0 + scale) + shift


def _adamw(w, g, m, v):
    m = ADAM_B1 * m + (1.0 - ADAM_B1) * g
    v = ADAM_B2 * v + (1.0 - ADAM_B2) * jnp.square(g)
    m_hat = m / (1.0 - ADAM_B1 ** ADAM_STEP)
    v_hat = v / (1.0 - ADAM_B2 ** ADAM_STEP)
    delta = -ADAM_LR * (m_hat / (jnp.sqrt(v_hat) + ADAM_EPS) + ADAM_WD * w)
    return delta, m, v


def _lower_bound(lg):
    e = jnp.exp(lg - jnp.max(lg, axis=0, keepdims=True))
    return e[0:1] / jnp.sum(e, axis=0, keepdims=True)


def _hg_chunk(hq, hf, hi, lb, st):
    C = hq.shape[0]
    row = lax.broadcasted_iota(jnp.int32, (C, C), 0)
    col = lax.broadcasted_iota(jnp.int32, (C, C), 1)
    tri = row >= col
    sg = jax.nn.sigmoid(hf)
    f = lb + (1.0 - lb) * sg
    lf = jnp.log(f)
    k = 1.0 - f
    q = _silu(hq)
    b = _dot(tri.astype(F32), lf, NN, precision=HIGHEST)
    m = b[C // 2 - 1:C // 2]
    bl = b[C - 1:C]
    e_qm, e_km, e_kl, e_q = jnp.exp(b - m), jnp.exp(m - b), jnp.exp(bl - b), jnp.exp(b)
    qe, ke, kd, qb = q * e_qm, k * e_km, k * e_kl, q * e_q
    sc = jnp.where(tri, _bdot(qe, ke, NT), 0.0)
    o = _bdot(sc, hi, NN) + _bdot(qb, st, NT)
    dec = jnp.exp(bl)
    st_next = st * dec + _bdot(hi, kd, TN)
    return o, st_next, dict(tri=tri, sg=sg, f=f, k=k, q=q, qe=qe, ke=ke, kd=kd, qb=qb, sc=sc, dec=dec,
                            e_qm=e_qm, e_km=e_km, e_kl=e_kl, e_q=e_q)


def _hg_out(o, hgate, gout):
    return _rms(o, gout) * _silu(hgate)


HG_GROUP = 16


def _hgrn_fwd(p4, lb_logits, gout, H, comm=None):
    L = p4.shape[0]
    C = HG_CHUNK
    GR = _tile(L // C, HG_GROUP, 1)
    T = GR * C
    N = L // T

    def body(hq_ref, hf_ref, hi_ref, hg_ref, lg_ref, gout_ref, o_ref, s_ref, st_ref):
        @pl.when(pl.program_id(1) == 0)
        def _():
            st_ref[...] = jnp.zeros_like(st_ref)

        lb = _lower_bound(lg_ref[...])
        st = st_ref[...]
        for ci in range(GR):
            rows = pl.ds(ci * C, C)
            s_ref[0, ci] = st
            o, st, _ = _hg_chunk(hq_ref[rows, :], hf_ref[rows, :], hi_ref[rows, :], lb, st)
            o_ref[rows, :] = _hg_out(o, hg_ref[rows, :], gout_ref[...]).astype(o_ref.dtype)
        st_ref[...] = st

    blk = lambda s: pl.BlockSpec((T, HG_DK), functools.partial(lambda h, n, s: (n, s * H + h), s=s))
    return _call(
        body, [p4, p4, p4, p4, lb_logits, gout], name="hgrn_fwd", grid=(H, N),
        out_shape=(jax.ShapeDtypeStruct((L, H * HG_DK), BF16), jax.ShapeDtypeStruct((H, N * GR, HG_DK, HG_DK), F32)),
        in_specs=[blk(0), blk(1), blk(2), blk(3), pl.BlockSpec((2, HG_DK), lambda h, n: (0, h)),
                  pl.BlockSpec((1, HG_DK), lambda h, n: (0, 0))],
        out_specs=(pl.BlockSpec((T, HG_DK), lambda h, n: (n, h)),
                   pl.BlockSpec((1, GR, HG_DK, HG_DK), lambda h, n: (h, n, 0, 0))),
        scratch_shapes=[pltpu.VMEM((HG_DK, HG_DK), F32)], comm=comm)


def _hgrn_bwd(p4, lb_logits, gout, s_all, d_out, H, comm=None):
    L = p4.shape[0]
    C = HG_CHUNK
    GR = _tile(L // C, HG_GROUP, 1)
    T = GR * C
    N = L // T

    def body(hq_ref, hf_ref, hi_ref, hg_ref, lg_ref, gout_ref, s_ref, do_ref,
             dq_ref, df_ref, di_ref, dg_ref, dlb_ref, dgo_ref, dst_ref):
        @pl.when(pl.program_id(1) == 0)
        def _():
            dst_ref[...] = jnp.zeros_like(dst_ref)
            dlb_ref[...] = jnp.zeros_like(dlb_ref)

        @pl.when(jnp.logical_and(pl.program_id(0) == 0, pl.program_id(1) == 0))
        def _():
            dgo_ref[...] = jnp.zeros_like(dgo_ref)

        lb = _lower_bound(lg_ref[...])
        dst = dst_ref[...]
        d_lb = jnp.zeros((1, HG_DK), F32)
        d_go = jnp.zeros((1, HG_DK), F32)
        for ci in reversed(range(GR)):
            rows = pl.ds(ci * C, C)
            dst, d_lb_c, d_go_c = chunk_bwd(rows, lb, s_ref[0, ci], dst, hq_ref, hf_ref, hi_ref, hg_ref, gout_ref, do_ref,
                                            dq_ref, df_ref, di_ref, dg_ref)
            d_lb += d_lb_c
            d_go += d_go_c
        dst_ref[...] = dst
        dlb_ref[...] += d_lb
        dgo_ref[...] += d_go

    def chunk_bwd(rows, lb, st, dst_next, hq_ref, hf_ref, hi_ref, hg_ref, gout_ref, do_ref, dq_ref, df_ref, di_ref, dg_ref):
        hq, hf, hi, hgate = hq_ref[rows, :], hf_ref[rows, :], hi_ref[rows, :], hg_ref[rows, :]
        o, _, t = _hg_chunk(hq, hf, hi, lb, st)
        _, out_vjp = jax.vjp(_hg_out, o, hgate, gout_ref[...])
        do, d_hgate, d_gout = out_vjp(do_ref[rows, :])
        tri = t["tri"]
        dsc = jnp.where(tri, _bdot(do, hi, NT), 0.0)
        dv = _bdot(t["sc"], do, TN) + _bdot(t["kd"], dst_next, NT)
        dqe = _bdot(dsc, t["ke"], NN)
        dke = _bdot(dsc, t["qe"], TN)
        dqb = _bdot(do, st, NN)
        dkd = _bdot(hi, dst_next, NN)
        ddec = jnp.sum(dst_next * st, axis=0, keepdims=True)
        dst_prev = _bdot(do, t["qb"], TN) + dst_next * t["dec"]
        dq = dqe * t["e_qm"] + dqb * t["e_q"]
        dk = dke * t["e_km"] + dkd * t["e_kl"]
        tq, tk, td, tb = dqe * t["qe"], dke * t["ke"], dkd * t["kd"], dqb * t["qb"]
        db = tq - tk - td + tb
        dm = jnp.sum(tk - tq, axis=0, keepdims=True)
        dbl = jnp.sum(td, axis=0, keepdims=True) + ddec * t["dec"]
        rowi = lax.broadcasted_iota(jnp.int32, (C, HG_DK), 0)
        db = db + jnp.where(rowi == C // 2 - 1, dm, 0.0) + jnp.where(rowi == C - 1, dbl, 0.0)
        dlf = _dot(tri.astype(F32), db, TN, precision=HIGHEST)
        df = dlf / t["f"] - dk
        sg = t["sg"]
        df_ref[rows, :] = (df * (1.0 - lb) * sg * (1.0 - sg)).astype(df_ref.dtype)
        sq = jax.nn.sigmoid(hq)
        dq_ref[rows, :] = (dq * (sq * (1.0 + hq * (1.0 - sq)))).astype(dq_ref.dtype)
        di_ref[rows, :] = dv.astype(di_ref.dtype)
        dg_ref[rows, :] = d_hgate.astype(dg_ref.dtype)
        return dst_prev, jnp.sum(df * (1.0 - sg), axis=0, keepdims=True), d_gout

    blk = lambda s: pl.BlockSpec((T, HG_DK), functools.partial(lambda h, n, s: (N - 1 - n, s * H + h), s=s))
    oblk = pl.BlockSpec((T, HG_DK), lambda h, n: (N - 1 - n, h))
    vec = pl.BlockSpec((1, HG_DK), lambda h, n: (0, h))
    W = H * HG_DK
    return _call(
        body, [p4, p4, p4, p4, lb_logits, gout, s_all, d_out], name="hgrn_bwd", grid=(H, N),
        out_shape=tuple([jax.ShapeDtypeStruct((L, W), BF16)] * 4 + [jax.ShapeDtypeStruct((1, W), F32), jax.ShapeDtypeStruct((1, HG_DK), F32)]),
        in_specs=[blk(0), blk(1), blk(2), blk(3), pl.BlockSpec((2, HG_DK), lambda h, n: (0, h)),
                  pl.BlockSpec((1, HG_DK), lambda h, n: (0, 0)),
                  pl.BlockSpec((1, GR, HG_DK, HG_DK), lambda h, n: (h, N - 1 - n, 0, 0)), oblk],
        out_specs=(oblk, oblk, oblk, oblk, vec, pl.BlockSpec((1, HG_DK), lambda h, n: (0, 0))),
        scratch_shapes=[pltpu.VMEM((HG_DK, HG_DK), F32)], comm=comm)


def _bucket_ids():
    i = jnp.arange(AT_BLOCK, dtype=jnp.int32)[:, None]
    j = jnp.arange(2 * AT_BLOCK, dtype=jnp.int32)[None, :]
    n = jnp.maximum(i - j + AT_BLOCK, 0)
    nf = jnp.maximum(n, 1).astype(F32)
    large = MAX_EXACT + (jnp.log(nf / MAX_EXACT) / math.log(MAX_DISTANCE / MAX_EXACT) * (N_BUCKETS - MAX_EXACT)).astype(jnp.int32)
    large = jnp.minimum(large, N_BUCKETS - 1)
    return jnp.where(n < MAX_EXACT, n, large).reshape(1, -1)


def _onehot(bucket):
    ids = lax.broadcasted_iota(jnp.int32, (N_BUCKETS, bucket.shape[1]), 0)
    return (ids == bucket).astype(F32)


def _attn_probs(qn, kpn, kcn, bias_g, sink, first, scale):
    rows = qn.shape[0]
    i = jnp.bitwise_and(lax.broadcasted_iota(jnp.int32, (rows, AT_BLOCK), 0), AT_BLOCK - 1)
    j = lax.broadcasted_iota(jnp.int32, (rows, AT_BLOCK), 1)
    lp = _bdot(qn, kpn, NT) * scale + bias_g[:, :AT_BLOCK]
    lc = _bdot(qn, kcn, NT) * scale + bias_g[:, AT_BLOCK:]
    lp = jnp.where(jnp.logical_and(j > i, jnp.logical_not(first)), lp, NEG_INF)
    lc = jnp.where(j <= i, lc, NEG_INF)
    m = jnp.maximum(jnp.maximum(jnp.max(lp, axis=-1, keepdims=True), jnp.max(lc, axis=-1, keepdims=True)), sink)
    pp, pc, ps = jnp.exp(lp - m), jnp.exp(lc - m), jnp.exp(sink - m)
    den = jnp.sum(pp, axis=-1, keepdims=True) + jnp.sum(pc, axis=-1, keepdims=True) + ps
    return pp / den, pc / den, ps / den


def _sink_rows(sk_ref, G):
    head = lax.broadcasted_iota(jnp.int32, (G * AT_BLOCK, 1), 0) // AT_BLOCK
    sink = jnp.zeros((G * AT_BLOCK, 1), F32)
    for g in range(G):
        sink = jnp.where(head == g, sk_ref[0, g:g + 1, :], sink)
    return sink


def _attn_fwd(q_t, kp, vp, qg, kg, sinks, bias, KVH, comm=None):
    AH, L, DH = q_t.shape
    G = AH // KVH
    NB = L // AT_BLOCK
    scale = DH ** -0.5

    def body(q_ref, kp_ref, kc_ref, vp_ref, vc_ref, qg_ref, kg_ref, sk_ref, b_ref, o_ref):
        first = pl.program_id(1) == 0
        kpn, kcn = _rms(kp_ref[0], kg_ref[...]), _rms(kc_ref[0], kg_ref[...])
        qn = _rms(q_ref[...].reshape(G * AT_BLOCK, DH), qg_ref[...])
        sink = _sink_rows(sk_ref, G)
        pp, pc, _ = _attn_probs(qn, kpn, kcn, b_ref[...].reshape(G * AT_BLOCK, 2 * AT_BLOCK), sink, first, scale)
        o = _bdot(pp, vp_ref[0], NN) + _bdot(pc, vc_ref[0], NN)
        o_ref[...] = o.reshape(G, AT_BLOCK, DH).astype(o_ref.dtype)

    kblk = lambda off: pl.BlockSpec((1, AT_BLOCK, DH), functools.partial(lambda h, n, off: (h, n + off, 0), off=off))
    return _call(
        body, [q_t, kp, kp, vp, vp, qg, kg, sinks, bias], name="attn_fwd", grid=(KVH, NB),
        out_shape=jax.ShapeDtypeStruct((AH, L, DH), BF16),
        in_specs=[pl.BlockSpec((G, AT_BLOCK, DH), lambda h, n: (h, n, 0)), kblk(0), kblk(1), kblk(0), kblk(1),
                  pl.BlockSpec((1, DH), lambda h, n: (0, 0)), pl.BlockSpec((1, DH), lambda h, n: (0, 0)),
                  pl.BlockSpec((1, G, 1), lambda h, n: (h, 0, 0)),
                  pl.BlockSpec((G, AT_BLOCK, 2 * AT_BLOCK), lambda h, n: (h, 0, 0))],
        out_specs=pl.BlockSpec((G, AT_BLOCK, DH), lambda h, n: (h, n, 0)), comm=comm)


def _attn_bwd(q_t, kp, vp, qg, kg, sinks, bias, do_t, KVH, comm=None):
    AH, L, DH = q_t.shape
    G = AH // KVH
    NB = L // AT_BLOCK
    B = AT_BLOCK
    scale = DH ** -0.5

    def body(q_ref, kp_ref, kc_ref, vp_ref, vc_ref, qg_ref, kg_ref, sk_ref, b_ref, do_ref,
             dq_ref, dk_ref, dv_ref, dqg_ref, dkg_ref, dsk_ref, db_ref):
        n = pl.program_id(1)
        first = n == 0

        @pl.when(first)
        def _():
            for r in (dk_ref, dv_ref, dsk_ref, db_ref):
                r[...] = jnp.zeros_like(r)

        @pl.when(jnp.logical_and(first, pl.program_id(0) == 0))
        def _():
            dqg_ref[...] = jnp.zeros_like(dqg_ref)
            dkg_ref[...] = jnp.zeros_like(dkg_ref)

        kp_raw, kc_raw, kgv, qgv = kp_ref[0], kc_ref[0], kg_ref[...], qg_ref[...]
        kpn, kp_vjp = jax.vjp(_rms, kp_raw, kgv)
        kcn, kc_vjp = jax.vjp(_rms, kc_raw, kgv)
        qn, q_vjp = jax.vjp(_rms, q_ref[...].reshape(G * B, DH), qgv)
        pp, pc, ps = _attn_probs(qn, kpn, kcn, b_ref[...].reshape(G * B, 2 * B), _sink_rows(sk_ref, G), first, scale)
        do = do_ref[...].reshape(G * B, DH)
        dvp = _bdot(pp, do, TN)
        dvc = _bdot(pc, do, TN)
        dpp = _bdot(do, vp_ref[0], NT)
        dpc = _bdot(do, vc_ref[0], NT)
        dsum = jnp.sum(dpp * pp, axis=-1, keepdims=True) + jnp.sum(dpc * pc, axis=-1, keepdims=True)
        dlp = pp * (dpp - dsum)
        dlc = pc * (dpc - dsum)
        dsk_ref[0] += jnp.sum((-ps * dsum).reshape(G, B, 1), axis=1)
        db_ref[:, :, :B] += dlp.reshape(G, B, B)
        db_ref[:, :, B:] += dlc.reshape(G, B, B)
        dlp, dlc = dlp * scale, dlc * scale
        dqn = _bdot(dlp, kpn, NN) + _bdot(dlc, kcn, NN)
        dq_raw, dqg = q_vjp(dqn)
        dq_ref[...] = dq_raw.reshape(G, B, DH).astype(dq_ref.dtype)
        dkp_raw, dkg_p = kp_vjp(_bdot(dlp, qn, TN))
        dkc_raw, dkg_c = kc_vjp(_bdot(dlc, qn, TN))
        r0 = pl.multiple_of(n * B, B)
        r1 = pl.multiple_of(n * B + B, B)
        dk_ref[0, pl.ds(r0, B), :] += dkp_raw
        dk_ref[0, pl.ds(r1, B), :] += dkc_raw
        dv_ref[0, pl.ds(r0, B), :] += dvp
        dv_ref[0, pl.ds(r1, B), :] += dvc
        dqg_ref[...] += dqg
        dkg_ref[...] += dkg_p + dkg_c

    kblk = lambda off: pl.BlockSpec((1, B, DH), functools.partial(lambda h, n, off: (h, n + off, 0), off=off))
    qblk = pl.BlockSpec((G, B, DH), lambda h, n: (h, n, 0))
    accblk = pl.BlockSpec((1, L + B, DH), lambda h, n: (h, 0, 0))
    vecblk = pl.BlockSpec((1, DH), lambda h, n: (0, 0))
    return _call(
        body, [q_t, kp, kp, vp, vp, qg, kg, sinks, bias, do_t], name="attn_bwd", grid=(KVH, NB),
        out_shape=(jax.ShapeDtypeStruct((AH, L, DH), BF16), jax.ShapeDtypeStruct((KVH, L + B, DH), F32),
                   jax.ShapeDtypeStruct((KVH, L + B, DH), F32), jax.ShapeDtypeStruct((1, DH), F32),
                   jax.ShapeDtypeStruct((1, DH), F32), jax.ShapeDtypeStruct((KVH, G, 1), F32),
                   jax.ShapeDtypeStruct((AH, B, 2 * B), F32)),
        in_specs=[qblk, kblk(0), kblk(1), kblk(0), kblk(1),
                  pl.BlockSpec((1, DH), lambda h, n: (0, 0)), pl.BlockSpec((1, DH), lambda h, n: (0, 0)),
                  pl.BlockSpec((1, G, 1), lambda h, n: (h, 0, 0)),
                  pl.BlockSpec((G, B, 2 * B), lambda h, n: (h, 0, 0)), qblk],
        out_specs=(qblk, accblk, accblk, vecblk, vecblk, pl.BlockSpec((1, G, 1), lambda h, n: (h, 0, 0)),
                   pl.BlockSpec((G, B, 2 * B), lambda h, n: (h, 0, 0))), comm=comm)


def _heads_first(t, nh):
    L = t.shape[0]
    return jnp.transpose(t.reshape(L, nh, t.shape[1] // nh), (1, 0, 2))


def _heads_last(t):
    nh, L, dh = t.shape
    return jnp.transpose(t, (1, 0, 2)).reshape(L, nh * dh)


def _softmax0(lg):
    e = jnp.exp(lg - jnp.max(lg, axis=0, keepdims=True))
    return e[0:1] / jnp.sum(e, axis=0, keepdims=True)


def _ada_update_call(fn, c_all, d_cols, w, m, v, rt):
    D, n = w.shape

    def body(c_ref, d_ref, w_ref, m_ref, v_ref, g_out, dl_out, m_out, v_out):
        outs, _ = fn(c_ref[...], d_ref[...], w_ref[...], m_ref[...], v_ref[...])
        for r, val in zip((g_out, dl_out, m_out, v_out), outs):
            r[...] = val

    wblk = pl.BlockSpec((rt, n), lambda i: (i, 0))
    return _call(
        body, [c_all, d_cols, w, m, v], name="update_ada", grid=(D // rt,), out_shape=tuple([jax.ShapeDtypeStruct((D, n), F32)] * 4),
        in_specs=[pl.BlockSpec((N_DEV, rt), lambda i: (0, i)), pl.BlockSpec((N_DEV, n), lambda i: (0, 0)), wblk, wblk, wblk],
        out_specs=(wblk, wblk, wblk, wblk))


def kernel(x, c, w_ada, b_ada, norm1_g, norm2_g, w_in, hg_lb_logits, hg_out_norm_g, q_norm_g, k_norm_g, attn_sinks, rel_bias_table, w_branch_hg, w_branch_attn, w_out, w_ff1, w_ff2, loss_target, m_w_ada, m_b_ada, m_norm1_g, m_norm2_g, m_w_in, m_hg_lb_logits, m_hg_out_norm_g, m_q_norm_g, m_k_norm_g, m_attn_sinks, m_rel_bias_table, m_w_branch_hg, m_w_branch_attn, m_w_out, m_w_ff1, m_w_ff2, v_w_ada, v_b_ada, v_norm1_g, v_norm2_g, v_w_in, v_hg_lb_logits, v_hg_out_norm_g, v_q_norm_g, v_k_norm_g, v_attn_sinks, v_rel_bias_table, v_w_branch_hg, v_w_branch_attn, v_w_out, v_w_ff1, v_w_ff2):
    cc = lax.axis_index("c")
    me = 4 * lax.axis_index("x") + 2 * lax.axis_index("y") + cc
    x2 = x[0]
    tgt = loss_target[0]
    L, D = x2.shape
    HGW = hg_lb_logits.shape[1]
    H = HGW // HG_DK
    AH = attn_sinks.shape[1]
    DH = q_norm_g.shape[1]
    ATW = AH * DH
    BW = w_in.shape[2]
    INW = BW * N_DEV
    A = BW // LANES
    assert BW == LANES * A + LANES // 2
    KVW = (INW - 4 * HGW - ATW - 2 * D) // 2
    KVH = KVW // DH
    G = AH // KVH
    ADA_N = w_ada.shape[2]
    PAIR = 2 * A + 1

    c_all = _gather_small(c, me, "gather_c")[:, 0, :]
    b_cols = lax.dynamic_slice(b_ada, (0, me * ADA_N), (1, ADA_N))
    (ada_cols,) = _whole(lambda cv, w, b: (_bdot(_silu(cv), w, NN) + b,), [c_all, w_ada[0], b_cols],
                         [((N_DEV, ADA_N), F32)], "ada_fwd")
    ada_all = _gather_small(ada_cols, me, "gather_ada")
    ada_row = lax.dynamic_slice(ada_all, (0, me, 0), (N_DEV, 1, ADA_N)).reshape(1, 6 * D)

    w_in_b = w_in[0].astype(BF16)
    src_in = jnp.where(cc == 0, jnp.pad(w_in_b, ((0, 0), (0, LANES // 2))), jnp.pad(w_in_b, ((0, 0), (LANES // 2, 0))))
    (src_in,) = _behind([src_in], [ada_row])
    shift1, scale1, gate1, shift2, scale2, gate2 = [ada_row[:, i * D:(i + 1) * D] for i in range(6)]
    w_in_gapped, w_in_mid = _ag_w_in(src_in, A, D, INW)
    w_in_full = _patch_mid(w_in_gapped, w_in_mid, A)

    wnames = ("bhg", "bat", "out", "ff1", "ff2")
    small = ("bhg", "bat", "out")
    waxis = dict(zip(wnames, (1, 1, 0, 1, 0)))
    wsrc = dict(zip(wnames, (w_branch_hg, w_branch_attn, w_out, w_ff1, w_ff2)))
    wblk = {k: wsrc[k][0].astype(BF16) for k in wnames}
    wf = {}

    ag_small = _ag_split_start(_behind([wblk[k] for k in small], [w_in_full]), [waxis[k] for k in small], "ag_small_start")
    (x2_,) = _behind([x2], [ag_small["token"]])
    (h,) = _rowwise(lambda xv, g, sh, sc: ((_modnorm(xv, g, sh, sc),), ()), [(x2_, D, 0)], [norm1_g, shift1, scale1],
                    [(D, BF16)], [], "norm1")
    o4, oa = 4 * HGW, 4 * HGW + ATW + 2 * KVW
    p4 = _mm(h, w_in_full, "nn", F32, "proj_hg", n=o4)
    pa = _mm(h, w_in_full, "nn", F32, "proj_at", b_off=o4, n=oa - o4)
    small_blks, small_fulls = _ag_split_wait(ag_small, [p4, pa], "ag_small_wait")
    cm = _Comm()
    hs = {k: _ag_d2d_own(cm, small_fulls[i], small_blks[i], waxis[k]) for i, k in enumerate(small)}
    pg = _mm(h, w_in_full, "nn", F32, "proj_gate", b_off=oa, n=INW - oa, comm=cm)
    for k in small:
        wf[k] = cm.result(hs[k])

    ag_ff1 = _ag_split_start(_behind([wblk["ff1"]], [pg]), [waxis["ff1"]], "ag_ff1_start")
    p4, pa = _behind([p4, pa], [ag_ff1["token"]])
    o_hg, s_all = _hgrn_fwd(p4, hg_lb_logits, hg_out_norm_g, H)

    bucket = _bucket_ids()
    (bias_flat,) = _whole(lambda tb, bk: (_dot(tb, _onehot(bk), TN, precision=HIGHEST),), [rel_bias_table, bucket],
                          [((AH, AT_BLOCK * 2 * AT_BLOCK), F32)], "bias_fwd")
    bias = bias_flat.reshape(AH, AT_BLOCK, 2 * AT_BLOCK)
    q_t = _heads_first(pa[:, :ATW], AH)
    pad = lambda t: jnp.pad(t, ((0, 0), (AT_BLOCK, 0), (0, 0)))
    kp = pad(_heads_first(pa[:, ATW:ATW + KVW], KVH))
    vp = pad(_heads_first(pa[:, ATW + KVW:], KVH))
    sinks3 = attn_sinks.reshape(KVH, G, 1)
    o_at = _heads_last(_attn_fwd(q_t, kp, vp, q_norm_g, k_norm_g, sinks3, bias, KVH))
    (blk_ff1,), (full_ff1,) = _ag_split_wait(ag_ff1, [o_hg, o_at], "ag_ff1_wait")

    cm = _Comm()
    hs = {"ff1": _ag_d2d_own(cm, full_ff1, blk_ff1, waxis["ff1"])}
    bh = _mm(o_hg, wf["bhg"], "nn", F32, "branch_hg", comm=cm)
    wf["ff1"] = cm.result(hs["ff1"])
    ag_ff2 = _ag_split_start(_behind([wblk["ff2"]], [bh]), [waxis["ff2"]], "ag_ff2_start")
    (o_at,) = _behind([o_at], [ag_ff2["token"]])
    ba = _mm(o_at, wf["bat"], "nn", F32, "branch_at")

    def merge_fn(bhv, bav, ghg, gat):
        return jax.nn.sigmoid(ghg) * bhv + jax.nn.sigmoid(gat) * bav

    (merged,) = _rowwise(lambda *a: ((merge_fn(*a),), ()), [(bh, D, 0), (ba, D, 0), (pg, D, 0), (pg, D, 1)], [],
                         [(D, BF16)], [], "merge")
    mo = _mm(merged, wf["out"], "nn", F32, "out_proj")

    def resid1(xv, mov, g1, g2n, sh, sc):
        x1v = xv + g1 * mov
        return (x1v, _modnorm(x1v, g2n, sh, sc)), ()

    x1, h2 = _rowwise(resid1, [(x2, D, 0), (mo, D, 0)], [gate1, norm2_g, shift2, scale2], [(D, F32), (D, BF16)], [], "resid1")
    u, act = _mm(h2, wf["ff1"], "nn", (F32, BF16), "ff1", epi=lambda r: (r, jnp.square(jnp.maximum(r, 0.0))))
    (blk_ff2,), (full_ff2,) = _ag_split_wait(ag_ff2, [act], "ag_ff2_wait")
    cm = _Comm()
    hs = {"ff2": _ag_d2d_own(cm, full_ff2, blk_ff2, waxis["ff2"])}
    _call(lambda: None, [], name="ag_d2d_ff2", out_shape=(), comm=cm)
    wf["ff2"] = cm.result(hs["ff2"])
    ff = _mm(act, wf["ff2"], "nn", F32, "ff2")

    def loss_fn(x1v, ffv, tv, g2):
        e = x1v + g2 * ffv - tv
        dy = e * (1.0 / D)
        return (dy, dy * g2), (jnp.sum(e * e, axis=0, keepdims=True), jnp.sum(dy * ffv, axis=0, keepdims=True))

    dy, d_ff, sq_sum, d_gate2 = _rowwise(loss_fn, [(x1, D, 0), (ff, D, 0), (tgt, D, 0)], [gate2],
                                         [(D, F32), (D, BF16)], [(1, D), (1, D)], "loss")
    loss = lax.psum(jnp.sum(sq_sum) * (0.5 / D), ("x", "y", "c"))

    owner_base = jnp.stack([me ^ r for r in CHIP_RELS]).astype(jnp.int32)
    gw, recv1, part, recv2 = {}, {}, {}, {}
    gw["ff2"] = _mm(act, d_ff, "tn", BF16, "dw_ff2")
    cm = _Comm()
    hh = _rs_d2d(cm, gw["ff2"], waxis["ff2"])
    d_u = _mm(d_ff, wf["ff2"], "nt", BF16, "d_act", comm=cm, extras=[u], epi=lambda r, uv: (r * (2.0 * jnp.maximum(uv, 0.0)),))
    part["ff2"] = _rs_add(gw["ff2"], cm.result(hh), waxis["ff2"], owner_base, "rs_add_ff2")
    rows_ff2 = part["ff2"].shape[1]
    cm = _Comm()
    hh = _rs_ici(cm, part["ff2"], rows=(0, rows_ff2 // 2))
    gw["ff1"] = _mm(h2, d_u, "tn", BF16, "dw_ff1", comm=cm)
    cm2 = _Comm()
    hh2 = _rs_ici(cm2, part["ff2"], rows=(rows_ff2 // 2, rows_ff2), recv=cm.result(hh))
    hh1 = _rs_d2d(cm2, gw["ff1"], waxis["ff1"])
    d_h2 = _mm(d_u, wf["ff1"], "nt", F32, "d_h2", comm=cm2)
    recv2["ff2"] = cm2.result(hh2)
    part["ff1"] = _rs_add(gw["ff1"], cm2.result(hh1), waxis["ff1"], owner_base, "rs_add_ff1")

    def norm2_bwd(dh2v, x1v, dyv, mov, g2n, sh, sc, g1):
        _, vjp = jax.vjp(_modnorm, x1v, g2n, sh, sc)
        dx, dg, dsh, dsc = vjp(dh2v)
        dx1 = dyv + dx
        return (dx1, dx1 * g1), (dg, dsh, dsc, jnp.sum(dx1 * mov, axis=0, keepdims=True))

    d_x1, d_mo, d_g2n, d_shift2, d_scale2, d_gate1 = _rowwise(
        norm2_bwd, [(d_h2, D, 0), (x1, D, 0), (dy, D, 0), (mo, D, 0)], [norm2_g, shift2, scale2, gate1],
        [(D, F32), (D, BF16)], [(1, D)] * 4, "norm2_bwd")
    gw["out"] = _mm(merged, d_mo, "tn", BF16, "dw_out")
    cm = _Comm()
    hh = _rs_d2d(cm, gw["out"], waxis["out"])
    d_merged = _mm(d_mo, wf["out"], "nt", F32, "d_merged", comm=cm)
    part["out"] = _rs_add(gw["out"], cm.result(hh), waxis["out"], owner_base, "rs_add_out")

    def merge_bwd(dmv, bhv, bav, ghg, gat):
        _, vjp = jax.vjp(merge_fn, bhv, bav, ghg, gat)
        return vjp(dmv), ()

    d_bh, d_ba, d_ghg, d_gat = _rowwise(merge_bwd, [(d_merged, D, 0), (bh, D, 0), (ba, D, 0), (pg, D, 0), (pg, D, 1)], [],
                                        [(D, BF16)] * 4, [], "merge_bwd")
    gw["bhg"] = _mm(o_hg, d_bh, "tn", BF16, "dw_bhg")
    gw["bat"] = _mm(o_at, d_ba, "tn", BF16, "dw_bat")
    cm = _Comm()
    hh = {k: _rs_d2d(cm, gw[k], waxis[k]) for k in ("bhg", "bat")}
    d_ohg = _mm(d_bh, wf["bhg"], "nt", F32, "d_ohg", comm=cm)
    for k in ("bhg", "bat"):
        part[k] = _rs_add(gw[k], cm.result(hh[k]), waxis[k], owner_base, "rs_add_" + k)
    d_oat = _mm(d_ba, wf["bat"], "nt", BF16, "d_oat")

    cm = _Comm()
    hh = {"ff1": _rs_ici(cm, part["ff1"])}
    d_hq, d_hf, d_hi, d_hg, d_lb, d_gout_h = _hgrn_bwd(p4, hg_lb_logits, hg_out_norm_g, s_all, d_ohg, H, comm=cm)
    recv2["ff1"] = cm.result(hh["ff1"])
    cm = _Comm()
    hh = {k: _rs_ici(cm, part[k]) for k in small}
    dq_t, dkp, dvp, d_qg, d_kg, d_sk, d_bias = _attn_bwd(q_t, kp, vp, q_norm_g, k_norm_g, sinks3, bias,
                                                         _heads_first(d_oat, AH), KVH, comm=cm)
    for k in hh:
        recv2[k] = cm.result(hh[k])
    d_aq = _heads_last(dq_t)
    d_ak = _heads_last(dkp[:, AT_BLOCK:, :]).astype(BF16)
    d_av = _heads_last(dvp[:, AT_BLOCK:, :]).astype(BF16)
    d_proj = jnp.concatenate([d_hq, d_hf, d_hi, d_hg, d_aq, d_ak, d_av, d_ghg, d_gat], axis=1)
    gw_in = _mm(h, d_proj, "tn", BF16, "dw_in")

    wm = LANES * A
    cm = _Comm()
    hi_ = cm.inp(gw_in)
    h_main, h_mid = cm.out((4, D, wm), BF16), cm.out((4, D, LANES), BF16)
    for i, r in enumerate(CHIP_RELS):
        def main_view(ref, p, r=r):
            o = p["me"] ^ r ^ 1
            return ref.at[:, pl.ds(pl.multiple_of((PAIR * (o // 2) + (A + 1) * (1 - p["c"])) * LANES, LANES), wm)]

        def mid_view(ref, p, r=r):
            o = p["me"] ^ r
            return ref.at[:, pl.ds(pl.multiple_of((PAIR * (o // 2) + A) * LANES, LANES), LANES)]

        cm.copy(hi_, main_view, h_main, _slot_view(i), 1)
        cm.copy(hi_, mid_view, h_mid, _slot_view(i), 1)
    _call(lambda: None, [], name="rs_d2d_in", out_shape=(), comm=cm)
    chip = jnp.stack([(me ^ r) // 2 for r in CHIP_RELS]).astype(jnp.int32)
    part_main = _rs_add(gw_in, cm.result(h_main), 1, PAIR * chip + (A + 1) * cc, "rs_add_in_main", tw=LANES)
    part_mid = _rs_add(gw_in, cm.result(h_mid), 1, PAIR * chip + A, "rs_add_in_mid", tw=LANES)
    rs_in = _rs_split_start([part_main, part_mid], "rs_in_start")
    d_h = _mm(d_proj, w_in_full, "nt", F32, "d_h", after=[rs_in["token"]])

    def norm1_bwd(dhv, xv, dx1v, g1n, sh, sc):
        _, vjp = jax.vjp(_modnorm, xv, g1n, sh, sc)
        dx, dg, dsh, dsc = vjp(dhv)
        return (dx1v + dx,), (dg, dsh, dsc)

    grad_x, d_g1n, d_shift1, d_scale1 = _rowwise(norm1_bwd, [(d_h, D, 0), (x2, D, 0), (d_x1, D, 0)],
                                                 [norm1_g, shift1, scale1], [(D, F32)], [(1, D)] * 3, "norm1_bwd")

    def sum4(p0, p1, p2, p3):
        return ((p0.astype(F32) + p1.astype(F32)) + p2.astype(F32)) + p3.astype(F32)

    def update_fn(w, m, v, p0, p1, p2, p3):
        g = sum4(p0, p1, p2, p3)
        delta, mn, vn = _adamw(w, g, m, v)
        return (g, delta, mn, vn), ()

    wmv = dict(zip(wnames, ((w_branch_hg, m_w_branch_hg, v_w_branch_hg), (w_branch_attn, m_w_branch_attn, v_w_branch_attn),
                            (w_out, m_w_out, v_w_out), (w_ff1, m_w_ff1, v_w_ff1), (w_ff2, m_w_ff2, v_w_ff2))))
    res = {}

    def update(k, p, rx):
        w, m, v = (t[0] for t in wmv[k])
        n = w.shape[1]
        ins = [(t, n, 0) for t in (w, m, v)] + [(p, n, 0, 0)] + [(rx, n, 0, i) for i in range(3)]
        res[k] = [t[None] for t in _rowwise(update_fn, ins, [], [(n, F32)] * 4, [], "update_" + k)]

    for k in wnames:
        update(k, part[k], recv2[k])
    (part_main, part_mid), (rx_main, rx_mid) = _rs_split_wait(rs_in, [grad_x] + [res[k][0] for k in wnames], "rs_in_wait")
    g_main, = _rowwise(lambda *p: ((sum4(*p),), ()), [(part_main, wm, 0, 0)] + [(rx_main, wm, 0, i) for i in range(3)], [],
                       [(wm, F32)], [], "sum_in_main")
    g_mid, = _rowwise(lambda *p: ((sum4(*p),), ()), [(part_mid, LANES, 0, 0)] + [(rx_mid, LANES, 0, i) for i in range(3)], [],
                      [(LANES, F32)], [], "sum_in_mid")
    g_in = jnp.where(cc == 0, jnp.concatenate([g_main, g_mid[:, :LANES // 2]], axis=1),
                     jnp.concatenate([g_mid[:, LANES // 2:], g_main], axis=1))

    def update_given(w, m, v, g):
        delta, mn, vn = _adamw(w, g, m, v)
        return (g, delta, mn, vn), ()

    res["in"] = [t[None] for t in _rowwise(update_given, [(t, BW, 0) for t in (w_in[0], m_w_in[0], v_w_in[0], g_in)], [],
                                           [(BW, F32)] * 4, [], "update_in")]

    d_ada_row = jnp.concatenate([d_shift1, d_scale1, d_gate1, d_shift2, d_scale2, d_gate2], axis=1)
    (d_ada_row,) = _behind([d_ada_row], [g_mid])
    d_ada_all = _gather_small(d_ada_row, me, "gather_dada")[:, 0, :]
    d_ada_cols = lax.dynamic_slice(d_ada_all, (0, me * ADA_N), (N_DEV, ADA_N))

    def ada_update(cv, dav, w, m, v):
        g = _bdot(_silu(cv), dav, TN)
        delta, mn, vn = _adamw(w, g, m, v)
        return (g, delta, mn, vn), ()

    res["ada"] = [t[None] for t in _ada_update_call(ada_update, c_all, d_ada_cols, w_ada[0], m_w_ada[0], v_w_ada[0], _tile(D, 256, 16))]

    d_sinks = d_sk.reshape(1, AH)
    (d_table_t,) = _whole(lambda db, bk: (_dot(db, _onehot(bk), NT, precision=HIGHEST),),
                          [d_bias.reshape(AH, AT_BLOCK * 2 * AT_BLOCK), bucket], [((AH, N_BUCKETS), F32)], "bias_bwd")
    smalls = [d_g1n, d_g2n, d_lb, d_gout_h, d_qg, d_kg, d_sinks, d_table_t.T.reshape(1, N_BUCKETS * AH)]
    widths = [s.shape[1] for s in smalls]
    lanes = [-(-w // LANES) * LANES for w in widths]
    smalls = [jnp.pad(s, ((0, 0), (0, p - w))) for s, w, p in zip(smalls, widths, lanes)]
    (smalls_row,) = _behind([jnp.concatenate(smalls, axis=1)], [g_mid])
    packed = _gather_small(smalls_row, me, "gather_small")[:, 0, :]
    offs = [sum(lanes[:i]) for i in range(len(lanes))]

    def small_update(pk, dada, lg, *wmv_flat):
        tot = pk[0:1]
        for d in range(1, N_DEV):
            tot = tot + pk[d:d + 1]
        gb = dada[0:1]
        for d in range(1, N_DEV):
            gb = gb + dada[d:d + 1]
        gs = [tot[:, offs[i]:offs[i] + widths[i]] for i in range(len(widths))]
        _, lb_vjp = jax.vjp(_softmax0, lg)
        (g_lg,) = lb_vjp(gs[2])
        grads = [gb, gs[0], gs[1], g_lg, gs[3], gs[4], gs[5], gs[6], gs[7]]
        outs = []
        for i, g in enumerate(grads):
            w, m, v = wmv_flat[3 * i:3 * i + 3]
            delta, mn, vn = _adamw(w, g, m, v)
            outs += [g, delta, mn, vn]
        return tuple(outs)

    tbl = lambda t: t.reshape(1, N_BUCKETS * AH)
    small_wmv = [(b_ada, m_b_ada, v_b_ada), (norm1_g, m_norm1_g, v_norm1_g), (norm2_g, m_norm2_g, v_norm2_g),
                 (hg_lb_logits, m_hg_lb_logits, v_hg_lb_logits), (hg_out_norm_g, m_hg_out_norm_g, v_hg_out_norm_g),
                 (q_norm_g, m_q_norm_g, v_q_norm_g), (k_norm_g, m_k_norm_g, v_k_norm_g),
                 (attn_sinks, m_attn_sinks, v_attn_sinks),
                 (tbl(rel_bias_table), tbl(m_rel_bias_table), tbl(v_rel_bias_table))]
    flat = [t for trip in small_wmv for t in trip]
    out_shapes = [(trip[0].shape, F32) for trip in small_wmv for _ in range(4)]
    sres = _whole(small_update, [packed, d_ada_all, hg_lb_logits] + flat, out_shapes, "small_update")
    names_small = ("b_ada", "norm1_g", "norm2_g", "lb", "gout", "qg", "kg", "sinks", "table")
    for i, k in enumerate(names_small):
        r = sres[4 * i:4 * i + 4]
        if k == "table":
            r = [t.reshape(N_BUCKETS, AH) for t in r]
        res[k] = r

    order = ("ada", "b_ada", "norm1_g", "norm2_g", "in", "lb", "gout", "qg", "kg", "sinks", "table", "bhg", "bat", "out", "ff1", "ff2")
    outs = [loss, grad_x[None]]
    for j in range(4):
        outs += [res[k][j] for k in order]
    return tuple(outs)
```

```python
import functools
import math

import jax
import jax.numpy as jnp
from jax import lax
from jax.experimental import pallas as pl
from jax.experimental.pallas import tpu as pltpu

F32 = jnp.float32
BF16 = jnp.bfloat16
EPS = 1e-6
NEG_INF = -1e30
HG_DK = 128
HG_CHUNK = 64
AT_BLOCK = 128
N_BUCKETS = 32
MAX_EXACT = 16
MAX_DISTANCE = 128
N_DEV = 8
LANES = 128
VMEM_LIMIT = 56 * 1024 * 1024
ADAM_LR, ADAM_B1, ADAM_B2, ADAM_EPS, ADAM_WD, ADAM_STEP = 0.001, 0.9, 0.999, 1e-08, 0.01, 10
HIGHEST = lax.Precision.HIGHEST
MESH = pl.DeviceIdType.MESH
ANY = pl.BlockSpec(memory_space=pl.ANY)
CHIP_RELS = (0, 4, 2, 6)

NN = (((1,), (0,)), ((), ()))
NT = (((1,), (1,)), ((), ()))
TN = (((0,), (0,)), ((), ()))


def _tile(n, pref, unit):
    if n <= pref:
        return n
    t = (pref // unit) * unit
    while t >= unit:
        if n % t == 0:
            return t
        t -= unit
    return n


def _dot(a, b, dn, precision=None):
    return lax.dot_general(a, b, dn, preferred_element_type=F32, precision=precision)


def _bdot(a, b, dn):
    return _dot(a.astype(BF16), b.astype(BF16), dn)


def _position():
    x, y, c = lax.axis_index("x"), lax.axis_index("y"), lax.axis_index("c")
    return dict(x=x, y=y, c=c, me=4 * x + 2 * y + c)


def _peer_position(p, rel):
    x = 1 - p["x"] if rel & 4 else p["x"]
    y = 1 - p["y"] if rel & 2 else p["y"]
    c = 1 - p["c"] if rel & 1 else p["c"]
    return dict(x=x, y=y, c=c, me=4 * x + 2 * y + c)


class _Comm:
    def __init__(self):
        self.ins, self.outs, self.alias, self.plans, self.res = [], [], {}, [], None

    def inp(self, arr):
        self.ins.append(arr)
        return ("i", len(self.ins) - 1)

    def out(self, shape, dtype, alias=None):
        self.outs.append(jax.ShapeDtypeStruct(tuple(shape), dtype))
        if alias is not None:
            self.alias[alias[1]] = len(self.outs) - 1
        return ("o", len(self.outs) - 1)

    def copy(self, src, src_view, dst, dst_view, rel):
        self.plans.append((src, src_view, dst, dst_view, rel))

    def result(self, handle):
        return self.res[handle[1]]

    def build(self, in_refs, out_refs, send_sems, recv_sems):
        pos = _position()
        ref = lambda h: in_refs[h[1]] if h[0] == "i" else out_refs[h[1]]
        ops = []
        for k, (src, sv, dst, dv, rel) in enumerate(self.plans):
            s = sv(ref(src), pos)
            if rel == 0:
                cp = pltpu.make_async_copy(s, dv(ref(dst), pos), send_sems.at[k])
                ops.append((cp.start, cp.wait))
                continue
            peer = _peer_position(pos, rel)
            mk = lambda d: pltpu.make_async_remote_copy(
                src_ref=s, dst_ref=d, send_sem=send_sems.at[k], recv_sem=recv_sems.at[k],
                device_id=(peer["x"], peer["y"], peer["c"]), device_id_type=MESH)
            out_cp, in_cp = mk(dv(ref(dst), pos)), mk(dv(ref(dst), peer))

            def wait(out_cp=out_cp, in_cp=in_cp):
                out_cp.wait_send()
                in_cp.wait_recv()

            ops.append((out_cp.start, wait))
        return ops


def _call(body, args, *, name, out_shape, in_specs=None, out_specs=None, grid=None, scratch_shapes=(), comm=None,
          prefetch=None, aliases=None, after=()):
    single = not isinstance(out_shape, (tuple, list))
    out_shape = (out_shape,) if single else tuple(out_shape)
    n_in, n_out, n_scr = len(args), len(out_shape), len(scratch_shapes)
    vm = pl.BlockSpec(memory_space=pltpu.VMEM)
    in_specs = [vm] * n_in if in_specs is None else list(in_specs)
    out_specs = [vm] * n_out if out_specs is None else (list(out_specs) if isinstance(out_specs, (tuple, list)) else [out_specs])
    n_pf = 0 if prefetch is None else len(prefetch)
    kw = {} if aliases is None else {"input_output_aliases": dict(aliases)}
    if comm is None and after:
        n_dep = len(after)

        def fn(*refs):
            body(*refs[:n_pf + n_in], *refs[n_pf + n_in + n_dep:])

        all_args, all_scratch = list(args) + list(after), list(scratch_shapes)
        in_specs = in_specs + [ANY] * n_dep
    elif comm is None:
        fn = body
        all_args, all_scratch = list(args), list(scratch_shapes)
    else:
        n_ci, n_co, n_x = len(comm.ins), len(comm.outs), len(comm.plans)

        def fn(*refs):
            pf, refs = refs[:n_pf], refs[n_pf:]
            o_in, c_in = refs[:n_in], refs[n_in:n_in + n_ci]
            o_out = refs[n_in + n_ci:n_in + n_ci + n_out]
            c_out = refs[n_in + n_ci + n_out:n_in + n_ci + n_out + n_co]
            scr = refs[n_in + n_ci + n_out + n_co:]
            ops = comm.build(c_in, c_out, scr[n_scr], scr[n_scr + 1])
            if grid:
                first = functools.reduce(jnp.logical_and, [pl.program_id(i) == 0 for i in range(len(grid))])
                last = functools.reduce(jnp.logical_and, [pl.program_id(i) == g - 1 for i, g in enumerate(grid)])

                @pl.when(first)
                def _():
                    for start, _w in ops:
                        start()
            else:
                for start, _w in ops:
                    start()
            body(*pf, *o_in, *o_out, *scr[:n_scr])
            if grid:
                @pl.when(last)
                def _():
                    for _s, wait in ops:
                        wait()
            else:
                for _s, wait in ops:
                    wait()

        all_args = list(args) + list(comm.ins)
        in_specs = in_specs + [ANY] * n_ci
        out_shape = out_shape + tuple(comm.outs)
        out_specs = out_specs + [ANY] * n_co
        all_scratch = list(scratch_shapes) + [pltpu.SemaphoreType.DMA((n_x,)), pltpu.SemaphoreType.DMA((n_x,))]
        kw["input_output_aliases"] = {n_pf + n_in + i: n_out + o for i, o in comm.alias.items()}
    sem = None if grid is None else ("arbitrary",) * len(grid)
    params = pltpu.CompilerParams(dimension_semantics=sem, vmem_limit_bytes=VMEM_LIMIT)
    if prefetch is None:
        spec = dict(in_specs=in_specs, out_specs=tuple(out_specs), scratch_shapes=all_scratch)
        if grid is not None:
            spec["grid"] = grid
    else:
        spec = dict(grid_spec=pltpu.PrefetchScalarGridSpec(
            num_scalar_prefetch=n_pf, grid=grid, in_specs=in_specs, out_specs=tuple(out_specs), scratch_shapes=all_scratch))
        all_args = list(prefetch) + all_args
    res = pl.pallas_call(fn, name=name, out_shape=out_shape, compiler_params=params, **spec, **kw)(*all_args)
    res = list(res)
    if comm is not None:
        comm.res = res[n_out:]
        res = res[:n_out]
    return res[0] if single else res


def _whole_view(ref, pos):
    return ref


def _block_view(axis, n, index, rows=None):
    def view(ref, pos):
        off = pl.multiple_of(index(pos) * n, n)
        if rows is None:
            return ref.at[:, pl.ds(off, n)] if axis == 1 else ref.at[pl.ds(off, n), :]
        lo, cnt = rows[0], rows[1] - rows[0]
        if axis == 1:
            return ref.at[pl.ds(lo, cnt), pl.ds(off, n)]
        return ref.at[pl.ds(pl.multiple_of(off + lo, 16), cnt), :]
    return view


def _rows_view(rows):
    def view(ref, pos):
        return ref if rows is None else ref.at[pl.ds(rows[0], rows[1] - rows[0]), :]
    return view


def _slot_view(i, rows=None):
    def view(ref, pos):
        return ref.at[i] if rows is None else ref.at[i, pl.ds(rows[0], rows[1] - rows[0]), :]
    return view


def _exchange(items, name):
    cm = _Comm()
    for a, rel in items:
        cm.copy(cm.inp(a), _whole_view, cm.out(a.shape, a.dtype), _whole_view, rel)
    _call(lambda: None, [], name=name, out_shape=(), comm=cm)
    return cm.res


def _gather_small(v, me, name):
    cm = _Comm()
    hi, ho = cm.inp(v), cm.out((N_DEV,) + v.shape, v.dtype)
    for rel in range(N_DEV):
        cm.copy(hi, _whole_view, ho, lambda ref, p: ref.at[p["me"]], rel)
    _call(lambda: None, [], name=name, out_shape=(), comm=cm)
    return cm.result(ho)


def _ag_ici(cm, blk, axis, rows=None, into=None):
    n = blk.shape[axis]
    shape = list(blk.shape)
    shape[axis] = n * N_DEV
    hi = cm.inp(blk)
    ho = cm.out(shape, blk.dtype) if into is None else cm.out(shape, blk.dtype, alias=cm.inp(into))
    own = _block_view(axis, n, lambda p: p["me"], rows)
    for rel in CHIP_RELS:
        cm.copy(hi, _rows_view(rows), ho, own, rel)
    return ho


def _ag_d2d(cm, full, axis):
    n = full.shape[axis] // N_DEV
    hi = cm.inp(full)
    ho = cm.out(full.shape, full.dtype, alias=hi)
    for r in CHIP_RELS:
        v = _block_view(axis, n, functools.partial(lambda p, r: p["me"] ^ r, r=r))
        cm.copy(hi, v, ho, v, 1)
    return ho


def _rs_d2d(cm, gw, axis):
    n = gw.shape[axis] // N_DEV
    shape = list(gw.shape)
    shape[axis] = n
    hi, ho = cm.inp(gw), cm.out([4] + shape, gw.dtype)
    for i, r in enumerate(CHIP_RELS):
        cm.copy(hi, _block_view(axis, n, functools.partial(lambda p, r: p["me"] ^ r ^ 1, r=r)), ho, _slot_view(i), 1)
    return ho


def _rs_ici(cm, part, rows=None, recv=None):
    if recv is None:
        ho = cm.out((3,) + part.shape[1:], part.dtype)
    else:
        ho = cm.out(recv.shape, recv.dtype, alias=cm.inp(recv))
    hi = cm.inp(part)
    for i in (1, 2, 3):
        cm.copy(hi, _slot_view(i, rows), ho, _slot_view(i - 1, rows), CHIP_RELS[i])
    return ho


def _rs_add(gw, recv, axis, base, name, tw=None):
    _, R, n = recv.shape
    if axis == 1:
        tw = n if tw is None else tw
        gw_spec = pl.BlockSpec((R, tw), lambda i, t, b: (0, b[i] + t))
        rv_spec = pl.BlockSpec((None, R, tw), lambda i, t, b: (i, 0, t))
        grid = (4, n // tw)
    else:
        tw = _tile(n, 1024, LANES)
        gw_spec = pl.BlockSpec((R, tw), lambda i, t, b: (b[i], t))
        rv_spec = pl.BlockSpec((None, R, tw), lambda i, t, b: (i, 0, t))
        grid = (4, n // tw)

    def body(b_ref, g_ref, r_ref, o_ref):
        o_ref[...] = (g_ref[...].astype(F32) + r_ref[...].astype(F32)).astype(o_ref.dtype)

    return _call(body, [gw, recv], name=name, out_shape=jax.ShapeDtypeStruct(recv.shape, recv.dtype), grid=grid,
                 in_specs=[gw_spec, rv_spec], out_specs=rv_spec, prefetch=[base])


HBM_SPEC = pl.BlockSpec(memory_space=pltpu.HBM)
SEM_SPEC = pl.BlockSpec(memory_space=pltpu.SEMAPHORE)
SPLIT_PARAMS = pltpu.CompilerParams(has_side_effects=pltpu.SideEffectType.DATAFLOW_SIDE_EFFECTING)


def _split_copies(refs, plans, send_sems, recv_sems):
    pos = _position()
    out = []
    for k, (si, sv, li, lv, rel) in enumerate(plans):
        peer = _peer_position(pos, rel)
        mk = lambda d: pltpu.make_async_remote_copy(
            src_ref=sv(refs[si], pos), dst_ref=d, send_sem=send_sems.at[k], recv_sem=recv_sems.at[k],
            device_id=(peer["x"], peer["y"], peer["c"]), device_id_type=MESH)
        out.append((mk(lv(refs[li], pos)), mk(lv(refs[li], peer))))
    return out


def _split_start(arrays, plans, name):
    n = len(arrays)

    def body(*refs):
        send_sems, recv_sems = refs[n], refs[n + 1]
        for out_cp, _ in _split_copies(refs[:n], plans, send_sems, recv_sems):
            out_cp.start()
        refs[-1][...] = jnp.zeros_like(refs[-1])

    sems = pltpu.SemaphoreType.DMA((len(plans),))
    res = pl.pallas_call(
        body, name=name,
        out_shape=(sems, sems) + tuple(pltpu.HBM(a.shape, a.dtype) for a in arrays) + (jax.ShapeDtypeStruct((8, LANES), F32),),
        in_specs=[HBM_SPEC] * n, out_specs=(SEM_SPEC, SEM_SPEC) + (HBM_SPEC,) * n + (pl.BlockSpec(memory_space=pltpu.VMEM),),
        input_output_aliases={i: 2 + i for i in range(n)}, compiler_params=SPLIT_PARAMS,
    )(*[pltpu.with_memory_space_constraint(a, pltpu.HBM) for a in arrays])
    return res[0], res[1], list(res[2:2 + n]), res[-1]


def _split_wait(send_sems, recv_sems, arrays, plans, after, name):
    n, na = len(arrays), len(after)

    def body(*refs):
        for out_cp, in_cp in _split_copies(refs[:n], plans, refs[n], refs[n + 1]):
            out_cp.wait_send()
            in_cp.wait_recv()

    res = pl.pallas_call(
        body, name=name, out_shape=tuple(pltpu.HBM(a.shape, a.dtype) for a in arrays),
        in_specs=[HBM_SPEC] * n + [SEM_SPEC, SEM_SPEC] + [ANY] * na, out_specs=(HBM_SPEC,) * n,
        input_output_aliases={i: i for i in range(n)}, compiler_params=SPLIT_PARAMS,
    )(*arrays, send_sems, recv_sems, *after)
    return list(res)


def _rs_split_start(parts, name):
    nw = len(parts)
    lands = [lax.empty((3,) + p.shape[1:], p.dtype) for p in parts]
    plans = [(s, _slot_view(i), nw + s, _slot_view(i - 1), CHIP_RELS[i]) for s in range(nw) for i in (1, 2, 3)]
    send_sems, recv_sems, arrays, token = _split_start(list(parts) + lands, plans, name)
    return dict(sems=(send_sems, recv_sems), arrays=arrays, plans=plans, token=token, nw=nw)


def _rs_split_wait(h, after, name):
    arrays = _split_wait(h["sems"][0], h["sems"][1], h["arrays"], h["plans"], after, name)
    return arrays[:h["nw"]], arrays[h["nw"]:]


def _behind(xs, tokens):
    out = lax.optimization_barrier((tuple(xs), tuple(tokens)))
    return list(out[0])


def _ag_w_in(src, a, D, INW):
    wm = LANES * a

    hd = D // 2
    ALL, TOP, BOT = (0, D), (0, hd), (hd, D)

    def main_place(ref, p, rows=ALL):
        off = pl.multiple_of(((2 * a + 1) * (p["me"] // 2) + (a + 1) * p["c"]) * LANES, LANES)
        return ref.at[pl.ds(rows[0], rows[1] - rows[0]), pl.ds(off, wm)]

    def main_src(ref, p):
        return ref.at[:, pl.ds(pl.multiple_of(p["c"] * LANES, LANES), wm)]

    def mid_src(ref, p):
        return ref.at[:, pl.ds(pl.multiple_of((1 - p["c"]) * wm, LANES), LANES)]

    def mid_place(ref, p, rows=ALL):
        return ref.at[p["me"], pl.ds(rows[0], rows[1] - rows[0]), :]

    def body(src_ref, full_ref, mid_ref, send_sems, recv_sems):
        pos = _position()
        sib, xn, yn = (_peer_position(pos, r) for r in (1, 4, 2))
        dg = _peer_position(pos, 6)
        started = []

        def remote(k, s, d, to):
            return pltpu.make_async_remote_copy(src_ref=s, dst_ref=d, send_sem=send_sems.at[k], recv_sem=recv_sems.at[k],
                                                device_id=(to["x"], to["y"], to["c"]), device_id_type=MESH)

        def send(k, owner, rows, to, from_src=False):
            for j, (src_v, place) in enumerate(((main_src, main_place), (mid_src, mid_place))):
                s = src_v(src_ref, pos) if from_src else place(full_ref if j == 0 else mid_ref, owner, rows)
                cp = remote(k + j, s, place(full_ref if j == 0 else mid_ref, owner, rows), to)
                cp.start()
                started.append(cp)

        def landed(k, owner, rows, frm):
            for j, place in enumerate((main_place, mid_place)):
                ref = full_ref if j == 0 else mid_ref
                remote(k + j, place(ref, owner, rows), place(ref, owner, rows), frm).wait_recv()

        local = [pltpu.make_async_copy(main_src(src_ref, pos), main_place(full_ref, pos), send_sems.at[18]),
                 pltpu.make_async_copy(mid_src(src_ref, pos), mid_place(mid_ref, pos), send_sems.at[19])]
        for cp in local:
            cp.start()
        send(0, pos, ALL, sib, from_src=True)
        send(2, pos, ALL, xn, from_src=True)
        send(4, pos, ALL, yn, from_src=True)
        landed(2, xn, ALL, xn)
        send(10, xn, ALL, sib)
        send(6, xn, TOP, yn)
        landed(4, yn, ALL, yn)
        send(12, yn, ALL, sib)
        send(8, yn, BOT, xn)
        landed(6, dg, TOP, yn)
        send(14, dg, TOP, sib)
        landed(8, dg, BOT, xn)
        send(16, dg, BOT, sib)
        sib_of = lambda p: _peer_position(p, 1)
        landed(0, sib, ALL, sib)
        landed(10, sib_of(xn), ALL, sib)
        landed(12, sib_of(yn), ALL, sib)
        landed(14, sib_of(dg), TOP, sib)
        landed(16, sib_of(dg), BOT, sib)
        for cp in started:
            cp.wait_send()
        for cp in local:
            cp.wait()

    return _call(body, [src], name="ag_w_in", in_specs=[ANY], out_specs=[ANY, ANY],
                 out_shape=(jax.ShapeDtypeStruct((D, INW), BF16), jax.ShapeDtypeStruct((N_DEV, D, LANES), BF16)),
                 scratch_shapes=[pltpu.SemaphoreType.DMA((20,)), pltpu.SemaphoreType.DMA((20,))])


def _patch_mid(full, mid, a):
    D = full.shape[0]

    def body(full_ref, e_ref, o_ref, out_ref):
        out_ref[...] = e_ref[...] + o_ref[...]

    return _call(body, [full, mid, mid], name="patch_mid", grid=(N_DEV // 2,),
                 out_shape=jax.ShapeDtypeStruct(full.shape, full.dtype),
                 in_specs=[ANY, pl.BlockSpec((None, D, LANES), lambda j: (2 * j, 0, 0)),
                           pl.BlockSpec((None, D, LANES), lambda j: (2 * j + 1, 0, 0))],
                 out_specs=pl.BlockSpec((D, LANES), lambda j: (0, (2 * a + 1) * j + a)), aliases={0: 0})


MM_RESIDENT = 2048


def _mm(a, b, mode, out_dtype, name, b_off=0, n=None, comm=None, extras=(), epi=None, tn=None, after=()):
    if mode == "nn":
        (M, K), (K2, N) = a.shape, b.shape
    elif mode == "nt":
        (M, K), (N, K2) = a.shape, b.shape
    else:
        (K, M), (K2, N) = a.shape, b.shape
    assert K == K2, (a.shape, b.shape, mode)
    if n is not None:
        N = n
    single = not isinstance(out_dtype, (tuple, list))
    out_dtypes = (out_dtype,) if single else tuple(out_dtype)
    if epi is None:
        epi = lambda r: (r,)
    tk = K if K <= MM_RESIDENT else (MM_RESIDENT if K % MM_RESIDENT == 0 else _tile(K, 512, LANES))
    nk = K // tk
    if M > MM_RESIDENT and mode == "tn" and N <= MM_RESIDENT and not b_off:
        tm, tn = _tile(M, 512, LANES), N
    elif nk > 1:
        tm, tn = _tile(M, 1024, LANES), _tile(N, tn or 1024, LANES)
    else:
        tm = _tile(M, MM_RESIDENT, LANES)
        tn = _tile(math.gcd(N, b_off) if b_off else N, tn or 512, LANES)
    jb = b_off // tn
    dn = {"nn": NN, "nt": NT, "tn": TN}[mode]
    ne, no = len(extras), len(out_dtypes)

    def body(a_ref, b_ref, *rest):
        e_refs, o_refs = rest[:ne], rest[ne:ne + no]

        def finish(r):
            for o_ref, v in zip(o_refs, epi(r, *[e[...] for e in e_refs])):
                o_ref[...] = v.astype(o_ref.dtype)

        if nk == 1:
            finish(_bdot(a_ref[...], b_ref[...], dn))
            return
        acc_ref = rest[ne + no]
        k = pl.program_id(2)

        @pl.when(k == 0)
        def _():
            acc_ref[...] = _bdot(a_ref[...], b_ref[...], dn)

        @pl.when(jnp.logical_and(k > 0, k < nk - 1))
        def _():
            acc_ref[...] += _bdot(a_ref[...], b_ref[...], dn)

        @pl.when(k == nk - 1)
        def _():
            finish(acc_ref[...] + _bdot(a_ref[...], b_ref[...], dn))

    a_spec = pl.BlockSpec((tk, tm), lambda i, j, k: (k, i)) if mode == "tn" else pl.BlockSpec((tm, tk), lambda i, j, k: (i, k))
    b_spec = pl.BlockSpec((tn, tk), lambda i, j, k: (j, k)) if mode == "nt" else pl.BlockSpec((tk, tn), lambda i, j, k: (k, j + jb))
    o_spec = pl.BlockSpec((tm, tn), lambda i, j, k: (i, j))
    res = _call(body, [a, b] + list(extras), name=name, grid=(M // tm, N // tn, nk),
                out_shape=tuple(jax.ShapeDtypeStruct((M, N), dt) for dt in out_dtypes),
                in_specs=[a_spec, b_spec] + [o_spec] * ne, out_specs=[o_spec] * no,
                scratch_shapes=[pltpu.VMEM((tm, tn), F32)] if nk > 1 else [], comm=comm, after=after)
    return res[0] if single else res


def _rowwise(fn, row_ins, bcast_ins, row_outs, acc_outs, name, rt=256, comm=None):
    L = row_ins[0][0].shape[-2]
    rt = _tile(L, rt, 16)
    nr, nb, no = len(row_ins), len(bcast_ins), len(row_outs)

    def body(*refs):
        i = pl.program_id(0)
        vals = [r[...] for r in refs[:nr + nb]]
        outs, accs = fn(*vals)
        for r, v in zip(refs[nr + nb:nr + nb + no], outs):
            r[...] = v.astype(r.dtype)
        acc_refs = refs[nr + nb + no:]

        @pl.when(i == 0)
        def _():
            for r in acc_refs:
                r[...] = jnp.zeros_like(r)

        for r, v in zip(acc_refs, accs):
            r[...] += v

    in_specs = []
    for spec in row_ins:
        w, cb = spec[1], spec[2]
        if len(spec) == 4:
            in_specs.append(pl.BlockSpec((None, rt, w), functools.partial(lambda i, cb, ld: (ld, i, cb), cb=cb, ld=spec[3])))
        else:
            in_specs.append(pl.BlockSpec((rt, w), functools.partial(lambda i, cb: (i, cb), cb=cb)))
    in_specs += [pl.BlockSpec(b.shape, lambda i: (0, 0)) for b in bcast_ins]
    out_specs = [pl.BlockSpec((rt, w), lambda i: (i, 0)) for w, _ in row_outs]
    out_specs += [pl.BlockSpec(s, lambda i: (0, 0)) for s in acc_outs]
    out_shape = [jax.ShapeDtypeStruct((L, w), dt) for w, dt in row_outs] + [jax.ShapeDtypeStruct(s, F32) for s in acc_outs]
    return _call(body, [s[0] for s in row_ins] + list(bcast_ins), name=name, grid=(L // rt,), out_shape=tuple(out_shape),
                 in_specs=in_specs, out_specs=out_specs, comm=comm)


def _whole(fn, ins, out_shapes, name):
    def body(*refs):
        outs = fn(*[r[...] for r in refs[:len(ins)]])
        for r, v in zip(refs[len(ins):], outs):
            r[...] = v.astype(r.dtype)

    return _call(body, list(ins), name=name, out_shape=tuple(jax.ShapeDtypeStruct(s, dt) for s, dt in out_shapes))


def _silu(x):
    return x * jax.nn.sigmoid(x)


def _rms(x, g):
    return (x * lax.rsqrt(jnp.mean(x * x, axis=-1, keepdims=True) + EPS)) * g


def _modnorm(x, g, shift, scale):
    return _rms(x, g) * (1.0 + scale) + shift


def _adamw(w, g, m, v):
    m = ADAM_B1 * m + (1.0 - ADAM_B1) * g
    v = ADAM_B2 * v + (1.0 - ADAM_B2) * jnp.square(g)
    m_hat = m / (1.0 - ADAM_B1 ** ADAM_STEP)
    v_hat = v / (1.0 - ADAM_B2 ** ADAM_STEP)
    delta = -ADAM_LR * (m_hat / (jnp.sqrt(v_hat) + ADAM_EPS) + ADAM_WD * w)
    return delta, m, v


def _lower_bound(lg):
    e = jnp.exp(lg - jnp.max(lg, axis=0, keepdims=True))
    return e[0:1] / jnp.sum(e, axis=0, keepdims=True)


def _hg_chunk(hq, hf, hi, lb, st):
    C = hq.shape[0]
    row = lax.broadcasted_iota(jnp.int32, (C, C), 0)
    col = lax.broadcasted_iota(jnp.int32, (C, C), 1)
    tri = row >= col
    sg = jax.nn.sigmoid(hf)
    f = lb + (1.0 - lb) * sg
    lf = jnp.log(f)
    k = 1.0 - f
    q = _silu(hq)
    b = _dot(tri.astype(F32), lf, NN, precision=HIGHEST)
    m = b[C // 2 - 1:C // 2]
    bl = b[C - 1:C]
    e_qm, e_km, e_kl, e_q = jnp.exp(b - m), jnp.exp(m - b), jnp.exp(bl - b), jnp.exp(b)
    qe, ke, kd, qb = q * e_qm, k * e_km, k * e_kl, q * e_q
    sc = jnp.where(tri, _bdot(qe, ke, NT), 0.0)
    o = _bdot(sc, hi, NN) + _bdot(qb, st, NT)
    dec = jnp.exp(bl)
    st_next = st * dec + _bdot(hi, kd, TN)
    return o, st_next, dict(tri=tri, sg=sg, f=f, k=k, q=q, qe=qe, ke=ke, kd=kd, qb=qb, sc=sc, dec=dec,
                            e_qm=e_qm, e_km=e_km, e_kl=e_kl, e_q=e_q)


def _hg_out(o, hgate, gout):
    return _rms(o, gout) * _silu(hgate)


HG_GROUP = 32


def _hgrn_fwd(p4, lb_logits, gout, H, comm=None):
    L = p4.shape[0]
    C = HG_CHUNK
    GR = _tile(L // C, HG_GROUP, 1)
    T = GR * C
    N = L // T

    def body(hq_ref, hf_ref, hi_ref, hg_ref, lg_ref, gout_ref, o_ref, s_ref, st_ref):
        @pl.when(pl.program_id(1) == 0)
        def _():
            st_ref[...] = jnp.zeros_like(st_ref)

        lb = _lower_bound(lg_ref[...])
        st = st_ref[...]
        for ci in range(GR):
            rows = pl.ds(ci * C, C)
            s_ref[0, ci] = st
            o, st, _ = _hg_chunk(hq_ref[rows, :], hf_ref[rows, :], hi_ref[rows, :], lb, st)
            o_ref[rows, :] = _hg_out(o, hg_ref[rows, :], gout_ref[...]).astype(o_ref.dtype)
        st_ref[...] = st

    blk = lambda s: pl.BlockSpec((T, HG_DK), functools.partial(lambda h, n, s: (n, s * H + h), s=s))
    return _call(
        body, [p4, p4, p4, p4, lb_logits, gout], name="hgrn_fwd", grid=(H, N),
        out_shape=(jax.ShapeDtypeStruct((L, H * HG_DK), BF16), jax.ShapeDtypeStruct((H, N * GR, HG_DK, HG_DK), F32)),
        in_specs=[blk(0), blk(1), blk(2), blk(3), pl.BlockSpec((2, HG_DK), lambda h, n: (0, h)),
                  pl.BlockSpec((1, HG_DK), lambda h, n: (0, 0))],
        out_specs=(pl.BlockSpec((T, HG_DK), lambda h, n: (n, h)),
                   pl.BlockSpec((1, GR, HG_DK, HG_DK), lambda h, n: (h, n, 0, 0))),
        scratch_shapes=[pltpu.VMEM((HG_DK, HG_DK), F32)], comm=comm)


def _hgrn_bwd(p4, lb_logits, gout, s_all, d_out, H, comm=None):
    L = p4.shape[0]
    C = HG_CHUNK
    GR = _tile(L // C, HG_GROUP, 1)
    T = GR * C
    N = L // T

    def body(hq_ref, hf_ref, hi_ref, hg_ref, lg_ref, gout_ref, s_ref, do_ref,
             dq_ref, df_ref, di_ref, dg_ref, dlb_ref, dgo_ref, dst_ref):
        @pl.when(pl.program_id(1) == 0)
        def _():
            dst_ref[...] = jnp.zeros_like(dst_ref)
            dlb_ref[...] = jnp.zeros_like(dlb_ref)

        @pl.when(jnp.logical_and(pl.program_id(0) == 0, pl.program_id(1) == 0))
        def _():
            dgo_ref[...] = jnp.zeros_like(dgo_ref)

        lb = _lower_bound(lg_ref[...])
        dst = dst_ref[...]
        d_lb = jnp.zeros((1, HG_DK), F32)
        d_go = jnp.zeros((1, HG_DK), F32)
        for ci in reversed(range(GR)):
            rows = pl.ds(ci * C, C)
            dst, d_lb_c, d_go_c = chunk_bwd(rows, lb, s_ref[0, ci], dst, hq_ref, hf_ref, hi_ref, hg_ref, gout_ref, do_ref,
                                            dq_ref, df_ref, di_ref, dg_ref)
            d_lb += d_lb_c
            d_go += d_go_c
        dst_ref[...] = dst
        dlb_ref[...] += d_lb
        dgo_ref[...] += d_go

    def chunk_bwd(rows, lb, st, dst_next, hq_ref, hf_ref, hi_ref, hg_ref, gout_ref, do_ref, dq_ref, df_ref, di_ref, dg_ref):
        hq, hf, hi, hgate = hq_ref[rows, :], hf_ref[rows, :], hi_ref[rows, :], hg_ref[rows, :]
        o, _, t = _hg_chunk(hq, hf, hi, lb, st)
        _, out_vjp = jax.vjp(_hg_out, o, hgate, gout_ref[...])
        do, d_hgate, d_gout = out_vjp(do_ref[rows, :])
        tri = t["tri"]
        dsc = jnp.where(tri, _bdot(do, hi, NT), 0.0)
        dv = _bdot(t["sc"], do, TN) + _bdot(t["kd"], dst_next, NT)
        dqe = _bdot(dsc, t["ke"], NN)
        dke = _bdot(dsc, t["qe"], TN)
        dqb = _bdot(do, st, NN)
        dkd = _bdot(hi, dst_next, NN)
        ddec = jnp.sum(dst_next * st, axis=0, keepdims=True)
        dst_prev = _bdot(do, t["qb"], TN) + dst_next * t["dec"]
        dq = dqe * t["e_qm"] + dqb * t["e_q"]
        dk = dke * t["e_km"] + dkd * t["e_kl"]
        tq, tk, td, tb = dqe * t["qe"], dke * t["ke"], dkd * t["kd"], dqb * t["qb"]
        db = tq - tk - td + tb
        dm = jnp.sum(tk - tq, axis=0, keepdims=True)
        dbl = jnp.sum(td, axis=0, keepdims=True) + ddec * t["dec"]
        rowi = lax.broadcasted_iota(jnp.int32, (C, HG_DK), 0)
        db = db + jnp.where(rowi == C // 2 - 1, dm, 0.0) + jnp.where(rowi == C - 1, dbl, 0.0)
        dlf = _dot(tri.astype(F32), db, TN, precision=HIGHEST)
        df = dlf / t["f"] - dk
        sg = t["sg"]
        df_ref[rows, :] = (df * (1.0 - lb) * sg * (1.0 - sg)).astype(df_ref.dtype)
        sq = jax.nn.sigmoid(hq)
        dq_ref[rows, :] = (dq * (sq * (1.0 + hq * (1.0 - sq)))).astype(dq_ref.dtype)
        di_ref[rows, :] = dv.astype(di_ref.dtype)
        dg_ref[rows, :] = d_hgate.astype(dg_ref.dtype)
        return dst_prev, jnp.sum(df * (1.0 - sg), axis=0, keepdims=True), d_gout

    blk = lambda s: pl.BlockSpec((T, HG_DK), functools.partial(lambda h, n, s: (N - 1 - n, s * H + h), s=s))
    oblk = pl.BlockSpec((T, HG_DK), lambda h, n: (N - 1 - n, h))
    vec = pl.BlockSpec((1, HG_DK), lambda h, n: (0, h))
    W = H * HG_DK
    return _call(
        body, [p4, p4, p4, p4, lb_logits, gout, s_all, d_out], name="hgrn_bwd", grid=(H, N),
        out_shape=tuple([jax.ShapeDtypeStruct((L, W), BF16)] * 4 + [jax.ShapeDtypeStruct((1, W), F32), jax.ShapeDtypeStruct((1, HG_DK), F32)]),
        in_specs=[blk(0), blk(1), blk(2), blk(3), pl.BlockSpec((2, HG_DK), lambda h, n: (0, h)),
                  pl.BlockSpec((1, HG_DK), lambda h, n: (0, 0)),
                  pl.BlockSpec((1, GR, HG_DK, HG_DK), lambda h, n: (h, N - 1 - n, 0, 0)), oblk],
        out_specs=(oblk, oblk, oblk, oblk, vec, pl.BlockSpec((1, HG_DK), lambda h, n: (0, 0))),
        scratch_shapes=[pltpu.VMEM((HG_DK, HG_DK), F32)], comm=comm)


def _bucket_ids():
    i = jnp.arange(AT_BLOCK, dtype=jnp.int32)[:, None]
    j = jnp.arange(2 * AT_BLOCK, dtype=jnp.int32)[None, :]
    n = jnp.maximum(i - j + AT_BLOCK, 0)
    nf = jnp.maximum(n, 1).astype(F32)
    large = MAX_EXACT + (jnp.log(nf / MAX_EXACT) / math.log(MAX_DISTANCE / MAX_EXACT) * (N_BUCKETS - MAX_EXACT)).astype(jnp.int32)
    large = jnp.minimum(large, N_BUCKETS - 1)
    return jnp.where(n < MAX_EXACT, n, large).reshape(1, -1)


def _onehot(bucket):
    ids = lax.broadcasted_iota(jnp.int32, (N_BUCKETS, bucket.shape[1]), 0)
    return (ids == bucket).astype(F32)


def _attn_probs(qn, kpn, kcn, bias_g, sink, first, scale):
    rows = qn.shape[0]
    i = jnp.bitwise_and(lax.broadcasted_iota(jnp.int32, (rows, AT_BLOCK), 0), AT_BLOCK - 1)
    j = lax.broadcasted_iota(jnp.int32, (rows, AT_BLOCK), 1)
    lp = _bdot(qn, kpn, NT) * scale + bias_g[:, :AT_BLOCK]
    lc = _bdot(qn, kcn, NT) * scale + bias_g[:, AT_BLOCK:]
    lp = jnp.where(jnp.logical_and(j > i, jnp.logical_not(first)), lp, NEG_INF)
    lc = jnp.where(j <= i, lc, NEG_INF)
    m = jnp.maximum(jnp.maximum(jnp.max(lp, axis=-1, keepdims=True), jnp.max(lc, axis=-1, keepdims=True)), sink)
    pp, pc, ps = jnp.exp(lp - m), jnp.exp(lc - m), jnp.exp(sink - m)
    den = jnp.sum(pp, axis=-1, keepdims=True) + jnp.sum(pc, axis=-1, keepdims=True) + ps
    return pp / den, pc / den, ps / den


def _sink_rows(sk_ref, G):
    head = lax.broadcasted_iota(jnp.int32, (G * AT_BLOCK, 1), 0) // AT_BLOCK
    sink = jnp.zeros((G * AT_BLOCK, 1), F32)
    for g in range(G):
        sink = jnp.where(head == g, sk_ref[0, g:g + 1, :], sink)
    return sink


def _attn_fwd(q_t, kp, vp, qg, kg, sinks, bias, KVH, comm=None):
    AH, L, DH = q_t.shape
    G = AH // KVH
    NB = L // AT_BLOCK
    scale = DH ** -0.5

    def body(q_ref, kp_ref, kc_ref, vp_ref, vc_ref, qg_ref, kg_ref, sk_ref, b_ref, o_ref):
        first = pl.program_id(1) == 0
        kpn, kcn = _rms(kp_ref[0], kg_ref[...]), _rms(kc_ref[0], kg_ref[...])
        qn = _rms(q_ref[...].reshape(G * AT_BLOCK, DH), qg_ref[...])
        sink = _sink_rows(sk_ref, G)
        pp, pc, _ = _attn_probs(qn, kpn, kcn, b_ref[...].reshape(G * AT_BLOCK, 2 * AT_BLOCK), sink, first, scale)
        o = _bdot(pp, vp_ref[0], NN) + _bdot(pc, vc_ref[0], NN)
        o_ref[...] = o.reshape(G, AT_BLOCK, DH).astype(o_ref.dtype)

    kblk = lambda off: pl.BlockSpec((1, AT_BLOCK, DH), functools.partial(lambda h, n, off: (h, n + off, 0), off=off))
    return _call(
        body, [q_t, kp, kp, vp, vp, qg, kg, sinks, bias], name="attn_fwd", grid=(KVH, NB),
        out_shape=jax.ShapeDtypeStruct((AH, L, DH), BF16),
        in_specs=[pl.BlockSpec((G, AT_BLOCK, DH), lambda h, n: (h, n, 0)), kblk(0), kblk(1), kblk(0), kblk(1),
                  pl.BlockSpec((1, DH), lambda h, n: (0, 0)), pl.BlockSpec((1, DH), lambda h, n: (0, 0)),
                  pl.BlockSpec((1, G, 1), lambda h, n: (h, 0, 0)),
                  pl.BlockSpec((G, AT_BLOCK, 2 * AT_BLOCK), lambda h, n: (h, 0, 0))],
        out_specs=pl.BlockSpec((G, AT_BLOCK, DH), lambda h, n: (h, n, 0)), comm=comm)


def _attn_bwd(q_t, kp, vp, qg, kg, sinks, bias, do_t, KVH, comm=None):
    AH, L, DH = q_t.shape
    G = AH // KVH
    NB = L // AT_BLOCK
    B = AT_BLOCK
    scale = DH ** -0.5

    def body(q_ref, kp_ref, kc_ref, vp_ref, vc_ref, qg_ref, kg_ref, sk_ref, b_ref, do_ref,
             dq_ref, dk_ref, dv_ref, dqg_ref, dkg_ref, dsk_ref, db_ref):
        n = pl.program_id(1)
        first = n == 0

        @pl.when(first)
        def _():
            for r in (dk_ref, dv_ref, dsk_ref, db_ref):
                r[...] = jnp.zeros_like(r)

        @pl.when(jnp.logical_and(first, pl.program_id(0) == 0))
        def _():
            dqg_ref[...] = jnp.zeros_like(dqg_ref)
            dkg_ref[...] = jnp.zeros_like(dkg_ref)

        kp_raw, kc_raw, kgv, qgv = kp_ref[0], kc_ref[0], kg_ref[...], qg_ref[...]
        kpn, kp_vjp = jax.vjp(_rms, kp_raw, kgv)
        kcn, kc_vjp = jax.vjp(_rms, kc_raw, kgv)
        qn, q_vjp = jax.vjp(_rms, q_ref[...].reshape(G * B, DH), qgv)
        pp, pc, ps = _attn_probs(qn, kpn, kcn, b_ref[...].reshape(G * B, 2 * B), _sink_rows(sk_ref, G), first, scale)
        do = do_ref[...].reshape(G * B, DH)
        dvp = _bdot(pp, do, TN)
        dvc = _bdot(pc, do, TN)
        dpp = _bdot(do, vp_ref[0], NT)
        dpc = _bdot(do, vc_ref[0], NT)
        dsum = jnp.sum(dpp * pp, axis=-1, keepdims=True) + jnp.sum(dpc * pc, axis=-1, keepdims=True)
        dlp = pp * (dpp - dsum)
        dlc = pc * (dpc - dsum)
        dsk_ref[0] += jnp.sum((-ps * dsum).reshape(G, B, 1), axis=1)
        db_ref[:, :, :B] += dlp.reshape(G, B, B)
        db_ref[:, :, B:] += dlc.reshape(G, B, B)
        dlp, dlc = dlp * scale, dlc * scale
        dqn = _bdot(dlp, kpn, NN) + _bdot(dlc, kcn, NN)
        dq_raw, dqg = q_vjp(dqn)
        dq_ref[...] = dq_raw.reshape(G, B, DH).astype(dq_ref.dtype)
        dkp_raw, dkg_p = kp_vjp(_bdot(dlp, qn, TN))
        dkc_raw, dkg_c = kc_vjp(_bdot(dlc, qn, TN))
        r0 = pl.multiple_of(n * B, B)
        r1 = pl.multiple_of(n * B + B, B)
        dk_ref[0, pl.ds(r0, B), :] += dkp_raw
        dk_ref[0, pl.ds(r1, B), :] += dkc_raw
        dv_ref[0, pl.ds(r0, B), :] += dvp
        dv_ref[0, pl.ds(r1, B), :] += dvc
        dqg_ref[...] += dqg
        dkg_ref[...] += dkg_p + dkg_c

    kblk = lambda off: pl.BlockSpec((1, B, DH), functools.partial(lambda h, n, off: (h, n + off, 0), off=off))
    qblk = pl.BlockSpec((G, B, DH), lambda h, n: (h, n, 0))
    accblk = pl.BlockSpec((1, L + B, DH), lambda h, n: (h, 0, 0))
    vecblk = pl.BlockSpec((1, DH), lambda h, n: (0, 0))
    return _call(
        body, [q_t, kp, kp, vp, vp, qg, kg, sinks, bias, do_t], name="attn_bwd", grid=(KVH, NB),
        out_shape=(jax.ShapeDtypeStruct((AH, L, DH), BF16), jax.ShapeDtypeStruct((KVH, L + B, DH), F32),
                   jax.ShapeDtypeStruct((KVH, L + B, DH), F32), jax.ShapeDtypeStruct((1, DH), F32),
                   jax.ShapeDtypeStruct((1, DH), F32), jax.ShapeDtypeStruct((KVH, G, 1), F32),
                   jax.ShapeDtypeStruct((AH, B, 2 * B), F32)),
        in_specs=[qblk, kblk(0), kblk(1), kblk(0), kblk(1),
                  pl.BlockSpec((1, DH), lambda h, n: (0, 0)), pl.BlockSpec((1, DH), lambda h, n: (0, 0)),
                  pl.BlockSpec((1, G, 1), lambda h, n: (h, 0, 0)),
                  pl.BlockSpec((G, B, 2 * B), lambda h, n: (h, 0, 0)), qblk],
        out_specs=(qblk, accblk, accblk, vecblk, vecblk, pl.BlockSpec((1, G, 1), lambda h, n: (h, 0, 0)),
                   pl.BlockSpec((G, B, 2 * B), lambda h, n: (h, 0, 0))), comm=comm)


def _heads_first(t, nh):
    L = t.shape[0]
    return jnp.transpose(t.reshape(L, nh, t.shape[1] // nh), (1, 0, 2))


def _heads_last(t):
    nh, L, dh = t.shape
    return jnp.transpose(t, (1, 0, 2)).reshape(L, nh * dh)


def _softmax0(lg):
    e = jnp.exp(lg - jnp.max(lg, axis=0, keepdims=True))
    return e[0:1] / jnp.sum(e, axis=0, keepdims=True)


def _ada_update_call(fn, c_all, d_cols, w, m, v, rt):
    D, n = w.shape

    def body(c_ref, d_ref, w_ref, m_ref, v_ref, g_out, dl_out, m_out, v_out):
        outs, _ = fn(c_ref[...], d_ref[...], w_ref[...], m_ref[...], v_ref[...])
        for r, val in zip((g_out, dl_out, m_out, v_out), outs):
            r[...] = val

    wblk = pl.BlockSpec((rt, n), lambda i: (i, 0))
    return _call(
        body, [c_all, d_cols, w, m, v], name="update_ada", grid=(D // rt,), out_shape=tuple([jax.ShapeDtypeStruct((D, n), F32)] * 4),
        in_specs=[pl.BlockSpec((N_DEV, rt), lambda i: (0, i)), pl.BlockSpec((N_DEV, n), lambda i: (0, 0)), wblk, wblk, wblk],
        out_specs=(wblk, wblk, wblk, wblk))


def kernel(x, c, w_ada, b_ada, norm1_g, norm2_g, w_in, hg_lb_logits, hg_out_norm_g, q_norm_g, k_norm_g, attn_sinks, rel_bias_table, w_branch_hg, w_branch_attn, w_out, w_ff1, w_ff2, loss_target, m_w_ada, m_b_ada, m_norm1_g, m_norm2_g, m_w_in, m_hg_lb_logits, m_hg_out_norm_g, m_q_norm_g, m_k_norm_g, m_attn_sinks, m_rel_bias_table, m_w_branch_hg, m_w_branch_attn, m_w_out, m_w_ff1, m_w_ff2, v_w_ada, v_b_ada, v_norm1_g, v_norm2_g, v_w_in, v_hg_lb_logits, v_hg_out_norm_g, v_q_norm_g, v_k_norm_g, v_attn_sinks, v_rel_bias_table, v_w_branch_hg, v_w_branch_attn, v_w_out, v_w_ff1, v_w_ff2):
    cc = lax.axis_index("c")
    me = 4 * lax.axis_index("x") + 2 * lax.axis_index("y") + cc
    x2 = x[0]
    tgt = loss_target[0]
    L, D = x2.shape
    HGW = hg_lb_logits.shape[1]
    H = HGW // HG_DK
    AH = attn_sinks.shape[1]
    DH = q_norm_g.shape[1]
    ATW = AH * DH
    BW = w_in.shape[2]
    INW = BW * N_DEV
    A = BW // LANES
    assert BW == LANES * A + LANES // 2
    KVW = (INW - 4 * HGW - ATW - 2 * D) // 2
    KVH = KVW // DH
    G = AH // KVH
    ADA_N = w_ada.shape[2]
    PAIR = 2 * A + 1

    c_all = _gather_small(c, me, "gather_c")[:, 0, :]
    b_cols = lax.dynamic_slice(b_ada, (0, me * ADA_N), (1, ADA_N))
    (ada_cols,) = _whole(lambda cv, w, b: (_bdot(_silu(cv), w, NN) + b,), [c_all, w_ada[0], b_cols],
                         [((N_DEV, ADA_N), F32)], "ada_fwd")
    ada_all = _gather_small(ada_cols, me, "gather_ada")
    ada_row = lax.dynamic_slice(ada_all, (0, me, 0), (N_DEV, 1, ADA_N)).reshape(1, 6 * D)

    w_in_b = w_in[0].astype(BF16)
    src_in = jnp.where(cc == 0, jnp.pad(w_in_b, ((0, 0), (0, LANES // 2))), jnp.pad(w_in_b, ((0, 0), (LANES // 2, 0))))
    (src_in,) = _behind([src_in], [ada_row])
    shift1, scale1, gate1, shift2, scale2, gate2 = [ada_row[:, i * D:(i + 1) * D] for i in range(6)]
    w_in_gapped, w_in_mid = _ag_w_in(src_in, A, D, INW)
    w_in_full = _patch_mid(w_in_gapped, w_in_mid, A)

    wnames = ("bhg", "bat", "out", "ff1", "ff2")
    small = ("bhg", "bat", "out")
    waxis = dict(zip(wnames, (1, 1, 0, 1, 0)))
    wsrc = dict(zip(wnames, (w_branch_hg, w_branch_attn, w_out, w_ff1, w_ff2)))
    wblk = {k: wsrc[k][0].astype(BF16) for k in wnames}
    wf = {}

    (h,) = _rowwise(lambda xv, g, sh, sc: ((_modnorm(xv, g, sh, sc),), ()), [(x2, D, 0)], [norm1_g, shift1, scale1],
                    [(D, BF16)], [], "norm1")
    o4, oa = 4 * HGW, 4 * HGW + ATW + 2 * KVW
    r1, r2, ro = wblk["ff1"].shape[0], wblk["ff2"].shape[0], wblk["out"].shape[0]
    cm = _Comm()
    hs = {k: _ag_ici(cm, wblk[k], waxis[k]) for k in ("bhg", "bat")}
    hs["out"] = _ag_ici(cm, wblk["out"], waxis["out"], rows=(0, ro // 2))
    p4 = _mm(h, w_in_full, "nn", F32, "proj_hg", n=o4, comm=cm)
    half = {k: cm.result(hs[k]) for k in hs}
    cm = _Comm()
    hs = {"out": _ag_ici(cm, wblk["out"], waxis["out"], rows=(ro // 2, ro), into=half["out"])}
    pa = _mm(h, w_in_full, "nn", F32, "proj_at", b_off=o4, n=oa - o4, comm=cm)
    half["out"] = cm.result(hs["out"])
    cm = _Comm()
    hs = {k: _ag_d2d(cm, half[k], waxis[k]) for k in ("bhg", "bat")}
    hs["ff2"] = _ag_ici(cm, wblk["ff2"], waxis["ff2"], rows=(0, r2 // 4))
    pg = _mm(h, w_in_full, "nn", F32, "proj_gate", b_off=oa, n=INW - oa, comm=cm)
    wf["bhg"], wf["bat"], half["ff2"] = (cm.result(hs[k]) for k in ("bhg", "bat", "ff2"))

    cm = _Comm()
    hs = {"out": _ag_d2d(cm, half["out"], waxis["out"]), "ff1": _ag_ici(cm, wblk["ff1"], waxis["ff1"], rows=(0, r1 // 2))}
    o_hg, s_all = _hgrn_fwd(p4, hg_lb_logits, hg_out_norm_g, H, comm=cm)
    wf["out"], half["ff1"] = cm.result(hs["out"]), cm.result(hs["ff1"])

    bucket = _bucket_ids()
    (bias_flat,) = _whole(lambda tb, bk: (_dot(tb, _onehot(bk), TN, precision=HIGHEST),), [rel_bias_table, bucket],
                          [((AH, AT_BLOCK * 2 * AT_BLOCK), F32)], "bias_fwd")
    bias = bias_flat.reshape(AH, AT_BLOCK, 2 * AT_BLOCK)
    q_t = _heads_first(pa[:, :ATW], AH)
    pad = lambda t: jnp.pad(t, ((0, 0), (AT_BLOCK, 0), (0, 0)))
    kp = pad(_heads_first(pa[:, ATW:ATW + KVW], KVH))
    vp = pad(_heads_first(pa[:, ATW + KVW:], KVH))
    sinks3 = attn_sinks.reshape(KVH, G, 1)
    cm = _Comm()
    hs = {"ff1": _ag_ici(cm, wblk["ff1"], waxis["ff1"], rows=(r1 // 2, r1), into=half["ff1"])}
    o_at = _heads_last(_attn_fwd(q_t, kp, vp, q_norm_g, k_norm_g, sinks3, bias, KVH, comm=cm))
    half["ff1"] = cm.result(hs["ff1"])

    bh = _mm(o_hg, wf["bhg"], "nn", F32, "branch_hg")
    ba = _mm(o_at, wf["bat"], "nn", F32, "branch_at")

    def merge_fn(bhv, bav, ghg, gat):
        return jax.nn.sigmoid(ghg) * bhv + jax.nn.sigmoid(gat) * bav

    cm = _Comm()
    hs = {"ff1": _ag_d2d(cm, half["ff1"], waxis["ff1"])}
    (merged,) = _rowwise(lambda *a: ((merge_fn(*a),), ()), [(bh, D, 0), (ba, D, 0), (pg, D, 0), (pg, D, 1)], [],
                         [(D, BF16)], [], "merge", comm=cm)
    wf["ff1"] = cm.result(hs["ff1"])
    cm = _Comm()
    hs = {"ff2": _ag_ici(cm, wblk["ff2"], waxis["ff2"], rows=(r2 // 4, 3 * r2 // 8), into=half["ff2"])}
    mo = _mm(merged, wf["out"], "nn", F32, "out_proj", comm=cm)
    half["ff2"] = cm.result(hs["ff2"])

    def resid1(xv, mov, g1, g2n, sh, sc):
        x1v = xv + g1 * mov
        return (x1v, _modnorm(x1v, g2n, sh, sc)), ()

    cm = _Comm()
    hs = {"ff2": _ag_ici(cm, wblk["ff2"], waxis["ff2"], rows=(3 * r2 // 8, r2 // 2), into=half["ff2"])}
    x1, h2 = _rowwise(resid1, [(x2, D, 0), (mo, D, 0)], [gate1, norm2_g, shift2, scale2], [(D, F32), (D, BF16)], [], "resid1",
                      comm=cm)
    half["ff2"] = cm.result(hs["ff2"])
    cm = _Comm()
    hs = {"ff2": _ag_ici(cm, wblk["ff2"], waxis["ff2"], rows=(r2 // 2, r2), into=half["ff2"])}
    u, act = _mm(h2, wf["ff1"], "nn", (F32, BF16), "ff1", comm=cm, epi=lambda r: (r, jnp.square(jnp.maximum(r, 0.0))))
    half["ff2"] = cm.result(hs["ff2"])
    cm = _Comm()
    hs = {"ff2": _ag_d2d(cm, half["ff2"], waxis["ff2"])}
    _call(lambda: None, [], name="ag_d2d_ff2", out_shape=(), comm=cm)
    wf["ff2"] = cm.result(hs["ff2"])
    ff = _mm(act, wf["ff2"], "nn", F32, "ff2")

    def loss_fn(x1v, ffv, tv, g2):
        e = x1v + g2 * ffv - tv
        dy = e * (1.0 / D)
        return (dy, dy * g2), (jnp.sum(e * e, axis=0, keepdims=True), jnp.sum(dy * ffv, axis=0, keepdims=True))

    dy, d_ff, sq_sum, d_gate2 = _rowwise(loss_fn, [(x1, D, 0), (ff, D, 0), (tgt, D, 0)], [gate2],
                                         [(D, F32), (D, BF16)], [(1, D), (1, D)], "loss")
    loss = lax.psum(jnp.sum(sq_sum) * (0.5 / D), ("x", "y", "c"))

    owner_base = jnp.stack([me ^ r for r in CHIP_RELS]).astype(jnp.int32)
    gw, recv1, part, recv2 = {}, {}, {}, {}
    gw["ff2"] = _mm(act, d_ff, "tn", BF16, "dw_ff2")
    cm = _Comm()
    hh = _rs_d2d(cm, gw["ff2"], waxis["ff2"])
    d_u = _mm(d_ff, wf["ff2"], "nt", BF16, "d_act", comm=cm, extras=[u], epi=lambda r, uv: (r * (2.0 * jnp.maximum(uv, 0.0)),))
    part["ff2"] = _rs_add(gw["ff2"], cm.result(hh), waxis["ff2"], owner_base, "rs_add_ff2")
    rows_ff2 = part["ff2"].shape[1]
    cm = _Comm()
    hh = _rs_ici(cm, part["ff2"], rows=(0, rows_ff2 // 2))
    gw["ff1"] = _mm(h2, d_u, "tn", BF16, "dw_ff1", comm=cm)
    cm2 = _Comm()
    hh2 = _rs_ici(cm2, part["ff2"], rows=(rows_ff2 // 2, rows_ff2), recv=cm.result(hh))
    hh1 = _rs_d2d(cm2, gw["ff1"], waxis["ff1"])
    d_h2 = _mm(d_u, wf["ff1"], "nt", F32, "d_h2", comm=cm2)
    recv2["ff2"] = cm2.result(hh2)
    part["ff1"] = _rs_add(gw["ff1"], cm2.result(hh1), waxis["ff1"], owner_base, "rs_add_ff1")

    def norm2_bwd(dh2v, x1v, dyv, mov, g2n, sh, sc, g1):
        _, vjp = jax.vjp(_modnorm, x1v, g2n, sh, sc)
        dx, dg, dsh, dsc = vjp(dh2v)
        dx1 = dyv + dx
        return (dx1, dx1 * g1), (dg, dsh, dsc, jnp.sum(dx1 * mov, axis=0, keepdims=True))

    d_x1, d_mo, d_g2n, d_shift2, d_scale2, d_gate1 = _rowwise(
        norm2_bwd, [(d_h2, D, 0), (x1, D, 0), (dy, D, 0), (mo, D, 0)], [norm2_g, shift2, scale2, gate1],
        [(D, F32), (D, BF16)], [(1, D)] * 4, "norm2_bwd")
    gw["out"] = _mm(merged, d_mo, "tn", BF16, "dw_out")
    cm = _Comm()
    hh = _rs_d2d(cm, gw["out"], waxis["out"])
    d_merged = _mm(d_mo, wf["out"], "nt", F32, "d_merged", comm=cm)
    part["out"] = _rs_add(gw["out"], cm.result(hh), waxis["out"], owner_base, "rs_add_out")

    def merge_bwd(dmv, bhv, bav, ghg, gat):
        _, vjp = jax.vjp(merge_fn, bhv, bav, ghg, gat)
        return vjp(dmv), ()

    d_bh, d_ba, d_ghg, d_gat = _rowwise(merge_bwd, [(d_merged, D, 0), (bh, D, 0), (ba, D, 0), (pg, D, 0), (pg, D, 1)], [],
                                        [(D, BF16)] * 4, [], "merge_bwd")
    gw["bhg"] = _mm(o_hg, d_bh, "tn", BF16, "dw_bhg")
    gw["bat"] = _mm(o_at, d_ba, "tn", BF16, "dw_bat")
    cm = _Comm()
    hh = {k: _rs_d2d(cm, gw[k], waxis[k]) for k in ("bhg", "bat")}
    d_ohg = _mm(d_bh, wf["bhg"], "nt", F32, "d_ohg", comm=cm)
    for k in ("bhg", "bat"):
        part[k] = _rs_add(gw[k], cm.result(hh[k]), waxis[k], owner_base, "rs_add_" + k)
    d_oat = _mm(d_ba, wf["bat"], "nt", BF16, "d_oat")

    cm = _Comm()
    hh = {"ff1": _rs_ici(cm, part["ff1"])}
    d_hq, d_hf, d_hi, d_hg, d_lb, d_gout_h = _hgrn_bwd(p4, hg_lb_logits, hg_out_norm_g, s_all, d_ohg, H, comm=cm)
    recv2["ff1"] = cm.result(hh["ff1"])
    cm = _Comm()
    hh = {k: _rs_ici(cm, part[k]) for k in small}
    dq_t, dkp, dvp, d_qg, d_kg, d_sk, d_bias = _attn_bwd(q_t, kp, vp, q_norm_g, k_norm_g, sinks3, bias,
                                                         _heads_first(d_oat, AH), KVH, comm=cm)
    for k in hh:
        recv2[k] = cm.result(hh[k])
    d_aq = _heads_last(dq_t)
    d_ak = _heads_last(dkp[:, AT_BLOCK:, :]).astype(BF16)
    d_av = _heads_last(dvp[:, AT_BLOCK:, :]).astype(BF16)
    d_proj = jnp.concatenate([d_hq, d_hf, d_hi, d_hg, d_aq, d_ak, d_av, d_ghg, d_gat], axis=1)
    gw_in = _mm(h, d_proj, "tn", BF16, "dw_in")

    wm = LANES * A
    cm = _Comm()
    hi_ = cm.inp(gw_in)
    h_main, h_mid = cm.out((4, D, wm), BF16), cm.out((4, D, LANES), BF16)
    for i, r in enumerate(CHIP_RELS):
        def main_view(ref, p, r=r):
            o = p["me"] ^ r ^ 1
            return ref.at[:, pl.ds(pl.multiple_of((PAIR * (o // 2) + (A + 1) * (1 - p["c"])) * LANES, LANES), wm)]

        def mid_view(ref, p, r=r):
            o = p["me"] ^ r
            return ref.at[:, pl.ds(pl.multiple_of((PAIR * (o // 2) + A) * LANES, LANES), LANES)]

        cm.copy(hi_, main_view, h_main, _slot_view(i), 1)
        cm.copy(hi_, mid_view, h_mid, _slot_view(i), 1)
    _call(lambda: None, [], name="rs_d2d_in", out_shape=(), comm=cm)
    chip = jnp.stack([(me ^ r) // 2 for r in CHIP_RELS]).astype(jnp.int32)
    part_main = _rs_add(gw_in, cm.result(h_main), 1, PAIR * chip + (A + 1) * cc, "rs_add_in_main", tw=LANES)
    part_mid = _rs_add(gw_in, cm.result(h_mid), 1, PAIR * chip + A, "rs_add_in_mid", tw=LANES)
    rs_in = _rs_split_start([part_main, part_mid], "rs_in_start")
    d_h = _mm(d_proj, w_in_full, "nt", F32, "d_h", tn=D, after=[rs_in["token"]])

    def norm1_bwd(dhv, xv, dx1v, g1n, sh, sc):
        _, vjp = jax.vjp(_modnorm, xv, g1n, sh, sc)
        dx, dg, dsh, dsc = vjp(dhv)
        return (dx1v + dx,), (dg, dsh, dsc)

    grad_x, d_g1n, d_shift1, d_scale1 = _rowwise(norm1_bwd, [(d_h, D, 0), (x2, D, 0), (d_x1, D, 0)],
                                                 [norm1_g, shift1, scale1], [(D, F32)], [(1, D)] * 3, "norm1_bwd")

    def sum4(p0, p1, p2, p3):
        return ((p0.astype(F32) + p1.astype(F32)) + p2.astype(F32)) + p3.astype(F32)

    def update_fn(w, m, v, p0, p1, p2, p3):
        g = sum4(p0, p1, p2, p3)
        delta, mn, vn = _adamw(w, g, m, v)
        return (g, delta, mn, vn), ()

    wmv = dict(zip(wnames, ((w_branch_hg, m_w_branch_hg, v_w_branch_hg), (w_branch_attn, m_w_branch_attn, v_w_branch_attn),
                            (w_out, m_w_out, v_w_out), (w_ff1, m_w_ff1, v_w_ff1), (w_ff2, m_w_ff2, v_w_ff2))))
    res = {}

    def update(k, p, rx):
        w, m, v = (t[0] for t in wmv[k])
        n = w.shape[1]
        ins = [(t, n, 0) for t in (w, m, v)] + [(p, n, 0, 0)] + [(rx, n, 0, i) for i in range(3)]
        res[k] = [t[None] for t in _rowwise(update_fn, ins, [], [(n, F32)] * 4, [], "update_" + k)]

    for k in wnames:
        update(k, part[k], recv2[k])
    (part_main, part_mid), (rx_main, rx_mid) = _rs_split_wait(rs_in, [grad_x] + [res[k][0] for k in wnames], "rs_in_wait")
    g_main, = _rowwise(lambda *p: ((sum4(*p),), ()), [(part_main, wm, 0, 0)] + [(rx_main, wm, 0, i) for i in range(3)], [],
                       [(wm, F32)], [], "sum_in_main")
    g_mid, = _rowwise(lambda *p: ((sum4(*p),), ()), [(part_mid, LANES, 0, 0)] + [(rx_mid, LANES, 0, i) for i in range(3)], [],
                      [(LANES, F32)], [], "sum_in_mid")
    g_in = jnp.where(cc == 0, jnp.concatenate([g_main, g_mid[:, :LANES // 2]], axis=1),
                     jnp.concatenate([g_mid[:, LANES // 2:], g_main], axis=1))

    def update_given(w, m, v, g):
        delta, mn, vn = _adamw(w, g, m, v)
        return (g, delta, mn, vn), ()

    res["in"] = [t[None] for t in _rowwise(update_given, [(t, BW, 0) for t in (w_in[0], m_w_in[0], v_w_in[0], g_in)], [],
                                           [(BW, F32)] * 4, [], "update_in")]

    d_sinks = d_sk.reshape(1, AH)
    (d_table_t,) = _whole(lambda db, bk: (_dot(db, _onehot(bk), NT, precision=HIGHEST),),
                          [d_bias.reshape(AH, AT_BLOCK * 2 * AT_BLOCK), bucket], [((AH, N_BUCKETS), F32)], "bias_bwd")
    smalls = [d_g1n, d_g2n, d_lb, d_gout_h, d_qg, d_kg, d_sinks, d_table_t.T.reshape(1, N_BUCKETS * AH)]
    widths = [s.shape[1] for s in smalls]
    lanes = [-(-w // LANES) * LANES for w in widths]
    smalls = [jnp.pad(s, ((0, 0), (0, p - w))) for s, w, p in zip(smalls, widths, lanes)]
    tail_row = jnp.concatenate([d_shift1, d_scale1, d_gate1, d_shift2, d_scale2, d_gate2] + smalls, axis=1)
    (tail_row,) = _behind([tail_row], [g_mid])
    tail_all = _gather_small(tail_row, me, "gather_tail")[:, 0, :]
    d_ada_all, packed = tail_all[:, :6 * D], tail_all[:, 6 * D:]
    d_ada_cols = lax.dynamic_slice(d_ada_all, (0, me * ADA_N), (N_DEV, ADA_N))

    def ada_update(cv, dav, w, m, v):
        g = _bdot(_silu(cv), dav, TN)
        delta, mn, vn = _adamw(w, g, m, v)
        return (g, delta, mn, vn), ()

    res["ada"] = [t[None] for t in _ada_update_call(ada_update, c_all, d_ada_cols, w_ada[0], m_w_ada[0], v_w_ada[0], _tile(D, 256, 16))]

    offs = [sum(lanes[:i]) for i in range(len(lanes))]

    def small_update(pk, dada, lg, *wmv_flat):
        tot = pk[0:1]
        for d in range(1, N_DEV):
            tot = tot + pk[d:d + 1]
        gb = dada[0:1]
        for d in range(1, N_DEV):
            gb = gb + dada[d:d + 1]
        gs = [tot[:, offs[i]:offs[i] + widths[i]] for i in range(len(widths))]
        _, lb_vjp = jax.vjp(_softmax0, lg)
        (g_lg,) = lb_vjp(gs[2])
        grads = [gb, gs[0], gs[1], g_lg, gs[3], gs[4], gs[5], gs[6], gs[7]]
        outs = []
        for i, g in enumerate(grads):
            w, m, v = wmv_flat[3 * i:3 * i + 3]
            delta, mn, vn = _adamw(w, g, m, v)
            outs += [g, delta, mn, vn]
        return tuple(outs)

    tbl = lambda t: t.reshape(1, N_BUCKETS * AH)
    small_wmv = [(b_ada, m_b_ada, v_b_ada), (norm1_g, m_norm1_g, v_norm1_g), (norm2_g, m_norm2_g, v_norm2_g),
                 (hg_lb_logits, m_hg_lb_logits, v_hg_lb_logits), (hg_out_norm_g, m_hg_out_norm_g, v_hg_out_norm_g),
                 (q_norm_g, m_q_norm_g, v_q_norm_g), (k_norm_g, m_k_norm_g, v_k_norm_g),
                 (attn_sinks, m_attn_sinks, v_attn_sinks),
                 (tbl(rel_bias_table), tbl(m_rel_bias_table), tbl(v_rel_bias_table))]
    flat = [t for trip in small_wmv for t in trip]
    out_shapes = [(trip[0].shape, F32) for trip in small_wmv for _ in range(4)]
    sres = _whole(small_update, [packed, d_ada_all, hg_lb_logits] + flat, out_shapes, "small_update")
    names_small = ("b_ada", "norm1_g", "norm2_g", "lb", "gout", "qg", "kg", "sinks", "table")
    for i, k in enumerate(names_small):
        r = sres[4 * i:4 * i + 4]
        if k == "table":
            r = [t.reshape(N_BUCKETS, AH) for t in r]
        res[k] = r

    order = ("ada", "b_ada", "norm1_g", "norm2_g", "in", "lb", "gout", "qg", "kg", "sinks", "table", "bhg", "bat", "out", "ff1", "ff2")
    outs = [loss, grad_x[None]]
    for j in range(4):
        outs += [res[k][j] for k in order]
    return tuple(outs)
```

```python
import functools
import math

import jax
import jax.numpy as jnp
from jax import lax
from jax.experimental import pallas as pl
from jax.experimental.pallas import tpu as pltpu

F32 = jnp.float32
BF16 = jnp.bfloat16
EPS = 1e-6
NEG_INF = -1e30
HG_DK = 128
HG_CHUNK = 64
AT_BLOCK = 128
N_BUCKETS = 32
MAX_EXACT = 16
MAX_DISTANCE = 128
N_DEV = 8
LANES = 128
VMEM_LIMIT = 56 * 1024 * 1024
ADAM_LR, ADAM_B1, ADAM_B2, ADAM_EPS, ADAM_WD, ADAM_STEP = 0.001, 0.9, 0.999, 1e-08, 0.01, 10
HIGHEST = lax.Precision.HIGHEST
MESH = pl.DeviceIdType.MESH
ANY = pl.BlockSpec(memory_space=pl.ANY)
CHIP_RELS = (0, 4, 2, 6)

NN = (((1,), (0,)), ((), ()))
NT = (((1,), (1,)), ((), ()))
TN = (((0,), (0,)), ((), ()))


def _tile(n, pref, unit):
    if n <= pref:
        return n
    t = (pref // unit) * unit
    while t >= unit:
        if n % t == 0:
            return t
        t -= unit
    return n


def _dot(a, b, dn, precision=None):
    return lax.dot_general(a, b, dn, preferred_element_type=F32, precision=precision)


def _bdot(a, b, dn):
    return _dot(a.astype(BF16), b.astype(BF16), dn)


def _position():
    x, y, c = lax.axis_index("x"), lax.axis_index("y"), lax.axis_index("c")
    return dict(x=x, y=y, c=c, me=4 * x + 2 * y + c)


def _peer_position(p, rel):
    x = 1 - p["x"] if rel & 4 else p["x"]
    y = 1 - p["y"] if rel & 2 else p["y"]
    c = 1 - p["c"] if rel & 1 else p["c"]
    return dict(x=x, y=y, c=c, me=4 * x + 2 * y + c)


class _Comm:
    def __init__(self):
        self.ins, self.outs, self.alias, self.plans, self.res = [], [], {}, [], None

    def inp(self, arr):
        self.ins.append(arr)
        return ("i", len(self.ins) - 1)

    def out(self, shape, dtype, alias=None):
        self.outs.append(jax.ShapeDtypeStruct(tuple(shape), dtype))
        if alias is not None:
            self.alias[alias[1]] = len(self.outs) - 1
        return ("o", len(self.outs) - 1)

    def copy(self, src, src_view, dst, dst_view, rel):
        self.plans.append((src, src_view, dst, dst_view, rel))

    def result(self, handle):
        return self.res[handle[1]]

    def build(self, in_refs, out_refs, send_sems, recv_sems):
        pos = _position()
        ref = lambda h: in_refs[h[1]] if h[0] == "i" else out_refs[h[1]]
        ops = []
        for k, (src, sv, dst, dv, rel) in enumerate(self.plans):
            s = sv(ref(src), pos)
            if rel == 0:
                cp = pltpu.make_async_copy(s, dv(ref(dst), pos), send_sems.at[k])
                ops.append((cp.start, cp.wait))
                continue
            peer = _peer_position(pos, rel)
            mk = lambda d: pltpu.make_async_remote_copy(
                src_ref=s, dst_ref=d, send_sem=send_sems.at[k], recv_sem=recv_sems.at[k],
                device_id=(peer["x"], peer["y"], peer["c"]), device_id_type=MESH)
            out_cp, in_cp = mk(dv(ref(dst), pos)), mk(dv(ref(dst), peer))

            def wait(out_cp=out_cp, in_cp=in_cp):
                out_cp.wait_send()
                in_cp.wait_recv()

            ops.append((out_cp.start, wait))
        return ops


def _call(body, args, *, name, out_shape, in_specs=None, out_specs=None, grid=None, scratch_shapes=(), comm=None,
          prefetch=None, aliases=None, after=()):
    single = not isinstance(out_shape, (tuple, list))
    out_shape = (out_shape,) if single else tuple(out_shape)
    n_in, n_out, n_scr = len(args), len(out_shape), len(scratch_shapes)
    vm = pl.BlockSpec(memory_space=pltpu.VMEM)
    in_specs = [vm] * n_in if in_specs is None else list(in_specs)
    out_specs = [vm] * n_out if out_specs is None else (list(out_specs) if isinstance(out_specs, (tuple, list)) else [out_specs])
    n_pf = 0 if prefetch is None else len(prefetch)
    kw = {} if aliases is None else {"input_output_aliases": dict(aliases)}
    if comm is None and after:
        n_dep = len(after)

        def fn(*refs):
            body(*refs[:n_pf + n_in], *refs[n_pf + n_in + n_dep:])

        all_args, all_scratch = list(args) + list(after), list(scratch_shapes)
        in_specs = in_specs + [ANY] * n_dep
    elif comm is None:
        fn = body
        all_args, all_scratch = list(args), list(scratch_shapes)
    else:
        n_ci, n_co, n_x = len(comm.ins), len(comm.outs), len(comm.plans)

        def fn(*refs):
            pf, refs = refs[:n_pf], refs[n_pf:]
            o_in, c_in = refs[:n_in], refs[n_in:n_in + n_ci]
            o_out = refs[n_in + n_ci:n_in + n_ci + n_out]
            c_out = refs[n_in + n_ci + n_out:n_in + n_ci + n_out + n_co]
            scr = refs[n_in + n_ci + n_out + n_co:]
            ops = comm.build(c_in, c_out, scr[n_scr], scr[n_scr + 1])
            if grid:
                first = functools.reduce(jnp.logical_and, [pl.program_id(i) == 0 for i in range(len(grid))])
                last = functools.reduce(jnp.logical_and, [pl.program_id(i) == g - 1 for i, g in enumerate(grid)])

                @pl.when(first)
                def _():
                    for start, _w in ops:
                        start()
            else:
                for start, _w in ops:
                    start()
            body(*pf, *o_in, *o_out, *scr[:n_scr])
            if grid:
                @pl.when(last)
                def _():
                    for _s, wait in ops:
                        wait()
            else:
                for _s, wait in ops:
                    wait()

        all_args = list(args) + list(comm.ins)
        in_specs = in_specs + [ANY] * n_ci
        out_shape = out_shape + tuple(comm.outs)
        out_specs = out_specs + [ANY] * n_co
        all_scratch = list(scratch_shapes) + [pltpu.SemaphoreType.DMA((n_x,)), pltpu.SemaphoreType.DMA((n_x,))]
        kw["input_output_aliases"] = {n_pf + n_in + i: n_out + o for i, o in comm.alias.items()}
    sem = None if grid is None else ("arbitrary",) * len(grid)
    params = pltpu.CompilerParams(dimension_semantics=sem, vmem_limit_bytes=VMEM_LIMIT)
    if prefetch is None:
        spec = dict(in_specs=in_specs, out_specs=tuple(out_specs), scratch_shapes=all_scratch)
        if grid is not None:
            spec["grid"] = grid
    else:
        spec = dict(grid_spec=pltpu.PrefetchScalarGridSpec(
            num_scalar_prefetch=n_pf, grid=grid, in_specs=in_specs, out_specs=tuple(out_specs), scratch_shapes=all_scratch))
        all_args = list(prefetch) + all_args
    res = pl.pallas_call(fn, name=name, out_shape=out_shape, compiler_params=params, **spec, **kw)(*all_args)
    res = list(res)
    if comm is not None:
        comm.res = res[n_out:]
        res = res[:n_out]
    return res[0] if single else res


def _whole_view(ref, pos):
    return ref


def _block_view(axis, n, index, rows=None):
    def view(ref, pos):
        off = pl.multiple_of(index(pos) * n, n)
        if rows is None:
            return ref.at[:, pl.ds(off, n)] if axis == 1 else ref.at[pl.ds(off, n), :]
        lo, cnt = rows[0], rows[1] - rows[0]
        if axis == 1:
            return ref.at[pl.ds(lo, cnt), pl.ds(off, n)]
        return ref.at[pl.ds(pl.multiple_of(off + lo, 16), cnt), :]
    return view


def _rows_view(rows):
    def view(ref, pos):
        return ref if rows is None else ref.at[pl.ds(rows[0], rows[1] - rows[0]), :]
    return view


def _slot_view(i, rows=None):
    def view(ref, pos):
        return ref.at[i] if rows is None else ref.at[i, pl.ds(rows[0], rows[1] - rows[0]), :]
    return view


def _exchange(items, name):
    cm = _Comm()
    for a, rel in items:
        cm.copy(cm.inp(a), _whole_view, cm.out(a.shape, a.dtype), _whole_view, rel)
    _call(lambda: None, [], name=name, out_shape=(), comm=cm)
    return cm.res


def _gather_small(v, me, name):
    cm = _Comm()
    hi, ho = cm.inp(v), cm.out((N_DEV,) + v.shape, v.dtype)
    for rel in range(N_DEV):
        cm.copy(hi, _whole_view, ho, lambda ref, p: ref.at[p["me"]], rel)
    _call(lambda: None, [], name=name, out_shape=(), comm=cm)
    return cm.result(ho)


def _ag_ici(cm, blk, axis, rows=None, into=None):
    n = blk.shape[axis]
    shape = list(blk.shape)
    shape[axis] = n * N_DEV
    hi = cm.inp(blk)
    ho = cm.out(shape, blk.dtype) if into is None else cm.out(shape, blk.dtype, alias=cm.inp(into))
    own = _block_view(axis, n, lambda p: p["me"], rows)
    for rel in CHIP_RELS:
        cm.copy(hi, _rows_view(rows), ho, own, rel)
    return ho


def _ag_d2d(cm, full, axis):
    n = full.shape[axis] // N_DEV
    hi = cm.inp(full)
    ho = cm.out(full.shape, full.dtype, alias=hi)
    for r in CHIP_RELS:
        v = _block_view(axis, n, functools.partial(lambda p, r: p["me"] ^ r, r=r))
        cm.copy(hi, v, ho, v, 1)
    return ho


def _rs_d2d(cm, gw, axis):
    n = gw.shape[axis] // N_DEV
    shape = list(gw.shape)
    shape[axis] = n
    hi, ho = cm.inp(gw), cm.out([4] + shape, gw.dtype)
    for i, r in enumerate(CHIP_RELS):
        cm.copy(hi, _block_view(axis, n, functools.partial(lambda p, r: p["me"] ^ r ^ 1, r=r)), ho, _slot_view(i), 1)
    return ho


def _rs_ici(cm, part, rows=None, recv=None):
    if recv is None:
        ho = cm.out((3,) + part.shape[1:], part.dtype)
    else:
        ho = cm.out(recv.shape, recv.dtype, alias=cm.inp(recv))
    hi = cm.inp(part)
    for i in (1, 2, 3):
        cm.copy(hi, _slot_view(i, rows), ho, _slot_view(i - 1, rows), CHIP_RELS[i])
    return ho


def _rs_add(gw, recv, axis, base, name, tw=None):
    _, R, n = recv.shape
    fan = 1
    if axis == 1:
        tw = n if tw is None else tw
        fan = max(f for f in (4, 3, 2, 1) if (n // tw) % f == 0)
        gw_specs = [pl.BlockSpec((R, tw), functools.partial(lambda i, t, b, k: (0, b[i] + fan * t + k), k=k)) for k in range(fan)]
        rv_spec = pl.BlockSpec((None, R, tw * fan), lambda i, t, b: (i, 0, t))
        grid = (4, n // (tw * fan))
    else:
        tw = _tile(n, 1024, LANES)
        gw_specs = [pl.BlockSpec((R, tw), lambda i, t, b: (b[i], t))]
        rv_spec = pl.BlockSpec((None, R, tw), lambda i, t, b: (i, 0, t))
        grid = (4, n // tw)

    def body(b_ref, *refs):
        g_refs, r_ref, o_ref = refs[:fan], refs[fan], refs[fan + 1]
        g = g_refs[0][...] if fan == 1 else jnp.concatenate([g[...] for g in g_refs], axis=1)
        o_ref[...] = (g.astype(F32) + r_ref[...].astype(F32)).astype(o_ref.dtype)

    return _call(body, [gw] * fan + [recv], name=name, out_shape=jax.ShapeDtypeStruct(recv.shape, recv.dtype), grid=grid,
                 in_specs=gw_specs + [rv_spec], out_specs=rv_spec, prefetch=[base])


HBM_SPEC = pl.BlockSpec(memory_space=pltpu.HBM)
SEM_SPEC = pl.BlockSpec(memory_space=pltpu.SEMAPHORE)
SPLIT_PARAMS = pltpu.CompilerParams(has_side_effects=pltpu.SideEffectType.DATAFLOW_SIDE_EFFECTING)


def _split_copies(refs, plans, send_sems, recv_sems):
    pos = _position()
    out = []
    for k, (si, sv, li, lv, rel) in enumerate(plans):
        peer = _peer_position(pos, rel)
        mk = lambda d: pltpu.make_async_remote_copy(
            src_ref=sv(refs[si], pos), dst_ref=d, send_sem=send_sems.at[k], recv_sem=recv_sems.at[k],
            device_id=(peer["x"], peer["y"], peer["c"]), device_id_type=MESH)
        out.append((mk(lv(refs[li], pos)), mk(lv(refs[li], peer))))
    return out


def _split_start(arrays, plans, name):
    n = len(arrays)

    def body(*refs):
        send_sems, recv_sems = refs[n], refs[n + 1]
        for out_cp, _ in _split_copies(refs[:n], plans, send_sems, recv_sems):
            out_cp.start()
        refs[-1][...] = jnp.zeros_like(refs[-1])

    sems = pltpu.SemaphoreType.DMA((len(plans),))
    res = pl.pallas_call(
        body, name=name,
        out_shape=(sems, sems) + tuple(pltpu.HBM(a.shape, a.dtype) for a in arrays) + (jax.ShapeDtypeStruct((8, LANES), F32),),
        in_specs=[HBM_SPEC] * n, out_specs=(SEM_SPEC, SEM_SPEC) + (HBM_SPEC,) * n + (pl.BlockSpec(memory_space=pltpu.VMEM),),
        input_output_aliases={i: 2 + i for i in range(n)}, compiler_params=SPLIT_PARAMS,
    )(*[pltpu.with_memory_space_constraint(a, pltpu.HBM) for a in arrays])
    return res[0], res[1], list(res[2:2 + n]), res[-1]


def _split_wait(send_sems, recv_sems, arrays, plans, after, name):
    n, na = len(arrays), len(after)

    def body(*refs):
        for out_cp, in_cp in _split_copies(refs[:n], plans, refs[n], refs[n + 1]):
            out_cp.wait_send()
            in_cp.wait_recv()

    res = pl.pallas_call(
        body, name=name, out_shape=tuple(pltpu.HBM(a.shape, a.dtype) for a in arrays),
        in_specs=[HBM_SPEC] * n + [SEM_SPEC, SEM_SPEC] + [ANY] * na, out_specs=(HBM_SPEC,) * n,
        input_output_aliases={i: i for i in range(n)}, compiler_params=SPLIT_PARAMS,
    )(*arrays, send_sems, recv_sems, *after)
    return list(res)


def _rs_split_start(parts, name):
    nw = len(parts)
    lands = [lax.empty((3,) + p.shape[1:], p.dtype) for p in parts]
    plans = [(s, _slot_view(i), nw + s, _slot_view(i - 1), CHIP_RELS[i]) for s in range(nw) for i in (1, 2, 3)]
    send_sems, recv_sems, arrays, token = _split_start(list(parts) + lands, plans, name)
    return dict(sems=(send_sems, recv_sems), arrays=arrays, plans=plans, token=token, nw=nw)


def _rs_split_wait(h, after, name):
    arrays = _split_wait(h["sems"][0], h["sems"][1], h["arrays"], h["plans"], after, name)
    return arrays[:h["nw"]], arrays[h["nw"]:]


def _behind(xs, tokens):
    out = lax.optimization_barrier((tuple(xs), tuple(tokens)))
    return list(out[0])


def _ag_w_in(src, a, D, INW):
    wm = LANES * a

    hd = D // 2
    ALL, TOP, BOT = (0, D), (0, hd), (hd, D)

    def main_place(ref, p, rows=ALL):
        off = pl.multiple_of(((2 * a + 1) * (p["me"] // 2) + (a + 1) * p["c"]) * LANES, LANES)
        return ref.at[pl.ds(rows[0], rows[1] - rows[0]), pl.ds(off, wm)]

    def main_src(ref, p):
        return ref.at[:, pl.ds(pl.multiple_of(p["c"] * LANES, LANES), wm)]

    def mid_src(ref, p):
        return ref.at[:, pl.ds(pl.multiple_of((1 - p["c"]) * wm, LANES), LANES)]

    def mid_place(ref, p, rows=ALL):
        return ref.at[p["me"], pl.ds(rows[0], rows[1] - rows[0]), :]

    def body(src_ref, full_ref, mid_ref, send_sems, recv_sems):
        pos = _position()
        sib, xn, yn = (_peer_position(pos, r) for r in (1, 4, 2))
        dg = _peer_position(pos, 6)
        started = []

        def remote(k, s, d, to):
            return pltpu.make_async_remote_copy(src_ref=s, dst_ref=d, send_sem=send_sems.at[k], recv_sem=recv_sems.at[k],
                                                device_id=(to["x"], to["y"], to["c"]), device_id_type=MESH)

        def send(k, owner, rows, to, from_src=False):
            for j, (src_v, place) in enumerate(((main_src, main_place), (mid_src, mid_place))):
                s = src_v(src_ref, pos) if from_src else place(full_ref if j == 0 else mid_ref, owner, rows)
                cp = remote(k + j, s, place(full_ref if j == 0 else mid_ref, owner, rows), to)
                cp.start()
                started.append(cp)

        def landed(k, owner, rows, frm):
            for j, place in enumerate((main_place, mid_place)):
                ref = full_ref if j == 0 else mid_ref
                remote(k + j, place(ref, owner, rows), place(ref, owner, rows), frm).wait_recv()

        local = [pltpu.make_async_copy(main_src(src_ref, pos), main_place(full_ref, pos), send_sems.at[18]),
                 pltpu.make_async_copy(mid_src(src_ref, pos), mid_place(mid_ref, pos), send_sems.at[19])]
        for cp in local:
            cp.start()
        send(0, pos, ALL, sib, from_src=True)
        send(2, pos, ALL, xn, from_src=True)
        send(4, pos, ALL, yn, from_src=True)
        landed(2, xn, ALL, xn)
        send(10, xn, ALL, sib)
        send(6, xn, TOP, yn)
        landed(4, yn, ALL, yn)
        send(12, yn, ALL, sib)
        send(8, yn, BOT, xn)
        landed(6, dg, TOP, yn)
        send(14, dg, TOP, sib)
        landed(8, dg, BOT, xn)
        send(16, dg, BOT, sib)
        sib_of = lambda p: _peer_position(p, 1)
        landed(0, sib, ALL, sib)
        landed(10, sib_of(xn), ALL, sib)
        landed(12, sib_of(yn), ALL, sib)
        landed(14, sib_of(dg), TOP, sib)
        landed(16, sib_of(dg), BOT, sib)
        for cp in started:
            cp.wait_send()
        for cp in local:
            cp.wait()

    return _call(body, [src], name="ag_w_in", in_specs=[ANY], out_specs=[ANY, ANY],
                 out_shape=(jax.ShapeDtypeStruct((D, INW), BF16), jax.ShapeDtypeStruct((N_DEV, D, LANES), BF16)),
                 scratch_shapes=[pltpu.SemaphoreType.DMA((20,)), pltpu.SemaphoreType.DMA((20,))])


def _patch_mid(full, mid, a):
    D = full.shape[0]

    def body(full_ref, e_ref, o_ref, out_ref):
        out_ref[...] = e_ref[...] + o_ref[...]

    return _call(body, [full, mid, mid], name="patch_mid", grid=(N_DEV // 2,),
                 out_shape=jax.ShapeDtypeStruct(full.shape, full.dtype),
                 in_specs=[ANY, pl.BlockSpec((None, D, LANES), lambda j: (2 * j, 0, 0)),
                           pl.BlockSpec((None, D, LANES), lambda j: (2 * j + 1, 0, 0))],
                 out_specs=pl.BlockSpec((D, LANES), lambda j: (0, (2 * a + 1) * j + a)), aliases={0: 0})


MM_RESIDENT = 2048


def _mm(a, b, mode, out_dtype, name, b_off=0, n=None, comm=None, extras=(), epi=None, tn=None, after=()):
    if mode == "nn":
        (M, K), (K2, N) = a.shape, b.shape
    elif mode == "nt":
        (M, K), (N, K2) = a.shape, b.shape
    else:
        (K, M), (K2, N) = a.shape, b.shape
    assert K == K2, (a.shape, b.shape, mode)
    if n is not None:
        N = n
    single = not isinstance(out_dtype, (tuple, list))
    out_dtypes = (out_dtype,) if single else tuple(out_dtype)
    if epi is None:
        epi = lambda r: (r,)
    tk = K if K <= MM_RESIDENT else (MM_RESIDENT if K % MM_RESIDENT == 0 else _tile(K, 512, LANES))
    nk = K // tk
    if M > MM_RESIDENT and mode == "tn" and N <= MM_RESIDENT and not b_off:
        tm, tn = _tile(M, 512, LANES), N
    elif nk > 1:
        tm, tn = _tile(M, 1024, LANES), _tile(N, tn or 1024, LANES)
    else:
        tm = _tile(M, MM_RESIDENT, LANES)
        tn = _tile(math.gcd(N, b_off) if b_off else N, tn or 512, LANES)
    jb = b_off // tn
    dn = {"nn": NN, "nt": NT, "tn": TN}[mode]
    ne, no = len(extras), len(out_dtypes)

    def body(a_ref, b_ref, *rest):
        e_refs, o_refs = rest[:ne], rest[ne:ne + no]

        def finish(r):
            for o_ref, v in zip(o_refs, epi(r, *[e[...] for e in e_refs])):
                o_ref[...] = v.astype(o_ref.dtype)

        if nk == 1:
            finish(_bdot(a_ref[...], b_ref[...], dn))
            return
        acc_ref = rest[ne + no]
        k = pl.program_id(2)

        @pl.when(k == 0)
        def _():
            acc_ref[...] = _bdot(a_ref[...], b_ref[...], dn)

        @pl.when(jnp.logical_and(k > 0, k < nk - 1))
        def _():
            acc_ref[...] += _bdot(a_ref[...], b_ref[...], dn)

        @pl.when(k == nk - 1)
        def _():
            finish(acc_ref[...] + _bdot(a_ref[...], b_ref[...], dn))

    a_spec = pl.BlockSpec((tk, tm), lambda i, j, k: (k, i)) if mode == "tn" else pl.BlockSpec((tm, tk), lambda i, j, k: (i, k))
    b_spec = pl.BlockSpec((tn, tk), lambda i, j, k: (j, k)) if mode == "nt" else pl.BlockSpec((tk, tn), lambda i, j, k: (k, j + jb))
    o_spec = pl.BlockSpec((tm, tn), lambda i, j, k: (i, j))
    res = _call(body, [a, b] + list(extras), name=name, grid=(M // tm, N // tn, nk),
                out_shape=tuple(jax.ShapeDtypeStruct((M, N), dt) for dt in out_dtypes),
                in_specs=[a_spec, b_spec] + [o_spec] * ne, out_specs=[o_spec] * no,
                scratch_shapes=[pltpu.VMEM((tm, tn), F32)] if nk > 1 else [], comm=comm, after=after)
    return res[0] if single else res


def _rowwise(fn, row_ins, bcast_ins, row_outs, acc_outs, name, rt=256, comm=None):
    L = row_ins[0][0].shape[-2]
    rt = _tile(L, rt, 16)
    nr, nb, no = len(row_ins), len(bcast_ins), len(row_outs)

    def body(*refs):
        i = pl.program_id(0)
        vals = [r[...] for r in refs[:nr + nb]]
        outs, accs = fn(*vals)
        for r, v in zip(refs[nr + nb:nr + nb + no], outs):
            r[...] = v.astype(r.dtype)
        acc_refs = refs[nr + nb + no:]

        @pl.when(i == 0)
        def _():
            for r in acc_refs:
                r[...] = jnp.zeros_like(r)

        for r, v in zip(acc_refs, accs):
            r[...] += v

    in_specs = []
    for spec in row_ins:
        w, cb = spec[1], spec[2]
        if len(spec) == 4:
            in_specs.append(pl.BlockSpec((None, rt, w), functools.partial(lambda i, cb, ld: (ld, i, cb), cb=cb, ld=spec[3])))
        else:
            in_specs.append(pl.BlockSpec((rt, w), functools.partial(lambda i, cb: (i, cb), cb=cb)))
    in_specs += [pl.BlockSpec(b.shape, lambda i: (0, 0)) for b in bcast_ins]
    out_specs = [pl.BlockSpec((rt, w), lambda i: (i, 0)) for w, _ in row_outs]
    out_specs += [pl.BlockSpec(s, lambda i: (0, 0)) for s in acc_outs]
    out_shape = [jax.ShapeDtypeStruct((L, w), dt) for w, dt in row_outs] + [jax.ShapeDtypeStruct(s, F32) for s in acc_outs]
    return _call(body, [s[0] for s in row_ins] + list(bcast_ins), name=name, grid=(L // rt,), out_shape=tuple(out_shape),
                 in_specs=in_specs, out_specs=out_specs, comm=comm)


def _whole(fn, ins, out_shapes, name):
    def body(*refs):
        outs = fn(*[r[...] for r in refs[:len(ins)]])
        for r, v in zip(refs[len(ins):], outs):
            r[...] = v.astype(r.dtype)

    return _call(body, list(ins), name=name, out_shape=tuple(jax.ShapeDtypeStruct(s, dt) for s, dt in out_shapes))


def _silu(x):
    return x * jax.nn.sigmoid(x)


def _rms(x, g):
    return (x * lax.rsqrt(jnp.mean(x * x, axis=-1, keepdims=True) + EPS)) * g


def _modnorm(x, g, shift, scale):
    return _rms(x, g) * (1.0 + scale) + shift


def _adamw(w, g, m, v):
    m = ADAM_B1 * m + (1.0 - ADAM_B1) * g
    v = ADAM_B2 * v + (1.0 - ADAM_B2) * jnp.square(g)
    m_hat = m / (1.0 - ADAM_B1 ** ADAM_STEP)
    v_hat = v / (1.0 - ADAM_B2 ** ADAM_STEP)
    delta = -ADAM_LR * (m_hat / (jnp.sqrt(v_hat) + ADAM_EPS) + ADAM_WD * w)
    return delta, m, v


def _lower_bound(lg):
    e = jnp.exp(lg - jnp.max(lg, axis=0, keepdims=True))
    return e[0:1] / jnp.sum(e, axis=0, keepdims=True)


def _hg_chunk(hq, hf, hi, lb, st):
    C = hq.shape[0]
    row = lax.broadcasted_iota(jnp.int32, (C, C), 0)
    col = lax.broadcasted_iota(jnp.int32, (C, C), 1)
    tri = row >= col
    sg = jax.nn.sigmoid(hf)
    f = lb + (1.0 - lb) * sg
    lf = jnp.log(f)
    k = 1.0 - f
    q = _silu(hq)
    b = _dot(tri.astype(F32), lf, NN, precision=HIGHEST)
    m = b[C // 2 - 1:C // 2]
    bl = b[C - 1:C]
    e_qm, e_km, e_kl, e_q = jnp.exp(b - m), jnp.exp(m - b), jnp.exp(bl - b), jnp.exp(b)
    qe, ke, kd, qb = q * e_qm, k * e_km, k * e_kl, q * e_q
    sc = jnp.where(tri, _bdot(qe, ke, NT), 0.0)
    o = _bdot(sc, hi, NN) + _bdot(qb, st, NT)
    dec = jnp.exp(bl)
    st_next = st * dec + _bdot(hi, kd, TN)
    return o, st_next, dict(tri=tri, sg=sg, f=f, k=k, q=q, qe=qe, ke=ke, kd=kd, qb=qb, sc=sc, dec=dec,
                            e_qm=e_qm, e_km=e_km, e_kl=e_kl, e_q=e_q)


def _hg_out(o, hgate, gout):
    return _rms(o, gout) * _silu(hgate)


HG_GROUP = 32


def _hgrn_fwd(p4, lb_logits, gout, H, comm=None):
    L = p4.shape[0]
    C = HG_CHUNK
    GR = _tile(L // C, HG_GROUP, 1)
    T = GR * C
    N = L // T

    def body(hq_ref, hf_ref, hi_ref, hg_ref, lg_ref, gout_ref, o_ref, s_ref, st_ref):
        @pl.when(pl.program_id(1) == 0)
        def _():
            st_ref[...] = jnp.zeros_like(st_ref)

        lb = _lower_bound(lg_ref[...])
        st = st_ref[...]
        for ci in range(GR):
            rows = pl.ds(ci * C, C)
            s_ref[0, ci] = st
            o, st, _ = _hg_chunk(hq_ref[rows, :], hf_ref[rows, :], hi_ref[rows, :], lb, st)
            o_ref[rows, :] = _hg_out(o, hg_ref[rows, :], gout_ref[...]).astype(o_ref.dtype)
        st_ref[...] = st

    blk = lambda s: pl.BlockSpec((T, HG_DK), functools.partial(lambda h, n, s: (n, s * H + h), s=s))
    return _call(
        body, [p4, p4, p4, p4, lb_logits, gout], name="hgrn_fwd", grid=(H, N),
        out_shape=(jax.ShapeDtypeStruct((L, H * HG_DK), BF16), jax.ShapeDtypeStruct((H, N * GR, HG_DK, HG_DK), F32)),
        in_specs=[blk(0), blk(1), blk(2), blk(3), pl.BlockSpec((2, HG_DK), lambda h, n: (0, h)),
                  pl.BlockSpec((1, HG_DK), lambda h, n: (0, 0))],
        out_specs=(pl.BlockSpec((T, HG_DK), lambda h, n: (n, h)),
                   pl.BlockSpec((1, GR, HG_DK, HG_DK), lambda h, n: (h, n, 0, 0))),
        scratch_shapes=[pltpu.VMEM((HG_DK, HG_DK), F32)], comm=comm)


def _hgrn_bwd(p4, lb_logits, gout, s_all, d_out, H, comm=None):
    L = p4.shape[0]
    C = HG_CHUNK
    GR = _tile(L // C, HG_GROUP, 1)
    T = GR * C
    N = L // T

    def body(hq_ref, hf_ref, hi_ref, hg_ref, lg_ref, gout_ref, s_ref, do_ref,
             dq_ref, df_ref, di_ref, dg_ref, dlb_ref, dgo_ref, dst_ref):
        @pl.when(pl.program_id(1) == 0)
        def _():
            dst_ref[...] = jnp.zeros_like(dst_ref)
            dlb_ref[...] = jnp.zeros_like(dlb_ref)

        @pl.when(jnp.logical_and(pl.program_id(0) == 0, pl.program_id(1) == 0))
        def _():
            dgo_ref[...] = jnp.zeros_like(dgo_ref)

        lb = _lower_bound(lg_ref[...])
        dst = dst_ref[...]
        d_lb = jnp.zeros((1, HG_DK), F32)
        d_go = jnp.zeros((1, HG_DK), F32)
        for ci in reversed(range(GR)):
            rows = pl.ds(ci * C, C)
            dst, d_lb_c, d_go_c = chunk_bwd(rows, lb, s_ref[0, ci], dst, hq_ref, hf_ref, hi_ref, hg_ref, gout_ref, do_ref,
                                            dq_ref, df_ref, di_ref, dg_ref)
            d_lb += d_lb_c
            d_go += d_go_c
        dst_ref[...] = dst
        dlb_ref[...] += d_lb
        dgo_ref[...] += d_go

    def chunk_bwd(rows, lb, st, dst_next, hq_ref, hf_ref, hi_ref, hg_ref, gout_ref, do_ref, dq_ref, df_ref, di_ref, dg_ref):
        hq, hf, hi, hgate = hq_ref[rows, :], hf_ref[rows, :], hi_ref[rows, :], hg_ref[rows, :]
        o, _, t = _hg_chunk(hq, hf, hi, lb, st)
        _, out_vjp = jax.vjp(_hg_out, o, hgate, gout_ref[...])
        do, d_hgate, d_gout = out_vjp(do_ref[rows, :])
        tri = t["tri"]
        dsc = jnp.where(tri, _bdot(do, hi, NT), 0.0)
        dv = _bdot(t["sc"], do, TN) + _bdot(t["kd"], dst_next, NT)
        dqe = _bdot(dsc, t["ke"], NN)
        dke = _bdot(dsc, t["qe"], TN)
        dqb = _bdot(do, st, NN)
        dkd = _bdot(hi, dst_next, NN)
        ddec = jnp.sum(dst_next * st, axis=0, keepdims=True)
        dst_prev = _bdot(do, t["qb"], TN) + dst_next * t["dec"]
        dq = dqe * t["e_qm"] + dqb * t["e_q"]
        dk = dke * t["e_km"] + dkd * t["e_kl"]
        tq, tk, td, tb = dqe * t["qe"], dke * t["ke"], dkd * t["kd"], dqb * t["qb"]
        db = tq - tk - td + tb
        dm = jnp.sum(tk - tq, axis=0, keepdims=True)
        dbl = jnp.sum(td, axis=0, keepdims=True) + ddec * t["dec"]
        rowi = lax.broadcasted_iota(jnp.int32, (C, HG_DK), 0)
        db = db + jnp.where(rowi == C // 2 - 1, dm, 0.0) + jnp.where(rowi == C - 1, dbl, 0.0)
        dlf = _dot(tri.astype(F32), db, TN, precision=HIGHEST)
        df = dlf / t["f"] - dk
        sg = t["sg"]
        df_ref[rows, :] = (df * (1.0 - lb) * sg * (1.0 - sg)).astype(df_ref.dtype)
        sq = jax.nn.sigmoid(hq)
        dq_ref[rows, :] = (dq * (sq * (1.0 + hq * (1.0 - sq)))).astype(dq_ref.dtype)
        di_ref[rows, :] = dv.astype(di_ref.dtype)
        dg_ref[rows, :] = d_hgate.astype(dg_ref.dtype)
        return dst_prev, jnp.sum(df * (1.0 - sg), axis=0, keepdims=True), d_gout

    blk = lambda s: pl.BlockSpec((T, HG_DK), functools.partial(lambda h, n, s: (N - 1 - n, s * H + h), s=s))
    oblk = pl.BlockSpec((T, HG_DK), lambda h, n: (N - 1 - n, h))
    vec = pl.BlockSpec((1, HG_DK), lambda h, n: (0, h))
    W = H * HG_DK
    return _call(
        body, [p4, p4, p4, p4, lb_logits, gout, s_all, d_out], name="hgrn_bwd", grid=(H, N),
        out_shape=tuple([jax.ShapeDtypeStruct((L, W), BF16)] * 4 + [jax.ShapeDtypeStruct((1, W), F32), jax.ShapeDtypeStruct((1, HG_DK), F32)]),
        in_specs=[blk(0), blk(1), blk(2), blk(3), pl.BlockSpec((2, HG_DK), lambda h, n: (0, h)),
                  pl.BlockSpec((1, HG_DK), lambda h, n: (0, 0)),
                  pl.BlockSpec((1, GR, HG_DK, HG_DK), lambda h, n: (h, N - 1 - n, 0, 0)), oblk],
        out_specs=(oblk, oblk, oblk, oblk, vec, pl.BlockSpec((1, HG_DK), lambda h, n: (0, 0))),
        scratch_shapes=[pltpu.VMEM((HG_DK, HG_DK), F32)], comm=comm)


def _bucket_ids():
    i = jnp.arange(AT_BLOCK, dtype=jnp.int32)[:, None]
    j = jnp.arange(2 * AT_BLOCK, dtype=jnp.int32)[None, :]
    n = jnp.maximum(i - j + AT_BLOCK, 0)
    nf = jnp.maximum(n, 1).astype(F32)
    large = MAX_EXACT + (jnp.log(nf / MAX_EXACT) / math.log(MAX_DISTANCE / MAX_EXACT) * (N_BUCKETS - MAX_EXACT)).astype(jnp.int32)
    large = jnp.minimum(large, N_BUCKETS - 1)
    return jnp.where(n < MAX_EXACT, n, large).reshape(1, -1)


def _onehot(bucket):
    ids = lax.broadcasted_iota(jnp.int32, (N_BUCKETS, bucket.shape[1]), 0)
    return (ids == bucket).astype(F32)


def _attn_probs(qn, kpn, kcn, bias_g, sink, first, scale):
    rows = qn.shape[0]
    i = jnp.bitwise_and(lax.broadcasted_iota(jnp.int32, (rows, AT_BLOCK), 0), AT_BLOCK - 1)
    j = lax.broadcasted_iota(jnp.int32, (rows, AT_BLOCK), 1)
    lp = _bdot(qn, kpn, NT) * scale + bias_g[:, :AT_BLOCK]
    lc = _bdot(qn, kcn, NT) * scale + bias_g[:, AT_BLOCK:]
    lp = jnp.where(jnp.logical_and(j > i, jnp.logical_not(first)), lp, NEG_INF)
    lc = jnp.where(j <= i, lc, NEG_INF)
    m = jnp.maximum(jnp.maximum(jnp.max(lp, axis=-1, keepdims=True), jnp.max(lc, axis=-1, keepdims=True)), sink)
    pp, pc, ps = jnp.exp(lp - m), jnp.exp(lc - m), jnp.exp(sink - m)
    den = jnp.sum(pp, axis=-1, keepdims=True) + jnp.sum(pc, axis=-1, keepdims=True) + ps
    return pp / den, pc / den, ps / den


def _sink_rows(sk_ref, G):
    head = lax.broadcasted_iota(jnp.int32, (G * AT_BLOCK, 1), 0) // AT_BLOCK
    sink = jnp.zeros((G * AT_BLOCK, 1), F32)
    for g in range(G):
        sink = jnp.where(head == g, sk_ref[0, g:g + 1, :], sink)
    return sink


def _attn_fwd(q_t, kp, vp, qg, kg, sinks, bias, KVH, comm=None):
    AH, L, DH = q_t.shape
    G = AH // KVH
    NB = L // AT_BLOCK
    scale = DH ** -0.5

    def body(q_ref, kp_ref, kc_ref, vp_ref, vc_ref, qg_ref, kg_ref, sk_ref, b_ref, o_ref):
        first = pl.program_id(1) == 0
        kpn, kcn = _rms(kp_ref[0], kg_ref[...]), _rms(kc_ref[0], kg_ref[...])
        qn = _rms(q_ref[...].reshape(G * AT_BLOCK, DH), qg_ref[...])
        sink = _sink_rows(sk_ref, G)
        pp, pc, _ = _attn_probs(qn, kpn, kcn, b_ref[...].reshape(G * AT_BLOCK, 2 * AT_BLOCK), sink, first, scale)
        o = _bdot(pp, vp_ref[0], NN) + _bdot(pc, vc_ref[0], NN)
        o_ref[...] = o.reshape(G, AT_BLOCK, DH).astype(o_ref.dtype)

    kblk = lambda off: pl.BlockSpec((1, AT_BLOCK, DH),
                                    functools.partial(lambda h, n, off: (h, jnp.maximum(n + off - 1, 0), 0), off=off))
    return _call(
        body, [q_t, kp, kp, vp, vp, qg, kg, sinks, bias], name="attn_fwd", grid=(KVH, NB),
        out_shape=jax.ShapeDtypeStruct((AH, L, DH), BF16),
        in_specs=[pl.BlockSpec((G, AT_BLOCK, DH), lambda h, n: (h, n, 0)), kblk(0), kblk(1), kblk(0), kblk(1),
                  pl.BlockSpec((1, DH), lambda h, n: (0, 0)), pl.BlockSpec((1, DH), lambda h, n: (0, 0)),
                  pl.BlockSpec((1, G, 1), lambda h, n: (h, 0, 0)),
                  pl.BlockSpec((G, AT_BLOCK, 2 * AT_BLOCK), lambda h, n: (h, 0, 0))],
        out_specs=pl.BlockSpec((G, AT_BLOCK, DH), lambda h, n: (h, n, 0)), comm=comm)


def _attn_bwd(q_t, kp, vp, qg, kg, sinks, bias, do_t, KVH, comm=None):
    AH, L, DH = q_t.shape
    G = AH // KVH
    NB = L // AT_BLOCK
    B = AT_BLOCK
    scale = DH ** -0.5

    def body(q_ref, kp_ref, kc_ref, vp_ref, vc_ref, qg_ref, kg_ref, sk_ref, b_ref, do_ref,
             dq_ref, dk_ref, dv_ref, dqg_ref, dkg_ref, dsk_ref, db_ref):
        n = pl.program_id(1)
        first = n == 0

        @pl.when(first)
        def _():
            for r in (dk_ref, dv_ref, dsk_ref, db_ref):
                r[...] = jnp.zeros_like(r)

        @pl.when(jnp.logical_and(first, pl.program_id(0) == 0))
        def _():
            dqg_ref[...] = jnp.zeros_like(dqg_ref)
            dkg_ref[...] = jnp.zeros_like(dkg_ref)

        kp_raw, kc_raw, kgv, qgv = kp_ref[0], kc_ref[0], kg_ref[...], qg_ref[...]
        kpn, kp_vjp = jax.vjp(_rms, kp_raw, kgv)
        kcn, kc_vjp = jax.vjp(_rms, kc_raw, kgv)
        qn, q_vjp = jax.vjp(_rms, q_ref[...].reshape(G * B, DH), qgv)
        pp, pc, ps = _attn_probs(qn, kpn, kcn, b_ref[...].reshape(G * B, 2 * B), _sink_rows(sk_ref, G), first, scale)
        do = do_ref[...].reshape(G * B, DH)
        dvp = _bdot(pp, do, TN)
        dvc = _bdot(pc, do, TN)
        dpp = _bdot(do, vp_ref[0], NT)
        dpc = _bdot(do, vc_ref[0], NT)
        dsum = jnp.sum(dpp * pp, axis=-1, keepdims=True) + jnp.sum(dpc * pc, axis=-1, keepdims=True)
        dlp = pp * (dpp - dsum)
        dlc = pc * (dpc - dsum)
        dsk_ref[0] += jnp.sum((-ps * dsum).reshape(G, B, 1), axis=1)
        db_ref[:, :, :B] += dlp.reshape(G, B, B)
        db_ref[:, :, B:] += dlc.reshape(G, B, B)
        dlp, dlc = dlp * scale, dlc * scale
        dqn = _bdot(dlp, kpn, NN) + _bdot(dlc, kcn, NN)
        dq_raw, dqg = q_vjp(dqn)
        dq_ref[...] = dq_raw.reshape(G, B, DH).astype(dq_ref.dtype)
        dkp_raw, dkg_p = kp_vjp(_bdot(dlp, qn, TN))
        dkc_raw, dkg_c = kc_vjp(_bdot(dlc, qn, TN))
        r0 = pl.multiple_of(jnp.maximum(n - 1, 0) * B, B)
        r1 = pl.multiple_of(n * B, B)
        dk_ref[0, pl.ds(r0, B), :] += dkp_raw
        dk_ref[0, pl.ds(r1, B), :] += dkc_raw
        dv_ref[0, pl.ds(r0, B), :] += dvp
        dv_ref[0, pl.ds(r1, B), :] += dvc
        dqg_ref[...] += dqg
        dkg_ref[...] += dkg_p + dkg_c

    kblk = lambda off: pl.BlockSpec((1, B, DH), functools.partial(lambda h, n, off: (h, jnp.maximum(n + off - 1, 0), 0), off=off))
    qblk = pl.BlockSpec((G, B, DH), lambda h, n: (h, n, 0))
    accblk = pl.BlockSpec((1, L, DH), lambda h, n: (h, 0, 0))
    vecblk = pl.BlockSpec((1, DH), lambda h, n: (0, 0))
    return _call(
        body, [q_t, kp, kp, vp, vp, qg, kg, sinks, bias, do_t], name="attn_bwd", grid=(KVH, NB),
        out_shape=(jax.ShapeDtypeStruct((AH, L, DH), BF16), jax.ShapeDtypeStruct((KVH, L, DH), F32),
                   jax.ShapeDtypeStruct((KVH, L, DH), F32), jax.ShapeDtypeStruct((1, DH), F32),
                   jax.ShapeDtypeStruct((1, DH), F32), jax.ShapeDtypeStruct((KVH, G, 1), F32),
                   jax.ShapeDtypeStruct((AH, B, 2 * B), F32)),
        in_specs=[qblk, kblk(0), kblk(1), kblk(0), kblk(1),
                  pl.BlockSpec((1, DH), lambda h, n: (0, 0)), pl.BlockSpec((1, DH), lambda h, n: (0, 0)),
                  pl.BlockSpec((1, G, 1), lambda h, n: (h, 0, 0)),
                  pl.BlockSpec((G, B, 2 * B), lambda h, n: (h, 0, 0)), qblk],
        out_specs=(qblk, accblk, accblk, vecblk, vecblk, pl.BlockSpec((1, G, 1), lambda h, n: (h, 0, 0)),
                   pl.BlockSpec((G, B, 2 * B), lambda h, n: (h, 0, 0))), comm=comm)


def _heads_first(t, nh):
    L = t.shape[0]
    return jnp.transpose(t.reshape(L, nh, t.shape[1] // nh), (1, 0, 2))


def _heads_last(t):
    nh, L, dh = t.shape
    return jnp.transpose(t, (1, 0, 2)).reshape(L, nh * dh)


def _softmax0(lg):
    e = jnp.exp(lg - jnp.max(lg, axis=0, keepdims=True))
    return e[0:1] / jnp.sum(e, axis=0, keepdims=True)


def _ada_update_call(fn, c_all, d_cols, w, m, v, rt):
    D, n = w.shape

    def body(c_ref, d_ref, w_ref, m_ref, v_ref, g_out, dl_out, m_out, v_out):
        outs, _ = fn(c_ref[...], d_ref[...], w_ref[...], m_ref[...], v_ref[...])
        for r, val in zip((g_out, dl_out, m_out, v_out), outs):
            r[...] = val

    wblk = pl.BlockSpec((rt, n), lambda i: (i, 0))
    return _call(
        body, [c_all, d_cols, w, m, v], name="update_ada", grid=(D // rt,), out_shape=tuple([jax.ShapeDtypeStruct((D, n), F32)] * 4),
        in_specs=[pl.BlockSpec((N_DEV, rt), lambda i: (0, i)), pl.BlockSpec((N_DEV, n), lambda i: (0, 0)), wblk, wblk, wblk],
        out_specs=(wblk, wblk, wblk, wblk))


def kernel(x, c, w_ada, b_ada, norm1_g, norm2_g, w_in, hg_lb_logits, hg_out_norm_g, q_norm_g, k_norm_g, attn_sinks, rel_bias_table, w_branch_hg, w_branch_attn, w_out, w_ff1, w_ff2, loss_target, m_w_ada, m_b_ada, m_norm1_g, m_norm2_g, m_w_in, m_hg_lb_logits, m_hg_out_norm_g, m_q_norm_g, m_k_norm_g, m_attn_sinks, m_rel_bias_table, m_w_branch_hg, m_w_branch_attn, m_w_out, m_w_ff1, m_w_ff2, v_w_ada, v_b_ada, v_norm1_g, v_norm2_g, v_w_in, v_hg_lb_logits, v_hg_out_norm_g, v_q_norm_g, v_k_norm_g, v_attn_sinks, v_rel_bias_table, v_w_branch_hg, v_w_branch_attn, v_w_out, v_w_ff1, v_w_ff2):
    cc = lax.axis_index("c")
    me = 4 * lax.axis_index("x") + 2 * lax.axis_index("y") + cc
    x2 = x[0]
    tgt = loss_target[0]
    L, D = x2.shape
    HGW = hg_lb_logits.shape[1]
    H = HGW // HG_DK
    AH = attn_sinks.shape[1]
    DH = q_norm_g.shape[1]
    ATW = AH * DH
    BW = w_in.shape[2]
    INW = BW * N_DEV
    A = BW // LANES
    assert BW == LANES * A + LANES // 2
    KVW = (INW - 4 * HGW - ATW - 2 * D) // 2
    KVH = KVW // DH
    G = AH // KVH
    ADA_N = w_ada.shape[2]
    PAIR = 2 * A + 1

    c_all = _gather_small(c, me, "gather_c")[:, 0, :]
    b_cols = lax.dynamic_slice(b_ada, (0, me * ADA_N), (1, ADA_N))
    (ada_cols,) = _whole(lambda cv, w, b: (_bdot(_silu(cv), w, NN) + b,), [c_all, w_ada[0], b_cols],
                         [((N_DEV, ADA_N), F32)], "ada_fwd")
    ada_all = _gather_small(ada_cols, me, "gather_ada")
    ada_row = lax.dynamic_slice(ada_all, (0, me, 0), (N_DEV, 1, ADA_N)).reshape(1, 6 * D)

    w_in_b = w_in[0].astype(BF16)
    src_in = jnp.where(cc == 0, jnp.pad(w_in_b, ((0, 0), (0, LANES // 2))), jnp.pad(w_in_b, ((0, 0), (LANES // 2, 0))))
    (src_in,) = _behind([src_in], [ada_row])
    shift1, scale1, gate1, shift2, scale2, gate2 = [ada_row[:, i * D:(i + 1) * D] for i in range(6)]
    w_in_gapped, w_in_mid = _ag_w_in(src_in, A, D, INW)
    w_in_full = _patch_mid(w_in_gapped, w_in_mid, A)

    wnames = ("bhg", "bat", "out", "ff1", "ff2")
    small = ("bhg", "bat", "out")
    waxis = dict(zip(wnames, (1, 1, 0, 1, 0)))
    wsrc = dict(zip(wnames, (w_branch_hg, w_branch_attn, w_out, w_ff1, w_ff2)))
    wblk = {k: wsrc[k][0].astype(BF16) for k in wnames}
    wf = {}

    (h,) = _rowwise(lambda xv, g, sh, sc: ((_modnorm(xv, g, sh, sc),), ()), [(x2, D, 0)], [norm1_g, shift1, scale1],
                    [(D, BF16)], [], "norm1")
    o4, oa = 4 * HGW, 4 * HGW + ATW + 2 * KVW
    r1, r2, ro = wblk["ff1"].shape[0], wblk["ff2"].shape[0], wblk["out"].shape[0]
    cm = _Comm()
    hs = {k: _ag_ici(cm, wblk[k], waxis[k]) for k in ("bhg", "bat")}
    hs["out"] = _ag_ici(cm, wblk["out"], waxis["out"], rows=(0, ro // 2))
    p4 = _mm(h, w_in_full, "nn", F32, "proj_hg", n=o4, comm=cm)
    half = {k: cm.result(hs[k]) for k in hs}
    cm = _Comm()
    hs = {"out": _ag_ici(cm, wblk["out"], waxis["out"], rows=(ro // 2, ro), into=half["out"])}
    pa = _mm(h, w_in_full, "nn", F32, "proj_at", b_off=o4, n=oa - o4, comm=cm)
    half["out"] = cm.result(hs["out"])
    cm = _Comm()
    hs = {k: _ag_d2d(cm, half[k], waxis[k]) for k in ("bhg", "bat")}
    hs["ff2"] = _ag_ici(cm, wblk["ff2"], waxis["ff2"], rows=(0, r2 // 4))
    pg = _mm(h, w_in_full, "nn", F32, "proj_gate", b_off=oa, n=INW - oa, comm=cm)
    wf["bhg"], wf["bat"], half["ff2"] = (cm.result(hs[k]) for k in ("bhg", "bat", "ff2"))

    cm = _Comm()
    hs = {"out": _ag_d2d(cm, half["out"], waxis["out"]), "ff1": _ag_ici(cm, wblk["ff1"], waxis["ff1"], rows=(0, r1 // 2))}
    o_hg, s_all = _hgrn_fwd(p4, hg_lb_logits, hg_out_norm_g, H, comm=cm)
    wf["out"], half["ff1"] = cm.result(hs["out"]), cm.result(hs["ff1"])

    bucket = _bucket_ids()
    (bias_flat,) = _whole(lambda tb, bk: (_dot(tb, _onehot(bk), TN, precision=HIGHEST),), [rel_bias_table, bucket],
                          [((AH, AT_BLOCK * 2 * AT_BLOCK), F32)], "bias_fwd")
    bias = bias_flat.reshape(AH, AT_BLOCK, 2 * AT_BLOCK)
    q_t = _heads_first(pa[:, :ATW], AH)
    kp = _heads_first(pa[:, ATW:ATW + KVW], KVH)
    vp = _heads_first(pa[:, ATW + KVW:], KVH)
    sinks3 = attn_sinks.reshape(KVH, G, 1)
    cm = _Comm()
    hs = {"ff1": _ag_ici(cm, wblk["ff1"], waxis["ff1"], rows=(r1 // 2, r1), into=half["ff1"])}
    o_at = _heads_last(_attn_fwd(q_t, kp, vp, q_norm_g, k_norm_g, sinks3, bias, KVH, comm=cm))
    half["ff1"] = cm.result(hs["ff1"])

    bh = _mm(o_hg, wf["bhg"], "nn", F32, "branch_hg")
    ba = _mm(o_at, wf["bat"], "nn", F32, "branch_at")

    def merge_fn(bhv, bav, ghg, gat):
        return jax.nn.sigmoid(ghg) * bhv + jax.nn.sigmoid(gat) * bav

    cm = _Comm()
    hs = {"ff1": _ag_d2d(cm, half["ff1"], waxis["ff1"])}
    (merged,) = _rowwise(lambda *a: ((merge_fn(*a),), ()), [(bh, D, 0), (ba, D, 0), (pg, D, 0), (pg, D, 1)], [],
                         [(D, BF16)], [], "merge", comm=cm)
    wf["ff1"] = cm.result(hs["ff1"])
    cm = _Comm()
    hs = {"ff2": _ag_ici(cm, wblk["ff2"], waxis["ff2"], rows=(r2 // 4, 3 * r2 // 8), into=half["ff2"])}
    mo = _mm(merged, wf["out"], "nn", F32, "out_proj", comm=cm)
    half["ff2"] = cm.result(hs["ff2"])

    def resid1(xv, mov, g1, g2n, sh, sc):
        x1v = xv + g1 * mov
        return (x1v, _modnorm(x1v, g2n, sh, sc)), ()

    cm = _Comm()
    hs = {"ff2": _ag_ici(cm, wblk["ff2"], waxis["ff2"], rows=(3 * r2 // 8, r2 // 2), into=half["ff2"])}
    x1, h2 = _rowwise(resid1, [(x2, D, 0), (mo, D, 0)], [gate1, norm2_g, shift2, scale2], [(D, F32), (D, BF16)], [], "resid1",
                      comm=cm)
    half["ff2"] = cm.result(hs["ff2"])
    cm = _Comm()
    hs = {"ff2": _ag_ici(cm, wblk["ff2"], waxis["ff2"], rows=(r2 // 2, r2), into=half["ff2"])}
    u, act = _mm(h2, wf["ff1"], "nn", (F32, BF16), "ff1", comm=cm, epi=lambda r: (r, jnp.square(jnp.maximum(r, 0.0))))
    half["ff2"] = cm.result(hs["ff2"])
    cm = _Comm()
    hs = {"ff2": _ag_d2d(cm, half["ff2"], waxis["ff2"])}
    _call(lambda: None, [], name="ag_d2d_ff2", out_shape=(), comm=cm)
    wf["ff2"] = cm.result(hs["ff2"])
    ff = _mm(act, wf["ff2"], "nn", F32, "ff2")

    def loss_fn(x1v, ffv, tv, g2):
        e = x1v + g2 * ffv - tv
        dy = e * (1.0 / D)
        return (dy, dy * g2), (jnp.sum(e * e, axis=0, keepdims=True), jnp.sum(dy * ffv, axis=0, keepdims=True))

    dy, d_ff, sq_sum, d_gate2 = _rowwise(loss_fn, [(x1, D, 0), (ff, D, 0), (tgt, D, 0)], [gate2],
                                         [(D, F32), (D, BF16)], [(1, D), (1, D)], "loss")
    loss = lax.psum(jnp.sum(sq_sum) * (0.5 / D), ("x", "y", "c"))

    owner_base = jnp.stack([me ^ r for r in CHIP_RELS]).astype(jnp.int32)
    gw, recv1, part, recv2 = {}, {}, {}, {}
    gw["ff2"] = _mm(act, d_ff, "tn", BF16, "dw_ff2")
    cm = _Comm()
    hh = _rs_d2d(cm, gw["ff2"], waxis["ff2"])
    d_u = _mm(d_ff, wf["ff2"], "nt", BF16, "d_act", comm=cm, extras=[u], epi=lambda r, uv: (r * (2.0 * jnp.maximum(uv, 0.0)),))
    part["ff2"] = _rs_add(gw["ff2"], cm.result(hh), waxis["ff2"], owner_base, "rs_add_ff2")
    rows_ff2 = part["ff2"].shape[1]
    cm = _Comm()
    hh = _rs_ici(cm, part["ff2"], rows=(0, rows_ff2 // 2))
    gw["ff1"] = _mm(h2, d_u, "tn", BF16, "dw_ff1", comm=cm)
    cm2 = _Comm()
    hh2 = _rs_ici(cm2, part["ff2"], rows=(rows_ff2 // 2, rows_ff2), recv=cm.result(hh))
    hh1 = _rs_d2d(cm2, gw["ff1"], waxis["ff1"])
    d_h2 = _mm(d_u, wf["ff1"], "nt", F32, "d_h2", comm=cm2)
    recv2["ff2"] = cm2.result(hh2)
    part["ff1"] = _rs_add(gw["ff1"], cm2.result(hh1), waxis["ff1"], owner_base, "rs_add_ff1")

    def norm2_bwd(dh2v, x1v, dyv, mov, g2n, sh, sc, g1):
        _, vjp = jax.vjp(_modnorm, x1v, g2n, sh, sc)
        dx, dg, dsh, dsc = vjp(dh2v)
        dx1 = dyv + dx
        return (dx1, dx1 * g1), (dg, dsh, dsc, jnp.sum(dx1 * mov, axis=0, keepdims=True))

    d_x1, d_mo, d_g2n, d_shift2, d_scale2, d_gate1 = _rowwise(
        norm2_bwd, [(d_h2, D, 0), (x1, D, 0), (dy, D, 0), (mo, D, 0)], [norm2_g, shift2, scale2, gate1],
        [(D, F32), (D, BF16)], [(1, D)] * 4, "norm2_bwd")
    gw["out"] = _mm(merged, d_mo, "tn", BF16, "dw_out")
    cm = _Comm()
    hh = _rs_d2d(cm, gw["out"], waxis["out"])
    d_merged = _mm(d_mo, wf["out"], "nt", F32, "d_merged", comm=cm)
    part["out"] = _rs_add(gw["out"], cm.result(hh), waxis["out"], owner_base, "rs_add_out")

    def merge_bwd(dmv, bhv, bav, ghg, gat):
        _, vjp = jax.vjp(merge_fn, bhv, bav, ghg, gat)
        return vjp(dmv), ()

    d_bh, d_ba, d_ghg, d_gat = _rowwise(merge_bwd, [(d_merged, D, 0), (bh, D, 0), (ba, D, 0), (pg, D, 0), (pg, D, 1)], [],
                                        [(D, BF16)] * 4, [], "merge_bwd")
    gw["bhg"] = _mm(o_hg, d_bh, "tn", BF16, "dw_bhg")
    gw["bat"] = _mm(o_at, d_ba, "tn", BF16, "dw_bat")
    cm = _Comm()
    hh = {k: _rs_d2d(cm, gw[k], waxis[k]) for k in ("bhg", "bat")}
    d_ohg = _mm(d_bh, wf["bhg"], "nt", F32, "d_ohg", comm=cm)
    for k in ("bhg", "bat"):
        part[k] = _rs_add(gw[k], cm.result(hh[k]), waxis[k], owner_base, "rs_add_" + k)
    d_oat = _mm(d_ba, wf["bat"], "nt", BF16, "d_oat")

    cm = _Comm()
    hh = {"ff1": _rs_ici(cm, part["ff1"])}
    d_hq, d_hf, d_hi, d_hg, d_lb, d_gout_h = _hgrn_bwd(p4, hg_lb_logits, hg_out_norm_g, s_all, d_ohg, H, comm=cm)
    recv2["ff1"] = cm.result(hh["ff1"])
    cm = _Comm()
    hh = {k: _rs_ici(cm, part[k]) for k in small}
    dq_t, dkp, dvp, d_qg, d_kg, d_sk, d_bias = _attn_bwd(q_t, kp, vp, q_norm_g, k_norm_g, sinks3, bias,
                                                         _heads_first(d_oat, AH), KVH, comm=cm)
    for k in hh:
        recv2[k] = cm.result(hh[k])
    d_aq = _heads_last(dq_t)
    d_ak = _heads_last(dkp).astype(BF16)
    d_av = _heads_last(dvp).astype(BF16)
    d_proj = jnp.concatenate([d_hq, d_hf, d_hi, d_hg, d_aq, d_ak, d_av, d_ghg, d_gat], axis=1)
    gw_in = _mm(h, d_proj, "tn", BF16, "dw_in")

    wm = LANES * A
    cm = _Comm()
    hi_ = cm.inp(gw_in)
    h_main, h_mid = cm.out((4, D, wm), BF16), cm.out((4, D, LANES), BF16)
    for i, r in enumerate(CHIP_RELS):
        def main_view(ref, p, r=r):
            o = p["me"] ^ r ^ 1
            return ref.at[:, pl.ds(pl.multiple_of((PAIR * (o // 2) + (A + 1) * (1 - p["c"])) * LANES, LANES), wm)]

        def mid_view(ref, p, r=r):
            o = p["me"] ^ r
            return ref.at[:, pl.ds(pl.multiple_of((PAIR * (o // 2) + A) * LANES, LANES), LANES)]

        cm.copy(hi_, main_view, h_main, _slot_view(i), 1)
        cm.copy(hi_, mid_view, h_mid, _slot_view(i), 1)
    _call(lambda: None, [], name="rs_d2d_in", out_shape=(), comm=cm)
    chip = jnp.stack([(me ^ r) // 2 for r in CHIP_RELS]).astype(jnp.int32)
    part_main = _rs_add(gw_in, cm.result(h_main), 1, PAIR * chip + (A + 1) * cc, "rs_add_in_main", tw=LANES)
    part_mid = _rs_add(gw_in, cm.result(h_mid), 1, PAIR * chip + A, "rs_add_in_mid", tw=LANES)
    rs_in = _rs_split_start([part_main, part_mid], "rs_in_start")
    d_h = _mm(d_proj, w_in_full, "nt", F32, "d_h", tn=D, after=[rs_in["token"]])

    def norm1_bwd(dhv, xv, dx1v, g1n, sh, sc):
        _, vjp = jax.vjp(_modnorm, xv, g1n, sh, sc)
        dx, dg, dsh, dsc = vjp(dhv)
        return (dx1v + dx,), (dg, dsh, dsc)

    grad_x, d_g1n, d_shift1, d_scale1 = _rowwise(norm1_bwd, [(d_h, D, 0), (x2, D, 0), (d_x1, D, 0)],
                                                 [norm1_g, shift1, scale1], [(D, F32)], [(1, D)] * 3, "norm1_bwd")

    def sum4(p0, p1, p2, p3):
        return ((p0.astype(F32) + p1.astype(F32)) + p2.astype(F32)) + p3.astype(F32)

    def update_fn(w, m, v, p0, p1, p2, p3):
        g = sum4(p0, p1, p2, p3)
        delta, mn, vn = _adamw(w, g, m, v)
        return (g, delta, mn, vn), ()

    wmv = dict(zip(wnames, ((w_branch_hg, m_w_branch_hg, v_w_branch_hg), (w_branch_attn, m_w_branch_attn, v_w_branch_attn),
                            (w_out, m_w_out, v_w_out), (w_ff1, m_w_ff1, v_w_ff1), (w_ff2, m_w_ff2, v_w_ff2))))
    res = {}

    def update(k, p, rx):
        w, m, v = (t[0] for t in wmv[k])
        n = w.shape[1]
        ins = [(t, n, 0) for t in (w, m, v)] + [(p, n, 0, 0)] + [(rx, n, 0, i) for i in range(3)]
        res[k] = [t[None] for t in _rowwise(update_fn, ins, [], [(n, F32)] * 4, [], "update_" + k)]

    for k in wnames:
        update(k, part[k], recv2[k])
    (part_main, part_mid), (rx_main, rx_mid) = _rs_split_wait(rs_in, [grad_x] + [res[k][0] for k in wnames], "rs_in_wait")
    g_main, = _rowwise(lambda *p: ((sum4(*p),), ()), [(part_main, wm, 0, 0)] + [(rx_main, wm, 0, i) for i in range(3)], [],
                       [(wm, F32)], [], "sum_in_main")
    g_mid, = _rowwise(lambda *p: ((sum4(*p),), ()), [(part_mid, LANES, 0, 0)] + [(rx_mid, LANES, 0, i) for i in range(3)], [],
                      [(LANES, F32)], [], "sum_in_mid")
    g_in = jnp.where(cc == 0, jnp.concatenate([g_main, g_mid[:, :LANES // 2]], axis=1),
                     jnp.concatenate([g_mid[:, LANES // 2:], g_main], axis=1))

    def update_given(w, m, v, g):
        delta, mn, vn = _adamw(w, g, m, v)
        return (g, delta, mn, vn), ()

    res["in"] = [t[None] for t in _rowwise(update_given, [(t, BW, 0) for t in (w_in[0], m_w_in[0], v_w_in[0], g_in)], [],
                                           [(BW, F32)] * 4, [], "update_in")]

    d_sinks = d_sk.reshape(1, AH)
    (d_table_t,) = _whole(lambda db, bk: (_dot(db, _onehot(bk), NT, precision=HIGHEST),),
                          [d_bias.reshape(AH, AT_BLOCK * 2 * AT_BLOCK), bucket], [((AH, N_BUCKETS), F32)], "bias_bwd")
    smalls = [d_g1n, d_g2n, d_lb, d_gout_h, d_qg, d_kg, d_sinks, d_table_t.T.reshape(1, N_BUCKETS * AH)]
    widths = [s.shape[1] for s in smalls]
    lanes = [-(-w // LANES) * LANES for w in widths]
    smalls = [jnp.pad(s, ((0, 0), (0, p - w))) for s, w, p in zip(smalls, widths, lanes)]
    tail_row = jnp.concatenate([d_shift1, d_scale1, d_gate1, d_shift2, d_scale2, d_gate2] + smalls, axis=1)
    (tail_row,) = _behind([tail_row], [g_mid])
    tail_all = _gather_small(tail_row, me, "gather_tail")[:, 0, :]
    d_ada_all, packed = tail_all[:, :6 * D], tail_all[:, 6 * D:]
    d_ada_cols = lax.dynamic_slice(d_ada_all, (0, me * ADA_N), (N_DEV, ADA_N))

    def ada_update(cv, dav, w, m, v):
        g = _bdot(_silu(cv), dav, TN)
        delta, mn, vn = _adamw(w, g, m, v)
        return (g, delta, mn, vn), ()

    res["ada"] = [t[None] for t in _ada_update_call(ada_update, c_all, d_ada_cols, w_ada[0], m_w_ada[0], v_w_ada[0], _tile(D, 256, 16))]

    offs = [sum(lanes[:i]) for i in range(len(lanes))]

    def small_update(pk, dada, lg, *wmv_flat):
        tot = pk[0:1]
        for d in range(1, N_DEV):
            tot = tot + pk[d:d + 1]
        gb = dada[0:1]
        for d in range(1, N_DEV):
            gb = gb + dada[d:d + 1]
        gs = [tot[:, offs[i]:offs[i] + widths[i]] for i in range(len(widths))]
        _, lb_vjp = jax.vjp(_softmax0, lg)
        (g_lg,) = lb_vjp(gs[2])
        grads = [gb, gs[0], gs[1], g_lg, gs[3], gs[4], gs[5], gs[6], gs[7]]
        outs = []
        for i, g in enumerate(grads):
            w, m, v = wmv_flat[3 * i:3 * i + 3]
            delta, mn, vn = _adamw(w, g, m, v)
            outs += [g, delta, mn, vn]
        return tuple(outs)

    tbl = lambda t: t.reshape(1, N_BUCKETS * AH)
    small_wmv = [(b_ada, m_b_ada, v_b_ada), (norm1_g, m_norm1_g, v_norm1_g), (norm2_g, m_norm2_g, v_norm2_g),
                 (hg_lb_logits, m_hg_lb_logits, v_hg_lb_logits), (hg_out_norm_g, m_hg_out_norm_g, v_hg_out_norm_g),
                 (q_norm_g, m_q_norm_g, v_q_norm_g), (k_norm_g, m_k_norm_g, v_k_norm_g),
                 (attn_sinks, m_attn_sinks, v_attn_sinks),
                 (tbl(rel_bias_table), tbl(m_rel_bias_table), tbl(v_rel_bias_table))]
    flat = [t for trip in small_wmv for t in trip]
    out_shapes = [(trip[0].shape, F32) for trip in small_wmv for _ in range(4)]
    sres = _whole(small_update, [packed, d_ada_all, hg_lb_logits] + flat, out_shapes, "small_update")
    names_small = ("b_ada", "norm1_g", "norm2_g", "lb", "gout", "qg", "kg", "sinks", "table")
    for i, k in enumerate(names_small):
        r = sres[4 * i:4 * i + 4]
        if k == "table":
            r = [t.reshape(N_BUCKETS, AH) for t in r]
        res[k] = r

    order = ("ada", "b_ada", "norm1_g", "norm2_g", "in", "lb", "gout", "qg", "kg", "sinks", "table", "bhg", "bat", "out", "ff1", "ff2")
    outs = [loss, grad_x[None]]
    for j in range(4):
        outs += [res[k][j] for k in order]
    return tuple(outs)
```

```python
import functools
import math

import jax
import jax.numpy as jnp
from jax import lax
from jax.experimental import pallas as pl
from jax.experimental.pallas import tpu as pltpu

F32 = jnp.float32
BF16 = jnp.bfloat16
EPS = 1e-6
NEG_INF = -1e30
HG_DK = 128
HG_CHUNK = 64
AT_BLOCK = 128
N_BUCKETS = 32
MAX_EXACT = 16
MAX_DISTANCE = 128
N_DEV = 8
LANES = 128
VMEM_LIMIT = 56 * 1024 * 1024
ADAM_LR, ADAM_B1, ADAM_B2, ADAM_EPS, ADAM_WD, ADAM_STEP = 0.001, 0.9, 0.999, 1e-08, 0.01, 10
HIGHEST = lax.Precision.HIGHEST
MESH = pl.DeviceIdType.MESH
ANY = pl.BlockSpec(memory_space=pl.ANY)
CHIP_RELS = (0, 4, 2, 6)

NN = (((1,), (0,)), ((), ()))
NT = (((1,), (1,)), ((), ()))
TN = (((0,), (0,)), ((), ()))


def _tile(n, pref, unit):
    if n <= pref:
        return n
    t = (pref // unit) * unit
    while t >= unit:
        if n % t == 0:
            return t
        t -= unit
    return n


def _dot(a, b, dn, precision=None):
    return lax.dot_general(a, b, dn, preferred_element_type=F32, precision=precision)


def _bdot(a, b, dn):
    return _dot(a.astype(BF16), b.astype(BF16), dn)


def _position():
    x, y, c = lax.axis_index("x"), lax.axis_index("y"), lax.axis_index("c")
    return dict(x=x, y=y, c=c, me=4 * x + 2 * y + c)


def _peer_position(p, rel):
    x = 1 - p["x"] if rel & 4 else p["x"]
    y = 1 - p["y"] if rel & 2 else p["y"]
    c = 1 - p["c"] if rel & 1 else p["c"]
    return dict(x=x, y=y, c=c, me=4 * x + 2 * y + c)


class _Comm:
    def __init__(self):
        self.ins, self.outs, self.alias, self.plans, self.res = [], [], {}, [], None

    def inp(self, arr):
        self.ins.append(arr)
        return ("i", len(self.ins) - 1)

    def out(self, shape, dtype, alias=None):
        self.outs.append(jax.ShapeDtypeStruct(tuple(shape), dtype))
        if alias is not None:
            self.alias[alias[1]] = len(self.outs) - 1
        return ("o", len(self.outs) - 1)

    def copy(self, src, src_view, dst, dst_view, rel):
        self.plans.append((src, src_view, dst, dst_view, rel))

    def result(self, handle):
        return self.res[handle[1]]

    def build(self, in_refs, out_refs, send_sems, recv_sems):
        pos = _position()
        ref = lambda h: in_refs[h[1]] if h[0] == "i" else out_refs[h[1]]
        ops = []
        for k, (src, sv, dst, dv, rel) in enumerate(self.plans):
            s = sv(ref(src), pos)
            if rel == 0:
                cp = pltpu.make_async_copy(s, dv(ref(dst), pos), send_sems.at[k])
                ops.append((cp.start, cp.wait))
                continue
            peer = _peer_position(pos, rel)
            mk = lambda d: pltpu.make_async_remote_copy(
                src_ref=s, dst_ref=d, send_sem=send_sems.at[k], recv_sem=recv_sems.at[k],
                device_id=(peer["x"], peer["y"], peer["c"]), device_id_type=MESH)
            out_cp, in_cp = mk(dv(ref(dst), pos)), mk(dv(ref(dst), peer))

            def wait(out_cp=out_cp, in_cp=in_cp):
                out_cp.wait_send()
                in_cp.wait_recv()

            ops.append((out_cp.start, wait))
        return ops


def _call(body, args, *, name, out_shape, in_specs=None, out_specs=None, grid=None, scratch_shapes=(), comm=None,
          prefetch=None, aliases=None, after=()):
    single = not isinstance(out_shape, (tuple, list))
    out_shape = (out_shape,) if single else tuple(out_shape)
    n_in, n_out, n_scr = len(args), len(out_shape), len(scratch_shapes)
    vm = pl.BlockSpec(memory_space=pltpu.VMEM)
    in_specs = [vm] * n_in if in_specs is None else list(in_specs)
    out_specs = [vm] * n_out if out_specs is None else (list(out_specs) if isinstance(out_specs, (tuple, list)) else [out_specs])
    n_pf = 0 if prefetch is None else len(prefetch)
    kw = {} if aliases is None else {"input_output_aliases": dict(aliases)}
    if comm is None and after:
        n_dep = len(after)

        def fn(*refs):
            body(*refs[:n_pf + n_in], *refs[n_pf + n_in + n_dep:])

        all_args, all_scratch = list(args) + list(after), list(scratch_shapes)
        in_specs = in_specs + [ANY] * n_dep
    elif comm is None:
        fn = body
        all_args, all_scratch = list(args), list(scratch_shapes)
    else:
        n_ci, n_co, n_x = len(comm.ins), len(comm.outs), len(comm.plans)

        def fn(*refs):
            pf, refs = refs[:n_pf], refs[n_pf:]
            o_in, c_in = refs[:n_in], refs[n_in:n_in + n_ci]
            o_out = refs[n_in + n_ci:n_in + n_ci + n_out]
            c_out = refs[n_in + n_ci + n_out:n_in + n_ci + n_out + n_co]
            scr = refs[n_in + n_ci + n_out + n_co:]
            ops = comm.build(c_in, c_out, scr[n_scr], scr[n_scr + 1])
            if grid:
                first = functools.reduce(jnp.logical_and, [pl.program_id(i) == 0 for i in range(len(grid))])
                last = functools.reduce(jnp.logical_and, [pl.program_id(i) == g - 1 for i, g in enumerate(grid)])

                @pl.when(first)
                def _():
                    for start, _w in ops:
                        start()
            else:
                for start, _w in ops:
                    start()
            body(*pf, *o_in, *o_out, *scr[:n_scr])
            if grid:
                @pl.when(last)
                def _():
                    for _s, wait in ops:
                        wait()
            else:
                for _s, wait in ops:
                    wait()

        all_args = list(args) + list(comm.ins)
        in_specs = in_specs + [ANY] * n_ci
        out_shape = out_shape + tuple(comm.outs)
        out_specs = out_specs + [ANY] * n_co
        all_scratch = list(scratch_shapes) + [pltpu.SemaphoreType.DMA((n_x,)), pltpu.SemaphoreType.DMA((n_x,))]
        kw["input_output_aliases"] = {n_pf + n_in + i: n_out + o for i, o in comm.alias.items()}
    sem = None if grid is None else ("arbitrary",) * len(grid)
    params = pltpu.CompilerParams(dimension_semantics=sem, vmem_limit_bytes=VMEM_LIMIT)
    if prefetch is None:
        spec = dict(in_specs=in_specs, out_specs=tuple(out_specs), scratch_shapes=all_scratch)
        if grid is not None:
            spec["grid"] = grid
    else:
        spec = dict(grid_spec=pltpu.PrefetchScalarGridSpec(
            num_scalar_prefetch=n_pf, grid=grid, in_specs=in_specs, out_specs=tuple(out_specs), scratch_shapes=all_scratch))
        all_args = list(prefetch) + all_args
    res = pl.pallas_call(fn, name=name, out_shape=out_shape, compiler_params=params, **spec, **kw)(*all_args)
    res = list(res)
    if comm is not None:
        comm.res = res[n_out:]
        res = res[:n_out]
    return res[0] if single else res


def _whole_view(ref, pos):
    return ref


def _block_view(axis, n, index, rows=None):
    def view(ref, pos):
        off = pl.multiple_of(index(pos) * n, n)
        if rows is None:
            return ref.at[:, pl.ds(off, n)] if axis == 1 else ref.at[pl.ds(off, n), :]
        lo, cnt = rows[0], rows[1] - rows[0]
        if axis == 1:
            return ref.at[pl.ds(lo, cnt), pl.ds(off, n)]
        return ref.at[pl.ds(pl.multiple_of(off + lo, 16), cnt), :]
    return view


def _rows_view(rows):
    def view(ref, pos):
        return ref if rows is None else ref.at[pl.ds(rows[0], rows[1] - rows[0]), :]
    return view


def _slot_view(i, rows=None):
    def view(ref, pos):
        return ref.at[i] if rows is None else ref.at[i, pl.ds(rows[0], rows[1] - rows[0]), :]
    return view


def _exchange(items, name):
    cm = _Comm()
    for a, rel in items:
        cm.copy(cm.inp(a), _whole_view, cm.out(a.shape, a.dtype), _whole_view, rel)
    _call(lambda: None, [], name=name, out_shape=(), comm=cm)
    return cm.res


def _gather_small(v, me, name):
    cm = _Comm()
    hi, ho = cm.inp(v), cm.out((N_DEV,) + v.shape, v.dtype)
    for rel in range(N_DEV):
        cm.copy(hi, _whole_view, ho, lambda ref, p: ref.at[p["me"]], rel)
    _call(lambda: None, [], name=name, out_shape=(), comm=cm)
    return cm.result(ho)


def _ag_ici(cm, blk, axis, rows=None, into=None):
    n = blk.shape[axis]
    shape = list(blk.shape)
    shape[axis] = n * N_DEV
    hi = cm.inp(blk)
    ho = cm.out(shape, blk.dtype) if into is None else cm.out(shape, blk.dtype, alias=cm.inp(into))
    own = _block_view(axis, n, lambda p: p["me"], rows)
    for rel in CHIP_RELS:
        cm.copy(hi, _rows_view(rows), ho, own, rel)
    return ho


def _ag_d2d(cm, full, axis):
    n = full.shape[axis] // N_DEV
    hi = cm.inp(full)
    ho = cm.out(full.shape, full.dtype, alias=hi)
    for r in CHIP_RELS:
        v = _block_view(axis, n, functools.partial(lambda p, r: p["me"] ^ r, r=r))
        cm.copy(hi, v, ho, v, 1)
    return ho


def _rs_d2d(cm, gw, axis):
    n = gw.shape[axis] // N_DEV
    shape = list(gw.shape)
    shape[axis] = n
    hi, ho = cm.inp(gw), cm.out([4] + shape, gw.dtype)
    for i, r in enumerate(CHIP_RELS):
        cm.copy(hi, _block_view(axis, n, functools.partial(lambda p, r: p["me"] ^ r ^ 1, r=r)), ho, _slot_view(i), 1)
    return ho


def _rs_ici(cm, part, rows=None, recv=None):
    if recv is None:
        ho = cm.out((3,) + part.shape[1:], part.dtype)
    else:
        ho = cm.out(recv.shape, recv.dtype, alias=cm.inp(recv))
    hi = cm.inp(part)
    for i in (1, 2, 3):
        cm.copy(hi, _slot_view(i, rows), ho, _slot_view(i - 1, rows), CHIP_RELS[i])
    return ho


def _rs_add(gw, recv, axis, base, name, tw=None):
    _, R, n = recv.shape
    fan = 1
    if axis == 1:
        tw = n if tw is None else tw
        fan = max(f for f in (4, 3, 2, 1) if (n // tw) % f == 0)
        gw_specs = [pl.BlockSpec((R, tw), functools.partial(lambda i, t, b, k: (0, b[i] + fan * t + k), k=k)) for k in range(fan)]
        rv_spec = pl.BlockSpec((None, R, tw * fan), lambda i, t, b: (i, 0, t))
        grid = (4, n // (tw * fan))
    else:
        tw = _tile(n, 1024, LANES)
        gw_specs = [pl.BlockSpec((R, tw), lambda i, t, b: (b[i], t))]
        rv_spec = pl.BlockSpec((None, R, tw), lambda i, t, b: (i, 0, t))
        grid = (4, n // tw)

    def body(b_ref, *refs):
        g_refs, r_ref, o_ref = refs[:fan], refs[fan], refs[fan + 1]
        g = g_refs[0][...] if fan == 1 else jnp.concatenate([g[...] for g in g_refs], axis=1)
        o_ref[...] = (g.astype(F32) + r_ref[...].astype(F32)).astype(o_ref.dtype)

    return _call(body, [gw] * fan + [recv], name=name, out_shape=jax.ShapeDtypeStruct(recv.shape, recv.dtype), grid=grid,
                 in_specs=gw_specs + [rv_spec], out_specs=rv_spec, prefetch=[base])


HBM_SPEC = pl.BlockSpec(memory_space=pltpu.HBM)
SEM_SPEC = pl.BlockSpec(memory_space=pltpu.SEMAPHORE)
SPLIT_PARAMS = pltpu.CompilerParams(has_side_effects=pltpu.SideEffectType.DATAFLOW_SIDE_EFFECTING)


def _split_copies(refs, plans, send_sems, recv_sems):
    pos = _position()
    out = []
    for k, (si, sv, li, lv, rel) in enumerate(plans):
        peer = _peer_position(pos, rel)
        mk = lambda d: pltpu.make_async_remote_copy(
            src_ref=sv(refs[si], pos), dst_ref=d, send_sem=send_sems.at[k], recv_sem=recv_sems.at[k],
            device_id=(peer["x"], peer["y"], peer["c"]), device_id_type=MESH)
        out.append((mk(lv(refs[li], pos)), mk(lv(refs[li], peer))))
    return out


def _split_start(arrays, plans, name):
    n = len(arrays)

    def body(*refs):
        send_sems, recv_sems = refs[n], refs[n + 1]
        for out_cp, _ in _split_copies(refs[:n], plans, send_sems, recv_sems):
            out_cp.start()
        refs[-1][...] = jnp.zeros_like(refs[-1])

    sems = pltpu.SemaphoreType.DMA((len(plans),))
    res = pl.pallas_call(
        body, name=name,
        out_shape=(sems, sems) + tuple(pltpu.HBM(a.shape, a.dtype) for a in arrays) + (jax.ShapeDtypeStruct((8, LANES), F32),),
        in_specs=[HBM_SPEC] * n, out_specs=(SEM_SPEC, SEM_SPEC) + (HBM_SPEC,) * n + (pl.BlockSpec(memory_space=pltpu.VMEM),),
        input_output_aliases={i: 2 + i for i in range(n)}, compiler_params=SPLIT_PARAMS,
    )(*[pltpu.with_memory_space_constraint(a, pltpu.HBM) for a in arrays])
    return res[0], res[1], list(res[2:2 + n]), res[-1]


def _split_wait(send_sems, recv_sems, arrays, plans, after, name):
    n, na = len(arrays), len(after)

    def body(*refs):
        for out_cp, in_cp in _split_copies(refs[:n], plans, refs[n], refs[n + 1]):
            out_cp.wait_send()
            in_cp.wait_recv()

    res = pl.pallas_call(
        body, name=name, out_shape=tuple(pltpu.HBM(a.shape, a.dtype) for a in arrays),
        in_specs=[HBM_SPEC] * n + [SEM_SPEC, SEM_SPEC] + [ANY] * na, out_specs=(HBM_SPEC,) * n,
        input_output_aliases={i: i for i in range(n)}, compiler_params=SPLIT_PARAMS,
    )(*arrays, send_sems, recv_sems, *after)
    return list(res)


def _rs_split_start(parts, name):
    nw = len(parts)
    lands = [lax.empty((3,) + p.shape[1:], p.dtype) for p in parts]
    plans = [(s, _slot_view(i), nw + s, _slot_view(i - 1), CHIP_RELS[i]) for s in range(nw) for i in (1, 2, 3)]
    send_sems, recv_sems, arrays, token = _split_start(list(parts) + lands, plans, name)
    return dict(sems=(send_sems, recv_sems), arrays=arrays, plans=plans, token=token, nw=nw)


def _rs_split_wait(h, after, name):
    arrays = _split_wait(h["sems"][0], h["sems"][1], h["arrays"], h["plans"], after, name)
    return arrays[:h["nw"]], arrays[h["nw"]:]


def _behind(xs, tokens):
    out = lax.optimization_barrier((tuple(xs), tuple(tokens)))
    return list(out[0])


def _ag_w_in(src, a, D, INW):
    wm = LANES * a

    hd = D // 2
    ALL, TOP, BOT = (0, D), (0, hd), (hd, D)

    def main_place(ref, p, rows=ALL):
        off = pl.multiple_of(((2 * a + 1) * (p["me"] // 2) + (a + 1) * p["c"]) * LANES, LANES)
        return ref.at[pl.ds(rows[0], rows[1] - rows[0]), pl.ds(off, wm)]

    def main_src(ref, p):
        return ref.at[:, pl.ds(pl.multiple_of(p["c"] * LANES, LANES), wm)]

    def mid_src(ref, p):
        return ref.at[:, pl.ds(pl.multiple_of((1 - p["c"]) * wm, LANES), LANES)]

    def mid_place(ref, p, rows=ALL):
        return ref.at[p["me"], pl.ds(rows[0], rows[1] - rows[0]), :]

    def body(src_ref, full_ref, mid_ref, send_sems, recv_sems):
        pos = _position()
        sib, xn, yn = (_peer_position(pos, r) for r in (1, 4, 2))
        dg = _peer_position(pos, 6)
        started = []

        def remote(k, s, d, to):
            return pltpu.make_async_remote_copy(src_ref=s, dst_ref=d, send_sem=send_sems.at[k], recv_sem=recv_sems.at[k],
                                                device_id=(to["x"], to["y"], to["c"]), device_id_type=MESH)

        def send(k, owner, rows, to, from_src=False):
            for j, (src_v, place) in enumerate(((main_src, main_place), (mid_src, mid_place))):
                s = src_v(src_ref, pos) if from_src else place(full_ref if j == 0 else mid_ref, owner, rows)
                cp = remote(k + j, s, place(full_ref if j == 0 else mid_ref, owner, rows), to)
                cp.start()
                started.append(cp)

        def landed(k, owner, rows, frm):
            for j, place in enumerate((main_place, mid_place)):
                ref = full_ref if j == 0 else mid_ref
                remote(k + j, place(ref, owner, rows), place(ref, owner, rows), frm).wait_recv()

        local = [pltpu.make_async_copy(main_src(src_ref, pos), main_place(full_ref, pos), send_sems.at[18]),
                 pltpu.make_async_copy(mid_src(src_ref, pos), mid_place(mid_ref, pos), send_sems.at[19])]
        for cp in local:
            cp.start()
        send(0, pos, ALL, sib, from_src=True)
        send(2, pos, ALL, xn, from_src=True)
        send(4, pos, ALL, yn, from_src=True)
        landed(2, xn, ALL, xn)
        send(10, xn, ALL, sib)
        send(6, xn, TOP, yn)
        landed(4, yn, ALL, yn)
        send(12, yn, ALL, sib)
        send(8, yn, BOT, xn)
        landed(6, dg, TOP, yn)
        send(14, dg, TOP, sib)
        landed(8, dg, BOT, xn)
        send(16, dg, BOT, sib)
        sib_of = lambda p: _peer_position(p, 1)
        landed(0, sib, ALL, sib)
        landed(10, sib_of(xn), ALL, sib)
        landed(12, sib_of(yn), ALL, sib)
        landed(14, sib_of(dg), TOP, sib)
        landed(16, sib_of(dg), BOT, sib)
        for cp in started:
            cp.wait_send()
        for cp in local:
            cp.wait()

    return _call(body, [src], name="ag_w_in", in_specs=[ANY], out_specs=[ANY, ANY],
                 out_shape=(jax.ShapeDtypeStruct((D, INW), BF16), jax.ShapeDtypeStruct((N_DEV, D, LANES), BF16)),
                 scratch_shapes=[pltpu.SemaphoreType.DMA((20,)), pltpu.SemaphoreType.DMA((20,))])


def _patch_mid(full, mid, a):
    D = full.shape[0]

    def body(full_ref, e_ref, o_ref, out_ref):
        out_ref[...] = e_ref[...] + o_ref[...]

    return _call(body, [full, mid, mid], name="patch_mid", grid=(N_DEV // 2,),
                 out_shape=jax.ShapeDtypeStruct(full.shape, full.dtype),
                 in_specs=[ANY, pl.BlockSpec((None, D, LANES), lambda j: (2 * j, 0, 0)),
                           pl.BlockSpec((None, D, LANES), lambda j: (2 * j + 1, 0, 0))],
                 out_specs=pl.BlockSpec((D, LANES), lambda j: (0, (2 * a + 1) * j + a)), aliases={0: 0})


MM_RESIDENT = 2048


def _mm(a, b, mode, out_dtype, name, b_off=0, n=None, comm=None, extras=(), epi=None, tn=None, after=()):
    if mode == "nn":
        (M, K), (K2, N) = a.shape, b.shape
    elif mode == "nt":
        (M, K), (N, K2) = a.shape, b.shape
    else:
        (K, M), (K2, N) = a.shape, b.shape
    assert K == K2, (a.shape, b.shape, mode)
    if n is not None:
        N = n
    single = not isinstance(out_dtype, (tuple, list))
    out_dtypes = (out_dtype,) if single else tuple(out_dtype)
    if epi is None:
        epi = lambda r: (r,)
    tk = K if K <= MM_RESIDENT else (MM_RESIDENT if K % MM_RESIDENT == 0 else _tile(K, 512, LANES))
    nk = K // tk
    if M > MM_RESIDENT and mode == "tn" and N <= MM_RESIDENT and not b_off:
        tm, tn = _tile(M, 512, LANES), N
    elif nk > 1:
        tm, tn = _tile(M, 1024, LANES), _tile(N, tn or 1024, LANES)
    else:
        tm = _tile(M, MM_RESIDENT, LANES)
        tn = _tile(math.gcd(N, b_off) if b_off else N, tn or 512, LANES)
    jb = b_off // tn
    dn = {"nn": NN, "nt": NT, "tn": TN}[mode]
    ne, no = len(extras), len(out_dtypes)

    def body(a_ref, b_ref, *rest):
        e_refs, o_refs = rest[:ne], rest[ne:ne + no]

        def finish(r):
            for o_ref, v in zip(o_refs, epi(r, *[e[...] for e in e_refs])):
                o_ref[...] = v.astype(o_ref.dtype)

        if nk == 1:
            finish(_bdot(a_ref[...], b_ref[...], dn))
            return
        acc_ref = rest[ne + no]
        k = pl.program_id(2)

        @pl.when(k == 0)
        def _():
            acc_ref[...] = _bdot(a_ref[...], b_ref[...], dn)

        @pl.when(jnp.logical_and(k > 0, k < nk - 1))
        def _():
            acc_ref[...] += _bdot(a_ref[...], b_ref[...], dn)

        @pl.when(k == nk - 1)
        def _():
            finish(acc_ref[...] + _bdot(a_ref[...], b_ref[...], dn))

    a_spec = pl.BlockSpec((tk, tm), lambda i, j, k: (k, i)) if mode == "tn" else pl.BlockSpec((tm, tk), lambda i, j, k: (i, k))
    b_spec = pl.BlockSpec((tn, tk), lambda i, j, k: (j, k)) if mode == "nt" else pl.BlockSpec((tk, tn), lambda i, j, k: (k, j + jb))
    o_spec = pl.BlockSpec((tm, tn), lambda i, j, k: (i, j))
    res = _call(body, [a, b] + list(extras), name=name, grid=(M // tm, N // tn, nk),
                out_shape=tuple(jax.ShapeDtypeStruct((M, N), dt) for dt in out_dtypes),
                in_specs=[a_spec, b_spec] + [o_spec] * ne, out_specs=[o_spec] * no,
                scratch_shapes=[pltpu.VMEM((tm, tn), F32)] if nk > 1 else [], comm=comm, after=after)
    return res[0] if single else res


def _rowwise(fn, row_ins, bcast_ins, row_outs, acc_outs, name, rt=256, comm=None):
    L = row_ins[0][0].shape[-2]
    rt = _tile(L, rt, 16)
    nr, nb, no = len(row_ins), len(bcast_ins), len(row_outs)

    def body(*refs):
        i = pl.program_id(0)
        vals = [r[...] for r in refs[:nr + nb]]
        outs, accs = fn(*vals)
        for r, v in zip(refs[nr + nb:nr + nb + no], outs):
            r[...] = v.astype(r.dtype)
        acc_refs = refs[nr + nb + no:]

        @pl.when(i == 0)
        def _():
            for r in acc_refs:
                r[...] = jnp.zeros_like(r)

        for r, v in zip(acc_refs, accs):
            r[...] += v

    in_specs = []
    for spec in row_ins:
        w, cb = spec[1], spec[2]
        if len(spec) == 4:
            in_specs.append(pl.BlockSpec((None, rt, w), functools.partial(lambda i, cb, ld: (ld, i, cb), cb=cb, ld=spec[3])))
        else:
            in_specs.append(pl.BlockSpec((rt, w), functools.partial(lambda i, cb: (i, cb), cb=cb)))
    in_specs += [pl.BlockSpec(b.shape, lambda i: (0, 0)) for b in bcast_ins]
    out_specs = [pl.BlockSpec((rt, w), lambda i: (i, 0)) for w, _ in row_outs]
    out_specs += [pl.BlockSpec(s, lambda i: (0, 0)) for s in acc_outs]
    out_shape = [jax.ShapeDtypeStruct((L, w), dt) for w, dt in row_outs] + [jax.ShapeDtypeStruct(s, F32) for s in acc_outs]
    return _call(body, [s[0] for s in row_ins] + list(bcast_ins), name=name, grid=(L // rt,), out_shape=tuple(out_shape),
                 in_specs=in_specs, out_specs=out_specs, comm=comm)


def _whole(fn, ins, out_shapes, name):
    def body(*refs):
        outs = fn(*[r[...] for r in refs[:len(ins)]])
        for r, v in zip(refs[len(ins):], outs):
            r[...] = v.astype(r.dtype)

    return _call(body, list(ins), name=name, out_shape=tuple(jax.ShapeDtypeStruct(s, dt) for s, dt in out_shapes))


def _silu(x):
    return x * jax.nn.sigmoid(x)


def _rms(x, g):
    return (x * lax.rsqrt(jnp.mean(x * x, axis=-1, keepdims=True) + EPS)) * g


def _modnorm(x, g, shift, scale):
    return _rms(x, g) * (1.0 + scale) + shift


def _adamw(w, g, m, v):
    m = ADAM_B1 * m + (1.0 - ADAM_B1) * g
    v = ADAM_B2 * v + (1.0 - ADAM_B2) * jnp.square(g)
    m_hat = m / (1.0 - ADAM_B1 ** ADAM_STEP)
    v_hat = v / (1.0 - ADAM_B2 ** ADAM_STEP)
    delta = -ADAM_LR * (m_hat / (jnp.sqrt(v_hat) + ADAM_EPS) + ADAM_WD * w)
    return delta, m, v


def _lower_bound(lg):
    e = jnp.exp(lg - jnp.max(lg, axis=0, keepdims=True))
    return e[0:1] / jnp.sum(e, axis=0, keepdims=True)


def _hg_stages(hq_l, hf_l, hi_l, lb):
    C = hq_l[0].shape[0]
    row = lax.broadcasted_iota(jnp.int32, (C, C), 0)
    col = lax.broadcasted_iota(jnp.int32, (C, C), 1)
    tri = row >= col
    trif = tri.astype(F32)
    f_l = [lb + (1.0 - lb) * jax.nn.sigmoid(hf) for hf in hf_l]
    b_l = [_dot(trif, jnp.log(f), NN, precision=HIGHEST) for f in f_l]
    q_l = [_silu(hq) for hq in hq_l]
    m_l = [b[C // 2 - 1:C // 2] for b in b_l]
    bl_l = [b[C - 1:C] for b in b_l]
    sc_l = [jnp.where(tri, _bdot(q * jnp.exp(b - m), (1.0 - f) * jnp.exp(m - b), NT), 0.0)
            for q, f, b, m in zip(q_l, f_l, b_l, m_l)]
    o1_l = [_bdot(sc, hi, NN) for sc, hi in zip(sc_l, hi_l)]
    u_l = [_bdot(hi, (1.0 - f) * jnp.exp(bl - b), TN) for hi, f, b, bl in zip(hi_l, f_l, b_l, bl_l)]
    qb_l = [q * jnp.exp(b) for q, b in zip(q_l, b_l)]
    dec_l = [jnp.exp(bl) for bl in bl_l]
    return list(zip(o1_l, u_l, qb_l, dec_l))


def _hg_out(o, hgate, gout):
    return _rms(o, gout) * _silu(hgate)


HG_STAGE = 8
HG_GROUP = 32


def _hgrn_fwd(p4, lb_logits, gout, H, comm=None):
    L = p4.shape[0]
    C = HG_CHUNK
    GR = _tile(L // C, HG_GROUP, 1)
    T = GR * C
    N = L // T

    def body(hq_ref, hf_ref, hi_ref, hg_ref, lg_ref, gout_ref, o_ref, s_ref, st_ref):
        @pl.when(pl.program_id(1) == 0)
        def _():
            st_ref[...] = jnp.zeros_like(st_ref)

        lb = _lower_bound(lg_ref[...])
        st = st_ref[...]
        for c0 in range(0, GR, HG_STAGE):
            rows_l = [pl.ds(ci * C, C) for ci in range(c0, min(c0 + HG_STAGE, GR))]
            parts = _hg_stages([hq_ref[r, :] for r in rows_l], [hf_ref[r, :] for r in rows_l],
                               [hi_ref[r, :] for r in rows_l], lb)
            for ci, rows, (o1, u, qb, dec) in zip(range(c0, GR), rows_l, parts):
                s_ref[0, ci] = st
                o = o1 + _bdot(qb, st, NT)
                st = st * dec + u
                o_ref[rows, :] = _hg_out(o, hg_ref[rows, :], gout_ref[...]).astype(o_ref.dtype)
        st_ref[...] = st

    blk = lambda s: pl.BlockSpec((T, HG_DK), functools.partial(lambda h, n, s: (n, s * H + h), s=s))
    return _call(
        body, [p4, p4, p4, p4, lb_logits, gout], name="hgrn_fwd", grid=(H, N),
        out_shape=(jax.ShapeDtypeStruct((L, H * HG_DK), BF16), jax.ShapeDtypeStruct((H, N * GR, HG_DK, HG_DK), F32)),
        in_specs=[blk(0), blk(1), blk(2), blk(3), pl.BlockSpec((2, HG_DK), lambda h, n: (0, h)),
                  pl.BlockSpec((1, HG_DK), lambda h, n: (0, 0))],
        out_specs=(pl.BlockSpec((T, HG_DK), lambda h, n: (n, h)),
                   pl.BlockSpec((1, GR, HG_DK, HG_DK), lambda h, n: (h, n, 0, 0))),
        scratch_shapes=[pltpu.VMEM((HG_DK, HG_DK), F32)], comm=comm)


def _hgrn_bwd(p4, lb_logits, gout, s_all, d_out, H, comm=None):
    L = p4.shape[0]
    C = HG_CHUNK
    GR = _tile(L // C, HG_GROUP, 1)
    T = GR * C
    N = L // T

    def body(hq_ref, hf_ref, hi_ref, hg_ref, lg_ref, gout_ref, s_ref, do_ref,
             dq_ref, df_ref, di_ref, dg_ref, dlb_ref, dgo_ref, dst_ref):
        @pl.when(pl.program_id(1) == 0)
        def _():
            dst_ref[...] = jnp.zeros_like(dst_ref)
            dlb_ref[...] = jnp.zeros_like(dlb_ref)

        @pl.when(jnp.logical_and(pl.program_id(0) == 0, pl.program_id(1) == 0))
        def _():
            dgo_ref[...] = jnp.zeros_like(dgo_ref)

        lb = _lower_bound(lg_ref[...])
        dst = dst_ref[...]
        d_lb = jnp.zeros((1, HG_DK), F32)
        d_go = jnp.zeros((1, HG_DK), F32)
        for c0 in reversed(range(0, GR, HG_STAGE)):
            dst, d_lb_c, d_go_c = chunks_bwd(list(range(c0, min(c0 + HG_STAGE, GR))), lb, dst, hq_ref, hf_ref, hi_ref,
                                             hg_ref, gout_ref, s_ref, do_ref, dq_ref, df_ref, di_ref, dg_ref)
            d_lb += d_lb_c
            d_go += d_go_c
        dst_ref[...] = dst
        dlb_ref[...] += d_lb
        dgo_ref[...] += d_go

    def chunks_bwd(idx, lb, dst, hq_ref, hf_ref, hi_ref, hg_ref, gout_ref, s_ref, do_ref, dq_ref, df_ref, di_ref, dg_ref):
        n = len(idx)
        rows_l = [pl.ds(ci * C, C) for ci in idx]
        hq_l, hf_l, hi_l = ([r[rows, :] for rows in rows_l] for r in (hq_ref, hf_ref, hi_ref))
        st_l = [s_ref[0, ci] for ci in idx]
        row = lax.broadcasted_iota(jnp.int32, (C, C), 0)
        col = lax.broadcasted_iota(jnp.int32, (C, C), 1)
        tri = row >= col
        trif = tri.astype(F32)
        every = lambda fn, *ls: [fn(*a) for a in zip(*ls)]
        sg_l = every(jax.nn.sigmoid, hf_l)
        f_l = every(lambda sg: lb + (1.0 - lb) * sg, sg_l)
        b_l = every(lambda f: _dot(trif, jnp.log(f), NN, precision=HIGHEST), f_l)
        q_l = every(_silu, hq_l)
        m_l = every(lambda b: b[C // 2 - 1:C // 2], b_l)
        bl_l = every(lambda b: b[C - 1:C], b_l)
        e_qm_l = every(lambda b, m: jnp.exp(b - m), b_l, m_l)
        e_km_l = every(lambda b, m: jnp.exp(m - b), b_l, m_l)
        e_kl_l = every(lambda b, bl: jnp.exp(bl - b), b_l, bl_l)
        e_q_l = every(jnp.exp, b_l)
        dec_l = every(jnp.exp, bl_l)
        qe_l = every(lambda q, e: q * e, q_l, e_qm_l)
        ke_l = every(lambda f, e: (1.0 - f) * e, f_l, e_km_l)
        kd_l = every(lambda f, e: (1.0 - f) * e, f_l, e_kl_l)
        qb_l = every(lambda q, e: q * e, q_l, e_q_l)
        sc_l = every(lambda qe, ke: jnp.where(tri, _bdot(qe, ke, NT), 0.0), qe_l, ke_l)
        o_l = every(lambda sc, hi, qb, st: _bdot(sc, hi, NN) + _bdot(qb, st, NT), sc_l, hi_l, qb_l, st_l)
        vj_l = every(lambda o, rows: jax.vjp(_hg_out, o, hg_ref[rows, :], gout_ref[...])[1](do_ref[rows, :]), o_l, rows_l)
        do_l = [v[0] for v in vj_l]
        dsc_l = every(lambda do, hi: jnp.where(tri, _bdot(do, hi, NT), 0.0), do_l, hi_l)
        dv1_l = every(lambda sc, do: _bdot(sc, do, TN), sc_l, do_l)
        dqe_l = every(lambda dsc, ke: _bdot(dsc, ke, NN), dsc_l, ke_l)
        dke_l = every(lambda dsc, qe: _bdot(dsc, qe, TN), dsc_l, qe_l)
        dqb_l = every(lambda do, st: _bdot(do, st, NN), do_l, st_l)
        own_l = every(lambda do, qb: _bdot(do, qb, TN), do_l, qb_l)
        dst_next_l = [None] * n
        for j in reversed(range(n)):
            dst_next_l[j] = dst
            dst = own_l[j] + dst * dec_l[j]
        dv_l = every(lambda dv1, kd, dn: dv1 + _bdot(kd, dn, NT), dv1_l, kd_l, dst_next_l)
        dkd_l = every(lambda hi, dn: _bdot(hi, dn, NN), hi_l, dst_next_l)
        ddec_l = every(lambda dn, st: jnp.sum(dn * st, axis=0, keepdims=True), dst_next_l, st_l)
        rowi = lax.broadcasted_iota(jnp.int32, (C, HG_DK), 0)
        tq_l = every(lambda a, b_: a * b_, dqe_l, qe_l)
        tk_l = every(lambda a, b_: a * b_, dke_l, ke_l)
        td_l = every(lambda a, b_: a * b_, dkd_l, kd_l)
        tb_l = every(lambda a, b_: a * b_, dqb_l, qb_l)
        db_l = every(lambda tq, tk, td, tb, ddec, dec: tq - tk - td + tb
                     + jnp.where(rowi == C // 2 - 1, jnp.sum(tk - tq, axis=0, keepdims=True), 0.0)
                     + jnp.where(rowi == C - 1, jnp.sum(td, axis=0, keepdims=True) + ddec * dec, 0.0),
                     tq_l, tk_l, td_l, tb_l, ddec_l, dec_l)
        dlf_l = every(lambda db: _dot(trif, db, TN, precision=HIGHEST), db_l)
        dk_l = every(lambda dke, e1, dkd, e2: dke * e1 + dkd * e2, dke_l, e_km_l, dkd_l, e_kl_l)
        df_l = every(lambda dlf, f, dk: dlf / f - dk, dlf_l, f_l, dk_l)
        d_lb = jnp.zeros((1, HG_DK), F32)
        d_go = jnp.zeros((1, HG_DK), F32)
        for j, rows in enumerate(rows_l):
            sg, hq = sg_l[j], hq_l[j]
            df_ref[rows, :] = (df_l[j] * (1.0 - lb) * sg * (1.0 - sg)).astype(df_ref.dtype)
            sq = jax.nn.sigmoid(hq)
            dq = dqe_l[j] * e_qm_l[j] + dqb_l[j] * e_q_l[j]
            dq_ref[rows, :] = (dq * (sq * (1.0 + hq * (1.0 - sq)))).astype(dq_ref.dtype)
            di_ref[rows, :] = dv_l[j].astype(di_ref.dtype)
            dg_ref[rows, :] = vj_l[j][1].astype(dg_ref.dtype)
            d_lb += jnp.sum(df_l[j] * (1.0 - sg), axis=0, keepdims=True)
            d_go += vj_l[j][2]
        return dst, d_lb, d_go

    blk = lambda s: pl.BlockSpec((T, HG_DK), functools.partial(lambda h, n, s: (N - 1 - n, s * H + h), s=s))
    oblk = pl.BlockSpec((T, HG_DK), lambda h, n: (N - 1 - n, h))
    vec = pl.BlockSpec((1, HG_DK), lambda h, n: (0, h))
    W = H * HG_DK
    return _call(
        body, [p4, p4, p4, p4, lb_logits, gout, s_all, d_out], name="hgrn_bwd", grid=(H, N),
        out_shape=tuple([jax.ShapeDtypeStruct((L, W), BF16)] * 4 + [jax.ShapeDtypeStruct((1, W), F32), jax.ShapeDtypeStruct((1, HG_DK), F32)]),
        in_specs=[blk(0), blk(1), blk(2), blk(3), pl.BlockSpec((2, HG_DK), lambda h, n: (0, h)),
                  pl.BlockSpec((1, HG_DK), lambda h, n: (0, 0)),
                  pl.BlockSpec((1, GR, HG_DK, HG_DK), lambda h, n: (h, N - 1 - n, 0, 0)), oblk],
        out_specs=(oblk, oblk, oblk, oblk, vec, pl.BlockSpec((1, HG_DK), lambda h, n: (0, 0))),
        scratch_shapes=[pltpu.VMEM((HG_DK, HG_DK), F32)], comm=comm)


def _bucket_ids():
    i = jnp.arange(AT_BLOCK, dtype=jnp.int32)[:, None]
    j = jnp.arange(2 * AT_BLOCK, dtype=jnp.int32)[None, :]
    n = jnp.maximum(i - j + AT_BLOCK, 0)
    nf = jnp.maximum(n, 1).astype(F32)
    large = MAX_EXACT + (jnp.log(nf / MAX_EXACT) / math.log(MAX_DISTANCE / MAX_EXACT) * (N_BUCKETS - MAX_EXACT)).astype(jnp.int32)
    large = jnp.minimum(large, N_BUCKETS - 1)
    return jnp.where(n < MAX_EXACT, n, large).reshape(1, -1)


def _onehot(bucket):
    ids = lax.broadcasted_iota(jnp.int32, (N_BUCKETS, bucket.shape[1]), 0)
    return (ids == bucket).astype(F32)


def _attn_probs(qn, kpn, kcn, bias_g, sink, first, scale):
    rows = qn.shape[0]
    i = jnp.bitwise_and(lax.broadcasted_iota(jnp.int32, (rows, AT_BLOCK), 0), AT_BLOCK - 1)
    j = lax.broadcasted_iota(jnp.int32, (rows, AT_BLOCK), 1)
    lp = _bdot(qn, kpn, NT) * scale + bias_g[:, :AT_BLOCK]
    lc = _bdot(qn, kcn, NT) * scale + bias_g[:, AT_BLOCK:]
    lp = jnp.where(jnp.logical_and(j > i, jnp.logical_not(first)), lp, NEG_INF)
    lc = jnp.where(j <= i, lc, NEG_INF)
    m = jnp.maximum(jnp.maximum(jnp.max(lp, axis=-1, keepdims=True), jnp.max(lc, axis=-1, keepdims=True)), sink)
    pp, pc, ps = jnp.exp(lp - m), jnp.exp(lc - m), jnp.exp(sink - m)
    den = jnp.sum(pp, axis=-1, keepdims=True) + jnp.sum(pc, axis=-1, keepdims=True) + ps
    return pp / den, pc / den, ps / den


def _sink_rows(sk_ref, G):
    head = lax.broadcasted_iota(jnp.int32, (G * AT_BLOCK, 1), 0) // AT_BLOCK
    sink = jnp.zeros((G * AT_BLOCK, 1), F32)
    for g in range(G):
        sink = jnp.where(head == g, sk_ref[0, g:g + 1, :], sink)
    return sink


def _attn_fwd(q_t, kp, vp, qg, kg, sinks, bias, KVH, comm=None):
    AH, L, DH = q_t.shape
    G = AH // KVH
    NB = L // AT_BLOCK
    scale = DH ** -0.5

    def body(q_ref, kp_ref, kc_ref, vp_ref, vc_ref, qg_ref, kg_ref, sk_ref, b_ref, o_ref):
        first = pl.program_id(1) == 0
        kpn, kcn = _rms(kp_ref[0], kg_ref[...]), _rms(kc_ref[0], kg_ref[...])
        qn = _rms(q_ref[...].reshape(G * AT_BLOCK, DH), qg_ref[...])
        sink = _sink_rows(sk_ref, G)
        pp, pc, _ = _attn_probs(qn, kpn, kcn, b_ref[...].reshape(G * AT_BLOCK, 2 * AT_BLOCK), sink, first, scale)
        o = _bdot(pp, vp_ref[0], NN) + _bdot(pc, vc_ref[0], NN)
        o_ref[...] = o.reshape(G, AT_BLOCK, DH).astype(o_ref.dtype)

    kblk = lambda off: pl.BlockSpec((1, AT_BLOCK, DH),
                                    functools.partial(lambda h, n, off: (h, jnp.maximum(n + off - 1, 0), 0), off=off))
    return _call(
        body, [q_t, kp, kp, vp, vp, qg, kg, sinks, bias], name="attn_fwd", grid=(KVH, NB),
        out_shape=jax.ShapeDtypeStruct((AH, L, DH), BF16),
        in_specs=[pl.BlockSpec((G, AT_BLOCK, DH), lambda h, n: (h, n, 0)), kblk(0), kblk(1), kblk(0), kblk(1),
                  pl.BlockSpec((1, DH), lambda h, n: (0, 0)), pl.BlockSpec((1, DH), lambda h, n: (0, 0)),
                  pl.BlockSpec((1, G, 1), lambda h, n: (h, 0, 0)),
                  pl.BlockSpec((G, AT_BLOCK, 2 * AT_BLOCK), lambda h, n: (h, 0, 0))],
        out_specs=pl.BlockSpec((G, AT_BLOCK, DH), lambda h, n: (h, n, 0)), comm=comm)


def _attn_bwd(q_t, kp, vp, qg, kg, sinks, bias, do_t, KVH, comm=None):
    AH, L, DH = q_t.shape
    G = AH // KVH
    NB = L // AT_BLOCK
    B = AT_BLOCK
    scale = DH ** -0.5

    def body(q_ref, kp_ref, kc_ref, vp_ref, vc_ref, qg_ref, kg_ref, sk_ref, b_ref, do_ref,
             dq_ref, dk_ref, dv_ref, dqg_ref, dkg_ref, dsk_ref, db_ref):
        n = pl.program_id(1)
        first = n == 0

        @pl.when(first)
        def _():
            for r in (dk_ref, dv_ref, dsk_ref, db_ref):
                r[...] = jnp.zeros_like(r)

        @pl.when(jnp.logical_and(first, pl.program_id(0) == 0))
        def _():
            dqg_ref[...] = jnp.zeros_like(dqg_ref)
            dkg_ref[...] = jnp.zeros_like(dkg_ref)

        kp_raw, kc_raw, kgv, qgv = kp_ref[0], kc_ref[0], kg_ref[...], qg_ref[...]
        kpn, kp_vjp = jax.vjp(_rms, kp_raw, kgv)
        kcn, kc_vjp = jax.vjp(_rms, kc_raw, kgv)
        qn, q_vjp = jax.vjp(_rms, q_ref[...].reshape(G * B, DH), qgv)
        pp, pc, ps = _attn_probs(qn, kpn, kcn, b_ref[...].reshape(G * B, 2 * B), _sink_rows(sk_ref, G), first, scale)
        do = do_ref[...].reshape(G * B, DH)
        dvp = _bdot(pp, do, TN)
        dvc = _bdot(pc, do, TN)
        dpp = _bdot(do, vp_ref[0], NT)
        dpc = _bdot(do, vc_ref[0], NT)
        dsum = jnp.sum(dpp * pp, axis=-1, keepdims=True) + jnp.sum(dpc * pc, axis=-1, keepdims=True)
        dlp = pp * (dpp - dsum)
        dlc = pc * (dpc - dsum)
        dsk_ref[0] += jnp.sum((-ps * dsum).reshape(G, B, 1), axis=1)
        db_ref[:, :, :B] += dlp.reshape(G, B, B)
        db_ref[:, :, B:] += dlc.reshape(G, B, B)
        dlp, dlc = dlp * scale, dlc * scale
        dqn = _bdot(dlp, kpn, NN) + _bdot(dlc, kcn, NN)
        dq_raw, dqg = q_vjp(dqn)
        dq_ref[...] = dq_raw.reshape(G, B, DH).astype(dq_ref.dtype)
        dkp_raw, dkg_p = kp_vjp(_bdot(dlp, qn, TN))
        dkc_raw, dkg_c = kc_vjp(_bdot(dlc, qn, TN))
        r0 = pl.multiple_of(jnp.maximum(n - 1, 0) * B, B)
        r1 = pl.multiple_of(n * B, B)
        dk_ref[0, pl.ds(r0, B), :] += dkp_raw
        dk_ref[0, pl.ds(r1, B), :] += dkc_raw
        dv_ref[0, pl.ds(r0, B), :] += dvp
        dv_ref[0, pl.ds(r1, B), :] += dvc
        dqg_ref[...] += dqg
        dkg_ref[...] += dkg_p + dkg_c

    kblk = lambda off: pl.BlockSpec((1, B, DH), functools.partial(lambda h, n, off: (h, jnp.maximum(n + off - 1, 0), 0), off=off))
    qblk = pl.BlockSpec((G, B, DH), lambda h, n: (h, n, 0))
    accblk = pl.BlockSpec((1, L, DH), lambda h, n: (h, 0, 0))
    vecblk = pl.BlockSpec((1, DH), lambda h, n: (0, 0))
    return _call(
        body, [q_t, kp, kp, vp, vp, qg, kg, sinks, bias, do_t], name="attn_bwd", grid=(KVH, NB),
        out_shape=(jax.ShapeDtypeStruct((AH, L, DH), BF16), jax.ShapeDtypeStruct((KVH, L, DH), F32),
                   jax.ShapeDtypeStruct((KVH, L, DH), F32), jax.ShapeDtypeStruct((1, DH), F32),
                   jax.ShapeDtypeStruct((1, DH), F32), jax.ShapeDtypeStruct((KVH, G, 1), F32),
                   jax.ShapeDtypeStruct((AH, B, 2 * B), F32)),
        in_specs=[qblk, kblk(0), kblk(1), kblk(0), kblk(1),
                  pl.BlockSpec((1, DH), lambda h, n: (0, 0)), pl.BlockSpec((1, DH), lambda h, n: (0, 0)),
                  pl.BlockSpec((1, G, 1), lambda h, n: (h, 0, 0)),
                  pl.BlockSpec((G, B, 2 * B), lambda h, n: (h, 0, 0)), qblk],
        out_specs=(qblk, accblk, accblk, vecblk, vecblk, pl.BlockSpec((1, G, 1), lambda h, n: (h, 0, 0)),
                   pl.BlockSpec((G, B, 2 * B), lambda h, n: (h, 0, 0))), comm=comm)


def _heads_first(t, nh):
    L = t.shape[0]
    return jnp.transpose(t.reshape(L, nh, t.shape[1] // nh), (1, 0, 2))


def _heads_last(t):
    nh, L, dh = t.shape
    return jnp.transpose(t, (1, 0, 2)).reshape(L, nh * dh)


def _softmax0(lg):
    e = jnp.exp(lg - jnp.max(lg, axis=0, keepdims=True))
    return e[0:1] / jnp.sum(e, axis=0, keepdims=True)


def _ada_update_call(fn, c_all, d_cols, w, m, v, rt):
    D, n = w.shape

    def body(c_ref, d_ref, w_ref, m_ref, v_ref, g_out, dl_out, m_out, v_out):
        outs, _ = fn(c_ref[...], d_ref[...], w_ref[...], m_ref[...], v_ref[...])
        for r, val in zip((g_out, dl_out, m_out, v_out), outs):
            r[...] = val

    wblk = pl.BlockSpec((rt, n), lambda i: (i, 0))
    return _call(
        body, [c_all, d_cols, w, m, v], name="update_ada", grid=(D // rt,), out_shape=tuple([jax.ShapeDtypeStruct((D, n), F32)] * 4),
        in_specs=[pl.BlockSpec((N_DEV, rt), lambda i: (0, i)), pl.BlockSpec((N_DEV, n), lambda i: (0, 0)), wblk, wblk, wblk],
        out_specs=(wblk, wblk, wblk, wblk))


def kernel(x, c, w_ada, b_ada, norm1_g, norm2_g, w_in, hg_lb_logits, hg_out_norm_g, q_norm_g, k_norm_g, attn_sinks, rel_bias_table, w_branch_hg, w_branch_attn, w_out, w_ff1, w_ff2, loss_target, m_w_ada, m_b_ada, m_norm1_g, m_norm2_g, m_w_in, m_hg_lb_logits, m_hg_out_norm_g, m_q_norm_g, m_k_norm_g, m_attn_sinks, m_rel_bias_table, m_w_branch_hg, m_w_branch_attn, m_w_out, m_w_ff1, m_w_ff2, v_w_ada, v_b_ada, v_norm1_g, v_norm2_g, v_w_in, v_hg_lb_logits, v_hg_out_norm_g, v_q_norm_g, v_k_norm_g, v_attn_sinks, v_rel_bias_table, v_w_branch_hg, v_w_branch_attn, v_w_out, v_w_ff1, v_w_ff2):
    cc = lax.axis_index("c")
    me = 4 * lax.axis_index("x") + 2 * lax.axis_index("y") + cc
    x2 = x[0]
    tgt = loss_target[0]
    L, D = x2.shape
    HGW = hg_lb_logits.shape[1]
    H = HGW // HG_DK
    AH = attn_sinks.shape[1]
    DH = q_norm_g.shape[1]
    ATW = AH * DH
    BW = w_in.shape[2]
    INW = BW * N_DEV
    A = BW // LANES
    assert BW == LANES * A + LANES // 2
    KVW = (INW - 4 * HGW - ATW - 2 * D) // 2
    KVH = KVW // DH
    G = AH // KVH
    ADA_N = w_ada.shape[2]
    PAIR = 2 * A + 1

    c_all = _gather_small(c, me, "gather_c")[:, 0, :]
    b_cols = lax.dynamic_slice(b_ada, (0, me * ADA_N), (1, ADA_N))
    (ada_cols,) = _whole(lambda cv, w, b: (_bdot(_silu(cv), w, NN) + b,), [c_all, w_ada[0], b_cols],
                         [((N_DEV, ADA_N), F32)], "ada_fwd")
    ada_all = _gather_small(ada_cols, me, "gather_ada")
    ada_row = lax.dynamic_slice(ada_all, (0, me, 0), (N_DEV, 1, ADA_N)).reshape(1, 6 * D)

    w_in_b = w_in[0].astype(BF16)
    src_in = jnp.where(cc == 0, jnp.pad(w_in_b, ((0, 0), (0, LANES // 2))), jnp.pad(w_in_b, ((0, 0), (LANES // 2, 0))))
    (src_in,) = _behind([src_in], [ada_row])
    shift1, scale1, gate1, shift2, scale2, gate2 = [ada_row[:, i * D:(i + 1) * D] for i in range(6)]
    w_in_gapped, w_in_mid = _ag_w_in(src_in, A, D, INW)
    w_in_full = _patch_mid(w_in_gapped, w_in_mid, A)

    wnames = ("bhg", "bat", "out", "ff1", "ff2")
    small = ("bhg", "bat", "out")
    waxis = dict(zip(wnames, (1, 1, 0, 1, 0)))
    wsrc = dict(zip(wnames, (w_branch_hg, w_branch_attn, w_out, w_ff1, w_ff2)))
    wblk = {k: wsrc[k][0].astype(BF16) for k in wnames}
    wf = {}

    (h,) = _rowwise(lambda xv, g, sh, sc: ((_modnorm(xv, g, sh, sc),), ()), [(x2, D, 0)], [norm1_g, shift1, scale1],
                    [(D, BF16)], [], "norm1")
    o4, oa = 4 * HGW, 4 * HGW + ATW + 2 * KVW
    r1, r2, ro = wblk["ff1"].shape[0], wblk["ff2"].shape[0], wblk["out"].shape[0]
    cm = _Comm()
    hs = {k: _ag_ici(cm, wblk[k], waxis[k]) for k in ("bhg", "bat")}
    hs["out"] = _ag_ici(cm, wblk["out"], waxis["out"], rows=(0, ro // 2))
    p4 = _mm(h, w_in_full, "nn", F32, "proj_hg", n=o4, comm=cm)
    half = {k: cm.result(hs[k]) for k in hs}
    cm = _Comm()
    hs = {"out": _ag_ici(cm, wblk["out"], waxis["out"], rows=(ro // 2, ro), into=half["out"])}
    pa = _mm(h, w_in_full, "nn", F32, "proj_at", b_off=o4, n=oa - o4, comm=cm)
    half["out"] = cm.result(hs["out"])
    cm = _Comm()
    hs = {k: _ag_d2d(cm, half[k], waxis[k]) for k in ("bhg", "bat")}
    hs["ff2"] = _ag_ici(cm, wblk["ff2"], waxis["ff2"], rows=(0, r2 // 4))
    pg = _mm(h, w_in_full, "nn", F32, "proj_gate", b_off=oa, n=INW - oa, comm=cm)
    wf["bhg"], wf["bat"], half["ff2"] = (cm.result(hs[k]) for k in ("bhg", "bat", "ff2"))

    cm = _Comm()
    hs = {"out": _ag_d2d(cm, half["out"], waxis["out"]), "ff1": _ag_ici(cm, wblk["ff1"], waxis["ff1"], rows=(0, r1 // 2))}
    o_hg, s_all = _hgrn_fwd(p4, hg_lb_logits, hg_out_norm_g, H, comm=cm)
    wf["out"], half["ff1"] = cm.result(hs["out"]), cm.result(hs["ff1"])

    bucket = _bucket_ids()
    (bias_flat,) = _whole(lambda tb, bk: (_dot(tb, _onehot(bk), TN, precision=HIGHEST),), [rel_bias_table, bucket],
                          [((AH, AT_BLOCK * 2 * AT_BLOCK), F32)], "bias_fwd")
    bias = bias_flat.reshape(AH, AT_BLOCK, 2 * AT_BLOCK)
    q_t = _heads_first(pa[:, :ATW], AH)
    kp = _heads_first(pa[:, ATW:ATW + KVW], KVH)
    vp = _heads_first(pa[:, ATW + KVW:], KVH)
    sinks3 = attn_sinks.reshape(KVH, G, 1)
    cm = _Comm()
    hs = {"ff1": _ag_ici(cm, wblk["ff1"], waxis["ff1"], rows=(r1 // 2, r1), into=half["ff1"])}
    o_at = _heads_last(_attn_fwd(q_t, kp, vp, q_norm_g, k_norm_g, sinks3, bias, KVH, comm=cm))
    half["ff1"] = cm.result(hs["ff1"])

    bh = _mm(o_hg, wf["bhg"], "nn", F32, "branch_hg")
    ba = _mm(o_at, wf["bat"], "nn", F32, "branch_at")

    def merge_fn(bhv, bav, ghg, gat):
        return jax.nn.sigmoid(ghg) * bhv + jax.nn.sigmoid(gat) * bav

    cm = _Comm()
    hs = {"ff1": _ag_d2d(cm, half["ff1"], waxis["ff1"])}
    (merged,) = _rowwise(lambda *a: ((merge_fn(*a),), ()), [(bh, D, 0), (ba, D, 0), (pg, D, 0), (pg, D, 1)], [],
                         [(D, BF16)], [], "merge", comm=cm)
    wf["ff1"] = cm.result(hs["ff1"])
    cm = _Comm()
    hs = {"ff2": _ag_ici(cm, wblk["ff2"], waxis["ff2"], rows=(r2 // 4, 3 * r2 // 8), into=half["ff2"])}
    mo = _mm(merged, wf["out"], "nn", F32, "out_proj", comm=cm)
    half["ff2"] = cm.result(hs["ff2"])

    def resid1(xv, mov, g1, g2n, sh, sc):
        x1v = xv + g1 * mov
        return (x1v, _modnorm(x1v, g2n, sh, sc)), ()

    cm = _Comm()
    hs = {"ff2": _ag_ici(cm, wblk["ff2"], waxis["ff2"], rows=(3 * r2 // 8, r2 // 2), into=half["ff2"])}
    x1, h2 = _rowwise(resid1, [(x2, D, 0), (mo, D, 0)], [gate1, norm2_g, shift2, scale2], [(D, F32), (D, BF16)], [], "resid1",
                      comm=cm)
    half["ff2"] = cm.result(hs["ff2"])
    cm = _Comm()
    hs = {"ff2": _ag_ici(cm, wblk["ff2"], waxis["ff2"], rows=(r2 // 2, r2), into=half["ff2"])}
    u, act = _mm(h2, wf["ff1"], "nn", (F32, BF16), "ff1", comm=cm, epi=lambda r: (r, jnp.square(jnp.maximum(r, 0.0))))
    half["ff2"] = cm.result(hs["ff2"])
    cm = _Comm()
    hs = {"ff2": _ag_d2d(cm, half["ff2"], waxis["ff2"])}
    _call(lambda: None, [], name="ag_d2d_ff2", out_shape=(), comm=cm)
    wf["ff2"] = cm.result(hs["ff2"])
    ff = _mm(act, wf["ff2"], "nn", F32, "ff2")

    def loss_fn(x1v, ffv, tv, g2):
        e = x1v + g2 * ffv - tv
        dy = e * (1.0 / D)
        return (dy, dy * g2), (jnp.sum(e * e, axis=0, keepdims=True), jnp.sum(dy * ffv, axis=0, keepdims=True))

    dy, d_ff, sq_sum, d_gate2 = _rowwise(loss_fn, [(x1, D, 0), (ff, D, 0), (tgt, D, 0)], [gate2],
                                         [(D, F32), (D, BF16)], [(1, D), (1, D)], "loss")
    loss = lax.psum(jnp.sum(sq_sum) * (0.5 / D), ("x", "y", "c"))

    owner_base = jnp.stack([me ^ r for r in CHIP_RELS]).astype(jnp.int32)
    gw, recv1, part, recv2 = {}, {}, {}, {}
    gw["ff2"] = _mm(act, d_ff, "tn", BF16, "dw_ff2")
    cm = _Comm()
    hh = _rs_d2d(cm, gw["ff2"], waxis["ff2"])
    d_u = _mm(d_ff, wf["ff2"], "nt", BF16, "d_act", comm=cm, extras=[u], epi=lambda r, uv: (r * (2.0 * jnp.maximum(uv, 0.0)),))
    part["ff2"] = _rs_add(gw["ff2"], cm.result(hh), waxis["ff2"], owner_base, "rs_add_ff2")
    rows_ff2 = part["ff2"].shape[1]
    cm = _Comm()
    hh = _rs_ici(cm, part["ff2"], rows=(0, rows_ff2 // 2))
    gw["ff1"] = _mm(h2, d_u, "tn", BF16, "dw_ff1", comm=cm)
    cm2 = _Comm()
    hh2 = _rs_ici(cm2, part["ff2"], rows=(rows_ff2 // 2, rows_ff2), recv=cm.result(hh))
    hh1 = _rs_d2d(cm2, gw["ff1"], waxis["ff1"])
    d_h2 = _mm(d_u, wf["ff1"], "nt", F32, "d_h2", comm=cm2)
    recv2["ff2"] = cm2.result(hh2)
    part["ff1"] = _rs_add(gw["ff1"], cm2.result(hh1), waxis["ff1"], owner_base, "rs_add_ff1")
    rs_ff1 = _rs_split_start([part["ff1"]], "rs_ff1_start")
    (d_h2,) = _behind([d_h2], [rs_ff1["token"]])

    def norm2_bwd(dh2v, x1v, dyv, mov, g2n, sh, sc, g1):
        _, vjp = jax.vjp(_modnorm, x1v, g2n, sh, sc)
        dx, dg, dsh, dsc = vjp(dh2v)
        dx1 = dyv + dx
        return (dx1, dx1 * g1), (dg, dsh, dsc, jnp.sum(dx1 * mov, axis=0, keepdims=True))

    d_x1, d_mo, d_g2n, d_shift2, d_scale2, d_gate1 = _rowwise(
        norm2_bwd, [(d_h2, D, 0), (x1, D, 0), (dy, D, 0), (mo, D, 0)], [norm2_g, shift2, scale2, gate1],
        [(D, F32), (D, BF16)], [(1, D)] * 4, "norm2_bwd")
    gw["out"] = _mm(merged, d_mo, "tn", BF16, "dw_out")
    d_merged = _mm(d_mo, wf["out"], "nt", F32, "d_merged")

    def merge_bwd(dmv, bhv, bav, ghg, gat):
        _, vjp = jax.vjp(merge_fn, bhv, bav, ghg, gat)
        return vjp(dmv), ()

    d_bh, d_ba, d_ghg, d_gat = _rowwise(merge_bwd, [(d_merged, D, 0), (bh, D, 0), (ba, D, 0), (pg, D, 0), (pg, D, 1)], [],
                                        [(D, BF16)] * 4, [], "merge_bwd")
    gw["bhg"] = _mm(o_hg, d_bh, "tn", BF16, "dw_bhg")
    gw["bat"] = _mm(o_at, d_ba, "tn", BF16, "dw_bat")
    d_ohg = _mm(d_bh, wf["bhg"], "nt", F32, "d_ohg")
    d_oat = _mm(d_ba, wf["bat"], "nt", BF16, "d_oat")
    d_hq, d_hf, d_hi, d_hg, d_lb, d_gout_h = _hgrn_bwd(p4, hg_lb_logits, hg_out_norm_g, s_all, d_ohg, H)
    (part["ff1"],), (recv2["ff1"],) = _rs_split_wait(rs_ff1, [d_hq, d_oat], "rs_ff1_wait")

    (d_oat,) = _behind([d_oat], [recv2["ff1"]])
    cm = _Comm()
    hh = {k: _rs_d2d(cm, gw[k], waxis[k]) for k in small}
    dq_t, dkp, dvp, d_qg, d_kg, d_sk, d_bias = _attn_bwd(q_t, kp, vp, q_norm_g, k_norm_g, sinks3, bias,
                                                         _heads_first(d_oat, AH), KVH, comm=cm)
    for k in small:
        part[k] = _rs_add(gw[k], cm.result(hh[k]), waxis[k], owner_base, "rs_add_" + k)
    d_aq = _heads_last(dq_t)
    d_ak = _heads_last(dkp).astype(BF16)
    d_av = _heads_last(dvp).astype(BF16)
    d_proj = jnp.concatenate([d_hq, d_hf, d_hi, d_hg, d_aq, d_ak, d_av, d_ghg, d_gat], axis=1)
    cm = _Comm()
    hh = {k: _rs_ici(cm, part[k]) for k in small}
    gw_in = _mm(h, d_proj, "tn", BF16, "dw_in", comm=cm)
    for k in small:
        recv2[k] = cm.result(hh[k])

    wm = LANES * A
    cm = _Comm()
    hi_ = cm.inp(gw_in)
    h_main, h_mid = cm.out((4, D, wm), BF16), cm.out((4, D, LANES), BF16)
    for i, r in enumerate(CHIP_RELS):
        def main_view(ref, p, r=r):
            o = p["me"] ^ r ^ 1
            return ref.at[:, pl.ds(pl.multiple_of((PAIR * (o // 2) + (A + 1) * (1 - p["c"])) * LANES, LANES), wm)]

        def mid_view(ref, p, r=r):
            o = p["me"] ^ r
            return ref.at[:, pl.ds(pl.multiple_of((PAIR * (o // 2) + A) * LANES, LANES), LANES)]

        cm.copy(hi_, main_view, h_main, _slot_view(i), 1)
        cm.copy(hi_, mid_view, h_mid, _slot_view(i), 1)
    _call(lambda: None, [], name="rs_d2d_in", out_shape=(), comm=cm)
    chip = jnp.stack([(me ^ r) // 2 for r in CHIP_RELS]).astype(jnp.int32)
    part_main = _rs_add(gw_in, cm.result(h_main), 1, PAIR * chip + (A + 1) * cc, "rs_add_in_main", tw=LANES)
    part_mid = _rs_add(gw_in, cm.result(h_mid), 1, PAIR * chip + A, "rs_add_in_mid", tw=LANES)
    rs_in = _rs_split_start([part_main, part_mid], "rs_in_start")
    d_h = _mm(d_proj, w_in_full, "nt", F32, "d_h", tn=D, after=[rs_in["token"]])

    def norm1_bwd(dhv, xv, dx1v, g1n, sh, sc):
        _, vjp = jax.vjp(_modnorm, xv, g1n, sh, sc)
        dx, dg, dsh, dsc = vjp(dhv)
        return (dx1v + dx,), (dg, dsh, dsc)

    grad_x, d_g1n, d_shift1, d_scale1 = _rowwise(norm1_bwd, [(d_h, D, 0), (x2, D, 0), (d_x1, D, 0)],
                                                 [norm1_g, shift1, scale1], [(D, F32)], [(1, D)] * 3, "norm1_bwd")

    def sum4(p0, p1, p2, p3):
        return ((p0.astype(F32) + p1.astype(F32)) + p2.astype(F32)) + p3.astype(F32)

    def update_fn(w, m, v, p0, p1, p2, p3):
        g = sum4(p0, p1, p2, p3)
        delta, mn, vn = _adamw(w, g, m, v)
        return (g, delta, mn, vn), ()

    wmv = dict(zip(wnames, ((w_branch_hg, m_w_branch_hg, v_w_branch_hg), (w_branch_attn, m_w_branch_attn, v_w_branch_attn),
                            (w_out, m_w_out, v_w_out), (w_ff1, m_w_ff1, v_w_ff1), (w_ff2, m_w_ff2, v_w_ff2))))
    res = {}

    def update(k, p, rx):
        w, m, v = (t[0] for t in wmv[k])
        n = w.shape[1]
        ins = [(t, n, 0) for t in (w, m, v)] + [(p, n, 0, 0)] + [(rx, n, 0, i) for i in range(3)]
        res[k] = [t[None] for t in _rowwise(update_fn, ins, [], [(n, F32)] * 4, [], "update_" + k)]

    for k in wnames:
        update(k, part[k], recv2[k])
    (part_main, part_mid), (rx_main, rx_mid) = _rs_split_wait(rs_in, [grad_x] + [res[k][0] for k in wnames], "rs_in_wait")
    g_main, = _rowwise(lambda *p: ((sum4(*p),), ()), [(part_main, wm, 0, 0)] + [(rx_main, wm, 0, i) for i in range(3)], [],
                       [(wm, F32)], [], "sum_in_main")
    g_mid, = _rowwise(lambda *p: ((sum4(*p),), ()), [(part_mid, LANES, 0, 0)] + [(rx_mid, LANES, 0, i) for i in range(3)], [],
                      [(LANES, F32)], [], "sum_in_mid")
    g_in = jnp.where(cc == 0, jnp.concatenate([g_main, g_mid[:, :LANES // 2]], axis=1),
                     jnp.concatenate([g_mid[:, LANES // 2:], g_main], axis=1))

    def update_given(w, m, v, g):
        delta, mn, vn = _adamw(w, g, m, v)
        return (g, delta, mn, vn), ()

    res["in"] = [t[None] for t in _rowwise(update_given, [(t, BW, 0) for t in (w_in[0], m_w_in[0], v_w_in[0], g_in)], [],
                                           [(BW, F32)] * 4, [], "update_in")]

    d_sinks = d_sk.reshape(1, AH)
    (d_table_t,) = _whole(lambda db, bk: (_dot(db, _onehot(bk), NT, precision=HIGHEST),),
                          [d_bias.reshape(AH, AT_BLOCK * 2 * AT_BLOCK), bucket], [((AH, N_BUCKETS), F32)], "bias_bwd")
    smalls = [d_g1n, d_g2n, d_lb, d_gout_h, d_qg, d_kg, d_sinks, d_table_t.T.reshape(1, N_BUCKETS * AH)]
    widths = [s.shape[1] for s in smalls]
    lanes = [-(-w // LANES) * LANES for w in widths]
    smalls = [jnp.pad(s, ((0, 0), (0, p - w))) for s, w, p in zip(smalls, widths, lanes)]
    tail_row = jnp.concatenate([d_shift1, d_scale1, d_gate1, d_shift2, d_scale2, d_gate2] + smalls, axis=1)
    (tail_row,) = _behind([tail_row], [g_mid])
    tail_all = _gather_small(tail_row, me, "gather_tail")[:, 0, :]
    d_ada_all, packed = tail_all[:, :6 * D], tail_all[:, 6 * D:]
    d_ada_cols = lax.dynamic_slice(d_ada_all, (0, me * ADA_N), (N_DEV, ADA_N))

    def ada_update(cv, dav, w, m, v):
        g = _bdot(_silu(cv), dav, TN)
        delta, mn, vn = _adamw(w, g, m, v)
        return (g, delta, mn, vn), ()

    res["ada"] = [t[None] for t in _ada_update_call(ada_update, c_all, d_ada_cols, w_ada[0], m_w_ada[0], v_w_ada[0], _tile(D, 256, 16))]

    offs = [sum(lanes[:i]) for i in range(len(lanes))]

    def small_update(pk, dada, lg, *wmv_flat):
        tot = pk[0:1]
        for d in range(1, N_DEV):
            tot = tot + pk[d:d + 1]
        gb = dada[0:1]
        for d in range(1, N_DEV):
            gb = gb + dada[d:d + 1]
        gs = [tot[:, offs[i]:offs[i] + widths[i]] for i in range(len(widths))]
        _, lb_vjp = jax.vjp(_softmax0, lg)
        (g_lg,) = lb_vjp(gs[2])
        grads = [gb, gs[0], gs[1], g_lg, gs[3], gs[4], gs[5], gs[6], gs[7]]
        outs = []
        for i, g in enumerate(grads):
            w, m, v = wmv_flat[3 * i:3 * i + 3]
            delta, mn, vn = _adamw(w, g, m, v)
            outs += [g, delta, mn, vn]
        return tuple(outs)

    tbl = lambda t: t.reshape(1, N_BUCKETS * AH)
    small_wmv = [(b_ada, m_b_ada, v_b_ada), (norm1_g, m_norm1_g, v_norm1_g), (norm2_g, m_norm2_g, v_norm2_g),
                 (hg_lb_logits, m_hg_lb_logits, v_hg_lb_logits), (hg_out_norm_g, m_hg_out_norm_g, v_hg_out_norm_g),
                 (q_norm_g, m_q_norm_g, v_q_norm_g), (k_norm_g, m_k_norm_g, v_k_norm_g),
                 (attn_sinks, m_attn_sinks, v_attn_sinks),
                 (tbl(rel_bias_table), tbl(m_rel_bias_table), tbl(v_rel_bias_table))]
    flat = [t for trip in small_wmv for t in trip]
    out_shapes = [(trip[0].shape, F32) for trip in small_wmv for _ in range(4)]
    sres = _whole(small_update, [packed, d_ada_all, hg_lb_logits] + flat, out_shapes, "small_update")
    names_small = ("b_ada", "norm1_g", "norm2_g", "lb", "gout", "qg", "kg", "sinks", "table")
    for i, k in enumerate(names_small):
        r = sres[4 * i:4 * i + 4]
        if k == "table":
            r = [t.reshape(N_BUCKETS, AH) for t in r]
        res[k] = r

    order = ("ada", "b_ada", "norm1_g", "norm2_g", "in", "lb", "gout", "qg", "kg", "sinks", "table", "bhg", "bat", "out", "ff1", "ff2")
    outs = [loss, grad_x[None]]
    for j in range(4):
        outs += [res[k][j] for k in order]
    return tuple(outs)
```

```python
import functools
import math

import jax
import jax.numpy as jnp
from jax import lax
from jax.experimental import pallas as pl
from jax.experimental.pallas import tpu as pltpu

F32 = jnp.float32
BF16 = jnp.bfloat16
EPS = 1e-6
NEG_INF = -1e30
HG_DK = 128
HG_CHUNK = 64
AT_BLOCK = 128
N_BUCKETS = 32
MAX_EXACT = 16
MAX_DISTANCE = 128
N_DEV = 8
LANES = 128
VMEM_LIMIT = 56 * 1024 * 1024
ADAM_LR, ADAM_B1, ADAM_B2, ADAM_EPS, ADAM_WD, ADAM_STEP = 0.001, 0.9, 0.999, 1e-08, 0.01, 10
HIGHEST = lax.Precision.HIGHEST
MESH = pl.DeviceIdType.MESH
ANY = pl.BlockSpec(memory_space=pl.ANY)
CHIP_RELS = (0, 4, 2, 6)

NN = (((1,), (0,)), ((), ()))
NT = (((1,), (1,)), ((), ()))
TN = (((0,), (0,)), ((), ()))


def _tile(n, pref, unit):
    if n <= pref:
        return n
    t = (pref // unit) * unit
    while t >= unit:
        if n % t == 0:
            return t
        t -= unit
    return n


def _dot(a, b, dn, precision=None):
    return lax.dot_general(a, b, dn, preferred_element_type=F32, precision=precision)


def _bdot(a, b, dn):
    return _dot(a.astype(BF16), b.astype(BF16), dn)


def _position():
    x, y, c = lax.axis_index("x"), lax.axis_index("y"), lax.axis_index("c")
    return dict(x=x, y=y, c=c, me=4 * x + 2 * y + c)


def _peer_position(p, rel):
    x = 1 - p["x"] if rel & 4 else p["x"]
    y = 1 - p["y"] if rel & 2 else p["y"]
    c = 1 - p["c"] if rel & 1 else p["c"]
    return dict(x=x, y=y, c=c, me=4 * x + 2 * y + c)


class _Comm:
    def __init__(self):
        self.ins, self.outs, self.alias, self.plans, self.res = [], [], {}, [], None

    def inp(self, arr):
        self.ins.append(arr)
        return ("i", len(self.ins) - 1)

    def out(self, shape, dtype, alias=None):
        self.outs.append(jax.ShapeDtypeStruct(tuple(shape), dtype))
        if alias is not None:
            self.alias[alias[1]] = len(self.outs) - 1
        return ("o", len(self.outs) - 1)

    def copy(self, src, src_view, dst, dst_view, rel):
        self.plans.append((src, src_view, dst, dst_view, rel))

    def result(self, handle):
        return self.res[handle[1]]

    def build(self, in_refs, out_refs, send_sems, recv_sems):
        pos = _position()
        ref = lambda h: in_refs[h[1]] if h[0] == "i" else out_refs[h[1]]
        ops = []
        for k, (src, sv, dst, dv, rel) in enumerate(self.plans):
            s = sv(ref(src), pos)
            if rel == 0:
                cp = pltpu.make_async_copy(s, dv(ref(dst), pos), send_sems.at[k])
                ops.append((cp.start, cp.wait))
                continue
            peer = _peer_position(pos, rel)
            mk = lambda d: pltpu.make_async_remote_copy(
                src_ref=s, dst_ref=d, send_sem=send_sems.at[k], recv_sem=recv_sems.at[k],
                device_id=(peer["x"], peer["y"], peer["c"]), device_id_type=MESH)
            out_cp, in_cp = mk(dv(ref(dst), pos)), mk(dv(ref(dst), peer))

            def wait(out_cp=out_cp, in_cp=in_cp):
                out_cp.wait_send()
                in_cp.wait_recv()

            ops.append((out_cp.start, wait))
        return ops


def _call(body, args, *, name, out_shape, in_specs=None, out_specs=None, grid=None, scratch_shapes=(), comm=None,
          prefetch=None, aliases=None, after=()):
    single = not isinstance(out_shape, (tuple, list))
    out_shape = (out_shape,) if single else tuple(out_shape)
    n_in, n_out, n_scr = len(args), len(out_shape), len(scratch_shapes)
    vm = pl.BlockSpec(memory_space=pltpu.VMEM)
    in_specs = [vm] * n_in if in_specs is None else list(in_specs)
    out_specs = [vm] * n_out if out_specs is None else (list(out_specs) if isinstance(out_specs, (tuple, list)) else [out_specs])
    n_pf = 0 if prefetch is None else len(prefetch)
    kw = {} if aliases is None else {"input_output_aliases": dict(aliases)}
    if comm is None and after:
        n_dep = len(after)

        def fn(*refs):
            body(*refs[:n_pf + n_in], *refs[n_pf + n_in + n_dep:])

        all_args, all_scratch = list(args) + list(after), list(scratch_shapes)
        in_specs = in_specs + [ANY] * n_dep
    elif comm is None:
        fn = body
        all_args, all_scratch = list(args), list(scratch_shapes)
    else:
        n_ci, n_co, n_x = len(comm.ins), len(comm.outs), len(comm.plans)

        def fn(*refs):
            pf, refs = refs[:n_pf], refs[n_pf:]
            o_in, c_in = refs[:n_in], refs[n_in:n_in + n_ci]
            o_out = refs[n_in + n_ci:n_in + n_ci + n_out]
            c_out = refs[n_in + n_ci + n_out:n_in + n_ci + n_out + n_co]
            scr = refs[n_in + n_ci + n_out + n_co:]
            ops = comm.build(c_in, c_out, scr[n_scr], scr[n_scr + 1])
            if grid:
                first = functools.reduce(jnp.logical_and, [pl.program_id(i) == 0 for i in range(len(grid))])
                last = functools.reduce(jnp.logical_and, [pl.program_id(i) == g - 1 for i, g in enumerate(grid)])

                @pl.when(first)
                def _():
                    for start, _w in ops:
                        start()
            else:
                for start, _w in ops:
                    start()
            body(*pf, *o_in, *o_out, *scr[:n_scr])
            if grid:
                @pl.when(last)
                def _():
                    for _s, wait in ops:
                        wait()
            else:
                for _s, wait in ops:
                    wait()

        all_args = list(args) + list(comm.ins)
        in_specs = in_specs + [ANY] * n_ci
        out_shape = out_shape + tuple(comm.outs)
        out_specs = out_specs + [ANY] * n_co
        all_scratch = list(scratch_shapes) + [pltpu.SemaphoreType.DMA((n_x,)), pltpu.SemaphoreType.DMA((n_x,))]
        kw["input_output_aliases"] = {n_pf + n_in + i: n_out + o for i, o in comm.alias.items()}
    sem = None if grid is None else ("arbitrary",) * len(grid)
    params = pltpu.CompilerParams(dimension_semantics=sem, vmem_limit_bytes=VMEM_LIMIT)
    if prefetch is None:
        spec = dict(in_specs=in_specs, out_specs=tuple(out_specs), scratch_shapes=all_scratch)
        if grid is not None:
            spec["grid"] = grid
    else:
        spec = dict(grid_spec=pltpu.PrefetchScalarGridSpec(
            num_scalar_prefetch=n_pf, grid=grid, in_specs=in_specs, out_specs=tuple(out_specs), scratch_shapes=all_scratch))
        all_args = list(prefetch) + all_args
    res = pl.pallas_call(fn, name=name, out_shape=out_shape, compiler_params=params, **spec, **kw)(*all_args)
    res = list(res)
    if comm is not None:
        comm.res = res[n_out:]
        res = res[:n_out]
    return res[0] if single else res


def _whole_view(ref, pos):
    return ref


def _block_view(axis, n, index, rows=None):
    def view(ref, pos):
        off = pl.multiple_of(index(pos) * n, n)
        if rows is None:
            return ref.at[:, pl.ds(off, n)] if axis == 1 else ref.at[pl.ds(off, n), :]
        lo, cnt = rows[0], rows[1] - rows[0]
        if axis == 1:
            return ref.at[pl.ds(lo, cnt), pl.ds(off, n)]
        return ref.at[pl.ds(pl.multiple_of(off + lo, 16), cnt), :]
    return view


def _rows_view(rows):
    def view(ref, pos):
        return ref if rows is None else ref.at[pl.ds(rows[0], rows[1] - rows[0]), :]
    return view


def _slot_view(i, rows=None):
    def view(ref, pos):
        return ref.at[i] if rows is None else ref.at[i, pl.ds(rows[0], rows[1] - rows[0]), :]
    return view


def _exchange(items, name):
    cm = _Comm()
    for a, rel in items:
        cm.copy(cm.inp(a), _whole_view, cm.out(a.shape, a.dtype), _whole_view, rel)
    _call(lambda: None, [], name=name, out_shape=(), comm=cm)
    return cm.res


def _gather_small(v, me, name):
    cm = _Comm()
    hi, ho = cm.inp(v), cm.out((N_DEV,) + v.shape, v.dtype)
    for rel in range(N_DEV):
        cm.copy(hi, _whole_view, ho, lambda ref, p: ref.at[p["me"]], rel)
    _call(lambda: None, [], name=name, out_shape=(), comm=cm)
    return cm.result(ho)


def _ag_ici(cm, blk, axis, rows=None, into=None):
    n = blk.shape[axis]
    shape = list(blk.shape)
    shape[axis] = n * N_DEV
    hi = cm.inp(blk)
    ho = cm.out(shape, blk.dtype) if into is None else cm.out(shape, blk.dtype, alias=cm.inp(into))
    own = _block_view(axis, n, lambda p: p["me"], rows)
    for rel in CHIP_RELS:
        cm.copy(hi, _rows_view(rows), ho, own, rel)
    return ho


def _ag_d2d(cm, full, axis):
    n = full.shape[axis] // N_DEV
    hi = cm.inp(full)
    ho = cm.out(full.shape, full.dtype, alias=hi)
    for r in CHIP_RELS:
        v = _block_view(axis, n, functools.partial(lambda p, r: p["me"] ^ r, r=r))
        cm.copy(hi, v, ho, v, 1)
    return ho


def _rs_d2d(cm, gw, axis):
    n = gw.shape[axis] // N_DEV
    shape = list(gw.shape)
    shape[axis] = n
    hi, ho = cm.inp(gw), cm.out([4] + shape, gw.dtype)
    for i, r in enumerate(CHIP_RELS):
        cm.copy(hi, _block_view(axis, n, functools.partial(lambda p, r: p["me"] ^ r ^ 1, r=r)), ho, _slot_view(i), 1)
    return ho


def _rs_ici(cm, part, rows=None, recv=None):
    if recv is None:
        ho = cm.out((3,) + part.shape[1:], part.dtype)
    else:
        ho = cm.out(recv.shape, recv.dtype, alias=cm.inp(recv))
    hi = cm.inp(part)
    for i in (1, 2, 3):
        cm.copy(hi, _slot_view(i, rows), ho, _slot_view(i - 1, rows), CHIP_RELS[i])
    return ho


def _rs_add(gw, recv, axis, base, name, tw=None):
    _, R, n = recv.shape
    fan = 1
    if axis == 1:
        tw = n if tw is None else tw
        fan = max(f for f in (4, 3, 2, 1) if (n // tw) % f == 0)
        gw_specs = [pl.BlockSpec((R, tw), functools.partial(lambda i, t, b, k: (0, b[i] + fan * t + k), k=k)) for k in range(fan)]
        rv_spec = pl.BlockSpec((None, R, tw * fan), lambda i, t, b: (i, 0, t))
        grid = (4, n // (tw * fan))
    else:
        tw = _tile(n, 1024, LANES)
        gw_specs = [pl.BlockSpec((R, tw), lambda i, t, b: (b[i], t))]
        rv_spec = pl.BlockSpec((None, R, tw), lambda i, t, b: (i, 0, t))
        grid = (4, n // tw)

    def body(b_ref, *refs):
        g_refs, r_ref, o_ref = refs[:fan], refs[fan], refs[fan + 1]
        g = g_refs[0][...] if fan == 1 else jnp.concatenate([g[...] for g in g_refs], axis=1)
        o_ref[...] = (g.astype(F32) + r_ref[...].astype(F32)).astype(o_ref.dtype)

    return _call(body, [gw] * fan + [recv], name=name, out_shape=jax.ShapeDtypeStruct(recv.shape, recv.dtype), grid=grid,
                 in_specs=gw_specs + [rv_spec], out_specs=rv_spec, prefetch=[base])


HBM_SPEC = pl.BlockSpec(memory_space=pltpu.HBM)
SEM_SPEC = pl.BlockSpec(memory_space=pltpu.SEMAPHORE)
SPLIT_PARAMS = pltpu.CompilerParams(has_side_effects=pltpu.SideEffectType.DATAFLOW_SIDE_EFFECTING)


def _split_copies(refs, plans, send_sems, recv_sems):
    pos = _position()
    out = []
    for k, (si, sv, li, lv, rel) in enumerate(plans):
        peer = _peer_position(pos, rel)
        mk = lambda d: pltpu.make_async_remote_copy(
            src_ref=sv(refs[si], pos), dst_ref=d, send_sem=send_sems.at[k], recv_sem=recv_sems.at[k],
            device_id=(peer["x"], peer["y"], peer["c"]), device_id_type=MESH)
        out.append((mk(lv(refs[li], pos)), mk(lv(refs[li], peer))))
    return out


def _split_start(arrays, plans, name):
    n = len(arrays)

    def body(*refs):
        send_sems, recv_sems = refs[n], refs[n + 1]
        for out_cp, _ in _split_copies(refs[:n], plans, send_sems, recv_sems):
            out_cp.start()
        refs[-1][...] = jnp.zeros_like(refs[-1])

    sems = pltpu.SemaphoreType.DMA((len(plans),))
    res = pl.pallas_call(
        body, name=name,
        out_shape=(sems, sems) + tuple(pltpu.HBM(a.shape, a.dtype) for a in arrays) + (jax.ShapeDtypeStruct((8, LANES), F32),),
        in_specs=[HBM_SPEC] * n, out_specs=(SEM_SPEC, SEM_SPEC) + (HBM_SPEC,) * n + (pl.BlockSpec(memory_space=pltpu.VMEM),),
        input_output_aliases={i: 2 + i for i in range(n)}, compiler_params=SPLIT_PARAMS,
    )(*[pltpu.with_memory_space_constraint(a, pltpu.HBM) for a in arrays])
    return res[0], res[1], list(res[2:2 + n]), res[-1]


def _split_wait(send_sems, recv_sems, arrays, plans, after, name):
    n, na = len(arrays), len(after)

    def body(*refs):
        for out_cp, in_cp in _split_copies(refs[:n], plans, refs[n], refs[n + 1]):
            out_cp.wait_send()
            in_cp.wait_recv()

    res = pl.pallas_call(
        body, name=name, out_shape=tuple(pltpu.HBM(a.shape, a.dtype) for a in arrays),
        in_specs=[HBM_SPEC] * n + [SEM_SPEC, SEM_SPEC] + [ANY] * na, out_specs=(HBM_SPEC,) * n,
        input_output_aliases={i: i for i in range(n)}, compiler_params=SPLIT_PARAMS,
    )(*arrays, send_sems, recv_sems, *after)
    return list(res)


def _rs_split_start(parts, name):
    nw = len(parts)
    lands = [lax.empty((3,) + p.shape[1:], p.dtype) for p in parts]
    plans = [(s, _slot_view(i), nw + s, _slot_view(i - 1), CHIP_RELS[i]) for s in range(nw) for i in (1, 2, 3)]
    send_sems, recv_sems, arrays, token = _split_start(list(parts) + lands, plans, name)
    return dict(sems=(send_sems, recv_sems), arrays=arrays, plans=plans, token=token, nw=nw)


def _rs_split_wait(h, after, name):
    arrays = _split_wait(h["sems"][0], h["sems"][1], h["arrays"], h["plans"], after, name)
    return arrays[:h["nw"]], arrays[h["nw"]:]


def _behind(xs, tokens):
    out = lax.optimization_barrier((tuple(xs), tuple(tokens)))
    return list(out[0])


def _ag_w_in(src, a, D, INW):
    wm = LANES * a

    hd = D // 2
    ALL, TOP, BOT = (0, D), (0, hd), (hd, D)

    def main_place(ref, p, rows=ALL):
        off = pl.multiple_of(((2 * a + 1) * (p["me"] // 2) + (a + 1) * p["c"]) * LANES, LANES)
        return ref.at[pl.ds(rows[0], rows[1] - rows[0]), pl.ds(off, wm)]

    def main_src(ref, p):
        return ref.at[:, pl.ds(pl.multiple_of(p["c"] * LANES, LANES), wm)]

    def mid_src(ref, p):
        return ref.at[:, pl.ds(pl.multiple_of((1 - p["c"]) * wm, LANES), LANES)]

    def mid_place(ref, p, rows=ALL):
        return ref.at[p["me"], pl.ds(rows[0], rows[1] - rows[0]), :]

    def body(src_ref, full_ref, mid_ref, send_sems, recv_sems):
        pos = _position()
        sib, xn, yn = (_peer_position(pos, r) for r in (1, 4, 2))
        dg = _peer_position(pos, 6)
        started = []

        def remote(k, s, d, to):
            return pltpu.make_async_remote_copy(src_ref=s, dst_ref=d, send_sem=send_sems.at[k], recv_sem=recv_sems.at[k],
                                                device_id=(to["x"], to["y"], to["c"]), device_id_type=MESH)

        def send(k, owner, rows, to, from_src=False):
            for j, (src_v, place) in enumerate(((main_src, main_place), (mid_src, mid_place))):
                s = src_v(src_ref, pos) if from_src else place(full_ref if j == 0 else mid_ref, owner, rows)
                cp = remote(k + j, s, place(full_ref if j == 0 else mid_ref, owner, rows), to)
                cp.start()
                started.append(cp)

        def landed(k, owner, rows, frm):
            for j, place in enumerate((main_place, mid_place)):
                ref = full_ref if j == 0 else mid_ref
                remote(k + j, place(ref, owner, rows), place(ref, owner, rows), frm).wait_recv()

        local = [pltpu.make_async_copy(main_src(src_ref, pos), main_place(full_ref, pos), send_sems.at[18]),
                 pltpu.make_async_copy(mid_src(src_ref, pos), mid_place(mid_ref, pos), send_sems.at[19])]
        for cp in local:
            cp.start()
        send(0, pos, ALL, sib, from_src=True)
        send(2, pos, ALL, xn, from_src=True)
        send(4, pos, ALL, yn, from_src=True)
        landed(2, xn, ALL, xn)
        send(10, xn, ALL, sib)
        send(6, xn, TOP, yn)
        landed(4, yn, ALL, yn)
        send(12, yn, ALL, sib)
        send(8, yn, BOT, xn)
        landed(6, dg, TOP, yn)
        send(14, dg, TOP, sib)
        landed(8, dg, BOT, xn)
        send(16, dg, BOT, sib)
        sib_of = lambda p: _peer_position(p, 1)
        landed(0, sib, ALL, sib)
        landed(10, sib_of(xn), ALL, sib)
        landed(12, sib_of(yn), ALL, sib)
        landed(14, sib_of(dg), TOP, sib)
        landed(16, sib_of(dg), BOT, sib)
        for cp in started:
            cp.wait_send()
        for cp in local:
            cp.wait()

    return _call(body, [src], name="ag_w_in", in_specs=[ANY], out_specs=[ANY, ANY],
                 out_shape=(jax.ShapeDtypeStruct((D, INW), BF16), jax.ShapeDtypeStruct((N_DEV, D, LANES), BF16)),
                 scratch_shapes=[pltpu.SemaphoreType.DMA((20,)), pltpu.SemaphoreType.DMA((20,))])


def _patch_mid(full, mid, a):
    D = full.shape[0]

    def body(full_ref, e_ref, o_ref, out_ref):
        out_ref[...] = e_ref[...] + o_ref[...]

    return _call(body, [full, mid, mid], name="patch_mid", grid=(N_DEV // 2,),
                 out_shape=jax.ShapeDtypeStruct(full.shape, full.dtype),
                 in_specs=[ANY, pl.BlockSpec((None, D, LANES), lambda j: (2 * j, 0, 0)),
                           pl.BlockSpec((None, D, LANES), lambda j: (2 * j + 1, 0, 0))],
                 out_specs=pl.BlockSpec((D, LANES), lambda j: (0, (2 * a + 1) * j + a)), aliases={0: 0})


MM_RESIDENT = 2048


def _mm(a, b, mode, out_dtype, name, b_off=0, n=None, comm=None, extras=(), epi=None, tn=None, after=()):
    if mode == "nn":
        (M, K), (K2, N) = a.shape, b.shape
    elif mode == "nt":
        (M, K), (N, K2) = a.shape, b.shape
    else:
        (K, M), (K2, N) = a.shape, b.shape
    assert K == K2, (a.shape, b.shape, mode)
    if n is not None:
        N = n
    single = not isinstance(out_dtype, (tuple, list))
    out_dtypes = (out_dtype,) if single else tuple(out_dtype)
    if epi is None:
        epi = lambda r: (r,)
    tk = K if K <= MM_RESIDENT else (MM_RESIDENT if K % MM_RESIDENT == 0 else _tile(K, 512, LANES))
    nk = K // tk
    if M > MM_RESIDENT and mode == "tn" and N <= MM_RESIDENT and not b_off:
        tm, tn = _tile(M, 512, LANES), N
    elif nk > 1:
        tm, tn = _tile(M, 1024, LANES), _tile(N, tn or 1024, LANES)
    else:
        tm = _tile(M, MM_RESIDENT, LANES)
        tn = _tile(math.gcd(N, b_off) if b_off else N, tn or 512, LANES)
    jb = b_off // tn
    dn = {"nn": NN, "nt": NT, "tn": TN}[mode]
    ne, no = len(extras), len(out_dtypes)

    def body(a_ref, b_ref, *rest):
        e_refs, o_refs = rest[:ne], rest[ne:ne + no]

        def finish(r):
            for o_ref, v in zip(o_refs, epi(r, *[e[...] for e in e_refs])):
                o_ref[...] = v.astype(o_ref.dtype)

        if nk == 1:
            finish(_bdot(a_ref[...], b_ref[...], dn))
            return
        acc_ref = rest[ne + no]
        k = pl.program_id(2)

        @pl.when(k == 0)
        def _():
            acc_ref[...] = _bdot(a_ref[...], b_ref[...], dn)

        @pl.when(jnp.logical_and(k > 0, k < nk - 1))
        def _():
            acc_ref[...] += _bdot(a_ref[...], b_ref[...], dn)

        @pl.when(k == nk - 1)
        def _():
            finish(acc_ref[...] + _bdot(a_ref[...], b_ref[...], dn))

    a_spec = pl.BlockSpec((tk, tm), lambda i, j, k: (k, i)) if mode == "tn" else pl.BlockSpec((tm, tk), lambda i, j, k: (i, k))
    b_spec = pl.BlockSpec((tn, tk), lambda i, j, k: (j, k)) if mode == "nt" else pl.BlockSpec((tk, tn), lambda i, j, k: (k, j + jb))
    o_spec = pl.BlockSpec((tm, tn), lambda i, j, k: (i, j))
    res = _call(body, [a, b] + list(extras), name=name, grid=(M // tm, N // tn, nk),
                out_shape=tuple(jax.ShapeDtypeStruct((M, N), dt) for dt in out_dtypes),
                in_specs=[a_spec, b_spec] + [o_spec] * ne, out_specs=[o_spec] * no,
                scratch_shapes=[pltpu.VMEM((tm, tn), F32)] if nk > 1 else [], comm=comm, after=after)
    return res[0] if single else res


def _rowwise(fn, row_ins, bcast_ins, row_outs, acc_outs, name, rt=256, comm=None):
    L = row_ins[0][0].shape[-2]
    rt = _tile(L, rt, 16)
    nr, nb, no = len(row_ins), len(bcast_ins), len(row_outs)

    def body(*refs):
        i = pl.program_id(0)
        vals = [r[...] for r in refs[:nr + nb]]
        outs, accs = fn(*vals)
        for r, v in zip(refs[nr + nb:nr + nb + no], outs):
            r[...] = v.astype(r.dtype)
        acc_refs = refs[nr + nb + no:]

        @pl.when(i == 0)
        def _():
            for r in acc_refs:
                r[...] = jnp.zeros_like(r)

        for r, v in zip(acc_refs, accs):
            r[...] += v

    in_specs = []
    for spec in row_ins:
        w, cb = spec[1], spec[2]
        if len(spec) == 4:
            in_specs.append(pl.BlockSpec((None, rt, w), functools.partial(lambda i, cb, ld: (ld, i, cb), cb=cb, ld=spec[3])))
        else:
            in_specs.append(pl.BlockSpec((rt, w), functools.partial(lambda i, cb: (i, cb), cb=cb)))
    in_specs += [pl.BlockSpec(b.shape, lambda i: (0, 0)) for b in bcast_ins]
    out_specs = [pl.BlockSpec((rt, w), lambda i: (i, 0)) for w, _ in row_outs]
    out_specs += [pl.BlockSpec(s, lambda i: (0, 0)) for s in acc_outs]
    out_shape = [jax.ShapeDtypeStruct((L, w), dt) for w, dt in row_outs] + [jax.ShapeDtypeStruct(s, F32) for s in acc_outs]
    return _call(body, [s[0] for s in row_ins] + list(bcast_ins), name=name, grid=(L // rt,), out_shape=tuple(out_shape),
                 in_specs=in_specs, out_specs=out_specs, comm=comm)


def _whole(fn, ins, out_shapes, name):
    def body(*refs):
        outs = fn(*[r[...] for r in refs[:len(ins)]])
        for r, v in zip(refs[len(ins):], outs):
            r[...] = v.astype(r.dtype)

    return _call(body, list(ins), name=name, out_shape=tuple(jax.ShapeDtypeStruct(s, dt) for s, dt in out_shapes))


def _silu(x):
    return x * jax.nn.sigmoid(x)


def _rms(x, g):
    return (x * lax.rsqrt(jnp.mean(x * x, axis=-1, keepdims=True) + EPS)) * g


def _modnorm(x, g, shift, scale):
    return _rms(x, g) * (1.0 + scale) + shift


def _adamw(w, g, m, v):
    m = ADAM_B1 * m + (1.0 - ADAM_B1) * g
    v = ADAM_B2 * v + (1.0 - ADAM_B2) * jnp.square(g)
    m_hat = m / (1.0 - ADAM_B1 ** ADAM_STEP)
    v_hat = v / (1.0 - ADAM_B2 ** ADAM_STEP)
    delta = -ADAM_LR * (m_hat / (jnp.sqrt(v_hat) + ADAM_EPS) + ADAM_WD * w)
    return delta, m, v


def _lower_bound(lg):
    e = jnp.exp(lg - jnp.max(lg, axis=0, keepdims=True))
    return e[0:1] / jnp.sum(e, axis=0, keepdims=True)


def _hg_stages(hq_l, hf_l, hi_l, lb):
    C = hq_l[0].shape[0]
    row = lax.broadcasted_iota(jnp.int32, (C, C), 0)
    col = lax.broadcasted_iota(jnp.int32, (C, C), 1)
    tri = row >= col
    trif = tri.astype(F32)
    f_l = [lb + (1.0 - lb) * jax.nn.sigmoid(hf) for hf in hf_l]
    b_l = [_dot(trif, jnp.log(f), NN, precision=HIGHEST) for f in f_l]
    q_l = [_silu(hq) for hq in hq_l]
    m_l = [b[C // 2 - 1:C // 2] for b in b_l]
    bl_l = [b[C - 1:C] for b in b_l]
    sc_l = [jnp.where(tri, _bdot(q * jnp.exp(b - m), (1.0 - f) * jnp.exp(m - b), NT), 0.0)
            for q, f, b, m in zip(q_l, f_l, b_l, m_l)]
    o1_l = [_bdot(sc, hi, NN) for sc, hi in zip(sc_l, hi_l)]
    u_l = [_bdot(hi, (1.0 - f) * jnp.exp(bl - b), TN) for hi, f, b, bl in zip(hi_l, f_l, b_l, bl_l)]
    qb_l = [q * jnp.exp(b) for q, b in zip(q_l, b_l)]
    dec_l = [jnp.exp(bl) for bl in bl_l]
    return list(zip(o1_l, u_l, qb_l, dec_l))


def _hg_out(o, hgate, gout):
    return _rms(o, gout) * _silu(hgate)


HG_STAGE = 8
HG_GROUP = 32


def _hgrn_fwd(p4, lb_logits, gout, H, comm=None):
    L = p4.shape[0]
    C = HG_CHUNK
    GR = _tile(L // C, HG_GROUP, 1)
    T = GR * C
    N = L // T

    def body(hq_ref, hf_ref, hi_ref, hg_ref, lg_ref, gout_ref, o_ref, s_ref, st_ref):
        @pl.when(pl.program_id(1) == 0)
        def _():
            st_ref[...] = jnp.zeros_like(st_ref)

        lb = _lower_bound(lg_ref[...])
        st = st_ref[...]
        for c0 in range(0, GR, HG_STAGE):
            rows_l = [pl.ds(ci * C, C) for ci in range(c0, min(c0 + HG_STAGE, GR))]
            parts = _hg_stages([hq_ref[r, :] for r in rows_l], [hf_ref[r, :] for r in rows_l],
                               [hi_ref[r, :] for r in rows_l], lb)
            for ci, rows, (o1, u, qb, dec) in zip(range(c0, GR), rows_l, parts):
                s_ref[0, ci] = st
                o = o1 + _bdot(qb, st, NT)
                st = st * dec + u
                o_ref[rows, :] = _hg_out(o, hg_ref[rows, :], gout_ref[...]).astype(o_ref.dtype)
        st_ref[...] = st

    blk = lambda s: pl.BlockSpec((T, HG_DK), functools.partial(lambda h, n, s: (n, s * H + h), s=s))
    return _call(
        body, [p4, p4, p4, p4, lb_logits, gout], name="hgrn_fwd", grid=(H, N),
        out_shape=(jax.ShapeDtypeStruct((L, H * HG_DK), BF16), jax.ShapeDtypeStruct((H, N * GR, HG_DK, HG_DK), F32)),
        in_specs=[blk(0), blk(1), blk(2), blk(3), pl.BlockSpec((2, HG_DK), lambda h, n: (0, h)),
                  pl.BlockSpec((1, HG_DK), lambda h, n: (0, 0))],
        out_specs=(pl.BlockSpec((T, HG_DK), lambda h, n: (n, h)),
                   pl.BlockSpec((1, GR, HG_DK, HG_DK), lambda h, n: (h, n, 0, 0))),
        scratch_shapes=[pltpu.VMEM((HG_DK, HG_DK), F32)], comm=comm)


def _hgrn_bwd(p4, lb_logits, gout, s_all, d_out, H, comm=None):
    L = p4.shape[0]
    C = HG_CHUNK
    GR = _tile(L // C, HG_GROUP, 1)
    T = GR * C
    N = L // T

    def body(hq_ref, hf_ref, hi_ref, hg_ref, lg_ref, gout_ref, s_ref, do_ref,
             dq_ref, df_ref, di_ref, dg_ref, dlb_ref, dgo_ref, dst_ref):
        @pl.when(pl.program_id(1) == 0)
        def _():
            dst_ref[...] = jnp.zeros_like(dst_ref)
            dlb_ref[...] = jnp.zeros_like(dlb_ref)

        @pl.when(jnp.logical_and(pl.program_id(0) == 0, pl.program_id(1) == 0))
        def _():
            dgo_ref[...] = jnp.zeros_like(dgo_ref)

        lb = _lower_bound(lg_ref[...])
        dst = dst_ref[...]
        d_lb = jnp.zeros((1, HG_DK), F32)
        d_go = jnp.zeros((1, HG_DK), F32)
        for c0 in reversed(range(0, GR, HG_STAGE)):
            dst, d_lb_c, d_go_c = chunks_bwd(list(range(c0, min(c0 + HG_STAGE, GR))), lb, dst, hq_ref, hf_ref, hi_ref,
                                             hg_ref, gout_ref, s_ref, do_ref, dq_ref, df_ref, di_ref, dg_ref)
            d_lb += d_lb_c
            d_go += d_go_c
        dst_ref[...] = dst
        dlb_ref[...] += d_lb
        dgo_ref[...] += d_go

    def chunks_bwd(idx, lb, dst, hq_ref, hf_ref, hi_ref, hg_ref, gout_ref, s_ref, do_ref, dq_ref, df_ref, di_ref, dg_ref):
        n = len(idx)
        rows_l = [pl.ds(ci * C, C) for ci in idx]
        hq_l, hf_l, hi_l = ([r[rows, :] for rows in rows_l] for r in (hq_ref, hf_ref, hi_ref))
        st_l = [s_ref[0, ci] for ci in idx]
        row = lax.broadcasted_iota(jnp.int32, (C, C), 0)
        col = lax.broadcasted_iota(jnp.int32, (C, C), 1)
        tri = row >= col
        trif = tri.astype(F32)
        every = lambda fn, *ls: [fn(*a) for a in zip(*ls)]
        sg_l = every(jax.nn.sigmoid, hf_l)
        f_l = every(lambda sg: lb + (1.0 - lb) * sg, sg_l)
        b_l = every(lambda f: _dot(trif, jnp.log(f), NN, precision=HIGHEST), f_l)
        q_l = every(_silu, hq_l)
        m_l = every(lambda b: b[C // 2 - 1:C // 2], b_l)
        bl_l = every(lambda b: b[C - 1:C], b_l)
        e_qm_l = every(lambda b, m: jnp.exp(b - m), b_l, m_l)
        e_km_l = every(lambda b, m: jnp.exp(m - b), b_l, m_l)
        e_kl_l = every(lambda b, bl: jnp.exp(bl - b), b_l, bl_l)
        e_q_l = every(jnp.exp, b_l)
        dec_l = every(jnp.exp, bl_l)
        qe_l = every(lambda q, e: q * e, q_l, e_qm_l)
        ke_l = every(lambda f, e: (1.0 - f) * e, f_l, e_km_l)
        kd_l = every(lambda f, e: (1.0 - f) * e, f_l, e_kl_l)
        qb_l = every(lambda q, e: q * e, q_l, e_q_l)
        sc_l = every(lambda qe, ke: jnp.where(tri, _bdot(qe, ke, NT), 0.0), qe_l, ke_l)
        o_l = every(lambda sc, hi, qb, st: _bdot(sc, hi, NN) + _bdot(qb, st, NT), sc_l, hi_l, qb_l, st_l)
        vj_l = every(lambda o, rows: jax.vjp(_hg_out, o, hg_ref[rows, :], gout_ref[...])[1](do_ref[rows, :]), o_l, rows_l)
        do_l = [v[0] for v in vj_l]
        dsc_l = every(lambda do, hi: jnp.where(tri, _bdot(do, hi, NT), 0.0), do_l, hi_l)
        dv1_l = every(lambda sc, do: _bdot(sc, do, TN), sc_l, do_l)
        dqe_l = every(lambda dsc, ke: _bdot(dsc, ke, NN), dsc_l, ke_l)
        dke_l = every(lambda dsc, qe: _bdot(dsc, qe, TN), dsc_l, qe_l)
        dqb_l = every(lambda do, st: _bdot(do, st, NN), do_l, st_l)
        own_l = every(lambda do, qb: _bdot(do, qb, TN), do_l, qb_l)
        dst_next_l = [None] * n
        for j in reversed(range(n)):
            dst_next_l[j] = dst
            dst = own_l[j] + dst * dec_l[j]
        dv_l = every(lambda dv1, kd, dn: dv1 + _bdot(kd, dn, NT), dv1_l, kd_l, dst_next_l)
        dkd_l = every(lambda hi, dn: _bdot(hi, dn, NN), hi_l, dst_next_l)
        ddec_l = every(lambda dn, st: jnp.sum(dn * st, axis=0, keepdims=True), dst_next_l, st_l)
        rowi = lax.broadcasted_iota(jnp.int32, (C, HG_DK), 0)
        tq_l = every(lambda a, b_: a * b_, dqe_l, qe_l)
        tk_l = every(lambda a, b_: a * b_, dke_l, ke_l)
        td_l = every(lambda a, b_: a * b_, dkd_l, kd_l)
        tb_l = every(lambda a, b_: a * b_, dqb_l, qb_l)
        db_l = every(lambda tq, tk, td, tb, ddec, dec: tq - tk - td + tb
                     + jnp.where(rowi == C // 2 - 1, jnp.sum(tk - tq, axis=0, keepdims=True), 0.0)
                     + jnp.where(rowi == C - 1, jnp.sum(td, axis=0, keepdims=True) + ddec * dec, 0.0),
                     tq_l, tk_l, td_l, tb_l, ddec_l, dec_l)
        dlf_l = every(lambda db: _dot(trif, db, TN, precision=HIGHEST), db_l)
        dk_l = every(lambda dke, e1, dkd, e2: dke * e1 + dkd * e2, dke_l, e_km_l, dkd_l, e_kl_l)
        df_l = every(lambda dlf, f, dk: dlf / f - dk, dlf_l, f_l, dk_l)
        d_lb = jnp.zeros((1, HG_DK), F32)
        d_go = jnp.zeros((1, HG_DK), F32)
        for j, rows in enumerate(rows_l):
            sg, hq = sg_l[j], hq_l[j]
            df_ref[rows, :] = (df_l[j] * (1.0 - lb) * sg * (1.0 - sg)).astype(df_ref.dtype)
            sq = jax.nn.sigmoid(hq)
            dq = dqe_l[j] * e_qm_l[j] + dqb_l[j] * e_q_l[j]
            dq_ref[rows, :] = (dq * (sq * (1.0 + hq * (1.0 - sq)))).astype(dq_ref.dtype)
            di_ref[rows, :] = dv_l[j].astype(di_ref.dtype)
            dg_ref[rows, :] = vj_l[j][1].astype(dg_ref.dtype)
            d_lb += jnp.sum(df_l[j] * (1.0 - sg), axis=0, keepdims=True)
            d_go += vj_l[j][2]
        return dst, d_lb, d_go

    blk = lambda s: pl.BlockSpec((T, HG_DK), functools.partial(lambda h, n, s: (N - 1 - n, s * H + h), s=s))
    oblk = pl.BlockSpec((T, HG_DK), lambda h, n: (N - 1 - n, h))
    vec = pl.BlockSpec((1, HG_DK), lambda h, n: (0, h))
    W = H * HG_DK
    return _call(
        body, [p4, p4, p4, p4, lb_logits, gout, s_all, d_out], name="hgrn_bwd", grid=(H, N),
        out_shape=tuple([jax.ShapeDtypeStruct((L, W), BF16)] * 4 + [jax.ShapeDtypeStruct((1, W), F32), jax.ShapeDtypeStruct((1, HG_DK), F32)]),
        in_specs=[blk(0), blk(1), blk(2), blk(3), pl.BlockSpec((2, HG_DK), lambda h, n: (0, h)),
                  pl.BlockSpec((1, HG_DK), lambda h, n: (0, 0)),
                  pl.BlockSpec((1, GR, HG_DK, HG_DK), lambda h, n: (h, N - 1 - n, 0, 0)), oblk],
        out_specs=(oblk, oblk, oblk, oblk, vec, pl.BlockSpec((1, HG_DK), lambda h, n: (0, 0))),
        scratch_shapes=[pltpu.VMEM((HG_DK, HG_DK), F32)], comm=comm)


def _bucket_ids():
    i = jnp.arange(AT_BLOCK, dtype=jnp.int32)[:, None]
    j = jnp.arange(2 * AT_BLOCK, dtype=jnp.int32)[None, :]
    n = jnp.maximum(i - j + AT_BLOCK, 0)
    nf = jnp.maximum(n, 1).astype(F32)
    large = MAX_EXACT + (jnp.log(nf / MAX_EXACT) / math.log(MAX_DISTANCE / MAX_EXACT) * (N_BUCKETS - MAX_EXACT)).astype(jnp.int32)
    large = jnp.minimum(large, N_BUCKETS - 1)
    return jnp.where(n < MAX_EXACT, n, large).reshape(1, -1)


def _onehot(bucket):
    ids = lax.broadcasted_iota(jnp.int32, (N_BUCKETS, bucket.shape[1]), 0)
    return (ids == bucket).astype(F32)


def _attn_probs(qn, kpn, kcn, bias_g, sink, first, scale):
    rows = qn.shape[0]
    i = jnp.bitwise_and(lax.broadcasted_iota(jnp.int32, (rows, AT_BLOCK), 0), AT_BLOCK - 1)
    j = lax.broadcasted_iota(jnp.int32, (rows, AT_BLOCK), 1)
    lp = _bdot(qn, kpn, NT) * scale + bias_g[:, :AT_BLOCK]
    lc = _bdot(qn, kcn, NT) * scale + bias_g[:, AT_BLOCK:]
    lp = jnp.where(jnp.logical_and(j > i, jnp.logical_not(first)), lp, NEG_INF)
    lc = jnp.where(j <= i, lc, NEG_INF)
    m = jnp.maximum(jnp.maximum(jnp.max(lp, axis=-1, keepdims=True), jnp.max(lc, axis=-1, keepdims=True)), sink)
    pp, pc, ps = jnp.exp(lp - m), jnp.exp(lc - m), jnp.exp(sink - m)
    den = jnp.sum(pp, axis=-1, keepdims=True) + jnp.sum(pc, axis=-1, keepdims=True) + ps
    return pp / den, pc / den, ps / den


def _sink_rows(sk_ref, G):
    head = lax.broadcasted_iota(jnp.int32, (G * AT_BLOCK, 1), 0) // AT_BLOCK
    sink = jnp.zeros((G * AT_BLOCK, 1), F32)
    for g in range(G):
        sink = jnp.where(head == g, sk_ref[0, g:g + 1, :], sink)
    return sink


def _attn_fwd(q_t, kp, vp, qg, kg, sinks, bias, KVH, comm=None):
    AH, L, DH = q_t.shape
    G = AH // KVH
    NB = L // AT_BLOCK
    scale = DH ** -0.5

    def body(q_ref, kp_ref, kc_ref, vp_ref, vc_ref, qg_ref, kg_ref, sk_ref, b_ref, o_ref):
        first = pl.program_id(1) == 0
        kpn, kcn = _rms(kp_ref[0], kg_ref[...]), _rms(kc_ref[0], kg_ref[...])
        qn = _rms(q_ref[...].reshape(G * AT_BLOCK, DH), qg_ref[...])
        sink = _sink_rows(sk_ref, G)
        pp, pc, _ = _attn_probs(qn, kpn, kcn, b_ref[...].reshape(G * AT_BLOCK, 2 * AT_BLOCK), sink, first, scale)
        o = _bdot(pp, vp_ref[0], NN) + _bdot(pc, vc_ref[0], NN)
        o_ref[...] = o.reshape(G, AT_BLOCK, DH).astype(o_ref.dtype)

    kblk = lambda off: pl.BlockSpec((1, AT_BLOCK, DH),
                                    functools.partial(lambda h, n, off: (h, jnp.maximum(n + off - 1, 0), 0), off=off))
    return _call(
        body, [q_t, kp, kp, vp, vp, qg, kg, sinks, bias], name="attn_fwd", grid=(KVH, NB),
        out_shape=jax.ShapeDtypeStruct((AH, L, DH), BF16),
        in_specs=[pl.BlockSpec((G, AT_BLOCK, DH), lambda h, n: (h, n, 0)), kblk(0), kblk(1), kblk(0), kblk(1),
                  pl.BlockSpec((1, DH), lambda h, n: (0, 0)), pl.BlockSpec((1, DH), lambda h, n: (0, 0)),
                  pl.BlockSpec((1, G, 1), lambda h, n: (h, 0, 0)),
                  pl.BlockSpec((G, AT_BLOCK, 2 * AT_BLOCK), lambda h, n: (h, 0, 0))],
        out_specs=pl.BlockSpec((G, AT_BLOCK, DH), lambda h, n: (h, n, 0)), comm=comm)


def _attn_bwd(q_t, kp, vp, qg, kg, sinks, bias, do_t, KVH, comm=None):
    AH, L, DH = q_t.shape
    G = AH // KVH
    NB = L // AT_BLOCK
    B = AT_BLOCK
    scale = DH ** -0.5

    def body(q_ref, kp_ref, kc_ref, vp_ref, vc_ref, qg_ref, kg_ref, sk_ref, b_ref, do_ref,
             dq_ref, dk_ref, dv_ref, dqg_ref, dkg_ref, dsk_ref, db_ref):
        n = pl.program_id(1)
        first = n == 0

        @pl.when(first)
        def _():
            for r in (dk_ref, dv_ref, dsk_ref, db_ref):
                r[...] = jnp.zeros_like(r)

        @pl.when(jnp.logical_and(first, pl.program_id(0) == 0))
        def _():
            dqg_ref[...] = jnp.zeros_like(dqg_ref)
            dkg_ref[...] = jnp.zeros_like(dkg_ref)

        kp_raw, kc_raw, kgv, qgv = kp_ref[0], kc_ref[0], kg_ref[...], qg_ref[...]
        kpn, kp_vjp = jax.vjp(_rms, kp_raw, kgv)
        kcn, kc_vjp = jax.vjp(_rms, kc_raw, kgv)
        qn, q_vjp = jax.vjp(_rms, q_ref[...].reshape(G * B, DH), qgv)
        pp, pc, ps = _attn_probs(qn, kpn, kcn, b_ref[...].reshape(G * B, 2 * B), _sink_rows(sk_ref, G), first, scale)
        do = do_ref[...].reshape(G * B, DH)
        dvp = _bdot(pp, do, TN)
        dvc = _bdot(pc, do, TN)
        dpp = _bdot(do, vp_ref[0], NT)
        dpc = _bdot(do, vc_ref[0], NT)
        dsum = jnp.sum(dpp * pp, axis=-1, keepdims=True) + jnp.sum(dpc * pc, axis=-1, keepdims=True)
        dlp = pp * (dpp - dsum)
        dlc = pc * (dpc - dsum)
        dsk_ref[0] += jnp.sum((-ps * dsum).reshape(G, B, 1), axis=1)
        db_ref[:, :, :B] += dlp.reshape(G, B, B)
        db_ref[:, :, B:] += dlc.reshape(G, B, B)
        dlp, dlc = dlp * scale, dlc * scale
        dqn = _bdot(dlp, kpn, NN) + _bdot(dlc, kcn, NN)
        dq_raw, dqg = q_vjp(dqn)
        dq_ref[...] = dq_raw.reshape(G, B, DH).astype(dq_ref.dtype)
        dkp_raw, dkg_p = kp_vjp(_bdot(dlp, qn, TN))
        dkc_raw, dkg_c = kc_vjp(_bdot(dlc, qn, TN))
        r0 = pl.multiple_of(jnp.maximum(n - 1, 0) * B, B)
        r1 = pl.multiple_of(n * B, B)
        dk_ref[0, pl.ds(r0, B), :] += dkp_raw
        dk_ref[0, pl.ds(r1, B), :] += dkc_raw
        dv_ref[0, pl.ds(r0, B), :] += dvp
        dv_ref[0, pl.ds(r1, B), :] += dvc
        dqg_ref[...] += dqg
        dkg_ref[...] += dkg_p + dkg_c

    kblk = lambda off: pl.BlockSpec((1, B, DH), functools.partial(lambda h, n, off: (h, jnp.maximum(n + off - 1, 0), 0), off=off))
    qblk = pl.BlockSpec((G, B, DH), lambda h, n: (h, n, 0))
    accblk = pl.BlockSpec((1, L, DH), lambda h, n: (h, 0, 0))
    vecblk = pl.BlockSpec((1, DH), lambda h, n: (0, 0))
    return _call(
        body, [q_t, kp, kp, vp, vp, qg, kg, sinks, bias, do_t], name="attn_bwd", grid=(KVH, NB),
        out_shape=(jax.ShapeDtypeStruct((AH, L, DH), BF16), jax.ShapeDtypeStruct((KVH, L, DH), F32),
                   jax.ShapeDtypeStruct((KVH, L, DH), F32), jax.ShapeDtypeStruct((1, DH), F32),
                   jax.ShapeDtypeStruct((1, DH), F32), jax.ShapeDtypeStruct((KVH, G, 1), F32),
                   jax.ShapeDtypeStruct((AH, B, 2 * B), F32)),
        in_specs=[qblk, kblk(0), kblk(1), kblk(0), kblk(1),
                  pl.BlockSpec((1, DH), lambda h, n: (0, 0)), pl.BlockSpec((1, DH), lambda h, n: (0, 0)),
                  pl.BlockSpec((1, G, 1), lambda h, n: (h, 0, 0)),
                  pl.BlockSpec((G, B, 2 * B), lambda h, n: (h, 0, 0)), qblk],
        out_specs=(qblk, accblk, accblk, vecblk, vecblk, pl.BlockSpec((1, G, 1), lambda h, n: (h, 0, 0)),
                   pl.BlockSpec((G, B, 2 * B), lambda h, n: (h, 0, 0))), comm=comm)


def _heads_first(t, nh):
    L = t.shape[0]
    return jnp.transpose(t.reshape(L, nh, t.shape[1] // nh), (1, 0, 2))


def _heads_last(t):
    nh, L, dh = t.shape
    return jnp.transpose(t, (1, 0, 2)).reshape(L, nh * dh)


def _softmax0(lg):
    e = jnp.exp(lg - jnp.max(lg, axis=0, keepdims=True))
    return e[0:1] / jnp.sum(e, axis=0, keepdims=True)


def _ada_update_call(fn, c_all, d_cols, w, m, v, rt):
    D, n = w.shape

    def body(c_ref, d_ref, w_ref, m_ref, v_ref, g_out, dl_out, m_out, v_out):
        outs, _ = fn(c_ref[...], d_ref[...], w_ref[...], m_ref[...], v_ref[...])
        for r, val in zip((g_out, dl_out, m_out, v_out), outs):
            r[...] = val

    wblk = pl.BlockSpec((rt, n), lambda i: (i, 0))
    return _call(
        body, [c_all, d_cols, w, m, v], name="update_ada", grid=(D // rt,), out_shape=tuple([jax.ShapeDtypeStruct((D, n), F32)] * 4),
        in_specs=[pl.BlockSpec((N_DEV, rt), lambda i: (0, i)), pl.BlockSpec((N_DEV, n), lambda i: (0, 0)), wblk, wblk, wblk],
        out_specs=(wblk, wblk, wblk, wblk))


def kernel(x, c, w_ada, b_ada, norm1_g, norm2_g, w_in, hg_lb_logits, hg_out_norm_g, q_norm_g, k_norm_g, attn_sinks, rel_bias_table, w_branch_hg, w_branch_attn, w_out, w_ff1, w_ff2, loss_target, m_w_ada, m_b_ada, m_norm1_g, m_norm2_g, m_w_in, m_hg_lb_logits, m_hg_out_norm_g, m_q_norm_g, m_k_norm_g, m_attn_sinks, m_rel_bias_table, m_w_branch_hg, m_w_branch_attn, m_w_out, m_w_ff1, m_w_ff2, v_w_ada, v_b_ada, v_norm1_g, v_norm2_g, v_w_in, v_hg_lb_logits, v_hg_out_norm_g, v_q_norm_g, v_k_norm_g, v_attn_sinks, v_rel_bias_table, v_w_branch_hg, v_w_branch_attn, v_w_out, v_w_ff1, v_w_ff2):
    cc = lax.axis_index("c")
    me = 4 * lax.axis_index("x") + 2 * lax.axis_index("y") + cc
    x2 = x[0]
    tgt = loss_target[0]
    L, D = x2.shape
    HGW = hg_lb_logits.shape[1]
    H = HGW // HG_DK
    AH = attn_sinks.shape[1]
    DH = q_norm_g.shape[1]
    ATW = AH * DH
    BW = w_in.shape[2]
    INW = BW * N_DEV
    A = BW // LANES
    assert BW == LANES * A + LANES // 2
    KVW = (INW - 4 * HGW - ATW - 2 * D) // 2
    KVH = KVW // DH
    G = AH // KVH
    ADA_N = w_ada.shape[2]
    PAIR = 2 * A + 1

    c_all = _gather_small(c, me, "gather_c")[:, 0, :]
    b_cols = lax.dynamic_slice(b_ada, (0, me * ADA_N), (1, ADA_N))
    (ada_cols,) = _whole(lambda cv, w, b: (_bdot(_silu(cv), w, NN) + b,), [c_all, w_ada[0], b_cols],
                         [((N_DEV, ADA_N), F32)], "ada_fwd")
    ada_all = _gather_small(ada_cols, me, "gather_ada")
    ada_row = lax.dynamic_slice(ada_all, (0, me, 0), (N_DEV, 1, ADA_N)).reshape(1, 6 * D)

    w_in_b = w_in[0].astype(BF16)
    src_in = jnp.where(cc == 0, jnp.pad(w_in_b, ((0, 0), (0, LANES // 2))), jnp.pad(w_in_b, ((0, 0), (LANES // 2, 0))))
    (src_in,) = _behind([src_in], [ada_row])
    shift1, scale1, gate1, shift2, scale2, gate2 = [ada_row[:, i * D:(i + 1) * D] for i in range(6)]
    w_in_gapped, w_in_mid = _ag_w_in(src_in, A, D, INW)
    w_in_full = _patch_mid(w_in_gapped, w_in_mid, A)

    wnames = ("bhg", "bat", "out", "ff1", "ff2")
    small = ("bhg", "bat", "out")
    waxis = dict(zip(wnames, (1, 1, 0, 1, 0)))
    wsrc = dict(zip(wnames, (w_branch_hg, w_branch_attn, w_out, w_ff1, w_ff2)))
    wblk = {k: wsrc[k][0].astype(BF16) for k in wnames}
    wf = {}

    (h,) = _rowwise(lambda xv, g, sh, sc: ((_modnorm(xv, g, sh, sc),), ()), [(x2, D, 0)], [norm1_g, shift1, scale1],
                    [(D, BF16)], [], "norm1")
    o4, oa = 4 * HGW, 4 * HGW + ATW + 2 * KVW
    r1, r2, ro = wblk["ff1"].shape[0], wblk["ff2"].shape[0], wblk["out"].shape[0]
    cm = _Comm()
    hs = {k: _ag_ici(cm, wblk[k], waxis[k]) for k in ("bhg", "bat")}
    hs["out"] = _ag_ici(cm, wblk["out"], waxis["out"], rows=(0, ro // 2))
    p4 = _mm(h, w_in_full, "nn", F32, "proj_hg", n=o4, comm=cm)
    half = {k: cm.result(hs[k]) for k in hs}
    cm = _Comm()
    hs = {"out": _ag_ici(cm, wblk["out"], waxis["out"], rows=(ro // 2, ro), into=half["out"])}
    pa = _mm(h, w_in_full, "nn", F32, "proj_at", b_off=o4, n=oa - o4, comm=cm)
    half["out"] = cm.result(hs["out"])
    cm = _Comm()
    hs = {k: _ag_d2d(cm, half[k], waxis[k]) for k in ("bhg", "bat")}
    hs["ff2"] = _ag_ici(cm, wblk["ff2"], waxis["ff2"], rows=(0, r2 // 4))
    pg = _mm(h, w_in_full, "nn", F32, "proj_gate", b_off=oa, n=INW - oa, comm=cm)
    wf["bhg"], wf["bat"], half["ff2"] = (cm.result(hs[k]) for k in ("bhg", "bat", "ff2"))

    cm = _Comm()
    hs = {"out": _ag_d2d(cm, half["out"], waxis["out"]), "ff1": _ag_ici(cm, wblk["ff1"], waxis["ff1"], rows=(0, r1 // 2))}
    o_hg, s_all = _hgrn_fwd(p4, hg_lb_logits, hg_out_norm_g, H, comm=cm)
    wf["out"], half["ff1"] = cm.result(hs["out"]), cm.result(hs["ff1"])

    bucket = _bucket_ids()
    (bias_flat,) = _whole(lambda tb, bk: (_dot(tb, _onehot(bk), TN, precision=HIGHEST),), [rel_bias_table, bucket],
                          [((AH, AT_BLOCK * 2 * AT_BLOCK), F32)], "bias_fwd")
    bias = bias_flat.reshape(AH, AT_BLOCK, 2 * AT_BLOCK)
    q_t = _heads_first(pa[:, :ATW], AH)
    kp = _heads_first(pa[:, ATW:ATW + KVW], KVH)
    vp = _heads_first(pa[:, ATW + KVW:], KVH)
    sinks3 = attn_sinks.reshape(KVH, G, 1)
    cm = _Comm()
    hs = {"ff1": _ag_ici(cm, wblk["ff1"], waxis["ff1"], rows=(r1 // 2, r1), into=half["ff1"])}
    o_at = _heads_last(_attn_fwd(q_t, kp, vp, q_norm_g, k_norm_g, sinks3, bias, KVH, comm=cm))
    half["ff1"] = cm.result(hs["ff1"])

    bh = _mm(o_hg, wf["bhg"], "nn", F32, "branch_hg")
    ba = _mm(o_at, wf["bat"], "nn", F32, "branch_at")

    def merge_fn(bhv, bav, ghg, gat):
        return jax.nn.sigmoid(ghg) * bhv + jax.nn.sigmoid(gat) * bav

    cm = _Comm()
    hs = {"ff1": _ag_d2d(cm, half["ff1"], waxis["ff1"])}
    (merged,) = _rowwise(lambda *a: ((merge_fn(*a),), ()), [(bh, D, 0), (ba, D, 0), (pg, D, 0), (pg, D, 1)], [],
                         [(D, BF16)], [], "merge", comm=cm)
    wf["ff1"] = cm.result(hs["ff1"])
    cm = _Comm()
    hs = {"ff2": _ag_ici(cm, wblk["ff2"], waxis["ff2"], rows=(r2 // 4, 3 * r2 // 8), into=half["ff2"])}
    mo = _mm(merged, wf["out"], "nn", F32, "out_proj", comm=cm)
    half["ff2"] = cm.result(hs["ff2"])

    def resid1(xv, mov, g1, g2n, sh, sc):
        x1v = xv + g1 * mov
        return (x1v, _modnorm(x1v, g2n, sh, sc)), ()

    cm = _Comm()
    hs = {"ff2": _ag_ici(cm, wblk["ff2"], waxis["ff2"], rows=(3 * r2 // 8, r2 // 2), into=half["ff2"])}
    x1, h2 = _rowwise(resid1, [(x2, D, 0), (mo, D, 0)], [gate1, norm2_g, shift2, scale2], [(D, F32), (D, BF16)], [], "resid1",
                      comm=cm)
    half["ff2"] = cm.result(hs["ff2"])
    cm = _Comm()
    hs = {"ff2": _ag_ici(cm, wblk["ff2"], waxis["ff2"], rows=(r2 // 2, r2), into=half["ff2"])}
    u, act = _mm(h2, wf["ff1"], "nn", (F32, BF16), "ff1", comm=cm, epi=lambda r: (r, jnp.square(jnp.maximum(r, 0.0))))
    half["ff2"] = cm.result(hs["ff2"])
    cm = _Comm()
    hs = {"ff2": _ag_d2d(cm, half["ff2"], waxis["ff2"])}
    _call(lambda: None, [], name="ag_d2d_ff2", out_shape=(), comm=cm)
    wf["ff2"] = cm.result(hs["ff2"])
    ff = _mm(act, wf["ff2"], "nn", F32, "ff2")

    def loss_fn(x1v, ffv, tv, g2):
        e = x1v + g2 * ffv - tv
        dy = e * (1.0 / D)
        return (dy, dy * g2), (jnp.sum(e * e, axis=0, keepdims=True), jnp.sum(dy * ffv, axis=0, keepdims=True))

    dy, d_ff, sq_sum, d_gate2 = _rowwise(loss_fn, [(x1, D, 0), (ff, D, 0), (tgt, D, 0)], [gate2],
                                         [(D, F32), (D, BF16)], [(1, D), (1, D)], "loss")
    loss = lax.psum(jnp.sum(sq_sum) * (0.5 / D), ("x", "y", "c"))

    owner_base = jnp.stack([me ^ r for r in CHIP_RELS]).astype(jnp.int32)
    gw, recv1, part, recv2 = {}, {}, {}, {}
    gw["ff2"] = _mm(act, d_ff, "tn", BF16, "dw_ff2")
    cm = _Comm()
    hh = _rs_d2d(cm, gw["ff2"], waxis["ff2"])
    d_u = _mm(d_ff, wf["ff2"], "nt", BF16, "d_act", comm=cm, extras=[u], epi=lambda r, uv: (r * (2.0 * jnp.maximum(uv, 0.0)),))
    part["ff2"] = _rs_add(gw["ff2"], cm.result(hh), waxis["ff2"], owner_base, "rs_add_ff2")
    rows_ff2 = part["ff2"].shape[1]
    cm = _Comm()
    hh = _rs_ici(cm, part["ff2"], rows=(0, rows_ff2 // 2))
    gw["ff1"] = _mm(h2, d_u, "tn", BF16, "dw_ff1", comm=cm)
    cm2 = _Comm()
    hh2 = _rs_ici(cm2, part["ff2"], rows=(rows_ff2 // 2, rows_ff2), recv=cm.result(hh))
    hh1 = _rs_d2d(cm2, gw["ff1"], waxis["ff1"])
    d_h2 = _mm(d_u, wf["ff1"], "nt", F32, "d_h2", comm=cm2)
    recv2["ff2"] = cm2.result(hh2)
    part["ff1"] = _rs_add(gw["ff1"], cm2.result(hh1), waxis["ff1"], owner_base, "rs_add_ff1")

    def norm2_bwd(dh2v, x1v, dyv, mov, g2n, sh, sc, g1):
        _, vjp = jax.vjp(_modnorm, x1v, g2n, sh, sc)
        dx, dg, dsh, dsc = vjp(dh2v)
        dx1 = dyv + dx
        return (dx1, dx1 * g1), (dg, dsh, dsc, jnp.sum(dx1 * mov, axis=0, keepdims=True))

    d_x1, d_mo, d_g2n, d_shift2, d_scale2, d_gate1 = _rowwise(
        norm2_bwd, [(d_h2, D, 0), (x1, D, 0), (dy, D, 0), (mo, D, 0)], [norm2_g, shift2, scale2, gate1],
        [(D, F32), (D, BF16)], [(1, D)] * 4, "norm2_bwd")
    gw["out"] = _mm(merged, d_mo, "tn", BF16, "dw_out")
    d_merged = _mm(d_mo, wf["out"], "nt", F32, "d_merged")

    def merge_bwd(dmv, bhv, bav, ghg, gat):
        _, vjp = jax.vjp(merge_fn, bhv, bav, ghg, gat)
        return vjp(dmv), ()

    d_bh, d_ba, d_ghg, d_gat = _rowwise(merge_bwd, [(d_merged, D, 0), (bh, D, 0), (ba, D, 0), (pg, D, 0), (pg, D, 1)], [],
                                        [(D, BF16)] * 4, [], "merge_bwd")
    gw["bhg"] = _mm(o_hg, d_bh, "tn", BF16, "dw_bhg")
    gw["bat"] = _mm(o_at, d_ba, "tn", BF16, "dw_bat")
    d_ohg = _mm(d_bh, wf["bhg"], "nt", F32, "d_ohg")
    d_oat = _mm(d_ba, wf["bat"], "nt", BF16, "d_oat")
    rows_ff1 = part["ff1"].shape[1]
    cm = _Comm()
    hf1 = _rs_ici(cm, part["ff1"], rows=(0, rows_ff1 // 2))
    d_hq, d_hf, d_hi, d_hg, d_lb, d_gout_h = _hgrn_bwd(p4, hg_lb_logits, hg_out_norm_g, s_all, d_ohg, H, comm=cm)
    cm2 = _Comm()
    hf1 = _rs_ici(cm2, part["ff1"], rows=(rows_ff1 // 2, rows_ff1), recv=cm.result(hf1))
    hh = {k: _rs_d2d(cm2, gw[k], waxis[k]) for k in small}
    dq_t, dkp, dvp, d_qg, d_kg, d_sk, d_bias = _attn_bwd(q_t, kp, vp, q_norm_g, k_norm_g, sinks3, bias,
                                                         _heads_first(d_oat, AH), KVH, comm=cm2)
    recv2["ff1"] = cm2.result(hf1)
    for k in small:
        part[k] = _rs_add(gw[k], cm2.result(hh[k]), waxis[k], owner_base, "rs_add_" + k)
    d_aq = _heads_last(dq_t)
    d_ak = _heads_last(dkp).astype(BF16)
    d_av = _heads_last(dvp).astype(BF16)
    d_proj = jnp.concatenate([d_hq, d_hf, d_hi, d_hg, d_aq, d_ak, d_av, d_ghg, d_gat], axis=1)
    cm = _Comm()
    hh = {k: _rs_ici(cm, part[k]) for k in small}
    gw_in = _mm(h, d_proj, "tn", BF16, "dw_in", comm=cm)
    for k in small:
        recv2[k] = cm.result(hh[k])

    wm = LANES * A
    cm = _Comm()
    hi_ = cm.inp(gw_in)
    h_main, h_mid = cm.out((4, D, wm), BF16), cm.out((4, D, LANES), BF16)
    for i, r in enumerate(CHIP_RELS):
        def main_view(ref, p, r=r):
            o = p["me"] ^ r ^ 1
            return ref.at[:, pl.ds(pl.multiple_of((PAIR * (o // 2) + (A + 1) * (1 - p["c"])) * LANES, LANES), wm)]

        def mid_view(ref, p, r=r):
            o = p["me"] ^ r
            return ref.at[:, pl.ds(pl.multiple_of((PAIR * (o // 2) + A) * LANES, LANES), LANES)]

        cm.copy(hi_, main_view, h_main, _slot_view(i), 1)
        cm.copy(hi_, mid_view, h_mid, _slot_view(i), 1)
    _call(lambda: None, [], name="rs_d2d_in", out_shape=(), comm=cm)
    chip = jnp.stack([(me ^ r) // 2 for r in CHIP_RELS]).astype(jnp.int32)
    part_main = _rs_add(gw_in, cm.result(h_main), 1, PAIR * chip + (A + 1) * cc, "rs_add_in_main", tw=LANES)
    part_mid = _rs_add(gw_in, cm.result(h_mid), 1, PAIR * chip + A, "rs_add_in_mid", tw=LANES)
    rs_in = _rs_split_start([part_main, part_mid], "rs_in_start")
    d_h = _mm(d_proj, w_in_full, "nt", F32, "d_h", tn=D, after=[rs_in["token"]])

    def norm1_bwd(dhv, xv, dx1v, g1n, sh, sc):
        _, vjp = jax.vjp(_modnorm, xv, g1n, sh, sc)
        dx, dg, dsh, dsc = vjp(dhv)
        return (dx1v + dx,), (dg, dsh, dsc)

    grad_x, d_g1n, d_shift1, d_scale1 = _rowwise(norm1_bwd, [(d_h, D, 0), (x2, D, 0), (d_x1, D, 0)],
                                                 [norm1_g, shift1, scale1], [(D, F32)], [(1, D)] * 3, "norm1_bwd")

    def sum4(p0, p1, p2, p3):
        return ((p0.astype(F32) + p1.astype(F32)) + p2.astype(F32)) + p3.astype(F32)

    def update_fn(w, m, v, p0, p1, p2, p3):
        g = sum4(p0, p1, p2, p3)
        delta, mn, vn = _adamw(w, g, m, v)
        return (g, delta, mn, vn), ()

    wmv = dict(zip(wnames, ((w_branch_hg, m_w_branch_hg, v_w_branch_hg), (w_branch_attn, m_w_branch_attn, v_w_branch_attn),
                            (w_out, m_w_out, v_w_out), (w_ff1, m_w_ff1, v_w_ff1), (w_ff2, m_w_ff2, v_w_ff2))))
    res = {}

    def update(k, p, rx):
        w, m, v = (t[0] for t in wmv[k])
        n = w.shape[1]
        ins = [(t, n, 0) for t in (w, m, v)] + [(p, n, 0, 0)] + [(rx, n, 0, i) for i in range(3)]
        res[k] = [t[None] for t in _rowwise(update_fn, ins, [], [(n, F32)] * 4, [], "update_" + k)]

    for k in wnames:
        update(k, part[k], recv2[k])
    (part_main, part_mid), (rx_main, rx_mid) = _rs_split_wait(rs_in, [grad_x] + [res[k][0] for k in wnames], "rs_in_wait")
    g_main, = _rowwise(lambda *p: ((sum4(*p),), ()), [(part_main, wm, 0, 0)] + [(rx_main, wm, 0, i) for i in range(3)], [],
                       [(wm, F32)], [], "sum_in_main")
    g_mid, = _rowwise(lambda *p: ((sum4(*p),), ()), [(part_mid, LANES, 0, 0)] + [(rx_mid, LANES, 0, i) for i in range(3)], [],
                      [(LANES, F32)], [], "sum_in_mid")
    g_in = jnp.where(cc == 0, jnp.concatenate([g_main, g_mid[:, :LANES // 2]], axis=1),
                     jnp.concatenate([g_mid[:, LANES // 2:], g_main], axis=1))

    def update_given(w, m, v, g):
        delta, mn, vn = _adamw(w, g, m, v)
        return (g, delta, mn, vn), ()

    res["in"] = [t[None] for t in _rowwise(update_given, [(t, BW, 0) for t in (w_in[0], m_w_in[0], v_w_in[0], g_in)], [],
                                           [(BW, F32)] * 4, [], "update_in")]

    d_sinks = d_sk.reshape(1, AH)
    (d_table_t,) = _whole(lambda db, bk: (_dot(db, _onehot(bk), NT, precision=HIGHEST),),
                          [d_bias.reshape(AH, AT_BLOCK * 2 * AT_BLOCK), bucket], [((AH, N_BUCKETS), F32)], "bias_bwd")
    smalls = [d_g1n, d_g2n, d_lb, d_gout_h, d_qg, d_kg, d_sinks, d_table_t.T.reshape(1, N_BUCKETS * AH)]
    widths = [s.shape[1] for s in smalls]
    lanes = [-(-w // LANES) * LANES for w in widths]
    smalls = [jnp.pad(s, ((0, 0), (0, p - w))) for s, w, p in zip(smalls, widths, lanes)]
    tail_row = jnp.concatenate([d_shift1, d_scale1, d_gate1, d_shift2, d_scale2, d_gate2] + smalls, axis=1)
    (tail_row,) = _behind([tail_row], [g_mid])
    tail_all = _gather_small(tail_row, me, "gather_tail")[:, 0, :]
    d_ada_all, packed = tail_all[:, :6 * D], tail_all[:, 6 * D:]
    d_ada_cols = lax.dynamic_slice(d_ada_all, (0, me * ADA_N), (N_DEV, ADA_N))

    def ada_update(cv, dav, w, m, v):
        g = _bdot(_silu(cv), dav, TN)
        delta, mn, vn = _adamw(w, g, m, v)
        return (g, delta, mn, vn), ()

    res["ada"] = [t[None] for t in _ada_update_call(ada_update, c_all, d_ada_cols, w_ada[0], m_w_ada[0], v_w_ada[0], _tile(D, 256, 16))]

    offs = [sum(lanes[:i]) for i in range(len(lanes))]

    def small_update(pk, dada, lg, *wmv_flat):
        tot = pk[0:1]
        for d in range(1, N_DEV):
            tot = tot + pk[d:d + 1]
        gb = dada[0:1]
        for d in range(1, N_DEV):
            gb = gb + dada[d:d + 1]
        gs = [tot[:, offs[i]:offs[i] + widths[i]] for i in range(len(widths))]
        _, lb_vjp = jax.vjp(_softmax0, lg)
        (g_lg,) = lb_vjp(gs[2])
        grads = [gb, gs[0], gs[1], g_lg, gs[3], gs[4], gs[5], gs[6], gs[7]]
        outs = []
        for i, g in enumerate(grads):
            w, m, v = wmv_flat[3 * i:3 * i + 3]
            delta, mn, vn = _adamw(w, g, m, v)
            outs += [g, delta, mn, vn]
        return tuple(outs)

    tbl = lambda t: t.reshape(1, N_BUCKETS * AH)
    small_wmv = [(b_ada, m_b_ada, v_b_ada), (norm1_g, m_norm1_g, v_norm1_g), (norm2_g, m_norm2_g, v_norm2_g),
                 (hg_lb_logits, m_hg_lb_logits, v_hg_lb_logits), (hg_out_norm_g, m_hg_out_norm_g, v_hg_out_norm_g),
                 (q_norm_g, m_q_norm_g, v_q_norm_g), (k_norm_g, m_k_norm_g, v_k_norm_g),
                 (attn_sinks, m_attn_sinks, v_attn_sinks),
                 (tbl(rel_bias_table), tbl(m_rel_bias_table), tbl(v_rel_bias_table))]
    flat = [t for trip in small_wmv for t in trip]
    out_shapes = [(trip[0].shape, F32) for trip in small_wmv for _ in range(4)]
    sres = _whole(small_update, [packed, d_ada_all, hg_lb_logits] + flat, out_shapes, "small_update")
    names_small = ("b_ada", "norm1_g", "norm2_g", "lb", "gout", "qg", "kg", "sinks", "table")
    for i, k in enumerate(names_small):
        r = sres[4 * i:4 * i + 4]
        if k == "table":
            r = [t.reshape(N_BUCKETS, AH) for t in r]
        res[k] = r

    order = ("ada", "b_ada", "norm1_g", "norm2_g", "in", "lb", "gout", "qg", "kg", "sinks", "table", "bhg", "bat", "out", "ff1", "ff2")
    outs = [loss, grad_x[None]]
    for j in range(4):
        outs += [res[k][j] for k in order]
    return tuple(outs)
```

```python
import functools
import math

import jax
import jax.numpy as jnp
from jax import lax
from jax.experimental import pallas as pl
from jax.experimental.pallas import tpu as pltpu

F32 = jnp.float32
BF16 = jnp.bfloat16
EPS = 1e-6
NEG_INF = -1e30
HG_DK = 128
HG_CHUNK = 64
AT_BLOCK = 128
N_BUCKETS = 32
MAX_EXACT = 16
MAX_DISTANCE = 128
N_DEV = 8
LANES = 128
VMEM_LIMIT = 56 * 1024 * 1024
ADAM_LR, ADAM_B1, ADAM_B2, ADAM_EPS, ADAM_WD, ADAM_STEP = 0.001, 0.9, 0.999, 1e-08, 0.01, 10
HIGHEST = lax.Precision.HIGHEST
MESH = pl.DeviceIdType.MESH
ANY = pl.BlockSpec(memory_space=pl.ANY)
CHIP_RELS = (0, 4, 2, 6)

NN = (((1,), (0,)), ((), ()))
NT = (((1,), (1,)), ((), ()))
TN = (((0,), (0,)), ((), ()))


def _tile(n, pref, unit):
    if n <= pref:
        return n
    t = (pref // unit) * unit
    while t >= unit:
        if n % t == 0:
            return t
        t -= unit
    return n


def _dot(a, b, dn, precision=None):
    return lax.dot_general(a, b, dn, preferred_element_type=F32, precision=precision)


def _bdot(a, b, dn):
    return _dot(a.astype(BF16), b.astype(BF16), dn)


def _position():
    x, y, c = lax.axis_index("x"), lax.axis_index("y"), lax.axis_index("c")
    return dict(x=x, y=y, c=c, me=4 * x + 2 * y + c)


def _peer_position(p, rel):
    x = 1 - p["x"] if rel & 4 else p["x"]
    y = 1 - p["y"] if rel & 2 else p["y"]
    c = 1 - p["c"] if rel & 1 else p["c"]
    return dict(x=x, y=y, c=c, me=4 * x + 2 * y + c)


class _Comm:
    def __init__(self):
        self.ins, self.outs, self.alias, self.plans, self.res = [], [], {}, [], None

    def inp(self, arr):
        self.ins.append(arr)
        return ("i", len(self.ins) - 1)

    def out(self, shape, dtype, alias=None):
        self.outs.append(jax.ShapeDtypeStruct(tuple(shape), dtype))
        if alias is not None:
            self.alias[alias[1]] = len(self.outs) - 1
        return ("o", len(self.outs) - 1)

    def copy(self, src, src_view, dst, dst_view, rel):
        self.plans.append((src, src_view, dst, dst_view, rel))

    def result(self, handle):
        return self.res[handle[1]]

    def build(self, in_refs, out_refs, send_sems, recv_sems):
        pos = _position()
        ref = lambda h: in_refs[h[1]] if h[0] == "i" else out_refs[h[1]]
        ops = []
        for k, (src, sv, dst, dv, rel) in enumerate(self.plans):
            s = sv(ref(src), pos)
            if rel == 0:
                cp = pltpu.make_async_copy(s, dv(ref(dst), pos), send_sems.at[k])
                ops.append((cp.start, cp.wait))
                continue
            peer = _peer_position(pos, rel)
            mk = lambda d: pltpu.make_async_remote_copy(
                src_ref=s, dst_ref=d, send_sem=send_sems.at[k], recv_sem=recv_sems.at[k],
                device_id=(peer["x"], peer["y"], peer["c"]), device_id_type=MESH)
            out_cp, in_cp = mk(dv(ref(dst), pos)), mk(dv(ref(dst), peer))

            def wait(out_cp=out_cp, in_cp=in_cp):
                out_cp.wait_send()
                in_cp.wait_recv()

            ops.append((out_cp.start, wait))
        return ops


def _call(body, args, *, name, out_shape, in_specs=None, out_specs=None, grid=None, scratch_shapes=(), comm=None,
          prefetch=None, aliases=None, after=()):
    single = not isinstance(out_shape, (tuple, list))
    out_shape = (out_shape,) if single else tuple(out_shape)
    n_in, n_out, n_scr = len(args), len(out_shape), len(scratch_shapes)
    vm = pl.BlockSpec(memory_space=pltpu.VMEM)
    in_specs = [vm] * n_in if in_specs is None else list(in_specs)
    out_specs = [vm] * n_out if out_specs is None else (list(out_specs) if isinstance(out_specs, (tuple, list)) else [out_specs])
    n_pf = 0 if prefetch is None else len(prefetch)
    kw = {} if aliases is None else {"input_output_aliases": dict(aliases)}
    if comm is None and after:
        n_dep = len(after)

        def fn(*refs):
            body(*refs[:n_pf + n_in], *refs[n_pf + n_in + n_dep:])

        all_args, all_scratch = list(args) + list(after), list(scratch_shapes)
        in_specs = in_specs + [ANY] * n_dep
    elif comm is None:
        fn = body
        all_args, all_scratch = list(args), list(scratch_shapes)
    else:
        n_ci, n_co, n_x = len(comm.ins), len(comm.outs), len(comm.plans)

        def fn(*refs):
            pf, refs = refs[:n_pf], refs[n_pf:]
            o_in, c_in = refs[:n_in], refs[n_in:n_in + n_ci]
            o_out = refs[n_in + n_ci:n_in + n_ci + n_out]
            c_out = refs[n_in + n_ci + n_out:n_in + n_ci + n_out + n_co]
            scr = refs[n_in + n_ci + n_out + n_co:]
            ops = comm.build(c_in, c_out, scr[n_scr], scr[n_scr + 1])
            if grid:
                first = functools.reduce(jnp.logical_and, [pl.program_id(i) == 0 for i in range(len(grid))])
                last = functools.reduce(jnp.logical_and, [pl.program_id(i) == g - 1 for i, g in enumerate(grid)])

                @pl.when(first)
                def _():
                    for start, _w in ops:
                        start()
            else:
                for start, _w in ops:
                    start()
            body(*pf, *o_in, *o_out, *scr[:n_scr])
            if grid:
                @pl.when(last)
                def _():
                    for _s, wait in ops:
                        wait()
            else:
                for _s, wait in ops:
                    wait()

        all_args = list(args) + list(comm.ins)
        in_specs = in_specs + [ANY] * n_ci
        out_shape = out_shape + tuple(comm.outs)
        out_specs = out_specs + [ANY] * n_co
        all_scratch = list(scratch_shapes) + [pltpu.SemaphoreType.DMA((n_x,)), pltpu.SemaphoreType.DMA((n_x,))]
        kw["input_output_aliases"] = {n_pf + n_in + i: n_out + o for i, o in comm.alias.items()}
    sem = None if grid is None else ("arbitrary",) * len(grid)
    params = pltpu.CompilerParams(dimension_semantics=sem, vmem_limit_bytes=VMEM_LIMIT)
    if prefetch is None:
        spec = dict(in_specs=in_specs, out_specs=tuple(out_specs), scratch_shapes=all_scratch)
        if grid is not None:
            spec["grid"] = grid
    else:
        spec = dict(grid_spec=pltpu.PrefetchScalarGridSpec(
            num_scalar_prefetch=n_pf, grid=grid, in_specs=in_specs, out_specs=tuple(out_specs), scratch_shapes=all_scratch))
        all_args = list(prefetch) + all_args
    res = pl.pallas_call(fn, name=name, out_shape=out_shape, compiler_params=params, **spec, **kw)(*all_args)
    res = list(res)
    if comm is not None:
        comm.res = res[n_out:]
        res = res[:n_out]
    return res[0] if single else res


def _whole_view(ref, pos):
    return ref


def _block_view(axis, n, index, rows=None):
    def view(ref, pos):
        off = pl.multiple_of(index(pos) * n, n)
        if rows is None:
            return ref.at[:, pl.ds(off, n)] if axis == 1 else ref.at[pl.ds(off, n), :]
        lo, cnt = rows[0], rows[1] - rows[0]
        if axis == 1:
            return ref.at[pl.ds(lo, cnt), pl.ds(off, n)]
        return ref.at[pl.ds(pl.multiple_of(off + lo, 16), cnt), :]
    return view


def _rows_view(rows):
    def view(ref, pos):
        return ref if rows is None else ref.at[pl.ds(rows[0], rows[1] - rows[0]), :]
    return view


def _slot_view(i, rows=None):
    def view(ref, pos):
        return ref.at[i] if rows is None else ref.at[i, pl.ds(rows[0], rows[1] - rows[0]), :]
    return view


def _exchange(items, name):
    cm = _Comm()
    for a, rel in items:
        cm.copy(cm.inp(a), _whole_view, cm.out(a.shape, a.dtype), _whole_view, rel)
    _call(lambda: None, [], name=name, out_shape=(), comm=cm)
    return cm.res


def _gather_small(v, me, name):
    cm = _Comm()
    hi, ho = cm.inp(v), cm.out((N_DEV,) + v.shape, v.dtype)
    for rel in range(N_DEV):
        cm.copy(hi, _whole_view, ho, lambda ref, p: ref.at[p["me"]], rel)
    _call(lambda: None, [], name=name, out_shape=(), comm=cm)
    return cm.result(ho)


def _ag_ici(cm, blk, axis, rows=None, into=None):
    n = blk.shape[axis]
    shape = list(blk.shape)
    shape[axis] = n * N_DEV
    hi = cm.inp(blk)
    ho = cm.out(shape, blk.dtype) if into is None else cm.out(shape, blk.dtype, alias=cm.inp(into))
    own = _block_view(axis, n, lambda p: p["me"], rows)
    for rel in CHIP_RELS:
        cm.copy(hi, _rows_view(rows), ho, own, rel)
    return ho


def _ag_d2d(cm, full, axis):
    n = full.shape[axis] // N_DEV
    hi = cm.inp(full)
    ho = cm.out(full.shape, full.dtype, alias=hi)
    for r in CHIP_RELS:
        v = _block_view(axis, n, functools.partial(lambda p, r: p["me"] ^ r, r=r))
        cm.copy(hi, v, ho, v, 1)
    return ho


def _rs_d2d(cm, gw, axis):
    n = gw.shape[axis] // N_DEV
    shape = list(gw.shape)
    shape[axis] = n
    hi, ho = cm.inp(gw), cm.out([4] + shape, gw.dtype)
    for i, r in enumerate(CHIP_RELS):
        cm.copy(hi, _block_view(axis, n, functools.partial(lambda p, r: p["me"] ^ r ^ 1, r=r)), ho, _slot_view(i), 1)
    return ho


def _rs_ici(cm, part, rows=None, recv=None):
    if recv is None:
        ho = cm.out((3,) + part.shape[1:], part.dtype)
    else:
        ho = cm.out(recv.shape, recv.dtype, alias=cm.inp(recv))
    hi = cm.inp(part)
    for i in (1, 2, 3):
        cm.copy(hi, _slot_view(i, rows), ho, _slot_view(i - 1, rows), CHIP_RELS[i])
    return ho


def _rs_add(gw, recv, axis, base, name, tw=None):
    _, R, n = recv.shape
    fan = 1
    if axis == 1:
        tw = n if tw is None else tw
        fan = max(f for f in (4, 3, 2, 1) if (n // tw) % f == 0)
        gw_specs = [pl.BlockSpec((R, tw), functools.partial(lambda i, t, b, k: (0, b[i] + fan * t + k), k=k)) for k in range(fan)]
        rv_spec = pl.BlockSpec((None, R, tw * fan), lambda i, t, b: (i, 0, t))
        grid = (4, n // (tw * fan))
    else:
        tw = _tile(n, 1024, LANES)
        gw_specs = [pl.BlockSpec((R, tw), lambda i, t, b: (b[i], t))]
        rv_spec = pl.BlockSpec((None, R, tw), lambda i, t, b: (i, 0, t))
        grid = (4, n // tw)

    def body(b_ref, *refs):
        g_refs, r_ref, o_ref = refs[:fan], refs[fan], refs[fan + 1]
        g = g_refs[0][...] if fan == 1 else jnp.concatenate([g[...] for g in g_refs], axis=1)
        o_ref[...] = (g.astype(F32) + r_ref[...].astype(F32)).astype(o_ref.dtype)

    return _call(body, [gw] * fan + [recv], name=name, out_shape=jax.ShapeDtypeStruct(recv.shape, recv.dtype), grid=grid,
                 in_specs=gw_specs + [rv_spec], out_specs=rv_spec, prefetch=[base])


HBM_SPEC = pl.BlockSpec(memory_space=pltpu.HBM)
SEM_SPEC = pl.BlockSpec(memory_space=pltpu.SEMAPHORE)
SPLIT_PARAMS = pltpu.CompilerParams(has_side_effects=pltpu.SideEffectType.DATAFLOW_SIDE_EFFECTING)


def _split_copies(refs, plans, send_sems, recv_sems):
    pos = _position()
    out = []
    for k, (si, sv, li, lv, rel) in enumerate(plans):
        peer = _peer_position(pos, rel)
        mk = lambda d: pltpu.make_async_remote_copy(
            src_ref=sv(refs[si], pos), dst_ref=d, send_sem=send_sems.at[k], recv_sem=recv_sems.at[k],
            device_id=(peer["x"], peer["y"], peer["c"]), device_id_type=MESH)
        out.append((mk(lv(refs[li], pos)), mk(lv(refs[li], peer))))
    return out


def _split_start(arrays, plans, name):
    n = len(arrays)

    def body(*refs):
        send_sems, recv_sems = refs[n], refs[n + 1]
        for out_cp, _ in _split_copies(refs[:n], plans, send_sems, recv_sems):
            out_cp.start()
        refs[-1][...] = jnp.zeros_like(refs[-1])

    sems = pltpu.SemaphoreType.DMA((len(plans),))
    res = pl.pallas_call(
        body, name=name,
        out_shape=(sems, sems) + tuple(pltpu.HBM(a.shape, a.dtype) for a in arrays) + (jax.ShapeDtypeStruct((8, LANES), F32),),
        in_specs=[HBM_SPEC] * n, out_specs=(SEM_SPEC, SEM_SPEC) + (HBM_SPEC,) * n + (pl.BlockSpec(memory_space=pltpu.VMEM),),
        input_output_aliases={i: 2 + i for i in range(n)}, compiler_params=SPLIT_PARAMS,
    )(*[pltpu.with_memory_space_constraint(a, pltpu.HBM) for a in arrays])
    return res[0], res[1], list(res[2:2 + n]), res[-1]


def _split_wait(send_sems, recv_sems, arrays, plans, after, name):
    n, na = len(arrays), len(after)

    def body(*refs):
        for out_cp, in_cp in _split_copies(refs[:n], plans, refs[n], refs[n + 1]):
            out_cp.wait_send()
            in_cp.wait_recv()

    res = pl.pallas_call(
        body, name=name, out_shape=tuple(pltpu.HBM(a.shape, a.dtype) for a in arrays),
        in_specs=[HBM_SPEC] * n + [SEM_SPEC, SEM_SPEC] + [ANY] * na, out_specs=(HBM_SPEC,) * n,
        input_output_aliases={i: i for i in range(n)}, compiler_params=SPLIT_PARAMS,
    )(*arrays, send_sems, recv_sems, *after)
    return list(res)


def _rs_split_start(parts, name):
    nw = len(parts)
    lands = [lax.empty((3,) + p.shape[1:], p.dtype) for p in parts]
    plans = [(s, _slot_view(i), nw + s, _slot_view(i - 1), CHIP_RELS[i]) for s in range(nw) for i in (1, 2, 3)]
    send_sems, recv_sems, arrays, token = _split_start(list(parts) + lands, plans, name)
    return dict(sems=(send_sems, recv_sems), arrays=arrays, plans=plans, token=token, nw=nw)


def _rs_split_wait(h, after, name):
    arrays = _split_wait(h["sems"][0], h["sems"][1], h["arrays"], h["plans"], after, name)
    return arrays[:h["nw"]], arrays[h["nw"]:]


def _behind(xs, tokens):
    out = lax.optimization_barrier((tuple(xs), tuple(tokens)))
    return list(out[0])


def _ag_w_in(src, a, D, INW):
    wm = LANES * a

    hd = D // 2
    ALL, TOP, BOT = (0, D), (0, hd), (hd, D)

    def main_place(ref, p, rows=ALL):
        off = pl.multiple_of(((2 * a + 1) * (p["me"] // 2) + (a + 1) * p["c"]) * LANES, LANES)
        return ref.at[pl.ds(rows[0], rows[1] - rows[0]), pl.ds(off, wm)]

    def main_src(ref, p):
        return ref.at[:, pl.ds(pl.multiple_of(p["c"] * LANES, LANES), wm)]

    def mid_src(ref, p):
        return ref.at[:, pl.ds(pl.multiple_of((1 - p["c"]) * wm, LANES), LANES)]

    def mid_place(ref, p, rows=ALL):
        return ref.at[p["me"], pl.ds(rows[0], rows[1] - rows[0]), :]

    def body(src_ref, full_ref, mid_ref, send_sems, recv_sems):
        pos = _position()
        sib, xn, yn = (_peer_position(pos, r) for r in (1, 4, 2))
        dg = _peer_position(pos, 6)
        started = []

        def remote(k, s, d, to):
            return pltpu.make_async_remote_copy(src_ref=s, dst_ref=d, send_sem=send_sems.at[k], recv_sem=recv_sems.at[k],
                                                device_id=(to["x"], to["y"], to["c"]), device_id_type=MESH)

        def send(k, owner, rows, to, from_src=False):
            for j, (src_v, place) in enumerate(((main_src, main_place), (mid_src, mid_place))):
                s = src_v(src_ref, pos) if from_src else place(full_ref if j == 0 else mid_ref, owner, rows)
                cp = remote(k + j, s, place(full_ref if j == 0 else mid_ref, owner, rows), to)
                cp.start()
                started.append(cp)

        def landed(k, owner, rows, frm):
            for j, place in enumerate((main_place, mid_place)):
                ref = full_ref if j == 0 else mid_ref
                remote(k + j, place(ref, owner, rows), place(ref, owner, rows), frm).wait_recv()

        local = [pltpu.make_async_copy(main_src(src_ref, pos), main_place(full_ref, pos), send_sems.at[18]),
                 pltpu.make_async_copy(mid_src(src_ref, pos), mid_place(mid_ref, pos), send_sems.at[19])]
        for cp in local:
            cp.start()
        send(0, pos, ALL, sib, from_src=True)
        send(2, pos, ALL, xn, from_src=True)
        send(4, pos, ALL, yn, from_src=True)
        landed(2, xn, ALL, xn)
        send(10, xn, ALL, sib)
        send(6, xn, TOP, yn)
        landed(4, yn, ALL, yn)
        send(12, yn, ALL, sib)
        send(8, yn, BOT, xn)
        landed(6, dg, TOP, yn)
        send(14, dg, TOP, sib)
        landed(8, dg, BOT, xn)
        send(16, dg, BOT, sib)
        sib_of = lambda p: _peer_position(p, 1)
        landed(0, sib, ALL, sib)
        landed(10, sib_of(xn), ALL, sib)
        landed(12, sib_of(yn), ALL, sib)
        landed(14, sib_of(dg), TOP, sib)
        landed(16, sib_of(dg), BOT, sib)
        for cp in started:
            cp.wait_send()
        for cp in local:
            cp.wait()

    return _call(body, [src], name="ag_w_in", in_specs=[ANY], out_specs=[ANY, ANY],
                 out_shape=(jax.ShapeDtypeStruct((D, INW), BF16), jax.ShapeDtypeStruct((N_DEV, D, LANES), BF16)),
                 scratch_shapes=[pltpu.SemaphoreType.DMA((20,)), pltpu.SemaphoreType.DMA((20,))])


def _patch_mid(full, mid, a):
    D = full.shape[0]

    def body(full_ref, e_ref, o_ref, out_ref):
        out_ref[...] = e_ref[...] + o_ref[...]

    return _call(body, [full, mid, mid], name="patch_mid", grid=(N_DEV // 2,),
                 out_shape=jax.ShapeDtypeStruct(full.shape, full.dtype),
                 in_specs=[ANY, pl.BlockSpec((None, D, LANES), lambda j: (2 * j, 0, 0)),
                           pl.BlockSpec((None, D, LANES), lambda j: (2 * j + 1, 0, 0))],
                 out_specs=pl.BlockSpec((D, LANES), lambda j: (0, (2 * a + 1) * j + a)), aliases={0: 0})


MM_RESIDENT = 2048


def _mm(a, b, mode, out_dtype, name, b_off=0, n=None, comm=None, extras=(), epi=None, tn=None, after=()):
    if mode == "nn":
        (M, K), (K2, N) = a.shape, b.shape
    elif mode == "nt":
        (M, K), (N, K2) = a.shape, b.shape
    else:
        (K, M), (K2, N) = a.shape, b.shape
    assert K == K2, (a.shape, b.shape, mode)
    if n is not None:
        N = n
    single = not isinstance(out_dtype, (tuple, list))
    out_dtypes = (out_dtype,) if single else tuple(out_dtype)
    if epi is None:
        epi = lambda r: (r,)
    tk = K if K <= MM_RESIDENT else (MM_RESIDENT if K % MM_RESIDENT == 0 else _tile(K, 512, LANES))
    nk = K // tk
    if M > MM_RESIDENT and mode == "tn" and N <= MM_RESIDENT and not b_off:
        tm, tn = _tile(M, 512, LANES), N
    elif nk > 1:
        tm, tn = _tile(M, 1024, LANES), _tile(N, tn or 1024, LANES)
    else:
        tm = _tile(M, MM_RESIDENT, LANES)
        tn = _tile(math.gcd(N, b_off) if b_off else N, tn or 512, LANES)
    jb = b_off // tn
    dn = {"nn": NN, "nt": NT, "tn": TN}[mode]
    ne, no = len(extras), len(out_dtypes)

    def body(a_ref, b_ref, *rest):
        e_refs, o_refs = rest[:ne], rest[ne:ne + no]

        def finish(r):
            for o_ref, v in zip(o_refs, epi(r, *[e[...] for e in e_refs])):
                o_ref[...] = v.astype(o_ref.dtype)

        if nk == 1:
            finish(_bdot(a_ref[...], b_ref[...], dn))
            return
        acc_ref = rest[ne + no]
        k = pl.program_id(2)

        @pl.when(k == 0)
        def _():
            acc_ref[...] = _bdot(a_ref[...], b_ref[...], dn)

        @pl.when(jnp.logical_and(k > 0, k < nk - 1))
        def _():
            acc_ref[...] += _bdot(a_ref[...], b_ref[...], dn)

        @pl.when(k == nk - 1)
        def _():
            finish(acc_ref[...] + _bdot(a_ref[...], b_ref[...], dn))

    a_spec = pl.BlockSpec((tk, tm), lambda i, j, k: (k, i)) if mode == "tn" else pl.BlockSpec((tm, tk), lambda i, j, k: (i, k))
    b_spec = pl.BlockSpec((tn, tk), lambda i, j, k: (j, k)) if mode == "nt" else pl.BlockSpec((tk, tn), lambda i, j, k: (k, j + jb))
    o_spec = pl.BlockSpec((tm, tn), lambda i, j, k: (i, j))
    res = _call(body, [a, b] + list(extras), name=name, grid=(M // tm, N // tn, nk),
                out_shape=tuple(jax.ShapeDtypeStruct((M, N), dt) for dt in out_dtypes),
                in_specs=[a_spec, b_spec] + [o_spec] * ne, out_specs=[o_spec] * no,
                scratch_shapes=[pltpu.VMEM((tm, tn), F32)] if nk > 1 else [], comm=comm, after=after)
    return res[0] if single else res


def _rowwise(fn, row_ins, bcast_ins, row_outs, acc_outs, name, rt=256, comm=None):
    L = row_ins[0][0].shape[-2]
    rt = _tile(L, rt, 16)
    nr, nb, no = len(row_ins), len(bcast_ins), len(row_outs)

    def body(*refs):
        i = pl.program_id(0)
        vals = [r[...] for r in refs[:nr + nb]]
        outs, accs = fn(*vals)
        for r, v in zip(refs[nr + nb:nr + nb + no], outs):
            r[...] = v.astype(r.dtype)
        acc_refs = refs[nr + nb + no:]

        @pl.when(i == 0)
        def _():
            for r in acc_refs:
                r[...] = jnp.zeros_like(r)

        for r, v in zip(acc_refs, accs):
            r[...] += v

    in_specs = []
    for spec in row_ins:
        w, cb = spec[1], spec[2]
        if len(spec) == 4:
            in_specs.append(pl.BlockSpec((None, rt, w), functools.partial(lambda i, cb, ld: (ld, i, cb), cb=cb, ld=spec[3])))
        else:
            in_specs.append(pl.BlockSpec((rt, w), functools.partial(lambda i, cb: (i, cb), cb=cb)))
    in_specs += [pl.BlockSpec(b.shape, lambda i: (0, 0)) for b in bcast_ins]
    out_specs = [pl.BlockSpec((rt, w), lambda i: (i, 0)) for w, _ in row_outs]
    out_specs += [pl.BlockSpec(s, lambda i: (0, 0)) for s in acc_outs]
    out_shape = [jax.ShapeDtypeStruct((L, w), dt) for w, dt in row_outs] + [jax.ShapeDtypeStruct(s, F32) for s in acc_outs]
    return _call(body, [s[0] for s in row_ins] + list(bcast_ins), name=name, grid=(L // rt,), out_shape=tuple(out_shape),
                 in_specs=in_specs, out_specs=out_specs, comm=comm)


def _whole(fn, ins, out_shapes, name):
    def body(*refs):
        outs = fn(*[r[...] for r in refs[:len(ins)]])
        for r, v in zip(refs[len(ins):], outs):
            r[...] = v.astype(r.dtype)

    return _call(body, list(ins), name=name, out_shape=tuple(jax.ShapeDtypeStruct(s, dt) for s, dt in out_shapes))


def _silu(x):
    return x * jax.nn.sigmoid(x)


def _rms(x, g):
    return (x * lax.rsqrt(jnp.mean(x * x, axis=-1, keepdims=True) + EPS)) * g


def _modnorm(x, g, shift, scale):
    return _rms(x, g) * (1.0 + scale) + shift


def _adamw(w, g, m, v):
    m = ADAM_B1 * m + (1.0 - ADAM_B1) * g
    v = ADAM_B2 * v + (1.0 - ADAM_B2) * jnp.square(g)
    m_hat = m / (1.0 - ADAM_B1 ** ADAM_STEP)
    v_hat = v / (1.0 - ADAM_B2 ** ADAM_STEP)
    delta = -ADAM_LR * (m_hat / (jnp.sqrt(v_hat) + ADAM_EPS) + ADAM_WD * w)
    return delta, m, v


def _lower_bound(lg):
    e = jnp.exp(lg - jnp.max(lg, axis=0, keepdims=True))
    return e[0:1] / jnp.sum(e, axis=0, keepdims=True)


def _hg_stages(hq_l, hf_l, hi_l, lb):
    C = hq_l[0].shape[0]
    row = lax.broadcasted_iota(jnp.int32, (C, C), 0)
    col = lax.broadcasted_iota(jnp.int32, (C, C), 1)
    tri = row >= col
    trif = tri.astype(F32)
    f_l = [lb + (1.0 - lb) * jax.nn.sigmoid(hf) for hf in hf_l]
    b_l = [_dot(trif, jnp.log(f), NN, precision=HIGHEST) for f in f_l]
    q_l = [_silu(hq) for hq in hq_l]
    m_l = [b[C // 2 - 1:C // 2] for b in b_l]
    bl_l = [b[C - 1:C] for b in b_l]
    sc_l = [jnp.where(tri, _bdot(q * jnp.exp(b - m), (1.0 - f) * jnp.exp(m - b), NT), 0.0)
            for q, f, b, m in zip(q_l, f_l, b_l, m_l)]
    o1_l = [_bdot(sc, hi, NN) for sc, hi in zip(sc_l, hi_l)]
    u_l = [_bdot(hi, (1.0 - f) * jnp.exp(bl - b), TN) for hi, f, b, bl in zip(hi_l, f_l, b_l, bl_l)]
    qb_l = [q * jnp.exp(b) for q, b in zip(q_l, b_l)]
    dec_l = [jnp.exp(bl) for bl in bl_l]
    return list(zip(o1_l, u_l, qb_l, dec_l))


def _hg_out(o, hgate, gout):
    return _rms(o, gout) * _silu(hgate)


HG_STAGE = 8
HG_GROUP = 32


def _hgrn_fwd(p4, lb_logits, gout, H, comm=None):
    L = p4.shape[0]
    C = HG_CHUNK
    GR = _tile(L // C, HG_GROUP, 1)
    T = GR * C
    N = L // T

    def body(hq_ref, hf_ref, hi_ref, hg_ref, lg_ref, gout_ref, o_ref, s_ref, st_ref):
        @pl.when(pl.program_id(1) == 0)
        def _():
            st_ref[...] = jnp.zeros_like(st_ref)

        lb = _lower_bound(lg_ref[...])
        st = st_ref[...]
        for c0 in range(0, GR, HG_STAGE):
            rows_l = [pl.ds(ci * C, C) for ci in range(c0, min(c0 + HG_STAGE, GR))]
            parts = _hg_stages([hq_ref[r, :] for r in rows_l], [hf_ref[r, :] for r in rows_l],
                               [hi_ref[r, :] for r in rows_l], lb)
            for ci, rows, (o1, u, qb, dec) in zip(range(c0, GR), rows_l, parts):
                s_ref[0, ci] = st
                o = o1 + _bdot(qb, st, NT)
                st = st * dec + u
                o_ref[rows, :] = _hg_out(o, hg_ref[rows, :], gout_ref[...]).astype(o_ref.dtype)
        st_ref[...] = st

    blk = lambda s: pl.BlockSpec((T, HG_DK), functools.partial(lambda h, n, s: (n, s * H + h), s=s))
    return _call(
        body, [p4, p4, p4, p4, lb_logits, gout], name="hgrn_fwd", grid=(H, N),
        out_shape=(jax.ShapeDtypeStruct((L, H * HG_DK), BF16), jax.ShapeDtypeStruct((H, N * GR, HG_DK, HG_DK), F32)),
        in_specs=[blk(0), blk(1), blk(2), blk(3), pl.BlockSpec((2, HG_DK), lambda h, n: (0, h)),
                  pl.BlockSpec((1, HG_DK), lambda h, n: (0, 0))],
        out_specs=(pl.BlockSpec((T, HG_DK), lambda h, n: (n, h)),
                   pl.BlockSpec((1, GR, HG_DK, HG_DK), lambda h, n: (h, n, 0, 0))),
        scratch_shapes=[pltpu.VMEM((HG_DK, HG_DK), F32)], comm=comm)


def _hgrn_bwd(p4, lb_logits, gout, s_all, d_out, H, comm=None):
    L = p4.shape[0]
    C = HG_CHUNK
    GR = _tile(L // C, HG_GROUP, 1)
    T = GR * C
    N = L // T

    def body(hq_ref, hf_ref, hi_ref, hg_ref, lg_ref, gout_ref, s_ref, do_ref,
             dq_ref, df_ref, di_ref, dg_ref, dlb_ref, dgo_ref, dst_ref):
        @pl.when(pl.program_id(1) == 0)
        def _():
            dst_ref[...] = jnp.zeros_like(dst_ref)
            dlb_ref[...] = jnp.zeros_like(dlb_ref)

        @pl.when(jnp.logical_and(pl.program_id(0) == 0, pl.program_id(1) == 0))
        def _():
            dgo_ref[...] = jnp.zeros_like(dgo_ref)

        lb = _lower_bound(lg_ref[...])
        dst = dst_ref[...]
        d_lb = jnp.zeros((1, HG_DK), F32)
        d_go = jnp.zeros((1, HG_DK), F32)
        for c0 in reversed(range(0, GR, HG_STAGE)):
            dst, d_lb_c, d_go_c = chunks_bwd(list(range(c0, min(c0 + HG_STAGE, GR))), lb, dst, hq_ref, hf_ref, hi_ref,
                                             hg_ref, gout_ref, s_ref, do_ref, dq_ref, df_ref, di_ref, dg_ref)
            d_lb += d_lb_c
            d_go += d_go_c
        dst_ref[...] = dst
        dlb_ref[...] += d_lb
        dgo_ref[...] += d_go

    def chunks_bwd(idx, lb, dst, hq_ref, hf_ref, hi_ref, hg_ref, gout_ref, s_ref, do_ref, dq_ref, df_ref, di_ref, dg_ref):
        n = len(idx)
        rows_l = [pl.ds(ci * C, C) for ci in idx]
        hq_l, hf_l, hi_l = ([r[rows, :] for rows in rows_l] for r in (hq_ref, hf_ref, hi_ref))
        st_l = [s_ref[0, ci] for ci in idx]
        row = lax.broadcasted_iota(jnp.int32, (C, C), 0)
        col = lax.broadcasted_iota(jnp.int32, (C, C), 1)
        tri = row >= col
        trif = tri.astype(F32)
        every = lambda fn, *ls: [fn(*a) for a in zip(*ls)]
        sg_l = every(jax.nn.sigmoid, hf_l)
        f_l = every(lambda sg: lb + (1.0 - lb) * sg, sg_l)
        b_l = every(lambda f: _dot(trif, jnp.log(f), NN, precision=HIGHEST), f_l)
        q_l = every(_silu, hq_l)
        m_l = every(lambda b: b[C // 2 - 1:C // 2], b_l)
        bl_l = every(lambda b: b[C - 1:C], b_l)
        e_qm_l = every(lambda b, m: jnp.exp(b - m), b_l, m_l)
        e_km_l = every(lambda b, m: jnp.exp(m - b), b_l, m_l)
        e_kl_l = every(lambda b, bl: jnp.exp(bl - b), b_l, bl_l)
        e_q_l = every(jnp.exp, b_l)
        dec_l = every(jnp.exp, bl_l)
        qe_l = every(lambda q, e: q * e, q_l, e_qm_l)
        ke_l = every(lambda f, e: (1.0 - f) * e, f_l, e_km_l)
        kd_l = every(lambda f, e: (1.0 - f) * e, f_l, e_kl_l)
        qb_l = every(lambda q, e: q * e, q_l, e_q_l)
        sc_l = every(lambda qe, ke: jnp.where(tri, _bdot(qe, ke, NT), 0.0), qe_l, ke_l)
        o_l = every(lambda sc, hi, qb, st: _bdot(sc, hi, NN) + _bdot(qb, st, NT), sc_l, hi_l, qb_l, st_l)
        vj_l = every(lambda o, rows: jax.vjp(_hg_out, o, hg_ref[rows, :], gout_ref[...])[1](do_ref[rows, :]), o_l, rows_l)
        do_l = [v[0] for v in vj_l]
        dsc_l = every(lambda do, hi: jnp.where(tri, _bdot(do, hi, NT), 0.0), do_l, hi_l)
        dv1_l = every(lambda sc, do: _bdot(sc, do, TN), sc_l, do_l)
        dqe_l = every(lambda dsc, ke: _bdot(dsc, ke, NN), dsc_l, ke_l)
        dke_l = every(lambda dsc, qe: _bdot(dsc, qe, TN), dsc_l, qe_l)
        dqb_l = every(lambda do, st: _bdot(do, st, NN), do_l, st_l)
        own_l = every(lambda do, qb: _bdot(do, qb, TN), do_l, qb_l)
        dst_next_l = [None] * n
        for j in reversed(range(n)):
            dst_next_l[j] = dst
            dst = own_l[j] + dst * dec_l[j]
        dv_l = every(lambda dv1, kd, dn: dv1 + _bdot(kd, dn, NT), dv1_l, kd_l, dst_next_l)
        dkd_l = every(lambda hi, dn: _bdot(hi, dn, NN), hi_l, dst_next_l)
        ddec_l = every(lambda dn, st: jnp.sum(dn * st, axis=0, keepdims=True), dst_next_l, st_l)
        rowi = lax.broadcasted_iota(jnp.int32, (C, HG_DK), 0)
        tq_l = every(lambda a, b_: a * b_, dqe_l, qe_l)
        tk_l = every(lambda a, b_: a * b_, dke_l, ke_l)
        td_l = every(lambda a, b_: a * b_, dkd_l, kd_l)
        tb_l = every(lambda a, b_: a * b_, dqb_l, qb_l)
        db_l = every(lambda tq, tk, td, tb, ddec, dec: tq - tk - td + tb
                     + jnp.where(rowi == C // 2 - 1, jnp.sum(tk - tq, axis=0, keepdims=True), 0.0)
                     + jnp.where(rowi == C - 1, jnp.sum(td, axis=0, keepdims=True) + ddec * dec, 0.0),
                     tq_l, tk_l, td_l, tb_l, ddec_l, dec_l)
        dlf_l = every(lambda db: _dot(trif, db, TN, precision=HIGHEST), db_l)
        dk_l = every(lambda dke, e1, dkd, e2: dke * e1 + dkd * e2, dke_l, e_km_l, dkd_l, e_kl_l)
        df_l = every(lambda dlf, f, dk: dlf / f - dk, dlf_l, f_l, dk_l)
        d_lb = jnp.zeros((1, HG_DK), F32)
        d_go = jnp.zeros((1, HG_DK), F32)
        for j, rows in enumerate(rows_l):
            sg, hq = sg_l[j], hq_l[j]
            df_ref[rows, :] = (df_l[j] * (1.0 - lb) * sg * (1.0 - sg)).astype(df_ref.dtype)
            sq = jax.nn.sigmoid(hq)
            dq = dqe_l[j] * e_qm_l[j] + dqb_l[j] * e_q_l[j]
            dq_ref[rows, :] = (dq * (sq * (1.0 + hq * (1.0 - sq)))).astype(dq_ref.dtype)
            di_ref[rows, :] = dv_l[j].astype(di_ref.dtype)
            dg_ref[rows, :] = vj_l[j][1].astype(dg_ref.dtype)
            d_lb += jnp.sum(df_l[j] * (1.0 - sg), axis=0, keepdims=True)
            d_go += vj_l[j][2]
        return dst, d_lb, d_go

    blk = lambda s: pl.BlockSpec((T, HG_DK), functools.partial(lambda h, n, s: (N - 1 - n, s * H + h), s=s))
    oblk = pl.BlockSpec((T, HG_DK), lambda h, n: (N - 1 - n, h))
    vec = pl.BlockSpec((1, HG_DK), lambda h, n: (0, h))
    W = H * HG_DK
    return _call(
        body, [p4, p4, p4, p4, lb_logits, gout, s_all, d_out], name="hgrn_bwd", grid=(H, N),
        out_shape=tuple([jax.ShapeDtypeStruct((L, W), BF16)] * 4 + [jax.ShapeDtypeStruct((1, W), F32), jax.ShapeDtypeStruct((1, HG_DK), F32)]),
        in_specs=[blk(0), blk(1), blk(2), blk(3), pl.BlockSpec((2, HG_DK), lambda h, n: (0, h)),
                  pl.BlockSpec((1, HG_DK), lambda h, n: (0, 0)),
                  pl.BlockSpec((1, GR, HG_DK, HG_DK), lambda h, n: (h, N - 1 - n, 0, 0)), oblk],
        out_specs=(oblk, oblk, oblk, oblk, vec, pl.BlockSpec((1, HG_DK), lambda h, n: (0, 0))),
        scratch_shapes=[pltpu.VMEM((HG_DK, HG_DK), F32)], comm=comm)


def _bucket_ids():
    i = jnp.arange(AT_BLOCK, dtype=jnp.int32)[:, None]
    j = jnp.arange(2 * AT_BLOCK, dtype=jnp.int32)[None, :]
    n = jnp.maximum(i - j + AT_BLOCK, 0)
    nf = jnp.maximum(n, 1).astype(F32)
    large = MAX_EXACT + (jnp.log(nf / MAX_EXACT) / math.log(MAX_DISTANCE / MAX_EXACT) * (N_BUCKETS - MAX_EXACT)).astype(jnp.int32)
    large = jnp.minimum(large, N_BUCKETS - 1)
    return jnp.where(n < MAX_EXACT, n, large).reshape(1, -1)


def _onehot(bucket):
    ids = lax.broadcasted_iota(jnp.int32, (N_BUCKETS, bucket.shape[1]), 0)
    return (ids == bucket).astype(F32)


def _attn_probs(qn, kpn, kcn, bias_g, sink, first, scale):
    rows = qn.shape[0]
    i = jnp.bitwise_and(lax.broadcasted_iota(jnp.int32, (rows, AT_BLOCK), 0), AT_BLOCK - 1)
    j = lax.broadcasted_iota(jnp.int32, (rows, AT_BLOCK), 1)
    lp = _bdot(qn, kpn, NT) * scale + bias_g[:, :AT_BLOCK]
    lc = _bdot(qn, kcn, NT) * scale + bias_g[:, AT_BLOCK:]
    lp = jnp.where(jnp.logical_and(j > i, jnp.logical_not(first)), lp, NEG_INF)
    lc = jnp.where(j <= i, lc, NEG_INF)
    m = jnp.maximum(jnp.maximum(jnp.max(lp, axis=-1, keepdims=True), jnp.max(lc, axis=-1, keepdims=True)), sink)
    pp, pc, ps = jnp.exp(lp - m), jnp.exp(lc - m), jnp.exp(sink - m)
    den = jnp.sum(pp, axis=-1, keepdims=True) + jnp.sum(pc, axis=-1, keepdims=True) + ps
    return pp / den, pc / den, ps / den


def _sink_rows(sk_ref, G):
    head = lax.broadcasted_iota(jnp.int32, (G * AT_BLOCK, 1), 0) // AT_BLOCK
    sink = jnp.zeros((G * AT_BLOCK, 1), F32)
    for g in range(G):
        sink = jnp.where(head == g, sk_ref[0, g:g + 1, :], sink)
    return sink


def _attn_fwd(q_t, kp, vp, qg, kg, sinks, bias, KVH, comm=None):
    AH, L, DH = q_t.shape
    G = AH // KVH
    NB = L // AT_BLOCK
    scale = DH ** -0.5

    def body(q_ref, kp_ref, kc_ref, vp_ref, vc_ref, qg_ref, kg_ref, sk_ref, b_ref, o_ref):
        first = pl.program_id(1) == 0
        kpn, kcn = _rms(kp_ref[0], kg_ref[...]), _rms(kc_ref[0], kg_ref[...])
        qn = _rms(q_ref[...].reshape(G * AT_BLOCK, DH), qg_ref[...])
        sink = _sink_rows(sk_ref, G)
        pp, pc, _ = _attn_probs(qn, kpn, kcn, b_ref[...].reshape(G * AT_BLOCK, 2 * AT_BLOCK), sink, first, scale)
        o = _bdot(pp, vp_ref[0], NN) + _bdot(pc, vc_ref[0], NN)
        o_ref[...] = o.reshape(G, AT_BLOCK, DH).astype(o_ref.dtype)

    kblk = lambda off: pl.BlockSpec((1, AT_BLOCK, DH),
                                    functools.partial(lambda h, n, off: (h, jnp.maximum(n + off - 1, 0), 0), off=off))
    return _call(
        body, [q_t, kp, kp, vp, vp, qg, kg, sinks, bias], name="attn_fwd", grid=(KVH, NB),
        out_shape=jax.ShapeDtypeStruct((AH, L, DH), BF16),
        in_specs=[pl.BlockSpec((G, AT_BLOCK, DH), lambda h, n: (h, n, 0)), kblk(0), kblk(1), kblk(0), kblk(1),
                  pl.BlockSpec((1, DH), lambda h, n: (0, 0)), pl.BlockSpec((1, DH), lambda h, n: (0, 0)),
                  pl.BlockSpec((1, G, 1), lambda h, n: (h, 0, 0)),
                  pl.BlockSpec((G, AT_BLOCK, 2 * AT_BLOCK), lambda h, n: (h, 0, 0))],
        out_specs=pl.BlockSpec((G, AT_BLOCK, DH), lambda h, n: (h, n, 0)), comm=comm)


def _attn_bwd(q_t, kp, vp, qg, kg, sinks, bias, do_t, KVH, comm=None):
    AH, L, DH = q_t.shape
    G = AH // KVH
    NB = L // AT_BLOCK
    B = AT_BLOCK
    scale = DH ** -0.5

    def body(q_ref, kp_ref, kc_ref, vp_ref, vc_ref, qg_ref, kg_ref, sk_ref, b_ref, do_ref,
             dq_ref, dk_ref, dv_ref, dqg_ref, dkg_ref, dsk_ref, db_ref):
        n = pl.program_id(1)
        first = n == 0

        @pl.when(first)
        def _():
            for r in (dk_ref, dv_ref, dsk_ref, db_ref):
                r[...] = jnp.zeros_like(r)

        @pl.when(jnp.logical_and(first, pl.program_id(0) == 0))
        def _():
            dqg_ref[...] = jnp.zeros_like(dqg_ref)
            dkg_ref[...] = jnp.zeros_like(dkg_ref)

        kp_raw, kc_raw, kgv, qgv = kp_ref[0], kc_ref[0], kg_ref[...], qg_ref[...]
        kpn, kp_vjp = jax.vjp(_rms, kp_raw, kgv)
        kcn, kc_vjp = jax.vjp(_rms, kc_raw, kgv)
        qn, q_vjp = jax.vjp(_rms, q_ref[...].reshape(G * B, DH), qgv)
        pp, pc, ps = _attn_probs(qn, kpn, kcn, b_ref[...].reshape(G * B, 2 * B), _sink_rows(sk_ref, G), first, scale)
        do = do_ref[...].reshape(G * B, DH)
        dvp = _bdot(pp, do, TN)
        dvc = _bdot(pc, do, TN)
        dpp = _bdot(do, vp_ref[0], NT)
        dpc = _bdot(do, vc_ref[0], NT)
        dsum = jnp.sum(dpp * pp, axis=-1, keepdims=True) + jnp.sum(dpc * pc, axis=-1, keepdims=True)
        dlp = pp * (dpp - dsum)
        dlc = pc * (dpc - dsum)
        dsk_ref[0] += jnp.sum((-ps * dsum).reshape(G, B, 1), axis=1)
        db_ref[:, :, :B] += dlp.reshape(G, B, B)
        db_ref[:, :, B:] += dlc.reshape(G, B, B)
        dlp, dlc = dlp * scale, dlc * scale
        dqn = _bdot(dlp, kpn, NN) + _bdot(dlc, kcn, NN)
        dq_raw, dqg = q_vjp(dqn)
        dq_ref[...] = dq_raw.reshape(G, B, DH).astype(dq_ref.dtype)
        dkp_raw, dkg_p = kp_vjp(_bdot(dlp, qn, TN))
        dkc_raw, dkg_c = kc_vjp(_bdot(dlc, qn, TN))
        r0 = pl.multiple_of(jnp.maximum(n - 1, 0) * B, B)
        r1 = pl.multiple_of(n * B, B)
        dk_ref[0, pl.ds(r0, B), :] += dkp_raw
        dk_ref[0, pl.ds(r1, B), :] += dkc_raw
        dv_ref[0, pl.ds(r0, B), :] += dvp
        dv_ref[0, pl.ds(r1, B), :] += dvc
        dqg_ref[...] += dqg
        dkg_ref[...] += dkg_p + dkg_c

    kblk = lambda off: pl.BlockSpec((1, B, DH), functools.partial(lambda h, n, off: (h, jnp.maximum(n + off - 1, 0), 0), off=off))
    qblk = pl.BlockSpec((G, B, DH), lambda h, n: (h, n, 0))
    accblk = pl.BlockSpec((1, L, DH), lambda h, n: (h, 0, 0))
    vecblk = pl.BlockSpec((1, DH), lambda h, n: (0, 0))
    return _call(
        body, [q_t, kp, kp, vp, vp, qg, kg, sinks, bias, do_t], name="attn_bwd", grid=(KVH, NB),
        out_shape=(jax.ShapeDtypeStruct((AH, L, DH), BF16), jax.ShapeDtypeStruct((KVH, L, DH), F32),
                   jax.ShapeDtypeStruct((KVH, L, DH), F32), jax.ShapeDtypeStruct((1, DH), F32),
                   jax.ShapeDtypeStruct((1, DH), F32), jax.ShapeDtypeStruct((KVH, G, 1), F32),
                   jax.ShapeDtypeStruct((AH, B, 2 * B), F32)),
        in_specs=[qblk, kblk(0), kblk(1), kblk(0), kblk(1),
                  pl.BlockSpec((1, DH), lambda h, n: (0, 0)), pl.BlockSpec((1, DH), lambda h, n: (0, 0)),
                  pl.BlockSpec((1, G, 1), lambda h, n: (h, 0, 0)),
                  pl.BlockSpec((G, B, 2 * B), lambda h, n: (h, 0, 0)), qblk],
        out_specs=(qblk, accblk, accblk, vecblk, vecblk, pl.BlockSpec((1, G, 1), lambda h, n: (h, 0, 0)),
                   pl.BlockSpec((G, B, 2 * B), lambda h, n: (h, 0, 0))), comm=comm)


def _heads_first(t, nh):
    L = t.shape[0]
    return jnp.transpose(t.reshape(L, nh, t.shape[1] // nh), (1, 0, 2))


def _heads_last(t):
    nh, L, dh = t.shape
    return jnp.transpose(t, (1, 0, 2)).reshape(L, nh * dh)


def _softmax0(lg):
    e = jnp.exp(lg - jnp.max(lg, axis=0, keepdims=True))
    return e[0:1] / jnp.sum(e, axis=0, keepdims=True)


def _ada_update_call(fn, c_all, d_cols, w, m, v, rt):
    D, n = w.shape

    def body(c_ref, d_ref, w_ref, m_ref, v_ref, g_out, dl_out, m_out, v_out):
        outs, _ = fn(c_ref[...], d_ref[...], w_ref[...], m_ref[...], v_ref[...])
        for r, val in zip((g_out, dl_out, m_out, v_out), outs):
            r[...] = val

    wblk = pl.BlockSpec((rt, n), lambda i: (i, 0))
    return _call(
        body, [c_all, d_cols, w, m, v], name="update_ada", grid=(D // rt,), out_shape=tuple([jax.ShapeDtypeStruct((D, n), F32)] * 4),
        in_specs=[pl.BlockSpec((N_DEV, rt), lambda i: (0, i)), pl.BlockSpec((N_DEV, n), lambda i: (0, 0)), wblk, wblk, wblk],
        out_specs=(wblk, wblk, wblk, wblk))


def kernel(x, c, w_ada, b_ada, norm1_g, norm2_g, w_in, hg_lb_logits, hg_out_norm_g, q_norm_g, k_norm_g, attn_sinks, rel_bias_table, w_branch_hg, w_branch_attn, w_out, w_ff1, w_ff2, loss_target, m_w_ada, m_b_ada, m_norm1_g, m_norm2_g, m_w_in, m_hg_lb_logits, m_hg_out_norm_g, m_q_norm_g, m_k_norm_g, m_attn_sinks, m_rel_bias_table, m_w_branch_hg, m_w_branch_attn, m_w_out, m_w_ff1, m_w_ff2, v_w_ada, v_b_ada, v_norm1_g, v_norm2_g, v_w_in, v_hg_lb_logits, v_hg_out_norm_g, v_q_norm_g, v_k_norm_g, v_attn_sinks, v_rel_bias_table, v_w_branch_hg, v_w_branch_attn, v_w_out, v_w_ff1, v_w_ff2):
    cc = lax.axis_index("c")
    me = 4 * lax.axis_index("x") + 2 * lax.axis_index("y") + cc
    x2 = x[0]
    tgt = loss_target[0]
    L, D = x2.shape
    HGW = hg_lb_logits.shape[1]
    H = HGW // HG_DK
    AH = attn_sinks.shape[1]
    DH = q_norm_g.shape[1]
    ATW = AH * DH
    BW = w_in.shape[2]
    INW = BW * N_DEV
    A = BW // LANES
    assert BW == LANES * A + LANES // 2
    KVW = (INW - 4 * HGW - ATW - 2 * D) // 2
    KVH = KVW // DH
    G = AH // KVH
    ADA_N = w_ada.shape[2]
    PAIR = 2 * A + 1

    c_all = _gather_small(c, me, "gather_c")[:, 0, :]
    b_cols = lax.dynamic_slice(b_ada, (0, me * ADA_N), (1, ADA_N))
    (ada_cols,) = _whole(lambda cv, w, b: (_bdot(_silu(cv), w, NN) + b,), [c_all, w_ada[0], b_cols],
                         [((N_DEV, ADA_N), F32)], "ada_fwd")
    ada_all = _gather_small(ada_cols, me, "gather_ada")
    ada_row = lax.dynamic_slice(ada_all, (0, me, 0), (N_DEV, 1, ADA_N)).reshape(1, 6 * D)

    w_in_b = w_in[0].astype(BF16)
    src_in = jnp.where(cc == 0, jnp.pad(w_in_b, ((0, 0), (0, LANES // 2))), jnp.pad(w_in_b, ((0, 0), (LANES // 2, 0))))
    (src_in,) = _behind([src_in], [ada_row])
    shift1, scale1, gate1, shift2, scale2, gate2 = [ada_row[:, i * D:(i + 1) * D] for i in range(6)]
    w_in_gapped, w_in_mid = _ag_w_in(src_in, A, D, INW)
    w_in_full = _patch_mid(w_in_gapped, w_in_mid, A)

    wnames = ("bhg", "bat", "out", "ff1", "ff2")
    small = ("bhg", "bat", "out")
    waxis = dict(zip(wnames, (1, 1, 0, 1, 0)))
    wsrc = dict(zip(wnames, (w_branch_hg, w_branch_attn, w_out, w_ff1, w_ff2)))
    wblk = {k: wsrc[k][0].astype(BF16) for k in wnames}
    wf = {}

    (h,) = _rowwise(lambda xv, g, sh, sc: ((_modnorm(xv, g, sh, sc),), ()), [(x2, D, 0)], [norm1_g, shift1, scale1],
                    [(D, BF16)], [], "norm1")
    o4, oa = 4 * HGW, 4 * HGW + ATW + 2 * KVW
    r1, r2, ro = wblk["ff1"].shape[0], wblk["ff2"].shape[0], wblk["out"].shape[0]
    cm = _Comm()
    hs = {k: _ag_ici(cm, wblk[k], waxis[k]) for k in ("bhg", "bat")}
    hs["out"] = _ag_ici(cm, wblk["out"], waxis["out"], rows=(0, ro // 2))
    p4 = _mm(h, w_in_full, "nn", F32, "proj_hg", n=o4, comm=cm)
    half = {k: cm.result(hs[k]) for k in hs}
    cm = _Comm()
    hs = {"out": _ag_ici(cm, wblk["out"], waxis["out"], rows=(ro // 2, ro), into=half["out"])}
    pa = _mm(h, w_in_full, "nn", F32, "proj_at", b_off=o4, n=oa - o4, comm=cm)
    half["out"] = cm.result(hs["out"])
    cm = _Comm()
    hs = {k: _ag_d2d(cm, half[k], waxis[k]) for k in ("bhg", "bat")}
    hs["ff2"] = _ag_ici(cm, wblk["ff2"], waxis["ff2"], rows=(0, r2 // 4))
    pg = _mm(h, w_in_full, "nn", F32, "proj_gate", b_off=oa, n=INW - oa, comm=cm)
    wf["bhg"], wf["bat"], half["ff2"] = (cm.result(hs[k]) for k in ("bhg", "bat", "ff2"))

    cm = _Comm()
    hs = {"out": _ag_d2d(cm, half["out"], waxis["out"]), "ff1": _ag_ici(cm, wblk["ff1"], waxis["ff1"], rows=(0, r1 // 2))}
    o_hg, s_all = _hgrn_fwd(p4, hg_lb_logits, hg_out_norm_g, H, comm=cm)
    wf["out"], half["ff1"] = cm.result(hs["out"]), cm.result(hs["ff1"])

    bucket = _bucket_ids()
    (bias_flat,) = _whole(lambda tb, bk: (_dot(tb, _onehot(bk), TN, precision=HIGHEST),), [rel_bias_table, bucket],
                          [((AH, AT_BLOCK * 2 * AT_BLOCK), F32)], "bias_fwd")
    bias = bias_flat.reshape(AH, AT_BLOCK, 2 * AT_BLOCK)
    q_t = _heads_first(pa[:, :ATW], AH)
    kp = _heads_first(pa[:, ATW:ATW + KVW], KVH)
    vp = _heads_first(pa[:, ATW + KVW:], KVH)
    sinks3 = attn_sinks.reshape(KVH, G, 1)
    cm = _Comm()
    hs = {"ff1": _ag_ici(cm, wblk["ff1"], waxis["ff1"], rows=(r1 // 2, r1), into=half["ff1"])}
    o_at = _heads_last(_attn_fwd(q_t, kp, vp, q_norm_g, k_norm_g, sinks3, bias, KVH, comm=cm))
    half["ff1"] = cm.result(hs["ff1"])

    bh = _mm(o_hg, wf["bhg"], "nn", F32, "branch_hg")
    ba = _mm(o_at, wf["bat"], "nn", F32, "branch_at")

    def merge_fn(bhv, bav, ghg, gat):
        return jax.nn.sigmoid(ghg) * bhv + jax.nn.sigmoid(gat) * bav

    cm = _Comm()
    hs = {"ff1": _ag_d2d(cm, half["ff1"], waxis["ff1"])}
    (merged,) = _rowwise(lambda *a: ((merge_fn(*a),), ()), [(bh, D, 0), (ba, D, 0), (pg, D, 0), (pg, D, 1)], [],
                         [(D, BF16)], [], "merge", comm=cm)
    wf["ff1"] = cm.result(hs["ff1"])
    cm = _Comm()
    hs = {"ff2": _ag_ici(cm, wblk["ff2"], waxis["ff2"], rows=(r2 // 4, 3 * r2 // 8), into=half["ff2"])}
    mo = _mm(merged, wf["out"], "nn", F32, "out_proj", comm=cm)
    half["ff2"] = cm.result(hs["ff2"])

    def resid1(xv, mov, g1, g2n, sh, sc):
        x1v = xv + g1 * mov
        return (x1v, _modnorm(x1v, g2n, sh, sc)), ()

    cm = _Comm()
    hs = {"ff2": _ag_ici(cm, wblk["ff2"], waxis["ff2"], rows=(3 * r2 // 8, r2 // 2), into=half["ff2"])}
    x1, h2 = _rowwise(resid1, [(x2, D, 0), (mo, D, 0)], [gate1, norm2_g, shift2, scale2], [(D, F32), (D, BF16)], [], "resid1",
                      comm=cm)
    half["ff2"] = cm.result(hs["ff2"])
    cm = _Comm()
    hs = {"ff2": _ag_ici(cm, wblk["ff2"], waxis["ff2"], rows=(r2 // 2, r2), into=half["ff2"])}
    u, act = _mm(h2, wf["ff1"], "nn", (F32, BF16), "ff1", comm=cm, epi=lambda r: (r, jnp.square(jnp.maximum(r, 0.0))))
    half["ff2"] = cm.result(hs["ff2"])
    cm = _Comm()
    hs = {"ff2": _ag_d2d(cm, half["ff2"], waxis["ff2"])}
    _call(lambda: None, [], name="ag_d2d_ff2", out_shape=(), comm=cm)
    wf["ff2"] = cm.result(hs["ff2"])
    ff = _mm(act, wf["ff2"], "nn", F32, "ff2")

    def loss_fn(x1v, ffv, tv, g2):
        e = x1v + g2 * ffv - tv
        dy = e * (1.0 / D)
        return (dy, dy * g2), (jnp.sum(e * e, axis=0, keepdims=True), jnp.sum(dy * ffv, axis=0, keepdims=True))

    dy, d_ff, sq_sum, d_gate2 = _rowwise(loss_fn, [(x1, D, 0), (ff, D, 0), (tgt, D, 0)], [gate2],
                                         [(D, F32), (D, BF16)], [(1, D), (1, D)], "loss")
    loss = lax.psum(jnp.sum(sq_sum) * (0.5 / D), ("x", "y", "c"))

    owner_base = jnp.stack([me ^ r for r in CHIP_RELS]).astype(jnp.int32)
    gw, recv1, part, recv2 = {}, {}, {}, {}
    gw["ff2"] = _mm(act, d_ff, "tn", BF16, "dw_ff2")
    cm = _Comm()
    hh = _rs_d2d(cm, gw["ff2"], waxis["ff2"])
    d_u = _mm(d_ff, wf["ff2"], "nt", BF16, "d_act", comm=cm, extras=[u], epi=lambda r, uv: (r * (2.0 * jnp.maximum(uv, 0.0)),))
    part["ff2"] = _rs_add(gw["ff2"], cm.result(hh), waxis["ff2"], owner_base, "rs_add_ff2")
    rows_ff2 = part["ff2"].shape[1]
    cm = _Comm()
    hh = _rs_ici(cm, part["ff2"], rows=(0, rows_ff2 // 2))
    gw["ff1"] = _mm(h2, d_u, "tn", BF16, "dw_ff1", comm=cm)
    cm2 = _Comm()
    hh2 = _rs_ici(cm2, part["ff2"], rows=(rows_ff2 // 2, rows_ff2), recv=cm.result(hh))
    hh1 = _rs_d2d(cm2, gw["ff1"], waxis["ff1"])
    d_h2 = _mm(d_u, wf["ff1"], "nt", F32, "d_h2", comm=cm2)
    recv2["ff2"] = cm2.result(hh2)
    part["ff1"] = _rs_add(gw["ff1"], cm2.result(hh1), waxis["ff1"], owner_base, "rs_add_ff1")

    def norm2_bwd(dh2v, x1v, dyv, mov, g2n, sh, sc, g1):
        _, vjp = jax.vjp(_modnorm, x1v, g2n, sh, sc)
        dx, dg, dsh, dsc = vjp(dh2v)
        dx1 = dyv + dx
        return (dx1, dx1 * g1), (dg, dsh, dsc, jnp.sum(dx1 * mov, axis=0, keepdims=True))

    d_x1, d_mo, d_g2n, d_shift2, d_scale2, d_gate1 = _rowwise(
        norm2_bwd, [(d_h2, D, 0), (x1, D, 0), (dy, D, 0), (mo, D, 0)], [norm2_g, shift2, scale2, gate1],
        [(D, F32), (D, BF16)], [(1, D)] * 4, "norm2_bwd")
    gw["out"] = _mm(merged, d_mo, "tn", BF16, "dw_out")
    d_merged = _mm(d_mo, wf["out"], "nt", F32, "d_merged")

    def merge_bwd(dmv, bhv, bav, ghg, gat):
        _, vjp = jax.vjp(merge_fn, bhv, bav, ghg, gat)
        return vjp(dmv), ()

    d_bh, d_ba, d_ghg, d_gat = _rowwise(merge_bwd, [(d_merged, D, 0), (bh, D, 0), (ba, D, 0), (pg, D, 0), (pg, D, 1)], [],
                                        [(D, BF16)] * 4, [], "merge_bwd")
    gw["bhg"] = _mm(o_hg, d_bh, "tn", BF16, "dw_bhg")
    gw["bat"] = _mm(o_at, d_ba, "tn", BF16, "dw_bat")
    d_ohg = _mm(d_bh, wf["bhg"], "nt", F32, "d_ohg")
    d_oat = _mm(d_ba, wf["bat"], "nt", BF16, "d_oat")
    rows_ff1 = part["ff1"].shape[1]
    cut_ff1 = 3 * rows_ff1 // 8
    cm = _Comm()
    hf1 = _rs_ici(cm, part["ff1"], rows=(0, cut_ff1))
    d_hq, d_hf, d_hi, d_hg, d_lb, d_gout_h = _hgrn_bwd(p4, hg_lb_logits, hg_out_norm_g, s_all, d_ohg, H, comm=cm)
    cm2 = _Comm()
    hf1 = _rs_ici(cm2, part["ff1"], rows=(cut_ff1, rows_ff1), recv=cm.result(hf1))
    hh = {k: _rs_d2d(cm2, gw[k], waxis[k]) for k in small}
    dq_t, dkp, dvp, d_qg, d_kg, d_sk, d_bias = _attn_bwd(q_t, kp, vp, q_norm_g, k_norm_g, sinks3, bias,
                                                         _heads_first(d_oat, AH), KVH, comm=cm2)
    recv2["ff1"] = cm2.result(hf1)
    for k in small:
        part[k] = _rs_add(gw[k], cm2.result(hh[k]), waxis[k], owner_base, "rs_add_" + k)
    d_aq = _heads_last(dq_t)
    d_ak = _heads_last(dkp).astype(BF16)
    d_av = _heads_last(dvp).astype(BF16)
    d_proj = jnp.concatenate([d_hq, d_hf, d_hi, d_hg, d_aq, d_ak, d_av, d_ghg, d_gat], axis=1)
    cm = _Comm()
    hh = {k: _rs_ici(cm, part[k]) for k in small}
    gw_in = _mm(h, d_proj, "tn", BF16, "dw_in", comm=cm)
    for k in small:
        recv2[k] = cm.result(hh[k])

    wm = LANES * A
    cm = _Comm()
    hi_ = cm.inp(gw_in)
    h_main, h_mid = cm.out((4, D, wm), BF16), cm.out((4, D, LANES), BF16)
    for i, r in enumerate(CHIP_RELS):
        def main_view(ref, p, r=r):
            o = p["me"] ^ r ^ 1
            return ref.at[:, pl.ds(pl.multiple_of((PAIR * (o // 2) + (A + 1) * (1 - p["c"])) * LANES, LANES), wm)]

        def mid_view(ref, p, r=r):
            o = p["me"] ^ r
            return ref.at[:, pl.ds(pl.multiple_of((PAIR * (o // 2) + A) * LANES, LANES), LANES)]

        cm.copy(hi_, main_view, h_main, _slot_view(i), 1)
        cm.copy(hi_, mid_view, h_mid, _slot_view(i), 1)
    _call(lambda: None, [], name="rs_d2d_in", out_shape=(), comm=cm)
    chip = jnp.stack([(me ^ r) // 2 for r in CHIP_RELS]).astype(jnp.int32)
    part_main = _rs_add(gw_in, cm.result(h_main), 1, PAIR * chip + (A + 1) * cc, "rs_add_in_main", tw=LANES)
    part_mid = _rs_add(gw_in, cm.result(h_mid), 1, PAIR * chip + A, "rs_add_in_mid", tw=LANES)
    rs_in = _rs_split_start([part_main, part_mid], "rs_in_start")
    d_h = _mm(d_proj, w_in_full, "nt", F32, "d_h", tn=D, after=[rs_in["token"]])

    def norm1_bwd(dhv, xv, dx1v, g1n, sh, sc):
        _, vjp = jax.vjp(_modnorm, xv, g1n, sh, sc)
        dx, dg, dsh, dsc = vjp(dhv)
        return (dx1v + dx,), (dg, dsh, dsc)

    grad_x, d_g1n, d_shift1, d_scale1 = _rowwise(norm1_bwd, [(d_h, D, 0), (x2, D, 0), (d_x1, D, 0)],
                                                 [norm1_g, shift1, scale1], [(D, F32)], [(1, D)] * 3, "norm1_bwd")

    def sum4(p0, p1, p2, p3):
        return ((p0.astype(F32) + p1.astype(F32)) + p2.astype(F32)) + p3.astype(F32)

    def update_fn(w, m, v, p0, p1, p2, p3):
        g = sum4(p0, p1, p2, p3)
        delta, mn, vn = _adamw(w, g, m, v)
        return (g, delta, mn, vn), ()

    wmv = dict(zip(wnames, ((w_branch_hg, m_w_branch_hg, v_w_branch_hg), (w_branch_attn, m_w_branch_attn, v_w_branch_attn),
                            (w_out, m_w_out, v_w_out), (w_ff1, m_w_ff1, v_w_ff1), (w_ff2, m_w_ff2, v_w_ff2))))
    res = {}

    def update(k, p, rx):
        w, m, v = (t[0] for t in wmv[k])
        n = w.shape[1]
        ins = [(t, n, 0) for t in (w, m, v)] + [(p, n, 0, 0)] + [(rx, n, 0, i) for i in range(3)]
        res[k] = [t[None] for t in _rowwise(update_fn, ins, [], [(n, F32)] * 4, [], "update_" + k)]

    for k in wnames:
        update(k, part[k], recv2[k])
    (part_main, part_mid), (rx_main, rx_mid) = _rs_split_wait(rs_in, [grad_x] + [res[k][0] for k in wnames], "rs_in_wait")
    g_main, = _rowwise(lambda *p: ((sum4(*p),), ()), [(part_main, wm, 0, 0)] + [(rx_main, wm, 0, i) for i in range(3)], [],
                       [(wm, F32)], [], "sum_in_main")
    g_mid, = _rowwise(lambda *p: ((sum4(*p),), ()), [(part_mid, LANES, 0, 0)] + [(rx_mid, LANES, 0, i) for i in range(3)], [],
                      [(LANES, F32)], [], "sum_in_mid")
    g_in = jnp.where(cc == 0, jnp.concatenate([g_main, g_mid[:, :LANES // 2]], axis=1),
                     jnp.concatenate([g_mid[:, LANES // 2:], g_main], axis=1))

    def update_given(w, m, v, g):
        delta, mn, vn = _adamw(w, g, m, v)
        return (g, delta, mn, vn), ()

    res["in"] = [t[None] for t in _rowwise(update_given, [(t, BW, 0) for t in (w_in[0], m_w_in[0], v_w_in[0], g_in)], [],
                                           [(BW, F32)] * 4, [], "update_in")]

    d_sinks = d_sk.reshape(1, AH)
    (d_table_t,) = _whole(lambda db, bk: (_dot(db, _onehot(bk), NT, precision=HIGHEST),),
                          [d_bias.reshape(AH, AT_BLOCK * 2 * AT_BLOCK), bucket], [((AH, N_BUCKETS), F32)], "bias_bwd")
    smalls = [d_g1n, d_g2n, d_lb, d_gout_h, d_qg, d_kg, d_sinks, d_table_t.T.reshape(1, N_BUCKETS * AH)]
    widths = [s.shape[1] for s in smalls]
    lanes = [-(-w // LANES) * LANES for w in widths]
    smalls = [jnp.pad(s, ((0, 0), (0, p - w))) for s, w, p in zip(smalls, widths, lanes)]
    tail_row = jnp.concatenate([d_shift1, d_scale1, d_gate1, d_shift2, d_scale2, d_gate2] + smalls, axis=1)
    (tail_row,) = _behind([tail_row], [g_mid])
    tail_all = _gather_small(tail_row, me, "gather_tail")[:, 0, :]
    d_ada_all, packed = tail_all[:, :6 * D], tail_all[:, 6 * D:]
    d_ada_cols = lax.dynamic_slice(d_ada_all, (0, me * ADA_N), (N_DEV, ADA_N))

    def ada_update(cv, dav, w, m, v):
        g = _bdot(_silu(cv), dav, TN)
        delta, mn, vn = _adamw(w, g, m, v)
        return (g, delta, mn, vn), ()

    res["ada"] = [t[None] for t in _ada_update_call(ada_update, c_all, d_ada_cols, w_ada[0], m_w_ada[0], v_w_ada[0], _tile(D, 256, 16))]

    offs = [sum(lanes[:i]) for i in range(len(lanes))]

    def small_update(pk, dada, lg, *wmv_flat):
        tot = pk[0:1]
        for d in range(1, N_DEV):
            tot = tot + pk[d:d + 1]
        gb = dada[0:1]
        for d in range(1, N_DEV):
            gb = gb + dada[d:d + 1]
        gs = [tot[:, offs[i]:offs[i] + widths[i]] for i in range(len(widths))]
        _, lb_vjp = jax.vjp(_softmax0, lg)
        (g_lg,) = lb_vjp(gs[2])
        grads = [gb, gs[0], gs[1], g_lg, gs[3], gs[4], gs[5], gs[6], gs[7]]
        outs = []
        for i, g in enumerate(grads):
            w, m, v = wmv_flat[3 * i:3 * i + 3]
            delta, mn, vn = _adamw(w, g, m, v)
            outs += [g, delta, mn, vn]
        return tuple(outs)

    tbl = lambda t: t.reshape(1, N_BUCKETS * AH)
    small_wmv = [(b_ada, m_b_ada, v_b_ada), (norm1_g, m_norm1_g, v_norm1_g), (norm2_g, m_norm2_g, v_norm2_g),
                 (hg_lb_logits, m_hg_lb_logits, v_hg_lb_logits), (hg_out_norm_g, m_hg_out_norm_g, v_hg_out_norm_g),
                 (q_norm_g, m_q_norm_g, v_q_norm_g), (k_norm_g, m_k_norm_g, v_k_norm_g),
                 (attn_sinks, m_attn_sinks, v_attn_sinks),
                 (tbl(rel_bias_table), tbl(m_rel_bias_table), tbl(v_rel_bias_table))]
    flat = [t for trip in small_wmv for t in trip]
    out_shapes = [(trip[0].shape, F32) for trip in small_wmv for _ in range(4)]
    sres = _whole(small_update, [packed, d_ada_all, hg_lb_logits] + flat, out_shapes, "small_update")
    names_small = ("b_ada", "norm1_g", "norm2_g", "lb", "gout", "qg", "kg", "sinks", "table")
    for i, k in enumerate(names_small):
        r = sres[4 * i:4 * i + 4]
        if k == "table":
            r = [t.reshape(N_BUCKETS, AH) for t in r]
        res[k] = r

    order = ("ada", "b_ada", "norm1_g", "norm2_g", "in", "lb", "gout", "qg", "kg", "sinks", "table", "bhg", "bat", "out", "ff1", "ff2")
    outs = [loss, grad_x[None]]
    for j in range(4):
        outs += [res[k][j] for k in order]
    return tuple(outs)
```

```python
import functools
import math

import jax
import jax.numpy as jnp
from jax import lax
from jax.experimental import pallas as pl
from jax.experimental.pallas import tpu as pltpu

F32 = jnp.float32
BF16 = jnp.bfloat16
EPS = 1e-6
NEG_INF = -1e30
HG_DK = 128
HG_CHUNK = 64
AT_BLOCK = 128
N_BUCKETS = 32
MAX_EXACT = 16
MAX_DISTANCE = 128
N_DEV = 8
LANES = 128
VMEM_LIMIT = 56 * 1024 * 1024
ADAM_LR, ADAM_B1, ADAM_B2, ADAM_EPS, ADAM_WD, ADAM_STEP = 0.001, 0.9, 0.999, 1e-08, 0.01, 10
HIGHEST = lax.Precision.HIGHEST
MESH = pl.DeviceIdType.MESH
ANY = pl.BlockSpec(memory_space=pl.ANY)
CHIP_RELS = (0, 4, 2, 6)

NN = (((1,), (0,)), ((), ()))
NT = (((1,), (1,)), ((), ()))
TN = (((0,), (0,)), ((), ()))


def _tile(n, pref, unit):
    if n <= pref:
        return n
    t = (pref // unit) * unit
    while t >= unit:
        if n % t == 0:
            return t
        t -= unit
    return n


def _dot(a, b, dn, precision=None):
    return lax.dot_general(a, b, dn, preferred_element_type=F32, precision=precision)


def _bdot(a, b, dn):
    return _dot(a.astype(BF16), b.astype(BF16), dn)


def _position():
    x, y, c = lax.axis_index("x"), lax.axis_index("y"), lax.axis_index("c")
    return dict(x=x, y=y, c=c, me=4 * x + 2 * y + c)


def _peer_position(p, rel):
    x = 1 - p["x"] if rel & 4 else p["x"]
    y = 1 - p["y"] if rel & 2 else p["y"]
    c = 1 - p["c"] if rel & 1 else p["c"]
    return dict(x=x, y=y, c=c, me=4 * x + 2 * y + c)


class _Comm:
    def __init__(self):
        self.ins, self.outs, self.alias, self.plans, self.res = [], [], {}, [], None

    def inp(self, arr):
        self.ins.append(arr)
        return ("i", len(self.ins) - 1)

    def out(self, shape, dtype, alias=None):
        self.outs.append(jax.ShapeDtypeStruct(tuple(shape), dtype))
        if alias is not None:
            self.alias[alias[1]] = len(self.outs) - 1
        return ("o", len(self.outs) - 1)

    def copy(self, src, src_view, dst, dst_view, rel):
        self.plans.append((src, src_view, dst, dst_view, rel))

    def result(self, handle):
        return self.res[handle[1]]

    def build(self, in_refs, out_refs, send_sems, recv_sems):
        pos = _position()
        ref = lambda h: in_refs[h[1]] if h[0] == "i" else out_refs[h[1]]
        ops = []
        for k, (src, sv, dst, dv, rel) in enumerate(self.plans):
            s = sv(ref(src), pos)
            if rel == 0:
                cp = pltpu.make_async_copy(s, dv(ref(dst), pos), send_sems.at[k])
                ops.append((cp.start, cp.wait))
                continue
            peer = _peer_position(pos, rel)
            mk = lambda d: pltpu.make_async_remote_copy(
                src_ref=s, dst_ref=d, send_sem=send_sems.at[k], recv_sem=recv_sems.at[k],
                device_id=(peer["x"], peer["y"], peer["c"]), device_id_type=MESH)
            out_cp, in_cp = mk(dv(ref(dst), pos)), mk(dv(ref(dst), peer))

            def wait(out_cp=out_cp, in_cp=in_cp):
                out_cp.wait_send()
                in_cp.wait_recv()

            ops.append((out_cp.start, wait))
        return ops


def _call(body, args, *, name, out_shape, in_specs=None, out_specs=None, grid=None, scratch_shapes=(), comm=None,
          prefetch=None, aliases=None, after=()):
    single = not isinstance(out_shape, (tuple, list))
    out_shape = (out_shape,) if single else tuple(out_shape)
    n_in, n_out, n_scr = len(args), len(out_shape), len(scratch_shapes)
    vm = pl.BlockSpec(memory_space=pltpu.VMEM)
    in_specs = [vm] * n_in if in_specs is None else list(in_specs)
    out_specs = [vm] * n_out if out_specs is None else (list(out_specs) if isinstance(out_specs, (tuple, list)) else [out_specs])
    n_pf = 0 if prefetch is None else len(prefetch)
    kw = {} if aliases is None else {"input_output_aliases": dict(aliases)}
    if comm is None and after:
        n_dep = len(after)

        def fn(*refs):
            body(*refs[:n_pf + n_in], *refs[n_pf + n_in + n_dep:])

        all_args, all_scratch = list(args) + list(after), list(scratch_shapes)
        in_specs = in_specs + [ANY] * n_dep
    elif comm is None:
        fn = body
        all_args, all_scratch = list(args), list(scratch_shapes)
    else:
        n_ci, n_co, n_x = len(comm.ins), len(comm.outs), len(comm.plans)

        def fn(*refs):
            pf, refs = refs[:n_pf], refs[n_pf:]
            o_in, c_in = refs[:n_in], refs[n_in:n_in + n_ci]
            o_out = refs[n_in + n_ci:n_in + n_ci + n_out]
            c_out = refs[n_in + n_ci + n_out:n_in + n_ci + n_out + n_co]
            scr = refs[n_in + n_ci + n_out + n_co:]
            ops = comm.build(c_in, c_out, scr[n_scr], scr[n_scr + 1])
            if grid:
                first = functools.reduce(jnp.logical_and, [pl.program_id(i) == 0 for i in range(len(grid))])
                last = functools.reduce(jnp.logical_and, [pl.program_id(i) == g - 1 for i, g in enumerate(grid)])

                @pl.when(first)
                def _():
                    for start, _w in ops:
                        start()
            else:
                for start, _w in ops:
                    start()
            body(*pf, *o_in, *o_out, *scr[:n_scr])
            if grid:
                @pl.when(last)
                def _():
                    for _s, wait in ops:
                        wait()
            else:
                for _s, wait in ops:
                    wait()

        all_args = list(args) + list(comm.ins)
        in_specs = in_specs + [ANY] * n_ci
        out_shape = out_shape + tuple(comm.outs)
        out_specs = out_specs + [ANY] * n_co
        all_scratch = list(scratch_shapes) + [pltpu.SemaphoreType.DMA((n_x,)), pltpu.SemaphoreType.DMA((n_x,))]
        kw["input_output_aliases"] = {n_pf + n_in + i: n_out + o for i, o in comm.alias.items()}
    sem = None if grid is None else ("arbitrary",) * len(grid)
    params = pltpu.CompilerParams(dimension_semantics=sem, vmem_limit_bytes=VMEM_LIMIT)
    if prefetch is None:
        spec = dict(in_specs=in_specs, out_specs=tuple(out_specs), scratch_shapes=all_scratch)
        if grid is not None:
            spec["grid"] = grid
    else:
        spec = dict(grid_spec=pltpu.PrefetchScalarGridSpec(
            num_scalar_prefetch=n_pf, grid=grid, in_specs=in_specs, out_specs=tuple(out_specs), scratch_shapes=all_scratch))
        all_args = list(prefetch) + all_args
    res = pl.pallas_call(fn, name=name, out_shape=out_shape, compiler_params=params, **spec, **kw)(*all_args)
    res = list(res)
    if comm is not None:
        comm.res = res[n_out:]
        res = res[:n_out]
    return res[0] if single else res


def _whole_view(ref, pos):
    return ref


def _block_view(axis, n, index, rows=None):
    def view(ref, pos):
        off = pl.multiple_of(index(pos) * n, n)
        if rows is None:
            return ref.at[:, pl.ds(off, n)] if axis == 1 else ref.at[pl.ds(off, n), :]
        lo, cnt = rows[0], rows[1] - rows[0]
        if axis == 1:
            return ref.at[pl.ds(lo, cnt), pl.ds(off, n)]
        return ref.at[pl.ds(pl.multiple_of(off + lo, 16), cnt), :]
    return view


def _rows_view(rows):
    def view(ref, pos):
        return ref if rows is None else ref.at[pl.ds(rows[0], rows[1] - rows[0]), :]
    return view


def _slot_view(i, rows=None):
    def view(ref, pos):
        return ref.at[i] if rows is None else ref.at[i, pl.ds(rows[0], rows[1] - rows[0]), :]
    return view


def _exchange(items, name):
    cm = _Comm()
    for a, rel in items:
        cm.copy(cm.inp(a), _whole_view, cm.out(a.shape, a.dtype), _whole_view, rel)
    _call(lambda: None, [], name=name, out_shape=(), comm=cm)
    return cm.res


def _gather_small(v, me, name):
    cm = _Comm()
    hi, ho = cm.inp(v), cm.out((N_DEV,) + v.shape, v.dtype)
    for rel in range(N_DEV):
        cm.copy(hi, _whole_view, ho, lambda ref, p: ref.at[p["me"]], rel)
    _call(lambda: None, [], name=name, out_shape=(), comm=cm)
    return cm.result(ho)


def _ag_ici(cm, blk, axis, rows=None, into=None):
    n = blk.shape[axis]
    shape = list(blk.shape)
    shape[axis] = n * N_DEV
    hi = cm.inp(blk)
    ho = cm.out(shape, blk.dtype) if into is None else cm.out(shape, blk.dtype, alias=cm.inp(into))
    own = _block_view(axis, n, lambda p: p["me"], rows)
    for rel in CHIP_RELS:
        cm.copy(hi, _rows_view(rows), ho, own, rel)
    return ho


def _ag_d2d(cm, full, axis):
    n = full.shape[axis] // N_DEV
    hi = cm.inp(full)
    ho = cm.out(full.shape, full.dtype, alias=hi)
    for r in CHIP_RELS:
        v = _block_view(axis, n, functools.partial(lambda p, r: p["me"] ^ r, r=r))
        cm.copy(hi, v, ho, v, 1)
    return ho


def _rs_d2d(cm, gw, axis):
    n = gw.shape[axis] // N_DEV
    shape = list(gw.shape)
    shape[axis] = n
    hi, ho = cm.inp(gw), cm.out([4] + shape, gw.dtype)
    for i, r in enumerate(CHIP_RELS):
        cm.copy(hi, _block_view(axis, n, functools.partial(lambda p, r: p["me"] ^ r ^ 1, r=r)), ho, _slot_view(i), 1)
    return ho


def _rs_ici(cm, part, rows=None, recv=None):
    if recv is None:
        ho = cm.out((3,) + part.shape[1:], part.dtype)
    else:
        ho = cm.out(recv.shape, recv.dtype, alias=cm.inp(recv))
    hi = cm.inp(part)
    for i in (1, 2, 3):
        cm.copy(hi, _slot_view(i, rows), ho, _slot_view(i - 1, rows), CHIP_RELS[i])
    return ho


def _rs_add(gw, recv, axis, base, name, tw=None):
    _, R, n = recv.shape
    fan = 1
    if axis == 1:
        tw = n if tw is None else tw
        fan = max(f for f in (4, 3, 2, 1) if (n // tw) % f == 0)
        gw_specs = [pl.BlockSpec((R, tw), functools.partial(lambda i, t, b, k: (0, b[i] + fan * t + k), k=k)) for k in range(fan)]
        rv_spec = pl.BlockSpec((None, R, tw * fan), lambda i, t, b: (i, 0, t))
        grid = (4, n // (tw * fan))
    else:
        tw = _tile(n, 1024, LANES)
        gw_specs = [pl.BlockSpec((R, tw), lambda i, t, b: (b[i], t))]
        rv_spec = pl.BlockSpec((None, R, tw), lambda i, t, b: (i, 0, t))
        grid = (4, n // tw)

    def body(b_ref, *refs):
        g_refs, r_ref, o_ref = refs[:fan], refs[fan], refs[fan + 1]
        g = g_refs[0][...] if fan == 1 else jnp.concatenate([g[...] for g in g_refs], axis=1)
        o_ref[...] = (g.astype(F32) + r_ref[...].astype(F32)).astype(o_ref.dtype)

    return _call(body, [gw] * fan + [recv], name=name, out_shape=jax.ShapeDtypeStruct(recv.shape, recv.dtype), grid=grid,
                 in_specs=gw_specs + [rv_spec], out_specs=rv_spec, prefetch=[base])


HBM_SPEC = pl.BlockSpec(memory_space=pltpu.HBM)
SEM_SPEC = pl.BlockSpec(memory_space=pltpu.SEMAPHORE)
SPLIT_PARAMS = pltpu.CompilerParams(has_side_effects=pltpu.SideEffectType.DATAFLOW_SIDE_EFFECTING)


def _split_copies(refs, plans, send_sems, recv_sems):
    pos = _position()
    out = []
    for k, (si, sv, li, lv, rel) in enumerate(plans):
        peer = _peer_position(pos, rel)
        mk = lambda d: pltpu.make_async_remote_copy(
            src_ref=sv(refs[si], pos), dst_ref=d, send_sem=send_sems.at[k], recv_sem=recv_sems.at[k],
            device_id=(peer["x"], peer["y"], peer["c"]), device_id_type=MESH)
        out.append((mk(lv(refs[li], pos)), mk(lv(refs[li], peer))))
    return out


def _split_start(arrays, plans, name):
    n = len(arrays)

    def body(*refs):
        send_sems, recv_sems = refs[n], refs[n + 1]
        for out_cp, _ in _split_copies(refs[:n], plans, send_sems, recv_sems):
            out_cp.start()
        refs[-1][...] = jnp.zeros_like(refs[-1])

    sems = pltpu.SemaphoreType.DMA((len(plans),))
    res = pl.pallas_call(
        body, name=name,
        out_shape=(sems, sems) + tuple(pltpu.HBM(a.shape, a.dtype) for a in arrays) + (jax.ShapeDtypeStruct((8, LANES), F32),),
        in_specs=[HBM_SPEC] * n, out_specs=(SEM_SPEC, SEM_SPEC) + (HBM_SPEC,) * n + (pl.BlockSpec(memory_space=pltpu.VMEM),),
        input_output_aliases={i: 2 + i for i in range(n)}, compiler_params=SPLIT_PARAMS,
    )(*[pltpu.with_memory_space_constraint(a, pltpu.HBM) for a in arrays])
    return res[0], res[1], list(res[2:2 + n]), res[-1]


def _split_wait(send_sems, recv_sems, arrays, plans, after, name):
    n, na = len(arrays), len(after)

    def body(*refs):
        for out_cp, in_cp in _split_copies(refs[:n], plans, refs[n], refs[n + 1]):
            out_cp.wait_send()
            in_cp.wait_recv()

    res = pl.pallas_call(
        body, name=name, out_shape=tuple(pltpu.HBM(a.shape, a.dtype) for a in arrays),
        in_specs=[HBM_SPEC] * n + [SEM_SPEC, SEM_SPEC] + [ANY] * na, out_specs=(HBM_SPEC,) * n,
        input_output_aliases={i: i for i in range(n)}, compiler_params=SPLIT_PARAMS,
    )(*arrays, send_sems, recv_sems, *after)
    return list(res)


def _rs_split_start(parts, name):
    nw = len(parts)
    lands = [lax.empty((3,) + p.shape[1:], p.dtype) for p in parts]
    plans = [(s, _slot_view(i), nw + s, _slot_view(i - 1), CHIP_RELS[i]) for s in range(nw) for i in (1, 2, 3)]
    send_sems, recv_sems, arrays, token = _split_start(list(parts) + lands, plans, name)
    return dict(sems=(send_sems, recv_sems), arrays=arrays, plans=plans, token=token, nw=nw)


def _rs_split_wait(h, after, name):
    arrays = _split_wait(h["sems"][0], h["sems"][1], h["arrays"], h["plans"], after, name)
    return arrays[:h["nw"]], arrays[h["nw"]:]


def _behind(xs, tokens):
    out = lax.optimization_barrier((tuple(xs), tuple(tokens)))
    return list(out[0])


def _ag_w_in(src, a, D, INW):
    wm = LANES * a

    hd = D // 2
    ALL, TOP, BOT = (0, D), (0, hd), (hd, D)

    def main_place(ref, p, rows=ALL):
        off = pl.multiple_of(((2 * a + 1) * (p["me"] // 2) + (a + 1) * p["c"]) * LANES, LANES)
        return ref.at[pl.ds(rows[0], rows[1] - rows[0]), pl.ds(off, wm)]

    def main_src(ref, p):
        return ref.at[:, pl.ds(pl.multiple_of(p["c"] * LANES, LANES), wm)]

    def mid_src(ref, p):
        return ref.at[:, pl.ds(pl.multiple_of((1 - p["c"]) * wm, LANES), LANES)]

    def mid_place(ref, p, rows=ALL):
        return ref.at[p["me"], pl.ds(rows[0], rows[1] - rows[0]), :]

    def body(src_ref, full_ref, mid_ref, send_sems, recv_sems):
        pos = _position()
        sib, xn, yn = (_peer_position(pos, r) for r in (1, 4, 2))
        dg = _peer_position(pos, 6)
        started = []

        def remote(k, s, d, to):
            return pltpu.make_async_remote_copy(src_ref=s, dst_ref=d, send_sem=send_sems.at[k], recv_sem=recv_sems.at[k],
                                                device_id=(to["x"], to["y"], to["c"]), device_id_type=MESH)

        def send(k, owner, rows, to, from_src=False):
            for j, (src_v, place) in enumerate(((main_src, main_place), (mid_src, mid_place))):
                s = src_v(src_ref, pos) if from_src else place(full_ref if j == 0 else mid_ref, owner, rows)
                cp = remote(k + j, s, place(full_ref if j == 0 else mid_ref, owner, rows), to)
                cp.start()
                started.append(cp)

        def landed(k, owner, rows, frm):
            for j, place in enumerate((main_place, mid_place)):
                ref = full_ref if j == 0 else mid_ref
                remote(k + j, place(ref, owner, rows), place(ref, owner, rows), frm).wait_recv()

        local = [pltpu.make_async_copy(main_src(src_ref, pos), main_place(full_ref, pos), send_sems.at[18]),
                 pltpu.make_async_copy(mid_src(src_ref, pos), mid_place(mid_ref, pos), send_sems.at[19])]
        for cp in local:
            cp.start()
        send(0, pos, ALL, sib, from_src=True)
        send(2, pos, ALL, xn, from_src=True)
        send(4, pos, ALL, yn, from_src=True)
        landed(2, xn, ALL, xn)
        send(10, xn, ALL, sib)
        send(6, xn, TOP, yn)
        landed(4, yn, ALL, yn)
        send(12, yn, ALL, sib)
        send(8, yn, BOT, xn)
        landed(6, dg, TOP, yn)
        send(14, dg, TOP, sib)
        landed(8, dg, BOT, xn)
        send(16, dg, BOT, sib)
        sib_of = lambda p: _peer_position(p, 1)
        landed(0, sib, ALL, sib)
        landed(10, sib_of(xn), ALL, sib)
        landed(12, sib_of(yn), ALL, sib)
        landed(14, sib_of(dg), TOP, sib)
        landed(16, sib_of(dg), BOT, sib)
        for cp in started:
            cp.wait_send()
        for cp in local:
            cp.wait()

    return _call(body, [src], name="ag_w_in", in_specs=[ANY], out_specs=[ANY, ANY],
                 out_shape=(jax.ShapeDtypeStruct((D, INW), BF16), jax.ShapeDtypeStruct((N_DEV, D, LANES), BF16)),
                 scratch_shapes=[pltpu.SemaphoreType.DMA((20,)), pltpu.SemaphoreType.DMA((20,))])


def _patch_mid(full, mid, a):
    D = full.shape[0]

    def body(full_ref, e_ref, o_ref, out_ref):
        out_ref[...] = e_ref[...] + o_ref[...]

    return _call(body, [full, mid, mid], name="patch_mid", grid=(N_DEV // 2,),
                 out_shape=jax.ShapeDtypeStruct(full.shape, full.dtype),
                 in_specs=[ANY, pl.BlockSpec((None, D, LANES), lambda j: (2 * j, 0, 0)),
                           pl.BlockSpec((None, D, LANES), lambda j: (2 * j + 1, 0, 0))],
                 out_specs=pl.BlockSpec((D, LANES), lambda j: (0, (2 * a + 1) * j + a)), aliases={0: 0})


MM_RESIDENT = 2048


def _mm(a, b, mode, out_dtype, name, b_off=0, n=None, comm=None, extras=(), epi=None, tn=None, after=(), b_order=None):
    if mode == "nn":
        (M, K), (K2, N) = a.shape, b.shape
    elif mode == "nt":
        (M, K), (N, K2) = a.shape, b.shape
    else:
        (K, M), (K2, N) = a.shape, b.shape
    assert K == K2, (a.shape, b.shape, mode)
    if n is not None:
        N = n
    single = not isinstance(out_dtype, (tuple, list))
    out_dtypes = (out_dtype,) if single else tuple(out_dtype)
    if epi is None:
        epi = lambda r: (r,)
    tk = K if K <= MM_RESIDENT else (MM_RESIDENT if K % MM_RESIDENT == 0 else _tile(K, 512, LANES))
    nk = K // tk
    if M > MM_RESIDENT and mode == "tn" and N <= MM_RESIDENT and not b_off:
        tm, tn = _tile(M, 512, LANES), N
    elif nk > 1:
        tm, tn = _tile(M, 1024, LANES), _tile(N, tn or 1024, LANES)
    else:
        tm = _tile(M, MM_RESIDENT, LANES)
        tn = _tile(math.gcd(N, b_off) if b_off else N, tn or 512, LANES)
    jb = b_off // tn
    dn = {"nn": NN, "nt": NT, "tn": TN}[mode]
    ne, no = len(extras), len(out_dtypes)

    def body(a_ref, b_ref, *rest):
        e_refs, o_refs = rest[:ne], rest[ne:ne + no]

        def finish(r):
            for o_ref, v in zip(o_refs, epi(r, *[e[...] for e in e_refs])):
                o_ref[...] = v.astype(o_ref.dtype)

        if nk == 1:
            finish(_bdot(a_ref[...], b_ref[...], dn))
            return
        acc_ref = rest[ne + no]
        k = pl.program_id(2)

        @pl.when(k == 0)
        def _():
            acc_ref[...] = _bdot(a_ref[...], b_ref[...], dn)

        @pl.when(jnp.logical_and(k > 0, k < nk - 1))
        def _():
            acc_ref[...] += _bdot(a_ref[...], b_ref[...], dn)

        @pl.when(k == nk - 1)
        def _():
            finish(acc_ref[...] + _bdot(a_ref[...], b_ref[...], dn))

    a_spec = pl.BlockSpec((tk, tm), lambda i, j, k: (k, i)) if mode == "tn" else pl.BlockSpec((tm, tk), lambda i, j, k: (i, k))
    col = (lambda j: j + jb) if b_order is None else functools.partial(b_order, tn)
    b_spec = pl.BlockSpec((tn, tk), lambda i, j, k: (j, k)) if mode == "nt" else pl.BlockSpec((tk, tn), lambda i, j, k: (k, col(j)))
    o_spec = pl.BlockSpec((tm, tn), lambda i, j, k: (i, j))
    res = _call(body, [a, b] + list(extras), name=name, grid=(M // tm, N // tn, nk),
                out_shape=tuple(jax.ShapeDtypeStruct((M, N), dt) for dt in out_dtypes),
                in_specs=[a_spec, b_spec] + [o_spec] * ne, out_specs=[o_spec] * no,
                scratch_shapes=[pltpu.VMEM((tm, tn), F32)] if nk > 1 else [], comm=comm, after=after)
    return res[0] if single else res


def _rowwise(fn, row_ins, bcast_ins, row_outs, acc_outs, name, rt=256, comm=None):
    L = row_ins[0][0].shape[-2]
    rt = _tile(L, rt, 16)
    nr, nb, no = len(row_ins), len(bcast_ins), len(row_outs)

    def body(*refs):
        i = pl.program_id(0)
        vals = [r[...] for r in refs[:nr + nb]]
        outs, accs = fn(*vals)
        for r, v in zip(refs[nr + nb:nr + nb + no], outs):
            r[...] = v.astype(r.dtype)
        acc_refs = refs[nr + nb + no:]

        @pl.when(i == 0)
        def _():
            for r in acc_refs:
                r[...] = jnp.zeros_like(r)

        for r, v in zip(acc_refs, accs):
            r[...] += v

    in_specs = []
    for spec in row_ins:
        w, cb = spec[1], spec[2]
        if len(spec) == 4:
            in_specs.append(pl.BlockSpec((None, rt, w), functools.partial(lambda i, cb, ld: (ld, i, cb), cb=cb, ld=spec[3])))
        else:
            in_specs.append(pl.BlockSpec((rt, w), functools.partial(lambda i, cb: (i, cb), cb=cb)))
    in_specs += [pl.BlockSpec(b.shape, lambda i: (0, 0)) for b in bcast_ins]
    out_specs = [pl.BlockSpec((rt, w), lambda i: (i, 0)) for w, _ in row_outs]
    out_specs += [pl.BlockSpec(s, lambda i: (0, 0)) for s in acc_outs]
    out_shape = [jax.ShapeDtypeStruct((L, w), dt) for w, dt in row_outs] + [jax.ShapeDtypeStruct(s, F32) for s in acc_outs]
    return _call(body, [s[0] for s in row_ins] + list(bcast_ins), name=name, grid=(L // rt,), out_shape=tuple(out_shape),
                 in_specs=in_specs, out_specs=out_specs, comm=comm)


def _whole(fn, ins, out_shapes, name):
    def body(*refs):
        outs = fn(*[r[...] for r in refs[:len(ins)]])
        for r, v in zip(refs[len(ins):], outs):
            r[...] = v.astype(r.dtype)

    return _call(body, list(ins), name=name, out_shape=tuple(jax.ShapeDtypeStruct(s, dt) for s, dt in out_shapes))


def _silu(x):
    return x * jax.nn.sigmoid(x)


def _rms(x, g):
    return (x * lax.rsqrt(jnp.mean(x * x, axis=-1, keepdims=True) + EPS)) * g


def _modnorm(x, g, shift, scale):
    return _rms(x, g) * (1.0 + scale) + shift


def _adamw(w, g, m, v):
    m = ADAM_B1 * m + (1.0 - ADAM_B1) * g
    v = ADAM_B2 * v + (1.0 - ADAM_B2) * jnp.square(g)
    m_hat = m / (1.0 - ADAM_B1 ** ADAM_STEP)
    v_hat = v / (1.0 - ADAM_B2 ** ADAM_STEP)
    delta = -ADAM_LR * (m_hat / (jnp.sqrt(v_hat) + ADAM_EPS) + ADAM_WD * w)
    return delta, m, v


def _lower_bound(lg):
    e = jnp.exp(lg - jnp.max(lg, axis=0, keepdims=True))
    return e[0:1] / jnp.sum(e, axis=0, keepdims=True)


def _hg_stages(hq_l, hf_l, hi_l, lb):
    C = hq_l[0].shape[0]
    row = lax.broadcasted_iota(jnp.int32, (C, C), 0)
    col = lax.broadcasted_iota(jnp.int32, (C, C), 1)
    tri = row >= col
    trif = tri.astype(F32)
    f_l = [lb + (1.0 - lb) * jax.nn.sigmoid(hf) for hf in hf_l]
    b_l = [_dot(trif, jnp.log(f), NN, precision=HIGHEST) for f in f_l]
    q_l = [_silu(hq) for hq in hq_l]
    m_l = [b[C // 2 - 1:C // 2] for b in b_l]
    bl_l = [b[C - 1:C] for b in b_l]
    sc_l = [jnp.where(tri, _bdot(q * jnp.exp(b - m), (1.0 - f) * jnp.exp(m - b), NT), 0.0)
            for q, f, b, m in zip(q_l, f_l, b_l, m_l)]
    o1_l = [_bdot(sc, hi, NN) for sc, hi in zip(sc_l, hi_l)]
    u_l = [_bdot(hi, (1.0 - f) * jnp.exp(bl - b), TN) for hi, f, b, bl in zip(hi_l, f_l, b_l, bl_l)]
    qb_l = [q * jnp.exp(b) for q, b in zip(q_l, b_l)]
    dec_l = [jnp.exp(bl) for bl in bl_l]
    return list(zip(o1_l, u_l, qb_l, dec_l))


def _hg_out(o, hgate, gout):
    return _rms(o, gout) * _silu(hgate)


HG_STAGE = 8
HG_GROUP = 32


def _hgrn_fwd(p4, lb_logits, gout, H, comm=None):
    L = p4.shape[0]
    C = HG_CHUNK
    GR = _tile(L // C, HG_GROUP, 1)
    T = GR * C
    N = L // T

    def body(hq_ref, hf_ref, hi_ref, hg_ref, lg_ref, gout_ref, o_ref, s_ref, st_ref):
        @pl.when(pl.program_id(1) == 0)
        def _():
            st_ref[...] = jnp.zeros_like(st_ref)

        lb = _lower_bound(lg_ref[...])
        st = st_ref[...]
        for c0 in range(0, GR, HG_STAGE):
            rows_l = [pl.ds(ci * C, C) for ci in range(c0, min(c0 + HG_STAGE, GR))]
            parts = _hg_stages([hq_ref[r, :] for r in rows_l], [hf_ref[r, :] for r in rows_l],
                               [hi_ref[r, :] for r in rows_l], lb)
            for ci, rows, (o1, u, qb, dec) in zip(range(c0, GR), rows_l, parts):
                s_ref[0, ci] = st
                o = o1 + _bdot(qb, st, NT)
                st = st * dec + u
                o_ref[rows, :] = _hg_out(o, hg_ref[rows, :], gout_ref[...]).astype(o_ref.dtype)
        st_ref[...] = st

    blk = lambda s: pl.BlockSpec((T, HG_DK), functools.partial(lambda h, n, s: (n, s * H + h), s=s))
    return _call(
        body, [p4, p4, p4, p4, lb_logits, gout], name="hgrn_fwd", grid=(H, N),
        out_shape=(jax.ShapeDtypeStruct((L, H * HG_DK), BF16), jax.ShapeDtypeStruct((H, N * GR, HG_DK, HG_DK), F32)),
        in_specs=[blk(0), blk(1), blk(2), blk(3), pl.BlockSpec((2, HG_DK), lambda h, n: (0, h)),
                  pl.BlockSpec((1, HG_DK), lambda h, n: (0, 0))],
        out_specs=(pl.BlockSpec((T, HG_DK), lambda h, n: (n, h)),
                   pl.BlockSpec((1, GR, HG_DK, HG_DK), lambda h, n: (h, n, 0, 0))),
        scratch_shapes=[pltpu.VMEM((HG_DK, HG_DK), F32)], comm=comm)


def _hgrn_bwd(p4, lb_logits, gout, s_all, d_out, H, comm=None):
    L = p4.shape[0]
    C = HG_CHUNK
    GR = _tile(L // C, HG_GROUP, 1)
    T = GR * C
    N = L // T

    def body(hq_ref, hf_ref, hi_ref, hg_ref, lg_ref, gout_ref, s_ref, do_ref,
             dq_ref, df_ref, di_ref, dg_ref, dlb_ref, dgo_ref, dst_ref):
        @pl.when(pl.program_id(1) == 0)
        def _():
            dst_ref[...] = jnp.zeros_like(dst_ref)
            dlb_ref[...] = jnp.zeros_like(dlb_ref)

        @pl.when(jnp.logical_and(pl.program_id(0) == 0, pl.program_id(1) == 0))
        def _():
            dgo_ref[...] = jnp.zeros_like(dgo_ref)

        lb = _lower_bound(lg_ref[...])
        dst = dst_ref[...]
        d_lb = jnp.zeros((1, HG_DK), F32)
        d_go = jnp.zeros((1, HG_DK), F32)
        for c0 in reversed(range(0, GR, HG_STAGE)):
            dst, d_lb_c, d_go_c = chunks_bwd(list(range(c0, min(c0 + HG_STAGE, GR))), lb, dst, hq_ref, hf_ref, hi_ref,
                                             hg_ref, gout_ref, s_ref, do_ref, dq_ref, df_ref, di_ref, dg_ref)
            d_lb += d_lb_c
            d_go += d_go_c
        dst_ref[...] = dst
        dlb_ref[...] += d_lb
        dgo_ref[...] += d_go

    def chunks_bwd(idx, lb, dst, hq_ref, hf_ref, hi_ref, hg_ref, gout_ref, s_ref, do_ref, dq_ref, df_ref, di_ref, dg_ref):
        n = len(idx)
        rows_l = [pl.ds(ci * C, C) for ci in idx]
        hq_l, hf_l, hi_l = ([r[rows, :] for rows in rows_l] for r in (hq_ref, hf_ref, hi_ref))
        st_l = [s_ref[0, ci] for ci in idx]
        row = lax.broadcasted_iota(jnp.int32, (C, C), 0)
        col = lax.broadcasted_iota(jnp.int32, (C, C), 1)
        tri = row >= col
        trif = tri.astype(F32)
        every = lambda fn, *ls: [fn(*a) for a in zip(*ls)]
        sg_l = every(jax.nn.sigmoid, hf_l)
        f_l = every(lambda sg: lb + (1.0 - lb) * sg, sg_l)
        b_l = every(lambda f: _dot(trif, jnp.log(f), NN, precision=HIGHEST), f_l)
        q_l = every(_silu, hq_l)
        m_l = every(lambda b: b[C // 2 - 1:C // 2], b_l)
        bl_l = every(lambda b: b[C - 1:C], b_l)
        e_qm_l = every(lambda b, m: jnp.exp(b - m), b_l, m_l)
        e_km_l = every(lambda b, m: jnp.exp(m - b), b_l, m_l)
        e_kl_l = every(lambda b, bl: jnp.exp(bl - b), b_l, bl_l)
        e_q_l = every(jnp.exp, b_l)
        dec_l = every(jnp.exp, bl_l)
        qe_l = every(lambda q, e: q * e, q_l, e_qm_l)
        ke_l = every(lambda f, e: (1.0 - f) * e, f_l, e_km_l)
        kd_l = every(lambda f, e: (1.0 - f) * e, f_l, e_kl_l)
        qb_l = every(lambda q, e: q * e, q_l, e_q_l)
        sc_l = every(lambda qe, ke: jnp.where(tri, _bdot(qe, ke, NT), 0.0), qe_l, ke_l)
        o_l = every(lambda sc, hi, qb, st: _bdot(sc, hi, NN) + _bdot(qb, st, NT), sc_l, hi_l, qb_l, st_l)
        vj_l = every(lambda o, rows: jax.vjp(_hg_out, o, hg_ref[rows, :], gout_ref[...])[1](do_ref[rows, :]), o_l, rows_l)
        do_l = [v[0] for v in vj_l]
        dsc_l = every(lambda do, hi: jnp.where(tri, _bdot(do, hi, NT), 0.0), do_l, hi_l)
        dv1_l = every(lambda sc, do: _bdot(sc, do, TN), sc_l, do_l)
        dqe_l = every(lambda dsc, ke: _bdot(dsc, ke, NN), dsc_l, ke_l)
        dke_l = every(lambda dsc, qe: _bdot(dsc, qe, TN), dsc_l, qe_l)
        dqb_l = every(lambda do, st: _bdot(do, st, NN), do_l, st_l)
        own_l = every(lambda do, qb: _bdot(do, qb, TN), do_l, qb_l)
        dst_next_l = [None] * n
        for j in reversed(range(n)):
            dst_next_l[j] = dst
            dst = own_l[j] + dst * dec_l[j]
        dv_l = every(lambda dv1, kd, dn: dv1 + _bdot(kd, dn, NT), dv1_l, kd_l, dst_next_l)
        dkd_l = every(lambda hi, dn: _bdot(hi, dn, NN), hi_l, dst_next_l)
        ddec_l = every(lambda dn, st: jnp.sum(dn * st, axis=0, keepdims=True), dst_next_l, st_l)
        rowi = lax.broadcasted_iota(jnp.int32, (C, HG_DK), 0)
        tq_l = every(lambda a, b_: a * b_, dqe_l, qe_l)
        tk_l = every(lambda a, b_: a * b_, dke_l, ke_l)
        td_l = every(lambda a, b_: a * b_, dkd_l, kd_l)
        tb_l = every(lambda a, b_: a * b_, dqb_l, qb_l)
        db_l = every(lambda tq, tk, td, tb, ddec, dec: tq - tk - td + tb
                     + jnp.where(rowi == C // 2 - 1, jnp.sum(tk - tq, axis=0, keepdims=True), 0.0)
                     + jnp.where(rowi == C - 1, jnp.sum(td, axis=0, keepdims=True) + ddec * dec, 0.0),
                     tq_l, tk_l, td_l, tb_l, ddec_l, dec_l)
        dlf_l = every(lambda db: _dot(trif, db, TN, precision=HIGHEST), db_l)
        dk_l = every(lambda dke, e1, dkd, e2: dke * e1 + dkd * e2, dke_l, e_km_l, dkd_l, e_kl_l)
        df_l = every(lambda dlf, f, dk: dlf / f - dk, dlf_l, f_l, dk_l)
        d_lb = jnp.zeros((1, HG_DK), F32)
        d_go = jnp.zeros((1, HG_DK), F32)
        for j, rows in enumerate(rows_l):
            sg, hq = sg_l[j], hq_l[j]
            df_ref[rows, :] = (df_l[j] * (1.0 - lb) * sg * (1.0 - sg)).astype(df_ref.dtype)
            sq = jax.nn.sigmoid(hq)
            dq = dqe_l[j] * e_qm_l[j] + dqb_l[j] * e_q_l[j]
            dq_ref[rows, :] = (dq * (sq * (1.0 + hq * (1.0 - sq)))).astype(dq_ref.dtype)
            di_ref[rows, :] = dv_l[j].astype(di_ref.dtype)
            dg_ref[rows, :] = vj_l[j][1].astype(dg_ref.dtype)
            d_lb += jnp.sum(df_l[j] * (1.0 - sg), axis=0, keepdims=True)
            d_go += vj_l[j][2]
        return dst, d_lb, d_go

    blk = lambda s: pl.BlockSpec((T, HG_DK), functools.partial(lambda h, n, s: (N - 1 - n, s * H + h), s=s))
    oblk = pl.BlockSpec((T, HG_DK), lambda h, n: (N - 1 - n, h))
    vec = pl.BlockSpec((1, HG_DK), lambda h, n: (0, h))
    W = H * HG_DK
    return _call(
        body, [p4, p4, p4, p4, lb_logits, gout, s_all, d_out], name="hgrn_bwd", grid=(H, N),
        out_shape=tuple([jax.ShapeDtypeStruct((L, W), BF16)] * 4 + [jax.ShapeDtypeStruct((1, W), F32), jax.ShapeDtypeStruct((1, HG_DK), F32)]),
        in_specs=[blk(0), blk(1), blk(2), blk(3), pl.BlockSpec((2, HG_DK), lambda h, n: (0, h)),
                  pl.BlockSpec((1, HG_DK), lambda h, n: (0, 0)),
                  pl.BlockSpec((1, GR, HG_DK, HG_DK), lambda h, n: (h, N - 1 - n, 0, 0)), oblk],
        out_specs=(oblk, oblk, oblk, oblk, vec, pl.BlockSpec((1, HG_DK), lambda h, n: (0, 0))),
        scratch_shapes=[pltpu.VMEM((HG_DK, HG_DK), F32)], comm=comm)


def _bucket_ids():
    i = jnp.arange(AT_BLOCK, dtype=jnp.int32)[:, None]
    j = jnp.arange(2 * AT_BLOCK, dtype=jnp.int32)[None, :]
    n = jnp.maximum(i - j + AT_BLOCK, 0)
    nf = jnp.maximum(n, 1).astype(F32)
    large = MAX_EXACT + (jnp.log(nf / MAX_EXACT) / math.log(MAX_DISTANCE / MAX_EXACT) * (N_BUCKETS - MAX_EXACT)).astype(jnp.int32)
    large = jnp.minimum(large, N_BUCKETS - 1)
    return jnp.where(n < MAX_EXACT, n, large).reshape(1, -1)


def _onehot(bucket):
    ids = lax.broadcasted_iota(jnp.int32, (N_BUCKETS, bucket.shape[1]), 0)
    return (ids == bucket).astype(F32)


def _attn_probs(qn, kpn, kcn, bias_g, sink, first, scale):
    rows = qn.shape[0]
    i = jnp.bitwise_and(lax.broadcasted_iota(jnp.int32, (rows, AT_BLOCK), 0), AT_BLOCK - 1)
    j = lax.broadcasted_iota(jnp.int32, (rows, AT_BLOCK), 1)
    lp = _bdot(qn, kpn, NT) * scale + bias_g[:, :AT_BLOCK]
    lc = _bdot(qn, kcn, NT) * scale + bias_g[:, AT_BLOCK:]
    lp = jnp.where(jnp.logical_and(j > i, jnp.logical_not(first)), lp, NEG_INF)
    lc = jnp.where(j <= i, lc, NEG_INF)
    m = jnp.maximum(jnp.maximum(jnp.max(lp, axis=-1, keepdims=True), jnp.max(lc, axis=-1, keepdims=True)), sink)
    pp, pc, ps = jnp.exp(lp - m), jnp.exp(lc - m), jnp.exp(sink - m)
    den = jnp.sum(pp, axis=-1, keepdims=True) + jnp.sum(pc, axis=-1, keepdims=True) + ps
    return pp / den, pc / den, ps / den


def _sink_rows(sk_ref, G):
    head = lax.broadcasted_iota(jnp.int32, (G * AT_BLOCK, 1), 0) // AT_BLOCK
    sink = jnp.zeros((G * AT_BLOCK, 1), F32)
    for g in range(G):
        sink = jnp.where(head == g, sk_ref[0, g:g + 1, :], sink)
    return sink


def _attn_fwd(q_t, kp, vp, qg, kg, sinks, bias, KVH, comm=None):
    AH, L, DH = q_t.shape
    G = AH // KVH
    NB = L // AT_BLOCK
    scale = DH ** -0.5

    def body(q_ref, kp_ref, kc_ref, vp_ref, vc_ref, qg_ref, kg_ref, sk_ref, b_ref, o_ref):
        first = pl.program_id(1) == 0
        kpn, kcn = _rms(kp_ref[0], kg_ref[...]), _rms(kc_ref[0], kg_ref[...])
        qn = _rms(q_ref[...].reshape(G * AT_BLOCK, DH), qg_ref[...])
        sink = _sink_rows(sk_ref, G)
        pp, pc, _ = _attn_probs(qn, kpn, kcn, b_ref[...].reshape(G * AT_BLOCK, 2 * AT_BLOCK), sink, first, scale)
        o = _bdot(pp, vp_ref[0], NN) + _bdot(pc, vc_ref[0], NN)
        o_ref[...] = o.reshape(G, AT_BLOCK, DH).astype(o_ref.dtype)

    kblk = lambda off: pl.BlockSpec((1, AT_BLOCK, DH),
                                    functools.partial(lambda h, n, off: (h, jnp.maximum(n + off - 1, 0), 0), off=off))
    return _call(
        body, [q_t, kp, kp, vp, vp, qg, kg, sinks, bias], name="attn_fwd", grid=(KVH, NB),
        out_shape=jax.ShapeDtypeStruct((AH, L, DH), BF16),
        in_specs=[pl.BlockSpec((G, AT_BLOCK, DH), lambda h, n: (h, n, 0)), kblk(0), kblk(1), kblk(0), kblk(1),
                  pl.BlockSpec((1, DH), lambda h, n: (0, 0)), pl.BlockSpec((1, DH), lambda h, n: (0, 0)),
                  pl.BlockSpec((1, G, 1), lambda h, n: (h, 0, 0)),
                  pl.BlockSpec((G, AT_BLOCK, 2 * AT_BLOCK), lambda h, n: (h, 0, 0))],
        out_specs=pl.BlockSpec((G, AT_BLOCK, DH), lambda h, n: (h, n, 0)), comm=comm)


def _attn_bwd(q_t, kp, vp, qg, kg, sinks, bias, do_t, KVH, comm=None):
    AH, L, DH = q_t.shape
    G = AH // KVH
    NB = L // AT_BLOCK
    B = AT_BLOCK
    scale = DH ** -0.5

    def body(q_ref, kp_ref, kc_ref, vp_ref, vc_ref, qg_ref, kg_ref, sk_ref, b_ref, do_ref,
             dq_ref, dk_ref, dv_ref, dqg_ref, dkg_ref, dsk_ref, db_ref):
        n = pl.program_id(1)
        first = n == 0

        @pl.when(first)
        def _():
            for r in (dk_ref, dv_ref, dsk_ref, db_ref):
                r[...] = jnp.zeros_like(r)

        @pl.when(jnp.logical_and(first, pl.program_id(0) == 0))
        def _():
            dqg_ref[...] = jnp.zeros_like(dqg_ref)
            dkg_ref[...] = jnp.zeros_like(dkg_ref)

        kp_raw, kc_raw, kgv, qgv = kp_ref[0], kc_ref[0], kg_ref[...], qg_ref[...]
        kpn, kp_vjp = jax.vjp(_rms, kp_raw, kgv)
        kcn, kc_vjp = jax.vjp(_rms, kc_raw, kgv)
        qn, q_vjp = jax.vjp(_rms, q_ref[...].reshape(G * B, DH), qgv)
        pp, pc, ps = _attn_probs(qn, kpn, kcn, b_ref[...].reshape(G * B, 2 * B), _sink_rows(sk_ref, G), first, scale)
        do = do_ref[...].reshape(G * B, DH)
        dvp = _bdot(pp, do, TN)
        dvc = _bdot(pc, do, TN)
        dpp = _bdot(do, vp_ref[0], NT)
        dpc = _bdot(do, vc_ref[0], NT)
        dsum = jnp.sum(dpp * pp, axis=-1, keepdims=True) + jnp.sum(dpc * pc, axis=-1, keepdims=True)
        dlp = pp * (dpp - dsum)
        dlc = pc * (dpc - dsum)
        dsk_ref[0] += jnp.sum((-ps * dsum).reshape(G, B, 1), axis=1)
        db_ref[:, :, :B] += dlp.reshape(G, B, B)
        db_ref[:, :, B:] += dlc.reshape(G, B, B)
        dlp, dlc = dlp * scale, dlc * scale
        dqn = _bdot(dlp, kpn, NN) + _bdot(dlc, kcn, NN)
        dq_raw, dqg = q_vjp(dqn)
        dq_ref[...] = dq_raw.reshape(G, B, DH).astype(dq_ref.dtype)
        dkp_raw, dkg_p = kp_vjp(_bdot(dlp, qn, TN))
        dkc_raw, dkg_c = kc_vjp(_bdot(dlc, qn, TN))
        r0 = pl.multiple_of(jnp.maximum(n - 1, 0) * B, B)
        r1 = pl.multiple_of(n * B, B)
        dk_ref[0, pl.ds(r0, B), :] += dkp_raw
        dk_ref[0, pl.ds(r1, B), :] += dkc_raw
        dv_ref[0, pl.ds(r0, B), :] += dvp
        dv_ref[0, pl.ds(r1, B), :] += dvc
        dqg_ref[...] += dqg
        dkg_ref[...] += dkg_p + dkg_c

    kblk = lambda off: pl.BlockSpec((1, B, DH), functools.partial(lambda h, n, off: (h, jnp.maximum(n + off - 1, 0), 0), off=off))
    qblk = pl.BlockSpec((G, B, DH), lambda h, n: (h, n, 0))
    accblk = pl.BlockSpec((1, L, DH), lambda h, n: (h, 0, 0))
    vecblk = pl.BlockSpec((1, DH), lambda h, n: (0, 0))
    return _call(
        body, [q_t, kp, kp, vp, vp, qg, kg, sinks, bias, do_t], name="attn_bwd", grid=(KVH, NB),
        out_shape=(jax.ShapeDtypeStruct((AH, L, DH), BF16), jax.ShapeDtypeStruct((KVH, L, DH), F32),
                   jax.ShapeDtypeStruct((KVH, L, DH), F32), jax.ShapeDtypeStruct((1, DH), F32),
                   jax.ShapeDtypeStruct((1, DH), F32), jax.ShapeDtypeStruct((KVH, G, 1), F32),
                   jax.ShapeDtypeStruct((AH, B, 2 * B), F32)),
        in_specs=[qblk, kblk(0), kblk(1), kblk(0), kblk(1),
                  pl.BlockSpec((1, DH), lambda h, n: (0, 0)), pl.BlockSpec((1, DH), lambda h, n: (0, 0)),
                  pl.BlockSpec((1, G, 1), lambda h, n: (h, 0, 0)),
                  pl.BlockSpec((G, B, 2 * B), lambda h, n: (h, 0, 0)), qblk],
        out_specs=(qblk, accblk, accblk, vecblk, vecblk, pl.BlockSpec((1, G, 1), lambda h, n: (h, 0, 0)),
                   pl.BlockSpec((G, B, 2 * B), lambda h, n: (h, 0, 0))), comm=comm)


def _heads_first(t, nh):
    L = t.shape[0]
    return jnp.transpose(t.reshape(L, nh, t.shape[1] // nh), (1, 0, 2))


def _heads_last(t):
    nh, L, dh = t.shape
    return jnp.transpose(t, (1, 0, 2)).reshape(L, nh * dh)


def _softmax0(lg):
    e = jnp.exp(lg - jnp.max(lg, axis=0, keepdims=True))
    return e[0:1] / jnp.sum(e, axis=0, keepdims=True)


def _ada_update_call(fn, c_all, d_cols, w, m, v, rt):
    D, n = w.shape

    def body(c_ref, d_ref, w_ref, m_ref, v_ref, g_out, dl_out, m_out, v_out):
        outs, _ = fn(c_ref[...], d_ref[...], w_ref[...], m_ref[...], v_ref[...])
        for r, val in zip((g_out, dl_out, m_out, v_out), outs):
            r[...] = val

    wblk = pl.BlockSpec((rt, n), lambda i: (i, 0))
    return _call(
        body, [c_all, d_cols, w, m, v], name="update_ada", grid=(D // rt,), out_shape=tuple([jax.ShapeDtypeStruct((D, n), F32)] * 4),
        in_specs=[pl.BlockSpec((N_DEV, rt), lambda i: (0, i)), pl.BlockSpec((N_DEV, n), lambda i: (0, 0)), wblk, wblk, wblk],
        out_specs=(wblk, wblk, wblk, wblk))


def kernel(x, c, w_ada, b_ada, norm1_g, norm2_g, w_in, hg_lb_logits, hg_out_norm_g, q_norm_g, k_norm_g, attn_sinks, rel_bias_table, w_branch_hg, w_branch_attn, w_out, w_ff1, w_ff2, loss_target, m_w_ada, m_b_ada, m_norm1_g, m_norm2_g, m_w_in, m_hg_lb_logits, m_hg_out_norm_g, m_q_norm_g, m_k_norm_g, m_attn_sinks, m_rel_bias_table, m_w_branch_hg, m_w_branch_attn, m_w_out, m_w_ff1, m_w_ff2, v_w_ada, v_b_ada, v_norm1_g, v_norm2_g, v_w_in, v_hg_lb_logits, v_hg_out_norm_g, v_q_norm_g, v_k_norm_g, v_attn_sinks, v_rel_bias_table, v_w_branch_hg, v_w_branch_attn, v_w_out, v_w_ff1, v_w_ff2):
    cc = lax.axis_index("c")
    me = 4 * lax.axis_index("x") + 2 * lax.axis_index("y") + cc
    x2 = x[0]
    tgt = loss_target[0]
    L, D = x2.shape
    HGW = hg_lb_logits.shape[1]
    H = HGW // HG_DK
    AH = attn_sinks.shape[1]
    DH = q_norm_g.shape[1]
    ATW = AH * DH
    BW = w_in.shape[2]
    INW = BW * N_DEV
    A = BW // LANES
    assert BW == LANES * A + LANES // 2
    KVW = (INW - 4 * HGW - ATW - 2 * D) // 2
    KVH = KVW // DH
    G = AH // KVH
    ADA_N = w_ada.shape[2]
    PAIR = 2 * A + 1

    c_all = _gather_small(c, me, "gather_c")[:, 0, :]
    b_cols = lax.dynamic_slice(b_ada, (0, me * ADA_N), (1, ADA_N))
    (ada_cols,) = _whole(lambda cv, w, b: (_bdot(_silu(cv), w, NN) + b,), [c_all, w_ada[0], b_cols],
                         [((N_DEV, ADA_N), F32)], "ada_fwd")
    ada_all = _gather_small(ada_cols, me, "gather_ada")
    ada_row = lax.dynamic_slice(ada_all, (0, me, 0), (N_DEV, 1, ADA_N)).reshape(1, 6 * D)

    w_in_b = w_in[0].astype(BF16)
    src_in = jnp.where(cc == 0, jnp.pad(w_in_b, ((0, 0), (0, LANES // 2))), jnp.pad(w_in_b, ((0, 0), (LANES // 2, 0))))
    (src_in,) = _behind([src_in], [ada_row])
    shift1, scale1, gate1, shift2, scale2, gate2 = [ada_row[:, i * D:(i + 1) * D] for i in range(6)]
    w_in_gapped, w_in_mid = _ag_w_in(src_in, A, D, INW)
    w_in_full = _patch_mid(w_in_gapped, w_in_mid, A)

    wnames = ("bhg", "bat", "out", "ff1", "ff2")
    small = ("bhg", "bat", "out")
    waxis = dict(zip(wnames, (1, 1, 0, 1, 0)))
    wsrc = dict(zip(wnames, (w_branch_hg, w_branch_attn, w_out, w_ff1, w_ff2)))
    wblk = {k: wsrc[k][0].astype(BF16) for k in wnames}
    wf = {}

    (h,) = _rowwise(lambda xv, g, sh, sc: ((_modnorm(xv, g, sh, sc),), ()), [(x2, D, 0)], [norm1_g, shift1, scale1],
                    [(D, BF16)], [], "norm1")
    o4, oa = 4 * HGW, 4 * HGW + ATW + 2 * KVW
    r1, r2 = wblk["ff1"].shape[0], wblk["ff2"].shape[0]
    assert o4 % D == 0

    def proj_order(tn_, j):
        t4, tg, ng = o4 // tn_, oa // tn_, (INW - oa) // tn_
        return jnp.where(j < t4, j, jnp.where(j < t4 + ng, j + (tg - t4), j - ng))

    cm = _Comm()
    hs = {k: _ag_ici(cm, wblk[k], waxis[k]) for k in small}
    hs["ff2"] = _ag_ici(cm, wblk["ff2"], waxis["ff2"], rows=(0, r2 // 4))
    proj = _mm(h, w_in_full, "nn", F32, "proj", comm=cm, b_order=proj_order)
    half = {k: cm.result(hs[k]) for k in hs}
    p4 = pg = proj
    GATE0 = o4 // D
    pa = proj[:, o4 + (INW - oa):]

    cm = _Comm()
    hs = {k: _ag_d2d(cm, half[k], waxis[k]) for k in small}
    hs["ff1"] = _ag_ici(cm, wblk["ff1"], waxis["ff1"], rows=(0, r1 // 2))
    o_hg, s_all = _hgrn_fwd(p4, hg_lb_logits, hg_out_norm_g, H, comm=cm)
    wf["bhg"], wf["bat"], wf["out"], half["ff1"] = (cm.result(hs[k]) for k in ("bhg", "bat", "out", "ff1"))

    bucket = _bucket_ids()
    (bias_flat,) = _whole(lambda tb, bk: (_dot(tb, _onehot(bk), TN, precision=HIGHEST),), [rel_bias_table, bucket],
                          [((AH, AT_BLOCK * 2 * AT_BLOCK), F32)], "bias_fwd")
    bias = bias_flat.reshape(AH, AT_BLOCK, 2 * AT_BLOCK)
    q_t = _heads_first(pa[:, :ATW], AH)
    kp = _heads_first(pa[:, ATW:ATW + KVW], KVH)
    vp = _heads_first(pa[:, ATW + KVW:], KVH)
    sinks3 = attn_sinks.reshape(KVH, G, 1)
    cm = _Comm()
    hs = {"ff1": _ag_ici(cm, wblk["ff1"], waxis["ff1"], rows=(r1 // 2, r1), into=half["ff1"])}
    o_at = _heads_last(_attn_fwd(q_t, kp, vp, q_norm_g, k_norm_g, sinks3, bias, KVH, comm=cm))
    half["ff1"] = cm.result(hs["ff1"])

    bh = _mm(o_hg, wf["bhg"], "nn", F32, "branch_hg")
    ba = _mm(o_at, wf["bat"], "nn", F32, "branch_at")

    def merge_fn(bhv, bav, ghg, gat):
        return jax.nn.sigmoid(ghg) * bhv + jax.nn.sigmoid(gat) * bav

    cm = _Comm()
    hs = {"ff1": _ag_d2d(cm, half["ff1"], waxis["ff1"])}
    (merged,) = _rowwise(lambda *a: ((merge_fn(*a),), ()), [(bh, D, 0), (ba, D, 0), (pg, D, GATE0), (pg, D, GATE0 + 1)], [],
                         [(D, BF16)], [], "merge", comm=cm)
    wf["ff1"] = cm.result(hs["ff1"])
    cm = _Comm()
    hs = {"ff2": _ag_ici(cm, wblk["ff2"], waxis["ff2"], rows=(r2 // 4, 3 * r2 // 8), into=half["ff2"])}
    mo = _mm(merged, wf["out"], "nn", F32, "out_proj", comm=cm)
    half["ff2"] = cm.result(hs["ff2"])

    def resid1(xv, mov, g1, g2n, sh, sc):
        x1v = xv + g1 * mov
        return (x1v, _modnorm(x1v, g2n, sh, sc)), ()

    cm = _Comm()
    hs = {"ff2": _ag_ici(cm, wblk["ff2"], waxis["ff2"], rows=(3 * r2 // 8, r2 // 2), into=half["ff2"])}
    x1, h2 = _rowwise(resid1, [(x2, D, 0), (mo, D, 0)], [gate1, norm2_g, shift2, scale2], [(D, F32), (D, BF16)], [], "resid1",
                      comm=cm)
    half["ff2"] = cm.result(hs["ff2"])
    cm = _Comm()
    hs = {"ff2": _ag_ici(cm, wblk["ff2"], waxis["ff2"], rows=(r2 // 2, r2), into=half["ff2"])}
    u, act = _mm(h2, wf["ff1"], "nn", (F32, BF16), "ff1", comm=cm, epi=lambda r: (r, jnp.square(jnp.maximum(r, 0.0))))
    half["ff2"] = cm.result(hs["ff2"])
    cm = _Comm()
    hs = {"ff2": _ag_d2d(cm, half["ff2"], waxis["ff2"])}
    _call(lambda: None, [], name="ag_d2d_ff2", out_shape=(), comm=cm)
    wf["ff2"] = cm.result(hs["ff2"])
    ff = _mm(act, wf["ff2"], "nn", F32, "ff2")

    def loss_fn(x1v, ffv, tv, g2):
        e = x1v + g2 * ffv - tv
        dy = e * (1.0 / D)
        return (dy, dy * g2), (jnp.sum(e * e, axis=0, keepdims=True), jnp.sum(dy * ffv, axis=0, keepdims=True))

    dy, d_ff, sq_sum, d_gate2 = _rowwise(loss_fn, [(x1, D, 0), (ff, D, 0), (tgt, D, 0)], [gate2],
                                         [(D, F32), (D, BF16)], [(1, D), (1, D)], "loss")
    loss = lax.psum(jnp.sum(sq_sum) * (0.5 / D), ("x", "y", "c"))

    owner_base = jnp.stack([me ^ r for r in CHIP_RELS]).astype(jnp.int32)
    gw, recv1, part, recv2 = {}, {}, {}, {}
    gw["ff2"] = _mm(act, d_ff, "tn", BF16, "dw_ff2")
    cm = _Comm()
    hh = _rs_d2d(cm, gw["ff2"], waxis["ff2"])
    d_u = _mm(d_ff, wf["ff2"], "nt", BF16, "d_act", comm=cm, extras=[u], epi=lambda r, uv: (r * (2.0 * jnp.maximum(uv, 0.0)),))
    part["ff2"] = _rs_add(gw["ff2"], cm.result(hh), waxis["ff2"], owner_base, "rs_add_ff2")
    rows_ff2 = part["ff2"].shape[1]
    cm = _Comm()
    hh = _rs_ici(cm, part["ff2"], rows=(0, rows_ff2 // 2))
    gw["ff1"] = _mm(h2, d_u, "tn", BF16, "dw_ff1", comm=cm)
    cm2 = _Comm()
    hh2 = _rs_ici(cm2, part["ff2"], rows=(rows_ff2 // 2, rows_ff2), recv=cm.result(hh))
    hh1 = _rs_d2d(cm2, gw["ff1"], waxis["ff1"])
    d_h2 = _mm(d_u, wf["ff1"], "nt", F32, "d_h2", comm=cm2)
    recv2["ff2"] = cm2.result(hh2)
    part["ff1"] = _rs_add(gw["ff1"], cm2.result(hh1), waxis["ff1"], owner_base, "rs_add_ff1")

    def norm2_bwd(dh2v, x1v, dyv, mov, g2n, sh, sc, g1):
        _, vjp = jax.vjp(_modnorm, x1v, g2n, sh, sc)
        dx, dg, dsh, dsc = vjp(dh2v)
        dx1 = dyv + dx
        return (dx1, dx1 * g1), (dg, dsh, dsc, jnp.sum(dx1 * mov, axis=0, keepdims=True))

    d_x1, d_mo, d_g2n, d_shift2, d_scale2, d_gate1 = _rowwise(
        norm2_bwd, [(d_h2, D, 0), (x1, D, 0), (dy, D, 0), (mo, D, 0)], [norm2_g, shift2, scale2, gate1],
        [(D, F32), (D, BF16)], [(1, D)] * 4, "norm2_bwd")
    gw["out"] = _mm(merged, d_mo, "tn", BF16, "dw_out")
    d_merged = _mm(d_mo, wf["out"], "nt", F32, "d_merged")

    def merge_bwd(dmv, bhv, bav, ghg, gat):
        _, vjp = jax.vjp(merge_fn, bhv, bav, ghg, gat)
        return vjp(dmv), ()

    d_bh, d_ba, d_ghg, d_gat = _rowwise(merge_bwd, [(d_merged, D, 0), (bh, D, 0), (ba, D, 0), (pg, D, GATE0), (pg, D, GATE0 + 1)], [],
                                        [(D, BF16)] * 4, [], "merge_bwd")
    gw["bhg"] = _mm(o_hg, d_bh, "tn", BF16, "dw_bhg")
    gw["bat"] = _mm(o_at, d_ba, "tn", BF16, "dw_bat")
    d_ohg = _mm(d_bh, wf["bhg"], "nt", F32, "d_ohg")
    d_oat = _mm(d_ba, wf["bat"], "nt", BF16, "d_oat")
    rows_ff1 = part["ff1"].shape[1]
    cut_ff1 = 3 * rows_ff1 // 8
    cm = _Comm()
    hf1 = _rs_ici(cm, part["ff1"], rows=(0, cut_ff1))
    d_hq, d_hf, d_hi, d_hg, d_lb, d_gout_h = _hgrn_bwd(p4, hg_lb_logits, hg_out_norm_g, s_all, d_ohg, H, comm=cm)
    cm2 = _Comm()
    hf1 = _rs_ici(cm2, part["ff1"], rows=(cut_ff1, rows_ff1), recv=cm.result(hf1))
    hh = {k: _rs_d2d(cm2, gw[k], waxis[k]) for k in small}
    dq_t, dkp, dvp, d_qg, d_kg, d_sk, d_bias = _attn_bwd(q_t, kp, vp, q_norm_g, k_norm_g, sinks3, bias,
                                                         _heads_first(d_oat, AH), KVH, comm=cm2)
    recv2["ff1"] = cm2.result(hf1)
    for k in small:
        part[k] = _rs_add(gw[k], cm2.result(hh[k]), waxis[k], owner_base, "rs_add_" + k)
    d_aq = _heads_last(dq_t)
    d_ak = _heads_last(dkp).astype(BF16)
    d_av = _heads_last(dvp).astype(BF16)
    d_proj = jnp.concatenate([d_hq, d_hf, d_hi, d_hg, d_aq, d_ak, d_av, d_ghg, d_gat], axis=1)
    cm = _Comm()
    hh = {k: _rs_ici(cm, part[k]) for k in small}
    gw_in = _mm(h, d_proj, "tn", BF16, "dw_in", comm=cm)
    for k in small:
        recv2[k] = cm.result(hh[k])

    wm = LANES * A
    cm = _Comm()
    hi_ = cm.inp(gw_in)
    h_main, h_mid = cm.out((4, D, wm), BF16), cm.out((4, D, LANES), BF16)
    for i, r in enumerate(CHIP_RELS):
        def main_view(ref, p, r=r):
            o = p["me"] ^ r ^ 1
            return ref.at[:, pl.ds(pl.multiple_of((PAIR * (o // 2) + (A + 1) * (1 - p["c"])) * LANES, LANES), wm)]

        def mid_view(ref, p, r=r):
            o = p["me"] ^ r
            return ref.at[:, pl.ds(pl.multiple_of((PAIR * (o // 2) + A) * LANES, LANES), LANES)]

        cm.copy(hi_, main_view, h_main, _slot_view(i), 1)
        cm.copy(hi_, mid_view, h_mid, _slot_view(i), 1)
    _call(lambda: None, [], name="rs_d2d_in", out_shape=(), comm=cm)
    chip = jnp.stack([(me ^ r) // 2 for r in CHIP_RELS]).astype(jnp.int32)
    part_main = _rs_add(gw_in, cm.result(h_main), 1, PAIR * chip + (A + 1) * cc, "rs_add_in_main", tw=LANES)
    part_mid = _rs_add(gw_in, cm.result(h_mid), 1, PAIR * chip + A, "rs_add_in_mid", tw=LANES)
    rs_in = _rs_split_start([part_main, part_mid], "rs_in_start")
    d_h = _mm(d_proj, w_in_full, "nt", F32, "d_h", tn=D, after=[rs_in["token"]])

    def norm1_bwd(dhv, xv, dx1v, g1n, sh, sc):
        _, vjp = jax.vjp(_modnorm, xv, g1n, sh, sc)
        dx, dg, dsh, dsc = vjp(dhv)
        return (dx1v + dx,), (dg, dsh, dsc)

    grad_x, d_g1n, d_shift1, d_scale1 = _rowwise(norm1_bwd, [(d_h, D, 0), (x2, D, 0), (d_x1, D, 0)],
                                                 [norm1_g, shift1, scale1], [(D, F32)], [(1, D)] * 3, "norm1_bwd")

    def sum4(p0, p1, p2, p3):
        return ((p0.astype(F32) + p1.astype(F32)) + p2.astype(F32)) + p3.astype(F32)

    def update_fn(w, m, v, p0, p1, p2, p3):
        g = sum4(p0, p1, p2, p3)
        delta, mn, vn = _adamw(w, g, m, v)
        return (g, delta, mn, vn), ()

    wmv = dict(zip(wnames, ((w_branch_hg, m_w_branch_hg, v_w_branch_hg), (w_branch_attn, m_w_branch_attn, v_w_branch_attn),
                            (w_out, m_w_out, v_w_out), (w_ff1, m_w_ff1, v_w_ff1), (w_ff2, m_w_ff2, v_w_ff2))))
    res = {}

    def update(k, p, rx):
        w, m, v = (t[0] for t in wmv[k])
        n = w.shape[1]
        ins = [(t, n, 0) for t in (w, m, v)] + [(p, n, 0, 0)] + [(rx, n, 0, i) for i in range(3)]
        res[k] = [t[None] for t in _rowwise(update_fn, ins, [], [(n, F32)] * 4, [], "update_" + k)]

    for k in wnames:
        update(k, part[k], recv2[k])
    (part_main, part_mid), (rx_main, rx_mid) = _rs_split_wait(rs_in, [grad_x] + [res[k][0] for k in wnames], "rs_in_wait")
    g_main, = _rowwise(lambda *p: ((sum4(*p),), ()), [(part_main, wm, 0, 0)] + [(rx_main, wm, 0, i) for i in range(3)], [],
                       [(wm, F32)], [], "sum_in_main")
    g_mid, = _rowwise(lambda *p: ((sum4(*p),), ()), [(part_mid, LANES, 0, 0)] + [(rx_mid, LANES, 0, i) for i in range(3)], [],
                      [(LANES, F32)], [], "sum_in_mid")
    g_in = jnp.where(cc == 0, jnp.concatenate([g_main, g_mid[:, :LANES // 2]], axis=1),
                     jnp.concatenate([g_mid[:, LANES // 2:], g_main], axis=1))

    def update_given(w, m, v, g):
        delta, mn, vn = _adamw(w, g, m, v)
        return (g, delta, mn, vn), ()

    res["in"] = [t[None] for t in _rowwise(update_given, [(t, BW, 0) for t in (w_in[0], m_w_in[0], v_w_in[0], g_in)], [],
                                           [(BW, F32)] * 4, [], "update_in")]

    d_sinks = d_sk.reshape(1, AH)
    (d_table_t,) = _whole(lambda db, bk: (_dot(db, _onehot(bk), NT, precision=HIGHEST),),
                          [d_bias.reshape(AH, AT_BLOCK * 2 * AT_BLOCK), bucket], [((AH, N_BUCKETS), F32)], "bias_bwd")
    smalls = [d_g1n, d_g2n, d_lb, d_gout_h, d_qg, d_kg, d_sinks, d_table_t.T.reshape(1, N_BUCKETS * AH)]
    widths = [s.shape[1] for s in smalls]
    lanes = [-(-w // LANES) * LANES for w in widths]
    smalls = [jnp.pad(s, ((0, 0), (0, p - w))) for s, w, p in zip(smalls, widths, lanes)]
    tail_row = jnp.concatenate([d_shift1, d_scale1, d_gate1, d_shift2, d_scale2, d_gate2] + smalls, axis=1)
    (tail_row,) = _behind([tail_row], [g_mid])
    tail_all = _gather_small(tail_row, me, "gather_tail")[:, 0, :]
    d_ada_all, packed = tail_all[:, :6 * D], tail_all[:, 6 * D:]
    d_ada_cols = lax.dynamic_slice(d_ada_all, (0, me * ADA_N), (N_DEV, ADA_N))

    def ada_update(cv, dav, w, m, v):
        g = _bdot(_silu(cv), dav, TN)
        delta, mn, vn = _adamw(w, g, m, v)
        return (g, delta, mn, vn), ()

    res["ada"] = [t[None] for t in _ada_update_call(ada_update, c_all, d_ada_cols, w_ada[0], m_w_ada[0], v_w_ada[0], _tile(D, 256, 16))]

    offs = [sum(lanes[:i]) for i in range(len(lanes))]

    def small_update(pk, dada, lg, *wmv_flat):
        tot = pk[0:1]
        for d in range(1, N_DEV):
            tot = tot + pk[d:d + 1]
        gb = dada[0:1]
        for d in range(1, N_DEV):
            gb = gb + dada[d:d + 1]
        gs = [tot[:, offs[i]:offs[i] + widths[i]] for i in range(len(widths))]
        _, lb_vjp = jax.vjp(_softmax0, lg)
        (g_lg,) = lb_vjp(gs[2])
        grads = [gb, gs[0], gs[1], g_lg, gs[3], gs[4], gs[5], gs[6], gs[7]]
        outs = []
        for i, g in enumerate(grads):
            w, m, v = wmv_flat[3 * i:3 * i + 3]
            delta, mn, vn = _adamw(w, g, m, v)
            outs += [g, delta, mn, vn]
        return tuple(outs)

    tbl = lambda t: t.reshape(1, N_BUCKETS * AH)
    small_wmv = [(b_ada, m_b_ada, v_b_ada), (norm1_g, m_norm1_g, v_norm1_g), (norm2_g, m_norm2_g, v_norm2_g),
                 (hg_lb_logits, m_hg_lb_logits, v_hg_lb_logits), (hg_out_norm_g, m_hg_out_norm_g, v_hg_out_norm_g),
                 (q_norm_g, m_q_norm_g, v_q_norm_g), (k_norm_g, m_k_norm_g, v_k_norm_g),
                 (attn_sinks, m_attn_sinks, v_attn_sinks),
                 (tbl(rel_bias_table), tbl(m_rel_bias_table), tbl(v_rel_bias_table))]
    flat = [t for trip in small_wmv for t in trip]
    out_shapes = [(trip[0].shape, F32) for trip in small_wmv for _ in range(4)]
    sres = _whole(small_update, [packed, d_ada_all, hg_lb_logits] + flat, out_shapes, "small_update")
    names_small = ("b_ada", "norm1_g", "norm2_g", "lb", "gout", "qg", "kg", "sinks", "table")
    for i, k in enumerate(names_small):
        r = sres[4 * i:4 * i + 4]
        if k == "table":
            r = [t.reshape(N_BUCKETS, AH) for t in r]
        res[k] = r

    order = ("ada", "b_ada", "norm1_g", "norm2_g", "in", "lb", "gout", "qg", "kg", "sinks", "table", "bhg", "bat", "out", "ff1", "ff2")
    outs = [loss, grad_x[None]]
    for j in range(4):
        outs += [res[k][j] for k in order]
    return tuple(outs)
```

```python
import functools
import math

import jax
import jax.numpy as jnp
from jax import lax
from jax.experimental import pallas as pl
from jax.experimental.pallas import tpu as pltpu

F32 = jnp.float32
BF16 = jnp.bfloat16
EPS = 1e-6
NEG_INF = -1e30
HG_DK = 128
HG_CHUNK = 64
AT_BLOCK = 128
N_BUCKETS = 32
MAX_EXACT = 16
MAX_DISTANCE = 128
N_DEV = 8
LANES = 128
VMEM_LIMIT = 56 * 1024 * 1024
ADAM_LR, ADAM_B1, ADAM_B2, ADAM_EPS, ADAM_WD, ADAM_STEP = 0.001, 0.9, 0.999, 1e-08, 0.01, 10
HIGHEST = lax.Precision.HIGHEST
MESH = pl.DeviceIdType.MESH
ANY = pl.BlockSpec(memory_space=pl.ANY)
CHIP_RELS = (0, 4, 2, 6)

NN = (((1,), (0,)), ((), ()))
NT = (((1,), (1,)), ((), ()))
TN = (((0,), (0,)), ((), ()))


def _tile(n, pref, unit):
    if n <= pref:
        return n
    t = (pref // unit) * unit
    while t >= unit:
        if n % t == 0:
            return t
        t -= unit
    return n


def _dot(a, b, dn, precision=None):
    return lax.dot_general(a, b, dn, preferred_element_type=F32, precision=precision)


def _bdot(a, b, dn):
    return _dot(a.astype(BF16), b.astype(BF16), dn)


def _position():
    x, y, c = lax.axis_index("x"), lax.axis_index("y"), lax.axis_index("c")
    return dict(x=x, y=y, c=c, me=4 * x + 2 * y + c)


def _peer_position(p, rel):
    x = 1 - p["x"] if rel & 4 else p["x"]
    y = 1 - p["y"] if rel & 2 else p["y"]
    c = 1 - p["c"] if rel & 1 else p["c"]
    return dict(x=x, y=y, c=c, me=4 * x + 2 * y + c)


class _Comm:
    def __init__(self):
        self.ins, self.outs, self.alias, self.plans, self.res = [], [], {}, [], None

    def inp(self, arr):
        self.ins.append(arr)
        return ("i", len(self.ins) - 1)

    def out(self, shape, dtype, alias=None):
        self.outs.append(jax.ShapeDtypeStruct(tuple(shape), dtype))
        if alias is not None:
            self.alias[alias[1]] = len(self.outs) - 1
        return ("o", len(self.outs) - 1)

    def copy(self, src, src_view, dst, dst_view, rel):
        self.plans.append((src, src_view, dst, dst_view, rel))

    def result(self, handle):
        return self.res[handle[1]]

    def build(self, in_refs, out_refs, send_sems, recv_sems):
        pos = _position()
        ref = lambda h: in_refs[h[1]] if h[0] == "i" else out_refs[h[1]]
        ops = []
        for k, (src, sv, dst, dv, rel) in enumerate(self.plans):
            s = sv(ref(src), pos)
            if rel == 0:
                cp = pltpu.make_async_copy(s, dv(ref(dst), pos), send_sems.at[k])
                ops.append((cp.start, cp.wait))
                continue
            peer = _peer_position(pos, rel)
            mk = lambda d: pltpu.make_async_remote_copy(
                src_ref=s, dst_ref=d, send_sem=send_sems.at[k], recv_sem=recv_sems.at[k],
                device_id=(peer["x"], peer["y"], peer["c"]), device_id_type=MESH)
            out_cp, in_cp = mk(dv(ref(dst), pos)), mk(dv(ref(dst), peer))

            def wait(out_cp=out_cp, in_cp=in_cp):
                out_cp.wait_send()
                in_cp.wait_recv()

            ops.append((out_cp.start, wait))
        return ops


def _call(body, args, *, name, out_shape, in_specs=None, out_specs=None, grid=None, scratch_shapes=(), comm=None,
          prefetch=None, aliases=None, after=()):
    single = not isinstance(out_shape, (tuple, list))
    out_shape = (out_shape,) if single else tuple(out_shape)
    n_in, n_out, n_scr = len(args), len(out_shape), len(scratch_shapes)
    vm = pl.BlockSpec(memory_space=pltpu.VMEM)
    in_specs = [vm] * n_in if in_specs is None else list(in_specs)
    out_specs = [vm] * n_out if out_specs is None else (list(out_specs) if isinstance(out_specs, (tuple, list)) else [out_specs])
    n_pf = 0 if prefetch is None else len(prefetch)
    kw = {} if aliases is None else {"input_output_aliases": dict(aliases)}
    if comm is None and after:
        n_dep = len(after)

        def fn(*refs):
            body(*refs[:n_pf + n_in], *refs[n_pf + n_in + n_dep:])

        all_args, all_scratch = list(args) + list(after), list(scratch_shapes)
        in_specs = in_specs + [ANY] * n_dep
    elif comm is None:
        fn = body
        all_args, all_scratch = list(args), list(scratch_shapes)
    else:
        n_ci, n_co, n_x = len(comm.ins), len(comm.outs), len(comm.plans)

        def fn(*refs):
            pf, refs = refs[:n_pf], refs[n_pf:]
            o_in, c_in = refs[:n_in], refs[n_in:n_in + n_ci]
            o_out = refs[n_in + n_ci:n_in + n_ci + n_out]
            c_out = refs[n_in + n_ci + n_out:n_in + n_ci + n_out + n_co]
            scr = refs[n_in + n_ci + n_out + n_co:]
            ops = comm.build(c_in, c_out, scr[n_scr], scr[n_scr + 1])
            if grid:
                first = functools.reduce(jnp.logical_and, [pl.program_id(i) == 0 for i in range(len(grid))])
                last = functools.reduce(jnp.logical_and, [pl.program_id(i) == g - 1 for i, g in enumerate(grid)])

                @pl.when(first)
                def _():
                    for start, _w in ops:
                        start()
            else:
                for start, _w in ops:
                    start()
            body(*pf, *o_in, *o_out, *scr[:n_scr])
            if grid:
                @pl.when(last)
                def _():
                    for _s, wait in ops:
                        wait()
            else:
                for _s, wait in ops:
                    wait()

        all_args = list(args) + list(comm.ins)
        in_specs = in_specs + [ANY] * n_ci
        out_shape = out_shape + tuple(comm.outs)
        out_specs = out_specs + [ANY] * n_co
        all_scratch = list(scratch_shapes) + [pltpu.SemaphoreType.DMA((n_x,)), pltpu.SemaphoreType.DMA((n_x,))]
        kw["input_output_aliases"] = {n_pf + n_in + i: n_out + o for i, o in comm.alias.items()}
    sem = None if grid is None else ("arbitrary",) * len(grid)
    params = pltpu.CompilerParams(dimension_semantics=sem, vmem_limit_bytes=VMEM_LIMIT)
    if prefetch is None:
        spec = dict(in_specs=in_specs, out_specs=tuple(out_specs), scratch_shapes=all_scratch)
        if grid is not None:
            spec["grid"] = grid
    else:
        spec = dict(grid_spec=pltpu.PrefetchScalarGridSpec(
            num_scalar_prefetch=n_pf, grid=grid, in_specs=in_specs, out_specs=tuple(out_specs), scratch_shapes=all_scratch))
        all_args = list(prefetch) + all_args
    res = pl.pallas_call(fn, name=name, out_shape=out_shape, compiler_params=params, **spec, **kw)(*all_args)
    res = list(res)
    if comm is not None:
        comm.res = res[n_out:]
        res = res[:n_out]
    return res[0] if single else res


def _whole_view(ref, pos):
    return ref


def _block_view(axis, n, index, rows=None):
    def view(ref, pos):
        off = pl.multiple_of(index(pos) * n, n)
        if rows is None:
            return ref.at[:, pl.ds(off, n)] if axis == 1 else ref.at[pl.ds(off, n), :]
        lo, cnt = rows[0], rows[1] - rows[0]
        if axis == 1:
            return ref.at[pl.ds(lo, cnt), pl.ds(off, n)]
        return ref.at[pl.ds(pl.multiple_of(off + lo, 16), cnt), :]
    return view


def _rows_view(rows):
    def view(ref, pos):
        return ref if rows is None else ref.at[pl.ds(rows[0], rows[1] - rows[0]), :]
    return view


def _slot_view(i, rows=None):
    def view(ref, pos):
        return ref.at[i] if rows is None else ref.at[i, pl.ds(rows[0], rows[1] - rows[0]), :]
    return view


def _exchange(items, name):
    cm = _Comm()
    for a, rel in items:
        cm.copy(cm.inp(a), _whole_view, cm.out(a.shape, a.dtype), _whole_view, rel)
    _call(lambda: None, [], name=name, out_shape=(), comm=cm)
    return cm.res


def _gather_small(v, me, name):
    cm = _Comm()
    hi, ho = cm.inp(v), cm.out((N_DEV,) + v.shape, v.dtype)
    for rel in range(N_DEV):
        cm.copy(hi, _whole_view, ho, lambda ref, p: ref.at[p["me"]], rel)
    _call(lambda: None, [], name=name, out_shape=(), comm=cm)
    return cm.result(ho)


def _ag_ici(cm, blk, axis, rows=None, into=None):
    n = blk.shape[axis]
    shape = list(blk.shape)
    shape[axis] = n * N_DEV
    hi = cm.inp(blk)
    ho = cm.out(shape, blk.dtype) if into is None else cm.out(shape, blk.dtype, alias=cm.inp(into))
    own = _block_view(axis, n, lambda p: p["me"], rows)
    for rel in CHIP_RELS:
        cm.copy(hi, _rows_view(rows), ho, own, rel)
    return ho


def _ag_d2d(cm, full, axis):
    n = full.shape[axis] // N_DEV
    hi = cm.inp(full)
    ho = cm.out(full.shape, full.dtype, alias=hi)
    for r in CHIP_RELS:
        v = _block_view(axis, n, functools.partial(lambda p, r: p["me"] ^ r, r=r))
        cm.copy(hi, v, ho, v, 1)
    return ho


def _rs_d2d(cm, gw, axis):
    n = gw.shape[axis] // N_DEV
    shape = list(gw.shape)
    shape[axis] = n
    hi, ho = cm.inp(gw), cm.out([4] + shape, gw.dtype)
    for i, r in enumerate(CHIP_RELS):
        cm.copy(hi, _block_view(axis, n, functools.partial(lambda p, r: p["me"] ^ r ^ 1, r=r)), ho, _slot_view(i), 1)
    return ho


def _rs_ici(cm, part, rows=None, recv=None):
    if recv is None:
        ho = cm.out((3,) + part.shape[1:], part.dtype)
    else:
        ho = cm.out(recv.shape, recv.dtype, alias=cm.inp(recv))
    hi = cm.inp(part)
    for i in (1, 2, 3):
        cm.copy(hi, _slot_view(i, rows), ho, _slot_view(i - 1, rows), CHIP_RELS[i])
    return ho


def _rs_add(gw, recv, axis, base, name, tw=None):
    _, R, n = recv.shape
    fan = 1
    if axis == 1:
        tw = n if tw is None else tw
        fan = max(f for f in (4, 3, 2, 1) if (n // tw) % f == 0)
        gw_specs = [pl.BlockSpec((R, tw), functools.partial(lambda i, t, b, k: (0, b[i] + fan * t + k), k=k)) for k in range(fan)]
        rv_spec = pl.BlockSpec((None, R, tw * fan), lambda i, t, b: (i, 0, t))
        grid = (4, n // (tw * fan))
    else:
        tw = _tile(n, 1024, LANES)
        gw_specs = [pl.BlockSpec((R, tw), lambda i, t, b: (b[i], t))]
        rv_spec = pl.BlockSpec((None, R, tw), lambda i, t, b: (i, 0, t))
        grid = (4, n // tw)

    def body(b_ref, *refs):
        g_refs, r_ref, o_ref = refs[:fan], refs[fan], refs[fan + 1]
        g = g_refs[0][...] if fan == 1 else jnp.concatenate([g[...] for g in g_refs], axis=1)
        o_ref[...] = (g.astype(F32) + r_ref[...].astype(F32)).astype(o_ref.dtype)

    return _call(body, [gw] * fan + [recv], name=name, out_shape=jax.ShapeDtypeStruct(recv.shape, recv.dtype), grid=grid,
                 in_specs=gw_specs + [rv_spec], out_specs=rv_spec, prefetch=[base])


HBM_SPEC = pl.BlockSpec(memory_space=pltpu.HBM)
SEM_SPEC = pl.BlockSpec(memory_space=pltpu.SEMAPHORE)
SPLIT_PARAMS = pltpu.CompilerParams(has_side_effects=pltpu.SideEffectType.DATAFLOW_SIDE_EFFECTING)


def _split_copies(refs, plans, send_sems, recv_sems):
    pos = _position()
    out = []
    for k, (si, sv, li, lv, rel) in enumerate(plans):
        peer = _peer_position(pos, rel)
        mk = lambda d: pltpu.make_async_remote_copy(
            src_ref=sv(refs[si], pos), dst_ref=d, send_sem=send_sems.at[k], recv_sem=recv_sems.at[k],
            device_id=(peer["x"], peer["y"], peer["c"]), device_id_type=MESH)
        out.append((mk(lv(refs[li], pos)), mk(lv(refs[li], peer))))
    return out


def _split_start(arrays, plans, name):
    n = len(arrays)

    def body(*refs):
        send_sems, recv_sems = refs[n], refs[n + 1]
        for out_cp, _ in _split_copies(refs[:n], plans, send_sems, recv_sems):
            out_cp.start()
        refs[-1][...] = jnp.zeros_like(refs[-1])

    sems = pltpu.SemaphoreType.DMA((len(plans),))
    res = pl.pallas_call(
        body, name=name,
        out_shape=(sems, sems) + tuple(pltpu.HBM(a.shape, a.dtype) for a in arrays) + (jax.ShapeDtypeStruct((8, LANES), F32),),
        in_specs=[HBM_SPEC] * n, out_specs=(SEM_SPEC, SEM_SPEC) + (HBM_SPEC,) * n + (pl.BlockSpec(memory_space=pltpu.VMEM),),
        input_output_aliases={i: 2 + i for i in range(n)}, compiler_params=SPLIT_PARAMS,
    )(*[pltpu.with_memory_space_constraint(a, pltpu.HBM) for a in arrays])
    return res[0], res[1], list(res[2:2 + n]), res[-1]


def _split_wait(send_sems, recv_sems, arrays, plans, after, name):
    n, na = len(arrays), len(after)

    def body(*refs):
        for out_cp, in_cp in _split_copies(refs[:n], plans, refs[n], refs[n + 1]):
            out_cp.wait_send()
            in_cp.wait_recv()

    res = pl.pallas_call(
        body, name=name, out_shape=tuple(pltpu.HBM(a.shape, a.dtype) for a in arrays),
        in_specs=[HBM_SPEC] * n + [SEM_SPEC, SEM_SPEC] + [ANY] * na, out_specs=(HBM_SPEC,) * n,
        input_output_aliases={i: i for i in range(n)}, compiler_params=SPLIT_PARAMS,
    )(*arrays, send_sems, recv_sems, *after)
    return list(res)


def _rs_split_start(parts, name):
    nw = len(parts)
    lands = [lax.empty((3,) + p.shape[1:], p.dtype) for p in parts]
    plans = [(s, _slot_view(i), nw + s, _slot_view(i - 1), CHIP_RELS[i]) for s in range(nw) for i in (1, 2, 3)]
    send_sems, recv_sems, arrays, token = _split_start(list(parts) + lands, plans, name)
    return dict(sems=(send_sems, recv_sems), arrays=arrays, plans=plans, token=token, nw=nw)


def _rs_split_wait(h, after, name):
    arrays = _split_wait(h["sems"][0], h["sems"][1], h["arrays"], h["plans"], after, name)
    return arrays[:h["nw"]], arrays[h["nw"]:]


def _behind(xs, tokens):
    out = lax.optimization_barrier((tuple(xs), tuple(tokens)))
    return list(out[0])


def _ag_w_in(src, a, D, INW):
    wm = LANES * a

    hd = D // 2
    ALL, TOP, BOT = (0, D), (0, hd), (hd, D)

    def main_place(ref, p, rows=ALL):
        off = pl.multiple_of(((2 * a + 1) * (p["me"] // 2) + (a + 1) * p["c"]) * LANES, LANES)
        return ref.at[pl.ds(rows[0], rows[1] - rows[0]), pl.ds(off, wm)]

    def main_src(ref, p, rows=ALL):
        return ref.at[pl.ds(rows[0], rows[1] - rows[0]), pl.ds(pl.multiple_of(p["c"] * LANES, LANES), wm)]

    def mid_src(ref, p, rows=ALL):
        return ref.at[pl.ds(rows[0], rows[1] - rows[0]), pl.ds(pl.multiple_of((1 - p["c"]) * wm, LANES), LANES)]

    def mid_place(ref, p, rows=ALL):
        return ref.at[p["me"], pl.ds(rows[0], rows[1] - rows[0]), :]

    def body(src_ref, full_ref, mid_ref, send_sems, recv_sems):
        pos = _position()
        sib, xn, yn = (_peer_position(pos, r) for r in (1, 4, 2))
        dg = _peer_position(pos, 6)
        started = []

        def remote(k, s, d, to):
            return pltpu.make_async_remote_copy(src_ref=s, dst_ref=d, send_sem=send_sems.at[k], recv_sem=recv_sems.at[k],
                                                device_id=(to["x"], to["y"], to["c"]), device_id_type=MESH)

        def send(k, owner, rows, to, from_src=False):
            for j, (src_v, place) in enumerate(((main_src, main_place), (mid_src, mid_place))):
                s = src_v(src_ref, pos, rows) if from_src else place(full_ref if j == 0 else mid_ref, owner, rows)
                cp = remote(k + j, s, place(full_ref if j == 0 else mid_ref, owner, rows), to)
                cp.start()
                started.append(cp)

        def landed(k, owner, rows, frm):
            for j, place in enumerate((main_place, mid_place)):
                ref = full_ref if j == 0 else mid_ref
                remote(k + j, place(ref, owner, rows), place(ref, owner, rows), frm).wait_recv()

        local = [pltpu.make_async_copy(main_src(src_ref, pos), main_place(full_ref, pos), send_sems.at[24]),
                 pltpu.make_async_copy(mid_src(src_ref, pos), mid_place(mid_ref, pos), send_sems.at[25])]
        for cp in local:
            cp.start()
        send(0, pos, ALL, sib, from_src=True)
        send(2, pos, TOP, xn, from_src=True)
        send(4, pos, BOT, yn, from_src=True)
        send(20, pos, BOT, xn, from_src=True)
        send(22, pos, TOP, yn, from_src=True)
        landed(2, xn, TOP, xn)
        send(6, xn, TOP, yn)
        landed(4, yn, BOT, yn)
        send(8, yn, BOT, xn)
        landed(20, xn, BOT, xn)
        send(10, xn, ALL, sib)
        landed(22, yn, TOP, yn)
        send(12, yn, ALL, sib)
        landed(6, dg, TOP, yn)
        send(14, dg, TOP, sib)
        landed(8, dg, BOT, xn)
        send(16, dg, BOT, sib)
        sib_of = lambda p: _peer_position(p, 1)
        landed(0, sib, ALL, sib)
        landed(10, sib_of(xn), ALL, sib)
        landed(12, sib_of(yn), ALL, sib)
        landed(14, sib_of(dg), TOP, sib)
        landed(16, sib_of(dg), BOT, sib)
        for cp in started:
            cp.wait_send()
        for cp in local:
            cp.wait()

    return _call(body, [src], name="ag_w_in", in_specs=[ANY], out_specs=[ANY, ANY],
                 out_shape=(jax.ShapeDtypeStruct((D, INW), BF16), jax.ShapeDtypeStruct((N_DEV, D, LANES), BF16)),
                 scratch_shapes=[pltpu.SemaphoreType.DMA((26,)), pltpu.SemaphoreType.DMA((26,))])


def _patch_mid(full, mid, a):
    D = full.shape[0]

    def body(full_ref, e_ref, o_ref, out_ref):
        out_ref[...] = e_ref[...] + o_ref[...]

    return _call(body, [full, mid, mid], name="patch_mid", grid=(N_DEV // 2,),
                 out_shape=jax.ShapeDtypeStruct(full.shape, full.dtype),
                 in_specs=[ANY, pl.BlockSpec((None, D, LANES), lambda j: (2 * j, 0, 0)),
                           pl.BlockSpec((None, D, LANES), lambda j: (2 * j + 1, 0, 0))],
                 out_specs=pl.BlockSpec((D, LANES), lambda j: (0, (2 * a + 1) * j + a)), aliases={0: 0})


MM_RESIDENT = 2048


def _mm(a, b, mode, out_dtype, name, b_off=0, n=None, comm=None, extras=(), epi=None, tn=None, after=(), b_order=None):
    if mode == "nn":
        (M, K), (K2, N) = a.shape, b.shape
    elif mode == "nt":
        (M, K), (N, K2) = a.shape, b.shape
    else:
        (K, M), (K2, N) = a.shape, b.shape
    assert K == K2, (a.shape, b.shape, mode)
    if n is not None:
        N = n
    single = not isinstance(out_dtype, (tuple, list))
    out_dtypes = (out_dtype,) if single else tuple(out_dtype)
    if epi is None:
        epi = lambda r: (r,)
    tk = K if K <= MM_RESIDENT else (MM_RESIDENT if K % MM_RESIDENT == 0 else _tile(K, 512, LANES))
    nk = K // tk
    if M > MM_RESIDENT and mode == "tn" and N <= MM_RESIDENT and not b_off:
        tm, tn = _tile(M, 512, LANES), N
    elif nk > 1:
        tm, tn = _tile(M, 1024, LANES), _tile(N, tn or 1024, LANES)
    else:
        tm = _tile(M, MM_RESIDENT, LANES)
        tn = _tile(math.gcd(N, b_off) if b_off else N, tn or 512, LANES)
    jb = b_off // tn
    dn = {"nn": NN, "nt": NT, "tn": TN}[mode]
    ne, no = len(extras), len(out_dtypes)

    def body(a_ref, b_ref, *rest):
        e_refs, o_refs = rest[:ne], rest[ne:ne + no]

        def finish(r):
            for o_ref, v in zip(o_refs, epi(r, *[e[...] for e in e_refs])):
                o_ref[...] = v.astype(o_ref.dtype)

        if nk == 1:
            finish(_bdot(a_ref[...], b_ref[...], dn))
            return
        acc_ref = rest[ne + no]
        k = pl.program_id(2)

        @pl.when(k == 0)
        def _():
            acc_ref[...] = _bdot(a_ref[...], b_ref[...], dn)

        @pl.when(jnp.logical_and(k > 0, k < nk - 1))
        def _():
            acc_ref[...] += _bdot(a_ref[...], b_ref[...], dn)

        @pl.when(k == nk - 1)
        def _():
            finish(acc_ref[...] + _bdot(a_ref[...], b_ref[...], dn))

    a_spec = pl.BlockSpec((tk, tm), lambda i, j, k: (k, i)) if mode == "tn" else pl.BlockSpec((tm, tk), lambda i, j, k: (i, k))
    col = (lambda j: j + jb) if b_order is None else functools.partial(b_order, tn)
    b_spec = pl.BlockSpec((tn, tk), lambda i, j, k: (j, k)) if mode == "nt" else pl.BlockSpec((tk, tn), lambda i, j, k: (k, col(j)))
    o_spec = pl.BlockSpec((tm, tn), lambda i, j, k: (i, j))
    res = _call(body, [a, b] + list(extras), name=name, grid=(M // tm, N // tn, nk),
                out_shape=tuple(jax.ShapeDtypeStruct((M, N), dt) for dt in out_dtypes),
                in_specs=[a_spec, b_spec] + [o_spec] * ne, out_specs=[o_spec] * no,
                scratch_shapes=[pltpu.VMEM((tm, tn), F32)] if nk > 1 else [], comm=comm, after=after)
    return res[0] if single else res


def _rowwise(fn, row_ins, bcast_ins, row_outs, acc_outs, name, rt=256, comm=None):
    L = row_ins[0][0].shape[-2]
    rt = _tile(L, rt, 16)
    nr, nb, no = len(row_ins), len(bcast_ins), len(row_outs)

    def body(*refs):
        i = pl.program_id(0)
        vals = [r[...] for r in refs[:nr + nb]]
        outs, accs = fn(*vals)
        for r, v in zip(refs[nr + nb:nr + nb + no], outs):
            r[...] = v.astype(r.dtype)
        acc_refs = refs[nr + nb + no:]

        @pl.when(i == 0)
        def _():
            for r in acc_refs:
                r[...] = jnp.zeros_like(r)

        for r, v in zip(acc_refs, accs):
            r[...] += v

    in_specs = []
    for spec in row_ins:
        w, cb = spec[1], spec[2]
        if len(spec) == 4:
            in_specs.append(pl.BlockSpec((None, rt, w), functools.partial(lambda i, cb, ld: (ld, i, cb), cb=cb, ld=spec[3])))
        else:
            in_specs.append(pl.BlockSpec((rt, w), functools.partial(lambda i, cb: (i, cb), cb=cb)))
    in_specs += [pl.BlockSpec(b.shape, lambda i: (0, 0)) for b in bcast_ins]
    out_specs = [pl.BlockSpec((rt, w), lambda i: (i, 0)) for w, _ in row_outs]
    out_specs += [pl.BlockSpec(s, lambda i: (0, 0)) for s in acc_outs]
    out_shape = [jax.ShapeDtypeStruct((L, w), dt) for w, dt in row_outs] + [jax.ShapeDtypeStruct(s, F32) for s in acc_outs]
    return _call(body, [s[0] for s in row_ins] + list(bcast_ins), name=name, grid=(L // rt,), out_shape=tuple(out_shape),
                 in_specs=in_specs, out_specs=out_specs, comm=comm)


def _whole(fn, ins, out_shapes, name):
    def body(*refs):
        outs = fn(*[r[...] for r in refs[:len(ins)]])
        for r, v in zip(refs[len(ins):], outs):
            r[...] = v.astype(r.dtype)

    return _call(body, list(ins), name=name, out_shape=tuple(jax.ShapeDtypeStruct(s, dt) for s, dt in out_shapes))


def _silu(x):
    return x * jax.nn.sigmoid(x)


def _rms(x, g):
    return (x * lax.rsqrt(jnp.mean(x * x, axis=-1, keepdims=True) + EPS)) * g


def _modnorm(x, g, shift, scale):
    return _rms(x, g) * (1.0 + scale) + shift


def _adamw(w, g, m, v):
    m = ADAM_B1 * m + (1.0 - ADAM_B1) * g
    v = ADAM_B2 * v + (1.0 - ADAM_B2) * jnp.square(g)
    m_hat = m / (1.0 - ADAM_B1 ** ADAM_STEP)
    v_hat = v / (1.0 - ADAM_B2 ** ADAM_STEP)
    delta = -ADAM_LR * (m_hat / (jnp.sqrt(v_hat) + ADAM_EPS) + ADAM_WD * w)
    return delta, m, v


def _lower_bound(lg):
    e = jnp.exp(lg - jnp.max(lg, axis=0, keepdims=True))
    return e[0:1] / jnp.sum(e, axis=0, keepdims=True)


def _hg_stages(hq_l, hf_l, hi_l, lb):
    C = hq_l[0].shape[0]
    row = lax.broadcasted_iota(jnp.int32, (C, C), 0)
    col = lax.broadcasted_iota(jnp.int32, (C, C), 1)
    tri = row >= col
    trif = tri.astype(F32)
    f_l = [lb + (1.0 - lb) * jax.nn.sigmoid(hf) for hf in hf_l]
    b_l = [_dot(trif, jnp.log(f), NN, precision=HIGHEST) for f in f_l]
    q_l = [_silu(hq) for hq in hq_l]
    m_l = [b[C // 2 - 1:C // 2] for b in b_l]
    bl_l = [b[C - 1:C] for b in b_l]
    sc_l = [jnp.where(tri, _bdot(q * jnp.exp(b - m), (1.0 - f) * jnp.exp(m - b), NT), 0.0)
            for q, f, b, m in zip(q_l, f_l, b_l, m_l)]
    o1_l = [_bdot(sc, hi, NN) for sc, hi in zip(sc_l, hi_l)]
    u_l = [_bdot(hi, (1.0 - f) * jnp.exp(bl - b), TN) for hi, f, b, bl in zip(hi_l, f_l, b_l, bl_l)]
    qb_l = [q * jnp.exp(b) for q, b in zip(q_l, b_l)]
    dec_l = [jnp.exp(bl) for bl in bl_l]
    return list(zip(o1_l, u_l, qb_l, dec_l))


def _hg_out(o, hgate, gout):
    return _rms(o, gout) * _silu(hgate)


HG_STAGE = 8
HG_GROUP = 32


def _hgrn_fwd(p4, lb_logits, gout, H, comm=None):
    L = p4.shape[0]
    C = HG_CHUNK
    GR = _tile(L // C, HG_GROUP, 1)
    T = GR * C
    N = L // T

    def body(hq_ref, hf_ref, hi_ref, hg_ref, lg_ref, gout_ref, o_ref, s_ref, st_ref):
        @pl.when(pl.program_id(1) == 0)
        def _():
            st_ref[...] = jnp.zeros_like(st_ref)

        lb = _lower_bound(lg_ref[...])
        st = st_ref[...]
        for c0 in range(0, GR, HG_STAGE):
            rows_l = [pl.ds(ci * C, C) for ci in range(c0, min(c0 + HG_STAGE, GR))]
            parts = _hg_stages([hq_ref[r, :] for r in rows_l], [hf_ref[r, :] for r in rows_l],
                               [hi_ref[r, :] for r in rows_l], lb)
            for ci, rows, (o1, u, qb, dec) in zip(range(c0, GR), rows_l, parts):
                s_ref[0, ci] = st
                o = o1 + _bdot(qb, st, NT)
                st = st * dec + u
                o_ref[rows, :] = _hg_out(o, hg_ref[rows, :], gout_ref[...]).astype(o_ref.dtype)
        st_ref[...] = st

    blk = lambda s: pl.BlockSpec((T, HG_DK), functools.partial(lambda h, n, s: (n, s * H + h), s=s))
    return _call(
        body, [p4, p4, p4, p4, lb_logits, gout], name="hgrn_fwd", grid=(H, N),
        out_shape=(jax.ShapeDtypeStruct((L, H * HG_DK), BF16), jax.ShapeDtypeStruct((H, N * GR, HG_DK, HG_DK), F32)),
        in_specs=[blk(0), blk(1), blk(2), blk(3), pl.BlockSpec((2, HG_DK), lambda h, n: (0, h)),
                  pl.BlockSpec((1, HG_DK), lambda h, n: (0, 0))],
        out_specs=(pl.BlockSpec((T, HG_DK), lambda h, n: (n, h)),
                   pl.BlockSpec((1, GR, HG_DK, HG_DK), lambda h, n: (h, n, 0, 0))),
        scratch_shapes=[pltpu.VMEM((HG_DK, HG_DK), F32)], comm=comm)


def _hgrn_bwd(p4, lb_logits, gout, s_all, d_out, H, comm=None):
    L = p4.shape[0]
    C = HG_CHUNK
    GR = _tile(L // C, HG_GROUP, 1)
    T = GR * C
    N = L // T

    def body(hq_ref, hf_ref, hi_ref, hg_ref, lg_ref, gout_ref, s_ref, do_ref,
             dq_ref, df_ref, di_ref, dg_ref, dlb_ref, dgo_ref, dst_ref):
        @pl.when(pl.program_id(1) == 0)
        def _():
            dst_ref[...] = jnp.zeros_like(dst_ref)
            dlb_ref[...] = jnp.zeros_like(dlb_ref)

        @pl.when(jnp.logical_and(pl.program_id(0) == 0, pl.program_id(1) == 0))
        def _():
            dgo_ref[...] = jnp.zeros_like(dgo_ref)

        lb = _lower_bound(lg_ref[...])
        dst = dst_ref[...]
        d_lb = jnp.zeros((1, HG_DK), F32)
        d_go = jnp.zeros((1, HG_DK), F32)
        for c0 in reversed(range(0, GR, HG_STAGE)):
            dst, d_lb_c, d_go_c = chunks_bwd(list(range(c0, min(c0 + HG_STAGE, GR))), lb, dst, hq_ref, hf_ref, hi_ref,
                                             hg_ref, gout_ref, s_ref, do_ref, dq_ref, df_ref, di_ref, dg_ref)
            d_lb += d_lb_c
            d_go += d_go_c
        dst_ref[...] = dst
        dlb_ref[...] += d_lb
        dgo_ref[...] += d_go

    def chunks_bwd(idx, lb, dst, hq_ref, hf_ref, hi_ref, hg_ref, gout_ref, s_ref, do_ref, dq_ref, df_ref, di_ref, dg_ref):
        n = len(idx)
        rows_l = [pl.ds(ci * C, C) for ci in idx]
        hq_l, hf_l, hi_l = ([r[rows, :] for rows in rows_l] for r in (hq_ref, hf_ref, hi_ref))
        st_l = [s_ref[0, ci] for ci in idx]
        row = lax.broadcasted_iota(jnp.int32, (C, C), 0)
        col = lax.broadcasted_iota(jnp.int32, (C, C), 1)
        tri = row >= col
        trif = tri.astype(F32)
        every = lambda fn, *ls: [fn(*a) for a in zip(*ls)]
        sg_l = every(jax.nn.sigmoid, hf_l)
        f_l = every(lambda sg: lb + (1.0 - lb) * sg, sg_l)
        b_l = every(lambda f: _dot(trif, jnp.log(f), NN, precision=HIGHEST), f_l)
        q_l = every(_silu, hq_l)
        m_l = every(lambda b: b[C // 2 - 1:C // 2], b_l)
        bl_l = every(lambda b: b[C - 1:C], b_l)
        e_qm_l = every(lambda b, m: jnp.exp(b - m), b_l, m_l)
        e_km_l = every(lambda b, m: jnp.exp(m - b), b_l, m_l)
        e_kl_l = every(lambda b, bl: jnp.exp(bl - b), b_l, bl_l)
        e_q_l = every(jnp.exp, b_l)
        dec_l = every(jnp.exp, bl_l)
        qe_l = every(lambda q, e: q * e, q_l, e_qm_l)
        ke_l = every(lambda f, e: (1.0 - f) * e, f_l, e_km_l)
        kd_l = every(lambda f, e: (1.0 - f) * e, f_l, e_kl_l)
        qb_l = every(lambda q, e: q * e, q_l, e_q_l)
        sc_l = every(lambda qe, ke: jnp.where(tri, _bdot(qe, ke, NT), 0.0), qe_l, ke_l)
        o_l = every(lambda sc, hi, qb, st: _bdot(sc, hi, NN) + _bdot(qb, st, NT), sc_l, hi_l, qb_l, st_l)
        vj_l = every(lambda o, rows: jax.vjp(_hg_out, o, hg_ref[rows, :], gout_ref[...])[1](do_ref[rows, :]), o_l, rows_l)
        do_l = [v[0] for v in vj_l]
        dsc_l = every(lambda do, hi: jnp.where(tri, _bdot(do, hi, NT), 0.0), do_l, hi_l)
        dv1_l = every(lambda sc, do: _bdot(sc, do, TN), sc_l, do_l)
        dqe_l = every(lambda dsc, ke: _bdot(dsc, ke, NN), dsc_l, ke_l)
        dke_l = every(lambda dsc, qe: _bdot(dsc, qe, TN), dsc_l, qe_l)
        dqb_l = every(lambda do, st: _bdot(do, st, NN), do_l, st_l)
        own_l = every(lambda do, qb: _bdot(do, qb, TN), do_l, qb_l)
        dst_next_l = [None] * n
        for j in reversed(range(n)):
            dst_next_l[j] = dst
            dst = own_l[j] + dst * dec_l[j]
        dv_l = every(lambda dv1, kd, dn: dv1 + _bdot(kd, dn, NT), dv1_l, kd_l, dst_next_l)
        dkd_l = every(lambda hi, dn: _bdot(hi, dn, NN), hi_l, dst_next_l)
        ddec_l = every(lambda dn, st: jnp.sum(dn * st, axis=0, keepdims=True), dst_next_l, st_l)
        rowi = lax.broadcasted_iota(jnp.int32, (C, HG_DK), 0)
        tq_l = every(lambda a, b_: a * b_, dqe_l, qe_l)
        tk_l = every(lambda a, b_: a * b_, dke_l, ke_l)
        td_l = every(lambda a, b_: a * b_, dkd_l, kd_l)
        tb_l = every(lambda a, b_: a * b_, dqb_l, qb_l)
        db_l = every(lambda tq, tk, td, tb, ddec, dec: tq - tk - td + tb
                     + jnp.where(rowi == C // 2 - 1, jnp.sum(tk - tq, axis=0, keepdims=True), 0.0)
                     + jnp.where(rowi == C - 1, jnp.sum(td, axis=0, keepdims=True) + ddec * dec, 0.0),
                     tq_l, tk_l, td_l, tb_l, ddec_l, dec_l)
        dlf_l = every(lambda db: _dot(trif, db, TN, precision=HIGHEST), db_l)
        dk_l = every(lambda dke, e1, dkd, e2: dke * e1 + dkd * e2, dke_l, e_km_l, dkd_l, e_kl_l)
        df_l = every(lambda dlf, f, dk: dlf / f - dk, dlf_l, f_l, dk_l)
        d_lb = jnp.zeros((1, HG_DK), F32)
        d_go = jnp.zeros((1, HG_DK), F32)
        for j, rows in enumerate(rows_l):
            sg, hq = sg_l[j], hq_l[j]
            df_ref[rows, :] = (df_l[j] * (1.0 - lb) * sg * (1.0 - sg)).astype(df_ref.dtype)
            sq = jax.nn.sigmoid(hq)
            dq = dqe_l[j] * e_qm_l[j] + dqb_l[j] * e_q_l[j]
            dq_ref[rows, :] = (dq * (sq * (1.0 + hq * (1.0 - sq)))).astype(dq_ref.dtype)
            di_ref[rows, :] = dv_l[j].astype(di_ref.dtype)
            dg_ref[rows, :] = vj_l[j][1].astype(dg_ref.dtype)
            d_lb += jnp.sum(df_l[j] * (1.0 - sg), axis=0, keepdims=True)
            d_go += vj_l[j][2]
        return dst, d_lb, d_go

    blk = lambda s: pl.BlockSpec((T, HG_DK), functools.partial(lambda h, n, s: (N - 1 - n, s * H + h), s=s))
    oblk = pl.BlockSpec((T, HG_DK), lambda h, n: (N - 1 - n, h))
    vec = pl.BlockSpec((1, HG_DK), lambda h, n: (0, h))
    W = H * HG_DK
    return _call(
        body, [p4, p4, p4, p4, lb_logits, gout, s_all, d_out], name="hgrn_bwd", grid=(H, N),
        out_shape=tuple([jax.ShapeDtypeStruct((L, W), BF16)] * 4 + [jax.ShapeDtypeStruct((1, W), F32), jax.ShapeDtypeStruct((1, HG_DK), F32)]),
        in_specs=[blk(0), blk(1), blk(2), blk(3), pl.BlockSpec((2, HG_DK), lambda h, n: (0, h)),
                  pl.BlockSpec((1, HG_DK), lambda h, n: (0, 0)),
                  pl.BlockSpec((1, GR, HG_DK, HG_DK), lambda h, n: (h, N - 1 - n, 0, 0)), oblk],
        out_specs=(oblk, oblk, oblk, oblk, vec, pl.BlockSpec((1, HG_DK), lambda h, n: (0, 0))),
        scratch_shapes=[pltpu.VMEM((HG_DK, HG_DK), F32)], comm=comm)


def _bucket_ids():
    i = jnp.arange(AT_BLOCK, dtype=jnp.int32)[:, None]
    j = jnp.arange(2 * AT_BLOCK, dtype=jnp.int32)[None, :]
    n = jnp.maximum(i - j + AT_BLOCK, 0)
    nf = jnp.maximum(n, 1).astype(F32)
    large = MAX_EXACT + (jnp.log(nf / MAX_EXACT) / math.log(MAX_DISTANCE / MAX_EXACT) * (N_BUCKETS - MAX_EXACT)).astype(jnp.int32)
    large = jnp.minimum(large, N_BUCKETS - 1)
    return jnp.where(n < MAX_EXACT, n, large).reshape(1, -1)


def _onehot(bucket):
    ids = lax.broadcasted_iota(jnp.int32, (N_BUCKETS, bucket.shape[1]), 0)
    return (ids == bucket).astype(F32)


def _attn_probs(qn, kpn, kcn, bias_g, sink, first, scale):
    rows = qn.shape[0]
    i = jnp.bitwise_and(lax.broadcasted_iota(jnp.int32, (rows, AT_BLOCK), 0), AT_BLOCK - 1)
    j = lax.broadcasted_iota(jnp.int32, (rows, AT_BLOCK), 1)
    lp = _bdot(qn, kpn, NT) * scale + bias_g[:, :AT_BLOCK]
    lc = _bdot(qn, kcn, NT) * scale + bias_g[:, AT_BLOCK:]
    lp = jnp.where(jnp.logical_and(j > i, jnp.logical_not(first)), lp, NEG_INF)
    lc = jnp.where(j <= i, lc, NEG_INF)
    m = jnp.maximum(jnp.maximum(jnp.max(lp, axis=-1, keepdims=True), jnp.max(lc, axis=-1, keepdims=True)), sink)
    pp, pc, ps = jnp.exp(lp - m), jnp.exp(lc - m), jnp.exp(sink - m)
    den = jnp.sum(pp, axis=-1, keepdims=True) + jnp.sum(pc, axis=-1, keepdims=True) + ps
    return pp / den, pc / den, ps / den


def _sink_rows(sk_ref, G):
    head = lax.broadcasted_iota(jnp.int32, (G * AT_BLOCK, 1), 0) // AT_BLOCK
    sink = jnp.zeros((G * AT_BLOCK, 1), F32)
    for g in range(G):
        sink = jnp.where(head == g, sk_ref[0, g:g + 1, :], sink)
    return sink


def _attn_fwd(q_t, kp, vp, qg, kg, sinks, bias, KVH, comm=None):
    AH, L, DH = q_t.shape
    G = AH // KVH
    NB = L // AT_BLOCK
    scale = DH ** -0.5

    def body(q_ref, kp_ref, kc_ref, vp_ref, vc_ref, qg_ref, kg_ref, sk_ref, b_ref, o_ref):
        first = pl.program_id(1) == 0
        kpn, kcn = _rms(kp_ref[0], kg_ref[...]), _rms(kc_ref[0], kg_ref[...])
        qn = _rms(q_ref[...].reshape(G * AT_BLOCK, DH), qg_ref[...])
        sink = _sink_rows(sk_ref, G)
        pp, pc, _ = _attn_probs(qn, kpn, kcn, b_ref[...].reshape(G * AT_BLOCK, 2 * AT_BLOCK), sink, first, scale)
        o = _bdot(pp, vp_ref[0], NN) + _bdot(pc, vc_ref[0], NN)
        o_ref[...] = o.reshape(G, AT_BLOCK, DH).astype(o_ref.dtype)

    kblk = lambda off: pl.BlockSpec((1, AT_BLOCK, DH),
                                    functools.partial(lambda h, n, off: (h, jnp.maximum(n + off - 1, 0), 0), off=off))
    return _call(
        body, [q_t, kp, kp, vp, vp, qg, kg, sinks, bias], name="attn_fwd", grid=(KVH, NB),
        out_shape=jax.ShapeDtypeStruct((AH, L, DH), BF16),
        in_specs=[pl.BlockSpec((G, AT_BLOCK, DH), lambda h, n: (h, n, 0)), kblk(0), kblk(1), kblk(0), kblk(1),
                  pl.BlockSpec((1, DH), lambda h, n: (0, 0)), pl.BlockSpec((1, DH), lambda h, n: (0, 0)),
                  pl.BlockSpec((1, G, 1), lambda h, n: (h, 0, 0)),
                  pl.BlockSpec((G, AT_BLOCK, 2 * AT_BLOCK), lambda h, n: (h, 0, 0))],
        out_specs=pl.BlockSpec((G, AT_BLOCK, DH), lambda h, n: (h, n, 0)), comm=comm)


def _attn_bwd(q_t, kp, vp, qg, kg, sinks, bias, do_t, KVH, comm=None):
    AH, L, DH = q_t.shape
    G = AH // KVH
    NB = L // AT_BLOCK
    B = AT_BLOCK
    scale = DH ** -0.5

    def body(q_ref, kp_ref, kc_ref, vp_ref, vc_ref, qg_ref, kg_ref, sk_ref, b_ref, do_ref,
             dq_ref, dk_ref, dv_ref, dqg_ref, dkg_ref, dsk_ref, db_ref):
        n = pl.program_id(1)
        first = n == 0

        @pl.when(first)
        def _():
            for r in (dk_ref, dv_ref, dsk_ref, db_ref):
                r[...] = jnp.zeros_like(r)

        @pl.when(jnp.logical_and(first, pl.program_id(0) == 0))
        def _():
            dqg_ref[...] = jnp.zeros_like(dqg_ref)
            dkg_ref[...] = jnp.zeros_like(dkg_ref)

        kp_raw, kc_raw, kgv, qgv = kp_ref[0], kc_ref[0], kg_ref[...], qg_ref[...]
        kpn, kp_vjp = jax.vjp(_rms, kp_raw, kgv)
        kcn, kc_vjp = jax.vjp(_rms, kc_raw, kgv)
        qn, q_vjp = jax.vjp(_rms, q_ref[...].reshape(G * B, DH), qgv)
        pp, pc, ps = _attn_probs(qn, kpn, kcn, b_ref[...].reshape(G * B, 2 * B), _sink_rows(sk_ref, G), first, scale)
        do = do_ref[...].reshape(G * B, DH)
        dvp = _bdot(pp, do, TN)
        dvc = _bdot(pc, do, TN)
        dpp = _bdot(do, vp_ref[0], NT)
        dpc = _bdot(do, vc_ref[0], NT)
        dsum = jnp.sum(dpp * pp, axis=-1, keepdims=True) + jnp.sum(dpc * pc, axis=-1, keepdims=True)
        dlp = pp * (dpp - dsum)
        dlc = pc * (dpc - dsum)
        dsk_ref[0] += jnp.sum((-ps * dsum).reshape(G, B, 1), axis=1)
        db_ref[:, :, :B] += dlp.reshape(G, B, B)
        db_ref[:, :, B:] += dlc.reshape(G, B, B)
        dlp, dlc = dlp * scale, dlc * scale
        dqn = _bdot(dlp, kpn, NN) + _bdot(dlc, kcn, NN)
        dq_raw, dqg = q_vjp(dqn)
        dq_ref[...] = dq_raw.reshape(G, B, DH).astype(dq_ref.dtype)
        dkp_raw, dkg_p = kp_vjp(_bdot(dlp, qn, TN))
        dkc_raw, dkg_c = kc_vjp(_bdot(dlc, qn, TN))
        r0 = pl.multiple_of(jnp.maximum(n - 1, 0) * B, B)
        r1 = pl.multiple_of(n * B, B)
        dk_ref[0, pl.ds(r0, B), :] += dkp_raw
        dk_ref[0, pl.ds(r1, B), :] += dkc_raw
        dv_ref[0, pl.ds(r0, B), :] += dvp
        dv_ref[0, pl.ds(r1, B), :] += dvc
        dqg_ref[...] += dqg
        dkg_ref[...] += dkg_p + dkg_c

    kblk = lambda off: pl.BlockSpec((1, B, DH), functools.partial(lambda h, n, off: (h, jnp.maximum(n + off - 1, 0), 0), off=off))
    qblk = pl.BlockSpec((G, B, DH), lambda h, n: (h, n, 0))
    accblk = pl.BlockSpec((1, L, DH), lambda h, n: (h, 0, 0))
    vecblk = pl.BlockSpec((1, DH), lambda h, n: (0, 0))
    return _call(
        body, [q_t, kp, kp, vp, vp, qg, kg, sinks, bias, do_t], name="attn_bwd", grid=(KVH, NB),
        out_shape=(jax.ShapeDtypeStruct((AH, L, DH), BF16), jax.ShapeDtypeStruct((KVH, L, DH), F32),
                   jax.ShapeDtypeStruct((KVH, L, DH), F32), jax.ShapeDtypeStruct((1, DH), F32),
                   jax.ShapeDtypeStruct((1, DH), F32), jax.ShapeDtypeStruct((KVH, G, 1), F32),
                   jax.ShapeDtypeStruct((AH, B, 2 * B), F32)),
        in_specs=[qblk, kblk(0), kblk(1), kblk(0), kblk(1),
                  pl.BlockSpec((1, DH), lambda h, n: (0, 0)), pl.BlockSpec((1, DH), lambda h, n: (0, 0)),
                  pl.BlockSpec((1, G, 1), lambda h, n: (h, 0, 0)),
                  pl.BlockSpec((G, B, 2 * B), lambda h, n: (h, 0, 0)), qblk],
        out_specs=(qblk, accblk, accblk, vecblk, vecblk, pl.BlockSpec((1, G, 1), lambda h, n: (h, 0, 0)),
                   pl.BlockSpec((G, B, 2 * B), lambda h, n: (h, 0, 0))), comm=comm)


def _heads_first(t, nh):
    L = t.shape[0]
    return jnp.transpose(t.reshape(L, nh, t.shape[1] // nh), (1, 0, 2))


def _heads_last(t):
    nh, L, dh = t.shape
    return jnp.transpose(t, (1, 0, 2)).reshape(L, nh * dh)


def _softmax0(lg):
    e = jnp.exp(lg - jnp.max(lg, axis=0, keepdims=True))
    return e[0:1] / jnp.sum(e, axis=0, keepdims=True)


def _ada_update_call(fn, c_all, d_cols, w, m, v, rt):
    D, n = w.shape

    def body(c_ref, d_ref, w_ref, m_ref, v_ref, g_out, dl_out, m_out, v_out):
        outs, _ = fn(c_ref[...], d_ref[...], w_ref[...], m_ref[...], v_ref[...])
        for r, val in zip((g_out, dl_out, m_out, v_out), outs):
            r[...] = val

    wblk = pl.BlockSpec((rt, n), lambda i: (i, 0))
    return _call(
        body, [c_all, d_cols, w, m, v], name="update_ada", grid=(D // rt,), out_shape=tuple([jax.ShapeDtypeStruct((D, n), F32)] * 4),
        in_specs=[pl.BlockSpec((N_DEV, rt), lambda i: (0, i)), pl.BlockSpec((N_DEV, n), lambda i: (0, 0)), wblk, wblk, wblk],
        out_specs=(wblk, wblk, wblk, wblk))


def kernel(x, c, w_ada, b_ada, norm1_g, norm2_g, w_in, hg_lb_logits, hg_out_norm_g, q_norm_g, k_norm_g, attn_sinks, rel_bias_table, w_branch_hg, w_branch_attn, w_out, w_ff1, w_ff2, loss_target, m_w_ada, m_b_ada, m_norm1_g, m_norm2_g, m_w_in, m_hg_lb_logits, m_hg_out_norm_g, m_q_norm_g, m_k_norm_g, m_attn_sinks, m_rel_bias_table, m_w_branch_hg, m_w_branch_attn, m_w_out, m_w_ff1, m_w_ff2, v_w_ada, v_b_ada, v_norm1_g, v_norm2_g, v_w_in, v_hg_lb_logits, v_hg_out_norm_g, v_q_norm_g, v_k_norm_g, v_attn_sinks, v_rel_bias_table, v_w_branch_hg, v_w_branch_attn, v_w_out, v_w_ff1, v_w_ff2):
    cc = lax.axis_index("c")
    me = 4 * lax.axis_index("x") + 2 * lax.axis_index("y") + cc
    x2 = x[0]
    tgt = loss_target[0]
    L, D = x2.shape
    HGW = hg_lb_logits.shape[1]
    H = HGW // HG_DK
    AH = attn_sinks.shape[1]
    DH = q_norm_g.shape[1]
    ATW = AH * DH
    BW = w_in.shape[2]
    INW = BW * N_DEV
    A = BW // LANES
    assert BW == LANES * A + LANES // 2
    KVW = (INW - 4 * HGW - ATW - 2 * D) // 2
    KVH = KVW // DH
    G = AH // KVH
    ADA_N = w_ada.shape[2]
    PAIR = 2 * A + 1

    c_all = _gather_small(c, me, "gather_c")[:, 0, :]
    b_cols = lax.dynamic_slice(b_ada, (0, me * ADA_N), (1, ADA_N))
    (ada_cols,) = _whole(lambda cv, w, b: (_bdot(_silu(cv), w, NN) + b,), [c_all, w_ada[0], b_cols],
                         [((N_DEV, ADA_N), F32)], "ada_fwd")
    ada_all = _gather_small(ada_cols, me, "gather_ada")
    ada_row = lax.dynamic_slice(ada_all, (0, me, 0), (N_DEV, 1, ADA_N)).reshape(1, 6 * D)

    w_in_b = w_in[0].astype(BF16)
    src_in = jnp.where(cc == 0, jnp.pad(w_in_b, ((0, 0), (0, LANES // 2))), jnp.pad(w_in_b, ((0, 0), (LANES // 2, 0))))
    (src_in,) = _behind([src_in], [ada_row])
    shift1, scale1, gate1, shift2, scale2, gate2 = [ada_row[:, i * D:(i + 1) * D] for i in range(6)]
    w_in_gapped, w_in_mid = _ag_w_in(src_in, A, D, INW)
    w_in_full = _patch_mid(w_in_gapped, w_in_mid, A)

    wnames = ("bhg", "bat", "out", "ff1", "ff2")
    small = ("bhg", "bat", "out")
    waxis = dict(zip(wnames, (1, 1, 0, 1, 0)))
    wsrc = dict(zip(wnames, (w_branch_hg, w_branch_attn, w_out, w_ff1, w_ff2)))
    wblk = {k: wsrc[k][0].astype(BF16) for k in wnames}
    wf = {}

    (h,) = _rowwise(lambda xv, g, sh, sc: ((_modnorm(xv, g, sh, sc),), ()), [(x2, D, 0)], [norm1_g, shift1, scale1],
                    [(D, BF16)], [], "norm1")
    o4, oa = 4 * HGW, 4 * HGW + ATW + 2 * KVW
    r1, r2 = wblk["ff1"].shape[0], wblk["ff2"].shape[0]
    assert o4 % D == 0

    def proj_order(tn_, j):
        t4, tg, ng = o4 // tn_, oa // tn_, (INW - oa) // tn_
        return jnp.where(j < t4, j, jnp.where(j < t4 + ng, j + (tg - t4), j - ng))

    cm = _Comm()
    hs = {k: _ag_ici(cm, wblk[k], waxis[k]) for k in small}
    hs["ff2"] = _ag_ici(cm, wblk["ff2"], waxis["ff2"], rows=(0, r2 // 4))
    proj = _mm(h, w_in_full, "nn", F32, "proj", comm=cm, b_order=proj_order)
    half = {k: cm.result(hs[k]) for k in hs}
    p4 = pg = proj
    GATE0 = o4 // D
    pa = proj[:, o4 + (INW - oa):]

    cm = _Comm()
    hs = {k: _ag_d2d(cm, half[k], waxis[k]) for k in small}
    hs["ff1"] = _ag_ici(cm, wblk["ff1"], waxis["ff1"], rows=(0, r1 // 2))
    o_hg, s_all = _hgrn_fwd(p4, hg_lb_logits, hg_out_norm_g, H, comm=cm)
    wf["bhg"], wf["bat"], wf["out"], half["ff1"] = (cm.result(hs[k]) for k in ("bhg", "bat", "out", "ff1"))

    bucket = _bucket_ids()
    (bias_flat,) = _whole(lambda tb, bk: (_dot(tb, _onehot(bk), TN, precision=HIGHEST),), [rel_bias_table, bucket],
                          [((AH, AT_BLOCK * 2 * AT_BLOCK), F32)], "bias_fwd")
    bias = bias_flat.reshape(AH, AT_BLOCK, 2 * AT_BLOCK)
    q_t = _heads_first(pa[:, :ATW], AH)
    kp = _heads_first(pa[:, ATW:ATW + KVW], KVH)
    vp = _heads_first(pa[:, ATW + KVW:], KVH)
    sinks3 = attn_sinks.reshape(KVH, G, 1)
    cm = _Comm()
    hs = {"ff1": _ag_ici(cm, wblk["ff1"], waxis["ff1"], rows=(r1 // 2, r1), into=half["ff1"])}
    o_at = _heads_last(_attn_fwd(q_t, kp, vp, q_norm_g, k_norm_g, sinks3, bias, KVH, comm=cm))
    half["ff1"] = cm.result(hs["ff1"])

    bh = _mm(o_hg, wf["bhg"], "nn", F32, "branch_hg")
    ba = _mm(o_at, wf["bat"], "nn", F32, "branch_at")

    def merge_fn(bhv, bav, ghg, gat):
        return jax.nn.sigmoid(ghg) * bhv + jax.nn.sigmoid(gat) * bav

    cm = _Comm()
    hs = {"ff1": _ag_d2d(cm, half["ff1"], waxis["ff1"])}
    (merged,) = _rowwise(lambda *a: ((merge_fn(*a),), ()), [(bh, D, 0), (ba, D, 0), (pg, D, GATE0), (pg, D, GATE0 + 1)], [],
                         [(D, BF16)], [], "merge", comm=cm)
    wf["ff1"] = cm.result(hs["ff1"])
    cm = _Comm()
    hs = {"ff2": _ag_ici(cm, wblk["ff2"], waxis["ff2"], rows=(r2 // 4, 3 * r2 // 8), into=half["ff2"])}
    mo = _mm(merged, wf["out"], "nn", F32, "out_proj", comm=cm)
    half["ff2"] = cm.result(hs["ff2"])

    def resid1(xv, mov, g1, g2n, sh, sc):
        x1v = xv + g1 * mov
        return (x1v, _modnorm(x1v, g2n, sh, sc)), ()

    cm = _Comm()
    hs = {"ff2": _ag_ici(cm, wblk["ff2"], waxis["ff2"], rows=(3 * r2 // 8, r2 // 2), into=half["ff2"])}
    x1, h2 = _rowwise(resid1, [(x2, D, 0), (mo, D, 0)], [gate1, norm2_g, shift2, scale2], [(D, F32), (D, BF16)], [], "resid1",
                      comm=cm)
    half["ff2"] = cm.result(hs["ff2"])
    cm = _Comm()
    hs = {"ff2": _ag_ici(cm, wblk["ff2"], waxis["ff2"], rows=(r2 // 2, r2), into=half["ff2"])}
    u, act = _mm(h2, wf["ff1"], "nn", (F32, BF16), "ff1", comm=cm, epi=lambda r: (r, jnp.square(jnp.maximum(r, 0.0))))
    half["ff2"] = cm.result(hs["ff2"])
    cm = _Comm()
    hs = {"ff2": _ag_d2d(cm, half["ff2"], waxis["ff2"])}
    _call(lambda: None, [], name="ag_d2d_ff2", out_shape=(), comm=cm)
    wf["ff2"] = cm.result(hs["ff2"])
    ff = _mm(act, wf["ff2"], "nn", F32, "ff2")

    def loss_fn(x1v, ffv, tv, g2):
        e = x1v + g2 * ffv - tv
        dy = e * (1.0 / D)
        return (dy, dy * g2), (jnp.sum(e * e, axis=0, keepdims=True), jnp.sum(dy * ffv, axis=0, keepdims=True))

    dy, d_ff, sq_sum, d_gate2 = _rowwise(loss_fn, [(x1, D, 0), (ff, D, 0), (tgt, D, 0)], [gate2],
                                         [(D, F32), (D, BF16)], [(1, D), (1, D)], "loss")
    loss = lax.psum(jnp.sum(sq_sum) * (0.5 / D), ("x", "y", "c"))

    owner_base = jnp.stack([me ^ r for r in CHIP_RELS]).astype(jnp.int32)
    gw, recv1, part, recv2 = {}, {}, {}, {}
    gw["ff2"] = _mm(act, d_ff, "tn", BF16, "dw_ff2")
    cm = _Comm()
    hh = _rs_d2d(cm, gw["ff2"], waxis["ff2"])
    d_u = _mm(d_ff, wf["ff2"], "nt", BF16, "d_act", comm=cm, extras=[u], epi=lambda r, uv: (r * (2.0 * jnp.maximum(uv, 0.0)),))
    part["ff2"] = _rs_add(gw["ff2"], cm.result(hh), waxis["ff2"], owner_base, "rs_add_ff2")
    rows_ff2 = part["ff2"].shape[1]
    cm = _Comm()
    hh = _rs_ici(cm, part["ff2"], rows=(0, rows_ff2 // 2))
    gw["ff1"] = _mm(h2, d_u, "tn", BF16, "dw_ff1", comm=cm)
    cm2 = _Comm()
    hh2 = _rs_ici(cm2, part["ff2"], rows=(rows_ff2 // 2, rows_ff2), recv=cm.result(hh))
    hh1 = _rs_d2d(cm2, gw["ff1"], waxis["ff1"])
    d_h2 = _mm(d_u, wf["ff1"], "nt", F32, "d_h2", comm=cm2)
    recv2["ff2"] = cm2.result(hh2)
    part["ff1"] = _rs_add(gw["ff1"], cm2.result(hh1), waxis["ff1"], owner_base, "rs_add_ff1")

    def norm2_bwd(dh2v, x1v, dyv, mov, g2n, sh, sc, g1):
        _, vjp = jax.vjp(_modnorm, x1v, g2n, sh, sc)
        dx, dg, dsh, dsc = vjp(dh2v)
        dx1 = dyv + dx
        return (dx1, dx1 * g1), (dg, dsh, dsc, jnp.sum(dx1 * mov, axis=0, keepdims=True))

    d_x1, d_mo, d_g2n, d_shift2, d_scale2, d_gate1 = _rowwise(
        norm2_bwd, [(d_h2, D, 0), (x1, D, 0), (dy, D, 0), (mo, D, 0)], [norm2_g, shift2, scale2, gate1],
        [(D, F32), (D, BF16)], [(1, D)] * 4, "norm2_bwd")
    gw["out"] = _mm(merged, d_mo, "tn", BF16, "dw_out")
    d_merged = _mm(d_mo, wf["out"], "nt", F32, "d_merged")

    def merge_bwd(dmv, bhv, bav, ghg, gat):
        _, vjp = jax.vjp(merge_fn, bhv, bav, ghg, gat)
        return vjp(dmv), ()

    d_bh, d_ba, d_ghg, d_gat = _rowwise(merge_bwd, [(d_merged, D, 0), (bh, D, 0), (ba, D, 0), (pg, D, GATE0), (pg, D, GATE0 + 1)], [],
                                        [(D, BF16)] * 4, [], "merge_bwd")
    gw["bhg"] = _mm(o_hg, d_bh, "tn", BF16, "dw_bhg")
    gw["bat"] = _mm(o_at, d_ba, "tn", BF16, "dw_bat")
    d_ohg = _mm(d_bh, wf["bhg"], "nt", F32, "d_ohg")
    d_oat = _mm(d_ba, wf["bat"], "nt", BF16, "d_oat")
    rows_ff1 = part["ff1"].shape[1]
    cut_ff1 = 3 * rows_ff1 // 8
    cm = _Comm()
    hf1 = _rs_ici(cm, part["ff1"], rows=(0, cut_ff1))
    d_hq, d_hf, d_hi, d_hg, d_lb, d_gout_h = _hgrn_bwd(p4, hg_lb_logits, hg_out_norm_g, s_all, d_ohg, H, comm=cm)
    cm2 = _Comm()
    hf1 = _rs_ici(cm2, part["ff1"], rows=(cut_ff1, rows_ff1), recv=cm.result(hf1))
    hh = {k: _rs_d2d(cm2, gw[k], waxis[k]) for k in small}
    dq_t, dkp, dvp, d_qg, d_kg, d_sk, d_bias = _attn_bwd(q_t, kp, vp, q_norm_g, k_norm_g, sinks3, bias,
                                                         _heads_first(d_oat, AH), KVH, comm=cm2)
    recv2["ff1"] = cm2.result(hf1)
    for k in small:
        part[k] = _rs_add(gw[k], cm2.result(hh[k]), waxis[k], owner_base, "rs_add_" + k)
    d_aq = _heads_last(dq_t)
    d_ak = _heads_last(dkp).astype(BF16)
    d_av = _heads_last(dvp).astype(BF16)
    d_proj = jnp.concatenate([d_hq, d_hf, d_hi, d_hg, d_aq, d_ak, d_av, d_ghg, d_gat], axis=1)
    cm = _Comm()
    hh = {k: _rs_ici(cm, part[k]) for k in small}
    gw_in = _mm(h, d_proj, "tn", BF16, "dw_in", comm=cm)
    for k in small:
        recv2[k] = cm.result(hh[k])

    wm = LANES * A
    cm = _Comm()
    hi_ = cm.inp(gw_in)
    h_main, h_mid = cm.out((4, D, wm), BF16), cm.out((4, D, LANES), BF16)
    for i, r in enumerate(CHIP_RELS):
        def main_view(ref, p, r=r):
            o = p["me"] ^ r ^ 1
            return ref.at[:, pl.ds(pl.multiple_of((PAIR * (o // 2) + (A + 1) * (1 - p["c"])) * LANES, LANES), wm)]

        def mid_view(ref, p, r=r):
            o = p["me"] ^ r
            return ref.at[:, pl.ds(pl.multiple_of((PAIR * (o // 2) + A) * LANES, LANES), LANES)]

        cm.copy(hi_, main_view, h_main, _slot_view(i), 1)
        cm.copy(hi_, mid_view, h_mid, _slot_view(i), 1)
    _call(lambda: None, [], name="rs_d2d_in", out_shape=(), comm=cm)
    chip = jnp.stack([(me ^ r) // 2 for r in CHIP_RELS]).astype(jnp.int32)
    part_main = _rs_add(gw_in, cm.result(h_main), 1, PAIR * chip + (A + 1) * cc, "rs_add_in_main", tw=LANES)
    part_mid = _rs_add(gw_in, cm.result(h_mid), 1, PAIR * chip + A, "rs_add_in_mid", tw=LANES)
    rs_in = _rs_split_start([part_main, part_mid], "rs_in_start")
    d_h = _mm(d_proj, w_in_full, "nt", F32, "d_h", tn=D, after=[rs_in["token"]])

    def norm1_bwd(dhv, xv, dx1v, g1n, sh, sc):
        _, vjp = jax.vjp(_modnorm, xv, g1n, sh, sc)
        dx, dg, dsh, dsc = vjp(dhv)
        return (dx1v + dx,), (dg, dsh, dsc)

    grad_x, d_g1n, d_shift1, d_scale1 = _rowwise(norm1_bwd, [(d_h, D, 0), (x2, D, 0), (d_x1, D, 0)],
                                                 [norm1_g, shift1, scale1], [(D, F32)], [(1, D)] * 3, "norm1_bwd")

    def sum4(p0, p1, p2, p3):
        return ((p0.astype(F32) + p1.astype(F32)) + p2.astype(F32)) + p3.astype(F32)

    def update_fn(w, m, v, p0, p1, p2, p3):
        g = sum4(p0, p1, p2, p3)
        delta, mn, vn = _adamw(w, g, m, v)
        return (g, delta, mn, vn), ()

    wmv = dict(zip(wnames, ((w_branch_hg, m_w_branch_hg, v_w_branch_hg), (w_branch_attn, m_w_branch_attn, v_w_branch_attn),
                            (w_out, m_w_out, v_w_out), (w_ff1, m_w_ff1, v_w_ff1), (w_ff2, m_w_ff2, v_w_ff2))))
    res = {}

    def update(k, p, rx):
        w, m, v = (t[0] for t in wmv[k])
        n = w.shape[1]
        ins = [(t, n, 0) for t in (w, m, v)] + [(p, n, 0, 0)] + [(rx, n, 0, i) for i in range(3)]
        res[k] = [t[None] for t in _rowwise(update_fn, ins, [], [(n, F32)] * 4, [], "update_" + k)]

    for k in wnames:
        update(k, part[k], recv2[k])
    (part_main, part_mid), (rx_main, rx_mid) = _rs_split_wait(rs_in, [grad_x] + [res[k][0] for k in wnames], "rs_in_wait")
    g_main, = _rowwise(lambda *p: ((sum4(*p),), ()), [(part_main, wm, 0, 0)] + [(rx_main, wm, 0, i) for i in range(3)], [],
                       [(wm, F32)], [], "sum_in_main")
    g_mid, = _rowwise(lambda *p: ((sum4(*p),), ()), [(part_mid, LANES, 0, 0)] + [(rx_mid, LANES, 0, i) for i in range(3)], [],
                      [(LANES, F32)], [], "sum_in_mid")
    g_in = jnp.where(cc == 0, jnp.concatenate([g_main, g_mid[:, :LANES // 2]], axis=1),
                     jnp.concatenate([g_mid[:, LANES // 2:], g_main], axis=1))

    def update_given(w, m, v, g):
        delta, mn, vn = _adamw(w, g, m, v)
        return (g, delta, mn, vn), ()

    res["in"] = [t[None] for t in _rowwise(update_given, [(t, BW, 0) for t in (w_in[0], m_w_in[0], v_w_in[0], g_in)], [],
                                           [(BW, F32)] * 4, [], "update_in")]

    d_sinks = d_sk.reshape(1, AH)
    (d_table_t,) = _whole(lambda db, bk: (_dot(db, _onehot(bk), NT, precision=HIGHEST),),
                          [d_bias.reshape(AH, AT_BLOCK * 2 * AT_BLOCK), bucket], [((AH, N_BUCKETS), F32)], "bias_bwd")
    smalls = [d_g1n, d_g2n, d_lb, d_gout_h, d_qg, d_kg, d_sinks, d_table_t.T.reshape(1, N_BUCKETS * AH)]
    widths = [s.shape[1] for s in smalls]
    lanes = [-(-w // LANES) * LANES for w in widths]
    smalls = [jnp.pad(s, ((0, 0), (0, p - w))) for s, w, p in zip(smalls, widths, lanes)]
    tail_row = jnp.concatenate([d_shift1, d_scale1, d_gate1, d_shift2, d_scale2, d_gate2] + smalls, axis=1)
    (tail_row,) = _behind([tail_row], [g_mid])
    tail_all = _gather_small(tail_row, me, "gather_tail")[:, 0, :]
    d_ada_all, packed = tail_all[:, :6 * D], tail_all[:, 6 * D:]
    d_ada_cols = lax.dynamic_slice(d_ada_all, (0, me * ADA_N), (N_DEV, ADA_N))

    def ada_update(cv, dav, w, m, v):
        g = _bdot(_silu(cv), dav, TN)
        delta, mn, vn = _adamw(w, g, m, v)
        return (g, delta, mn, vn), ()

    res["ada"] = [t[None] for t in _ada_update_call(ada_update, c_all, d_ada_cols, w_ada[0], m_w_ada[0], v_w_ada[0], _tile(D, 256, 16))]

    offs = [sum(lanes[:i]) for i in range(len(lanes))]

    def small_update(pk, dada, lg, *wmv_flat):
        tot = pk[0:1]
        for d in range(1, N_DEV):
            tot = tot + pk[d:d + 1]
        gb = dada[0:1]
        for d in range(1, N_DEV):
            gb = gb + dada[d:d + 1]
        gs = [tot[:, offs[i]:offs[i] + widths[i]] for i in range(len(widths))]
        _, lb_vjp = jax.vjp(_softmax0, lg)
        (g_lg,) = lb_vjp(gs[2])
        grads = [gb, gs[0], gs[1], g_lg, gs[3], gs[4], gs[5], gs[6], gs[7]]
        outs = []
        for i, g in enumerate(grads):
            w, m, v = wmv_flat[3 * i:3 * i + 3]
            delta, mn, vn = _adamw(w, g, m, v)
            outs += [g, delta, mn, vn]
        return tuple(outs)

    tbl = lambda t: t.reshape(1, N_BUCKETS * AH)
    small_wmv = [(b_ada, m_b_ada, v_b_ada), (norm1_g, m_norm1_g, v_norm1_g), (norm2_g, m_norm2_g, v_norm2_g),
                 (hg_lb_logits, m_hg_lb_logits, v_hg_lb_logits), (hg_out_norm_g, m_hg_out_norm_g, v_hg_out_norm_g),
                 (q_norm_g, m_q_norm_g, v_q_norm_g), (k_norm_g, m_k_norm_g, v_k_norm_g),
                 (attn_sinks, m_attn_sinks, v_attn_sinks),
                 (tbl(rel_bias_table), tbl(m_rel_bias_table), tbl(v_rel_bias_table))]
    flat = [t for trip in small_wmv for t in trip]
    out_shapes = [(trip[0].shape, F32) for trip in small_wmv for _ in range(4)]
    sres = _whole(small_update, [packed, d_ada_all, hg_lb_logits] + flat, out_shapes, "small_update")
    names_small = ("b_ada", "norm1_g", "norm2_g", "lb", "gout", "qg", "kg", "sinks", "table")
    for i, k in enumerate(names_small):
        r = sres[4 * i:4 * i + 4]
        if k == "table":
            r = [t.reshape(N_BUCKETS, AH) for t in r]
        res[k] = r

    order = ("ada", "b_ada", "norm1_g", "norm2_g", "in", "lb", "gout", "qg", "kg", "sinks", "table", "bhg", "bat", "out", "ff1", "ff2")
    outs = [loss, grad_x[None]]
    for j in range(4):
        outs += [res[k][j] for k in order]
    return tuple(outs)
```

```python
import functools
import math

import jax
import jax.numpy as jnp
from jax import lax
from jax.experimental import pallas as pl
from jax.experimental.pallas import tpu as pltpu

F32 = jnp.float32
BF16 = jnp.bfloat16
EPS = 1e-6
NEG_INF = -1e30
HG_DK = 128
HG_CHUNK = 64
AT_BLOCK = 128
N_BUCKETS = 32
MAX_EXACT = 16
MAX_DISTANCE = 128
N_DEV = 8
LANES = 128
VMEM_LIMIT = 56 * 1024 * 1024
ADAM_LR, ADAM_B1, ADAM_B2, ADAM_EPS, ADAM_WD, ADAM_STEP = 0.001, 0.9, 0.999, 1e-08, 0.01, 10
HIGHEST = lax.Precision.HIGHEST
MESH = pl.DeviceIdType.MESH
ANY = pl.BlockSpec(memory_space=pl.ANY)
CHIP_RELS = (0, 4, 2, 6)

NN = (((1,), (0,)), ((), ()))
NT = (((1,), (1,)), ((), ()))
TN = (((0,), (0,)), ((), ()))


def _tile(n, pref, unit):
    if n <= pref:
        return n
    t = (pref // unit) * unit
    while t >= unit:
        if n % t == 0:
            return t
        t -= unit
    return n


def _dot(a, b, dn, precision=None):
    return lax.dot_general(a, b, dn, preferred_element_type=F32, precision=precision)


def _bdot(a, b, dn):
    return _dot(a.astype(BF16), b.astype(BF16), dn)


def _position():
    x, y, c = lax.axis_index("x"), lax.axis_index("y"), lax.axis_index("c")
    return dict(x=x, y=y, c=c, me=4 * x + 2 * y + c)


def _peer_position(p, rel):
    x = 1 - p["x"] if rel & 4 else p["x"]
    y = 1 - p["y"] if rel & 2 else p["y"]
    c = 1 - p["c"] if rel & 1 else p["c"]
    return dict(x=x, y=y, c=c, me=4 * x + 2 * y + c)


class _Comm:
    def __init__(self):
        self.ins, self.outs, self.alias, self.plans, self.res = [], [], {}, [], None

    def inp(self, arr):
        self.ins.append(arr)
        return ("i", len(self.ins) - 1)

    def out(self, shape, dtype, alias=None):
        self.outs.append(jax.ShapeDtypeStruct(tuple(shape), dtype))
        if alias is not None:
            self.alias[alias[1]] = len(self.outs) - 1
        return ("o", len(self.outs) - 1)

    def copy(self, src, src_view, dst, dst_view, rel):
        self.plans.append((src, src_view, dst, dst_view, rel))

    def result(self, handle):
        return self.res[handle[1]]

    def build(self, in_refs, out_refs, send_sems, recv_sems):
        pos = _position()
        ref = lambda h: in_refs[h[1]] if h[0] == "i" else out_refs[h[1]]
        ops = []
        for k, (src, sv, dst, dv, rel) in enumerate(self.plans):
            s = sv(ref(src), pos)
            if rel == 0:
                cp = pltpu.make_async_copy(s, dv(ref(dst), pos), send_sems.at[k])
                ops.append((cp.start, cp.wait))
                continue
            peer = _peer_position(pos, rel)
            mk = lambda d: pltpu.make_async_remote_copy(
                src_ref=s, dst_ref=d, send_sem=send_sems.at[k], recv_sem=recv_sems.at[k],
                device_id=(peer["x"], peer["y"], peer["c"]), device_id_type=MESH)
            out_cp, in_cp = mk(dv(ref(dst), pos)), mk(dv(ref(dst), peer))

            def wait(out_cp=out_cp, in_cp=in_cp):
                out_cp.wait_send()
                in_cp.wait_recv()

            ops.append((out_cp.start, wait))
        return ops


def _call(body, args, *, name, out_shape, in_specs=None, out_specs=None, grid=None, scratch_shapes=(), comm=None,
          prefetch=None, aliases=None, after=()):
    single = not isinstance(out_shape, (tuple, list))
    out_shape = (out_shape,) if single else tuple(out_shape)
    n_in, n_out, n_scr = len(args), len(out_shape), len(scratch_shapes)
    vm = pl.BlockSpec(memory_space=pltpu.VMEM)
    in_specs = [vm] * n_in if in_specs is None else list(in_specs)
    out_specs = [vm] * n_out if out_specs is None else (list(out_specs) if isinstance(out_specs, (tuple, list)) else [out_specs])
    n_pf = 0 if prefetch is None else len(prefetch)
    kw = {} if aliases is None else {"input_output_aliases": dict(aliases)}
    if comm is None and after:
        n_dep = len(after)

        def fn(*refs):
            body(*refs[:n_pf + n_in], *refs[n_pf + n_in + n_dep:])

        all_args, all_scratch = list(args) + list(after), list(scratch_shapes)
        in_specs = in_specs + [ANY] * n_dep
    elif comm is None:
        fn = body
        all_args, all_scratch = list(args), list(scratch_shapes)
    else:
        n_ci, n_co, n_x = len(comm.ins), len(comm.outs), len(comm.plans)

        def fn(*refs):
            pf, refs = refs[:n_pf], refs[n_pf:]
            o_in, c_in = refs[:n_in], refs[n_in:n_in + n_ci]
            o_out = refs[n_in + n_ci:n_in + n_ci + n_out]
            c_out = refs[n_in + n_ci + n_out:n_in + n_ci + n_out + n_co]
            scr = refs[n_in + n_ci + n_out + n_co:]
            ops = comm.build(c_in, c_out, scr[n_scr], scr[n_scr + 1])
            if grid:
                first = functools.reduce(jnp.logical_and, [pl.program_id(i) == 0 for i in range(len(grid))])
                last = functools.reduce(jnp.logical_and, [pl.program_id(i) == g - 1 for i, g in enumerate(grid)])

                @pl.when(first)
                def _():
                    for start, _w in ops:
                        start()
            else:
                for start, _w in ops:
                    start()
            body(*pf, *o_in, *o_out, *scr[:n_scr])
            if grid:
                @pl.when(last)
                def _():
                    for _s, wait in ops:
                        wait()
            else:
                for _s, wait in ops:
                    wait()

        all_args = list(args) + list(comm.ins)
        in_specs = in_specs + [ANY] * n_ci
        out_shape = out_shape + tuple(comm.outs)
        out_specs = out_specs + [ANY] * n_co
        all_scratch = list(scratch_shapes) + [pltpu.SemaphoreType.DMA((n_x,)), pltpu.SemaphoreType.DMA((n_x,))]
        kw["input_output_aliases"] = {n_pf + n_in + i: n_out + o for i, o in comm.alias.items()}
    sem = None if grid is None else ("arbitrary",) * len(grid)
    params = pltpu.CompilerParams(dimension_semantics=sem, vmem_limit_bytes=VMEM_LIMIT)
    if prefetch is None:
        spec = dict(in_specs=in_specs, out_specs=tuple(out_specs), scratch_shapes=all_scratch)
        if grid is not None:
            spec["grid"] = grid
    else:
        spec = dict(grid_spec=pltpu.PrefetchScalarGridSpec(
            num_scalar_prefetch=n_pf, grid=grid, in_specs=in_specs, out_specs=tuple(out_specs), scratch_shapes=all_scratch))
        all_args = list(prefetch) + all_args
    res = pl.pallas_call(fn, name=name, out_shape=out_shape, compiler_params=params, **spec, **kw)(*all_args)
    res = list(res)
    if comm is not None:
        comm.res = res[n_out:]
        res = res[:n_out]
    return res[0] if single else res


def _whole_view(ref, pos):
    return ref


def _block_view(axis, n, index, rows=None):
    def view(ref, pos):
        off = pl.multiple_of(index(pos) * n, n)
        if rows is None:
            return ref.at[:, pl.ds(off, n)] if axis == 1 else ref.at[pl.ds(off, n), :]
        lo, cnt = rows[0], rows[1] - rows[0]
        if axis == 1:
            return ref.at[pl.ds(lo, cnt), pl.ds(off, n)]
        return ref.at[pl.ds(pl.multiple_of(off + lo, 16), cnt), :]
    return view


def _rows_view(rows):
    def view(ref, pos):
        return ref if rows is None else ref.at[pl.ds(rows[0], rows[1] - rows[0]), :]
    return view


def _slot_view(i, rows=None):
    def view(ref, pos):
        return ref.at[i] if rows is None else ref.at[i, pl.ds(rows[0], rows[1] - rows[0]), :]
    return view


def _exchange(items, name):
    cm = _Comm()
    for a, rel in items:
        cm.copy(cm.inp(a), _whole_view, cm.out(a.shape, a.dtype), _whole_view, rel)
    _call(lambda: None, [], name=name, out_shape=(), comm=cm)
    return cm.res


def _gather_small(v, me, name):
    cm = _Comm()
    hi, ho = cm.inp(v), cm.out((N_DEV,) + v.shape, v.dtype)
    for rel in range(N_DEV):
        cm.copy(hi, _whole_view, ho, lambda ref, p: ref.at[p["me"]], rel)
    _call(lambda: None, [], name=name, out_shape=(), comm=cm)
    return cm.result(ho)


def _ag_ici(cm, blk, axis, rows=None, into=None):
    n = blk.shape[axis]
    shape = list(blk.shape)
    shape[axis] = n * N_DEV
    hi = cm.inp(blk)
    ho = cm.out(shape, blk.dtype) if into is None else cm.out(shape, blk.dtype, alias=cm.inp(into))
    own = _block_view(axis, n, lambda p: p["me"], rows)
    for rel in CHIP_RELS:
        cm.copy(hi, _rows_view(rows), ho, own, rel)
    return ho


def _ag_d2d(cm, full, axis):
    n = full.shape[axis] // N_DEV
    hi = cm.inp(full)
    ho = cm.out(full.shape, full.dtype, alias=hi)
    for r in CHIP_RELS:
        v = _block_view(axis, n, functools.partial(lambda p, r: p["me"] ^ r, r=r))
        cm.copy(hi, v, ho, v, 1)
    return ho


def _rs_d2d(cm, gw, axis):
    n = gw.shape[axis] // N_DEV
    shape = list(gw.shape)
    shape[axis] = n
    hi, ho = cm.inp(gw), cm.out([4] + shape, gw.dtype)
    for i, r in enumerate(CHIP_RELS):
        cm.copy(hi, _block_view(axis, n, functools.partial(lambda p, r: p["me"] ^ r ^ 1, r=r)), ho, _slot_view(i), 1)
    return ho


def _rs_ici(cm, part, rows=None, recv=None):
    if recv is None:
        ho = cm.out((3,) + part.shape[1:], part.dtype)
    else:
        ho = cm.out(recv.shape, recv.dtype, alias=cm.inp(recv))
    hi = cm.inp(part)
    for i in (1, 2, 3):
        cm.copy(hi, _slot_view(i, rows), ho, _slot_view(i - 1, rows), CHIP_RELS[i])
    return ho


def _rs_add(gw, recv, axis, base, name, tw=None):
    _, R, n = recv.shape
    fan = 1
    if axis == 1:
        tw = n if tw is None else tw
        fan = max(f for f in (4, 3, 2, 1) if (n // tw) % f == 0)
        gw_specs = [pl.BlockSpec((R, tw), functools.partial(lambda i, t, b, k: (0, b[i] + fan * t + k), k=k)) for k in range(fan)]
        rv_spec = pl.BlockSpec((None, R, tw * fan), lambda i, t, b: (i, 0, t))
        grid = (4, n // (tw * fan))
    else:
        tw = _tile(n, 1024, LANES)
        gw_specs = [pl.BlockSpec((R, tw), lambda i, t, b: (b[i], t))]
        rv_spec = pl.BlockSpec((None, R, tw), lambda i, t, b: (i, 0, t))
        grid = (4, n // tw)

    def body(b_ref, *refs):
        g_refs, r_ref, o_ref = refs[:fan], refs[fan], refs[fan + 1]
        g = g_refs[0][...] if fan == 1 else jnp.concatenate([g[...] for g in g_refs], axis=1)
        o_ref[...] = (g.astype(F32) + r_ref[...].astype(F32)).astype(o_ref.dtype)

    return _call(body, [gw] * fan + [recv], name=name, out_shape=jax.ShapeDtypeStruct(recv.shape, recv.dtype), grid=grid,
                 in_specs=gw_specs + [rv_spec], out_specs=rv_spec, prefetch=[base])


HBM_SPEC = pl.BlockSpec(memory_space=pltpu.HBM)
SEM_SPEC = pl.BlockSpec(memory_space=pltpu.SEMAPHORE)
SPLIT_PARAMS = pltpu.CompilerParams(has_side_effects=pltpu.SideEffectType.DATAFLOW_SIDE_EFFECTING)


def _split_copies(refs, plans, send_sems, recv_sems):
    pos = _position()
    out = []
    for k, (si, sv, li, lv, rel) in enumerate(plans):
        peer = _peer_position(pos, rel)
        mk = lambda d: pltpu.make_async_remote_copy(
            src_ref=sv(refs[si], pos), dst_ref=d, send_sem=send_sems.at[k], recv_sem=recv_sems.at[k],
            device_id=(peer["x"], peer["y"], peer["c"]), device_id_type=MESH)
        out.append((mk(lv(refs[li], pos)), mk(lv(refs[li], peer))))
    return out


def _split_start(arrays, plans, name):
    n = len(arrays)

    def body(*refs):
        send_sems, recv_sems = refs[n], refs[n + 1]
        for out_cp, _ in _split_copies(refs[:n], plans, send_sems, recv_sems):
            out_cp.start()
        refs[-1][...] = jnp.zeros_like(refs[-1])

    sems = pltpu.SemaphoreType.DMA((len(plans),))
    res = pl.pallas_call(
        body, name=name,
        out_shape=(sems, sems) + tuple(pltpu.HBM(a.shape, a.dtype) for a in arrays) + (jax.ShapeDtypeStruct((8, LANES), F32),),
        in_specs=[HBM_SPEC] * n, out_specs=(SEM_SPEC, SEM_SPEC) + (HBM_SPEC,) * n + (pl.BlockSpec(memory_space=pltpu.VMEM),),
        input_output_aliases={i: 2 + i for i in range(n)}, compiler_params=SPLIT_PARAMS,
    )(*[pltpu.with_memory_space_constraint(a, pltpu.HBM) for a in arrays])
    return res[0], res[1], list(res[2:2 + n]), res[-1]


def _split_wait(send_sems, recv_sems, arrays, plans, after, name):
    n, na = len(arrays), len(after)

    def body(*refs):
        for out_cp, in_cp in _split_copies(refs[:n], plans, refs[n], refs[n + 1]):
            out_cp.wait_send()
            in_cp.wait_recv()

    res = pl.pallas_call(
        body, name=name, out_shape=tuple(pltpu.HBM(a.shape, a.dtype) for a in arrays),
        in_specs=[HBM_SPEC] * n + [SEM_SPEC, SEM_SPEC] + [ANY] * na, out_specs=(HBM_SPEC,) * n,
        input_output_aliases={i: i for i in range(n)}, compiler_params=SPLIT_PARAMS,
    )(*arrays, send_sems, recv_sems, *after)
    return list(res)


def _rs_split_start(parts, name):
    nw = len(parts)
    lands = [lax.empty((3,) + p.shape[1:], p.dtype) for p in parts]
    plans = [(s, _slot_view(i), nw + s, _slot_view(i - 1), CHIP_RELS[i]) for s in range(nw) for i in (1, 2, 3)]
    send_sems, recv_sems, arrays, token = _split_start(list(parts) + lands, plans, name)
    return dict(sems=(send_sems, recv_sems), arrays=arrays, plans=plans, token=token, nw=nw)


def _rs_split_wait(h, after, name):
    arrays = _split_wait(h["sems"][0], h["sems"][1], h["arrays"], h["plans"], after, name)
    return arrays[:h["nw"]], arrays[h["nw"]:]


def _behind(xs, tokens):
    out = lax.optimization_barrier((tuple(xs), tuple(tokens)))
    return list(out[0])


def _ag_w_in(src, small, a, D, INW):
    wm = LANES * a

    hd = D // 2
    ALL, TOP, BOT = (0, D), (0, hd), (hd, D)

    def main_place(ref, p, rows=ALL):
        off = pl.multiple_of(((2 * a + 1) * (p["me"] // 2) + (a + 1) * p["c"]) * LANES, LANES)
        return ref.at[pl.ds(rows[0], rows[1] - rows[0]), pl.ds(off, wm)]

    def main_src(ref, p):
        return ref.at[:, pl.ds(pl.multiple_of(p["c"] * LANES, LANES), wm)]

    def mid_src(ref, p):
        return ref.at[:, pl.ds(pl.multiple_of((1 - p["c"]) * wm, LANES), LANES)]

    def mid_place(ref, p, rows=ALL):
        return ref.at[p["me"], pl.ds(rows[0], rows[1] - rows[0]), :]

    def body(src_ref, small_ref, full_ref, mid_ref, all_ref, send_sems, recv_sems):
        pos = _position()
        sib, xn, yn = (_peer_position(pos, r) for r in (1, 4, 2))
        dg = _peer_position(pos, 6)
        started = []

        def remote(k, s, d, to):
            return pltpu.make_async_remote_copy(src_ref=s, dst_ref=d, send_sem=send_sems.at[k], recv_sem=recv_sems.at[k],
                                                device_id=(to["x"], to["y"], to["c"]), device_id_type=MESH)

        small_local = pltpu.make_async_copy(small_ref, all_ref.at[pos["me"]], send_sems.at[20])
        small_local.start()
        small_cps = []
        for rel in range(1, N_DEV):
            to = _peer_position(pos, rel)
            cp = remote(20 + rel, small_ref, all_ref.at[pos["me"]], to)
            cp.start()
            started.append(cp)
            small_cps.append(remote(20 + rel, small_ref, all_ref.at[to["me"]], to))

        def send(k, owner, rows, to, from_src=False):
            for j, (src_v, place) in enumerate(((main_src, main_place), (mid_src, mid_place))):
                s = src_v(src_ref, pos) if from_src else place(full_ref if j == 0 else mid_ref, owner, rows)
                cp = remote(k + j, s, place(full_ref if j == 0 else mid_ref, owner, rows), to)
                cp.start()
                started.append(cp)

        def landed(k, owner, rows, frm):
            for j, place in enumerate((main_place, mid_place)):
                ref = full_ref if j == 0 else mid_ref
                remote(k + j, place(ref, owner, rows), place(ref, owner, rows), frm).wait_recv()

        local = [pltpu.make_async_copy(main_src(src_ref, pos), main_place(full_ref, pos), send_sems.at[18]),
                 pltpu.make_async_copy(mid_src(src_ref, pos), mid_place(mid_ref, pos), send_sems.at[19])]
        for cp in local:
            cp.start()
        send(0, pos, ALL, sib, from_src=True)
        send(2, pos, ALL, xn, from_src=True)
        send(4, pos, ALL, yn, from_src=True)
        landed(2, xn, ALL, xn)
        send(10, xn, ALL, sib)
        send(6, xn, TOP, yn)
        landed(4, yn, ALL, yn)
        send(12, yn, ALL, sib)
        send(8, yn, BOT, xn)
        landed(6, dg, TOP, yn)
        send(14, dg, TOP, sib)
        landed(8, dg, BOT, xn)
        send(16, dg, BOT, sib)
        sib_of = lambda p: _peer_position(p, 1)
        landed(0, sib, ALL, sib)
        landed(10, sib_of(xn), ALL, sib)
        landed(12, sib_of(yn), ALL, sib)
        landed(14, sib_of(dg), TOP, sib)
        landed(16, sib_of(dg), BOT, sib)
        for cp in small_cps:
            cp.wait_recv()
        for cp in started:
            cp.wait_send()
        for cp in local + [small_local]:
            cp.wait()

    return _call(body, [src, small], name="ag_w_in", in_specs=[ANY, ANY], out_specs=[ANY, ANY, ANY],
                 out_shape=(jax.ShapeDtypeStruct((D, INW), BF16), jax.ShapeDtypeStruct((N_DEV, D, LANES), BF16),
                            jax.ShapeDtypeStruct((N_DEV,) + small.shape, small.dtype)),
                 scratch_shapes=[pltpu.SemaphoreType.DMA((28,)), pltpu.SemaphoreType.DMA((28,))])


def _patch_mid(full, mid, a):
    D = full.shape[0]

    def body(full_ref, e_ref, o_ref, out_ref):
        out_ref[...] = e_ref[...] + o_ref[...]

    return _call(body, [full, mid, mid], name="patch_mid", grid=(N_DEV // 2,),
                 out_shape=jax.ShapeDtypeStruct(full.shape, full.dtype),
                 in_specs=[ANY, pl.BlockSpec((None, D, LANES), lambda j: (2 * j, 0, 0)),
                           pl.BlockSpec((None, D, LANES), lambda j: (2 * j + 1, 0, 0))],
                 out_specs=pl.BlockSpec((D, LANES), lambda j: (0, (2 * a + 1) * j + a)), aliases={0: 0})


MM_RESIDENT = 2048


def _mm(a, b, mode, out_dtype, name, b_off=0, n=None, comm=None, extras=(), epi=None, tn=None, after=(), b_order=None):
    if mode == "nn":
        (M, K), (K2, N) = a.shape, b.shape
    elif mode == "nt":
        (M, K), (N, K2) = a.shape, b.shape
    else:
        (K, M), (K2, N) = a.shape, b.shape
    assert K == K2, (a.shape, b.shape, mode)
    if n is not None:
        N = n
    single = not isinstance(out_dtype, (tuple, list))
    out_dtypes = (out_dtype,) if single else tuple(out_dtype)
    if epi is None:
        epi = lambda r: (r,)
    tk = K if K <= MM_RESIDENT else (MM_RESIDENT if K % MM_RESIDENT == 0 else _tile(K, 512, LANES))
    nk = K // tk
    if M > MM_RESIDENT and mode == "tn" and N <= MM_RESIDENT and not b_off:
        tm, tn = _tile(M, 512, LANES), N
    elif nk > 1:
        tm, tn = _tile(M, 1024, LANES), _tile(N, tn or 1024, LANES)
    else:
        tm = _tile(M, MM_RESIDENT, LANES)
        tn = _tile(math.gcd(N, b_off) if b_off else N, tn or 512, LANES)
    jb = b_off // tn
    dn = {"nn": NN, "nt": NT, "tn": TN}[mode]
    ne, no = len(extras), len(out_dtypes)

    def body(a_ref, b_ref, *rest):
        e_refs, o_refs = rest[:ne], rest[ne:ne + no]

        def finish(r):
            for o_ref, v in zip(o_refs, epi(r, *[e[...] for e in e_refs])):
                o_ref[...] = v.astype(o_ref.dtype)

        if nk == 1:
            finish(_bdot(a_ref[...], b_ref[...], dn))
            return
        acc_ref = rest[ne + no]
        k = pl.program_id(2)

        @pl.when(k == 0)
        def _():
            acc_ref[...] = _bdot(a_ref[...], b_ref[...], dn)

        @pl.when(jnp.logical_and(k > 0, k < nk - 1))
        def _():
            acc_ref[...] += _bdot(a_ref[...], b_ref[...], dn)

        @pl.when(k == nk - 1)
        def _():
            finish(acc_ref[...] + _bdot(a_ref[...], b_ref[...], dn))

    a_spec = pl.BlockSpec((tk, tm), lambda i, j, k: (k, i)) if mode == "tn" else pl.BlockSpec((tm, tk), lambda i, j, k: (i, k))
    col = (lambda j: j + jb) if b_order is None else functools.partial(b_order, tn)
    b_spec = pl.BlockSpec((tn, tk), lambda i, j, k: (j, k)) if mode == "nt" else pl.BlockSpec((tk, tn), lambda i, j, k: (k, col(j)))
    o_spec = pl.BlockSpec((tm, tn), lambda i, j, k: (i, j))
    res = _call(body, [a, b] + list(extras), name=name, grid=(M // tm, N // tn, nk),
                out_shape=tuple(jax.ShapeDtypeStruct((M, N), dt) for dt in out_dtypes),
                in_specs=[a_spec, b_spec] + [o_spec] * ne, out_specs=[o_spec] * no,
                scratch_shapes=[pltpu.VMEM((tm, tn), F32)] if nk > 1 else [], comm=comm, after=after)
    return res[0] if single else res


def _rowwise(fn, row_ins, bcast_ins, row_outs, acc_outs, name, rt=256, comm=None):
    L = row_ins[0][0].shape[-2]
    rt = _tile(L, rt, 16)
    nr, nb, no = len(row_ins), len(bcast_ins), len(row_outs)

    def body(*refs):
        i = pl.program_id(0)
        vals = [r[...] for r in refs[:nr + nb]]
        outs, accs = fn(*vals)
        for r, v in zip(refs[nr + nb:nr + nb + no], outs):
            r[...] = v.astype(r.dtype)
        acc_refs = refs[nr + nb + no:]

        @pl.when(i == 0)
        def _():
            for r in acc_refs:
                r[...] = jnp.zeros_like(r)

        for r, v in zip(acc_refs, accs):
            r[...] += v

    in_specs = []
    for spec in row_ins:
        w, cb = spec[1], spec[2]
        if len(spec) == 4:
            in_specs.append(pl.BlockSpec((None, rt, w), functools.partial(lambda i, cb, ld: (ld, i, cb), cb=cb, ld=spec[3])))
        else:
            in_specs.append(pl.BlockSpec((rt, w), functools.partial(lambda i, cb: (i, cb), cb=cb)))
    in_specs += [pl.BlockSpec(b.shape, lambda i: (0, 0)) for b in bcast_ins]
    out_specs = [pl.BlockSpec((rt, w), lambda i: (i, 0)) for w, _ in row_outs]
    out_specs += [pl.BlockSpec(s, lambda i: (0, 0)) for s in acc_outs]
    out_shape = [jax.ShapeDtypeStruct((L, w), dt) for w, dt in row_outs] + [jax.ShapeDtypeStruct(s, F32) for s in acc_outs]
    return _call(body, [s[0] for s in row_ins] + list(bcast_ins), name=name, grid=(L // rt,), out_shape=tuple(out_shape),
                 in_specs=in_specs, out_specs=out_specs, comm=comm)


def _whole(fn, ins, out_shapes, name):
    def body(*refs):
        outs = fn(*[r[...] for r in refs[:len(ins)]])
        for r, v in zip(refs[len(ins):], outs):
            r[...] = v.astype(r.dtype)

    return _call(body, list(ins), name=name, out_shape=tuple(jax.ShapeDtypeStruct(s, dt) for s, dt in out_shapes))


def _silu(x):
    return x * jax.nn.sigmoid(x)


def _rms(x, g):
    return (x * lax.rsqrt(jnp.mean(x * x, axis=-1, keepdims=True) + EPS)) * g


def _modnorm(x, g, shift, scale):
    return _rms(x, g) * (1.0 + scale) + shift


def _adamw(w, g, m, v):
    m = ADAM_B1 * m + (1.0 - ADAM_B1) * g
    v = ADAM_B2 * v + (1.0 - ADAM_B2) * jnp.square(g)
    m_hat = m / (1.0 - ADAM_B1 ** ADAM_STEP)
    v_hat = v / (1.0 - ADAM_B2 ** ADAM_STEP)
    delta = -ADAM_LR * (m_hat / (jnp.sqrt(v_hat) + ADAM_EPS) + ADAM_WD * w)
    return delta, m, v


def _lower_bound(lg):
    e = jnp.exp(lg - jnp.max(lg, axis=0, keepdims=True))
    return e[0:1] / jnp.sum(e, axis=0, keepdims=True)


def _hg_stages(hq_l, hf_l, hi_l, lb):
    C = hq_l[0].shape[0]
    row = lax.broadcasted_iota(jnp.int32, (C, C), 0)
    col = lax.broadcasted_iota(jnp.int32, (C, C), 1)
    tri = row >= col
    trif = tri.astype(F32)
    f_l = [lb + (1.0 - lb) * jax.nn.sigmoid(hf) for hf in hf_l]
    b_l = [_dot(trif, jnp.log(f), NN, precision=HIGHEST) for f in f_l]
    q_l = [_silu(hq) for hq in hq_l]
    m_l = [b[C // 2 - 1:C // 2] for b in b_l]
    bl_l = [b[C - 1:C] for b in b_l]
    sc_l = [jnp.where(tri, _bdot(q * jnp.exp(b - m), (1.0 - f) * jnp.exp(m - b), NT), 0.0)
            for q, f, b, m in zip(q_l, f_l, b_l, m_l)]
    o1_l = [_bdot(sc, hi, NN) for sc, hi in zip(sc_l, hi_l)]
    u_l = [_bdot(hi, (1.0 - f) * jnp.exp(bl - b), TN) for hi, f, b, bl in zip(hi_l, f_l, b_l, bl_l)]
    qb_l = [q * jnp.exp(b) for q, b in zip(q_l, b_l)]
    dec_l = [jnp.exp(bl) for bl in bl_l]
    return list(zip(o1_l, u_l, qb_l, dec_l))


def _hg_out(o, hgate, gout):
    return _rms(o, gout) * _silu(hgate)


HG_STAGE = 8
HG_GROUP = 32


def _hgrn_fwd(p4, lb_logits, gout, H, comm=None):
    L = p4.shape[0]
    C = HG_CHUNK
    GR = _tile(L // C, HG_GROUP, 1)
    T = GR * C
    N = L // T

    def body(hq_ref, hf_ref, hi_ref, hg_ref, lg_ref, gout_ref, o_ref, s_ref, st_ref):
        @pl.when(pl.program_id(1) == 0)
        def _():
            st_ref[...] = jnp.zeros_like(st_ref)

        lb = _lower_bound(lg_ref[...])
        st = st_ref[...]
        for c0 in range(0, GR, HG_STAGE):
            rows_l = [pl.ds(ci * C, C) for ci in range(c0, min(c0 + HG_STAGE, GR))]
            parts = _hg_stages([hq_ref[r, :] for r in rows_l], [hf_ref[r, :] for r in rows_l],
                               [hi_ref[r, :] for r in rows_l], lb)
            for ci, rows, (o1, u, qb, dec) in zip(range(c0, GR), rows_l, parts):
                s_ref[0, ci] = st
                o = o1 + _bdot(qb, st, NT)
                st = st * dec + u
                o_ref[rows, :] = _hg_out(o, hg_ref[rows, :], gout_ref[...]).astype(o_ref.dtype)
        st_ref[...] = st

    blk = lambda s: pl.BlockSpec((T, HG_DK), functools.partial(lambda h, n, s: (n, s * H + h), s=s))
    return _call(
        body, [p4, p4, p4, p4, lb_logits, gout], name="hgrn_fwd", grid=(H, N),
        out_shape=(jax.ShapeDtypeStruct((L, H * HG_DK), BF16), jax.ShapeDtypeStruct((H, N * GR, HG_DK, HG_DK), F32)),
        in_specs=[blk(0), blk(1), blk(2), blk(3), pl.BlockSpec((2, HG_DK), lambda h, n: (0, h)),
                  pl.BlockSpec((1, HG_DK), lambda h, n: (0, 0))],
        out_specs=(pl.BlockSpec((T, HG_DK), lambda h, n: (n, h)),
                   pl.BlockSpec((1, GR, HG_DK, HG_DK), lambda h, n: (h, n, 0, 0))),
        scratch_shapes=[pltpu.VMEM((HG_DK, HG_DK), F32)], comm=comm)


def _hgrn_bwd(p4, lb_logits, gout, s_all, d_out, H, comm=None):
    L = p4.shape[0]
    C = HG_CHUNK
    GR = _tile(L // C, HG_GROUP, 1)
    T = GR * C
    N = L // T

    def body(hq_ref, hf_ref, hi_ref, hg_ref, lg_ref, gout_ref, s_ref, do_ref,
             dq_ref, df_ref, di_ref, dg_ref, dlb_ref, dgo_ref, dst_ref):
        @pl.when(pl.program_id(1) == 0)
        def _():
            dst_ref[...] = jnp.zeros_like(dst_ref)
            dlb_ref[...] = jnp.zeros_like(dlb_ref)

        @pl.when(jnp.logical_and(pl.program_id(0) == 0, pl.program_id(1) == 0))
        def _():
            dgo_ref[...] = jnp.zeros_like(dgo_ref)

        lb = _lower_bound(lg_ref[...])
        dst = dst_ref[...]
        d_lb = jnp.zeros((1, HG_DK), F32)
        d_go = jnp.zeros((1, HG_DK), F32)
        for c0 in reversed(range(0, GR, HG_STAGE)):
            dst, d_lb_c, d_go_c = chunks_bwd(list(range(c0, min(c0 + HG_STAGE, GR))), lb, dst, hq_ref, hf_ref, hi_ref,
                                             hg_ref, gout_ref, s_ref, do_ref, dq_ref, df_ref, di_ref, dg_ref)
            d_lb += d_lb_c
            d_go += d_go_c
        dst_ref[...] = dst
        dlb_ref[...] += d_lb
        dgo_ref[...] += d_go

    def chunks_bwd(idx, lb, dst, hq_ref, hf_ref, hi_ref, hg_ref, gout_ref, s_ref, do_ref, dq_ref, df_ref, di_ref, dg_ref):
        n = len(idx)
        rows_l = [pl.ds(ci * C, C) for ci in idx]
        hq_l, hf_l, hi_l = ([r[rows, :] for rows in rows_l] for r in (hq_ref, hf_ref, hi_ref))
        st_l = [s_ref[0, ci] for ci in idx]
        row = lax.broadcasted_iota(jnp.int32, (C, C), 0)
        col = lax.broadcasted_iota(jnp.int32, (C, C), 1)
        tri = row >= col
        trif = tri.astype(F32)
        every = lambda fn, *ls: [fn(*a) for a in zip(*ls)]
        sg_l = every(jax.nn.sigmoid, hf_l)
        f_l = every(lambda sg: lb + (1.0 - lb) * sg, sg_l)
        b_l = every(lambda f: _dot(trif, jnp.log(f), NN, precision=HIGHEST), f_l)
        q_l = every(_silu, hq_l)
        m_l = every(lambda b: b[C // 2 - 1:C // 2], b_l)
        bl_l = every(lambda b: b[C - 1:C], b_l)
        e_qm_l = every(lambda b, m: jnp.exp(b - m), b_l, m_l)
        e_km_l = every(lambda b, m: jnp.exp(m - b), b_l, m_l)
        e_kl_l = every(lambda b, bl: jnp.exp(bl - b), b_l, bl_l)
        e_q_l = every(jnp.exp, b_l)
        dec_l = every(jnp.exp, bl_l)
        qe_l = every(lambda q, e: q * e, q_l, e_qm_l)
        ke_l = every(lambda f, e: (1.0 - f) * e, f_l, e_km_l)
        kd_l = every(lambda f, e: (1.0 - f) * e, f_l, e_kl_l)
        qb_l = every(lambda q, e: q * e, q_l, e_q_l)
        sc_l = every(lambda qe, ke: jnp.where(tri, _bdot(qe, ke, NT), 0.0), qe_l, ke_l)
        o_l = every(lambda sc, hi, qb, st: _bdot(sc, hi, NN) + _bdot(qb, st, NT), sc_l, hi_l, qb_l, st_l)
        vj_l = every(lambda o, rows: jax.vjp(_hg_out, o, hg_ref[rows, :], gout_ref[...])[1](do_ref[rows, :]), o_l, rows_l)
        do_l = [v[0] for v in vj_l]
        dsc_l = every(lambda do, hi: jnp.where(tri, _bdot(do, hi, NT), 0.0), do_l, hi_l)
        dv1_l = every(lambda sc, do: _bdot(sc, do, TN), sc_l, do_l)
        dqe_l = every(lambda dsc, ke: _bdot(dsc, ke, NN), dsc_l, ke_l)
        dke_l = every(lambda dsc, qe: _bdot(dsc, qe, TN), dsc_l, qe_l)
        dqb_l = every(lambda do, st: _bdot(do, st, NN), do_l, st_l)
        own_l = every(lambda do, qb: _bdot(do, qb, TN), do_l, qb_l)
        dst_next_l = [None] * n
        for j in reversed(range(n)):
            dst_next_l[j] = dst
            dst = own_l[j] + dst * dec_l[j]
        dv_l = every(lambda dv1, kd, dn: dv1 + _bdot(kd, dn, NT), dv1_l, kd_l, dst_next_l)
        dkd_l = every(lambda hi, dn: _bdot(hi, dn, NN), hi_l, dst_next_l)
        ddec_l = every(lambda dn, st: jnp.sum(dn * st, axis=0, keepdims=True), dst_next_l, st_l)
        rowi = lax.broadcasted_iota(jnp.int32, (C, HG_DK), 0)
        tq_l = every(lambda a, b_: a * b_, dqe_l, qe_l)
        tk_l = every(lambda a, b_: a * b_, dke_l, ke_l)
        td_l = every(lambda a, b_: a * b_, dkd_l, kd_l)
        tb_l = every(lambda a, b_: a * b_, dqb_l, qb_l)
        db_l = every(lambda tq, tk, td, tb, ddec, dec: tq - tk - td + tb
                     + jnp.where(rowi == C // 2 - 1, jnp.sum(tk - tq, axis=0, keepdims=True), 0.0)
                     + jnp.where(rowi == C - 1, jnp.sum(td, axis=0, keepdims=True) + ddec * dec, 0.0),
                     tq_l, tk_l, td_l, tb_l, ddec_l, dec_l)
        dlf_l = every(lambda db: _dot(trif, db, TN, precision=HIGHEST), db_l)
        dk_l = every(lambda dke, e1, dkd, e2: dke * e1 + dkd * e2, dke_l, e_km_l, dkd_l, e_kl_l)
        df_l = every(lambda dlf, f, dk: dlf / f - dk, dlf_l, f_l, dk_l)
        d_lb = jnp.zeros((1, HG_DK), F32)
        d_go = jnp.zeros((1, HG_DK), F32)
        for j, rows in enumerate(rows_l):
            sg, hq = sg_l[j], hq_l[j]
            df_ref[rows, :] = (df_l[j] * (1.0 - lb) * sg * (1.0 - sg)).astype(df_ref.dtype)
            sq = jax.nn.sigmoid(hq)
            dq = dqe_l[j] * e_qm_l[j] + dqb_l[j] * e_q_l[j]
            dq_ref[rows, :] = (dq * (sq * (1.0 + hq * (1.0 - sq)))).astype(dq_ref.dtype)
            di_ref[rows, :] = dv_l[j].astype(di_ref.dtype)
            dg_ref[rows, :] = vj_l[j][1].astype(dg_ref.dtype)
            d_lb += jnp.sum(df_l[j] * (1.0 - sg), axis=0, keepdims=True)
            d_go += vj_l[j][2]
        return dst, d_lb, d_go

    blk = lambda s: pl.BlockSpec((T, HG_DK), functools.partial(lambda h, n, s: (N - 1 - n, s * H + h), s=s))
    oblk = pl.BlockSpec((T, HG_DK), lambda h, n: (N - 1 - n, h))
    vec = pl.BlockSpec((1, HG_DK), lambda h, n: (0, h))
    W = H * HG_DK
    return _call(
        body, [p4, p4, p4, p4, lb_logits, gout, s_all, d_out], name="hgrn_bwd", grid=(H, N),
        out_shape=tuple([jax.ShapeDtypeStruct((L, W), BF16)] * 4 + [jax.ShapeDtypeStruct((1, W), F32), jax.ShapeDtypeStruct((1, HG_DK), F32)]),
        in_specs=[blk(0), blk(1), blk(2), blk(3), pl.BlockSpec((2, HG_DK), lambda h, n: (0, h)),
                  pl.BlockSpec((1, HG_DK), lambda h, n: (0, 0)),
                  pl.BlockSpec((1, GR, HG_DK, HG_DK), lambda h, n: (h, N - 1 - n, 0, 0)), oblk],
        out_specs=(oblk, oblk, oblk, oblk, vec, pl.BlockSpec((1, HG_DK), lambda h, n: (0, 0))),
        scratch_shapes=[pltpu.VMEM((HG_DK, HG_DK), F32)], comm=comm)


def _bucket_ids():
    i = jnp.arange(AT_BLOCK, dtype=jnp.int32)[:, None]
    j = jnp.arange(2 * AT_BLOCK, dtype=jnp.int32)[None, :]
    n = jnp.maximum(i - j + AT_BLOCK, 0)
    nf = jnp.maximum(n, 1).astype(F32)
    large = MAX_EXACT + (jnp.log(nf / MAX_EXACT) / math.log(MAX_DISTANCE / MAX_EXACT) * (N_BUCKETS - MAX_EXACT)).astype(jnp.int32)
    large = jnp.minimum(large, N_BUCKETS - 1)
    return jnp.where(n < MAX_EXACT, n, large).reshape(1, -1)


def _onehot(bucket):
    ids = lax.broadcasted_iota(jnp.int32, (N_BUCKETS, bucket.shape[1]), 0)
    return (ids == bucket).astype(F32)


def _attn_probs(qn, kpn, kcn, bias_g, sink, first, scale):
    rows = qn.shape[0]
    i = jnp.bitwise_and(lax.broadcasted_iota(jnp.int32, (rows, AT_BLOCK), 0), AT_BLOCK - 1)
    j = lax.broadcasted_iota(jnp.int32, (rows, AT_BLOCK), 1)
    lp = _bdot(qn, kpn, NT) * scale + bias_g[:, :AT_BLOCK]
    lc = _bdot(qn, kcn, NT) * scale + bias_g[:, AT_BLOCK:]
    lp = jnp.where(jnp.logical_and(j > i, jnp.logical_not(first)), lp, NEG_INF)
    lc = jnp.where(j <= i, lc, NEG_INF)
    m = jnp.maximum(jnp.maximum(jnp.max(lp, axis=-1, keepdims=True), jnp.max(lc, axis=-1, keepdims=True)), sink)
    pp, pc, ps = jnp.exp(lp - m), jnp.exp(lc - m), jnp.exp(sink - m)
    den = jnp.sum(pp, axis=-1, keepdims=True) + jnp.sum(pc, axis=-1, keepdims=True) + ps
    return pp / den, pc / den, ps / den


def _sink_rows(sk_ref, G):
    head = lax.broadcasted_iota(jnp.int32, (G * AT_BLOCK, 1), 0) // AT_BLOCK
    sink = jnp.zeros((G * AT_BLOCK, 1), F32)
    for g in range(G):
        sink = jnp.where(head == g, sk_ref[0, g:g + 1, :], sink)
    return sink


def _attn_fwd(q_t, kp, vp, qg, kg, sinks, bias, KVH, comm=None):
    AH, L, DH = q_t.shape
    G = AH // KVH
    NB = L // AT_BLOCK
    scale = DH ** -0.5

    def body(q_ref, kp_ref, kc_ref, vp_ref, vc_ref, qg_ref, kg_ref, sk_ref, b_ref, o_ref):
        first = pl.program_id(1) == 0
        kpn, kcn = _rms(kp_ref[0], kg_ref[...]), _rms(kc_ref[0], kg_ref[...])
        qn = _rms(q_ref[...].reshape(G * AT_BLOCK, DH), qg_ref[...])
        sink = _sink_rows(sk_ref, G)
        pp, pc, _ = _attn_probs(qn, kpn, kcn, b_ref[...].reshape(G * AT_BLOCK, 2 * AT_BLOCK), sink, first, scale)
        o = _bdot(pp, vp_ref[0], NN) + _bdot(pc, vc_ref[0], NN)
        o_ref[...] = o.reshape(G, AT_BLOCK, DH).astype(o_ref.dtype)

    kblk = lambda off: pl.BlockSpec((1, AT_BLOCK, DH),
                                    functools.partial(lambda h, n, off: (h, jnp.maximum(n + off - 1, 0), 0), off=off))
    return _call(
        body, [q_t, kp, kp, vp, vp, qg, kg, sinks, bias], name="attn_fwd", grid=(KVH, NB),
        out_shape=jax.ShapeDtypeStruct((AH, L, DH), BF16),
        in_specs=[pl.BlockSpec((G, AT_BLOCK, DH), lambda h, n: (h, n, 0)), kblk(0), kblk(1), kblk(0), kblk(1),
                  pl.BlockSpec((1, DH), lambda h, n: (0, 0)), pl.BlockSpec((1, DH), lambda h, n: (0, 0)),
                  pl.BlockSpec((1, G, 1), lambda h, n: (h, 0, 0)),
                  pl.BlockSpec((G, AT_BLOCK, 2 * AT_BLOCK), lambda h, n: (h, 0, 0))],
        out_specs=pl.BlockSpec((G, AT_BLOCK, DH), lambda h, n: (h, n, 0)), comm=comm)


def _attn_bwd(q_t, kp, vp, qg, kg, sinks, bias, do_t, KVH, comm=None):
    AH, L, DH = q_t.shape
    G = AH // KVH
    NB = L // AT_BLOCK
    B = AT_BLOCK
    scale = DH ** -0.5

    def body(q_ref, kp_ref, kc_ref, vp_ref, vc_ref, qg_ref, kg_ref, sk_ref, b_ref, do_ref,
             dq_ref, dk_ref, dv_ref, dqg_ref, dkg_ref, dsk_ref, db_ref):
        n = pl.program_id(1)
        first = n == 0

        @pl.when(first)
        def _():
            for r in (dk_ref, dv_ref, dsk_ref, db_ref):
                r[...] = jnp.zeros_like(r)

        @pl.when(jnp.logical_and(first, pl.program_id(0) == 0))
        def _():
            dqg_ref[...] = jnp.zeros_like(dqg_ref)
            dkg_ref[...] = jnp.zeros_like(dkg_ref)

        kp_raw, kc_raw, kgv, qgv = kp_ref[0], kc_ref[0], kg_ref[...], qg_ref[...]
        kpn, kp_vjp = jax.vjp(_rms, kp_raw, kgv)
        kcn, kc_vjp = jax.vjp(_rms, kc_raw, kgv)
        qn, q_vjp = jax.vjp(_rms, q_ref[...].reshape(G * B, DH), qgv)
        pp, pc, ps = _attn_probs(qn, kpn, kcn, b_ref[...].reshape(G * B, 2 * B), _sink_rows(sk_ref, G), first, scale)
        do = do_ref[...].reshape(G * B, DH)
        dvp = _bdot(pp, do, TN)
        dvc = _bdot(pc, do, TN)
        dpp = _bdot(do, vp_ref[0], NT)
        dpc = _bdot(do, vc_ref[0], NT)
        dsum = jnp.sum(dpp * pp, axis=-1, keepdims=True) + jnp.sum(dpc * pc, axis=-1, keepdims=True)
        dlp = pp * (dpp - dsum)
        dlc = pc * (dpc - dsum)
        dsk_ref[0] += jnp.sum((-ps * dsum).reshape(G, B, 1), axis=1)
        db_ref[:, :, :B] += dlp.reshape(G, B, B)
        db_ref[:, :, B:] += dlc.reshape(G, B, B)
        dlp, dlc = dlp * scale, dlc * scale
        dqn = _bdot(dlp, kpn, NN) + _bdot(dlc, kcn, NN)
        dq_raw, dqg = q_vjp(dqn)
        dq_ref[...] = dq_raw.reshape(G, B, DH).astype(dq_ref.dtype)
        dkp_raw, dkg_p = kp_vjp(_bdot(dlp, qn, TN))
        dkc_raw, dkg_c = kc_vjp(_bdot(dlc, qn, TN))
        r0 = pl.multiple_of(jnp.maximum(n - 1, 0) * B, B)
        r1 = pl.multiple_of(n * B, B)
        dk_ref[0, pl.ds(r0, B), :] += dkp_raw
        dk_ref[0, pl.ds(r1, B), :] += dkc_raw
        dv_ref[0, pl.ds(r0, B), :] += dvp
        dv_ref[0, pl.ds(r1, B), :] += dvc
        dqg_ref[...] += dqg
        dkg_ref[...] += dkg_p + dkg_c

    kblk = lambda off: pl.BlockSpec((1, B, DH), functools.partial(lambda h, n, off: (h, jnp.maximum(n + off - 1, 0), 0), off=off))
    qblk = pl.BlockSpec((G, B, DH), lambda h, n: (h, n, 0))
    accblk = pl.BlockSpec((1, L, DH), lambda h, n: (h, 0, 0))
    vecblk = pl.BlockSpec((1, DH), lambda h, n: (0, 0))
    return _call(
        body, [q_t, kp, kp, vp, vp, qg, kg, sinks, bias, do_t], name="attn_bwd", grid=(KVH, NB),
        out_shape=(jax.ShapeDtypeStruct((AH, L, DH), BF16), jax.ShapeDtypeStruct((KVH, L, DH), F32),
                   jax.ShapeDtypeStruct((KVH, L, DH), F32), jax.ShapeDtypeStruct((1, DH), F32),
                   jax.ShapeDtypeStruct((1, DH), F32), jax.ShapeDtypeStruct((KVH, G, 1), F32),
                   jax.ShapeDtypeStruct((AH, B, 2 * B), F32)),
        in_specs=[qblk, kblk(0), kblk(1), kblk(0), kblk(1),
                  pl.BlockSpec((1, DH), lambda h, n: (0, 0)), pl.BlockSpec((1, DH), lambda h, n: (0, 0)),
                  pl.BlockSpec((1, G, 1), lambda h, n: (h, 0, 0)),
                  pl.BlockSpec((G, B, 2 * B), lambda h, n: (h, 0, 0)), qblk],
        out_specs=(qblk, accblk, accblk, vecblk, vecblk, pl.BlockSpec((1, G, 1), lambda h, n: (h, 0, 0)),
                   pl.BlockSpec((G, B, 2 * B), lambda h, n: (h, 0, 0))), comm=comm)


def _heads_first(t, nh):
    L = t.shape[0]
    return jnp.transpose(t.reshape(L, nh, t.shape[1] // nh), (1, 0, 2))


def _heads_last(t):
    nh, L, dh = t.shape
    return jnp.transpose(t, (1, 0, 2)).reshape(L, nh * dh)


def _softmax0(lg):
    e = jnp.exp(lg - jnp.max(lg, axis=0, keepdims=True))
    return e[0:1] / jnp.sum(e, axis=0, keepdims=True)


def _ada_update_call(fn, c_all, d_cols, w, m, v, rt):
    D, n = w.shape

    def body(c_ref, d_ref, w_ref, m_ref, v_ref, g_out, dl_out, m_out, v_out):
        outs, _ = fn(c_ref[...], d_ref[...], w_ref[...], m_ref[...], v_ref[...])
        for r, val in zip((g_out, dl_out, m_out, v_out), outs):
            r[...] = val

    wblk = pl.BlockSpec((rt, n), lambda i: (i, 0))
    return _call(
        body, [c_all, d_cols, w, m, v], name="update_ada", grid=(D // rt,), out_shape=tuple([jax.ShapeDtypeStruct((D, n), F32)] * 4),
        in_specs=[pl.BlockSpec((N_DEV, rt), lambda i: (0, i)), pl.BlockSpec((N_DEV, n), lambda i: (0, 0)), wblk, wblk, wblk],
        out_specs=(wblk, wblk, wblk, wblk))


def kernel(x, c, w_ada, b_ada, norm1_g, norm2_g, w_in, hg_lb_logits, hg_out_norm_g, q_norm_g, k_norm_g, attn_sinks, rel_bias_table, w_branch_hg, w_branch_attn, w_out, w_ff1, w_ff2, loss_target, m_w_ada, m_b_ada, m_norm1_g, m_norm2_g, m_w_in, m_hg_lb_logits, m_hg_out_norm_g, m_q_norm_g, m_k_norm_g, m_attn_sinks, m_rel_bias_table, m_w_branch_hg, m_w_branch_attn, m_w_out, m_w_ff1, m_w_ff2, v_w_ada, v_b_ada, v_norm1_g, v_norm2_g, v_w_in, v_hg_lb_logits, v_hg_out_norm_g, v_q_norm_g, v_k_norm_g, v_attn_sinks, v_rel_bias_table, v_w_branch_hg, v_w_branch_attn, v_w_out, v_w_ff1, v_w_ff2):
    cc = lax.axis_index("c")
    me = 4 * lax.axis_index("x") + 2 * lax.axis_index("y") + cc
    x2 = x[0]
    tgt = loss_target[0]
    L, D = x2.shape
    HGW = hg_lb_logits.shape[1]
    H = HGW // HG_DK
    AH = attn_sinks.shape[1]
    DH = q_norm_g.shape[1]
    ATW = AH * DH
    BW = w_in.shape[2]
    INW = BW * N_DEV
    A = BW // LANES
    assert BW == LANES * A + LANES // 2
    KVW = (INW - 4 * HGW - ATW - 2 * D) // 2
    KVH = KVW // DH
    G = AH // KVH
    ADA_N = w_ada.shape[2]
    PAIR = 2 * A + 1

    c_all = _gather_small(c, me, "gather_c")[:, 0, :]
    b_cols = lax.dynamic_slice(b_ada, (0, me * ADA_N), (1, ADA_N))
    (ada_cols,) = _whole(lambda cv, w, b: (_bdot(_silu(cv), w, NN) + b,), [c_all, w_ada[0], b_cols],
                         [((N_DEV, ADA_N), F32)], "ada_fwd")

    w_in_b = w_in[0].astype(BF16)
    src_in = jnp.where(cc == 0, jnp.pad(w_in_b, ((0, 0), (0, LANES // 2))), jnp.pad(w_in_b, ((0, 0), (LANES // 2, 0))))
    w_in_gapped, w_in_mid, ada_all = _ag_w_in(src_in, ada_cols, A, D, INW)
    ada_row = lax.dynamic_slice(ada_all, (0, me, 0), (N_DEV, 1, ADA_N)).reshape(1, 6 * D)
    shift1, scale1, gate1, shift2, scale2, gate2 = [ada_row[:, i * D:(i + 1) * D] for i in range(6)]
    w_in_full = _patch_mid(w_in_gapped, w_in_mid, A)

    wnames = ("bhg", "bat", "out", "ff1", "ff2")
    small = ("bhg", "bat", "out")
    waxis = dict(zip(wnames, (1, 1, 0, 1, 0)))
    wsrc = dict(zip(wnames, (w_branch_hg, w_branch_attn, w_out, w_ff1, w_ff2)))
    wblk = {k: wsrc[k][0].astype(BF16) for k in wnames}
    wf = {}

    (h,) = _rowwise(lambda xv, g, sh, sc: ((_modnorm(xv, g, sh, sc),), ()), [(x2, D, 0)], [norm1_g, shift1, scale1],
                    [(D, BF16)], [], "norm1")
    o4, oa = 4 * HGW, 4 * HGW + ATW + 2 * KVW
    r1, r2 = wblk["ff1"].shape[0], wblk["ff2"].shape[0]
    assert o4 % D == 0

    def proj_order(tn_, j):
        t4, tg, ng = o4 // tn_, oa // tn_, (INW - oa) // tn_
        return jnp.where(j < t4, j, jnp.where(j < t4 + ng, j + (tg - t4), j - ng))

    cm = _Comm()
    hs = {k: _ag_ici(cm, wblk[k], waxis[k]) for k in small}
    hs["ff2"] = _ag_ici(cm, wblk["ff2"], waxis["ff2"], rows=(0, r2 // 4))
    proj = _mm(h, w_in_full, "nn", F32, "proj", comm=cm, b_order=proj_order)
    half = {k: cm.result(hs[k]) for k in hs}
    p4 = pg = proj
    GATE0 = o4 // D
    pa = proj[:, o4 + (INW - oa):]

    cm = _Comm()
    hs = {k: _ag_d2d(cm, half[k], waxis[k]) for k in small}
    hs["ff1"] = _ag_ici(cm, wblk["ff1"], waxis["ff1"], rows=(0, r1 // 2))
    o_hg, s_all = _hgrn_fwd(p4, hg_lb_logits, hg_out_norm_g, H, comm=cm)
    wf["bhg"], wf["bat"], wf["out"], half["ff1"] = (cm.result(hs[k]) for k in ("bhg", "bat", "out", "ff1"))

    bucket = _bucket_ids()
    (bias_flat,) = _whole(lambda tb, bk: (_dot(tb, _onehot(bk), TN, precision=HIGHEST),), [rel_bias_table, bucket],
                          [((AH, AT_BLOCK * 2 * AT_BLOCK), F32)], "bias_fwd")
    bias = bias_flat.reshape(AH, AT_BLOCK, 2 * AT_BLOCK)
    q_t = _heads_first(pa[:, :ATW], AH)
    kp = _heads_first(pa[:, ATW:ATW + KVW], KVH)
    vp = _heads_first(pa[:, ATW + KVW:], KVH)
    sinks3 = attn_sinks.reshape(KVH, G, 1)
    cm = _Comm()
    hs = {"ff1": _ag_ici(cm, wblk["ff1"], waxis["ff1"], rows=(r1 // 2, r1), into=half["ff1"])}
    o_at = _heads_last(_attn_fwd(q_t, kp, vp, q_norm_g, k_norm_g, sinks3, bias, KVH, comm=cm))
    half["ff1"] = cm.result(hs["ff1"])

    bh = _mm(o_hg, wf["bhg"], "nn", F32, "branch_hg")
    ba = _mm(o_at, wf["bat"], "nn", F32, "branch_at")

    def merge_fn(bhv, bav, ghg, gat):
        return jax.nn.sigmoid(ghg) * bhv + jax.nn.sigmoid(gat) * bav

    cm = _Comm()
    hs = {"ff1": _ag_d2d(cm, half["ff1"], waxis["ff1"])}
    (merged,) = _rowwise(lambda *a: ((merge_fn(*a),), ()), [(bh, D, 0), (ba, D, 0), (pg, D, GATE0), (pg, D, GATE0 + 1)], [],
                         [(D, BF16)], [], "merge", comm=cm)
    wf["ff1"] = cm.result(hs["ff1"])
    cm = _Comm()
    hs = {"ff2": _ag_ici(cm, wblk["ff2"], waxis["ff2"], rows=(r2 // 4, 3 * r2 // 8), into=half["ff2"])}
    mo = _mm(merged, wf["out"], "nn", F32, "out_proj", comm=cm)
    half["ff2"] = cm.result(hs["ff2"])

    def resid1(xv, mov, g1, g2n, sh, sc):
        x1v = xv + g1 * mov
        return (x1v, _modnorm(x1v, g2n, sh, sc)), ()

    cm = _Comm()
    hs = {"ff2": _ag_ici(cm, wblk["ff2"], waxis["ff2"], rows=(3 * r2 // 8, r2 // 2), into=half["ff2"])}
    x1, h2 = _rowwise(resid1, [(x2, D, 0), (mo, D, 0)], [gate1, norm2_g, shift2, scale2], [(D, F32), (D, BF16)], [], "resid1",
                      comm=cm)
    half["ff2"] = cm.result(hs["ff2"])
    cm = _Comm()
    hs = {"ff2": _ag_ici(cm, wblk["ff2"], waxis["ff2"], rows=(r2 // 2, r2), into=half["ff2"])}
    u, act = _mm(h2, wf["ff1"], "nn", (F32, BF16), "ff1", comm=cm, epi=lambda r: (r, jnp.square(jnp.maximum(r, 0.0))))
    half["ff2"] = cm.result(hs["ff2"])
    cm = _Comm()
    hs = {"ff2": _ag_d2d(cm, half["ff2"], waxis["ff2"])}
    _call(lambda: None, [], name="ag_d2d_ff2", out_shape=(), comm=cm)
    wf["ff2"] = cm.result(hs["ff2"])
    ff = _mm(act, wf["ff2"], "nn", F32, "ff2")

    def loss_fn(x1v, ffv, tv, g2):
        e = x1v + g2 * ffv - tv
        dy = e * (1.0 / D)
        return (dy, dy * g2), (jnp.sum(e * e, axis=0, keepdims=True), jnp.sum(dy * ffv, axis=0, keepdims=True))

    dy, d_ff, sq_sum, d_gate2 = _rowwise(loss_fn, [(x1, D, 0), (ff, D, 0), (tgt, D, 0)], [gate2],
                                         [(D, F32), (D, BF16)], [(1, D), (1, D)], "loss")
    loss = lax.psum(jnp.sum(sq_sum) * (0.5 / D), ("x", "y", "c"))

    owner_base = jnp.stack([me ^ r for r in CHIP_RELS]).astype(jnp.int32)
    gw, recv1, part, recv2 = {}, {}, {}, {}
    gw["ff2"] = _mm(act, d_ff, "tn", BF16, "dw_ff2")
    cm = _Comm()
    hh = _rs_d2d(cm, gw["ff2"], waxis["ff2"])
    d_u = _mm(d_ff, wf["ff2"], "nt", BF16, "d_act", comm=cm, extras=[u], epi=lambda r, uv: (r * (2.0 * jnp.maximum(uv, 0.0)),))
    part["ff2"] = _rs_add(gw["ff2"], cm.result(hh), waxis["ff2"], owner_base, "rs_add_ff2")
    rows_ff2 = part["ff2"].shape[1]
    cm = _Comm()
    hh = _rs_ici(cm, part["ff2"], rows=(0, rows_ff2 // 2))
    gw["ff1"] = _mm(h2, d_u, "tn", BF16, "dw_ff1", comm=cm)
    cm2 = _Comm()
    hh2 = _rs_ici(cm2, part["ff2"], rows=(rows_ff2 // 2, rows_ff2), recv=cm.result(hh))
    hh1 = _rs_d2d(cm2, gw["ff1"], waxis["ff1"])
    d_h2 = _mm(d_u, wf["ff1"], "nt", F32, "d_h2", comm=cm2)
    recv2["ff2"] = cm2.result(hh2)
    part["ff1"] = _rs_add(gw["ff1"], cm2.result(hh1), waxis["ff1"], owner_base, "rs_add_ff1")

    def norm2_bwd(dh2v, x1v, dyv, mov, g2n, sh, sc, g1):
        _, vjp = jax.vjp(_modnorm, x1v, g2n, sh, sc)
        dx, dg, dsh, dsc = vjp(dh2v)
        dx1 = dyv + dx
        return (dx1, dx1 * g1), (dg, dsh, dsc, jnp.sum(dx1 * mov, axis=0, keepdims=True))

    d_x1, d_mo, d_g2n, d_shift2, d_scale2, d_gate1 = _rowwise(
        norm2_bwd, [(d_h2, D, 0), (x1, D, 0), (dy, D, 0), (mo, D, 0)], [norm2_g, shift2, scale2, gate1],
        [(D, F32), (D, BF16)], [(1, D)] * 4, "norm2_bwd")
    gw["out"] = _mm(merged, d_mo, "tn", BF16, "dw_out")
    d_merged = _mm(d_mo, wf["out"], "nt", F32, "d_merged")

    def merge_bwd(dmv, bhv, bav, ghg, gat):
        _, vjp = jax.vjp(merge_fn, bhv, bav, ghg, gat)
        return vjp(dmv), ()

    d_bh, d_ba, d_ghg, d_gat = _rowwise(merge_bwd, [(d_merged, D, 0), (bh, D, 0), (ba, D, 0), (pg, D, GATE0), (pg, D, GATE0 + 1)], [],
                                        [(D, BF16)] * 4, [], "merge_bwd")
    gw["bhg"] = _mm(o_hg, d_bh, "tn", BF16, "dw_bhg")
    gw["bat"] = _mm(o_at, d_ba, "tn", BF16, "dw_bat")
    d_ohg = _mm(d_bh, wf["bhg"], "nt", F32, "d_ohg")
    d_oat = _mm(d_ba, wf["bat"], "nt", BF16, "d_oat")
    rows_ff1 = part["ff1"].shape[1]
    cut_ff1 = 3 * rows_ff1 // 8
    cm = _Comm()
    hf1 = _rs_ici(cm, part["ff1"], rows=(0, cut_ff1))
    d_hq, d_hf, d_hi, d_hg, d_lb, d_gout_h = _hgrn_bwd(p4, hg_lb_logits, hg_out_norm_g, s_all, d_ohg, H, comm=cm)
    cm2 = _Comm()
    hf1 = _rs_ici(cm2, part["ff1"], rows=(cut_ff1, rows_ff1), recv=cm.result(hf1))
    hh = {k: _rs_d2d(cm2, gw[k], waxis[k]) for k in small}
    dq_t, dkp, dvp, d_qg, d_kg, d_sk, d_bias = _attn_bwd(q_t, kp, vp, q_norm_g, k_norm_g, sinks3, bias,
                                                         _heads_first(d_oat, AH), KVH, comm=cm2)
    recv2["ff1"] = cm2.result(hf1)
    for k in small:
        part[k] = _rs_add(gw[k], cm2.result(hh[k]), waxis[k], owner_base, "rs_add_" + k)
    d_aq = _heads_last(dq_t)
    d_ak = _heads_last(dkp).astype(BF16)
    d_av = _heads_last(dvp).astype(BF16)
    d_proj = jnp.concatenate([d_hq, d_hf, d_hi, d_hg, d_aq, d_ak, d_av, d_ghg, d_gat], axis=1)
    cm = _Comm()
    hh = {k: _rs_ici(cm, part[k]) for k in small}
    gw_in = _mm(h, d_proj, "tn", BF16, "dw_in", comm=cm)
    for k in small:
        recv2[k] = cm.result(hh[k])

    wm = LANES * A
    cm = _Comm()
    hi_ = cm.inp(gw_in)
    h_main, h_mid = cm.out((4, D, wm), BF16), cm.out((4, D, LANES), BF16)
    for i, r in enumerate(CHIP_RELS):
        def main_view(ref, p, r=r):
            o = p["me"] ^ r ^ 1
            return ref.at[:, pl.ds(pl.multiple_of((PAIR * (o // 2) + (A + 1) * (1 - p["c"])) * LANES, LANES), wm)]

        def mid_view(ref, p, r=r):
            o = p["me"] ^ r
            return ref.at[:, pl.ds(pl.multiple_of((PAIR * (o // 2) + A) * LANES, LANES), LANES)]

        cm.copy(hi_, main_view, h_main, _slot_view(i), 1)
        cm.copy(hi_, mid_view, h_mid, _slot_view(i), 1)
    _call(lambda: None, [], name="rs_d2d_in", out_shape=(), comm=cm)
    chip = jnp.stack([(me ^ r) // 2 for r in CHIP_RELS]).astype(jnp.int32)
    part_main = _rs_add(gw_in, cm.result(h_main), 1, PAIR * chip + (A + 1) * cc, "rs_add_in_main", tw=LANES)
    part_mid = _rs_add(gw_in, cm.result(h_mid), 1, PAIR * chip + A, "rs_add_in_mid", tw=LANES)
    rs_in = _rs_split_start([part_main, part_mid], "rs_in_start")
    d_h = _mm(d_proj, w_in_full, "nt", F32, "d_h", tn=D, after=[rs_in["token"]])

    def norm1_bwd(dhv, xv, dx1v, g1n, sh, sc):
        _, vjp = jax.vjp(_modnorm, xv, g1n, sh, sc)
        dx, dg, dsh, dsc = vjp(dhv)
        return (dx1v + dx,), (dg, dsh, dsc)

    grad_x, d_g1n, d_shift1, d_scale1 = _rowwise(norm1_bwd, [(d_h, D, 0), (x2, D, 0), (d_x1, D, 0)],
                                                 [norm1_g, shift1, scale1], [(D, F32)], [(1, D)] * 3, "norm1_bwd")

    def sum4(p0, p1, p2, p3):
        return ((p0.astype(F32) + p1.astype(F32)) + p2.astype(F32)) + p3.astype(F32)

    def update_fn(w, m, v, p0, p1, p2, p3):
        g = sum4(p0, p1, p2, p3)
        delta, mn, vn = _adamw(w, g, m, v)
        return (g, delta, mn, vn), ()

    wmv = dict(zip(wnames, ((w_branch_hg, m_w_branch_hg, v_w_branch_hg), (w_branch_attn, m_w_branch_attn, v_w_branch_attn),
                            (w_out, m_w_out, v_w_out), (w_ff1, m_w_ff1, v_w_ff1), (w_ff2, m_w_ff2, v_w_ff2))))
    res = {}

    def update(k, p, rx):
        w, m, v = (t[0] for t in wmv[k])
        n = w.shape[1]
        ins = [(t, n, 0) for t in (w, m, v)] + [(p, n, 0, 0)] + [(rx, n, 0, i) for i in range(3)]
        res[k] = [t[None] for t in _rowwise(update_fn, ins, [], [(n, F32)] * 4, [], "update_" + k)]

    for k in wnames:
        update(k, part[k], recv2[k])
    (part_main, part_mid), (rx_main, rx_mid) = _rs_split_wait(rs_in, [grad_x] + [res[k][0] for k in wnames], "rs_in_wait")
    g_main, = _rowwise(lambda *p: ((sum4(*p),), ()), [(part_main, wm, 0, 0)] + [(rx_main, wm, 0, i) for i in range(3)], [],
                       [(wm, F32)], [], "sum_in_main")
    g_mid, = _rowwise(lambda *p: ((sum4(*p),), ()), [(part_mid, LANES, 0, 0)] + [(rx_mid, LANES, 0, i) for i in range(3)], [],
                      [(LANES, F32)], [], "sum_in_mid")
    g_in = jnp.where(cc == 0, jnp.concatenate([g_main, g_mid[:, :LANES // 2]], axis=1),
                     jnp.concatenate([g_mid[:, LANES // 2:], g_main], axis=1))

    def update_given(w, m, v, g):
        delta, mn, vn = _adamw(w, g, m, v)
        return (g, delta, mn, vn), ()

    res["in"] = [t[None] for t in _rowwise(update_given, [(t, BW, 0) for t in (w_in[0], m_w_in[0], v_w_in[0], g_in)], [],
                                           [(BW, F32)] * 4, [], "update_in")]

    d_sinks = d_sk.reshape(1, AH)
    (d_table_t,) = _whole(lambda db, bk: (_dot(db, _onehot(bk), NT, precision=HIGHEST),),
                          [d_bias.reshape(AH, AT_BLOCK * 2 * AT_BLOCK), bucket], [((AH, N_BUCKETS), F32)], "bias_bwd")
    smalls = [d_g1n, d_g2n, d_lb, d_gout_h, d_qg, d_kg, d_sinks, d_table_t.T.reshape(1, N_BUCKETS * AH)]
    widths = [s.shape[1] for s in smalls]
    lanes = [-(-w // LANES) * LANES for w in widths]
    smalls = [jnp.pad(s, ((0, 0), (0, p - w))) for s, w, p in zip(smalls, widths, lanes)]
    tail_row = jnp.concatenate([d_shift1, d_scale1, d_gate1, d_shift2, d_scale2, d_gate2] + smalls, axis=1)
    (tail_row,) = _behind([tail_row], [g_mid])
    tail_all = _gather_small(tail_row, me, "gather_tail")[:, 0, :]
    d_ada_all, packed = tail_all[:, :6 * D], tail_all[:, 6 * D:]
    d_ada_cols = lax.dynamic_slice(d_ada_all, (0, me * ADA_N), (N_DEV, ADA_N))

    def ada_update(cv, dav, w, m, v):
        g = _bdot(_silu(cv), dav, TN)
        delta, mn, vn = _adamw(w, g, m, v)
        return (g, delta, mn, vn), ()

    res["ada"] = [t[None] for t in _ada_update_call(ada_update, c_all, d_ada_cols, w_ada[0], m_w_ada[0], v_w_ada[0], _tile(D, 256, 16))]

    offs = [sum(lanes[:i]) for i in range(len(lanes))]

    def small_update(pk, dada, lg, *wmv_flat):
        tot = pk[0:1]
        for d in range(1, N_DEV):
            tot = tot + pk[d:d + 1]
        gb = dada[0:1]
        for d in range(1, N_DEV):
            gb = gb + dada[d:d + 1]
        gs = [tot[:, offs[i]:offs[i] + widths[i]] for i in range(len(widths))]
        _, lb_vjp = jax.vjp(_softmax0, lg)
        (g_lg,) = lb_vjp(gs[2])
        grads = [gb, gs[0], gs[1], g_lg, gs[3], gs[4], gs[5], gs[6], gs[7]]
        outs = []
        for i, g in enumerate(grads):
            w, m, v = wmv_flat[3 * i:3 * i + 3]
            delta, mn, vn = _adamw(w, g, m, v)
            outs += [g, delta, mn, vn]
        return tuple(outs)

    tbl = lambda t: t.reshape(1, N_BUCKETS * AH)
    small_wmv = [(b_ada, m_b_ada, v_b_ada), (norm1_g, m_norm1_g, v_norm1_g), (norm2_g, m_norm2_g, v_norm2_g),
                 (hg_lb_logits, m_hg_lb_logits, v_hg_lb_logits), (hg_out_norm_g, m_hg_out_norm_g, v_hg_out_norm_g),
                 (q_norm_g, m_q_norm_g, v_q_norm_g), (k_norm_g, m_k_norm_g, v_k_norm_g),
                 (attn_sinks, m_attn_sinks, v_attn_sinks),
                 (tbl(rel_bias_table), tbl(m_rel_bias_table), tbl(v_rel_bias_table))]
    flat = [t for trip in small_wmv for t in trip]
    out_shapes = [(trip[0].shape, F32) for trip in small_wmv for _ in range(4)]
    sres = _whole(small_update, [packed, d_ada_all, hg_lb_logits] + flat, out_shapes, "small_update")
    names_small = ("b_ada", "norm1_g", "norm2_g", "lb", "gout", "qg", "kg", "sinks", "table")
    for i, k in enumerate(names_small):
        r = sres[4 * i:4 * i + 4]
        if k == "table":
            r = [t.reshape(N_BUCKETS, AH) for t in r]
        res[k] = r

    order = ("ada", "b_ada", "norm1_g", "norm2_g", "in", "lb", "gout", "qg", "kg", "sinks", "table", "bhg", "bat", "out", "ff1", "ff2")
    outs = [loss, grad_x[None]]
    for j in range(4):
        outs += [res[k][j] for k in order]
    return tuple(outs)
```

```python
import functools
import math

import jax
import jax.numpy as jnp
from jax import lax
from jax.experimental import pallas as pl
from jax.experimental.pallas import tpu as pltpu

F32 = jnp.float32
BF16 = jnp.bfloat16
EPS = 1e-6
NEG_INF = -1e30
HG_DK = 128
HG_CHUNK = 64
AT_BLOCK = 128
N_BUCKETS = 32
MAX_EXACT = 16
MAX_DISTANCE = 128
N_DEV = 8
LANES = 128
VMEM_LIMIT = 56 * 1024 * 1024
ADAM_LR, ADAM_B1, ADAM_B2, ADAM_EPS, ADAM_WD, ADAM_STEP = 0.001, 0.9, 0.999, 1e-08, 0.01, 10
HIGHEST = lax.Precision.HIGHEST
MESH = pl.DeviceIdType.MESH
ANY = pl.BlockSpec(memory_space=pl.ANY)
CHIP_RELS = (0, 4, 2, 6)

NN = (((1,), (0,)), ((), ()))
NT = (((1,), (1,)), ((), ()))
TN = (((0,), (0,)), ((), ()))


def _tile(n, pref, unit):
    if n <= pref:
        return n
    t = (pref // unit) * unit
    while t >= unit:
        if n % t == 0:
            return t
        t -= unit
    return n


def _dot(a, b, dn, precision=None):
    return lax.dot_general(a, b, dn, preferred_element_type=F32, precision=precision)


def _bdot(a, b, dn):
    return _dot(a.astype(BF16), b.astype(BF16), dn)


def _position():
    x, y, c = lax.axis_index("x"), lax.axis_index("y"), lax.axis_index("c")
    return dict(x=x, y=y, c=c, me=4 * x + 2 * y + c)


def _peer_position(p, rel):
    x = 1 - p["x"] if rel & 4 else p["x"]
    y = 1 - p["y"] if rel & 2 else p["y"]
    c = 1 - p["c"] if rel & 1 else p["c"]
    return dict(x=x, y=y, c=c, me=4 * x + 2 * y + c)


class _Comm:
    def __init__(self):
        self.ins, self.outs, self.alias, self.plans, self.res = [], [], {}, [], None

    def inp(self, arr):
        self.ins.append(arr)
        return ("i", len(self.ins) - 1)

    def out(self, shape, dtype, alias=None):
        self.outs.append(jax.ShapeDtypeStruct(tuple(shape), dtype))
        if alias is not None:
            self.alias[alias[1]] = len(self.outs) - 1
        return ("o", len(self.outs) - 1)

    def copy(self, src, src_view, dst, dst_view, rel):
        self.plans.append((src, src_view, dst, dst_view, rel))

    def result(self, handle):
        return self.res[handle[1]]

    def build(self, in_refs, out_refs, send_sems, recv_sems):
        pos = _position()
        ref = lambda h: in_refs[h[1]] if h[0] == "i" else out_refs[h[1]]
        ops = []
        for k, (src, sv, dst, dv, rel) in enumerate(self.plans):
            s = sv(ref(src), pos)
            if rel == 0:
                cp = pltpu.make_async_copy(s, dv(ref(dst), pos), send_sems.at[k])
                ops.append((cp.start, cp.wait))
                continue
            peer = _peer_position(pos, rel)
            mk = lambda d: pltpu.make_async_remote_copy(
                src_ref=s, dst_ref=d, send_sem=send_sems.at[k], recv_sem=recv_sems.at[k],
                device_id=(peer["x"], peer["y"], peer["c"]), device_id_type=MESH)
            out_cp, in_cp = mk(dv(ref(dst), pos)), mk(dv(ref(dst), peer))

            def wait(out_cp=out_cp, in_cp=in_cp):
                out_cp.wait_send()
                in_cp.wait_recv()

            ops.append((out_cp.start, wait))
        return ops


def _call(body, args, *, name, out_shape, in_specs=None, out_specs=None, grid=None, scratch_shapes=(), comm=None,
          prefetch=None, aliases=None, after=()):
    single = not isinstance(out_shape, (tuple, list))
    out_shape = (out_shape,) if single else tuple(out_shape)
    n_in, n_out, n_scr = len(args), len(out_shape), len(scratch_shapes)
    vm = pl.BlockSpec(memory_space=pltpu.VMEM)
    in_specs = [vm] * n_in if in_specs is None else list(in_specs)
    out_specs = [vm] * n_out if out_specs is None else (list(out_specs) if isinstance(out_specs, (tuple, list)) else [out_specs])
    n_pf = 0 if prefetch is None else len(prefetch)
    kw = {} if aliases is None else {"input_output_aliases": dict(aliases)}
    if comm is None and after:
        n_dep = len(after)

        def fn(*refs):
            body(*refs[:n_pf + n_in], *refs[n_pf + n_in + n_dep:])

        all_args, all_scratch = list(args) + list(after), list(scratch_shapes)
        in_specs = in_specs + [ANY] * n_dep
    elif comm is None:
        fn = body
        all_args, all_scratch = list(args), list(scratch_shapes)
    else:
        n_ci, n_co, n_x = len(comm.ins), len(comm.outs), len(comm.plans)

        def fn(*refs):
            pf, refs = refs[:n_pf], refs[n_pf:]
            o_in, c_in = refs[:n_in], refs[n_in:n_in + n_ci]
            o_out = refs[n_in + n_ci:n_in + n_ci + n_out]
            c_out = refs[n_in + n_ci + n_out:n_in + n_ci + n_out + n_co]
            scr = refs[n_in + n_ci + n_out + n_co:]
            ops = comm.build(c_in, c_out, scr[n_scr], scr[n_scr + 1])
            if grid:
                first = functools.reduce(jnp.logical_and, [pl.program_id(i) == 0 for i in range(len(grid))])
                last = functools.reduce(jnp.logical_and, [pl.program_id(i) == g - 1 for i, g in enumerate(grid)])

                @pl.when(first)
                def _():
                    for start, _w in ops:
                        start()
            else:
                for start, _w in ops:
                    start()
            body(*pf, *o_in, *o_out, *scr[:n_scr])
            if grid:
                @pl.when(last)
                def _():
                    for _s, wait in ops:
                        wait()
            else:
                for _s, wait in ops:
                    wait()

        all_args = list(args) + list(comm.ins)
        in_specs = in_specs + [ANY] * n_ci
        out_shape = out_shape + tuple(comm.outs)
        out_specs = out_specs + [ANY] * n_co
        all_scratch = list(scratch_shapes) + [pltpu.SemaphoreType.DMA((n_x,)), pltpu.SemaphoreType.DMA((n_x,))]
        kw["input_output_aliases"] = {n_pf + n_in + i: n_out + o for i, o in comm.alias.items()}
    sem = None if grid is None else ("arbitrary",) * len(grid)
    params = pltpu.CompilerParams(dimension_semantics=sem, vmem_limit_bytes=VMEM_LIMIT)
    if prefetch is None:
        spec = dict(in_specs=in_specs, out_specs=tuple(out_specs), scratch_shapes=all_scratch)
        if grid is not None:
            spec["grid"] = grid
    else:
        spec = dict(grid_spec=pltpu.PrefetchScalarGridSpec(
            num_scalar_prefetch=n_pf, grid=grid, in_specs=in_specs, out_specs=tuple(out_specs), scratch_shapes=all_scratch))
        all_args = list(prefetch) + all_args
    res = pl.pallas_call(fn, name=name, out_shape=out_shape, compiler_params=params, **spec, **kw)(*all_args)
    res = list(res)
    if comm is not None:
        comm.res = res[n_out:]
        res = res[:n_out]
    return res[0] if single else res


def _whole_view(ref, pos):
    return ref


def _block_view(axis, n, index, rows=None):
    def view(ref, pos):
        off = pl.multiple_of(index(pos) * n, n)
        if rows is None:
            return ref.at[:, pl.ds(off, n)] if axis == 1 else ref.at[pl.ds(off, n), :]
        lo, cnt = rows[0], rows[1] - rows[0]
        if axis == 1:
            return ref.at[pl.ds(lo, cnt), pl.ds(off, n)]
        return ref.at[pl.ds(pl.multiple_of(off + lo, 16), cnt), :]
    return view


def _rows_view(rows):
    def view(ref, pos):
        return ref if rows is None else ref.at[pl.ds(rows[0], rows[1] - rows[0]), :]
    return view


def _slot_view(i, rows=None):
    def view(ref, pos):
        return ref.at[i] if rows is None else ref.at[i, pl.ds(rows[0], rows[1] - rows[0]), :]
    return view


def _exchange(items, name):
    cm = _Comm()
    for a, rel in items:
        cm.copy(cm.inp(a), _whole_view, cm.out(a.shape, a.dtype), _whole_view, rel)
    _call(lambda: None, [], name=name, out_shape=(), comm=cm)
    return cm.res


def _gather_small(v, me, name):
    cm = _Comm()
    hi, ho = cm.inp(v), cm.out((N_DEV,) + v.shape, v.dtype)
    for rel in range(N_DEV):
        cm.copy(hi, _whole_view, ho, lambda ref, p: ref.at[p["me"]], rel)
    _call(lambda: None, [], name=name, out_shape=(), comm=cm)
    return cm.result(ho)


def _ag_ici(cm, blk, axis, rows=None, into=None):
    n = blk.shape[axis]
    shape = list(blk.shape)
    shape[axis] = n * N_DEV
    hi = cm.inp(blk)
    ho = cm.out(shape, blk.dtype) if into is None else cm.out(shape, blk.dtype, alias=cm.inp(into))
    own = _block_view(axis, n, lambda p: p["me"], rows)
    for rel in CHIP_RELS:
        cm.copy(hi, _rows_view(rows), ho, own, rel)
    return ho


def _ag_d2d(cm, full, axis):
    n = full.shape[axis] // N_DEV
    hi = cm.inp(full)
    ho = cm.out(full.shape, full.dtype, alias=hi)
    for r in CHIP_RELS:
        v = _block_view(axis, n, functools.partial(lambda p, r: p["me"] ^ r, r=r))
        cm.copy(hi, v, ho, v, 1)
    return ho


def _rs_d2d(cm, gw, axis):
    n = gw.shape[axis] // N_DEV
    shape = list(gw.shape)
    shape[axis] = n
    hi, ho = cm.inp(gw), cm.out([4] + shape, gw.dtype)
    for i, r in enumerate(CHIP_RELS):
        cm.copy(hi, _block_view(axis, n, functools.partial(lambda p, r: p["me"] ^ r ^ 1, r=r)), ho, _slot_view(i), 1)
    return ho


def _rs_ici(cm, part, rows=None, recv=None):
    if recv is None:
        ho = cm.out((3,) + part.shape[1:], part.dtype)
    else:
        ho = cm.out(recv.shape, recv.dtype, alias=cm.inp(recv))
    hi = cm.inp(part)
    for i in (1, 2, 3):
        cm.copy(hi, _slot_view(i, rows), ho, _slot_view(i - 1, rows), CHIP_RELS[i])
    return ho


def _rs_add(gw, recv, axis, base, name, tw=None):
    _, R, n = recv.shape
    fan = 1
    if axis == 1:
        tw = n if tw is None else tw
        fan = max(f for f in (4, 3, 2, 1) if (n // tw) % f == 0)
        gw_specs = [pl.BlockSpec((R, tw), functools.partial(lambda i, t, b, k: (0, b[i] + fan * t + k), k=k)) for k in range(fan)]
        rv_spec = pl.BlockSpec((None, R, tw * fan), lambda i, t, b: (i, 0, t))
        grid = (4, n // (tw * fan))
    else:
        tw = _tile(n, 1024, LANES)
        gw_specs = [pl.BlockSpec((R, tw), lambda i, t, b: (b[i], t))]
        rv_spec = pl.BlockSpec((None, R, tw), lambda i, t, b: (i, 0, t))
        grid = (4, n // tw)

    def body(b_ref, *refs):
        g_refs, r_ref, o_ref = refs[:fan], refs[fan], refs[fan + 1]
        g = g_refs[0][...] if fan == 1 else jnp.concatenate([g[...] for g in g_refs], axis=1)
        o_ref[...] = (g.astype(F32) + r_ref[...].astype(F32)).astype(o_ref.dtype)

    return _call(body, [gw] * fan + [recv], name=name, out_shape=jax.ShapeDtypeStruct(recv.shape, recv.dtype), grid=grid,
                 in_specs=gw_specs + [rv_spec], out_specs=rv_spec, prefetch=[base])


HBM_SPEC = pl.BlockSpec(memory_space=pltpu.HBM)
SEM_SPEC = pl.BlockSpec(memory_space=pltpu.SEMAPHORE)
SPLIT_PARAMS = pltpu.CompilerParams(has_side_effects=pltpu.SideEffectType.DATAFLOW_SIDE_EFFECTING)


def _split_copies(refs, plans, send_sems, recv_sems):
    pos = _position()
    out = []
    for k, (si, sv, li, lv, rel) in enumerate(plans):
        peer = _peer_position(pos, rel)
        mk = lambda d: pltpu.make_async_remote_copy(
            src_ref=sv(refs[si], pos), dst_ref=d, send_sem=send_sems.at[k], recv_sem=recv_sems.at[k],
            device_id=(peer["x"], peer["y"], peer["c"]), device_id_type=MESH)
        out.append((mk(lv(refs[li], pos)), mk(lv(refs[li], peer))))
    return out


def _split_start(arrays, plans, name):
    n = len(arrays)

    def body(*refs):
        send_sems, recv_sems = refs[n], refs[n + 1]
        for out_cp, _ in _split_copies(refs[:n], plans, send_sems, recv_sems):
            out_cp.start()
        refs[-1][...] = jnp.zeros_like(refs[-1])

    sems = pltpu.SemaphoreType.DMA((len(plans),))
    res = pl.pallas_call(
        body, name=name,
        out_shape=(sems, sems) + tuple(pltpu.HBM(a.shape, a.dtype) for a in arrays) + (jax.ShapeDtypeStruct((8, LANES), F32),),
        in_specs=[HBM_SPEC] * n, out_specs=(SEM_SPEC, SEM_SPEC) + (HBM_SPEC,) * n + (pl.BlockSpec(memory_space=pltpu.VMEM),),
        input_output_aliases={i: 2 + i for i in range(n)}, compiler_params=SPLIT_PARAMS,
    )(*[pltpu.with_memory_space_constraint(a, pltpu.HBM) for a in arrays])
    return res[0], res[1], list(res[2:2 + n]), res[-1]


def _split_wait(send_sems, recv_sems, arrays, plans, after, name):
    n, na = len(arrays), len(after)

    def body(*refs):
        for out_cp, in_cp in _split_copies(refs[:n], plans, refs[n], refs[n + 1]):
            out_cp.wait_send()
            in_cp.wait_recv()

    res = pl.pallas_call(
        body, name=name, out_shape=tuple(pltpu.HBM(a.shape, a.dtype) for a in arrays),
        in_specs=[HBM_SPEC] * n + [SEM_SPEC, SEM_SPEC] + [ANY] * na, out_specs=(HBM_SPEC,) * n,
        input_output_aliases={i: i for i in range(n)}, compiler_params=SPLIT_PARAMS,
    )(*arrays, send_sems, recv_sems, *after)
    return list(res)


def _rs_split_start(parts, name):
    nw = len(parts)
    lands = [lax.empty((3,) + p.shape[1:], p.dtype) for p in parts]
    plans = [(s, _slot_view(i), nw + s, _slot_view(i - 1), CHIP_RELS[i]) for s in range(nw) for i in (1, 2, 3)]
    send_sems, recv_sems, arrays, token = _split_start(list(parts) + lands, plans, name)
    return dict(sems=(send_sems, recv_sems), arrays=arrays, plans=plans, token=token, nw=nw)


def _rs_split_wait(h, after, name):
    arrays = _split_wait(h["sems"][0], h["sems"][1], h["arrays"], h["plans"], after, name)
    return arrays[:h["nw"]], arrays[h["nw"]:]


def _behind(xs, tokens):
    out = lax.optimization_barrier((tuple(xs), tuple(tokens)))
    return list(out[0])


def _ag_w_in(src, a, D, INW):
    wm = LANES * a

    hd = D // 2
    ALL, TOP, BOT = (0, D), (0, hd), (hd, D)

    def main_place(ref, p, rows=ALL):
        off = pl.multiple_of(((2 * a + 1) * (p["me"] // 2) + (a + 1) * p["c"]) * LANES, LANES)
        return ref.at[pl.ds(rows[0], rows[1] - rows[0]), pl.ds(off, wm)]

    def main_src(ref, p):
        return ref.at[:, pl.ds(pl.multiple_of(p["c"] * LANES, LANES), wm)]

    def mid_src(ref, p):
        return ref.at[:, pl.ds(pl.multiple_of((1 - p["c"]) * wm, LANES), LANES)]

    def mid_place(ref, p, rows=ALL):
        return ref.at[p["me"], pl.ds(rows[0], rows[1] - rows[0]), :]

    def body(src_ref, full_ref, mid_ref, send_sems, recv_sems):
        pos = _position()
        sib, xn, yn = (_peer_position(pos, r) for r in (1, 4, 2))
        dg = _peer_position(pos, 6)
        started = []

        def remote(k, s, d, to):
            return pltpu.make_async_remote_copy(src_ref=s, dst_ref=d, send_sem=send_sems.at[k], recv_sem=recv_sems.at[k],
                                                device_id=(to["x"], to["y"], to["c"]), device_id_type=MESH)

        def send(k, owner, rows, to, from_src=False):
            for j, (src_v, place) in enumerate(((main_src, main_place), (mid_src, mid_place))):
                s = src_v(src_ref, pos) if from_src else place(full_ref if j == 0 else mid_ref, owner, rows)
                cp = remote(k + j, s, place(full_ref if j == 0 else mid_ref, owner, rows), to)
                cp.start()
                started.append(cp)

        def landed(k, owner, rows, frm):
            for j, place in enumerate((main_place, mid_place)):
                ref = full_ref if j == 0 else mid_ref
                remote(k + j, place(ref, owner, rows), place(ref, owner, rows), frm).wait_recv()

        local = [pltpu.make_async_copy(main_src(src_ref, pos), main_place(full_ref, pos), send_sems.at[18]),
                 pltpu.make_async_copy(mid_src(src_ref, pos), mid_place(mid_ref, pos), send_sems.at[19])]
        for cp in local:
            cp.start()
        send(0, pos, ALL, sib, from_src=True)
        send(2, pos, ALL, xn, from_src=True)
        send(4, pos, ALL, yn, from_src=True)
        landed(2, xn, ALL, xn)
        send(10, xn, ALL, sib)
        send(6, xn, TOP, yn)
        landed(4, yn, ALL, yn)
        send(12, yn, ALL, sib)
        send(8, yn, BOT, xn)
        landed(6, dg, TOP, yn)
        send(14, dg, TOP, sib)
        landed(8, dg, BOT, xn)
        send(16, dg, BOT, sib)
        sib_of = lambda p: _peer_position(p, 1)
        landed(0, sib, ALL, sib)
        landed(10, sib_of(xn), ALL, sib)
        landed(12, sib_of(yn), ALL, sib)
        landed(14, sib_of(dg), TOP, sib)
        landed(16, sib_of(dg), BOT, sib)
        for cp in started:
            cp.wait_send()
        for cp in local:
            cp.wait()

    return _call(body, [src], name="ag_w_in", in_specs=[ANY], out_specs=[ANY, ANY],
                 out_shape=(jax.ShapeDtypeStruct((D, INW), BF16), jax.ShapeDtypeStruct((N_DEV, D, LANES), BF16)),
                 scratch_shapes=[pltpu.SemaphoreType.DMA((20,)), pltpu.SemaphoreType.DMA((20,))])


def _patch_mid(full, mid, a):
    D = full.shape[0]

    def body(full_ref, e_ref, o_ref, out_ref):
        out_ref[...] = e_ref[...] + o_ref[...]

    return _call(body, [full, mid, mid], name="patch_mid", grid=(N_DEV // 2,),
                 out_shape=jax.ShapeDtypeStruct(full.shape, full.dtype),
                 in_specs=[ANY, pl.BlockSpec((None, D, LANES), lambda j: (2 * j, 0, 0)),
                           pl.BlockSpec((None, D, LANES), lambda j: (2 * j + 1, 0, 0))],
                 out_specs=pl.BlockSpec((D, LANES), lambda j: (0, (2 * a + 1) * j + a)), aliases={0: 0})


MM_RESIDENT = 2048


def _mm(a, b, mode, out_dtype, name, b_off=0, n=None, comm=None, extras=(), epi=None, tn=None, after=(), b_order=None):
    if mode == "nn":
        (M, K), (K2, N) = a.shape, b.shape
    elif mode == "nt":
        (M, K), (N, K2) = a.shape, b.shape
    else:
        (K, M), (K2, N) = a.shape, b.shape
    assert K == K2, (a.shape, b.shape, mode)
    if n is not None:
        N = n
    single = not isinstance(out_dtype, (tuple, list))
    out_dtypes = (out_dtype,) if single else tuple(out_dtype)
    if epi is None:
        epi = lambda r: (r,)
    tk = K if K <= MM_RESIDENT else (MM_RESIDENT if K % MM_RESIDENT == 0 else _tile(K, 512, LANES))
    nk = K // tk
    if M > MM_RESIDENT and mode == "tn" and N <= MM_RESIDENT and not b_off:
        tm, tn = _tile(M, 512, LANES), N
    elif nk > 1:
        tm, tn = _tile(M, 1024, LANES), _tile(N, tn or 1024, LANES)
    else:
        tm = _tile(M, MM_RESIDENT, LANES)
        tn = _tile(math.gcd(N, b_off) if b_off else N, tn or 512, LANES)
    jb = b_off // tn
    dn = {"nn": NN, "nt": NT, "tn": TN}[mode]
    ne, no = len(extras), len(out_dtypes)

    def body(a_ref, b_ref, *rest):
        e_refs, o_refs = rest[:ne], rest[ne:ne + no]

        def finish(r):
            for o_ref, v in zip(o_refs, epi(r, *[e[...] for e in e_refs])):
                o_ref[...] = v.astype(o_ref.dtype)

        if nk == 1:
            finish(_bdot(a_ref[...], b_ref[...], dn))
            return
        acc_ref = rest[ne + no]
        k = pl.program_id(2)

        @pl.when(k == 0)
        def _():
            acc_ref[...] = _bdot(a_ref[...], b_ref[...], dn)

        @pl.when(jnp.logical_and(k > 0, k < nk - 1))
        def _():
            acc_ref[...] += _bdot(a_ref[...], b_ref[...], dn)

        @pl.when(k == nk - 1)
        def _():
            finish(acc_ref[...] + _bdot(a_ref[...], b_ref[...], dn))

    a_spec = pl.BlockSpec((tk, tm), lambda i, j, k: (k, i)) if mode == "tn" else pl.BlockSpec((tm, tk), lambda i, j, k: (i, k))
    col = (lambda j: j + jb) if b_order is None else functools.partial(b_order, tn)
    b_spec = pl.BlockSpec((tn, tk), lambda i, j, k: (j, k)) if mode == "nt" else pl.BlockSpec((tk, tn), lambda i, j, k: (k, col(j)))
    o_spec = pl.BlockSpec((tm, tn), lambda i, j, k: (i, j))
    res = _call(body, [a, b] + list(extras), name=name, grid=(M // tm, N // tn, nk),
                out_shape=tuple(jax.ShapeDtypeStruct((M, N), dt) for dt in out_dtypes),
                in_specs=[a_spec, b_spec] + [o_spec] * ne, out_specs=[o_spec] * no,
                scratch_shapes=[pltpu.VMEM((tm, tn), F32)] if nk > 1 else [], comm=comm, after=after)
    return res[0] if single else res


def _rowwise(fn, row_ins, bcast_ins, row_outs, acc_outs, name, rt=256, comm=None):
    L = row_ins[0][0].shape[-2]
    rt = _tile(L, rt, 16)
    nr, nb, no = len(row_ins), len(bcast_ins), len(row_outs)

    def body(*refs):
        i = pl.program_id(0)
        vals = [r[...] for r in refs[:nr + nb]]
        outs, accs = fn(*vals)
        for r, v in zip(refs[nr + nb:nr + nb + no], outs):
            r[...] = v.astype(r.dtype)
        acc_refs = refs[nr + nb + no:]

        @pl.when(i == 0)
        def _():
            for r in acc_refs:
                r[...] = jnp.zeros_like(r)

        for r, v in zip(acc_refs, accs):
            r[...] += v

    in_specs = []
    for spec in row_ins:
        w, cb = spec[1], spec[2]
        if len(spec) == 4:
            in_specs.append(pl.BlockSpec((None, rt, w), functools.partial(lambda i, cb, ld: (ld, i, cb), cb=cb, ld=spec[3])))
        else:
            in_specs.append(pl.BlockSpec((rt, w), functools.partial(lambda i, cb: (i, cb), cb=cb)))
    in_specs += [pl.BlockSpec(b.shape, lambda i: (0, 0)) for b in bcast_ins]
    out_specs = [pl.BlockSpec((rt, w), lambda i: (i, 0)) for w, _ in row_outs]
    out_specs += [pl.BlockSpec(s, lambda i: (0, 0)) for s in acc_outs]
    out_shape = [jax.ShapeDtypeStruct((L, w), dt) for w, dt in row_outs] + [jax.ShapeDtypeStruct(s, F32) for s in acc_outs]
    return _call(body, [s[0] for s in row_ins] + list(bcast_ins), name=name, grid=(L // rt,), out_shape=tuple(out_shape),
                 in_specs=in_specs, out_specs=out_specs, comm=comm)


def _whole(fn, ins, out_shapes, name):
    def body(*refs):
        outs = fn(*[r[...] for r in refs[:len(ins)]])
        for r, v in zip(refs[len(ins):], outs):
            r[...] = v.astype(r.dtype)

    return _call(body, list(ins), name=name, out_shape=tuple(jax.ShapeDtypeStruct(s, dt) for s, dt in out_shapes))


def _silu(x):
    return x * jax.nn.sigmoid(x)


def _rms(x, g):
    return (x * lax.rsqrt(jnp.mean(x * x, axis=-1, keepdims=True) + EPS)) * g


def _modnorm(x, g, shift, scale):
    return _rms(x, g) * (1.0 + scale) + shift


def _adamw(w, g, m, v):
    m = ADAM_B1 * m + (1.0 - ADAM_B1) * g
    v = ADAM_B2 * v + (1.0 - ADAM_B2) * jnp.square(g)
    m_hat = m / (1.0 - ADAM_B1 ** ADAM_STEP)
    v_hat = v / (1.0 - ADAM_B2 ** ADAM_STEP)
    delta = -ADAM_LR * (m_hat / (jnp.sqrt(v_hat) + ADAM_EPS) + ADAM_WD * w)
    return delta, m, v


def _lower_bound(lg):
    e = jnp.exp(lg - jnp.max(lg, axis=0, keepdims=True))
    return e[0:1] / jnp.sum(e, axis=0, keepdims=True)


def _hg_stages(hq_l, hf_l, hi_l, lb):
    C = hq_l[0].shape[0]
    row = lax.broadcasted_iota(jnp.int32, (C, C), 0)
    col = lax.broadcasted_iota(jnp.int32, (C, C), 1)
    tri = row >= col
    trif = tri.astype(F32)
    f_l = [lb + (1.0 - lb) * jax.nn.sigmoid(hf) for hf in hf_l]
    b_l = [_dot(trif, jnp.log(f), NN, precision=HIGHEST) for f in f_l]
    q_l = [_silu(hq) for hq in hq_l]
    m_l = [b[C // 2 - 1:C // 2] for b in b_l]
    bl_l = [b[C - 1:C] for b in b_l]
    sc_l = [jnp.where(tri, _bdot(q * jnp.exp(b - m), (1.0 - f) * jnp.exp(m - b), NT), 0.0)
            for q, f, b, m in zip(q_l, f_l, b_l, m_l)]
    o1_l = [_bdot(sc, hi, NN) for sc, hi in zip(sc_l, hi_l)]
    u_l = [_bdot(hi, (1.0 - f) * jnp.exp(bl - b), TN) for hi, f, b, bl in zip(hi_l, f_l, b_l, bl_l)]
    qb_l = [q * jnp.exp(b) for q, b in zip(q_l, b_l)]
    dec_l = [jnp.exp(bl) for bl in bl_l]
    return list(zip(o1_l, u_l, qb_l, dec_l))


def _hg_out(o, hgate, gout):
    return _rms(o, gout) * _silu(hgate)


HG_STAGE = 8
HG_GROUP = 32


def _hgrn_fwd(p4, lb_logits, gout, H, comm=None):
    L = p4.shape[0]
    C = HG_CHUNK
    GR = _tile(L // C, HG_GROUP, 1)
    T = GR * C
    N = L // T

    def body(hq_ref, hf_ref, hi_ref, hg_ref, lg_ref, gout_ref, o_ref, s_ref, st_ref):
        @pl.when(pl.program_id(1) == 0)
        def _():
            st_ref[...] = jnp.zeros_like(st_ref)

        lb = _lower_bound(lg_ref[...])
        st = st_ref[...]
        for c0 in range(0, GR, HG_STAGE):
            rows_l = [pl.ds(ci * C, C) for ci in range(c0, min(c0 + HG_STAGE, GR))]
            parts = _hg_stages([hq_ref[r, :] for r in rows_l], [hf_ref[r, :] for r in rows_l],
                               [hi_ref[r, :] for r in rows_l], lb)
            for ci, rows, (o1, u, qb, dec) in zip(range(c0, GR), rows_l, parts):
                s_ref[0, ci] = st
                o = o1 + _bdot(qb, st, NT)
                st = st * dec + u
                o_ref[rows, :] = _hg_out(o, hg_ref[rows, :], gout_ref[...]).astype(o_ref.dtype)
        st_ref[...] = st

    blk = lambda s: pl.BlockSpec((T, HG_DK), functools.partial(lambda h, n, s: (n, s * H + h), s=s))
    return _call(
        body, [p4, p4, p4, p4, lb_logits, gout], name="hgrn_fwd", grid=(H, N),
        out_shape=(jax.ShapeDtypeStruct((L, H * HG_DK), BF16), jax.ShapeDtypeStruct((H, N * GR, HG_DK, HG_DK), F32)),
        in_specs=[blk(0), blk(1), blk(2), blk(3), pl.BlockSpec((2, HG_DK), lambda h, n: (0, h)),
                  pl.BlockSpec((1, HG_DK), lambda h, n: (0, 0))],
        out_specs=(pl.BlockSpec((T, HG_DK), lambda h, n: (n, h)),
                   pl.BlockSpec((1, GR, HG_DK, HG_DK), lambda h, n: (h, n, 0, 0))),
        scratch_shapes=[pltpu.VMEM((HG_DK, HG_DK), F32)], comm=comm)


def _hgrn_bwd(p4, lb_logits, gout, s_all, d_out, H, comm=None):
    L = p4.shape[0]
    C = HG_CHUNK
    GR = _tile(L // C, HG_GROUP, 1)
    T = GR * C
    N = L // T

    def body(hq_ref, hf_ref, hi_ref, hg_ref, lg_ref, gout_ref, s_ref, do_ref,
             dq_ref, df_ref, di_ref, dg_ref, dlb_ref, dgo_ref, dst_ref):
        @pl.when(pl.program_id(1) == 0)
        def _():
            dst_ref[...] = jnp.zeros_like(dst_ref)
            dlb_ref[...] = jnp.zeros_like(dlb_ref)

        @pl.when(jnp.logical_and(pl.program_id(0) == 0, pl.program_id(1) == 0))
        def _():
            dgo_ref[...] = jnp.zeros_like(dgo_ref)

        lb = _lower_bound(lg_ref[...])
        dst = dst_ref[...]
        d_lb = jnp.zeros((1, HG_DK), F32)
        d_go = jnp.zeros((1, HG_DK), F32)
        for c0 in reversed(range(0, GR, HG_STAGE)):
            dst, d_lb_c, d_go_c = chunks_bwd(list(range(c0, min(c0 + HG_STAGE, GR))), lb, dst, hq_ref, hf_ref, hi_ref,
                                             hg_ref, gout_ref, s_ref, do_ref, dq_ref, df_ref, di_ref, dg_ref)
            d_lb += d_lb_c
            d_go += d_go_c
        dst_ref[...] = dst
        dlb_ref[...] += d_lb
        dgo_ref[...] += d_go

    def chunks_bwd(idx, lb, dst, hq_ref, hf_ref, hi_ref, hg_ref, gout_ref, s_ref, do_ref, dq_ref, df_ref, di_ref, dg_ref):
        n = len(idx)
        rows_l = [pl.ds(ci * C, C) for ci in idx]
        hq_l, hf_l, hi_l = ([r[rows, :] for rows in rows_l] for r in (hq_ref, hf_ref, hi_ref))
        st_l = [s_ref[0, ci] for ci in idx]
        row = lax.broadcasted_iota(jnp.int32, (C, C), 0)
        col = lax.broadcasted_iota(jnp.int32, (C, C), 1)
        tri = row >= col
        trif = tri.astype(F32)
        every = lambda fn, *ls: [fn(*a) for a in zip(*ls)]
        sg_l = every(jax.nn.sigmoid, hf_l)
        f_l = every(lambda sg: lb + (1.0 - lb) * sg, sg_l)
        b_l = every(lambda f: _dot(trif, jnp.log(f), NN, precision=HIGHEST), f_l)
        q_l = every(_silu, hq_l)
        m_l = every(lambda b: b[C // 2 - 1:C // 2], b_l)
        bl_l = every(lambda b: b[C - 1:C], b_l)
        e_qm_l = every(lambda b, m: jnp.exp(b - m), b_l, m_l)
        e_km_l = every(lambda b, m: jnp.exp(m - b), b_l, m_l)
        e_kl_l = every(lambda b, bl: jnp.exp(bl - b), b_l, bl_l)
        e_q_l = every(jnp.exp, b_l)
        dec_l = every(jnp.exp, bl_l)
        qe_l = every(lambda q, e: q * e, q_l, e_qm_l)
        ke_l = every(lambda f, e: (1.0 - f) * e, f_l, e_km_l)
        kd_l = every(lambda f, e: (1.0 - f) * e, f_l, e_kl_l)
        qb_l = every(lambda q, e: q * e, q_l, e_q_l)
        sc_l = every(lambda qe, ke: jnp.where(tri, _bdot(qe, ke, NT), 0.0), qe_l, ke_l)
        o_l = every(lambda sc, hi, qb, st: _bdot(sc, hi, NN) + _bdot(qb, st, NT), sc_l, hi_l, qb_l, st_l)
        vj_l = every(lambda o, rows: jax.vjp(_hg_out, o, hg_ref[rows, :], gout_ref[...])[1](do_ref[rows, :]), o_l, rows_l)
        do_l = [v[0] for v in vj_l]
        dsc_l = every(lambda do, hi: jnp.where(tri, _bdot(do, hi, NT), 0.0), do_l, hi_l)
        dv1_l = every(lambda sc, do: _bdot(sc, do, TN), sc_l, do_l)
        dqe_l = every(lambda dsc, ke: _bdot(dsc, ke, NN), dsc_l, ke_l)
        dke_l = every(lambda dsc, qe: _bdot(dsc, qe, TN), dsc_l, qe_l)
        dqb_l = every(lambda do, st: _bdot(do, st, NN), do_l, st_l)
        own_l = every(lambda do, qb: _bdot(do, qb, TN), do_l, qb_l)
        dst_next_l = [None] * n
        for j in reversed(range(n)):
            dst_next_l[j] = dst
            dst = own_l[j] + dst * dec_l[j]
        dv_l = every(lambda dv1, kd, dn: dv1 + _bdot(kd, dn, NT), dv1_l, kd_l, dst_next_l)
        dkd_l = every(lambda hi, dn: _bdot(hi, dn, NN), hi_l, dst_next_l)
        ddec_l = every(lambda dn, st: jnp.sum(dn * st, axis=0, keepdims=True), dst_next_l, st_l)
        rowi = lax.broadcasted_iota(jnp.int32, (C, HG_DK), 0)
        tq_l = every(lambda a, b_: a * b_, dqe_l, qe_l)
        tk_l = every(lambda a, b_: a * b_, dke_l, ke_l)
        td_l = every(lambda a, b_: a * b_, dkd_l, kd_l)
        tb_l = every(lambda a, b_: a * b_, dqb_l, qb_l)
        db_l = every(lambda tq, tk, td, tb, ddec, dec: tq - tk - td + tb
                     + jnp.where(rowi == C // 2 - 1, jnp.sum(tk - tq, axis=0, keepdims=True), 0.0)
                     + jnp.where(rowi == C - 1, jnp.sum(td, axis=0, keepdims=True) + ddec * dec, 0.0),
                     tq_l, tk_l, td_l, tb_l, ddec_l, dec_l)
        dlf_l = every(lambda db: _dot(trif, db, TN, precision=HIGHEST), db_l)
        dk_l = every(lambda dke, e1, dkd, e2: dke * e1 + dkd * e2, dke_l, e_km_l, dkd_l, e_kl_l)
        df_l = every(lambda dlf, f, dk: dlf / f - dk, dlf_l, f_l, dk_l)
        d_lb = jnp.zeros((1, HG_DK), F32)
        d_go = jnp.zeros((1, HG_DK), F32)
        for j, rows in enumerate(rows_l):
            sg, hq = sg_l[j], hq_l[j]
            df_ref[rows, :] = (df_l[j] * (1.0 - lb) * sg * (1.0 - sg)).astype(df_ref.dtype)
            sq = jax.nn.sigmoid(hq)
            dq = dqe_l[j] * e_qm_l[j] + dqb_l[j] * e_q_l[j]
            dq_ref[rows, :] = (dq * (sq * (1.0 + hq * (1.0 - sq)))).astype(dq_ref.dtype)
            di_ref[rows, :] = dv_l[j].astype(di_ref.dtype)
            dg_ref[rows, :] = vj_l[j][1].astype(dg_ref.dtype)
            d_lb += jnp.sum(df_l[j] * (1.0 - sg), axis=0, keepdims=True)
            d_go += vj_l[j][2]
        return dst, d_lb, d_go

    blk = lambda s: pl.BlockSpec((T, HG_DK), functools.partial(lambda h, n, s: (N - 1 - n, s * H + h), s=s))
    oblk = pl.BlockSpec((T, HG_DK), lambda h, n: (N - 1 - n, h))
    vec = pl.BlockSpec((1, HG_DK), lambda h, n: (0, h))
    W = H * HG_DK
    return _call(
        body, [p4, p4, p4, p4, lb_logits, gout, s_all, d_out], name="hgrn_bwd", grid=(H, N),
        out_shape=tuple([jax.ShapeDtypeStruct((L, W), BF16)] * 4 + [jax.ShapeDtypeStruct((1, W), F32), jax.ShapeDtypeStruct((1, HG_DK), F32)]),
        in_specs=[blk(0), blk(1), blk(2), blk(3), pl.BlockSpec((2, HG_DK), lambda h, n: (0, h)),
                  pl.BlockSpec((1, HG_DK), lambda h, n: (0, 0)),
                  pl.BlockSpec((1, GR, HG_DK, HG_DK), lambda h, n: (h, N - 1 - n, 0, 0)), oblk],
        out_specs=(oblk, oblk, oblk, oblk, vec, pl.BlockSpec((1, HG_DK), lambda h, n: (0, 0))),
        scratch_shapes=[pltpu.VMEM((HG_DK, HG_DK), F32)], comm=comm)


def _bucket_ids():
    i = jnp.arange(AT_BLOCK, dtype=jnp.int32)[:, None]
    j = jnp.arange(2 * AT_BLOCK, dtype=jnp.int32)[None, :]
    n = jnp.maximum(i - j + AT_BLOCK, 0)
    nf = jnp.maximum(n, 1).astype(F32)
    large = MAX_EXACT + (jnp.log(nf / MAX_EXACT) / math.log(MAX_DISTANCE / MAX_EXACT) * (N_BUCKETS - MAX_EXACT)).astype(jnp.int32)
    large = jnp.minimum(large, N_BUCKETS - 1)
    return jnp.where(n < MAX_EXACT, n, large).reshape(1, -1)


def _onehot(bucket):
    ids = lax.broadcasted_iota(jnp.int32, (N_BUCKETS, bucket.shape[1]), 0)
    return (ids == bucket).astype(F32)


def _attn_probs(qn, kpn, kcn, bias_g, sink, first, scale):
    rows = qn.shape[0]
    i = jnp.bitwise_and(lax.broadcasted_iota(jnp.int32, (rows, AT_BLOCK), 0), AT_BLOCK - 1)
    j = lax.broadcasted_iota(jnp.int32, (rows, AT_BLOCK), 1)
    lp = _bdot(qn, kpn, NT) * scale + bias_g[:, :AT_BLOCK]
    lc = _bdot(qn, kcn, NT) * scale + bias_g[:, AT_BLOCK:]
    lp = jnp.where(jnp.logical_and(j > i, jnp.logical_not(first)), lp, NEG_INF)
    lc = jnp.where(j <= i, lc, NEG_INF)
    m = jnp.maximum(jnp.maximum(jnp.max(lp, axis=-1, keepdims=True), jnp.max(lc, axis=-1, keepdims=True)), sink)
    pp, pc, ps = jnp.exp(lp - m), jnp.exp(lc - m), jnp.exp(sink - m)
    den = jnp.sum(pp, axis=-1, keepdims=True) + jnp.sum(pc, axis=-1, keepdims=True) + ps
    return pp / den, pc / den, ps / den


def _sink_rows(sk_ref, G):
    head = lax.broadcasted_iota(jnp.int32, (G * AT_BLOCK, 1), 0) // AT_BLOCK
    sink = jnp.zeros((G * AT_BLOCK, 1), F32)
    for g in range(G):
        sink = jnp.where(head == g, sk_ref[0, g:g + 1, :], sink)
    return sink


def _attn_fwd(q_t, kp, vp, qg, kg, sinks, bias, KVH, comm=None):
    AH, L, DH = q_t.shape
    G = AH // KVH
    NB = L // AT_BLOCK
    scale = DH ** -0.5

    def body(q_ref, kp_ref, kc_ref, vp_ref, vc_ref, qg_ref, kg_ref, sk_ref, b_ref, o_ref):
        first = pl.program_id(1) == 0
        kpn, kcn = _rms(kp_ref[0], kg_ref[...]), _rms(kc_ref[0], kg_ref[...])
        qn = _rms(q_ref[...].reshape(G * AT_BLOCK, DH), qg_ref[...])
        sink = _sink_rows(sk_ref, G)
        pp, pc, _ = _attn_probs(qn, kpn, kcn, b_ref[...].reshape(G * AT_BLOCK, 2 * AT_BLOCK), sink, first, scale)
        o = _bdot(pp, vp_ref[0], NN) + _bdot(pc, vc_ref[0], NN)
        o_ref[...] = o.reshape(G, AT_BLOCK, DH).astype(o_ref.dtype)

    kblk = lambda off: pl.BlockSpec((1, AT_BLOCK, DH),
                                    functools.partial(lambda h, n, off: (h, jnp.maximum(n + off - 1, 0), 0), off=off))
    return _call(
        body, [q_t, kp, kp, vp, vp, qg, kg, sinks, bias], name="attn_fwd", grid=(KVH, NB),
        out_shape=jax.ShapeDtypeStruct((AH, L, DH), BF16),
        in_specs=[pl.BlockSpec((G, AT_BLOCK, DH), lambda h, n: (h, n, 0)), kblk(0), kblk(1), kblk(0), kblk(1),
                  pl.BlockSpec((1, DH), lambda h, n: (0, 0)), pl.BlockSpec((1, DH), lambda h, n: (0, 0)),
                  pl.BlockSpec((1, G, 1), lambda h, n: (h, 0, 0)),
                  pl.BlockSpec((G, AT_BLOCK, 2 * AT_BLOCK), lambda h, n: (h, 0, 0))],
        out_specs=pl.BlockSpec((G, AT_BLOCK, DH), lambda h, n: (h, n, 0)), comm=comm)


def _attn_bwd(q_t, kp, vp, qg, kg, sinks, bias, do_t, KVH, comm=None):
    AH, L, DH = q_t.shape
    G = AH // KVH
    NB = L // AT_BLOCK
    B = AT_BLOCK
    scale = DH ** -0.5

    def body(q_ref, kp_ref, kc_ref, vp_ref, vc_ref, qg_ref, kg_ref, sk_ref, b_ref, do_ref,
             dq_ref, dk_ref, dv_ref, dqg_ref, dkg_ref, dsk_ref, db_ref):
        n = pl.program_id(1)
        first = n == 0

        @pl.when(first)
        def _():
            for r in (dk_ref, dv_ref, dsk_ref, db_ref):
                r[...] = jnp.zeros_like(r)

        @pl.when(jnp.logical_and(first, pl.program_id(0) == 0))
        def _():
            dqg_ref[...] = jnp.zeros_like(dqg_ref)
            dkg_ref[...] = jnp.zeros_like(dkg_ref)

        kp_raw, kc_raw, kgv, qgv = kp_ref[0], kc_ref[0], kg_ref[...], qg_ref[...]
        kpn, kp_vjp = jax.vjp(_rms, kp_raw, kgv)
        kcn, kc_vjp = jax.vjp(_rms, kc_raw, kgv)
        qn, q_vjp = jax.vjp(_rms, q_ref[...].reshape(G * B, DH), qgv)
        pp, pc, ps = _attn_probs(qn, kpn, kcn, b_ref[...].reshape(G * B, 2 * B), _sink_rows(sk_ref, G), first, scale)
        do = do_ref[...].reshape(G * B, DH)
        dvp = _bdot(pp, do, TN)
        dvc = _bdot(pc, do, TN)
        dpp = _bdot(do, vp_ref[0], NT)
        dpc = _bdot(do, vc_ref[0], NT)
        dsum = jnp.sum(dpp * pp, axis=-1, keepdims=True) + jnp.sum(dpc * pc, axis=-1, keepdims=True)
        dlp = pp * (dpp - dsum)
        dlc = pc * (dpc - dsum)
        dsk_ref[0] += jnp.sum((-ps * dsum).reshape(G, B, 1), axis=1)
        db_ref[:, :, :B] += dlp.reshape(G, B, B)
        db_ref[:, :, B:] += dlc.reshape(G, B, B)
        dlp, dlc = dlp * scale, dlc * scale
        dqn = _bdot(dlp, kpn, NN) + _bdot(dlc, kcn, NN)
        dq_raw, dqg = q_vjp(dqn)
        dq_ref[...] = dq_raw.reshape(G, B, DH).astype(dq_ref.dtype)
        dkp_raw, dkg_p = kp_vjp(_bdot(dlp, qn, TN))
        dkc_raw, dkg_c = kc_vjp(_bdot(dlc, qn, TN))
        r0 = pl.multiple_of(jnp.maximum(n - 1, 0) * B, B)
        r1 = pl.multiple_of(n * B, B)
        dk_ref[0, pl.ds(r0, B), :] += dkp_raw
        dk_ref[0, pl.ds(r1, B), :] += dkc_raw
        dv_ref[0, pl.ds(r0, B), :] += dvp
        dv_ref[0, pl.ds(r1, B), :] += dvc
        dqg_ref[...] += dqg
        dkg_ref[...] += dkg_p + dkg_c

    kblk = lambda off: pl.BlockSpec((1, B, DH), functools.partial(lambda h, n, off: (h, jnp.maximum(n + off - 1, 0), 0), off=off))
    qblk = pl.BlockSpec((G, B, DH), lambda h, n: (h, n, 0))
    accblk = pl.BlockSpec((1, L, DH), lambda h, n: (h, 0, 0))
    vecblk = pl.BlockSpec((1, DH), lambda h, n: (0, 0))
    return _call(
        body, [q_t, kp, kp, vp, vp, qg, kg, sinks, bias, do_t], name="attn_bwd", grid=(KVH, NB),
        out_shape=(jax.ShapeDtypeStruct((AH, L, DH), BF16), jax.ShapeDtypeStruct((KVH, L, DH), F32),
                   jax.ShapeDtypeStruct((KVH, L, DH), F32), jax.ShapeDtypeStruct((1, DH), F32),
                   jax.ShapeDtypeStruct((1, DH), F32), jax.ShapeDtypeStruct((KVH, G, 1), F32),
                   jax.ShapeDtypeStruct((AH, B, 2 * B), F32)),
        in_specs=[qblk, kblk(0), kblk(1), kblk(0), kblk(1),
                  pl.BlockSpec((1, DH), lambda h, n: (0, 0)), pl.BlockSpec((1, DH), lambda h, n: (0, 0)),
                  pl.BlockSpec((1, G, 1), lambda h, n: (h, 0, 0)),
                  pl.BlockSpec((G, B, 2 * B), lambda h, n: (h, 0, 0)), qblk],
        out_specs=(qblk, accblk, accblk, vecblk, vecblk, pl.BlockSpec((1, G, 1), lambda h, n: (h, 0, 0)),
                   pl.BlockSpec((G, B, 2 * B), lambda h, n: (h, 0, 0))), comm=comm)


def _heads_first(t, nh):
    L = t.shape[0]
    return jnp.transpose(t.reshape(L, nh, t.shape[1] // nh), (1, 0, 2))


def _heads_last(t):
    nh, L, dh = t.shape
    return jnp.transpose(t, (1, 0, 2)).reshape(L, nh * dh)


def _softmax0(lg):
    e = jnp.exp(lg - jnp.max(lg, axis=0, keepdims=True))
    return e[0:1] / jnp.sum(e, axis=0, keepdims=True)


def _ada_update_call(fn, c_all, d_cols, w, m, v, rt):
    D, n = w.shape

    def body(c_ref, d_ref, w_ref, m_ref, v_ref, g_out, dl_out, m_out, v_out):
        outs, _ = fn(c_ref[...], d_ref[...], w_ref[...], m_ref[...], v_ref[...])
        for r, val in zip((g_out, dl_out, m_out, v_out), outs):
            r[...] = val

    wblk = pl.BlockSpec((rt, n), lambda i: (i, 0))
    return _call(
        body, [c_all, d_cols, w, m, v], name="update_ada", grid=(D // rt,), out_shape=tuple([jax.ShapeDtypeStruct((D, n), F32)] * 4),
        in_specs=[pl.BlockSpec((N_DEV, rt), lambda i: (0, i)), pl.BlockSpec((N_DEV, n), lambda i: (0, 0)), wblk, wblk, wblk],
        out_specs=(wblk, wblk, wblk, wblk))


def kernel(x, c, w_ada, b_ada, norm1_g, norm2_g, w_in, hg_lb_logits, hg_out_norm_g, q_norm_g, k_norm_g, attn_sinks, rel_bias_table, w_branch_hg, w_branch_attn, w_out, w_ff1, w_ff2, loss_target, m_w_ada, m_b_ada, m_norm1_g, m_norm2_g, m_w_in, m_hg_lb_logits, m_hg_out_norm_g, m_q_norm_g, m_k_norm_g, m_attn_sinks, m_rel_bias_table, m_w_branch_hg, m_w_branch_attn, m_w_out, m_w_ff1, m_w_ff2, v_w_ada, v_b_ada, v_norm1_g, v_norm2_g, v_w_in, v_hg_lb_logits, v_hg_out_norm_g, v_q_norm_g, v_k_norm_g, v_attn_sinks, v_rel_bias_table, v_w_branch_hg, v_w_branch_attn, v_w_out, v_w_ff1, v_w_ff2):
    cc = lax.axis_index("c")
    me = 4 * lax.axis_index("x") + 2 * lax.axis_index("y") + cc
    x2 = x[0]
    tgt = loss_target[0]
    L, D = x2.shape
    HGW = hg_lb_logits.shape[1]
    H = HGW // HG_DK
    AH = attn_sinks.shape[1]
    DH = q_norm_g.shape[1]
    ATW = AH * DH
    BW = w_in.shape[2]
    INW = BW * N_DEV
    A = BW // LANES
    assert BW == LANES * A + LANES // 2
    KVW = (INW - 4 * HGW - ATW - 2 * D) // 2
    KVH = KVW // DH
    G = AH // KVH
    ADA_N = w_ada.shape[2]
    PAIR = 2 * A + 1

    c_all = _gather_small(c, me, "gather_c")[:, 0, :]
    b_cols = lax.dynamic_slice(b_ada, (0, me * ADA_N), (1, ADA_N))
    (ada_cols,) = _whole(lambda cv, w, b: (_bdot(_silu(cv), w, NN) + b,), [c_all, w_ada[0], b_cols],
                         [((N_DEV, ADA_N), F32)], "ada_fwd")
    ada_all = _gather_small(ada_cols, me, "gather_ada")
    ada_row = lax.dynamic_slice(ada_all, (0, me, 0), (N_DEV, 1, ADA_N)).reshape(1, 6 * D)

    w_in_b = w_in[0].astype(BF16)
    src_in = jnp.where(cc == 0, jnp.pad(w_in_b, ((0, 0), (0, LANES // 2))), jnp.pad(w_in_b, ((0, 0), (LANES // 2, 0))))
    (src_in,) = _behind([src_in], [ada_row])
    shift1, scale1, gate1, shift2, scale2, gate2 = [ada_row[:, i * D:(i + 1) * D] for i in range(6)]
    w_in_gapped, w_in_mid = _ag_w_in(src_in, A, D, INW)
    w_in_full = _patch_mid(w_in_gapped, w_in_mid, A)

    wnames = ("bhg", "bat", "out", "ff1", "ff2")
    small = ("bhg", "bat", "out")
    waxis = dict(zip(wnames, (1, 1, 0, 1, 0)))
    wsrc = dict(zip(wnames, (w_branch_hg, w_branch_attn, w_out, w_ff1, w_ff2)))
    wblk = {k: wsrc[k][0].astype(BF16) for k in wnames}
    wf = {}

    (h,) = _rowwise(lambda xv, g, sh, sc: ((_modnorm(xv, g, sh, sc),), ()), [(x2, D, 0)], [norm1_g, shift1, scale1],
                    [(D, BF16)], [], "norm1")
    o4, oa = 4 * HGW, 4 * HGW + ATW + 2 * KVW
    r1, r2 = wblk["ff1"].shape[0], wblk["ff2"].shape[0]
    assert o4 % D == 0

    def proj_order(tn_, j):
        t4, tg, ng = o4 // tn_, oa // tn_, (INW - oa) // tn_
        return jnp.where(j < t4, j, jnp.where(j < t4 + ng, j + (tg - t4), j - ng))

    cm = _Comm()
    hs = {k: _ag_ici(cm, wblk[k], waxis[k]) for k in small}
    hs["ff2"] = _ag_ici(cm, wblk["ff2"], waxis["ff2"], rows=(0, r2 // 4))
    proj = _mm(h, w_in_full, "nn", F32, "proj", comm=cm, b_order=proj_order)
    half = {k: cm.result(hs[k]) for k in hs}
    p4 = pg = proj
    GATE0 = o4 // D
    pa = proj[:, o4 + (INW - oa):]

    cm = _Comm()
    hs = {k: _ag_d2d(cm, half[k], waxis[k]) for k in small}
    hs["ff1"] = _ag_ici(cm, wblk["ff1"], waxis["ff1"], rows=(0, r1 // 2))
    o_hg, s_all = _hgrn_fwd(p4, hg_lb_logits, hg_out_norm_g, H, comm=cm)
    wf["bhg"], wf["bat"], wf["out"], half["ff1"] = (cm.result(hs[k]) for k in ("bhg", "bat", "out", "ff1"))

    bucket = _bucket_ids()
    (bias_flat,) = _whole(lambda tb, bk: (_dot(tb, _onehot(bk), TN, precision=HIGHEST),), [rel_bias_table, bucket],
                          [((AH, AT_BLOCK * 2 * AT_BLOCK), F32)], "bias_fwd")
    bias = bias_flat.reshape(AH, AT_BLOCK, 2 * AT_BLOCK)
    q_t = _heads_first(pa[:, :ATW], AH)
    kp = _heads_first(pa[:, ATW:ATW + KVW], KVH)
    vp = _heads_first(pa[:, ATW + KVW:], KVH)
    sinks3 = attn_sinks.reshape(KVH, G, 1)
    cm = _Comm()
    hs = {"ff1": _ag_ici(cm, wblk["ff1"], waxis["ff1"], rows=(r1 // 2, r1), into=half["ff1"])}
    o_at = _heads_last(_attn_fwd(q_t, kp, vp, q_norm_g, k_norm_g, sinks3, bias, KVH, comm=cm))
    half["ff1"] = cm.result(hs["ff1"])

    bh = _mm(o_hg, wf["bhg"], "nn", F32, "branch_hg")
    ba = _mm(o_at, wf["bat"], "nn", F32, "branch_at")

    def merge_fn(bhv, bav, ghg, gat):
        return jax.nn.sigmoid(ghg) * bhv + jax.nn.sigmoid(gat) * bav

    cm = _Comm()
    hs = {"ff1": _ag_d2d(cm, half["ff1"], waxis["ff1"]),
          "ff2": _ag_ici(cm, wblk["ff2"], waxis["ff2"], rows=(r2 // 4, 3 * r2 // 8), into=half["ff2"])}
    (merged,) = _rowwise(lambda *a: ((merge_fn(*a),), ()), [(bh, D, 0), (ba, D, 0), (pg, D, GATE0), (pg, D, GATE0 + 1)], [],
                         [(D, BF16)], [], "merge", comm=cm)
    wf["ff1"], half["ff2"] = cm.result(hs["ff1"]), cm.result(hs["ff2"])
    cm = _Comm()
    hs = {"ff2": _ag_ici(cm, wblk["ff2"], waxis["ff2"], rows=(3 * r2 // 8, r2 // 2), into=half["ff2"])}
    mo = _mm(merged, wf["out"], "nn", F32, "out_proj", comm=cm)
    half["ff2"] = cm.result(hs["ff2"])

    def resid1(xv, mov, g1, g2n, sh, sc):
        x1v = xv + g1 * mov
        return (x1v, _modnorm(x1v, g2n, sh, sc)), ()

    x1, h2 = _rowwise(resid1, [(x2, D, 0), (mo, D, 0)], [gate1, norm2_g, shift2, scale2], [(D, F32), (D, BF16)], [], "resid1")
    cm = _Comm()
    hs = {"ff2": _ag_ici(cm, wblk["ff2"], waxis["ff2"], rows=(r2 // 2, r2), into=half["ff2"])}
    u, act = _mm(h2, wf["ff1"], "nn", (F32, BF16), "ff1", comm=cm, epi=lambda r: (r, jnp.square(jnp.maximum(r, 0.0))))
    half["ff2"] = cm.result(hs["ff2"])
    cm = _Comm()
    hs = {"ff2": _ag_d2d(cm, half["ff2"], waxis["ff2"])}
    _call(lambda: None, [], name="ag_d2d_ff2", out_shape=(), comm=cm)
    wf["ff2"] = cm.result(hs["ff2"])
    ff = _mm(act, wf["ff2"], "nn", F32, "ff2")

    def loss_fn(x1v, ffv, tv, g2):
        e = x1v + g2 * ffv - tv
        dy = e * (1.0 / D)
        return (dy, dy * g2), (jnp.sum(e * e, axis=0, keepdims=True), jnp.sum(dy * ffv, axis=0, keepdims=True))

    dy, d_ff, sq_sum, d_gate2 = _rowwise(loss_fn, [(x1, D, 0), (ff, D, 0), (tgt, D, 0)], [gate2],
                                         [(D, F32), (D, BF16)], [(1, D), (1, D)], "loss")
    loss = lax.psum(jnp.sum(sq_sum) * (0.5 / D), ("x", "y", "c"))

    owner_base = jnp.stack([me ^ r for r in CHIP_RELS]).astype(jnp.int32)
    gw, recv1, part, recv2 = {}, {}, {}, {}
    gw["ff2"] = _mm(act, d_ff, "tn", BF16, "dw_ff2")
    cm = _Comm()
    hh = _rs_d2d(cm, gw["ff2"], waxis["ff2"])
    d_u = _mm(d_ff, wf["ff2"], "nt", BF16, "d_act", comm=cm, extras=[u], epi=lambda r, uv: (r * (2.0 * jnp.maximum(uv, 0.0)),))
    part["ff2"] = _rs_add(gw["ff2"], cm.result(hh), waxis["ff2"], owner_base, "rs_add_ff2")
    rows_ff2 = part["ff2"].shape[1]
    cm = _Comm()
    hh = _rs_ici(cm, part["ff2"], rows=(0, rows_ff2 // 2))
    gw["ff1"] = _mm(h2, d_u, "tn", BF16, "dw_ff1", comm=cm)
    cm2 = _Comm()
    hh2 = _rs_ici(cm2, part["ff2"], rows=(rows_ff2 // 2, rows_ff2), recv=cm.result(hh))
    hh1 = _rs_d2d(cm2, gw["ff1"], waxis["ff1"])
    d_h2 = _mm(d_u, wf["ff1"], "nt", F32, "d_h2", comm=cm2)
    recv2["ff2"] = cm2.result(hh2)
    part["ff1"] = _rs_add(gw["ff1"], cm2.result(hh1), waxis["ff1"], owner_base, "rs_add_ff1")

    def norm2_bwd(dh2v, x1v, dyv, mov, g2n, sh, sc, g1):
        _, vjp = jax.vjp(_modnorm, x1v, g2n, sh, sc)
        dx, dg, dsh, dsc = vjp(dh2v)
        dx1 = dyv + dx
        return (dx1, dx1 * g1), (dg, dsh, dsc, jnp.sum(dx1 * mov, axis=0, keepdims=True))

    d_x1, d_mo, d_g2n, d_shift2, d_scale2, d_gate1 = _rowwise(
        norm2_bwd, [(d_h2, D, 0), (x1, D, 0), (dy, D, 0), (mo, D, 0)], [norm2_g, shift2, scale2, gate1],
        [(D, F32), (D, BF16)], [(1, D)] * 4, "norm2_bwd")
    gw["out"] = _mm(merged, d_mo, "tn", BF16, "dw_out")
    d_merged = _mm(d_mo, wf["out"], "nt", F32, "d_merged")

    def merge_bwd(dmv, bhv, bav, ghg, gat):
        _, vjp = jax.vjp(merge_fn, bhv, bav, ghg, gat)
        return vjp(dmv), ()

    d_bh, d_ba, d_ghg, d_gat = _rowwise(merge_bwd, [(d_merged, D, 0), (bh, D, 0), (ba, D, 0), (pg, D, GATE0), (pg, D, GATE0 + 1)], [],
                                        [(D, BF16)] * 4, [], "merge_bwd")
    gw["bhg"] = _mm(o_hg, d_bh, "tn", BF16, "dw_bhg")
    gw["bat"] = _mm(o_at, d_ba, "tn", BF16, "dw_bat")
    d_ohg = _mm(d_bh, wf["bhg"], "nt", F32, "d_ohg")
    d_oat = _mm(d_ba, wf["bat"], "nt", BF16, "d_oat")
    rows_ff1 = part["ff1"].shape[1]
    cut_ff1 = 3 * rows_ff1 // 8
    cm = _Comm()
    hf1 = _rs_ici(cm, part["ff1"], rows=(0, cut_ff1))
    d_hq, d_hf, d_hi, d_hg, d_lb, d_gout_h = _hgrn_bwd(p4, hg_lb_logits, hg_out_norm_g, s_all, d_ohg, H, comm=cm)
    cm2 = _Comm()
    hf1 = _rs_ici(cm2, part["ff1"], rows=(cut_ff1, rows_ff1), recv=cm.result(hf1))
    hh = {k: _rs_d2d(cm2, gw[k], waxis[k]) for k in small}
    dq_t, dkp, dvp, d_qg, d_kg, d_sk, d_bias = _attn_bwd(q_t, kp, vp, q_norm_g, k_norm_g, sinks3, bias,
                                                         _heads_first(d_oat, AH), KVH, comm=cm2)
    recv2["ff1"] = cm2.result(hf1)
    for k in small:
        part[k] = _rs_add(gw[k], cm2.result(hh[k]), waxis[k], owner_base, "rs_add_" + k)
    d_aq = _heads_last(dq_t)
    d_ak = _heads_last(dkp).astype(BF16)
    d_av = _heads_last(dvp).astype(BF16)
    d_proj = jnp.concatenate([d_hq, d_hf, d_hi, d_hg, d_aq, d_ak, d_av, d_ghg, d_gat], axis=1)
    cm = _Comm()
    hh = {k: _rs_ici(cm, part[k]) for k in small}
    gw_in = _mm(h, d_proj, "tn", BF16, "dw_in", comm=cm)
    for k in small:
        recv2[k] = cm.result(hh[k])

    wm = LANES * A
    cm = _Comm()
    hi_ = cm.inp(gw_in)
    h_main, h_mid = cm.out((4, D, wm), BF16), cm.out((4, D, LANES), BF16)
    for i, r in enumerate(CHIP_RELS):
        def main_view(ref, p, r=r):
            o = p["me"] ^ r ^ 1
            return ref.at[:, pl.ds(pl.multiple_of((PAIR * (o // 2) + (A + 1) * (1 - p["c"])) * LANES, LANES), wm)]

        def mid_view(ref, p, r=r):
            o = p["me"] ^ r
            return ref.at[:, pl.ds(pl.multiple_of((PAIR * (o // 2) + A) * LANES, LANES), LANES)]

        cm.copy(hi_, main_view, h_main, _slot_view(i), 1)
        cm.copy(hi_, mid_view, h_mid, _slot_view(i), 1)
    _call(lambda: None, [], name="rs_d2d_in", out_shape=(), comm=cm)
    chip = jnp.stack([(me ^ r) // 2 for r in CHIP_RELS]).astype(jnp.int32)
    part_main = _rs_add(gw_in, cm.result(h_main), 1, PAIR * chip + (A + 1) * cc, "rs_add_in_main", tw=LANES)
    part_mid = _rs_add(gw_in, cm.result(h_mid), 1, PAIR * chip + A, "rs_add_in_mid", tw=LANES)
    rs_in = _rs_split_start([part_main, part_mid], "rs_in_start")
    d_h = _mm(d_proj, w_in_full, "nt", F32, "d_h", tn=D, after=[rs_in["token"]])

    def norm1_bwd(dhv, xv, dx1v, g1n, sh, sc):
        _, vjp = jax.vjp(_modnorm, xv, g1n, sh, sc)
        dx, dg, dsh, dsc = vjp(dhv)
        return (dx1v + dx,), (dg, dsh, dsc)

    grad_x, d_g1n, d_shift1, d_scale1 = _rowwise(norm1_bwd, [(d_h, D, 0), (x2, D, 0), (d_x1, D, 0)],
                                                 [norm1_g, shift1, scale1], [(D, F32)], [(1, D)] * 3, "norm1_bwd")

    def sum4(p0, p1, p2, p3):
        return ((p0.astype(F32) + p1.astype(F32)) + p2.astype(F32)) + p3.astype(F32)

    def update_fn(w, m, v, p0, p1, p2, p3):
        g = sum4(p0, p1, p2, p3)
        delta, mn, vn = _adamw(w, g, m, v)
        return (g, delta, mn, vn), ()

    wmv = dict(zip(wnames, ((w_branch_hg, m_w_branch_hg, v_w_branch_hg), (w_branch_attn, m_w_branch_attn, v_w_branch_attn),
                            (w_out, m_w_out, v_w_out), (w_ff1, m_w_ff1, v_w_ff1), (w_ff2, m_w_ff2, v_w_ff2))))
    res = {}

    def update(k, p, rx):
        w, m, v = (t[0] for t in wmv[k])
        n = w.shape[1]
        ins = [(t, n, 0) for t in (w, m, v)] + [(p, n, 0, 0)] + [(rx, n, 0, i) for i in range(3)]
        res[k] = [t[None] for t in _rowwise(update_fn, ins, [], [(n, F32)] * 4, [], "update_" + k)]

    for k in wnames:
        update(k, part[k], recv2[k])
    (part_main, part_mid), (rx_main, rx_mid) = _rs_split_wait(rs_in, [grad_x] + [res[k][0] for k in wnames], "rs_in_wait")
    g_main, = _rowwise(lambda *p: ((sum4(*p),), ()), [(part_main, wm, 0, 0)] + [(rx_main, wm, 0, i) for i in range(3)], [],
                       [(wm, F32)], [], "sum_in_main")
    g_mid, = _rowwise(lambda *p: ((sum4(*p),), ()), [(part_mid, LANES, 0, 0)] + [(rx_mid, LANES, 0, i) for i in range(3)], [],
                      [(LANES, F32)], [], "sum_in_mid")
    g_in = jnp.where(cc == 0, jnp.concatenate([g_main, g_mid[:, :LANES // 2]], axis=1),
                     jnp.concatenate([g_mid[:, LANES // 2:], g_main], axis=1))

    def update_given(w, m, v, g):
        delta, mn, vn = _adamw(w, g, m, v)
        return (g, delta, mn, vn), ()

    res["in"] = [t[None] for t in _rowwise(update_given, [(t, BW, 0) for t in (w_in[0], m_w_in[0], v_w_in[0], g_in)], [],
                                           [(BW, F32)] * 4, [], "update_in")]

    d_sinks = d_sk.reshape(1, AH)
    (d_table_t,) = _whole(lambda db, bk: (_dot(db, _onehot(bk), NT, precision=HIGHEST),),
                          [d_bias.reshape(AH, AT_BLOCK * 2 * AT_BLOCK), bucket], [((AH, N_BUCKETS), F32)], "bias_bwd")
    smalls = [d_g1n, d_g2n, d_lb, d_gout_h, d_qg, d_kg, d_sinks, d_table_t.T.reshape(1, N_BUCKETS * AH)]
    widths = [s.shape[1] for s in smalls]
    lanes = [-(-w // LANES) * LANES for w in widths]
    smalls = [jnp.pad(s, ((0, 0), (0, p - w))) for s, w, p in zip(smalls, widths, lanes)]
    tail_row = jnp.concatenate([d_shift1, d_scale1, d_gate1, d_shift2, d_scale2, d_gate2] + smalls, axis=1)
    (tail_row,) = _behind([tail_row], [g_mid])
    tail_all = _gather_small(tail_row, me, "gather_tail")[:, 0, :]
    d_ada_all, packed = tail_all[:, :6 * D], tail_all[:, 6 * D:]
    d_ada_cols = lax.dynamic_slice(d_ada_all, (0, me * ADA_N), (N_DEV, ADA_N))

    def ada_update(cv, dav, w, m, v):
        g = _bdot(_silu(cv), dav, TN)
        delta, mn, vn = _adamw(w, g, m, v)
        return (g, delta, mn, vn), ()

    res["ada"] = [t[None] for t in _ada_update_call(ada_update, c_all, d_ada_cols, w_ada[0], m_w_ada[0], v_w_ada[0], _tile(D, 256, 16))]

    offs = [sum(lanes[:i]) for i in range(len(lanes))]

    def small_update(pk, dada, lg, *wmv_flat):
        tot = pk[0:1]
        for d in range(1, N_DEV):
            tot = tot + pk[d:d + 1]
        gb = dada[0:1]
        for d in range(1, N_DEV):
            gb = gb + dada[d:d + 1]
        gs = [tot[:, offs[i]:offs[i] + widths[i]] for i in range(len(widths))]
        _, lb_vjp = jax.vjp(_softmax0, lg)
        (g_lg,) = lb_vjp(gs[2])
        grads = [gb, gs[0], gs[1], g_lg, gs[3], gs[4], gs[5], gs[6], gs[7]]
        outs = []
        for i, g in enumerate(grads):
            w, m, v = wmv_flat[3 * i:3 * i + 3]
            delta, mn, vn = _adamw(w, g, m, v)
            outs += [g, delta, mn, vn]
        return tuple(outs)

    tbl = lambda t: t.reshape(1, N_BUCKETS * AH)
    small_wmv = [(b_ada, m_b_ada, v_b_ada), (norm1_g, m_norm1_g, v_norm1_g), (norm2_g, m_norm2_g, v_norm2_g),
                 (hg_lb_logits, m_hg_lb_logits, v_hg_lb_logits), (hg_out_norm_g, m_hg_out_norm_g, v_hg_out_norm_g),
                 (q_norm_g, m_q_norm_g, v_q_norm_g), (k_norm_g, m_k_norm_g, v_k_norm_g),
                 (attn_sinks, m_attn_sinks, v_attn_sinks),
                 (tbl(rel_bias_table), tbl(m_rel_bias_table), tbl(v_rel_bias_table))]
    flat = [t for trip in small_wmv for t in trip]
    out_shapes = [(trip[0].shape, F32) for trip in small_wmv for _ in range(4)]
    sres = _whole(small_update, [packed, d_ada_all, hg_lb_logits] + flat, out_shapes, "small_update")
    names_small = ("b_ada", "norm1_g", "norm2_g", "lb", "gout", "qg", "kg", "sinks", "table")
    for i, k in enumerate(names_small):
        r = sres[4 * i:4 * i + 4]
        if k == "table":
            r = [t.reshape(N_BUCKETS, AH) for t in r]
        res[k] = r

    order = ("ada", "b_ada", "norm1_g", "norm2_g", "in", "lb", "gout", "qg", "kg", "sinks", "table", "bhg", "bat", "out", "ff1", "ff2")
    outs = [loss, grad_x[None]]
    for j in range(4):
        outs += [res[k][j] for k in order]
    return tuple(outs)
```

```python
import functools
import math

import jax
import jax.numpy as jnp
from jax import lax
from jax.experimental import pallas as pl
from jax.experimental.pallas import tpu as pltpu

F32 = jnp.float32
BF16 = jnp.bfloat16
EPS = 1e-6
NEG_INF = -1e30
HG_DK = 128
HG_CHUNK = 64
AT_BLOCK = 128
N_BUCKETS = 32
MAX_EXACT = 16
MAX_DISTANCE = 128
N_DEV = 8
LANES = 128
VMEM_LIMIT = 56 * 1024 * 1024
ADAM_LR, ADAM_B1, ADAM_B2, ADAM_EPS, ADAM_WD, ADAM_STEP = 0.001, 0.9, 0.999, 1e-08, 0.01, 10
HIGHEST = lax.Precision.HIGHEST
MESH = pl.DeviceIdType.MESH
ANY = pl.BlockSpec(memory_space=pl.ANY)
CHIP_RELS = (0, 4, 2, 6)

NN = (((1,), (0,)), ((), ()))
NT = (((1,), (1,)), ((), ()))
TN = (((0,), (0,)), ((), ()))


def _tile(n, pref, unit):
    if n <= pref:
        return n
    t = (pref // unit) * unit
    while t >= unit:
        if n % t == 0:
            return t
        t -= unit
    return n


def _dot(a, b, dn, precision=None):
    return lax.dot_general(a, b, dn, preferred_element_type=F32, precision=precision)


def _bdot(a, b, dn):
    return _dot(a.astype(BF16), b.astype(BF16), dn)


def _position():
    x, y, c = lax.axis_index("x"), lax.axis_index("y"), lax.axis_index("c")
    return dict(x=x, y=y, c=c, me=4 * x + 2 * y + c)


def _peer_position(p, rel):
    x = 1 - p["x"] if rel & 4 else p["x"]
    y = 1 - p["y"] if rel & 2 else p["y"]
    c = 1 - p["c"] if rel & 1 else p["c"]
    return dict(x=x, y=y, c=c, me=4 * x + 2 * y + c)


class _Comm:
    def __init__(self):
        self.ins, self.outs, self.alias, self.plans, self.res = [], [], {}, [], None

    def inp(self, arr):
        self.ins.append(arr)
        return ("i", len(self.ins) - 1)

    def out(self, shape, dtype, alias=None):
        self.outs.append(jax.ShapeDtypeStruct(tuple(shape), dtype))
        if alias is not None:
            self.alias[alias[1]] = len(self.outs) - 1
        return ("o", len(self.outs) - 1)

    def copy(self, src, src_view, dst, dst_view, rel):
        self.plans.append((src, src_view, dst, dst_view, rel))

    def result(self, handle):
        return self.res[handle[1]]

    def build(self, in_refs, out_refs, send_sems, recv_sems):
        pos = _position()
        ref = lambda h: in_refs[h[1]] if h[0] == "i" else out_refs[h[1]]
        ops = []
        for k, (src, sv, dst, dv, rel) in enumerate(self.plans):
            s = sv(ref(src), pos)
            if rel == 0:
                cp = pltpu.make_async_copy(s, dv(ref(dst), pos), send_sems.at[k])
                ops.append((cp.start, cp.wait))
                continue
            peer = _peer_position(pos, rel)
            mk = lambda d: pltpu.make_async_remote_copy(
                src_ref=s, dst_ref=d, send_sem=send_sems.at[k], recv_sem=recv_sems.at[k],
                device_id=(peer["x"], peer["y"], peer["c"]), device_id_type=MESH)
            out_cp, in_cp = mk(dv(ref(dst), pos)), mk(dv(ref(dst), peer))

            def wait(out_cp=out_cp, in_cp=in_cp):
                out_cp.wait_send()
                in_cp.wait_recv()

            ops.append((out_cp.start, wait))
        return ops


def _call(body, args, *, name, out_shape, in_specs=None, out_specs=None, grid=None, scratch_shapes=(), comm=None,
          prefetch=None, aliases=None, after=()):
    single = not isinstance(out_shape, (tuple, list))
    out_shape = (out_shape,) if single else tuple(out_shape)
    n_in, n_out, n_scr = len(args), len(out_shape), len(scratch_shapes)
    vm = pl.BlockSpec(memory_space=pltpu.VMEM)
    in_specs = [vm] * n_in if in_specs is None else list(in_specs)
    out_specs = [vm] * n_out if out_specs is None else (list(out_specs) if isinstance(out_specs, (tuple, list)) else [out_specs])
    n_pf = 0 if prefetch is None else len(prefetch)
    kw = {} if aliases is None else {"input_output_aliases": dict(aliases)}
    if comm is None and after:
        n_dep = len(after)

        def fn(*refs):
            body(*refs[:n_pf + n_in], *refs[n_pf + n_in + n_dep:])

        all_args, all_scratch = list(args) + list(after), list(scratch_shapes)
        in_specs = in_specs + [ANY] * n_dep
    elif comm is None:
        fn = body
        all_args, all_scratch = list(args), list(scratch_shapes)
    else:
        n_ci, n_co, n_x = len(comm.ins), len(comm.outs), len(comm.plans)

        def fn(*refs):
            pf, refs = refs[:n_pf], refs[n_pf:]
            o_in, c_in = refs[:n_in], refs[n_in:n_in + n_ci]
            o_out = refs[n_in + n_ci:n_in + n_ci + n_out]
            c_out = refs[n_in + n_ci + n_out:n_in + n_ci + n_out + n_co]
            scr = refs[n_in + n_ci + n_out + n_co:]
            ops = comm.build(c_in, c_out, scr[n_scr], scr[n_scr + 1])
            if grid:
                first = functools.reduce(jnp.logical_and, [pl.program_id(i) == 0 for i in range(len(grid))])
                last = functools.reduce(jnp.logical_and, [pl.program_id(i) == g - 1 for i, g in enumerate(grid)])

                @pl.when(first)
                def _():
                    for start, _w in ops:
                        start()
            else:
                for start, _w in ops:
                    start()
            body(*pf, *o_in, *o_out, *scr[:n_scr])
            if grid:
                @pl.when(last)
                def _():
                    for _s, wait in ops:
                        wait()
            else:
                for _s, wait in ops:
                    wait()

        all_args = list(args) + list(comm.ins)
        in_specs = in_specs + [ANY] * n_ci
        out_shape = out_shape + tuple(comm.outs)
        out_specs = out_specs + [ANY] * n_co
        all_scratch = list(scratch_shapes) + [pltpu.SemaphoreType.DMA((n_x,)), pltpu.SemaphoreType.DMA((n_x,))]
        kw["input_output_aliases"] = {n_pf + n_in + i: n_out + o for i, o in comm.alias.items()}
    sem = None if grid is None else ("arbitrary",) * len(grid)
    params = pltpu.CompilerParams(dimension_semantics=sem, vmem_limit_bytes=VMEM_LIMIT)
    if prefetch is None:
        spec = dict(in_specs=in_specs, out_specs=tuple(out_specs), scratch_shapes=all_scratch)
        if grid is not None:
            spec["grid"] = grid
    else:
        spec = dict(grid_spec=pltpu.PrefetchScalarGridSpec(
            num_scalar_prefetch=n_pf, grid=grid, in_specs=in_specs, out_specs=tuple(out_specs), scratch_shapes=all_scratch))
        all_args = list(prefetch) + all_args
    res = pl.pallas_call(fn, name=name, out_shape=out_shape, compiler_params=params, **spec, **kw)(*all_args)
    res = list(res)
    if comm is not None:
        comm.res = res[n_out:]
        res = res[:n_out]
    return res[0] if single else res


def _whole_view(ref, pos):
    return ref


def _block_view(axis, n, index, rows=None):
    def view(ref, pos):
        off = pl.multiple_of(index(pos) * n, n)
        if rows is None:
            return ref.at[:, pl.ds(off, n)] if axis == 1 else ref.at[pl.ds(off, n), :]
        lo, cnt = rows[0], rows[1] - rows[0]
        if axis == 1:
            return ref.at[pl.ds(lo, cnt), pl.ds(off, n)]
        return ref.at[pl.ds(pl.multiple_of(off + lo, 16), cnt), :]
    return view


def _rows_view(rows):
    def view(ref, pos):
        return ref if rows is None else ref.at[pl.ds(rows[0], rows[1] - rows[0]), :]
    return view


def _slot_view(i, rows=None):
    def view(ref, pos):
        return ref.at[i] if rows is None else ref.at[i, pl.ds(rows[0], rows[1] - rows[0]), :]
    return view


def _exchange(items, name):
    cm = _Comm()
    for a, rel in items:
        cm.copy(cm.inp(a), _whole_view, cm.out(a.shape, a.dtype), _whole_view, rel)
    _call(lambda: None, [], name=name, out_shape=(), comm=cm)
    return cm.res


def _gather_small(v, me, name):
    cm = _Comm()
    hi, ho = cm.inp(v), cm.out((N_DEV,) + v.shape, v.dtype)
    for rel in range(N_DEV):
        cm.copy(hi, _whole_view, ho, lambda ref, p: ref.at[p["me"]], rel)
    _call(lambda: None, [], name=name, out_shape=(), comm=cm)
    return cm.result(ho)


def _ag_ici(cm, blk, axis, rows=None, into=None):
    n = blk.shape[axis]
    shape = list(blk.shape)
    shape[axis] = n * N_DEV
    hi = cm.inp(blk)
    ho = cm.out(shape, blk.dtype) if into is None else cm.out(shape, blk.dtype, alias=cm.inp(into))
    own = _block_view(axis, n, lambda p: p["me"], rows)
    for rel in CHIP_RELS:
        cm.copy(hi, _rows_view(rows), ho, own, rel)
    return ho


def _ag_d2d(cm, full, axis):
    n = full.shape[axis] // N_DEV
    hi = cm.inp(full)
    ho = cm.out(full.shape, full.dtype, alias=hi)
    for r in CHIP_RELS:
        v = _block_view(axis, n, functools.partial(lambda p, r: p["me"] ^ r, r=r))
        cm.copy(hi, v, ho, v, 1)
    return ho


def _rs_d2d(cm, gw, axis):
    n = gw.shape[axis] // N_DEV
    shape = list(gw.shape)
    shape[axis] = n
    hi, ho = cm.inp(gw), cm.out([4] + shape, gw.dtype)
    for i, r in enumerate(CHIP_RELS):
        cm.copy(hi, _block_view(axis, n, functools.partial(lambda p, r: p["me"] ^ r ^ 1, r=r)), ho, _slot_view(i), 1)
    return ho


def _rs_ici(cm, part, rows=None, recv=None):
    if recv is None:
        ho = cm.out((3,) + part.shape[1:], part.dtype)
    else:
        ho = cm.out(recv.shape, recv.dtype, alias=cm.inp(recv))
    hi = cm.inp(part)
    for i in (1, 2, 3):
        cm.copy(hi, _slot_view(i, rows), ho, _slot_view(i - 1, rows), CHIP_RELS[i])
    return ho


def _rs_add(gw, recv, axis, base, name, tw=None):
    _, R, n = recv.shape
    fan = 1
    if axis == 1:
        tw = n if tw is None else tw
        fan = max(f for f in (4, 3, 2, 1) if (n // tw) % f == 0)
        gw_specs = [pl.BlockSpec((R, tw), functools.partial(lambda i, t, b, k: (0, b[i] + fan * t + k), k=k)) for k in range(fan)]
        rv_spec = pl.BlockSpec((None, R, tw * fan), lambda i, t, b: (i, 0, t))
        grid = (4, n // (tw * fan))
    else:
        tw = _tile(n, 1024, LANES)
        gw_specs = [pl.BlockSpec((R, tw), lambda i, t, b: (b[i], t))]
        rv_spec = pl.BlockSpec((None, R, tw), lambda i, t, b: (i, 0, t))
        grid = (4, n // tw)

    def body(b_ref, *refs):
        g_refs, r_ref, o_ref = refs[:fan], refs[fan], refs[fan + 1]
        g = g_refs[0][...] if fan == 1 else jnp.concatenate([g[...] for g in g_refs], axis=1)
        o_ref[...] = (g.astype(F32) + r_ref[...].astype(F32)).astype(o_ref.dtype)

    return _call(body, [gw] * fan + [recv], name=name, out_shape=jax.ShapeDtypeStruct(recv.shape, recv.dtype), grid=grid,
                 in_specs=gw_specs + [rv_spec], out_specs=rv_spec, prefetch=[base])


HBM_SPEC = pl.BlockSpec(memory_space=pltpu.HBM)
SEM_SPEC = pl.BlockSpec(memory_space=pltpu.SEMAPHORE)
SPLIT_PARAMS = pltpu.CompilerParams(has_side_effects=pltpu.SideEffectType.DATAFLOW_SIDE_EFFECTING)


def _split_copies(refs, plans, send_sems, recv_sems):
    pos = _position()
    out = []
    for k, (si, sv, li, lv, rel) in enumerate(plans):
        peer = _peer_position(pos, rel)
        mk = lambda d: pltpu.make_async_remote_copy(
            src_ref=sv(refs[si], pos), dst_ref=d, send_sem=send_sems.at[k], recv_sem=recv_sems.at[k],
            device_id=(peer["x"], peer["y"], peer["c"]), device_id_type=MESH)
        out.append((mk(lv(refs[li], pos)), mk(lv(refs[li], peer))))
    return out


def _split_start(arrays, plans, name):
    n = len(arrays)

    def body(*refs):
        send_sems, recv_sems = refs[n], refs[n + 1]
        for out_cp, _ in _split_copies(refs[:n], plans, send_sems, recv_sems):
            out_cp.start()
        refs[-1][...] = jnp.zeros_like(refs[-1])

    sems = pltpu.SemaphoreType.DMA((len(plans),))
    res = pl.pallas_call(
        body, name=name,
        out_shape=(sems, sems) + tuple(pltpu.HBM(a.shape, a.dtype) for a in arrays) + (jax.ShapeDtypeStruct((8, LANES), F32),),
        in_specs=[HBM_SPEC] * n, out_specs=(SEM_SPEC, SEM_SPEC) + (HBM_SPEC,) * n + (pl.BlockSpec(memory_space=pltpu.VMEM),),
        input_output_aliases={i: 2 + i for i in range(n)}, compiler_params=SPLIT_PARAMS,
    )(*[pltpu.with_memory_space_constraint(a, pltpu.HBM) for a in arrays])
    return res[0], res[1], list(res[2:2 + n]), res[-1]


def _split_wait(send_sems, recv_sems, arrays, plans, after, name):
    n, na = len(arrays), len(after)

    def body(*refs):
        for out_cp, in_cp in _split_copies(refs[:n], plans, refs[n], refs[n + 1]):
            out_cp.wait_send()
            in_cp.wait_recv()

    res = pl.pallas_call(
        body, name=name, out_shape=tuple(pltpu.HBM(a.shape, a.dtype) for a in arrays),
        in_specs=[HBM_SPEC] * n + [SEM_SPEC, SEM_SPEC] + [ANY] * na, out_specs=(HBM_SPEC,) * n,
        input_output_aliases={i: i for i in range(n)}, compiler_params=SPLIT_PARAMS,
    )(*arrays, send_sems, recv_sems, *after)
    return list(res)


def _rs_split_start(parts, name):
    nw = len(parts)
    lands = [lax.empty((3,) + p.shape[1:], p.dtype) for p in parts]
    plans = [(s, _slot_view(i), nw + s, _slot_view(i - 1), CHIP_RELS[i]) for s in range(nw) for i in (1, 2, 3)]
    send_sems, recv_sems, arrays, token = _split_start(list(parts) + lands, plans, name)
    return dict(sems=(send_sems, recv_sems), arrays=arrays, plans=plans, token=token, nw=nw)


def _rs_split_wait(h, after, name):
    arrays = _split_wait(h["sems"][0], h["sems"][1], h["arrays"], h["plans"], after, name)
    return arrays[:h["nw"]], arrays[h["nw"]:]


def _behind(xs, tokens):
    out = lax.optimization_barrier((tuple(xs), tuple(tokens)))
    return list(out[0])


def _ag_w_in(src, a, D, INW):
    wm = LANES * a

    hd = D // 2
    ALL, TOP, BOT = (0, D), (0, hd), (hd, D)

    def main_place(ref, p, rows=ALL):
        off = pl.multiple_of(((2 * a + 1) * (p["me"] // 2) + (a + 1) * p["c"]) * LANES, LANES)
        return ref.at[pl.ds(rows[0], rows[1] - rows[0]), pl.ds(off, wm)]

    def main_src(ref, p):
        return ref.at[:, pl.ds(pl.multiple_of(p["c"] * LANES, LANES), wm)]

    def mid_src(ref, p):
        return ref.at[:, pl.ds(pl.multiple_of((1 - p["c"]) * wm, LANES), LANES)]

    def mid_place(ref, p, rows=ALL):
        return ref.at[p["me"], pl.ds(rows[0], rows[1] - rows[0]), :]

    def body(src_ref, full_ref, mid_ref, send_sems, recv_sems):
        pos = _position()
        sib, xn, yn = (_peer_position(pos, r) for r in (1, 4, 2))
        dg = _peer_position(pos, 6)
        started = []

        def remote(k, s, d, to):
            return pltpu.make_async_remote_copy(src_ref=s, dst_ref=d, send_sem=send_sems.at[k], recv_sem=recv_sems.at[k],
                                                device_id=(to["x"], to["y"], to["c"]), device_id_type=MESH)

        def send(k, owner, rows, to, from_src=False):
            for j, (src_v, place) in enumerate(((main_src, main_place), (mid_src, mid_place))):
                s = src_v(src_ref, pos) if from_src else place(full_ref if j == 0 else mid_ref, owner, rows)
                cp = remote(k + j, s, place(full_ref if j == 0 else mid_ref, owner, rows), to)
                cp.start()
                started.append(cp)

        def landed(k, owner, rows, frm):
            for j, place in enumerate((main_place, mid_place)):
                ref = full_ref if j == 0 else mid_ref
                remote(k + j, place(ref, owner, rows), place(ref, owner, rows), frm).wait_recv()

        local = [pltpu.make_async_copy(main_src(src_ref, pos), main_place(full_ref, pos), send_sems.at[18]),
                 pltpu.make_async_copy(mid_src(src_ref, pos), mid_place(mid_ref, pos), send_sems.at[19])]
        for cp in local:
            cp.start()
        send(0, pos, ALL, sib, from_src=True)
        send(2, pos, ALL, xn, from_src=True)
        send(4, pos, ALL, yn, from_src=True)
        landed(2, xn, ALL, xn)
        send(10, xn, ALL, sib)
        send(6, xn, TOP, yn)
        landed(4, yn, ALL, yn)
        send(12, yn, ALL, sib)
        send(8, yn, BOT, xn)
        landed(6, dg, TOP, yn)
        send(14, dg, TOP, sib)
        landed(8, dg, BOT, xn)
        send(16, dg, BOT, sib)
        sib_of = lambda p: _peer_position(p, 1)
        landed(0, sib, ALL, sib)
        landed(10, sib_of(xn), ALL, sib)
        landed(12, sib_of(yn), ALL, sib)
        landed(14, sib_of(dg), TOP, sib)
        landed(16, sib_of(dg), BOT, sib)
        for cp in started:
            cp.wait_send()
        for cp in local:
            cp.wait()

    return _call(body, [src], name="ag_w_in", in_specs=[ANY], out_specs=[ANY, ANY],
                 out_shape=(jax.ShapeDtypeStruct((D, INW), BF16), jax.ShapeDtypeStruct((N_DEV, D, LANES), BF16)),
                 scratch_shapes=[pltpu.SemaphoreType.DMA((20,)), pltpu.SemaphoreType.DMA((20,))])


def _patch_mid(full, mid, a):
    D = full.shape[0]

    def body(full_ref, e_ref, o_ref, out_ref):
        out_ref[...] = e_ref[...] + o_ref[...]

    return _call(body, [full, mid, mid], name="patch_mid", grid=(N_DEV // 2,),
                 out_shape=jax.ShapeDtypeStruct(full.shape, full.dtype),
                 in_specs=[ANY, pl.BlockSpec((None, D, LANES), lambda j: (2 * j, 0, 0)),
                           pl.BlockSpec((None, D, LANES), lambda j: (2 * j + 1, 0, 0))],
                 out_specs=pl.BlockSpec((D, LANES), lambda j: (0, (2 * a + 1) * j + a)), aliases={0: 0})


MM_RESIDENT = 2048


def _mm(a, b, mode, out_dtype, name, b_off=0, n=None, comm=None, extras=(), epi=None, tn=None, after=(), b_order=None):
    if mode == "nn":
        (M, K), (K2, N) = a.shape, b.shape
    elif mode == "nt":
        (M, K), (N, K2) = a.shape, b.shape
    else:
        (K, M), (K2, N) = a.shape, b.shape
    assert K == K2, (a.shape, b.shape, mode)
    if n is not None:
        N = n
    single = not isinstance(out_dtype, (tuple, list))
    out_dtypes = (out_dtype,) if single else tuple(out_dtype)
    if epi is None:
        epi = lambda r: (r,)
    tk = K if K <= MM_RESIDENT else (MM_RESIDENT if K % MM_RESIDENT == 0 else _tile(K, 512, LANES))
    nk = K // tk
    if M > MM_RESIDENT and mode == "tn" and N <= MM_RESIDENT and not b_off:
        tm, tn = _tile(M, 512, LANES), N
    elif nk > 1:
        tm, tn = _tile(M, 1024, LANES), _tile(N, tn or 1024, LANES)
    else:
        tm = _tile(M, MM_RESIDENT, LANES)
        tn = _tile(math.gcd(N, b_off) if b_off else N, tn or 512, LANES)
    jb = b_off // tn
    dn = {"nn": NN, "nt": NT, "tn": TN}[mode]
    ne, no = len(extras), len(out_dtypes)

    def body(a_ref, b_ref, *rest):
        e_refs, o_refs = rest[:ne], rest[ne:ne + no]

        def finish(r):
            for o_ref, v in zip(o_refs, epi(r, *[e[...] for e in e_refs])):
                o_ref[...] = v.astype(o_ref.dtype)

        if nk == 1:
            finish(_bdot(a_ref[...], b_ref[...], dn))
            return
        acc_ref = rest[ne + no]
        k = pl.program_id(2)

        @pl.when(k == 0)
        def _():
            acc_ref[...] = _bdot(a_ref[...], b_ref[...], dn)

        @pl.when(jnp.logical_and(k > 0, k < nk - 1))
        def _():
            acc_ref[...] += _bdot(a_ref[...], b_ref[...], dn)

        @pl.when(k == nk - 1)
        def _():
            finish(acc_ref[...] + _bdot(a_ref[...], b_ref[...], dn))

    a_spec = pl.BlockSpec((tk, tm), lambda i, j, k: (k, i)) if mode == "tn" else pl.BlockSpec((tm, tk), lambda i, j, k: (i, k))
    col = (lambda j: j + jb) if b_order is None else functools.partial(b_order, tn)
    b_spec = pl.BlockSpec((tn, tk), lambda i, j, k: (j, k)) if mode == "nt" else pl.BlockSpec((tk, tn), lambda i, j, k: (k, col(j)))
    o_spec = pl.BlockSpec((tm, tn), lambda i, j, k: (i, j))
    res = _call(body, [a, b] + list(extras), name=name, grid=(M // tm, N // tn, nk),
                out_shape=tuple(jax.ShapeDtypeStruct((M, N), dt) for dt in out_dtypes),
                in_specs=[a_spec, b_spec] + [o_spec] * ne, out_specs=[o_spec] * no,
                scratch_shapes=[pltpu.VMEM((tm, tn), F32)] if nk > 1 else [], comm=comm, after=after)
    return res[0] if single else res


def _rowwise(fn, row_ins, bcast_ins, row_outs, acc_outs, name, rt=256, comm=None):
    L = row_ins[0][0].shape[-2]
    rt = _tile(L, rt, 16)
    nr, nb, no = len(row_ins), len(bcast_ins), len(row_outs)

    def body(*refs):
        i = pl.program_id(0)
        vals = [r[...] for r in refs[:nr + nb]]
        outs, accs = fn(*vals)
        for r, v in zip(refs[nr + nb:nr + nb + no], outs):
            r[...] = v.astype(r.dtype)
        acc_refs = refs[nr + nb + no:]

        @pl.when(i == 0)
        def _():
            for r in acc_refs:
                r[...] = jnp.zeros_like(r)

        for r, v in zip(acc_refs, accs):
            r[...] += v

    in_specs = []
    for spec in row_ins:
        w, cb = spec[1], spec[2]
        if len(spec) == 4:
            in_specs.append(pl.BlockSpec((None, rt, w), functools.partial(lambda i, cb, ld: (ld, i, cb), cb=cb, ld=spec[3])))
        else:
            in_specs.append(pl.BlockSpec((rt, w), functools.partial(lambda i, cb: (i, cb), cb=cb)))
    in_specs += [pl.BlockSpec(b.shape, lambda i: (0, 0)) for b in bcast_ins]
    out_specs = [pl.BlockSpec((rt, w), lambda i: (i, 0)) for w, _ in row_outs]
    out_specs += [pl.BlockSpec(s, lambda i: (0, 0)) for s in acc_outs]
    out_shape = [jax.ShapeDtypeStruct((L, w), dt) for w, dt in row_outs] + [jax.ShapeDtypeStruct(s, F32) for s in acc_outs]
    return _call(body, [s[0] for s in row_ins] + list(bcast_ins), name=name, grid=(L // rt,), out_shape=tuple(out_shape),
                 in_specs=in_specs, out_specs=out_specs, comm=comm)


def _whole(fn, ins, out_shapes, name):
    def body(*refs):
        outs = fn(*[r[...] for r in refs[:len(ins)]])
        for r, v in zip(refs[len(ins):], outs):
            r[...] = v.astype(r.dtype)

    return _call(body, list(ins), name=name, out_shape=tuple(jax.ShapeDtypeStruct(s, dt) for s, dt in out_shapes))


def _silu(x):
    return x * jax.nn.sigmoid(x)


def _rms(x, g):
    return (x * lax.rsqrt(jnp.mean(x * x, axis=-1, keepdims=True) + EPS)) * g


def _modnorm(x, g, shift, scale):
    return _rms(x, g) * (1.0 + scale) + shift


def _adamw(w, g, m, v):
    m = ADAM_B1 * m + (1.0 - ADAM_B1) * g
    v = ADAM_B2 * v + (1.0 - ADAM_B2) * jnp.square(g)
    m_hat = m / (1.0 - ADAM_B1 ** ADAM_STEP)
    v_hat = v / (1.0 - ADAM_B2 ** ADAM_STEP)
    delta = -ADAM_LR * (m_hat / (jnp.sqrt(v_hat) + ADAM_EPS) + ADAM_WD * w)
    return delta, m, v


def _lower_bound(lg):
    e = jnp.exp(lg - jnp.max(lg, axis=0, keepdims=True))
    return e[0:1] / jnp.sum(e, axis=0, keepdims=True)


def _hg_stages(hq_l, hf_l, hi_l, lb):
    C = hq_l[0].shape[0]
    row = lax.broadcasted_iota(jnp.int32, (C, C), 0)
    col = lax.broadcasted_iota(jnp.int32, (C, C), 1)
    tri = row >= col
    trif = tri.astype(F32)
    f_l = [lb + (1.0 - lb) * jax.nn.sigmoid(hf) for hf in hf_l]
    b_l = [_dot(trif, jnp.log(f), NN, precision=HIGHEST) for f in f_l]
    q_l = [_silu(hq) for hq in hq_l]
    m_l = [b[C // 2 - 1:C // 2] for b in b_l]
    bl_l = [b[C - 1:C] for b in b_l]
    sc_l = [jnp.where(tri, _bdot(q * jnp.exp(b - m), (1.0 - f) * jnp.exp(m - b), NT), 0.0)
            for q, f, b, m in zip(q_l, f_l, b_l, m_l)]
    o1_l = [_bdot(sc, hi, NN) for sc, hi in zip(sc_l, hi_l)]
    u_l = [_bdot(hi, (1.0 - f) * jnp.exp(bl - b), TN) for hi, f, b, bl in zip(hi_l, f_l, b_l, bl_l)]
    qb_l = [q * jnp.exp(b) for q, b in zip(q_l, b_l)]
    dec_l = [jnp.exp(bl) for bl in bl_l]
    return list(zip(o1_l, u_l, qb_l, dec_l))


def _hg_out(o, hgate, gout):
    return _rms(o, gout) * _silu(hgate)


HG_STAGE = 8
HG_GROUP = 32


def _hgrn_fwd(p4, lb_logits, gout, H, comm=None):
    L = p4.shape[0]
    C = HG_CHUNK
    GR = _tile(L // C, HG_GROUP, 1)
    T = GR * C
    N = L // T

    def body(hq_ref, hf_ref, hi_ref, hg_ref, lg_ref, gout_ref, o_ref, s_ref, st_ref):
        @pl.when(pl.program_id(1) == 0)
        def _():
            st_ref[...] = jnp.zeros_like(st_ref)

        lb = _lower_bound(lg_ref[...])
        st = st_ref[...]
        for c0 in range(0, GR, HG_STAGE):
            rows_l = [pl.ds(ci * C, C) for ci in range(c0, min(c0 + HG_STAGE, GR))]
            parts = _hg_stages([hq_ref[r, :] for r in rows_l], [hf_ref[r, :] for r in rows_l],
                               [hi_ref[r, :] for r in rows_l], lb)
            for ci, rows, (o1, u, qb, dec) in zip(range(c0, GR), rows_l, parts):
                s_ref[0, ci] = st
                o = o1 + _bdot(qb, st, NT)
                st = st * dec + u
                o_ref[rows, :] = _hg_out(o, hg_ref[rows, :], gout_ref[...]).astype(o_ref.dtype)
        st_ref[...] = st

    blk = lambda s: pl.BlockSpec((T, HG_DK), functools.partial(lambda h, n, s: (n, s * H + h), s=s))
    return _call(
        body, [p4, p4, p4, p4, lb_logits, gout], name="hgrn_fwd", grid=(H, N),
        out_shape=(jax.ShapeDtypeStruct((L, H * HG_DK), BF16), jax.ShapeDtypeStruct((H, N * GR, HG_DK, HG_DK), F32)),
        in_specs=[blk(0), blk(1), blk(2), blk(3), pl.BlockSpec((2, HG_DK), lambda h, n: (0, h)),
                  pl.BlockSpec((1, HG_DK), lambda h, n: (0, 0))],
        out_specs=(pl.BlockSpec((T, HG_DK), lambda h, n: (n, h)),
                   pl.BlockSpec((1, GR, HG_DK, HG_DK), lambda h, n: (h, n, 0, 0))),
        scratch_shapes=[pltpu.VMEM((HG_DK, HG_DK), F32)], comm=comm)


def _hgrn_bwd(p4, lb_logits, gout, s_all, d_out, H, comm=None):
    L = p4.shape[0]
    C = HG_CHUNK
    GR = _tile(L // C, HG_GROUP, 1)
    T = GR * C
    N = L // T

    def body(hq_ref, hf_ref, hi_ref, hg_ref, lg_ref, gout_ref, s_ref, do_ref,
             dq_ref, df_ref, di_ref, dg_ref, dlb_ref, dgo_ref, dst_ref):
        @pl.when(pl.program_id(1) == 0)
        def _():
            dst_ref[...] = jnp.zeros_like(dst_ref)
            dlb_ref[...] = jnp.zeros_like(dlb_ref)

        @pl.when(jnp.logical_and(pl.program_id(0) == 0, pl.program_id(1) == 0))
        def _():
            dgo_ref[...] = jnp.zeros_like(dgo_ref)

        lb = _lower_bound(lg_ref[...])
        dst = dst_ref[...]
        d_lb = jnp.zeros((1, HG_DK), F32)
        d_go = jnp.zeros((1, HG_DK), F32)
        for c0 in reversed(range(0, GR, HG_STAGE)):
            dst, d_lb_c, d_go_c = chunks_bwd(list(range(c0, min(c0 + HG_STAGE, GR))), lb, dst, hq_ref, hf_ref, hi_ref,
                                             hg_ref, gout_ref, s_ref, do_ref, dq_ref, df_ref, di_ref, dg_ref)
            d_lb += d_lb_c
            d_go += d_go_c
        dst_ref[...] = dst
        dlb_ref[...] += d_lb
        dgo_ref[...] += d_go

    def chunks_bwd(idx, lb, dst, hq_ref, hf_ref, hi_ref, hg_ref, gout_ref, s_ref, do_ref, dq_ref, df_ref, di_ref, dg_ref):
        n = len(idx)
        rows_l = [pl.ds(ci * C, C) for ci in idx]
        hq_l, hf_l, hi_l = ([r[rows, :] for rows in rows_l] for r in (hq_ref, hf_ref, hi_ref))
        st_l = [s_ref[0, ci] for ci in idx]
        row = lax.broadcasted_iota(jnp.int32, (C, C), 0)
        col = lax.broadcasted_iota(jnp.int32, (C, C), 1)
        tri = row >= col
        trif = tri.astype(F32)
        every = lambda fn, *ls: [fn(*a) for a in zip(*ls)]
        sg_l = every(jax.nn.sigmoid, hf_l)
        f_l = every(lambda sg: lb + (1.0 - lb) * sg, sg_l)
        b_l = every(lambda f: _dot(trif, jnp.log(f), NN, precision=HIGHEST), f_l)
        q_l = every(_silu, hq_l)
        m_l = every(lambda b: b[C // 2 - 1:C // 2], b_l)
        bl_l = every(lambda b: b[C - 1:C], b_l)
        e_qm_l = every(lambda b, m: jnp.exp(b - m), b_l, m_l)
        e_km_l = every(lambda b, m: jnp.exp(m - b), b_l, m_l)
        e_kl_l = every(lambda b, bl: jnp.exp(bl - b), b_l, bl_l)
        e_q_l = every(jnp.exp, b_l)
        dec_l = every(jnp.exp, bl_l)
        qe_l = every(lambda q, e: q * e, q_l, e_qm_l)
        ke_l = every(lambda f, e: (1.0 - f) * e, f_l, e_km_l)
        kd_l = every(lambda f, e: (1.0 - f) * e, f_l, e_kl_l)
        qb_l = every(lambda q, e: q * e, q_l, e_q_l)
        sc_l = every(lambda qe, ke: jnp.where(tri, _bdot(qe, ke, NT), 0.0), qe_l, ke_l)
        o_l = every(lambda sc, hi, qb, st: _bdot(sc, hi, NN) + _bdot(qb, st, NT), sc_l, hi_l, qb_l, st_l)
        vj_l = every(lambda o, rows: jax.vjp(_hg_out, o, hg_ref[rows, :], gout_ref[...])[1](do_ref[rows, :]), o_l, rows_l)
        do_l = [v[0] for v in vj_l]
        dsc_l = every(lambda do, hi: jnp.where(tri, _bdot(do, hi, NT), 0.0), do_l, hi_l)
        dv1_l = every(lambda sc, do: _bdot(sc, do, TN), sc_l, do_l)
        dqe_l = every(lambda dsc, ke: _bdot(dsc, ke, NN), dsc_l, ke_l)
        dke_l = every(lambda dsc, qe: _bdot(dsc, qe, TN), dsc_l, qe_l)
        dqb_l = every(lambda do, st: _bdot(do, st, NN), do_l, st_l)
        own_l = every(lambda do, qb: _bdot(do, qb, TN), do_l, qb_l)
        dst_next_l = [None] * n
        for j in reversed(range(n)):
            dst_next_l[j] = dst
            dst = own_l[j] + dst * dec_l[j]
        dv_l = every(lambda dv1, kd, dn: dv1 + _bdot(kd, dn, NT), dv1_l, kd_l, dst_next_l)
        dkd_l = every(lambda hi, dn: _bdot(hi, dn, NN), hi_l, dst_next_l)
        ddec_l = every(lambda dn, st: jnp.sum(dn * st, axis=0, keepdims=True), dst_next_l, st_l)
        rowi = lax.broadcasted_iota(jnp.int32, (C, HG_DK), 0)
        tq_l = every(lambda a, b_: a * b_, dqe_l, qe_l)
        tk_l = every(lambda a, b_: a * b_, dke_l, ke_l)
        td_l = every(lambda a, b_: a * b_, dkd_l, kd_l)
        tb_l = every(lambda a, b_: a * b_, dqb_l, qb_l)
        db_l = every(lambda tq, tk, td, tb, ddec, dec: tq - tk - td + tb
                     + jnp.where(rowi == C // 2 - 1, jnp.sum(tk - tq, axis=0, keepdims=True), 0.0)
                     + jnp.where(rowi == C - 1, jnp.sum(td, axis=0, keepdims=True) + ddec * dec, 0.0),
                     tq_l, tk_l, td_l, tb_l, ddec_l, dec_l)
        dlf_l = every(lambda db: _dot(trif, db, TN, precision=HIGHEST), db_l)
        dk_l = every(lambda dke, e1, dkd, e2: dke * e1 + dkd * e2, dke_l, e_km_l, dkd_l, e_kl_l)
        df_l = every(lambda dlf, f, dk: dlf / f - dk, dlf_l, f_l, dk_l)
        d_lb = jnp.zeros((1, HG_DK), F32)
        d_go = jnp.zeros((1, HG_DK), F32)
        for j, rows in enumerate(rows_l):
            sg, hq = sg_l[j], hq_l[j]
            df_ref[rows, :] = (df_l[j] * (1.0 - lb) * sg * (1.0 - sg)).astype(df_ref.dtype)
            sq = jax.nn.sigmoid(hq)
            dq = dqe_l[j] * e_qm_l[j] + dqb_l[j] * e_q_l[j]
            dq_ref[rows, :] = (dq * (sq * (1.0 + hq * (1.0 - sq)))).astype(dq_ref.dtype)
            di_ref[rows, :] = dv_l[j].astype(di_ref.dtype)
            dg_ref[rows, :] = vj_l[j][1].astype(dg_ref.dtype)
            d_lb += jnp.sum(df_l[j] * (1.0 - sg), axis=0, keepdims=True)
            d_go += vj_l[j][2]
        return dst, d_lb, d_go

    blk = lambda s: pl.BlockSpec((T, HG_DK), functools.partial(lambda h, n, s: (N - 1 - n, s * H + h), s=s))
    oblk = pl.BlockSpec((T, HG_DK), lambda h, n: (N - 1 - n, h))
    vec = pl.BlockSpec((1, HG_DK), lambda h, n: (0, h))
    W = H * HG_DK
    return _call(
        body, [p4, p4, p4, p4, lb_logits, gout, s_all, d_out], name="hgrn_bwd", grid=(H, N),
        out_shape=tuple([jax.ShapeDtypeStruct((L, W), BF16)] * 4 + [jax.ShapeDtypeStruct((1, W), F32), jax.ShapeDtypeStruct((1, HG_DK), F32)]),
        in_specs=[blk(0), blk(1), blk(2), blk(3), pl.BlockSpec((2, HG_DK), lambda h, n: (0, h)),
                  pl.BlockSpec((1, HG_DK), lambda h, n: (0, 0)),
                  pl.BlockSpec((1, GR, HG_DK, HG_DK), lambda h, n: (h, N - 1 - n, 0, 0)), oblk],
        out_specs=(oblk, oblk, oblk, oblk, vec, pl.BlockSpec((1, HG_DK), lambda h, n: (0, 0))),
        scratch_shapes=[pltpu.VMEM((HG_DK, HG_DK), F32)], comm=comm)


def _bucket_ids():
    i = jnp.arange(AT_BLOCK, dtype=jnp.int32)[:, None]
    j = jnp.arange(2 * AT_BLOCK, dtype=jnp.int32)[None, :]
    n = jnp.maximum(i - j + AT_BLOCK, 0)
    nf = jnp.maximum(n, 1).astype(F32)
    large = MAX_EXACT + (jnp.log(nf / MAX_EXACT) / math.log(MAX_DISTANCE / MAX_EXACT) * (N_BUCKETS - MAX_EXACT)).astype(jnp.int32)
    large = jnp.minimum(large, N_BUCKETS - 1)
    return jnp.where(n < MAX_EXACT, n, large).reshape(1, -1)


def _onehot(bucket):
    ids = lax.broadcasted_iota(jnp.int32, (N_BUCKETS, bucket.shape[1]), 0)
    return (ids == bucket).astype(F32)


def _attn_probs(qn, kpn, kcn, bias_g, sink, first, scale):
    rows = qn.shape[0]
    i = jnp.bitwise_and(lax.broadcasted_iota(jnp.int32, (rows, AT_BLOCK), 0), AT_BLOCK - 1)
    j = lax.broadcasted_iota(jnp.int32, (rows, AT_BLOCK), 1)
    lp = _bdot(qn, kpn, NT) * scale + bias_g[:, :AT_BLOCK]
    lc = _bdot(qn, kcn, NT) * scale + bias_g[:, AT_BLOCK:]
    lp = jnp.where(jnp.logical_and(j > i, jnp.logical_not(first)), lp, NEG_INF)
    lc = jnp.where(j <= i, lc, NEG_INF)
    m = jnp.maximum(jnp.maximum(jnp.max(lp, axis=-1, keepdims=True), jnp.max(lc, axis=-1, keepdims=True)), sink)
    pp, pc, ps = jnp.exp(lp - m), jnp.exp(lc - m), jnp.exp(sink - m)
    den = jnp.sum(pp, axis=-1, keepdims=True) + jnp.sum(pc, axis=-1, keepdims=True) + ps
    return pp / den, pc / den, ps / den


def _sink_rows(sk_ref, G):
    head = lax.broadcasted_iota(jnp.int32, (G * AT_BLOCK, 1), 0) // AT_BLOCK
    sink = jnp.zeros((G * AT_BLOCK, 1), F32)
    for g in range(G):
        sink = jnp.where(head == g, sk_ref[0, g:g + 1, :], sink)
    return sink


def _attn_fwd(q_t, kp, vp, qg, kg, sinks, bias, KVH, comm=None):
    AH, L, DH = q_t.shape
    G = AH // KVH
    NB = L // AT_BLOCK
    scale = DH ** -0.5

    def body(q_ref, kp_ref, kc_ref, vp_ref, vc_ref, qg_ref, kg_ref, sk_ref, b_ref, o_ref):
        first = pl.program_id(1) == 0
        kpn, kcn = _rms(kp_ref[0], kg_ref[...]), _rms(kc_ref[0], kg_ref[...])
        qn = _rms(q_ref[...].reshape(G * AT_BLOCK, DH), qg_ref[...])
        sink = _sink_rows(sk_ref, G)
        pp, pc, _ = _attn_probs(qn, kpn, kcn, b_ref[...].reshape(G * AT_BLOCK, 2 * AT_BLOCK), sink, first, scale)
        o = _bdot(pp, vp_ref[0], NN) + _bdot(pc, vc_ref[0], NN)
        o_ref[...] = o.reshape(G, AT_BLOCK, DH).astype(o_ref.dtype)

    kblk = lambda off: pl.BlockSpec((1, AT_BLOCK, DH),
                                    functools.partial(lambda h, n, off: (h, jnp.maximum(n + off - 1, 0), 0), off=off))
    return _call(
        body, [q_t, kp, kp, vp, vp, qg, kg, sinks, bias], name="attn_fwd", grid=(KVH, NB),
        out_shape=jax.ShapeDtypeStruct((AH, L, DH), BF16),
        in_specs=[pl.BlockSpec((G, AT_BLOCK, DH), lambda h, n: (h, n, 0)), kblk(0), kblk(1), kblk(0), kblk(1),
                  pl.BlockSpec((1, DH), lambda h, n: (0, 0)), pl.BlockSpec((1, DH), lambda h, n: (0, 0)),
                  pl.BlockSpec((1, G, 1), lambda h, n: (h, 0, 0)),
                  pl.BlockSpec((G, AT_BLOCK, 2 * AT_BLOCK), lambda h, n: (h, 0, 0))],
        out_specs=pl.BlockSpec((G, AT_BLOCK, DH), lambda h, n: (h, n, 0)), comm=comm)


def _attn_bwd(q_t, kp, vp, qg, kg, sinks, bias, do_t, KVH, comm=None):
    AH, L, DH = q_t.shape
    G = AH // KVH
    NB = L // AT_BLOCK
    B = AT_BLOCK
    scale = DH ** -0.5

    def body(q_ref, kp_ref, kc_ref, vp_ref, vc_ref, qg_ref, kg_ref, sk_ref, b_ref, do_ref,
             dq_ref, dk_ref, dv_ref, dqg_ref, dkg_ref, dsk_ref, db_ref):
        n = pl.program_id(1)
        first = n == 0

        @pl.when(first)
        def _():
            for r in (dk_ref, dv_ref, dsk_ref, db_ref):
                r[...] = jnp.zeros_like(r)

        @pl.when(jnp.logical_and(first, pl.program_id(0) == 0))
        def _():
            dqg_ref[...] = jnp.zeros_like(dqg_ref)
            dkg_ref[...] = jnp.zeros_like(dkg_ref)

        kp_raw, kc_raw, kgv, qgv = kp_ref[0], kc_ref[0], kg_ref[...], qg_ref[...]
        kpn, kp_vjp = jax.vjp(_rms, kp_raw, kgv)
        kcn, kc_vjp = jax.vjp(_rms, kc_raw, kgv)
        qn, q_vjp = jax.vjp(_rms, q_ref[...].reshape(G * B, DH), qgv)
        pp, pc, ps = _attn_probs(qn, kpn, kcn, b_ref[...].reshape(G * B, 2 * B), _sink_rows(sk_ref, G), first, scale)
        do = do_ref[...].reshape(G * B, DH)
        dvp = _bdot(pp, do, TN)
        dvc = _bdot(pc, do, TN)
        dpp = _bdot(do, vp_ref[0], NT)
        dpc = _bdot(do, vc_ref[0], NT)
        dsum = jnp.sum(dpp * pp, axis=-1, keepdims=True) + jnp.sum(dpc * pc, axis=-1, keepdims=True)
        dlp = pp * (dpp - dsum)
        dlc = pc * (dpc - dsum)
        dsk_ref[0] += jnp.sum((-ps * dsum).reshape(G, B, 1), axis=1)
        db_ref[:, :, :B] += dlp.reshape(G, B, B)
        db_ref[:, :, B:] += dlc.reshape(G, B, B)
        dlp, dlc = dlp * scale, dlc * scale
        dqn = _bdot(dlp, kpn, NN) + _bdot(dlc, kcn, NN)
        dq_raw, dqg = q_vjp(dqn)
        dq_ref[...] = dq_raw.reshape(G, B, DH).astype(dq_ref.dtype)
        dkp_raw, dkg_p = kp_vjp(_bdot(dlp, qn, TN))
        dkc_raw, dkg_c = kc_vjp(_bdot(dlc, qn, TN))
        r0 = pl.multiple_of(jnp.maximum(n - 1, 0) * B, B)
        r1 = pl.multiple_of(n * B, B)
        dk_ref[0, pl.ds(r0, B), :] += dkp_raw
        dk_ref[0, pl.ds(r1, B), :] += dkc_raw
        dv_ref[0, pl.ds(r0, B), :] += dvp
        dv_ref[0, pl.ds(r1, B), :] += dvc
        dqg_ref[...] += dqg
        dkg_ref[...] += dkg_p + dkg_c

    kblk = lambda off: pl.BlockSpec((1, B, DH), functools.partial(lambda h, n, off: (h, jnp.maximum(n + off - 1, 0), 0), off=off))
    qblk = pl.BlockSpec((G, B, DH), lambda h, n: (h, n, 0))
    accblk = pl.BlockSpec((1, L, DH), lambda h, n: (h, 0, 0))
    vecblk = pl.BlockSpec((1, DH), lambda h, n: (0, 0))
    return _call(
        body, [q_t, kp, kp, vp, vp, qg, kg, sinks, bias, do_t], name="attn_bwd", grid=(KVH, NB),
        out_shape=(jax.ShapeDtypeStruct((AH, L, DH), BF16), jax.ShapeDtypeStruct((KVH, L, DH), F32),
                   jax.ShapeDtypeStruct((KVH, L, DH), F32), jax.ShapeDtypeStruct((1, DH), F32),
                   jax.ShapeDtypeStruct((1, DH), F32), jax.ShapeDtypeStruct((KVH, G, 1), F32),
                   jax.ShapeDtypeStruct((AH, B, 2 * B), F32)),
        in_specs=[qblk, kblk(0), kblk(1), kblk(0), kblk(1),
                  pl.BlockSpec((1, DH), lambda h, n: (0, 0)), pl.BlockSpec((1, DH), lambda h, n: (0, 0)),
                  pl.BlockSpec((1, G, 1), lambda h, n: (h, 0, 0)),
                  pl.BlockSpec((G, B, 2 * B), lambda h, n: (h, 0, 0)), qblk],
        out_specs=(qblk, accblk, accblk, vecblk, vecblk, pl.BlockSpec((1, G, 1), lambda h, n: (h, 0, 0)),
                   pl.BlockSpec((G, B, 2 * B), lambda h, n: (h, 0, 0))), comm=comm)


def _heads_first(t, nh):
    L = t.shape[0]
    return jnp.transpose(t.reshape(L, nh, t.shape[1] // nh), (1, 0, 2))


def _heads_last(t):
    nh, L, dh = t.shape
    return jnp.transpose(t, (1, 0, 2)).reshape(L, nh * dh)


def _softmax0(lg):
    e = jnp.exp(lg - jnp.max(lg, axis=0, keepdims=True))
    return e[0:1] / jnp.sum(e, axis=0, keepdims=True)


def _ada_update_call(fn, c_all, d_cols, w, m, v, rt):
    D, n = w.shape

    def body(c_ref, d_ref, w_ref, m_ref, v_ref, g_out, dl_out, m_out, v_out):
        outs, _ = fn(c_ref[...], d_ref[...], w_ref[...], m_ref[...], v_ref[...])
        for r, val in zip((g_out, dl_out, m_out, v_out), outs):
            r[...] = val

    wblk = pl.BlockSpec((rt, n), lambda i: (i, 0))
    return _call(
        body, [c_all, d_cols, w, m, v], name="update_ada", grid=(D // rt,), out_shape=tuple([jax.ShapeDtypeStruct((D, n), F32)] * 4),
        in_specs=[pl.BlockSpec((N_DEV, rt), lambda i: (0, i)), pl.BlockSpec((N_DEV, n), lambda i: (0, 0)), wblk, wblk, wblk],
        out_specs=(wblk, wblk, wblk, wblk))


def kernel(x, c, w_ada, b_ada, norm1_g, norm2_g, w_in, hg_lb_logits, hg_out_norm_g, q_norm_g, k_norm_g, attn_sinks, rel_bias_table, w_branch_hg, w_branch_attn, w_out, w_ff1, w_ff2, loss_target, m_w_ada, m_b_ada, m_norm1_g, m_norm2_g, m_w_in, m_hg_lb_logits, m_hg_out_norm_g, m_q_norm_g, m_k_norm_g, m_attn_sinks, m_rel_bias_table, m_w_branch_hg, m_w_branch_attn, m_w_out, m_w_ff1, m_w_ff2, v_w_ada, v_b_ada, v_norm1_g, v_norm2_g, v_w_in, v_hg_lb_logits, v_hg_out_norm_g, v_q_norm_g, v_k_norm_g, v_attn_sinks, v_rel_bias_table, v_w_branch_hg, v_w_branch_attn, v_w_out, v_w_ff1, v_w_ff2):
    cc = lax.axis_index("c")
    me = 4 * lax.axis_index("x") + 2 * lax.axis_index("y") + cc
    x2 = x[0]
    tgt = loss_target[0]
    L, D = x2.shape
    HGW = hg_lb_logits.shape[1]
    H = HGW // HG_DK
    AH = attn_sinks.shape[1]
    DH = q_norm_g.shape[1]
    ATW = AH * DH
    BW = w_in.shape[2]
    INW = BW * N_DEV
    A = BW // LANES
    assert BW == LANES * A + LANES // 2
    KVW = (INW - 4 * HGW - ATW - 2 * D) // 2
    KVH = KVW // DH
    G = AH // KVH
    ADA_N = w_ada.shape[2]
    PAIR = 2 * A + 1

    c_all = _gather_small(c, me, "gather_c")[:, 0, :]
    b_cols = lax.dynamic_slice(b_ada, (0, me * ADA_N), (1, ADA_N))
    (ada_cols,) = _whole(lambda cv, w, b: (_bdot(_silu(cv), w, NN) + b,), [c_all, w_ada[0], b_cols],
                         [((N_DEV, ADA_N), F32)], "ada_fwd")
    ada_all = _gather_small(ada_cols, me, "gather_ada")
    ada_row = lax.dynamic_slice(ada_all, (0, me, 0), (N_DEV, 1, ADA_N)).reshape(1, 6 * D)

    w_in_b = w_in[0].astype(BF16)
    src_in = jnp.where(cc == 0, jnp.pad(w_in_b, ((0, 0), (0, LANES // 2))), jnp.pad(w_in_b, ((0, 0), (LANES // 2, 0))))
    (src_in,) = _behind([src_in], [ada_row])
    shift1, scale1, gate1, shift2, scale2, gate2 = [ada_row[:, i * D:(i + 1) * D] for i in range(6)]
    w_in_gapped, w_in_mid = _ag_w_in(src_in, A, D, INW)
    w_in_full = _patch_mid(w_in_gapped, w_in_mid, A)

    wnames = ("bhg", "bat", "out", "ff1", "ff2")
    small = ("bhg", "bat", "out")
    waxis = dict(zip(wnames, (1, 1, 0, 1, 0)))
    wsrc = dict(zip(wnames, (w_branch_hg, w_branch_attn, w_out, w_ff1, w_ff2)))
    wblk = {k: wsrc[k][0].astype(BF16) for k in wnames}
    wf = {}

    (h,) = _rowwise(lambda xv, g, sh, sc: ((_modnorm(xv, g, sh, sc),), ()), [(x2, D, 0)], [norm1_g, shift1, scale1],
                    [(D, BF16)], [], "norm1")
    o4, oa = 4 * HGW, 4 * HGW + ATW + 2 * KVW
    r1, r2 = wblk["ff1"].shape[0], wblk["ff2"].shape[0]
    assert o4 % D == 0

    def proj_order(tn_, j):
        t4, tg, ng = o4 // tn_, oa // tn_, (INW - oa) // tn_
        return jnp.where(j < t4, j, jnp.where(j < t4 + ng, j + (tg - t4), j - ng))

    cm = _Comm()
    hs = {k: _ag_ici(cm, wblk[k], waxis[k]) for k in small}
    hs["ff2"] = _ag_ici(cm, wblk["ff2"], waxis["ff2"], rows=(0, r2 // 4))
    proj = _mm(h, w_in_full, "nn", F32, "proj", comm=cm, b_order=proj_order)
    half = {k: cm.result(hs[k]) for k in hs}
    p4 = pg = proj
    GATE0 = o4 // D
    pa = proj[:, o4 + (INW - oa):]

    cm = _Comm()
    hs = {k: _ag_d2d(cm, half[k], waxis[k]) for k in small}
    hs["ff1"] = _ag_ici(cm, wblk["ff1"], waxis["ff1"], rows=(0, r1 // 2))
    o_hg, s_all = _hgrn_fwd(p4, hg_lb_logits, hg_out_norm_g, H, comm=cm)
    wf["bhg"], wf["bat"], wf["out"], half["ff1"] = (cm.result(hs[k]) for k in ("bhg", "bat", "out", "ff1"))

    bucket = _bucket_ids()
    (bias_flat,) = _whole(lambda tb, bk: (_dot(tb, _onehot(bk), TN, precision=HIGHEST),), [rel_bias_table, bucket],
                          [((AH, AT_BLOCK * 2 * AT_BLOCK), F32)], "bias_fwd")
    bias = bias_flat.reshape(AH, AT_BLOCK, 2 * AT_BLOCK)
    q_t = _heads_first(pa[:, :ATW], AH)
    kp = _heads_first(pa[:, ATW:ATW + KVW], KVH)
    vp = _heads_first(pa[:, ATW + KVW:], KVH)
    sinks3 = attn_sinks.reshape(KVH, G, 1)
    cm = _Comm()
    hs = {"ff1": _ag_ici(cm, wblk["ff1"], waxis["ff1"], rows=(r1 // 2, r1), into=half["ff1"])}
    o_at = _heads_last(_attn_fwd(q_t, kp, vp, q_norm_g, k_norm_g, sinks3, bias, KVH, comm=cm))
    half["ff1"] = cm.result(hs["ff1"])

    bh = _mm(o_hg, wf["bhg"], "nn", F32, "branch_hg")
    ba = _mm(o_at, wf["bat"], "nn", F32, "branch_at")

    def merge_fn(bhv, bav, ghg, gat):
        return jax.nn.sigmoid(ghg) * bhv + jax.nn.sigmoid(gat) * bav

    cm = _Comm()
    hs = {"ff1": _ag_d2d(cm, half["ff1"], waxis["ff1"]),
          "ff2": _ag_ici(cm, wblk["ff2"], waxis["ff2"], rows=(r2 // 4, 3 * r2 // 8), into=half["ff2"])}
    (merged,) = _rowwise(lambda *a: ((merge_fn(*a),), ()), [(bh, D, 0), (ba, D, 0), (pg, D, GATE0), (pg, D, GATE0 + 1)], [],
                         [(D, BF16)], [], "merge", comm=cm)
    wf["ff1"], half["ff2"] = cm.result(hs["ff1"]), cm.result(hs["ff2"])
    cm = _Comm()
    hs = {"ff2": _ag_ici(cm, wblk["ff2"], waxis["ff2"], rows=(3 * r2 // 8, r2 // 2), into=half["ff2"])}
    mo = _mm(merged, wf["out"], "nn", F32, "out_proj", comm=cm)
    half["ff2"] = cm.result(hs["ff2"])

    def resid1(xv, mov, g1, g2n, sh, sc):
        x1v = xv + g1 * mov
        return (x1v, _modnorm(x1v, g2n, sh, sc)), ()

    x1, h2 = _rowwise(resid1, [(x2, D, 0), (mo, D, 0)], [gate1, norm2_g, shift2, scale2], [(D, F32), (D, BF16)], [], "resid1")
    cm = _Comm()
    hs = {"ff2": _ag_ici(cm, wblk["ff2"], waxis["ff2"], rows=(r2 // 2, r2), into=half["ff2"])}
    u, act = _mm(h2, wf["ff1"], "nn", (F32, BF16), "ff1", comm=cm, epi=lambda r: (r, jnp.square(jnp.maximum(r, 0.0))))
    half["ff2"] = cm.result(hs["ff2"])
    cm = _Comm()
    hs = {"ff2": _ag_d2d(cm, half["ff2"], waxis["ff2"])}
    _call(lambda: None, [], name="ag_d2d_ff2", out_shape=(), comm=cm)
    wf["ff2"] = cm.result(hs["ff2"])
    ff = _mm(act, wf["ff2"], "nn", F32, "ff2")

    def loss_fn(x1v, ffv, tv, g2):
        e = x1v + g2 * ffv - tv
        dy = e * (1.0 / D)
        return (dy, dy * g2), (jnp.sum(e * e, axis=0, keepdims=True), jnp.sum(dy * ffv, axis=0, keepdims=True))

    dy, d_ff, sq_sum, d_gate2 = _rowwise(loss_fn, [(x1, D, 0), (ff, D, 0), (tgt, D, 0)], [gate2],
                                         [(D, F32), (D, BF16)], [(1, D), (1, D)], "loss")
    loss = lax.psum(jnp.sum(sq_sum) * (0.5 / D), ("x", "y", "c"))

    owner_base = jnp.stack([me ^ r for r in CHIP_RELS]).astype(jnp.int32)
    gw, recv1, part, recv2 = {}, {}, {}, {}
    gw["ff2"] = _mm(act, d_ff, "tn", BF16, "dw_ff2")
    cm = _Comm()
    hh = _rs_d2d(cm, gw["ff2"], waxis["ff2"])
    d_u = _mm(d_ff, wf["ff2"], "nt", BF16, "d_act", comm=cm, extras=[u], epi=lambda r, uv: (r * (2.0 * jnp.maximum(uv, 0.0)),))
    part["ff2"] = _rs_add(gw["ff2"], cm.result(hh), waxis["ff2"], owner_base, "rs_add_ff2")
    rows_ff2 = part["ff2"].shape[1]
    cm = _Comm()
    hh = _rs_ici(cm, part["ff2"], rows=(0, rows_ff2 // 2))
    gw["ff1"] = _mm(h2, d_u, "tn", BF16, "dw_ff1", comm=cm)
    cm2 = _Comm()
    hh2 = _rs_ici(cm2, part["ff2"], rows=(rows_ff2 // 2, rows_ff2), recv=cm.result(hh))
    hh1 = _rs_d2d(cm2, gw["ff1"], waxis["ff1"])
    d_h2 = _mm(d_u, wf["ff1"], "nt", F32, "d_h2", comm=cm2)
    recv2["ff2"] = cm2.result(hh2)
    part["ff1"] = _rs_add(gw["ff1"], cm2.result(hh1), waxis["ff1"], owner_base, "rs_add_ff1")

    def norm2_bwd(dh2v, x1v, dyv, mov, g2n, sh, sc, g1):
        _, vjp = jax.vjp(_modnorm, x1v, g2n, sh, sc)
        dx, dg, dsh, dsc = vjp(dh2v)
        dx1 = dyv + dx
        return (dx1, dx1 * g1), (dg, dsh, dsc, jnp.sum(dx1 * mov, axis=0, keepdims=True))

    d_x1, d_mo, d_g2n, d_shift2, d_scale2, d_gate1 = _rowwise(
        norm2_bwd, [(d_h2, D, 0), (x1, D, 0), (dy, D, 0), (mo, D, 0)], [norm2_g, shift2, scale2, gate1],
        [(D, F32), (D, BF16)], [(1, D)] * 4, "norm2_bwd")
    gw["out"] = _mm(merged, d_mo, "tn", BF16, "dw_out")
    d_merged = _mm(d_mo, wf["out"], "nt", F32, "d_merged")

    def merge_bwd(dmv, bhv, bav, ghg, gat):
        _, vjp = jax.vjp(merge_fn, bhv, bav, ghg, gat)
        return vjp(dmv), ()

    d_bh, d_ba, d_ghg, d_gat = _rowwise(merge_bwd, [(d_merged, D, 0), (bh, D, 0), (ba, D, 0), (pg, D, GATE0), (pg, D, GATE0 + 1)], [],
                                        [(D, BF16)] * 4, [], "merge_bwd")
    gw["bhg"] = _mm(o_hg, d_bh, "tn", BF16, "dw_bhg")
    gw["bat"] = _mm(o_at, d_ba, "tn", BF16, "dw_bat")
    d_ohg = _mm(d_bh, wf["bhg"], "nt", F32, "d_ohg")
    d_oat = _mm(d_ba, wf["bat"], "nt", BF16, "d_oat")
    rows_ff1 = part["ff1"].shape[1]
    cut_ff1 = 3 * rows_ff1 // 8
    cm = _Comm()
    hf1 = _rs_ici(cm, part["ff1"], rows=(0, cut_ff1))
    d_hq, d_hf, d_hi, d_hg, d_lb, d_gout_h = _hgrn_bwd(p4, hg_lb_logits, hg_out_norm_g, s_all, d_ohg, H, comm=cm)
    cm2 = _Comm()
    hf1 = _rs_ici(cm2, part["ff1"], rows=(cut_ff1, rows_ff1), recv=cm.result(hf1))
    hh = {k: _rs_d2d(cm2, gw[k], waxis[k]) for k in small}
    dq_t, dkp, dvp, d_qg, d_kg, d_sk, d_bias = _attn_bwd(q_t, kp, vp, q_norm_g, k_norm_g, sinks3, bias,
                                                         _heads_first(d_oat, AH), KVH, comm=cm2)
    recv2["ff1"] = cm2.result(hf1)
    for k in small:
        part[k] = _rs_add(gw[k], cm2.result(hh[k]), waxis[k], owner_base, "rs_add_" + k)
    d_aq = _heads_last(dq_t)
    d_ak = _heads_last(dkp).astype(BF16)
    d_av = _heads_last(dvp).astype(BF16)
    d_proj = jnp.concatenate([d_hq, d_hf, d_hi, d_hg, d_aq, d_ak, d_av, d_ghg, d_gat], axis=1)
    cm = _Comm()
    hh = {k: _rs_ici(cm, part[k]) for k in small}
    gw_in = _mm(h, d_proj, "tn", BF16, "dw_in", comm=cm)
    for k in small:
        recv2[k] = cm.result(hh[k])

    wm = LANES * A
    cm = _Comm()
    hi_ = cm.inp(gw_in)
    h_main, h_mid = cm.out((4, D, wm), BF16), cm.out((4, D, LANES), BF16)
    for i, r in enumerate(CHIP_RELS):
        def main_view(ref, p, r=r):
            o = p["me"] ^ r ^ 1
            return ref.at[:, pl.ds(pl.multiple_of((PAIR * (o // 2) + (A + 1) * (1 - p["c"])) * LANES, LANES), wm)]

        def mid_view(ref, p, r=r):
            o = p["me"] ^ r
            return ref.at[:, pl.ds(pl.multiple_of((PAIR * (o // 2) + A) * LANES, LANES), LANES)]

        cm.copy(hi_, main_view, h_main, _slot_view(i), 1)
        cm.copy(hi_, mid_view, h_mid, _slot_view(i), 1)
    _call(lambda: None, [], name="rs_d2d_in", out_shape=(), comm=cm)
    chip = jnp.stack([(me ^ r) // 2 for r in CHIP_RELS]).astype(jnp.int32)
    part_main = _rs_add(gw_in, cm.result(h_main), 1, PAIR * chip + (A + 1) * cc, "rs_add_in_main", tw=LANES)
    part_mid = _rs_add(gw_in, cm.result(h_mid), 1, PAIR * chip + A, "rs_add_in_mid", tw=LANES)
    rs_in = _rs_split_start([part_main, part_mid], "rs_in_start")
    d_h = _mm(d_proj, w_in_full, "nt", F32, "d_h", tn=D, after=[rs_in["token"]])

    def norm1_bwd(dhv, xv, dx1v, g1n, sh, sc):
        _, vjp = jax.vjp(_modnorm, xv, g1n, sh, sc)
        dx, dg, dsh, dsc = vjp(dhv)
        return (dx1v + dx,), (dg, dsh, dsc)

    grad_x, d_g1n, d_shift1, d_scale1 = _rowwise(norm1_bwd, [(d_h, D, 0), (x2, D, 0), (d_x1, D, 0)],
                                                 [norm1_g, shift1, scale1], [(D, F32)], [(1, D)] * 3, "norm1_bwd")

    def sum4(p0, p1, p2, p3):
        return ((p0.astype(F32) + p1.astype(F32)) + p2.astype(F32)) + p3.astype(F32)

    def update_fn(w, m, v, p0, p1, p2, p3):
        g = sum4(p0, p1, p2, p3)
        delta, mn, vn = _adamw(w, g, m, v)
        return (g, delta, mn, vn), ()

    wmv = dict(zip(wnames, ((w_branch_hg, m_w_branch_hg, v_w_branch_hg), (w_branch_attn, m_w_branch_attn, v_w_branch_attn),
                            (w_out, m_w_out, v_w_out), (w_ff1, m_w_ff1, v_w_ff1), (w_ff2, m_w_ff2, v_w_ff2))))
    res = {}

    def update(k, p, rx):
        w, m, v = (t[0] for t in wmv[k])
        n = w.shape[1]
        ins = [(t, n, 0) for t in (w, m, v)] + [(p, n, 0, 0)] + [(rx, n, 0, i) for i in range(3)]
        res[k] = [t[None] for t in _rowwise(update_fn, ins, [], [(n, F32)] * 4, [], "update_" + k)]

    for k in wnames:
        update(k, part[k], recv2[k])
    d_sinks = d_sk.reshape(1, AH)
    (d_table_t,) = _whole(lambda db, bk: (_dot(db, _onehot(bk), NT, precision=HIGHEST),),
                          [d_bias.reshape(AH, AT_BLOCK * 2 * AT_BLOCK), bucket], [((AH, N_BUCKETS), F32)], "bias_bwd")
    smalls = [d_g1n, d_g2n, d_lb, d_gout_h, d_qg, d_kg, d_sinks, d_table_t.T.reshape(1, N_BUCKETS * AH)]
    widths = [s.shape[1] for s in smalls]
    lanes = [-(-w // LANES) * LANES for w in widths]
    smalls = [jnp.pad(s, ((0, 0), (0, p - w))) for s, w, p in zip(smalls, widths, lanes)]
    tail_row = jnp.concatenate([d_shift1, d_scale1, d_gate1, d_shift2, d_scale2, d_gate2] + smalls, axis=1)
    (part_main, part_mid), (rx_main, rx_mid) = _rs_split_wait(rs_in, [grad_x] + [res[k][0] for k in wnames], "rs_in_wait")
    (tail_row,) = _behind([tail_row], [rx_mid])
    tail_plans = [(0, _whole_view, 1, lambda ref, p: ref.at[p["me"]], rel) for rel in range(1, N_DEV)]
    tail_send, tail_recv, tail_arrays, tail_token = _split_start(
        [tail_row, lax.empty((N_DEV,) + tail_row.shape, F32)], tail_plans, "tail_start")
    (part_main,) = _behind([part_main], [tail_token])
    g_main, = _rowwise(lambda *p: ((sum4(*p),), ()), [(part_main, wm, 0, 0)] + [(rx_main, wm, 0, i) for i in range(3)], [],
                       [(wm, F32)], [], "sum_in_main")
    g_mid, = _rowwise(lambda *p: ((sum4(*p),), ()), [(part_mid, LANES, 0, 0)] + [(rx_mid, LANES, 0, i) for i in range(3)], [],
                      [(LANES, F32)], [], "sum_in_mid")
    g_in = jnp.where(cc == 0, jnp.concatenate([g_main, g_mid[:, :LANES // 2]], axis=1),
                     jnp.concatenate([g_mid[:, LANES // 2:], g_main], axis=1))

    def update_given(w, m, v, g):
        delta, mn, vn = _adamw(w, g, m, v)
        return (g, delta, mn, vn), ()

    res["in"] = [t[None] for t in _rowwise(update_given, [(t, BW, 0) for t in (w_in[0], m_w_in[0], v_w_in[0], g_in)], [],
                                           [(BW, F32)] * 4, [], "update_in")]

    tail_row, tail_land = _split_wait(tail_send, tail_recv, tail_arrays, tail_plans, [res["in"][0]], "tail_wait")
    tail_all = lax.dynamic_update_slice(tail_land, tail_row[None], (me, 0, 0))[:, 0, :]
    d_ada_all, packed = tail_all[:, :6 * D], tail_all[:, 6 * D:]
    d_ada_cols = lax.dynamic_slice(d_ada_all, (0, me * ADA_N), (N_DEV, ADA_N))

    def ada_update(cv, dav, w, m, v):
        g = _bdot(_silu(cv), dav, TN)
        delta, mn, vn = _adamw(w, g, m, v)
        return (g, delta, mn, vn), ()

    res["ada"] = [t[None] for t in _ada_update_call(ada_update, c_all, d_ada_cols, w_ada[0], m_w_ada[0], v_w_ada[0], _tile(D, 256, 16))]

    offs = [sum(lanes[:i]) for i in range(len(lanes))]

    def small_update(pk, dada, lg, *wmv_flat):
        tot = pk[0:1]
        for d in range(1, N_DEV):
            tot = tot + pk[d:d + 1]
        gb = dada[0:1]
        for d in range(1, N_DEV):
            gb = gb + dada[d:d + 1]
        gs = [tot[:, offs[i]:offs[i] + widths[i]] for i in range(len(widths))]
        _, lb_vjp = jax.vjp(_softmax0, lg)
        (g_lg,) = lb_vjp(gs[2])
        grads = [gb, gs[0], gs[1], g_lg, gs[3], gs[4], gs[5], gs[6], gs[7]]
        outs = []
        for i, g in enumerate(grads):
            w, m, v = wmv_flat[3 * i:3 * i + 3]
            delta, mn, vn = _adamw(w, g, m, v)
            outs += [g, delta, mn, vn]
        return tuple(outs)

    tbl = lambda t: t.reshape(1, N_BUCKETS * AH)
    small_wmv = [(b_ada, m_b_ada, v_b_ada), (norm1_g, m_norm1_g, v_norm1_g), (norm2_g, m_norm2_g, v_norm2_g),
                 (hg_lb_logits, m_hg_lb_logits, v_hg_lb_logits), (hg_out_norm_g, m_hg_out_norm_g, v_hg_out_norm_g),
                 (q_norm_g, m_q_norm_g, v_q_norm_g), (k_norm_g, m_k_norm_g, v_k_norm_g),
                 (attn_sinks, m_attn_sinks, v_attn_sinks),
                 (tbl(rel_bias_table), tbl(m_rel_bias_table), tbl(v_rel_bias_table))]
    flat = [t for trip in small_wmv for t in trip]
    out_shapes = [(trip[0].shape, F32) for trip in small_wmv for _ in range(4)]
    sres = _whole(small_update, [packed, d_ada_all, hg_lb_logits] + flat, out_shapes, "small_update")
    names_small = ("b_ada", "norm1_g", "norm2_g", "lb", "gout", "qg", "kg", "sinks", "table")
    for i, k in enumerate(names_small):
        r = sres[4 * i:4 * i + 4]
        if k == "table":
            r = [t.reshape(N_BUCKETS, AH) for t in r]
        res[k] = r

    order = ("ada", "b_ada", "norm1_g", "norm2_g", "in", "lb", "gout", "qg", "kg", "sinks", "table", "bhg", "bat", "out", "ff1", "ff2")
    outs = [loss, grad_x[None]]
    for j in range(4):
        outs += [res[k][j] for k in order]
    return tuple(outs)
```

```python
import functools
import math

import jax
import jax.numpy as jnp
from jax import lax
from jax.experimental import pallas as pl
from jax.experimental.pallas import tpu as pltpu

F32 = jnp.float32
BF16 = jnp.bfloat16
EPS = 1e-6
NEG_INF = -1e30
HG_DK = 128
HG_CHUNK = 64
AT_BLOCK = 128
N_BUCKETS = 32
MAX_EXACT = 16
MAX_DISTANCE = 128
N_DEV = 8
LANES = 128
VMEM_LIMIT = 56 * 1024 * 1024
ADAM_LR, ADAM_B1, ADAM_B2, ADAM_EPS, ADAM_WD, ADAM_STEP = 0.001, 0.9, 0.999, 1e-08, 0.01, 10
HIGHEST = lax.Precision.HIGHEST
MESH = pl.DeviceIdType.MESH
ANY = pl.BlockSpec(memory_space=pl.ANY)
CHIP_RELS = (0, 4, 2, 6)

NN = (((1,), (0,)), ((), ()))
NT = (((1,), (1,)), ((), ()))
TN = (((0,), (0,)), ((), ()))


def _tile(n, pref, unit):
    if n <= pref:
        return n
    t = (pref // unit) * unit
    while t >= unit:
        if n % t == 0:
            return t
        t -= unit
    return n


def _dot(a, b, dn, precision=None):
    return lax.dot_general(a, b, dn, preferred_element_type=F32, precision=precision)


def _bdot(a, b, dn):
    return _dot(a.astype(BF16), b.astype(BF16), dn)


def _position():
    x, y, c = lax.axis_index("x"), lax.axis_index("y"), lax.axis_index("c")
    return dict(x=x, y=y, c=c, me=4 * x + 2 * y + c)


def _peer_position(p, rel):
    x = 1 - p["x"] if rel & 4 else p["x"]
    y = 1 - p["y"] if rel & 2 else p["y"]
    c = 1 - p["c"] if rel & 1 else p["c"]
    return dict(x=x, y=y, c=c, me=4 * x + 2 * y + c)


class _Comm:
    def __init__(self):
        self.ins, self.outs, self.alias, self.plans, self.res = [], [], {}, [], None

    def inp(self, arr):
        self.ins.append(arr)
        return ("i", len(self.ins) - 1)

    def out(self, shape, dtype, alias=None):
        self.outs.append(jax.ShapeDtypeStruct(tuple(shape), dtype))
        if alias is not None:
            self.alias[alias[1]] = len(self.outs) - 1
        return ("o", len(self.outs) - 1)

    def copy(self, src, src_view, dst, dst_view, rel):
        self.plans.append((src, src_view, dst, dst_view, rel))

    def result(self, handle):
        return self.res[handle[1]]

    def build(self, in_refs, out_refs, send_sems, recv_sems):
        pos = _position()
        ref = lambda h: in_refs[h[1]] if h[0] == "i" else out_refs[h[1]]
        ops = []
        for k, (src, sv, dst, dv, rel) in enumerate(self.plans):
            s = sv(ref(src), pos)
            if rel == 0:
                cp = pltpu.make_async_copy(s, dv(ref(dst), pos), send_sems.at[k])
                ops.append((cp.start, cp.wait))
                continue
            peer = _peer_position(pos, rel)
            mk = lambda d: pltpu.make_async_remote_copy(
                src_ref=s, dst_ref=d, send_sem=send_sems.at[k], recv_sem=recv_sems.at[k],
                device_id=(peer["x"], peer["y"], peer["c"]), device_id_type=MESH)
            out_cp, in_cp = mk(dv(ref(dst), pos)), mk(dv(ref(dst), peer))

            def wait(out_cp=out_cp, in_cp=in_cp):
                out_cp.wait_send()
                in_cp.wait_recv()

            ops.append((out_cp.start, wait))
        return ops


def _call(body, args, *, name, out_shape, in_specs=None, out_specs=None, grid=None, scratch_shapes=(), comm=None,
          prefetch=None, aliases=None, after=()):
    single = not isinstance(out_shape, (tuple, list))
    out_shape = (out_shape,) if single else tuple(out_shape)
    n_in, n_out, n_scr = len(args), len(out_shape), len(scratch_shapes)
    vm = pl.BlockSpec(memory_space=pltpu.VMEM)
    in_specs = [vm] * n_in if in_specs is None else list(in_specs)
    out_specs = [vm] * n_out if out_specs is None else (list(out_specs) if isinstance(out_specs, (tuple, list)) else [out_specs])
    n_pf = 0 if prefetch is None else len(prefetch)
    kw = {} if aliases is None else {"input_output_aliases": dict(aliases)}
    if comm is None and after:
        n_dep = len(after)

        def fn(*refs):
            body(*refs[:n_pf + n_in], *refs[n_pf + n_in + n_dep:])

        all_args, all_scratch = list(args) + list(after), list(scratch_shapes)
        in_specs = in_specs + [ANY] * n_dep
    elif comm is None:
        fn = body
        all_args, all_scratch = list(args), list(scratch_shapes)
    else:
        n_ci, n_co, n_x = len(comm.ins), len(comm.outs), len(comm.plans)

        def fn(*refs):
            pf, refs = refs[:n_pf], refs[n_pf:]
            o_in, c_in = refs[:n_in], refs[n_in:n_in + n_ci]
            o_out = refs[n_in + n_ci:n_in + n_ci + n_out]
            c_out = refs[n_in + n_ci + n_out:n_in + n_ci + n_out + n_co]
            scr = refs[n_in + n_ci + n_out + n_co:]
            ops = comm.build(c_in, c_out, scr[n_scr], scr[n_scr + 1])
            if grid:
                first = functools.reduce(jnp.logical_and, [pl.program_id(i) == 0 for i in range(len(grid))])
                last = functools.reduce(jnp.logical_and, [pl.program_id(i) == g - 1 for i, g in enumerate(grid)])

                @pl.when(first)
                def _():
                    for start, _w in ops:
                        start()
            else:
                for start, _w in ops:
                    start()
            body(*pf, *o_in, *o_out, *scr[:n_scr])
            if grid:
                @pl.when(last)
                def _():
                    for _s, wait in ops:
                        wait()
            else:
                for _s, wait in ops:
                    wait()

        all_args = list(args) + list(comm.ins)
        in_specs = in_specs + [ANY] * n_ci
        out_shape = out_shape + tuple(comm.outs)
        out_specs = out_specs + [ANY] * n_co
        all_scratch = list(scratch_shapes) + [pltpu.SemaphoreType.DMA((n_x,)), pltpu.SemaphoreType.DMA((n_x,))]
        kw["input_output_aliases"] = {n_pf + n_in + i: n_out + o for i, o in comm.alias.items()}
    sem = None if grid is None else ("arbitrary",) * len(grid)
    params = pltpu.CompilerParams(dimension_semantics=sem, vmem_limit_bytes=VMEM_LIMIT)
    if prefetch is None:
        spec = dict(in_specs=in_specs, out_specs=tuple(out_specs), scratch_shapes=all_scratch)
        if grid is not None:
            spec["grid"] = grid
    else:
        spec = dict(grid_spec=pltpu.PrefetchScalarGridSpec(
            num_scalar_prefetch=n_pf, grid=grid, in_specs=in_specs, out_specs=tuple(out_specs), scratch_shapes=all_scratch))
        all_args = list(prefetch) + all_args
    res = pl.pallas_call(fn, name=name, out_shape=out_shape, compiler_params=params, **spec, **kw)(*all_args)
    res = list(res)
    if comm is not None:
        comm.res = res[n_out:]
        res = res[:n_out]
    return res[0] if single else res


def _whole_view(ref, pos):
    return ref


def _block_view(axis, n, index, rows=None):
    def view(ref, pos):
        off = pl.multiple_of(index(pos) * n, n)
        if rows is None:
            return ref.at[:, pl.ds(off, n)] if axis == 1 else ref.at[pl.ds(off, n), :]
        lo, cnt = rows[0], rows[1] - rows[0]
        if axis == 1:
            return ref.at[pl.ds(lo, cnt), pl.ds(off, n)]
        return ref.at[pl.ds(pl.multiple_of(off + lo, 16), cnt), :]
    return view


def _rows_view(rows):
    def view(ref, pos):
        return ref if rows is None else ref.at[pl.ds(rows[0], rows[1] - rows[0]), :]
    return view


def _slot_view(i, rows=None):
    def view(ref, pos):
        return ref.at[i] if rows is None else ref.at[i, pl.ds(rows[0], rows[1] - rows[0]), :]
    return view


def _exchange(items, name):
    cm = _Comm()
    for a, rel in items:
        cm.copy(cm.inp(a), _whole_view, cm.out(a.shape, a.dtype), _whole_view, rel)
    _call(lambda: None, [], name=name, out_shape=(), comm=cm)
    return cm.res


def _gather_small(v, me, name):
    cm = _Comm()
    hi, ho = cm.inp(v), cm.out((N_DEV,) + v.shape, v.dtype)
    for rel in range(N_DEV):
        cm.copy(hi, _whole_view, ho, lambda ref, p: ref.at[p["me"]], rel)
    _call(lambda: None, [], name=name, out_shape=(), comm=cm)
    return cm.result(ho)


def _ag_ici(cm, blk, axis, rows=None, into=None):
    n = blk.shape[axis]
    shape = list(blk.shape)
    shape[axis] = n * N_DEV
    hi = cm.inp(blk)
    ho = cm.out(shape, blk.dtype) if into is None else cm.out(shape, blk.dtype, alias=cm.inp(into))
    own = _block_view(axis, n, lambda p: p["me"], rows)
    for rel in CHIP_RELS:
        cm.copy(hi, _rows_view(rows), ho, own, rel)
    return ho


def _ag_d2d(cm, full, axis):
    n = full.shape[axis] // N_DEV
    hi = cm.inp(full)
    ho = cm.out(full.shape, full.dtype, alias=hi)
    for r in CHIP_RELS:
        v = _block_view(axis, n, functools.partial(lambda p, r: p["me"] ^ r, r=r))
        cm.copy(hi, v, ho, v, 1)
    return ho


def _rs_d2d(cm, gw, axis):
    n = gw.shape[axis] // N_DEV
    shape = list(gw.shape)
    shape[axis] = n
    hi, ho = cm.inp(gw), cm.out([4] + shape, gw.dtype)
    for i, r in enumerate(CHIP_RELS):
        cm.copy(hi, _block_view(axis, n, functools.partial(lambda p, r: p["me"] ^ r ^ 1, r=r)), ho, _slot_view(i), 1)
    return ho


def _rs_ici(cm, part, rows=None, recv=None):
    if recv is None:
        ho = cm.out((3,) + part.shape[1:], part.dtype)
    else:
        ho = cm.out(recv.shape, recv.dtype, alias=cm.inp(recv))
    hi = cm.inp(part)
    for i in (1, 2, 3):
        cm.copy(hi, _slot_view(i, rows), ho, _slot_view(i - 1, rows), CHIP_RELS[i])
    return ho


def _rs_add(gw, recv, axis, base, name, tw=None):
    _, R, n = recv.shape
    fan = 1
    if axis == 1:
        tw = n if tw is None else tw
        fan = max(f for f in (4, 3, 2, 1) if (n // tw) % f == 0)
        gw_specs = [pl.BlockSpec((R, tw), functools.partial(lambda i, t, b, k: (0, b[i] + fan * t + k), k=k)) for k in range(fan)]
        rv_spec = pl.BlockSpec((None, R, tw * fan), lambda i, t, b: (i, 0, t))
        grid = (4, n // (tw * fan))
    else:
        tw = _tile(n, 1024, LANES)
        gw_specs = [pl.BlockSpec((R, tw), lambda i, t, b: (b[i], t))]
        rv_spec = pl.BlockSpec((None, R, tw), lambda i, t, b: (i, 0, t))
        grid = (4, n // tw)

    def body(b_ref, *refs):
        g_refs, r_ref, o_ref = refs[:fan], refs[fan], refs[fan + 1]
        g = g_refs[0][...] if fan == 1 else jnp.concatenate([g[...] for g in g_refs], axis=1)
        o_ref[...] = (g.astype(F32) + r_ref[...].astype(F32)).astype(o_ref.dtype)

    return _call(body, [gw] * fan + [recv], name=name, out_shape=jax.ShapeDtypeStruct(recv.shape, recv.dtype), grid=grid,
                 in_specs=gw_specs + [rv_spec], out_specs=rv_spec, prefetch=[base])


HBM_SPEC = pl.BlockSpec(memory_space=pltpu.HBM)
SEM_SPEC = pl.BlockSpec(memory_space=pltpu.SEMAPHORE)
SPLIT_PARAMS = pltpu.CompilerParams(has_side_effects=pltpu.SideEffectType.DATAFLOW_SIDE_EFFECTING)


def _split_copies(refs, plans, send_sems, recv_sems):
    pos = _position()
    out = []
    for k, (si, sv, li, lv, rel) in enumerate(plans):
        peer = _peer_position(pos, rel)
        mk = lambda d: pltpu.make_async_remote_copy(
            src_ref=sv(refs[si], pos), dst_ref=d, send_sem=send_sems.at[k], recv_sem=recv_sems.at[k],
            device_id=(peer["x"], peer["y"], peer["c"]), device_id_type=MESH)
        out.append((mk(lv(refs[li], pos)), mk(lv(refs[li], peer))))
    return out


def _split_start(arrays, plans, name):
    n = len(arrays)

    def body(*refs):
        send_sems, recv_sems = refs[n], refs[n + 1]
        for out_cp, _ in _split_copies(refs[:n], plans, send_sems, recv_sems):
            out_cp.start()
        refs[-1][...] = jnp.zeros_like(refs[-1])

    sems = pltpu.SemaphoreType.DMA((len(plans),))
    res = pl.pallas_call(
        body, name=name,
        out_shape=(sems, sems) + tuple(pltpu.HBM(a.shape, a.dtype) for a in arrays) + (jax.ShapeDtypeStruct((8, LANES), F32),),
        in_specs=[HBM_SPEC] * n, out_specs=(SEM_SPEC, SEM_SPEC) + (HBM_SPEC,) * n + (pl.BlockSpec(memory_space=pltpu.VMEM),),
        input_output_aliases={i: 2 + i for i in range(n)}, compiler_params=SPLIT_PARAMS,
    )(*[pltpu.with_memory_space_constraint(a, pltpu.HBM) for a in arrays])
    return res[0], res[1], list(res[2:2 + n]), res[-1]


def _split_wait(send_sems, recv_sems, arrays, plans, after, name):
    n, na = len(arrays), len(after)

    def body(*refs):
        for out_cp, in_cp in _split_copies(refs[:n], plans, refs[n], refs[n + 1]):
            out_cp.wait_send()
            in_cp.wait_recv()

    res = pl.pallas_call(
        body, name=name, out_shape=tuple(pltpu.HBM(a.shape, a.dtype) for a in arrays),
        in_specs=[HBM_SPEC] * n + [SEM_SPEC, SEM_SPEC] + [ANY] * na, out_specs=(HBM_SPEC,) * n,
        input_output_aliases={i: i for i in range(n)}, compiler_params=SPLIT_PARAMS,
    )(*arrays, send_sems, recv_sems, *after)
    return list(res)


def _rs_split_start(parts, name):
    nw = len(parts)
    lands = [lax.empty((3,) + p.shape[1:], p.dtype) for p in parts]
    plans = [(s, _slot_view(i), nw + s, _slot_view(i - 1), CHIP_RELS[i]) for s in range(nw) for i in (1, 2, 3)]
    send_sems, recv_sems, arrays, token = _split_start(list(parts) + lands, plans, name)
    return dict(sems=(send_sems, recv_sems), arrays=arrays, plans=plans, token=token, nw=nw)


def _rs_split_wait(h, after, name):
    arrays = _split_wait(h["sems"][0], h["sems"][1], h["arrays"], h["plans"], after, name)
    return arrays[:h["nw"]], arrays[h["nw"]:]


def _behind(xs, tokens):
    out = lax.optimization_barrier((tuple(xs), tuple(tokens)))
    return list(out[0])


def _ag_w_in(src, a, D, INW):
    wm = LANES * a

    hd = D // 2
    ALL, TOP, BOT = (0, D), (0, hd), (hd, D)

    def main_place(ref, p, rows=ALL):
        off = pl.multiple_of(((2 * a + 1) * (p["me"] // 2) + (a + 1) * p["c"]) * LANES, LANES)
        return ref.at[pl.ds(rows[0], rows[1] - rows[0]), pl.ds(off, wm)]

    def main_src(ref, p):
        return ref.at[:, pl.ds(pl.multiple_of(p["c"] * LANES, LANES), wm)]

    def mid_src(ref, p):
        return ref.at[:, pl.ds(pl.multiple_of((1 - p["c"]) * wm, LANES), LANES)]

    def mid_place(ref, p, rows=ALL):
        return ref.at[p["me"], pl.ds(rows[0], rows[1] - rows[0]), :]

    def body(src_ref, full_ref, mid_ref, send_sems, recv_sems):
        pos = _position()
        sib, xn, yn = (_peer_position(pos, r) for r in (1, 4, 2))
        dg = _peer_position(pos, 6)
        started = []

        def remote(k, s, d, to):
            return pltpu.make_async_remote_copy(src_ref=s, dst_ref=d, send_sem=send_sems.at[k], recv_sem=recv_sems.at[k],
                                                device_id=(to["x"], to["y"], to["c"]), device_id_type=MESH)

        def send(k, owner, rows, to, from_src=False):
            for j, (src_v, place) in enumerate(((main_src, main_place), (mid_src, mid_place))):
                s = src_v(src_ref, pos) if from_src else place(full_ref if j == 0 else mid_ref, owner, rows)
                cp = remote(k + j, s, place(full_ref if j == 0 else mid_ref, owner, rows), to)
                cp.start()
                started.append(cp)

        def landed(k, owner, rows, frm):
            for j, place in enumerate((main_place, mid_place)):
                ref = full_ref if j == 0 else mid_ref
                remote(k + j, place(ref, owner, rows), place(ref, owner, rows), frm).wait_recv()

        local = [pltpu.make_async_copy(main_src(src_ref, pos), main_place(full_ref, pos), send_sems.at[18]),
                 pltpu.make_async_copy(mid_src(src_ref, pos), mid_place(mid_ref, pos), send_sems.at[19])]
        for cp in local:
            cp.start()
        send(0, pos, ALL, sib, from_src=True)
        send(2, pos, ALL, xn, from_src=True)
        send(4, pos, ALL, yn, from_src=True)
        landed(2, xn, ALL, xn)
        send(10, xn, ALL, sib)
        send(6, xn, TOP, yn)
        landed(4, yn, ALL, yn)
        send(12, yn, ALL, sib)
        send(8, yn, BOT, xn)
        landed(6, dg, TOP, yn)
        send(14, dg, TOP, sib)
        landed(8, dg, BOT, xn)
        send(16, dg, BOT, sib)
        sib_of = lambda p: _peer_position(p, 1)
        landed(0, sib, ALL, sib)
        landed(10, sib_of(xn), ALL, sib)
        landed(12, sib_of(yn), ALL, sib)
        landed(14, sib_of(dg), TOP, sib)
        landed(16, sib_of(dg), BOT, sib)
        for cp in started:
            cp.wait_send()
        for cp in local:
            cp.wait()

    return _call(body, [src], name="ag_w_in", in_specs=[ANY], out_specs=[ANY, ANY],
                 out_shape=(jax.ShapeDtypeStruct((D, INW), BF16), jax.ShapeDtypeStruct((N_DEV, D, LANES), BF16)),
                 scratch_shapes=[pltpu.SemaphoreType.DMA((20,)), pltpu.SemaphoreType.DMA((20,))])


def _patch_mid(full, mid, a):
    D = full.shape[0]

    def body(full_ref, e_ref, o_ref, out_ref):
        out_ref[...] = e_ref[...] + o_ref[...]

    return _call(body, [full, mid, mid], name="patch_mid", grid=(N_DEV // 2,),
                 out_shape=jax.ShapeDtypeStruct(full.shape, full.dtype),
                 in_specs=[ANY, pl.BlockSpec((None, D, LANES), lambda j: (2 * j, 0, 0)),
                           pl.BlockSpec((None, D, LANES), lambda j: (2 * j + 1, 0, 0))],
                 out_specs=pl.BlockSpec((D, LANES), lambda j: (0, (2 * a + 1) * j + a)), aliases={0: 0})


MM_RESIDENT = 2048


def _mm(a, b, mode, out_dtype, name, b_off=0, n=None, comm=None, extras=(), epi=None, tn=None, after=(), b_order=None):
    if mode == "nn":
        (M, K), (K2, N) = a.shape, b.shape
    elif mode == "nt":
        (M, K), (N, K2) = a.shape, b.shape
    else:
        (K, M), (K2, N) = a.shape, b.shape
    assert K == K2, (a.shape, b.shape, mode)
    if n is not None:
        N = n
    single = not isinstance(out_dtype, (tuple, list))
    out_dtypes = (out_dtype,) if single else tuple(out_dtype)
    if epi is None:
        epi = lambda r: (r,)
    tk = K if K <= MM_RESIDENT else (MM_RESIDENT if K % MM_RESIDENT == 0 else _tile(K, 512, LANES))
    nk = K // tk
    if M > MM_RESIDENT and mode == "tn" and N <= MM_RESIDENT and not b_off:
        tm, tn = _tile(M, 512, LANES), N
    elif nk > 1:
        tm, tn = _tile(M, 1024, LANES), _tile(N, tn or 1024, LANES)
    else:
        tm = _tile(M, MM_RESIDENT, LANES)
        tn = _tile(math.gcd(N, b_off) if b_off else N, tn or 512, LANES)
    jb = b_off // tn
    dn = {"nn": NN, "nt": NT, "tn": TN}[mode]
    ne, no = len(extras), len(out_dtypes)

    def body(a_ref, b_ref, *rest):
        e_refs, o_refs = rest[:ne], rest[ne:ne + no]

        def finish(r):
            for o_ref, v in zip(o_refs, epi(r, *[e[...] for e in e_refs])):
                o_ref[...] = v.astype(o_ref.dtype)

        if nk == 1:
            finish(_bdot(a_ref[...], b_ref[...], dn))
            return
        acc_ref = rest[ne + no]
        k = pl.program_id(2)

        @pl.when(k == 0)
        def _():
            acc_ref[...] = _bdot(a_ref[...], b_ref[...], dn)

        @pl.when(jnp.logical_and(k > 0, k < nk - 1))
        def _():
            acc_ref[...] += _bdot(a_ref[...], b_ref[...], dn)

        @pl.when(k == nk - 1)
        def _():
            finish(acc_ref[...] + _bdot(a_ref[...], b_ref[...], dn))

    a_spec = pl.BlockSpec((tk, tm), lambda i, j, k: (k, i)) if mode == "tn" else pl.BlockSpec((tm, tk), lambda i, j, k: (i, k))
    col = (lambda j: j + jb) if b_order is None else functools.partial(b_order, tn)
    b_spec = pl.BlockSpec((tn, tk), lambda i, j, k: (j, k)) if mode == "nt" else pl.BlockSpec((tk, tn), lambda i, j, k: (k, col(j)))
    o_spec = pl.BlockSpec((tm, tn), lambda i, j, k: (i, j))
    res = _call(body, [a, b] + list(extras), name=name, grid=(M // tm, N // tn, nk),
                out_shape=tuple(jax.ShapeDtypeStruct((M, N), dt) for dt in out_dtypes),
                in_specs=[a_spec, b_spec] + [o_spec] * ne, out_specs=[o_spec] * no,
                scratch_shapes=[pltpu.VMEM((tm, tn), F32)] if nk > 1 else [], comm=comm, after=after)
    return res[0] if single else res


def _rowwise(fn, row_ins, bcast_ins, row_outs, acc_outs, name, rt=256, comm=None):
    L = row_ins[0][0].shape[-2]
    rt = _tile(L, rt, 16)
    nr, nb, no = len(row_ins), len(bcast_ins), len(row_outs)

    def body(*refs):
        i = pl.program_id(0)
        vals = [r[...] for r in refs[:nr + nb]]
        outs, accs = fn(*vals)
        for r, v in zip(refs[nr + nb:nr + nb + no], outs):
            r[...] = v.astype(r.dtype)
        acc_refs = refs[nr + nb + no:]

        @pl.when(i == 0)
        def _():
            for r in acc_refs:
                r[...] = jnp.zeros_like(r)

        for r, v in zip(acc_refs, accs):
            r[...] += v

    in_specs = []
    for spec in row_ins:
        w, cb = spec[1], spec[2]
        if len(spec) == 4:
            in_specs.append(pl.BlockSpec((None, rt, w), functools.partial(lambda i, cb, ld: (ld, i, cb), cb=cb, ld=spec[3])))
        else:
            in_specs.append(pl.BlockSpec((rt, w), functools.partial(lambda i, cb: (i, cb), cb=cb)))
    in_specs += [pl.BlockSpec(b.shape, lambda i: (0, 0)) for b in bcast_ins]
    out_specs = [pl.BlockSpec((rt, w), lambda i: (i, 0)) for w, _ in row_outs]
    out_specs += [pl.BlockSpec(s, lambda i: (0, 0)) for s in acc_outs]
    out_shape = [jax.ShapeDtypeStruct((L, w), dt) for w, dt in row_outs] + [jax.ShapeDtypeStruct(s, F32) for s in acc_outs]
    return _call(body, [s[0] for s in row_ins] + list(bcast_ins), name=name, grid=(L // rt,), out_shape=tuple(out_shape),
                 in_specs=in_specs, out_specs=out_specs, comm=comm)


def _whole(fn, ins, out_shapes, name):
    def body(*refs):
        outs = fn(*[r[...] for r in refs[:len(ins)]])
        for r, v in zip(refs[len(ins):], outs):
            r[...] = v.astype(r.dtype)

    return _call(body, list(ins), name=name, out_shape=tuple(jax.ShapeDtypeStruct(s, dt) for s, dt in out_shapes))


def _silu(x):
    return x * jax.nn.sigmoid(x)


def _rms(x, g):
    return (x * lax.rsqrt(jnp.mean(x * x, axis=-1, keepdims=True) + EPS)) * g


def _modnorm(x, g, shift, scale):
    return _rms(x, g) * (1.0 + scale) + shift


def _adamw(w, g, m, v):
    m = ADAM_B1 * m + (1.0 - ADAM_B1) * g
    v = ADAM_B2 * v + (1.0 - ADAM_B2) * jnp.square(g)
    m_hat = m / (1.0 - ADAM_B1 ** ADAM_STEP)
    v_hat = v / (1.0 - ADAM_B2 ** ADAM_STEP)
    delta = -ADAM_LR * (m_hat / (jnp.sqrt(v_hat) + ADAM_EPS) + ADAM_WD * w)
    return delta, m, v


def _lower_bound(lg):
    e = jnp.exp(lg - jnp.max(lg, axis=0, keepdims=True))
    return e[0:1] / jnp.sum(e, axis=0, keepdims=True)


def _hg_stages(hq_l, hf_l, hi_l, lb):
    C = hq_l[0].shape[0]
    row = lax.broadcasted_iota(jnp.int32, (C, C), 0)
    col = lax.broadcasted_iota(jnp.int32, (C, C), 1)
    tri = row >= col
    trif = tri.astype(F32)
    f_l = [lb + (1.0 - lb) * jax.nn.sigmoid(hf) for hf in hf_l]
    b_l = [_dot(trif, jnp.log(f), NN, precision=HIGHEST) for f in f_l]
    q_l = [_silu(hq) for hq in hq_l]
    m_l = [b[C // 2 - 1:C // 2] for b in b_l]
    bl_l = [b[C - 1:C] for b in b_l]
    sc_l = [jnp.where(tri, _bdot(q * jnp.exp(b - m), (1.0 - f) * jnp.exp(m - b), NT), 0.0)
            for q, f, b, m in zip(q_l, f_l, b_l, m_l)]
    o1_l = [_bdot(sc, hi, NN) for sc, hi in zip(sc_l, hi_l)]
    u_l = [_bdot(hi, (1.0 - f) * jnp.exp(bl - b), TN) for hi, f, b, bl in zip(hi_l, f_l, b_l, bl_l)]
    qb_l = [q * jnp.exp(b) for q, b in zip(q_l, b_l)]
    dec_l = [jnp.exp(bl) for bl in bl_l]
    return list(zip(o1_l, u_l, qb_l, dec_l))


def _hg_out(o, hgate, gout):
    return _rms(o, gout) * _silu(hgate)


HG_STAGE = 8
HG_GROUP = 32


def _hgrn_fwd(p4, lb_logits, gout, H, comm=None):
    L = p4.shape[0]
    C = HG_CHUNK
    GR = _tile(L // C, HG_GROUP, 1)
    T = GR * C
    N = L // T

    def body(hq_ref, hf_ref, hi_ref, hg_ref, lg_ref, gout_ref, o_ref, s_ref, st_ref):
        @pl.when(pl.program_id(1) == 0)
        def _():
            st_ref[...] = jnp.zeros_like(st_ref)

        lb = _lower_bound(lg_ref[...])
        st = st_ref[...]
        for c0 in range(0, GR, HG_STAGE):
            rows_l = [pl.ds(ci * C, C) for ci in range(c0, min(c0 + HG_STAGE, GR))]
            parts = _hg_stages([hq_ref[r, :] for r in rows_l], [hf_ref[r, :] for r in rows_l],
                               [hi_ref[r, :] for r in rows_l], lb)
            for ci, rows, (o1, u, qb, dec) in zip(range(c0, GR), rows_l, parts):
                s_ref[0, ci] = st
                o = o1 + _bdot(qb, st, NT)
                st = st * dec + u
                o_ref[rows, :] = _hg_out(o, hg_ref[rows, :], gout_ref[...]).astype(o_ref.dtype)
        st_ref[...] = st

    blk = lambda s: pl.BlockSpec((T, HG_DK), functools.partial(lambda h, n, s: (n, s * H + h), s=s))
    return _call(
        body, [p4, p4, p4, p4, lb_logits, gout], name="hgrn_fwd", grid=(H, N),
        out_shape=(jax.ShapeDtypeStruct((L, H * HG_DK), BF16), jax.ShapeDtypeStruct((H, N * GR, HG_DK, HG_DK), F32)),
        in_specs=[blk(0), blk(1), blk(2), blk(3), pl.BlockSpec((2, HG_DK), lambda h, n: (0, h)),
                  pl.BlockSpec((1, HG_DK), lambda h, n: (0, 0))],
        out_specs=(pl.BlockSpec((T, HG_DK), lambda h, n: (n, h)),
                   pl.BlockSpec((1, GR, HG_DK, HG_DK), lambda h, n: (h, n, 0, 0))),
        scratch_shapes=[pltpu.VMEM((HG_DK, HG_DK), F32)], comm=comm)


def _hgrn_bwd(p4, lb_logits, gout, s_all, d_out, H, comm=None):
    L = p4.shape[0]
    C = HG_CHUNK
    GR = _tile(L // C, HG_GROUP, 1)
    T = GR * C
    N = L // T

    def body(hq_ref, hf_ref, hi_ref, hg_ref, lg_ref, gout_ref, s_ref, do_ref,
             dq_ref, df_ref, di_ref, dg_ref, dlb_ref, dgo_ref, dst_ref):
        @pl.when(pl.program_id(1) == 0)
        def _():
            dst_ref[...] = jnp.zeros_like(dst_ref)
            dlb_ref[...] = jnp.zeros_like(dlb_ref)

        @pl.when(jnp.logical_and(pl.program_id(0) == 0, pl.program_id(1) == 0))
        def _():
            dgo_ref[...] = jnp.zeros_like(dgo_ref)

        lb = _lower_bound(lg_ref[...])
        dst = dst_ref[...]
        d_lb = jnp.zeros((1, HG_DK), F32)
        d_go = jnp.zeros((1, HG_DK), F32)
        for c0 in reversed(range(0, GR, HG_STAGE)):
            dst, d_lb_c, d_go_c = chunks_bwd(list(range(c0, min(c0 + HG_STAGE, GR))), lb, dst, hq_ref, hf_ref, hi_ref,
                                             hg_ref, gout_ref, s_ref, do_ref, dq_ref, df_ref, di_ref, dg_ref)
            d_lb += d_lb_c
            d_go += d_go_c
        dst_ref[...] = dst
        dlb_ref[...] += d_lb
        dgo_ref[...] += d_go

    def chunks_bwd(idx, lb, dst, hq_ref, hf_ref, hi_ref, hg_ref, gout_ref, s_ref, do_ref, dq_ref, df_ref, di_ref, dg_ref):
        n = len(idx)
        rows_l = [pl.ds(ci * C, C) for ci in idx]
        hq_l, hf_l, hi_l = ([r[rows, :] for rows in rows_l] for r in (hq_ref, hf_ref, hi_ref))
        st_l = [s_ref[0, ci] for ci in idx]
        row = lax.broadcasted_iota(jnp.int32, (C, C), 0)
        col = lax.broadcasted_iota(jnp.int32, (C, C), 1)
        tri = row >= col
        trif = tri.astype(F32)
        every = lambda fn, *ls: [fn(*a) for a in zip(*ls)]
        sg_l = every(jax.nn.sigmoid, hf_l)
        f_l = every(lambda sg: lb + (1.0 - lb) * sg, sg_l)
        b_l = every(lambda f: _dot(trif, jnp.log(f), NN, precision=HIGHEST), f_l)
        q_l = every(_silu, hq_l)
        m_l = every(lambda b: b[C // 2 - 1:C // 2], b_l)
        bl_l = every(lambda b: b[C - 1:C], b_l)
        e_qm_l = every(lambda b, m: jnp.exp(b - m), b_l, m_l)
        e_km_l = every(lambda b, m: jnp.exp(m - b), b_l, m_l)
        e_kl_l = every(lambda b, bl: jnp.exp(bl - b), b_l, bl_l)
        e_q_l = every(jnp.exp, b_l)
        dec_l = every(jnp.exp, bl_l)
        qe_l = every(lambda q, e: q * e, q_l, e_qm_l)
        ke_l = every(lambda f, e: (1.0 - f) * e, f_l, e_km_l)
        kd_l = every(lambda f, e: (1.0 - f) * e, f_l, e_kl_l)
        qb_l = every(lambda q, e: q * e, q_l, e_q_l)
        sc_l = every(lambda qe, ke: jnp.where(tri, _bdot(qe, ke, NT), 0.0), qe_l, ke_l)
        o_l = every(lambda sc, hi, qb, st: _bdot(sc, hi, NN) + _bdot(qb, st, NT), sc_l, hi_l, qb_l, st_l)
        vj_l = every(lambda o, rows: jax.vjp(_hg_out, o, hg_ref[rows, :], gout_ref[...])[1](do_ref[rows, :]), o_l, rows_l)
        do_l = [v[0] for v in vj_l]
        dsc_l = every(lambda do, hi: jnp.where(tri, _bdot(do, hi, NT), 0.0), do_l, hi_l)
        dv1_l = every(lambda sc, do: _bdot(sc, do, TN), sc_l, do_l)
        dqe_l = every(lambda dsc, ke: _bdot(dsc, ke, NN), dsc_l, ke_l)
        dke_l = every(lambda dsc, qe: _bdot(dsc, qe, TN), dsc_l, qe_l)
        dqb_l = every(lambda do, st: _bdot(do, st, NN), do_l, st_l)
        own_l = every(lambda do, qb: _bdot(do, qb, TN), do_l, qb_l)
        dst_next_l = [None] * n
        for j in reversed(range(n)):
            dst_next_l[j] = dst
            dst = own_l[j] + dst * dec_l[j]
        dv_l = every(lambda dv1, kd, dn: dv1 + _bdot(kd, dn, NT), dv1_l, kd_l, dst_next_l)
        dkd_l = every(lambda hi, dn: _bdot(hi, dn, NN), hi_l, dst_next_l)
        ddec_l = every(lambda dn, st: jnp.sum(dn * st, axis=0, keepdims=True), dst_next_l, st_l)
        rowi = lax.broadcasted_iota(jnp.int32, (C, HG_DK), 0)
        tq_l = every(lambda a, b_: a * b_, dqe_l, qe_l)
        tk_l = every(lambda a, b_: a * b_, dke_l, ke_l)
        td_l = every(lambda a, b_: a * b_, dkd_l, kd_l)
        tb_l = every(lambda a, b_: a * b_, dqb_l, qb_l)
        db_l = every(lambda tq, tk, td, tb, ddec, dec: tq - tk - td + tb
                     + jnp.where(rowi == C // 2 - 1, jnp.sum(tk - tq, axis=0, keepdims=True), 0.0)
                     + jnp.where(rowi == C - 1, jnp.sum(td, axis=0, keepdims=True) + ddec * dec, 0.0),
                     tq_l, tk_l, td_l, tb_l, ddec_l, dec_l)
        dlf_l = every(lambda db: _dot(trif, db, TN, precision=HIGHEST), db_l)
        dk_l = every(lambda dke, e1, dkd, e2: dke * e1 + dkd * e2, dke_l, e_km_l, dkd_l, e_kl_l)
        df_l = every(lambda dlf, f, dk: dlf / f - dk, dlf_l, f_l, dk_l)
        d_lb = jnp.zeros((1, HG_DK), F32)
        d_go = jnp.zeros((1, HG_DK), F32)
        for j, rows in enumerate(rows_l):
            sg, hq = sg_l[j], hq_l[j]
            df_ref[rows, :] = (df_l[j] * (1.0 - lb) * sg * (1.0 - sg)).astype(df_ref.dtype)
            sq = jax.nn.sigmoid(hq)
            dq = dqe_l[j] * e_qm_l[j] + dqb_l[j] * e_q_l[j]
            dq_ref[rows, :] = (dq * (sq * (1.0 + hq * (1.0 - sq)))).astype(dq_ref.dtype)
            di_ref[rows, :] = dv_l[j].astype(di_ref.dtype)
            dg_ref[rows, :] = vj_l[j][1].astype(dg_ref.dtype)
            d_lb += jnp.sum(df_l[j] * (1.0 - sg), axis=0, keepdims=True)
            d_go += vj_l[j][2]
        return dst, d_lb, d_go

    blk = lambda s: pl.BlockSpec((T, HG_DK), functools.partial(lambda h, n, s: (N - 1 - n, s * H + h), s=s))
    oblk = pl.BlockSpec((T, HG_DK), lambda h, n: (N - 1 - n, h))
    vec = pl.BlockSpec((1, HG_DK), lambda h, n: (0, h))
    W = H * HG_DK
    return _call(
        body, [p4, p4, p4, p4, lb_logits, gout, s_all, d_out], name="hgrn_bwd", grid=(H, N),
        out_shape=tuple([jax.ShapeDtypeStruct((L, W), BF16)] * 4 + [jax.ShapeDtypeStruct((1, W), F32), jax.ShapeDtypeStruct((1, HG_DK), F32)]),
        in_specs=[blk(0), blk(1), blk(2), blk(3), pl.BlockSpec((2, HG_DK), lambda h, n: (0, h)),
                  pl.BlockSpec((1, HG_DK), lambda h, n: (0, 0)),
                  pl.BlockSpec((1, GR, HG_DK, HG_DK), lambda h, n: (h, N - 1 - n, 0, 0)), oblk],
        out_specs=(oblk, oblk, oblk, oblk, vec, pl.BlockSpec((1, HG_DK), lambda h, n: (0, 0))),
        scratch_shapes=[pltpu.VMEM((HG_DK, HG_DK), F32)], comm=comm)


def _bucket_ids():
    i = jnp.arange(AT_BLOCK, dtype=jnp.int32)[:, None]
    j = jnp.arange(2 * AT_BLOCK, dtype=jnp.int32)[None, :]
    n = jnp.maximum(i - j + AT_BLOCK, 0)
    nf = jnp.maximum(n, 1).astype(F32)
    large = MAX_EXACT + (jnp.log(nf / MAX_EXACT) / math.log(MAX_DISTANCE / MAX_EXACT) * (N_BUCKETS - MAX_EXACT)).astype(jnp.int32)
    large = jnp.minimum(large, N_BUCKETS - 1)
    return jnp.where(n < MAX_EXACT, n, large).reshape(1, -1)


def _onehot(bucket):
    ids = lax.broadcasted_iota(jnp.int32, (N_BUCKETS, bucket.shape[1]), 0)
    return (ids == bucket).astype(F32)


AT_PAIR = 2


def _attn_probs(qn_l, kn_l, bias_g, sink, first, scale):
    rows = qn_l[0].shape[0]
    i = jnp.bitwise_and(lax.broadcasted_iota(jnp.int32, (rows, AT_BLOCK), 0), AT_BLOCK - 1)
    j = lax.broadcasted_iota(jnp.int32, (rows, AT_BLOCK), 1)
    n = len(qn_l)
    lp_l = [_bdot(qn_l[s], kn_l[s], NT) * scale + bias_g[:, :AT_BLOCK] for s in range(n)]
    lc_l = [_bdot(qn_l[s], kn_l[s + 1], NT) * scale + bias_g[:, AT_BLOCK:] for s in range(n)]
    seen = [jnp.logical_and(j > i, jnp.logical_not(first))] + [j > i] * (n - 1)
    lp_l = [jnp.where(seen[s], lp_l[s], NEG_INF) for s in range(n)]
    lc_l = [jnp.where(j <= i, lc, NEG_INF) for lc in lc_l]
    m_l = [jnp.maximum(jnp.maximum(jnp.max(lp, axis=-1, keepdims=True), jnp.max(lc, axis=-1, keepdims=True)), sink)
           for lp, lc in zip(lp_l, lc_l)]
    pp_l = [jnp.exp(lp - m) for lp, m in zip(lp_l, m_l)]
    pc_l = [jnp.exp(lc - m) for lc, m in zip(lc_l, m_l)]
    ps_l = [jnp.exp(sink - m) for m in m_l]
    den_l = [jnp.sum(pp, axis=-1, keepdims=True) + jnp.sum(pc, axis=-1, keepdims=True) + ps
             for pp, pc, ps in zip(pp_l, pc_l, ps_l)]
    return [(pp / den, pc / den, ps / den) for pp, pc, ps, den in zip(pp_l, pc_l, ps_l, den_l)]


def _sink_rows(sk_ref, G):
    head = lax.broadcasted_iota(jnp.int32, (G * AT_BLOCK, 1), 0) // AT_BLOCK
    sink = jnp.zeros((G * AT_BLOCK, 1), F32)
    for g in range(G):
        sink = jnp.where(head == g, sk_ref[0, g:g + 1, :], sink)
    return sink


def _attn_fwd(q_t, kp, vp, qg, kg, sinks, bias, KVH, comm=None):
    AH, L, DH = q_t.shape
    G = AH // KVH
    NB = L // AT_BLOCK
    scale = DH ** -0.5

    P, B = AT_PAIR, AT_BLOCK
    assert NB % P == 0

    def body(q_ref, *rest):
        k_refs, v_refs = rest[:P + 1], rest[P + 1:2 * P + 2]
        qg_ref, kg_ref, sk_ref, b_ref, o_ref = rest[2 * P + 2:]
        first = pl.program_id(1) == 0
        kn_l = [_rms(r[0], kg_ref[...]) for r in k_refs]
        qn_l = [_rms(q_ref[:, pl.ds(s * B, B), :].reshape(G * B, DH), qg_ref[...]) for s in range(P)]
        probs = _attn_probs(qn_l, kn_l, b_ref[...].reshape(G * B, 2 * B), _sink_rows(sk_ref, G), first, scale)
        o_l = [_bdot(pp, v_refs[s][0], NN) + _bdot(pc, v_refs[s + 1][0], NN) for s, (pp, pc, _) in enumerate(probs)]
        for s, o in enumerate(o_l):
            o_ref[:, pl.ds(s * B, B), :] = o.reshape(G, B, DH).astype(o_ref.dtype)

    kblk = lambda off: pl.BlockSpec((1, B, DH), functools.partial(lambda h, m, off: (h, jnp.maximum(P * m + off - 1, 0), 0), off=off))
    kspecs = [kblk(off) for off in range(P + 1)]
    return _call(
        body, [q_t] + [kp] * (P + 1) + [vp] * (P + 1) + [qg, kg, sinks, bias], name="attn_fwd", grid=(KVH, NB // P),
        out_shape=jax.ShapeDtypeStruct((AH, L, DH), BF16),
        in_specs=[pl.BlockSpec((G, P * B, DH), lambda h, m: (h, m, 0))] + kspecs + kspecs
        + [pl.BlockSpec((1, DH), lambda h, m: (0, 0)), pl.BlockSpec((1, DH), lambda h, m: (0, 0)),
           pl.BlockSpec((1, G, 1), lambda h, m: (h, 0, 0)), pl.BlockSpec((G, B, 2 * B), lambda h, m: (h, 0, 0))],
        out_specs=pl.BlockSpec((G, P * B, DH), lambda h, m: (h, m, 0)), comm=comm)


def _attn_bwd(q_t, kp, vp, qg, kg, sinks, bias, do_t, KVH, comm=None):
    AH, L, DH = q_t.shape
    G = AH // KVH
    NB = L // AT_BLOCK
    B = AT_BLOCK
    scale = DH ** -0.5

    P = AT_PAIR
    assert NB % P == 0

    def body(q_ref, *rest):
        k_refs, v_refs = rest[:P + 1], rest[P + 1:2 * P + 2]
        qg_ref, kg_ref, sk_ref, b_ref, do_ref, dq_ref, dk_ref, dv_ref, dqg_ref, dkg_ref, dsk_ref, db_ref = rest[2 * P + 2:]
        m = pl.program_id(1)
        first = m == 0

        @pl.when(first)
        def _():
            for r in (dk_ref, dv_ref, dsk_ref, db_ref):
                r[...] = jnp.zeros_like(r)

        @pl.when(jnp.logical_and(first, pl.program_id(0) == 0))
        def _():
            dqg_ref[...] = jnp.zeros_like(dqg_ref)
            dkg_ref[...] = jnp.zeros_like(dkg_ref)

        kgv, qgv = kg_ref[...], qg_ref[...]
        k_fw = [jax.vjp(_rms, r[0], kgv) for r in k_refs]
        kn_l = [f[0] for f in k_fw]
        q_fw = [jax.vjp(_rms, q_ref[:, pl.ds(s * B, B), :].reshape(G * B, DH), qgv) for s in range(P)]
        qn_l = [f[0] for f in q_fw]
        probs = _attn_probs(qn_l, kn_l, b_ref[...].reshape(G * B, 2 * B), _sink_rows(sk_ref, G), first, scale)
        pp_l, pc_l, ps_l = ([p[t] for p in probs] for t in range(3))
        do_l = [do_ref[:, pl.ds(s * B, B), :].reshape(G * B, DH) for s in range(P)]
        dvp_l = [_bdot(pp, do, TN) for pp, do in zip(pp_l, do_l)]
        dvc_l = [_bdot(pc, do, TN) for pc, do in zip(pc_l, do_l)]
        dpp_l = [_bdot(do_l[s], v_refs[s][0], NT) for s in range(P)]
        dpc_l = [_bdot(do_l[s], v_refs[s + 1][0], NT) for s in range(P)]
        dsum_l = [jnp.sum(dpp * pp, axis=-1, keepdims=True) + jnp.sum(dpc * pc, axis=-1, keepdims=True)
                  for dpp, pp, dpc, pc in zip(dpp_l, pp_l, dpc_l, pc_l)]
        dlp_l = [pp * (dpp - ds) for pp, dpp, ds in zip(pp_l, dpp_l, dsum_l)]
        dlc_l = [pc * (dpc - ds) for pc, dpc, ds in zip(pc_l, dpc_l, dsum_l)]
        dsk_ref[0] += sum(jnp.sum((-ps * ds).reshape(G, B, 1), axis=1) for ps, ds in zip(ps_l, dsum_l))
        db_ref[:, :, :B] += sum(dlp_l).reshape(G, B, B)
        db_ref[:, :, B:] += sum(dlc_l).reshape(G, B, B)
        dlp_l, dlc_l = [d * scale for d in dlp_l], [d * scale for d in dlc_l]
        dqn_l = [_bdot(dlp_l[s], kn_l[s], NN) + _bdot(dlc_l[s], kn_l[s + 1], NN) for s in range(P)]
        dq_l = [q_fw[s][1](dqn_l[s]) for s in range(P)]
        for s in range(P):
            dq_ref[:, pl.ds(s * B, B), :] = dq_l[s][0].reshape(G, B, DH).astype(dq_ref.dtype)
        dkn_l = [jnp.zeros((B, DH), F32)] * (P + 1)
        dvk_l = [jnp.zeros((B, DH), F32)] * (P + 1)
        for s in range(P):
            dkn_l[s] = dkn_l[s] + _bdot(dlp_l[s], qn_l[s], TN)
            dkn_l[s + 1] = dkn_l[s + 1] + _bdot(dlc_l[s], qn_l[s], TN)
            dvk_l[s] = dvk_l[s] + dvp_l[s]
            dvk_l[s + 1] = dvk_l[s + 1] + dvc_l[s]
        dk_l = [k_fw[t][1](dkn_l[t]) for t in range(P + 1)]
        for t in range(P + 1):
            r = pl.multiple_of(jnp.maximum(P * m + t - 1, 0) * B, B)
            dk_ref[0, pl.ds(r, B), :] += dk_l[t][0]
            dv_ref[0, pl.ds(r, B), :] += dvk_l[t]
        dqg_ref[...] += sum(d[1] for d in dq_l)
        dkg_ref[...] += sum(d[1] for d in dk_l)

    kblk = lambda off: pl.BlockSpec((1, B, DH), functools.partial(lambda h, m, off: (h, jnp.maximum(P * m + off - 1, 0), 0), off=off))
    kspecs = [kblk(off) for off in range(P + 1)]
    qblk = pl.BlockSpec((G, P * B, DH), lambda h, m: (h, m, 0))
    accblk = pl.BlockSpec((1, L, DH), lambda h, m: (h, 0, 0))
    vecblk = pl.BlockSpec((1, DH), lambda h, m: (0, 0))
    return _call(
        body, [q_t] + [kp] * (P + 1) + [vp] * (P + 1) + [qg, kg, sinks, bias, do_t], name="attn_bwd", grid=(KVH, NB // P),
        out_shape=(jax.ShapeDtypeStruct((AH, L, DH), BF16), jax.ShapeDtypeStruct((KVH, L, DH), F32),
                   jax.ShapeDtypeStruct((KVH, L, DH), F32), jax.ShapeDtypeStruct((1, DH), F32),
                   jax.ShapeDtypeStruct((1, DH), F32), jax.ShapeDtypeStruct((KVH, G, 1), F32),
                   jax.ShapeDtypeStruct((AH, B, 2 * B), F32)),
        in_specs=[qblk] + kspecs + kspecs
        + [pl.BlockSpec((1, DH), lambda h, m: (0, 0)), pl.BlockSpec((1, DH), lambda h, m: (0, 0)),
           pl.BlockSpec((1, G, 1), lambda h, m: (h, 0, 0)), pl.BlockSpec((G, B, 2 * B), lambda h, m: (h, 0, 0)), qblk],
        out_specs=(qblk, accblk, accblk, vecblk, vecblk, pl.BlockSpec((1, G, 1), lambda h, m: (h, 0, 0)),
                   pl.BlockSpec((G, B, 2 * B), lambda h, m: (h, 0, 0))), comm=comm)


def _heads_first(t, nh):
    L = t.shape[0]
    return jnp.transpose(t.reshape(L, nh, t.shape[1] // nh), (1, 0, 2))


def _heads_last(t):
    nh, L, dh = t.shape
    return jnp.transpose(t, (1, 0, 2)).reshape(L, nh * dh)


def _softmax0(lg):
    e = jnp.exp(lg - jnp.max(lg, axis=0, keepdims=True))
    return e[0:1] / jnp.sum(e, axis=0, keepdims=True)


def _ada_update_call(fn, c_all, d_cols, w, m, v, rt):
    D, n = w.shape

    def body(c_ref, d_ref, w_ref, m_ref, v_ref, g_out, dl_out, m_out, v_out):
        outs, _ = fn(c_ref[...], d_ref[...], w_ref[...], m_ref[...], v_ref[...])
        for r, val in zip((g_out, dl_out, m_out, v_out), outs):
            r[...] = val

    wblk = pl.BlockSpec((rt, n), lambda i: (i, 0))
    return _call(
        body, [c_all, d_cols, w, m, v], name="update_ada", grid=(D // rt,), out_shape=tuple([jax.ShapeDtypeStruct((D, n), F32)] * 4),
        in_specs=[pl.BlockSpec((N_DEV, rt), lambda i: (0, i)), pl.BlockSpec((N_DEV, n), lambda i: (0, 0)), wblk, wblk, wblk],
        out_specs=(wblk, wblk, wblk, wblk))


def kernel(x, c, w_ada, b_ada, norm1_g, norm2_g, w_in, hg_lb_logits, hg_out_norm_g, q_norm_g, k_norm_g, attn_sinks, rel_bias_table, w_branch_hg, w_branch_attn, w_out, w_ff1, w_ff2, loss_target, m_w_ada, m_b_ada, m_norm1_g, m_norm2_g, m_w_in, m_hg_lb_logits, m_hg_out_norm_g, m_q_norm_g, m_k_norm_g, m_attn_sinks, m_rel_bias_table, m_w_branch_hg, m_w_branch_attn, m_w_out, m_w_ff1, m_w_ff2, v_w_ada, v_b_ada, v_norm1_g, v_norm2_g, v_w_in, v_hg_lb_logits, v_hg_out_norm_g, v_q_norm_g, v_k_norm_g, v_attn_sinks, v_rel_bias_table, v_w_branch_hg, v_w_branch_attn, v_w_out, v_w_ff1, v_w_ff2):
    cc = lax.axis_index("c")
    me = 4 * lax.axis_index("x") + 2 * lax.axis_index("y") + cc
    x2 = x[0]
    tgt = loss_target[0]
    L, D = x2.shape
    HGW = hg_lb_logits.shape[1]
    H = HGW // HG_DK
    AH = attn_sinks.shape[1]
    DH = q_norm_g.shape[1]
    ATW = AH * DH
    BW = w_in.shape[2]
    INW = BW * N_DEV
    A = BW // LANES
    assert BW == LANES * A + LANES // 2
    KVW = (INW - 4 * HGW - ATW - 2 * D) // 2
    KVH = KVW // DH
    G = AH // KVH
    ADA_N = w_ada.shape[2]
    PAIR = 2 * A + 1

    c_all = _gather_small(c, me, "gather_c")[:, 0, :]
    b_cols = lax.dynamic_slice(b_ada, (0, me * ADA_N), (1, ADA_N))
    (ada_cols,) = _whole(lambda cv, w, b: (_bdot(_silu(cv), w, NN) + b,), [c_all, w_ada[0], b_cols],
                         [((N_DEV, ADA_N), F32)], "ada_fwd")
    ada_all = _gather_small(ada_cols, me, "gather_ada")
    ada_row = lax.dynamic_slice(ada_all, (0, me, 0), (N_DEV, 1, ADA_N)).reshape(1, 6 * D)

    w_in_b = w_in[0].astype(BF16)
    src_in = jnp.where(cc == 0, jnp.pad(w_in_b, ((0, 0), (0, LANES // 2))), jnp.pad(w_in_b, ((0, 0), (LANES // 2, 0))))
    (src_in,) = _behind([src_in], [ada_row])
    shift1, scale1, gate1, shift2, scale2, gate2 = [ada_row[:, i * D:(i + 1) * D] for i in range(6)]
    w_in_gapped, w_in_mid = _ag_w_in(src_in, A, D, INW)
    w_in_full = _patch_mid(w_in_gapped, w_in_mid, A)

    wnames = ("bhg", "bat", "out", "ff1", "ff2")
    small = ("bhg", "bat", "out")
    waxis = dict(zip(wnames, (1, 1, 0, 1, 0)))
    wsrc = dict(zip(wnames, (w_branch_hg, w_branch_attn, w_out, w_ff1, w_ff2)))
    wblk = {k: wsrc[k][0].astype(BF16) for k in wnames}
    wf = {}

    (h,) = _rowwise(lambda xv, g, sh, sc: ((_modnorm(xv, g, sh, sc),), ()), [(x2, D, 0)], [norm1_g, shift1, scale1],
                    [(D, BF16)], [], "norm1")
    o4, oa = 4 * HGW, 4 * HGW + ATW + 2 * KVW
    r1, r2 = wblk["ff1"].shape[0], wblk["ff2"].shape[0]
    assert o4 % D == 0

    def proj_order(tn_, j):
        t4, tg, ng = o4 // tn_, oa // tn_, (INW - oa) // tn_
        return jnp.where(j < t4, j, jnp.where(j < t4 + ng, j + (tg - t4), j - ng))

    cm = _Comm()
    hs = {k: _ag_ici(cm, wblk[k], waxis[k]) for k in small}
    hs["ff2"] = _ag_ici(cm, wblk["ff2"], waxis["ff2"], rows=(0, r2 // 4))
    proj = _mm(h, w_in_full, "nn", F32, "proj", comm=cm, b_order=proj_order)
    half = {k: cm.result(hs[k]) for k in hs}
    p4 = pg = proj
    GATE0 = o4 // D
    pa = proj[:, o4 + (INW - oa):]

    cm = _Comm()
    hs = {k: _ag_d2d(cm, half[k], waxis[k]) for k in small}
    hs["ff1"] = _ag_ici(cm, wblk["ff1"], waxis["ff1"], rows=(0, r1 // 2))
    o_hg, s_all = _hgrn_fwd(p4, hg_lb_logits, hg_out_norm_g, H, comm=cm)
    wf["bhg"], wf["bat"], wf["out"], half["ff1"] = (cm.result(hs[k]) for k in ("bhg", "bat", "out", "ff1"))

    bucket = _bucket_ids()
    (bias_flat,) = _whole(lambda tb, bk: (_dot(tb, _onehot(bk), TN, precision=HIGHEST),), [rel_bias_table, bucket],
                          [((AH, AT_BLOCK * 2 * AT_BLOCK), F32)], "bias_fwd")
    bias = bias_flat.reshape(AH, AT_BLOCK, 2 * AT_BLOCK)
    q_t = _heads_first(pa[:, :ATW], AH)
    kp = _heads_first(pa[:, ATW:ATW + KVW], KVH)
    vp = _heads_first(pa[:, ATW + KVW:], KVH)
    sinks3 = attn_sinks.reshape(KVH, G, 1)
    cm = _Comm()
    hs = {"ff1": _ag_ici(cm, wblk["ff1"], waxis["ff1"], rows=(r1 // 2, r1), into=half["ff1"])}
    o_at = _heads_last(_attn_fwd(q_t, kp, vp, q_norm_g, k_norm_g, sinks3, bias, KVH, comm=cm))
    half["ff1"] = cm.result(hs["ff1"])

    bh = _mm(o_hg, wf["bhg"], "nn", F32, "branch_hg")
    ba = _mm(o_at, wf["bat"], "nn", F32, "branch_at")

    def merge_fn(bhv, bav, ghg, gat):
        return jax.nn.sigmoid(ghg) * bhv + jax.nn.sigmoid(gat) * bav

    cm = _Comm()
    hs = {"ff1": _ag_d2d(cm, half["ff1"], waxis["ff1"]),
          "ff2": _ag_ici(cm, wblk["ff2"], waxis["ff2"], rows=(r2 // 4, 3 * r2 // 8), into=half["ff2"])}
    (merged,) = _rowwise(lambda *a: ((merge_fn(*a),), ()), [(bh, D, 0), (ba, D, 0), (pg, D, GATE0), (pg, D, GATE0 + 1)], [],
                         [(D, BF16)], [], "merge", comm=cm)
    wf["ff1"], half["ff2"] = cm.result(hs["ff1"]), cm.result(hs["ff2"])
    cm = _Comm()
    hs = {"ff2": _ag_ici(cm, wblk["ff2"], waxis["ff2"], rows=(3 * r2 // 8, r2 // 2), into=half["ff2"])}
    mo = _mm(merged, wf["out"], "nn", F32, "out_proj", comm=cm)
    half["ff2"] = cm.result(hs["ff2"])

    def resid1(xv, mov, g1, g2n, sh, sc):
        x1v = xv + g1 * mov
        return (x1v, _modnorm(x1v, g2n, sh, sc)), ()

    x1, h2 = _rowwise(resid1, [(x2, D, 0), (mo, D, 0)], [gate1, norm2_g, shift2, scale2], [(D, F32), (D, BF16)], [], "resid1")
    cm = _Comm()
    hs = {"ff2": _ag_ici(cm, wblk["ff2"], waxis["ff2"], rows=(r2 // 2, r2), into=half["ff2"])}
    u, act = _mm(h2, wf["ff1"], "nn", (F32, BF16), "ff1", comm=cm, epi=lambda r: (r, jnp.square(jnp.maximum(r, 0.0))))
    half["ff2"] = cm.result(hs["ff2"])
    cm = _Comm()
    hs = {"ff2": _ag_d2d(cm, half["ff2"], waxis["ff2"])}
    _call(lambda: None, [], name="ag_d2d_ff2", out_shape=(), comm=cm)
    wf["ff2"] = cm.result(hs["ff2"])
    ff = _mm(act, wf["ff2"], "nn", F32, "ff2")

    def loss_fn(x1v, ffv, tv, g2):
        e = x1v + g2 * ffv - tv
        dy = e * (1.0 / D)
        return (dy, dy * g2), (jnp.sum(e * e, axis=0, keepdims=True), jnp.sum(dy * ffv, axis=0, keepdims=True))

    dy, d_ff, sq_sum, d_gate2 = _rowwise(loss_fn, [(x1, D, 0), (ff, D, 0), (tgt, D, 0)], [gate2],
                                         [(D, F32), (D, BF16)], [(1, D), (1, D)], "loss")
    loss = lax.psum(jnp.sum(sq_sum) * (0.5 / D), ("x", "y", "c"))

    owner_base = jnp.stack([me ^ r for r in CHIP_RELS]).astype(jnp.int32)
    gw, recv1, part, recv2 = {}, {}, {}, {}
    gw["ff2"] = _mm(act, d_ff, "tn", BF16, "dw_ff2")
    cm = _Comm()
    hh = _rs_d2d(cm, gw["ff2"], waxis["ff2"])
    d_u = _mm(d_ff, wf["ff2"], "nt", BF16, "d_act", comm=cm, extras=[u], epi=lambda r, uv: (r * (2.0 * jnp.maximum(uv, 0.0)),))
    part["ff2"] = _rs_add(gw["ff2"], cm.result(hh), waxis["ff2"], owner_base, "rs_add_ff2")
    rows_ff2 = part["ff2"].shape[1]
    cm = _Comm()
    hh = _rs_ici(cm, part["ff2"], rows=(0, rows_ff2 // 2))
    gw["ff1"] = _mm(h2, d_u, "tn", BF16, "dw_ff1", comm=cm)
    cm2 = _Comm()
    hh2 = _rs_ici(cm2, part["ff2"], rows=(rows_ff2 // 2, rows_ff2), recv=cm.result(hh))
    hh1 = _rs_d2d(cm2, gw["ff1"], waxis["ff1"])
    d_h2 = _mm(d_u, wf["ff1"], "nt", F32, "d_h2", comm=cm2)
    recv2["ff2"] = cm2.result(hh2)
    part["ff1"] = _rs_add(gw["ff1"], cm2.result(hh1), waxis["ff1"], owner_base, "rs_add_ff1")

    def norm2_bwd(dh2v, x1v, dyv, mov, g2n, sh, sc, g1):
        _, vjp = jax.vjp(_modnorm, x1v, g2n, sh, sc)
        dx, dg, dsh, dsc = vjp(dh2v)
        dx1 = dyv + dx
        return (dx1, dx1 * g1), (dg, dsh, dsc, jnp.sum(dx1 * mov, axis=0, keepdims=True))

    d_x1, d_mo, d_g2n, d_shift2, d_scale2, d_gate1 = _rowwise(
        norm2_bwd, [(d_h2, D, 0), (x1, D, 0), (dy, D, 0), (mo, D, 0)], [norm2_g, shift2, scale2, gate1],
        [(D, F32), (D, BF16)], [(1, D)] * 4, "norm2_bwd")
    gw["out"] = _mm(merged, d_mo, "tn", BF16, "dw_out")
    d_merged = _mm(d_mo, wf["out"], "nt", F32, "d_merged")

    def merge_bwd(dmv, bhv, bav, ghg, gat):
        _, vjp = jax.vjp(merge_fn, bhv, bav, ghg, gat)
        return vjp(dmv), ()

    d_bh, d_ba, d_ghg, d_gat = _rowwise(merge_bwd, [(d_merged, D, 0), (bh, D, 0), (ba, D, 0), (pg, D, GATE0), (pg, D, GATE0 + 1)], [],
                                        [(D, BF16)] * 4, [], "merge_bwd")
    gw["bhg"] = _mm(o_hg, d_bh, "tn", BF16, "dw_bhg")
    gw["bat"] = _mm(o_at, d_ba, "tn", BF16, "dw_bat")
    d_ohg = _mm(d_bh, wf["bhg"], "nt", F32, "d_ohg")
    d_oat = _mm(d_ba, wf["bat"], "nt", BF16, "d_oat")
    rows_ff1 = part["ff1"].shape[1]
    cut_ff1 = 3 * rows_ff1 // 8
    cm = _Comm()
    hf1 = _rs_ici(cm, part["ff1"], rows=(0, cut_ff1))
    d_hq, d_hf, d_hi, d_hg, d_lb, d_gout_h = _hgrn_bwd(p4, hg_lb_logits, hg_out_norm_g, s_all, d_ohg, H, comm=cm)
    cm2 = _Comm()
    hf1 = _rs_ici(cm2, part["ff1"], rows=(cut_ff1, rows_ff1), recv=cm.result(hf1))
    hh = {k: _rs_d2d(cm2, gw[k], waxis[k]) for k in small}
    dq_t, dkp, dvp, d_qg, d_kg, d_sk, d_bias = _attn_bwd(q_t, kp, vp, q_norm_g, k_norm_g, sinks3, bias,
                                                         _heads_first(d_oat, AH), KVH, comm=cm2)
    recv2["ff1"] = cm2.result(hf1)
    for k in small:
        part[k] = _rs_add(gw[k], cm2.result(hh[k]), waxis[k], owner_base, "rs_add_" + k)
    d_aq = _heads_last(dq_t)
    d_ak = _heads_last(dkp).astype(BF16)
    d_av = _heads_last(dvp).astype(BF16)
    d_proj = jnp.concatenate([d_hq, d_hf, d_hi, d_hg, d_aq, d_ak, d_av, d_ghg, d_gat], axis=1)
    cm = _Comm()
    hh = {k: _rs_ici(cm, part[k]) for k in small}
    gw_in = _mm(h, d_proj, "tn", BF16, "dw_in", comm=cm)
    for k in small:
        recv2[k] = cm.result(hh[k])

    wm = LANES * A
    cm = _Comm()
    hi_ = cm.inp(gw_in)
    h_main, h_mid = cm.out((4, D, wm), BF16), cm.out((4, D, LANES), BF16)
    for i, r in enumerate(CHIP_RELS):
        def main_view(ref, p, r=r):
            o = p["me"] ^ r ^ 1
            return ref.at[:, pl.ds(pl.multiple_of((PAIR * (o // 2) + (A + 1) * (1 - p["c"])) * LANES, LANES), wm)]

        def mid_view(ref, p, r=r):
            o = p["me"] ^ r
            return ref.at[:, pl.ds(pl.multiple_of((PAIR * (o // 2) + A) * LANES, LANES), LANES)]

        cm.copy(hi_, main_view, h_main, _slot_view(i), 1)
        cm.copy(hi_, mid_view, h_mid, _slot_view(i), 1)
    _call(lambda: None, [], name="rs_d2d_in", out_shape=(), comm=cm)
    chip = jnp.stack([(me ^ r) // 2 for r in CHIP_RELS]).astype(jnp.int32)
    part_main = _rs_add(gw_in, cm.result(h_main), 1, PAIR * chip + (A + 1) * cc, "rs_add_in_main", tw=LANES)
    part_mid = _rs_add(gw_in, cm.result(h_mid), 1, PAIR * chip + A, "rs_add_in_mid", tw=LANES)
    rs_in = _rs_split_start([part_main, part_mid], "rs_in_start")
    d_h = _mm(d_proj, w_in_full, "nt", F32, "d_h", tn=D, after=[rs_in["token"]])

    def norm1_bwd(dhv, xv, dx1v, g1n, sh, sc):
        _, vjp = jax.vjp(_modnorm, xv, g1n, sh, sc)
        dx, dg, dsh, dsc = vjp(dhv)
        return (dx1v + dx,), (dg, dsh, dsc)

    grad_x, d_g1n, d_shift1, d_scale1 = _rowwise(norm1_bwd, [(d_h, D, 0), (x2, D, 0), (d_x1, D, 0)],
                                                 [norm1_g, shift1, scale1], [(D, F32)], [(1, D)] * 3, "norm1_bwd")

    def sum4(p0, p1, p2, p3):
        return ((p0.astype(F32) + p1.astype(F32)) + p2.astype(F32)) + p3.astype(F32)

    def update_fn(w, m, v, p0, p1, p2, p3):
        g = sum4(p0, p1, p2, p3)
        delta, mn, vn = _adamw(w, g, m, v)
        return (g, delta, mn, vn), ()

    wmv = dict(zip(wnames, ((w_branch_hg, m_w_branch_hg, v_w_branch_hg), (w_branch_attn, m_w_branch_attn, v_w_branch_attn),
                            (w_out, m_w_out, v_w_out), (w_ff1, m_w_ff1, v_w_ff1), (w_ff2, m_w_ff2, v_w_ff2))))
    res = {}

    def update(k, p, rx):
        w, m, v = (t[0] for t in wmv[k])
        n = w.shape[1]
        ins = [(t, n, 0) for t in (w, m, v)] + [(p, n, 0, 0)] + [(rx, n, 0, i) for i in range(3)]
        res[k] = [t[None] for t in _rowwise(update_fn, ins, [], [(n, F32)] * 4, [], "update_" + k)]

    for k in wnames:
        update(k, part[k], recv2[k])
    (part_main, part_mid), (rx_main, rx_mid) = _rs_split_wait(rs_in, [grad_x] + [res[k][0] for k in wnames], "rs_in_wait")
    g_main, = _rowwise(lambda *p: ((sum4(*p),), ()), [(part_main, wm, 0, 0)] + [(rx_main, wm, 0, i) for i in range(3)], [],
                       [(wm, F32)], [], "sum_in_main")
    g_mid, = _rowwise(lambda *p: ((sum4(*p),), ()), [(part_mid, LANES, 0, 0)] + [(rx_mid, LANES, 0, i) for i in range(3)], [],
                      [(LANES, F32)], [], "sum_in_mid")
    g_in = jnp.where(cc == 0, jnp.concatenate([g_main, g_mid[:, :LANES // 2]], axis=1),
                     jnp.concatenate([g_mid[:, LANES // 2:], g_main], axis=1))

    def update_given(w, m, v, g):
        delta, mn, vn = _adamw(w, g, m, v)
        return (g, delta, mn, vn), ()

    res["in"] = [t[None] for t in _rowwise(update_given, [(t, BW, 0) for t in (w_in[0], m_w_in[0], v_w_in[0], g_in)], [],
                                           [(BW, F32)] * 4, [], "update_in")]

    d_sinks = d_sk.reshape(1, AH)
    (d_table_t,) = _whole(lambda db, bk: (_dot(db, _onehot(bk), NT, precision=HIGHEST),),
                          [d_bias.reshape(AH, AT_BLOCK * 2 * AT_BLOCK), bucket], [((AH, N_BUCKETS), F32)], "bias_bwd")
    smalls = [d_g1n, d_g2n, d_lb, d_gout_h, d_qg, d_kg, d_sinks, d_table_t.T.reshape(1, N_BUCKETS * AH)]
    widths = [s.shape[1] for s in smalls]
    lanes = [-(-w // LANES) * LANES for w in widths]
    smalls = [jnp.pad(s, ((0, 0), (0, p - w))) for s, w, p in zip(smalls, widths, lanes)]
    tail_row = jnp.concatenate([d_shift1, d_scale1, d_gate1, d_shift2, d_scale2, d_gate2] + smalls, axis=1)
    (tail_row,) = _behind([tail_row], [g_mid])
    tail_all = _gather_small(tail_row, me, "gather_tail")[:, 0, :]
    d_ada_all, packed = tail_all[:, :6 * D], tail_all[:, 6 * D:]
    d_ada_cols = lax.dynamic_slice(d_ada_all, (0, me * ADA_N), (N_DEV, ADA_N))

    def ada_update(cv, dav, w, m, v):
        g = _bdot(_silu(cv), dav, TN)
        delta, mn, vn = _adamw(w, g, m, v)
        return (g, delta, mn, vn), ()

    res["ada"] = [t[None] for t in _ada_update_call(ada_update, c_all, d_ada_cols, w_ada[0], m_w_ada[0], v_w_ada[0], _tile(D, 256, 16))]

    offs = [sum(lanes[:i]) for i in range(len(lanes))]

    def small_update(pk, dada, lg, *wmv_flat):
        tot = pk[0:1]
        for d in range(1, N_DEV):
            tot = tot + pk[d:d + 1]
        gb = dada[0:1]
        for d in range(1, N_DEV):
            gb = gb + dada[d:d + 1]
        gs = [tot[:, offs[i]:offs[i] + widths[i]] for i in range(len(widths))]
        _, lb_vjp = jax.vjp(_softmax0, lg)
        (g_lg,) = lb_vjp(gs[2])
        grads = [gb, gs[0], gs[1], g_lg, gs[3], gs[4], gs[5], gs[6], gs[7]]
        outs = []
        for i, g in enumerate(grads):
            w, m, v = wmv_flat[3 * i:3 * i + 3]
            delta, mn, vn = _adamw(w, g, m, v)
            outs += [g, delta, mn, vn]
        return tuple(outs)

    tbl = lambda t: t.reshape(1, N_BUCKETS * AH)
    small_wmv = [(b_ada, m_b_ada, v_b_ada), (norm1_g, m_norm1_g, v_norm1_g), (norm2_g, m_norm2_g, v_norm2_g),
                 (hg_lb_logits, m_hg_lb_logits, v_hg_lb_logits), (hg_out_norm_g, m_hg_out_norm_g, v_hg_out_norm_g),
                 (q_norm_g, m_q_norm_g, v_q_norm_g), (k_norm_g, m_k_norm_g, v_k_norm_g),
                 (attn_sinks, m_attn_sinks, v_attn_sinks),
                 (tbl(rel_bias_table), tbl(m_rel_bias_table), tbl(v_rel_bias_table))]
    flat = [t for trip in small_wmv for t in trip]
    out_shapes = [(trip[0].shape, F32) for trip in small_wmv for _ in range(4)]
    sres = _whole(small_update, [packed, d_ada_all, hg_lb_logits] + flat, out_shapes, "small_update")
    names_small = ("b_ada", "norm1_g", "norm2_g", "lb", "gout", "qg", "kg", "sinks", "table")
    for i, k in enumerate(names_small):
        r = sres[4 * i:4 * i + 4]
        if k == "table":
            r = [t.reshape(N_BUCKETS, AH) for t in r]
        res[k] = r

    order = ("ada", "b_ada", "norm1_g", "norm2_g", "in", "lb", "gout", "qg", "kg", "sinks", "table", "bhg", "bat", "out", "ff1", "ff2")
    outs = [loss, grad_x[None]]
    for j in range(4):
        outs += [res[k][j] for k in order]
    return tuple(outs)
```

```python
import functools
import math

import jax
import jax.numpy as jnp
from jax import lax
from jax.experimental import pallas as pl
from jax.experimental.pallas import tpu as pltpu

F32 = jnp.float32
BF16 = jnp.bfloat16
EPS = 1e-6
NEG_INF = -1e30
HG_DK = 128
HG_CHUNK = 64
AT_BLOCK = 128
N_BUCKETS = 32
MAX_EXACT = 16
MAX_DISTANCE = 128
N_DEV = 8
LANES = 128
VMEM_LIMIT = 56 * 1024 * 1024
ADAM_LR, ADAM_B1, ADAM_B2, ADAM_EPS, ADAM_WD, ADAM_STEP = 0.001, 0.9, 0.999, 1e-08, 0.01, 10
HIGHEST = lax.Precision.HIGHEST
MESH = pl.DeviceIdType.MESH
ANY = pl.BlockSpec(memory_space=pl.ANY)
CHIP_RELS = (0, 4, 2, 6)

NN = (((1,), (0,)), ((), ()))
NT = (((1,), (1,)), ((), ()))
TN = (((0,), (0,)), ((), ()))


def _tile(n, pref, unit):
    if n <= pref:
        return n
    t = (pref // unit) * unit
    while t >= unit:
        if n % t == 0:
            return t
        t -= unit
    return n


def _dot(a, b, dn, precision=None):
    return lax.dot_general(a, b, dn, preferred_element_type=F32, precision=precision)


def _bdot(a, b, dn):
    return _dot(a.astype(BF16), b.astype(BF16), dn)


def _position():
    x, y, c = lax.axis_index("x"), lax.axis_index("y"), lax.axis_index("c")
    return dict(x=x, y=y, c=c, me=4 * x + 2 * y + c)


def _peer_position(p, rel):
    x = 1 - p["x"] if rel & 4 else p["x"]
    y = 1 - p["y"] if rel & 2 else p["y"]
    c = 1 - p["c"] if rel & 1 else p["c"]
    return dict(x=x, y=y, c=c, me=4 * x + 2 * y + c)


class _Comm:
    def __init__(self):
        self.ins, self.outs, self.alias, self.plans, self.res = [], [], {}, [], None

    def inp(self, arr):
        self.ins.append(arr)
        return ("i", len(self.ins) - 1)

    def out(self, shape, dtype, alias=None):
        self.outs.append(jax.ShapeDtypeStruct(tuple(shape), dtype))
        if alias is not None:
            self.alias[alias[1]] = len(self.outs) - 1
        return ("o", len(self.outs) - 1)

    def copy(self, src, src_view, dst, dst_view, rel):
        self.plans.append((src, src_view, dst, dst_view, rel))

    def result(self, handle):
        return self.res[handle[1]]

    def build(self, in_refs, out_refs, send_sems, recv_sems):
        pos = _position()
        ref = lambda h: in_refs[h[1]] if h[0] == "i" else out_refs[h[1]]
        ops = []
        for k, (src, sv, dst, dv, rel) in enumerate(self.plans):
            s = sv(ref(src), pos)
            if rel == 0:
                cp = pltpu.make_async_copy(s, dv(ref(dst), pos), send_sems.at[k])
                ops.append((cp.start, cp.wait))
                continue
            peer = _peer_position(pos, rel)
            mk = lambda d: pltpu.make_async_remote_copy(
                src_ref=s, dst_ref=d, send_sem=send_sems.at[k], recv_sem=recv_sems.at[k],
                device_id=(peer["x"], peer["y"], peer["c"]), device_id_type=MESH)
            out_cp, in_cp = mk(dv(ref(dst), pos)), mk(dv(ref(dst), peer))

            def wait(out_cp=out_cp, in_cp=in_cp):
                out_cp.wait_send()
                in_cp.wait_recv()

            ops.append((out_cp.start, wait))
        return ops


def _call(body, args, *, name, out_shape, in_specs=None, out_specs=None, grid=None, scratch_shapes=(), comm=None,
          prefetch=None, aliases=None, after=()):
    single = not isinstance(out_shape, (tuple, list))
    out_shape = (out_shape,) if single else tuple(out_shape)
    n_in, n_out, n_scr = len(args), len(out_shape), len(scratch_shapes)
    vm = pl.BlockSpec(memory_space=pltpu.VMEM)
    in_specs = [vm] * n_in if in_specs is None else list(in_specs)
    out_specs = [vm] * n_out if out_specs is None else (list(out_specs) if isinstance(out_specs, (tuple, list)) else [out_specs])
    n_pf = 0 if prefetch is None else len(prefetch)
    kw = {} if aliases is None else {"input_output_aliases": dict(aliases)}
    if comm is None and after:
        n_dep = len(after)

        def fn(*refs):
            body(*refs[:n_pf + n_in], *refs[n_pf + n_in + n_dep:])

        all_args, all_scratch = list(args) + list(after), list(scratch_shapes)
        in_specs = in_specs + [ANY] * n_dep
    elif comm is None:
        fn = body
        all_args, all_scratch = list(args), list(scratch_shapes)
    else:
        n_ci, n_co, n_x = len(comm.ins), len(comm.outs), len(comm.plans)

        def fn(*refs):
            pf, refs = refs[:n_pf], refs[n_pf:]
            o_in, c_in = refs[:n_in], refs[n_in:n_in + n_ci]
            o_out = refs[n_in + n_ci:n_in + n_ci + n_out]
            c_out = refs[n_in + n_ci + n_out:n_in + n_ci + n_out + n_co]
            scr = refs[n_in + n_ci + n_out + n_co:]
            ops = comm.build(c_in, c_out, scr[n_scr], scr[n_scr + 1])
            if grid:
                first = functools.reduce(jnp.logical_and, [pl.program_id(i) == 0 for i in range(len(grid))])
                last = functools.reduce(jnp.logical_and, [pl.program_id(i) == g - 1 for i, g in enumerate(grid)])

                @pl.when(first)
                def _():
                    for start, _w in ops:
                        start()
            else:
                for start, _w in ops:
                    start()
            body(*pf, *o_in, *o_out, *scr[:n_scr])
            if grid:
                @pl.when(last)
                def _():
                    for _s, wait in ops:
                        wait()
            else:
                for _s, wait in ops:
                    wait()

        all_args = list(args) + list(comm.ins)
        in_specs = in_specs + [ANY] * n_ci
        out_shape = out_shape + tuple(comm.outs)
        out_specs = out_specs + [ANY] * n_co
        all_scratch = list(scratch_shapes) + [pltpu.SemaphoreType.DMA((n_x,)), pltpu.SemaphoreType.DMA((n_x,))]
        kw["input_output_aliases"] = {n_pf + n_in + i: n_out + o for i, o in comm.alias.items()}
    sem = None if grid is None else ("arbitrary",) * len(grid)
    params = pltpu.CompilerParams(dimension_semantics=sem, vmem_limit_bytes=VMEM_LIMIT)
    if prefetch is None:
        spec = dict(in_specs=in_specs, out_specs=tuple(out_specs), scratch_shapes=all_scratch)
        if grid is not None:
            spec["grid"] = grid
    else:
        spec = dict(grid_spec=pltpu.PrefetchScalarGridSpec(
            num_scalar_prefetch=n_pf, grid=grid, in_specs=in_specs, out_specs=tuple(out_specs), scratch_shapes=all_scratch))
        all_args = list(prefetch) + all_args
    res = pl.pallas_call(fn, name=name, out_shape=out_shape, compiler_params=params, **spec, **kw)(*all_args)
    res = list(res)
    if comm is not None:
        comm.res = res[n_out:]
        res = res[:n_out]
    return res[0] if single else res


def _whole_view(ref, pos):
    return ref


def _block_view(axis, n, index, rows=None):
    def view(ref, pos):
        off = pl.multiple_of(index(pos) * n, n)
        if rows is None:
            return ref.at[:, pl.ds(off, n)] if axis == 1 else ref.at[pl.ds(off, n), :]
        lo, cnt = rows[0], rows[1] - rows[0]
        if axis == 1:
            return ref.at[pl.ds(lo, cnt), pl.ds(off, n)]
        return ref.at[pl.ds(pl.multiple_of(off + lo, 16), cnt), :]
    return view


def _rows_view(rows):
    def view(ref, pos):
        return ref if rows is None else ref.at[pl.ds(rows[0], rows[1] - rows[0]), :]
    return view


def _slot_view(i, rows=None):
    def view(ref, pos):
        return ref.at[i] if rows is None else ref.at[i, pl.ds(rows[0], rows[1] - rows[0]), :]
    return view


def _exchange(items, name):
    cm = _Comm()
    for a, rel in items:
        cm.copy(cm.inp(a), _whole_view, cm.out(a.shape, a.dtype), _whole_view, rel)
    _call(lambda: None, [], name=name, out_shape=(), comm=cm)
    return cm.res


def _gather_small(v, me, name):
    cm = _Comm()
    hi, ho = cm.inp(v), cm.out((N_DEV,) + v.shape, v.dtype)
    for rel in range(N_DEV):
        cm.copy(hi, _whole_view, ho, lambda ref, p: ref.at[p["me"]], rel)
    _call(lambda: None, [], name=name, out_shape=(), comm=cm)
    return cm.result(ho)


def _ag_ici(cm, blk, axis, rows=None, into=None):
    n = blk.shape[axis]
    shape = list(blk.shape)
    shape[axis] = n * N_DEV
    hi = cm.inp(blk)
    ho = cm.out(shape, blk.dtype) if into is None else cm.out(shape, blk.dtype, alias=cm.inp(into))
    own = _block_view(axis, n, lambda p: p["me"], rows)
    for rel in CHIP_RELS:
        cm.copy(hi, _rows_view(rows), ho, own, rel)
    return ho


def _ag_d2d(cm, full, axis):
    n = full.shape[axis] // N_DEV
    hi = cm.inp(full)
    ho = cm.out(full.shape, full.dtype, alias=hi)
    for r in CHIP_RELS:
        v = _block_view(axis, n, functools.partial(lambda p, r: p["me"] ^ r, r=r))
        cm.copy(hi, v, ho, v, 1)
    return ho


def _rs_d2d(cm, gw, axis):
    n = gw.shape[axis] // N_DEV
    shape = list(gw.shape)
    shape[axis] = n
    hi, ho = cm.inp(gw), cm.out([4] + shape, gw.dtype)
    for i, r in enumerate(CHIP_RELS):
        cm.copy(hi, _block_view(axis, n, functools.partial(lambda p, r: p["me"] ^ r ^ 1, r=r)), ho, _slot_view(i), 1)
    return ho


def _rs_ici(cm, part, rows=None, recv=None):
    if recv is None:
        ho = cm.out((3,) + part.shape[1:], part.dtype)
    else:
        ho = cm.out(recv.shape, recv.dtype, alias=cm.inp(recv))
    hi = cm.inp(part)
    for i in (1, 2, 3):
        cm.copy(hi, _slot_view(i, rows), ho, _slot_view(i - 1, rows), CHIP_RELS[i])
    return ho


def _rs_add(gw, recv, axis, base, name, tw=None):
    _, R, n = recv.shape
    fan = 1
    if axis == 1:
        tw = n if tw is None else tw
        fan = max(f for f in (4, 3, 2, 1) if (n // tw) % f == 0)
        gw_specs = [pl.BlockSpec((R, tw), functools.partial(lambda i, t, b, k: (0, b[i] + fan * t + k), k=k)) for k in range(fan)]
        rv_spec = pl.BlockSpec((None, R, tw * fan), lambda i, t, b: (i, 0, t))
        grid = (4, n // (tw * fan))
    else:
        tw = _tile(n, 1024, LANES)
        gw_specs = [pl.BlockSpec((R, tw), lambda i, t, b: (b[i], t))]
        rv_spec = pl.BlockSpec((None, R, tw), lambda i, t, b: (i, 0, t))
        grid = (4, n // tw)

    def body(b_ref, *refs):
        g_refs, r_ref, o_ref = refs[:fan], refs[fan], refs[fan + 1]
        g = g_refs[0][...] if fan == 1 else jnp.concatenate([g[...] for g in g_refs], axis=1)
        o_ref[...] = (g.astype(F32) + r_ref[...].astype(F32)).astype(o_ref.dtype)

    return _call(body, [gw] * fan + [recv], name=name, out_shape=jax.ShapeDtypeStruct(recv.shape, recv.dtype), grid=grid,
                 in_specs=gw_specs + [rv_spec], out_specs=rv_spec, prefetch=[base])


HBM_SPEC = pl.BlockSpec(memory_space=pltpu.HBM)
SEM_SPEC = pl.BlockSpec(memory_space=pltpu.SEMAPHORE)
SPLIT_PARAMS = pltpu.CompilerParams(has_side_effects=pltpu.SideEffectType.DATAFLOW_SIDE_EFFECTING)


def _split_copies(refs, plans, send_sems, recv_sems):
    pos = _position()
    out = []
    for k, (si, sv, li, lv, rel) in enumerate(plans):
        peer = _peer_position(pos, rel)
        mk = lambda d: pltpu.make_async_remote_copy(
            src_ref=sv(refs[si], pos), dst_ref=d, send_sem=send_sems.at[k], recv_sem=recv_sems.at[k],
            device_id=(peer["x"], peer["y"], peer["c"]), device_id_type=MESH)
        out.append((mk(lv(refs[li], pos)), mk(lv(refs[li], peer))))
    return out


def _split_start(arrays, plans, name):
    n = len(arrays)

    def body(*refs):
        send_sems, recv_sems = refs[n], refs[n + 1]
        for out_cp, _ in _split_copies(refs[:n], plans, send_sems, recv_sems):
            out_cp.start()
        refs[-1][...] = jnp.zeros_like(refs[-1])

    sems = pltpu.SemaphoreType.DMA((len(plans),))
    res = pl.pallas_call(
        body, name=name,
        out_shape=(sems, sems) + tuple(pltpu.HBM(a.shape, a.dtype) for a in arrays) + (jax.ShapeDtypeStruct((8, LANES), F32),),
        in_specs=[HBM_SPEC] * n, out_specs=(SEM_SPEC, SEM_SPEC) + (HBM_SPEC,) * n + (pl.BlockSpec(memory_space=pltpu.VMEM),),
        input_output_aliases={i: 2 + i for i in range(n)}, compiler_params=SPLIT_PARAMS,
    )(*[pltpu.with_memory_space_constraint(a, pltpu.HBM) for a in arrays])
    return res[0], res[1], list(res[2:2 + n]), res[-1]


def _split_wait(send_sems, recv_sems, arrays, plans, after, name):
    n, na = len(arrays), len(after)

    def body(*refs):
        for out_cp, in_cp in _split_copies(refs[:n], plans, refs[n], refs[n + 1]):
            out_cp.wait_send()
            in_cp.wait_recv()

    res = pl.pallas_call(
        body, name=name, out_shape=tuple(pltpu.HBM(a.shape, a.dtype) for a in arrays),
        in_specs=[HBM_SPEC] * n + [SEM_SPEC, SEM_SPEC] + [ANY] * na, out_specs=(HBM_SPEC,) * n,
        input_output_aliases={i: i for i in range(n)}, compiler_params=SPLIT_PARAMS,
    )(*arrays, send_sems, recv_sems, *after)
    return list(res)


def _rs_split_start(parts, name):
    nw = len(parts)
    lands = [lax.empty((3,) + p.shape[1:], p.dtype) for p in parts]
    plans = [(s, _slot_view(i), nw + s, _slot_view(i - 1), CHIP_RELS[i]) for s in range(nw) for i in (1, 2, 3)]
    send_sems, recv_sems, arrays, token = _split_start(list(parts) + lands, plans, name)
    return dict(sems=(send_sems, recv_sems), arrays=arrays, plans=plans, token=token, nw=nw)


def _rs_split_wait(h, after, name):
    arrays = _split_wait(h["sems"][0], h["sems"][1], h["arrays"], h["plans"], after, name)
    return arrays[:h["nw"]], arrays[h["nw"]:]


def _behind(xs, tokens):
    out = lax.optimization_barrier((tuple(xs), tuple(tokens)))
    return list(out[0])


def _ag_w_in(src, a, D, INW):
    wm = LANES * a

    hd = D // 2
    ALL, TOP, BOT = (0, D), (0, hd), (hd, D)

    def main_place(ref, p, rows=ALL):
        off = pl.multiple_of(((2 * a + 1) * (p["me"] // 2) + (a + 1) * p["c"]) * LANES, LANES)
        return ref.at[pl.ds(rows[0], rows[1] - rows[0]), pl.ds(off, wm)]

    def main_src(ref, p):
        return ref.at[:, pl.ds(pl.multiple_of(p["c"] * LANES, LANES), wm)]

    def mid_src(ref, p):
        return ref.at[:, pl.ds(pl.multiple_of((1 - p["c"]) * wm, LANES), LANES)]

    def mid_place(ref, p, rows=ALL):
        return ref.at[p["me"], pl.ds(rows[0], rows[1] - rows[0]), :]

    def body(src_ref, full_ref, mid_ref, send_sems, recv_sems):
        pos = _position()
        sib, xn, yn = (_peer_position(pos, r) for r in (1, 4, 2))
        dg = _peer_position(pos, 6)
        started = []

        def remote(k, s, d, to):
            return pltpu.make_async_remote_copy(src_ref=s, dst_ref=d, send_sem=send_sems.at[k], recv_sem=recv_sems.at[k],
                                                device_id=(to["x"], to["y"], to["c"]), device_id_type=MESH)

        def send(k, owner, rows, to, from_src=False):
            for j, (src_v, place) in enumerate(((main_src, main_place), (mid_src, mid_place))):
                s = src_v(src_ref, pos) if from_src else place(full_ref if j == 0 else mid_ref, owner, rows)
                cp = remote(k + j, s, place(full_ref if j == 0 else mid_ref, owner, rows), to)
                cp.start()
                started.append(cp)

        def landed(k, owner, rows, frm):
            for j, place in enumerate((main_place, mid_place)):
                ref = full_ref if j == 0 else mid_ref
                remote(k + j, place(ref, owner, rows), place(ref, owner, rows), frm).wait_recv()

        local = [pltpu.make_async_copy(main_src(src_ref, pos), main_place(full_ref, pos), send_sems.at[18]),
                 pltpu.make_async_copy(mid_src(src_ref, pos), mid_place(mid_ref, pos), send_sems.at[19])]
        for cp in local:
            cp.start()
        send(0, pos, ALL, sib, from_src=True)
        send(2, pos, ALL, xn, from_src=True)
        send(4, pos, ALL, yn, from_src=True)
        landed(2, xn, ALL, xn)
        send(10, xn, ALL, sib)
        send(6, xn, TOP, yn)
        landed(4, yn, ALL, yn)
        send(12, yn, ALL, sib)
        send(8, yn, BOT, xn)
        landed(6, dg, TOP, yn)
        send(14, dg, TOP, sib)
        landed(8, dg, BOT, xn)
        send(16, dg, BOT, sib)
        sib_of = lambda p: _peer_position(p, 1)
        landed(0, sib, ALL, sib)
        landed(10, sib_of(xn), ALL, sib)
        landed(12, sib_of(yn), ALL, sib)
        landed(14, sib_of(dg), TOP, sib)
        landed(16, sib_of(dg), BOT, sib)
        for cp in started:
            cp.wait_send()
        for cp in local:
            cp.wait()

    return _call(body, [src], name="ag_w_in", in_specs=[ANY], out_specs=[ANY, ANY],
                 out_shape=(jax.ShapeDtypeStruct((D, INW), BF16), jax.ShapeDtypeStruct((N_DEV, D, LANES), BF16)),
                 scratch_shapes=[pltpu.SemaphoreType.DMA((20,)), pltpu.SemaphoreType.DMA((20,))])


def _patch_mid(full, mid, a):
    D = full.shape[0]

    def body(full_ref, e_ref, o_ref, out_ref):
        out_ref[...] = e_ref[...] + o_ref[...]

    return _call(body, [full, mid, mid], name="patch_mid", grid=(N_DEV // 2,),
                 out_shape=jax.ShapeDtypeStruct(full.shape, full.dtype),
                 in_specs=[ANY, pl.BlockSpec((None, D, LANES), lambda j: (2 * j, 0, 0)),
                           pl.BlockSpec((None, D, LANES), lambda j: (2 * j + 1, 0, 0))],
                 out_specs=pl.BlockSpec((D, LANES), lambda j: (0, (2 * a + 1) * j + a)), aliases={0: 0})


MM_RESIDENT = 2048


def _mm(a, b, mode, out_dtype, name, b_off=0, n=None, comm=None, extras=(), epi=None, tn=None, after=(), b_order=None):
    if mode == "nn":
        (M, K), (K2, N) = a.shape, b.shape
    elif mode == "nt":
        (M, K), (N, K2) = a.shape, b.shape
    else:
        (K, M), (K2, N) = a.shape, b.shape
    assert K == K2, (a.shape, b.shape, mode)
    if n is not None:
        N = n
    single = not isinstance(out_dtype, (tuple, list))
    out_dtypes = (out_dtype,) if single else tuple(out_dtype)
    if epi is None:
        epi = lambda r: (r,)
    tk = K if K <= MM_RESIDENT else (MM_RESIDENT if K % MM_RESIDENT == 0 else _tile(K, 512, LANES))
    nk = K // tk
    if M > MM_RESIDENT and mode == "tn" and N <= MM_RESIDENT and not b_off:
        tm, tn = _tile(M, 512, LANES), N
    elif nk > 1:
        tm, tn = _tile(M, 1024, LANES), _tile(N, tn or 1024, LANES)
    else:
        tm = _tile(M, MM_RESIDENT, LANES)
        tn = _tile(math.gcd(N, b_off) if b_off else N, tn or 512, LANES)
    jb = b_off // tn
    dn = {"nn": NN, "nt": NT, "tn": TN}[mode]
    ne, no = len(extras), len(out_dtypes)

    def body(a_ref, b_ref, *rest):
        e_refs, o_refs = rest[:ne], rest[ne:ne + no]

        def finish(r):
            for o_ref, v in zip(o_refs, epi(r, *[e[...] for e in e_refs])):
                o_ref[...] = v.astype(o_ref.dtype)

        if nk == 1:
            finish(_bdot(a_ref[...], b_ref[...], dn))
            return
        acc_ref = rest[ne + no]
        k = pl.program_id(2)

        @pl.when(k == 0)
        def _():
            acc_ref[...] = _bdot(a_ref[...], b_ref[...], dn)

        @pl.when(jnp.logical_and(k > 0, k < nk - 1))
        def _():
            acc_ref[...] += _bdot(a_ref[...], b_ref[...], dn)

        @pl.when(k == nk - 1)
        def _():
            finish(acc_ref[...] + _bdot(a_ref[...], b_ref[...], dn))

    a_spec = pl.BlockSpec((tk, tm), lambda i, j, k: (k, i)) if mode == "tn" else pl.BlockSpec((tm, tk), lambda i, j, k: (i, k))
    col = (lambda j: j + jb) if b_order is None else functools.partial(b_order, tn)
    b_spec = pl.BlockSpec((tn, tk), lambda i, j, k: (j, k)) if mode == "nt" else pl.BlockSpec((tk, tn), lambda i, j, k: (k, col(j)))
    o_spec = pl.BlockSpec((tm, tn), lambda i, j, k: (i, j))
    res = _call(body, [a, b] + list(extras), name=name, grid=(M // tm, N // tn, nk),
                out_shape=tuple(jax.ShapeDtypeStruct((M, N), dt) for dt in out_dtypes),
                in_specs=[a_spec, b_spec] + [o_spec] * ne, out_specs=[o_spec] * no,
                scratch_shapes=[pltpu.VMEM((tm, tn), F32)] if nk > 1 else [], comm=comm, after=after)
    return res[0] if single else res


def _rowwise(fn, row_ins, bcast_ins, row_outs, acc_outs, name, rt=256, comm=None):
    L = row_ins[0][0].shape[-2]
    rt = _tile(L, rt, 16)
    nr, nb, no = len(row_ins), len(bcast_ins), len(row_outs)

    def body(*refs):
        i = pl.program_id(0)
        vals = [r[...] for r in refs[:nr + nb]]
        outs, accs = fn(*vals)
        for r, v in zip(refs[nr + nb:nr + nb + no], outs):
            r[...] = v.astype(r.dtype)
        acc_refs = refs[nr + nb + no:]

        @pl.when(i == 0)
        def _():
            for r in acc_refs:
                r[...] = jnp.zeros_like(r)

        for r, v in zip(acc_refs, accs):
            r[...] += v

    in_specs = []
    for spec in row_ins:
        w, cb = spec[1], spec[2]
        if len(spec) == 4:
            in_specs.append(pl.BlockSpec((None, rt, w), functools.partial(lambda i, cb, ld: (ld, i, cb), cb=cb, ld=spec[3])))
        else:
            in_specs.append(pl.BlockSpec((rt, w), functools.partial(lambda i, cb: (i, cb), cb=cb)))
    in_specs += [pl.BlockSpec(b.shape, lambda i: (0, 0)) for b in bcast_ins]
    out_specs = [pl.BlockSpec((rt, w), lambda i: (i, 0)) for w, _ in row_outs]
    out_specs += [pl.BlockSpec(s, lambda i: (0, 0)) for s in acc_outs]
    out_shape = [jax.ShapeDtypeStruct((L, w), dt) for w, dt in row_outs] + [jax.ShapeDtypeStruct(s, F32) for s in acc_outs]
    return _call(body, [s[0] for s in row_ins] + list(bcast_ins), name=name, grid=(L // rt,), out_shape=tuple(out_shape),
                 in_specs=in_specs, out_specs=out_specs, comm=comm)


def _whole(fn, ins, out_shapes, name):
    def body(*refs):
        outs = fn(*[r[...] for r in refs[:len(ins)]])
        for r, v in zip(refs[len(ins):], outs):
            r[...] = v.astype(r.dtype)

    return _call(body, list(ins), name=name, out_shape=tuple(jax.ShapeDtypeStruct(s, dt) for s, dt in out_shapes))


def _silu(x):
    return x * jax.nn.sigmoid(x)


def _rms(x, g):
    return (x * lax.rsqrt(jnp.mean(x * x, axis=-1, keepdims=True) + EPS)) * g


def _modnorm(x, g, shift, scale):
    return _rms(x, g) * (1.0 + scale) + shift


def _adamw(w, g, m, v):
    m = ADAM_B1 * m + (1.0 - ADAM_B1) * g
    v = ADAM_B2 * v + (1.0 - ADAM_B2) * jnp.square(g)
    m_hat = m / (1.0 - ADAM_B1 ** ADAM_STEP)
    v_hat = v / (1.0 - ADAM_B2 ** ADAM_STEP)
    delta = -ADAM_LR * (m_hat / (jnp.sqrt(v_hat) + ADAM_EPS) + ADAM_WD * w)
    return delta, m, v


def _lower_bound(lg):
    e = jnp.exp(lg - jnp.max(lg, axis=0, keepdims=True))
    return e[0:1] / jnp.sum(e, axis=0, keepdims=True)


def _hg_stages(hq_l, hf_l, hi_l, lb):
    C = hq_l[0].shape[0]
    row = lax.broadcasted_iota(jnp.int32, (C, C), 0)
    col = lax.broadcasted_iota(jnp.int32, (C, C), 1)
    tri = row >= col
    trif = tri.astype(F32)
    f_l = [lb + (1.0 - lb) * jax.nn.sigmoid(hf) for hf in hf_l]
    b_l = [_dot(trif, jnp.log(f), NN, precision=HIGHEST) for f in f_l]
    q_l = [_silu(hq) for hq in hq_l]
    m_l = [b[C // 2 - 1:C // 2] for b in b_l]
    bl_l = [b[C - 1:C] for b in b_l]
    sc_l = [jnp.where(tri, _bdot(q * jnp.exp(b - m), (1.0 - f) * jnp.exp(m - b), NT), 0.0)
            for q, f, b, m in zip(q_l, f_l, b_l, m_l)]
    o1_l = [_bdot(sc, hi, NN) for sc, hi in zip(sc_l, hi_l)]
    u_l = [_bdot(hi, (1.0 - f) * jnp.exp(bl - b), TN) for hi, f, b, bl in zip(hi_l, f_l, b_l, bl_l)]
    qb_l = [q * jnp.exp(b) for q, b in zip(q_l, b_l)]
    dec_l = [jnp.exp(bl) for bl in bl_l]
    return list(zip(o1_l, u_l, qb_l, dec_l))


def _hg_out(o, hgate, gout):
    return _rms(o, gout) * _silu(hgate)


HG_STAGE = 8
HG_GROUP = 32


def _hgrn_fwd(p4, lb_logits, gout, H, comm=None):
    L = p4.shape[0]
    C = HG_CHUNK
    GR = _tile(L // C, HG_GROUP, 1)
    T = GR * C
    N = L // T

    def body(hq_ref, hf_ref, hi_ref, hg_ref, lg_ref, gout_ref, o_ref, s_ref, st_ref):
        @pl.when(pl.program_id(1) == 0)
        def _():
            st_ref[...] = jnp.zeros_like(st_ref)

        lb = _lower_bound(lg_ref[...])
        st = st_ref[...]
        for c0 in range(0, GR, HG_STAGE):
            rows_l = [pl.ds(ci * C, C) for ci in range(c0, min(c0 + HG_STAGE, GR))]
            parts = _hg_stages([hq_ref[r, :] for r in rows_l], [hf_ref[r, :] for r in rows_l],
                               [hi_ref[r, :] for r in rows_l], lb)
            for ci, rows, (o1, u, qb, dec) in zip(range(c0, GR), rows_l, parts):
                s_ref[0, ci] = st
                o = o1 + _bdot(qb, st, NT)
                st = st * dec + u
                o_ref[rows, :] = _hg_out(o, hg_ref[rows, :], gout_ref[...]).astype(o_ref.dtype)
        st_ref[...] = st

    blk = lambda s: pl.BlockSpec((T, HG_DK), functools.partial(lambda h, n, s: (n, s * H + h), s=s))
    return _call(
        body, [p4, p4, p4, p4, lb_logits, gout], name="hgrn_fwd", grid=(H, N),
        out_shape=(jax.ShapeDtypeStruct((L, H * HG_DK), BF16), jax.ShapeDtypeStruct((H, N * GR, HG_DK, HG_DK), F32)),
        in_specs=[blk(0), blk(1), blk(2), blk(3), pl.BlockSpec((2, HG_DK), lambda h, n: (0, h)),
                  pl.BlockSpec((1, HG_DK), lambda h, n: (0, 0))],
        out_specs=(pl.BlockSpec((T, HG_DK), lambda h, n: (n, h)),
                   pl.BlockSpec((1, GR, HG_DK, HG_DK), lambda h, n: (h, n, 0, 0))),
        scratch_shapes=[pltpu.VMEM((HG_DK, HG_DK), F32)], comm=comm)


def _hgrn_bwd(p4, lb_logits, gout, s_all, d_out, H, comm=None):
    L = p4.shape[0]
    C = HG_CHUNK
    GR = _tile(L // C, HG_GROUP, 1)
    T = GR * C
    N = L // T

    def body(hq_ref, hf_ref, hi_ref, hg_ref, lg_ref, gout_ref, s_ref, do_ref,
             dq_ref, df_ref, di_ref, dg_ref, dlb_ref, dgo_ref, dst_ref):
        @pl.when(pl.program_id(1) == 0)
        def _():
            dst_ref[...] = jnp.zeros_like(dst_ref)
            dlb_ref[...] = jnp.zeros_like(dlb_ref)

        @pl.when(jnp.logical_and(pl.program_id(0) == 0, pl.program_id(1) == 0))
        def _():
            dgo_ref[...] = jnp.zeros_like(dgo_ref)

        lb = _lower_bound(lg_ref[...])
        dst = dst_ref[...]
        d_lb = jnp.zeros((1, HG_DK), F32)
        d_go = jnp.zeros((1, HG_DK), F32)
        for c0 in reversed(range(0, GR, HG_STAGE)):
            dst, d_lb_c, d_go_c = chunks_bwd(list(range(c0, min(c0 + HG_STAGE, GR))), lb, dst, hq_ref, hf_ref, hi_ref,
                                             hg_ref, gout_ref, s_ref, do_ref, dq_ref, df_ref, di_ref, dg_ref)
            d_lb += d_lb_c
            d_go += d_go_c
        dst_ref[...] = dst
        dlb_ref[...] += d_lb
        dgo_ref[...] += d_go

    def chunks_bwd(idx, lb, dst, hq_ref, hf_ref, hi_ref, hg_ref, gout_ref, s_ref, do_ref, dq_ref, df_ref, di_ref, dg_ref):
        n = len(idx)
        rows_l = [pl.ds(ci * C, C) for ci in idx]
        hq_l, hf_l, hi_l = ([r[rows, :] for rows in rows_l] for r in (hq_ref, hf_ref, hi_ref))
        st_l = [s_ref[0, ci] for ci in idx]
        row = lax.broadcasted_iota(jnp.int32, (C, C), 0)
        col = lax.broadcasted_iota(jnp.int32, (C, C), 1)
        tri = row >= col
        trif = tri.astype(F32)
        every = lambda fn, *ls: [fn(*a) for a in zip(*ls)]
        sg_l = every(jax.nn.sigmoid, hf_l)
        f_l = every(lambda sg: lb + (1.0 - lb) * sg, sg_l)
        b_l = every(lambda f: _dot(trif, jnp.log(f), NN, precision=HIGHEST), f_l)
        q_l = every(_silu, hq_l)
        m_l = every(lambda b: b[C // 2 - 1:C // 2], b_l)
        bl_l = every(lambda b: b[C - 1:C], b_l)
        e_qm_l = every(lambda b, m: jnp.exp(b - m), b_l, m_l)
        e_km_l = every(lambda b, m: jnp.exp(m - b), b_l, m_l)
        e_kl_l = every(lambda b, bl: jnp.exp(bl - b), b_l, bl_l)
        e_q_l = every(jnp.exp, b_l)
        dec_l = every(jnp.exp, bl_l)
        qe_l = every(lambda q, e: q * e, q_l, e_qm_l)
        ke_l = every(lambda f, e: (1.0 - f) * e, f_l, e_km_l)
        kd_l = every(lambda f, e: (1.0 - f) * e, f_l, e_kl_l)
        qb_l = every(lambda q, e: q * e, q_l, e_q_l)
        sc_l = every(lambda qe, ke: jnp.where(tri, _bdot(qe, ke, NT), 0.0), qe_l, ke_l)
        o_l = every(lambda sc, hi, qb, st: _bdot(sc, hi, NN) + _bdot(qb, st, NT), sc_l, hi_l, qb_l, st_l)
        vj_l = every(lambda o, rows: jax.vjp(_hg_out, o, hg_ref[rows, :], gout_ref[...])[1](do_ref[rows, :]), o_l, rows_l)
        do_l = [v[0] for v in vj_l]
        dsc_l = every(lambda do, hi: jnp.where(tri, _bdot(do, hi, NT), 0.0), do_l, hi_l)
        dv1_l = every(lambda sc, do: _bdot(sc, do, TN), sc_l, do_l)
        dqe_l = every(lambda dsc, ke: _bdot(dsc, ke, NN), dsc_l, ke_l)
        dke_l = every(lambda dsc, qe: _bdot(dsc, qe, TN), dsc_l, qe_l)
        dqb_l = every(lambda do, st: _bdot(do, st, NN), do_l, st_l)
        own_l = every(lambda do, qb: _bdot(do, qb, TN), do_l, qb_l)
        dst_next_l = [None] * n
        for j in reversed(range(n)):
            dst_next_l[j] = dst
            dst = own_l[j] + dst * dec_l[j]
        dv_l = every(lambda dv1, kd, dn: dv1 + _bdot(kd, dn, NT), dv1_l, kd_l, dst_next_l)
        dkd_l = every(lambda hi, dn: _bdot(hi, dn, NN), hi_l, dst_next_l)
        ddec_l = every(lambda dn, st: jnp.sum(dn * st, axis=0, keepdims=True), dst_next_l, st_l)
        rowi = lax.broadcasted_iota(jnp.int32, (C, HG_DK), 0)
        tq_l = every(lambda a, b_: a * b_, dqe_l, qe_l)
        tk_l = every(lambda a, b_: a * b_, dke_l, ke_l)
        td_l = every(lambda a, b_: a * b_, dkd_l, kd_l)
        tb_l = every(lambda a, b_: a * b_, dqb_l, qb_l)
        db_l = every(lambda tq, tk, td, tb, ddec, dec: tq - tk - td + tb
                     + jnp.where(rowi == C // 2 - 1, jnp.sum(tk - tq, axis=0, keepdims=True), 0.0)
                     + jnp.where(rowi == C - 1, jnp.sum(td, axis=0, keepdims=True) + ddec * dec, 0.0),
                     tq_l, tk_l, td_l, tb_l, ddec_l, dec_l)
        dlf_l = every(lambda db: _dot(trif, db, TN, precision=HIGHEST), db_l)
        dk_l = every(lambda dke, e1, dkd, e2: dke * e1 + dkd * e2, dke_l, e_km_l, dkd_l, e_kl_l)
        df_l = every(lambda dlf, f, dk: dlf / f - dk, dlf_l, f_l, dk_l)
        d_lb = jnp.zeros((1, HG_DK), F32)
        d_go = jnp.zeros((1, HG_DK), F32)
        for j, rows in enumerate(rows_l):
            sg, hq = sg_l[j], hq_l[j]
            df_ref[rows, :] = (df_l[j] * (1.0 - lb) * sg * (1.0 - sg)).astype(df_ref.dtype)
            sq = jax.nn.sigmoid(hq)
            dq = dqe_l[j] * e_qm_l[j] + dqb_l[j] * e_q_l[j]
            dq_ref[rows, :] = (dq * (sq * (1.0 + hq * (1.0 - sq)))).astype(dq_ref.dtype)
            di_ref[rows, :] = dv_l[j].astype(di_ref.dtype)
            dg_ref[rows, :] = vj_l[j][1].astype(dg_ref.dtype)
            d_lb += jnp.sum(df_l[j] * (1.0 - sg), axis=0, keepdims=True)
            d_go += vj_l[j][2]
        return dst, d_lb, d_go

    blk = lambda s: pl.BlockSpec((T, HG_DK), functools.partial(lambda h, n, s: (N - 1 - n, s * H + h), s=s))
    oblk = pl.BlockSpec((T, HG_DK), lambda h, n: (N - 1 - n, h))
    vec = pl.BlockSpec((1, HG_DK), lambda h, n: (0, h))
    W = H * HG_DK
    return _call(
        body, [p4, p4, p4, p4, lb_logits, gout, s_all, d_out], name="hgrn_bwd", grid=(H, N),
        out_shape=tuple([jax.ShapeDtypeStruct((L, W), BF16)] * 4 + [jax.ShapeDtypeStruct((1, W), F32), jax.ShapeDtypeStruct((1, HG_DK), F32)]),
        in_specs=[blk(0), blk(1), blk(2), blk(3), pl.BlockSpec((2, HG_DK), lambda h, n: (0, h)),
                  pl.BlockSpec((1, HG_DK), lambda h, n: (0, 0)),
                  pl.BlockSpec((1, GR, HG_DK, HG_DK), lambda h, n: (h, N - 1 - n, 0, 0)), oblk],
        out_specs=(oblk, oblk, oblk, oblk, vec, pl.BlockSpec((1, HG_DK), lambda h, n: (0, 0))),
        scratch_shapes=[pltpu.VMEM((HG_DK, HG_DK), F32)], comm=comm)


def _bucket_ids():
    i = jnp.arange(AT_BLOCK, dtype=jnp.int32)[:, None]
    j = jnp.arange(2 * AT_BLOCK, dtype=jnp.int32)[None, :]
    n = jnp.maximum(i - j + AT_BLOCK, 0)
    nf = jnp.maximum(n, 1).astype(F32)
    large = MAX_EXACT + (jnp.log(nf / MAX_EXACT) / math.log(MAX_DISTANCE / MAX_EXACT) * (N_BUCKETS - MAX_EXACT)).astype(jnp.int32)
    large = jnp.minimum(large, N_BUCKETS - 1)
    return jnp.where(n < MAX_EXACT, n, large).reshape(1, -1)


def _onehot(bucket):
    ids = lax.broadcasted_iota(jnp.int32, (N_BUCKETS, bucket.shape[1]), 0)
    return (ids == bucket).astype(F32)


AT_PAIR = 2


def _attn_probs(qn_l, kn_l, bias_g, sink, first, scale):
    rows = qn_l[0].shape[0]
    i = jnp.bitwise_and(lax.broadcasted_iota(jnp.int32, (rows, AT_BLOCK), 0), AT_BLOCK - 1)
    j = lax.broadcasted_iota(jnp.int32, (rows, AT_BLOCK), 1)
    n = len(qn_l)
    lp_l = [_bdot(qn_l[s], kn_l[s], NT) * scale + bias_g[:, :AT_BLOCK] for s in range(n)]
    lc_l = [_bdot(qn_l[s], kn_l[s + 1], NT) * scale + bias_g[:, AT_BLOCK:] for s in range(n)]
    seen = [jnp.logical_and(j > i, jnp.logical_not(first))] + [j > i] * (n - 1)
    lp_l = [jnp.where(seen[s], lp_l[s], NEG_INF) for s in range(n)]
    lc_l = [jnp.where(j <= i, lc, NEG_INF) for lc in lc_l]
    m_l = [jnp.maximum(jnp.maximum(jnp.max(lp, axis=-1, keepdims=True), jnp.max(lc, axis=-1, keepdims=True)), sink)
           for lp, lc in zip(lp_l, lc_l)]
    pp_l = [jnp.exp(lp - m) for lp, m in zip(lp_l, m_l)]
    pc_l = [jnp.exp(lc - m) for lc, m in zip(lc_l, m_l)]
    ps_l = [jnp.exp(sink - m) for m in m_l]
    den_l = [jnp.sum(pp, axis=-1, keepdims=True) + jnp.sum(pc, axis=-1, keepdims=True) + ps
             for pp, pc, ps in zip(pp_l, pc_l, ps_l)]
    return [(pp / den, pc / den, ps / den) for pp, pc, ps, den in zip(pp_l, pc_l, ps_l, den_l)]


def _sink_rows(sk_ref, G):
    head = lax.broadcasted_iota(jnp.int32, (G * AT_BLOCK, 1), 0) // AT_BLOCK
    sink = jnp.zeros((G * AT_BLOCK, 1), F32)
    for g in range(G):
        sink = jnp.where(head == g, sk_ref[0, g:g + 1, :], sink)
    return sink


def _group_rows(ref, s, G, DH):
    B = AT_BLOCK
    rows = ref[pl.ds(s * B, B), :].astype(F32)
    return jnp.concatenate([rows[:, g * DH:(g + 1) * DH] for g in range(G)], axis=0)


def _ungroup_rows(val, G):
    B = AT_BLOCK
    return jnp.concatenate([val[g * B:(g + 1) * B] for g in range(G)], axis=1)


def _attn_specs(cols, G, DH):
    P, B = AT_PAIR, AT_BLOCK
    pk = _heads_per_tile(DH)
    q0, k0, v0 = cols[0] // (G * DH), cols[1] // (pk * DH), cols[2] // (pk * DH)
    assert cols[0] % (G * DH) == 0 and cols[1] % (pk * DH) == 0 and cols[2] % (pk * DH) == 0 and (G * DH) % LANES == 0
    qblk = pl.BlockSpec((P * B, G * DH), lambda h, m: (m, q0 + h))
    kblk = lambda c0, off: pl.BlockSpec((B, pk * DH), functools.partial(
        lambda h, m, c0, off: (jnp.maximum(P * m + off - 1, 0), c0 + h // pk), c0=c0, off=off))
    return qblk, [kblk(k0, off) for off in range(P + 1)], [kblk(v0, off) for off in range(P + 1)]


def _heads_per_tile(DH):
    return LANES // DH if DH < LANES else 1


def _my_head(ref, DH):
    pk = _heads_per_tile(DH)
    val = ref[...]
    if pk == 1:
        return val
    sub = pl.program_id(0) % pk
    out = val[:, :DH]
    for j in range(1, pk):
        out = jnp.where(sub == j, val[:, j * DH:(j + 1) * DH], out)
    return out


def _to_my_head(val, DH):
    pk = _heads_per_tile(DH)
    if pk == 1:
        return val
    sub = pl.program_id(0) % pk
    wide = jnp.concatenate([val] * pk, axis=1)
    lane = lax.broadcasted_iota(jnp.int32, wide.shape, 1)
    return jnp.where(lane // DH == sub, wide, 0.0)


def _attn_fwd(proj, cols, qg, kg, sinks, bias, AH, KVH, comm=None):
    L, DH = proj.shape[0], qg.shape[1]
    G = AH // KVH
    NB = L // AT_BLOCK
    scale = DH ** -0.5

    P, B = AT_PAIR, AT_BLOCK
    assert NB % P == 0

    def body(q_ref, *rest):
        k_refs, v_refs = rest[:P + 1], rest[P + 1:2 * P + 2]
        qg_ref, kg_ref, sk_ref, b_ref, o_ref = rest[2 * P + 2:]
        first = pl.program_id(1) == 0
        kn_l = [_rms(_my_head(r, DH), kg_ref[...]) for r in k_refs]
        v_l = [_my_head(r, DH) for r in v_refs]
        qn_l = [_rms(_group_rows(q_ref, s, G, DH), qg_ref[...]) for s in range(P)]
        probs = _attn_probs(qn_l, kn_l, b_ref[...].reshape(G * B, 2 * B), _sink_rows(sk_ref, G), first, scale)
        o_l = [_bdot(pp, v_l[s], NN) + _bdot(pc, v_l[s + 1], NN) for s, (pp, pc, _) in enumerate(probs)]
        for s, o in enumerate(o_l):
            o_ref[pl.ds(s * B, B), :] = _ungroup_rows(o, G).astype(o_ref.dtype)

    qblk, kspecs, vspecs = _attn_specs(cols, G, DH)
    return _call(
        body, [proj] * (2 * P + 3) + [qg, kg, sinks, bias], name="attn_fwd", grid=(KVH, NB // P),
        out_shape=jax.ShapeDtypeStruct((L, AH * DH), BF16),
        in_specs=[qblk] + kspecs + vspecs
        + [pl.BlockSpec((1, DH), lambda h, m: (0, 0)), pl.BlockSpec((1, DH), lambda h, m: (0, 0)),
           pl.BlockSpec((1, G, 1), lambda h, m: (h, 0, 0)), pl.BlockSpec((G, B, 2 * B), lambda h, m: (h, 0, 0))],
        out_specs=pl.BlockSpec((P * B, G * DH), lambda h, m: (m, h)), comm=comm)


def _attn_bwd(proj, cols, qg, kg, sinks, bias, d_o, AH, KVH, comm=None):
    L, DH = proj.shape[0], qg.shape[1]
    G = AH // KVH
    NB = L // AT_BLOCK
    B = AT_BLOCK
    scale = DH ** -0.5

    P = AT_PAIR
    assert NB % P == 0

    def body(q_ref, *rest):
        k_refs, v_refs = rest[:P + 1], rest[P + 1:2 * P + 2]
        qg_ref, kg_ref, sk_ref, b_ref, do_ref, dq_ref, dk_ref, dv_ref, dqg_ref, dkg_ref, dsk_ref, db_ref = rest[2 * P + 2:]
        m = pl.program_id(1)
        first = m == 0

        @pl.when(first)
        def _():
            for r in (dsk_ref, db_ref):
                r[...] = jnp.zeros_like(r)

        @pl.when(jnp.logical_and(first, pl.program_id(0) % _heads_per_tile(DH) == 0))
        def _():
            for r in (dk_ref, dv_ref):
                r[...] = jnp.zeros_like(r)

        @pl.when(jnp.logical_and(first, pl.program_id(0) == 0))
        def _():
            dqg_ref[...] = jnp.zeros_like(dqg_ref)
            dkg_ref[...] = jnp.zeros_like(dkg_ref)

        kgv, qgv = kg_ref[...], qg_ref[...]
        k_fw = [jax.vjp(_rms, _my_head(r, DH), kgv) for r in k_refs]
        v_l = [_my_head(r, DH) for r in v_refs]
        kn_l = [f[0] for f in k_fw]
        q_fw = [jax.vjp(_rms, _group_rows(q_ref, s, G, DH), qgv) for s in range(P)]
        qn_l = [f[0] for f in q_fw]
        probs = _attn_probs(qn_l, kn_l, b_ref[...].reshape(G * B, 2 * B), _sink_rows(sk_ref, G), first, scale)
        pp_l, pc_l, ps_l = ([p[t] for p in probs] for t in range(3))
        do_l = [_group_rows(do_ref, s, G, DH) for s in range(P)]
        dvp_l = [_bdot(pp, do, TN) for pp, do in zip(pp_l, do_l)]
        dvc_l = [_bdot(pc, do, TN) for pc, do in zip(pc_l, do_l)]
        dpp_l = [_bdot(do_l[s], v_l[s], NT) for s in range(P)]
        dpc_l = [_bdot(do_l[s], v_l[s + 1], NT) for s in range(P)]
        dsum_l = [jnp.sum(dpp * pp, axis=-1, keepdims=True) + jnp.sum(dpc * pc, axis=-1, keepdims=True)
                  for dpp, pp, dpc, pc in zip(dpp_l, pp_l, dpc_l, pc_l)]
        dlp_l = [pp * (dpp - ds) for pp, dpp, ds in zip(pp_l, dpp_l, dsum_l)]
        dlc_l = [pc * (dpc - ds) for pc, dpc, ds in zip(pc_l, dpc_l, dsum_l)]
        dsk_ref[0] += sum(jnp.sum((-ps * ds).reshape(G, B, 1), axis=1) for ps, ds in zip(ps_l, dsum_l))
        db_ref[:, :, :B] += sum(dlp_l).reshape(G, B, B)
        db_ref[:, :, B:] += sum(dlc_l).reshape(G, B, B)
        dlp_l, dlc_l = [d * scale for d in dlp_l], [d * scale for d in dlc_l]
        dqn_l = [_bdot(dlp_l[s], kn_l[s], NN) + _bdot(dlc_l[s], kn_l[s + 1], NN) for s in range(P)]
        dq_l = [q_fw[s][1](dqn_l[s]) for s in range(P)]
        for s in range(P):
            dq_ref[pl.ds(s * B, B), :] = _ungroup_rows(dq_l[s][0], G).astype(dq_ref.dtype)
        dkn_l = [jnp.zeros((B, DH), F32)] * (P + 1)
        dvk_l = [jnp.zeros((B, DH), F32)] * (P + 1)
        for s in range(P):
            dkn_l[s] = dkn_l[s] + _bdot(dlp_l[s], qn_l[s], TN)
            dkn_l[s + 1] = dkn_l[s + 1] + _bdot(dlc_l[s], qn_l[s], TN)
            dvk_l[s] = dvk_l[s] + dvp_l[s]
            dvk_l[s + 1] = dvk_l[s + 1] + dvc_l[s]
        dk_l = [k_fw[t][1](dkn_l[t]) for t in range(P + 1)]
        for t in range(P + 1):
            r = pl.multiple_of(jnp.maximum(P * m + t - 1, 0) * B, B)
            dk_ref[pl.ds(r, B), :] += _to_my_head(dk_l[t][0], DH)
            dv_ref[pl.ds(r, B), :] += _to_my_head(dvk_l[t], DH)
        dqg_ref[...] += sum(d[1] for d in dq_l)
        dkg_ref[...] += sum(d[1] for d in dk_l)

    qblk, kspecs, vspecs = _attn_specs(cols, G, DH)
    oblk = pl.BlockSpec((P * B, G * DH), lambda h, m: (m, h))
    pk = _heads_per_tile(DH)
    assert KVH % pk == 0
    accblk = pl.BlockSpec((L, pk * DH), lambda h, m: (0, h // pk))
    vecblk = pl.BlockSpec((1, DH), lambda h, m: (0, 0))
    return _call(
        body, [proj] * (2 * P + 3) + [qg, kg, sinks, bias, d_o], name="attn_bwd", grid=(KVH, NB // P),
        out_shape=(jax.ShapeDtypeStruct((L, AH * DH), BF16), jax.ShapeDtypeStruct((L, KVH * DH), F32),
                   jax.ShapeDtypeStruct((L, KVH * DH), F32), jax.ShapeDtypeStruct((1, DH), F32),
                   jax.ShapeDtypeStruct((1, DH), F32), jax.ShapeDtypeStruct((KVH, G, 1), F32),
                   jax.ShapeDtypeStruct((AH, B, 2 * B), F32)),
        in_specs=[qblk] + kspecs + vspecs
        + [pl.BlockSpec((1, DH), lambda h, m: (0, 0)), pl.BlockSpec((1, DH), lambda h, m: (0, 0)),
           pl.BlockSpec((1, G, 1), lambda h, m: (h, 0, 0)), pl.BlockSpec((G, B, 2 * B), lambda h, m: (h, 0, 0)), oblk],
        out_specs=(oblk, accblk, accblk, vecblk, vecblk, pl.BlockSpec((1, G, 1), lambda h, m: (h, 0, 0)),
                   pl.BlockSpec((G, B, 2 * B), lambda h, m: (h, 0, 0))), comm=comm)


def _heads_first(t, nh):
    L = t.shape[0]
    return jnp.transpose(t.reshape(L, nh, t.shape[1] // nh), (1, 0, 2))


def _heads_last(t):
    nh, L, dh = t.shape
    return jnp.transpose(t, (1, 0, 2)).reshape(L, nh * dh)


def _softmax0(lg):
    e = jnp.exp(lg - jnp.max(lg, axis=0, keepdims=True))
    return e[0:1] / jnp.sum(e, axis=0, keepdims=True)


def _ada_update_call(fn, c_all, d_cols, w, m, v, rt):
    D, n = w.shape

    def body(c_ref, d_ref, w_ref, m_ref, v_ref, g_out, dl_out, m_out, v_out):
        outs, _ = fn(c_ref[...], d_ref[...], w_ref[...], m_ref[...], v_ref[...])
        for r, val in zip((g_out, dl_out, m_out, v_out), outs):
            r[...] = val

    wblk = pl.BlockSpec((rt, n), lambda i: (i, 0))
    return _call(
        body, [c_all, d_cols, w, m, v], name="update_ada", grid=(D // rt,), out_shape=tuple([jax.ShapeDtypeStruct((D, n), F32)] * 4),
        in_specs=[pl.BlockSpec((N_DEV, rt), lambda i: (0, i)), pl.BlockSpec((N_DEV, n), lambda i: (0, 0)), wblk, wblk, wblk],
        out_specs=(wblk, wblk, wblk, wblk))


def kernel(x, c, w_ada, b_ada, norm1_g, norm2_g, w_in, hg_lb_logits, hg_out_norm_g, q_norm_g, k_norm_g, attn_sinks, rel_bias_table, w_branch_hg, w_branch_attn, w_out, w_ff1, w_ff2, loss_target, m_w_ada, m_b_ada, m_norm1_g, m_norm2_g, m_w_in, m_hg_lb_logits, m_hg_out_norm_g, m_q_norm_g, m_k_norm_g, m_attn_sinks, m_rel_bias_table, m_w_branch_hg, m_w_branch_attn, m_w_out, m_w_ff1, m_w_ff2, v_w_ada, v_b_ada, v_norm1_g, v_norm2_g, v_w_in, v_hg_lb_logits, v_hg_out_norm_g, v_q_norm_g, v_k_norm_g, v_attn_sinks, v_rel_bias_table, v_w_branch_hg, v_w_branch_attn, v_w_out, v_w_ff1, v_w_ff2):
    cc = lax.axis_index("c")
    me = 4 * lax.axis_index("x") + 2 * lax.axis_index("y") + cc
    x2 = x[0]
    tgt = loss_target[0]
    L, D = x2.shape
    HGW = hg_lb_logits.shape[1]
    H = HGW // HG_DK
    AH = attn_sinks.shape[1]
    DH = q_norm_g.shape[1]
    ATW = AH * DH
    BW = w_in.shape[2]
    INW = BW * N_DEV
    A = BW // LANES
    assert BW == LANES * A + LANES // 2
    KVW = (INW - 4 * HGW - ATW - 2 * D) // 2
    KVH = KVW // DH
    G = AH // KVH
    ADA_N = w_ada.shape[2]
    PAIR = 2 * A + 1

    c_all = _gather_small(c, me, "gather_c")[:, 0, :]
    b_cols = lax.dynamic_slice(b_ada, (0, me * ADA_N), (1, ADA_N))
    (ada_cols,) = _whole(lambda cv, w, b: (_bdot(_silu(cv), w, NN) + b,), [c_all, w_ada[0], b_cols],
                         [((N_DEV, ADA_N), F32)], "ada_fwd")
    ada_all = _gather_small(ada_cols, me, "gather_ada")
    ada_row = lax.dynamic_slice(ada_all, (0, me, 0), (N_DEV, 1, ADA_N)).reshape(1, 6 * D)

    w_in_b = w_in[0].astype(BF16)
    src_in = jnp.where(cc == 0, jnp.pad(w_in_b, ((0, 0), (0, LANES // 2))), jnp.pad(w_in_b, ((0, 0), (LANES // 2, 0))))
    (src_in,) = _behind([src_in], [ada_row])
    shift1, scale1, gate1, shift2, scale2, gate2 = [ada_row[:, i * D:(i + 1) * D] for i in range(6)]
    w_in_gapped, w_in_mid = _ag_w_in(src_in, A, D, INW)
    w_in_full = _patch_mid(w_in_gapped, w_in_mid, A)

    wnames = ("bhg", "bat", "out", "ff1", "ff2")
    small = ("bhg", "bat", "out")
    waxis = dict(zip(wnames, (1, 1, 0, 1, 0)))
    wsrc = dict(zip(wnames, (w_branch_hg, w_branch_attn, w_out, w_ff1, w_ff2)))
    wblk = {k: wsrc[k][0].astype(BF16) for k in wnames}
    wf = {}

    (h,) = _rowwise(lambda xv, g, sh, sc: ((_modnorm(xv, g, sh, sc),), ()), [(x2, D, 0)], [norm1_g, shift1, scale1],
                    [(D, BF16)], [], "norm1")
    o4, oa = 4 * HGW, 4 * HGW + ATW + 2 * KVW
    r1, r2 = wblk["ff1"].shape[0], wblk["ff2"].shape[0]
    assert o4 % D == 0

    def proj_order(tn_, j):
        t4, tg, ng = o4 // tn_, oa // tn_, (INW - oa) // tn_
        return jnp.where(j < t4, j, jnp.where(j < t4 + ng, j + (tg - t4), j - ng))

    cm = _Comm()
    hs = {k: _ag_ici(cm, wblk[k], waxis[k]) for k in small}
    hs["ff2"] = _ag_ici(cm, wblk["ff2"], waxis["ff2"], rows=(0, r2 // 4))
    proj = _mm(h, w_in_full, "nn", F32, "proj", comm=cm, b_order=proj_order)
    half = {k: cm.result(hs[k]) for k in hs}
    p4 = pg = proj
    GATE0 = o4 // D
    AT0 = o4 + (INW - oa)

    cm = _Comm()
    hs = {k: _ag_d2d(cm, half[k], waxis[k]) for k in small}
    hs["ff1"] = _ag_ici(cm, wblk["ff1"], waxis["ff1"], rows=(0, r1 // 2))
    o_hg, s_all = _hgrn_fwd(p4, hg_lb_logits, hg_out_norm_g, H, comm=cm)
    wf["bhg"], wf["bat"], wf["out"], half["ff1"] = (cm.result(hs[k]) for k in ("bhg", "bat", "out", "ff1"))

    bucket = _bucket_ids()
    (bias_flat,) = _whole(lambda tb, bk: (_dot(tb, _onehot(bk), TN, precision=HIGHEST),), [rel_bias_table, bucket],
                          [((AH, AT_BLOCK * 2 * AT_BLOCK), F32)], "bias_fwd")
    bias = bias_flat.reshape(AH, AT_BLOCK, 2 * AT_BLOCK)
    at_cols = (AT0, AT0 + ATW, AT0 + ATW + KVW)
    sinks3 = attn_sinks.reshape(KVH, G, 1)
    cm = _Comm()
    hs = {"ff1": _ag_ici(cm, wblk["ff1"], waxis["ff1"], rows=(r1 // 2, r1), into=half["ff1"])}
    o_at = _attn_fwd(proj, at_cols, q_norm_g, k_norm_g, sinks3, bias, AH, KVH, comm=cm)
    half["ff1"] = cm.result(hs["ff1"])

    bh = _mm(o_hg, wf["bhg"], "nn", F32, "branch_hg")
    ba = _mm(o_at, wf["bat"], "nn", F32, "branch_at")

    def merge_fn(bhv, bav, ghg, gat):
        return jax.nn.sigmoid(ghg) * bhv + jax.nn.sigmoid(gat) * bav

    cm = _Comm()
    hs = {"ff1": _ag_d2d(cm, half["ff1"], waxis["ff1"]),
          "ff2": _ag_ici(cm, wblk["ff2"], waxis["ff2"], rows=(r2 // 4, 3 * r2 // 8), into=half["ff2"])}
    (merged,) = _rowwise(lambda *a: ((merge_fn(*a),), ()), [(bh, D, 0), (ba, D, 0), (pg, D, GATE0), (pg, D, GATE0 + 1)], [],
                         [(D, BF16)], [], "merge", comm=cm)
    wf["ff1"], half["ff2"] = cm.result(hs["ff1"]), cm.result(hs["ff2"])
    cm = _Comm()
    hs = {"ff2": _ag_ici(cm, wblk["ff2"], waxis["ff2"], rows=(3 * r2 // 8, r2 // 2), into=half["ff2"])}
    mo = _mm(merged, wf["out"], "nn", F32, "out_proj", comm=cm)
    half["ff2"] = cm.result(hs["ff2"])

    def resid1(xv, mov, g1, g2n, sh, sc):
        x1v = xv + g1 * mov
        return (x1v, _modnorm(x1v, g2n, sh, sc)), ()

    x1, h2 = _rowwise(resid1, [(x2, D, 0), (mo, D, 0)], [gate1, norm2_g, shift2, scale2], [(D, F32), (D, BF16)], [], "resid1")
    cm = _Comm()
    hs = {"ff2": _ag_ici(cm, wblk["ff2"], waxis["ff2"], rows=(r2 // 2, r2), into=half["ff2"])}
    u, act = _mm(h2, wf["ff1"], "nn", (F32, BF16), "ff1", comm=cm, epi=lambda r: (r, jnp.square(jnp.maximum(r, 0.0))))
    half["ff2"] = cm.result(hs["ff2"])
    cm = _Comm()
    hs = {"ff2": _ag_d2d(cm, half["ff2"], waxis["ff2"])}
    _call(lambda: None, [], name="ag_d2d_ff2", out_shape=(), comm=cm)
    wf["ff2"] = cm.result(hs["ff2"])
    ff = _mm(act, wf["ff2"], "nn", F32, "ff2")

    def loss_fn(x1v, ffv, tv, g2):
        e = x1v + g2 * ffv - tv
        dy = e * (1.0 / D)
        return (dy, dy * g2), (jnp.sum(e * e, axis=0, keepdims=True), jnp.sum(dy * ffv, axis=0, keepdims=True))

    dy, d_ff, sq_sum, d_gate2 = _rowwise(loss_fn, [(x1, D, 0), (ff, D, 0), (tgt, D, 0)], [gate2],
                                         [(D, F32), (D, BF16)], [(1, D), (1, D)], "loss")
    loss = lax.psum(jnp.sum(sq_sum) * (0.5 / D), ("x", "y", "c"))

    owner_base = jnp.stack([me ^ r for r in CHIP_RELS]).astype(jnp.int32)
    gw, recv1, part, recv2 = {}, {}, {}, {}
    gw["ff2"] = _mm(act, d_ff, "tn", BF16, "dw_ff2")
    cm = _Comm()
    hh = _rs_d2d(cm, gw["ff2"], waxis["ff2"])
    d_u = _mm(d_ff, wf["ff2"], "nt", BF16, "d_act", comm=cm, extras=[u], epi=lambda r, uv: (r * (2.0 * jnp.maximum(uv, 0.0)),))
    part["ff2"] = _rs_add(gw["ff2"], cm.result(hh), waxis["ff2"], owner_base, "rs_add_ff2")
    rows_ff2 = part["ff2"].shape[1]
    cm = _Comm()
    hh = _rs_ici(cm, part["ff2"], rows=(0, rows_ff2 // 2))
    gw["ff1"] = _mm(h2, d_u, "tn", BF16, "dw_ff1", comm=cm)
    cm2 = _Comm()
    hh2 = _rs_ici(cm2, part["ff2"], rows=(rows_ff2 // 2, rows_ff2), recv=cm.result(hh))
    hh1 = _rs_d2d(cm2, gw["ff1"], waxis["ff1"])
    d_h2 = _mm(d_u, wf["ff1"], "nt", F32, "d_h2", comm=cm2)
    recv2["ff2"] = cm2.result(hh2)
    part["ff1"] = _rs_add(gw["ff1"], cm2.result(hh1), waxis["ff1"], owner_base, "rs_add_ff1")

    def norm2_bwd(dh2v, x1v, dyv, mov, g2n, sh, sc, g1):
        _, vjp = jax.vjp(_modnorm, x1v, g2n, sh, sc)
        dx, dg, dsh, dsc = vjp(dh2v)
        dx1 = dyv + dx
        return (dx1, dx1 * g1), (dg, dsh, dsc, jnp.sum(dx1 * mov, axis=0, keepdims=True))

    d_x1, d_mo, d_g2n, d_shift2, d_scale2, d_gate1 = _rowwise(
        norm2_bwd, [(d_h2, D, 0), (x1, D, 0), (dy, D, 0), (mo, D, 0)], [norm2_g, shift2, scale2, gate1],
        [(D, F32), (D, BF16)], [(1, D)] * 4, "norm2_bwd")
    gw["out"] = _mm(merged, d_mo, "tn", BF16, "dw_out")
    d_merged = _mm(d_mo, wf["out"], "nt", F32, "d_merged")

    def merge_bwd(dmv, bhv, bav, ghg, gat):
        _, vjp = jax.vjp(merge_fn, bhv, bav, ghg, gat)
        return vjp(dmv), ()

    d_bh, d_ba, d_ghg, d_gat = _rowwise(merge_bwd, [(d_merged, D, 0), (bh, D, 0), (ba, D, 0), (pg, D, GATE0), (pg, D, GATE0 + 1)], [],
                                        [(D, BF16)] * 4, [], "merge_bwd")
    gw["bhg"] = _mm(o_hg, d_bh, "tn", BF16, "dw_bhg")
    gw["bat"] = _mm(o_at, d_ba, "tn", BF16, "dw_bat")
    d_ohg = _mm(d_bh, wf["bhg"], "nt", F32, "d_ohg")
    d_oat = _mm(d_ba, wf["bat"], "nt", BF16, "d_oat")
    rows_ff1 = part["ff1"].shape[1]
    cut_ff1 = 3 * rows_ff1 // 8
    cm = _Comm()
    hf1 = _rs_ici(cm, part["ff1"], rows=(0, cut_ff1))
    d_hq, d_hf, d_hi, d_hg, d_lb, d_gout_h = _hgrn_bwd(p4, hg_lb_logits, hg_out_norm_g, s_all, d_ohg, H, comm=cm)
    cm2 = _Comm()
    hf1 = _rs_ici(cm2, part["ff1"], rows=(cut_ff1, rows_ff1), recv=cm.result(hf1))
    hh = {k: _rs_d2d(cm2, gw[k], waxis[k]) for k in small}
    d_aq, dkp, dvp, d_qg, d_kg, d_sk, d_bias = _attn_bwd(proj, at_cols, q_norm_g, k_norm_g, sinks3, bias,
                                                         d_oat, AH, KVH, comm=cm2)
    recv2["ff1"] = cm2.result(hf1)
    for k in small:
        part[k] = _rs_add(gw[k], cm2.result(hh[k]), waxis[k], owner_base, "rs_add_" + k)
    d_ak = dkp.astype(BF16)
    d_av = dvp.astype(BF16)
    d_proj = jnp.concatenate([d_hq, d_hf, d_hi, d_hg, d_aq, d_ak, d_av, d_ghg, d_gat], axis=1)
    cm = _Comm()
    hh = {k: _rs_ici(cm, part[k]) for k in small}
    gw_in = _mm(h, d_proj, "tn", BF16, "dw_in", comm=cm)
    for k in small:
        recv2[k] = cm.result(hh[k])

    wm = LANES * A
    cm = _Comm()
    hi_ = cm.inp(gw_in)
    h_main, h_mid = cm.out((4, D, wm), BF16), cm.out((4, D, LANES), BF16)
    for i, r in enumerate(CHIP_RELS):
        def main_view(ref, p, r=r):
            o = p["me"] ^ r ^ 1
            return ref.at[:, pl.ds(pl.multiple_of((PAIR * (o // 2) + (A + 1) * (1 - p["c"])) * LANES, LANES), wm)]

        def mid_view(ref, p, r=r):
            o = p["me"] ^ r
            return ref.at[:, pl.ds(pl.multiple_of((PAIR * (o // 2) + A) * LANES, LANES), LANES)]

        cm.copy(hi_, main_view, h_main, _slot_view(i), 1)
        cm.copy(hi_, mid_view, h_mid, _slot_view(i), 1)
    _call(lambda: None, [], name="rs_d2d_in", out_shape=(), comm=cm)
    chip = jnp.stack([(me ^ r) // 2 for r in CHIP_RELS]).astype(jnp.int32)
    part_main = _rs_add(gw_in, cm.result(h_main), 1, PAIR * chip + (A + 1) * cc, "rs_add_in_main", tw=LANES)
    part_mid = _rs_add(gw_in, cm.result(h_mid), 1, PAIR * chip + A, "rs_add_in_mid", tw=LANES)
    rs_in = _rs_split_start([part_main, part_mid], "rs_in_start")
    d_h = _mm(d_proj, w_in_full, "nt", F32, "d_h", tn=D, after=[rs_in["token"]])

    def norm1_bwd(dhv, xv, dx1v, g1n, sh, sc):
        _, vjp = jax.vjp(_modnorm, xv, g1n, sh, sc)
        dx, dg, dsh, dsc = vjp(dhv)
        return (dx1v + dx,), (dg, dsh, dsc)

    grad_x, d_g1n, d_shift1, d_scale1 = _rowwise(norm1_bwd, [(d_h, D, 0), (x2, D, 0), (d_x1, D, 0)],
                                                 [norm1_g, shift1, scale1], [(D, F32)], [(1, D)] * 3, "norm1_bwd")

    def sum4(p0, p1, p2, p3):
        return ((p0.astype(F32) + p1.astype(F32)) + p2.astype(F32)) + p3.astype(F32)

    def update_fn(w, m, v, p0, p1, p2, p3):
        g = sum4(p0, p1, p2, p3)
        delta, mn, vn = _adamw(w, g, m, v)
        return (g, delta, mn, vn), ()

    wmv = dict(zip(wnames, ((w_branch_hg, m_w_branch_hg, v_w_branch_hg), (w_branch_attn, m_w_branch_attn, v_w_branch_attn),
                            (w_out, m_w_out, v_w_out), (w_ff1, m_w_ff1, v_w_ff1), (w_ff2, m_w_ff2, v_w_ff2))))
    res = {}

    def update(k, p, rx):
        w, m, v = (t[0] for t in wmv[k])
        n = w.shape[1]
        ins = [(t, n, 0) for t in (w, m, v)] + [(p, n, 0, 0)] + [(rx, n, 0, i) for i in range(3)]
        res[k] = [t[None] for t in _rowwise(update_fn, ins, [], [(n, F32)] * 4, [], "update_" + k)]

    for k in wnames:
        update(k, part[k], recv2[k])
    (part_main, part_mid), (rx_main, rx_mid) = _rs_split_wait(rs_in, [grad_x] + [res[k][0] for k in wnames], "rs_in_wait")
    g_main, = _rowwise(lambda *p: ((sum4(*p),), ()), [(part_main, wm, 0, 0)] + [(rx_main, wm, 0, i) for i in range(3)], [],
                       [(wm, F32)], [], "sum_in_main")
    g_mid, = _rowwise(lambda *p: ((sum4(*p),), ()), [(part_mid, LANES, 0, 0)] + [(rx_mid, LANES, 0, i) for i in range(3)], [],
                      [(LANES, F32)], [], "sum_in_mid")
    g_in = jnp.where(cc == 0, jnp.concatenate([g_main, g_mid[:, :LANES // 2]], axis=1),
                     jnp.concatenate([g_mid[:, LANES // 2:], g_main], axis=1))

    def update_given(w, m, v, g):
        delta, mn, vn = _adamw(w, g, m, v)
        return (g, delta, mn, vn), ()

    res["in"] = [t[None] for t in _rowwise(update_given, [(t, BW, 0) for t in (w_in[0], m_w_in[0], v_w_in[0], g_in)], [],
                                           [(BW, F32)] * 4, [], "update_in")]

    d_sinks = d_sk.reshape(1, AH)
    (d_table_t,) = _whole(lambda db, bk: (_dot(db, _onehot(bk), NT, precision=HIGHEST),),
                          [d_bias.reshape(AH, AT_BLOCK * 2 * AT_BLOCK), bucket], [((AH, N_BUCKETS), F32)], "bias_bwd")
    smalls = [d_g1n, d_g2n, d_lb, d_gout_h, d_qg, d_kg, d_sinks, d_table_t.T.reshape(1, N_BUCKETS * AH)]
    widths = [s.shape[1] for s in smalls]
    lanes = [-(-w // LANES) * LANES for w in widths]
    smalls = [jnp.pad(s, ((0, 0), (0, p - w))) for s, w, p in zip(smalls, widths, lanes)]
    tail_row = jnp.concatenate([d_shift1, d_scale1, d_gate1, d_shift2, d_scale2, d_gate2] + smalls, axis=1)
    (tail_row,) = _behind([tail_row], [g_mid])
    tail_all = _gather_small(tail_row, me, "gather_tail")[:, 0, :]
    d_ada_all, packed = tail_all[:, :6 * D], tail_all[:, 6 * D:]
    d_ada_cols = lax.dynamic_slice(d_ada_all, (0, me * ADA_N), (N_DEV, ADA_N))

    def ada_update(cv, dav, w, m, v):
        g = _bdot(_silu(cv), dav, TN)
        delta, mn, vn = _adamw(w, g, m, v)
        return (g, delta, mn, vn), ()

    res["ada"] = [t[None] for t in _ada_update_call(ada_update, c_all, d_ada_cols, w_ada[0], m_w_ada[0], v_w_ada[0], _tile(D, 256, 16))]

    offs = [sum(lanes[:i]) for i in range(len(lanes))]

    def small_update(pk, dada, lg, *wmv_flat):
        tot = pk[0:1]
        for d in range(1, N_DEV):
            tot = tot + pk[d:d + 1]
        gb = dada[0:1]
        for d in range(1, N_DEV):
            gb = gb + dada[d:d + 1]
        gs = [tot[:, offs[i]:offs[i] + widths[i]] for i in range(len(widths))]
        _, lb_vjp = jax.vjp(_softmax0, lg)
        (g_lg,) = lb_vjp(gs[2])
        grads = [gb, gs[0], gs[1], g_lg, gs[3], gs[4], gs[5], gs[6], gs[7]]
        outs = []
        for i, g in enumerate(grads):
            w, m, v = wmv_flat[3 * i:3 * i + 3]
            delta, mn, vn = _adamw(w, g, m, v)
            outs += [g, delta, mn, vn]
        return tuple(outs)

    tbl = lambda t: t.reshape(1, N_BUCKETS * AH)
    small_wmv = [(b_ada, m_b_ada, v_b_ada), (norm1_g, m_norm1_g, v_norm1_g), (norm2_g, m_norm2_g, v_norm2_g),
                 (hg_lb_logits, m_hg_lb_logits, v_hg_lb_logits), (hg_out_norm_g, m_hg_out_norm_g, v_hg_out_norm_g),
                 (q_norm_g, m_q_norm_g, v_q_norm_g), (k_norm_g, m_k_norm_g, v_k_norm_g),
                 (attn_sinks, m_attn_sinks, v_attn_sinks),
                 (tbl(rel_bias_table), tbl(m_rel_bias_table), tbl(v_rel_bias_table))]
    flat = [t for trip in small_wmv for t in trip]
    out_shapes = [(trip[0].shape, F32) for trip in small_wmv for _ in range(4)]
    sres = _whole(small_update, [packed, d_ada_all, hg_lb_logits] + flat, out_shapes, "small_update")
    names_small = ("b_ada", "norm1_g", "norm2_g", "lb", "gout", "qg", "kg", "sinks", "table")
    for i, k in enumerate(names_small):
        r = sres[4 * i:4 * i + 4]
        if k == "table":
            r = [t.reshape(N_BUCKETS, AH) for t in r]
        res[k] = r

    order = ("ada", "b_ada", "norm1_g", "norm2_g", "in", "lb", "gout", "qg", "kg", "sinks", "table", "bhg", "bat", "out", "ff1", "ff2")
    outs = [loss, grad_x[None]]
    for j in range(4):
        outs += [res[k][j] for k in order]
    return tuple(outs)
```

```python
import functools
import math

import jax
import jax.numpy as jnp
from jax import lax
from jax.experimental import pallas as pl
from jax.experimental.pallas import tpu as pltpu

F32 = jnp.float32
BF16 = jnp.bfloat16
EPS = 1e-6
NEG_INF = -1e30
HG_DK = 128
HG_CHUNK = 64
AT_BLOCK = 128
N_BUCKETS = 32
MAX_EXACT = 16
MAX_DISTANCE = 128
N_DEV = 8
LANES = 128
VMEM_LIMIT = 56 * 1024 * 1024
ADAM_LR, ADAM_B1, ADAM_B2, ADAM_EPS, ADAM_WD, ADAM_STEP = 0.001, 0.9, 0.999, 1e-08, 0.01, 10
HIGHEST = lax.Precision.HIGHEST
MESH = pl.DeviceIdType.MESH
ANY = pl.BlockSpec(memory_space=pl.ANY)
CHIP_RELS = (0, 4, 2, 6)

NN = (((1,), (0,)), ((), ()))
NT = (((1,), (1,)), ((), ()))
TN = (((0,), (0,)), ((), ()))


def _tile(n, pref, unit):
    if n <= pref:
        return n
    t = (pref // unit) * unit
    while t >= unit:
        if n % t == 0:
            return t
        t -= unit
    return n


def _dot(a, b, dn, precision=None):
    return lax.dot_general(a, b, dn, preferred_element_type=F32, precision=precision)


def _bdot(a, b, dn):
    return _dot(a.astype(BF16), b.astype(BF16), dn)


def _position():
    x, y, c = lax.axis_index("x"), lax.axis_index("y"), lax.axis_index("c")
    return dict(x=x, y=y, c=c, me=4 * x + 2 * y + c)


def _peer_position(p, rel):
    x = 1 - p["x"] if rel & 4 else p["x"]
    y = 1 - p["y"] if rel & 2 else p["y"]
    c = 1 - p["c"] if rel & 1 else p["c"]
    return dict(x=x, y=y, c=c, me=4 * x + 2 * y + c)


class _Comm:
    def __init__(self):
        self.ins, self.outs, self.alias, self.plans, self.res = [], [], {}, [], None

    def inp(self, arr):
        self.ins.append(arr)
        return ("i", len(self.ins) - 1)

    def out(self, shape, dtype, alias=None):
        self.outs.append(jax.ShapeDtypeStruct(tuple(shape), dtype))
        if alias is not None:
            self.alias[alias[1]] = len(self.outs) - 1
        return ("o", len(self.outs) - 1)

    def copy(self, src, src_view, dst, dst_view, rel):
        self.plans.append((src, src_view, dst, dst_view, rel))

    def result(self, handle):
        return self.res[handle[1]]

    def build(self, in_refs, out_refs, send_sems, recv_sems):
        pos = _position()
        ref = lambda h: in_refs[h[1]] if h[0] == "i" else out_refs[h[1]]
        ops = []
        for k, (src, sv, dst, dv, rel) in enumerate(self.plans):
            s = sv(ref(src), pos)
            if rel == 0:
                cp = pltpu.make_async_copy(s, dv(ref(dst), pos), send_sems.at[k])
                ops.append((cp.start, cp.wait))
                continue
            peer = _peer_position(pos, rel)
            mk = lambda d: pltpu.make_async_remote_copy(
                src_ref=s, dst_ref=d, send_sem=send_sems.at[k], recv_sem=recv_sems.at[k],
                device_id=(peer["x"], peer["y"], peer["c"]), device_id_type=MESH)
            out_cp, in_cp = mk(dv(ref(dst), pos)), mk(dv(ref(dst), peer))

            def wait(out_cp=out_cp, in_cp=in_cp):
                out_cp.wait_send()
                in_cp.wait_recv()

            ops.append((out_cp.start, wait))
        return ops


def _call(body, args, *, name, out_shape, in_specs=None, out_specs=None, grid=None, scratch_shapes=(), comm=None,
          prefetch=None, aliases=None, after=()):
    single = not isinstance(out_shape, (tuple, list))
    out_shape = (out_shape,) if single else tuple(out_shape)
    n_in, n_out, n_scr = len(args), len(out_shape), len(scratch_shapes)
    vm = pl.BlockSpec(memory_space=pltpu.VMEM)
    in_specs = [vm] * n_in if in_specs is None else list(in_specs)
    out_specs = [vm] * n_out if out_specs is None else (list(out_specs) if isinstance(out_specs, (tuple, list)) else [out_specs])
    n_pf = 0 if prefetch is None else len(prefetch)
    kw = {} if aliases is None else {"input_output_aliases": dict(aliases)}
    if comm is None and after:
        n_dep = len(after)

        def fn(*refs):
            body(*refs[:n_pf + n_in], *refs[n_pf + n_in + n_dep:])

        all_args, all_scratch = list(args) + list(after), list(scratch_shapes)
        in_specs = in_specs + [ANY] * n_dep
    elif comm is None:
        fn = body
        all_args, all_scratch = list(args), list(scratch_shapes)
    else:
        n_ci, n_co, n_x = len(comm.ins), len(comm.outs), len(comm.plans)

        def fn(*refs):
            pf, refs = refs[:n_pf], refs[n_pf:]
            o_in, c_in = refs[:n_in], refs[n_in:n_in + n_ci]
            o_out = refs[n_in + n_ci:n_in + n_ci + n_out]
            c_out = refs[n_in + n_ci + n_out:n_in + n_ci + n_out + n_co]
            scr = refs[n_in + n_ci + n_out + n_co:]
            ops = comm.build(c_in, c_out, scr[n_scr], scr[n_scr + 1])
            if grid:
                first = functools.reduce(jnp.logical_and, [pl.program_id(i) == 0 for i in range(len(grid))])
                last = functools.reduce(jnp.logical_and, [pl.program_id(i) == g - 1 for i, g in enumerate(grid)])

                @pl.when(first)
                def _():
                    for start, _w in ops:
                        start()
            else:
                for start, _w in ops:
                    start()
            body(*pf, *o_in, *o_out, *scr[:n_scr])
            if grid:
                @pl.when(last)
                def _():
                    for _s, wait in ops:
                        wait()
            else:
                for _s, wait in ops:
                    wait()

        all_args = list(args) + list(comm.ins)
        in_specs = in_specs + [ANY] * n_ci
        out_shape = out_shape + tuple(comm.outs)
        out_specs = out_specs + [ANY] * n_co
        all_scratch = list(scratch_shapes) + [pltpu.SemaphoreType.DMA((n_x,)), pltpu.SemaphoreType.DMA((n_x,))]
        kw["input_output_aliases"] = {n_pf + n_in + i: n_out + o for i, o in comm.alias.items()}
    sem = None if grid is None else ("arbitrary",) * len(grid)
    params = pltpu.CompilerParams(dimension_semantics=sem, vmem_limit_bytes=VMEM_LIMIT)
    if prefetch is None:
        spec = dict(in_specs=in_specs, out_specs=tuple(out_specs), scratch_shapes=all_scratch)
        if grid is not None:
            spec["grid"] = grid
    else:
        spec = dict(grid_spec=pltpu.PrefetchScalarGridSpec(
            num_scalar_prefetch=n_pf, grid=grid, in_specs=in_specs, out_specs=tuple(out_specs), scratch_shapes=all_scratch))
        all_args = list(prefetch) + all_args
    res = pl.pallas_call(fn, name=name, out_shape=out_shape, compiler_params=params, **spec, **kw)(*all_args)
    res = list(res)
    if comm is not None:
        comm.res = res[n_out:]
        res = res[:n_out]
    return res[0] if single else res


def _whole_view(ref, pos):
    return ref


def _block_view(axis, n, index, rows=None):
    def view(ref, pos):
        off = pl.multiple_of(index(pos) * n, n)
        if rows is None:
            return ref.at[:, pl.ds(off, n)] if axis == 1 else ref.at[pl.ds(off, n), :]
        lo, cnt = rows[0], rows[1] - rows[0]
        if axis == 1:
            return ref.at[pl.ds(lo, cnt), pl.ds(off, n)]
        return ref.at[pl.ds(pl.multiple_of(off + lo, 16), cnt), :]
    return view


def _rows_view(rows):
    def view(ref, pos):
        return ref if rows is None else ref.at[pl.ds(rows[0], rows[1] - rows[0]), :]
    return view


def _slot_view(i, rows=None):
    def view(ref, pos):
        return ref.at[i] if rows is None else ref.at[i, pl.ds(rows[0], rows[1] - rows[0]), :]
    return view


def _exchange(items, name):
    cm = _Comm()
    for a, rel in items:
        cm.copy(cm.inp(a), _whole_view, cm.out(a.shape, a.dtype), _whole_view, rel)
    _call(lambda: None, [], name=name, out_shape=(), comm=cm)
    return cm.res


def _gather_small(v, me, name):
    cm = _Comm()
    hi, ho = cm.inp(v), cm.out((N_DEV,) + v.shape, v.dtype)
    for rel in range(N_DEV):
        cm.copy(hi, _whole_view, ho, lambda ref, p: ref.at[p["me"]], rel)
    _call(lambda: None, [], name=name, out_shape=(), comm=cm)
    return cm.result(ho)


def _ag_ici(cm, blk, axis, rows=None, into=None):
    n = blk.shape[axis]
    shape = list(blk.shape)
    shape[axis] = n * N_DEV
    hi = cm.inp(blk)
    ho = cm.out(shape, blk.dtype) if into is None else cm.out(shape, blk.dtype, alias=cm.inp(into))
    own = _block_view(axis, n, lambda p: p["me"], rows)
    for rel in CHIP_RELS:
        cm.copy(hi, _rows_view(rows), ho, own, rel)
    return ho


def _ag_d2d(cm, full, axis):
    n = full.shape[axis] // N_DEV
    hi = cm.inp(full)
    ho = cm.out(full.shape, full.dtype, alias=hi)
    for r in CHIP_RELS:
        v = _block_view(axis, n, functools.partial(lambda p, r: p["me"] ^ r, r=r))
        cm.copy(hi, v, ho, v, 1)
    return ho


def _rs_d2d(cm, gw, axis):
    n = gw.shape[axis] // N_DEV
    shape = list(gw.shape)
    shape[axis] = n
    hi, ho = cm.inp(gw), cm.out([4] + shape, gw.dtype)
    for i, r in enumerate(CHIP_RELS):
        cm.copy(hi, _block_view(axis, n, functools.partial(lambda p, r: p["me"] ^ r ^ 1, r=r)), ho, _slot_view(i), 1)
    return ho


def _rs_ici(cm, part, rows=None, recv=None):
    if recv is None:
        ho = cm.out((3,) + part.shape[1:], part.dtype)
    else:
        ho = cm.out(recv.shape, recv.dtype, alias=cm.inp(recv))
    hi = cm.inp(part)
    for i in (1, 2, 3):
        cm.copy(hi, _slot_view(i, rows), ho, _slot_view(i - 1, rows), CHIP_RELS[i])
    return ho


def _rs_add(gw, recv, axis, base, name, tw=None):
    _, R, n = recv.shape
    fan = 1
    if axis == 1:
        tw = n if tw is None else tw
        fan = max(f for f in (4, 3, 2, 1) if (n // tw) % f == 0)
        gw_specs = [pl.BlockSpec((R, tw), functools.partial(lambda i, t, b, k: (0, b[i] + fan * t + k), k=k)) for k in range(fan)]
        rv_spec = pl.BlockSpec((None, R, tw * fan), lambda i, t, b: (i, 0, t))
        grid = (4, n // (tw * fan))
    else:
        tw = _tile(n, 1024, LANES)
        gw_specs = [pl.BlockSpec((R, tw), lambda i, t, b: (b[i], t))]
        rv_spec = pl.BlockSpec((None, R, tw), lambda i, t, b: (i, 0, t))
        grid = (4, n // tw)

    def body(b_ref, *refs):
        g_refs, r_ref, o_ref = refs[:fan], refs[fan], refs[fan + 1]
        g = g_refs[0][...] if fan == 1 else jnp.concatenate([g[...] for g in g_refs], axis=1)
        o_ref[...] = (g.astype(F32) + r_ref[...].astype(F32)).astype(o_ref.dtype)

    return _call(body, [gw] * fan + [recv], name=name, out_shape=jax.ShapeDtypeStruct(recv.shape, recv.dtype), grid=grid,
                 in_specs=gw_specs + [rv_spec], out_specs=rv_spec, prefetch=[base])


HBM_SPEC = pl.BlockSpec(memory_space=pltpu.HBM)
SEM_SPEC = pl.BlockSpec(memory_space=pltpu.SEMAPHORE)
SPLIT_PARAMS = pltpu.CompilerParams(has_side_effects=pltpu.SideEffectType.DATAFLOW_SIDE_EFFECTING)


def _split_copies(refs, plans, send_sems, recv_sems):
    pos = _position()
    out = []
    for k, (si, sv, li, lv, rel) in enumerate(plans):
        peer = _peer_position(pos, rel)
        mk = lambda d: pltpu.make_async_remote_copy(
            src_ref=sv(refs[si], pos), dst_ref=d, send_sem=send_sems.at[k], recv_sem=recv_sems.at[k],
            device_id=(peer["x"], peer["y"], peer["c"]), device_id_type=MESH)
        out.append((mk(lv(refs[li], pos)), mk(lv(refs[li], peer))))
    return out


def _split_start(arrays, plans, name):
    n = len(arrays)

    def body(*refs):
        send_sems, recv_sems = refs[n], refs[n + 1]
        for out_cp, _ in _split_copies(refs[:n], plans, send_sems, recv_sems):
            out_cp.start()
        refs[-1][...] = jnp.zeros_like(refs[-1])

    sems = pltpu.SemaphoreType.DMA((len(plans),))
    res = pl.pallas_call(
        body, name=name,
        out_shape=(sems, sems) + tuple(pltpu.HBM(a.shape, a.dtype) for a in arrays) + (jax.ShapeDtypeStruct((8, LANES), F32),),
        in_specs=[HBM_SPEC] * n, out_specs=(SEM_SPEC, SEM_SPEC) + (HBM_SPEC,) * n + (pl.BlockSpec(memory_space=pltpu.VMEM),),
        input_output_aliases={i: 2 + i for i in range(n)}, compiler_params=SPLIT_PARAMS,
    )(*[pltpu.with_memory_space_constraint(a, pltpu.HBM) for a in arrays])
    return res[0], res[1], list(res[2:2 + n]), res[-1]


def _split_wait(send_sems, recv_sems, arrays, plans, after, name):
    n, na = len(arrays), len(after)

    def body(*refs):
        for out_cp, in_cp in _split_copies(refs[:n], plans, refs[n], refs[n + 1]):
            out_cp.wait_send()
            in_cp.wait_recv()

    res = pl.pallas_call(
        body, name=name, out_shape=tuple(pltpu.HBM(a.shape, a.dtype) for a in arrays),
        in_specs=[HBM_SPEC] * n + [SEM_SPEC, SEM_SPEC] + [ANY] * na, out_specs=(HBM_SPEC,) * n,
        input_output_aliases={i: i for i in range(n)}, compiler_params=SPLIT_PARAMS,
    )(*arrays, send_sems, recv_sems, *after)
    return list(res)


def _rs_split_start(parts, name):
    nw = len(parts)
    lands = [lax.empty((3,) + p.shape[1:], p.dtype) for p in parts]
    plans = [(s, _slot_view(i), nw + s, _slot_view(i - 1), CHIP_RELS[i]) for s in range(nw) for i in (1, 2, 3)]
    send_sems, recv_sems, arrays, token = _split_start(list(parts) + lands, plans, name)
    return dict(sems=(send_sems, recv_sems), arrays=arrays, plans=plans, token=token, nw=nw)


def _rs_split_wait(h, after, name):
    arrays = _split_wait(h["sems"][0], h["sems"][1], h["arrays"], h["plans"], after, name)
    return arrays[:h["nw"]], arrays[h["nw"]:]


def _behind(xs, tokens):
    out = lax.optimization_barrier((tuple(xs), tuple(tokens)))
    return list(out[0])


def _ag_w_in(src, a, D, INW):
    wm = LANES * a

    hd = D // 2
    ALL, TOP, BOT = (0, D), (0, hd), (hd, D)

    def main_place(ref, p, rows=ALL):
        off = pl.multiple_of(((2 * a + 1) * (p["me"] // 2) + (a + 1) * p["c"]) * LANES, LANES)
        return ref.at[pl.ds(rows[0], rows[1] - rows[0]), pl.ds(off, wm)]

    def main_src(ref, p):
        return ref.at[:, pl.ds(pl.multiple_of(p["c"] * LANES, LANES), wm)]

    def mid_src(ref, p):
        return ref.at[:, pl.ds(pl.multiple_of((1 - p["c"]) * wm, LANES), LANES)]

    def mid_place(ref, p, rows=ALL):
        return ref.at[p["me"], pl.ds(rows[0], rows[1] - rows[0]), :]

    def body(src_ref, full_ref, mid_ref, send_sems, recv_sems):
        pos = _position()
        sib, xn, yn = (_peer_position(pos, r) for r in (1, 4, 2))
        dg = _peer_position(pos, 6)
        started = []

        def remote(k, s, d, to):
            return pltpu.make_async_remote_copy(src_ref=s, dst_ref=d, send_sem=send_sems.at[k], recv_sem=recv_sems.at[k],
                                                device_id=(to["x"], to["y"], to["c"]), device_id_type=MESH)

        def send(k, owner, rows, to, from_src=False):
            for j, (src_v, place) in enumerate(((main_src, main_place), (mid_src, mid_place))):
                s = src_v(src_ref, pos) if from_src else place(full_ref if j == 0 else mid_ref, owner, rows)
                cp = remote(k + j, s, place(full_ref if j == 0 else mid_ref, owner, rows), to)
                cp.start()
                started.append(cp)

        def landed(k, owner, rows, frm):
            for j, place in enumerate((main_place, mid_place)):
                ref = full_ref if j == 0 else mid_ref
                remote(k + j, place(ref, owner, rows), place(ref, owner, rows), frm).wait_recv()

        local = [pltpu.make_async_copy(main_src(src_ref, pos), main_place(full_ref, pos), send_sems.at[18]),
                 pltpu.make_async_copy(mid_src(src_ref, pos), mid_place(mid_ref, pos), send_sems.at[19])]
        for cp in local:
            cp.start()
        send(0, pos, ALL, sib, from_src=True)
        send(2, pos, ALL, xn, from_src=True)
        send(4, pos, ALL, yn, from_src=True)
        landed(2, xn, ALL, xn)
        send(10, xn, ALL, sib)
        send(6, xn, TOP, yn)
        landed(4, yn, ALL, yn)
        send(12, yn, ALL, sib)
        send(8, yn, BOT, xn)
        landed(6, dg, TOP, yn)
        send(14, dg, TOP, sib)
        landed(8, dg, BOT, xn)
        send(16, dg, BOT, sib)
        sib_of = lambda p: _peer_position(p, 1)
        landed(0, sib, ALL, sib)
        landed(10, sib_of(xn), ALL, sib)
        landed(12, sib_of(yn), ALL, sib)
        landed(14, sib_of(dg), TOP, sib)
        landed(16, sib_of(dg), BOT, sib)
        for cp in started:
            cp.wait_send()
        for cp in local:
            cp.wait()

    return _call(body, [src], name="ag_w_in", in_specs=[ANY], out_specs=[ANY, ANY],
                 out_shape=(jax.ShapeDtypeStruct((D, INW), BF16), jax.ShapeDtypeStruct((N_DEV, D, LANES), BF16)),
                 scratch_shapes=[pltpu.SemaphoreType.DMA((20,)), pltpu.SemaphoreType.DMA((20,))])


def _patch_mid(full, mid, a):
    D = full.shape[0]

    def body(full_ref, e_ref, o_ref, out_ref):
        out_ref[...] = e_ref[...] + o_ref[...]

    return _call(body, [full, mid, mid], name="patch_mid", grid=(N_DEV // 2,),
                 out_shape=jax.ShapeDtypeStruct(full.shape, full.dtype),
                 in_specs=[ANY, pl.BlockSpec((None, D, LANES), lambda j: (2 * j, 0, 0)),
                           pl.BlockSpec((None, D, LANES), lambda j: (2 * j + 1, 0, 0))],
                 out_specs=pl.BlockSpec((D, LANES), lambda j: (0, (2 * a + 1) * j + a)), aliases={0: 0})


MM_RESIDENT = 2048


def _mm(a, b, mode, out_dtype, name, b_off=0, n=None, comm=None, extras=(), epi=None, tn=None, after=(), b_order=None):
    if mode == "nn":
        (M, K), (K2, N) = a.shape, b.shape
    elif mode == "nt":
        (M, K), (N, K2) = a.shape, b.shape
    else:
        (K, M), (K2, N) = a.shape, b.shape
    assert K == K2, (a.shape, b.shape, mode)
    if n is not None:
        N = n
    single = not isinstance(out_dtype, (tuple, list))
    out_dtypes = (out_dtype,) if single else tuple(out_dtype)
    if epi is None:
        epi = lambda r: (r,)
    tk = K if K <= MM_RESIDENT else (MM_RESIDENT if K % MM_RESIDENT == 0 else _tile(K, 512, LANES))
    nk = K // tk
    if M > MM_RESIDENT and mode == "tn" and N <= MM_RESIDENT and not b_off:
        tm, tn = _tile(M, 512, LANES), N
    elif nk > 1:
        tm, tn = _tile(M, 1024, LANES), _tile(N, tn or 1024, LANES)
    else:
        tm = _tile(M, MM_RESIDENT, LANES)
        tn = _tile(math.gcd(N, b_off) if b_off else N, tn or 512, LANES)
    jb = b_off // tn
    dn = {"nn": NN, "nt": NT, "tn": TN}[mode]
    ne, no = len(extras), len(out_dtypes)

    def body(a_ref, b_ref, *rest):
        e_refs, o_refs = rest[:ne], rest[ne:ne + no]

        def finish(r):
            for o_ref, v in zip(o_refs, epi(r, *[e[...] for e in e_refs])):
                o_ref[...] = v.astype(o_ref.dtype)

        if nk == 1:
            finish(_bdot(a_ref[...], b_ref[...], dn))
            return
        acc_ref = rest[ne + no]
        k = pl.program_id(2)

        @pl.when(k == 0)
        def _():
            acc_ref[...] = _bdot(a_ref[...], b_ref[...], dn)

        @pl.when(jnp.logical_and(k > 0, k < nk - 1))
        def _():
            acc_ref[...] += _bdot(a_ref[...], b_ref[...], dn)

        @pl.when(k == nk - 1)
        def _():
            finish(acc_ref[...] + _bdot(a_ref[...], b_ref[...], dn))

    a_spec = pl.BlockSpec((tk, tm), lambda i, j, k: (k, i)) if mode == "tn" else pl.BlockSpec((tm, tk), lambda i, j, k: (i, k))
    col = (lambda j: j + jb) if b_order is None else functools.partial(b_order, tn)
    b_spec = pl.BlockSpec((tn, tk), lambda i, j, k: (j, k)) if mode == "nt" else pl.BlockSpec((tk, tn), lambda i, j, k: (k, col(j)))
    o_spec = pl.BlockSpec((tm, tn), lambda i, j, k: (i, j))
    res = _call(body, [a, b] + list(extras), name=name, grid=(M // tm, N // tn, nk),
                out_shape=tuple(jax.ShapeDtypeStruct((M, N), dt) for dt in out_dtypes),
                in_specs=[a_spec, b_spec] + [o_spec] * ne, out_specs=[o_spec] * no,
                scratch_shapes=[pltpu.VMEM((tm, tn), F32)] if nk > 1 else [], comm=comm, after=after)
    return res[0] if single else res


def _rowwise(fn, row_ins, bcast_ins, row_outs, acc_outs, name, rt=256, comm=None):
    L = row_ins[0][0].shape[-2]
    rt = _tile(L, rt, 16)
    nr, nb, no = len(row_ins), len(bcast_ins), len(row_outs)

    def body(*refs):
        i = pl.program_id(0)
        vals = [r[...] for r in refs[:nr + nb]]
        outs, accs = fn(*vals)
        for r, v in zip(refs[nr + nb:nr + nb + no], outs):
            r[...] = v.astype(r.dtype)
        acc_refs = refs[nr + nb + no:]

        @pl.when(i == 0)
        def _():
            for r in acc_refs:
                r[...] = jnp.zeros_like(r)

        for r, v in zip(acc_refs, accs):
            r[...] += v

    in_specs = []
    for spec in row_ins:
        w, cb = spec[1], spec[2]
        if len(spec) == 4:
            in_specs.append(pl.BlockSpec((None, rt, w), functools.partial(lambda i, cb, ld: (ld, i, cb), cb=cb, ld=spec[3])))
        else:
            in_specs.append(pl.BlockSpec((rt, w), functools.partial(lambda i, cb: (i, cb), cb=cb)))
    in_specs += [pl.BlockSpec(b.shape, lambda i: (0, 0)) for b in bcast_ins]
    out_specs = [pl.BlockSpec((rt, w), lambda i: (i, 0)) for w, _ in row_outs]
    out_specs += [pl.BlockSpec(s, lambda i: (0, 0)) for s in acc_outs]
    out_shape = [jax.ShapeDtypeStruct((L, w), dt) for w, dt in row_outs] + [jax.ShapeDtypeStruct(s, F32) for s in acc_outs]
    return _call(body, [s[0] for s in row_ins] + list(bcast_ins), name=name, grid=(L // rt,), out_shape=tuple(out_shape),
                 in_specs=in_specs, out_specs=out_specs, comm=comm)


def _whole(fn, ins, out_shapes, name):
    def body(*refs):
        outs = fn(*[r[...] for r in refs[:len(ins)]])
        for r, v in zip(refs[len(ins):], outs):
            r[...] = v.astype(r.dtype)

    return _call(body, list(ins), name=name, out_shape=tuple(jax.ShapeDtypeStruct(s, dt) for s, dt in out_shapes))


def _silu(x):
    return x * jax.nn.sigmoid(x)


def _rms(x, g):
    return (x * lax.rsqrt(jnp.mean(x * x, axis=-1, keepdims=True) + EPS)) * g


def _modnorm(x, g, shift, scale):
    return _rms(x, g) * (1.0 + scale) + shift


def _adamw(w, g, m, v):
    m = ADAM_B1 * m + (1.0 - ADAM_B1) * g
    v = ADAM_B2 * v + (1.0 - ADAM_B2) * jnp.square(g)
    m_hat = m / (1.0 - ADAM_B1 ** ADAM_STEP)
    v_hat = v / (1.0 - ADAM_B2 ** ADAM_STEP)
    delta = -ADAM_LR * (m_hat / (jnp.sqrt(v_hat) + ADAM_EPS) + ADAM_WD * w)
    return delta, m, v


def _lower_bound(lg):
    e = jnp.exp(lg - jnp.max(lg, axis=0, keepdims=True))
    return e[0:1] / jnp.sum(e, axis=0, keepdims=True)


def _hg_stages(hq_l, hf_l, hi_l, lb):
    C = hq_l[0].shape[0]
    row = lax.broadcasted_iota(jnp.int32, (C, C), 0)
    col = lax.broadcasted_iota(jnp.int32, (C, C), 1)
    tri = row >= col
    trif = tri.astype(F32)
    f_l = [lb + (1.0 - lb) * jax.nn.sigmoid(hf) for hf in hf_l]
    b_l = [_dot(trif, jnp.log(f), NN, precision=HIGHEST) for f in f_l]
    q_l = [_silu(hq) for hq in hq_l]
    m_l = [b[C // 2 - 1:C // 2] for b in b_l]
    bl_l = [b[C - 1:C] for b in b_l]
    sc_l = [jnp.where(tri, _bdot(q * jnp.exp(b - m), (1.0 - f) * jnp.exp(m - b), NT), 0.0)
            for q, f, b, m in zip(q_l, f_l, b_l, m_l)]
    o1_l = [_bdot(sc, hi, NN) for sc, hi in zip(sc_l, hi_l)]
    u_l = [_bdot(hi, (1.0 - f) * jnp.exp(bl - b), TN) for hi, f, b, bl in zip(hi_l, f_l, b_l, bl_l)]
    qb_l = [q * jnp.exp(b) for q, b in zip(q_l, b_l)]
    dec_l = [jnp.exp(bl) for bl in bl_l]
    return list(zip(o1_l, u_l, qb_l, dec_l))


def _hg_out(o, hgate, gout):
    return _rms(o, gout) * _silu(hgate)


HG_STAGE = 8
HG_GROUP = 32


def _hgrn_fwd(p4, lb_logits, gout, H, comm=None):
    L = p4.shape[0]
    C = HG_CHUNK
    GR = _tile(L // C, HG_GROUP, 1)
    T = GR * C
    N = L // T

    def body(hq_ref, hf_ref, hi_ref, hg_ref, lg_ref, gout_ref, o_ref, s_ref, st_ref):
        @pl.when(pl.program_id(1) == 0)
        def _():
            st_ref[...] = jnp.zeros_like(st_ref)

        lb = _lower_bound(lg_ref[...])
        st = st_ref[...]
        for c0 in range(0, GR, HG_STAGE):
            rows_l = [pl.ds(ci * C, C) for ci in range(c0, min(c0 + HG_STAGE, GR))]
            parts = _hg_stages([hq_ref[r, :] for r in rows_l], [hf_ref[r, :] for r in rows_l],
                               [hi_ref[r, :] for r in rows_l], lb)
            for ci, rows, (o1, u, qb, dec) in zip(range(c0, GR), rows_l, parts):
                s_ref[0, ci] = st
                o = o1 + _bdot(qb, st, NT)
                st = st * dec + u
                o_ref[rows, :] = _hg_out(o, hg_ref[rows, :], gout_ref[...]).astype(o_ref.dtype)
        st_ref[...] = st

    blk = lambda s: pl.BlockSpec((T, HG_DK), functools.partial(lambda h, n, s: (n, s * H + h), s=s))
    return _call(
        body, [p4, p4, p4, p4, lb_logits, gout], name="hgrn_fwd", grid=(H, N),
        out_shape=(jax.ShapeDtypeStruct((L, H * HG_DK), BF16), jax.ShapeDtypeStruct((H, N * GR, HG_DK, HG_DK), F32)),
        in_specs=[blk(0), blk(1), blk(2), blk(3), pl.BlockSpec((2, HG_DK), lambda h, n: (0, h)),
                  pl.BlockSpec((1, HG_DK), lambda h, n: (0, 0))],
        out_specs=(pl.BlockSpec((T, HG_DK), lambda h, n: (n, h)),
                   pl.BlockSpec((1, GR, HG_DK, HG_DK), lambda h, n: (h, n, 0, 0))),
        scratch_shapes=[pltpu.VMEM((HG_DK, HG_DK), F32)], comm=comm)


def _hgrn_bwd(p4, lb_logits, gout, s_all, d_out, H, comm=None):
    L = p4.shape[0]
    C = HG_CHUNK
    GR = _tile(L // C, HG_GROUP, 1)
    T = GR * C
    N = L // T

    def body(hq_ref, hf_ref, hi_ref, hg_ref, lg_ref, gout_ref, s_ref, do_ref,
             dq_ref, df_ref, di_ref, dg_ref, dlb_ref, dgo_ref, dst_ref):
        @pl.when(pl.program_id(1) == 0)
        def _():
            dst_ref[...] = jnp.zeros_like(dst_ref)
            dlb_ref[...] = jnp.zeros_like(dlb_ref)

        @pl.when(jnp.logical_and(pl.program_id(0) == 0, pl.program_id(1) == 0))
        def _():
            dgo_ref[...] = jnp.zeros_like(dgo_ref)

        lb = _lower_bound(lg_ref[...])
        dst = dst_ref[...]
        d_lb = jnp.zeros((1, HG_DK), F32)
        d_go = jnp.zeros((1, HG_DK), F32)
        for c0 in reversed(range(0, GR, HG_STAGE)):
            dst, d_lb_c, d_go_c = chunks_bwd(list(range(c0, min(c0 + HG_STAGE, GR))), lb, dst, hq_ref, hf_ref, hi_ref,
                                             hg_ref, gout_ref, s_ref, do_ref, dq_ref, df_ref, di_ref, dg_ref)
            d_lb += d_lb_c
            d_go += d_go_c
        dst_ref[...] = dst
        dlb_ref[...] += d_lb
        dgo_ref[...] += d_go

    def chunks_bwd(idx, lb, dst, hq_ref, hf_ref, hi_ref, hg_ref, gout_ref, s_ref, do_ref, dq_ref, df_ref, di_ref, dg_ref):
        n = len(idx)
        rows_l = [pl.ds(ci * C, C) for ci in idx]
        hq_l, hf_l, hi_l = ([r[rows, :] for rows in rows_l] for r in (hq_ref, hf_ref, hi_ref))
        st_l = [s_ref[0, ci] for ci in idx]
        row = lax.broadcasted_iota(jnp.int32, (C, C), 0)
        col = lax.broadcasted_iota(jnp.int32, (C, C), 1)
        tri = row >= col
        trif = tri.astype(F32)
        every = lambda fn, *ls: [fn(*a) for a in zip(*ls)]
        sg_l = every(jax.nn.sigmoid, hf_l)
        f_l = every(lambda sg: lb + (1.0 - lb) * sg, sg_l)
        b_l = every(lambda f: _dot(trif, jnp.log(f), NN, precision=HIGHEST), f_l)
        q_l = every(_silu, hq_l)
        m_l = every(lambda b: b[C // 2 - 1:C // 2], b_l)
        bl_l = every(lambda b: b[C - 1:C], b_l)
        e_qm_l = every(lambda b, m: jnp.exp(b - m), b_l, m_l)
        e_km_l = every(lambda b, m: jnp.exp(m - b), b_l, m_l)
        e_kl_l = every(lambda b, bl: jnp.exp(bl - b), b_l, bl_l)
        e_q_l = every(jnp.exp, b_l)
        dec_l = every(jnp.exp, bl_l)
        qe_l = every(lambda q, e: q * e, q_l, e_qm_l)
        ke_l = every(lambda f, e: (1.0 - f) * e, f_l, e_km_l)
        kd_l = every(lambda f, e: (1.0 - f) * e, f_l, e_kl_l)
        qb_l = every(lambda q, e: q * e, q_l, e_q_l)
        sc_l = every(lambda qe, ke: jnp.where(tri, _bdot(qe, ke, NT), 0.0), qe_l, ke_l)
        o_l = every(lambda sc, hi, qb, st: _bdot(sc, hi, NN) + _bdot(qb, st, NT), sc_l, hi_l, qb_l, st_l)
        vj_l = every(lambda o, rows: jax.vjp(_hg_out, o, hg_ref[rows, :], gout_ref[...])[1](do_ref[rows, :]), o_l, rows_l)
        do_l = [v[0] for v in vj_l]
        dsc_l = every(lambda do, hi: jnp.where(tri, _bdot(do, hi, NT), 0.0), do_l, hi_l)
        dv1_l = every(lambda sc, do: _bdot(sc, do, TN), sc_l, do_l)
        dqe_l = every(lambda dsc, ke: _bdot(dsc, ke, NN), dsc_l, ke_l)
        dke_l = every(lambda dsc, qe: _bdot(dsc, qe, TN), dsc_l, qe_l)
        dqb_l = every(lambda do, st: _bdot(do, st, NN), do_l, st_l)
        own_l = every(lambda do, qb: _bdot(do, qb, TN), do_l, qb_l)
        dst_next_l = [None] * n
        for j in reversed(range(n)):
            dst_next_l[j] = dst
            dst = own_l[j] + dst * dec_l[j]
        dv_l = every(lambda dv1, kd, dn: dv1 + _bdot(kd, dn, NT), dv1_l, kd_l, dst_next_l)
        dkd_l = every(lambda hi, dn: _bdot(hi, dn, NN), hi_l, dst_next_l)
        ddec_l = every(lambda dn, st: jnp.sum(dn * st, axis=0, keepdims=True), dst_next_l, st_l)
        rowi = lax.broadcasted_iota(jnp.int32, (C, HG_DK), 0)
        tq_l = every(lambda a, b_: a * b_, dqe_l, qe_l)
        tk_l = every(lambda a, b_: a * b_, dke_l, ke_l)
        td_l = every(lambda a, b_: a * b_, dkd_l, kd_l)
        tb_l = every(lambda a, b_: a * b_, dqb_l, qb_l)
        db_l = every(lambda tq, tk, td, tb, ddec, dec: tq - tk - td + tb
                     + jnp.where(rowi == C // 2 - 1, jnp.sum(tk - tq, axis=0, keepdims=True), 0.0)
                     + jnp.where(rowi == C - 1, jnp.sum(td, axis=0, keepdims=True) + ddec * dec, 0.0),
                     tq_l, tk_l, td_l, tb_l, ddec_l, dec_l)
        dlf_l = every(lambda db: _dot(trif, db, TN, precision=HIGHEST), db_l)
        dk_l = every(lambda dke, e1, dkd, e2: dke * e1 + dkd * e2, dke_l, e_km_l, dkd_l, e_kl_l)
        df_l = every(lambda dlf, f, dk: dlf / f - dk, dlf_l, f_l, dk_l)
        d_lb = jnp.zeros((1, HG_DK), F32)
        d_go = jnp.zeros((1, HG_DK), F32)
        for j, rows in enumerate(rows_l):
            sg, hq = sg_l[j], hq_l[j]
            df_ref[rows, :] = (df_l[j] * (1.0 - lb) * sg * (1.0 - sg)).astype(df_ref.dtype)
            sq = jax.nn.sigmoid(hq)
            dq = dqe_l[j] * e_qm_l[j] + dqb_l[j] * e_q_l[j]
            dq_ref[rows, :] = (dq * (sq * (1.0 + hq * (1.0 - sq)))).astype(dq_ref.dtype)
            di_ref[rows, :] = dv_l[j].astype(di_ref.dtype)
            dg_ref[rows, :] = vj_l[j][1].astype(dg_ref.dtype)
            d_lb += jnp.sum(df_l[j] * (1.0 - sg), axis=0, keepdims=True)
            d_go += vj_l[j][2]
        return dst, d_lb, d_go

    blk = lambda s: pl.BlockSpec((T, HG_DK), functools.partial(lambda h, n, s: (N - 1 - n, s * H + h), s=s))
    oblk = pl.BlockSpec((T, HG_DK), lambda h, n: (N - 1 - n, h))
    vec = pl.BlockSpec((1, HG_DK), lambda h, n: (0, h))
    W = H * HG_DK
    return _call(
        body, [p4, p4, p4, p4, lb_logits, gout, s_all, d_out], name="hgrn_bwd", grid=(H, N),
        out_shape=tuple([jax.ShapeDtypeStruct((L, W), BF16)] * 4 + [jax.ShapeDtypeStruct((1, W), F32), jax.ShapeDtypeStruct((1, HG_DK), F32)]),
        in_specs=[blk(0), blk(1), blk(2), blk(3), pl.BlockSpec((2, HG_DK), lambda h, n: (0, h)),
                  pl.BlockSpec((1, HG_DK), lambda h, n: (0, 0)),
                  pl.BlockSpec((1, GR, HG_DK, HG_DK), lambda h, n: (h, N - 1 - n, 0, 0)), oblk],
        out_specs=(oblk, oblk, oblk, oblk, vec, pl.BlockSpec((1, HG_DK), lambda h, n: (0, 0))),
        scratch_shapes=[pltpu.VMEM((HG_DK, HG_DK), F32)], comm=comm)


def _bucket_ids():
    i = jnp.arange(AT_BLOCK, dtype=jnp.int32)[:, None]
    j = jnp.arange(2 * AT_BLOCK, dtype=jnp.int32)[None, :]
    n = jnp.maximum(i - j + AT_BLOCK, 0)
    nf = jnp.maximum(n, 1).astype(F32)
    large = MAX_EXACT + (jnp.log(nf / MAX_EXACT) / math.log(MAX_DISTANCE / MAX_EXACT) * (N_BUCKETS - MAX_EXACT)).astype(jnp.int32)
    large = jnp.minimum(large, N_BUCKETS - 1)
    return jnp.where(n < MAX_EXACT, n, large).reshape(1, -1)


def _onehot(bucket):
    ids = lax.broadcasted_iota(jnp.int32, (N_BUCKETS, bucket.shape[1]), 0)
    return (ids == bucket).astype(F32)


AT_PAIR = 2


def _attn_probs(qn_l, kn_l, bias_g, sink, first, scale):
    rows = qn_l[0].shape[0]
    i = jnp.bitwise_and(lax.broadcasted_iota(jnp.int32, (rows, AT_BLOCK), 0), AT_BLOCK - 1)
    j = lax.broadcasted_iota(jnp.int32, (rows, AT_BLOCK), 1)
    n = len(qn_l)
    lp_l = [_bdot(qn_l[s], kn_l[s], NT) * scale + bias_g[:, :AT_BLOCK] for s in range(n)]
    lc_l = [_bdot(qn_l[s], kn_l[s + 1], NT) * scale + bias_g[:, AT_BLOCK:] for s in range(n)]
    seen = [jnp.logical_and(j > i, jnp.logical_not(first))] + [j > i] * (n - 1)
    lp_l = [jnp.where(seen[s], lp_l[s], NEG_INF) for s in range(n)]
    lc_l = [jnp.where(j <= i, lc, NEG_INF) for lc in lc_l]
    m_l = [jnp.maximum(jnp.maximum(jnp.max(lp, axis=-1, keepdims=True), jnp.max(lc, axis=-1, keepdims=True)), sink)
           for lp, lc in zip(lp_l, lc_l)]
    pp_l = [jnp.exp(lp - m) for lp, m in zip(lp_l, m_l)]
    pc_l = [jnp.exp(lc - m) for lc, m in zip(lc_l, m_l)]
    ps_l = [jnp.exp(sink - m) for m in m_l]
    den_l = [jnp.sum(pp, axis=-1, keepdims=True) + jnp.sum(pc, axis=-1, keepdims=True) + ps
             for pp, pc, ps in zip(pp_l, pc_l, ps_l)]
    return [(pp / den, pc / den, ps / den) for pp, pc, ps, den in zip(pp_l, pc_l, ps_l, den_l)]


def _sink_rows(sk_ref, G):
    head = lax.broadcasted_iota(jnp.int32, (G * AT_BLOCK, 1), 0) // AT_BLOCK
    sink = jnp.zeros((G * AT_BLOCK, 1), F32)
    for g in range(G):
        sink = jnp.where(head == g, sk_ref[0, g:g + 1, :], sink)
    return sink


def _group_rows(ref, s, G, DH):
    B = AT_BLOCK
    rows = ref[pl.ds(s * B, B), :].astype(F32)
    return jnp.concatenate([rows[:, g * DH:(g + 1) * DH] for g in range(G)], axis=0)


def _ungroup_rows(val, G):
    B = AT_BLOCK
    return jnp.concatenate([val[g * B:(g + 1) * B] for g in range(G)], axis=1)


def _attn_specs(cols, G, DH):
    P, B = AT_PAIR, AT_BLOCK
    pk = _heads_per_tile(DH)
    q0, k0, v0 = cols[0] // (G * DH), cols[1] // (pk * DH), cols[2] // (pk * DH)
    assert cols[0] % (G * DH) == 0 and cols[1] % (pk * DH) == 0 and cols[2] % (pk * DH) == 0 and (G * DH) % LANES == 0
    qblk = pl.BlockSpec((P * B, G * DH), lambda h, m: (m, q0 + h))
    kblk = lambda c0, off: pl.BlockSpec((B, pk * DH), functools.partial(
        lambda h, m, c0, off: (jnp.maximum(P * m + off - 1, 0), c0 + h // pk), c0=c0, off=off))
    return qblk, [kblk(k0, off) for off in range(P + 1)], [kblk(v0, off) for off in range(P + 1)]


def _heads_per_tile(DH):
    return LANES // DH if DH < LANES else 1


def _my_head(ref, DH):
    pk = _heads_per_tile(DH)
    val = ref[...]
    if pk == 1:
        return val
    sub = pl.program_id(0) % pk
    out = val[:, :DH]
    for j in range(1, pk):
        out = jnp.where(sub == j, val[:, j * DH:(j + 1) * DH], out)
    return out


def _to_my_head(val, DH):
    pk = _heads_per_tile(DH)
    if pk == 1:
        return val
    sub = pl.program_id(0) % pk
    wide = jnp.concatenate([val] * pk, axis=1)
    lane = lax.broadcasted_iota(jnp.int32, wide.shape, 1)
    return jnp.where(lane // DH == sub, wide, 0.0)


def _attn_fwd(proj, cols, qg, kg, sinks, bias, AH, KVH, comm=None):
    L, DH = proj.shape[0], qg.shape[1]
    G = AH // KVH
    NB = L // AT_BLOCK
    scale = DH ** -0.5

    P, B = AT_PAIR, AT_BLOCK
    assert NB % P == 0

    def body(q_ref, *rest):
        k_refs, v_refs = rest[:P + 1], rest[P + 1:2 * P + 2]
        qg_ref, kg_ref, sk_ref, b_ref, o_ref = rest[2 * P + 2:]
        first = pl.program_id(1) == 0
        kn_l = [_rms(_my_head(r, DH), kg_ref[...]) for r in k_refs]
        v_l = [_my_head(r, DH) for r in v_refs]
        qn_l = [_rms(_group_rows(q_ref, s, G, DH), qg_ref[...]) for s in range(P)]
        probs = _attn_probs(qn_l, kn_l, b_ref[...].reshape(G * B, 2 * B), _sink_rows(sk_ref, G), first, scale)
        o_l = [_bdot(pp, v_l[s], NN) + _bdot(pc, v_l[s + 1], NN) for s, (pp, pc, _) in enumerate(probs)]
        for s, o in enumerate(o_l):
            o_ref[pl.ds(s * B, B), :] = _ungroup_rows(o, G).astype(o_ref.dtype)

    qblk, kspecs, vspecs = _attn_specs(cols, G, DH)
    return _call(
        body, [proj] * (2 * P + 3) + [qg, kg, sinks, bias], name="attn_fwd", grid=(KVH, NB // P),
        out_shape=jax.ShapeDtypeStruct((L, AH * DH), BF16),
        in_specs=[qblk] + kspecs + vspecs
        + [pl.BlockSpec((1, DH), lambda h, m: (0, 0)), pl.BlockSpec((1, DH), lambda h, m: (0, 0)),
           pl.BlockSpec((1, G, 1), lambda h, m: (h, 0, 0)), pl.BlockSpec((G, B, 2 * B), lambda h, m: (h, 0, 0))],
        out_specs=pl.BlockSpec((P * B, G * DH), lambda h, m: (m, h)), comm=comm)


def _attn_bwd(proj, cols, qg, kg, sinks, bias, d_o, AH, KVH, comm=None):
    L, DH = proj.shape[0], qg.shape[1]
    G = AH // KVH
    NB = L // AT_BLOCK
    B = AT_BLOCK
    scale = DH ** -0.5

    P = AT_PAIR
    assert NB % P == 0

    def body(q_ref, *rest):
        k_refs, v_refs = rest[:P + 1], rest[P + 1:2 * P + 2]
        qg_ref, kg_ref, sk_ref, b_ref, do_ref, dq_ref, dk_ref, dv_ref, dqg_ref, dkg_ref, dsk_ref, db_ref = rest[2 * P + 2:]
        m = pl.program_id(1)
        first = m == 0

        @pl.when(first)
        def _():
            for r in (dsk_ref, db_ref):
                r[...] = jnp.zeros_like(r)

        @pl.when(jnp.logical_and(first, pl.program_id(0) % _heads_per_tile(DH) == 0))
        def _():
            for r in (dk_ref, dv_ref):
                r[...] = jnp.zeros_like(r)

        @pl.when(jnp.logical_and(first, pl.program_id(0) == 0))
        def _():
            dqg_ref[...] = jnp.zeros_like(dqg_ref)
            dkg_ref[...] = jnp.zeros_like(dkg_ref)

        kgv, qgv = kg_ref[...], qg_ref[...]
        k_fw = [jax.vjp(_rms, _my_head(r, DH), kgv) for r in k_refs]
        v_l = [_my_head(r, DH) for r in v_refs]
        kn_l = [f[0] for f in k_fw]
        q_fw = [jax.vjp(_rms, _group_rows(q_ref, s, G, DH), qgv) for s in range(P)]
        qn_l = [f[0] for f in q_fw]
        probs = _attn_probs(qn_l, kn_l, b_ref[...].reshape(G * B, 2 * B), _sink_rows(sk_ref, G), first, scale)
        pp_l, pc_l, ps_l = ([p[t] for p in probs] for t in range(3))
        do_l = [_group_rows(do_ref, s, G, DH).astype(BF16) for s in range(P)]
        dvp_l = [_bdot(pp, do, TN) for pp, do in zip(pp_l, do_l)]
        dvc_l = [_bdot(pc, do, TN) for pc, do in zip(pc_l, do_l)]
        dpp_l = [_bdot(do_l[s], v_l[s], NT) for s in range(P)]
        dpc_l = [_bdot(do_l[s], v_l[s + 1], NT) for s in range(P)]
        dsum_l = [jnp.sum(dpp * pp, axis=-1, keepdims=True) + jnp.sum(dpc * pc, axis=-1, keepdims=True)
                  for dpp, pp, dpc, pc in zip(dpp_l, pp_l, dpc_l, pc_l)]
        dlp_l = [pp * (dpp - ds) for pp, dpp, ds in zip(pp_l, dpp_l, dsum_l)]
        dlc_l = [pc * (dpc - ds) for pc, dpc, ds in zip(pc_l, dpc_l, dsum_l)]
        dsk_ref[0] += sum(jnp.sum((-ps * ds).reshape(G, B, 1), axis=1) for ps, ds in zip(ps_l, dsum_l))
        db_ref[:, :, :B] += sum(dlp_l).reshape(G, B, B)
        db_ref[:, :, B:] += sum(dlc_l).reshape(G, B, B)
        dlp_l, dlc_l = [d * scale for d in dlp_l], [d * scale for d in dlc_l]
        dqn_l = [_bdot(dlp_l[s], kn_l[s], NN) + _bdot(dlc_l[s], kn_l[s + 1], NN) for s in range(P)]
        dq_l = [q_fw[s][1](dqn_l[s]) for s in range(P)]
        for s in range(P):
            dq_ref[pl.ds(s * B, B), :] = _ungroup_rows(dq_l[s][0], G).astype(dq_ref.dtype)
        dkn_l = [jnp.zeros((B, DH), F32)] * (P + 1)
        dvk_l = [jnp.zeros((B, DH), F32)] * (P + 1)
        for s in range(P):
            dkn_l[s] = dkn_l[s] + _bdot(dlp_l[s], qn_l[s], TN)
            dkn_l[s + 1] = dkn_l[s + 1] + _bdot(dlc_l[s], qn_l[s], TN)
            dvk_l[s] = dvk_l[s] + dvp_l[s]
            dvk_l[s + 1] = dvk_l[s + 1] + dvc_l[s]
        dk_l = [k_fw[t][1](dkn_l[t]) for t in range(P + 1)]
        for t in range(P + 1):
            r = pl.multiple_of(jnp.maximum(P * m + t - 1, 0) * B, B)
            dk_ref[pl.ds(r, B), :] += _to_my_head(dk_l[t][0], DH)
            dv_ref[pl.ds(r, B), :] += _to_my_head(dvk_l[t], DH)
        dqg_ref[...] += sum(d[1] for d in dq_l)
        dkg_ref[...] += sum(d[1] for d in dk_l)

    qblk, kspecs, vspecs = _attn_specs(cols, G, DH)
    oblk = pl.BlockSpec((P * B, G * DH), lambda h, m: (m, h))
    pk = _heads_per_tile(DH)
    assert KVH % pk == 0
    accblk = pl.BlockSpec((L, pk * DH), lambda h, m: (0, h // pk))
    vecblk = pl.BlockSpec((1, DH), lambda h, m: (0, 0))
    return _call(
        body, [proj] * (2 * P + 3) + [qg, kg, sinks, bias, d_o], name="attn_bwd", grid=(KVH, NB // P),
        out_shape=(jax.ShapeDtypeStruct((L, AH * DH), BF16), jax.ShapeDtypeStruct((L, KVH * DH), F32),
                   jax.ShapeDtypeStruct((L, KVH * DH), F32), jax.ShapeDtypeStruct((1, DH), F32),
                   jax.ShapeDtypeStruct((1, DH), F32), jax.ShapeDtypeStruct((KVH, G, 1), F32),
                   jax.ShapeDtypeStruct((AH, B, 2 * B), F32)),
        in_specs=[qblk] + kspecs + vspecs
        + [pl.BlockSpec((1, DH), lambda h, m: (0, 0)), pl.BlockSpec((1, DH), lambda h, m: (0, 0)),
           pl.BlockSpec((1, G, 1), lambda h, m: (h, 0, 0)), pl.BlockSpec((G, B, 2 * B), lambda h, m: (h, 0, 0)), oblk],
        out_specs=(oblk, accblk, accblk, vecblk, vecblk, pl.BlockSpec((1, G, 1), lambda h, m: (h, 0, 0)),
                   pl.BlockSpec((G, B, 2 * B), lambda h, m: (h, 0, 0))), comm=comm)


def _heads_first(t, nh):
    L = t.shape[0]
    return jnp.transpose(t.reshape(L, nh, t.shape[1] // nh), (1, 0, 2))


def _heads_last(t):
    nh, L, dh = t.shape
    return jnp.transpose(t, (1, 0, 2)).reshape(L, nh * dh)


def _softmax0(lg):
    e = jnp.exp(lg - jnp.max(lg, axis=0, keepdims=True))
    return e[0:1] / jnp.sum(e, axis=0, keepdims=True)


def _ada_update_call(fn, c_all, d_cols, w, m, v, rt):
    D, n = w.shape

    def body(c_ref, d_ref, w_ref, m_ref, v_ref, g_out, dl_out, m_out, v_out):
        outs, _ = fn(c_ref[...], d_ref[...], w_ref[...], m_ref[...], v_ref[...])
        for r, val in zip((g_out, dl_out, m_out, v_out), outs):
            r[...] = val

    wblk = pl.BlockSpec((rt, n), lambda i: (i, 0))
    return _call(
        body, [c_all, d_cols, w, m, v], name="update_ada", grid=(D // rt,), out_shape=tuple([jax.ShapeDtypeStruct((D, n), F32)] * 4),
        in_specs=[pl.BlockSpec((N_DEV, rt), lambda i: (0, i)), pl.BlockSpec((N_DEV, n), lambda i: (0, 0)), wblk, wblk, wblk],
        out_specs=(wblk, wblk, wblk, wblk))


def kernel(x, c, w_ada, b_ada, norm1_g, norm2_g, w_in, hg_lb_logits, hg_out_norm_g, q_norm_g, k_norm_g, attn_sinks, rel_bias_table, w_branch_hg, w_branch_attn, w_out, w_ff1, w_ff2, loss_target, m_w_ada, m_b_ada, m_norm1_g, m_norm2_g, m_w_in, m_hg_lb_logits, m_hg_out_norm_g, m_q_norm_g, m_k_norm_g, m_attn_sinks, m_rel_bias_table, m_w_branch_hg, m_w_branch_attn, m_w_out, m_w_ff1, m_w_ff2, v_w_ada, v_b_ada, v_norm1_g, v_norm2_g, v_w_in, v_hg_lb_logits, v_hg_out_norm_g, v_q_norm_g, v_k_norm_g, v_attn_sinks, v_rel_bias_table, v_w_branch_hg, v_w_branch_attn, v_w_out, v_w_ff1, v_w_ff2):
    cc = lax.axis_index("c")
    me = 4 * lax.axis_index("x") + 2 * lax.axis_index("y") + cc
    x2 = x[0]
    tgt = loss_target[0]
    L, D = x2.shape
    HGW = hg_lb_logits.shape[1]
    H = HGW // HG_DK
    AH = attn_sinks.shape[1]
    DH = q_norm_g.shape[1]
    ATW = AH * DH
    BW = w_in.shape[2]
    INW = BW * N_DEV
    A = BW // LANES
    assert BW == LANES * A + LANES // 2
    KVW = (INW - 4 * HGW - ATW - 2 * D) // 2
    KVH = KVW // DH
    G = AH // KVH
    ADA_N = w_ada.shape[2]
    PAIR = 2 * A + 1

    c_all = _gather_small(c, me, "gather_c")[:, 0, :]
    b_cols = lax.dynamic_slice(b_ada, (0, me * ADA_N), (1, ADA_N))
    (ada_cols,) = _whole(lambda cv, w, b: (_bdot(_silu(cv), w, NN) + b,), [c_all, w_ada[0], b_cols],
                         [((N_DEV, ADA_N), F32)], "ada_fwd")
    ada_all = _gather_small(ada_cols, me, "gather_ada")
    ada_row = lax.dynamic_slice(ada_all, (0, me, 0), (N_DEV, 1, ADA_N)).reshape(1, 6 * D)

    w_in_b = w_in[0].astype(BF16)
    src_in = jnp.where(cc == 0, jnp.pad(w_in_b, ((0, 0), (0, LANES // 2))), jnp.pad(w_in_b, ((0, 0), (LANES // 2, 0))))
    (src_in,) = _behind([src_in], [ada_row])
    shift1, scale1, gate1, shift2, scale2, gate2 = [ada_row[:, i * D:(i + 1) * D] for i in range(6)]
    w_in_gapped, w_in_mid = _ag_w_in(src_in, A, D, INW)
    w_in_full = _patch_mid(w_in_gapped, w_in_mid, A)

    wnames = ("bhg", "bat", "out", "ff1", "ff2")
    small = ("bhg", "bat", "out")
    waxis = dict(zip(wnames, (1, 1, 0, 1, 0)))
    wsrc = dict(zip(wnames, (w_branch_hg, w_branch_attn, w_out, w_ff1, w_ff2)))
    wblk = {k: wsrc[k][0].astype(BF16) for k in wnames}
    wf = {}

    (h,) = _rowwise(lambda xv, g, sh, sc: ((_modnorm(xv, g, sh, sc),), ()), [(x2, D, 0)], [norm1_g, shift1, scale1],
                    [(D, BF16)], [], "norm1")
    o4, oa = 4 * HGW, 4 * HGW + ATW + 2 * KVW
    r1, r2 = wblk["ff1"].shape[0], wblk["ff2"].shape[0]
    assert o4 % D == 0

    def proj_order(tn_, j):
        t4, tg, ng = o4 // tn_, oa // tn_, (INW - oa) // tn_
        return jnp.where(j < t4, j, jnp.where(j < t4 + ng, j + (tg - t4), j - ng))

    cm = _Comm()
    hs = {k: _ag_ici(cm, wblk[k], waxis[k]) for k in small}
    hs["ff2"] = _ag_ici(cm, wblk["ff2"], waxis["ff2"], rows=(0, r2 // 4))
    proj = _mm(h, w_in_full, "nn", F32, "proj", comm=cm, b_order=proj_order)
    half = {k: cm.result(hs[k]) for k in hs}
    p4 = pg = proj
    GATE0 = o4 // D
    AT0 = o4 + (INW - oa)

    cm = _Comm()
    hs = {k: _ag_d2d(cm, half[k], waxis[k]) for k in small}
    hs["ff1"] = _ag_ici(cm, wblk["ff1"], waxis["ff1"], rows=(0, r1 // 2))
    o_hg, s_all = _hgrn_fwd(p4, hg_lb_logits, hg_out_norm_g, H, comm=cm)
    wf["bhg"], wf["bat"], wf["out"], half["ff1"] = (cm.result(hs[k]) for k in ("bhg", "bat", "out", "ff1"))

    bucket = _bucket_ids()
    (bias_flat,) = _whole(lambda tb, bk: (_dot(tb, _onehot(bk), TN, precision=HIGHEST),), [rel_bias_table, bucket],
                          [((AH, AT_BLOCK * 2 * AT_BLOCK), F32)], "bias_fwd")
    bias = bias_flat.reshape(AH, AT_BLOCK, 2 * AT_BLOCK)
    at_cols = (AT0, AT0 + ATW, AT0 + ATW + KVW)
    sinks3 = attn_sinks.reshape(KVH, G, 1)
    cm = _Comm()
    hs = {"ff1": _ag_ici(cm, wblk["ff1"], waxis["ff1"], rows=(r1 // 2, r1), into=half["ff1"])}
    o_at = _attn_fwd(proj, at_cols, q_norm_g, k_norm_g, sinks3, bias, AH, KVH, comm=cm)
    half["ff1"] = cm.result(hs["ff1"])

    bh = _mm(o_hg, wf["bhg"], "nn", F32, "branch_hg")
    ba = _mm(o_at, wf["bat"], "nn", F32, "branch_at")

    def merge_fn(bhv, bav, ghg, gat):
        return jax.nn.sigmoid(ghg) * bhv + jax.nn.sigmoid(gat) * bav

    cm = _Comm()
    hs = {"ff1": _ag_d2d(cm, half["ff1"], waxis["ff1"]),
          "ff2": _ag_ici(cm, wblk["ff2"], waxis["ff2"], rows=(r2 // 4, 3 * r2 // 8), into=half["ff2"])}
    (merged,) = _rowwise(lambda *a: ((merge_fn(*a),), ()), [(bh, D, 0), (ba, D, 0), (pg, D, GATE0), (pg, D, GATE0 + 1)], [],
                         [(D, BF16)], [], "merge", comm=cm)
    wf["ff1"], half["ff2"] = cm.result(hs["ff1"]), cm.result(hs["ff2"])
    cm = _Comm()
    hs = {"ff2": _ag_ici(cm, wblk["ff2"], waxis["ff2"], rows=(3 * r2 // 8, r2 // 2), into=half["ff2"])}
    mo = _mm(merged, wf["out"], "nn", F32, "out_proj", comm=cm)
    half["ff2"] = cm.result(hs["ff2"])

    def resid1(xv, mov, g1, g2n, sh, sc):
        x1v = xv + g1 * mov
        return (x1v, _modnorm(x1v, g2n, sh, sc)), ()

    x1, h2 = _rowwise(resid1, [(x2, D, 0), (mo, D, 0)], [gate1, norm2_g, shift2, scale2], [(D, F32), (D, BF16)], [], "resid1")
    cm = _Comm()
    hs = {"ff2": _ag_ici(cm, wblk["ff2"], waxis["ff2"], rows=(r2 // 2, r2), into=half["ff2"])}
    u, act = _mm(h2, wf["ff1"], "nn", (F32, BF16), "ff1", comm=cm, epi=lambda r: (r, jnp.square(jnp.maximum(r, 0.0))))
    half["ff2"] = cm.result(hs["ff2"])
    cm = _Comm()
    hs = {"ff2": _ag_d2d(cm, half["ff2"], waxis["ff2"])}
    _call(lambda: None, [], name="ag_d2d_ff2", out_shape=(), comm=cm)
    wf["ff2"] = cm.result(hs["ff2"])
    ff = _mm(act, wf["ff2"], "nn", F32, "ff2")

    def loss_fn(x1v, ffv, tv, g2):
        e = x1v + g2 * ffv - tv
        dy = e * (1.0 / D)
        return (dy, dy * g2), (jnp.sum(e * e, axis=0, keepdims=True), jnp.sum(dy * ffv, axis=0, keepdims=True))

    dy, d_ff, sq_sum, d_gate2 = _rowwise(loss_fn, [(x1, D, 0), (ff, D, 0), (tgt, D, 0)], [gate2],
                                         [(D, F32), (D, BF16)], [(1, D), (1, D)], "loss")
    loss = lax.psum(jnp.sum(sq_sum) * (0.5 / D), ("x", "y", "c"))

    owner_base = jnp.stack([me ^ r for r in CHIP_RELS]).astype(jnp.int32)
    gw, recv1, part, recv2 = {}, {}, {}, {}
    gw["ff2"] = _mm(act, d_ff, "tn", BF16, "dw_ff2")
    cm = _Comm()
    hh = _rs_d2d(cm, gw["ff2"], waxis["ff2"])
    d_u = _mm(d_ff, wf["ff2"], "nt", BF16, "d_act", comm=cm, extras=[u], epi=lambda r, uv: (r * (2.0 * jnp.maximum(uv, 0.0)),))
    part["ff2"] = _rs_add(gw["ff2"], cm.result(hh), waxis["ff2"], owner_base, "rs_add_ff2")
    rows_ff2 = part["ff2"].shape[1]
    cm = _Comm()
    hh = _rs_ici(cm, part["ff2"], rows=(0, rows_ff2 // 2))
    gw["ff1"] = _mm(h2, d_u, "tn", BF16, "dw_ff1", comm=cm)
    cm2 = _Comm()
    hh2 = _rs_ici(cm2, part["ff2"], rows=(rows_ff2 // 2, rows_ff2), recv=cm.result(hh))
    hh1 = _rs_d2d(cm2, gw["ff1"], waxis["ff1"])
    d_h2 = _mm(d_u, wf["ff1"], "nt", F32, "d_h2", comm=cm2)
    recv2["ff2"] = cm2.result(hh2)
    part["ff1"] = _rs_add(gw["ff1"], cm2.result(hh1), waxis["ff1"], owner_base, "rs_add_ff1")

    def norm2_bwd(dh2v, x1v, dyv, mov, g2n, sh, sc, g1):
        _, vjp = jax.vjp(_modnorm, x1v, g2n, sh, sc)
        dx, dg, dsh, dsc = vjp(dh2v)
        dx1 = dyv + dx
        return (dx1, dx1 * g1), (dg, dsh, dsc, jnp.sum(dx1 * mov, axis=0, keepdims=True))

    d_x1, d_mo, d_g2n, d_shift2, d_scale2, d_gate1 = _rowwise(
        norm2_bwd, [(d_h2, D, 0), (x1, D, 0), (dy, D, 0), (mo, D, 0)], [norm2_g, shift2, scale2, gate1],
        [(D, F32), (D, BF16)], [(1, D)] * 4, "norm2_bwd")
    gw["out"] = _mm(merged, d_mo, "tn", BF16, "dw_out")
    d_merged = _mm(d_mo, wf["out"], "nt", F32, "d_merged")

    def merge_bwd(dmv, bhv, bav, ghg, gat):
        _, vjp = jax.vjp(merge_fn, bhv, bav, ghg, gat)
        return vjp(dmv), ()

    d_bh, d_ba, d_ghg, d_gat = _rowwise(merge_bwd, [(d_merged, D, 0), (bh, D, 0), (ba, D, 0), (pg, D, GATE0), (pg, D, GATE0 + 1)], [],
                                        [(D, BF16)] * 4, [], "merge_bwd")
    gw["bhg"] = _mm(o_hg, d_bh, "tn", BF16, "dw_bhg")
    gw["bat"] = _mm(o_at, d_ba, "tn", BF16, "dw_bat")
    d_ohg = _mm(d_bh, wf["bhg"], "nt", F32, "d_ohg")
    d_oat = _mm(d_ba, wf["bat"], "nt", BF16, "d_oat")
    rows_ff1 = part["ff1"].shape[1]
    cut_ff1 = rows_ff1 // 4
    cm = _Comm()
    hf1 = _rs_ici(cm, part["ff1"], rows=(0, cut_ff1))
    d_hq, d_hf, d_hi, d_hg, d_lb, d_gout_h = _hgrn_bwd(p4, hg_lb_logits, hg_out_norm_g, s_all, d_ohg, H, comm=cm)
    cm2 = _Comm()
    hf1 = _rs_ici(cm2, part["ff1"], rows=(cut_ff1, rows_ff1), recv=cm.result(hf1))
    hh = {k: _rs_d2d(cm2, gw[k], waxis[k]) for k in small}
    d_aq, dkp, dvp, d_qg, d_kg, d_sk, d_bias = _attn_bwd(proj, at_cols, q_norm_g, k_norm_g, sinks3, bias,
                                                         d_oat, AH, KVH, comm=cm2)
    recv2["ff1"] = cm2.result(hf1)
    for k in small:
        part[k] = _rs_add(gw[k], cm2.result(hh[k]), waxis[k], owner_base, "rs_add_" + k)
    d_ak = dkp.astype(BF16)
    d_av = dvp.astype(BF16)
    d_proj = jnp.concatenate([d_hq, d_hf, d_hi, d_hg, d_aq, d_ak, d_av, d_ghg, d_gat], axis=1)
    cm = _Comm()
    hh = {k: _rs_ici(cm, part[k]) for k in small}
    gw_in = _mm(h, d_proj, "tn", BF16, "dw_in", comm=cm)
    for k in small:
        recv2[k] = cm.result(hh[k])

    wm = LANES * A
    cm = _Comm()
    hi_ = cm.inp(gw_in)
    h_main, h_mid = cm.out((4, D, wm), BF16), cm.out((4, D, LANES), BF16)
    for i, r in enumerate(CHIP_RELS):
        def main_view(ref, p, r=r):
            o = p["me"] ^ r ^ 1
            return ref.at[:, pl.ds(pl.multiple_of((PAIR * (o // 2) + (A + 1) * (1 - p["c"])) * LANES, LANES), wm)]

        def mid_view(ref, p, r=r):
            o = p["me"] ^ r
            return ref.at[:, pl.ds(pl.multiple_of((PAIR * (o // 2) + A) * LANES, LANES), LANES)]

        cm.copy(hi_, main_view, h_main, _slot_view(i), 1)
        cm.copy(hi_, mid_view, h_mid, _slot_view(i), 1)
    _call(lambda: None, [], name="rs_d2d_in", out_shape=(), comm=cm)
    chip = jnp.stack([(me ^ r) // 2 for r in CHIP_RELS]).astype(jnp.int32)
    part_main = _rs_add(gw_in, cm.result(h_main), 1, PAIR * chip + (A + 1) * cc, "rs_add_in_main", tw=LANES)
    part_mid = _rs_add(gw_in, cm.result(h_mid), 1, PAIR * chip + A, "rs_add_in_mid", tw=LANES)
    rs_in = _rs_split_start([part_main, part_mid], "rs_in_start")
    d_h = _mm(d_proj, w_in_full, "nt", F32, "d_h", tn=D, after=[rs_in["token"]])

    def norm1_bwd(dhv, xv, dx1v, g1n, sh, sc):
        _, vjp = jax.vjp(_modnorm, xv, g1n, sh, sc)
        dx, dg, dsh, dsc = vjp(dhv)
        return (dx1v + dx,), (dg, dsh, dsc)

    grad_x, d_g1n, d_shift1, d_scale1 = _rowwise(norm1_bwd, [(d_h, D, 0), (x2, D, 0), (d_x1, D, 0)],
                                                 [norm1_g, shift1, scale1], [(D, F32)], [(1, D)] * 3, "norm1_bwd")

    def sum4(p0, p1, p2, p3):
        return ((p0.astype(F32) + p1.astype(F32)) + p2.astype(F32)) + p3.astype(F32)

    def update_fn(w, m, v, p0, p1, p2, p3):
        g = sum4(p0, p1, p2, p3)
        delta, mn, vn = _adamw(w, g, m, v)
        return (g, delta, mn, vn), ()

    wmv = dict(zip(wnames, ((w_branch_hg, m_w_branch_hg, v_w_branch_hg), (w_branch_attn, m_w_branch_attn, v_w_branch_attn),
                            (w_out, m_w_out, v_w_out), (w_ff1, m_w_ff1, v_w_ff1), (w_ff2, m_w_ff2, v_w_ff2))))
    res = {}

    def update(k, p, rx):
        w, m, v = (t[0] for t in wmv[k])
        n = w.shape[1]
        ins = [(t, n, 0) for t in (w, m, v)] + [(p, n, 0, 0)] + [(rx, n, 0, i) for i in range(3)]
        res[k] = [t[None] for t in _rowwise(update_fn, ins, [], [(n, F32)] * 4, [], "update_" + k)]

    for k in wnames:
        update(k, part[k], recv2[k])
    (part_main, part_mid), (rx_main, rx_mid) = _rs_split_wait(rs_in, [grad_x] + [res[k][0] for k in wnames], "rs_in_wait")
    g_main, = _rowwise(lambda *p: ((sum4(*p),), ()), [(part_main, wm, 0, 0)] + [(rx_main, wm, 0, i) for i in range(3)], [],
                       [(wm, F32)], [], "sum_in_main")
    g_mid, = _rowwise(lambda *p: ((sum4(*p),), ()), [(part_mid, LANES, 0, 0)] + [(rx_mid, LANES, 0, i) for i in range(3)], [],
                      [(LANES, F32)], [], "sum_in_mid")
    g_in = jnp.where(cc == 0, jnp.concatenate([g_main, g_mid[:, :LANES // 2]], axis=1),
                     jnp.concatenate([g_mid[:, LANES // 2:], g_main], axis=1))

    def update_given(w, m, v, g):
        delta, mn, vn = _adamw(w, g, m, v)
        return (g, delta, mn, vn), ()

    res["in"] = [t[None] for t in _rowwise(update_given, [(t, BW, 0) for t in (w_in[0], m_w_in[0], v_w_in[0], g_in)], [],
                                           [(BW, F32)] * 4, [], "update_in")]

    d_sinks = d_sk.reshape(1, AH)
    (d_table_t,) = _whole(lambda db, bk: (_dot(db, _onehot(bk), NT, precision=HIGHEST),),
                          [d_bias.reshape(AH, AT_BLOCK * 2 * AT_BLOCK), bucket], [((AH, N_BUCKETS), F32)], "bias_bwd")
    smalls = [d_g1n, d_g2n, d_lb, d_gout_h, d_qg, d_kg, d_sinks, d_table_t.T.reshape(1, N_BUCKETS * AH)]
    widths = [s.shape[1] for s in smalls]
    lanes = [-(-w // LANES) * LANES for w in widths]
    smalls = [jnp.pad(s, ((0, 0), (0, p - w))) for s, w, p in zip(smalls, widths, lanes)]
    tail_row = jnp.concatenate([d_shift1, d_scale1, d_gate1, d_shift2, d_scale2, d_gate2] + smalls, axis=1)
    (tail_row,) = _behind([tail_row], [g_mid])
    tail_all = _gather_small(tail_row, me, "gather_tail")[:, 0, :]
    d_ada_all, packed = tail_all[:, :6 * D], tail_all[:, 6 * D:]
    d_ada_cols = lax.dynamic_slice(d_ada_all, (0, me * ADA_N), (N_DEV, ADA_N))

    def ada_update(cv, dav, w, m, v):
        g = _bdot(_silu(cv), dav, TN)
        delta, mn, vn = _adamw(w, g, m, v)
        return (g, delta, mn, vn), ()

    res["ada"] = [t[None] for t in _ada_update_call(ada_update, c_all, d_ada_cols, w_ada[0], m_w_ada[0], v_w_ada[0], _tile(D, 256, 16))]

    offs = [sum(lanes[:i]) for i in range(len(lanes))]

    def small_update(pk, dada, lg, *wmv_flat):
        tot = pk[0:1]
        for d in range(1, N_DEV):
            tot = tot + pk[d:d + 1]
        gb = dada[0:1]
        for d in range(1, N_DEV):
            gb = gb + dada[d:d + 1]
        gs = [tot[:, offs[i]:offs[i] + widths[i]] for i in range(len(widths))]
        _, lb_vjp = jax.vjp(_softmax0, lg)
        (g_lg,) = lb_vjp(gs[2])
        grads = [gb, gs[0], gs[1], g_lg, gs[3], gs[4], gs[5], gs[6], gs[7]]
        outs = []
        for i, g in enumerate(grads):
            w, m, v = wmv_flat[3 * i:3 * i + 3]
            delta, mn, vn = _adamw(w, g, m, v)
            outs += [g, delta, mn, vn]
        return tuple(outs)

    tbl = lambda t: t.reshape(1, N_BUCKETS * AH)
    small_wmv = [(b_ada, m_b_ada, v_b_ada), (norm1_g, m_norm1_g, v_norm1_g), (norm2_g, m_norm2_g, v_norm2_g),
                 (hg_lb_logits, m_hg_lb_logits, v_hg_lb_logits), (hg_out_norm_g, m_hg_out_norm_g, v_hg_out_norm_g),
                 (q_norm_g, m_q_norm_g, v_q_norm_g), (k_norm_g, m_k_norm_g, v_k_norm_g),
                 (attn_sinks, m_attn_sinks, v_attn_sinks),
                 (tbl(rel_bias_table), tbl(m_rel_bias_table), tbl(v_rel_bias_table))]
    flat = [t for trip in small_wmv for t in trip]
    out_shapes = [(trip[0].shape, F32) for trip in small_wmv for _ in range(4)]
    sres = _whole(small_update, [packed, d_ada_all, hg_lb_logits] + flat, out_shapes, "small_update")
    names_small = ("b_ada", "norm1_g", "norm2_g", "lb", "gout", "qg", "kg", "sinks", "table")
    for i, k in enumerate(names_small):
        r = sres[4 * i:4 * i + 4]
        if k == "table":
            r = [t.reshape(N_BUCKETS, AH) for t in r]
        res[k] = r

    order = ("ada", "b_ada", "norm1_g", "norm2_g", "in", "lb", "gout", "qg", "kg", "sinks", "table", "bhg", "bat", "out", "ff1", "ff2")
    outs = [loss, grad_x[None]]
    for j in range(4):
        outs += [res[k][j] for k in order]
    return tuple(outs)
```

```python
import functools
import math

import jax
import jax.numpy as jnp
from jax import lax
from jax.experimental import pallas as pl
from jax.experimental.pallas import tpu as pltpu

F32 = jnp.float32
BF16 = jnp.bfloat16
EPS = 1e-6
NEG_INF = -1e30
HG_DK = 128
HG_CHUNK = 64
AT_BLOCK = 128
N_BUCKETS = 32
MAX_EXACT = 16
MAX_DISTANCE = 128
N_DEV = 8
LANES = 128
VMEM_LIMIT = 56 * 1024 * 1024
ADAM_LR, ADAM_B1, ADAM_B2, ADAM_EPS, ADAM_WD, ADAM_STEP = 0.001, 0.9, 0.999, 1e-08, 0.01, 10
HIGHEST = lax.Precision.HIGHEST
MESH = pl.DeviceIdType.MESH
ANY = pl.BlockSpec(memory_space=pl.ANY)
CHIP_RELS = (0, 4, 2, 6)

NN = (((1,), (0,)), ((), ()))
NT = (((1,), (1,)), ((), ()))
TN = (((0,), (0,)), ((), ()))


def _tile(n, pref, unit):
    if n <= pref:
        return n
    t = (pref // unit) * unit
    while t >= unit:
        if n % t == 0:
            return t
        t -= unit
    return n


def _dot(a, b, dn, precision=None):
    return lax.dot_general(a, b, dn, preferred_element_type=F32, precision=precision)


def _bdot(a, b, dn):
    return _dot(a.astype(BF16), b.astype(BF16), dn)


def _position():
    x, y, c = lax.axis_index("x"), lax.axis_index("y"), lax.axis_index("c")
    return dict(x=x, y=y, c=c, me=4 * x + 2 * y + c)


def _peer_position(p, rel):
    x = 1 - p["x"] if rel & 4 else p["x"]
    y = 1 - p["y"] if rel & 2 else p["y"]
    c = 1 - p["c"] if rel & 1 else p["c"]
    return dict(x=x, y=y, c=c, me=4 * x + 2 * y + c)


class _Comm:
    def __init__(self):
        self.ins, self.outs, self.alias, self.plans, self.res = [], [], {}, [], None

    def inp(self, arr):
        self.ins.append(arr)
        return ("i", len(self.ins) - 1)

    def out(self, shape, dtype, alias=None):
        self.outs.append(jax.ShapeDtypeStruct(tuple(shape), dtype))
        if alias is not None:
            self.alias[alias[1]] = len(self.outs) - 1
        return ("o", len(self.outs) - 1)

    def copy(self, src, src_view, dst, dst_view, rel):
        self.plans.append((src, src_view, dst, dst_view, rel))

    def result(self, handle):
        return self.res[handle[1]]

    def build(self, in_refs, out_refs, send_sems, recv_sems):
        pos = _position()
        ref = lambda h: in_refs[h[1]] if h[0] == "i" else out_refs[h[1]]
        ops = []
        for k, (src, sv, dst, dv, rel) in enumerate(self.plans):
            s = sv(ref(src), pos)
            if rel == 0:
                cp = pltpu.make_async_copy(s, dv(ref(dst), pos), send_sems.at[k])
                ops.append((cp.start, cp.wait))
                continue
            peer = _peer_position(pos, rel)
            mk = lambda d: pltpu.make_async_remote_copy(
                src_ref=s, dst_ref=d, send_sem=send_sems.at[k], recv_sem=recv_sems.at[k],
                device_id=(peer["x"], peer["y"], peer["c"]), device_id_type=MESH)
            out_cp, in_cp = mk(dv(ref(dst), pos)), mk(dv(ref(dst), peer))

            def wait(out_cp=out_cp, in_cp=in_cp):
                out_cp.wait_send()
                in_cp.wait_recv()

            ops.append((out_cp.start, wait))
        return ops


def _call(body, args, *, name, out_shape, in_specs=None, out_specs=None, grid=None, scratch_shapes=(), comm=None,
          prefetch=None, aliases=None, after=()):
    single = not isinstance(out_shape, (tuple, list))
    out_shape = (out_shape,) if single else tuple(out_shape)
    n_in, n_out, n_scr = len(args), len(out_shape), len(scratch_shapes)
    vm = pl.BlockSpec(memory_space=pltpu.VMEM)
    in_specs = [vm] * n_in if in_specs is None else list(in_specs)
    out_specs = [vm] * n_out if out_specs is None else (list(out_specs) if isinstance(out_specs, (tuple, list)) else [out_specs])
    n_pf = 0 if prefetch is None else len(prefetch)
    kw = {} if aliases is None else {"input_output_aliases": dict(aliases)}
    if comm is None and after:
        n_dep = len(after)

        def fn(*refs):
            body(*refs[:n_pf + n_in], *refs[n_pf + n_in + n_dep:])

        all_args, all_scratch = list(args) + list(after), list(scratch_shapes)
        in_specs = in_specs + [ANY] * n_dep
    elif comm is None:
        fn = body
        all_args, all_scratch = list(args), list(scratch_shapes)
    else:
        n_ci, n_co, n_x = len(comm.ins), len(comm.outs), len(comm.plans)

        def fn(*refs):
            pf, refs = refs[:n_pf], refs[n_pf:]
            o_in, c_in = refs[:n_in], refs[n_in:n_in + n_ci]
            o_out = refs[n_in + n_ci:n_in + n_ci + n_out]
            c_out = refs[n_in + n_ci + n_out:n_in + n_ci + n_out + n_co]
            scr = refs[n_in + n_ci + n_out + n_co:]
            ops = comm.build(c_in, c_out, scr[n_scr], scr[n_scr + 1])
            if grid:
                first = functools.reduce(jnp.logical_and, [pl.program_id(i) == 0 for i in range(len(grid))])
                last = functools.reduce(jnp.logical_and, [pl.program_id(i) == g - 1 for i, g in enumerate(grid)])

                @pl.when(first)
                def _():
                    for start, _w in ops:
                        start()
            else:
                for start, _w in ops:
                    start()
            body(*pf, *o_in, *o_out, *scr[:n_scr])
            if grid:
                @pl.when(last)
                def _():
                    for _s, wait in ops:
                        wait()
            else:
                for _s, wait in ops:
                    wait()

        all_args = list(args) + list(comm.ins)
        in_specs = in_specs + [ANY] * n_ci
        out_shape = out_shape + tuple(comm.outs)
        out_specs = out_specs + [ANY] * n_co
        all_scratch = list(scratch_shapes) + [pltpu.SemaphoreType.DMA((n_x,)), pltpu.SemaphoreType.DMA((n_x,))]
        kw["input_output_aliases"] = {n_pf + n_in + i: n_out + o for i, o in comm.alias.items()}
    sem = None if grid is None else ("arbitrary",) * len(grid)
    params = pltpu.CompilerParams(dimension_semantics=sem, vmem_limit_bytes=VMEM_LIMIT)
    if prefetch is None:
        spec = dict(in_specs=in_specs, out_specs=tuple(out_specs), scratch_shapes=all_scratch)
        if grid is not None:
            spec["grid"] = grid
    else:
        spec = dict(grid_spec=pltpu.PrefetchScalarGridSpec(
            num_scalar_prefetch=n_pf, grid=grid, in_specs=in_specs, out_specs=tuple(out_specs), scratch_shapes=all_scratch))
        all_args = list(prefetch) + all_args
    res = pl.pallas_call(fn, name=name, out_shape=out_shape, compiler_params=params, **spec, **kw)(*all_args)
    res = list(res)
    if comm is not None:
        comm.res = res[n_out:]
        res = res[:n_out]
    return res[0] if single else res


def _whole_view(ref, pos):
    return ref


def _block_view(axis, n, index, rows=None):
    def view(ref, pos):
        off = pl.multiple_of(index(pos) * n, n)
        if rows is None:
            return ref.at[:, pl.ds(off, n)] if axis == 1 else ref.at[pl.ds(off, n), :]
        lo, cnt = rows[0], rows[1] - rows[0]
        if axis == 1:
            return ref.at[pl.ds(lo, cnt), pl.ds(off, n)]
        return ref.at[pl.ds(pl.multiple_of(off + lo, 16), cnt), :]
    return view


def _rows_view(rows):
    def view(ref, pos):
        return ref if rows is None else ref.at[pl.ds(rows[0], rows[1] - rows[0]), :]
    return view


def _slot_view(i, rows=None):
    def view(ref, pos):
        return ref.at[i] if rows is None else ref.at[i, pl.ds(rows[0], rows[1] - rows[0]), :]
    return view


def _exchange(items, name):
    cm = _Comm()
    for a, rel in items:
        cm.copy(cm.inp(a), _whole_view, cm.out(a.shape, a.dtype), _whole_view, rel)
    _call(lambda: None, [], name=name, out_shape=(), comm=cm)
    return cm.res


def _gather_small(v, me, name):
    cm = _Comm()
    hi, ho = cm.inp(v), cm.out((N_DEV,) + v.shape, v.dtype)
    for rel in range(N_DEV):
        cm.copy(hi, _whole_view, ho, lambda ref, p: ref.at[p["me"]], rel)
    _call(lambda: None, [], name=name, out_shape=(), comm=cm)
    return cm.result(ho)


def _ag_ici(cm, blk, axis, rows=None, into=None):
    n = blk.shape[axis]
    shape = list(blk.shape)
    shape[axis] = n * N_DEV
    hi = cm.inp(blk)
    ho = cm.out(shape, blk.dtype) if into is None else cm.out(shape, blk.dtype, alias=cm.inp(into))
    own = _block_view(axis, n, lambda p: p["me"], rows)
    for rel in CHIP_RELS:
        cm.copy(hi, _rows_view(rows), ho, own, rel)
    return ho


def _ag_d2d(cm, full, axis):
    n = full.shape[axis] // N_DEV
    hi = cm.inp(full)
    ho = cm.out(full.shape, full.dtype, alias=hi)
    for r in CHIP_RELS:
        v = _block_view(axis, n, functools.partial(lambda p, r: p["me"] ^ r, r=r))
        cm.copy(hi, v, ho, v, 1)
    return ho


def _rs_d2d(cm, gw, axis):
    n = gw.shape[axis] // N_DEV
    shape = list(gw.shape)
    shape[axis] = n
    hi, ho = cm.inp(gw), cm.out([4] + shape, gw.dtype)
    for i, r in enumerate(CHIP_RELS):
        cm.copy(hi, _block_view(axis, n, functools.partial(lambda p, r: p["me"] ^ r ^ 1, r=r)), ho, _slot_view(i), 1)
    return ho


def _rs_ici(cm, part, rows=None, recv=None):
    if recv is None:
        ho = cm.out((3,) + part.shape[1:], part.dtype)
    else:
        ho = cm.out(recv.shape, recv.dtype, alias=cm.inp(recv))
    hi = cm.inp(part)
    for i in (1, 2, 3):
        cm.copy(hi, _slot_view(i, rows), ho, _slot_view(i - 1, rows), CHIP_RELS[i])
    return ho


def _rs_add(gw, recv, axis, base, name, tw=None):
    _, R, n = recv.shape
    fan = 1
    if axis == 1:
        tw = n if tw is None else tw
        fan = max(f for f in (4, 3, 2, 1) if (n // tw) % f == 0)
        gw_specs = [pl.BlockSpec((R, tw), functools.partial(lambda i, t, b, k: (0, b[i] + fan * t + k), k=k)) for k in range(fan)]
        rv_spec = pl.BlockSpec((None, R, tw * fan), lambda i, t, b: (i, 0, t))
        grid = (4, n // (tw * fan))
    else:
        tw = _tile(n, 1024, LANES)
        gw_specs = [pl.BlockSpec((R, tw), lambda i, t, b: (b[i], t))]
        rv_spec = pl.BlockSpec((None, R, tw), lambda i, t, b: (i, 0, t))
        grid = (4, n // tw)

    def body(b_ref, *refs):
        g_refs, r_ref, o_ref = refs[:fan], refs[fan], refs[fan + 1]
        g = g_refs[0][...] if fan == 1 else jnp.concatenate([g[...] for g in g_refs], axis=1)
        o_ref[...] = (g.astype(F32) + r_ref[...].astype(F32)).astype(o_ref.dtype)

    return _call(body, [gw] * fan + [recv], name=name, out_shape=jax.ShapeDtypeStruct(recv.shape, recv.dtype), grid=grid,
                 in_specs=gw_specs + [rv_spec], out_specs=rv_spec, prefetch=[base])


HBM_SPEC = pl.BlockSpec(memory_space=pltpu.HBM)
SEM_SPEC = pl.BlockSpec(memory_space=pltpu.SEMAPHORE)
SPLIT_PARAMS = pltpu.CompilerParams(has_side_effects=pltpu.SideEffectType.DATAFLOW_SIDE_EFFECTING)


def _split_copies(refs, plans, send_sems, recv_sems):
    pos = _position()
    out = []
    for k, (si, sv, li, lv, rel) in enumerate(plans):
        peer = _peer_position(pos, rel)
        mk = lambda d: pltpu.make_async_remote_copy(
            src_ref=sv(refs[si], pos), dst_ref=d, send_sem=send_sems.at[k], recv_sem=recv_sems.at[k],
            device_id=(peer["x"], peer["y"], peer["c"]), device_id_type=MESH)
        out.append((mk(lv(refs[li], pos)), mk(lv(refs[li], peer))))
    return out


def _split_start(arrays, plans, name):
    n = len(arrays)

    def body(*refs):
        send_sems, recv_sems = refs[n], refs[n + 1]
        for out_cp, _ in _split_copies(refs[:n], plans, send_sems, recv_sems):
            out_cp.start()
        refs[-1][...] = jnp.zeros_like(refs[-1])

    sems = pltpu.SemaphoreType.DMA((len(plans),))
    res = pl.pallas_call(
        body, name=name,
        out_shape=(sems, sems) + tuple(pltpu.HBM(a.shape, a.dtype) for a in arrays) + (jax.ShapeDtypeStruct((8, LANES), F32),),
        in_specs=[HBM_SPEC] * n, out_specs=(SEM_SPEC, SEM_SPEC) + (HBM_SPEC,) * n + (pl.BlockSpec(memory_space=pltpu.VMEM),),
        input_output_aliases={i: 2 + i for i in range(n)}, compiler_params=SPLIT_PARAMS,
    )(*[pltpu.with_memory_space_constraint(a, pltpu.HBM) for a in arrays])
    return res[0], res[1], list(res[2:2 + n]), res[-1]


def _split_wait(send_sems, recv_sems, arrays, plans, after, name):
    n, na = len(arrays), len(after)

    def body(*refs):
        for out_cp, in_cp in _split_copies(refs[:n], plans, refs[n], refs[n + 1]):
            out_cp.wait_send()
            in_cp.wait_recv()

    res = pl.pallas_call(
        body, name=name, out_shape=tuple(pltpu.HBM(a.shape, a.dtype) for a in arrays),
        in_specs=[HBM_SPEC] * n + [SEM_SPEC, SEM_SPEC] + [ANY] * na, out_specs=(HBM_SPEC,) * n,
        input_output_aliases={i: i for i in range(n)}, compiler_params=SPLIT_PARAMS,
    )(*arrays, send_sems, recv_sems, *after)
    return list(res)


def _rs_split_start(parts, name):
    nw = len(parts)
    lands = [lax.empty((3,) + p.shape[1:], p.dtype) for p in parts]
    plans = [(s, _slot_view(i), nw + s, _slot_view(i - 1), CHIP_RELS[i]) for s in range(nw) for i in (1, 2, 3)]
    send_sems, recv_sems, arrays, token = _split_start(list(parts) + lands, plans, name)
    return dict(sems=(send_sems, recv_sems), arrays=arrays, plans=plans, token=token, nw=nw)


def _rs_split_wait(h, after, name):
    arrays = _split_wait(h["sems"][0], h["sems"][1], h["arrays"], h["plans"], after, name)
    return arrays[:h["nw"]], arrays[h["nw"]:]


def _behind(xs, tokens):
    out = lax.optimization_barrier((tuple(xs), tuple(tokens)))
    return list(out[0])


def _ag_w_in(src, a, D, INW):
    wm = LANES * a

    hd = D // 2
    ALL, TOP, BOT = (0, D), (0, hd), (hd, D)

    def main_place(ref, p, rows=ALL):
        off = pl.multiple_of(((2 * a + 1) * (p["me"] // 2) + (a + 1) * p["c"]) * LANES, LANES)
        return ref.at[pl.ds(rows[0], rows[1] - rows[0]), pl.ds(off, wm)]

    def main_src(ref, p):
        return ref.at[:, pl.ds(pl.multiple_of(p["c"] * LANES, LANES), wm)]

    def mid_src(ref, p):
        return ref.at[:, pl.ds(pl.multiple_of((1 - p["c"]) * wm, LANES), LANES)]

    def mid_place(ref, p, rows=ALL):
        return ref.at[p["me"], pl.ds(rows[0], rows[1] - rows[0]), :]

    def body(src_ref, full_ref, mid_ref, send_sems, recv_sems):
        pos = _position()
        sib, xn, yn = (_peer_position(pos, r) for r in (1, 4, 2))
        dg = _peer_position(pos, 6)
        started = []

        def remote(k, s, d, to):
            return pltpu.make_async_remote_copy(src_ref=s, dst_ref=d, send_sem=send_sems.at[k], recv_sem=recv_sems.at[k],
                                                device_id=(to["x"], to["y"], to["c"]), device_id_type=MESH)

        def send(k, owner, rows, to, from_src=False):
            for j, (src_v, place) in enumerate(((main_src, main_place), (mid_src, mid_place))):
                s = src_v(src_ref, pos) if from_src else place(full_ref if j == 0 else mid_ref, owner, rows)
                cp = remote(k + j, s, place(full_ref if j == 0 else mid_ref, owner, rows), to)
                cp.start()
                started.append(cp)

        def landed(k, owner, rows, frm):
            for j, place in enumerate((main_place, mid_place)):
                ref = full_ref if j == 0 else mid_ref
                remote(k + j, place(ref, owner, rows), place(ref, owner, rows), frm).wait_recv()

        local = [pltpu.make_async_copy(main_src(src_ref, pos), main_place(full_ref, pos), send_sems.at[18]),
                 pltpu.make_async_copy(mid_src(src_ref, pos), mid_place(mid_ref, pos), send_sems.at[19])]
        for cp in local:
            cp.start()
        send(0, pos, ALL, sib, from_src=True)
        send(2, pos, ALL, xn, from_src=True)
        send(4, pos, ALL, yn, from_src=True)
        landed(2, xn, ALL, xn)
        send(10, xn, ALL, sib)
        send(6, xn, TOP, yn)
        landed(4, yn, ALL, yn)
        send(12, yn, ALL, sib)
        send(8, yn, BOT, xn)
        landed(6, dg, TOP, yn)
        send(14, dg, TOP, sib)
        landed(8, dg, BOT, xn)
        send(16, dg, BOT, sib)
        sib_of = lambda p: _peer_position(p, 1)
        landed(0, sib, ALL, sib)
        landed(10, sib_of(xn), ALL, sib)
        landed(12, sib_of(yn), ALL, sib)
        landed(14, sib_of(dg), TOP, sib)
        landed(16, sib_of(dg), BOT, sib)
        for cp in started:
            cp.wait_send()
        for cp in local:
            cp.wait()

    return _call(body, [src], name="ag_w_in", in_specs=[ANY], out_specs=[ANY, ANY],
                 out_shape=(jax.ShapeDtypeStruct((D, INW), BF16), jax.ShapeDtypeStruct((N_DEV, D, LANES), BF16)),
                 scratch_shapes=[pltpu.SemaphoreType.DMA((20,)), pltpu.SemaphoreType.DMA((20,))])


def _patch_mid(full, mid, a):
    D = full.shape[0]

    def body(full_ref, e_ref, o_ref, out_ref):
        out_ref[...] = e_ref[...] + o_ref[...]

    return _call(body, [full, mid, mid], name="patch_mid", grid=(N_DEV // 2,),
                 out_shape=jax.ShapeDtypeStruct(full.shape, full.dtype),
                 in_specs=[ANY, pl.BlockSpec((None, D, LANES), lambda j: (2 * j, 0, 0)),
                           pl.BlockSpec((None, D, LANES), lambda j: (2 * j + 1, 0, 0))],
                 out_specs=pl.BlockSpec((D, LANES), lambda j: (0, (2 * a + 1) * j + a)), aliases={0: 0})


MM_RESIDENT = 2048


def _mm(a, b, mode, out_dtype, name, b_off=0, n=None, comm=None, extras=(), epi=None, tn=None, after=(), b_order=None):
    if mode == "nn":
        (M, K), (K2, N) = a.shape, b.shape
    elif mode == "nt":
        (M, K), (N, K2) = a.shape, b.shape
    else:
        (K, M), (K2, N) = a.shape, b.shape
    assert K == K2, (a.shape, b.shape, mode)
    if n is not None:
        N = n
    single = not isinstance(out_dtype, (tuple, list))
    out_dtypes = (out_dtype,) if single else tuple(out_dtype)
    if epi is None:
        epi = lambda r: (r,)
    tk = K if K <= MM_RESIDENT else (MM_RESIDENT if K % MM_RESIDENT == 0 else _tile(K, 512, LANES))
    nk = K // tk
    if M > MM_RESIDENT and mode == "tn" and N <= MM_RESIDENT and not b_off:
        tm, tn = _tile(M, 512, LANES), N
    elif nk > 1:
        tm, tn = _tile(M, 1024, LANES), _tile(N, tn or 1024, LANES)
    else:
        tm = _tile(M, MM_RESIDENT, LANES)
        tn = _tile(math.gcd(N, b_off) if b_off else N, tn or 512, LANES)
    jb = b_off // tn
    dn = {"nn": NN, "nt": NT, "tn": TN}[mode]
    ne, no = len(extras), len(out_dtypes)

    def body(a_ref, b_ref, *rest):
        e_refs, o_refs = rest[:ne], rest[ne:ne + no]

        def finish(r):
            for o_ref, v in zip(o_refs, epi(r, *[e[...] for e in e_refs])):
                o_ref[...] = v.astype(o_ref.dtype)

        if nk == 1:
            finish(_bdot(a_ref[...], b_ref[...], dn))
            return
        acc_ref = rest[ne + no]
        k = pl.program_id(2)

        @pl.when(k == 0)
        def _():
            acc_ref[...] = _bdot(a_ref[...], b_ref[...], dn)

        @pl.when(jnp.logical_and(k > 0, k < nk - 1))
        def _():
            acc_ref[...] += _bdot(a_ref[...], b_ref[...], dn)

        @pl.when(k == nk - 1)
        def _():
            finish(acc_ref[...] + _bdot(a_ref[...], b_ref[...], dn))

    a_spec = pl.BlockSpec((tk, tm), lambda i, j, k: (k, i)) if mode == "tn" else pl.BlockSpec((tm, tk), lambda i, j, k: (i, k))
    col = (lambda j: j + jb) if b_order is None else functools.partial(b_order, tn)
    b_spec = pl.BlockSpec((tn, tk), lambda i, j, k: (j, k)) if mode == "nt" else pl.BlockSpec((tk, tn), lambda i, j, k: (k, col(j)))
    o_spec = pl.BlockSpec((tm, tn), lambda i, j, k: (i, j))
    res = _call(body, [a, b] + list(extras), name=name, grid=(M // tm, N // tn, nk),
                out_shape=tuple(jax.ShapeDtypeStruct((M, N), dt) for dt in out_dtypes),
                in_specs=[a_spec, b_spec] + [o_spec] * ne, out_specs=[o_spec] * no,
                scratch_shapes=[pltpu.VMEM((tm, tn), F32)] if nk > 1 else [], comm=comm, after=after)
    return res[0] if single else res


def _rowwise(fn, row_ins, bcast_ins, row_outs, acc_outs, name, rt=256, comm=None):
    L = row_ins[0][0].shape[-2]
    rt = _tile(L, rt, 16)
    nr, nb, no = len(row_ins), len(bcast_ins), len(row_outs)

    def body(*refs):
        i = pl.program_id(0)
        vals = [r[...] for r in refs[:nr + nb]]
        outs, accs = fn(*vals)
        for r, v in zip(refs[nr + nb:nr + nb + no], outs):
            r[...] = v.astype(r.dtype)
        acc_refs = refs[nr + nb + no:]

        @pl.when(i == 0)
        def _():
            for r in acc_refs:
                r[...] = jnp.zeros_like(r)

        for r, v in zip(acc_refs, accs):
            r[...] += v

    in_specs = []
    for spec in row_ins:
        w, cb = spec[1], spec[2]
        if len(spec) == 4:
            in_specs.append(pl.BlockSpec((None, rt, w), functools.partial(lambda i, cb, ld: (ld, i, cb), cb=cb, ld=spec[3])))
        else:
            in_specs.append(pl.BlockSpec((rt, w), functools.partial(lambda i, cb: (i, cb), cb=cb)))
    in_specs += [pl.BlockSpec(b.shape, lambda i: (0, 0)) for b in bcast_ins]
    out_specs = [pl.BlockSpec((rt, w), lambda i: (i, 0)) for w, _ in row_outs]
    out_specs += [pl.BlockSpec(s, lambda i: (0, 0)) for s in acc_outs]
    out_shape = [jax.ShapeDtypeStruct((L, w), dt) for w, dt in row_outs] + [jax.ShapeDtypeStruct(s, F32) for s in acc_outs]
    return _call(body, [s[0] for s in row_ins] + list(bcast_ins), name=name, grid=(L // rt,), out_shape=tuple(out_shape),
                 in_specs=in_specs, out_specs=out_specs, comm=comm)


def _whole(fn, ins, out_shapes, name):
    def body(*refs):
        outs = fn(*[r[...] for r in refs[:len(ins)]])
        for r, v in zip(refs[len(ins):], outs):
            r[...] = v.astype(r.dtype)

    return _call(body, list(ins), name=name, out_shape=tuple(jax.ShapeDtypeStruct(s, dt) for s, dt in out_shapes))


def _silu(x):
    return x * jax.nn.sigmoid(x)


def _rms(x, g):
    return (x * lax.rsqrt(jnp.mean(x * x, axis=-1, keepdims=True) + EPS)) * g


def _modnorm(x, g, shift, scale):
    return _rms(x, g) * (1.0 + scale) + shift


def _adamw(w, g, m, v):
    m = ADAM_B1 * m + (1.0 - ADAM_B1) * g
    v = ADAM_B2 * v + (1.0 - ADAM_B2) * jnp.square(g)
    m_hat = m / (1.0 - ADAM_B1 ** ADAM_STEP)
    v_hat = v / (1.0 - ADAM_B2 ** ADAM_STEP)
    delta = -ADAM_LR * (m_hat / (jnp.sqrt(v_hat) + ADAM_EPS) + ADAM_WD * w)
    return delta, m, v


def _lower_bound(lg):
    e = jnp.exp(lg - jnp.max(lg, axis=0, keepdims=True))
    return e[0:1] / jnp.sum(e, axis=0, keepdims=True)


def _hg_stages(hq_l, hf_l, hi_l, lb):
    C = hq_l[0].shape[0]
    row = lax.broadcasted_iota(jnp.int32, (C, C), 0)
    col = lax.broadcasted_iota(jnp.int32, (C, C), 1)
    tri = row >= col
    trif = tri.astype(F32)
    f_l = [lb + (1.0 - lb) * jax.nn.sigmoid(hf) for hf in hf_l]
    b_l = [_dot(trif, jnp.log(f), NN, precision=HIGHEST) for f in f_l]
    q_l = [_silu(hq) for hq in hq_l]
    m_l = [b[C // 2 - 1:C // 2] for b in b_l]
    bl_l = [b[C - 1:C] for b in b_l]
    sc_l = [jnp.where(tri, _bdot(q * jnp.exp(b - m), (1.0 - f) * jnp.exp(m - b), NT), 0.0)
            for q, f, b, m in zip(q_l, f_l, b_l, m_l)]
    o1_l = [_bdot(sc, hi, NN) for sc, hi in zip(sc_l, hi_l)]
    u_l = [_bdot(hi, (1.0 - f) * jnp.exp(bl - b), TN) for hi, f, b, bl in zip(hi_l, f_l, b_l, bl_l)]
    qb_l = [q * jnp.exp(b) for q, b in zip(q_l, b_l)]
    dec_l = [jnp.exp(bl) for bl in bl_l]
    return list(zip(o1_l, u_l, qb_l, dec_l))


def _hg_out(o, hgate, gout):
    return _rms(o, gout) * _silu(hgate)


HG_STAGE = 8
HG_GROUP = 32


def _hgrn_fwd(p4, lb_logits, gout, H, comm=None):
    L = p4.shape[0]
    C = HG_CHUNK
    GR = _tile(L // C, HG_GROUP, 1)
    T = GR * C
    N = L // T

    def body(hq_ref, hf_ref, hi_ref, hg_ref, lg_ref, gout_ref, o_ref, s_ref, st_ref):
        @pl.when(pl.program_id(1) == 0)
        def _():
            st_ref[...] = jnp.zeros_like(st_ref)

        lb = _lower_bound(lg_ref[...])
        st = st_ref[...]
        for c0 in range(0, GR, HG_STAGE):
            rows_l = [pl.ds(ci * C, C) for ci in range(c0, min(c0 + HG_STAGE, GR))]
            parts = _hg_stages([hq_ref[r, :] for r in rows_l], [hf_ref[r, :] for r in rows_l],
                               [hi_ref[r, :] for r in rows_l], lb)
            for ci, rows, (o1, u, qb, dec) in zip(range(c0, GR), rows_l, parts):
                s_ref[0, ci] = st
                o = o1 + _bdot(qb, st, NT)
                st = st * dec + u
                o_ref[rows, :] = _hg_out(o, hg_ref[rows, :], gout_ref[...]).astype(o_ref.dtype)
        st_ref[...] = st

    blk = lambda s: pl.BlockSpec((T, HG_DK), functools.partial(lambda h, n, s: (n, s * H + h), s=s))
    return _call(
        body, [p4, p4, p4, p4, lb_logits, gout], name="hgrn_fwd", grid=(H, N),
        out_shape=(jax.ShapeDtypeStruct((L, H * HG_DK), BF16), jax.ShapeDtypeStruct((H, N * GR, HG_DK, HG_DK), F32)),
        in_specs=[blk(0), blk(1), blk(2), blk(3), pl.BlockSpec((2, HG_DK), lambda h, n: (0, h)),
                  pl.BlockSpec((1, HG_DK), lambda h, n: (0, 0))],
        out_specs=(pl.BlockSpec((T, HG_DK), lambda h, n: (n, h)),
                   pl.BlockSpec((1, GR, HG_DK, HG_DK), lambda h, n: (h, n, 0, 0))),
        scratch_shapes=[pltpu.VMEM((HG_DK, HG_DK), F32)], comm=comm)


def _hgrn_bwd(p4, lb_logits, gout, s_all, d_out, H, comm=None):
    L = p4.shape[0]
    C = HG_CHUNK
    GR = _tile(L // C, HG_GROUP, 1)
    T = GR * C
    N = L // T

    def body(hq_ref, hf_ref, hi_ref, hg_ref, lg_ref, gout_ref, s_ref, do_ref,
             dq_ref, df_ref, di_ref, dg_ref, dlb_ref, dgo_ref, dst_ref):
        @pl.when(pl.program_id(1) == 0)
        def _():
            dst_ref[...] = jnp.zeros_like(dst_ref)
            dlb_ref[...] = jnp.zeros_like(dlb_ref)

        @pl.when(jnp.logical_and(pl.program_id(0) == 0, pl.program_id(1) == 0))
        def _():
            dgo_ref[...] = jnp.zeros_like(dgo_ref)

        lb = _lower_bound(lg_ref[...])
        dst = dst_ref[...]
        d_lb = jnp.zeros((1, HG_DK), F32)
        d_go = jnp.zeros((1, HG_DK), F32)
        for c0 in reversed(range(0, GR, HG_STAGE)):
            dst, d_lb_c, d_go_c = chunks_bwd(list(range(c0, min(c0 + HG_STAGE, GR))), lb, dst, hq_ref, hf_ref, hi_ref,
                                             hg_ref, gout_ref, s_ref, do_ref, dq_ref, df_ref, di_ref, dg_ref)
            d_lb += d_lb_c
            d_go += d_go_c
        dst_ref[...] = dst
        dlb_ref[...] += d_lb
        dgo_ref[...] += d_go

    def chunks_bwd(idx, lb, dst, hq_ref, hf_ref, hi_ref, hg_ref, gout_ref, s_ref, do_ref, dq_ref, df_ref, di_ref, dg_ref):
        n = len(idx)
        rows_l = [pl.ds(ci * C, C) for ci in idx]
        hq_l, hf_l, hi_l = ([r[rows, :] for rows in rows_l] for r in (hq_ref, hf_ref, hi_ref))
        st_l = [s_ref[0, ci] for ci in idx]
        row = lax.broadcasted_iota(jnp.int32, (C, C), 0)
        col = lax.broadcasted_iota(jnp.int32, (C, C), 1)
        tri = row >= col
        trif = tri.astype(F32)
        every = lambda fn, *ls: [fn(*a) for a in zip(*ls)]
        sg_l = every(jax.nn.sigmoid, hf_l)
        f_l = every(lambda sg: lb + (1.0 - lb) * sg, sg_l)
        b_l = every(lambda f: _dot(trif, jnp.log(f), NN, precision=HIGHEST), f_l)
        q_l = every(_silu, hq_l)
        m_l = every(lambda b: b[C // 2 - 1:C // 2], b_l)
        bl_l = every(lambda b: b[C - 1:C], b_l)
        e_qm_l = every(lambda b, m: jnp.exp(b - m), b_l, m_l)
        e_km_l = every(lambda b, m: jnp.exp(m - b), b_l, m_l)
        e_kl_l = every(lambda b, bl: jnp.exp(bl - b), b_l, bl_l)
        e_q_l = every(jnp.exp, b_l)
        dec_l = every(jnp.exp, bl_l)
        qe_l = every(lambda q, e: q * e, q_l, e_qm_l)
        ke_l = every(lambda f, e: (1.0 - f) * e, f_l, e_km_l)
        kd_l = every(lambda f, e: (1.0 - f) * e, f_l, e_kl_l)
        qb_l = every(lambda q, e: q * e, q_l, e_q_l)
        sc_l = every(lambda qe, ke: jnp.where(tri, _bdot(qe, ke, NT), 0.0), qe_l, ke_l)
        o_l = every(lambda sc, hi, qb, st: _bdot(sc, hi, NN) + _bdot(qb, st, NT), sc_l, hi_l, qb_l, st_l)
        vj_l = every(lambda o, rows: jax.vjp(_hg_out, o, hg_ref[rows, :], gout_ref[...])[1](do_ref[rows, :]), o_l, rows_l)
        do_l = [v[0] for v in vj_l]
        dsc_l = every(lambda do, hi: jnp.where(tri, _bdot(do, hi, NT), 0.0), do_l, hi_l)
        dv1_l = every(lambda sc, do: _bdot(sc, do, TN), sc_l, do_l)
        dqe_l = every(lambda dsc, ke: _bdot(dsc, ke, NN), dsc_l, ke_l)
        dke_l = every(lambda dsc, qe: _bdot(dsc, qe, TN), dsc_l, qe_l)
        dqb_l = every(lambda do, st: _bdot(do, st, NN), do_l, st_l)
        own_l = every(lambda do, qb: _bdot(do, qb, TN), do_l, qb_l)
        dst_next_l = [None] * n
        for j in reversed(range(n)):
            dst_next_l[j] = dst
            dst = own_l[j] + dst * dec_l[j]
        dv_l = every(lambda dv1, kd, dn: dv1 + _bdot(kd, dn, NT), dv1_l, kd_l, dst_next_l)
        dkd_l = every(lambda hi, dn: _bdot(hi, dn, NN), hi_l, dst_next_l)
        ddec_l = every(lambda dn, st: jnp.sum(dn * st, axis=0, keepdims=True), dst_next_l, st_l)
        rowi = lax.broadcasted_iota(jnp.int32, (C, HG_DK), 0)
        tq_l = every(lambda a, b_: a * b_, dqe_l, qe_l)
        tk_l = every(lambda a, b_: a * b_, dke_l, ke_l)
        td_l = every(lambda a, b_: a * b_, dkd_l, kd_l)
        tb_l = every(lambda a, b_: a * b_, dqb_l, qb_l)
        db_l = every(lambda tq, tk, td, tb, ddec, dec: tq - tk - td + tb
                     + jnp.where(rowi == C // 2 - 1, jnp.sum(tk - tq, axis=0, keepdims=True), 0.0)
                     + jnp.where(rowi == C - 1, jnp.sum(td, axis=0, keepdims=True) + ddec * dec, 0.0),
                     tq_l, tk_l, td_l, tb_l, ddec_l, dec_l)
        dlf_l = every(lambda db: _dot(trif, db, TN, precision=HIGHEST), db_l)
        dk_l = every(lambda dke, e1, dkd, e2: dke * e1 + dkd * e2, dke_l, e_km_l, dkd_l, e_kl_l)
        df_l = every(lambda dlf, f, dk: dlf / f - dk, dlf_l, f_l, dk_l)
        d_lb = jnp.zeros((1, HG_DK), F32)
        d_go = jnp.zeros((1, HG_DK), F32)
        for j, rows in enumerate(rows_l):
            sg, hq = sg_l[j], hq_l[j]
            df_ref[rows, :] = (df_l[j] * (1.0 - lb) * sg * (1.0 - sg)).astype(df_ref.dtype)
            sq = jax.nn.sigmoid(hq)
            dq = dqe_l[j] * e_qm_l[j] + dqb_l[j] * e_q_l[j]
            dq_ref[rows, :] = (dq * (sq * (1.0 + hq * (1.0 - sq)))).astype(dq_ref.dtype)
            di_ref[rows, :] = dv_l[j].astype(di_ref.dtype)
            dg_ref[rows, :] = vj_l[j][1].astype(dg_ref.dtype)
            d_lb += jnp.sum(df_l[j] * (1.0 - sg), axis=0, keepdims=True)
            d_go += vj_l[j][2]
        return dst, d_lb, d_go

    blk = lambda s: pl.BlockSpec((T, HG_DK), functools.partial(lambda h, n, s: (N - 1 - n, s * H + h), s=s))
    oblk = pl.BlockSpec((T, HG_DK), lambda h, n: (N - 1 - n, h))
    vec = pl.BlockSpec((1, HG_DK), lambda h, n: (0, h))
    W = H * HG_DK
    return _call(
        body, [p4, p4, p4, p4, lb_logits, gout, s_all, d_out], name="hgrn_bwd", grid=(H, N),
        out_shape=tuple([jax.ShapeDtypeStruct((L, W), BF16)] * 4 + [jax.ShapeDtypeStruct((1, W), F32), jax.ShapeDtypeStruct((1, HG_DK), F32)]),
        in_specs=[blk(0), blk(1), blk(2), blk(3), pl.BlockSpec((2, HG_DK), lambda h, n: (0, h)),
                  pl.BlockSpec((1, HG_DK), lambda h, n: (0, 0)),
                  pl.BlockSpec((1, GR, HG_DK, HG_DK), lambda h, n: (h, N - 1 - n, 0, 0)), oblk],
        out_specs=(oblk, oblk, oblk, oblk, vec, pl.BlockSpec((1, HG_DK), lambda h, n: (0, 0))),
        scratch_shapes=[pltpu.VMEM((HG_DK, HG_DK), F32)], comm=comm)


def _bucket_ids():
    i = jnp.arange(AT_BLOCK, dtype=jnp.int32)[:, None]
    j = jnp.arange(2 * AT_BLOCK, dtype=jnp.int32)[None, :]
    n = jnp.maximum(i - j + AT_BLOCK, 0)
    nf = jnp.maximum(n, 1).astype(F32)
    large = MAX_EXACT + (jnp.log(nf / MAX_EXACT) / math.log(MAX_DISTANCE / MAX_EXACT) * (N_BUCKETS - MAX_EXACT)).astype(jnp.int32)
    large = jnp.minimum(large, N_BUCKETS - 1)
    return jnp.where(n < MAX_EXACT, n, large).reshape(1, -1)


def _onehot(bucket):
    ids = lax.broadcasted_iota(jnp.int32, (N_BUCKETS, bucket.shape[1]), 0)
    return (ids == bucket).astype(F32)


AT_PAIR = 2


def _attn_probs(qn_l, kn_l, bias_g, sink, first, scale):
    rows = qn_l[0].shape[0]
    i = jnp.bitwise_and(lax.broadcasted_iota(jnp.int32, (rows, AT_BLOCK), 0), AT_BLOCK - 1)
    j = lax.broadcasted_iota(jnp.int32, (rows, AT_BLOCK), 1)
    n = len(qn_l)
    lp_l = [_bdot(qn_l[s], kn_l[s], NT) * scale + bias_g[:, :AT_BLOCK] for s in range(n)]
    lc_l = [_bdot(qn_l[s], kn_l[s + 1], NT) * scale + bias_g[:, AT_BLOCK:] for s in range(n)]
    seen = [jnp.logical_and(j > i, jnp.logical_not(first))] + [j > i] * (n - 1)
    lp_l = [jnp.where(seen[s], lp_l[s], NEG_INF) for s in range(n)]
    lc_l = [jnp.where(j <= i, lc, NEG_INF) for lc in lc_l]
    m_l = [jnp.maximum(jnp.maximum(jnp.max(lp, axis=-1, keepdims=True), jnp.max(lc, axis=-1, keepdims=True)), sink)
           for lp, lc in zip(lp_l, lc_l)]
    pp_l = [jnp.exp(lp - m) for lp, m in zip(lp_l, m_l)]
    pc_l = [jnp.exp(lc - m) for lc, m in zip(lc_l, m_l)]
    ps_l = [jnp.exp(sink - m) for m in m_l]
    den_l = [jnp.sum(pp, axis=-1, keepdims=True) + jnp.sum(pc, axis=-1, keepdims=True) + ps
             for pp, pc, ps in zip(pp_l, pc_l, ps_l)]
    return [(pp / den, pc / den, ps / den) for pp, pc, ps, den in zip(pp_l, pc_l, ps_l, den_l)]


def _sink_rows(sk_ref, G, j=0):
    head = lax.broadcasted_iota(jnp.int32, (G * AT_BLOCK, 1), 0) // AT_BLOCK
    sink = jnp.zeros((G * AT_BLOCK, 1), F32)
    for g in range(G):
        sink = jnp.where(head == g, sk_ref[j, g:g + 1, :], sink)
    return sink


def _group_rows(ref, s, G, DH, j=0):
    B = AT_BLOCK
    rows = ref[pl.ds(s * B, B), pl.ds(j * G * DH, G * DH)].astype(F32)
    return jnp.concatenate([rows[:, g * DH:(g + 1) * DH] for g in range(G)], axis=0)


def _ungroup_rows(val, G):
    B = AT_BLOCK
    return jnp.concatenate([val[g * B:(g + 1) * B] for g in range(G)], axis=1)


def _attn_specs(cols, G, DH):
    P, B = AT_PAIR, AT_BLOCK
    pk = _heads_per_tile(DH)
    q0, k0, v0 = cols[0] // (pk * G * DH), cols[1] // (pk * DH), cols[2] // (pk * DH)
    assert cols[0] % (pk * G * DH) == 0 and cols[1] % (pk * DH) == 0 and cols[2] % (pk * DH) == 0 and (G * DH) % LANES == 0
    qblk = pl.BlockSpec((P * B, pk * G * DH), lambda h, m: (m, q0 + h))
    kblk = lambda c0, off: pl.BlockSpec((B, pk * DH), functools.partial(
        lambda h, m, c0, off: (jnp.maximum(P * m + off - 1, 0), c0 + h), c0=c0, off=off))
    return qblk, [kblk(k0, off) for off in range(P + 1)], [kblk(v0, off) for off in range(P + 1)]


def _heads_per_tile(DH):
    return LANES // DH if DH < LANES else 1


def _head_of(ref, j, DH):
    return ref[...][:, j * DH:(j + 1) * DH]


def _attn_fwd(proj, cols, qg, kg, sinks, bias, AH, KVH, comm=None):
    L, DH = proj.shape[0], qg.shape[1]
    G = AH // KVH
    NB = L // AT_BLOCK
    scale = DH ** -0.5

    P, B = AT_PAIR, AT_BLOCK
    assert NB % P == 0

    def body(q_ref, *rest):
        k_refs, v_refs = rest[:P + 1], rest[P + 1:2 * P + 2]
        qg_ref, kg_ref, sk_ref, b_ref, o_ref = rest[2 * P + 2:]
        first = pl.program_id(1) == 0
        o_heads = []
        for j in range(pk):
            kn_l = [_rms(_head_of(r, j, DH), kg_ref[...]) for r in k_refs]
            v_l = [_head_of(r, j, DH) for r in v_refs]
            qn_l = [_rms(_group_rows(q_ref, s, G, DH, j), qg_ref[...]) for s in range(P)]
            probs = _attn_probs(qn_l, kn_l, b_ref[j * G:(j + 1) * G].reshape(G * B, 2 * B), _sink_rows(sk_ref, G, j),
                                first, scale)
            o_l = [_bdot(pp, v_l[s], NN) + _bdot(pc, v_l[s + 1], NN) for s, (pp, pc, _) in enumerate(probs)]
            o_heads.append([_ungroup_rows(o, G) for o in o_l])
        for s in range(P):
            o_ref[pl.ds(s * B, B), :] = jnp.concatenate([o_heads[j][s] for j in range(pk)], axis=1).astype(o_ref.dtype)

    pk = _heads_per_tile(DH)
    assert KVH % pk == 0
    qblk, kspecs, vspecs = _attn_specs(cols, G, DH)
    return _call(
        body, [proj] * (2 * P + 3) + [qg, kg, sinks, bias], name="attn_fwd", grid=(KVH // pk, NB // P),
        out_shape=jax.ShapeDtypeStruct((L, AH * DH), BF16),
        in_specs=[qblk] + kspecs + vspecs
        + [pl.BlockSpec((1, DH), lambda h, m: (0, 0)), pl.BlockSpec((1, DH), lambda h, m: (0, 0)),
           pl.BlockSpec((pk, G, 1), lambda h, m: (h, 0, 0)), pl.BlockSpec((pk * G, B, 2 * B), lambda h, m: (h, 0, 0))],
        out_specs=pl.BlockSpec((P * B, pk * G * DH), lambda h, m: (m, h)), comm=comm)


def _attn_bwd(proj, cols, qg, kg, sinks, bias, d_o, AH, KVH, comm=None):
    L, DH = proj.shape[0], qg.shape[1]
    G = AH // KVH
    NB = L // AT_BLOCK
    B = AT_BLOCK
    scale = DH ** -0.5

    P = AT_PAIR
    assert NB % P == 0

    def body(q_ref, *rest):
        k_refs, v_refs = rest[:P + 1], rest[P + 1:2 * P + 2]
        qg_ref, kg_ref, sk_ref, b_ref, do_ref, dq_ref, dk_ref, dv_ref, dqg_ref, dkg_ref, dsk_ref, db_ref = rest[2 * P + 2:]
        m = pl.program_id(1)
        first = m == 0

        @pl.when(first)
        def _():
            for r in (dk_ref, dv_ref, dsk_ref, db_ref):
                r[...] = jnp.zeros_like(r)

        @pl.when(jnp.logical_and(first, pl.program_id(0) == 0))
        def _():
            dqg_ref[...] = jnp.zeros_like(dqg_ref)
            dkg_ref[...] = jnp.zeros_like(dkg_ref)

        kgv, qgv = kg_ref[...], qg_ref[...]
        heads = [head(j, q_ref, k_refs, v_refs, sk_ref, b_ref, do_ref, dsk_ref, db_ref, kgv, qgv, first) for j in range(pk)]
        for s in range(P):
            dq_ref[pl.ds(s * B, B), :] = jnp.concatenate([hd[0][s] for hd in heads], axis=1).astype(dq_ref.dtype)
        for t in range(P + 1):
            r = pl.multiple_of(jnp.maximum(P * m + t - 1, 0) * B, B)
            dk_ref[pl.ds(r, B), :] += jnp.concatenate([hd[1][t] for hd in heads], axis=1)
            dv_ref[pl.ds(r, B), :] += jnp.concatenate([hd[2][t] for hd in heads], axis=1)
        dqg_ref[...] += sum(hd[3] for hd in heads)
        dkg_ref[...] += sum(hd[4] for hd in heads)

    def head(j, q_ref, k_refs, v_refs, sk_ref, b_ref, do_ref, dsk_ref, db_ref, kgv, qgv, first):
        k_fw = [jax.vjp(_rms, _head_of(r, j, DH), kgv) for r in k_refs]
        v_l = [_head_of(r, j, DH) for r in v_refs]
        kn_l = [f[0] for f in k_fw]
        q_fw = [jax.vjp(_rms, _group_rows(q_ref, s, G, DH, j), qgv) for s in range(P)]
        qn_l = [f[0] for f in q_fw]
        probs = _attn_probs(qn_l, kn_l, b_ref[j * G:(j + 1) * G].reshape(G * B, 2 * B), _sink_rows(sk_ref, G, j), first, scale)
        pp_l, pc_l, ps_l = ([p[t] for p in probs] for t in range(3))
        do_l = [_group_rows(do_ref, s, G, DH, j).astype(BF16) for s in range(P)]
        dvp_l = [_bdot(pp, do, TN) for pp, do in zip(pp_l, do_l)]
        dvc_l = [_bdot(pc, do, TN) for pc, do in zip(pc_l, do_l)]
        dpp_l = [_bdot(do_l[s], v_l[s], NT) for s in range(P)]
        dpc_l = [_bdot(do_l[s], v_l[s + 1], NT) for s in range(P)]
        dsum_l = [jnp.sum(dpp * pp, axis=-1, keepdims=True) + jnp.sum(dpc * pc, axis=-1, keepdims=True)
                  for dpp, pp, dpc, pc in zip(dpp_l, pp_l, dpc_l, pc_l)]
        dlp_l = [pp * (dpp - ds) for pp, dpp, ds in zip(pp_l, dpp_l, dsum_l)]
        dlc_l = [pc * (dpc - ds) for pc, dpc, ds in zip(pc_l, dpc_l, dsum_l)]
        dsk_ref[j] += sum(jnp.sum((-ps * ds).reshape(G, B, 1), axis=1) for ps, ds in zip(ps_l, dsum_l))
        db_ref[j * G:(j + 1) * G, :, :B] += sum(dlp_l).reshape(G, B, B)
        db_ref[j * G:(j + 1) * G, :, B:] += sum(dlc_l).reshape(G, B, B)
        dlp_l, dlc_l = [d * scale for d in dlp_l], [d * scale for d in dlc_l]
        dqn_l = [_bdot(dlp_l[s], kn_l[s], NN) + _bdot(dlc_l[s], kn_l[s + 1], NN) for s in range(P)]
        dq_l = [q_fw[s][1](dqn_l[s]) for s in range(P)]
        dkn_l = [jnp.zeros((B, DH), F32)] * (P + 1)
        dvk_l = [jnp.zeros((B, DH), F32)] * (P + 1)
        for s in range(P):
            dkn_l[s] = dkn_l[s] + _bdot(dlp_l[s], qn_l[s], TN)
            dkn_l[s + 1] = dkn_l[s + 1] + _bdot(dlc_l[s], qn_l[s], TN)
            dvk_l[s] = dvk_l[s] + dvp_l[s]
            dvk_l[s + 1] = dvk_l[s + 1] + dvc_l[s]
        dk_l = [k_fw[t][1](dkn_l[t]) for t in range(P + 1)]
        return ([_ungroup_rows(d[0], G) for d in dq_l], [d[0] for d in dk_l], dvk_l,
                sum(d[1] for d in dq_l), sum(d[1] for d in dk_l))

    qblk, kspecs, vspecs = _attn_specs(cols, G, DH)
    pk = _heads_per_tile(DH)
    assert KVH % pk == 0
    oblk = pl.BlockSpec((P * B, pk * G * DH), lambda h, m: (m, h))
    accblk = pl.BlockSpec((L, pk * DH), lambda h, m: (0, h))
    vecblk = pl.BlockSpec((1, DH), lambda h, m: (0, 0))
    return _call(
        body, [proj] * (2 * P + 3) + [qg, kg, sinks, bias, d_o], name="attn_bwd", grid=(KVH // pk, NB // P),
        out_shape=(jax.ShapeDtypeStruct((L, AH * DH), BF16), jax.ShapeDtypeStruct((L, KVH * DH), F32),
                   jax.ShapeDtypeStruct((L, KVH * DH), F32), jax.ShapeDtypeStruct((1, DH), F32),
                   jax.ShapeDtypeStruct((1, DH), F32), jax.ShapeDtypeStruct((KVH, G, 1), F32),
                   jax.ShapeDtypeStruct((AH, B, 2 * B), F32)),
        in_specs=[qblk] + kspecs + vspecs
        + [pl.BlockSpec((1, DH), lambda h, m: (0, 0)), pl.BlockSpec((1, DH), lambda h, m: (0, 0)),
           pl.BlockSpec((pk, G, 1), lambda h, m: (h, 0, 0)), pl.BlockSpec((pk * G, B, 2 * B), lambda h, m: (h, 0, 0)), oblk],
        out_specs=(oblk, accblk, accblk, vecblk, vecblk, pl.BlockSpec((pk, G, 1), lambda h, m: (h, 0, 0)),
                   pl.BlockSpec((pk * G, B, 2 * B), lambda h, m: (h, 0, 0))), comm=comm)


def _heads_first(t, nh):
    L = t.shape[0]
    return jnp.transpose(t.reshape(L, nh, t.shape[1] // nh), (1, 0, 2))


def _heads_last(t):
    nh, L, dh = t.shape
    return jnp.transpose(t, (1, 0, 2)).reshape(L, nh * dh)


def _softmax0(lg):
    e = jnp.exp(lg - jnp.max(lg, axis=0, keepdims=True))
    return e[0:1] / jnp.sum(e, axis=0, keepdims=True)


def _ada_update_call(fn, c_all, d_cols, w, m, v, rt):
    D, n = w.shape

    def body(c_ref, d_ref, w_ref, m_ref, v_ref, g_out, dl_out, m_out, v_out):
        outs, _ = fn(c_ref[...], d_ref[...], w_ref[...], m_ref[...], v_ref[...])
        for r, val in zip((g_out, dl_out, m_out, v_out), outs):
            r[...] = val

    wblk = pl.BlockSpec((rt, n), lambda i: (i, 0))
    return _call(
        body, [c_all, d_cols, w, m, v], name="update_ada", grid=(D // rt,), out_shape=tuple([jax.ShapeDtypeStruct((D, n), F32)] * 4),
        in_specs=[pl.BlockSpec((N_DEV, rt), lambda i: (0, i)), pl.BlockSpec((N_DEV, n), lambda i: (0, 0)), wblk, wblk, wblk],
        out_specs=(wblk, wblk, wblk, wblk))


def kernel(x, c, w_ada, b_ada, norm1_g, norm2_g, w_in, hg_lb_logits, hg_out_norm_g, q_norm_g, k_norm_g, attn_sinks, rel_bias_table, w_branch_hg, w_branch_attn, w_out, w_ff1, w_ff2, loss_target, m_w_ada, m_b_ada, m_norm1_g, m_norm2_g, m_w_in, m_hg_lb_logits, m_hg_out_norm_g, m_q_norm_g, m_k_norm_g, m_attn_sinks, m_rel_bias_table, m_w_branch_hg, m_w_branch_attn, m_w_out, m_w_ff1, m_w_ff2, v_w_ada, v_b_ada, v_norm1_g, v_norm2_g, v_w_in, v_hg_lb_logits, v_hg_out_norm_g, v_q_norm_g, v_k_norm_g, v_attn_sinks, v_rel_bias_table, v_w_branch_hg, v_w_branch_attn, v_w_out, v_w_ff1, v_w_ff2):
    cc = lax.axis_index("c")
    me = 4 * lax.axis_index("x") + 2 * lax.axis_index("y") + cc
    x2 = x[0]
    tgt = loss_target[0]
    L, D = x2.shape
    HGW = hg_lb_logits.shape[1]
    H = HGW // HG_DK
    AH = attn_sinks.shape[1]
    DH = q_norm_g.shape[1]
    ATW = AH * DH
    BW = w_in.shape[2]
    INW = BW * N_DEV
    A = BW // LANES
    assert BW == LANES * A + LANES // 2
    KVW = (INW - 4 * HGW - ATW - 2 * D) // 2
    KVH = KVW // DH
    G = AH // KVH
    ADA_N = w_ada.shape[2]
    PAIR = 2 * A + 1

    c_all = _gather_small(c, me, "gather_c")[:, 0, :]
    b_cols = lax.dynamic_slice(b_ada, (0, me * ADA_N), (1, ADA_N))
    (ada_cols,) = _whole(lambda cv, w, b: (_bdot(_silu(cv), w, NN) + b,), [c_all, w_ada[0], b_cols],
                         [((N_DEV, ADA_N), F32)], "ada_fwd")
    ada_all = _gather_small(ada_cols, me, "gather_ada")
    ada_row = lax.dynamic_slice(ada_all, (0, me, 0), (N_DEV, 1, ADA_N)).reshape(1, 6 * D)

    w_in_b = w_in[0].astype(BF16)
    src_in = jnp.where(cc == 0, jnp.pad(w_in_b, ((0, 0), (0, LANES // 2))), jnp.pad(w_in_b, ((0, 0), (LANES // 2, 0))))
    (src_in,) = _behind([src_in], [ada_row])
    shift1, scale1, gate1, shift2, scale2, gate2 = [ada_row[:, i * D:(i + 1) * D] for i in range(6)]
    w_in_gapped, w_in_mid = _ag_w_in(src_in, A, D, INW)
    w_in_full = _patch_mid(w_in_gapped, w_in_mid, A)

    wnames = ("bhg", "bat", "out", "ff1", "ff2")
    small = ("bhg", "bat", "out")
    waxis = dict(zip(wnames, (1, 1, 0, 1, 0)))
    wsrc = dict(zip(wnames, (w_branch_hg, w_branch_attn, w_out, w_ff1, w_ff2)))
    wblk = {k: wsrc[k][0].astype(BF16) for k in wnames}
    wf = {}

    (h,) = _rowwise(lambda xv, g, sh, sc: ((_modnorm(xv, g, sh, sc),), ()), [(x2, D, 0)], [norm1_g, shift1, scale1],
                    [(D, BF16)], [], "norm1")
    o4, oa = 4 * HGW, 4 * HGW + ATW + 2 * KVW
    r1, r2 = wblk["ff1"].shape[0], wblk["ff2"].shape[0]
    assert o4 % D == 0

    def proj_order(tn_, j):
        t4, tg, ng = o4 // tn_, oa // tn_, (INW - oa) // tn_
        return jnp.where(j < t4, j, jnp.where(j < t4 + ng, j + (tg - t4), j - ng))

    cm = _Comm()
    hs = {k: _ag_ici(cm, wblk[k], waxis[k]) for k in small}
    hs["ff2"] = _ag_ici(cm, wblk["ff2"], waxis["ff2"], rows=(0, r2 // 4))
    proj = _mm(h, w_in_full, "nn", F32, "proj", comm=cm, b_order=proj_order)
    half = {k: cm.result(hs[k]) for k in hs}
    p4 = pg = proj
    GATE0 = o4 // D
    AT0 = o4 + (INW - oa)

    cm = _Comm()
    hs = {k: _ag_d2d(cm, half[k], waxis[k]) for k in small}
    hs["ff1"] = _ag_ici(cm, wblk["ff1"], waxis["ff1"], rows=(0, r1 // 2))
    o_hg, s_all = _hgrn_fwd(p4, hg_lb_logits, hg_out_norm_g, H, comm=cm)
    wf["bhg"], wf["bat"], wf["out"], half["ff1"] = (cm.result(hs[k]) for k in ("bhg", "bat", "out", "ff1"))

    bucket = _bucket_ids()
    (bias_flat,) = _whole(lambda tb, bk: (_dot(tb, _onehot(bk), TN, precision=HIGHEST),), [rel_bias_table, bucket],
                          [((AH, AT_BLOCK * 2 * AT_BLOCK), F32)], "bias_fwd")
    bias = bias_flat.reshape(AH, AT_BLOCK, 2 * AT_BLOCK)
    at_cols = (AT0, AT0 + ATW, AT0 + ATW + KVW)
    sinks3 = attn_sinks.reshape(KVH, G, 1)
    cm = _Comm()
    hs = {"ff1": _ag_ici(cm, wblk["ff1"], waxis["ff1"], rows=(r1 // 2, r1), into=half["ff1"])}
    o_at = _attn_fwd(proj, at_cols, q_norm_g, k_norm_g, sinks3, bias, AH, KVH, comm=cm)
    half["ff1"] = cm.result(hs["ff1"])

    bh = _mm(o_hg, wf["bhg"], "nn", F32, "branch_hg")
    ba = _mm(o_at, wf["bat"], "nn", F32, "branch_at")

    def merge_fn(bhv, bav, ghg, gat):
        return jax.nn.sigmoid(ghg) * bhv + jax.nn.sigmoid(gat) * bav

    cm = _Comm()
    hs = {"ff1": _ag_d2d(cm, half["ff1"], waxis["ff1"]),
          "ff2": _ag_ici(cm, wblk["ff2"], waxis["ff2"], rows=(r2 // 4, 3 * r2 // 8), into=half["ff2"])}
    (merged,) = _rowwise(lambda *a: ((merge_fn(*a),), ()), [(bh, D, 0), (ba, D, 0), (pg, D, GATE0), (pg, D, GATE0 + 1)], [],
                         [(D, BF16)], [], "merge", comm=cm)
    wf["ff1"], half["ff2"] = cm.result(hs["ff1"]), cm.result(hs["ff2"])
    cm = _Comm()
    hs = {"ff2": _ag_ici(cm, wblk["ff2"], waxis["ff2"], rows=(3 * r2 // 8, r2 // 2), into=half["ff2"])}
    mo = _mm(merged, wf["out"], "nn", F32, "out_proj", comm=cm)
    half["ff2"] = cm.result(hs["ff2"])

    def resid1(xv, mov, g1, g2n, sh, sc):
        x1v = xv + g1 * mov
        return (x1v, _modnorm(x1v, g2n, sh, sc)), ()

    x1, h2 = _rowwise(resid1, [(x2, D, 0), (mo, D, 0)], [gate1, norm2_g, shift2, scale2], [(D, F32), (D, BF16)], [], "resid1")
    cm = _Comm()
    hs = {"ff2": _ag_ici(cm, wblk["ff2"], waxis["ff2"], rows=(r2 // 2, r2), into=half["ff2"])}
    u, act = _mm(h2, wf["ff1"], "nn", (F32, BF16), "ff1", comm=cm, epi=lambda r: (r, jnp.square(jnp.maximum(r, 0.0))))
    half["ff2"] = cm.result(hs["ff2"])
    cm = _Comm()
    hs = {"ff2": _ag_d2d(cm, half["ff2"], waxis["ff2"])}
    _call(lambda: None, [], name="ag_d2d_ff2", out_shape=(), comm=cm)
    wf["ff2"] = cm.result(hs["ff2"])
    ff = _mm(act, wf["ff2"], "nn", F32, "ff2")

    def loss_fn(x1v, ffv, tv, g2):
        e = x1v + g2 * ffv - tv
        dy = e * (1.0 / D)
        return (dy, dy * g2), (jnp.sum(e * e, axis=0, keepdims=True), jnp.sum(dy * ffv, axis=0, keepdims=True))

    dy, d_ff, sq_sum, d_gate2 = _rowwise(loss_fn, [(x1, D, 0), (ff, D, 0), (tgt, D, 0)], [gate2],
                                         [(D, F32), (D, BF16)], [(1, D), (1, D)], "loss")
    loss = lax.psum(jnp.sum(sq_sum) * (0.5 / D), ("x", "y", "c"))

    owner_base = jnp.stack([me ^ r for r in CHIP_RELS]).astype(jnp.int32)
    gw, recv1, part, recv2 = {}, {}, {}, {}
    gw["ff2"] = _mm(act, d_ff, "tn", BF16, "dw_ff2")
    cm = _Comm()
    hh = _rs_d2d(cm, gw["ff2"], waxis["ff2"])
    d_u = _mm(d_ff, wf["ff2"], "nt", BF16, "d_act", comm=cm, extras=[u], epi=lambda r, uv: (r * (2.0 * jnp.maximum(uv, 0.0)),))
    part["ff2"] = _rs_add(gw["ff2"], cm.result(hh), waxis["ff2"], owner_base, "rs_add_ff2")
    rows_ff2 = part["ff2"].shape[1]
    cm = _Comm()
    hh = _rs_ici(cm, part["ff2"], rows=(0, rows_ff2 // 2))
    gw["ff1"] = _mm(h2, d_u, "tn", BF16, "dw_ff1", comm=cm)
    cm2 = _Comm()
    hh2 = _rs_ici(cm2, part["ff2"], rows=(rows_ff2 // 2, rows_ff2), recv=cm.result(hh))
    hh1 = _rs_d2d(cm2, gw["ff1"], waxis["ff1"])
    d_h2 = _mm(d_u, wf["ff1"], "nt", F32, "d_h2", comm=cm2)
    recv2["ff2"] = cm2.result(hh2)
    part["ff1"] = _rs_add(gw["ff1"], cm2.result(hh1), waxis["ff1"], owner_base, "rs_add_ff1")

    def norm2_bwd(dh2v, x1v, dyv, mov, g2n, sh, sc, g1):
        _, vjp = jax.vjp(_modnorm, x1v, g2n, sh, sc)
        dx, dg, dsh, dsc = vjp(dh2v)
        dx1 = dyv + dx
        return (dx1, dx1 * g1), (dg, dsh, dsc, jnp.sum(dx1 * mov, axis=0, keepdims=True))

    d_x1, d_mo, d_g2n, d_shift2, d_scale2, d_gate1 = _rowwise(
        norm2_bwd, [(d_h2, D, 0), (x1, D, 0), (dy, D, 0), (mo, D, 0)], [norm2_g, shift2, scale2, gate1],
        [(D, F32), (D, BF16)], [(1, D)] * 4, "norm2_bwd")
    gw["out"] = _mm(merged, d_mo, "tn", BF16, "dw_out")
    d_merged = _mm(d_mo, wf["out"], "nt", F32, "d_merged")

    def merge_bwd(dmv, bhv, bav, ghg, gat):
        _, vjp = jax.vjp(merge_fn, bhv, bav, ghg, gat)
        return vjp(dmv), ()

    d_bh, d_ba, d_ghg, d_gat = _rowwise(merge_bwd, [(d_merged, D, 0), (bh, D, 0), (ba, D, 0), (pg, D, GATE0), (pg, D, GATE0 + 1)], [],
                                        [(D, BF16)] * 4, [], "merge_bwd")
    gw["bhg"] = _mm(o_hg, d_bh, "tn", BF16, "dw_bhg")
    gw["bat"] = _mm(o_at, d_ba, "tn", BF16, "dw_bat")
    d_ohg = _mm(d_bh, wf["bhg"], "nt", F32, "d_ohg")
    d_oat = _mm(d_ba, wf["bat"], "nt", BF16, "d_oat")
    rows_ff1 = part["ff1"].shape[1]
    cut_ff1 = rows_ff1 // 4
    cm = _Comm()
    hf1 = _rs_ici(cm, part["ff1"], rows=(0, cut_ff1))
    d_hq, d_hf, d_hi, d_hg, d_lb, d_gout_h = _hgrn_bwd(p4, hg_lb_logits, hg_out_norm_g, s_all, d_ohg, H, comm=cm)
    cm2 = _Comm()
    hf1 = _rs_ici(cm2, part["ff1"], rows=(cut_ff1, rows_ff1), recv=cm.result(hf1))
    hh = {k: _rs_d2d(cm2, gw[k], waxis[k]) for k in small}
    d_aq, dkp, dvp, d_qg, d_kg, d_sk, d_bias = _attn_bwd(proj, at_cols, q_norm_g, k_norm_g, sinks3, bias,
                                                         d_oat, AH, KVH, comm=cm2)
    recv2["ff1"] = cm2.result(hf1)
    for k in small:
        part[k] = _rs_add(gw[k], cm2.result(hh[k]), waxis[k], owner_base, "rs_add_" + k)
    d_ak = dkp.astype(BF16)
    d_av = dvp.astype(BF16)
    d_proj = jnp.concatenate([d_hq, d_hf, d_hi, d_hg, d_aq, d_ak, d_av, d_ghg, d_gat], axis=1)
    cm = _Comm()
    hh = {k: _rs_ici(cm, part[k]) for k in small}
    gw_in = _mm(h, d_proj, "tn", BF16, "dw_in", comm=cm)
    for k in small:
        recv2[k] = cm.result(hh[k])

    wm = LANES * A
    cm = _Comm()
    hi_ = cm.inp(gw_in)
    h_main, h_mid = cm.out((4, D, wm), BF16), cm.out((4, D, LANES), BF16)
    for i, r in enumerate(CHIP_RELS):
        def main_view(ref, p, r=r):
            o = p["me"] ^ r ^ 1
            return ref.at[:, pl.ds(pl.multiple_of((PAIR * (o // 2) + (A + 1) * (1 - p["c"])) * LANES, LANES), wm)]

        def mid_view(ref, p, r=r):
            o = p["me"] ^ r
            return ref.at[:, pl.ds(pl.multiple_of((PAIR * (o // 2) + A) * LANES, LANES), LANES)]

        cm.copy(hi_, main_view, h_main, _slot_view(i), 1)
        cm.copy(hi_, mid_view, h_mid, _slot_view(i), 1)
    _call(lambda: None, [], name="rs_d2d_in", out_shape=(), comm=cm)
    chip = jnp.stack([(me ^ r) // 2 for r in CHIP_RELS]).astype(jnp.int32)
    part_main = _rs_add(gw_in, cm.result(h_main), 1, PAIR * chip + (A + 1) * cc, "rs_add_in_main", tw=LANES)
    part_mid = _rs_add(gw_in, cm.result(h_mid), 1, PAIR * chip + A, "rs_add_in_mid", tw=LANES)
    rs_in = _rs_split_start([part_main, part_mid], "rs_in_start")
    d_h = _mm(d_proj, w_in_full, "nt", F32, "d_h", tn=D, after=[rs_in["token"]])

    def norm1_bwd(dhv, xv, dx1v, g1n, sh, sc):
        _, vjp = jax.vjp(_modnorm, xv, g1n, sh, sc)
        dx, dg, dsh, dsc = vjp(dhv)
        return (dx1v + dx,), (dg, dsh, dsc)

    grad_x, d_g1n, d_shift1, d_scale1 = _rowwise(norm1_bwd, [(d_h, D, 0), (x2, D, 0), (d_x1, D, 0)],
                                                 [norm1_g, shift1, scale1], [(D, F32)], [(1, D)] * 3, "norm1_bwd")

    def sum4(p0, p1, p2, p3):
        return ((p0.astype(F32) + p1.astype(F32)) + p2.astype(F32)) + p3.astype(F32)

    def update_fn(w, m, v, p0, p1, p2, p3):
        g = sum4(p0, p1, p2, p3)
        delta, mn, vn = _adamw(w, g, m, v)
        return (g, delta, mn, vn), ()

    wmv = dict(zip(wnames, ((w_branch_hg, m_w_branch_hg, v_w_branch_hg), (w_branch_attn, m_w_branch_attn, v_w_branch_attn),
                            (w_out, m_w_out, v_w_out), (w_ff1, m_w_ff1, v_w_ff1), (w_ff2, m_w_ff2, v_w_ff2))))
    res = {}

    def update(k, p, rx):
        w, m, v = (t[0] for t in wmv[k])
        n = w.shape[1]
        ins = [(t, n, 0) for t in (w, m, v)] + [(p, n, 0, 0)] + [(rx, n, 0, i) for i in range(3)]
        res[k] = [t[None] for t in _rowwise(update_fn, ins, [], [(n, F32)] * 4, [], "update_" + k)]

    for k in wnames:
        update(k, part[k], recv2[k])
    (part_main, part_mid), (rx_main, rx_mid) = _rs_split_wait(rs_in, [grad_x] + [res[k][0] for k in wnames], "rs_in_wait")
    g_main, = _rowwise(lambda *p: ((sum4(*p),), ()), [(part_main, wm, 0, 0)] + [(rx_main, wm, 0, i) for i in range(3)], [],
                       [(wm, F32)], [], "sum_in_main")
    g_mid, = _rowwise(lambda *p: ((sum4(*p),), ()), [(part_mid, LANES, 0, 0)] + [(rx_mid, LANES, 0, i) for i in range(3)], [],
                      [(LANES, F32)], [], "sum_in_mid")
    g_in = jnp.where(cc == 0, jnp.concatenate([g_main, g_mid[:, :LANES // 2]], axis=1),
                     jnp.concatenate([g_mid[:, LANES // 2:], g_main], axis=1))

    def update_given(w, m, v, g):
        delta, mn, vn = _adamw(w, g, m, v)
        return (g, delta, mn, vn), ()

    res["in"] = [t[None] for t in _rowwise(update_given, [(t, BW, 0) for t in (w_in[0], m_w_in[0], v_w_in[0], g_in)], [],
                                           [(BW, F32)] * 4, [], "update_in")]

    d_sinks = d_sk.reshape(1, AH)
    (d_table_t,) = _whole(lambda db, bk: (_dot(db, _onehot(bk), NT, precision=HIGHEST),),
                          [d_bias.reshape(AH, AT_BLOCK * 2 * AT_BLOCK), bucket], [((AH, N_BUCKETS), F32)], "bias_bwd")
    smalls = [d_g1n, d_g2n, d_lb, d_gout_h, d_qg, d_kg, d_sinks, d_table_t.T.reshape(1, N_BUCKETS * AH)]
    widths = [s.shape[1] for s in smalls]
    lanes = [-(-w // LANES) * LANES for w in widths]
    smalls = [jnp.pad(s, ((0, 0), (0, p - w))) for s, w, p in zip(smalls, widths, lanes)]
    tail_row = jnp.concatenate([d_shift1, d_scale1, d_gate1, d_shift2, d_scale2, d_gate2] + smalls, axis=1)
    (tail_row,) = _behind([tail_row], [g_mid])
    tail_all = _gather_small(tail_row, me, "gather_tail")[:, 0, :]
    d_ada_all, packed = tail_all[:, :6 * D], tail_all[:, 6 * D:]
    d_ada_cols = lax.dynamic_slice(d_ada_all, (0, me * ADA_N), (N_DEV, ADA_N))

    def ada_update(cv, dav, w, m, v):
        g = _bdot(_silu(cv), dav, TN)
        delta, mn, vn = _adamw(w, g, m, v)
        return (g, delta, mn, vn), ()

    res["ada"] = [t[None] for t in _ada_update_call(ada_update, c_all, d_ada_cols, w_ada[0], m_w_ada[0], v_w_ada[0], _tile(D, 256, 16))]

    offs = [sum(lanes[:i]) for i in range(len(lanes))]

    def small_update(pk, dada, lg, *wmv_flat):
        tot = pk[0:1]
        for d in range(1, N_DEV):
            tot = tot + pk[d:d + 1]
        gb = dada[0:1]
        for d in range(1, N_DEV):
            gb = gb + dada[d:d + 1]
        gs = [tot[:, offs[i]:offs[i] + widths[i]] for i in range(len(widths))]
        _, lb_vjp = jax.vjp(_softmax0, lg)
        (g_lg,) = lb_vjp(gs[2])
        grads = [gb, gs[0], gs[1], g_lg, gs[3], gs[4], gs[5], gs[6], gs[7]]
        outs = []
        for i, g in enumerate(grads):
            w, m, v = wmv_flat[3 * i:3 * i + 3]
            delta, mn, vn = _adamw(w, g, m, v)
            outs += [g, delta, mn, vn]
        return tuple(outs)

    tbl = lambda t: t.reshape(1, N_BUCKETS * AH)
    small_wmv = [(b_ada, m_b_ada, v_b_ada), (norm1_g, m_norm1_g, v_norm1_g), (norm2_g, m_norm2_g, v_norm2_g),
                 (hg_lb_logits, m_hg_lb_logits, v_hg_lb_logits), (hg_out_norm_g, m_hg_out_norm_g, v_hg_out_norm_g),
                 (q_norm_g, m_q_norm_g, v_q_norm_g), (k_norm_g, m_k_norm_g, v_k_norm_g),
                 (attn_sinks, m_attn_sinks, v_attn_sinks),
                 (tbl(rel_bias_table), tbl(m_rel_bias_table), tbl(v_rel_bias_table))]
    flat = [t for trip in small_wmv for t in trip]
    out_shapes = [(trip[0].shape, F32) for trip in small_wmv for _ in range(4)]
    sres = _whole(small_update, [packed, d_ada_all, hg_lb_logits] + flat, out_shapes, "small_update")
    names_small = ("b_ada", "norm1_g", "norm2_g", "lb", "gout", "qg", "kg", "sinks", "table")
    for i, k in enumerate(names_small):
        r = sres[4 * i:4 * i + 4]
        if k == "table":
            r = [t.reshape(N_BUCKETS, AH) for t in r]
        res[k] = r

    order = ("ada", "b_ada", "norm1_g", "norm2_g", "in", "lb", "gout", "qg", "kg", "sinks", "table", "bhg", "bat", "out", "ff1", "ff2")
    outs = [loss, grad_x[None]]
    for j in range(4):
        outs += [res[k][j] for k in order]
    return tuple(outs)
```

```python
import functools
import math

import jax
import jax.numpy as jnp
from jax import lax
from jax.experimental import pallas as pl
from jax.experimental.pallas import tpu as pltpu

F32 = jnp.float32
BF16 = jnp.bfloat16
EPS = 1e-6
NEG_INF = -1e30
HG_DK = 128
HG_CHUNK = 64
AT_BLOCK = 128
N_BUCKETS = 32
MAX_EXACT = 16
MAX_DISTANCE = 128
N_DEV = 8
LANES = 128
VMEM_LIMIT = 56 * 1024 * 1024
ADAM_LR, ADAM_B1, ADAM_B2, ADAM_EPS, ADAM_WD, ADAM_STEP = 0.001, 0.9, 0.999, 1e-08, 0.01, 10
HIGHEST = lax.Precision.HIGHEST
MESH = pl.DeviceIdType.MESH
ANY = pl.BlockSpec(memory_space=pl.ANY)
CHIP_RELS = (0, 4, 2, 6)

NN = (((1,), (0,)), ((), ()))
NT = (((1,), (1,)), ((), ()))
TN = (((0,), (0,)), ((), ()))


def _tile(n, pref, unit):
    if n <= pref:
        return n
    t = (pref // unit) * unit
    while t >= unit:
        if n % t == 0:
            return t
        t -= unit
    return n


def _dot(a, b, dn, precision=None):
    return lax.dot_general(a, b, dn, preferred_element_type=F32, precision=precision)


def _bdot(a, b, dn):
    return _dot(a.astype(BF16), b.astype(BF16), dn)


def _position():
    x, y, c = lax.axis_index("x"), lax.axis_index("y"), lax.axis_index("c")
    return dict(x=x, y=y, c=c, me=4 * x + 2 * y + c)


def _peer_position(p, rel):
    x = 1 - p["x"] if rel & 4 else p["x"]
    y = 1 - p["y"] if rel & 2 else p["y"]
    c = 1 - p["c"] if rel & 1 else p["c"]
    return dict(x=x, y=y, c=c, me=4 * x + 2 * y + c)


class _Comm:
    def __init__(self):
        self.ins, self.outs, self.alias, self.plans, self.res = [], [], {}, [], None

    def inp(self, arr):
        self.ins.append(arr)
        return ("i", len(self.ins) - 1)

    def out(self, shape, dtype, alias=None):
        self.outs.append(jax.ShapeDtypeStruct(tuple(shape), dtype))
        if alias is not None:
            self.alias[alias[1]] = len(self.outs) - 1
        return ("o", len(self.outs) - 1)

    def copy(self, src, src_view, dst, dst_view, rel):
        self.plans.append((src, src_view, dst, dst_view, rel))

    def result(self, handle):
        return self.res[handle[1]]

    def build(self, in_refs, out_refs, send_sems, recv_sems):
        pos = _position()
        ref = lambda h: in_refs[h[1]] if h[0] == "i" else out_refs[h[1]]
        ops = []
        for k, (src, sv, dst, dv, rel) in enumerate(self.plans):
            s = sv(ref(src), pos)
            if rel == 0:
                cp = pltpu.make_async_copy(s, dv(ref(dst), pos), send_sems.at[k])
                ops.append((cp.start, cp.wait))
                continue
            peer = _peer_position(pos, rel)
            mk = lambda d: pltpu.make_async_remote_copy(
                src_ref=s, dst_ref=d, send_sem=send_sems.at[k], recv_sem=recv_sems.at[k],
                device_id=(peer["x"], peer["y"], peer["c"]), device_id_type=MESH)
            out_cp, in_cp = mk(dv(ref(dst), pos)), mk(dv(ref(dst), peer))

            def wait(out_cp=out_cp, in_cp=in_cp):
                out_cp.wait_send()
                in_cp.wait_recv()

            ops.append((out_cp.start, wait))
        return ops


def _call(body, args, *, name, out_shape, in_specs=None, out_specs=None, grid=None, scratch_shapes=(), comm=None,
          prefetch=None, aliases=None, after=()):
    single = not isinstance(out_shape, (tuple, list))
    out_shape = (out_shape,) if single else tuple(out_shape)
    n_in, n_out, n_scr = len(args), len(out_shape), len(scratch_shapes)
    vm = pl.BlockSpec(memory_space=pltpu.VMEM)
    in_specs = [vm] * n_in if in_specs is None else list(in_specs)
    out_specs = [vm] * n_out if out_specs is None else (list(out_specs) if isinstance(out_specs, (tuple, list)) else [out_specs])
    n_pf = 0 if prefetch is None else len(prefetch)
    kw = {} if aliases is None else {"input_output_aliases": dict(aliases)}
    if comm is None and after:
        n_dep = len(after)

        def fn(*refs):
            body(*refs[:n_pf + n_in], *refs[n_pf + n_in + n_dep:])

        all_args, all_scratch = list(args) + list(after), list(scratch_shapes)
        in_specs = in_specs + [ANY] * n_dep
    elif comm is None:
        fn = body
        all_args, all_scratch = list(args), list(scratch_shapes)
    else:
        n_ci, n_co, n_x = len(comm.ins), len(comm.outs), len(comm.plans)

        def fn(*refs):
            pf, refs = refs[:n_pf], refs[n_pf:]
            o_in, c_in = refs[:n_in], refs[n_in:n_in + n_ci]
            o_out = refs[n_in + n_ci:n_in + n_ci + n_out]
            c_out = refs[n_in + n_ci + n_out:n_in + n_ci + n_out + n_co]
            scr = refs[n_in + n_ci + n_out + n_co:]
            ops = comm.build(c_in, c_out, scr[n_scr], scr[n_scr + 1])
            if grid:
                first = functools.reduce(jnp.logical_and, [pl.program_id(i) == 0 for i in range(len(grid))])
                last = functools.reduce(jnp.logical_and, [pl.program_id(i) == g - 1 for i, g in enumerate(grid)])

                @pl.when(first)
                def _():
                    for start, _w in ops:
                        start()
            else:
                for start, _w in ops:
                    start()
            body(*pf, *o_in, *o_out, *scr[:n_scr])
            if grid:
                @pl.when(last)
                def _():
                    for _s, wait in ops:
                        wait()
            else:
                for _s, wait in ops:
                    wait()

        all_args = list(args) + list(comm.ins)
        in_specs = in_specs + [ANY] * n_ci
        out_shape = out_shape + tuple(comm.outs)
        out_specs = out_specs + [ANY] * n_co
        all_scratch = list(scratch_shapes) + [pltpu.SemaphoreType.DMA((n_x,)), pltpu.SemaphoreType.DMA((n_x,))]
        kw["input_output_aliases"] = {n_pf + n_in + i: n_out + o for i, o in comm.alias.items()}
    sem = None if grid is None else ("arbitrary",) * len(grid)
    params = pltpu.CompilerParams(dimension_semantics=sem, vmem_limit_bytes=VMEM_LIMIT)
    if prefetch is None:
        spec = dict(in_specs=in_specs, out_specs=tuple(out_specs), scratch_shapes=all_scratch)
        if grid is not None:
            spec["grid"] = grid
    else:
        spec = dict(grid_spec=pltpu.PrefetchScalarGridSpec(
            num_scalar_prefetch=n_pf, grid=grid, in_specs=in_specs, out_specs=tuple(out_specs), scratch_shapes=all_scratch))
        all_args = list(prefetch) + all_args
    res = pl.pallas_call(fn, name=name, out_shape=out_shape, compiler_params=params, **spec, **kw)(*all_args)
    res = list(res)
    if comm is not None:
        comm.res = res[n_out:]
        res = res[:n_out]
    return res[0] if single else res


def _whole_view(ref, pos):
    return ref


def _block_view(axis, n, index, rows=None):
    def view(ref, pos):
        off = pl.multiple_of(index(pos) * n, n)
        if rows is None:
            return ref.at[:, pl.ds(off, n)] if axis == 1 else ref.at[pl.ds(off, n), :]
        lo, cnt = rows[0], rows[1] - rows[0]
        if axis == 1:
            return ref.at[pl.ds(lo, cnt), pl.ds(off, n)]
        return ref.at[pl.ds(pl.multiple_of(off + lo, 16), cnt), :]
    return view


def _rows_view(rows):
    def view(ref, pos):
        return ref if rows is None else ref.at[pl.ds(rows[0], rows[1] - rows[0]), :]
    return view


def _slot_view(i, rows=None):
    def view(ref, pos):
        return ref.at[i] if rows is None else ref.at[i, pl.ds(rows[0], rows[1] - rows[0]), :]
    return view


def _exchange(items, name):
    cm = _Comm()
    for a, rel in items:
        cm.copy(cm.inp(a), _whole_view, cm.out(a.shape, a.dtype), _whole_view, rel)
    _call(lambda: None, [], name=name, out_shape=(), comm=cm)
    return cm.res


def _gather_small(v, me, name):
    cm = _Comm()
    hi, ho = cm.inp(v), cm.out((N_DEV,) + v.shape, v.dtype)
    for rel in range(N_DEV):
        cm.copy(hi, _whole_view, ho, lambda ref, p: ref.at[p["me"]], rel)
    _call(lambda: None, [], name=name, out_shape=(), comm=cm)
    return cm.result(ho)


def _ag_ici(cm, blk, axis, rows=None, into=None):
    n = blk.shape[axis]
    shape = list(blk.shape)
    shape[axis] = n * N_DEV
    hi = cm.inp(blk)
    ho = cm.out(shape, blk.dtype) if into is None else cm.out(shape, blk.dtype, alias=cm.inp(into))
    own = _block_view(axis, n, lambda p: p["me"], rows)
    for rel in CHIP_RELS:
        cm.copy(hi, _rows_view(rows), ho, own, rel)
    return ho


def _ag_d2d(cm, full, axis):
    n = full.shape[axis] // N_DEV
    hi = cm.inp(full)
    ho = cm.out(full.shape, full.dtype, alias=hi)
    for r in CHIP_RELS:
        v = _block_view(axis, n, functools.partial(lambda p, r: p["me"] ^ r, r=r))
        cm.copy(hi, v, ho, v, 1)
    return ho


def _rs_d2d(cm, gw, axis):
    n = gw.shape[axis] // N_DEV
    shape = list(gw.shape)
    shape[axis] = n
    hi, ho = cm.inp(gw), cm.out([4] + shape, gw.dtype)
    for i, r in enumerate(CHIP_RELS):
        cm.copy(hi, _block_view(axis, n, functools.partial(lambda p, r: p["me"] ^ r ^ 1, r=r)), ho, _slot_view(i), 1)
    return ho


def _rs_ici(cm, part, rows=None, recv=None):
    if recv is None:
        ho = cm.out((3,) + part.shape[1:], part.dtype)
    else:
        ho = cm.out(recv.shape, recv.dtype, alias=cm.inp(recv))
    hi = cm.inp(part)
    for i in (1, 2, 3):
        cm.copy(hi, _slot_view(i, rows), ho, _slot_view(i - 1, rows), CHIP_RELS[i])
    return ho


def _rs_add(gw, recv, axis, base, name, tw=None):
    _, R, n = recv.shape
    fan = 1
    if axis == 1:
        tw = n if tw is None else tw
        fan = max(f for f in (4, 3, 2, 1) if (n // tw) % f == 0)
        gw_specs = [pl.BlockSpec((R, tw), functools.partial(lambda i, t, b, k: (0, b[i] + fan * t + k), k=k)) for k in range(fan)]
        rv_spec = pl.BlockSpec((None, R, tw * fan), lambda i, t, b: (i, 0, t))
        grid = (4, n // (tw * fan))
    else:
        tw = _tile(n, 1024, LANES)
        gw_specs = [pl.BlockSpec((R, tw), lambda i, t, b: (b[i], t))]
        rv_spec = pl.BlockSpec((None, R, tw), lambda i, t, b: (i, 0, t))
        grid = (4, n // tw)

    def body(b_ref, *refs):
        g_refs, r_ref, o_ref = refs[:fan], refs[fan], refs[fan + 1]
        g = g_refs[0][...] if fan == 1 else jnp.concatenate([g[...] for g in g_refs], axis=1)
        o_ref[...] = (g.astype(F32) + r_ref[...].astype(F32)).astype(o_ref.dtype)

    return _call(body, [gw] * fan + [recv], name=name, out_shape=jax.ShapeDtypeStruct(recv.shape, recv.dtype), grid=grid,
                 in_specs=gw_specs + [rv_spec], out_specs=rv_spec, prefetch=[base])


HBM_SPEC = pl.BlockSpec(memory_space=pltpu.HBM)
SEM_SPEC = pl.BlockSpec(memory_space=pltpu.SEMAPHORE)
SPLIT_PARAMS = pltpu.CompilerParams(has_side_effects=pltpu.SideEffectType.DATAFLOW_SIDE_EFFECTING)


def _split_copies(refs, plans, send_sems, recv_sems):
    pos = _position()
    out = []
    for k, (si, sv, li, lv, rel) in enumerate(plans):
        peer = _peer_position(pos, rel)
        mk = lambda d: pltpu.make_async_remote_copy(
            src_ref=sv(refs[si], pos), dst_ref=d, send_sem=send_sems.at[k], recv_sem=recv_sems.at[k],
            device_id=(peer["x"], peer["y"], peer["c"]), device_id_type=MESH)
        out.append((mk(lv(refs[li], pos)), mk(lv(refs[li], peer))))
    return out


def _split_start(arrays, plans, name):
    n = len(arrays)

    def body(*refs):
        send_sems, recv_sems = refs[n], refs[n + 1]
        for out_cp, _ in _split_copies(refs[:n], plans, send_sems, recv_sems):
            out_cp.start()
        refs[-1][...] = jnp.zeros_like(refs[-1])

    sems = pltpu.SemaphoreType.DMA((len(plans),))
    res = pl.pallas_call(
        body, name=name,
        out_shape=(sems, sems) + tuple(pltpu.HBM(a.shape, a.dtype) for a in arrays) + (jax.ShapeDtypeStruct((8, LANES), F32),),
        in_specs=[HBM_SPEC] * n, out_specs=(SEM_SPEC, SEM_SPEC) + (HBM_SPEC,) * n + (pl.BlockSpec(memory_space=pltpu.VMEM),),
        input_output_aliases={i: 2 + i for i in range(n)}, compiler_params=SPLIT_PARAMS,
    )(*[pltpu.with_memory_space_constraint(a, pltpu.HBM) for a in arrays])
    return res[0], res[1], list(res[2:2 + n]), res[-1]


def _split_wait(send_sems, recv_sems, arrays, plans, after, name):
    n, na = len(arrays), len(after)

    def body(*refs):
        for out_cp, in_cp in _split_copies(refs[:n], plans, refs[n], refs[n + 1]):
            out_cp.wait_send()
            in_cp.wait_recv()

    res = pl.pallas_call(
        body, name=name, out_shape=tuple(pltpu.HBM(a.shape, a.dtype) for a in arrays),
        in_specs=[HBM_SPEC] * n + [SEM_SPEC, SEM_SPEC] + [ANY] * na, out_specs=(HBM_SPEC,) * n,
        input_output_aliases={i: i for i in range(n)}, compiler_params=SPLIT_PARAMS,
    )(*arrays, send_sems, recv_sems, *after)
    return list(res)


def _rs_split_start(parts, name):
    nw = len(parts)
    lands = [lax.empty((3,) + p.shape[1:], p.dtype) for p in parts]
    plans = [(s, _slot_view(i), nw + s, _slot_view(i - 1), CHIP_RELS[i]) for s in range(nw) for i in (1, 2, 3)]
    send_sems, recv_sems, arrays, token = _split_start(list(parts) + lands, plans, name)
    return dict(sems=(send_sems, recv_sems), arrays=arrays, plans=plans, token=token, nw=nw)


def _rs_split_wait(h, after, name):
    arrays = _split_wait(h["sems"][0], h["sems"][1], h["arrays"], h["plans"], after, name)
    return arrays[:h["nw"]], arrays[h["nw"]:]


def _behind(xs, tokens):
    out = lax.optimization_barrier((tuple(xs), tuple(tokens)))
    return list(out[0])


def _ag_w_in(src, a, D, INW):
    wm = LANES * a

    hd = D // 2
    ALL, TOP, BOT = (0, D), (0, hd), (hd, D)

    def main_place(ref, p, rows=ALL):
        off = pl.multiple_of(((2 * a + 1) * (p["me"] // 2) + (a + 1) * p["c"]) * LANES, LANES)
        return ref.at[pl.ds(rows[0], rows[1] - rows[0]), pl.ds(off, wm)]

    def main_src(ref, p):
        return ref.at[:, pl.ds(pl.multiple_of(p["c"] * LANES, LANES), wm)]

    def mid_src(ref, p):
        return ref.at[:, pl.ds(pl.multiple_of((1 - p["c"]) * wm, LANES), LANES)]

    def mid_place(ref, p, rows=ALL):
        return ref.at[p["me"], pl.ds(rows[0], rows[1] - rows[0]), :]

    def body(src_ref, full_ref, mid_ref, send_sems, recv_sems):
        pos = _position()
        sib, xn, yn = (_peer_position(pos, r) for r in (1, 4, 2))
        dg = _peer_position(pos, 6)
        started = []

        def remote(k, s, d, to):
            return pltpu.make_async_remote_copy(src_ref=s, dst_ref=d, send_sem=send_sems.at[k], recv_sem=recv_sems.at[k],
                                                device_id=(to["x"], to["y"], to["c"]), device_id_type=MESH)

        def send(k, owner, rows, to, from_src=False):
            for j, (src_v, place) in enumerate(((main_src, main_place), (mid_src, mid_place))):
                s = src_v(src_ref, pos) if from_src else place(full_ref if j == 0 else mid_ref, owner, rows)
                cp = remote(k + j, s, place(full_ref if j == 0 else mid_ref, owner, rows), to)
                cp.start()
                started.append(cp)

        def landed(k, owner, rows, frm):
            for j, place in enumerate((main_place, mid_place)):
                ref = full_ref if j == 0 else mid_ref
                remote(k + j, place(ref, owner, rows), place(ref, owner, rows), frm).wait_recv()

        local = [pltpu.make_async_copy(main_src(src_ref, pos), main_place(full_ref, pos), send_sems.at[18]),
                 pltpu.make_async_copy(mid_src(src_ref, pos), mid_place(mid_ref, pos), send_sems.at[19])]
        for cp in local:
            cp.start()
        send(0, pos, ALL, sib, from_src=True)
        send(2, pos, ALL, xn, from_src=True)
        send(4, pos, ALL, yn, from_src=True)
        landed(2, xn, ALL, xn)
        send(10, xn, ALL, sib)
        send(6, xn, TOP, yn)
        landed(4, yn, ALL, yn)
        send(12, yn, ALL, sib)
        send(8, yn, BOT, xn)
        landed(6, dg, TOP, yn)
        send(14, dg, TOP, sib)
        landed(8, dg, BOT, xn)
        send(16, dg, BOT, sib)
        sib_of = lambda p: _peer_position(p, 1)
        landed(0, sib, ALL, sib)
        landed(10, sib_of(xn), ALL, sib)
        landed(12, sib_of(yn), ALL, sib)
        landed(14, sib_of(dg), TOP, sib)
        landed(16, sib_of(dg), BOT, sib)
        for cp in started:
            cp.wait_send()
        for cp in local:
            cp.wait()

    return _call(body, [src], name="ag_w_in", in_specs=[ANY], out_specs=[ANY, ANY],
                 out_shape=(jax.ShapeDtypeStruct((D, INW), BF16), jax.ShapeDtypeStruct((N_DEV, D, LANES), BF16)),
                 scratch_shapes=[pltpu.SemaphoreType.DMA((20,)), pltpu.SemaphoreType.DMA((20,))])


def _patch_mid(full, mid, a):
    D = full.shape[0]

    def body(full_ref, e_ref, o_ref, out_ref):
        out_ref[...] = e_ref[...] + o_ref[...]

    return _call(body, [full, mid, mid], name="patch_mid", grid=(N_DEV // 2,),
                 out_shape=jax.ShapeDtypeStruct(full.shape, full.dtype),
                 in_specs=[ANY, pl.BlockSpec((None, D, LANES), lambda j: (2 * j, 0, 0)),
                           pl.BlockSpec((None, D, LANES), lambda j: (2 * j + 1, 0, 0))],
                 out_specs=pl.BlockSpec((D, LANES), lambda j: (0, (2 * a + 1) * j + a)), aliases={0: 0})


MM_RESIDENT = 2048


def _mm(a, b, mode, out_dtype, name, b_off=0, n=None, comm=None, extras=(), epi=None, tn=None, after=(), b_order=None):
    if mode == "nn":
        (M, K), (K2, N) = a.shape, b.shape
    elif mode == "nt":
        (M, K), (N, K2) = a.shape, b.shape
    else:
        (K, M), (K2, N) = a.shape, b.shape
    assert K == K2, (a.shape, b.shape, mode)
    if n is not None:
        N = n
    single = not isinstance(out_dtype, (tuple, list))
    out_dtypes = (out_dtype,) if single else tuple(out_dtype)
    if epi is None:
        epi = lambda r: (r,)
    tk = K if K <= MM_RESIDENT else (MM_RESIDENT if K % MM_RESIDENT == 0 else _tile(K, 512, LANES))
    nk = K // tk
    if M > MM_RESIDENT and mode == "tn" and N <= MM_RESIDENT and not b_off:
        tm, tn = _tile(M, 512, LANES), N
    elif nk > 1:
        tm, tn = _tile(M, 1024, LANES), _tile(N, tn or 1024, LANES)
    else:
        tm = _tile(M, MM_RESIDENT, LANES)
        tn = _tile(math.gcd(N, b_off) if b_off else N, tn or 512, LANES)
    jb = b_off // tn
    dn = {"nn": NN, "nt": NT, "tn": TN}[mode]
    ne, no = len(extras), len(out_dtypes)

    def body(a_ref, b_ref, *rest):
        e_refs, o_refs = rest[:ne], rest[ne:ne + no]

        def finish(r):
            for o_ref, v in zip(o_refs, epi(r, *[e[...] for e in e_refs])):
                o_ref[...] = v.astype(o_ref.dtype)

        if nk == 1:
            finish(_bdot(a_ref[...], b_ref[...], dn))
            return
        acc_ref = rest[ne + no]
        k = pl.program_id(2)

        @pl.when(k == 0)
        def _():
            acc_ref[...] = _bdot(a_ref[...], b_ref[...], dn)

        @pl.when(jnp.logical_and(k > 0, k < nk - 1))
        def _():
            acc_ref[...] += _bdot(a_ref[...], b_ref[...], dn)

        @pl.when(k == nk - 1)
        def _():
            finish(acc_ref[...] + _bdot(a_ref[...], b_ref[...], dn))

    a_spec = pl.BlockSpec((tk, tm), lambda i, j, k: (k, i)) if mode == "tn" else pl.BlockSpec((tm, tk), lambda i, j, k: (i, k))
    col = (lambda j: j + jb) if b_order is None else functools.partial(b_order, tn)
    b_spec = pl.BlockSpec((tn, tk), lambda i, j, k: (j, k)) if mode == "nt" else pl.BlockSpec((tk, tn), lambda i, j, k: (k, col(j)))
    o_spec = pl.BlockSpec((tm, tn), lambda i, j, k: (i, j))
    res = _call(body, [a, b] + list(extras), name=name, grid=(M // tm, N // tn, nk),
                out_shape=tuple(jax.ShapeDtypeStruct((M, N), dt) for dt in out_dtypes),
                in_specs=[a_spec, b_spec] + [o_spec] * ne, out_specs=[o_spec] * no,
                scratch_shapes=[pltpu.VMEM((tm, tn), F32)] if nk > 1 else [], comm=comm, after=after)
    return res[0] if single else res


def _rowwise(fn, row_ins, bcast_ins, row_outs, acc_outs, name, rt=256, comm=None):
    L = row_ins[0][0].shape[-2]
    rt = _tile(L, rt, 16)
    nr, nb, no = len(row_ins), len(bcast_ins), len(row_outs)

    def body(*refs):
        i = pl.program_id(0)
        vals = [r[...] for r in refs[:nr + nb]]
        outs, accs = fn(*vals)
        for r, v in zip(refs[nr + nb:nr + nb + no], outs):
            r[...] = v.astype(r.dtype)
        acc_refs = refs[nr + nb + no:]

        @pl.when(i == 0)
        def _():
            for r in acc_refs:
                r[...] = jnp.zeros_like(r)

        for r, v in zip(acc_refs, accs):
            r[...] += v

    in_specs = []
    for spec in row_ins:
        w, cb = spec[1], spec[2]
        if len(spec) == 4:
            in_specs.append(pl.BlockSpec((None, rt, w), functools.partial(lambda i, cb, ld: (ld, i, cb), cb=cb, ld=spec[3])))
        else:
            in_specs.append(pl.BlockSpec((rt, w), functools.partial(lambda i, cb: (i, cb), cb=cb)))
    in_specs += [pl.BlockSpec(b.shape, lambda i: (0, 0)) for b in bcast_ins]
    out_specs = [pl.BlockSpec((rt, w), lambda i: (i, 0)) for w, _ in row_outs]
    out_specs += [pl.BlockSpec(s, lambda i: (0, 0)) for s in acc_outs]
    out_shape = [jax.ShapeDtypeStruct((L, w), dt) for w, dt in row_outs] + [jax.ShapeDtypeStruct(s, F32) for s in acc_outs]
    return _call(body, [s[0] for s in row_ins] + list(bcast_ins), name=name, grid=(L // rt,), out_shape=tuple(out_shape),
                 in_specs=in_specs, out_specs=out_specs, comm=comm)


def _whole(fn, ins, out_shapes, name):
    def body(*refs):
        outs = fn(*[r[...] for r in refs[:len(ins)]])
        for r, v in zip(refs[len(ins):], outs):
            r[...] = v.astype(r.dtype)

    return _call(body, list(ins), name=name, out_shape=tuple(jax.ShapeDtypeStruct(s, dt) for s, dt in out_shapes))


def _silu(x):
    return x * jax.nn.sigmoid(x)


def _rms(x, g):
    return (x * lax.rsqrt(jnp.mean(x * x, axis=-1, keepdims=True) + EPS)) * g


def _modnorm(x, g, shift, scale):
    return _rms(x, g) * (1.0 + scale) + shift


def _adamw(w, g, m, v):
    m = ADAM_B1 * m + (1.0 - ADAM_B1) * g
    v = ADAM_B2 * v + (1.0 - ADAM_B2) * jnp.square(g)
    m_hat = m / (1.0 - ADAM_B1 ** ADAM_STEP)
    v_hat = v / (1.0 - ADAM_B2 ** ADAM_STEP)
    delta = -ADAM_LR * (m_hat / (jnp.sqrt(v_hat) + ADAM_EPS) + ADAM_WD * w)
    return delta, m, v


def _lower_bound(lg):
    e = jnp.exp(lg - jnp.max(lg, axis=0, keepdims=True))
    return e[0:1] / jnp.sum(e, axis=0, keepdims=True)


def _hg_stages(hq_l, hf_l, hi_l, lb):
    C = hq_l[0].shape[0]
    row = lax.broadcasted_iota(jnp.int32, (C, C), 0)
    col = lax.broadcasted_iota(jnp.int32, (C, C), 1)
    tri = row >= col
    trif = tri.astype(F32)
    f_l = [lb + (1.0 - lb) * jax.nn.sigmoid(hf) for hf in hf_l]
    b_l = [_dot(trif, jnp.log(f), NN, precision=HIGHEST) for f in f_l]
    q_l = [_silu(hq) for hq in hq_l]
    m_l = [b[C // 2 - 1:C // 2] for b in b_l]
    bl_l = [b[C - 1:C] for b in b_l]
    sc_l = [jnp.where(tri, _bdot(q * jnp.exp(b - m), (1.0 - f) * jnp.exp(m - b), NT), 0.0)
            for q, f, b, m in zip(q_l, f_l, b_l, m_l)]
    o1_l = [_bdot(sc, hi, NN) for sc, hi in zip(sc_l, hi_l)]
    u_l = [_bdot(hi, (1.0 - f) * jnp.exp(bl - b), TN) for hi, f, b, bl in zip(hi_l, f_l, b_l, bl_l)]
    qb_l = [q * jnp.exp(b) for q, b in zip(q_l, b_l)]
    dec_l = [jnp.exp(bl) for bl in bl_l]
    return list(zip(o1_l, u_l, qb_l, dec_l))


def _hg_out(o, hgate, gout):
    return _rms(o, gout) * _silu(hgate)


HG_STAGE = 8
HG_GROUP = 32


def _hgrn_fwd(p4, lb_logits, gout, H, comm=None):
    L = p4.shape[0]
    C = HG_CHUNK
    GR = _tile(L // C, HG_GROUP, 1)
    T = GR * C
    N = L // T

    def body(hq_ref, hf_ref, hi_ref, hg_ref, lg_ref, gout_ref, o_ref, s_ref, st_ref):
        @pl.when(pl.program_id(1) == 0)
        def _():
            st_ref[...] = jnp.zeros_like(st_ref)

        lb = _lower_bound(lg_ref[...])
        st = st_ref[...]
        for c0 in range(0, GR, HG_STAGE):
            rows_l = [pl.ds(ci * C, C) for ci in range(c0, min(c0 + HG_STAGE, GR))]
            parts = _hg_stages([hq_ref[r, :] for r in rows_l], [hf_ref[r, :] for r in rows_l],
                               [hi_ref[r, :] for r in rows_l], lb)
            for ci, rows, (o1, u, qb, dec) in zip(range(c0, GR), rows_l, parts):
                s_ref[0, ci] = st
                o = o1 + _bdot(qb, st, NT)
                st = st * dec + u
                o_ref[rows, :] = _hg_out(o, hg_ref[rows, :], gout_ref[...]).astype(o_ref.dtype)
        st_ref[...] = st

    blk = lambda s: pl.BlockSpec((T, HG_DK), functools.partial(lambda h, n, s: (n, s * H + h), s=s))
    return _call(
        body, [p4, p4, p4, p4, lb_logits, gout], name="hgrn_fwd", grid=(H, N),
        out_shape=(jax.ShapeDtypeStruct((L, H * HG_DK), BF16), jax.ShapeDtypeStruct((H, N * GR, HG_DK, HG_DK), F32)),
        in_specs=[blk(0), blk(1), blk(2), blk(3), pl.BlockSpec((2, HG_DK), lambda h, n: (0, h)),
                  pl.BlockSpec((1, HG_DK), lambda h, n: (0, 0))],
        out_specs=(pl.BlockSpec((T, HG_DK), lambda h, n: (n, h)),
                   pl.BlockSpec((1, GR, HG_DK, HG_DK), lambda h, n: (h, n, 0, 0))),
        scratch_shapes=[pltpu.VMEM((HG_DK, HG_DK), F32)], comm=comm)


def _hgrn_bwd(p4, lb_logits, gout, s_all, d_out, H, comm=None):
    L = p4.shape[0]
    C = HG_CHUNK
    GR = _tile(L // C, HG_GROUP, 1)
    T = GR * C
    N = L // T

    def body(hq_ref, hf_ref, hi_ref, hg_ref, lg_ref, gout_ref, s_ref, do_ref,
             dq_ref, df_ref, di_ref, dg_ref, dlb_ref, dgo_ref, dst_ref):
        @pl.when(pl.program_id(1) == 0)
        def _():
            dst_ref[...] = jnp.zeros_like(dst_ref)
            dlb_ref[...] = jnp.zeros_like(dlb_ref)

        @pl.when(jnp.logical_and(pl.program_id(0) == 0, pl.program_id(1) == 0))
        def _():
            dgo_ref[...] = jnp.zeros_like(dgo_ref)

        lb = _lower_bound(lg_ref[...])
        dst = dst_ref[...]
        d_lb = jnp.zeros((1, HG_DK), F32)
        d_go = jnp.zeros((1, HG_DK), F32)
        for c0 in reversed(range(0, GR, HG_STAGE)):
            dst, d_lb_c, d_go_c = chunks_bwd(list(range(c0, min(c0 + HG_STAGE, GR))), lb, dst, hq_ref, hf_ref, hi_ref,
                                             hg_ref, gout_ref, s_ref, do_ref, dq_ref, df_ref, di_ref, dg_ref)
            d_lb += d_lb_c
            d_go += d_go_c
        dst_ref[...] = dst
        dlb_ref[...] += d_lb
        dgo_ref[...] += d_go

    def chunks_bwd(idx, lb, dst, hq_ref, hf_ref, hi_ref, hg_ref, gout_ref, s_ref, do_ref, dq_ref, df_ref, di_ref, dg_ref):
        n = len(idx)
        rows_l = [pl.ds(ci * C, C) for ci in idx]
        hq_l, hf_l, hi_l = ([r[rows, :] for rows in rows_l] for r in (hq_ref, hf_ref, hi_ref))
        st_l = [s_ref[0, ci] for ci in idx]
        row = lax.broadcasted_iota(jnp.int32, (C, C), 0)
        col = lax.broadcasted_iota(jnp.int32, (C, C), 1)
        tri = row >= col
        trif = tri.astype(F32)
        every = lambda fn, *ls: [fn(*a) for a in zip(*ls)]
        sg_l = every(jax.nn.sigmoid, hf_l)
        f_l = every(lambda sg: lb + (1.0 - lb) * sg, sg_l)
        b_l = every(lambda f: _dot(trif, jnp.log(f), NN, precision=HIGHEST), f_l)
        q_l = every(_silu, hq_l)
        m_l = every(lambda b: b[C // 2 - 1:C // 2], b_l)
        bl_l = every(lambda b: b[C - 1:C], b_l)
        e_qm_l = every(lambda b, m: jnp.exp(b - m), b_l, m_l)
        e_km_l = every(lambda b, m: jnp.exp(m - b), b_l, m_l)
        e_kl_l = every(lambda b, bl: jnp.exp(bl - b), b_l, bl_l)
        e_q_l = every(jnp.exp, b_l)
        dec_l = every(jnp.exp, bl_l)
        qe_l = every(lambda q, e: q * e, q_l, e_qm_l)
        ke_l = every(lambda f, e: (1.0 - f) * e, f_l, e_km_l)
        kd_l = every(lambda f, e: (1.0 - f) * e, f_l, e_kl_l)
        qb_l = every(lambda q, e: q * e, q_l, e_q_l)
        sc_l = every(lambda qe, ke: jnp.where(tri, _bdot(qe, ke, NT), 0.0), qe_l, ke_l)
        o_l = every(lambda sc, hi, qb, st: _bdot(sc, hi, NN) + _bdot(qb, st, NT), sc_l, hi_l, qb_l, st_l)
        vj_l = every(lambda o, rows: jax.vjp(_hg_out, o, hg_ref[rows, :], gout_ref[...])[1](do_ref[rows, :]), o_l, rows_l)
        do_l = [v[0] for v in vj_l]
        dsc_l = every(lambda do, hi: jnp.where(tri, _bdot(do, hi, NT), 0.0), do_l, hi_l)
        dv1_l = every(lambda sc, do: _bdot(sc, do, TN), sc_l, do_l)
        dqe_l = every(lambda dsc, ke: _bdot(dsc, ke, NN), dsc_l, ke_l)
        dke_l = every(lambda dsc, qe: _bdot(dsc, qe, TN), dsc_l, qe_l)
        dqb_l = every(lambda do, st: _bdot(do, st, NN), do_l, st_l)
        own_l = every(lambda do, qb: _bdot(do, qb, TN), do_l, qb_l)
        dst_next_l = [None] * n
        for j in reversed(range(n)):
            dst_next_l[j] = dst
            dst = own_l[j] + dst * dec_l[j]
        dv_l = every(lambda dv1, kd, dn: dv1 + _bdot(kd, dn, NT), dv1_l, kd_l, dst_next_l)
        dkd_l = every(lambda hi, dn: _bdot(hi, dn, NN), hi_l, dst_next_l)
        ddec_l = every(lambda dn, st: jnp.sum(dn * st, axis=0, keepdims=True), dst_next_l, st_l)
        rowi = lax.broadcasted_iota(jnp.int32, (C, HG_DK), 0)
        tq_l = every(lambda a, b_: a * b_, dqe_l, qe_l)
        tk_l = every(lambda a, b_: a * b_, dke_l, ke_l)
        td_l = every(lambda a, b_: a * b_, dkd_l, kd_l)
        tb_l = every(lambda a, b_: a * b_, dqb_l, qb_l)
        db_l = every(lambda tq, tk, td, tb, ddec, dec: tq - tk - td + tb
                     + jnp.where(rowi == C // 2 - 1, jnp.sum(tk - tq, axis=0, keepdims=True), 0.0)
                     + jnp.where(rowi == C - 1, jnp.sum(td, axis=0, keepdims=True) + ddec * dec, 0.0),
                     tq_l, tk_l, td_l, tb_l, ddec_l, dec_l)
        dlf_l = every(lambda db: _dot(trif, db, TN, precision=HIGHEST), db_l)
        dk_l = every(lambda dke, e1, dkd, e2: dke * e1 + dkd * e2, dke_l, e_km_l, dkd_l, e_kl_l)
        df_l = every(lambda dlf, f, dk: dlf / f - dk, dlf_l, f_l, dk_l)
        d_lb = jnp.zeros((1, HG_DK), F32)
        d_go = jnp.zeros((1, HG_DK), F32)
        for j, rows in enumerate(rows_l):
            sg, hq = sg_l[j], hq_l[j]
            df_ref[rows, :] = (df_l[j] * (1.0 - lb) * sg * (1.0 - sg)).astype(df_ref.dtype)
            sq = jax.nn.sigmoid(hq)
            dq = dqe_l[j] * e_qm_l[j] + dqb_l[j] * e_q_l[j]
            dq_ref[rows, :] = (dq * (sq * (1.0 + hq * (1.0 - sq)))).astype(dq_ref.dtype)
            di_ref[rows, :] = dv_l[j].astype(di_ref.dtype)
            dg_ref[rows, :] = vj_l[j][1].astype(dg_ref.dtype)
            d_lb += jnp.sum(df_l[j] * (1.0 - sg), axis=0, keepdims=True)
            d_go += vj_l[j][2]
        return dst, d_lb, d_go

    blk = lambda s: pl.BlockSpec((T, HG_DK), functools.partial(lambda h, n, s: (N - 1 - n, s * H + h), s=s))
    oblk = pl.BlockSpec((T, HG_DK), lambda h, n: (N - 1 - n, h))
    vec = pl.BlockSpec((1, HG_DK), lambda h, n: (0, h))
    W = H * HG_DK
    return _call(
        body, [p4, p4, p4, p4, lb_logits, gout, s_all, d_out], name="hgrn_bwd", grid=(H, N),
        out_shape=tuple([jax.ShapeDtypeStruct((L, W), BF16)] * 4 + [jax.ShapeDtypeStruct((1, W), F32), jax.ShapeDtypeStruct((1, HG_DK), F32)]),
        in_specs=[blk(0), blk(1), blk(2), blk(3), pl.BlockSpec((2, HG_DK), lambda h, n: (0, h)),
                  pl.BlockSpec((1, HG_DK), lambda h, n: (0, 0)),
                  pl.BlockSpec((1, GR, HG_DK, HG_DK), lambda h, n: (h, N - 1 - n, 0, 0)), oblk],
        out_specs=(oblk, oblk, oblk, oblk, vec, pl.BlockSpec((1, HG_DK), lambda h, n: (0, 0))),
        scratch_shapes=[pltpu.VMEM((HG_DK, HG_DK), F32)], comm=comm)


def _bucket_ids():
    i = jnp.arange(AT_BLOCK, dtype=jnp.int32)[:, None]
    j = jnp.arange(2 * AT_BLOCK, dtype=jnp.int32)[None, :]
    n = jnp.maximum(i - j + AT_BLOCK, 0)
    nf = jnp.maximum(n, 1).astype(F32)
    large = MAX_EXACT + (jnp.log(nf / MAX_EXACT) / math.log(MAX_DISTANCE / MAX_EXACT) * (N_BUCKETS - MAX_EXACT)).astype(jnp.int32)
    large = jnp.minimum(large, N_BUCKETS - 1)
    return jnp.where(n < MAX_EXACT, n, large).reshape(1, -1)


def _onehot(bucket):
    ids = lax.broadcasted_iota(jnp.int32, (N_BUCKETS, bucket.shape[1]), 0)
    return (ids == bucket).astype(F32)


AT_PAIR = 2


def _attn_probs(qn_l, kn_l, bias_g, sink, first, scale):
    rows = qn_l[0].shape[0]
    i = jnp.bitwise_and(lax.broadcasted_iota(jnp.int32, (rows, AT_BLOCK), 0), AT_BLOCK - 1)
    j = lax.broadcasted_iota(jnp.int32, (rows, AT_BLOCK), 1)
    n = len(qn_l)
    lp_l = [_bdot(qn_l[s], kn_l[s], NT) * scale + bias_g[:, :AT_BLOCK] for s in range(n)]
    lc_l = [_bdot(qn_l[s], kn_l[s + 1], NT) * scale + bias_g[:, AT_BLOCK:] for s in range(n)]
    seen = [jnp.logical_and(j > i, jnp.logical_not(first))] + [j > i] * (n - 1)
    lp_l = [jnp.where(seen[s], lp_l[s], NEG_INF) for s in range(n)]
    lc_l = [jnp.where(j <= i, lc, NEG_INF) for lc in lc_l]
    m_l = [jnp.maximum(jnp.maximum(jnp.max(lp, axis=-1, keepdims=True), jnp.max(lc, axis=-1, keepdims=True)), sink)
           for lp, lc in zip(lp_l, lc_l)]
    pp_l = [jnp.exp(lp - m) for lp, m in zip(lp_l, m_l)]
    pc_l = [jnp.exp(lc - m) for lc, m in zip(lc_l, m_l)]
    ps_l = [jnp.exp(sink - m) for m in m_l]
    den_l = [jnp.sum(pp, axis=-1, keepdims=True) + jnp.sum(pc, axis=-1, keepdims=True) + ps
             for pp, pc, ps in zip(pp_l, pc_l, ps_l)]
    return [(pp / den, pc / den, ps / den) for pp, pc, ps, den in zip(pp_l, pc_l, ps_l, den_l)]


def _sink_rows(sk_ref, G, j=0):
    head = lax.broadcasted_iota(jnp.int32, (G * AT_BLOCK, 1), 0) // AT_BLOCK
    sink = jnp.zeros((G * AT_BLOCK, 1), F32)
    for g in range(G):
        sink = jnp.where(head == g, sk_ref[j, g:g + 1, :], sink)
    return sink


def _group_rows(ref, s, G, DH, j=0):
    B = AT_BLOCK
    rows = ref[pl.ds(s * B, B), pl.ds(j * G * DH, G * DH)].astype(F32)
    return jnp.concatenate([rows[:, g * DH:(g + 1) * DH] for g in range(G)], axis=0)


def _ungroup_rows(val, G):
    B = AT_BLOCK
    return jnp.concatenate([val[g * B:(g + 1) * B] for g in range(G)], axis=1)


def _attn_specs(cols, G, DH):
    P, B = AT_PAIR, AT_BLOCK
    pk = _heads_per_tile(DH)
    q0, k0, v0 = cols[0] // (pk * G * DH), cols[1] // (pk * DH), cols[2] // (pk * DH)
    assert cols[0] % (pk * G * DH) == 0 and cols[1] % (pk * DH) == 0 and cols[2] % (pk * DH) == 0 and (G * DH) % LANES == 0
    qblk = pl.BlockSpec((P * B, pk * G * DH), lambda h, m: (m, q0 + h))
    kblk = lambda c0, off: pl.BlockSpec((B, pk * DH), functools.partial(
        lambda h, m, c0, off: (jnp.maximum(P * m + off - 1, 0), c0 + h), c0=c0, off=off))
    return qblk, [kblk(k0, off) for off in range(P + 1)], [kblk(v0, off) for off in range(P + 1)]


def _heads_per_tile(DH):
    return LANES // DH if DH < LANES else 1


def _head_of(ref, j, DH):
    return ref[...][:, j * DH:(j + 1) * DH]


def _attn_fwd(proj, cols, qg, kg, sinks, bias, AH, KVH, comm=None):
    L, DH = proj.shape[0], qg.shape[1]
    G = AH // KVH
    NB = L // AT_BLOCK
    scale = DH ** -0.5

    P, B = AT_PAIR, AT_BLOCK
    assert NB % P == 0

    def body(q_ref, *rest):
        k_refs, v_refs = rest[:P + 1], rest[P + 1:2 * P + 2]
        qg_ref, kg_ref, sk_ref, b_ref, o_ref = rest[2 * P + 2:]
        first = pl.program_id(1) == 0
        o_heads = []
        for j in range(pk):
            kn_l = [_rms(_head_of(r, j, DH), kg_ref[...]) for r in k_refs]
            v_l = [_head_of(r, j, DH) for r in v_refs]
            qn_l = [_rms(_group_rows(q_ref, s, G, DH, j), qg_ref[...]) for s in range(P)]
            probs = _attn_probs(qn_l, kn_l, b_ref[j * G:(j + 1) * G].reshape(G * B, 2 * B), _sink_rows(sk_ref, G, j),
                                first, scale)
            o_l = [_bdot(pp, v_l[s], NN) + _bdot(pc, v_l[s + 1], NN) for s, (pp, pc, _) in enumerate(probs)]
            o_heads.append([_ungroup_rows(o, G) for o in o_l])
        for s in range(P):
            o_ref[pl.ds(s * B, B), :] = jnp.concatenate([o_heads[j][s] for j in range(pk)], axis=1).astype(o_ref.dtype)

    pk = _heads_per_tile(DH)
    assert KVH % pk == 0
    qblk, kspecs, vspecs = _attn_specs(cols, G, DH)
    return _call(
        body, [proj] * (2 * P + 3) + [qg, kg, sinks, bias], name="attn_fwd", grid=(KVH // pk, NB // P),
        out_shape=jax.ShapeDtypeStruct((L, AH * DH), BF16),
        in_specs=[qblk] + kspecs + vspecs
        + [pl.BlockSpec((1, DH), lambda h, m: (0, 0)), pl.BlockSpec((1, DH), lambda h, m: (0, 0)),
           pl.BlockSpec((pk, G, 1), lambda h, m: (h, 0, 0)), pl.BlockSpec((pk * G, B, 2 * B), lambda h, m: (h, 0, 0))],
        out_specs=pl.BlockSpec((P * B, pk * G * DH), lambda h, m: (m, h)), comm=comm)


def _attn_bwd(proj, cols, qg, kg, sinks, bias, d_o, AH, KVH, comm=None):
    L, DH = proj.shape[0], qg.shape[1]
    G = AH // KVH
    NB = L // AT_BLOCK
    B = AT_BLOCK
    scale = DH ** -0.5

    P = AT_PAIR
    assert NB % P == 0

    def body(q_ref, *rest):
        k_refs, v_refs = rest[:P + 1], rest[P + 1:2 * P + 2]
        qg_ref, kg_ref, sk_ref, b_ref, do_ref, dq_ref, dk_ref, dv_ref, dqg_ref, dkg_ref, dsk_ref, db_ref = rest[2 * P + 2:]
        m = pl.program_id(1)
        first = m == 0

        @pl.when(first)
        def _():
            for r in (dk_ref, dv_ref, dsk_ref, db_ref):
                r[...] = jnp.zeros_like(r)

        @pl.when(jnp.logical_and(first, pl.program_id(0) == 0))
        def _():
            dqg_ref[...] = jnp.zeros_like(dqg_ref)
            dkg_ref[...] = jnp.zeros_like(dkg_ref)

        kgv, qgv = kg_ref[...], qg_ref[...]
        heads = [head(j, q_ref, k_refs, v_refs, sk_ref, b_ref, do_ref, dsk_ref, db_ref, kgv, qgv, first) for j in range(pk)]
        for s in range(P):
            dq_ref[pl.ds(s * B, B), :] = jnp.concatenate([hd[0][s] for hd in heads], axis=1).astype(dq_ref.dtype)
        for t in range(P + 1):
            r = pl.multiple_of(jnp.maximum(P * m + t - 1, 0) * B, B)
            dk_ref[pl.ds(r, B), :] += jnp.concatenate([hd[1][t] for hd in heads], axis=1)
            dv_ref[pl.ds(r, B), :] += jnp.concatenate([hd[2][t] for hd in heads], axis=1)
        dqg_ref[...] += sum(hd[3] for hd in heads)
        dkg_ref[...] += sum(hd[4] for hd in heads)

    def head(j, q_ref, k_refs, v_refs, sk_ref, b_ref, do_ref, dsk_ref, db_ref, kgv, qgv, first):
        k_fw = [jax.vjp(_rms, _head_of(r, j, DH), kgv) for r in k_refs]
        v_l = [_head_of(r, j, DH) for r in v_refs]
        kn_l = [f[0] for f in k_fw]
        q_fw = [jax.vjp(_rms, _group_rows(q_ref, s, G, DH, j), qgv) for s in range(P)]
        qn_l = [f[0] for f in q_fw]
        probs = _attn_probs(qn_l, kn_l, b_ref[j * G:(j + 1) * G].reshape(G * B, 2 * B), _sink_rows(sk_ref, G, j), first, scale)
        pp_l, pc_l, ps_l = ([p[t] for p in probs] for t in range(3))
        do_l = [_group_rows(do_ref, s, G, DH, j).astype(BF16) for s in range(P)]
        dvp_l = [_bdot(pp, do, TN) for pp, do in zip(pp_l, do_l)]
        dvc_l = [_bdot(pc, do, TN) for pc, do in zip(pc_l, do_l)]
        dpp_l = [_bdot(do_l[s], v_l[s], NT) for s in range(P)]
        dpc_l = [_bdot(do_l[s], v_l[s + 1], NT) for s in range(P)]
        dsum_l = [jnp.sum(dpp * pp, axis=-1, keepdims=True) + jnp.sum(dpc * pc, axis=-1, keepdims=True)
                  for dpp, pp, dpc, pc in zip(dpp_l, pp_l, dpc_l, pc_l)]
        dlp_l = [pp * (dpp - ds) for pp, dpp, ds in zip(pp_l, dpp_l, dsum_l)]
        dlc_l = [pc * (dpc - ds) for pc, dpc, ds in zip(pc_l, dpc_l, dsum_l)]
        dsk_ref[j] += sum(jnp.sum((-ps * ds).reshape(G, B, 1), axis=1) for ps, ds in zip(ps_l, dsum_l))
        db_ref[j * G:(j + 1) * G, :, :B] += sum(dlp_l).reshape(G, B, B)
        db_ref[j * G:(j + 1) * G, :, B:] += sum(dlc_l).reshape(G, B, B)
        dlp_l, dlc_l = [d * scale for d in dlp_l], [d * scale for d in dlc_l]
        dqn_l = [_bdot(dlp_l[s], kn_l[s], NN) + _bdot(dlc_l[s], kn_l[s + 1], NN) for s in range(P)]
        dq_l = [q_fw[s][1](dqn_l[s]) for s in range(P)]
        dkn_l = [jnp.zeros((B, DH), F32)] * (P + 1)
        dvk_l = [jnp.zeros((B, DH), F32)] * (P + 1)
        for s in range(P):
            dkn_l[s] = dkn_l[s] + _bdot(dlp_l[s], qn_l[s], TN)
            dkn_l[s + 1] = dkn_l[s + 1] + _bdot(dlc_l[s], qn_l[s], TN)
            dvk_l[s] = dvk_l[s] + dvp_l[s]
            dvk_l[s + 1] = dvk_l[s + 1] + dvc_l[s]
        dk_l = [k_fw[t][1](dkn_l[t]) for t in range(P + 1)]
        return ([_ungroup_rows(d[0], G) for d in dq_l], [d[0] for d in dk_l], dvk_l,
                sum(d[1] for d in dq_l), sum(d[1] for d in dk_l))

    qblk, kspecs, vspecs = _attn_specs(cols, G, DH)
    pk = _heads_per_tile(DH)
    assert KVH % pk == 0
    oblk = pl.BlockSpec((P * B, pk * G * DH), lambda h, m: (m, h))
    accblk = pl.BlockSpec((L, pk * DH), lambda h, m: (0, h))
    vecblk = pl.BlockSpec((1, DH), lambda h, m: (0, 0))
    return _call(
        body, [proj] * (2 * P + 3) + [qg, kg, sinks, bias, d_o], name="attn_bwd", grid=(KVH // pk, NB // P),
        out_shape=(jax.ShapeDtypeStruct((L, AH * DH), BF16), jax.ShapeDtypeStruct((L, KVH * DH), F32),
                   jax.ShapeDtypeStruct((L, KVH * DH), F32), jax.ShapeDtypeStruct((1, DH), F32),
                   jax.ShapeDtypeStruct((1, DH), F32), jax.ShapeDtypeStruct((KVH, G, 1), F32),
                   jax.ShapeDtypeStruct((AH, B, 2 * B), F32)),
        in_specs=[qblk] + kspecs + vspecs
        + [pl.BlockSpec((1, DH), lambda h, m: (0, 0)), pl.BlockSpec((1, DH), lambda h, m: (0, 0)),
           pl.BlockSpec((pk, G, 1), lambda h, m: (h, 0, 0)), pl.BlockSpec((pk * G, B, 2 * B), lambda h, m: (h, 0, 0)), oblk],
        out_specs=(oblk, accblk, accblk, vecblk, vecblk, pl.BlockSpec((pk, G, 1), lambda h, m: (h, 0, 0)),
                   pl.BlockSpec((pk * G, B, 2 * B), lambda h, m: (h, 0, 0))), comm=comm)


def _heads_first(t, nh):
    L = t.shape[0]
    return jnp.transpose(t.reshape(L, nh, t.shape[1] // nh), (1, 0, 2))


def _heads_last(t):
    nh, L, dh = t.shape
    return jnp.transpose(t, (1, 0, 2)).reshape(L, nh * dh)


def _softmax0(lg):
    e = jnp.exp(lg - jnp.max(lg, axis=0, keepdims=True))
    return e[0:1] / jnp.sum(e, axis=0, keepdims=True)


def _ada_update_call(fn, c_all, d_cols, w, m, v, rt):
    D, n = w.shape

    def body(c_ref, d_ref, w_ref, m_ref, v_ref, g_out, dl_out, m_out, v_out):
        outs, _ = fn(c_ref[...], d_ref[...], w_ref[...], m_ref[...], v_ref[...])
        for r, val in zip((g_out, dl_out, m_out, v_out), outs):
            r[...] = val

    wblk = pl.BlockSpec((rt, n), lambda i: (i, 0))
    return _call(
        body, [c_all, d_cols, w, m, v], name="update_ada", grid=(D // rt,), out_shape=tuple([jax.ShapeDtypeStruct((D, n), F32)] * 4),
        in_specs=[pl.BlockSpec((N_DEV, rt), lambda i: (0, i)), pl.BlockSpec((N_DEV, n), lambda i: (0, 0)), wblk, wblk, wblk],
        out_specs=(wblk, wblk, wblk, wblk))


def kernel(x, c, w_ada, b_ada, norm1_g, norm2_g, w_in, hg_lb_logits, hg_out_norm_g, q_norm_g, k_norm_g, attn_sinks, rel_bias_table, w_branch_hg, w_branch_attn, w_out, w_ff1, w_ff2, loss_target, m_w_ada, m_b_ada, m_norm1_g, m_norm2_g, m_w_in, m_hg_lb_logits, m_hg_out_norm_g, m_q_norm_g, m_k_norm_g, m_attn_sinks, m_rel_bias_table, m_w_branch_hg, m_w_branch_attn, m_w_out, m_w_ff1, m_w_ff2, v_w_ada, v_b_ada, v_norm1_g, v_norm2_g, v_w_in, v_hg_lb_logits, v_hg_out_norm_g, v_q_norm_g, v_k_norm_g, v_attn_sinks, v_rel_bias_table, v_w_branch_hg, v_w_branch_attn, v_w_out, v_w_ff1, v_w_ff2):
    cc = lax.axis_index("c")
    me = 4 * lax.axis_index("x") + 2 * lax.axis_index("y") + cc
    x2 = x[0]
    tgt = loss_target[0]
    L, D = x2.shape
    HGW = hg_lb_logits.shape[1]
    H = HGW // HG_DK
    AH = attn_sinks.shape[1]
    DH = q_norm_g.shape[1]
    ATW = AH * DH
    BW = w_in.shape[2]
    INW = BW * N_DEV
    A = BW // LANES
    assert BW == LANES * A + LANES // 2
    KVW = (INW - 4 * HGW - ATW - 2 * D) // 2
    KVH = KVW // DH
    G = AH // KVH
    ADA_N = w_ada.shape[2]
    PAIR = 2 * A + 1

    c_all = _gather_small(c, me, "gather_c")[:, 0, :]
    b_cols = lax.dynamic_slice(b_ada, (0, me * ADA_N), (1, ADA_N))
    (ada_cols,) = _whole(lambda cv, w, b: (_bdot(_silu(cv), w, NN) + b,), [c_all, w_ada[0], b_cols],
                         [((N_DEV, ADA_N), F32)], "ada_fwd")
    ada_all = _gather_small(ada_cols, me, "gather_ada")
    ada_row = lax.dynamic_slice(ada_all, (0, me, 0), (N_DEV, 1, ADA_N)).reshape(1, 6 * D)

    w_in_b = w_in[0].astype(BF16)
    src_in = jnp.where(cc == 0, jnp.pad(w_in_b, ((0, 0), (0, LANES // 2))), jnp.pad(w_in_b, ((0, 0), (LANES // 2, 0))))
    (src_in,) = _behind([src_in], [ada_row])
    shift1, scale1, gate1, shift2, scale2, gate2 = [ada_row[:, i * D:(i + 1) * D] for i in range(6)]
    w_in_gapped, w_in_mid = _ag_w_in(src_in, A, D, INW)
    w_in_full = _patch_mid(w_in_gapped, w_in_mid, A)

    wnames = ("bhg", "bat", "out", "ff1", "ff2")
    small = ("bhg", "bat", "out")
    waxis = dict(zip(wnames, (1, 1, 0, 1, 0)))
    wsrc = dict(zip(wnames, (w_branch_hg, w_branch_attn, w_out, w_ff1, w_ff2)))
    wblk = {k: wsrc[k][0].astype(BF16) for k in wnames}
    wf = {}

    (h,) = _rowwise(lambda xv, g, sh, sc: ((_modnorm(xv, g, sh, sc),), ()), [(x2, D, 0)], [norm1_g, shift1, scale1],
                    [(D, BF16)], [], "norm1")
    o4, oa = 4 * HGW, 4 * HGW + ATW + 2 * KVW
    r1, r2 = wblk["ff1"].shape[0], wblk["ff2"].shape[0]
    assert o4 % D == 0

    def proj_order(tn_, j):
        t4, tg, ng = o4 // tn_, oa // tn_, (INW - oa) // tn_
        return jnp.where(j < t4, j, jnp.where(j < t4 + ng, j + (tg - t4), j - ng))

    cm = _Comm()
    hs = {k: _ag_ici(cm, wblk[k], waxis[k]) for k in small}
    hs["ff2"] = _ag_ici(cm, wblk["ff2"], waxis["ff2"], rows=(0, r2 // 4))
    proj = _mm(h, w_in_full, "nn", F32, "proj", comm=cm, b_order=proj_order)
    half = {k: cm.result(hs[k]) for k in hs}
    p4 = pg = proj
    GATE0 = o4 // D
    AT0 = o4 + (INW - oa)

    cm = _Comm()
    hs = {k: _ag_d2d(cm, half[k], waxis[k]) for k in small}
    hs["ff1"] = _ag_ici(cm, wblk["ff1"], waxis["ff1"], rows=(0, r1 // 2))
    o_hg, s_all = _hgrn_fwd(p4, hg_lb_logits, hg_out_norm_g, H, comm=cm)
    wf["bhg"], wf["bat"], wf["out"], half["ff1"] = (cm.result(hs[k]) for k in ("bhg", "bat", "out", "ff1"))

    bucket = _bucket_ids()
    (bias_flat,) = _whole(lambda tb, bk: (_dot(tb, _onehot(bk), TN, precision=HIGHEST),), [rel_bias_table, bucket],
                          [((AH, AT_BLOCK * 2 * AT_BLOCK), F32)], "bias_fwd")
    bias = bias_flat.reshape(AH, AT_BLOCK, 2 * AT_BLOCK)
    at_cols = (AT0, AT0 + ATW, AT0 + ATW + KVW)
    sinks3 = attn_sinks.reshape(KVH, G, 1)
    cm = _Comm()
    hs = {"ff1": _ag_ici(cm, wblk["ff1"], waxis["ff1"], rows=(r1 // 2, r1), into=half["ff1"])}
    o_at = _attn_fwd(proj, at_cols, q_norm_g, k_norm_g, sinks3, bias, AH, KVH, comm=cm)
    half["ff1"] = cm.result(hs["ff1"])

    bh = _mm(o_hg, wf["bhg"], "nn", F32, "branch_hg")
    ba = _mm(o_at, wf["bat"], "nn", F32, "branch_at")

    def merge_fn(bhv, bav, ghg, gat):
        return jax.nn.sigmoid(ghg) * bhv + jax.nn.sigmoid(gat) * bav

    cm = _Comm()
    hs = {"ff1": _ag_d2d(cm, half["ff1"], waxis["ff1"]),
          "ff2": _ag_ici(cm, wblk["ff2"], waxis["ff2"], rows=(r2 // 4, 3 * r2 // 8), into=half["ff2"])}
    (merged,) = _rowwise(lambda *a: ((merge_fn(*a),), ()), [(bh, D, 0), (ba, D, 0), (pg, D, GATE0), (pg, D, GATE0 + 1)], [],
                         [(D, BF16)], [], "merge", comm=cm)
    wf["ff1"], half["ff2"] = cm.result(hs["ff1"]), cm.result(hs["ff2"])
    cm = _Comm()
    hs = {"ff2": _ag_ici(cm, wblk["ff2"], waxis["ff2"], rows=(3 * r2 // 8, r2 // 2), into=half["ff2"])}
    mo = _mm(merged, wf["out"], "nn", F32, "out_proj", comm=cm)
    half["ff2"] = cm.result(hs["ff2"])

    def resid1(xv, mov, g1, g2n, sh, sc):
        x1v = xv + g1 * mov
        return (x1v, _modnorm(x1v, g2n, sh, sc)), ()

    x1, h2 = _rowwise(resid1, [(x2, D, 0), (mo, D, 0)], [gate1, norm2_g, shift2, scale2], [(D, F32), (D, BF16)], [], "resid1")
    cm = _Comm()
    hs = {"ff2": _ag_ici(cm, wblk["ff2"], waxis["ff2"], rows=(r2 // 2, r2), into=half["ff2"])}
    u, act = _mm(h2, wf["ff1"], "nn", (F32, BF16), "ff1", comm=cm, epi=lambda r: (r, jnp.square(jnp.maximum(r, 0.0))))
    half["ff2"] = cm.result(hs["ff2"])
    cm = _Comm()
    hs = {"ff2": _ag_d2d(cm, half["ff2"], waxis["ff2"])}
    _call(lambda: None, [], name="ag_d2d_ff2", out_shape=(), comm=cm)
    wf["ff2"] = cm.result(hs["ff2"])
    ff = _mm(act, wf["ff2"], "nn", F32, "ff2")

    def loss_fn(x1v, ffv, tv, g2):
        e = x1v + g2 * ffv - tv
        dy = e * (1.0 / D)
        return (dy, dy * g2), (jnp.sum(e * e, axis=0, keepdims=True), jnp.sum(dy * ffv, axis=0, keepdims=True))

    dy, d_ff, sq_sum, d_gate2 = _rowwise(loss_fn, [(x1, D, 0), (ff, D, 0), (tgt, D, 0)], [gate2],
                                         [(D, F32), (D, BF16)], [(1, D), (1, D)], "loss")
    loss = lax.psum(jnp.sum(sq_sum) * (0.5 / D), ("x", "y", "c"))

    owner_base = jnp.stack([me ^ r for r in CHIP_RELS]).astype(jnp.int32)
    gw, recv1, part, recv2 = {}, {}, {}, {}
    gw["ff2"] = _mm(act, d_ff, "tn", BF16, "dw_ff2")
    cm = _Comm()
    hh = _rs_d2d(cm, gw["ff2"], waxis["ff2"])
    d_u = _mm(d_ff, wf["ff2"], "nt", BF16, "d_act", comm=cm, extras=[u], epi=lambda r, uv: (r * (2.0 * jnp.maximum(uv, 0.0)),))
    part["ff2"] = _rs_add(gw["ff2"], cm.result(hh), waxis["ff2"], owner_base, "rs_add_ff2")
    rows_ff2 = part["ff2"].shape[1]
    cm = _Comm()
    hh = _rs_ici(cm, part["ff2"], rows=(0, rows_ff2 // 2))
    gw["ff1"] = _mm(h2, d_u, "tn", BF16, "dw_ff1", comm=cm)
    cm2 = _Comm()
    hh2 = _rs_ici(cm2, part["ff2"], rows=(rows_ff2 // 2, rows_ff2), recv=cm.result(hh))
    hh1 = _rs_d2d(cm2, gw["ff1"], waxis["ff1"])
    d_h2 = _mm(d_u, wf["ff1"], "nt", F32, "d_h2", comm=cm2)
    recv2["ff2"] = cm2.result(hh2)
    part["ff1"] = _rs_add(gw["ff1"], cm2.result(hh1), waxis["ff1"], owner_base, "rs_add_ff1")

    def norm2_bwd(dh2v, x1v, dyv, mov, g2n, sh, sc, g1):
        _, vjp = jax.vjp(_modnorm, x1v, g2n, sh, sc)
        dx, dg, dsh, dsc = vjp(dh2v)
        dx1 = dyv + dx
        return (dx1, dx1 * g1), (dg, dsh, dsc, jnp.sum(dx1 * mov, axis=0, keepdims=True))

    d_x1, d_mo, d_g2n, d_shift2, d_scale2, d_gate1 = _rowwise(
        norm2_bwd, [(d_h2, D, 0), (x1, D, 0), (dy, D, 0), (mo, D, 0)], [norm2_g, shift2, scale2, gate1],
        [(D, F32), (D, BF16)], [(1, D)] * 4, "norm2_bwd")
    gw["out"] = _mm(merged, d_mo, "tn", BF16, "dw_out")
    d_merged = _mm(d_mo, wf["out"], "nt", F32, "d_merged")

    def merge_bwd(dmv, bhv, bav, ghg, gat):
        _, vjp = jax.vjp(merge_fn, bhv, bav, ghg, gat)
        return vjp(dmv), ()

    d_bh, d_ba, d_ghg, d_gat = _rowwise(merge_bwd, [(d_merged, D, 0), (bh, D, 0), (ba, D, 0), (pg, D, GATE0), (pg, D, GATE0 + 1)], [],
                                        [(D, BF16)] * 4, [], "merge_bwd")
    gw["bhg"] = _mm(o_hg, d_bh, "tn", BF16, "dw_bhg")
    gw["bat"] = _mm(o_at, d_ba, "tn", BF16, "dw_bat")
    d_ohg = _mm(d_bh, wf["bhg"], "nt", F32, "d_ohg")
    d_oat = _mm(d_ba, wf["bat"], "nt", BF16, "d_oat")
    rows_ff1 = part["ff1"].shape[1]
    cut_ff1 = 3 * rows_ff1 // 8
    cm = _Comm()
    hf1 = _rs_ici(cm, part["ff1"], rows=(0, cut_ff1))
    d_hq, d_hf, d_hi, d_hg, d_lb, d_gout_h = _hgrn_bwd(p4, hg_lb_logits, hg_out_norm_g, s_all, d_ohg, H, comm=cm)
    cm2 = _Comm()
    hf1 = _rs_ici(cm2, part["ff1"], rows=(cut_ff1, rows_ff1), recv=cm.result(hf1))
    hh = {k: _rs_d2d(cm2, gw[k], waxis[k]) for k in small}
    d_aq, dkp, dvp, d_qg, d_kg, d_sk, d_bias = _attn_bwd(proj, at_cols, q_norm_g, k_norm_g, sinks3, bias,
                                                         d_oat, AH, KVH, comm=cm2)
    recv2["ff1"] = cm2.result(hf1)
    for k in small:
        part[k] = _rs_add(gw[k], cm2.result(hh[k]), waxis[k], owner_base, "rs_add_" + k)
    d_ak = dkp.astype(BF16)
    d_av = dvp.astype(BF16)
    d_proj = jnp.concatenate([d_hq, d_hf, d_hi, d_hg, d_aq, d_ak, d_av, d_ghg, d_gat], axis=1)
    cm = _Comm()
    hh = {k: _rs_ici(cm, part[k]) for k in small}
    gw_in = _mm(h, d_proj, "tn", BF16, "dw_in", comm=cm)
    for k in small:
        recv2[k] = cm.result(hh[k])

    wm = LANES * A
    cm = _Comm()
    hi_ = cm.inp(gw_in)
    h_main, h_mid = cm.out((4, D, wm), BF16), cm.out((4, D, LANES), BF16)
    for i, r in enumerate(CHIP_RELS):
        def main_view(ref, p, r=r):
            o = p["me"] ^ r ^ 1
            return ref.at[:, pl.ds(pl.multiple_of((PAIR * (o // 2) + (A + 1) * (1 - p["c"])) * LANES, LANES), wm)]

        def mid_view(ref, p, r=r):
            o = p["me"] ^ r
            return ref.at[:, pl.ds(pl.multiple_of((PAIR * (o // 2) + A) * LANES, LANES), LANES)]

        cm.copy(hi_, main_view, h_main, _slot_view(i), 1)
        cm.copy(hi_, mid_view, h_mid, _slot_view(i), 1)
    _call(lambda: None, [], name="rs_d2d_in", out_shape=(), comm=cm)
    chip = jnp.stack([(me ^ r) // 2 for r in CHIP_RELS]).astype(jnp.int32)
    part_main = _rs_add(gw_in, cm.result(h_main), 1, PAIR * chip + (A + 1) * cc, "rs_add_in_main", tw=LANES)
    part_mid = _rs_add(gw_in, cm.result(h_mid), 1, PAIR * chip + A, "rs_add_in_mid", tw=LANES)
    rs_in = _rs_split_start([part_main, part_mid], "rs_in_start")
    d_h = _mm(d_proj, w_in_full, "nt", F32, "d_h", tn=D, after=[rs_in["token"]])

    def norm1_bwd(dhv, xv, dx1v, g1n, sh, sc):
        _, vjp = jax.vjp(_modnorm, xv, g1n, sh, sc)
        dx, dg, dsh, dsc = vjp(dhv)
        return (dx1v + dx,), (dg, dsh, dsc)

    grad_x, d_g1n, d_shift1, d_scale1 = _rowwise(norm1_bwd, [(d_h, D, 0), (x2, D, 0), (d_x1, D, 0)],
                                                 [norm1_g, shift1, scale1], [(D, F32)], [(1, D)] * 3, "norm1_bwd")

    def sum4(p0, p1, p2, p3):
        return ((p0.astype(F32) + p1.astype(F32)) + p2.astype(F32)) + p3.astype(F32)

    def update_fn(w, m, v, p0, p1, p2, p3):
        g = sum4(p0, p1, p2, p3)
        delta, mn, vn = _adamw(w, g, m, v)
        return (g, delta, mn, vn), ()

    wmv = dict(zip(wnames, ((w_branch_hg, m_w_branch_hg, v_w_branch_hg), (w_branch_attn, m_w_branch_attn, v_w_branch_attn),
                            (w_out, m_w_out, v_w_out), (w_ff1, m_w_ff1, v_w_ff1), (w_ff2, m_w_ff2, v_w_ff2))))
    res = {}

    def update(k, p, rx):
        w, m, v = (t[0] for t in wmv[k])
        n = w.shape[1]
        ins = [(t, n, 0) for t in (w, m, v)] + [(p, n, 0, 0)] + [(rx, n, 0, i) for i in range(3)]
        res[k] = [t[None] for t in _rowwise(update_fn, ins, [], [(n, F32)] * 4, [], "update_" + k)]

    for k in wnames:
        update(k, part[k], recv2[k])
    (part_main, part_mid), (rx_main, rx_mid) = _rs_split_wait(rs_in, [grad_x] + [res[k][0] for k in wnames], "rs_in_wait")
    g_main, = _rowwise(lambda *p: ((sum4(*p),), ()), [(part_main, wm, 0, 0)] + [(rx_main, wm, 0, i) for i in range(3)], [],
                       [(wm, F32)], [], "sum_in_main")
    g_mid, = _rowwise(lambda *p: ((sum4(*p),), ()), [(part_mid, LANES, 0, 0)] + [(rx_mid, LANES, 0, i) for i in range(3)], [],
                      [(LANES, F32)], [], "sum_in_mid")
    g_in = jnp.where(cc == 0, jnp.concatenate([g_main, g_mid[:, :LANES // 2]], axis=1),
                     jnp.concatenate([g_mid[:, LANES // 2:], g_main], axis=1))

    def update_given(w, m, v, g):
        delta, mn, vn = _adamw(w, g, m, v)
        return (g, delta, mn, vn), ()

    res["in"] = [t[None] for t in _rowwise(update_given, [(t, BW, 0) for t in (w_in[0], m_w_in[0], v_w_in[0], g_in)], [],
                                           [(BW, F32)] * 4, [], "update_in")]

    d_sinks = d_sk.reshape(1, AH)
    (d_table_t,) = _whole(lambda db, bk: (_dot(db, _onehot(bk), NT, precision=HIGHEST),),
                          [d_bias.reshape(AH, AT_BLOCK * 2 * AT_BLOCK), bucket], [((AH, N_BUCKETS), F32)], "bias_bwd")
    smalls = [d_g1n, d_g2n, d_lb, d_gout_h, d_qg, d_kg, d_sinks, d_table_t.T.reshape(1, N_BUCKETS * AH)]
    widths = [s.shape[1] for s in smalls]
    lanes = [-(-w // LANES) * LANES for w in widths]
    smalls = [jnp.pad(s, ((0, 0), (0, p - w))) for s, w, p in zip(smalls, widths, lanes)]
    tail_row = jnp.concatenate([d_shift1, d_scale1, d_gate1, d_shift2, d_scale2, d_gate2] + smalls, axis=1)
    (tail_row,) = _behind([tail_row], [g_mid])
    tail_all = _gather_small(tail_row, me, "gather_tail")[:, 0, :]
    d_ada_all, packed = tail_all[:, :6 * D], tail_all[:, 6 * D:]
    d_ada_cols = lax.dynamic_slice(d_ada_all, (0, me * ADA_N), (N_DEV, ADA_N))

    def ada_update(cv, dav, w, m, v):
        g = _bdot(_silu(cv), dav, TN)
        delta, mn, vn = _adamw(w, g, m, v)
        return (g, delta, mn, vn), ()

    res["ada"] = [t[None] for t in _ada_update_call(ada_update, c_all, d_ada_cols, w_ada[0], m_w_ada[0], v_w_ada[0], _tile(D, 256, 16))]

    offs = [sum(lanes[:i]) for i in range(len(lanes))]

    def small_update(pk, dada, lg, *wmv_flat):
        tot = pk[0:1]
        for d in range(1, N_DEV):
            tot = tot + pk[d:d + 1]
        gb = dada[0:1]
        for d in range(1, N_DEV):
            gb = gb + dada[d:d + 1]
        gs = [tot[:, offs[i]:offs[i] + widths[i]] for i in range(len(widths))]
        _, lb_vjp = jax.vjp(_softmax0, lg)
        (g_lg,) = lb_vjp(gs[2])
        grads = [gb, gs[0], gs[1], g_lg, gs[3], gs[4], gs[5], gs[6], gs[7]]
        outs = []
        for i, g in enumerate(grads):
            w, m, v = wmv_flat[3 * i:3 * i + 3]
            delta, mn, vn = _adamw(w, g, m, v)
            outs += [g, delta, mn, vn]
        return tuple(outs)

    tbl = lambda t: t.reshape(1, N_BUCKETS * AH)
    small_wmv = [(b_ada, m_b_ada, v_b_ada), (norm1_g, m_norm1_g, v_norm1_g), (norm2_g, m_norm2_g, v_norm2_g),
                 (hg_lb_logits, m_hg_lb_logits, v_hg_lb_logits), (hg_out_norm_g, m_hg_out_norm_g, v_hg_out_norm_g),
                 (q_norm_g, m_q_norm_g, v_q_norm_g), (k_norm_g, m_k_norm_g, v_k_norm_g),
                 (attn_sinks, m_attn_sinks, v_attn_sinks),
                 (tbl(rel_bias_table), tbl(m_rel_bias_table), tbl(v_rel_bias_table))]
    flat = [t for trip in small_wmv for t in trip]
    out_shapes = [(trip[0].shape, F32) for trip in small_wmv for _ in range(4)]
    sres = _whole(small_update, [packed, d_ada_all, hg_lb_logits] + flat, out_shapes, "small_update")
    names_small = ("b_ada", "norm1_g", "norm2_g", "lb", "gout", "qg", "kg", "sinks", "table")
    for i, k in enumerate(names_small):
        r = sres[4 * i:4 * i + 4]
        if k == "table":
            r = [t.reshape(N_BUCKETS, AH) for t in r]
        res[k] = r

    order = ("ada", "b_ada", "norm1_g", "norm2_g", "in", "lb", "gout", "qg", "kg", "sinks", "table", "bhg", "bat", "out", "ff1", "ff2")
    outs = [loss, grad_x[None]]
    for j in range(4):
        outs += [res[k][j] for k in order]
    return tuple(outs)
```
